```python
import math
import jax, jax.numpy as jnp
from jax import lax
import numpy as np

D_MODEL = 1024
BATCH = 16
SEQ = 2048
DEPTH = 1

CTX_LEN = 256
GRID_W = 64
RET_HEADS = 4
RET_DK = 64
RET_DV = 128
RET_CHUNK = 128
MLA_HEADS = 4
MLA_NOPE = 128
MLA_ROPE = 64
MLA_V = 128
Q_LORA = 384
KV_LORA = 256
D_MIX = RET_HEADS * RET_DV + MLA_HEADS * MLA_V
D_FF = 4 * D_MODEL
ROPE_BASE = 10000.0
Q_BLOCK = 128
EPS = 1e-6
IN_SPLITS = (RET_HEADS * RET_DK, RET_HEADS * RET_DK, RET_HEADS * RET_DV, RET_HEADS * RET_DV,
             Q_LORA, KV_LORA, MLA_ROPE)
IN_COLS = sum(IN_SPLITS)
SPLIT_POINTS = tuple(int(v) for v in np.cumsum(IN_SPLITS)[:-1])

kernel_name = "hymba_retention_mla_adaln_prefix_block"


def rms_norm(x, g):
    x32 = x.astype(jnp.float32)
    y = x32 * lax.rsqrt(jnp.mean(x32 * x32, axis=-1, keepdims=True) + EPS)
    return (y * g.astype(jnp.float32)).astype(x.dtype)


def modulate(h, shift, scale):
    return h * (1 + scale) + shift


def axial_rope_tables(rows, dim):
    row = jnp.repeat(jnp.arange(rows, dtype=jnp.float32), GRID_W)
    col = jnp.tile(jnp.arange(GRID_W, dtype=jnp.float32), rows)
    n_freq = dim // 4
    freq = ROPE_BASE ** (-jnp.arange(n_freq, dtype=jnp.float32) / n_freq)
    ang = jnp.concatenate([row[:, None] * freq, col[:, None] * freq], axis=-1)
    return jnp.cos(ang)[:, None, :], jnp.sin(ang)[:, None, :]


def apply_rope(x, cos, sin):
    half = x.shape[-1] // 2
    x1, x2 = x[..., :half], x[..., half:]
    return jnp.concatenate([x1 * cos - x2 * sin, x2 * cos + x1 * sin], axis=-1).astype(x.dtype)


def retention_chunked(q, k, v, log_gamma, s0):
    B, L, H, dk = q.shape
    dv = v.shape[-1]
    n = L // RET_CHUNK
    qc = q.reshape(B, n, RET_CHUNK, H, dk)
    kc = k.reshape(B, n, RET_CHUNK, H, dk)
    vc = v.reshape(B, n, RET_CHUNK, H, dv)
    pos = jnp.arange(RET_CHUNK, dtype=jnp.float32)
    diff = pos[:, None] - pos[None, :]
    dec = jnp.where(diff >= 0, jnp.exp(log_gamma[:, None, None] * jnp.maximum(diff, 0.0)), 0.0)
    scores = jnp.einsum('bnihd,bnjhd->bnhij', qc, kc) * dec
    o_intra = jnp.einsum('bnhij,bnjhe->bnihe', scores, vc)
    w_k = jnp.exp(log_gamma[:, None] * (RET_CHUNK - 1 - pos)[None, :])
    kv = jnp.einsum('bnjhd,hj,bnjhe->bnhde', kc, w_k, vc).astype(jnp.float32)
    chunk_decay = jnp.exp(log_gamma * RET_CHUNK)[:, None, None]

    def step(s, kv_n):
        return chunk_decay * s + kv_n, s

    _, s_prev = lax.scan(step, s0.astype(jnp.float32), jnp.moveaxis(kv, 1, 0))
    w_q = jnp.exp(log_gamma[:, None] * (pos + 1.0)[None, :])
    o_cross = jnp.einsum('bnihd,hi,nbhde->bnihe', qc, w_q, s_prev)
    return (o_intra + o_cross).reshape(B, L, H, dv)


def retention_final_state(k, v, log_gamma):
    L = k.shape[1]
    w = jnp.exp(log_gamma[:, None] * (L - 1 - jnp.arange(L, dtype=jnp.float32))[None, :])
    return jnp.einsum('blhd,hl,blhe->bhde', k, w, v).astype(jnp.float32)


def retention_mix(q, k, v, gate, lg_f, lg_b, g_ret, s_f, s_b):
    B, L, H, dv = v.shape
    o_f = retention_chunked(q, k, v, lg_f, s_f)
    o_b = retention_chunked(q[:, ::-1], k[:, ::-1], v[:, ::-1], lg_b, s_b)[:, ::-1]
    o = (o_f + o_b).astype(jnp.float32)
    mu = jnp.mean(o, axis=-1, keepdims=True)
    var = jnp.mean(jnp.square(o - mu), axis=-1, keepdims=True)
    o = (o - mu) * lax.rsqrt(var + EPS) * g_ret.astype(jnp.float32).reshape(H, dv)
    return (o.reshape(B, L, H * dv) * jax.nn.silu(gate.astype(jnp.float32))).astype(gate.dtype)


def attend(q, k, v):
    s = jnp.einsum('bqhd,bkhd->bhqk', q, k).astype(jnp.float32) * (1.0 / math.sqrt(q.shape[-1]))
    p = jax.nn.softmax(s, axis=-1)
    return jnp.einsum('bhqk,bkhe->bqhe', p.astype(v.dtype), v)


def blocked_attention(q, k, v):
    B, L, H, d = q.shape
    qb = q.reshape(B, L // Q_BLOCK, Q_BLOCK, H, d).swapaxes(0, 1)
    out = lax.map(lambda qi: attend(qi, k, v), qb)
    return out.swapaxes(0, 1).reshape(B, L, H, v.shape[-1])


def head_group_inputs(h, w_in, g_q, w_uq, g_kv, w_ukv):
    B, L, _ = h.shape
    r_q, r_k, r_v, r_g, c_q, c_kv, k_pe = jnp.split(h @ w_in, SPLIT_POINTS, axis=-1)
    r_q = r_q.reshape(B, L, RET_HEADS, RET_DK)
    r_k = r_k.reshape(B, L, RET_HEADS, RET_DK) * (RET_DK ** -0.5)
    r_v = r_v.reshape(B, L, RET_HEADS, RET_DV)
    q = (rms_norm(c_q, g_q) @ w_uq).reshape(B, L, MLA_HEADS, MLA_NOPE + MLA_ROPE)
    kv = (rms_norm(c_kv, g_kv) @ w_ukv).reshape(B, L, MLA_HEADS, MLA_NOPE + MLA_V)
    q_nope, q_pe = q[..., :MLA_NOPE], q[..., MLA_NOPE:]
    k_nope, m_v = kv[..., :MLA_NOPE], kv[..., MLA_NOPE:]
    k_pe = k_pe[:, :, None, :]
    return r_q, r_k, r_v, r_g, q_nope, q_pe, k_nope, k_pe, m_v


def mla_qk(q_nope, q_pe, k_nope, k_pe):
    q = jnp.concatenate([q_nope, q_pe], axis=-1)
    k = jnp.concatenate([k_nope, jnp.broadcast_to(k_pe, k_nope.shape[:-1] + (MLA_ROPE,))], axis=-1)
    return q, k


def sq_relu_mlp(h, w1, w2):
    return jnp.square(jax.nn.relu(h @ w1)) @ w2


def _fwd_setup_inputs(seed: int = 0) -> dict:
    key = jax.random.key(seed)
    ks = jax.random.split(key, 24)
    f32 = jnp.float32

    def nrm(k, shape, scale):
        return jax.random.normal(k, shape, f32) * scale

    base_logit = jnp.log(2.0 ** (5.0 + jnp.arange(RET_HEADS, dtype=f32)) - 1.0)
    return {
        "x": nrm(ks[0], (BATCH, SEQ, D_MODEL), 1.0),
        "c": nrm(ks[1], (BATCH, D_MODEL), 1.0),
        "ctx": nrm(ks[2], (BATCH, CTX_LEN, D_MODEL), 1.0),
        "c_ctx": nrm(ks[3], (D_MODEL,), 1.0),
        "w_ada": nrm(ks[4], (DEPTH, D_MODEL, 6 * D_MODEL), 0.5 * D_MODEL ** -0.5),
        "b_ada": nrm(ks[5], (DEPTH, 6 * D_MODEL), 0.01),
        "g_attn": 1.0 + nrm(ks[6], (DEPTH, D_MODEL), 0.05),
        "g_ffn": 1.0 + nrm(ks[7], (DEPTH, D_MODEL), 0.05),
        "w_in": nrm(ks[8], (DEPTH, D_MODEL, IN_COLS), D_MODEL ** -0.5),
        "ret_decay_fwd": base_logit + nrm(ks[9], (DEPTH, RET_HEADS), 0.1),
        "ret_decay_bwd": base_logit + nrm(ks[10], (DEPTH, RET_HEADS), 0.1),
        "g_ret": 1.0 + nrm(ks[11], (DEPTH, RET_HEADS * RET_DV), 0.05),
        "g_q_lora": 1.0 + nrm(ks[12], (DEPTH, Q_LORA), 0.05),
        "w_uq": nrm(ks[13], (DEPTH, Q_LORA, MLA_HEADS * (MLA_NOPE + MLA_ROPE)), Q_LORA ** -0.5),
        "g_kv_lora": 1.0 + nrm(ks[14], (DEPTH, KV_LORA), 0.05),
        "w_ukv": nrm(ks[15], (DEPTH, KV_LORA, MLA_HEADS * (MLA_NOPE + MLA_V)), KV_LORA ** -0.5),
        "w_out": nrm(ks[16], (DEPTH, D_MIX, D_MODEL), D_MIX ** -0.5),
        "w_ff1": nrm(ks[17], (DEPTH, D_MODEL, D_FF), D_MODEL ** -0.5),
        "w_ff2": nrm(ks[18], (DEPTH, D_FF, D_MODEL), D_FF ** -0.5),
        "g_final": 1.0 + nrm(ks[19], (D_MODEL,), 0.05),
    }


def _fwd_reference(x, c, ctx, c_ctx, w_ada, b_ada, g_attn, g_ffn, w_in, ret_decay_fwd, ret_decay_bwd,
              g_ret, g_q_lora, w_uq, g_kv_lora, w_ukv, w_out, w_ff1, w_ff2, g_final):
    B, L, _ = x.shape
    rows = L // GRID_W
    cos, sin = axial_rope_tables(rows, RET_DK)
    for l in range(DEPTH):
        mod = jax.nn.silu(c) @ w_ada[l] + b_ada[l]
        mod_c = jax.nn.silu(c_ctx) @ w_ada[l] + b_ada[l]
        sh_a, sc_a, gt_a, sh_f, sc_f, gt_f = [m[:, None, :] for m in jnp.split(mod, 6, axis=-1)]
        csh_a, csc_a, cgt_a, csh_f, csc_f, cgt_f = jnp.split(mod_c, 6, axis=-1)

        h = modulate(rms_norm(x, g_attn[l]), sh_a, sc_a)
        hc = modulate(rms_norm(ctx, g_attn[l]), csh_a, csc_a)
        rq, rk, rv, rg, qn, qp, kn, kp, mv = head_group_inputs(h, w_in[l], g_q_lora[l], w_uq[l],
                                                                g_kv_lora[l], w_ukv[l])
        rqc, rkc, rvc, rgc, qnc, qpc, knc, kpc, mvc = head_group_inputs(hc, w_in[l], g_q_lora[l], w_uq[l],
                                                                        g_kv_lora[l], w_ukv[l])
        rq, rk = apply_rope(rq, cos, sin), apply_rope(rk, cos, sin)
        qp, kp = apply_rope(qp, cos, sin), apply_rope(kp, cos, sin)

        lg_f = jax.nn.log_sigmoid(ret_decay_fwd[l].astype(jnp.float32))
        lg_b = jax.nn.log_sigmoid(ret_decay_bwd[l].astype(jnp.float32))
        s_f = retention_final_state(rkc, rvc, lg_f)
        s_b = retention_final_state(rkc[:, ::-1], rvc[:, ::-1], lg_b)
        y_ret = retention_mix(rq, rk, rv, rg, lg_f, lg_b, g_ret[l], s_f, s_b)

        q_m, k_m = mla_qk(qn, qp, kn, kp)
        q_mc, k_mc = mla_qk(qnc, qpc, knc, kpc)
        y_mla = blocked_attention(q_m, jnp.concatenate([k_mc, k_m], axis=1),
                                  jnp.concatenate([mvc, mv], axis=1)).reshape(B, L, MLA_HEADS * MLA_V)

        x_mid = x + gt_a * (jnp.concatenate([y_ret, y_mla], axis=-1) @ w_out[l])

        if l + 1 < DEPTH:
            zero_state = jnp.zeros((B, RET_HEADS, RET_DK, RET_DV), jnp.float32)
            y_ret_c = retention_mix(rqc, rkc, rvc, rgc, lg_f, lg_b, g_ret[l], zero_state, zero_state)
            y_mla_c = attend(q_mc, k_mc, mvc).reshape(B, CTX_LEN, MLA_HEADS * MLA_V)
            ctx = ctx + cgt_a * (jnp.concatenate([y_ret_c, y_mla_c], axis=-1) @ w_out[l])
            ctx = ctx + cgt_f * sq_relu_mlp(modulate(rms_norm(ctx, g_ffn[l]), csh_f, csc_f),
                                            w_ff1[l], w_ff2[l])

        x = x_mid + gt_f * sq_relu_mlp(modulate(rms_norm(x_mid, g_ffn[l]), sh_f, sc_f),
                                       w_ff1[l], w_ff2[l])
    return rms_norm(x, g_final)


import jax as _jax
import jax.numpy as _jnp

TWIN_FORMAT = 'train_step'
FWD_PARAMS = ['x', 'c', 'ctx', 'c_ctx', 'w_ada', 'b_ada', 'g_attn', 'g_ffn', 'w_in', 'ret_decay_fwd', 'ret_decay_bwd', 'g_ret', 'g_q_lora', 'w_uq', 'g_kv_lora', 'w_ukv', 'w_out', 'w_ff1', 'w_ff2', 'g_final']
TWIN_WEIGHTS = ['c_ctx', 'w_ada', 'b_ada', 'g_attn', 'g_ffn', 'w_in', 'ret_decay_fwd', 'ret_decay_bwd', 'g_ret', 'g_q_lora', 'w_uq', 'g_kv_lora', 'w_ukv', 'w_out', 'w_ff1', 'w_ff2', 'g_final']
TWIN_DIFF_INPUT = 'x'
TWIN_INPUTS = ['x', 'c', 'ctx', 'c_ctx', 'w_ada', 'b_ada', 'g_attn', 'g_ffn', 'w_in', 'ret_decay_fwd', 'ret_decay_bwd', 'g_ret', 'g_q_lora', 'w_uq', 'g_kv_lora', 'w_ukv', 'w_out', 'w_ff1', 'w_ff2', 'g_final', 'loss_target', 'm_c_ctx', 'm_w_ada', 'm_b_ada', 'm_g_attn', 'm_g_ffn', 'm_w_in', 'm_ret_decay_fwd', 'm_ret_decay_bwd', 'm_g_ret', 'm_g_q_lora', 'm_w_uq', 'm_g_kv_lora', 'm_w_ukv', 'm_w_out', 'm_w_ff1', 'm_w_ff2', 'm_g_final', 'v_c_ctx', 'v_w_ada', 'v_b_ada', 'v_g_attn', 'v_g_ffn', 'v_w_in', 'v_ret_decay_fwd', 'v_ret_decay_bwd', 'v_g_ret', 'v_g_q_lora', 'v_w_uq', 'v_g_kv_lora', 'v_w_ukv', 'v_w_out', 'v_w_ff1', 'v_w_ff2', 'v_g_final']
TWIN_OUTPUTS = ['loss', 'grad_x', 'grad_c_ctx', 'grad_w_ada', 'grad_b_ada', 'grad_g_attn', 'grad_g_ffn', 'grad_w_in', 'grad_ret_decay_fwd', 'grad_ret_decay_bwd', 'grad_g_ret', 'grad_g_q_lora', 'grad_w_uq', 'grad_g_kv_lora', 'grad_w_ukv', 'grad_w_out', 'grad_w_ff1', 'grad_w_ff2', 'grad_g_final', 'delta_c_ctx', 'delta_w_ada', 'delta_b_ada', 'delta_g_attn', 'delta_g_ffn', 'delta_w_in', 'delta_ret_decay_fwd', 'delta_ret_decay_bwd', 'delta_g_ret', 'delta_g_q_lora', 'delta_w_uq', 'delta_g_kv_lora', 'delta_w_ukv', 'delta_w_out', 'delta_w_ff1', 'delta_w_ff2', 'delta_g_final', 'new_m_c_ctx', 'new_m_w_ada', 'new_m_b_ada', 'new_m_g_attn', 'new_m_g_ffn', 'new_m_w_in', 'new_m_ret_decay_fwd', 'new_m_ret_decay_bwd', 'new_m_g_ret', 'new_m_g_q_lora', 'new_m_w_uq', 'new_m_g_kv_lora', 'new_m_w_ukv', 'new_m_w_out', 'new_m_w_ff1', 'new_m_w_ff2', 'new_m_g_final', 'new_v_c_ctx', 'new_v_w_ada', 'new_v_b_ada', 'new_v_g_attn', 'new_v_g_ffn', 'new_v_w_in', 'new_v_ret_decay_fwd', 'new_v_ret_decay_bwd', 'new_v_g_ret', 'new_v_g_q_lora', 'new_v_w_uq', 'new_v_g_kv_lora', 'new_v_w_ukv', 'new_v_w_out', 'new_v_w_ff1', 'new_v_w_ff2', 'new_v_g_final']
TWIN_LEAF_KINDS = {'loss': 'loss', 'grad_x': 'grad_x', 'grad_c_ctx': 'grad_w', 'grad_w_ada': 'grad_w', 'grad_b_ada': 'grad_w', 'grad_g_attn': 'grad_w', 'grad_g_ffn': 'grad_w', 'grad_w_in': 'grad_w', 'grad_ret_decay_fwd': 'grad_w', 'grad_ret_decay_bwd': 'grad_w', 'grad_g_ret': 'grad_w', 'grad_g_q_lora': 'grad_w', 'grad_w_uq': 'grad_w', 'grad_g_kv_lora': 'grad_w', 'grad_w_ukv': 'grad_w', 'grad_w_out': 'grad_w', 'grad_w_ff1': 'grad_w', 'grad_w_ff2': 'grad_w', 'grad_g_final': 'grad_w', 'delta_c_ctx': 'delta_w', 'delta_w_ada': 'delta_w', 'delta_b_ada': 'delta_w', 'delta_g_attn': 'delta_w', 'delta_g_ffn': 'delta_w', 'delta_w_in': 'delta_w', 'delta_ret_decay_fwd': 'delta_w', 'delta_ret_decay_bwd': 'delta_w', 'delta_g_ret': 'delta_w', 'delta_g_q_lora': 'delta_w', 'delta_w_uq': 'delta_w', 'delta_g_kv_lora': 'delta_w', 'delta_w_ukv': 'delta_w', 'delta_w_out': 'delta_w', 'delta_w_ff1': 'delta_w', 'delta_w_ff2': 'delta_w', 'delta_g_final': 'delta_w', 'new_m_c_ctx': 'new_m', 'new_m_w_ada': 'new_m', 'new_m_b_ada': 'new_m', 'new_m_g_attn': 'new_m', 'new_m_g_ffn': 'new_m', 'new_m_w_in': 'new_m', 'new_m_ret_decay_fwd': 'new_m', 'new_m_ret_decay_bwd': 'new_m', 'new_m_g_ret': 'new_m', 'new_m_g_q_lora': 'new_m', 'new_m_w_uq': 'new_m', 'new_m_g_kv_lora': 'new_m', 'new_m_w_ukv': 'new_m', 'new_m_w_out': 'new_m', 'new_m_w_ff1': 'new_m', 'new_m_w_ff2': 'new_m', 'new_m_g_final': 'new_m', 'new_v_c_ctx': 'new_v', 'new_v_w_ada': 'new_v', 'new_v_b_ada': 'new_v', 'new_v_g_attn': 'new_v', 'new_v_g_ffn': 'new_v', 'new_v_w_in': 'new_v', 'new_v_ret_decay_fwd': 'new_v', 'new_v_ret_decay_bwd': 'new_v', 'new_v_g_ret': 'new_v', 'new_v_g_q_lora': 'new_v', 'new_v_w_uq': 'new_v', 'new_v_g_kv_lora': 'new_v', 'new_v_w_ukv': 'new_v', 'new_v_w_out': 'new_v', 'new_v_w_ff1': 'new_v', 'new_v_w_ff2': 'new_v', 'new_v_g_final': 'new_v'}


def _forward(args):
    return _fwd_reference(*[args[k] for k in FWD_PARAMS])


def _output_shape():
    out = _jax.eval_shape(lambda: _forward(_fwd_setup_inputs(0)))
    return out.shape, out.dtype

N_MICROBATCH = 1
ADAM_LR = 0.001
ADAM_B1 = 0.9
ADAM_B2 = 0.999
ADAM_EPS = 1e-08
ADAM_WD = 0.01
ADAM_STEP = 10
PER_EXAMPLE_BATCH_AXIS = {'x': 0, 'c': 0, 'ctx': 0, 'loss_target': 0}
SHARED_INPUTS = []
_WEIGHT_DTYPES = {'c_ctx': _jnp.float32, 'w_ada': _jnp.float32, 'b_ada': _jnp.float32, 'g_attn': _jnp.float32, 'g_ffn': _jnp.float32, 'w_in': _jnp.float32, 'ret_decay_fwd': _jnp.float32, 'ret_decay_bwd': _jnp.float32, 'g_ret': _jnp.float32, 'g_q_lora': _jnp.float32, 'w_uq': _jnp.float32, 'g_kv_lora': _jnp.float32, 'w_ukv': _jnp.float32, 'w_out': _jnp.float32, 'w_ff1': _jnp.float32, 'w_ff2': _jnp.float32, 'g_final': _jnp.float32}
MOMENT_SCALE = {'c_ctx': 1.486459e-02, 'w_ada': 1.692355e-01, 'b_ada': 3.092276e-01, 'g_attn': 4.946364e-02, 'g_ffn': 8.254370e-02, 'w_in': 3.954884e-02, 'ret_decay_fwd': 2.156234e-01, 'ret_decay_bwd': 1.933502e-01, 'g_ret': 3.622585e-02, 'g_q_lora': 6.594313e-03, 'w_uq': 4.409190e-03, 'g_kv_lora': 3.233343e-02, 'w_ukv': 1.652764e-02, 'w_out': 2.904256e-02, 'w_ff1': 4.439181e-02, 'w_ff2': 1.014609e-01, 'g_final': 3.227993e+01}


def _to_microbatches(a, axis):
    t = _jnp.moveaxis(a, axis, 0)
    t = t.reshape((N_MICROBATCH, t.shape[0] // N_MICROBATCH) + t.shape[1:])
    return _jnp.moveaxis(t, 1, axis + 1)


def setup_inputs(seed: int = 0) -> dict:
    inp = _fwd_setup_inputs(seed)
    key = _jax.random.fold_in(_jax.random.key(seed), 7919)
    shape, _ = _output_shape()
    out = dict(inp)
    out["loss_target"] = _jax.random.normal(_jax.random.fold_in(key, 0), shape, _jnp.float32)
    for i, name in enumerate(TWIN_WEIGHTS):
        w = inp[name].astype(_jnp.float32)
        if MOMENT_SCALE is None:
            s = _jnp.sqrt(_jnp.mean(_jnp.square(w)) + 1e-30)
        else:
            s = MOMENT_SCALE[name]
        km, kv = _jax.random.split(_jax.random.fold_in(key, i + 1))
        out[name] = w
        out["m_" + name] = s * _jax.random.normal(km, w.shape, _jnp.float32)
        out["v_" + name] = (s * s) * _jax.random.uniform(kv, w.shape, _jnp.float32, 0.5, 1.5)
    if N_MICROBATCH > 1:
        for name, axis in PER_EXAMPLE_BATCH_AXIS.items():
            out[name] = _to_microbatches(out[name], axis)
    return {'x': out['x'], 'c': out['c'], 'ctx': out['ctx'], 'c_ctx': out['c_ctx'], 'w_ada': out['w_ada'], 'b_ada': out['b_ada'], 'g_attn': out['g_attn'], 'g_ffn': out['g_ffn'], 'w_in': out['w_in'], 'ret_decay_fwd': out['ret_decay_fwd'], 'ret_decay_bwd': out['ret_decay_bwd'], 'g_ret': out['g_ret'], 'g_q_lora': out['g_q_lora'], 'w_uq': out['w_uq'], 'g_kv_lora': out['g_kv_lora'], 'w_ukv': out['w_ukv'], 'w_out': out['w_out'], 'w_ff1': out['w_ff1'], 'w_ff2': out['w_ff2'], 'g_final': out['g_final'], 'loss_target': out['loss_target'], 'm_c_ctx': out['m_c_ctx'], 'm_w_ada': out['m_w_ada'], 'm_b_ada': out['m_b_ada'], 'm_g_attn': out['m_g_attn'], 'm_g_ffn': out['m_g_ffn'], 'm_w_in': out['m_w_in'], 'm_ret_decay_fwd': out['m_ret_decay_fwd'], 'm_ret_decay_bwd': out['m_ret_decay_bwd'], 'm_g_ret': out['m_g_ret'], 'm_g_q_lora': out['m_g_q_lora'], 'm_w_uq': out['m_w_uq'], 'm_g_kv_lora': out['m_g_kv_lora'], 'm_w_ukv': out['m_w_ukv'], 'm_w_out': out['m_w_out'], 'm_w_ff1': out['m_w_ff1'], 'm_w_ff2': out['m_w_ff2'], 'm_g_final': out['m_g_final'], 'v_c_ctx': out['v_c_ctx'], 'v_w_ada': out['v_w_ada'], 'v_b_ada': out['v_b_ada'], 'v_g_attn': out['v_g_attn'], 'v_g_ffn': out['v_g_ffn'], 'v_w_in': out['v_w_in'], 'v_ret_decay_fwd': out['v_ret_decay_fwd'], 'v_ret_decay_bwd': out['v_ret_decay_bwd'], 'v_g_ret': out['v_g_ret'], 'v_g_q_lora': out['v_g_q_lora'], 'v_w_uq': out['v_w_uq'], 'v_g_kv_lora': out['v_g_kv_lora'], 'v_w_ukv': out['v_w_ukv'], 'v_w_out': out['v_w_out'], 'v_w_ff1': out['v_w_ff1'], 'v_w_ff2': out['v_w_ff2'], 'v_g_final': out['v_g_final']}


def _loss(weights, diff, rest, loss_target):
    with _jax.named_scope("forward"):
        args = {**rest, TWIN_DIFF_INPUT: diff, **{k: w.astype(_WEIGHT_DTYPES[k]) for k, w in weights.items()}}
        y = _forward(args)
    with _jax.named_scope("loss_head"):
        err = _jnp.square(y.astype(_jnp.float32) - loss_target)
        return 0.5 * _jnp.sum(_jnp.mean(err, axis=-1)) if err.ndim else 0.5 * err


def _adamw(w, g, m, v):
    m = ADAM_B1 * m + (1.0 - ADAM_B1) * g
    v = ADAM_B2 * v + (1.0 - ADAM_B2) * _jnp.square(g)
    m_hat = m / (1.0 - ADAM_B1 ** ADAM_STEP)
    v_hat = v / (1.0 - ADAM_B2 ** ADAM_STEP)
    delta = -ADAM_LR * (m_hat / (_jnp.sqrt(v_hat) + ADAM_EPS) + ADAM_WD * w)
    return delta, m, v


def reference(x, c, ctx, c_ctx, w_ada, b_ada, g_attn, g_ffn, w_in, ret_decay_fwd, ret_decay_bwd, g_ret, g_q_lora, w_uq, g_kv_lora, w_ukv, w_out, w_ff1, w_ff2, g_final, loss_target, m_c_ctx, m_w_ada, m_b_ada, m_g_attn, m_g_ffn, m_w_in, m_ret_decay_fwd, m_ret_decay_bwd, m_g_ret, m_g_q_lora, m_w_uq, m_g_kv_lora, m_w_ukv, m_w_out, m_w_ff1, m_w_ff2, m_g_final, v_c_ctx, v_w_ada, v_b_ada, v_g_attn, v_g_ffn, v_w_in, v_ret_decay_fwd, v_ret_decay_bwd, v_g_ret, v_g_q_lora, v_w_uq, v_g_kv_lora, v_w_ukv, v_w_out, v_w_ff1, v_w_ff2, v_g_final):
    given = dict(x=x, c=c, ctx=ctx, c_ctx=c_ctx, w_ada=w_ada, b_ada=b_ada, g_attn=g_attn, g_ffn=g_ffn, w_in=w_in, ret_decay_fwd=ret_decay_fwd, ret_decay_bwd=ret_decay_bwd, g_ret=g_ret, g_q_lora=g_q_lora, w_uq=w_uq, g_kv_lora=g_kv_lora, w_ukv=w_ukv, w_out=w_out, w_ff1=w_ff1, w_ff2=w_ff2, g_final=g_final, loss_target=loss_target, m_c_ctx=m_c_ctx, m_w_ada=m_w_ada, m_b_ada=m_b_ada, m_g_attn=m_g_attn, m_g_ffn=m_g_ffn, m_w_in=m_w_in, m_ret_decay_fwd=m_ret_decay_fwd, m_ret_decay_bwd=m_ret_decay_bwd, m_g_ret=m_g_ret, m_g_q_lora=m_g_q_lora, m_w_uq=m_w_uq, m_g_kv_lora=m_g_kv_lora, m_w_ukv=m_w_ukv, m_w_out=m_w_out, m_w_ff1=m_w_ff1, m_w_ff2=m_w_ff2, m_g_final=m_g_final, v_c_ctx=v_c_ctx, v_w_ada=v_w_ada, v_b_ada=v_b_ada, v_g_attn=v_g_attn, v_g_ffn=v_g_ffn, v_w_in=v_w_in, v_ret_decay_fwd=v_ret_decay_fwd, v_ret_decay_bwd=v_ret_decay_bwd, v_g_ret=v_g_ret, v_g_q_lora=v_g_q_lora, v_w_uq=v_w_uq, v_g_kv_lora=v_g_kv_lora, v_w_ukv=v_w_ukv, v_w_out=v_w_out, v_w_ff1=v_w_ff1, v_w_ff2=v_w_ff2, v_g_final=v_g_final)
    weights = {n: given[n] for n in TWIN_WEIGHTS}
    shared = {n: given[n] for n in SHARED_INPUTS}
    per_example = {n: given[n] for n in ['x', 'c', 'ctx']}
    grad_fn = _jax.value_and_grad(_loss, argnums=(0, 1))

    def one_microbatch(ex, loss_target):
        ex = dict(ex)
        diff = ex.pop(TWIN_DIFF_INPUT)
        return grad_fn(weights, diff, {**shared, **ex}, loss_target)

    if N_MICROBATCH == 1:
        loss, (grad_w, grad_x) = one_microbatch(per_example, given["loss_target"])
    else:
        def body(carry, xs):
            loss_sum, grad_sum = carry
            l_k, (gw_k, gx_k) = one_microbatch(xs[0], xs[1])
            with _jax.named_scope("update"):
                return (loss_sum + l_k, _jax.tree.map(_jnp.add, grad_sum, gw_k)), gx_k

        init = (_jnp.zeros((), _jnp.float32), _jax.tree.map(_jnp.zeros_like, weights))
        (loss, grad_w), grad_x = _jax.lax.scan(body, init, (per_example, given["loss_target"]))
    with _jax.named_scope("update"):
        delta_w, new_m, new_v = {}, {}, {}
        for n in TWIN_WEIGHTS:
            delta_w[n], new_m[n], new_v[n] = _adamw(weights[n], grad_w[n], given["m_" + n], given["v_" + n])
    return (loss, grad_x, *[grad_w[n] for n in TWIN_WEIGHTS], *[delta_w[n] for n in TWIN_WEIGHTS],
            *[new_m[n] for n in TWIN_WEIGHTS], *[new_v[n] for n in TWIN_WEIGHTS])
```

```python
import functools
import math

import jax
import jax.numpy as jnp
from jax import lax
from jax.experimental import pallas as pl
from jax.experimental.pallas import tpu as pltpu

F32 = jnp.float32
BF = jnp.bfloat16
EPS = 1e-6
LANE = 128
N_DEV = 8
D_MODEL = 1024
SEQ = 2048
CTX_LEN = 256
GRID_W = 64
N_HEADS = 4
RET_CHUNK = 128
N_CHUNK = SEQ // RET_CHUNK
D_FF = 4096
FF_BLK = D_FF // N_DEV
IN_PAD = 2816
KV_LEN = CTX_LEN + SEQ
ROPE_BASE = 10000.0
ADAM_LR, ADAM_B1, ADAM_B2, ADAM_EPS, ADAM_WD, ADAM_STEP = 0.001, 0.9, 0.999, 1e-08, 0.01, 10
TOK = 256
TOK_B = 128
VMEM_LIMIT = 56 * 1024 * 1024
ARB = "arbitrary"
MESH = pl.DeviceIdType.MESH


def _dot(a, b, ca, cb):
    return lax.dot_general(a.astype(BF), b.astype(BF), (((ca,), (cb,)), ((), ())), preferred_element_type=F32)


@jax.custom_vjp
def mm(a, b):
    return _dot(a, b, 1, 0)


@jax.custom_vjp
def mm_nt(a, b):
    return _dot(a, b, 1, 1)


@jax.custom_vjp
def mm_tn(a, b):
    return _dot(a, b, 0, 0)


mm.defvjp(lambda a, b: (_dot(a, b, 1, 0), (a, b)), lambda r, g: (mm_nt(g, r[1]), mm_tn(r[0], g)))
mm_nt.defvjp(lambda a, b: (_dot(a, b, 1, 1), (a, b)), lambda r, g: (mm(g, r[1]), mm_tn(g, r[0])))
mm_tn.defvjp(lambda a, b: (_dot(a, b, 0, 0), (a, b)), lambda r, g: (mm_nt(r[1], g), mm(r[0], g)))


@jax.custom_vjp
def _mmw(a, w, probe):
    return _dot(a, w, 1, 0)


def _mmw_bwd(r, g):
    a, w = r
    return mm_nt(g, w), jnp.zeros_like(w), mm_tn(a, g)


_mmw.defvjp(lambda a, w, probe: (_dot(a, w, 1, 0), (a, w)), _mmw_bwd)


def mmw(a, w, probe):
    return _dot(a, w, 1, 0) if probe is None else _mmw(a, w, probe)


def rmsn(x, g):
    return x * lax.rsqrt(jnp.mean(x * x, axis=-1, keepdims=True) + EPS) * g


def silu(x):
    return x * jax.nn.sigmoid(x)


def _swap32_impl(x):
    n = x.shape[-1]
    lane = lax.broadcasted_iota(jnp.int32, x.shape, x.ndim - 1) % LANE
    up = pltpu.roll(x, n - 32, x.ndim - 1)
    dn = pltpu.roll(x, 32, x.ndim - 1)
    return jnp.where(lane < 32, up, jnp.where(lane < 64, dn, 0.0))


@jax.custom_vjp
def swap32(x):
    return _swap32_impl(x)


swap32.defvjp(lambda x: (_swap32_impl(x), None), lambda _, g: (_swap32_impl(g),))


def rope(x, cs, sn):
    return x * cs + swap32(x) * sn


def k1_tile(x, sh, sc, g_attn, g_q, g_kv, w_in, w_uq, w_ukv, p_in, p_uq, p_ukv, cs, sn, is_ctx):
    h = rmsn(x, g_attn) * (1.0 + sc) + sh
    p = mmw(h, w_in, p_in)
    rk = p[:, 512:1024] * 0.125
    rv = p[:, 1024:1536]
    ckv = p[:, 2432:2688]
    kpe = p[:, 2688:2816]
    kv = mmw(rmsn(ckv, g_kv), w_ukv, p_ukv)
    kn, v = kv[:, :512], kv[:, 512:]
    if is_ctx:
        return rk, rv, kn, kpe, v
    rq = p[:, 0:512]
    rg = p[:, 1536:2048]
    cq = p[:, 2048:2432]
    q = mmw(rmsn(cq, g_q), w_uq, p_uq)
    qn, qp = q[:, :512], q[:, 512:]
    return (rope(rq, cs, sn), rope(rk, cs, sn), rv, rg, qn, rope(qp, cs, sn), kn,
            rope(kpe, cs[:, :LANE], sn[:, :LANE]), v)


def log_sigmoid(x):
    return jnp.minimum(x, 0.0) - jnp.log(1.0 + jnp.exp(-jnp.abs(x)))


def ret_chunk(q, k, v, s, lg, reverse):
    c = RET_CHUNK
    ii = lax.broadcasted_iota(jnp.int32, (c, c), 0).astype(F32)
    jj = lax.broadcasted_iota(jnp.int32, (c, c), 1).astype(F32)
    diff = (jj - ii) if reverse else (ii - jj)
    dec = jnp.where(diff >= 0, jnp.exp(lg * jnp.maximum(diff, 0.0)), 0.0)
    pos = lax.broadcasted_iota(jnp.int32, (c, 1), 0).astype(F32)
    if reverse:
        wk, wq = jnp.exp(lg * pos), jnp.exp(lg * (c - pos))
    else:
        wk, wq = jnp.exp(lg * (c - 1.0 - pos)), jnp.exp(lg * (pos + 1.0))
    o = mm(mm_nt(q, k) * dec, v) + mm(q * wq, s)
    s_next = jnp.exp(lg * float(c)) * s + mm_tn(k * wk, v)
    return o, s_next


def ctx_state(kc, vc, lg, reverse):
    n = kc.shape[0]
    pos = lax.broadcasted_iota(jnp.int32, (n, 1), 0).astype(F32)
    w = jnp.exp(lg * pos) if reverse else jnp.exp(lg * (n - 1.0 - pos))
    return mm_tn(kc * w, vc)


def attn_head(qn, qp, kn, kp, v):
    s = (mm_nt(qn, kn) + mm_nt(qp, kp)) * (1.0 / math.sqrt(192.0))
    e = jnp.exp(s - jnp.max(s, axis=-1, keepdims=True))
    return mm(e / jnp.sum(e, axis=-1, keepdims=True), v)


def gn_gate(o, rg, g_ret):
    ys = []
    for h in range(N_HEADS):
        sl = slice(LANE * h, LANE * (h + 1))
        oh = o[:, sl]
        mu = jnp.mean(oh, axis=-1, keepdims=True)
        var = jnp.mean(jnp.square(oh - mu), axis=-1, keepdims=True)
        ys.append((oh - mu) * lax.rsqrt(var + EPS) * g_ret[:, sl])
    return jnp.concatenate(ys, axis=-1) * silu(rg)


def k4a_tile(x, o_f, o_b, rg, y_mla, g_ret, gt_a, g_ffn, sh_f, sc_f, w_out, p_out):
    mix = jnp.concatenate([gn_gate(o_f + o_b, rg, g_ret), y_mla], axis=-1)
    x_mid = x + gt_a * mmw(mix, w_out, p_out)
    h2 = rmsn(x_mid, g_ffn) * (1.0 + sc_f) + sh_f
    return x_mid, h2


def k4c_tile(x_mid, mlp, gt_f, g_final, tgt):
    y = rmsn(x_mid + gt_f * mlp, g_final)
    per_tok = jnp.mean(jnp.square(y - tgt), axis=-1, keepdims=True)
    return 0.5 * jnp.sum(per_tok, axis=0, keepdims=True)


def _cp(sem=None, vmem=VMEM_LIMIT):
    return pltpu.CompilerParams(dimension_semantics=sem, vmem_limit_bytes=vmem)


def _acc(ref, val, first):
    @pl.when(first)
    def _():
        ref[...] = val

    @pl.when(jnp.logical_not(first))
    def _():
        ref[...] += val


def _full(shape):
    nd = len(shape)
    return pl.BlockSpec(shape, lambda *_: (0,) * nd)


ANY = pl.BlockSpec(memory_space=pl.ANY)


def _sds(shape, dtype=F32):
    return jax.ShapeDtypeStruct(shape, dtype)


def _exchange(arrs, gather, name):
    n = len(arrs)
    out_shape = [_sds(((N_DEV,) + a.shape) if gather else a.shape, a.dtype) for a in arrs]

    def body(*refs):
        ins, outs = refs[:n], refs[n:2 * n]
        send_sems, recv_sems, local_sems = refs[2 * n:]
        x, y, c = lax.axis_index("x"), lax.axis_index("y"), lax.axis_index("c")
        me = 4 * x + 2 * y + c
        sends, recvs, locs = [], [], []
        for i in range(n):
            for k in range(N_DEV - 1):
                bits = k + 1
                px = x ^ ((bits >> 2) & 1)
                py = y ^ ((bits >> 1) & 1)
                pc = c ^ (bits & 1)
                peer = 4 * px + 2 * py + pc
                src = ins[i] if gather else ins[i].at[peer]
                sem = i * (N_DEV - 1) + k
                sends.append(pltpu.make_async_remote_copy(
                    src_ref=src, dst_ref=outs[i].at[me], send_sem=send_sems.at[sem], recv_sem=recv_sems.at[sem],
                    device_id=(px, py, pc), device_id_type=MESH))
                recvs.append(pltpu.make_async_remote_copy(
                    src_ref=src, dst_ref=outs[i].at[peer], send_sem=send_sems.at[sem], recv_sem=recv_sems.at[sem],
                    device_id=(px, py, pc), device_id_type=MESH))
            locs.append(pltpu.make_async_copy(ins[i] if gather else ins[i].at[me], outs[i].at[me], local_sems.at[i]))
        for cp in locs + sends:
            cp.start()
        for cp in recvs:
            cp.wait_recv()
        for cp in sends:
            cp.wait_send()
        for cp in locs:
            cp.wait()

    outs = pl.pallas_call(
        body, name=name, out_shape=out_shape, in_specs=[ANY] * n, out_specs=[ANY] * n,
        scratch_shapes=[pltpu.SemaphoreType.DMA((n * (N_DEV - 1),)), pltpu.SemaphoreType.DMA((n * (N_DEV - 1),)),
                        pltpu.SemaphoreType.DMA((n,))],
    )(*arrs)
    return list(outs)


def _mod_fwd(crows, w_ada, b_blk):
    def body(c_ref, w_ref, b_ref, o_ref):
        o_ref[...] = mm(silu(c_ref[...]), w_ref[...]) + b_ref[...]

    return pl.pallas_call(body, name="mod_fwd", out_shape=_sds((24, 768)), compiler_params=_cp())(crows, w_ada, b_blk)


def _mod_bwd(crows, w_ada, dmod_blk, dmodc_blk, dmod_full, dmodc_full):
    def body(c_ref, w_ref, d_ref, dc_ref, df_ref, dcf_ref, gw_ref, gc_ref, gb_ref):
        cr = c_ref[...]
        dc, dcf = dc_ref[0:1, :], dcf_ref[0:1, :]
        for p in range(1, N_DEV):
            dc = dc + dc_ref[p:p + 1, :]
            dcf = dcf + dcf_ref[p:p + 1, :]
        row = lax.broadcasted_iota(jnp.int32, (24, 1), 0)
        gw_ref[...] = mm_tn(silu(cr), jnp.where(row == 16, dc, d_ref[...]))
        cc = cr[16:17, :]
        sg = jax.nn.sigmoid(cc)
        part = mm_nt(jnp.broadcast_to(dc, (8, 768)), w_ref[...])
        gc_ref[...] = part * (sg * (1.0 + cc * (1.0 - sg)))
        gb_ref[...] = jnp.sum(df_ref[...], axis=0, keepdims=True) + dcf

    return pl.pallas_call(
        body, name="mod_bwd", out_shape=[_sds((D_MODEL, 768)), _sds((8, D_MODEL)), _sds((1, 6 * D_MODEL))],
        compiler_params=_cp())(crows, w_ada, dmod_blk, dmodc_blk, dmod_full, dmodc_full)


def _k1_fwd(x, mod, g_attn, g_q, g_kv, w_in, w_uq, w_ukv, cs, sn, is_ctx):
    b, l, _ = x.shape
    nt = l // TOK
    widths = (512, 512, 512, 128, 512) if is_ctx else (512, 512, 512, 512, 512, 512, 512, 128, 512)

    def body(x_ref, mod_ref, ga_ref, gq_ref, gk_ref, wi_ref, wq_ref, wk_ref, cs_ref, sn_ref, *outs):
        res = k1_tile(x_ref[...], mod_ref[0:1, :], mod_ref[1:2, :], ga_ref[...], gq_ref[...], gk_ref[...],
                      wi_ref[...], wq_ref[...], wk_ref[...], None, None, None, cs_ref[...], sn_ref[...], is_ctx)
        for o_ref, r in zip(outs, res):
            o_ref[...] = r

    tok = lambda w: pl.BlockSpec((None, TOK, w), lambda i, t: (i, t, 0))
    mod_spec = pl.BlockSpec((None, 8, D_MODEL), (lambda i, t: (0, 0, 0)) if is_ctx else (lambda i, t: (i, 0, 0)))
    return pl.pallas_call(
        body, name="k1_fwd_ctx" if is_ctx else "k1_fwd", grid=(b, nt),
        in_specs=[tok(D_MODEL), mod_spec, _full((1, D_MODEL)), _full((1, 384)), _full((1, 256)),
                  _full((D_MODEL, IN_PAD)), _full((384, 1024)), _full((256, 1024)),
                  pl.BlockSpec((TOK, 512), lambda i, t: (t, 0)), pl.BlockSpec((TOK, 512), lambda i, t: (t, 0))],
        out_specs=[tok(w) for w in widths], out_shape=[_sds((b, l, w)) for w in widths],
        compiler_params=_cp((ARB, ARB)),
    )(x, mod, g_attn, g_q, g_kv, w_in, w_uq, w_ukv, cs, sn)


def _k1_bwd(x, mod, g_attn, g_q, g_kv, w_in, w_uq, w_ukv, cs, sn, cts, dx_res, init, is_ctx):
    b, l, _ = x.shape
    tk = TOK_B
    nt = l // tk
    flat_cts = [a for group in cts for a in group]
    group_sizes = [len(g) for g in cts]
    n_ct = len(flat_cts)
    has_res = dx_res is not None
    has_init = init is not None
    acc_shapes = [(D_MODEL, IN_PAD), (384, 1024), (256, 1024), (1, D_MODEL), (1, 384), (1, 256)]

    def body(*refs):
        it = iter(refs)
        x_ref, mod_ref, ga_ref, gq_ref, gk_ref, wi_hbm, wq_hbm, wk_hbm, cs_ref, sn_ref = [next(it) for _ in range(10)]
        ct_refs = [next(it) for _ in range(n_ct)]
        res_ref = next(it) if has_res else None
        init_refs = [next(it) for _ in range(6)] if has_init else None
        gx_ref = next(it) if not is_ctx else None
        out_hbm = [next(it) for _ in range(6)]
        dmod_ref = next(it)
        wi_v, wq_v, wk_v = next(it), next(it), next(it)
        accs = [next(it) for _ in range(6)]
        sem = next(it)
        i, t = pl.program_id(0), pl.program_id(1)
        first = jnp.logical_and(i == 0, t == 0)
        last = jnp.logical_and(i == b - 1, t == nt - 1)

        @pl.when(first)
        def _():
            for src, dst in ((wi_hbm, wi_v), (wq_hbm, wq_v), (wk_hbm, wk_v)):
                pltpu.sync_copy(src, dst)
            for k in range(6):
                if has_init:
                    pltpu.sync_copy(init_refs[k], accs[k])
                else:
                    accs[k][...] = jnp.zeros(acc_shapes[k], F32)

        ct_vals, pos = [], 0
        for gsz in group_sizes:
            v = ct_refs[pos][...]
            for r in ct_refs[pos + 1:pos + gsz]:
                v = v + r[...]
            ct_vals.append(v)
            pos += gsz
        wi, wq, wk = wi_v[...], wq_v[...], wk_v[...]
        csv, snv = cs_ref[...], sn_ref[...]

        def f(xv, sh, sc, ga, gq, gk, p_in, p_uq, p_ukv):
            return k1_tile(xv, sh, sc, ga, gq, gk, wi, wq, wk, p_in, p_uq, p_ukv, csv, snv, is_ctx)

        probes = [jnp.zeros(s, F32) for s in acc_shapes[:3]]
        _, vjp = jax.vjp(f, x_ref[...], mod_ref[0:1, :], mod_ref[1:2, :], ga_ref[...], gq_ref[...], gk_ref[...], *probes)
        dx, dsh, dsc, dga, dgq, dgk, dwi, dwq, dwk = vjp(tuple(ct_vals))
        if not is_ctx:
            gx_ref[...] = dx + res_ref[...] if has_res else dx
        for ref, val in zip(accs, (dwi, dwq, dwk, dga, dgq, dgk)):
            ref[...] += val
        t0 = first if is_ctx else t == 0
        _acc(dmod_ref.at[0:1, :], dsh, t0)
        _acc(dmod_ref.at[1:2, :], dsc, t0)

        @pl.when(t0)
        def _():
            dmod_ref[2:8, :] = jnp.zeros((6, D_MODEL), F32)

        @pl.when(last)
        def _():
            cps = [pltpu.make_async_copy(accs[k], out_hbm[k], sem.at[k]) for k in range(6)]
            for cp in cps:
                cp.start()
            for cp in cps:
                cp.wait()

    tok = lambda w, off=0: pl.BlockSpec((None, tk, w), lambda i, t: (i, t + off, 0))
    mod_spec = pl.BlockSpec((None, 8, D_MODEL), (lambda i, t: (0, 0, 0)) if is_ctx else (lambda i, t: (i, 0, 0)))
    in_specs = [tok(D_MODEL), mod_spec, _full((1, D_MODEL)), _full((1, 384)), _full((1, 256)), ANY, ANY, ANY,
                pl.BlockSpec((tk, 512), lambda i, t: (t, 0)), pl.BlockSpec((tk, 512), lambda i, t: (t, 0))]
    args = [x, mod, g_attn, g_q, g_kv, w_in, w_uq, w_ukv, cs, sn]
    for a, off in flat_cts:
        in_specs.append(tok(a.shape[-1], off // tk))
        args.append(a)
    if has_res:
        in_specs.append(tok(D_MODEL))
        args.append(dx_res)
    if has_init:
        in_specs += [ANY] * 6
        args += list(init)
    out_shape, out_specs = [], []
    if not is_ctx:
        out_shape.append(_sds((b, l, D_MODEL)))
        out_specs.append(tok(D_MODEL))
    out_shape += [_sds(s) for s in acc_shapes] + [_sds((1 if is_ctx else b, 8, D_MODEL))]
    out_specs += [ANY] * 6 + [mod_spec]
    outs = pl.pallas_call(
        body, name="k1_bwd_ctx" if is_ctx else "k1_bwd", grid=(b, nt), in_specs=in_specs, out_specs=out_specs,
        out_shape=out_shape,
        scratch_shapes=[pltpu.VMEM((D_MODEL, IN_PAD), BF), pltpu.VMEM((384, 1024), BF), pltpu.VMEM((256, 1024), BF)]
        + [pltpu.VMEM(s, F32) for s in acc_shapes] + [pltpu.SemaphoreType.DMA((6,))],
        compiler_params=_cp((ARB, ARB)),
    )(*args)
    outs = list(outs)
    gx = None if is_ctx else outs.pop(0)
    return gx, outs[:6], outs[6]


def _chunk_spec(rev):
    if rev:
        return pl.BlockSpec((None, RET_CHUNK, LANE), lambda i, h, n: (i, N_CHUNK - 1 - n, h))
    return pl.BlockSpec((None, RET_CHUNK, LANE), lambda i, h, n: (i, n, h))


def _state_spec(rev):
    if rev:
        return pl.BlockSpec((None, None, None, LANE, LANE), lambda i, h, n: (i, h, N_CHUNK - 1 - n, 0, 0))
    return pl.BlockSpec((None, None, None, LANE, LANE), lambda i, h, n: (i, h, n, 0, 0))


_CTX_SPEC = pl.BlockSpec((None, CTX_LEN, LANE), lambda i, h, n: (i, 0, h))
_DEC_SPEC = pl.BlockSpec((None, 1, 1), lambda i, h, n: (h, 0, 0))


def _k2_fwd(rq, rk, rv, rkc, rvc, dec_f, dec_b):
    b = rq.shape[0]

    def body(qf, kf, vf, qb, kb, vb, kc, vc, df, db, of_ref, ob_ref, sf_out, sb_out, sf, sb):
        n = pl.program_id(2)
        lgf, lgb = log_sigmoid(df[...]), log_sigmoid(db[...])

        @pl.when(n == 0)
        def _():
            sf[...] = ctx_state(kc[...], vc[...], lgf, False)
            sb[...] = ctx_state(kc[...], vc[...], lgb, True)

        sf_out[...] = sf[...]
        sb_out[...] = sb[...]
        o, s = ret_chunk(qf[...], kf[...], vf[...], sf[...], lgf, False)
        of_ref[...] = o
        sf[...] = s
        o, s = ret_chunk(qb[...], kb[...], vb[...], sb[...], lgb, True)
        ob_ref[...] = o
        sb[...] = s

    l = rq.shape[1]
    return pl.pallas_call(
        body, name="k2_fwd", grid=(b, N_HEADS, N_CHUNK),
        in_specs=[_chunk_spec(False)] * 3 + [_chunk_spec(True)] * 3 + [_CTX_SPEC, _CTX_SPEC, _DEC_SPEC, _DEC_SPEC],
        out_specs=[_chunk_spec(False), _chunk_spec(True), _state_spec(False), _state_spec(True)],
        out_shape=[_sds((b, l, 512)), _sds((b, l, 512)), _sds((b, N_HEADS, N_CHUNK, LANE, LANE)),
                   _sds((b, N_HEADS, N_CHUNK, LANE, LANE))],
        scratch_shapes=[pltpu.VMEM((LANE, LANE), F32), pltpu.VMEM((LANE, LANE), F32)],
        compiler_params=_cp((ARB, ARB, ARB)),
    )(rq, rk, rv, rq, rk, rv, rkc, rvc, dec_f, dec_b)


def _k2_bwd(rq, rk, rv, do, sf_prev, sb_prev, rkc, rvc, dec_f, dec_b):
    b, l, _ = rq.shape

    def body(qf, kf, vf, gf, spf, qb, kb, vb, gb, spb, kc, vc, df, db,
             dqf, dkf, dvf, dqb, dkb, dvb, dkc, dvc, ddf, ddb, dsf, dsb):
        n = pl.program_id(2)

        @pl.when(n == 0)
        def _():
            dsf[...] = jnp.zeros((LANE, LANE), F32)
            dsb[...] = jnp.zeros((LANE, LANE), F32)

        def one(q, k, v, g, sp, dec, ds, dq, dk, dv, dd, rev):
            def f(qv, kv_, vv, sv, dcy):
                return ret_chunk(qv, kv_, vv, sv, log_sigmoid(dcy), rev)

            _, vjp = jax.vjp(f, q[...], k[...], v[...], sp[...], dec[...])
            gq, gk, gv, gs, gd = vjp((g[...], ds[...]))
            dq[...] = gq
            dk[...] = gk
            dv[...] = gv
            ds[...] = gs
            _acc(dd, jnp.broadcast_to(gd, (8, LANE)), n == 0)

        one(qf, kf, vf, gf, spf, df, dsf, dqf, dkf, dvf, ddf, False)
        one(qb, kb, vb, gb, spb, db, dsb, dqb, dkb, dvb, ddb, True)

        @pl.when(n == N_CHUNK - 1)
        def _():
            def f(kcv, vcv, dcy, rev):
                return ctx_state(kcv, vcv, log_sigmoid(dcy), rev)

            _, vjp_f = jax.vjp(functools.partial(f, rev=False), kc[...], vc[...], df[...])
            gk_f, gv_f, gd_f = vjp_f(dsf[...])
            _, vjp_b = jax.vjp(functools.partial(f, rev=True), kc[...], vc[...], db[...])
            gk_b, gv_b, gd_b = vjp_b(dsb[...])
            dkc[...] = gk_f + gk_b
            dvc[...] = gv_f + gv_b
            ddf[...] += jnp.broadcast_to(gd_f, (8, LANE))
            ddb[...] += jnp.broadcast_to(gd_b, (8, LANE))

    dd_spec = pl.BlockSpec((None, None, 8, LANE), lambda i, h, n: (i, h, 0, 0))
    return pl.pallas_call(
        body, name="k2_bwd", grid=(b, N_HEADS, N_CHUNK),
        in_specs=[_chunk_spec(True)] * 4 + [_state_spec(True)] + [_chunk_spec(False)] * 4 + [_state_spec(False)]
        + [_CTX_SPEC, _CTX_SPEC, _DEC_SPEC, _DEC_SPEC],
        out_specs=[_chunk_spec(True)] * 3 + [_chunk_spec(False)] * 3 + [_CTX_SPEC, _CTX_SPEC, dd_spec, dd_spec],
        out_shape=[_sds((b, l, 512))] * 6 + [_sds((b, CTX_LEN, 512))] * 2 + [_sds((b, N_HEADS, 8, LANE))] * 2,
        scratch_shapes=[pltpu.VMEM((LANE, LANE), F32), pltpu.VMEM((LANE, LANE), F32)],
        compiler_params=_cp((ARB, ARB, ARB)),
    )(rq, rk, rv, do, sf_prev, rq, rk, rv, do, sb_prev, rkc, rvc, dec_f, dec_b)


TQ = 256
TQ_B = 128


def _k3_fwd(qn, qp, kn, kp, v):
    b, l, _ = qn.shape

    def body(qn_ref, qp_ref, kn_ref, kp_ref, v_ref, o_ref):
        kpv = kp_ref[...]
        for h in range(N_HEADS):
            sl = slice(LANE * h, LANE * (h + 1))
            o_ref[:, sl] = attn_head(qn_ref[:, sl], qp_ref[:, sl], kn_ref[:, sl], kpv, v_ref[:, sl])

    qs = pl.BlockSpec((None, TQ, 512), lambda i, t: (i, t, 0))
    ks = lambda w: pl.BlockSpec((None, KV_LEN, w), lambda i, t: (i, 0, 0))
    return pl.pallas_call(
        body, name="k3_fwd", grid=(b, l // TQ), in_specs=[qs, qs, ks(512), ks(LANE), ks(512)], out_specs=qs,
        out_shape=_sds((b, l, 512)), compiler_params=_cp((ARB, ARB)),
    )(qn, qp, kn, kp, v)


def _k3_bwd(qn, qp, kn, kp, v, dy):
    b, l, _ = qn.shape

    def body(qn_ref, qp_ref, kn_ref, kp_ref, v_ref, dy_ref, dqn_ref, dqp_ref, dkn_ref, dkp_ref, dv_ref):
        t0 = pl.program_id(1) == 0
        kpv = kp_ref[...].astype(F32)
        dkp = jnp.zeros((KV_LEN, LANE), F32)
        for h in range(N_HEADS):
            sl = slice(LANE * h, LANE * (h + 1))
            _, vjp = jax.vjp(attn_head, qn_ref[:, sl], qp_ref[:, sl], kn_ref[:, sl].astype(F32), kpv,
                             v_ref[:, sl].astype(F32))
            gqn, gqp, gkn, gkp, gv = vjp(dy_ref[:, sl])
            dqn_ref[:, sl] = gqn
            dqp_ref[:, sl] = gqp
            _acc(dkn_ref.at[:, sl], gkn, t0)
            _acc(dv_ref.at[:, sl], gv, t0)
            dkp = dkp + gkp
        _acc(dkp_ref, dkp, t0)

    qs = pl.BlockSpec((None, TQ_B, 512), lambda i, t: (i, t, 0))
    ks = lambda w: pl.BlockSpec((None, KV_LEN, w), lambda i, t: (i, 0, 0))
    return pl.pallas_call(
        body, name="k3_bwd", grid=(b, l // TQ_B), in_specs=[qs, qs, ks(512), ks(LANE), ks(512), qs],
        out_specs=[qs, qs, ks(512), ks(LANE), ks(512)],
        out_shape=[_sds((b, l, 512)), _sds((b, l, 512)), _sds((b, KV_LEN, 512)), _sds((b, KV_LEN, LANE)),
                   _sds((b, KV_LEN, 512))],
        compiler_params=_cp((ARB, ARB)),
    )(qn, qp, kn, kp, v, dy)


def _mod_rows(mod_ref, rows):
    return [mod_ref[r:r + 1, :] for r in rows]


def _k4a_fwd(x, o_f, o_b, rg, y_mla, g_ret, w_out, mod, g_ffn):
    b, l, _ = x.shape

    def body(x_ref, of_ref, ob_ref, rg_ref, ym_ref, gr_ref, wo_ref, mod_ref, gf_ref, xm_ref, h2_ref):
        gt_a, sh_f, sc_f = _mod_rows(mod_ref, (2, 3, 4))
        x_mid, h2 = k4a_tile(x_ref[...], of_ref[...], ob_ref[...], rg_ref[...], ym_ref[...], gr_ref[...], gt_a,
                             gf_ref[...], sh_f, sc_f, wo_ref[...], None)
        xm_ref[...] = x_mid
        h2_ref[...] = h2.astype(BF)

    tok = lambda w: pl.BlockSpec((None, TOK, w), lambda i, t: (i, t, 0))
    mod_spec = pl.BlockSpec((None, 8, D_MODEL), lambda i, t: (i, 0, 0))
    return pl.pallas_call(
        body, name="k4a_fwd", grid=(b, l // TOK),
        in_specs=[tok(D_MODEL), tok(512), tok(512), tok(512), tok(512), _full((1, 512)), _full((D_MODEL, D_MODEL)),
                  mod_spec, _full((1, D_MODEL))],
        out_specs=[tok(D_MODEL), tok(D_MODEL)], out_shape=[_sds((b, l, D_MODEL)), _sds((b, l, D_MODEL), BF)],
        compiler_params=_cp((ARB, ARB)),
    )(x, o_f, o_b, rg, y_mla, g_ret, w_out, mod, g_ffn)


def _k4b_mlp_loss(h2, w1, w2, x_mid, mod, g_final, tgt):
    b, l, _ = h2.shape
    nt = l // TOK

    def body(h2_ref, w1_ref, w2_ref, xm_ref, mod_ref, gfin_ref, tgt_ref, dxm_ref, dmlp_ref, loss_ref, dgt_ref,
             dgfin_ref, acc):
        i, t, j = pl.program_id(0), pl.program_id(1), pl.program_id(2)
        a = _dot(h2_ref[...], w1_ref[...], 1, 0)
        part = mm(jnp.square(jnp.maximum(a, 0.0)), w2_ref[...])
        _acc(acc, part, j == 0)

        @pl.when(j == N_DEV - 1)
        def _():
            (gt_f,) = _mod_rows(mod_ref, (5,))
            loss, vjp = jax.vjp(k4c_tile, xm_ref[...], acc[...], gt_f, gfin_ref[...], tgt_ref[...])
            dxm, dmlp, dgt, dgfin, _ = vjp(jnp.ones((1, 1), F32))
            dxm_ref[...] = dxm
            dmlp_ref[...] = dmlp.astype(BF)
            first = jnp.logical_and(i == 0, t == 0)
            _acc(loss_ref, jnp.broadcast_to(loss, (8, LANE)), first)
            _acc(dgfin_ref, dgfin, first)
            _acc(dgt_ref, dgt, t == 0)

    tok = lambda w: pl.BlockSpec((None, TOK, w), lambda i, t, j: (i, t, 0))
    return pl.pallas_call(
        body, name="k4b_mlp_loss", grid=(b, nt, N_DEV),
        in_specs=[tok(D_MODEL), pl.BlockSpec((None, D_MODEL, FF_BLK), lambda i, t, j: (j, 0, 0)),
                  pl.BlockSpec((None, FF_BLK, D_MODEL), lambda i, t, j: (j, 0, 0)), tok(D_MODEL),
                  pl.BlockSpec((None, 8, D_MODEL), lambda i, t, j: (i, 0, 0)),
                  pl.BlockSpec((1, D_MODEL), lambda i, t, j: (0, 0)), tok(D_MODEL)],
        out_specs=[tok(D_MODEL), tok(D_MODEL), pl.BlockSpec((8, LANE), lambda i, t, j: (0, 0)),
                   pl.BlockSpec((None, 1, D_MODEL), lambda i, t, j: (i, 0, 0)),
                   pl.BlockSpec((1, D_MODEL), lambda i, t, j: (0, 0))],
        out_shape=[_sds((b, l, D_MODEL)), _sds((b, l, D_MODEL), BF), _sds((8, LANE)), _sds((b, 1, D_MODEL)),
                   _sds((1, D_MODEL))],
        scratch_shapes=[pltpu.VMEM((TOK, D_MODEL), F32)],
        compiler_params=_cp((ARB, ARB, ARB)),
    )(h2, w1, w2, x_mid, mod, g_final, tgt)


def _k4d_mlp_bwd(h2, dmlp, w1, w2):
    b, l, _ = h2.shape
    nt = l // TOK

    def body(h2_ref, dm_ref, w1_ref, w2_ref, dh2_ref, dw1_hbm, dw2_hbm, dw1, dw2, sem):
        i, t, j = pl.program_id(0), pl.program_id(1), pl.program_id(2)
        first = jnp.logical_and(i == 0, t == 0)
        h2v, dm = h2_ref[...], dm_ref[...]
        a = _dot(h2v, w1_ref[...], 1, 0)
        r = jnp.maximum(a, 0.0)
        dhid = _dot(dm, w2_ref[...], 1, 1)
        g2 = _dot(jnp.square(r), dm, 0, 0)
        da = dhid * (2.0 * r)
        g1 = _dot(h2v, da, 0, 0)
        _acc(dw1.at[j], g1, first)
        _acc(dw2.at[j], g2, first)
        _acc(dh2_ref, _dot(da, w1_ref[...], 1, 1), j == 0)

        @pl.when(jnp.logical_and(jnp.logical_and(i == b - 1, t == nt - 1), j == N_DEV - 1))
        def _():
            c1 = pltpu.make_async_copy(dw1, dw1_hbm, sem.at[0])
            c2 = pltpu.make_async_copy(dw2, dw2_hbm, sem.at[1])
            c1.start()
            c2.start()
            c1.wait()
            c2.wait()

    tok = lambda w: pl.BlockSpec((None, TOK, w), lambda i, t, j: (i, t, 0))
    return pl.pallas_call(
        body, name="k4d_mlp_bwd", grid=(b, nt, N_DEV),
        in_specs=[tok(D_MODEL), tok(D_MODEL), pl.BlockSpec((None, D_MODEL, FF_BLK), lambda i, t, j: (j, 0, 0)),
                  pl.BlockSpec((None, FF_BLK, D_MODEL), lambda i, t, j: (j, 0, 0))],
        out_specs=[tok(D_MODEL), ANY, ANY],
        out_shape=[_sds((b, l, D_MODEL)), _sds((N_DEV, D_MODEL, FF_BLK)), _sds((N_DEV, FF_BLK, D_MODEL))],
        scratch_shapes=[pltpu.VMEM((N_DEV, D_MODEL, FF_BLK), F32), pltpu.VMEM((N_DEV, FF_BLK, D_MODEL), F32),
                        pltpu.SemaphoreType.DMA((2,))],
        compiler_params=_cp((ARB, ARB, ARB)),
    )(h2, dmlp, w1, w2)


def _k4e_bwd(x, o_f, o_b, rg, y_mla, g_ret, w_out, mod, g_ffn, dxm, dh2):
    b, l, _ = x.shape

    def body(x_ref, of_ref, ob_ref, rg_ref, ym_ref, gr_ref, wo_ref, mod_ref, gf_ref, dxm_ref, dh2_ref,
             dx_ref, do_ref, drg_ref, dym_ref, dwo_ref, dgr_ref, dgf_ref, dmod_ref):
        i, t = pl.program_id(0), pl.program_id(1)
        first = jnp.logical_and(i == 0, t == 0)
        gt_a, sh_f, sc_f = _mod_rows(mod_ref, (2, 3, 4))
        wo = wo_ref[...]

        def f(xv, ofv, rgv, ymv, grv, gta, gfv, shf, scf, p_out):
            return k4a_tile(xv, ofv, ob_ref[...], rgv, ymv, grv, gta, gfv, shf, scf, wo, p_out)

        _, vjp = jax.vjp(f, x_ref[...], of_ref[...], rg_ref[...], ym_ref[...], gr_ref[...], gt_a, gf_ref[...], sh_f,
                         sc_f, jnp.zeros((D_MODEL, D_MODEL), F32))
        dx, do, drg, dym, dgr, dgta, dgf, dshf, dscf, dwo = vjp((dxm_ref[...], dh2_ref[...]))
        dx_ref[...] = dx
        do_ref[...] = do
        drg_ref[...] = drg
        dym_ref[...] = dym
        _acc(dwo_ref, dwo, first)
        _acc(dgr_ref, dgr, first)
        _acc(dgf_ref, dgf, first)
        t0 = t == 0
        _acc(dmod_ref.at[2:3, :], dgta, t0)
        _acc(dmod_ref.at[3:4, :], dshf, t0)
        _acc(dmod_ref.at[4:5, :], dscf, t0)

        @pl.when(t0)
        def _():
            dmod_ref[0:2, :] = jnp.zeros((2, D_MODEL), F32)
            dmod_ref[5:8, :] = jnp.zeros((3, D_MODEL), F32)

    tok = lambda w: pl.BlockSpec((None, TOK_B, w), lambda i, t: (i, t, 0))
    mod_spec = pl.BlockSpec((None, 8, D_MODEL), lambda i, t: (i, 0, 0))
    return pl.pallas_call(
        body, name="k4e_bwd", grid=(b, l // TOK_B),
        in_specs=[tok(D_MODEL), tok(512), tok(512), tok(512), tok(512), _full((1, 512)), _full((D_MODEL, D_MODEL)),
                  mod_spec, _full((1, D_MODEL)), tok(D_MODEL), tok(D_MODEL)],
        out_specs=[tok(D_MODEL), tok(512), tok(512), tok(512), _full((D_MODEL, D_MODEL)), _full((1, 512)),
                   _full((1, D_MODEL)), mod_spec],
        out_shape=[_sds((b, l, D_MODEL)), _sds((b, l, 512)), _sds((b, l, 512)), _sds((b, l, 512)),
                   _sds((D_MODEL, D_MODEL)), _sds((1, 512)), _sds((1, D_MODEL)), _sds((b, 8, D_MODEL))],
        compiler_params=_cp((ARB, ARB)),
    )(x, o_f, o_b, rg, y_mla, g_ret, w_out, mod, g_ffn, dxm, dh2)


def _adamw(w, m, v, pieces, name):
    r, c = w.shape
    npc = pieces.shape[0]
    rb = r
    for cand in (256, 128, 64, 32, 16, 8):
        if r > cand and r % cand == 0 and cand * c * 4 * (npc + 7) * 2 <= 24 * 1024 * 1024:
            rb = cand
            break

    def body(w_ref, m_ref, v_ref, p_ref, g_ref, d_ref, nm_ref, nv_ref):
        g = p_ref[0]
        for k in range(1, npc):
            g = g + p_ref[k]
        wv = w_ref[...]
        mn = ADAM_B1 * m_ref[...] + (1.0 - ADAM_B1) * g
        vn = ADAM_B2 * v_ref[...] + (1.0 - ADAM_B2) * jnp.square(g)
        m_hat = mn / (1.0 - ADAM_B1 ** ADAM_STEP)
        v_hat = vn / (1.0 - ADAM_B2 ** ADAM_STEP)
        g_ref[...] = g
        d_ref[...] = -ADAM_LR * (m_hat / (jnp.sqrt(v_hat) + ADAM_EPS) + ADAM_WD * wv)
        nm_ref[...] = mn
        nv_ref[...] = vn

    blk = pl.BlockSpec((rb, c), lambda i: (i, 0))
    return pl.pallas_call(
        body, name=name, grid=(r // rb,), in_specs=[blk, blk, blk, pl.BlockSpec((npc, rb, c), lambda i: (0, i, 0))],
        out_specs=[blk] * 4, out_shape=[_sds((r, c))] * 4, compiler_params=_cp((ARB,)),
    )(w, m, v, pieces)


def _pad_heads(w, d):
    k = w.shape[0]
    return jnp.pad(w.reshape(k, N_HEADS, d), ((0, 0), (0, 0), (0, LANE - d))).reshape(k, N_HEADS * LANE)


def _cut_heads(w, d):
    k = w.shape[0]
    return w.reshape(k, N_HEADS, LANE)[:, :, :d].reshape(k, N_HEADS * d)


def _w_in_pad(w):
    return jnp.concatenate([_pad_heads(w[:, 0:256], 64), _pad_heads(w[:, 256:512], 64), w[:, 512:2176],
                            jnp.pad(w[:, 2176:2240], ((0, 0), (0, 64)))], axis=1)


def _w_in_cut(g):
    return jnp.concatenate([_cut_heads(g[:, 0:512], 64), _cut_heads(g[:, 512:1024], 64), g[:, 1024:2688],
                            g[:, 2688:2752]], axis=1)


def _w_uq_pad(w):
    w = w.reshape(384, N_HEADS, 192)
    return jnp.concatenate([w[:, :, :128].reshape(384, 512),
                            jnp.pad(w[:, :, 128:], ((0, 0), (0, 0), (0, 64))).reshape(384, 512)], axis=1)


def _w_uq_cut(g):
    return jnp.concatenate([g[:, :512].reshape(384, N_HEADS, 128), g[:, 512:].reshape(384, N_HEADS, 128)[:, :, :64]],
                           axis=2).reshape(384, 768)


def _w_ukv_perm(w):
    w = w.reshape(256, N_HEADS, 256)
    return jnp.concatenate([w[:, :, :128].reshape(256, 512), w[:, :, 128:].reshape(256, 512)], axis=1)


def _w_ukv_unperm(g):
    return jnp.concatenate([g[:, :512].reshape(256, N_HEADS, 128), g[:, 512:].reshape(256, N_HEADS, 128)],
                           axis=2).reshape(256, 1024)


def _unshard_cols(g):
    return jnp.transpose(g, (1, 0, 2)).reshape(g.shape[1], N_DEV * g.shape[2])


def _shard_cols(w):
    k, n = w.shape
    return jnp.transpose(w.reshape(k, N_DEV, n // N_DEV), (1, 0, 2))


def _rope_tables():
    rows = SEQ // GRID_W
    row = jnp.repeat(jnp.arange(rows, dtype=F32), GRID_W)
    col = jnp.tile(jnp.arange(GRID_W, dtype=F32), rows)
    freq = ROPE_BASE ** (-jnp.arange(16, dtype=F32) / 16)
    ang = jnp.concatenate([row[:, None] * freq, col[:, None] * freq], axis=-1)
    cos, sin = jnp.cos(ang), jnp.sin(ang)
    z = jnp.zeros((SEQ, 64), F32)
    cs = jnp.concatenate([cos, cos, z], axis=1)
    sn = jnp.concatenate([-sin, sin, z], axis=1)
    return jnp.tile(cs, (1, N_HEADS)), jnp.tile(sn, (1, N_HEADS))


_SMALL = (("c_ctx", 1024), ("b_ada", 6144), ("g_attn", 1024), ("g_ffn", 1024), ("ret_decay_fwd", 128),
          ("ret_decay_bwd", 128), ("g_ret", 512), ("g_q_lora", 384), ("g_kv_lora", 256), ("g_final", 1024))
_SMALL_REAL = {"c_ctx": 1024, "b_ada": 6144, "g_attn": 1024, "g_ffn": 1024, "ret_decay_fwd": 4, "ret_decay_bwd": 4,
               "g_ret": 512, "g_q_lora": 384, "g_kv_lora": 256, "g_final": 1024}
_SMALL_ROWS = sum(n for _, n in _SMALL) // LANE


def _pack_small(vals):
    parts = []
    for name, n in _SMALL:
        a = vals[name].reshape(-1).astype(F32)
        parts.append(jnp.pad(a, (0, n - a.shape[0])))
    return jnp.concatenate(parts).reshape(_SMALL_ROWS, LANE)


def _unpack_small(packed, shapes):
    flat = packed.reshape(-1)
    out, off = {}, 0
    for name, n in _SMALL:
        out[name] = flat[off:off + _SMALL_REAL[name]].reshape(shapes[name])
        off += n
    return out


def kernel(x, c, ctx, c_ctx, w_ada, b_ada, g_attn, g_ffn, w_in, ret_decay_fwd, ret_decay_bwd, g_ret, g_q_lora, w_uq, g_kv_lora, w_ukv, w_out, w_ff1, w_ff2, g_final, loss_target, m_c_ctx, m_w_ada, m_b_ada, m_g_attn, m_g_ffn, m_w_in, m_ret_decay_fwd, m_ret_decay_bwd, m_g_ret, m_g_q_lora, m_w_uq, m_g_kv_lora, m_w_ukv, m_w_out, m_w_ff1, m_w_ff2, m_g_final, v_c_ctx, v_w_ada, v_b_ada, v_g_attn, v_g_ffn, v_w_in, v_ret_decay_fwd, v_ret_decay_bwd, v_g_ret, v_g_q_lora, v_w_uq, v_g_kv_lora, v_w_ukv, v_w_out, v_w_ff1, v_w_ff2, v_g_final):
    me = 4 * lax.axis_index("x") + 2 * lax.axis_index("y") + lax.axis_index("c")
    nb = x.shape[0]

    c_pad = jnp.pad(c, ((0, 8 - nb), (0, 0)))
    c_all, g_in, g_uq, g_ukv, g_out, g_ff1, g_ff2 = _exchange(
        [c_pad, w_in[0].astype(BF), w_uq[0].astype(BF), w_ukv[0].astype(BF), w_out[0].astype(BF),
         w_ff1[0].astype(BF), w_ff2[0].astype(BF)], True, "gather_weights")
    wi = _w_in_pad(_unshard_cols(g_in))
    wq = _w_uq_pad(_unshard_cols(g_uq))
    wk = _w_ukv_perm(_unshard_cols(g_ukv))
    wo = g_out.reshape(D_MODEL, D_MODEL)

    crows = jnp.concatenate([c_all[:, :nb].reshape(N_DEV * nb, D_MODEL), c_ctx[None], jnp.zeros((7, D_MODEL), F32)])
    b_blk = lax.dynamic_slice(b_ada, (0, me * 768), (1, 768))
    (mod_g,) = _exchange([_mod_fwd(crows, w_ada[0], b_blk)], True, "gather_mod")
    mod_all = _unshard_cols(mod_g)
    mod_mine = lax.dynamic_slice(mod_all, (me * nb, 0), (nb, 6 * D_MODEL)).reshape(nb, 6, D_MODEL)
    mod = jnp.pad(mod_mine, ((0, 0), (0, 2), (0, 0)))
    mod_c = jnp.pad(mod_all[16].reshape(1, 6, D_MODEL), ((0, 0), (0, 2), (0, 0)))

    cs, sn = _rope_tables()
    dec_f = ret_decay_fwd.reshape(N_HEADS, 1, 1)
    dec_b = ret_decay_bwd.reshape(N_HEADS, 1, 1)

    rkc, rvc, knc, kpc, vc = _k1_fwd(ctx, mod_c, g_attn, g_q_lora, g_kv_lora, wi, wq, wk, cs, sn, True)
    rq, rk, rv, rg, qn, qp, kn, kp, vv = _k1_fwd(x, mod, g_attn, g_q_lora, g_kv_lora, wi, wq, wk, cs, sn, False)
    o_f, o_b, sf_prev, sb_prev = _k2_fwd(rq, rk, rv, rkc, rvc, dec_f, dec_b)
    kn_all = jnp.concatenate([knc, kn], axis=1).astype(BF)
    kp_all = jnp.concatenate([kpc, kp], axis=1).astype(BF)
    v_all = jnp.concatenate([vc, vv], axis=1).astype(BF)
    y_mla = _k3_fwd(qn, qp, kn_all, kp_all, v_all)
    x_mid, h2 = _k4a_fwd(x, o_f, o_b, rg, y_mla, g_ret, wo, mod, g_ffn)
    dxm, dmlp, loss_acc, dgt_f, dg_final = _k4b_mlp_loss(h2, g_ff1, g_ff2, x_mid, mod, g_final.reshape(1, D_MODEL),
                                                         loss_target)

    dh2, dw1, dw2 = _k4d_mlp_bwd(h2, dmlp, g_ff1, g_ff2)
    dx_res, do, drg, dym, dwo, dg_ret, dg_ffn, dmod_a = _k4e_bwd(x, o_f, o_b, rg, y_mla, g_ret, wo, mod, g_ffn, dxm, dh2)
    dqn, dqp, dkn_all, dkp_all, dv_all = _k3_bwd(qn, qp, kn_all, kp_all, v_all, dym)
    dqf, dkf, dvf, dqb, dkb, dvb, dkc, dvc, ddf, ddb = _k2_bwd(rq, rk, rv, do, sf_prev, sb_prev, rkc, rvc, dec_f, dec_b)
    cts = [[(dqf, 0), (dqb, 0)], [(dkf, 0), (dkb, 0)], [(dvf, 0), (dvb, 0)], [(drg, 0)], [(dqn, 0)], [(dqp, 0)],
           [(dkn_all, CTX_LEN)], [(dkp_all, CTX_LEN)], [(dv_all, CTX_LEN)]]
    grad_x, accs, dmod_1 = _k1_bwd(x, mod, g_attn, g_q_lora, g_kv_lora, wi, wq, wk, cs, sn, cts, dx_res, None, False)
    cts_c = [[(dkc, 0)], [(dvc, 0)], [(dkn_all, 0)], [(dkp_all, 0)], [(dv_all, 0)]]
    _, accs, dmod_c1 = _k1_bwd(ctx, mod_c, g_attn, g_q_lora, g_kv_lora, wi, wq, wk, cs, sn, cts_c, None, accs, True)
    dwi, dwq, dwk, dg_attn, dg_q, dg_kv = accs

    dmod_loc = (dmod_a + dmod_1).at[:, 5, :].set(dgt_f[:, 0, :])[:, :6, :].reshape(nb, 6 * D_MODEL)
    dmod_ctx = dmod_c1[:, :6, :].reshape(1, 6 * D_MODEL)
    small = {"c_ctx": jnp.zeros((D_MODEL,), F32), "b_ada": jnp.zeros((6 * D_MODEL,), F32), "g_attn": dg_attn,
             "g_ffn": dg_ffn, "ret_decay_fwd": jnp.sum(ddf[:, :, 0, 0], axis=0), "ret_decay_bwd": jnp.sum(ddb[:, :, 0, 0], axis=0),
             "g_ret": dg_ret, "g_q_lora": dg_q, "g_kv_lora": dg_kv, "g_final": dg_final}
    extra = jnp.concatenate([dmod_loc, dmod_ctx, jnp.zeros((5, 6 * D_MODEL), F32)]).reshape(8 * 48, LANE)
    loss_rows = loss_acc
    sm_g, ex_g, loss_g = _exchange([_pack_small(small), extra, loss_rows], True, "gather_small")
    ex_g = ex_g.reshape(N_DEV, 8, 6 * D_MODEL)
    dmod_all = ex_g[:, :nb].reshape(N_DEV * nb, 6 * D_MODEL)
    dmodc_parts = ex_g[:, nb]
    dmod_full = jnp.concatenate([dmod_all, jnp.zeros((8, 6 * D_MODEL), F32)])
    dmod_blk = lax.dynamic_slice(dmod_full, (0, me * 768), (24, 768))
    dmodc_blk = lax.dynamic_slice(dmodc_parts, (0, me * 768), (N_DEV, 768))
    gw_ada, gcc_part, gb_ada = _mod_bwd(crows, w_ada[0], dmod_blk, dmodc_blk, dmod_full, dmodc_parts)
    (gcc_g,) = _exchange([gcc_part], True, "gather_c_ctx")

    pieces = _exchange([_shard_cols(_w_in_cut(dwi)), _shard_cols(_w_uq_cut(dwq)), _shard_cols(_w_ukv_unperm(dwk)),
                        dwo.reshape(N_DEV, 128, D_MODEL), dw1, dw2], False, "scatter_grads")

    res = {}
    big = (("w_in", w_in, m_w_in, v_w_in, pieces[0]), ("w_uq", w_uq, m_w_uq, v_w_uq, pieces[1]),
           ("w_ukv", w_ukv, m_w_ukv, v_w_ukv, pieces[2]), ("w_out", w_out, m_w_out, v_w_out, pieces[3]),
           ("w_ff1", w_ff1, m_w_ff1, v_w_ff1, pieces[4]), ("w_ff2", w_ff2, m_w_ff2, v_w_ff2, pieces[5]),
           ("w_ada", w_ada, m_w_ada, v_w_ada, gw_ada[None]))
    for name, w, m, v, pcs in big:
        res[name] = [a[None] for a in _adamw(w[0], m[0], v[0], pcs, "adamw_" + name)]

    smalls = {"c_ctx": (c_ctx, m_c_ctx, v_c_ctx), "b_ada": (b_ada, m_b_ada, v_b_ada), "g_attn": (g_attn, m_g_attn, v_g_attn),
              "g_ffn": (g_ffn, m_g_ffn, v_g_ffn), "ret_decay_fwd": (ret_decay_fwd, m_ret_decay_fwd, v_ret_decay_fwd),
              "ret_decay_bwd": (ret_decay_bwd, m_ret_decay_bwd, v_ret_decay_bwd), "g_ret": (g_ret, m_g_ret, v_g_ret),
              "g_q_lora": (g_q_lora, m_g_q_lora, v_g_q_lora), "g_kv_lora": (g_kv_lora, m_g_kv_lora, v_g_kv_lora),
              "g_final": (g_final, m_g_final, v_g_final)}
    sm_pieces = sm_g.reshape(N_DEV, _SMALL_ROWS * LANE)
    sm_pieces = sm_pieces.at[:, 0:1024].set(gcc_g[:, 0, :])
    sm_pieces = sm_pieces.at[0, 1024:1024 + 6144].set(gb_ada[0])
    sm_pieces = sm_pieces.reshape(N_DEV, _SMALL_ROWS, LANE)
    packed = [_pack_small({k: t[i] for k, t in smalls.items()}) for i in range(3)]
    sm_out = _adamw(packed[0], packed[1], packed[2], sm_pieces, "adamw_small")
    shapes = {k: t[0].shape for k, t in smalls.items()}
    sm_res = [_unpack_small(o, shapes) for o in sm_out]
    for name in smalls:
        res[name] = [r[name] for r in sm_res]

    loss = loss_g[0, 0, 0]
    for k in range(1, N_DEV):
        loss = loss + loss_g[k, 0, 0]

    order = ("c_ctx", "w_ada", "b_ada", "g_attn", "g_ffn", "w_in", "ret_decay_fwd", "ret_decay_bwd", "g_ret", "g_q_lora",
             "w_uq", "g_kv_lora", "w_ukv", "w_out", "w_ff1", "w_ff2", "g_final")
    return (loss, grad_x, *[res[n][0] for n in order], *[res[n][1] for n in order], *[res[n][2] for n in order],
            *[res[n][3] for n in order])
```

```python
import functools
import math

import jax
import jax.numpy as jnp
from jax import lax
from jax.experimental import pallas as pl
from jax.experimental.pallas import tpu as pltpu

F32 = jnp.float32
BF = jnp.bfloat16
EPS = 1e-6
LANE = 128
N_DEV = 8
D_MODEL = 1024
SEQ = 2048
CTX_LEN = 256
GRID_W = 64
N_HEADS = 4
RET_CHUNK = 128
N_CHUNK = SEQ // RET_CHUNK
D_FF = 4096
FF_BLK = D_FF // N_DEV
IN_PAD = 2816
KV_LEN = CTX_LEN + SEQ
ROPE_BASE = 10000.0
ADAM_LR, ADAM_B1, ADAM_B2, ADAM_EPS, ADAM_WD, ADAM_STEP = 0.001, 0.9, 0.999, 1e-08, 0.01, 10
TOK = 256
TOK_B = 128
VMEM_LIMIT = 56 * 1024 * 1024
ARB = "arbitrary"
MESH = pl.DeviceIdType.MESH


def _dot(a, b, ca, cb):
    return lax.dot_general(a.astype(BF), b.astype(BF), (((ca,), (cb,)), ((), ())), preferred_element_type=F32)


@jax.custom_vjp
def mm(a, b):
    return _dot(a, b, 1, 0)


@jax.custom_vjp
def mm_nt(a, b):
    return _dot(a, b, 1, 1)


@jax.custom_vjp
def mm_tn(a, b):
    return _dot(a, b, 0, 0)


mm.defvjp(lambda a, b: (_dot(a, b, 1, 0), (a, b)), lambda r, g: (mm_nt(g, r[1]), mm_tn(r[0], g)))
mm_nt.defvjp(lambda a, b: (_dot(a, b, 1, 1), (a, b)), lambda r, g: (mm(g, r[1]), mm_tn(g, r[0])))
mm_tn.defvjp(lambda a, b: (_dot(a, b, 0, 0), (a, b)), lambda r, g: (mm_nt(r[1], g), mm(r[0], g)))


@jax.custom_vjp
def _mmw(a, w, probe):
    return _dot(a, w, 1, 0)


def _mmw_bwd(r, g):
    a, w = r
    return mm_nt(g, w), jnp.zeros_like(w), mm_tn(a, g)


_mmw.defvjp(lambda a, w, probe: (_dot(a, w, 1, 0), (a, w)), _mmw_bwd)


def mmw(a, w, probe):
    return _dot(a, w, 1, 0) if probe is None else _mmw(a, w, probe)


def rmsn(x, g):
    return x * lax.rsqrt(jnp.mean(x * x, axis=-1, keepdims=True) + EPS) * g


def silu(x):
    return x * jax.nn.sigmoid(x)


def _swap32_impl(x):
    n = x.shape[-1]
    lane = lax.broadcasted_iota(jnp.int32, x.shape, x.ndim - 1) % LANE
    up = pltpu.roll(x, n - 32, x.ndim - 1)
    dn = pltpu.roll(x, 32, x.ndim - 1)
    return jnp.where(lane < 32, up, jnp.where(lane < 64, dn, 0.0))


@jax.custom_vjp
def swap32(x):
    return _swap32_impl(x)


swap32.defvjp(lambda x: (_swap32_impl(x), None), lambda _, g: (_swap32_impl(g),))


def rope(x, cs, sn):
    return x * cs + swap32(x) * sn


def k1_tile(x, sh, sc, g_attn, g_q, g_kv, w_in, w_uq, w_ukv, p_in, p_uq, p_ukv, cs, sn, is_ctx):
    h = rmsn(x, g_attn) * (1.0 + sc) + sh
    p = mmw(h, w_in, p_in)
    rk = p[:, 512:1024] * 0.125
    rv = p[:, 1024:1536]
    ckv = p[:, 2432:2688]
    kpe = p[:, 2688:2816]
    kv = mmw(rmsn(ckv, g_kv), w_ukv, p_ukv)
    kn, v = kv[:, :512], kv[:, 512:]
    if is_ctx:
        return rk, rv, kn, kpe, v
    rq = p[:, 0:512]
    rg = p[:, 1536:2048]
    cq = p[:, 2048:2432]
    q = mmw(rmsn(cq, g_q), w_uq, p_uq)
    qn, qp = q[:, :512], q[:, 512:]
    return (rope(rq, cs, sn), rope(rk, cs, sn), rv, rg, qn, rope(qp, cs, sn), kn,
            rope(kpe, cs[:, :LANE], sn[:, :LANE]), v)


def log_sigmoid(x):
    return jnp.minimum(x, 0.0) - jnp.log(1.0 + jnp.exp(-jnp.abs(x)))


def ret_chunk(q, k, v, s, lg, reverse):
    c = RET_CHUNK
    ii = lax.broadcasted_iota(jnp.int32, (c, c), 0).astype(F32)
    jj = lax.broadcasted_iota(jnp.int32, (c, c), 1).astype(F32)
    diff = (jj - ii) if reverse else (ii - jj)
    dec = jnp.where(diff >= 0, jnp.exp(lg * jnp.maximum(diff, 0.0)), 0.0)
    pos = lax.broadcasted_iota(jnp.int32, (c, 1), 0).astype(F32)
    if reverse:
        wk, wq = jnp.exp(lg * pos), jnp.exp(lg * (c - pos))
    else:
        wk, wq = jnp.exp(lg * (c - 1.0 - pos)), jnp.exp(lg * (pos + 1.0))
    o = mm(mm_nt(q, k) * dec, v) + mm(q * wq, s)
    s_next = jnp.exp(lg * float(c)) * s + mm_tn(k * wk, v)
    return o, s_next


def ctx_state(kc, vc, lg, reverse):
    n = kc.shape[0]
    pos = lax.broadcasted_iota(jnp.int32, (n, 1), 0).astype(F32)
    w = jnp.exp(lg * pos) if reverse else jnp.exp(lg * (n - 1.0 - pos))
    return mm_tn(kc * w, vc)


def attn_head(qn, qp, kn, kp, v):
    s = (mm_nt(qn, kn) + mm_nt(qp, kp)) * (1.0 / math.sqrt(192.0))
    e = jnp.exp(s - jnp.max(s, axis=-1, keepdims=True))
    return mm(e / jnp.sum(e, axis=-1, keepdims=True), v)


def gn_gate(o, rg, g_ret):
    ys = []
    for h in range(N_HEADS):
        sl = slice(LANE * h, LANE * (h + 1))
        oh = o[:, sl]
        mu = jnp.mean(oh, axis=-1, keepdims=True)
        var = jnp.mean(jnp.square(oh - mu), axis=-1, keepdims=True)
        ys.append((oh - mu) * lax.rsqrt(var + EPS) * g_ret[:, sl])
    return jnp.concatenate(ys, axis=-1) * silu(rg)


def k4a_tile(x, o_f, o_b, rg, y_mla, g_ret, gt_a, g_ffn, sh_f, sc_f, w_out, p_out):
    mix = jnp.concatenate([gn_gate(o_f + o_b, rg, g_ret), y_mla], axis=-1)
    x_mid = x + gt_a * mmw(mix, w_out, p_out)
    h2 = rmsn(x_mid, g_ffn) * (1.0 + sc_f) + sh_f
    return x_mid, h2


def k4c_tile(x_mid, mlp, gt_f, g_final, tgt):
    y = rmsn(x_mid + gt_f * mlp, g_final)
    per_tok = jnp.mean(jnp.square(y - tgt), axis=-1, keepdims=True)
    return 0.5 * jnp.sum(per_tok, axis=0, keepdims=True)


def _cp(sem=None, vmem=VMEM_LIMIT):
    return pltpu.CompilerParams(dimension_semantics=sem, vmem_limit_bytes=vmem)


def _acc(ref, val, first):
    @pl.when(first)
    def _():
        ref[...] = val

    @pl.when(jnp.logical_not(first))
    def _():
        ref[...] += val


def _full(shape):
    nd = len(shape)
    return pl.BlockSpec(shape, lambda *_: (0,) * nd)


ANY = pl.BlockSpec(memory_space=pl.ANY)


def _sds(shape, dtype=F32):
    return jax.ShapeDtypeStruct(shape, dtype)


def _exchange(arrs, gather, name):
    n = len(arrs)
    out_shape = [_sds(((N_DEV,) + a.shape) if gather else a.shape, a.dtype) for a in arrs]

    def body(*refs):
        ins, outs = refs[:n], refs[n:2 * n]
        send_sems, recv_sems, local_sems = refs[2 * n:]
        x, y, c = lax.axis_index("x"), lax.axis_index("y"), lax.axis_index("c")
        me = 4 * x + 2 * y + c
        sends, recvs, locs = [], [], []
        for i in range(n):
            for k in range(N_DEV - 1):
                bits = k + 1
                px = x ^ ((bits >> 2) & 1)
                py = y ^ ((bits >> 1) & 1)
                pc = c ^ (bits & 1)
                peer = 4 * px + 2 * py + pc
                src = ins[i] if gather else ins[i].at[peer]
                sem = i * (N_DEV - 1) + k
                sends.append(pltpu.make_async_remote_copy(
                    src_ref=src, dst_ref=outs[i].at[me], send_sem=send_sems.at[sem], recv_sem=recv_sems.at[sem],
                    device_id=(px, py, pc), device_id_type=MESH))
                recvs.append(pltpu.make_async_remote_copy(
                    src_ref=src, dst_ref=outs[i].at[peer], send_sem=send_sems.at[sem], recv_sem=recv_sems.at[sem],
                    device_id=(px, py, pc), device_id_type=MESH))
            locs.append(pltpu.make_async_copy(ins[i] if gather else ins[i].at[me], outs[i].at[me], local_sems.at[i]))
        for cp in locs + sends:
            cp.start()
        for cp in recvs:
            cp.wait_recv()
        for cp in sends:
            cp.wait_send()
        for cp in locs:
            cp.wait()

    outs = pl.pallas_call(
        body, name=name, out_shape=out_shape, in_specs=[ANY] * n, out_specs=[ANY] * n,
        scratch_shapes=[pltpu.SemaphoreType.DMA((n * (N_DEV - 1),)), pltpu.SemaphoreType.DMA((n * (N_DEV - 1),)),
                        pltpu.SemaphoreType.DMA((n,))],
    )(*arrs)
    return list(outs)


HBM = pl.BlockSpec(memory_space=pltpu.HBM)
SEM = pl.BlockSpec(memory_space=pltpu.SEMAPHORE)
EFFECT = pltpu.SideEffectType.DATAFLOW_SIDE_EFFECTING


def _peer(k):
    x, y, c = lax.axis_index("x"), lax.axis_index("y"), lax.axis_index("c")
    bits = k + 1
    px, py, pc = x ^ ((bits >> 2) & 1), y ^ ((bits >> 1) & 1), c ^ (bits & 1)
    return (px, py, pc), 4 * px + 2 * py + pc, 4 * x + 2 * y + c


def _exchange_start(arrs, gather, name):
    n = len(arrs)
    lands = [pltpu.with_memory_space_constraint(lax.empty(((N_DEV,) + a.shape) if gather else a.shape, a.dtype),
                                                pltpu.HBM) for a in arrs]
    srcs = [pltpu.with_memory_space_constraint(a, pltpu.HBM) for a in arrs]

    def body(*refs):
        ins, zones = refs[:n], refs[n:2 * n]
        send_sems, recv_sems = refs[2 * n], refs[2 * n + 1]
        token = refs[-1]
        for i in range(n):
            for k in range(N_DEV - 1):
                dev, peer, me = _peer(k)
                sem = i * (N_DEV - 1) + k
                pltpu.make_async_remote_copy(
                    src_ref=ins[i] if gather else ins[i].at[peer], dst_ref=zones[i].at[me],
                    send_sem=send_sems.at[sem], recv_sem=recv_sems.at[sem], device_id=dev, device_id_type=MESH).start()
        token[...] = jnp.zeros_like(token)

    nsem = n * (N_DEV - 1)
    outs = pl.pallas_call(
        body, name=name,
        out_shape=[pltpu.SemaphoreType.DMA((nsem,)), pltpu.SemaphoreType.DMA((nsem,))]
        + [pltpu.HBM(a.shape, a.dtype) for a in srcs] + [pltpu.HBM(z.shape, z.dtype) for z in lands]
        + [_sds((8, LANE))],
        in_specs=[HBM] * (2 * n), out_specs=[SEM, SEM] + [HBM] * (2 * n) + [pl.BlockSpec(memory_space=pltpu.VMEM)],
        input_output_aliases={i: 2 + i for i in range(2 * n)},
        compiler_params=pltpu.CompilerParams(has_side_effects=EFFECT),
    )(*srcs, *lands)
    return {"n": n, "gather": gather, "sems": outs[:2], "srcs": outs[2:2 + n], "lands": outs[2 + n:2 + 2 * n],
            "token": outs[-1]}


def _exchange_wait(st, after, name):
    n, gather = st["n"], st["gather"]

    def body(*refs):
        ins, zones = refs[:n], refs[n:2 * n]
        send_sems, recv_sems = refs[2 * n], refs[2 * n + 1]
        local_sems = refs[-1]
        for i in range(n):
            for k in range(N_DEV - 1):
                dev, peer, me = _peer(k)
                sem = i * (N_DEV - 1) + k
                src = ins[i] if gather else ins[i].at[peer]
                cp = pltpu.make_async_remote_copy(
                    src_ref=src, dst_ref=zones[i].at[peer], send_sem=send_sems.at[sem], recv_sem=recv_sems.at[sem],
                    device_id=dev, device_id_type=MESH)
                cp.wait_send()
                cp.wait_recv()
        _, _, me = _peer(0)
        locs = [pltpu.make_async_copy(ins[i] if gather else ins[i].at[me], zones[i].at[me], local_sems.at[i])
                for i in range(n)]
        for cp in locs:
            cp.start()
        for cp in locs:
            cp.wait()

    outs = pl.pallas_call(
        body, name=name,
        out_shape=[pltpu.HBM(a.shape, a.dtype) for a in st["srcs"]] + [pltpu.HBM(z.shape, z.dtype) for z in st["lands"]],
        in_specs=[HBM] * (2 * n) + [SEM, SEM, ANY], out_specs=[HBM] * (2 * n),
        input_output_aliases={i: i for i in range(2 * n)},
        scratch_shapes=[pltpu.SemaphoreType.DMA((n,))],
        compiler_params=pltpu.CompilerParams(has_side_effects=EFFECT),
    )(*st["srcs"], *st["lands"], *st["sems"], after)
    return list(outs[n:])


def _mod_fwd(crows, w_ada, b_blk):
    def body(c_ref, w_ref, b_ref, o_ref):
        o_ref[...] = mm(silu(c_ref[...]), w_ref[...]) + b_ref[...]

    return pl.pallas_call(body, name="mod_fwd", out_shape=_sds((24, 768)), compiler_params=_cp())(crows, w_ada, b_blk)


def _mod_bwd(crows, w_ada, dmod_blk, dmodc_blk, dmod_full, dmodc_full):
    def body(c_ref, w_ref, d_ref, dc_ref, df_ref, dcf_ref, gw_ref, gc_ref, gb_ref):
        cr = c_ref[...]
        dc, dcf = dc_ref[0:1, :], dcf_ref[0:1, :]
        for p in range(1, N_DEV):
            dc = dc + dc_ref[p:p + 1, :]
            dcf = dcf + dcf_ref[p:p + 1, :]
        row = lax.broadcasted_iota(jnp.int32, (24, 1), 0)
        gw_ref[...] = mm_tn(silu(cr), jnp.where(row == 16, dc, d_ref[...]))
        cc = cr[16:17, :]
        sg = jax.nn.sigmoid(cc)
        part = mm_nt(jnp.broadcast_to(dc, (8, 768)), w_ref[...])
        gc_ref[...] = part * (sg * (1.0 + cc * (1.0 - sg)))
        gb_ref[...] = jnp.sum(df_ref[...], axis=0, keepdims=True) + dcf

    return pl.pallas_call(
        body, name="mod_bwd", out_shape=[_sds((D_MODEL, 768)), _sds((8, D_MODEL)), _sds((1, 6 * D_MODEL))],
        compiler_params=_cp())(crows, w_ada, dmod_blk, dmodc_blk, dmod_full, dmodc_full)


def _k1_fwd(x, mod, g_attn, g_q, g_kv, w_in, w_uq, w_ukv, cs, sn, is_ctx):
    b, l, _ = x.shape
    nt = l // TOK
    widths = (512, 512, 512, 128, 512) if is_ctx else (512, 512, 512, 512, 512, 512, 512, 128, 512)

    def body(x_ref, mod_ref, ga_ref, gq_ref, gk_ref, wi_ref, wq_ref, wk_ref, cs_ref, sn_ref, *outs):
        res = k1_tile(x_ref[...], mod_ref[0:1, :], mod_ref[1:2, :], ga_ref[...], gq_ref[...], gk_ref[...],
                      wi_ref[...], wq_ref[...], wk_ref[...], None, None, None, cs_ref[...], sn_ref[...], is_ctx)
        for o_ref, r in zip(outs, res):
            o_ref[...] = r

    tok = lambda w: pl.BlockSpec((None, TOK, w), lambda i, t: (i, t, 0))
    mod_spec = pl.BlockSpec((None, 8, D_MODEL), (lambda i, t: (0, 0, 0)) if is_ctx else (lambda i, t: (i, 0, 0)))
    return pl.pallas_call(
        body, name="k1_fwd_ctx" if is_ctx else "k1_fwd", grid=(b, nt),
        in_specs=[tok(D_MODEL), mod_spec, _full((1, D_MODEL)), _full((1, 384)), _full((1, 256)),
                  _full((D_MODEL, IN_PAD)), _full((384, 1024)), _full((256, 1024)),
                  pl.BlockSpec((TOK, 512), lambda i, t: (t, 0)), pl.BlockSpec((TOK, 512), lambda i, t: (t, 0))],
        out_specs=[tok(w) for w in widths], out_shape=[_sds((b, l, w)) for w in widths],
        compiler_params=_cp((ARB, ARB)),
    )(x, mod, g_attn, g_q, g_kv, w_in, w_uq, w_ukv, cs, sn)


def _k1_bwd(x, mod, g_attn, g_q, g_kv, w_in, w_uq, w_ukv, cs, sn, cts, dx_res, init, is_ctx):
    b, l, _ = x.shape
    tk = TOK_B
    nt = l // tk
    flat_cts = [a for group in cts for a in group]
    group_sizes = [len(g) for g in cts]
    n_ct = len(flat_cts)
    has_res = dx_res is not None
    has_init = init is not None
    acc_shapes = [(D_MODEL, IN_PAD), (384, 1024), (256, 1024), (1, D_MODEL), (1, 384), (1, 256)]

    def body(*refs):
        it = iter(refs)
        x_ref, mod_ref, ga_ref, gq_ref, gk_ref, wi_hbm, wq_hbm, wk_hbm, cs_ref, sn_ref = [next(it) for _ in range(10)]
        ct_refs = [next(it) for _ in range(n_ct)]
        res_ref = next(it) if has_res else None
        init_refs = [next(it) for _ in range(6)] if has_init else None
        gx_ref = next(it) if not is_ctx else None
        out_hbm = [next(it) for _ in range(6)]
        dmod_ref = next(it)
        wi_v, wq_v, wk_v = next(it), next(it), next(it)
        accs = [next(it) for _ in range(6)]
        sem = next(it)
        i, t = pl.program_id(0), pl.program_id(1)
        first = jnp.logical_and(i == 0, t == 0)
        last = jnp.logical_and(i == b - 1, t == nt - 1)

        @pl.when(first)
        def _():
            for src, dst in ((wi_hbm, wi_v), (wq_hbm, wq_v), (wk_hbm, wk_v)):
                pltpu.sync_copy(src, dst)
            for k in range(6):
                if has_init:
                    pltpu.sync_copy(init_refs[k], accs[k])
                else:
                    accs[k][...] = jnp.zeros(acc_shapes[k], F32)

        ct_vals, pos = [], 0
        for gsz in group_sizes:
            v = ct_refs[pos][...]
            for r in ct_refs[pos + 1:pos + gsz]:
                v = v + r[...]
            ct_vals.append(v)
            pos += gsz
        wi, wq, wk = wi_v[...], wq_v[...], wk_v[...]
        csv, snv = cs_ref[...], sn_ref[...]

        def f(xv, sh, sc, ga, gq, gk, p_in, p_uq, p_ukv):
            return k1_tile(xv, sh, sc, ga, gq, gk, wi, wq, wk, p_in, p_uq, p_ukv, csv, snv, is_ctx)

        probes = [jnp.zeros(s, F32) for s in acc_shapes[:3]]
        _, vjp = jax.vjp(f, x_ref[...], mod_ref[0:1, :], mod_ref[1:2, :], ga_ref[...], gq_ref[...], gk_ref[...], *probes)
        dx, dsh, dsc, dga, dgq, dgk, dwi, dwq, dwk = vjp(tuple(ct_vals))
        if not is_ctx:
            gx_ref[...] = dx + res_ref[...] if has_res else dx
        for ref, val in zip(accs, (dwi, dwq, dwk, dga, dgq, dgk)):
            ref[...] += val
        t0 = first if is_ctx else t == 0
        _acc(dmod_ref.at[0:1, :], dsh, t0)
        _acc(dmod_ref.at[1:2, :], dsc, t0)

        @pl.when(t0)
        def _():
            dmod_ref[2:8, :] = jnp.zeros((6, D_MODEL), F32)

        @pl.when(last)
        def _():
            cps = [pltpu.make_async_copy(accs[k], out_hbm[k], sem.at[k]) for k in range(6)]
            for cp in cps:
                cp.start()
            for cp in cps:
                cp.wait()

    tok = lambda w, off=0: pl.BlockSpec((None, tk, w), lambda i, t: (i, t + off, 0))
    mod_spec = pl.BlockSpec((None, 8, D_MODEL), (lambda i, t: (0, 0, 0)) if is_ctx else (lambda i, t: (i, 0, 0)))
    in_specs = [tok(D_MODEL), mod_spec, _full((1, D_MODEL)), _full((1, 384)), _full((1, 256)), ANY, ANY, ANY,
                pl.BlockSpec((tk, 512), lambda i, t: (t, 0)), pl.BlockSpec((tk, 512), lambda i, t: (t, 0))]
    args = [x, mod, g_attn, g_q, g_kv, w_in, w_uq, w_ukv, cs, sn]
    for a, off in flat_cts:
        in_specs.append(tok(a.shape[-1], off // tk))
        args.append(a)
    if has_res:
        in_specs.append(tok(D_MODEL))
        args.append(dx_res)
    if has_init:
        in_specs += [ANY] * 6
        args += list(init)
    out_shape, out_specs = [], []
    if not is_ctx:
        out_shape.append(_sds((b, l, D_MODEL)))
        out_specs.append(tok(D_MODEL))
    out_shape += [_sds(s) for s in acc_shapes] + [_sds((1 if is_ctx else b, 8, D_MODEL))]
    out_specs += [ANY] * 6 + [mod_spec]
    outs = pl.pallas_call(
        body, name="k1_bwd_ctx" if is_ctx else "k1_bwd", grid=(b, nt), in_specs=in_specs, out_specs=out_specs,
        out_shape=out_shape,
        scratch_shapes=[pltpu.VMEM((D_MODEL, IN_PAD), BF), pltpu.VMEM((384, 1024), BF), pltpu.VMEM((256, 1024), BF)]
        + [pltpu.VMEM(s, F32) for s in acc_shapes] + [pltpu.SemaphoreType.DMA((6,))],
        compiler_params=_cp((ARB, ARB)),
    )(*args)
    outs = list(outs)
    gx = None if is_ctx else outs.pop(0)
    return gx, outs[:6], outs[6]


def _chunk_spec(rev):
    if rev:
        return pl.BlockSpec((None, RET_CHUNK, LANE), lambda i, h, n: (i, N_CHUNK - 1 - n, h))
    return pl.BlockSpec((None, RET_CHUNK, LANE), lambda i, h, n: (i, n, h))


def _state_spec(rev):
    if rev:
        return pl.BlockSpec((None, None, None, LANE, LANE), lambda i, h, n: (i, h, N_CHUNK - 1 - n, 0, 0))
    return pl.BlockSpec((None, None, None, LANE, LANE), lambda i, h, n: (i, h, n, 0, 0))


_CTX_SPEC = pl.BlockSpec((None, CTX_LEN, LANE), lambda i, h, n: (i, 0, h))
_DEC_SPEC = pl.BlockSpec((None, 1, 1), lambda i, h, n: (h, 0, 0))


def _k2_fwd(rq, rk, rv, rkc, rvc, dec_f, dec_b):
    b = rq.shape[0]

    def body(qf, kf, vf, qb, kb, vb, kc, vc, df, db, of_ref, ob_ref, sf_out, sb_out, sf, sb):
        n = pl.program_id(2)
        lgf, lgb = log_sigmoid(df[...]), log_sigmoid(db[...])

        @pl.when(n == 0)
        def _():
            sf[...] = ctx_state(kc[...], vc[...], lgf, False)
            sb[...] = ctx_state(kc[...], vc[...], lgb, True)

        sf_out[...] = sf[...]
        sb_out[...] = sb[...]
        o, s = ret_chunk(qf[...], kf[...], vf[...], sf[...], lgf, False)
        of_ref[...] = o
        sf[...] = s
        o, s = ret_chunk(qb[...], kb[...], vb[...], sb[...], lgb, True)
        ob_ref[...] = o
        sb[...] = s

    l = rq.shape[1]
    return pl.pallas_call(
        body, name="k2_fwd", grid=(b, N_HEADS, N_CHUNK),
        in_specs=[_chunk_spec(False)] * 3 + [_chunk_spec(True)] * 3 + [_CTX_SPEC, _CTX_SPEC, _DEC_SPEC, _DEC_SPEC],
        out_specs=[_chunk_spec(False), _chunk_spec(True), _state_spec(False), _state_spec(True)],
        out_shape=[_sds((b, l, 512)), _sds((b, l, 512)), _sds((b, N_HEADS, N_CHUNK, LANE, LANE)),
                   _sds((b, N_HEADS, N_CHUNK, LANE, LANE))],
        scratch_shapes=[pltpu.VMEM((LANE, LANE), F32), pltpu.VMEM((LANE, LANE), F32)],
        compiler_params=_cp((ARB, ARB, ARB)),
    )(rq, rk, rv, rq, rk, rv, rkc, rvc, dec_f, dec_b)


def _k2_bwd(rq, rk, rv, do, sf_prev, sb_prev, rkc, rvc, dec_f, dec_b):
    b, l, _ = rq.shape

    def body(qf, kf, vf, gf, spf, qb, kb, vb, gb, spb, kc, vc, df, db,
             dqf, dkf, dvf, dqb, dkb, dvb, dkc, dvc, ddf, ddb, dsf, dsb):
        n = pl.program_id(2)

        @pl.when(n == 0)
        def _():
            dsf[...] = jnp.zeros((LANE, LANE), F32)
            dsb[...] = jnp.zeros((LANE, LANE), F32)

        def one(q, k, v, g, sp, dec, ds, dq, dk, dv, dd, rev):
            def f(qv, kv_, vv, sv, dcy):
                return ret_chunk(qv, kv_, vv, sv, log_sigmoid(dcy), rev)

            _, vjp = jax.vjp(f, q[...], k[...], v[...], sp[...], dec[...])
            gq, gk, gv, gs, gd = vjp((g[...], ds[...]))
            dq[...] = gq
            dk[...] = gk
            dv[...] = gv
            ds[...] = gs
            _acc(dd, jnp.broadcast_to(gd, (8, LANE)), n == 0)

        one(qf, kf, vf, gf, spf, df, dsf, dqf, dkf, dvf, ddf, False)
        one(qb, kb, vb, gb, spb, db, dsb, dqb, dkb, dvb, ddb, True)

        @pl.when(n == N_CHUNK - 1)
        def _():
            def f(kcv, vcv, dcy, rev):
                return ctx_state(kcv, vcv, log_sigmoid(dcy), rev)

            _, vjp_f = jax.vjp(functools.partial(f, rev=False), kc[...], vc[...], df[...])
            gk_f, gv_f, gd_f = vjp_f(dsf[...])
            _, vjp_b = jax.vjp(functools.partial(f, rev=True), kc[...], vc[...], db[...])
            gk_b, gv_b, gd_b = vjp_b(dsb[...])
            dkc[...] = gk_f + gk_b
            dvc[...] = gv_f + gv_b
            ddf[...] += jnp.broadcast_to(gd_f, (8, LANE))
            ddb[...] += jnp.broadcast_to(gd_b, (8, LANE))

    dd_spec = pl.BlockSpec((None, None, 8, LANE), lambda i, h, n: (i, h, 0, 0))
    return pl.pallas_call(
        body, name="k2_bwd", grid=(b, N_HEADS, N_CHUNK),
        in_specs=[_chunk_spec(True)] * 4 + [_state_spec(True)] + [_chunk_spec(False)] * 4 + [_state_spec(False)]
        + [_CTX_SPEC, _CTX_SPEC, _DEC_SPEC, _DEC_SPEC],
        out_specs=[_chunk_spec(True)] * 3 + [_chunk_spec(False)] * 3 + [_CTX_SPEC, _CTX_SPEC, dd_spec, dd_spec],
        out_shape=[_sds((b, l, 512))] * 6 + [_sds((b, CTX_LEN, 512))] * 2 + [_sds((b, N_HEADS, 8, LANE))] * 2,
        scratch_shapes=[pltpu.VMEM((LANE, LANE), F32), pltpu.VMEM((LANE, LANE), F32)],
        compiler_params=_cp((ARB, ARB, ARB)),
    )(rq, rk, rv, do, sf_prev, rq, rk, rv, do, sb_prev, rkc, rvc, dec_f, dec_b)


TQ = 256
TQ_B = 128


def _k3_fwd(qn, qp, kn, kp, v):
    b, l, _ = qn.shape

    def body(qn_ref, qp_ref, kn_ref, kp_ref, v_ref, o_ref):
        kpv = kp_ref[...]
        for h in range(N_HEADS):
            sl = slice(LANE * h, LANE * (h + 1))
            o_ref[:, sl] = attn_head(qn_ref[:, sl], qp_ref[:, sl], kn_ref[:, sl], kpv, v_ref[:, sl])

    qs = pl.BlockSpec((None, TQ, 512), lambda i, t: (i, t, 0))
    ks = lambda w: pl.BlockSpec((None, KV_LEN, w), lambda i, t: (i, 0, 0))
    return pl.pallas_call(
        body, name="k3_fwd", grid=(b, l // TQ), in_specs=[qs, qs, ks(512), ks(LANE), ks(512)], out_specs=qs,
        out_shape=_sds((b, l, 512)), compiler_params=_cp((ARB, ARB)),
    )(qn, qp, kn, kp, v)


def _k3_bwd(qn, qp, kn, kp, v, dy):
    b, l, _ = qn.shape

    def body(qn_ref, qp_ref, kn_ref, kp_ref, v_ref, dy_ref, dqn_ref, dqp_ref, dkn_ref, dkp_ref, dv_ref):
        t0 = pl.program_id(1) == 0
        kpv = kp_ref[...].astype(F32)
        dkp = jnp.zeros((KV_LEN, LANE), F32)
        for h in range(N_HEADS):
            sl = slice(LANE * h, LANE * (h + 1))
            _, vjp = jax.vjp(attn_head, qn_ref[:, sl], qp_ref[:, sl], kn_ref[:, sl].astype(F32), kpv,
                             v_ref[:, sl].astype(F32))
            gqn, gqp, gkn, gkp, gv = vjp(dy_ref[:, sl])
            dqn_ref[:, sl] = gqn
            dqp_ref[:, sl] = gqp
            _acc(dkn_ref.at[:, sl], gkn, t0)
            _acc(dv_ref.at[:, sl], gv, t0)
            dkp = dkp + gkp
        _acc(dkp_ref, dkp, t0)

    qs = pl.BlockSpec((None, TQ_B, 512), lambda i, t: (i, t, 0))
    ks = lambda w: pl.BlockSpec((None, KV_LEN, w), lambda i, t: (i, 0, 0))
    return pl.pallas_call(
        body, name="k3_bwd", grid=(b, l // TQ_B), in_specs=[qs, qs, ks(512), ks(LANE), ks(512), qs],
        out_specs=[qs, qs, ks(512), ks(LANE), ks(512)],
        out_shape=[_sds((b, l, 512)), _sds((b, l, 512)), _sds((b, KV_LEN, 512)), _sds((b, KV_LEN, LANE)),
                   _sds((b, KV_LEN, 512))],
        compiler_params=_cp((ARB, ARB)),
    )(qn, qp, kn, kp, v, dy)


def _mod_rows(mod_ref, rows):
    return [mod_ref[r:r + 1, :] for r in rows]


def _k4a_fwd(x, o_f, o_b, rg, y_mla, g_ret, w_out, mod, g_ffn):
    b, l, _ = x.shape

    def body(x_ref, of_ref, ob_ref, rg_ref, ym_ref, gr_ref, wo_ref, mod_ref, gf_ref, xm_ref, h2_ref):
        gt_a, sh_f, sc_f = _mod_rows(mod_ref, (2, 3, 4))
        x_mid, h2 = k4a_tile(x_ref[...], of_ref[...], ob_ref[...], rg_ref[...], ym_ref[...], gr_ref[...], gt_a,
                             gf_ref[...], sh_f, sc_f, wo_ref[...], None)
        xm_ref[...] = x_mid
        h2_ref[...] = h2.astype(BF)

    tok = lambda w: pl.BlockSpec((None, TOK, w), lambda i, t: (i, t, 0))
    mod_spec = pl.BlockSpec((None, 8, D_MODEL), lambda i, t: (i, 0, 0))
    return pl.pallas_call(
        body, name="k4a_fwd", grid=(b, l // TOK),
        in_specs=[tok(D_MODEL), tok(512), tok(512), tok(512), tok(512), _full((1, 512)), _full((D_MODEL, D_MODEL)),
                  mod_spec, _full((1, D_MODEL))],
        out_specs=[tok(D_MODEL), tok(D_MODEL)], out_shape=[_sds((b, l, D_MODEL)), _sds((b, l, D_MODEL), BF)],
        compiler_params=_cp((ARB, ARB)),
    )(x, o_f, o_b, rg, y_mla, g_ret, w_out, mod, g_ffn)


def _k4b_mlp_loss(h2, w1, w2, x_mid, mod, g_final, tgt):
    b, l, _ = h2.shape
    nt = l // TOK

    def body(h2_ref, w1_ref, w2_ref, xm_ref, mod_ref, gfin_ref, tgt_ref, dxm_ref, dmlp_ref, loss_ref, dgt_ref,
             dgfin_ref, acc):
        i, t, j = pl.program_id(0), pl.program_id(1), pl.program_id(2)
        a = _dot(h2_ref[...], w1_ref[...], 1, 0)
        part = mm(jnp.square(jnp.maximum(a, 0.0)), w2_ref[...])
        _acc(acc, part, j == 0)

        @pl.when(j == N_DEV - 1)
        def _():
            (gt_f,) = _mod_rows(mod_ref, (5,))
            loss, vjp = jax.vjp(k4c_tile, xm_ref[...], acc[...], gt_f, gfin_ref[...], tgt_ref[...])
            dxm, dmlp, dgt, dgfin, _ = vjp(jnp.ones((1, 1), F32))
            dxm_ref[...] = dxm
            dmlp_ref[...] = dmlp.astype(BF)
            first = jnp.logical_and(i == 0, t == 0)
            _acc(loss_ref, jnp.broadcast_to(loss, (8, LANE)), first)
            _acc(dgfin_ref, dgfin, first)
            _acc(dgt_ref, dgt, t == 0)

    tok = lambda w: pl.BlockSpec((None, TOK, w), lambda i, t, j: (i, t, 0))
    return pl.pallas_call(
        body, name="k4b_mlp_loss", grid=(b, nt, N_DEV),
        in_specs=[tok(D_MODEL), pl.BlockSpec((None, D_MODEL, FF_BLK), lambda i, t, j: (j, 0, 0)),
                  pl.BlockSpec((None, FF_BLK, D_MODEL), lambda i, t, j: (j, 0, 0)), tok(D_MODEL),
                  pl.BlockSpec((None, 8, D_MODEL), lambda i, t, j: (i, 0, 0)),
                  pl.BlockSpec((1, D_MODEL), lambda i, t, j: (0, 0)), tok(D_MODEL)],
        out_specs=[tok(D_MODEL), tok(D_MODEL), pl.BlockSpec((8, LANE), lambda i, t, j: (0, 0)),
                   pl.BlockSpec((None, 1, D_MODEL), lambda i, t, j: (i, 0, 0)),
                   pl.BlockSpec((1, D_MODEL), lambda i, t, j: (0, 0))],
        out_shape=[_sds((b, l, D_MODEL)), _sds((b, l, D_MODEL), BF), _sds((8, LANE)), _sds((b, 1, D_MODEL)),
                   _sds((1, D_MODEL))],
        scratch_shapes=[pltpu.VMEM((TOK, D_MODEL), F32)],
        compiler_params=_cp((ARB, ARB, ARB)),
    )(h2, w1, w2, x_mid, mod, g_final, tgt)


def _k4d_mlp_bwd(h2, dmlp, w1, w2):
    b, l, _ = h2.shape
    nt = l // TOK

    def body(h2_ref, dm_ref, w1_ref, w2_ref, dh2_ref, dw1_hbm, dw2_hbm, dw1, dw2, st1, st2, sem):
        i, t, j = pl.program_id(0), pl.program_id(1), pl.program_id(2)
        first = jnp.logical_and(i == 0, t == 0)
        h2v, dm = h2_ref[...], dm_ref[...]
        a = _dot(h2v, w1_ref[...], 1, 0)
        r = jnp.maximum(a, 0.0)
        dhid = _dot(dm, w2_ref[...], 1, 1)
        g2 = _dot(jnp.square(r), dm, 0, 0)
        da = dhid * (2.0 * r)
        g1 = _dot(h2v, da, 0, 0)
        _acc(dw1.at[j], g1, first)
        _acc(dw2.at[j], g2, first)
        _acc(dh2_ref, _dot(da, w1_ref[...], 1, 1), j == 0)

        @pl.when(jnp.logical_and(jnp.logical_and(i == b - 1, t == nt - 1), j == N_DEV - 1))
        def _():
            for k in range(N_DEV):
                st1[...] = dw1[k].astype(BF)
                st2[...] = dw2[k].astype(BF)
                c1 = pltpu.make_async_copy(st1, dw1_hbm.at[k], sem.at[0])
                c2 = pltpu.make_async_copy(st2, dw2_hbm.at[k], sem.at[1])
                c1.start()
                c2.start()
                c1.wait()
                c2.wait()

    tok = lambda w: pl.BlockSpec((None, TOK, w), lambda i, t, j: (i, t, 0))
    return pl.pallas_call(
        body, name="k4d_mlp_bwd", grid=(b, nt, N_DEV),
        in_specs=[tok(D_MODEL), tok(D_MODEL), pl.BlockSpec((None, D_MODEL, FF_BLK), lambda i, t, j: (j, 0, 0)),
                  pl.BlockSpec((None, FF_BLK, D_MODEL), lambda i, t, j: (j, 0, 0))],
        out_specs=[tok(D_MODEL), ANY, ANY],
        out_shape=[_sds((b, l, D_MODEL)), _sds((N_DEV, D_MODEL, FF_BLK), BF), _sds((N_DEV, FF_BLK, D_MODEL), BF)],
        scratch_shapes=[pltpu.VMEM((N_DEV, D_MODEL, FF_BLK), F32), pltpu.VMEM((N_DEV, FF_BLK, D_MODEL), F32),
                        pltpu.VMEM((D_MODEL, FF_BLK), BF), pltpu.VMEM((FF_BLK, D_MODEL), BF),
                        pltpu.SemaphoreType.DMA((2,))],
        compiler_params=_cp((ARB, ARB, ARB)),
    )(h2, dmlp, w1, w2)


def _k4e_bwd(x, o_f, o_b, rg, y_mla, g_ret, w_out, mod, g_ffn, dxm, dh2):
    b, l, _ = x.shape

    def body(x_ref, of_ref, ob_ref, rg_ref, ym_ref, gr_ref, wo_ref, mod_ref, gf_ref, dxm_ref, dh2_ref,
             dx_ref, do_ref, drg_ref, dym_ref, dwo_ref, dgr_ref, dgf_ref, dmod_ref):
        i, t = pl.program_id(0), pl.program_id(1)
        first = jnp.logical_and(i == 0, t == 0)
        gt_a, sh_f, sc_f = _mod_rows(mod_ref, (2, 3, 4))
        wo = wo_ref[...]

        def f(xv, ofv, rgv, ymv, grv, gta, gfv, shf, scf, p_out):
            return k4a_tile(xv, ofv, ob_ref[...], rgv, ymv, grv, gta, gfv, shf, scf, wo, p_out)

        _, vjp = jax.vjp(f, x_ref[...], of_ref[...], rg_ref[...], ym_ref[...], gr_ref[...], gt_a, gf_ref[...], sh_f,
                         sc_f, jnp.zeros((D_MODEL, D_MODEL), F32))
        dx, do, drg, dym, dgr, dgta, dgf, dshf, dscf, dwo = vjp((dxm_ref[...], dh2_ref[...]))
        dx_ref[...] = dx
        do_ref[...] = do
        drg_ref[...] = drg
        dym_ref[...] = dym
        _acc(dwo_ref, dwo, first)
        _acc(dgr_ref, dgr, first)
        _acc(dgf_ref, dgf, first)
        t0 = t == 0
        _acc(dmod_ref.at[2:3, :], dgta, t0)
        _acc(dmod_ref.at[3:4, :], dshf, t0)
        _acc(dmod_ref.at[4:5, :], dscf, t0)

        @pl.when(t0)
        def _():
            dmod_ref[0:2, :] = jnp.zeros((2, D_MODEL), F32)
            dmod_ref[5:8, :] = jnp.zeros((3, D_MODEL), F32)

    tok = lambda w: pl.BlockSpec((None, TOK_B, w), lambda i, t: (i, t, 0))
    mod_spec = pl.BlockSpec((None, 8, D_MODEL), lambda i, t: (i, 0, 0))
    return pl.pallas_call(
        body, name="k4e_bwd", grid=(b, l // TOK_B),
        in_specs=[tok(D_MODEL), tok(512), tok(512), tok(512), tok(512), _full((1, 512)), _full((D_MODEL, D_MODEL)),
                  mod_spec, _full((1, D_MODEL)), tok(D_MODEL), tok(D_MODEL)],
        out_specs=[tok(D_MODEL), tok(512), tok(512), tok(512), _full((D_MODEL, D_MODEL)), _full((1, 512)),
                   _full((1, D_MODEL)), mod_spec],
        out_shape=[_sds((b, l, D_MODEL)), _sds((b, l, 512)), _sds((b, l, 512)), _sds((b, l, 512)),
                   _sds((D_MODEL, D_MODEL)), _sds((1, 512)), _sds((1, D_MODEL)), _sds((b, 8, D_MODEL))],
        compiler_params=_cp((ARB, ARB)),
    )(x, o_f, o_b, rg, y_mla, g_ret, w_out, mod, g_ffn, dxm, dh2)


def _adamw(w, m, v, pieces, name):
    r, c = w.shape
    npc = pieces.shape[0]
    rb = r
    for cand in (256, 128, 64, 32, 16, 8):
        if r > cand and r % cand == 0 and cand * c * 4 * (npc + 7) * 2 <= 24 * 1024 * 1024:
            rb = cand
            break

    def body(w_ref, m_ref, v_ref, p_ref, g_ref, d_ref, nm_ref, nv_ref):
        g = p_ref[0].astype(F32)
        for k in range(1, npc):
            g = g + p_ref[k].astype(F32)
        wv = w_ref[...]
        mn = ADAM_B1 * m_ref[...] + (1.0 - ADAM_B1) * g
        vn = ADAM_B2 * v_ref[...] + (1.0 - ADAM_B2) * jnp.square(g)
        m_hat = mn / (1.0 - ADAM_B1 ** ADAM_STEP)
        v_hat = vn / (1.0 - ADAM_B2 ** ADAM_STEP)
        g_ref[...] = g
        d_ref[...] = -ADAM_LR * (m_hat / (jnp.sqrt(v_hat) + ADAM_EPS) + ADAM_WD * wv)
        nm_ref[...] = mn
        nv_ref[...] = vn

    blk = pl.BlockSpec((rb, c), lambda i: (i, 0))
    return pl.pallas_call(
        body, name=name, grid=(r // rb,), in_specs=[blk, blk, blk, pl.BlockSpec((npc, rb, c), lambda i: (0, i, 0))],
        out_specs=[blk] * 4, out_shape=[_sds((r, c))] * 4, compiler_params=_cp((ARB,)),
    )(w, m, v, pieces)


def _pad_heads(w, d):
    k = w.shape[0]
    return jnp.pad(w.reshape(k, N_HEADS, d), ((0, 0), (0, 0), (0, LANE - d))).reshape(k, N_HEADS * LANE)


def _cut_heads(w, d):
    k = w.shape[0]
    return w.reshape(k, N_HEADS, LANE)[:, :, :d].reshape(k, N_HEADS * d)


def _w_in_pad(w):
    return jnp.concatenate([_pad_heads(w[:, 0:256], 64), _pad_heads(w[:, 256:512], 64), w[:, 512:2176],
                            jnp.pad(w[:, 2176:2240], ((0, 0), (0, 64)))], axis=1)


def _w_in_cut(g):
    return jnp.concatenate([_cut_heads(g[:, 0:512], 64), _cut_heads(g[:, 512:1024], 64), g[:, 1024:2688],
                            g[:, 2688:2752]], axis=1)


def _w_uq_pad(w):
    w = w.reshape(384, N_HEADS, 192)
    return jnp.concatenate([w[:, :, :128].reshape(384, 512),
                            jnp.pad(w[:, :, 128:], ((0, 0), (0, 0), (0, 64))).reshape(384, 512)], axis=1)


def _w_uq_cut(g):
    return jnp.concatenate([g[:, :512].reshape(384, N_HEADS, 128), g[:, 512:].reshape(384, N_HEADS, 128)[:, :, :64]],
                           axis=2).reshape(384, 768)


def _w_ukv_perm(w):
    w = w.reshape(256, N_HEADS, 256)
    return jnp.concatenate([w[:, :, :128].reshape(256, 512), w[:, :, 128:].reshape(256, 512)], axis=1)


def _w_ukv_unperm(g):
    return jnp.concatenate([g[:, :512].reshape(256, N_HEADS, 128), g[:, 512:].reshape(256, N_HEADS, 128)],
                           axis=2).reshape(256, 1024)


def _unshard_cols(g):
    return jnp.transpose(g, (1, 0, 2)).reshape(g.shape[1], N_DEV * g.shape[2])


def _shard_cols(w):
    k, n = w.shape
    return jnp.transpose(w.reshape(k, N_DEV, n // N_DEV), (1, 0, 2))


def _rope_tables():
    rows = SEQ // GRID_W
    row = jnp.repeat(jnp.arange(rows, dtype=F32), GRID_W)
    col = jnp.tile(jnp.arange(GRID_W, dtype=F32), rows)
    freq = ROPE_BASE ** (-jnp.arange(16, dtype=F32) / 16)
    ang = jnp.concatenate([row[:, None] * freq, col[:, None] * freq], axis=-1)
    cos, sin = jnp.cos(ang), jnp.sin(ang)
    z = jnp.zeros((SEQ, 64), F32)
    cs = jnp.concatenate([cos, cos, z], axis=1)
    sn = jnp.concatenate([-sin, sin, z], axis=1)
    return jnp.tile(cs, (1, N_HEADS)), jnp.tile(sn, (1, N_HEADS))


_SMALL = (("c_ctx", 1024), ("b_ada", 6144), ("g_attn", 1024), ("g_ffn", 1024), ("ret_decay_fwd", 128),
          ("ret_decay_bwd", 128), ("g_ret", 512), ("g_q_lora", 384), ("g_kv_lora", 256), ("g_final", 1024))
_SMALL_REAL = {"c_ctx": 1024, "b_ada": 6144, "g_attn": 1024, "g_ffn": 1024, "ret_decay_fwd": 4, "ret_decay_bwd": 4,
               "g_ret": 512, "g_q_lora": 384, "g_kv_lora": 256, "g_final": 1024}
_SMALL_ROWS = sum(n for _, n in _SMALL) // LANE


def _pack_small(vals):
    parts = []
    for name, n in _SMALL:
        a = vals[name].reshape(-1).astype(F32)
        parts.append(jnp.pad(a, (0, n - a.shape[0])))
    return jnp.concatenate(parts).reshape(_SMALL_ROWS, LANE)


def _unpack_small(packed, shapes):
    flat = packed.reshape(-1)
    out, off = {}, 0
    for name, n in _SMALL:
        out[name] = flat[off:off + _SMALL_REAL[name]].reshape(shapes[name])
        off += n
    return out


def kernel(x, c, ctx, c_ctx, w_ada, b_ada, g_attn, g_ffn, w_in, ret_decay_fwd, ret_decay_bwd, g_ret, g_q_lora, w_uq, g_kv_lora, w_ukv, w_out, w_ff1, w_ff2, g_final, loss_target, m_c_ctx, m_w_ada, m_b_ada, m_g_attn, m_g_ffn, m_w_in, m_ret_decay_fwd, m_ret_decay_bwd, m_g_ret, m_g_q_lora, m_w_uq, m_g_kv_lora, m_w_ukv, m_w_out, m_w_ff1, m_w_ff2, m_g_final, v_c_ctx, v_w_ada, v_b_ada, v_g_attn, v_g_ffn, v_w_in, v_ret_decay_fwd, v_ret_decay_bwd, v_g_ret, v_g_q_lora, v_w_uq, v_g_kv_lora, v_w_ukv, v_w_out, v_w_ff1, v_w_ff2, v_g_final):
    me = 4 * lax.axis_index("x") + 2 * lax.axis_index("y") + lax.axis_index("c")
    nb = x.shape[0]

    c_pad = jnp.pad(c, ((0, 8 - nb), (0, 0)))
    c_all, g_in, g_uq, g_ukv = _exchange([c_pad, w_in[0].astype(BF), w_uq[0].astype(BF), w_ukv[0].astype(BF)], True,
                                         "gather_weights")
    st_g = _exchange_start([w_out[0].astype(BF), w_ff1[0].astype(BF), w_ff2[0].astype(BF)], True, "gather_ff_start")
    tok_g = st_g["token"][0:1, 0:1]
    wi = _w_in_pad(_unshard_cols(g_in))
    wq = _w_uq_pad(_unshard_cols(g_uq))
    wk = _w_ukv_perm(_unshard_cols(g_ukv))

    crows = jnp.concatenate([c_all[:, :nb].reshape(N_DEV * nb, D_MODEL), c_ctx[None], jnp.zeros((7, D_MODEL), F32)])
    crows = crows + tok_g
    b_blk = lax.dynamic_slice(b_ada, (0, me * 768), (1, 768))
    (mod_g,) = _exchange([_mod_fwd(crows, w_ada[0], b_blk)], True, "gather_mod")
    mod_all = _unshard_cols(mod_g)
    mod_mine = lax.dynamic_slice(mod_all, (me * nb, 0), (nb, 6 * D_MODEL)).reshape(nb, 6, D_MODEL)
    mod = jnp.pad(mod_mine, ((0, 0), (0, 2), (0, 0)))
    mod_c = jnp.pad(mod_all[16].reshape(1, 6, D_MODEL), ((0, 0), (0, 2), (0, 0)))

    cs, sn = _rope_tables()
    dec_f = ret_decay_fwd.reshape(N_HEADS, 1, 1)
    dec_b = ret_decay_bwd.reshape(N_HEADS, 1, 1)

    rkc, rvc, knc, kpc, vc = _k1_fwd(ctx, mod_c, g_attn, g_q_lora, g_kv_lora, wi, wq, wk, cs, sn, True)
    rq, rk, rv, rg, qn, qp, kn, kp, vv = _k1_fwd(x, mod, g_attn, g_q_lora, g_kv_lora, wi, wq, wk, cs, sn, False)
    o_f, o_b, sf_prev, sb_prev = _k2_fwd(rq, rk, rv, rkc, rvc, dec_f, dec_b)
    kn_all = jnp.concatenate([knc, kn], axis=1).astype(BF)
    kp_all = jnp.concatenate([kpc, kp], axis=1).astype(BF)
    v_all = jnp.concatenate([vc, vv], axis=1).astype(BF)
    y_mla = _k3_fwd(qn, qp, kn_all, kp_all, v_all)
    g_out, g_ff1, g_ff2 = _exchange_wait(st_g, y_mla, "gather_ff_wait")
    wo = g_out.reshape(D_MODEL, D_MODEL)
    x_mid, h2 = _k4a_fwd(x, o_f, o_b, rg, y_mla, g_ret, wo, mod, g_ffn)
    dxm, dmlp, loss_acc, dgt_f, dg_final = _k4b_mlp_loss(h2, g_ff1, g_ff2, x_mid, mod, g_final.reshape(1, D_MODEL),
                                                         loss_target)

    dh2, dw1, dw2 = _k4d_mlp_bwd(h2, dmlp, g_ff1, g_ff2)
    st_s = _exchange_start([dw1, dw2], False, "scatter_ff_start")
    g_ret_t = g_ret + st_s["token"][0:1, 0:1]
    dx_res, do, drg, dym, dwo, dg_ret, dg_ffn, dmod_a = _k4e_bwd(x, o_f, o_b, rg, y_mla, g_ret_t, wo, mod, g_ffn, dxm, dh2)
    dqn, dqp, dkn_all, dkp_all, dv_all = _k3_bwd(qn, qp, kn_all, kp_all, v_all, dym)
    dqf, dkf, dvf, dqb, dkb, dvb, dkc, dvc, ddf, ddb = _k2_bwd(rq, rk, rv, do, sf_prev, sb_prev, rkc, rvc, dec_f, dec_b)
    cts = [[(dqf, 0), (dqb, 0)], [(dkf, 0), (dkb, 0)], [(dvf, 0), (dvb, 0)], [(drg, 0)], [(dqn, 0)], [(dqp, 0)],
           [(dkn_all, CTX_LEN)], [(dkp_all, CTX_LEN)], [(dv_all, CTX_LEN)]]
    grad_x, accs, dmod_1 = _k1_bwd(x, mod, g_attn, g_q_lora, g_kv_lora, wi, wq, wk, cs, sn, cts, dx_res, None, False)
    cts_c = [[(dkc, 0)], [(dvc, 0)], [(dkn_all, 0)], [(dkp_all, 0)], [(dv_all, 0)]]
    _, accs, dmod_c1 = _k1_bwd(ctx, mod_c, g_attn, g_q_lora, g_kv_lora, wi, wq, wk, cs, sn, cts_c, None, accs, True)
    dwi, dwq, dwk, dg_attn, dg_q, dg_kv = accs

    dmod_loc = (dmod_a + dmod_1).at[:, 5, :].set(dgt_f[:, 0, :])[:, :6, :].reshape(nb, 6 * D_MODEL)
    dmod_ctx = dmod_c1[:, :6, :].reshape(1, 6 * D_MODEL)
    small = {"c_ctx": jnp.zeros((D_MODEL,), F32), "b_ada": jnp.zeros((6 * D_MODEL,), F32), "g_attn": dg_attn,
             "g_ffn": dg_ffn, "ret_decay_fwd": jnp.sum(ddf[:, :, 0, 0], axis=0), "ret_decay_bwd": jnp.sum(ddb[:, :, 0, 0], axis=0),
             "g_ret": dg_ret, "g_q_lora": dg_q, "g_kv_lora": dg_kv, "g_final": dg_final}
    extra = jnp.concatenate([dmod_loc, dmod_ctx, jnp.zeros((5, 6 * D_MODEL), F32)]).reshape(8 * 48, LANE)
    loss_rows = loss_acc
    sm_g, ex_g, loss_g = _exchange([_pack_small(small), extra, loss_rows], True, "gather_small")
    ex_g = ex_g.reshape(N_DEV, 8, 6 * D_MODEL)
    dmod_all = ex_g[:, :nb].reshape(N_DEV * nb, 6 * D_MODEL)
    dmodc_parts = ex_g[:, nb]
    dmod_full = jnp.concatenate([dmod_all, jnp.zeros((8, 6 * D_MODEL), F32)])
    dmod_blk = lax.dynamic_slice(dmod_full, (0, me * 768), (24, 768))
    dmodc_blk = lax.dynamic_slice(dmodc_parts, (0, me * 768), (N_DEV, 768))
    gw_ada, gcc_part, gb_ada = _mod_bwd(crows, w_ada[0], dmod_blk, dmodc_blk, dmod_full, dmodc_parts)
    (gcc_g,) = _exchange([gcc_part], True, "gather_c_ctx")

    p_ff1, p_ff2 = _exchange_wait(st_s, dmod_c1, "scatter_ff_wait")
    pieces = _exchange([_shard_cols(_w_in_cut(dwi)).astype(BF), _shard_cols(_w_uq_cut(dwq)).astype(BF),
                        _shard_cols(_w_ukv_unperm(dwk)).astype(BF), dwo.reshape(N_DEV, 128, D_MODEL).astype(BF)],
                       False, "scatter_grads") + [p_ff1, p_ff2]

    res = {}
    big = (("w_in", w_in, m_w_in, v_w_in, pieces[0]), ("w_uq", w_uq, m_w_uq, v_w_uq, pieces[1]),
           ("w_ukv", w_ukv, m_w_ukv, v_w_ukv, pieces[2]), ("w_out", w_out, m_w_out, v_w_out, pieces[3]),
           ("w_ff1", w_ff1, m_w_ff1, v_w_ff1, pieces[4]), ("w_ff2", w_ff2, m_w_ff2, v_w_ff2, pieces[5]),
           ("w_ada", w_ada, m_w_ada, v_w_ada, gw_ada[None]))
    for name, w, m, v, pcs in big:
        res[name] = [a[None] for a in _adamw(w[0], m[0], v[0], pcs, "adamw_" + name)]

    smalls = {"c_ctx": (c_ctx, m_c_ctx, v_c_ctx), "b_ada": (b_ada, m_b_ada, v_b_ada), "g_attn": (g_attn, m_g_attn, v_g_attn),
              "g_ffn": (g_ffn, m_g_ffn, v_g_ffn), "ret_decay_fwd": (ret_decay_fwd, m_ret_decay_fwd, v_ret_decay_fwd),
              "ret_decay_bwd": (ret_decay_bwd, m_ret_decay_bwd, v_ret_decay_bwd), "g_ret": (g_ret, m_g_ret, v_g_ret),
              "g_q_lora": (g_q_lora, m_g_q_lora, v_g_q_lora), "g_kv_lora": (g_kv_lora, m_g_kv_lora, v_g_kv_lora),
              "g_final": (g_final, m_g_final, v_g_final)}
    sm_pieces = sm_g.reshape(N_DEV, _SMALL_ROWS * LANE)
    sm_pieces = sm_pieces.at[:, 0:1024].set(gcc_g[:, 0, :])
    sm_pieces = sm_pieces.at[0, 1024:1024 + 6144].set(gb_ada[0])
    sm_pieces = sm_pieces.reshape(N_DEV, _SMALL_ROWS, LANE)
    packed = [_pack_small({k: t[i] for k, t in smalls.items()}) for i in range(3)]
    sm_out = _adamw(packed[0], packed[1], packed[2], sm_pieces, "adamw_small")
    shapes = {k: t[0].shape for k, t in smalls.items()}
    sm_res = [_unpack_small(o, shapes) for o in sm_out]
    for name in smalls:
        res[name] = [r[name] for r in sm_res]

    loss = loss_g[0, 0, 0]
    for k in range(1, N_DEV):
        loss = loss + loss_g[k, 0, 0]

    order = ("c_ctx", "w_ada", "b_ada", "g_attn", "g_ffn", "w_in", "ret_decay_fwd", "ret_decay_bwd", "g_ret", "g_q_lora",
             "w_uq", "g_kv_lora", "w_ukv", "w_out", "w_ff1", "w_ff2", "g_final")
    return (loss, grad_x, *[res[n][0] for n in order], *[res[n][1] for n in order], *[res[n][2] for n in order],
            *[res[n][3] for n in order])
```

```python
import functools
import math

import jax
import jax.numpy as jnp
from jax import lax
from jax.experimental import pallas as pl
from jax.experimental.pallas import tpu as pltpu

F32 = jnp.float32
BF = jnp.bfloat16
EPS = 1e-6
LANE = 128
N_DEV = 8
D_MODEL = 1024
SEQ = 2048
CTX_LEN = 256
GRID_W = 64
N_HEADS = 4
RET_CHUNK = 128
N_CHUNK = SEQ // RET_CHUNK
D_FF = 4096
FF_BLK = D_FF // N_DEV
IN_PAD = 2816
KV_LEN = CTX_LEN + SEQ
ROPE_BASE = 10000.0
ADAM_LR, ADAM_B1, ADAM_B2, ADAM_EPS, ADAM_WD, ADAM_STEP = 0.001, 0.9, 0.999, 1e-08, 0.01, 10
TOK = 256
TOK_B = 128
VMEM_LIMIT = 56 * 1024 * 1024
ARB = "arbitrary"
MESH = pl.DeviceIdType.MESH


def _dot(a, b, ca, cb):
    return lax.dot_general(a.astype(BF), b.astype(BF), (((ca,), (cb,)), ((), ())), preferred_element_type=F32)


@jax.custom_vjp
def mm(a, b):
    return _dot(a, b, 1, 0)


@jax.custom_vjp
def mm_nt(a, b):
    return _dot(a, b, 1, 1)


@jax.custom_vjp
def mm_tn(a, b):
    return _dot(a, b, 0, 0)


mm.defvjp(lambda a, b: (_dot(a, b, 1, 0), (a, b)), lambda r, g: (mm_nt(g, r[1]), mm_tn(r[0], g)))
mm_nt.defvjp(lambda a, b: (_dot(a, b, 1, 1), (a, b)), lambda r, g: (mm(g, r[1]), mm_tn(g, r[0])))
mm_tn.defvjp(lambda a, b: (_dot(a, b, 0, 0), (a, b)), lambda r, g: (mm_nt(r[1], g), mm(r[0], g)))


@jax.custom_vjp
def _mmw(a, w, probe):
    return _dot(a, w, 1, 0)


def _mmw_bwd(r, g):
    a, w = r
    return mm_nt(g, w), jnp.zeros_like(w), mm_tn(a, g)


_mmw.defvjp(lambda a, w, probe: (_dot(a, w, 1, 0), (a, w)), _mmw_bwd)


def mmw(a, w, probe):
    return _dot(a, w, 1, 0) if probe is None else _mmw(a, w, probe)


def rmsn(x, g):
    return x * lax.rsqrt(jnp.mean(x * x, axis=-1, keepdims=True) + EPS) * g


def silu(x):
    return x * jax.nn.sigmoid(x)


def _swap32_impl(x):
    n = x.shape[-1]
    lane = lax.broadcasted_iota(jnp.int32, x.shape, x.ndim - 1) % LANE
    up = pltpu.roll(x, n - 32, x.ndim - 1)
    dn = pltpu.roll(x, 32, x.ndim - 1)
    return jnp.where(lane < 32, up, jnp.where(lane < 64, dn, 0.0))


@jax.custom_vjp
def swap32(x):
    return _swap32_impl(x)


swap32.defvjp(lambda x: (_swap32_impl(x), None), lambda _, g: (_swap32_impl(g),))


def rope(x, cs, sn):
    return x * cs + swap32(x) * sn


def k1_tile(x, sh, sc, g_attn, g_q, g_kv, w_in, w_uq, w_ukv, p_in, p_uq, p_ukv, cs, sn, is_ctx):
    h = rmsn(x, g_attn) * (1.0 + sc) + sh
    p = mmw(h, w_in, p_in)
    rk = p[:, 512:1024] * 0.125
    rv = p[:, 1024:1536]
    ckv = p[:, 2432:2688]
    kpe = p[:, 2688:2816]
    kv = mmw(rmsn(ckv, g_kv), w_ukv, p_ukv)
    kn, v = kv[:, :512], kv[:, 512:]
    if is_ctx:
        return rk, rv, kn, kpe, v
    rq = p[:, 0:512]
    rg = p[:, 1536:2048]
    cq = p[:, 2048:2432]
    q = mmw(rmsn(cq, g_q), w_uq, p_uq)
    qn, qp = q[:, :512], q[:, 512:]
    return (rope(rq, cs, sn), rope(rk, cs, sn), rv, rg, qn, rope(qp, cs, sn), kn,
            rope(kpe, cs[:, :LANE], sn[:, :LANE]), v)


def log_sigmoid(x):
    return jnp.minimum(x, 0.0) - jnp.log(1.0 + jnp.exp(-jnp.abs(x)))


def ret_chunk(q, k, v, s, lg, reverse):
    c = RET_CHUNK
    ii = lax.broadcasted_iota(jnp.int32, (c, c), 0).astype(F32)
    jj = lax.broadcasted_iota(jnp.int32, (c, c), 1).astype(F32)
    diff = (jj - ii) if reverse else (ii - jj)
    dec = jnp.where(diff >= 0, jnp.exp(lg * jnp.maximum(diff, 0.0)), 0.0)
    pos = lax.broadcasted_iota(jnp.int32, (c, 1), 0).astype(F32)
    if reverse:
        wk, wq = jnp.exp(lg * pos), jnp.exp(lg * (c - pos))
    else:
        wk, wq = jnp.exp(lg * (c - 1.0 - pos)), jnp.exp(lg * (pos + 1.0))
    o = mm(mm_nt(q, k) * dec, v) + mm(q * wq, s)
    s_next = jnp.exp(lg * float(c)) * s + mm_tn(k * wk, v)
    return o, s_next


def ctx_state(kc, vc, lg, reverse):
    n = kc.shape[0]
    pos = lax.broadcasted_iota(jnp.int32, (n, 1), 0).astype(F32)
    w = jnp.exp(lg * pos) if reverse else jnp.exp(lg * (n - 1.0 - pos))
    return mm_tn(kc * w, vc)


def attn_head(qn, qp, kn, kp, v):
    s = (mm_nt(qn, kn) + mm_nt(qp, kp)) * (1.0 / math.sqrt(192.0))
    e = jnp.exp(s - jnp.max(s, axis=-1, keepdims=True))
    return mm(e / jnp.sum(e, axis=-1, keepdims=True), v)


def gn_gate(o, rg, g_ret):
    ys = []
    for h in range(N_HEADS):
        sl = slice(LANE * h, LANE * (h + 1))
        oh = o[:, sl]
        mu = jnp.mean(oh, axis=-1, keepdims=True)
        var = jnp.mean(jnp.square(oh - mu), axis=-1, keepdims=True)
        ys.append((oh - mu) * lax.rsqrt(var + EPS) * g_ret[:, sl])
    return jnp.concatenate(ys, axis=-1) * silu(rg)


def k4a_tile(x, o_f, o_b, rg, y_mla, g_ret, gt_a, g_ffn, sh_f, sc_f, w_out, p_out):
    mix = jnp.concatenate([gn_gate(o_f + o_b, rg, g_ret), y_mla], axis=-1)
    x_mid = x + gt_a * mmw(mix, w_out, p_out)
    h2 = rmsn(x_mid, g_ffn) * (1.0 + sc_f) + sh_f
    return x_mid, h2


def k4c_tile(x_mid, mlp, gt_f, g_final, tgt):
    y = rmsn(x_mid + gt_f * mlp, g_final)
    per_tok = jnp.mean(jnp.square(y - tgt), axis=-1, keepdims=True)
    return 0.5 * jnp.sum(per_tok, axis=0, keepdims=True)


def _cp(sem=None, vmem=VMEM_LIMIT):
    return pltpu.CompilerParams(dimension_semantics=sem, vmem_limit_bytes=vmem)


def _acc(ref, val, first):
    @pl.when(first)
    def _():
        ref[...] = val

    @pl.when(jnp.logical_not(first))
    def _():
        ref[...] += val


def _full(shape):
    nd = len(shape)
    return pl.BlockSpec(shape, lambda *_: (0,) * nd)


ANY = pl.BlockSpec(memory_space=pl.ANY)


def _sds(shape, dtype=F32):
    return jax.ShapeDtypeStruct(shape, dtype)


def _exchange(arrs, gather, name):
    n = len(arrs)
    out_shape = [_sds(((N_DEV,) + a.shape) if gather else a.shape, a.dtype) for a in arrs]

    def body(*refs):
        ins, outs = refs[:n], refs[n:2 * n]
        send_sems, recv_sems, local_sems = refs[2 * n:]
        x, y, c = lax.axis_index("x"), lax.axis_index("y"), lax.axis_index("c")
        me = 4 * x + 2 * y + c
        sends, recvs, locs = [], [], []
        for i in range(n):
            for k in range(N_DEV - 1):
                bits = k + 1
                px = x ^ ((bits >> 2) & 1)
                py = y ^ ((bits >> 1) & 1)
                pc = c ^ (bits & 1)
                peer = 4 * px + 2 * py + pc
                src = ins[i] if gather else ins[i].at[peer]
                sem = i * (N_DEV - 1) + k
                sends.append(pltpu.make_async_remote_copy(
                    src_ref=src, dst_ref=outs[i].at[me], send_sem=send_sems.at[sem], recv_sem=recv_sems.at[sem],
                    device_id=(px, py, pc), device_id_type=MESH))
                recvs.append(pltpu.make_async_remote_copy(
                    src_ref=src, dst_ref=outs[i].at[peer], send_sem=send_sems.at[sem], recv_sem=recv_sems.at[sem],
                    device_id=(px, py, pc), device_id_type=MESH))
            locs.append(pltpu.make_async_copy(ins[i] if gather else ins[i].at[me], outs[i].at[me], local_sems.at[i]))
        for cp in locs + sends:
            cp.start()
        for cp in recvs:
            cp.wait_recv()
        for cp in sends:
            cp.wait_send()
        for cp in locs:
            cp.wait()

    outs = pl.pallas_call(
        body, name=name, out_shape=out_shape, in_specs=[ANY] * n, out_specs=[ANY] * n,
        scratch_shapes=[pltpu.SemaphoreType.DMA((n * (N_DEV - 1),)), pltpu.SemaphoreType.DMA((n * (N_DEV - 1),)),
                        pltpu.SemaphoreType.DMA((n,))],
    )(*arrs)
    return list(outs)


HBM = pl.BlockSpec(memory_space=pltpu.HBM)
SEM = pl.BlockSpec(memory_space=pltpu.SEMAPHORE)
EFFECT = pltpu.SideEffectType.DATAFLOW_SIDE_EFFECTING


def _peer(k):
    x, y, c = lax.axis_index("x"), lax.axis_index("y"), lax.axis_index("c")
    bits = k + 1
    px, py, pc = x ^ ((bits >> 2) & 1), y ^ ((bits >> 1) & 1), c ^ (bits & 1)
    return (px, py, pc), 4 * px + 2 * py + pc, 4 * x + 2 * y + c


def _exchange_start(arrs, gather, name):
    n = len(arrs)
    lands = [pltpu.with_memory_space_constraint(lax.empty(((N_DEV,) + a.shape) if gather else a.shape, a.dtype),
                                                pltpu.HBM) for a in arrs]
    srcs = [pltpu.with_memory_space_constraint(a, pltpu.HBM) for a in arrs]

    def body(*refs):
        ins, zones = refs[:n], refs[n:2 * n]
        send_sems, recv_sems, local_sems = refs[2 * n:2 * n + 3]
        token = refs[-1]
        for i in range(n):
            for k in range(N_DEV - 1):
                dev, peer, me = _peer(k)
                sem = i * (N_DEV - 1) + k
                pltpu.make_async_remote_copy(
                    src_ref=ins[i] if gather else ins[i].at[peer], dst_ref=zones[i].at[me],
                    send_sem=send_sems.at[sem], recv_sem=recv_sems.at[sem], device_id=dev, device_id_type=MESH).start()
            _, _, me = _peer(0)
            pltpu.make_async_copy(ins[i] if gather else ins[i].at[me], zones[i].at[me], local_sems.at[i]).start()
        token[...] = jnp.zeros_like(token)

    nsem = n * (N_DEV - 1)
    outs = pl.pallas_call(
        body, name=name,
        out_shape=[pltpu.SemaphoreType.DMA((nsem,)), pltpu.SemaphoreType.DMA((nsem,)), pltpu.SemaphoreType.DMA((n,))]
        + [pltpu.HBM(a.shape, a.dtype) for a in srcs] + [pltpu.HBM(z.shape, z.dtype) for z in lands]
        + [_sds((8, LANE))],
        in_specs=[HBM] * (2 * n),
        out_specs=[SEM, SEM, SEM] + [HBM] * (2 * n) + [pl.BlockSpec(memory_space=pltpu.VMEM)],
        input_output_aliases={i: 3 + i for i in range(2 * n)},
        compiler_params=pltpu.CompilerParams(has_side_effects=EFFECT),
    )(*srcs, *lands)
    return {"n": n, "gather": gather, "sems": outs[:3], "srcs": outs[3:3 + n], "lands": outs[3 + n:3 + 2 * n],
            "token": outs[-1]}


def _exchange_wait(st, after, name):
    n, gather = st["n"], st["gather"]

    def body(*refs):
        ins, zones = refs[:n], refs[n:2 * n]
        send_sems, recv_sems, local_sems = refs[2 * n:2 * n + 3]
        for i in range(n):
            for k in range(N_DEV - 1):
                dev, peer, me = _peer(k)
                sem = i * (N_DEV - 1) + k
                src = ins[i] if gather else ins[i].at[peer]
                cp = pltpu.make_async_remote_copy(
                    src_ref=src, dst_ref=zones[i].at[peer], send_sem=send_sems.at[sem], recv_sem=recv_sems.at[sem],
                    device_id=dev, device_id_type=MESH)
                cp.wait_send()
                cp.wait_recv()
            _, _, me = _peer(0)
            pltpu.make_async_copy(ins[i] if gather else ins[i].at[me], zones[i].at[me], local_sems.at[i]).wait()

    outs = pl.pallas_call(
        body, name=name,
        out_shape=[pltpu.HBM(a.shape, a.dtype) for a in st["srcs"]] + [pltpu.HBM(z.shape, z.dtype) for z in st["lands"]],
        in_specs=[HBM] * (2 * n) + [SEM, SEM, SEM, ANY], out_specs=[HBM] * (2 * n),
        input_output_aliases={i: i for i in range(2 * n)},
        compiler_params=pltpu.CompilerParams(has_side_effects=EFFECT),
    )(*st["srcs"], *st["lands"], *st["sems"], after)
    return list(outs[n:])


def _mod_fwd(crows, w_ada, b_blk):
    def body(c_ref, w_ref, b_ref, o_ref):
        o_ref[...] = mm(silu(c_ref[...]), w_ref[...]) + b_ref[...]

    return pl.pallas_call(body, name="mod_fwd", out_shape=_sds((24, 768)), compiler_params=_cp())(crows, w_ada, b_blk)


def _mod_bwd(crows, w_ada, dmod_blk, dmodc_blk, dmod_full, dmodc_full):
    def body(c_ref, w_ref, d_ref, dc_ref, df_ref, dcf_ref, gw_ref, gc_ref, gb_ref):
        cr = c_ref[...]
        dc, dcf = dc_ref[0:1, :], dcf_ref[0:1, :]
        for p in range(1, N_DEV):
            dc = dc + dc_ref[p:p + 1, :]
            dcf = dcf + dcf_ref[p:p + 1, :]
        row = lax.broadcasted_iota(jnp.int32, (24, 1), 0)
        gw_ref[...] = mm_tn(silu(cr), jnp.where(row == 16, dc, d_ref[...]))
        cc = cr[16:17, :]
        sg = jax.nn.sigmoid(cc)
        part = mm_nt(jnp.broadcast_to(dc, (8, 768)), w_ref[...])
        gc_ref[...] = part * (sg * (1.0 + cc * (1.0 - sg)))
        gb_ref[...] = jnp.sum(df_ref[...], axis=0, keepdims=True) + dcf

    return pl.pallas_call(
        body, name="mod_bwd", out_shape=[_sds((D_MODEL, 768)), _sds((8, D_MODEL)), _sds((1, 6 * D_MODEL))],
        compiler_params=_cp())(crows, w_ada, dmod_blk, dmodc_blk, dmod_full, dmodc_full)


def _k1_fwd(x, mod, g_attn, g_q, g_kv, w_in, w_uq, w_ukv, cs, sn, is_ctx):
    b, l, _ = x.shape
    nt = l // TOK
    widths = (512, 512, 512, 128, 512) if is_ctx else (512, 512, 512, 512, 512, 512, 512, 128, 512)

    def body(x_ref, mod_ref, ga_ref, gq_ref, gk_ref, wi_ref, wq_ref, wk_ref, cs_ref, sn_ref, *outs):
        res = k1_tile(x_ref[...], mod_ref[0:1, :], mod_ref[1:2, :], ga_ref[...], gq_ref[...], gk_ref[...],
                      wi_ref[...], wq_ref[...], wk_ref[...], None, None, None, cs_ref[...], sn_ref[...], is_ctx)
        for o_ref, r in zip(outs, res):
            o_ref[...] = r

    tok = lambda w: pl.BlockSpec((None, TOK, w), lambda i, t: (i, t, 0))
    mod_spec = pl.BlockSpec((None, 8, D_MODEL), (lambda i, t: (0, 0, 0)) if is_ctx else (lambda i, t: (i, 0, 0)))
    return pl.pallas_call(
        body, name="k1_fwd_ctx" if is_ctx else "k1_fwd", grid=(b, nt),
        in_specs=[tok(D_MODEL), mod_spec, _full((1, D_MODEL)), _full((1, 384)), _full((1, 256)),
                  _full((D_MODEL, IN_PAD)), _full((384, 1024)), _full((256, 1024)),
                  pl.BlockSpec((TOK, 512), lambda i, t: (t, 0)), pl.BlockSpec((TOK, 512), lambda i, t: (t, 0))],
        out_specs=[tok(w) for w in widths], out_shape=[_sds((b, l, w)) for w in widths],
        compiler_params=_cp((ARB, ARB)),
    )(x, mod, g_attn, g_q, g_kv, w_in, w_uq, w_ukv, cs, sn)


def _k1_bwd(x, mod, g_attn, g_q, g_kv, w_in, w_uq, w_ukv, cs, sn, cts, dx_res, init, is_ctx):
    b, l, _ = x.shape
    tk = TOK_B
    nt = l // tk
    flat_cts = [a for group in cts for a in group]
    group_sizes = [len(g) for g in cts]
    n_ct = len(flat_cts)
    has_res = dx_res is not None
    has_init = init is not None
    acc_shapes = [(D_MODEL, IN_PAD), (384, 1024), (256, 1024), (1, D_MODEL), (1, 384), (1, 256)]

    def body(*refs):
        it = iter(refs)
        x_ref, mod_ref, ga_ref, gq_ref, gk_ref, wi_hbm, wq_hbm, wk_hbm, cs_ref, sn_ref = [next(it) for _ in range(10)]
        ct_refs = [next(it) for _ in range(n_ct)]
        res_ref = next(it) if has_res else None
        init_refs = [next(it) for _ in range(6)] if has_init else None
        gx_ref = next(it) if not is_ctx else None
        out_hbm = [next(it) for _ in range(6)]
        dmod_ref = next(it)
        wi_v, wq_v, wk_v = next(it), next(it), next(it)
        accs = [next(it) for _ in range(6)]
        sem = next(it)
        i, t = pl.program_id(0), pl.program_id(1)
        first = jnp.logical_and(i == 0, t == 0)
        last = jnp.logical_and(i == b - 1, t == nt - 1)

        @pl.when(first)
        def _():
            for src, dst in ((wi_hbm, wi_v), (wq_hbm, wq_v), (wk_hbm, wk_v)):
                pltpu.sync_copy(src, dst)
            for k in range(6):
                if has_init:
                    pltpu.sync_copy(init_refs[k], accs[k])
                else:
                    accs[k][...] = jnp.zeros(acc_shapes[k], F32)

        ct_vals, pos = [], 0
        for gsz in group_sizes:
            v = ct_refs[pos][...]
            for r in ct_refs[pos + 1:pos + gsz]:
                v = v + r[...]
            ct_vals.append(v)
            pos += gsz
        wi, wq, wk = wi_v[...], wq_v[...], wk_v[...]
        csv, snv = cs_ref[...], sn_ref[...]

        def f(xv, sh, sc, ga, gq, gk, p_in, p_uq, p_ukv):
            return k1_tile(xv, sh, sc, ga, gq, gk, wi, wq, wk, p_in, p_uq, p_ukv, csv, snv, is_ctx)

        probes = [jnp.zeros(s, F32) for s in acc_shapes[:3]]
        _, vjp = jax.vjp(f, x_ref[...], mod_ref[0:1, :], mod_ref[1:2, :], ga_ref[...], gq_ref[...], gk_ref[...], *probes)
        dx, dsh, dsc, dga, dgq, dgk, dwi, dwq, dwk = vjp(tuple(ct_vals))
        if not is_ctx:
            gx_ref[...] = dx + res_ref[...] if has_res else dx
        for ref, val in zip(accs, (dwi, dwq, dwk, dga, dgq, dgk)):
            ref[...] += val
        t0 = first if is_ctx else t == 0
        _acc(dmod_ref.at[0:1, :], dsh, t0)
        _acc(dmod_ref.at[1:2, :], dsc, t0)

        @pl.when(t0)
        def _():
            dmod_ref[2:8, :] = jnp.zeros((6, D_MODEL), F32)

        @pl.when(last)
        def _():
            cps = [pltpu.make_async_copy(accs[k], out_hbm[k], sem.at[k]) for k in range(6)]
            for cp in cps:
                cp.start()
            for cp in cps:
                cp.wait()

    tok = lambda w, off=0: pl.BlockSpec((None, tk, w), lambda i, t: (i, t + off, 0))
    mod_spec = pl.BlockSpec((None, 8, D_MODEL), (lambda i, t: (0, 0, 0)) if is_ctx else (lambda i, t: (i, 0, 0)))
    in_specs = [tok(D_MODEL), mod_spec, _full((1, D_MODEL)), _full((1, 384)), _full((1, 256)), ANY, ANY, ANY,
                pl.BlockSpec((tk, 512), lambda i, t: (t, 0)), pl.BlockSpec((tk, 512), lambda i, t: (t, 0))]
    args = [x, mod, g_attn, g_q, g_kv, w_in, w_uq, w_ukv, cs, sn]
    for a, off in flat_cts:
        in_specs.append(tok(a.shape[-1], off // tk))
        args.append(a)
    if has_res:
        in_specs.append(tok(D_MODEL))
        args.append(dx_res)
    if has_init:
        in_specs += [ANY] * 6
        args += list(init)
    out_shape, out_specs = [], []
    if not is_ctx:
        out_shape.append(_sds((b, l, D_MODEL)))
        out_specs.append(tok(D_MODEL))
    out_shape += [_sds(s) for s in acc_shapes] + [_sds((1 if is_ctx else b, 8, D_MODEL))]
    out_specs += [ANY] * 6 + [mod_spec]
    outs = pl.pallas_call(
        body, name="k1_bwd_ctx" if is_ctx else "k1_bwd", grid=(b, nt), in_specs=in_specs, out_specs=out_specs,
        out_shape=out_shape,
        scratch_shapes=[pltpu.VMEM((D_MODEL, IN_PAD), BF), pltpu.VMEM((384, 1024), BF), pltpu.VMEM((256, 1024), BF)]
        + [pltpu.VMEM(s, F32) for s in acc_shapes] + [pltpu.SemaphoreType.DMA((6,))],
        compiler_params=_cp((ARB, ARB)),
    )(*args)
    outs = list(outs)
    gx = None if is_ctx else outs.pop(0)
    return gx, outs[:6], outs[6]


def _chunk_spec(rev):
    if rev:
        return pl.BlockSpec((None, RET_CHUNK, 512), lambda i, n: (i, N_CHUNK - 1 - n, 0))
    return pl.BlockSpec((None, RET_CHUNK, 512), lambda i, n: (i, n, 0))


def _state_spec(rev):
    if rev:
        return pl.BlockSpec((None, N_HEADS, None, LANE, LANE), lambda i, n: (i, 0, N_CHUNK - 1 - n, 0, 0))
    return pl.BlockSpec((None, N_HEADS, None, LANE, LANE), lambda i, n: (i, 0, n, 0, 0))


_CTX_SPEC = pl.BlockSpec((None, CTX_LEN, 512), lambda i, n: (i, 0, 0))
_DEC_SPEC = pl.BlockSpec((N_HEADS, 1, 1), lambda i, n: (0, 0, 0))
_HEAD_SL = [slice(LANE * h, LANE * (h + 1)) for h in range(N_HEADS)]


def _k2_fwd(rq, rk, rv, rkc, rvc, dec_f, dec_b):
    b = rq.shape[0]

    def body(qf, kf, vf, qb, kb, vb, kc, vc, df, db, of_ref, ob_ref, sf_out, sb_out, sf, sb):
        n = pl.program_id(1)
        for h, sl in enumerate(_HEAD_SL):
            lgf, lgb = log_sigmoid(df[h]), log_sigmoid(db[h])

            @pl.when(n == 0)
            def _():
                sf[h] = ctx_state(kc[:, sl], vc[:, sl], lgf, False)
                sb[h] = ctx_state(kc[:, sl], vc[:, sl], lgb, True)

            sf_out[h] = sf[h]
            sb_out[h] = sb[h]
            o, s = ret_chunk(qf[:, sl], kf[:, sl], vf[:, sl], sf[h], lgf, False)
            of_ref[:, sl] = o
            sf[h] = s
            o, s = ret_chunk(qb[:, sl], kb[:, sl], vb[:, sl], sb[h], lgb, True)
            ob_ref[:, sl] = o
            sb[h] = s

    l = rq.shape[1]
    return pl.pallas_call(
        body, name="k2_fwd", grid=(b, N_CHUNK),
        in_specs=[_chunk_spec(False)] * 3 + [_chunk_spec(True)] * 3 + [_CTX_SPEC, _CTX_SPEC, _DEC_SPEC, _DEC_SPEC],
        out_specs=[_chunk_spec(False), _chunk_spec(True), _state_spec(False), _state_spec(True)],
        out_shape=[_sds((b, l, 512)), _sds((b, l, 512)), _sds((b, N_HEADS, N_CHUNK, LANE, LANE)),
                   _sds((b, N_HEADS, N_CHUNK, LANE, LANE))],
        scratch_shapes=[pltpu.VMEM((N_HEADS, LANE, LANE), F32), pltpu.VMEM((N_HEADS, LANE, LANE), F32)],
        compiler_params=_cp((ARB, ARB)),
    )(rq, rk, rv, rq, rk, rv, rkc, rvc, dec_f, dec_b)


def _k2_bwd(rq, rk, rv, do, sf_prev, sb_prev, rkc, rvc, dec_f, dec_b):
    b, l, _ = rq.shape

    def body(qf, kf, vf, gf, spf, qb, kb, vb, gb, spb, kc, vc, df, db,
             dqf, dkf, dvf, dqb, dkb, dvb, dkc, dvc, ddf, ddb, dsf, dsb):
        n = pl.program_id(1)

        @pl.when(n == 0)
        def _():
            dsf[...] = jnp.zeros((N_HEADS, LANE, LANE), F32)
            dsb[...] = jnp.zeros((N_HEADS, LANE, LANE), F32)

        def one(h, sl, q, k, v, g, sp, dec, ds, dq, dk, dv, dd, rev):
            def f(qv, kv_, vv, sv, dcy):
                return ret_chunk(qv, kv_, vv, sv, log_sigmoid(dcy), rev)

            _, vjp = jax.vjp(f, q[:, sl], k[:, sl], v[:, sl], sp[h], dec[h])
            gq, gk, gv, gs, gd = vjp((g[:, sl], ds[h]))
            dq[:, sl] = gq
            dk[:, sl] = gk
            dv[:, sl] = gv
            ds[h] = gs
            _acc(dd.at[h], jnp.broadcast_to(gd, (8, LANE)), n == 0)

        for h, sl in enumerate(_HEAD_SL):
            one(h, sl, qf, kf, vf, gf, spf, df, dsf, dqf, dkf, dvf, ddf, False)
            one(h, sl, qb, kb, vb, gb, spb, db, dsb, dqb, dkb, dvb, ddb, True)

        @pl.when(n == N_CHUNK - 1)
        def _():
            def f(kcv, vcv, dcy, rev):
                return ctx_state(kcv, vcv, log_sigmoid(dcy), rev)

            for h, sl in enumerate(_HEAD_SL):
                _, vjp_f = jax.vjp(functools.partial(f, rev=False), kc[:, sl], vc[:, sl], df[h])
                gk_f, gv_f, gd_f = vjp_f(dsf[h])
                _, vjp_b = jax.vjp(functools.partial(f, rev=True), kc[:, sl], vc[:, sl], db[h])
                gk_b, gv_b, gd_b = vjp_b(dsb[h])
                dkc[:, sl] = gk_f + gk_b
                dvc[:, sl] = gv_f + gv_b
                ddf[h] += jnp.broadcast_to(gd_f, (8, LANE))
                ddb[h] += jnp.broadcast_to(gd_b, (8, LANE))

    dd_spec = pl.BlockSpec((None, N_HEADS, 8, LANE), lambda i, n: (i, 0, 0, 0))
    return pl.pallas_call(
        body, name="k2_bwd", grid=(b, N_CHUNK),
        in_specs=[_chunk_spec(True)] * 4 + [_state_spec(True)] + [_chunk_spec(False)] * 4 + [_state_spec(False)]
        + [_CTX_SPEC, _CTX_SPEC, _DEC_SPEC, _DEC_SPEC],
        out_specs=[_chunk_spec(True)] * 3 + [_chunk_spec(False)] * 3 + [_CTX_SPEC, _CTX_SPEC, dd_spec, dd_spec],
        out_shape=[_sds((b, l, 512))] * 6 + [_sds((b, CTX_LEN, 512))] * 2 + [_sds((b, N_HEADS, 8, LANE))] * 2,
        scratch_shapes=[pltpu.VMEM((N_HEADS, LANE, LANE), F32), pltpu.VMEM((N_HEADS, LANE, LANE), F32)],
        compiler_params=_cp((ARB, ARB)),
    )(rq, rk, rv, do, sf_prev, rq, rk, rv, do, sb_prev, rkc, rvc, dec_f, dec_b)


TQ = 256
TQ_B = 128


def _k3_fwd(qn, qp, kn, kp, v):
    b, l, _ = qn.shape

    def body(qn_ref, qp_ref, kn_ref, kp_ref, v_ref, o_ref):
        kpv = kp_ref[...]
        for h in range(N_HEADS):
            sl = slice(LANE * h, LANE * (h + 1))
            o_ref[:, sl] = attn_head(qn_ref[:, sl], qp_ref[:, sl], kn_ref[:, sl], kpv, v_ref[:, sl])

    qs = pl.BlockSpec((None, TQ, 512), lambda i, t: (i, t, 0))
    ks = lambda w: pl.BlockSpec((None, KV_LEN, w), lambda i, t: (i, 0, 0))
    return pl.pallas_call(
        body, name="k3_fwd", grid=(b, l // TQ), in_specs=[qs, qs, ks(512), ks(LANE), ks(512)], out_specs=qs,
        out_shape=_sds((b, l, 512)), compiler_params=_cp((ARB, ARB)),
    )(qn, qp, kn, kp, v)


def _k3_bwd(qn, qp, kn, kp, v, dy):
    b, l, _ = qn.shape

    def body(qn_ref, qp_ref, kn_ref, kp_ref, v_ref, dy_ref, dqn_ref, dqp_ref, dkn_ref, dkp_ref, dv_ref):
        t0 = pl.program_id(1) == 0
        kpv = kp_ref[...].astype(F32)
        dkp = jnp.zeros((KV_LEN, LANE), F32)
        for h in range(N_HEADS):
            sl = slice(LANE * h, LANE * (h + 1))
            _, vjp = jax.vjp(attn_head, qn_ref[:, sl], qp_ref[:, sl], kn_ref[:, sl].astype(F32), kpv,
                             v_ref[:, sl].astype(F32))
            gqn, gqp, gkn, gkp, gv = vjp(dy_ref[:, sl])
            dqn_ref[:, sl] = gqn
            dqp_ref[:, sl] = gqp
            _acc(dkn_ref.at[:, sl], gkn, t0)
            _acc(dv_ref.at[:, sl], gv, t0)
            dkp = dkp + gkp
        _acc(dkp_ref, dkp, t0)

    qs = pl.BlockSpec((None, TQ_B, 512), lambda i, t: (i, t, 0))
    ks = lambda w: pl.BlockSpec((None, KV_LEN, w), lambda i, t: (i, 0, 0))
    return pl.pallas_call(
        body, name="k3_bwd", grid=(b, l // TQ_B), in_specs=[qs, qs, ks(512), ks(LANE), ks(512), qs],
        out_specs=[qs, qs, ks(512), ks(LANE), ks(512)],
        out_shape=[_sds((b, l, 512)), _sds((b, l, 512)), _sds((b, KV_LEN, 512)), _sds((b, KV_LEN, LANE)),
                   _sds((b, KV_LEN, 512))],
        compiler_params=_cp((ARB, ARB)),
    )(qn, qp, kn, kp, v, dy)


def _mod_rows(mod_ref, rows):
    return [mod_ref[r:r + 1, :] for r in rows]


def _k4a_fwd(x, o_f, o_b, rg, y_mla, g_ret, w_out, mod, g_ffn):
    b, l, _ = x.shape

    def body(x_ref, of_ref, ob_ref, rg_ref, ym_ref, gr_ref, wo_ref, mod_ref, gf_ref, xm_ref, h2_ref):
        gt_a, sh_f, sc_f = _mod_rows(mod_ref, (2, 3, 4))
        x_mid, h2 = k4a_tile(x_ref[...], of_ref[...], ob_ref[...], rg_ref[...], ym_ref[...], gr_ref[...], gt_a,
                             gf_ref[...], sh_f, sc_f, wo_ref[...], None)
        xm_ref[...] = x_mid
        h2_ref[...] = h2.astype(BF)

    tok = lambda w: pl.BlockSpec((None, TOK, w), lambda i, t: (i, t, 0))
    mod_spec = pl.BlockSpec((None, 8, D_MODEL), lambda i, t: (i, 0, 0))
    return pl.pallas_call(
        body, name="k4a_fwd", grid=(b, l // TOK),
        in_specs=[tok(D_MODEL), tok(512), tok(512), tok(512), tok(512), _full((1, 512)), _full((D_MODEL, D_MODEL)),
                  mod_spec, _full((1, D_MODEL))],
        out_specs=[tok(D_MODEL), tok(D_MODEL)], out_shape=[_sds((b, l, D_MODEL)), _sds((b, l, D_MODEL), BF)],
        compiler_params=_cp((ARB, ARB)),
    )(x, o_f, o_b, rg, y_mla, g_ret, w_out, mod, g_ffn)


def _k4b_mlp_loss(h2, w1, w2, x_mid, mod, g_final, tgt):
    b, l, _ = h2.shape
    nt = l // TOK

    def body(h2_ref, w1_ref, w2_ref, xm_ref, mod_ref, gfin_ref, tgt_ref, dxm_ref, dmlp_ref, loss_ref, dgt_ref,
             dgfin_ref, acc):
        i, t, j = pl.program_id(0), pl.program_id(1), pl.program_id(2)
        a = _dot(h2_ref[...], w1_ref[...], 1, 0)
        part = mm(jnp.square(jnp.maximum(a, 0.0)), w2_ref[...])
        _acc(acc, part, j == 0)

        @pl.when(j == N_DEV - 1)
        def _():
            (gt_f,) = _mod_rows(mod_ref, (5,))
            loss, vjp = jax.vjp(k4c_tile, xm_ref[...], acc[...], gt_f, gfin_ref[...], tgt_ref[...])
            dxm, dmlp, dgt, dgfin, _ = vjp(jnp.ones((1, 1), F32))
            dxm_ref[...] = dxm
            dmlp_ref[...] = dmlp.astype(BF)
            first = jnp.logical_and(i == 0, t == 0)
            _acc(loss_ref, jnp.broadcast_to(loss, (8, LANE)), first)
            _acc(dgfin_ref, dgfin, first)
            _acc(dgt_ref, dgt, t == 0)

    tok = lambda w: pl.BlockSpec((None, TOK, w), lambda i, t, j: (i, t, 0))
    return pl.pallas_call(
        body, name="k4b_mlp_loss", grid=(b, nt, N_DEV),
        in_specs=[tok(D_MODEL), pl.BlockSpec((None, D_MODEL, FF_BLK), lambda i, t, j: (j, 0, 0)),
                  pl.BlockSpec((None, FF_BLK, D_MODEL), lambda i, t, j: (j, 0, 0)), tok(D_MODEL),
                  pl.BlockSpec((None, 8, D_MODEL), lambda i, t, j: (i, 0, 0)),
                  pl.BlockSpec((1, D_MODEL), lambda i, t, j: (0, 0)), tok(D_MODEL)],
        out_specs=[tok(D_MODEL), tok(D_MODEL), pl.BlockSpec((8, LANE), lambda i, t, j: (0, 0)),
                   pl.BlockSpec((None, 1, D_MODEL), lambda i, t, j: (i, 0, 0)),
                   pl.BlockSpec((1, D_MODEL), lambda i, t, j: (0, 0))],
        out_shape=[_sds((b, l, D_MODEL)), _sds((b, l, D_MODEL), BF), _sds((8, LANE)), _sds((b, 1, D_MODEL)),
                   _sds((1, D_MODEL))],
        scratch_shapes=[pltpu.VMEM((TOK, D_MODEL), F32)],
        compiler_params=_cp((ARB, ARB, ARB)),
    )(h2, w1, w2, x_mid, mod, g_final, tgt)


def _k4d_mlp_bwd(h2, dmlp, w1, w2):
    b, l, _ = h2.shape
    nt = l // TOK

    def body(h2_ref, dm_ref, w1_ref, w2_ref, dh2_ref, dw1_hbm, dw2_hbm, dw1, dw2, st1, st2, sem):
        i, t, j = pl.program_id(0), pl.program_id(1), pl.program_id(2)
        first = jnp.logical_and(i == 0, t == 0)
        h2v, dm = h2_ref[...], dm_ref[...]
        a = _dot(h2v, w1_ref[...], 1, 0)
        r = jnp.maximum(a, 0.0)
        dhid = _dot(dm, w2_ref[...], 1, 1)
        g2 = _dot(jnp.square(r), dm, 0, 0)
        da = dhid * (2.0 * r)
        g1 = _dot(h2v, da, 0, 0)
        _acc(dw1.at[j], g1, first)
        _acc(dw2.at[j], g2, first)
        _acc(dh2_ref, _dot(da, w1_ref[...], 1, 1), j == 0)

        @pl.when(jnp.logical_and(jnp.logical_and(i == b - 1, t == nt - 1), j == N_DEV - 1))
        def _():
            for k in range(N_DEV):
                st1[...] = dw1[k].astype(BF)
                st2[...] = dw2[k].astype(BF)
                c1 = pltpu.make_async_copy(st1, dw1_hbm.at[k], sem.at[0])
                c2 = pltpu.make_async_copy(st2, dw2_hbm.at[k], sem.at[1])
                c1.start()
                c2.start()
                c1.wait()
                c2.wait()

    tok = lambda w: pl.BlockSpec((None, TOK, w), lambda i, t, j: (i, t, 0))
    return pl.pallas_call(
        body, name="k4d_mlp_bwd", grid=(b, nt, N_DEV),
        in_specs=[tok(D_MODEL), tok(D_MODEL), pl.BlockSpec((None, D_MODEL, FF_BLK), lambda i, t, j: (j, 0, 0)),
                  pl.BlockSpec((None, FF_BLK, D_MODEL), lambda i, t, j: (j, 0, 0))],
        out_specs=[tok(D_MODEL), ANY, ANY],
        out_shape=[_sds((b, l, D_MODEL)), _sds((N_DEV, D_MODEL, FF_BLK), BF), _sds((N_DEV, FF_BLK, D_MODEL), BF)],
        scratch_shapes=[pltpu.VMEM((N_DEV, D_MODEL, FF_BLK), F32), pltpu.VMEM((N_DEV, FF_BLK, D_MODEL), F32),
                        pltpu.VMEM((D_MODEL, FF_BLK), BF), pltpu.VMEM((FF_BLK, D_MODEL), BF),
                        pltpu.SemaphoreType.DMA((2,))],
        compiler_params=_cp((ARB, ARB, ARB)),
    )(h2, dmlp, w1, w2)


def _k4e_bwd(x, o_f, o_b, rg, y_mla, g_ret, w_out, mod, g_ffn, dxm, dh2):
    b, l, _ = x.shape

    def body(x_ref, of_ref, ob_ref, rg_ref, ym_ref, gr_ref, wo_ref, mod_ref, gf_ref, dxm_ref, dh2_ref,
             dx_ref, do_ref, drg_ref, dym_ref, dwo_ref, dgr_ref, dgf_ref, dmod_ref):
        i, t = pl.program_id(0), pl.program_id(1)
        first = jnp.logical_and(i == 0, t == 0)
        gt_a, sh_f, sc_f = _mod_rows(mod_ref, (2, 3, 4))
        wo = wo_ref[...]

        def f(xv, ofv, rgv, ymv, grv, gta, gfv, shf, scf, p_out):
            return k4a_tile(xv, ofv, ob_ref[...], rgv, ymv, grv, gta, gfv, shf, scf, wo, p_out)

        _, vjp = jax.vjp(f, x_ref[...], of_ref[...], rg_ref[...], ym_ref[...], gr_ref[...], gt_a, gf_ref[...], sh_f,
                         sc_f, jnp.zeros((D_MODEL, D_MODEL), F32))
        dx, do, drg, dym, dgr, dgta, dgf, dshf, dscf, dwo = vjp((dxm_ref[...], dh2_ref[...]))
        dx_ref[...] = dx
        do_ref[...] = do
        drg_ref[...] = drg
        dym_ref[...] = dym
        _acc(dwo_ref, dwo, first)
        _acc(dgr_ref, dgr, first)
        _acc(dgf_ref, dgf, first)
        t0 = t == 0
        _acc(dmod_ref.at[2:3, :], dgta, t0)
        _acc(dmod_ref.at[3:4, :], dshf, t0)
        _acc(dmod_ref.at[4:5, :], dscf, t0)

        @pl.when(t0)
        def _():
            dmod_ref[0:2, :] = jnp.zeros((2, D_MODEL), F32)
            dmod_ref[5:8, :] = jnp.zeros((3, D_MODEL), F32)

    tok = lambda w: pl.BlockSpec((None, TOK_B, w), lambda i, t: (i, t, 0))
    mod_spec = pl.BlockSpec((None, 8, D_MODEL), lambda i, t: (i, 0, 0))
    return pl.pallas_call(
        body, name="k4e_bwd", grid=(b, l // TOK_B),
        in_specs=[tok(D_MODEL), tok(512), tok(512), tok(512), tok(512), _full((1, 512)), _full((D_MODEL, D_MODEL)),
                  mod_spec, _full((1, D_MODEL)), tok(D_MODEL), tok(D_MODEL)],
        out_specs=[tok(D_MODEL), tok(512), tok(512), tok(512), _full((D_MODEL, D_MODEL)), _full((1, 512)),
                   _full((1, D_MODEL)), mod_spec],
        out_shape=[_sds((b, l, D_MODEL)), _sds((b, l, 512)), _sds((b, l, 512)), _sds((b, l, 512)),
                   _sds((D_MODEL, D_MODEL)), _sds((1, 512)), _sds((1, D_MODEL)), _sds((b, 8, D_MODEL))],
        compiler_params=_cp((ARB, ARB)),
    )(x, o_f, o_b, rg, y_mla, g_ret, w_out, mod, g_ffn, dxm, dh2)


def _adamw(w, m, v, pieces, name):
    r, c = w.shape
    npc = pieces.shape[0]
    rb = r
    for cand in (256, 128, 64, 32, 16, 8):
        if r > cand and r % cand == 0 and cand * c * 4 * (npc + 7) * 2 <= 24 * 1024 * 1024:
            rb = cand
            break

    def body(w_ref, m_ref, v_ref, p_ref, g_ref, d_ref, nm_ref, nv_ref):
        g = p_ref[0].astype(F32)
        for k in range(1, npc):
            g = g + p_ref[k].astype(F32)
        wv = w_ref[...]
        mn = ADAM_B1 * m_ref[...] + (1.0 - ADAM_B1) * g
        vn = ADAM_B2 * v_ref[...] + (1.0 - ADAM_B2) * jnp.square(g)
        m_hat = mn / (1.0 - ADAM_B1 ** ADAM_STEP)
        v_hat = vn / (1.0 - ADAM_B2 ** ADAM_STEP)
        g_ref[...] = g
        d_ref[...] = -ADAM_LR * (m_hat / (jnp.sqrt(v_hat) + ADAM_EPS) + ADAM_WD * wv)
        nm_ref[...] = mn
        nv_ref[...] = vn

    blk = pl.BlockSpec((rb, c), lambda i: (i, 0))
    return pl.pallas_call(
        body, name=name, grid=(r // rb,), in_specs=[blk, blk, blk, pl.BlockSpec((npc, rb, c), lambda i: (0, i, 0))],
        out_specs=[blk] * 4, out_shape=[_sds((r, c))] * 4, compiler_params=_cp((ARB,)),
    )(w, m, v, pieces)


def _pad_heads(w, d):
    k = w.shape[0]
    return jnp.pad(w.reshape(k, N_HEADS, d), ((0, 0), (0, 0), (0, LANE - d))).reshape(k, N_HEADS * LANE)


def _cut_heads(w, d):
    k = w.shape[0]
    return w.reshape(k, N_HEADS, LANE)[:, :, :d].reshape(k, N_HEADS * d)


def _w_in_pad(w):
    return jnp.concatenate([_pad_heads(w[:, 0:256], 64), _pad_heads(w[:, 256:512], 64), w[:, 512:2176],
                            jnp.pad(w[:, 2176:2240], ((0, 0), (0, 64)))], axis=1)


def _w_in_cut(g):
    return jnp.concatenate([_cut_heads(g[:, 0:512], 64), _cut_heads(g[:, 512:1024], 64), g[:, 1024:2688],
                            g[:, 2688:2752]], axis=1)


def _w_uq_pad(w):
    w = w.reshape(384, N_HEADS, 192)
    return jnp.concatenate([w[:, :, :128].reshape(384, 512),
                            jnp.pad(w[:, :, 128:], ((0, 0), (0, 0), (0, 64))).reshape(384, 512)], axis=1)


def _w_uq_cut(g):
    return jnp.concatenate([g[:, :512].reshape(384, N_HEADS, 128), g[:, 512:].reshape(384, N_HEADS, 128)[:, :, :64]],
                           axis=2).reshape(384, 768)


def _w_ukv_perm(w):
    w = w.reshape(256, N_HEADS, 256)
    return jnp.concatenate([w[:, :, :128].reshape(256, 512), w[:, :, 128:].reshape(256, 512)], axis=1)


def _w_ukv_unperm(g):
    return jnp.concatenate([g[:, :512].reshape(256, N_HEADS, 128), g[:, 512:].reshape(256, N_HEADS, 128)],
                           axis=2).reshape(256, 1024)


def _unshard_cols(g):
    return jnp.transpose(g, (1, 0, 2)).reshape(g.shape[1], N_DEV * g.shape[2])


def _shard_cols(w):
    k, n = w.shape
    return jnp.transpose(w.reshape(k, N_DEV, n // N_DEV), (1, 0, 2))


def _rope_tables():
    rows = SEQ // GRID_W
    row = jnp.repeat(jnp.arange(rows, dtype=F32), GRID_W)
    col = jnp.tile(jnp.arange(GRID_W, dtype=F32), rows)
    freq = ROPE_BASE ** (-jnp.arange(16, dtype=F32) / 16)
    ang = jnp.concatenate([row[:, None] * freq, col[:, None] * freq], axis=-1)
    cos, sin = jnp.cos(ang), jnp.sin(ang)
    z = jnp.zeros((SEQ, 64), F32)
    cs = jnp.concatenate([cos, cos, z], axis=1)
    sn = jnp.concatenate([-sin, sin, z], axis=1)
    return jnp.tile(cs, (1, N_HEADS)), jnp.tile(sn, (1, N_HEADS))


_SMALL = (("c_ctx", 1024), ("b_ada", 6144), ("g_attn", 1024), ("g_ffn", 1024), ("ret_decay_fwd", 128),
          ("ret_decay_bwd", 128), ("g_ret", 512), ("g_q_lora", 384), ("g_kv_lora", 256), ("g_final", 1024))
_SMALL_REAL = {"c_ctx": 1024, "b_ada": 6144, "g_attn": 1024, "g_ffn": 1024, "ret_decay_fwd": 4, "ret_decay_bwd": 4,
               "g_ret": 512, "g_q_lora": 384, "g_kv_lora": 256, "g_final": 1024}
_SMALL_ROWS = sum(n for _, n in _SMALL) // LANE


def _pack_small(vals):
    parts = []
    for name, n in _SMALL:
        a = vals[name].reshape(-1).astype(F32)
        parts.append(jnp.pad(a, (0, n - a.shape[0])))
    return jnp.concatenate(parts).reshape(_SMALL_ROWS, LANE)


def _unpack_small(packed, shapes):
    flat = packed.reshape(-1)
    out, off = {}, 0
    for name, n in _SMALL:
        out[name] = flat[off:off + _SMALL_REAL[name]].reshape(shapes[name])
        off += n
    return out


def kernel(x, c, ctx, c_ctx, w_ada, b_ada, g_attn, g_ffn, w_in, ret_decay_fwd, ret_decay_bwd, g_ret, g_q_lora, w_uq, g_kv_lora, w_ukv, w_out, w_ff1, w_ff2, g_final, loss_target, m_c_ctx, m_w_ada, m_b_ada, m_g_attn, m_g_ffn, m_w_in, m_ret_decay_fwd, m_ret_decay_bwd, m_g_ret, m_g_q_lora, m_w_uq, m_g_kv_lora, m_w_ukv, m_w_out, m_w_ff1, m_w_ff2, m_g_final, v_c_ctx, v_w_ada, v_b_ada, v_g_attn, v_g_ffn, v_w_in, v_ret_decay_fwd, v_ret_decay_bwd, v_g_ret, v_g_q_lora, v_w_uq, v_g_kv_lora, v_w_ukv, v_w_out, v_w_ff1, v_w_ff2, v_g_final):
    me = 4 * lax.axis_index("x") + 2 * lax.axis_index("y") + lax.axis_index("c")
    nb = x.shape[0]

    c_pad = jnp.pad(c, ((0, 8 - nb), (0, 0)))
    c_all, g_in, g_uq, g_ukv = _exchange([c_pad, w_in[0].astype(BF), w_uq[0].astype(BF), w_ukv[0].astype(BF)], True,
                                         "gather_weights")
    behind = c_all[0, 0, 0:1] * 0.0
    st_g = _exchange_start([(w_out[0] + behind).astype(BF), w_ff1[0].astype(BF), w_ff2[0].astype(BF)], True,
                           "gather_ff_start")
    tok_g = st_g["token"][0:1, 0:1]
    wi = _w_in_pad(_unshard_cols(g_in))
    wq = _w_uq_pad(_unshard_cols(g_uq))
    wk = _w_ukv_perm(_unshard_cols(g_ukv))

    crows = jnp.concatenate([c_all[:, :nb].reshape(N_DEV * nb, D_MODEL), c_ctx[None], jnp.zeros((7, D_MODEL), F32)])
    crows = crows + tok_g
    b_blk = lax.dynamic_slice(b_ada, (0, me * 768), (1, 768))
    (mod_g,) = _exchange([_mod_fwd(crows, w_ada[0], b_blk)], True, "gather_mod")
    mod_all = _unshard_cols(mod_g)
    mod_mine = lax.dynamic_slice(mod_all, (me * nb, 0), (nb, 6 * D_MODEL)).reshape(nb, 6, D_MODEL)
    mod = jnp.pad(mod_mine, ((0, 0), (0, 2), (0, 0)))
    mod_c = jnp.pad(mod_all[16].reshape(1, 6, D_MODEL), ((0, 0), (0, 2), (0, 0)))

    cs, sn = _rope_tables()
    dec_f = ret_decay_fwd.reshape(N_HEADS, 1, 1)
    dec_b = ret_decay_bwd.reshape(N_HEADS, 1, 1)

    rkc, rvc, knc, kpc, vc = _k1_fwd(ctx, mod_c, g_attn, g_q_lora, g_kv_lora, wi, wq, wk, cs, sn, True)
    rq, rk, rv, rg, qn, qp, kn, kp, vv = _k1_fwd(x, mod, g_attn, g_q_lora, g_kv_lora, wi, wq, wk, cs, sn, False)
    o_f, o_b, sf_prev, sb_prev = _k2_fwd(rq, rk, rv, rkc, rvc, dec_f, dec_b)
    kn_all = jnp.concatenate([knc, kn], axis=1).astype(BF)
    kp_all = jnp.concatenate([kpc, kp], axis=1).astype(BF)
    v_all = jnp.concatenate([vc, vv], axis=1).astype(BF)
    y_mla = _k3_fwd(qn, qp, kn_all, kp_all, v_all)
    g_out, g_ff1, g_ff2 = _exchange_wait(st_g, y_mla, "gather_ff_wait")
    wo = g_out.reshape(D_MODEL, D_MODEL)
    x_mid, h2 = _k4a_fwd(x, o_f, o_b, rg, y_mla, g_ret, wo, mod, g_ffn)
    dxm, dmlp, loss_acc, dgt_f, dg_final = _k4b_mlp_loss(h2, g_ff1, g_ff2, x_mid, mod, g_final.reshape(1, D_MODEL),
                                                         loss_target)

    dh2, dw1, dw2 = _k4d_mlp_bwd(h2, dmlp, g_ff1, g_ff2)
    st_s = _exchange_start([dw1, dw2], False, "scatter_ff_start")
    g_ret_t = g_ret + st_s["token"][0:1, 0:1]
    dx_res, do, drg, dym, dwo, dg_ret, dg_ffn, dmod_a = _k4e_bwd(x, o_f, o_b, rg, y_mla, g_ret_t, wo, mod, g_ffn, dxm, dh2)
    dqn, dqp, dkn_all, dkp_all, dv_all = _k3_bwd(qn, qp, kn_all, kp_all, v_all, dym)
    dqf, dkf, dvf, dqb, dkb, dvb, dkc, dvc, ddf, ddb = _k2_bwd(rq, rk, rv, do, sf_prev, sb_prev, rkc, rvc, dec_f, dec_b)
    cts = [[(dqf, 0), (dqb, 0)], [(dkf, 0), (dkb, 0)], [(dvf, 0), (dvb, 0)], [(drg, 0)], [(dqn, 0)], [(dqp, 0)],
           [(dkn_all, CTX_LEN)], [(dkp_all, CTX_LEN)], [(dv_all, CTX_LEN)]]
    grad_x, accs, dmod_1 = _k1_bwd(x, mod, g_attn, g_q_lora, g_kv_lora, wi, wq, wk, cs, sn, cts, dx_res, None, False)
    cts_c = [[(dkc, 0)], [(dvc, 0)], [(dkn_all, 0)], [(dkp_all, 0)], [(dv_all, 0)]]
    _, accs, dmod_c1 = _k1_bwd(ctx, mod_c, g_attn, g_q_lora, g_kv_lora, wi, wq, wk, cs, sn, cts_c, None, accs, True)
    dwi, dwq, dwk, dg_attn, dg_q, dg_kv = accs

    dmod_loc = (dmod_a + dmod_1).at[:, 5, :].set(dgt_f[:, 0, :])[:, :6, :].reshape(nb, 6 * D_MODEL)
    dmod_ctx = dmod_c1[:, :6, :].reshape(1, 6 * D_MODEL)
    small = {"c_ctx": jnp.zeros((D_MODEL,), F32), "b_ada": jnp.zeros((6 * D_MODEL,), F32), "g_attn": dg_attn,
             "g_ffn": dg_ffn, "ret_decay_fwd": jnp.sum(ddf[:, :, 0, 0], axis=0), "ret_decay_bwd": jnp.sum(ddb[:, :, 0, 0], axis=0),
             "g_ret": dg_ret, "g_q_lora": dg_q, "g_kv_lora": dg_kv, "g_final": dg_final}
    extra = jnp.concatenate([dmod_loc, dmod_ctx, jnp.zeros((5, 6 * D_MODEL), F32)]).reshape(8 * 48, LANE)
    loss_rows = loss_acc
    sm_g, ex_g, loss_g = _exchange([_pack_small(small), extra, loss_rows], True, "gather_small")
    ex_g = ex_g.reshape(N_DEV, 8, 6 * D_MODEL)
    dmod_all = ex_g[:, :nb].reshape(N_DEV * nb, 6 * D_MODEL)
    dmodc_parts = ex_g[:, nb]
    dmod_full = jnp.concatenate([dmod_all, jnp.zeros((8, 6 * D_MODEL), F32)])
    dmod_blk = lax.dynamic_slice(dmod_full, (0, me * 768), (24, 768))
    dmodc_blk = lax.dynamic_slice(dmodc_parts, (0, me * 768), (N_DEV, 768))
    gw_ada, gcc_part, gb_ada = _mod_bwd(crows, w_ada[0], dmod_blk, dmodc_blk, dmod_full, dmodc_parts)
    (gcc_g,) = _exchange([gcc_part], True, "gather_c_ctx")

    p_ff1, p_ff2 = _exchange_wait(st_s, dmod_c1, "scatter_ff_wait")
    pieces = _exchange([_shard_cols(_w_in_cut(dwi)).astype(BF), _shard_cols(_w_uq_cut(dwq)).astype(BF),
                        _shard_cols(_w_ukv_unperm(dwk)).astype(BF), dwo.reshape(N_DEV, 128, D_MODEL).astype(BF)],
                       False, "scatter_grads") + [p_ff1, p_ff2]

    res = {}
    big = (("w_in", w_in, m_w_in, v_w_in, pieces[0]), ("w_uq", w_uq, m_w_uq, v_w_uq, pieces[1]),
           ("w_ukv", w_ukv, m_w_ukv, v_w_ukv, pieces[2]), ("w_out", w_out, m_w_out, v_w_out, pieces[3]),
           ("w_ff1", w_ff1, m_w_ff1, v_w_ff1, pieces[4]), ("w_ff2", w_ff2, m_w_ff2, v_w_ff2, pieces[5]),
           ("w_ada", w_ada, m_w_ada, v_w_ada, gw_ada[None]))
    for name, w, m, v, pcs in big:
        res[name] = [a[None] for a in _adamw(w[0], m[0], v[0], pcs, "adamw_" + name)]

    smalls = {"c_ctx": (c_ctx, m_c_ctx, v_c_ctx), "b_ada": (b_ada, m_b_ada, v_b_ada), "g_attn": (g_attn, m_g_attn, v_g_attn),
              "g_ffn": (g_ffn, m_g_ffn, v_g_ffn), "ret_decay_fwd": (ret_decay_fwd, m_ret_decay_fwd, v_ret_decay_fwd),
              "ret_decay_bwd": (ret_decay_bwd, m_ret_decay_bwd, v_ret_decay_bwd), "g_ret": (g_ret, m_g_ret, v_g_ret),
              "g_q_lora": (g_q_lora, m_g_q_lora, v_g_q_lora), "g_kv_lora": (g_kv_lora, m_g_kv_lora, v_g_kv_lora),
              "g_final": (g_final, m_g_final, v_g_final)}
    sm_pieces = sm_g.reshape(N_DEV, _SMALL_ROWS * LANE)
    sm_pieces = sm_pieces.at[:, 0:1024].set(gcc_g[:, 0, :])
    sm_pieces = sm_pieces.at[0, 1024:1024 + 6144].set(gb_ada[0])
    sm_pieces = sm_pieces.reshape(N_DEV, _SMALL_ROWS, LANE)
    packed = [_pack_small({k: t[i] for k, t in smalls.items()}) for i in range(3)]
    sm_out = _adamw(packed[0], packed[1], packed[2], sm_pieces, "adamw_small")
    shapes = {k: t[0].shape for k, t in smalls.items()}
    sm_res = [_unpack_small(o, shapes) for o in sm_out]
    for name in smalls:
        res[name] = [r[name] for r in sm_res]

    loss = loss_g[0, 0, 0]
    for k in range(1, N_DEV):
        loss = loss + loss_g[k, 0, 0]

    order = ("c_ctx", "w_ada", "b_ada", "g_attn", "g_ffn", "w_in", "ret_decay_fwd", "ret_decay_bwd", "g_ret", "g_q_lora",
             "w_uq", "g_kv_lora", "w_ukv", "w_out", "w_ff1", "w_ff2", "g_final")
    return (loss, grad_x, *[res[n][0] for n in order], *[res[n][1] for n in order], *[res[n][2] for n in order],
            *[res[n][3] for n in order])
```

```python
import functools
import math

import jax
import jax.numpy as jnp
from jax import lax
from jax.experimental import pallas as pl
from jax.experimental.pallas import tpu as pltpu

F32 = jnp.float32
BF = jnp.bfloat16
EPS = 1e-6
LANE = 128
N_DEV = 8
D_MODEL = 1024
SEQ = 2048
CTX_LEN = 256
GRID_W = 64
N_HEADS = 4
RET_CHUNK = 128
N_CHUNK = SEQ // RET_CHUNK
D_FF = 4096
FF_BLK = D_FF // N_DEV
IN_PAD = 2816
KV_LEN = CTX_LEN + SEQ
ROPE_BASE = 10000.0
ADAM_LR, ADAM_B1, ADAM_B2, ADAM_EPS, ADAM_WD, ADAM_STEP = 0.001, 0.9, 0.999, 1e-08, 0.01, 10
TOK = 256
TOK_B = 128
VMEM_LIMIT = 56 * 1024 * 1024
ARB = "arbitrary"
MESH = pl.DeviceIdType.MESH


def _dot(a, b, ca, cb):
    return lax.dot_general(a.astype(BF), b.astype(BF), (((ca,), (cb,)), ((), ())), preferred_element_type=F32)


@jax.custom_vjp
def mm(a, b):
    return _dot(a, b, 1, 0)


@jax.custom_vjp
def mm_nt(a, b):
    return _dot(a, b, 1, 1)


@jax.custom_vjp
def mm_tn(a, b):
    return _dot(a, b, 0, 0)


mm.defvjp(lambda a, b: (_dot(a, b, 1, 0), (a, b)), lambda r, g: (mm_nt(g, r[1]), mm_tn(r[0], g)))
mm_nt.defvjp(lambda a, b: (_dot(a, b, 1, 1), (a, b)), lambda r, g: (mm(g, r[1]), mm_tn(g, r[0])))
mm_tn.defvjp(lambda a, b: (_dot(a, b, 0, 0), (a, b)), lambda r, g: (mm_nt(r[1], g), mm(r[0], g)))


@jax.custom_vjp
def _mmw(a, w, probe):
    return _dot(a, w, 1, 0)


def _mmw_bwd(r, g):
    a, w = r
    return mm_nt(g, w), jnp.zeros_like(w), mm_tn(a, g)


_mmw.defvjp(lambda a, w, probe: (_dot(a, w, 1, 0), (a, w)), _mmw_bwd)


def mmw(a, w, probe):
    return _dot(a, w, 1, 0) if probe is None else _mmw(a, w, probe)


def rmsn(x, g):
    return x * lax.rsqrt(jnp.mean(x * x, axis=-1, keepdims=True) + EPS) * g


def silu(x):
    return x * jax.nn.sigmoid(x)


def _swap32_impl(x):
    n = x.shape[-1]
    lane = lax.broadcasted_iota(jnp.int32, x.shape, x.ndim - 1) % LANE
    up = pltpu.roll(x, n - 32, x.ndim - 1)
    dn = pltpu.roll(x, 32, x.ndim - 1)
    return jnp.where(lane < 32, up, jnp.where(lane < 64, dn, 0.0))


@jax.custom_vjp
def swap32(x):
    return _swap32_impl(x)


swap32.defvjp(lambda x: (_swap32_impl(x), None), lambda _, g: (_swap32_impl(g),))


def rope(x, cs, sn):
    return x * cs + swap32(x) * sn


def k1_tile(x, sh, sc, g_attn, g_q, g_kv, w_in, w_uq, w_ukv, p_in, p_uq, p_ukv, cs, sn, is_ctx):
    h = rmsn(x, g_attn) * (1.0 + sc) + sh
    p = mmw(h, w_in, p_in)
    rk = p[:, 512:1024] * 0.125
    rv = p[:, 1024:1536]
    ckv = p[:, 2432:2688]
    kpe = p[:, 2688:2816]
    kv = mmw(rmsn(ckv, g_kv), w_ukv, p_ukv)
    kn, v = kv[:, :512], kv[:, 512:]
    if is_ctx:
        return rk, rv, kn, kpe, v
    rq = p[:, 0:512]
    rg = p[:, 1536:2048]
    cq = p[:, 2048:2432]
    q = mmw(rmsn(cq, g_q), w_uq, p_uq)
    qn, qp = q[:, :512], q[:, 512:]
    return (rope(rq, cs, sn), rope(rk, cs, sn), rv, rg, qn, rope(qp, cs, sn), kn,
            rope(kpe, cs[:, :LANE], sn[:, :LANE]), v)


def log_sigmoid(x):
    return jnp.minimum(x, 0.0) - jnp.log(1.0 + jnp.exp(-jnp.abs(x)))


def ret_chunk(q, k, v, s, lg, reverse):
    c = RET_CHUNK
    ii = lax.broadcasted_iota(jnp.int32, (c, c), 0).astype(F32)
    jj = lax.broadcasted_iota(jnp.int32, (c, c), 1).astype(F32)
    diff = (jj - ii) if reverse else (ii - jj)
    dec = jnp.where(diff >= 0, jnp.exp(lg * jnp.maximum(diff, 0.0)), 0.0)
    pos = lax.broadcasted_iota(jnp.int32, (c, 1), 0).astype(F32)
    if reverse:
        wk, wq = jnp.exp(lg * pos), jnp.exp(lg * (c - pos))
    else:
        wk, wq = jnp.exp(lg * (c - 1.0 - pos)), jnp.exp(lg * (pos + 1.0))
    o = mm(mm_nt(q, k) * dec, v) + mm(q * wq, s)
    s_next = jnp.exp(lg * float(c)) * s + mm_tn(k * wk, v)
    return o, s_next


def ctx_state(kc, vc, lg, reverse):
    n = kc.shape[0]
    pos = lax.broadcasted_iota(jnp.int32, (n, 1), 0).astype(F32)
    w = jnp.exp(lg * pos) if reverse else jnp.exp(lg * (n - 1.0 - pos))
    return mm_tn(kc * w, vc)


def attn_head(qn, qp, kn, kp, v):
    s = (mm_nt(qn, kn) + mm_nt(qp, kp)) * (1.0 / math.sqrt(192.0))
    e = jnp.exp(s - jnp.max(s, axis=-1, keepdims=True))
    return mm(e / jnp.sum(e, axis=-1, keepdims=True), v)


def gn_gate(o, rg, g_ret):
    ys = []
    for h in range(N_HEADS):
        sl = slice(LANE * h, LANE * (h + 1))
        oh = o[:, sl]
        mu = jnp.mean(oh, axis=-1, keepdims=True)
        var = jnp.mean(jnp.square(oh - mu), axis=-1, keepdims=True)
        ys.append((oh - mu) * lax.rsqrt(var + EPS) * g_ret[:, sl])
    return jnp.concatenate(ys, axis=-1) * silu(rg)


def k4a_tile(x, o_f, o_b, rg, y_mla, g_ret, gt_a, g_ffn, sh_f, sc_f, w_out, p_out):
    mix = jnp.concatenate([gn_gate(o_f + o_b, rg, g_ret), y_mla], axis=-1)
    x_mid = x + gt_a * mmw(mix, w_out, p_out)
    h2 = rmsn(x_mid, g_ffn) * (1.0 + sc_f) + sh_f
    return x_mid, h2


def k4c_tile(x_mid, mlp, gt_f, g_final, tgt):
    y = rmsn(x_mid + gt_f * mlp, g_final)
    per_tok = jnp.mean(jnp.square(y - tgt), axis=-1, keepdims=True)
    return 0.5 * jnp.sum(per_tok, axis=0, keepdims=True)


def _cp(sem=None, vmem=VMEM_LIMIT):
    return pltpu.CompilerParams(dimension_semantics=sem, vmem_limit_bytes=vmem)


def _acc(ref, val, first):
    @pl.when(first)
    def _():
        ref[...] = val

    @pl.when(jnp.logical_not(first))
    def _():
        ref[...] += val


def _full(shape):
    nd = len(shape)
    return pl.BlockSpec(shape, lambda *_: (0,) * nd)


ANY = pl.BlockSpec(memory_space=pl.ANY)


def _sds(shape, dtype=F32):
    return jax.ShapeDtypeStruct(shape, dtype)


def _exchange(arrs, gather, name):
    n = len(arrs)
    out_shape = [_sds(((N_DEV,) + a.shape) if gather else a.shape, a.dtype) for a in arrs]

    def body(*refs):
        ins, outs = refs[:n], refs[n:2 * n]
        send_sems, recv_sems, local_sems = refs[2 * n:]
        x, y, c = lax.axis_index("x"), lax.axis_index("y"), lax.axis_index("c")
        me = 4 * x + 2 * y + c
        sends, recvs, locs = [], [], []
        for i in range(n):
            for k in range(N_DEV - 1):
                bits = k + 1
                px = x ^ ((bits >> 2) & 1)
                py = y ^ ((bits >> 1) & 1)
                pc = c ^ (bits & 1)
                peer = 4 * px + 2 * py + pc
                src = ins[i] if gather else ins[i].at[peer]
                sem = i * (N_DEV - 1) + k
                sends.append(pltpu.make_async_remote_copy(
                    src_ref=src, dst_ref=outs[i].at[me], send_sem=send_sems.at[sem], recv_sem=recv_sems.at[sem],
                    device_id=(px, py, pc), device_id_type=MESH))
                recvs.append(pltpu.make_async_remote_copy(
                    src_ref=src, dst_ref=outs[i].at[peer], send_sem=send_sems.at[sem], recv_sem=recv_sems.at[sem],
                    device_id=(px, py, pc), device_id_type=MESH))
            locs.append(pltpu.make_async_copy(ins[i] if gather else ins[i].at[me], outs[i].at[me], local_sems.at[i]))
        for cp in locs + sends:
            cp.start()
        for cp in recvs:
            cp.wait_recv()
        for cp in sends:
            cp.wait_send()
        for cp in locs:
            cp.wait()

    outs = pl.pallas_call(
        body, name=name, out_shape=out_shape, in_specs=[ANY] * n, out_specs=[ANY] * n,
        scratch_shapes=[pltpu.SemaphoreType.DMA((n * (N_DEV - 1),)), pltpu.SemaphoreType.DMA((n * (N_DEV - 1),)),
                        pltpu.SemaphoreType.DMA((n,))],
    )(*arrs)
    return list(outs)


HBM = pl.BlockSpec(memory_space=pltpu.HBM)
SEM = pl.BlockSpec(memory_space=pltpu.SEMAPHORE)
EFFECT = pltpu.SideEffectType.DATAFLOW_SIDE_EFFECTING


def _peer(k):
    x, y, c = lax.axis_index("x"), lax.axis_index("y"), lax.axis_index("c")
    bits = k + 1
    px, py, pc = x ^ ((bits >> 2) & 1), y ^ ((bits >> 1) & 1), c ^ (bits & 1)
    return (px, py, pc), 4 * px + 2 * py + pc, 4 * x + 2 * y + c


def _exchange_start(arrs, gather, name):
    n = len(arrs)
    lands = [pltpu.with_memory_space_constraint(lax.empty(((N_DEV,) + a.shape) if gather else a.shape, a.dtype),
                                                pltpu.HBM) for a in arrs]
    srcs = [pltpu.with_memory_space_constraint(a, pltpu.HBM) for a in arrs]

    def body(*refs):
        ins, zones = refs[:n], refs[n:2 * n]
        send_sems, recv_sems, local_sems = refs[2 * n:2 * n + 3]
        token = refs[-1]
        for i in range(n):
            for k in range(N_DEV - 1):
                dev, peer, me = _peer(k)
                sem = i * (N_DEV - 1) + k
                pltpu.make_async_remote_copy(
                    src_ref=ins[i] if gather else ins[i].at[peer], dst_ref=zones[i].at[me],
                    send_sem=send_sems.at[sem], recv_sem=recv_sems.at[sem], device_id=dev, device_id_type=MESH).start()
            _, _, me = _peer(0)
            pltpu.make_async_copy(ins[i] if gather else ins[i].at[me], zones[i].at[me], local_sems.at[i]).start()
        token[...] = jnp.zeros_like(token)

    nsem = n * (N_DEV - 1)
    outs = pl.pallas_call(
        body, name=name,
        out_shape=[pltpu.SemaphoreType.DMA((nsem,)), pltpu.SemaphoreType.DMA((nsem,)), pltpu.SemaphoreType.DMA((n,))]
        + [pltpu.HBM(a.shape, a.dtype) for a in srcs] + [pltpu.HBM(z.shape, z.dtype) for z in lands]
        + [_sds((8, LANE))],
        in_specs=[HBM] * (2 * n),
        out_specs=[SEM, SEM, SEM] + [HBM] * (2 * n) + [pl.BlockSpec(memory_space=pltpu.VMEM)],
        input_output_aliases={i: 3 + i for i in range(2 * n)},
        compiler_params=pltpu.CompilerParams(has_side_effects=EFFECT),
    )(*srcs, *lands)
    return {"n": n, "gather": gather, "sems": outs[:3], "srcs": outs[3:3 + n], "lands": outs[3 + n:3 + 2 * n],
            "token": outs[-1]}


def _exchange_wait(st, after, name):
    n, gather = st["n"], st["gather"]

    def body(*refs):
        ins, zones = refs[:n], refs[n:2 * n]
        send_sems, recv_sems, local_sems = refs[2 * n:2 * n + 3]
        for i in range(n):
            for k in range(N_DEV - 1):
                dev, peer, me = _peer(k)
                sem = i * (N_DEV - 1) + k
                src = ins[i] if gather else ins[i].at[peer]
                cp = pltpu.make_async_remote_copy(
                    src_ref=src, dst_ref=zones[i].at[peer], send_sem=send_sems.at[sem], recv_sem=recv_sems.at[sem],
                    device_id=dev, device_id_type=MESH)
                cp.wait_send()
                cp.wait_recv()
            _, _, me = _peer(0)
            pltpu.make_async_copy(ins[i] if gather else ins[i].at[me], zones[i].at[me], local_sems.at[i]).wait()

    outs = pl.pallas_call(
        body, name=name,
        out_shape=[pltpu.HBM(a.shape, a.dtype) for a in st["srcs"]] + [pltpu.HBM(z.shape, z.dtype) for z in st["lands"]],
        in_specs=[HBM] * (2 * n) + [SEM, SEM, SEM, ANY], out_specs=[HBM] * (2 * n),
        input_output_aliases={i: i for i in range(2 * n)},
        compiler_params=pltpu.CompilerParams(has_side_effects=EFFECT),
    )(*st["srcs"], *st["lands"], *st["sems"], after)
    return list(outs[n:])


def _mod_fwd(crows, w_ada, b_blk):
    def body(c_ref, w_ref, b_ref, o_ref):
        o_ref[...] = mm(silu(c_ref[...]), w_ref[...]) + b_ref[...]

    return pl.pallas_call(body, name="mod_fwd", out_shape=_sds((24, 768)), compiler_params=_cp())(crows, w_ada, b_blk)


def _mod_bwd(crows, w_ada, dmod_blk, dmodc_blk, dmod_full, dmodc_full):
    def body(c_ref, w_ref, d_ref, dc_ref, df_ref, dcf_ref, gw_ref, gc_ref, gb_ref):
        cr = c_ref[...]
        dc, dcf = dc_ref[0:1, :], dcf_ref[0:1, :]
        for p in range(1, N_DEV):
            dc = dc + dc_ref[p:p + 1, :]
            dcf = dcf + dcf_ref[p:p + 1, :]
        row = lax.broadcasted_iota(jnp.int32, (24, 1), 0)
        gw_ref[...] = mm_tn(silu(cr), jnp.where(row == 16, dc, d_ref[...]))
        cc = cr[16:17, :]
        sg = jax.nn.sigmoid(cc)
        part = mm_nt(jnp.broadcast_to(dc, (8, 768)), w_ref[...])
        gc_ref[...] = part * (sg * (1.0 + cc * (1.0 - sg)))
        gb_ref[...] = jnp.sum(df_ref[...], axis=0, keepdims=True) + dcf

    return pl.pallas_call(
        body, name="mod_bwd", out_shape=[_sds((D_MODEL, 768)), _sds((8, D_MODEL)), _sds((1, 6 * D_MODEL))],
        compiler_params=_cp())(crows, w_ada, dmod_blk, dmodc_blk, dmod_full, dmodc_full)


def _k1_fwd(x, mod, g_attn, g_q, g_kv, w_in, w_uq, w_ukv, cs, sn, is_ctx):
    b, l, _ = x.shape
    nt = l // TOK
    widths = (512, 512, 512, 128, 512) if is_ctx else (512, 512, 512, 512, 512, 512, 512, 128, 512)

    def body(x_ref, mod_ref, ga_ref, gq_ref, gk_ref, wi_ref, wq_ref, wk_ref, cs_ref, sn_ref, *outs):
        res = k1_tile(x_ref[...], mod_ref[0:1, :], mod_ref[1:2, :], ga_ref[...], gq_ref[...], gk_ref[...],
                      wi_ref[...], wq_ref[...], wk_ref[...], None, None, None, cs_ref[...], sn_ref[...], is_ctx)
        for o_ref, r in zip(outs, res):
            o_ref[...] = r

    tok = lambda w: pl.BlockSpec((None, TOK, w), lambda i, t: (i, t, 0))
    mod_spec = pl.BlockSpec((None, 8, D_MODEL), (lambda i, t: (0, 0, 0)) if is_ctx else (lambda i, t: (i, 0, 0)))
    return pl.pallas_call(
        body, name="k1_fwd_ctx" if is_ctx else "k1_fwd", grid=(b, nt),
        in_specs=[tok(D_MODEL), mod_spec, _full((1, D_MODEL)), _full((1, 384)), _full((1, 256)),
                  _full((D_MODEL, IN_PAD)), _full((384, 1024)), _full((256, 1024)),
                  pl.BlockSpec((TOK, 512), lambda i, t: (t, 0)), pl.BlockSpec((TOK, 512), lambda i, t: (t, 0))],
        out_specs=[tok(w) for w in widths], out_shape=[_sds((b, l, w)) for w in widths],
        compiler_params=_cp((ARB, ARB)),
    )(x, mod, g_attn, g_q, g_kv, w_in, w_uq, w_ukv, cs, sn)


def _k1_bwd(x, mod, g_attn, g_q, g_kv, w_in, w_uq, w_ukv, cs, sn, cts, dx_res, init, is_ctx):
    b, l, _ = x.shape
    tk = TOK_B
    nt = l // tk
    flat_cts = [a for group in cts for a in group]
    group_sizes = [len(g) for g in cts]
    n_ct = len(flat_cts)
    has_res = dx_res is not None
    has_init = init is not None
    acc_shapes = [(D_MODEL, IN_PAD), (384, 1024), (256, 1024), (1, D_MODEL), (1, 384), (1, 256)]

    def body(*refs):
        it = iter(refs)
        x_ref, mod_ref, ga_ref, gq_ref, gk_ref, wi_hbm, wq_hbm, wk_hbm, cs_ref, sn_ref = [next(it) for _ in range(10)]
        ct_refs = [next(it) for _ in range(n_ct)]
        res_ref = next(it) if has_res else None
        init_refs = [next(it) for _ in range(6)] if has_init else None
        gx_ref = next(it) if not is_ctx else None
        out_hbm = [next(it) for _ in range(6)]
        dmod_ref = next(it)
        wi_v, wq_v, wk_v = next(it), next(it), next(it)
        accs = [next(it) for _ in range(6)]
        sem = next(it)
        i, t = pl.program_id(0), pl.program_id(1)
        first = jnp.logical_and(i == 0, t == 0)
        last = jnp.logical_and(i == b - 1, t == nt - 1)

        @pl.when(first)
        def _():
            for src, dst in ((wi_hbm, wi_v), (wq_hbm, wq_v), (wk_hbm, wk_v)):
                pltpu.sync_copy(src, dst)
            for k in range(6):
                if has_init:
                    pltpu.sync_copy(init_refs[k], accs[k])
                else:
                    accs[k][...] = jnp.zeros(acc_shapes[k], F32)

        ct_vals, pos = [], 0
        for gsz in group_sizes:
            v = ct_refs[pos][...]
            for r in ct_refs[pos + 1:pos + gsz]:
                v = v + r[...]
            ct_vals.append(v)
            pos += gsz
        wi, wq, wk = wi_v[...], wq_v[...], wk_v[...]
        csv, snv = cs_ref[...], sn_ref[...]

        def f(xv, sh, sc, ga, gq, gk, p_in, p_uq, p_ukv):
            return k1_tile(xv, sh, sc, ga, gq, gk, wi, wq, wk, p_in, p_uq, p_ukv, csv, snv, is_ctx)

        probes = [jnp.zeros(s, F32) for s in acc_shapes[:3]]
        _, vjp = jax.vjp(f, x_ref[...], mod_ref[0:1, :], mod_ref[1:2, :], ga_ref[...], gq_ref[...], gk_ref[...], *probes)
        dx, dsh, dsc, dga, dgq, dgk, dwi, dwq, dwk = vjp(tuple(ct_vals))
        if not is_ctx:
            gx_ref[...] = dx + res_ref[...] if has_res else dx
        for ref, val in zip(accs, (dwi, dwq, dwk, dga, dgq, dgk)):
            ref[...] += val
        t0 = first if is_ctx else t == 0
        _acc(dmod_ref.at[0:1, :], dsh, t0)
        _acc(dmod_ref.at[1:2, :], dsc, t0)

        @pl.when(t0)
        def _():
            dmod_ref[2:8, :] = jnp.zeros((6, D_MODEL), F32)

        @pl.when(last)
        def _():
            cps = [pltpu.make_async_copy(accs[k], out_hbm[k], sem.at[k]) for k in range(6)]
            for cp in cps:
                cp.start()
            for cp in cps:
                cp.wait()

    tok = lambda w, off=0: pl.BlockSpec((None, tk, w), lambda i, t: (i, t + off, 0))
    mod_spec = pl.BlockSpec((None, 8, D_MODEL), (lambda i, t: (0, 0, 0)) if is_ctx else (lambda i, t: (i, 0, 0)))
    in_specs = [tok(D_MODEL), mod_spec, _full((1, D_MODEL)), _full((1, 384)), _full((1, 256)), ANY, ANY, ANY,
                pl.BlockSpec((tk, 512), lambda i, t: (t, 0)), pl.BlockSpec((tk, 512), lambda i, t: (t, 0))]
    args = [x, mod, g_attn, g_q, g_kv, w_in, w_uq, w_ukv, cs, sn]
    for a, off in flat_cts:
        in_specs.append(tok(a.shape[-1], off // tk))
        args.append(a)
    if has_res:
        in_specs.append(tok(D_MODEL))
        args.append(dx_res)
    if has_init:
        in_specs += [ANY] * 6
        args += list(init)
    out_shape, out_specs = [], []
    if not is_ctx:
        out_shape.append(_sds((b, l, D_MODEL)))
        out_specs.append(tok(D_MODEL))
    out_shape += [_sds(s) for s in acc_shapes] + [_sds((1 if is_ctx else b, 8, D_MODEL))]
    out_specs += [ANY] * 6 + [mod_spec]
    outs = pl.pallas_call(
        body, name="k1_bwd_ctx" if is_ctx else "k1_bwd", grid=(b, nt), in_specs=in_specs, out_specs=out_specs,
        out_shape=out_shape,
        scratch_shapes=[pltpu.VMEM((D_MODEL, IN_PAD), BF), pltpu.VMEM((384, 1024), BF), pltpu.VMEM((256, 1024), BF)]
        + [pltpu.VMEM(s, F32) for s in acc_shapes] + [pltpu.SemaphoreType.DMA((6,))],
        compiler_params=_cp((ARB, ARB)),
    )(*args)
    outs = list(outs)
    gx = None if is_ctx else outs.pop(0)
    return gx, outs[:6], outs[6]


def _chunk_spec(rev):
    if rev:
        return pl.BlockSpec((None, RET_CHUNK, 512), lambda i, n: (i, N_CHUNK - 1 - n, 0))
    return pl.BlockSpec((None, RET_CHUNK, 512), lambda i, n: (i, n, 0))


def _state_spec(rev):
    if rev:
        return pl.BlockSpec((None, N_HEADS, None, LANE, LANE), lambda i, n: (i, 0, N_CHUNK - 1 - n, 0, 0))
    return pl.BlockSpec((None, N_HEADS, None, LANE, LANE), lambda i, n: (i, 0, n, 0, 0))


_CTX_SPEC = pl.BlockSpec((None, CTX_LEN, 512), lambda i, n: (i, 0, 0))
_DEC_SPEC = pl.BlockSpec((N_HEADS, 1, 1), lambda i, n: (0, 0, 0))
_HEAD_SL = [slice(LANE * h, LANE * (h + 1)) for h in range(N_HEADS)]


def _k2_fwd(rq, rk, rv, rkc, rvc, dec_f, dec_b):
    b = rq.shape[0]

    def body(qf, kf, vf, qb, kb, vb, kc, vc, df, db, of_ref, ob_ref, sf_out, sb_out, sf, sb):
        n = pl.program_id(1)
        for h, sl in enumerate(_HEAD_SL):
            lgf, lgb = log_sigmoid(df[h]), log_sigmoid(db[h])

            @pl.when(n == 0)
            def _():
                sf[h] = ctx_state(kc[:, sl], vc[:, sl], lgf, False)
                sb[h] = ctx_state(kc[:, sl], vc[:, sl], lgb, True)

            sf_out[h] = sf[h]
            sb_out[h] = sb[h]
            o, s = ret_chunk(qf[:, sl], kf[:, sl], vf[:, sl], sf[h], lgf, False)
            of_ref[:, sl] = o
            sf[h] = s
            o, s = ret_chunk(qb[:, sl], kb[:, sl], vb[:, sl], sb[h], lgb, True)
            ob_ref[:, sl] = o
            sb[h] = s

    l = rq.shape[1]
    return pl.pallas_call(
        body, name="k2_fwd", grid=(b, N_CHUNK),
        in_specs=[_chunk_spec(False)] * 3 + [_chunk_spec(True)] * 3 + [_CTX_SPEC, _CTX_SPEC, _DEC_SPEC, _DEC_SPEC],
        out_specs=[_chunk_spec(False), _chunk_spec(True), _state_spec(False), _state_spec(True)],
        out_shape=[_sds((b, l, 512)), _sds((b, l, 512)), _sds((b, N_HEADS, N_CHUNK, LANE, LANE)),
                   _sds((b, N_HEADS, N_CHUNK, LANE, LANE))],
        scratch_shapes=[pltpu.VMEM((N_HEADS, LANE, LANE), F32), pltpu.VMEM((N_HEADS, LANE, LANE), F32)],
        compiler_params=_cp((ARB, ARB)),
    )(rq, rk, rv, rq, rk, rv, rkc, rvc, dec_f, dec_b)


def _k2_bwd(rq, rk, rv, do, sf_prev, sb_prev, rkc, rvc, dec_f, dec_b):
    b, l, _ = rq.shape

    def body(qf, kf, vf, gf, spf, qb, kb, vb, gb, spb, kc, vc, df, db,
             dqf, dkf, dvf, dqb, dkb, dvb, dkc, dvc, ddf, ddb, dsf, dsb):
        n = pl.program_id(1)

        @pl.when(n == 0)
        def _():
            dsf[...] = jnp.zeros((N_HEADS, LANE, LANE), F32)
            dsb[...] = jnp.zeros((N_HEADS, LANE, LANE), F32)

        def one(h, sl, q, k, v, g, sp, dec, ds, dq, dk, dv, dd, rev):
            def f(qv, kv_, vv, sv, dcy):
                return ret_chunk(qv, kv_, vv, sv, log_sigmoid(dcy), rev)

            _, vjp = jax.vjp(f, q[:, sl], k[:, sl], v[:, sl], sp[h], dec[h])
            gq, gk, gv, gs, gd = vjp((g[:, sl], ds[h]))
            dq[:, sl] = gq
            dk[:, sl] = gk
            dv[:, sl] = gv
            ds[h] = gs
            _acc(dd.at[h], jnp.broadcast_to(gd, (8, LANE)), n == 0)

        for h, sl in enumerate(_HEAD_SL):
            one(h, sl, qf, kf, vf, gf, spf, df, dsf, dqf, dkf, dvf, ddf, False)
            one(h, sl, qb, kb, vb, gb, spb, db, dsb, dqb, dkb, dvb, ddb, True)

        @pl.when(n == N_CHUNK - 1)
        def _():
            def f(kcv, vcv, dcy, rev):
                return ctx_state(kcv, vcv, log_sigmoid(dcy), rev)

            for h, sl in enumerate(_HEAD_SL):
                _, vjp_f = jax.vjp(functools.partial(f, rev=False), kc[:, sl], vc[:, sl], df[h])
                gk_f, gv_f, gd_f = vjp_f(dsf[h])
                _, vjp_b = jax.vjp(functools.partial(f, rev=True), kc[:, sl], vc[:, sl], db[h])
                gk_b, gv_b, gd_b = vjp_b(dsb[h])
                dkc[:, sl] = gk_f + gk_b
                dvc[:, sl] = gv_f + gv_b
                ddf[h] += jnp.broadcast_to(gd_f, (8, LANE))
                ddb[h] += jnp.broadcast_to(gd_b, (8, LANE))

    dd_spec = pl.BlockSpec((None, N_HEADS, 8, LANE), lambda i, n: (i, 0, 0, 0))
    return pl.pallas_call(
        body, name="k2_bwd", grid=(b, N_CHUNK),
        in_specs=[_chunk_spec(True)] * 4 + [_state_spec(True)] + [_chunk_spec(False)] * 4 + [_state_spec(False)]
        + [_CTX_SPEC, _CTX_SPEC, _DEC_SPEC, _DEC_SPEC],
        out_specs=[_chunk_spec(True)] * 3 + [_chunk_spec(False)] * 3 + [_CTX_SPEC, _CTX_SPEC, dd_spec, dd_spec],
        out_shape=[_sds((b, l, 512))] * 6 + [_sds((b, CTX_LEN, 512))] * 2 + [_sds((b, N_HEADS, 8, LANE))] * 2,
        scratch_shapes=[pltpu.VMEM((N_HEADS, LANE, LANE), F32), pltpu.VMEM((N_HEADS, LANE, LANE), F32)],
        compiler_params=_cp((ARB, ARB)),
    )(rq, rk, rv, do, sf_prev, rq, rk, rv, do, sb_prev, rkc, rvc, dec_f, dec_b)


TQ = 256


def _softmax_parts(qn, qp, kn, kp):
    s = (_dot(qn, kn, 1, 1) + _dot(qp, kp, 1, 1)) * (1.0 / math.sqrt(192.0))
    e = jnp.exp(s - jnp.max(s, axis=-1, keepdims=True))
    return e, 1.0 / jnp.sum(e, axis=-1, keepdims=True)


def _k3_fwd(qn, qp, kn, kp, v):
    b, l, _ = qn.shape

    def body(qn_ref, qp_ref, kn_ref, kp_ref, v_ref, o_ref):
        kpv = kp_ref[...]
        for sl in _HEAD_SL:
            e, inv = _softmax_parts(qn_ref[:, sl], qp_ref[:, sl], kn_ref[:, sl], kpv)
            o_ref[:, sl] = _dot(e, v_ref[:, sl], 1, 0) * inv

    qs = pl.BlockSpec((None, TQ, 512), lambda i, t: (i, t, 0))
    ks = lambda w: pl.BlockSpec((None, KV_LEN, w), lambda i, t: (i, 0, 0))
    return pl.pallas_call(
        body, name="k3_fwd", grid=(b, l // TQ), in_specs=[qs, qs, ks(512), ks(LANE), ks(512)], out_specs=qs,
        out_shape=_sds((b, l, 512)), compiler_params=_cp((ARB, ARB)),
    )(qn, qp, kn, kp, v)


def _k3_bwd(qn, qp, kn, kp, v, dy):
    b, l, _ = qn.shape
    scale = 1.0 / math.sqrt(192.0)

    def body(qn_ref, qp_ref, kn_ref, kp_ref, v_ref, dy_ref, dqn_ref, dqp_ref, dkn_ref, dkp_ref, dv_ref):
        t0 = pl.program_id(1) == 0
        kpv = kp_ref[...]
        dkp = None
        for sl in _HEAD_SL:
            q_n, q_p, k_n = qn_ref[:, sl].astype(BF), qp_ref[:, sl].astype(BF), kn_ref[:, sl]
            g = dy_ref[:, sl].astype(BF)
            e, inv = _softmax_parts(q_n, q_p, k_n, kpv)
            p = e * inv
            dp = _dot(g, v_ref[:, sl], 1, 1)
            ds = (p * (dp - jnp.sum(dp * p, axis=-1, keepdims=True)) * scale).astype(BF)
            _acc(dv_ref.at[:, sl], _dot(p, g, 0, 0), t0)
            dqn_ref[:, sl] = _dot(ds, k_n, 1, 0)
            dqp_ref[:, sl] = _dot(ds, kpv, 1, 0)
            _acc(dkn_ref.at[:, sl], _dot(ds, q_n, 0, 0), t0)
            gkp = _dot(ds, q_p, 0, 0)
            dkp = gkp if dkp is None else dkp + gkp
        _acc(dkp_ref, dkp, t0)

    qs = pl.BlockSpec((None, TQ, 512), lambda i, t: (i, t, 0))
    ks = lambda w: pl.BlockSpec((None, KV_LEN, w), lambda i, t: (i, 0, 0))
    return pl.pallas_call(
        body, name="k3_bwd", grid=(b, l // TQ), in_specs=[qs, qs, ks(512), ks(LANE), ks(512), qs],
        out_specs=[qs, qs, ks(512), ks(LANE), ks(512)],
        out_shape=[_sds((b, l, 512)), _sds((b, l, 512)), _sds((b, KV_LEN, 512)), _sds((b, KV_LEN, LANE)),
                   _sds((b, KV_LEN, 512))],
        compiler_params=_cp((ARB, ARB)),
    )(qn, qp, kn, kp, v, dy)


def _mod_rows(mod_ref, rows):
    return [mod_ref[r:r + 1, :] for r in rows]


def _k4a_fwd(x, o_f, o_b, rg, y_mla, g_ret, w_out, mod, g_ffn):
    b, l, _ = x.shape

    def body(x_ref, of_ref, ob_ref, rg_ref, ym_ref, gr_ref, wo_ref, mod_ref, gf_ref, xm_ref, h2_ref):
        gt_a, sh_f, sc_f = _mod_rows(mod_ref, (2, 3, 4))
        x_mid, h2 = k4a_tile(x_ref[...], of_ref[...], ob_ref[...], rg_ref[...], ym_ref[...], gr_ref[...], gt_a,
                             gf_ref[...], sh_f, sc_f, wo_ref[...], None)
        xm_ref[...] = x_mid
        h2_ref[...] = h2.astype(BF)

    tok = lambda w: pl.BlockSpec((None, TOK, w), lambda i, t: (i, t, 0))
    mod_spec = pl.BlockSpec((None, 8, D_MODEL), lambda i, t: (i, 0, 0))
    return pl.pallas_call(
        body, name="k4a_fwd", grid=(b, l // TOK),
        in_specs=[tok(D_MODEL), tok(512), tok(512), tok(512), tok(512), _full((1, 512)), _full((D_MODEL, D_MODEL)),
                  mod_spec, _full((1, D_MODEL))],
        out_specs=[tok(D_MODEL), tok(D_MODEL)], out_shape=[_sds((b, l, D_MODEL)), _sds((b, l, D_MODEL), BF)],
        compiler_params=_cp((ARB, ARB)),
    )(x, o_f, o_b, rg, y_mla, g_ret, w_out, mod, g_ffn)


TOK_M = 512


def _last_j(j, idx):
    return jnp.where(j == N_DEV - 1, idx, 0)


def _k4b_mlp_loss(h2, w1, w2, x_mid, mod, g_final, tgt):
    b, l, _ = h2.shape
    nt = l // TOK_M

    def body(h2_ref, w1_ref, w2_ref, xm_ref, mod_ref, gfin_ref, tgt_ref, dxm_ref, dmlp_ref, loss_ref, dgt_ref,
             dgfin_ref, acc):
        j, i, t = pl.program_id(0), pl.program_id(1), pl.program_id(2)
        rows = pl.ds(pl.multiple_of((i * nt + t) * TOK_M, TOK_M), TOK_M)
        a = _dot(h2_ref[...], w1_ref[...], 1, 0)
        part = _dot(jnp.square(jnp.maximum(a, 0.0)), w2_ref[...], 1, 0)
        _acc(acc.at[rows, :], part, j == 0)

        @pl.when(j == N_DEV - 1)
        def _():
            (gt_f,) = _mod_rows(mod_ref, (5,))
            loss, vjp = jax.vjp(k4c_tile, xm_ref[...], acc[rows, :], gt_f, gfin_ref[...], tgt_ref[...])
            dxm, dmlp, dgt, dgfin, _ = vjp(jnp.ones((1, 1), F32))
            dxm_ref[...] = dxm
            dmlp_ref[...] = dmlp.astype(BF)
            first = jnp.logical_and(i == 0, t == 0)
            _acc(loss_ref, jnp.broadcast_to(loss, (8, LANE)), first)
            _acc(dgfin_ref, dgfin, first)
            _acc(dgt_ref, dgt, t == 0)

    tok = lambda w: pl.BlockSpec((None, TOK_M, w), lambda j, i, t: (i, t, 0))
    tok_last = lambda w: pl.BlockSpec((None, TOK_M, w), lambda j, i, t: (_last_j(j, i), _last_j(j, t), 0))
    return pl.pallas_call(
        body, name="k4b_mlp_loss", grid=(N_DEV, b, nt),
        in_specs=[tok(D_MODEL), pl.BlockSpec((None, D_MODEL, FF_BLK), lambda j, i, t: (j, 0, 0)),
                  pl.BlockSpec((None, FF_BLK, D_MODEL), lambda j, i, t: (j, 0, 0)), tok_last(D_MODEL),
                  pl.BlockSpec((None, 8, D_MODEL), lambda j, i, t: (i, 0, 0)),
                  pl.BlockSpec((1, D_MODEL), lambda j, i, t: (0, 0)), tok_last(D_MODEL)],
        out_specs=[tok_last(D_MODEL), tok_last(D_MODEL), pl.BlockSpec((8, LANE), lambda j, i, t: (0, 0)),
                   pl.BlockSpec((None, 1, D_MODEL), lambda j, i, t: (_last_j(j, i), 0, 0)),
                   pl.BlockSpec((1, D_MODEL), lambda j, i, t: (0, 0))],
        out_shape=[_sds((b, l, D_MODEL)), _sds((b, l, D_MODEL), BF), _sds((8, LANE)), _sds((b, 1, D_MODEL)),
                   _sds((1, D_MODEL))],
        scratch_shapes=[pltpu.VMEM((b * l, D_MODEL), F32)],
        compiler_params=_cp((ARB, ARB, ARB)),
    )(h2, w1, w2, x_mid, mod, g_final, tgt)


def _k4d_mlp_bwd(h2, dmlp, w1, w2):
    b, l, _ = h2.shape
    nt = l // TOK_M

    def body(h2_ref, dm_ref, w1_ref, w2_ref, dh2_ref, dw1_ref, dw2_ref, acc1, acc2, dh2s):
        j, i, t = pl.program_id(0), pl.program_id(1), pl.program_id(2)
        first = jnp.logical_and(i == 0, t == 0)
        rows = pl.ds(pl.multiple_of((i * nt + t) * TOK_M, TOK_M), TOK_M)
        h2v, dm = h2_ref[...], dm_ref[...]
        r = jnp.maximum(_dot(h2v, w1_ref[...], 1, 0), 0.0)
        da = _dot(dm, w2_ref[...], 1, 1) * (2.0 * r)
        _acc(acc2, _dot(jnp.square(r), dm, 0, 0), first)
        _acc(acc1, _dot(h2v, da, 0, 0), first)
        _acc(dh2s.at[rows, :], _dot(da, w1_ref[...], 1, 1), j == 0)

        @pl.when(jnp.logical_and(i == b - 1, t == nt - 1))
        def _():
            dw1_ref[...] = acc1[...].astype(BF)
            dw2_ref[...] = acc2[...].astype(BF)

        @pl.when(j == N_DEV - 1)
        def _():
            dh2_ref[...] = dh2s[rows, :]

    tok = lambda w: pl.BlockSpec((None, TOK_M, w), lambda j, i, t: (i, t, 0))
    return pl.pallas_call(
        body, name="k4d_mlp_bwd", grid=(N_DEV, b, nt),
        in_specs=[tok(D_MODEL), tok(D_MODEL), pl.BlockSpec((None, D_MODEL, FF_BLK), lambda j, i, t: (j, 0, 0)),
                  pl.BlockSpec((None, FF_BLK, D_MODEL), lambda j, i, t: (j, 0, 0))],
        out_specs=[pl.BlockSpec((None, TOK_M, D_MODEL), lambda j, i, t: (_last_j(j, i), _last_j(j, t), 0)),
                   pl.BlockSpec((None, D_MODEL, FF_BLK), lambda j, i, t: (j, 0, 0)),
                   pl.BlockSpec((None, FF_BLK, D_MODEL), lambda j, i, t: (j, 0, 0))],
        out_shape=[_sds((b, l, D_MODEL)), _sds((N_DEV, D_MODEL, FF_BLK), BF), _sds((N_DEV, FF_BLK, D_MODEL), BF)],
        scratch_shapes=[pltpu.VMEM((D_MODEL, FF_BLK), F32), pltpu.VMEM((FF_BLK, D_MODEL), F32),
                        pltpu.VMEM((b * l, D_MODEL), F32)],
        compiler_params=_cp((ARB, ARB, ARB)),
    )(h2, dmlp, w1, w2)


def _k4e_bwd(x, o_f, o_b, rg, y_mla, g_ret, w_out, mod, g_ffn, dxm, dh2):
    b, l, _ = x.shape

    def body(x_ref, of_ref, ob_ref, rg_ref, ym_ref, gr_ref, wo_ref, mod_ref, gf_ref, dxm_ref, dh2_ref,
             dx_ref, do_ref, drg_ref, dym_ref, dwo_ref, dgr_ref, dgf_ref, dmod_ref):
        i, t = pl.program_id(0), pl.program_id(1)
        first = jnp.logical_and(i == 0, t == 0)
        gt_a, sh_f, sc_f = _mod_rows(mod_ref, (2, 3, 4))
        wo = wo_ref[...]

        def f(xv, ofv, rgv, ymv, grv, gta, gfv, shf, scf, p_out):
            return k4a_tile(xv, ofv, ob_ref[...], rgv, ymv, grv, gta, gfv, shf, scf, wo, p_out)

        _, vjp = jax.vjp(f, x_ref[...], of_ref[...], rg_ref[...], ym_ref[...], gr_ref[...], gt_a, gf_ref[...], sh_f,
                         sc_f, jnp.zeros((D_MODEL, D_MODEL), F32))
        dx, do, drg, dym, dgr, dgta, dgf, dshf, dscf, dwo = vjp((dxm_ref[...], dh2_ref[...]))
        dx_ref[...] = dx
        do_ref[...] = do
        drg_ref[...] = drg
        dym_ref[...] = dym
        _acc(dwo_ref, dwo, first)
        _acc(dgr_ref, dgr, first)
        _acc(dgf_ref, dgf, first)
        t0 = t == 0
        _acc(dmod_ref.at[2:3, :], dgta, t0)
        _acc(dmod_ref.at[3:4, :], dshf, t0)
        _acc(dmod_ref.at[4:5, :], dscf, t0)

        @pl.when(t0)
        def _():
            dmod_ref[0:2, :] = jnp.zeros((2, D_MODEL), F32)
            dmod_ref[5:8, :] = jnp.zeros((3, D_MODEL), F32)

    tok = lambda w: pl.BlockSpec((None, TOK_B, w), lambda i, t: (i, t, 0))
    mod_spec = pl.BlockSpec((None, 8, D_MODEL), lambda i, t: (i, 0, 0))
    return pl.pallas_call(
        body, name="k4e_bwd", grid=(b, l // TOK_B),
        in_specs=[tok(D_MODEL), tok(512), tok(512), tok(512), tok(512), _full((1, 512)), _full((D_MODEL, D_MODEL)),
                  mod_spec, _full((1, D_MODEL)), tok(D_MODEL), tok(D_MODEL)],
        out_specs=[tok(D_MODEL), tok(512), tok(512), tok(512), _full((D_MODEL, D_MODEL)), _full((1, 512)),
                   _full((1, D_MODEL)), mod_spec],
        out_shape=[_sds((b, l, D_MODEL)), _sds((b, l, 512)), _sds((b, l, 512)), _sds((b, l, 512)),
                   _sds((D_MODEL, D_MODEL)), _sds((1, 512)), _sds((1, D_MODEL)), _sds((b, 8, D_MODEL))],
        compiler_params=_cp((ARB, ARB)),
    )(x, o_f, o_b, rg, y_mla, g_ret, w_out, mod, g_ffn, dxm, dh2)


def _adamw(w, m, v, pieces, name):
    r, c = w.shape
    npc = pieces.shape[0]
    rb = r
    for cand in (256, 128, 64, 32, 16, 8):
        if r > cand and r % cand == 0 and cand * c * 4 * (npc + 7) * 2 <= 24 * 1024 * 1024:
            rb = cand
            break

    def body(w_ref, m_ref, v_ref, p_ref, g_ref, d_ref, nm_ref, nv_ref):
        g = p_ref[0].astype(F32)
        for k in range(1, npc):
            g = g + p_ref[k].astype(F32)
        wv = w_ref[...]
        mn = ADAM_B1 * m_ref[...] + (1.0 - ADAM_B1) * g
        vn = ADAM_B2 * v_ref[...] + (1.0 - ADAM_B2) * jnp.square(g)
        m_hat = mn / (1.0 - ADAM_B1 ** ADAM_STEP)
        v_hat = vn / (1.0 - ADAM_B2 ** ADAM_STEP)
        g_ref[...] = g
        d_ref[...] = -ADAM_LR * (m_hat / (jnp.sqrt(v_hat) + ADAM_EPS) + ADAM_WD * wv)
        nm_ref[...] = mn
        nv_ref[...] = vn

    blk = pl.BlockSpec((rb, c), lambda i: (i, 0))
    return pl.pallas_call(
        body, name=name, grid=(r // rb,), in_specs=[blk, blk, blk, pl.BlockSpec((npc, rb, c), lambda i: (0, i, 0))],
        out_specs=[blk] * 4, out_shape=[_sds((r, c))] * 4, compiler_params=_cp((ARB,)),
    )(w, m, v, pieces)


def _pad_heads(w, d):
    k = w.shape[0]
    return jnp.pad(w.reshape(k, N_HEADS, d), ((0, 0), (0, 0), (0, LANE - d))).reshape(k, N_HEADS * LANE)


def _cut_heads(w, d):
    k = w.shape[0]
    return w.reshape(k, N_HEADS, LANE)[:, :, :d].reshape(k, N_HEADS * d)


def _w_in_pad(w):
    return jnp.concatenate([_pad_heads(w[:, 0:256], 64), _pad_heads(w[:, 256:512], 64), w[:, 512:2176],
                            jnp.pad(w[:, 2176:2240], ((0, 0), (0, 64)))], axis=1)


def _w_in_cut(g):
    return jnp.concatenate([_cut_heads(g[:, 0:512], 64), _cut_heads(g[:, 512:1024], 64), g[:, 1024:2688],
                            g[:, 2688:2752]], axis=1)


def _w_uq_pad(w):
    w = w.reshape(384, N_HEADS, 192)
    return jnp.concatenate([w[:, :, :128].reshape(384, 512),
                            jnp.pad(w[:, :, 128:], ((0, 0), (0, 0), (0, 64))).reshape(384, 512)], axis=1)


def _w_uq_cut(g):
    return jnp.concatenate([g[:, :512].reshape(384, N_HEADS, 128), g[:, 512:].reshape(384, N_HEADS, 128)[:, :, :64]],
                           axis=2).reshape(384, 768)


def _w_ukv_perm(w):
    w = w.reshape(256, N_HEADS, 256)
    return jnp.concatenate([w[:, :, :128].reshape(256, 512), w[:, :, 128:].reshape(256, 512)], axis=1)


def _w_ukv_unperm(g):
    return jnp.concatenate([g[:, :512].reshape(256, N_HEADS, 128), g[:, 512:].reshape(256, N_HEADS, 128)],
                           axis=2).reshape(256, 1024)


def _unshard_cols(g):
    return jnp.transpose(g, (1, 0, 2)).reshape(g.shape[1], N_DEV * g.shape[2])


def _shard_cols(w):
    k, n = w.shape
    return jnp.transpose(w.reshape(k, N_DEV, n // N_DEV), (1, 0, 2))


def _rope_tables():
    rows = SEQ // GRID_W
    row = jnp.repeat(jnp.arange(rows, dtype=F32), GRID_W)
    col = jnp.tile(jnp.arange(GRID_W, dtype=F32), rows)
    freq = ROPE_BASE ** (-jnp.arange(16, dtype=F32) / 16)
    ang = jnp.concatenate([row[:, None] * freq, col[:, None] * freq], axis=-1)
    cos, sin = jnp.cos(ang), jnp.sin(ang)
    z = jnp.zeros((SEQ, 64), F32)
    cs = jnp.concatenate([cos, cos, z], axis=1)
    sn = jnp.concatenate([-sin, sin, z], axis=1)
    return jnp.tile(cs, (1, N_HEADS)), jnp.tile(sn, (1, N_HEADS))


_SMALL = (("c_ctx", 1024), ("b_ada", 6144), ("g_attn", 1024), ("g_ffn", 1024), ("ret_decay_fwd", 128),
          ("ret_decay_bwd", 128), ("g_ret", 512), ("g_q_lora", 384), ("g_kv_lora", 256), ("g_final", 1024))
_SMALL_REAL = {"c_ctx": 1024, "b_ada": 6144, "g_attn": 1024, "g_ffn": 1024, "ret_decay_fwd": 4, "ret_decay_bwd": 4,
               "g_ret": 512, "g_q_lora": 384, "g_kv_lora": 256, "g_final": 1024}
_SMALL_ROWS = sum(n for _, n in _SMALL) // LANE


def _pack_small(vals):
    parts = []
    for name, n in _SMALL:
        a = vals[name].reshape(-1).astype(F32)
        parts.append(jnp.pad(a, (0, n - a.shape[0])))
    return jnp.concatenate(parts).reshape(_SMALL_ROWS, LANE)


def _unpack_small(packed, shapes):
    flat = packed.reshape(-1)
    out, off = {}, 0
    for name, n in _SMALL:
        out[name] = flat[off:off + _SMALL_REAL[name]].reshape(shapes[name])
        off += n
    return out


def kernel(x, c, ctx, c_ctx, w_ada, b_ada, g_attn, g_ffn, w_in, ret_decay_fwd, ret_decay_bwd, g_ret, g_q_lora, w_uq, g_kv_lora, w_ukv, w_out, w_ff1, w_ff2, g_final, loss_target, m_c_ctx, m_w_ada, m_b_ada, m_g_attn, m_g_ffn, m_w_in, m_ret_decay_fwd, m_ret_decay_bwd, m_g_ret, m_g_q_lora, m_w_uq, m_g_kv_lora, m_w_ukv, m_w_out, m_w_ff1, m_w_ff2, m_g_final, v_c_ctx, v_w_ada, v_b_ada, v_g_attn, v_g_ffn, v_w_in, v_ret_decay_fwd, v_ret_decay_bwd, v_g_ret, v_g_q_lora, v_w_uq, v_g_kv_lora, v_w_ukv, v_w_out, v_w_ff1, v_w_ff2, v_g_final):
    me = 4 * lax.axis_index("x") + 2 * lax.axis_index("y") + lax.axis_index("c")
    nb = x.shape[0]

    c_pad = jnp.pad(c, ((0, 8 - nb), (0, 0)))
    c_all, g_in, g_uq, g_ukv = _exchange([c_pad, w_in[0].astype(BF), w_uq[0].astype(BF), w_ukv[0].astype(BF)], True,
                                         "gather_weights")
    wi = _w_in_pad(_unshard_cols(g_in))
    wq = _w_uq_pad(_unshard_cols(g_uq))
    wk = _w_ukv_perm(_unshard_cols(g_ukv))

    crows = jnp.concatenate([c_all[:, :nb].reshape(N_DEV * nb, D_MODEL), c_ctx[None], jnp.zeros((7, D_MODEL), F32)])
    b_blk = lax.dynamic_slice(b_ada, (0, me * 768), (1, 768))
    (mod_g,) = _exchange([_mod_fwd(crows, w_ada[0], b_blk)], True, "gather_mod")
    mod_all = _unshard_cols(mod_g)
    behind = mod_g[0, 0, 0:1] * 0.0
    st_g = _exchange_start([(w_out[0] + behind).astype(BF), w_ff1[0].astype(BF), w_ff2[0].astype(BF)], True,
                           "gather_ff_start")
    mod_all = mod_all + st_g["token"][0:1, 0:1]
    mod_mine = lax.dynamic_slice(mod_all, (me * nb, 0), (nb, 6 * D_MODEL)).reshape(nb, 6, D_MODEL)
    mod = jnp.pad(mod_mine, ((0, 0), (0, 2), (0, 0)))
    mod_c = jnp.pad(mod_all[16].reshape(1, 6, D_MODEL), ((0, 0), (0, 2), (0, 0)))

    cs, sn = _rope_tables()
    dec_f = ret_decay_fwd.reshape(N_HEADS, 1, 1)
    dec_b = ret_decay_bwd.reshape(N_HEADS, 1, 1)

    rkc, rvc, knc, kpc, vc = _k1_fwd(ctx, mod_c, g_attn, g_q_lora, g_kv_lora, wi, wq, wk, cs, sn, True)
    rq, rk, rv, rg, qn, qp, kn, kp, vv = _k1_fwd(x, mod, g_attn, g_q_lora, g_kv_lora, wi, wq, wk, cs, sn, False)
    o_f, o_b, sf_prev, sb_prev = _k2_fwd(rq, rk, rv, rkc, rvc, dec_f, dec_b)
    kn_all = jnp.concatenate([knc, kn], axis=1).astype(BF)
    kp_all = jnp.concatenate([kpc, kp], axis=1).astype(BF)
    v_all = jnp.concatenate([vc, vv], axis=1).astype(BF)
    y_mla = _k3_fwd(qn, qp, kn_all, kp_all, v_all)
    g_out, g_ff1, g_ff2 = _exchange_wait(st_g, y_mla, "gather_ff_wait")
    wo = g_out.reshape(D_MODEL, D_MODEL)
    x_mid, h2 = _k4a_fwd(x, o_f, o_b, rg, y_mla, g_ret, wo, mod, g_ffn)
    dxm, dmlp, loss_acc, dgt_f, dg_final = _k4b_mlp_loss(h2, g_ff1, g_ff2, x_mid, mod, g_final.reshape(1, D_MODEL),
                                                         loss_target)

    dh2, dw1, dw2 = _k4d_mlp_bwd(h2, dmlp, g_ff1, g_ff2)
    st_s = _exchange_start([dw1, dw2], False, "scatter_ff_start")
    g_ret_t = g_ret + st_s["token"][0:1, 0:1]
    dx_res, do, drg, dym, dwo, dg_ret, dg_ffn, dmod_a = _k4e_bwd(x, o_f, o_b, rg, y_mla, g_ret_t, wo, mod, g_ffn, dxm, dh2)
    dqn, dqp, dkn_all, dkp_all, dv_all = _k3_bwd(qn, qp, kn_all, kp_all, v_all, dym)
    dqf, dkf, dvf, dqb, dkb, dvb, dkc, dvc, ddf, ddb = _k2_bwd(rq, rk, rv, do, sf_prev, sb_prev, rkc, rvc, dec_f, dec_b)
    cts = [[(dqf, 0), (dqb, 0)], [(dkf, 0), (dkb, 0)], [(dvf, 0), (dvb, 0)], [(drg, 0)], [(dqn, 0)], [(dqp, 0)],
           [(dkn_all, CTX_LEN)], [(dkp_all, CTX_LEN)], [(dv_all, CTX_LEN)]]
    grad_x, accs, dmod_1 = _k1_bwd(x, mod, g_attn, g_q_lora, g_kv_lora, wi, wq, wk, cs, sn, cts, dx_res, None, False)
    cts_c = [[(dkc, 0)], [(dvc, 0)], [(dkn_all, 0)], [(dkp_all, 0)], [(dv_all, 0)]]
    _, accs, dmod_c1 = _k1_bwd(ctx, mod_c, g_attn, g_q_lora, g_kv_lora, wi, wq, wk, cs, sn, cts_c, None, accs, True)
    dwi, dwq, dwk, dg_attn, dg_q, dg_kv = accs

    dmod_loc = (dmod_a + dmod_1).at[:, 5, :].set(dgt_f[:, 0, :])[:, :6, :].reshape(nb, 6 * D_MODEL)
    dmod_ctx = dmod_c1[:, :6, :].reshape(1, 6 * D_MODEL)
    small = {"c_ctx": jnp.zeros((D_MODEL,), F32), "b_ada": jnp.zeros((6 * D_MODEL,), F32), "g_attn": dg_attn,
             "g_ffn": dg_ffn, "ret_decay_fwd": jnp.sum(ddf[:, :, 0, 0], axis=0), "ret_decay_bwd": jnp.sum(ddb[:, :, 0, 0], axis=0),
             "g_ret": dg_ret, "g_q_lora": dg_q, "g_kv_lora": dg_kv, "g_final": dg_final}
    extra = jnp.concatenate([dmod_loc, dmod_ctx, jnp.zeros((5, 6 * D_MODEL), F32)]).reshape(8 * 48, LANE)
    loss_rows = loss_acc
    sm_g, ex_g, loss_g = _exchange([_pack_small(small), extra, loss_rows], True, "gather_small")
    ex_g = ex_g.reshape(N_DEV, 8, 6 * D_MODEL)
    dmod_all = ex_g[:, :nb].reshape(N_DEV * nb, 6 * D_MODEL)
    dmodc_parts = ex_g[:, nb]
    dmod_full = jnp.concatenate([dmod_all, jnp.zeros((8, 6 * D_MODEL), F32)])
    dmod_blk = lax.dynamic_slice(dmod_full, (0, me * 768), (24, 768))
    dmodc_blk = lax.dynamic_slice(dmodc_parts, (0, me * 768), (N_DEV, 768))
    gw_ada, gcc_part, gb_ada = _mod_bwd(crows, w_ada[0], dmod_blk, dmodc_blk, dmod_full, dmodc_parts)
    (gcc_g,) = _exchange([gcc_part], True, "gather_c_ctx")

    p_ff1, p_ff2 = _exchange_wait(st_s, dmod_c1, "scatter_ff_wait")
    pieces = _exchange([_shard_cols(_w_in_cut(dwi)).astype(BF), _shard_cols(_w_uq_cut(dwq)).astype(BF),
                        _shard_cols(_w_ukv_unperm(dwk)).astype(BF), dwo.reshape(N_DEV, 128, D_MODEL).astype(BF)],
                       False, "scatter_grads") + [p_ff1, p_ff2]

    res = {}
    big = (("w_in", w_in, m_w_in, v_w_in, pieces[0]), ("w_uq", w_uq, m_w_uq, v_w_uq, pieces[1]),
           ("w_ukv", w_ukv, m_w_ukv, v_w_ukv, pieces[2]), ("w_out", w_out, m_w_out, v_w_out, pieces[3]),
           ("w_ff1", w_ff1, m_w_ff1, v_w_ff1, pieces[4]), ("w_ff2", w_ff2, m_w_ff2, v_w_ff2, pieces[5]),
           ("w_ada", w_ada, m_w_ada, v_w_ada, gw_ada[None]))
    for name, w, m, v, pcs in big:
        res[name] = [a[None] for a in _adamw(w[0], m[0], v[0], pcs, "adamw_" + name)]

    smalls = {"c_ctx": (c_ctx, m_c_ctx, v_c_ctx), "b_ada": (b_ada, m_b_ada, v_b_ada), "g_attn": (g_attn, m_g_attn, v_g_attn),
              "g_ffn": (g_ffn, m_g_ffn, v_g_ffn), "ret_decay_fwd": (ret_decay_fwd, m_ret_decay_fwd, v_ret_decay_fwd),
              "ret_decay_bwd": (ret_decay_bwd, m_ret_decay_bwd, v_ret_decay_bwd), "g_ret": (g_ret, m_g_ret, v_g_ret),
              "g_q_lora": (g_q_lora, m_g_q_lora, v_g_q_lora), "g_kv_lora": (g_kv_lora, m_g_kv_lora, v_g_kv_lora),
              "g_final": (g_final, m_g_final, v_g_final)}
    sm_pieces = sm_g.reshape(N_DEV, _SMALL_ROWS * LANE)
    sm_pieces = sm_pieces.at[:, 0:1024].set(gcc_g[:, 0, :])
    sm_pieces = sm_pieces.at[0, 1024:1024 + 6144].set(gb_ada[0])
    sm_pieces = sm_pieces.reshape(N_DEV, _SMALL_ROWS, LANE)
    packed = [_pack_small({k: t[i] for k, t in smalls.items()}) for i in range(3)]
    sm_out = _adamw(packed[0], packed[1], packed[2], sm_pieces, "adamw_small")
    shapes = {k: t[0].shape for k, t in smalls.items()}
    sm_res = [_unpack_small(o, shapes) for o in sm_out]
    for name in smalls:
        res[name] = [r[name] for r in sm_res]

    loss = loss_g[0, 0, 0]
    for k in range(1, N_DEV):
        loss = loss + loss_g[k, 0, 0]

    order = ("c_ctx", "w_ada", "b_ada", "g_attn", "g_ffn", "w_in", "ret_decay_fwd", "ret_decay_bwd", "g_ret", "g_q_lora",
             "w_uq", "g_kv_lora", "w_ukv", "w_out", "w_ff1", "w_ff2", "g_final")
    return (loss, grad_x, *[res[n][0] for n in order], *[res[n][1] for n in order], *[res[n][2] for n in order],
            *[res[n][3] for n in order])
```

```python
import functools
import math

import jax
import jax.numpy as jnp
from jax import lax
from jax.experimental import pallas as pl
from jax.experimental.pallas import tpu as pltpu

F32 = jnp.float32
BF = jnp.bfloat16
EPS = 1e-6
LANE = 128
N_DEV = 8
D_MODEL = 1024
SEQ = 2048
CTX_LEN = 256
GRID_W = 64
N_HEADS = 4
RET_CHUNK = 128
N_CHUNK = SEQ // RET_CHUNK
D_FF = 4096
FF_BLK = D_FF // N_DEV
IN_PAD = 2816
KV_LEN = CTX_LEN + SEQ
ROPE_BASE = 10000.0
ADAM_LR, ADAM_B1, ADAM_B2, ADAM_EPS, ADAM_WD, ADAM_STEP = 0.001, 0.9, 0.999, 1e-08, 0.01, 10
TOK = 256
TOK_B = 128
VMEM_LIMIT = 56 * 1024 * 1024
ARB = "arbitrary"
MESH = pl.DeviceIdType.MESH
_HEAD_SL = [slice(LANE * h, LANE * (h + 1)) for h in range(N_HEADS)]
W_SHAPES = [(D_MODEL, 2048), (D_MODEL, 768), (384, 1024), (256, 1024)]


def _dot(a, b, ca, cb):
    return lax.dot_general(a.astype(BF), b.astype(BF), (((ca,), (cb,)), ((), ())), preferred_element_type=F32)


@jax.custom_vjp
def mm(a, b):
    return _dot(a, b, 1, 0)


@jax.custom_vjp
def mm_nt(a, b):
    return _dot(a, b, 1, 1)


@jax.custom_vjp
def mm_tn(a, b):
    return _dot(a, b, 0, 0)


mm.defvjp(lambda a, b: (_dot(a, b, 1, 0), (a, b)), lambda r, g: (mm_nt(g, r[1]), mm_tn(r[0], g)))
mm_nt.defvjp(lambda a, b: (_dot(a, b, 1, 1), (a, b)), lambda r, g: (mm(g, r[1]), mm_tn(g, r[0])))
mm_tn.defvjp(lambda a, b: (_dot(a, b, 0, 0), (a, b)), lambda r, g: (mm_nt(r[1], g), mm(r[0], g)))


@jax.custom_vjp
def _mmw(a, w, probe):
    return _dot(a, w, 1, 0)


def _mmw_bwd(r, g):
    a, w = r
    return mm_nt(g, w), jnp.zeros_like(w), mm_tn(a, g)


_mmw.defvjp(lambda a, w, probe: (_dot(a, w, 1, 0), (a, w)), _mmw_bwd)


def mmw(a, w, probe):
    return _dot(a, w, 1, 0) if probe is None else _mmw(a, w, probe)


def rmsn(x, g):
    return x * lax.rsqrt(jnp.mean(x * x, axis=-1, keepdims=True) + EPS) * g


def silu(x):
    return x * jax.nn.sigmoid(x)


def _swap32_impl(x):
    n = x.shape[-1]
    lane = lax.broadcasted_iota(jnp.int32, x.shape, x.ndim - 1) % LANE
    up = pltpu.roll(x, n - 32, x.ndim - 1)
    dn = pltpu.roll(x, 32, x.ndim - 1)
    return jnp.where(lane < 32, up, jnp.where(lane < 64, dn, 0.0))


@jax.custom_vjp
def swap32(x):
    return _swap32_impl(x)


swap32.defvjp(lambda x: (_swap32_impl(x), None), lambda _, g: (_swap32_impl(g),))


def rope(x, cs, sn):
    return x * cs + swap32(x) * sn


def k1_tile(x, sh, sc, g_attn, g_q, g_kv, ws, ps, tabs, is_ctx):
    w_a, w_b, w_uq, w_ukv = ws
    p_a, p_b, p_uq, p_ukv = ps
    cs, sn, cq_t, sq_t = tabs
    h = rmsn(x, g_attn) * (1.0 + sc) + sh
    pa = mmw(h, w_a, p_a)
    pb = mmw(h, w_b, p_b)
    rk = pa[:, 512:1024] * 0.125
    rv = pa[:, 1024:1536]
    kpe = pb[:, 640:768]
    kv = mmw(rmsn(pb[:, 384:640], g_kv), w_ukv, p_ukv)
    if not is_ctx:
        rk = rope(rk, cs, sn)
        kpe = rope(kpe, cs[:, :LANE], sn[:, :LANE])
    k_full = jnp.concatenate([piece for sl in _HEAD_SL for piece in (kv[:, sl], kpe)], axis=-1)
    v = kv[:, 512:]
    if is_ctx:
        return rk, rv, k_full, v
    rq = rope(pa[:, 0:512], cs, sn)
    rg = pa[:, 1536:2048]
    q = rope(mmw(rmsn(pb[:, 0:384], g_q), w_uq, p_uq), cq_t, sq_t)
    return rq, rk, rv, rg, q, k_full, v


def log_sigmoid(x):
    return jnp.minimum(x, 0.0) - jnp.log(1.0 + jnp.exp(-jnp.abs(x)))


def ret_chunk(q, k, v, s, lg, reverse):
    c = RET_CHUNK
    ii = lax.broadcasted_iota(jnp.int32, (c, c), 0).astype(F32)
    jj = lax.broadcasted_iota(jnp.int32, (c, c), 1).astype(F32)
    diff = (jj - ii) if reverse else (ii - jj)
    dec = jnp.where(diff >= 0, jnp.exp(lg * jnp.maximum(diff, 0.0)), 0.0)
    pos = lax.broadcasted_iota(jnp.int32, (c, 1), 0).astype(F32)
    if reverse:
        wk, wq = jnp.exp(lg * pos), jnp.exp(lg * (c - pos))
    else:
        wk, wq = jnp.exp(lg * (c - 1.0 - pos)), jnp.exp(lg * (pos + 1.0))
    o = mm(mm_nt(q, k) * dec, v) + mm(q * wq, s)
    s_next = jnp.exp(lg * float(c)) * s + mm_tn(k * wk, v)
    return o, s_next


def ctx_state(kc, vc, lg, reverse):
    n = kc.shape[0]
    pos = lax.broadcasted_iota(jnp.int32, (n, 1), 0).astype(F32)
    w = jnp.exp(lg * pos) if reverse else jnp.exp(lg * (n - 1.0 - pos))
    return mm_tn(kc * w, vc)


def attn_head(qn, qp, kn, kp, v):
    s = (mm_nt(qn, kn) + mm_nt(qp, kp)) * (1.0 / math.sqrt(192.0))
    e = jnp.exp(s - jnp.max(s, axis=-1, keepdims=True))
    return mm(e / jnp.sum(e, axis=-1, keepdims=True), v)


def gn_gate(o, rg, g_ret):
    ys = []
    for h in range(N_HEADS):
        sl = slice(LANE * h, LANE * (h + 1))
        oh = o[:, sl]
        mu = jnp.mean(oh, axis=-1, keepdims=True)
        var = jnp.mean(jnp.square(oh - mu), axis=-1, keepdims=True)
        ys.append((oh - mu) * lax.rsqrt(var + EPS) * g_ret[:, sl])
    return jnp.concatenate(ys, axis=-1) * silu(rg)


def k4a_tile(x, o_f, o_b, rg, y_mla, g_ret, gt_a, g_ffn, sh_f, sc_f, w_out, p_out):
    mix = jnp.concatenate([gn_gate(o_f + o_b, rg, g_ret), y_mla], axis=-1)
    x_mid = x + gt_a * mmw(mix, w_out, p_out)
    h2 = rmsn(x_mid, g_ffn) * (1.0 + sc_f) + sh_f
    return x_mid, h2


def k4c_tile(x_mid, mlp, gt_f, g_final, tgt):
    y = rmsn(x_mid + gt_f * mlp, g_final)
    per_tok = jnp.mean(jnp.square(y - tgt), axis=-1, keepdims=True)
    return 0.5 * jnp.sum(per_tok, axis=0, keepdims=True)


def _cp(sem=None, vmem=VMEM_LIMIT):
    return pltpu.CompilerParams(dimension_semantics=sem, vmem_limit_bytes=vmem)


def _acc(ref, val, first):
    @pl.when(first)
    def _():
        ref[...] = val

    @pl.when(jnp.logical_not(first))
    def _():
        ref[...] += val


def _full(shape):
    nd = len(shape)
    return pl.BlockSpec(shape, lambda *_: (0,) * nd)


ANY = pl.BlockSpec(memory_space=pl.ANY)


def _sds(shape, dtype=F32):
    return jax.ShapeDtypeStruct(shape, dtype)


def _exchange(arrs, gather, name):
    n = len(arrs)
    out_shape = [_sds(((N_DEV,) + a.shape) if gather else a.shape, a.dtype) for a in arrs]

    def body(*refs):
        ins, outs = refs[:n], refs[n:2 * n]
        send_sems, recv_sems, local_sems = refs[2 * n:]
        x, y, c = lax.axis_index("x"), lax.axis_index("y"), lax.axis_index("c")
        me = 4 * x + 2 * y + c
        sends, recvs, locs = [], [], []
        for i in range(n):
            for k in range(N_DEV - 1):
                bits = k + 1
                px = x ^ ((bits >> 2) & 1)
                py = y ^ ((bits >> 1) & 1)
                pc = c ^ (bits & 1)
                peer = 4 * px + 2 * py + pc
                src = ins[i] if gather else ins[i].at[peer]
                sem = i * (N_DEV - 1) + k
                sends.append(pltpu.make_async_remote_copy(
                    src_ref=src, dst_ref=outs[i].at[me], send_sem=send_sems.at[sem], recv_sem=recv_sems.at[sem],
                    device_id=(px, py, pc), device_id_type=MESH))
                recvs.append(pltpu.make_async_remote_copy(
                    src_ref=src, dst_ref=outs[i].at[peer], send_sem=send_sems.at[sem], recv_sem=recv_sems.at[sem],
                    device_id=(px, py, pc), device_id_type=MESH))
            locs.append(pltpu.make_async_copy(ins[i] if gather else ins[i].at[me], outs[i].at[me], local_sems.at[i]))
        for cp in locs + sends:
            cp.start()
        for cp in recvs:
            cp.wait_recv()
        for cp in sends:
            cp.wait_send()
        for cp in locs:
            cp.wait()

    outs = pl.pallas_call(
        body, name=name, out_shape=out_shape, in_specs=[ANY] * n, out_specs=[ANY] * n,
        scratch_shapes=[pltpu.SemaphoreType.DMA((n * (N_DEV - 1),)), pltpu.SemaphoreType.DMA((n * (N_DEV - 1),)),
                        pltpu.SemaphoreType.DMA((n,))],
    )(*arrs)
    return list(outs)


HBM = pl.BlockSpec(memory_space=pltpu.HBM)
SEM = pl.BlockSpec(memory_space=pltpu.SEMAPHORE)
EFFECT = pltpu.SideEffectType.DATAFLOW_SIDE_EFFECTING


def _peer(k):
    x, y, c = lax.axis_index("x"), lax.axis_index("y"), lax.axis_index("c")
    bits = k + 1
    px, py, pc = x ^ ((bits >> 2) & 1), y ^ ((bits >> 1) & 1), c ^ (bits & 1)
    return (px, py, pc), 4 * px + 2 * py + pc, 4 * x + 2 * y + c


def _exchange_start(arrs, gather, name):
    n = len(arrs)
    lands = [pltpu.with_memory_space_constraint(lax.empty(((N_DEV,) + a.shape) if gather else a.shape, a.dtype),
                                                pltpu.HBM) for a in arrs]
    srcs = [pltpu.with_memory_space_constraint(a, pltpu.HBM) for a in arrs]

    def body(*refs):
        ins, zones = refs[:n], refs[n:2 * n]
        send_sems, recv_sems, local_sems = refs[2 * n:2 * n + 3]
        token = refs[-1]
        for i in range(n):
            for k in range(N_DEV - 1):
                dev, peer, me = _peer(k)
                sem = i * (N_DEV - 1) + k
                pltpu.make_async_remote_copy(
                    src_ref=ins[i] if gather else ins[i].at[peer], dst_ref=zones[i].at[me],
                    send_sem=send_sems.at[sem], recv_sem=recv_sems.at[sem], device_id=dev, device_id_type=MESH).start()
            _, _, me = _peer(0)
            pltpu.make_async_copy(ins[i] if gather else ins[i].at[me], zones[i].at[me], local_sems.at[i]).start()
        token[...] = jnp.zeros_like(token)

    nsem = n * (N_DEV - 1)
    outs = pl.pallas_call(
        body, name=name,
        out_shape=[pltpu.SemaphoreType.DMA((nsem,)), pltpu.SemaphoreType.DMA((nsem,)), pltpu.SemaphoreType.DMA((n,))]
        + [pltpu.HBM(a.shape, a.dtype) for a in srcs] + [pltpu.HBM(z.shape, z.dtype) for z in lands]
        + [_sds((8, LANE))],
        in_specs=[HBM] * (2 * n),
        out_specs=[SEM, SEM, SEM] + [HBM] * (2 * n) + [pl.BlockSpec(memory_space=pltpu.VMEM)],
        input_output_aliases={i: 3 + i for i in range(2 * n)},
        compiler_params=pltpu.CompilerParams(has_side_effects=EFFECT),
    )(*srcs, *lands)
    return {"n": n, "gather": gather, "sems": outs[:3], "srcs": outs[3:3 + n], "lands": outs[3 + n:3 + 2 * n],
            "token": outs[-1]}


def _exchange_wait(st, after, name):
    n, gather = st["n"], st["gather"]

    def body(*refs):
        ins, zones = refs[:n], refs[n:2 * n]
        send_sems, recv_sems, local_sems = refs[2 * n:2 * n + 3]
        for i in range(n):
            for k in range(N_DEV - 1):
                dev, peer, me = _peer(k)
                sem = i * (N_DEV - 1) + k
                src = ins[i] if gather else ins[i].at[peer]
                cp = pltpu.make_async_remote_copy(
                    src_ref=src, dst_ref=zones[i].at[peer], send_sem=send_sems.at[sem], recv_sem=recv_sems.at[sem],
                    device_id=dev, device_id_type=MESH)
                cp.wait_send()
                cp.wait_recv()
            _, _, me = _peer(0)
            pltpu.make_async_copy(ins[i] if gather else ins[i].at[me], zones[i].at[me], local_sems.at[i]).wait()

    outs = pl.pallas_call(
        body, name=name,
        out_shape=[pltpu.HBM(a.shape, a.dtype) for a in st["srcs"]] + [pltpu.HBM(z.shape, z.dtype) for z in st["lands"]],
        in_specs=[HBM] * (2 * n) + [SEM, SEM, SEM, ANY], out_specs=[HBM] * (2 * n),
        input_output_aliases={i: i for i in range(2 * n)},
        compiler_params=pltpu.CompilerParams(has_side_effects=EFFECT),
    )(*st["srcs"], *st["lands"], *st["sems"], after)
    return list(outs[n:])


def _mod_fwd(crows, w_ada, b_blk):
    def body(c_ref, w_ref, b_ref, o_ref):
        o_ref[...] = mm(silu(c_ref[...]), w_ref[...]) + b_ref[...]

    return pl.pallas_call(body, name="mod_fwd", out_shape=_sds((24, 768)), compiler_params=_cp())(crows, w_ada, b_blk)


def _mod_bwd(crows, w_ada, dmod_blk, dmodc_blk, dmod_full, dmodc_full):
    def body(c_ref, w_ref, d_ref, dc_ref, df_ref, dcf_ref, gw_ref, gc_ref, gb_ref):
        cr = c_ref[...]
        dc, dcf = dc_ref[0:1, :], dcf_ref[0:1, :]
        for p in range(1, N_DEV):
            dc = dc + dc_ref[p:p + 1, :]
            dcf = dcf + dcf_ref[p:p + 1, :]
        row = lax.broadcasted_iota(jnp.int32, (24, 1), 0)
        gw_ref[...] = mm_tn(silu(cr), jnp.where(row == 16, dc, d_ref[...]))
        cc = cr[16:17, :]
        sg = jax.nn.sigmoid(cc)
        part = mm_nt(jnp.broadcast_to(dc, (8, 768)), w_ref[...])
        gc_ref[...] = part * (sg * (1.0 + cc * (1.0 - sg)))
        gb_ref[...] = jnp.sum(df_ref[...], axis=0, keepdims=True) + dcf

    return pl.pallas_call(
        body, name="mod_bwd", out_shape=[_sds((D_MODEL, 768)), _sds((8, D_MODEL)), _sds((1, 6 * D_MODEL))],
        compiler_params=_cp())(crows, w_ada, dmod_blk, dmodc_blk, dmod_full, dmodc_full)


def _tab_specs(tk):
    return [pl.BlockSpec((tk, w), lambda i, t: (t, 0)) for w in (512, 512, 1024, 1024)]


def _k1_fwd(x, mod, g_attn, g_q, g_kv, ws, tabs, kv_all, is_ctx):
    b, l, _ = x.shape
    nt = l // TOK
    n_f32 = 2 if is_ctx else 4

    def body(x_ref, mod_ref, ga_ref, gq_ref, gk_ref, wa_ref, wb_ref, wq_ref, wk_ref, cs_ref, sn_ref, cq_ref, sq_ref,
             *rest):
        outs = rest if is_ctx else rest[2:]
        res = k1_tile(x_ref[...], mod_ref[0:1, :], mod_ref[1:2, :], ga_ref[...], gq_ref[...], gk_ref[...],
                      (wa_ref[...], wb_ref[...], wq_ref[...], wk_ref[...]), (None,) * 4,
                      (cs_ref[...], sn_ref[...], cq_ref[...], sq_ref[...]), is_ctx)
        for o_ref, r in zip(outs, res):
            o_ref[...] = r.astype(o_ref.dtype)

    tok = lambda w, off=0: pl.BlockSpec((None, TOK, w), lambda i, t: (i, t + off, 0))
    mod_spec = pl.BlockSpec((None, 8, D_MODEL), (lambda i, t: (0, 0, 0)) if is_ctx else (lambda i, t: (i, 0, 0)))
    kv_off = 0 if is_ctx else CTX_LEN // TOK
    in_specs = ([tok(D_MODEL), mod_spec, _full((1, D_MODEL)), _full((1, 384)), _full((1, 256))]
                + [_full(s) for s in W_SHAPES] + _tab_specs(TOK))
    args = [x, mod, g_attn, g_q, g_kv, *ws, *tabs]
    out_specs = [tok(512)] * n_f32 + ([] if is_ctx else [tok(1024)]) + [tok(1024, kv_off), tok(512, kv_off)]
    out_shape = ([_sds((b, l, 512))] * n_f32 + ([] if is_ctx else [_sds((b, l, 1024), BF)])
                 + [_sds((b, KV_LEN, 1024), BF), _sds((b, KV_LEN, 512), BF)])
    aliases = {}
    if not is_ctx:
        aliases = {len(args): n_f32 + 1, len(args) + 1: n_f32 + 2}
        in_specs += [ANY, ANY]
        args += list(kv_all)
    return pl.pallas_call(
        body, name="k1_fwd_ctx" if is_ctx else "k1_fwd", grid=(b, nt), in_specs=in_specs, out_specs=out_specs,
        out_shape=out_shape, input_output_aliases=aliases, compiler_params=_cp((ARB, ARB)),
    )(*args)


N_ACC = 7


def _k1_bwd(x, mod, g_attn, g_q, g_kv, ws, tabs, cts, dx_res, init, is_ctx):
    b, l, _ = x.shape
    tk = TOK_B
    nt = l // tk
    flat_cts = [a for group in cts for a in group]
    group_sizes = [len(g) for g in cts]
    n_ct = len(flat_cts)
    has_res = dx_res is not None
    has_init = init is not None
    acc_shapes = W_SHAPES + [(1, D_MODEL), (1, 384), (1, 256)]

    def body(*refs):
        it = iter(refs)
        x_ref, mod_ref, ga_ref, gq_ref, gk_ref = [next(it) for _ in range(5)]
        w_hbm = [next(it) for _ in range(4)]
        tab_refs = [next(it) for _ in range(4)]
        ct_refs = [next(it) for _ in range(n_ct)]
        res_ref = next(it) if has_res else None
        init_refs = [next(it) for _ in range(N_ACC)] if has_init else None
        gx_ref = next(it) if not is_ctx else None
        out_hbm = [next(it) for _ in range(N_ACC)]
        dmod_ref = next(it)
        w_vmem = [next(it) for _ in range(4)]
        accs = [next(it) for _ in range(N_ACC)]
        sem = next(it)
        i, t = pl.program_id(0), pl.program_id(1)
        first = jnp.logical_and(i == 0, t == 0)
        last = jnp.logical_and(i == b - 1, t == nt - 1)

        @pl.when(first)
        def _():
            for src, dst in zip(w_hbm, w_vmem):
                pltpu.sync_copy(src, dst)
            for k in range(N_ACC):
                if has_init:
                    pltpu.sync_copy(init_refs[k], accs[k])
                else:
                    accs[k][...] = jnp.zeros(acc_shapes[k], F32)

        ct_vals, pos = [], 0
        for gsz in group_sizes:
            v = ct_refs[pos][...].astype(F32)
            for r in ct_refs[pos + 1:pos + gsz]:
                v = v + r[...]
            ct_vals.append(v)
            pos += gsz
        wv = tuple(r[...] for r in w_vmem)
        tv = tuple(r[...] for r in tab_refs)

        def f(xv, sh, sc, ga, gq, gk, *probes):
            return k1_tile(xv, sh, sc, ga, gq, gk, wv, probes, tv, is_ctx)

        probes = [jnp.zeros(s, F32) for s in W_SHAPES]
        _, vjp = jax.vjp(f, x_ref[...], mod_ref[0:1, :], mod_ref[1:2, :], ga_ref[...], gq_ref[...], gk_ref[...], *probes)
        dx, dsh, dsc, dga, dgq, dgk, dwa, dwb, dwq, dwk = vjp(tuple(ct_vals))
        if not is_ctx:
            gx_ref[...] = dx + res_ref[...] if has_res else dx
        for ref, val in zip(accs, (dwa, dwb, dwq, dwk, dga, dgq, dgk)):
            ref[...] += val
        t0 = first if is_ctx else t == 0
        _acc(dmod_ref.at[0:1, :], dsh, t0)
        _acc(dmod_ref.at[1:2, :], dsc, t0)

        @pl.when(t0)
        def _():
            dmod_ref[2:8, :] = jnp.zeros((6, D_MODEL), F32)

        @pl.when(last)
        def _():
            cps = [pltpu.make_async_copy(accs[k], out_hbm[k], sem.at[k]) for k in range(N_ACC)]
            for cp in cps:
                cp.start()
            for cp in cps:
                cp.wait()

    tok = lambda w, off=0: pl.BlockSpec((None, tk, w), lambda i, t: (i, t + off, 0))
    mod_spec = pl.BlockSpec((None, 8, D_MODEL), (lambda i, t: (0, 0, 0)) if is_ctx else (lambda i, t: (i, 0, 0)))
    in_specs = ([tok(D_MODEL), mod_spec, _full((1, D_MODEL)), _full((1, 384)), _full((1, 256))] + [ANY] * 4
                + _tab_specs(tk))
    args = [x, mod, g_attn, g_q, g_kv, *ws, *tabs]
    for a, off in flat_cts:
        in_specs.append(tok(a.shape[-1], off // tk))
        args.append(a)
    if has_res:
        in_specs.append(tok(D_MODEL))
        args.append(dx_res)
    if has_init:
        in_specs += [ANY] * N_ACC
        args += list(init)
    out_shape, out_specs = [], []
    if not is_ctx:
        out_shape.append(_sds((b, l, D_MODEL)))
        out_specs.append(tok(D_MODEL))
    out_shape += [_sds(s) for s in acc_shapes] + [_sds((1 if is_ctx else b, 8, D_MODEL))]
    out_specs += [ANY] * N_ACC + [mod_spec]
    outs = pl.pallas_call(
        body, name="k1_bwd_ctx" if is_ctx else "k1_bwd", grid=(b, nt), in_specs=in_specs, out_specs=out_specs,
        out_shape=out_shape,
        scratch_shapes=[pltpu.VMEM(s, BF) for s in W_SHAPES] + [pltpu.VMEM(s, F32) for s in acc_shapes]
        + [pltpu.SemaphoreType.DMA((N_ACC,))],
        compiler_params=_cp((ARB, ARB)),
    )(*args)
    outs = list(outs)
    gx = None if is_ctx else outs.pop(0)
    return gx, outs[:N_ACC], outs[N_ACC]


def _chunk_spec(rev):
    if rev:
        return pl.BlockSpec((None, RET_CHUNK, 512), lambda i, n: (i, N_CHUNK - 1 - n, 0))
    return pl.BlockSpec((None, RET_CHUNK, 512), lambda i, n: (i, n, 0))


def _state_spec(rev):
    if rev:
        return pl.BlockSpec((None, N_HEADS, None, LANE, LANE), lambda i, n: (i, 0, N_CHUNK - 1 - n, 0, 0))
    return pl.BlockSpec((None, N_HEADS, None, LANE, LANE), lambda i, n: (i, 0, n, 0, 0))


_CTX_SPEC = pl.BlockSpec((None, CTX_LEN, 512), lambda i, n: (i, 0, 0))
_DEC_SPEC = pl.BlockSpec((N_HEADS, 1, 1), lambda i, n: (0, 0, 0))


def _k2_fwd(rq, rk, rv, rkc, rvc, dec_f, dec_b):
    b = rq.shape[0]

    def body(qf, kf, vf, qb, kb, vb, kc, vc, df, db, of_ref, ob_ref, sf_out, sb_out, sf, sb):
        n = pl.program_id(1)
        for h, sl in enumerate(_HEAD_SL):
            lgf, lgb = log_sigmoid(df[h]), log_sigmoid(db[h])

            @pl.when(n == 0)
            def _():
                sf[h] = ctx_state(kc[:, sl], vc[:, sl], lgf, False)
                sb[h] = ctx_state(kc[:, sl], vc[:, sl], lgb, True)

            sf_out[h] = sf[h]
            sb_out[h] = sb[h]
            o, s = ret_chunk(qf[:, sl], kf[:, sl], vf[:, sl], sf[h], lgf, False)
            of_ref[:, sl] = o
            sf[h] = s
            o, s = ret_chunk(qb[:, sl], kb[:, sl], vb[:, sl], sb[h], lgb, True)
            ob_ref[:, sl] = o
            sb[h] = s

    l = rq.shape[1]
    return pl.pallas_call(
        body, name="k2_fwd", grid=(b, N_CHUNK),
        in_specs=[_chunk_spec(False)] * 3 + [_chunk_spec(True)] * 3 + [_CTX_SPEC, _CTX_SPEC, _DEC_SPEC, _DEC_SPEC],
        out_specs=[_chunk_spec(False), _chunk_spec(True), _state_spec(False), _state_spec(True)],
        out_shape=[_sds((b, l, 512)), _sds((b, l, 512)), _sds((b, N_HEADS, N_CHUNK, LANE, LANE)),
                   _sds((b, N_HEADS, N_CHUNK, LANE, LANE))],
        scratch_shapes=[pltpu.VMEM((N_HEADS, LANE, LANE), F32), pltpu.VMEM((N_HEADS, LANE, LANE), F32)],
        compiler_params=_cp((ARB, ARB)),
    )(rq, rk, rv, rq, rk, rv, rkc, rvc, dec_f, dec_b)


def _k2_bwd(rq, rk, rv, do, sf_prev, sb_prev, rkc, rvc, dec_f, dec_b):
    b, l, _ = rq.shape

    def body(qf, kf, vf, gf, spf, qb, kb, vb, gb, spb, kc, vc, df, db,
             dqf, dkf, dvf, dqb, dkb, dvb, dkc, dvc, ddf, ddb, dsf, dsb):
        n = pl.program_id(1)

        @pl.when(n == 0)
        def _():
            dsf[...] = jnp.zeros((N_HEADS, LANE, LANE), F32)
            dsb[...] = jnp.zeros((N_HEADS, LANE, LANE), F32)

        def one(h, sl, q, k, v, g, sp, dec, ds, dq, dk, dv, dd, rev):
            def f(qv, kv_, vv, sv, dcy):
                return ret_chunk(qv, kv_, vv, sv, log_sigmoid(dcy), rev)

            _, vjp = jax.vjp(f, q[:, sl], k[:, sl], v[:, sl], sp[h], dec[h])
            gq, gk, gv, gs, gd = vjp((g[:, sl], ds[h]))
            dq[:, sl] = gq
            dk[:, sl] = gk
            dv[:, sl] = gv
            ds[h] = gs
            _acc(dd.at[h], jnp.broadcast_to(gd, (8, LANE)), n == 0)

        for h, sl in enumerate(_HEAD_SL):
            one(h, sl, qf, kf, vf, gf, spf, df, dsf, dqf, dkf, dvf, ddf, False)
            one(h, sl, qb, kb, vb, gb, spb, db, dsb, dqb, dkb, dvb, ddb, True)

        @pl.when(n == N_CHUNK - 1)
        def _():
            def f(kcv, vcv, dcy, rev):
                return ctx_state(kcv, vcv, log_sigmoid(dcy), rev)

            for h, sl in enumerate(_HEAD_SL):
                _, vjp_f = jax.vjp(functools.partial(f, rev=False), kc[:, sl], vc[:, sl], df[h])
                gk_f, gv_f, gd_f = vjp_f(dsf[h])
                _, vjp_b = jax.vjp(functools.partial(f, rev=True), kc[:, sl], vc[:, sl], db[h])
                gk_b, gv_b, gd_b = vjp_b(dsb[h])
                dkc[:, sl] = gk_f + gk_b
                dvc[:, sl] = gv_f + gv_b
                ddf[h] += jnp.broadcast_to(gd_f, (8, LANE))
                ddb[h] += jnp.broadcast_to(gd_b, (8, LANE))

    dd_spec = pl.BlockSpec((None, N_HEADS, 8, LANE), lambda i, n: (i, 0, 0, 0))
    return pl.pallas_call(
        body, name="k2_bwd", grid=(b, N_CHUNK),
        in_specs=[_chunk_spec(True)] * 4 + [_state_spec(True)] + [_chunk_spec(False)] * 4 + [_state_spec(False)]
        + [_CTX_SPEC, _CTX_SPEC, _DEC_SPEC, _DEC_SPEC],
        out_specs=[_chunk_spec(True)] * 3 + [_chunk_spec(False)] * 3 + [_CTX_SPEC, _CTX_SPEC, dd_spec, dd_spec],
        out_shape=[_sds((b, l, 512))] * 6 + [_sds((b, CTX_LEN, 512))] * 2 + [_sds((b, N_HEADS, 8, LANE))] * 2,
        scratch_shapes=[pltpu.VMEM((N_HEADS, LANE, LANE), F32), pltpu.VMEM((N_HEADS, LANE, LANE), F32)],
        compiler_params=_cp((ARB, ARB)),
    )(rq, rk, rv, do, sf_prev, rq, rk, rv, do, sb_prev, rkc, rvc, dec_f, dec_b)


TQ = 512
QK_W = 2 * LANE


def _softmax_parts(q, k):
    s = _dot(q, k, 1, 1) * (1.0 / math.sqrt(192.0))
    e = jnp.exp(s - jnp.max(s, axis=-1, keepdims=True))
    return e, 1.0 / jnp.sum(e, axis=-1, keepdims=True)


def _k3_specs():
    qs = lambda w: pl.BlockSpec((None, TQ, w), lambda i, h, t: (i, t, h))
    ks = lambda w: pl.BlockSpec((None, KV_LEN, w), lambda i, h, t: (i, 0, h))
    return qs, ks


def _k3_fwd(q, k, v):
    b, l, _ = q.shape

    def body(q_ref, k_ref, v_ref, o_ref):
        e, inv = _softmax_parts(q_ref[...], k_ref[...])
        o_ref[...] = _dot(e, v_ref[...], 1, 0) * inv

    qs, ks = _k3_specs()
    return pl.pallas_call(
        body, name="k3_fwd", grid=(b, N_HEADS, l // TQ), in_specs=[qs(QK_W), ks(QK_W), ks(LANE)], out_specs=qs(LANE),
        out_shape=_sds((b, l, N_HEADS * LANE)), compiler_params=_cp((ARB, ARB, ARB)),
    )(q, k, v)


def _k3_bwd(q, k, v, dy):
    b, l, _ = q.shape
    scale = 1.0 / math.sqrt(192.0)

    def body(q_ref, k_ref, v_ref, dy_ref, dq_ref, dk_ref, dv_ref):
        t0 = pl.program_id(2) == 0
        qv, kv_ = q_ref[...], k_ref[...]
        g = dy_ref[...].astype(BF)
        e, inv = _softmax_parts(qv, kv_)
        p = e * inv
        dp = _dot(g, v_ref[...], 1, 1)
        ds = (p * (dp - jnp.sum(dp * p, axis=-1, keepdims=True)) * scale).astype(BF)
        _acc(dv_ref, _dot(p, g, 0, 0), t0)
        dq_ref[...] = _dot(ds, kv_, 1, 0)
        _acc(dk_ref, _dot(ds, qv, 0, 0), t0)

    qs, ks = _k3_specs()
    return pl.pallas_call(
        body, name="k3_bwd", grid=(b, N_HEADS, l // TQ), in_specs=[qs(QK_W), ks(QK_W), ks(LANE), qs(LANE)],
        out_specs=[qs(QK_W), ks(QK_W), ks(LANE)],
        out_shape=[_sds((b, l, N_HEADS * QK_W)), _sds((b, KV_LEN, N_HEADS * QK_W)), _sds((b, KV_LEN, N_HEADS * LANE))],
        compiler_params=_cp((ARB, ARB, ARB)),
    )(q, k, v, dy)


def _mod_rows(mod_ref, rows):
    return [mod_ref[r:r + 1, :] for r in rows]


def _k4a_fwd(x, o_f, o_b, rg, y_mla, g_ret, w_out, mod, g_ffn):
    b, l, _ = x.shape

    def body(x_ref, of_ref, ob_ref, rg_ref, ym_ref, gr_ref, wo_ref, mod_ref, gf_ref, xm_ref, h2_ref):
        gt_a, sh_f, sc_f = _mod_rows(mod_ref, (2, 3, 4))
        x_mid, h2 = k4a_tile(x_ref[...], of_ref[...], ob_ref[...], rg_ref[...], ym_ref[...], gr_ref[...], gt_a,
                             gf_ref[...], sh_f, sc_f, wo_ref[...], None)
        xm_ref[...] = x_mid
        h2_ref[...] = h2.astype(BF)

    tok = lambda w: pl.BlockSpec((None, TOK, w), lambda i, t: (i, t, 0))
    mod_spec = pl.BlockSpec((None, 8, D_MODEL), lambda i, t: (i, 0, 0))
    return pl.pallas_call(
        body, name="k4a_fwd", grid=(b, l // TOK),
        in_specs=[tok(D_MODEL), tok(512), tok(512), tok(512), tok(512), _full((1, 512)), _full((D_MODEL, D_MODEL)),
                  mod_spec, _full((1, D_MODEL))],
        out_specs=[tok(D_MODEL), tok(D_MODEL)], out_shape=[_sds((b, l, D_MODEL)), _sds((b, l, D_MODEL), BF)],
        compiler_params=_cp((ARB, ARB)),
    )(x, o_f, o_b, rg, y_mla, g_ret, w_out, mod, g_ffn)


TOK_M = 512


def _last_j(j, idx):
    return jnp.where(j == N_DEV - 1, idx, 0)


def _k4b_mlp_loss(h2, w1, w2, x_mid, mod, g_final, tgt):
    b, l, _ = h2.shape
    nt = l // TOK_M

    def body(h2_ref, w1_ref, w2_ref, xm_ref, mod_ref, gfin_ref, tgt_ref, dxm_ref, dmlp_ref, loss_ref, dgt_ref,
             dgfin_ref, acc):
        j, i, t = pl.program_id(0), pl.program_id(1), pl.program_id(2)
        rows = pl.ds(pl.multiple_of((i * nt + t) * TOK_M, TOK_M), TOK_M)
        a = _dot(h2_ref[...], w1_ref[...], 1, 0)
        part = _dot(jnp.square(jnp.maximum(a, 0.0)), w2_ref[...], 1, 0)
        _acc(acc.at[rows, :], part, j == 0)

        @pl.when(j == N_DEV - 1)
        def _():
            (gt_f,) = _mod_rows(mod_ref, (5,))
            loss, vjp = jax.vjp(k4c_tile, xm_ref[...], acc[rows, :], gt_f, gfin_ref[...], tgt_ref[...])
            dxm, dmlp, dgt, dgfin, _ = vjp(jnp.ones((1, 1), F32))
            dxm_ref[...] = dxm
            dmlp_ref[...] = dmlp.astype(BF)
            first = jnp.logical_and(i == 0, t == 0)
            _acc(loss_ref, jnp.broadcast_to(loss, (8, LANE)), first)
            _acc(dgfin_ref, dgfin, first)
            _acc(dgt_ref, dgt, t == 0)

    tok = lambda w: pl.BlockSpec((None, TOK_M, w), lambda j, i, t: (i, t, 0))
    tok_last = lambda w: pl.BlockSpec((None, TOK_M, w), lambda j, i, t: (_last_j(j, i), _last_j(j, t), 0))
    return pl.pallas_call(
        body, name="k4b_mlp_loss", grid=(N_DEV, b, nt),
        in_specs=[tok(D_MODEL), pl.BlockSpec((None, D_MODEL, FF_BLK), lambda j, i, t: (j, 0, 0)),
                  pl.BlockSpec((None, FF_BLK, D_MODEL), lambda j, i, t: (j, 0, 0)), tok_last(D_MODEL),
                  pl.BlockSpec((None, 8, D_MODEL), lambda j, i, t: (i, 0, 0)),
                  pl.BlockSpec((1, D_MODEL), lambda j, i, t: (0, 0)), tok_last(D_MODEL)],
        out_specs=[tok_last(D_MODEL), tok_last(D_MODEL), pl.BlockSpec((8, LANE), lambda j, i, t: (0, 0)),
                   pl.BlockSpec((None, 1, D_MODEL), lambda j, i, t: (_last_j(j, i), 0, 0)),
                   pl.BlockSpec((1, D_MODEL), lambda j, i, t: (0, 0))],
        out_shape=[_sds((b, l, D_MODEL)), _sds((b, l, D_MODEL), BF), _sds((8, LANE)), _sds((b, 1, D_MODEL)),
                   _sds((1, D_MODEL))],
        scratch_shapes=[pltpu.VMEM((b * l, D_MODEL), F32)],
        compiler_params=_cp((ARB, ARB, ARB)),
    )(h2, w1, w2, x_mid, mod, g_final, tgt)


def _k4d_mlp_bwd(h2, dmlp, w1, w2):
    b, l, _ = h2.shape
    nt = l // TOK_M

    def body(h2_ref, dm_ref, w1_ref, w2_ref, dh2_ref, dw1_ref, dw2_ref, acc1, acc2, dh2s):
        j, i, t = pl.program_id(0), pl.program_id(1), pl.program_id(2)
        first = jnp.logical_and(i == 0, t == 0)
        rows = pl.ds(pl.multiple_of((i * nt + t) * TOK_M, TOK_M), TOK_M)
        h2v, dm = h2_ref[...], dm_ref[...]
        r = jnp.maximum(_dot(h2v, w1_ref[...], 1, 0), 0.0)
        da = _dot(dm, w2_ref[...], 1, 1) * (2.0 * r)
        _acc(acc2, _dot(jnp.square(r), dm, 0, 0), first)
        _acc(acc1, _dot(h2v, da, 0, 0), first)
        _acc(dh2s.at[rows, :], _dot(da, w1_ref[...], 1, 1), j == 0)

        @pl.when(jnp.logical_and(i == b - 1, t == nt - 1))
        def _():
            dw1_ref[...] = acc1[...].astype(BF)
            dw2_ref[...] = acc2[...].astype(BF)

        @pl.when(j == N_DEV - 1)
        def _():
            dh2_ref[...] = dh2s[rows, :]

    tok = lambda w: pl.BlockSpec((None, TOK_M, w), lambda j, i, t: (i, t, 0))
    return pl.pallas_call(
        body, name="k4d_mlp_bwd", grid=(N_DEV, b, nt),
        in_specs=[tok(D_MODEL), tok(D_MODEL), pl.BlockSpec((None, D_MODEL, FF_BLK), lambda j, i, t: (j, 0, 0)),
                  pl.BlockSpec((None, FF_BLK, D_MODEL), lambda j, i, t: (j, 0, 0))],
        out_specs=[pl.BlockSpec((None, TOK_M, D_MODEL), lambda j, i, t: (_last_j(j, i), _last_j(j, t), 0)),
                   pl.BlockSpec((None, D_MODEL, FF_BLK), lambda j, i, t: (j, 0, 0)),
                   pl.BlockSpec((None, FF_BLK, D_MODEL), lambda j, i, t: (j, 0, 0))],
        out_shape=[_sds((b, l, D_MODEL)), _sds((N_DEV, D_MODEL, FF_BLK), BF), _sds((N_DEV, FF_BLK, D_MODEL), BF)],
        scratch_shapes=[pltpu.VMEM((D_MODEL, FF_BLK), F32), pltpu.VMEM((FF_BLK, D_MODEL), F32),
                        pltpu.VMEM((b * l, D_MODEL), F32)],
        compiler_params=_cp((ARB, ARB, ARB)),
    )(h2, dmlp, w1, w2)


def _k4e_bwd(x, o_f, o_b, rg, y_mla, g_ret, w_out, mod, g_ffn, dxm, dh2):
    b, l, _ = x.shape

    def body(x_ref, of_ref, ob_ref, rg_ref, ym_ref, gr_ref, wo_ref, mod_ref, gf_ref, dxm_ref, dh2_ref,
             dx_ref, do_ref, drg_ref, dym_ref, dwo_ref, dgr_ref, dgf_ref, dmod_ref):
        i, t = pl.program_id(0), pl.program_id(1)
        first = jnp.logical_and(i == 0, t == 0)
        gt_a, sh_f, sc_f = _mod_rows(mod_ref, (2, 3, 4))
        wo = wo_ref[...]

        def f(xv, ofv, rgv, ymv, grv, gta, gfv, shf, scf, p_out):
            return k4a_tile(xv, ofv, ob_ref[...], rgv, ymv, grv, gta, gfv, shf, scf, wo, p_out)

        _, vjp = jax.vjp(f, x_ref[...], of_ref[...], rg_ref[...], ym_ref[...], gr_ref[...], gt_a, gf_ref[...], sh_f,
                         sc_f, jnp.zeros((D_MODEL, D_MODEL), F32))
        dx, do, drg, dym, dgr, dgta, dgf, dshf, dscf, dwo = vjp((dxm_ref[...], dh2_ref[...]))
        dx_ref[...] = dx
        do_ref[...] = do
        drg_ref[...] = drg
        dym_ref[...] = dym
        _acc(dwo_ref, dwo, first)
        _acc(dgr_ref, dgr, first)
        _acc(dgf_ref, dgf, first)
        t0 = t == 0
        _acc(dmod_ref.at[2:3, :], dgta, t0)
        _acc(dmod_ref.at[3:4, :], dshf, t0)
        _acc(dmod_ref.at[4:5, :], dscf, t0)

        @pl.when(t0)
        def _():
            dmod_ref[0:2, :] = jnp.zeros((2, D_MODEL), F32)
            dmod_ref[5:8, :] = jnp.zeros((3, D_MODEL), F32)

    tok = lambda w: pl.BlockSpec((None, TOK_B, w), lambda i, t: (i, t, 0))
    mod_spec = pl.BlockSpec((None, 8, D_MODEL), lambda i, t: (i, 0, 0))
    return pl.pallas_call(
        body, name="k4e_bwd", grid=(b, l // TOK_B),
        in_specs=[tok(D_MODEL), tok(512), tok(512), tok(512), tok(512), _full((1, 512)), _full((D_MODEL, D_MODEL)),
                  mod_spec, _full((1, D_MODEL)), tok(D_MODEL), tok(D_MODEL)],
        out_specs=[tok(D_MODEL), tok(512), tok(512), tok(512), _full((D_MODEL, D_MODEL)), _full((1, 512)),
                   _full((1, D_MODEL)), mod_spec],
        out_shape=[_sds((b, l, D_MODEL)), _sds((b, l, 512)), _sds((b, l, 512)), _sds((b, l, 512)),
                   _sds((D_MODEL, D_MODEL)), _sds((1, 512)), _sds((1, D_MODEL)), _sds((b, 8, D_MODEL))],
        compiler_params=_cp((ARB, ARB)),
    )(x, o_f, o_b, rg, y_mla, g_ret, w_out, mod, g_ffn, dxm, dh2)


def _adamw(w, m, v, pieces, name):
    r, c = w.shape
    npc = pieces.shape[0]
    rb = r
    for cand in (256, 128, 64, 32, 16, 8):
        if r > cand and r % cand == 0 and cand * c * 4 * (npc + 7) * 2 <= 24 * 1024 * 1024:
            rb = cand
            break

    def body(w_ref, m_ref, v_ref, p_ref, g_ref, d_ref, nm_ref, nv_ref):
        g = p_ref[0].astype(F32)
        for k in range(1, npc):
            g = g + p_ref[k].astype(F32)
        wv = w_ref[...]
        mn = ADAM_B1 * m_ref[...] + (1.0 - ADAM_B1) * g
        vn = ADAM_B2 * v_ref[...] + (1.0 - ADAM_B2) * jnp.square(g)
        m_hat = mn / (1.0 - ADAM_B1 ** ADAM_STEP)
        v_hat = vn / (1.0 - ADAM_B2 ** ADAM_STEP)
        g_ref[...] = g
        d_ref[...] = -ADAM_LR * (m_hat / (jnp.sqrt(v_hat) + ADAM_EPS) + ADAM_WD * wv)
        nm_ref[...] = mn
        nv_ref[...] = vn

    blk = pl.BlockSpec((rb, c), lambda i: (i, 0))
    return pl.pallas_call(
        body, name=name, grid=(r // rb,), in_specs=[blk, blk, blk, pl.BlockSpec((npc, rb, c), lambda i: (0, i, 0))],
        out_specs=[blk] * 4, out_shape=[_sds((r, c))] * 4, compiler_params=_cp((ARB,)),
    )(w, m, v, pieces)


def _pad_heads(w, d):
    k = w.shape[0]
    return jnp.pad(w.reshape(k, N_HEADS, d), ((0, 0), (0, 0), (0, LANE - d))).reshape(k, N_HEADS * LANE)


def _cut_heads(w, d):
    k = w.shape[0]
    return w.reshape(k, N_HEADS, LANE)[:, :, :d].reshape(k, N_HEADS * d)


def _w_in_pad(w):
    w_a = jnp.concatenate([_pad_heads(w[:, 0:256], 64), _pad_heads(w[:, 256:512], 64), w[:, 512:1536]], axis=1)
    w_b = jnp.concatenate([w[:, 1536:2176], jnp.pad(w[:, 2176:2240], ((0, 0), (0, 64)))], axis=1)
    return w_a, w_b


def _w_in_cut(g_a, g_b):
    return jnp.concatenate([_cut_heads(g_a[:, 0:512], 64), _cut_heads(g_a[:, 512:1024], 64), g_a[:, 1024:2048],
                            g_b[:, 0:704]], axis=1)


def _w_uq_pad(w):
    return jnp.pad(w.reshape(384, N_HEADS, 192), ((0, 0), (0, 0), (0, 64))).reshape(384, 1024)


def _w_uq_cut(g):
    return g.reshape(384, N_HEADS, 256)[:, :, :192].reshape(384, 768)


def _w_ukv_perm(w):
    w = w.reshape(256, N_HEADS, 256)
    return jnp.concatenate([w[:, :, :128].reshape(256, 512), w[:, :, 128:].reshape(256, 512)], axis=1)


def _w_ukv_unperm(g):
    return jnp.concatenate([g[:, :512].reshape(256, N_HEADS, 128), g[:, 512:].reshape(256, N_HEADS, 128)],
                           axis=2).reshape(256, 1024)


def _unshard_cols(g):
    return jnp.transpose(g, (1, 0, 2)).reshape(g.shape[1], N_DEV * g.shape[2])


def _shard_cols(w):
    k, n = w.shape
    return jnp.transpose(w.reshape(k, N_DEV, n // N_DEV), (1, 0, 2))


def _rope_tables():
    rows = SEQ // GRID_W
    row = jnp.repeat(jnp.arange(rows, dtype=F32), GRID_W)
    col = jnp.tile(jnp.arange(GRID_W, dtype=F32), rows)
    freq = ROPE_BASE ** (-jnp.arange(16, dtype=F32) / 16)
    ang = jnp.concatenate([row[:, None] * freq, col[:, None] * freq], axis=-1)
    cos, sin = jnp.cos(ang), jnp.sin(ang)
    z = jnp.zeros((SEQ, 64), F32)
    cs = jnp.concatenate([cos, cos, z], axis=1)
    sn = jnp.concatenate([-sin, sin, z], axis=1)
    one, zero = jnp.ones((SEQ, LANE), F32), jnp.zeros((SEQ, LANE), F32)
    cq = jnp.concatenate([one, cs], axis=1)
    sq = jnp.concatenate([zero, sn], axis=1)
    return [jnp.tile(t, (1, N_HEADS)) for t in (cs, sn, cq, sq)]


_PACKED = (("g_attn", 1024), ("g_ffn", 1024), ("ret_decay_fwd", 4), ("ret_decay_bwd", 4), ("g_ret", 512),
           ("g_q_lora", 384), ("g_kv_lora", 256), ("g_final", 1024))
_PACK_OFF = {}
_off = 0
for _name, _n in _PACKED:
    _PACK_OFF[_name] = _off
    _off += -(-_n // LANE) * LANE
PACK_W = _off


def _pack_small(vals):
    parts = []
    for name, n in _PACKED:
        a = vals[name].reshape(-1).astype(F32)
        parts.append(jnp.pad(a, (0, -(-n // LANE) * LANE - n)))
    return jnp.concatenate(parts).reshape(1, PACK_W)


def _adamw_small(params, packed, gcc, gb_ada):
    names = list(params)
    n_p = len(names)

    def body(*refs):
        p_ref, gcc_ref, gb_ref = refs[3 * n_p:3 * n_p + 3]
        outs = refs[3 * n_p + 3:]
        for k, name in enumerate(names):
            w_ref, m_ref, v_ref = refs[3 * k:3 * k + 3]
            n = w_ref.shape[1]
            if name == "b_ada":
                g = gb_ref[...]
            elif name == "c_ctx":
                g = gcc_ref[0, 0:1, :]
                for d in range(1, N_DEV):
                    g = g + gcc_ref[d, 0:1, :]
            else:
                off = _PACK_OFF[name]
                g = p_ref[0, :, off:off + n]
                for d in range(1, N_DEV):
                    g = g + p_ref[d, :, off:off + n]
            mn = ADAM_B1 * m_ref[...] + (1.0 - ADAM_B1) * g
            vn = ADAM_B2 * v_ref[...] + (1.0 - ADAM_B2) * jnp.square(g)
            m_hat = mn / (1.0 - ADAM_B1 ** ADAM_STEP)
            v_hat = vn / (1.0 - ADAM_B2 ** ADAM_STEP)
            outs[4 * k][...] = g
            outs[4 * k + 1][...] = -ADAM_LR * (m_hat / (jnp.sqrt(v_hat) + ADAM_EPS) + ADAM_WD * w_ref[...])
            outs[4 * k + 2][...] = mn
            outs[4 * k + 3][...] = vn

    args = [a for name in names for a in params[name]] + [packed, gcc, gb_ada]
    out_shape = [_sds(params[name][0].shape) for name in names for _ in range(4)]
    outs = pl.pallas_call(body, name="adamw_small", out_shape=out_shape, compiler_params=_cp())(*args)
    return {name: list(outs[4 * k:4 * k + 4]) for k, name in enumerate(names)}


def kernel(x, c, ctx, c_ctx, w_ada, b_ada, g_attn, g_ffn, w_in, ret_decay_fwd, ret_decay_bwd, g_ret, g_q_lora, w_uq, g_kv_lora, w_ukv, w_out, w_ff1, w_ff2, g_final, loss_target, m_c_ctx, m_w_ada, m_b_ada, m_g_attn, m_g_ffn, m_w_in, m_ret_decay_fwd, m_ret_decay_bwd, m_g_ret, m_g_q_lora, m_w_uq, m_g_kv_lora, m_w_ukv, m_w_out, m_w_ff1, m_w_ff2, m_g_final, v_c_ctx, v_w_ada, v_b_ada, v_g_attn, v_g_ffn, v_w_in, v_ret_decay_fwd, v_ret_decay_bwd, v_g_ret, v_g_q_lora, v_w_uq, v_g_kv_lora, v_w_ukv, v_w_out, v_w_ff1, v_w_ff2, v_g_final):
    me = 4 * lax.axis_index("x") + 2 * lax.axis_index("y") + lax.axis_index("c")
    nb = x.shape[0]

    c_pad = jnp.pad(c, ((0, 8 - nb), (0, 0)))
    c_all, g_in, g_uq, g_ukv = _exchange([c_pad, w_in[0].astype(BF), w_uq[0].astype(BF), w_ukv[0].astype(BF)], True,
                                         "gather_weights")
    ws = (*_w_in_pad(_unshard_cols(g_in)), _w_uq_pad(_unshard_cols(g_uq)), _w_ukv_perm(_unshard_cols(g_ukv)))

    crows = jnp.concatenate([c_all[:, :nb].reshape(N_DEV * nb, D_MODEL), c_ctx[None], jnp.zeros((7, D_MODEL), F32)])
    b_blk = lax.dynamic_slice(b_ada, (0, me * 768), (1, 768))
    (mod_g,) = _exchange([_mod_fwd(crows, w_ada[0], b_blk)], True, "gather_mod")
    mod_all = _unshard_cols(mod_g)
    behind = mod_g[0, 0, 0:1] * 0.0
    st_g = _exchange_start([(w_out[0] + behind).astype(BF), w_ff1[0].astype(BF), w_ff2[0].astype(BF)], True,
                           "gather_ff_start")
    mod_all = mod_all + st_g["token"][0:1, 0:1]
    mod_mine = lax.dynamic_slice(mod_all, (me * nb, 0), (nb, 6 * D_MODEL)).reshape(nb, 6, D_MODEL)
    mod = jnp.pad(mod_mine, ((0, 0), (0, 2), (0, 0)))
    mod_c = jnp.pad(mod_all[16].reshape(1, 6, D_MODEL), ((0, 0), (0, 2), (0, 0)))

    tabs = _rope_tables()
    dec_f = ret_decay_fwd.reshape(N_HEADS, 1, 1)
    dec_b = ret_decay_bwd.reshape(N_HEADS, 1, 1)

    rkc, rvc, k_ctx, v_ctx = _k1_fwd(ctx, mod_c, g_attn, g_q_lora, g_kv_lora, ws, tabs, None, True)
    rq, rk, rv, rg, q, k_all, v_all = _k1_fwd(x, mod, g_attn, g_q_lora, g_kv_lora, ws, tabs, (k_ctx, v_ctx), False)
    o_f, o_b, sf_prev, sb_prev = _k2_fwd(rq, rk, rv, rkc, rvc, dec_f, dec_b)
    y_mla = _k3_fwd(q, k_all, v_all)
    g_out, g_ff1, g_ff2 = _exchange_wait(st_g, y_mla, "gather_ff_wait")
    wo = g_out.reshape(D_MODEL, D_MODEL)
    x_mid, h2 = _k4a_fwd(x, o_f, o_b, rg, y_mla, g_ret, wo, mod, g_ffn)
    dxm, dmlp, loss_acc, dgt_f, dg_final = _k4b_mlp_loss(h2, g_ff1, g_ff2, x_mid, mod, g_final.reshape(1, D_MODEL),
                                                         loss_target)

    dh2, dw1, dw2 = _k4d_mlp_bwd(h2, dmlp, g_ff1, g_ff2)
    st_s = _exchange_start([dw1, dw2], False, "scatter_ff_start")
    g_ret_t = g_ret + st_s["token"][0:1, 0:1]
    dx_res, do, drg, dym, dwo, dg_ret, dg_ffn, dmod_a = _k4e_bwd(x, o_f, o_b, rg, y_mla, g_ret_t, wo, mod, g_ffn, dxm, dh2)
    dq, dk_all, dv_all = _k3_bwd(q, k_all, v_all, dym)
    dqf, dkf, dvf, dqb, dkb, dvb, dkc, dvc, ddf, ddb = _k2_bwd(rq, rk, rv, do, sf_prev, sb_prev, rkc, rvc, dec_f, dec_b)
    cts = [[(dqf, 0), (dqb, 0)], [(dkf, 0), (dkb, 0)], [(dvf, 0), (dvb, 0)], [(drg, 0)], [(dq, 0)],
           [(dk_all, CTX_LEN)], [(dv_all, CTX_LEN)]]
    grad_x, accs, dmod_1 = _k1_bwd(x, mod, g_attn, g_q_lora, g_kv_lora, ws, tabs, cts, dx_res, None, False)
    cts_c = [[(dkc, 0)], [(dvc, 0)], [(dk_all, 0)], [(dv_all, 0)]]
    _, accs, dmod_c1 = _k1_bwd(ctx, mod_c, g_attn, g_q_lora, g_kv_lora, ws, tabs, cts_c, None, accs, True)
    dwa, dwb, dwq, dwk, dg_attn, dg_q, dg_kv = accs

    dmod_loc = (dmod_a + dmod_1).at[:, 5, :].set(dgt_f[:, 0, :])[:, :6, :].reshape(nb, 6 * D_MODEL)
    dmod_ctx = dmod_c1[:, :6, :].reshape(1, 6 * D_MODEL)
    small = {"g_attn": dg_attn, "g_ffn": dg_ffn, "ret_decay_fwd": jnp.sum(ddf[:, :, 0, 0], axis=0),
             "ret_decay_bwd": jnp.sum(ddb[:, :, 0, 0], axis=0), "g_ret": dg_ret, "g_q_lora": dg_q, "g_kv_lora": dg_kv,
             "g_final": dg_final}
    extra = jnp.concatenate([dmod_loc, dmod_ctx, jnp.zeros((5, 6 * D_MODEL), F32)])
    sm_g, ex_g, loss_g = _exchange([_pack_small(small), extra, loss_acc], True, "gather_small")
    dmod_all = ex_g[:, :nb].reshape(N_DEV * nb, 6 * D_MODEL)
    dmodc_parts = ex_g[:, nb]
    dmod_full = jnp.concatenate([dmod_all, jnp.zeros((8, 6 * D_MODEL), F32)])
    dmod_blk = lax.dynamic_slice(dmod_full, (0, me * 768), (24, 768))
    dmodc_blk = lax.dynamic_slice(dmodc_parts, (0, me * 768), (N_DEV, 768))
    gw_ada, gcc_part, gb_ada = _mod_bwd(crows, w_ada[0], dmod_blk, dmodc_blk, dmod_full, dmodc_parts)
    (gcc_g,) = _exchange([gcc_part], True, "gather_c_ctx")

    p_ff1, p_ff2 = _exchange_wait(st_s, dmod_c1, "scatter_ff_wait")
    pieces = _exchange([_shard_cols(_w_in_cut(dwa, dwb)).astype(BF), _shard_cols(_w_uq_cut(dwq)).astype(BF),
                        _shard_cols(_w_ukv_unperm(dwk)).astype(BF), dwo.reshape(N_DEV, 128, D_MODEL).astype(BF)],
                       False, "scatter_grads") + [p_ff1, p_ff2]

    res = {}
    big = (("w_in", w_in, m_w_in, v_w_in, pieces[0]), ("w_uq", w_uq, m_w_uq, v_w_uq, pieces[1]),
           ("w_ukv", w_ukv, m_w_ukv, v_w_ukv, pieces[2]), ("w_out", w_out, m_w_out, v_w_out, pieces[3]),
           ("w_ff1", w_ff1, m_w_ff1, v_w_ff1, pieces[4]), ("w_ff2", w_ff2, m_w_ff2, v_w_ff2, pieces[5]),
           ("w_ada", w_ada, m_w_ada, v_w_ada, gw_ada[None]))
    for name, w, m, v, pcs in big:
        res[name] = [a[None] for a in _adamw(w[0], m[0], v[0], pcs, "adamw_" + name)]

    smalls = {"c_ctx": (c_ctx, m_c_ctx, v_c_ctx), "b_ada": (b_ada, m_b_ada, v_b_ada), "g_attn": (g_attn, m_g_attn, v_g_attn),
              "g_ffn": (g_ffn, m_g_ffn, v_g_ffn), "ret_decay_fwd": (ret_decay_fwd, m_ret_decay_fwd, v_ret_decay_fwd),
              "ret_decay_bwd": (ret_decay_bwd, m_ret_decay_bwd, v_ret_decay_bwd), "g_ret": (g_ret, m_g_ret, v_g_ret),
              "g_q_lora": (g_q_lora, m_g_q_lora, v_g_q_lora), "g_kv_lora": (g_kv_lora, m_g_kv_lora, v_g_kv_lora),
              "g_final": (g_final, m_g_final, v_g_final)}
    rows = {k: tuple(a.reshape(1, -1) for a in t) for k, t in smalls.items()}
    for name, outs in _adamw_small(rows, sm_g, gcc_g, gb_ada).items():
        res[name] = [o.reshape(smalls[name][0].shape) for o in outs]

    loss = loss_g[0, 0, 0]
    for k in range(1, N_DEV):
        loss = loss + loss_g[k, 0, 0]

    order = ("c_ctx", "w_ada", "b_ada", "g_attn", "g_ffn", "w_in", "ret_decay_fwd", "ret_decay_bwd", "g_ret", "g_q_lora",
             "w_uq", "g_kv_lora", "w_ukv", "w_out", "w_ff1", "w_ff2", "g_final")
    return (loss, grad_x, *[res[n][0] for n in order], *[res[n][1] for n in order], *[res[n][2] for n in order],
            *[res[n][3] for n in order])
```

```python
import functools
import math

import jax
import jax.numpy as jnp
from jax import lax
from jax.experimental import pallas as pl
from jax.experimental.pallas import tpu as pltpu

F32 = jnp.float32
BF = jnp.bfloat16
EPS = 1e-6
LANE = 128
N_DEV = 8
D_MODEL = 1024
SEQ = 2048
CTX_LEN = 256
GRID_W = 64
N_HEADS = 4
RET_CHUNK = 512
N_CHUNK = SEQ // RET_CHUNK
D_FF = 4096
FF_BLK = D_FF // N_DEV
IN_PAD = 2816
KV_LEN = CTX_LEN + SEQ
ROPE_BASE = 10000.0
ADAM_LR, ADAM_B1, ADAM_B2, ADAM_EPS, ADAM_WD, ADAM_STEP = 0.001, 0.9, 0.999, 1e-08, 0.01, 10
TOK = 256
TOK_B = 256
VMEM_LIMIT = 56 * 1024 * 1024
ARB = "arbitrary"
MESH = pl.DeviceIdType.MESH
_HEAD_SL = [slice(LANE * h, LANE * (h + 1)) for h in range(N_HEADS)]
W_SHAPES = [(D_MODEL, 2048), (D_MODEL, 768), (384, 1024), (256, 1024)]


def _dot(a, b, ca, cb):
    return lax.dot_general(a.astype(BF), b.astype(BF), (((ca,), (cb,)), ((), ())), preferred_element_type=F32)


@jax.custom_vjp
def mm(a, b):
    return _dot(a, b, 1, 0)


@jax.custom_vjp
def mm_nt(a, b):
    return _dot(a, b, 1, 1)


@jax.custom_vjp
def mm_tn(a, b):
    return _dot(a, b, 0, 0)


mm.defvjp(lambda a, b: (_dot(a, b, 1, 0), (a, b)), lambda r, g: (mm_nt(g, r[1]), mm_tn(r[0], g)))
mm_nt.defvjp(lambda a, b: (_dot(a, b, 1, 1), (a, b)), lambda r, g: (mm(g, r[1]), mm_tn(g, r[0])))
mm_tn.defvjp(lambda a, b: (_dot(a, b, 0, 0), (a, b)), lambda r, g: (mm_nt(r[1], g), mm(r[0], g)))


@jax.custom_vjp
def _mmw(a, w, probe):
    return _dot(a, w, 1, 0)


def _mmw_bwd(r, g):
    a, w = r
    return mm_nt(g, w), jnp.zeros_like(w), mm_tn(a, g)


_mmw.defvjp(lambda a, w, probe: (_dot(a, w, 1, 0), (a, w)), _mmw_bwd)


def mmw(a, w, probe):
    return _dot(a, w, 1, 0) if probe is None else _mmw(a, w, probe)


def rmsn(x, g):
    return x * lax.rsqrt(jnp.mean(x * x, axis=-1, keepdims=True) + EPS) * g


def silu(x):
    return x * jax.nn.sigmoid(x)


def _swap32_impl(x):
    n = x.shape[-1]
    lane = lax.broadcasted_iota(jnp.int32, x.shape, x.ndim - 1) % LANE
    up = pltpu.roll(x, n - 32, x.ndim - 1)
    dn = pltpu.roll(x, 32, x.ndim - 1)
    return jnp.where(lane < 32, up, jnp.where(lane < 64, dn, 0.0))


@jax.custom_vjp
def swap32(x):
    return _swap32_impl(x)


swap32.defvjp(lambda x: (_swap32_impl(x), None), lambda _, g: (_swap32_impl(g),))


def rope(x, cs, sn):
    return x * cs + swap32(x) * sn


def k1_tile(x, sh, sc, g_attn, g_q, g_kv, ws, ps, tabs, is_ctx):
    w_a, w_b, w_uq, w_ukv = ws
    p_a, p_b, p_uq, p_ukv = ps
    cs1, sn1 = tabs
    cs, sn = jnp.concatenate([cs1] * N_HEADS, axis=-1), jnp.concatenate([sn1] * N_HEADS, axis=-1)
    cq_t = jnp.concatenate([jnp.ones_like(cs1), cs1] * N_HEADS, axis=-1)
    sq_t = jnp.concatenate([jnp.zeros_like(sn1), sn1] * N_HEADS, axis=-1)
    h = rmsn(x, g_attn) * (1.0 + sc) + sh
    pa = mmw(h, w_a, p_a)
    pb = mmw(h, w_b, p_b)
    rk = pa[:, 512:1024] * 0.125
    rv = pa[:, 1024:1536]
    kpe = pb[:, 640:768]
    kv = mmw(rmsn(pb[:, 384:640], g_kv), w_ukv, p_ukv)
    if not is_ctx:
        rk = rope(rk, cs, sn)
        kpe = rope(kpe, cs1, sn1)
    k_full = jnp.concatenate([piece for sl in _HEAD_SL for piece in (kv[:, sl], kpe)], axis=-1)
    v = kv[:, 512:]
    if is_ctx:
        return rk, rv, k_full, v
    rq = rope(pa[:, 0:512], cs, sn)
    rg = pa[:, 1536:2048]
    q = rope(mmw(rmsn(pb[:, 0:384], g_q), w_uq, p_uq), cq_t, sq_t)
    return rq, rk, rv, rg, q, k_full, v


def log_sigmoid(x):
    return jnp.minimum(x, 0.0) - jnp.log(1.0 + jnp.exp(-jnp.abs(x)))


def ret_chunk(q, k, v, s, lg, reverse):
    c = RET_CHUNK
    ii = lax.broadcasted_iota(jnp.int32, (c, c), 0).astype(F32)
    jj = lax.broadcasted_iota(jnp.int32, (c, c), 1).astype(F32)
    diff = (jj - ii) if reverse else (ii - jj)
    dec = jnp.where(diff >= 0, jnp.exp(lg * jnp.maximum(diff, 0.0)), 0.0)
    pos = lax.broadcasted_iota(jnp.int32, (c, 1), 0).astype(F32)
    if reverse:
        wk, wq = jnp.exp(lg * pos), jnp.exp(lg * (c - pos))
    else:
        wk, wq = jnp.exp(lg * (c - 1.0 - pos)), jnp.exp(lg * (pos + 1.0))
    o = mm(mm_nt(q, k) * dec, v) + mm(q * wq, s)
    s_next = jnp.exp(lg * float(c)) * s + mm_tn(k * wk, v)
    return o, s_next


def ctx_state(kc, vc, lg, reverse):
    n = kc.shape[0]
    pos = lax.broadcasted_iota(jnp.int32, (n, 1), 0).astype(F32)
    w = jnp.exp(lg * pos) if reverse else jnp.exp(lg * (n - 1.0 - pos))
    return mm_tn(kc * w, vc)


def attn_head(qn, qp, kn, kp, v):
    s = (mm_nt(qn, kn) + mm_nt(qp, kp)) * (1.0 / math.sqrt(192.0))
    e = jnp.exp(s - jnp.max(s, axis=-1, keepdims=True))
    return mm(e / jnp.sum(e, axis=-1, keepdims=True), v)


def gn_gate(o, rg, g_ret):
    ys = []
    for h in range(N_HEADS):
        sl = slice(LANE * h, LANE * (h + 1))
        oh = o[:, sl]
        mu = jnp.mean(oh, axis=-1, keepdims=True)
        var = jnp.mean(jnp.square(oh - mu), axis=-1, keepdims=True)
        ys.append((oh - mu) * lax.rsqrt(var + EPS) * g_ret[:, sl])
    return jnp.concatenate(ys, axis=-1) * silu(rg)


def k4a_tile(x, o_f, o_b, rg, y_mla, g_ret, gt_a, g_ffn, sh_f, sc_f, w_out, p_out):
    mix = jnp.concatenate([gn_gate(o_f + o_b, rg, g_ret), y_mla], axis=-1)
    x_mid = x + gt_a * mmw(mix, w_out, p_out)
    h2 = rmsn(x_mid, g_ffn) * (1.0 + sc_f) + sh_f
    return x_mid, h2


def k4c_tile(x_mid, mlp, gt_f, g_final, tgt):
    y = rmsn(x_mid + gt_f * mlp, g_final)
    per_tok = jnp.mean(jnp.square(y - tgt), axis=-1, keepdims=True)
    return 0.5 * jnp.sum(per_tok, axis=0, keepdims=True)


def _cp(sem=None, vmem=VMEM_LIMIT):
    return pltpu.CompilerParams(dimension_semantics=sem, vmem_limit_bytes=vmem)


def _acc(ref, val, first):
    @pl.when(first)
    def _():
        ref[...] = val

    @pl.when(jnp.logical_not(first))
    def _():
        ref[...] += val


def _full(shape):
    nd = len(shape)
    return pl.BlockSpec(shape, lambda *_: (0,) * nd)


ANY = pl.BlockSpec(memory_space=pl.ANY)


def _sds(shape, dtype=F32):
    return jax.ShapeDtypeStruct(shape, dtype)


def _exchange(arrs, gather, name):
    n = len(arrs)
    out_shape = [_sds(((N_DEV,) + a.shape) if gather else a.shape, a.dtype) for a in arrs]

    def body(*refs):
        ins, outs = refs[:n], refs[n:2 * n]
        send_sems, recv_sems, local_sems = refs[2 * n:]
        x, y, c = lax.axis_index("x"), lax.axis_index("y"), lax.axis_index("c")
        me = 4 * x + 2 * y + c
        sends, recvs, locs = [], [], []
        for i in range(n):
            for k in range(N_DEV - 1):
                bits = k + 1
                px = x ^ ((bits >> 2) & 1)
                py = y ^ ((bits >> 1) & 1)
                pc = c ^ (bits & 1)
                peer = 4 * px + 2 * py + pc
                src = ins[i] if gather else ins[i].at[peer]
                sem = i * (N_DEV - 1) + k
                sends.append(pltpu.make_async_remote_copy(
                    src_ref=src, dst_ref=outs[i].at[me], send_sem=send_sems.at[sem], recv_sem=recv_sems.at[sem],
                    device_id=(px, py, pc), device_id_type=MESH))
                recvs.append(pltpu.make_async_remote_copy(
                    src_ref=src, dst_ref=outs[i].at[peer], send_sem=send_sems.at[sem], recv_sem=recv_sems.at[sem],
                    device_id=(px, py, pc), device_id_type=MESH))
            locs.append(pltpu.make_async_copy(ins[i] if gather else ins[i].at[me], outs[i].at[me], local_sems.at[i]))
        for cp in locs + sends:
            cp.start()
        for cp in recvs:
            cp.wait_recv()
        for cp in sends:
            cp.wait_send()
        for cp in locs:
            cp.wait()

    outs = pl.pallas_call(
        body, name=name, out_shape=out_shape, in_specs=[ANY] * n, out_specs=[ANY] * n,
        scratch_shapes=[pltpu.SemaphoreType.DMA((n * (N_DEV - 1),)), pltpu.SemaphoreType.DMA((n * (N_DEV - 1),)),
                        pltpu.SemaphoreType.DMA((n,))],
    )(*arrs)
    return list(outs)


HBM = pl.BlockSpec(memory_space=pltpu.HBM)
SEM = pl.BlockSpec(memory_space=pltpu.SEMAPHORE)
EFFECT = pltpu.SideEffectType.DATAFLOW_SIDE_EFFECTING


def _peer(k):
    x, y, c = lax.axis_index("x"), lax.axis_index("y"), lax.axis_index("c")
    bits = k + 1
    px, py, pc = x ^ ((bits >> 2) & 1), y ^ ((bits >> 1) & 1), c ^ (bits & 1)
    return (px, py, pc), 4 * px + 2 * py + pc, 4 * x + 2 * y + c


def _exchange_start(arrs, gather, name):
    n = len(arrs)
    lands = [pltpu.with_memory_space_constraint(lax.empty(((N_DEV,) + a.shape) if gather else a.shape, a.dtype),
                                                pltpu.HBM) for a in arrs]
    srcs = [pltpu.with_memory_space_constraint(a, pltpu.HBM) for a in arrs]

    def body(*refs):
        ins, zones = refs[:n], refs[n:2 * n]
        send_sems, recv_sems, local_sems = refs[2 * n:2 * n + 3]
        token = refs[-1]
        for i in range(n):
            for k in range(N_DEV - 1):
                dev, peer, me = _peer(k)
                sem = i * (N_DEV - 1) + k
                pltpu.make_async_remote_copy(
                    src_ref=ins[i] if gather else ins[i].at[peer], dst_ref=zones[i].at[me],
                    send_sem=send_sems.at[sem], recv_sem=recv_sems.at[sem], device_id=dev, device_id_type=MESH).start()
            _, _, me = _peer(0)
            pltpu.make_async_copy(ins[i] if gather else ins[i].at[me], zones[i].at[me], local_sems.at[i]).start()
        token[...] = jnp.zeros_like(token)

    nsem = n * (N_DEV - 1)
    outs = pl.pallas_call(
        body, name=name,
        out_shape=[pltpu.SemaphoreType.DMA((nsem,)), pltpu.SemaphoreType.DMA((nsem,)), pltpu.SemaphoreType.DMA((n,))]
        + [pltpu.HBM(a.shape, a.dtype) for a in srcs] + [pltpu.HBM(z.shape, z.dtype) for z in lands]
        + [_sds((8, LANE))],
        in_specs=[HBM] * (2 * n),
        out_specs=[SEM, SEM, SEM] + [HBM] * (2 * n) + [pl.BlockSpec(memory_space=pltpu.VMEM)],
        input_output_aliases={i: 3 + i for i in range(2 * n)},
        compiler_params=pltpu.CompilerParams(has_side_effects=EFFECT),
    )(*srcs, *lands)
    return {"n": n, "gather": gather, "sems": outs[:3], "srcs": outs[3:3 + n], "lands": outs[3 + n:3 + 2 * n],
            "token": outs[-1]}


def _exchange_wait(st, after, name):
    n, gather = st["n"], st["gather"]

    def body(*refs):
        ins, zones = refs[:n], refs[n:2 * n]
        send_sems, recv_sems, local_sems = refs[2 * n:2 * n + 3]
        for i in range(n):
            for k in range(N_DEV - 1):
                dev, peer, me = _peer(k)
                sem = i * (N_DEV - 1) + k
                src = ins[i] if gather else ins[i].at[peer]
                cp = pltpu.make_async_remote_copy(
                    src_ref=src, dst_ref=zones[i].at[peer], send_sem=send_sems.at[sem], recv_sem=recv_sems.at[sem],
                    device_id=dev, device_id_type=MESH)
                cp.wait_send()
                cp.wait_recv()
            _, _, me = _peer(0)
            pltpu.make_async_copy(ins[i] if gather else ins[i].at[me], zones[i].at[me], local_sems.at[i]).wait()

    outs = pl.pallas_call(
        body, name=name,
        out_shape=[pltpu.HBM(a.shape, a.dtype) for a in st["srcs"]] + [pltpu.HBM(z.shape, z.dtype) for z in st["lands"]],
        in_specs=[HBM] * (2 * n) + [SEM, SEM, SEM, ANY], out_specs=[HBM] * (2 * n),
        input_output_aliases={i: i for i in range(2 * n)},
        compiler_params=pltpu.CompilerParams(has_side_effects=EFFECT),
    )(*st["srcs"], *st["lands"], *st["sems"], after)
    return list(outs[n:])


def _mod_fwd(crows, w_ada, b_blk):
    def body(c_ref, w_ref, b_ref, o_ref):
        o_ref[...] = mm(silu(c_ref[...]), w_ref[...]) + b_ref[...]

    return pl.pallas_call(body, name="mod_fwd", out_shape=_sds((24, 768)), compiler_params=_cp())(crows, w_ada, b_blk)


def _mod_bwd(crows, w_ada, dmod_blk, dmodc_blk, dmod_full, dmodc_full):
    def body(c_ref, w_ref, d_ref, dc_ref, df_ref, dcf_ref, gw_ref, gc_ref, gb_ref):
        cr = c_ref[...]
        dc, dcf = dc_ref[0:1, :], dcf_ref[0:1, :]
        for p in range(1, N_DEV):
            dc = dc + dc_ref[p:p + 1, :]
            dcf = dcf + dcf_ref[p:p + 1, :]
        row = lax.broadcasted_iota(jnp.int32, (24, 1), 0)
        gw_ref[...] = mm_tn(silu(cr), jnp.where(row == 16, dc, d_ref[...]))
        cc = cr[16:17, :]
        sg = jax.nn.sigmoid(cc)
        part = mm_nt(jnp.broadcast_to(dc, (8, 768)), w_ref[...])
        gc_ref[...] = part * (sg * (1.0 + cc * (1.0 - sg)))
        gb_ref[...] = jnp.sum(df_ref[...], axis=0, keepdims=True) + dcf

    return pl.pallas_call(
        body, name="mod_bwd", out_shape=[_sds((D_MODEL, 768)), _sds((8, D_MODEL)), _sds((1, 6 * D_MODEL))],
        compiler_params=_cp())(crows, w_ada, dmod_blk, dmodc_blk, dmod_full, dmodc_full)


def _tab_specs(tk):
    return [pl.BlockSpec((tk, LANE), lambda i, t: (t, 0))] * 2


def _k1_fwd(x, mod, g_attn, g_q, g_kv, ws, tabs, kv_all, is_ctx):
    b, l, _ = x.shape
    nt = l // TOK
    n_f32 = 2 if is_ctx else 4

    def body(x_ref, mod_ref, ga_ref, gq_ref, gk_ref, wa_ref, wb_ref, wq_ref, wk_ref, cs_ref, sn_ref, *rest):
        outs = rest if is_ctx else rest[2:]
        res = k1_tile(x_ref[...], mod_ref[0:1, :], mod_ref[1:2, :], ga_ref[...], gq_ref[...], gk_ref[...],
                      (wa_ref[...], wb_ref[...], wq_ref[...], wk_ref[...]), (None,) * 4,
                      (cs_ref[...], sn_ref[...]), is_ctx)
        for o_ref, r in zip(outs, res):
            o_ref[...] = r.astype(o_ref.dtype)

    tok = lambda w, off=0: pl.BlockSpec((None, TOK, w), lambda i, t: (i, t + off, 0))
    mod_spec = pl.BlockSpec((None, 8, D_MODEL), (lambda i, t: (0, 0, 0)) if is_ctx else (lambda i, t: (i, 0, 0)))
    kv_off = 0 if is_ctx else CTX_LEN // TOK
    in_specs = ([tok(D_MODEL), mod_spec, _full((1, D_MODEL)), _full((1, 384)), _full((1, 256))]
                + [_full(s) for s in W_SHAPES] + _tab_specs(TOK))
    args = [x, mod, g_attn, g_q, g_kv, *ws, *tabs]
    out_specs = [tok(512)] * n_f32 + ([] if is_ctx else [tok(1024)]) + [tok(1024, kv_off), tok(512, kv_off)]
    out_shape = ([_sds((b, l, 512))] * n_f32 + ([] if is_ctx else [_sds((b, l, 1024), BF)])
                 + [_sds((b, KV_LEN, 1024), BF), _sds((b, KV_LEN, 512), BF)])
    aliases = {}
    if not is_ctx:
        aliases = {len(args): n_f32 + 1, len(args) + 1: n_f32 + 2}
        in_specs += [ANY, ANY]
        args += list(kv_all)
    return pl.pallas_call(
        body, name="k1_fwd_ctx" if is_ctx else "k1_fwd", grid=(b, nt), in_specs=in_specs, out_specs=out_specs,
        out_shape=out_shape, input_output_aliases=aliases, compiler_params=_cp((ARB, ARB)),
    )(*args)


N_ACC = 7


def _k1_bwd(x, mod, g_attn, g_q, g_kv, ws, tabs, cts, dx_res, init, is_ctx):
    b, l, _ = x.shape
    tk = TOK_B
    nt = l // tk
    flat_cts = [a for group in cts for a in group]
    group_sizes = [len(g) for g in cts]
    n_ct = len(flat_cts)
    has_res = dx_res is not None
    has_init = init is not None
    acc_shapes = W_SHAPES + [(1, D_MODEL), (1, 384), (1, 256)]

    def body(*refs):
        it = iter(refs)
        x_ref, mod_ref, ga_ref, gq_ref, gk_ref = [next(it) for _ in range(5)]
        w_hbm = [next(it) for _ in range(4)]
        tab_refs = [next(it) for _ in range(2)]
        ct_refs = [next(it) for _ in range(n_ct)]
        res_ref = next(it) if has_res else None
        init_refs = [next(it) for _ in range(N_ACC)] if has_init else None
        gx_ref = next(it) if not is_ctx else None
        out_hbm = [next(it) for _ in range(N_ACC)]
        dmod_ref = next(it)
        w_vmem = [next(it) for _ in range(4)]
        accs = [next(it) for _ in range(N_ACC)]
        sem = next(it)
        i, t = pl.program_id(0), pl.program_id(1)
        first = jnp.logical_and(i == 0, t == 0)
        last = jnp.logical_and(i == b - 1, t == nt - 1)

        @pl.when(first)
        def _():
            for src, dst in zip(w_hbm, w_vmem):
                pltpu.sync_copy(src, dst)
            for k in range(N_ACC):
                if has_init:
                    pltpu.sync_copy(init_refs[k], accs[k])
                else:
                    accs[k][...] = jnp.zeros(acc_shapes[k], F32)

        ct_vals, pos = [], 0
        for gsz in group_sizes:
            v = ct_refs[pos][...].astype(F32)
            for r in ct_refs[pos + 1:pos + gsz]:
                v = v + r[...]
            ct_vals.append(v)
            pos += gsz
        wv = tuple(r[...] for r in w_vmem)
        tv = tuple(r[...] for r in tab_refs)

        def f(xv, sh, sc, ga, gq, gk, *probes):
            return k1_tile(xv, sh, sc, ga, gq, gk, wv, probes, tv, is_ctx)

        probes = [jnp.zeros(s, F32) for s in W_SHAPES]
        _, vjp = jax.vjp(f, x_ref[...], mod_ref[0:1, :], mod_ref[1:2, :], ga_ref[...], gq_ref[...], gk_ref[...], *probes)
        dx, dsh, dsc, dga, dgq, dgk, dwa, dwb, dwq, dwk = vjp(tuple(ct_vals))
        if not is_ctx:
            gx_ref[...] = dx + res_ref[...] if has_res else dx
        for ref, val in zip(accs, (dwa, dwb, dwq, dwk, dga, dgq, dgk)):
            ref[...] += val
        t0 = first if is_ctx else t == 0
        _acc(dmod_ref.at[0:1, :], dsh, t0)
        _acc(dmod_ref.at[1:2, :], dsc, t0)

        @pl.when(t0)
        def _():
            dmod_ref[2:8, :] = jnp.zeros((6, D_MODEL), F32)

        @pl.when(last)
        def _():
            cps = [pltpu.make_async_copy(accs[k], out_hbm[k], sem.at[k]) for k in range(N_ACC)]
            for cp in cps:
                cp.start()
            for cp in cps:
                cp.wait()

    tok = lambda w, off=0: pl.BlockSpec((None, tk, w), lambda i, t: (i, t + off, 0))
    mod_spec = pl.BlockSpec((None, 8, D_MODEL), (lambda i, t: (0, 0, 0)) if is_ctx else (lambda i, t: (i, 0, 0)))
    in_specs = ([tok(D_MODEL), mod_spec, _full((1, D_MODEL)), _full((1, 384)), _full((1, 256))] + [ANY] * 4
                + _tab_specs(tk))
    args = [x, mod, g_attn, g_q, g_kv, *ws, *tabs]
    for a, off in flat_cts:
        in_specs.append(tok(a.shape[-1], off // tk))
        args.append(a)
    if has_res:
        in_specs.append(tok(D_MODEL))
        args.append(dx_res)
    if has_init:
        in_specs += [ANY] * N_ACC
        args += list(init)
    out_shape, out_specs = [], []
    if not is_ctx:
        out_shape.append(_sds((b, l, D_MODEL)))
        out_specs.append(tok(D_MODEL))
    out_shape += [_sds(s) for s in acc_shapes] + [_sds((1 if is_ctx else b, 8, D_MODEL))]
    out_specs += [ANY] * N_ACC + [mod_spec]
    outs = pl.pallas_call(
        body, name="k1_bwd_ctx" if is_ctx else "k1_bwd", grid=(b, nt), in_specs=in_specs, out_specs=out_specs,
        out_shape=out_shape,
        scratch_shapes=[pltpu.VMEM(s, BF) for s in W_SHAPES] + [pltpu.VMEM(s, F32) for s in acc_shapes]
        + [pltpu.SemaphoreType.DMA((N_ACC,))],
        compiler_params=_cp((ARB, ARB)),
    )(*args)
    outs = list(outs)
    gx = None if is_ctx else outs.pop(0)
    return gx, outs[:N_ACC], outs[N_ACC]


def _chunk_spec(rev):
    if rev:
        return pl.BlockSpec((None, RET_CHUNK, 512), lambda i, n: (i, N_CHUNK - 1 - n, 0))
    return pl.BlockSpec((None, RET_CHUNK, 512), lambda i, n: (i, n, 0))


def _state_spec(rev):
    if rev:
        return pl.BlockSpec((None, N_HEADS, None, LANE, LANE), lambda i, n: (i, 0, N_CHUNK - 1 - n, 0, 0))
    return pl.BlockSpec((None, N_HEADS, None, LANE, LANE), lambda i, n: (i, 0, n, 0, 0))


_CTX_SPEC = pl.BlockSpec((None, CTX_LEN, 512), lambda i, n: (i, 0, 0))
_DEC_SPEC = pl.BlockSpec((N_HEADS, 1, 1), lambda i, n: (0, 0, 0))


def _k2_fwd(rq, rk, rv, rkc, rvc, dec_f, dec_b):
    b = rq.shape[0]

    def body(qf, kf, vf, qb, kb, vb, kc, vc, df, db, of_ref, ob_ref, sf_out, sb_out, sf, sb):
        n = pl.program_id(1)
        for h, sl in enumerate(_HEAD_SL):
            lgf, lgb = log_sigmoid(df[h]), log_sigmoid(db[h])

            @pl.when(n == 0)
            def _():
                sf[h] = ctx_state(kc[:, sl], vc[:, sl], lgf, False)
                sb[h] = ctx_state(kc[:, sl], vc[:, sl], lgb, True)

            sf_out[h] = sf[h]
            sb_out[h] = sb[h]
            o, s = ret_chunk(qf[:, sl], kf[:, sl], vf[:, sl], sf[h], lgf, False)
            of_ref[:, sl] = o
            sf[h] = s
            o, s = ret_chunk(qb[:, sl], kb[:, sl], vb[:, sl], sb[h], lgb, True)
            ob_ref[:, sl] = o
            sb[h] = s

    l = rq.shape[1]
    return pl.pallas_call(
        body, name="k2_fwd", grid=(b, N_CHUNK),
        in_specs=[_chunk_spec(False)] * 3 + [_chunk_spec(True)] * 3 + [_CTX_SPEC, _CTX_SPEC, _DEC_SPEC, _DEC_SPEC],
        out_specs=[_chunk_spec(False), _chunk_spec(True), _state_spec(False), _state_spec(True)],
        out_shape=[_sds((b, l, 512)), _sds((b, l, 512)), _sds((b, N_HEADS, N_CHUNK, LANE, LANE)),
                   _sds((b, N_HEADS, N_CHUNK, LANE, LANE))],
        scratch_shapes=[pltpu.VMEM((N_HEADS, LANE, LANE), F32), pltpu.VMEM((N_HEADS, LANE, LANE), F32)],
        compiler_params=_cp((ARB, ARB)),
    )(rq, rk, rv, rq, rk, rv, rkc, rvc, dec_f, dec_b)


def _k2_bwd(rq, rk, rv, do, sf_prev, sb_prev, rkc, rvc, dec_f, dec_b):
    b, l, _ = rq.shape

    def body(qf, kf, vf, gf, spf, qb, kb, vb, gb, spb, kc, vc, df, db,
             dqf, dkf, dvf, dqb, dkb, dvb, dkc, dvc, ddf, ddb, dsf, dsb):
        n = pl.program_id(1)

        @pl.when(n == 0)
        def _():
            dsf[...] = jnp.zeros((N_HEADS, LANE, LANE), F32)
            dsb[...] = jnp.zeros((N_HEADS, LANE, LANE), F32)

        def one(h, sl, q, k, v, g, sp, dec, ds, dq, dk, dv, dd, rev):
            def f(qv, kv_, vv, sv, dcy):
                return ret_chunk(qv, kv_, vv, sv, log_sigmoid(dcy), rev)

            _, vjp = jax.vjp(f, q[:, sl], k[:, sl], v[:, sl], sp[h], dec[h])
            gq, gk, gv, gs, gd = vjp((g[:, sl], ds[h]))
            dq[:, sl] = gq
            dk[:, sl] = gk
            dv[:, sl] = gv
            ds[h] = gs
            _acc(dd.at[h], jnp.broadcast_to(gd, (8, LANE)), n == 0)

        for h, sl in enumerate(_HEAD_SL):
            one(h, sl, qf, kf, vf, gf, spf, df, dsf, dqf, dkf, dvf, ddf, False)
            one(h, sl, qb, kb, vb, gb, spb, db, dsb, dqb, dkb, dvb, ddb, True)

        @pl.when(n == N_CHUNK - 1)
        def _():
            def f(kcv, vcv, dcy, rev):
                return ctx_state(kcv, vcv, log_sigmoid(dcy), rev)

            for h, sl in enumerate(_HEAD_SL):
                _, vjp_f = jax.vjp(functools.partial(f, rev=False), kc[:, sl], vc[:, sl], df[h])
                gk_f, gv_f, gd_f = vjp_f(dsf[h])
                _, vjp_b = jax.vjp(functools.partial(f, rev=True), kc[:, sl], vc[:, sl], db[h])
                gk_b, gv_b, gd_b = vjp_b(dsb[h])
                dkc[:, sl] = gk_f + gk_b
                dvc[:, sl] = gv_f + gv_b
                ddf[h] += jnp.broadcast_to(gd_f, (8, LANE))
                ddb[h] += jnp.broadcast_to(gd_b, (8, LANE))

    dd_spec = pl.BlockSpec((None, N_HEADS, 8, LANE), lambda i, n: (i, 0, 0, 0))
    return pl.pallas_call(
        body, name="k2_bwd", grid=(b, N_CHUNK),
        in_specs=[_chunk_spec(True)] * 4 + [_state_spec(True)] + [_chunk_spec(False)] * 4 + [_state_spec(False)]
        + [_CTX_SPEC, _CTX_SPEC, _DEC_SPEC, _DEC_SPEC],
        out_specs=[_chunk_spec(True)] * 3 + [_chunk_spec(False)] * 3 + [_CTX_SPEC, _CTX_SPEC, dd_spec, dd_spec],
        out_shape=[_sds((b, l, 512))] * 6 + [_sds((b, CTX_LEN, 512))] * 2 + [_sds((b, N_HEADS, 8, LANE))] * 2,
        scratch_shapes=[pltpu.VMEM((N_HEADS, LANE, LANE), F32), pltpu.VMEM((N_HEADS, LANE, LANE), F32)],
        compiler_params=_cp((ARB, ARB)),
    )(rq, rk, rv, do, sf_prev, rq, rk, rv, do, sb_prev, rkc, rvc, dec_f, dec_b)


TQ = 512
QK_W = 2 * LANE


def _softmax_parts(q, k):
    s = _dot(q, k, 1, 1) * (1.0 / math.sqrt(192.0))
    e = jnp.exp(s - jnp.max(s, axis=-1, keepdims=True))
    return e, 1.0 / jnp.sum(e, axis=-1, keepdims=True)


def _k3_specs():
    qs = lambda w: pl.BlockSpec((None, TQ, w), lambda i, h, t: (i, t, h))
    ks = lambda w: pl.BlockSpec((None, KV_LEN, w), lambda i, h, t: (i, 0, h))
    return qs, ks


def _k3_fwd(q, k, v):
    b, l, _ = q.shape

    def body(q_ref, k_ref, v_ref, o_ref):
        e, inv = _softmax_parts(q_ref[...], k_ref[...])
        o_ref[...] = _dot(e, v_ref[...], 1, 0) * inv

    qs, ks = _k3_specs()
    return pl.pallas_call(
        body, name="k3_fwd", grid=(b, N_HEADS, l // TQ), in_specs=[qs(QK_W), ks(QK_W), ks(LANE)], out_specs=qs(LANE),
        out_shape=_sds((b, l, N_HEADS * LANE)), compiler_params=_cp((ARB, ARB, ARB)),
    )(q, k, v)


def _k3_bwd(q, k, v, dy):
    b, l, _ = q.shape
    scale = 1.0 / math.sqrt(192.0)

    def body(q_ref, k_ref, v_ref, dy_ref, dq_ref, dk_ref, dv_ref):
        t0 = pl.program_id(2) == 0
        qv, kv_ = q_ref[...], k_ref[...]
        g = dy_ref[...].astype(BF)
        e, inv = _softmax_parts(qv, kv_)
        p = e * inv
        dp = _dot(g, v_ref[...], 1, 1)
        ds = (p * (dp - jnp.sum(dp * p, axis=-1, keepdims=True)) * scale).astype(BF)
        _acc(dv_ref, _dot(p, g, 0, 0), t0)
        dq_ref[...] = _dot(ds, kv_, 1, 0)
        _acc(dk_ref, _dot(ds, qv, 0, 0), t0)

    qs, ks = _k3_specs()
    return pl.pallas_call(
        body, name="k3_bwd", grid=(b, N_HEADS, l // TQ), in_specs=[qs(QK_W), ks(QK_W), ks(LANE), qs(LANE)],
        out_specs=[qs(QK_W), ks(QK_W), ks(LANE)],
        out_shape=[_sds((b, l, N_HEADS * QK_W)), _sds((b, KV_LEN, N_HEADS * QK_W)), _sds((b, KV_LEN, N_HEADS * LANE))],
        compiler_params=_cp((ARB, ARB, ARB)),
    )(q, k, v, dy)


def _mod_rows(mod_ref, rows):
    return [mod_ref[r:r + 1, :] for r in rows]


def _k4a_fwd(x, o_f, o_b, rg, y_mla, g_ret, w_out, mod, g_ffn):
    b, l, _ = x.shape

    def body(x_ref, of_ref, ob_ref, rg_ref, ym_ref, gr_ref, wo_ref, mod_ref, gf_ref, xm_ref, h2_ref):
        gt_a, sh_f, sc_f = _mod_rows(mod_ref, (2, 3, 4))
        x_mid, h2 = k4a_tile(x_ref[...], of_ref[...], ob_ref[...], rg_ref[...], ym_ref[...], gr_ref[...], gt_a,
                             gf_ref[...], sh_f, sc_f, wo_ref[...], None)
        xm_ref[...] = x_mid
        h2_ref[...] = h2.astype(BF)

    tok = lambda w: pl.BlockSpec((None, TOK, w), lambda i, t: (i, t, 0))
    mod_spec = pl.BlockSpec((None, 8, D_MODEL), lambda i, t: (i, 0, 0))
    return pl.pallas_call(
        body, name="k4a_fwd", grid=(b, l // TOK),
        in_specs=[tok(D_MODEL), tok(512), tok(512), tok(512), tok(512), _full((1, 512)), _full((D_MODEL, D_MODEL)),
                  mod_spec, _full((1, D_MODEL))],
        out_specs=[tok(D_MODEL), tok(D_MODEL)], out_shape=[_sds((b, l, D_MODEL)), _sds((b, l, D_MODEL), BF)],
        compiler_params=_cp((ARB, ARB)),
    )(x, o_f, o_b, rg, y_mla, g_ret, w_out, mod, g_ffn)


TOK_M = 512


def _last_j(j, idx):
    return jnp.where(j == N_DEV - 1, idx, 0)


def _k4b_mlp_loss(h2, w1, w2, x_mid, mod, g_final, tgt):
    b, l, _ = h2.shape
    nt = l // TOK_M

    def body(h2_ref, w1_ref, w2_ref, xm_ref, mod_ref, gfin_ref, tgt_ref, dxm_ref, dmlp_ref, loss_ref, dgt_ref,
             dgfin_ref, acc):
        j, i, t = pl.program_id(0), pl.program_id(1), pl.program_id(2)
        rows = pl.ds(pl.multiple_of((i * nt + t) * TOK_M, TOK_M), TOK_M)
        a = _dot(h2_ref[...], w1_ref[...], 1, 0)
        part = _dot(jnp.square(jnp.maximum(a, 0.0)), w2_ref[...], 1, 0)
        _acc(acc.at[rows, :], part, j == 0)

        @pl.when(j == N_DEV - 1)
        def _():
            (gt_f,) = _mod_rows(mod_ref, (5,))
            loss, vjp = jax.vjp(k4c_tile, xm_ref[...], acc[rows, :], gt_f, gfin_ref[...], tgt_ref[...])
            dxm, dmlp, dgt, dgfin, _ = vjp(jnp.ones((1, 1), F32))
            dxm_ref[...] = dxm
            dmlp_ref[...] = dmlp.astype(BF)
            first = jnp.logical_and(i == 0, t == 0)
            _acc(loss_ref, jnp.broadcast_to(loss, (8, LANE)), first)
            _acc(dgfin_ref, dgfin, first)
            _acc(dgt_ref, dgt, t == 0)

    tok = lambda w: pl.BlockSpec((None, TOK_M, w), lambda j, i, t: (i, t, 0))
    tok_last = lambda w: pl.BlockSpec((None, TOK_M, w), lambda j, i, t: (_last_j(j, i), _last_j(j, t), 0))
    return pl.pallas_call(
        body, name="k4b_mlp_loss", grid=(N_DEV, b, nt),
        in_specs=[tok(D_MODEL), pl.BlockSpec((None, D_MODEL, FF_BLK), lambda j, i, t: (j, 0, 0)),
                  pl.BlockSpec((None, FF_BLK, D_MODEL), lambda j, i, t: (j, 0, 0)), tok_last(D_MODEL),
                  pl.BlockSpec((None, 8, D_MODEL), lambda j, i, t: (i, 0, 0)),
                  pl.BlockSpec((1, D_MODEL), lambda j, i, t: (0, 0)), tok_last(D_MODEL)],
        out_specs=[tok_last(D_MODEL), tok_last(D_MODEL), pl.BlockSpec((8, LANE), lambda j, i, t: (0, 0)),
                   pl.BlockSpec((None, 1, D_MODEL), lambda j, i, t: (_last_j(j, i), 0, 0)),
                   pl.BlockSpec((1, D_MODEL), lambda j, i, t: (0, 0))],
        out_shape=[_sds((b, l, D_MODEL)), _sds((b, l, D_MODEL), BF), _sds((8, LANE)), _sds((b, 1, D_MODEL)),
                   _sds((1, D_MODEL))],
        scratch_shapes=[pltpu.VMEM((b * l, D_MODEL), F32)],
        compiler_params=_cp((ARB, ARB, ARB)),
    )(h2, w1, w2, x_mid, mod, g_final, tgt)


def _k4d_mlp_bwd(h2, dmlp, w1, w2):
    b, l, _ = h2.shape
    nt = l // TOK_M

    def body(h2_ref, dm_ref, w1_ref, w2_ref, dh2_ref, dw1_ref, dw2_ref, acc1, acc2, dh2s):
        j, i, t = pl.program_id(0), pl.program_id(1), pl.program_id(2)
        first = jnp.logical_and(i == 0, t == 0)
        rows = pl.ds(pl.multiple_of((i * nt + t) * TOK_M, TOK_M), TOK_M)
        h2v, dm = h2_ref[...], dm_ref[...]
        r = jnp.maximum(_dot(h2v, w1_ref[...], 1, 0), 0.0)
        da = _dot(dm, w2_ref[...], 1, 1) * (2.0 * r)
        _acc(acc2, _dot(jnp.square(r), dm, 0, 0), first)
        _acc(acc1, _dot(h2v, da, 0, 0), first)
        _acc(dh2s.at[rows, :], _dot(da, w1_ref[...], 1, 1), j == 0)

        @pl.when(jnp.logical_and(i == b - 1, t == nt - 1))
        def _():
            dw1_ref[...] = acc1[...].astype(BF)
            dw2_ref[...] = acc2[...].astype(BF)

        @pl.when(j == N_DEV - 1)
        def _():
            dh2_ref[...] = dh2s[rows, :]

    tok = lambda w: pl.BlockSpec((None, TOK_M, w), lambda j, i, t: (i, t, 0))
    return pl.pallas_call(
        body, name="k4d_mlp_bwd", grid=(N_DEV, b, nt),
        in_specs=[tok(D_MODEL), tok(D_MODEL), pl.BlockSpec((None, D_MODEL, FF_BLK), lambda j, i, t: (j, 0, 0)),
                  pl.BlockSpec((None, FF_BLK, D_MODEL), lambda j, i, t: (j, 0, 0))],
        out_specs=[pl.BlockSpec((None, TOK_M, D_MODEL), lambda j, i, t: (_last_j(j, i), _last_j(j, t), 0)),
                   pl.BlockSpec((None, D_MODEL, FF_BLK), lambda j, i, t: (j, 0, 0)),
                   pl.BlockSpec((None, FF_BLK, D_MODEL), lambda j, i, t: (j, 0, 0))],
        out_shape=[_sds((b, l, D_MODEL)), _sds((N_DEV, D_MODEL, FF_BLK), BF), _sds((N_DEV, FF_BLK, D_MODEL), BF)],
        scratch_shapes=[pltpu.VMEM((D_MODEL, FF_BLK), F32), pltpu.VMEM((FF_BLK, D_MODEL), F32),
                        pltpu.VMEM((b * l, D_MODEL), F32)],
        compiler_params=_cp((ARB, ARB, ARB)),
    )(h2, dmlp, w1, w2)


def _k4e_bwd(x, o_f, o_b, rg, y_mla, g_ret, w_out, mod, g_ffn, dxm, dh2):
    b, l, _ = x.shape

    def body(x_ref, of_ref, ob_ref, rg_ref, ym_ref, gr_ref, wo_ref, mod_ref, gf_ref, dxm_ref, dh2_ref,
             dx_ref, do_ref, drg_ref, dym_ref, dwo_ref, dgr_ref, dgf_ref, dmod_ref):
        i, t = pl.program_id(0), pl.program_id(1)
        first = jnp.logical_and(i == 0, t == 0)
        gt_a, sh_f, sc_f = _mod_rows(mod_ref, (2, 3, 4))
        wo = wo_ref[...]

        def f(xv, ofv, rgv, ymv, grv, gta, gfv, shf, scf, p_out):
            return k4a_tile(xv, ofv, ob_ref[...], rgv, ymv, grv, gta, gfv, shf, scf, wo, p_out)

        _, vjp = jax.vjp(f, x_ref[...], of_ref[...], rg_ref[...], ym_ref[...], gr_ref[...], gt_a, gf_ref[...], sh_f,
                         sc_f, jnp.zeros((D_MODEL, D_MODEL), F32))
        dx, do, drg, dym, dgr, dgta, dgf, dshf, dscf, dwo = vjp((dxm_ref[...], dh2_ref[...]))
        dx_ref[...] = dx
        do_ref[...] = do
        drg_ref[...] = drg
        dym_ref[...] = dym
        _acc(dwo_ref, dwo, first)
        _acc(dgr_ref, dgr, first)
        _acc(dgf_ref, dgf, first)
        t0 = t == 0
        _acc(dmod_ref.at[2:3, :], dgta, t0)
        _acc(dmod_ref.at[3:4, :], dshf, t0)
        _acc(dmod_ref.at[4:5, :], dscf, t0)

        @pl.when(t0)
        def _():
            dmod_ref[0:2, :] = jnp.zeros((2, D_MODEL), F32)
            dmod_ref[5:8, :] = jnp.zeros((3, D_MODEL), F32)

    tok = lambda w: pl.BlockSpec((None, TOK_B, w), lambda i, t: (i, t, 0))
    mod_spec = pl.BlockSpec((None, 8, D_MODEL), lambda i, t: (i, 0, 0))
    return pl.pallas_call(
        body, name="k4e_bwd", grid=(b, l // TOK_B),
        in_specs=[tok(D_MODEL), tok(512), tok(512), tok(512), tok(512), _full((1, 512)), _full((D_MODEL, D_MODEL)),
                  mod_spec, _full((1, D_MODEL)), tok(D_MODEL), tok(D_MODEL)],
        out_specs=[tok(D_MODEL), tok(512), tok(512), tok(512), _full((D_MODEL, D_MODEL)), _full((1, 512)),
                   _full((1, D_MODEL)), mod_spec],
        out_shape=[_sds((b, l, D_MODEL)), _sds((b, l, 512)), _sds((b, l, 512)), _sds((b, l, 512)),
                   _sds((D_MODEL, D_MODEL)), _sds((1, 512)), _sds((1, D_MODEL)), _sds((b, 8, D_MODEL))],
        compiler_params=_cp((ARB, ARB)),
    )(x, o_f, o_b, rg, y_mla, g_ret, w_out, mod, g_ffn, dxm, dh2)


def _adamw(w, m, v, pieces, name):
    r, c = w.shape
    npc = pieces.shape[0]
    rb = r
    for cand in (256, 128, 64, 32, 16, 8):
        if r > cand and r % cand == 0 and cand * c * 4 * (npc + 7) * 2 <= 24 * 1024 * 1024:
            rb = cand
            break

    def body(w_ref, m_ref, v_ref, p_ref, g_ref, d_ref, nm_ref, nv_ref):
        g = p_ref[0].astype(F32)
        for k in range(1, npc):
            g = g + p_ref[k].astype(F32)
        wv = w_ref[...]
        mn = ADAM_B1 * m_ref[...] + (1.0 - ADAM_B1) * g
        vn = ADAM_B2 * v_ref[...] + (1.0 - ADAM_B2) * jnp.square(g)
        m_hat = mn / (1.0 - ADAM_B1 ** ADAM_STEP)
        v_hat = vn / (1.0 - ADAM_B2 ** ADAM_STEP)
        g_ref[...] = g
        d_ref[...] = -ADAM_LR * (m_hat / (jnp.sqrt(v_hat) + ADAM_EPS) + ADAM_WD * wv)
        nm_ref[...] = mn
        nv_ref[...] = vn

    blk = pl.BlockSpec((rb, c), lambda i: (i, 0))
    return pl.pallas_call(
        body, name=name, grid=(r // rb,), in_specs=[blk, blk, blk, pl.BlockSpec((npc, rb, c), lambda i: (0, i, 0))],
        out_specs=[blk] * 4, out_shape=[_sds((r, c))] * 4, compiler_params=_cp((ARB,)),
    )(w, m, v, pieces)


def _pad_heads(w, d):
    k = w.shape[0]
    return jnp.pad(w.reshape(k, N_HEADS, d), ((0, 0), (0, 0), (0, LANE - d))).reshape(k, N_HEADS * LANE)


def _cut_heads(w, d):
    k = w.shape[0]
    return w.reshape(k, N_HEADS, LANE)[:, :, :d].reshape(k, N_HEADS * d)


def _w_in_pad(w):
    w_a = jnp.concatenate([_pad_heads(w[:, 0:256], 64), _pad_heads(w[:, 256:512], 64), w[:, 512:1536]], axis=1)
    w_b = jnp.concatenate([w[:, 1536:2176], jnp.pad(w[:, 2176:2240], ((0, 0), (0, 64)))], axis=1)
    return w_a, w_b


def _w_in_cut(g_a, g_b):
    return jnp.concatenate([_cut_heads(g_a[:, 0:512], 64), _cut_heads(g_a[:, 512:1024], 64), g_a[:, 1024:2048],
                            g_b[:, 0:704]], axis=1)


def _w_uq_pad(w):
    return jnp.pad(w.reshape(384, N_HEADS, 192), ((0, 0), (0, 0), (0, 64))).reshape(384, 1024)


def _w_uq_cut(g):
    return g.reshape(384, N_HEADS, 256)[:, :, :192].reshape(384, 768)


def _w_ukv_perm(w):
    w = w.reshape(256, N_HEADS, 256)
    return jnp.concatenate([w[:, :, :128].reshape(256, 512), w[:, :, 128:].reshape(256, 512)], axis=1)


def _w_ukv_unperm(g):
    return jnp.concatenate([g[:, :512].reshape(256, N_HEADS, 128), g[:, 512:].reshape(256, N_HEADS, 128)],
                           axis=2).reshape(256, 1024)


def _unshard_cols(g):
    return jnp.transpose(g, (1, 0, 2)).reshape(g.shape[1], N_DEV * g.shape[2])


def _shard_cols(w):
    k, n = w.shape
    return jnp.transpose(w.reshape(k, N_DEV, n // N_DEV), (1, 0, 2))


def _rope_tables():
    rows = SEQ // GRID_W
    row = jnp.repeat(jnp.arange(rows, dtype=F32), GRID_W)
    col = jnp.tile(jnp.arange(GRID_W, dtype=F32), rows)
    freq = ROPE_BASE ** (-jnp.arange(16, dtype=F32) / 16)
    ang = jnp.concatenate([row[:, None] * freq, col[:, None] * freq], axis=-1)
    cos, sin = jnp.cos(ang), jnp.sin(ang)
    z = jnp.zeros((SEQ, 64), F32)
    return jnp.concatenate([cos, cos, z], axis=1), jnp.concatenate([-sin, sin, z], axis=1)


_PACKED = (("g_attn", 1024), ("g_ffn", 1024), ("ret_decay_fwd", 4), ("ret_decay_bwd", 4), ("g_ret", 512),
           ("g_q_lora", 384), ("g_kv_lora", 256), ("g_final", 1024))
_PACK_OFF = {}
_off = 0
for _name, _n in _PACKED:
    _PACK_OFF[_name] = _off
    _off += -(-_n // LANE) * LANE
PACK_W = _off


def _pack_small(vals):
    parts = []
    for name, n in _PACKED:
        a = vals[name].reshape(-1).astype(F32)
        parts.append(jnp.pad(a, (0, -(-n // LANE) * LANE - n)))
    return jnp.concatenate(parts).reshape(1, PACK_W)


def _adamw_small(params, packed, gcc, gb_ada):
    names = list(params)
    n_p = len(names)

    def body(*refs):
        p_ref, gcc_ref, gb_ref = refs[3 * n_p:3 * n_p + 3]
        outs = refs[3 * n_p + 3:]
        for k, name in enumerate(names):
            w_ref, m_ref, v_ref = refs[3 * k:3 * k + 3]
            n = w_ref.shape[1]
            if name == "b_ada":
                g = gb_ref[...]
            elif name == "c_ctx":
                g = gcc_ref[0, 0:1, :]
                for d in range(1, N_DEV):
                    g = g + gcc_ref[d, 0:1, :]
            else:
                off = _PACK_OFF[name]
                g = p_ref[0, :, off:off + n]
                for d in range(1, N_DEV):
                    g = g + p_ref[d, :, off:off + n]
            mn = ADAM_B1 * m_ref[...] + (1.0 - ADAM_B1) * g
            vn = ADAM_B2 * v_ref[...] + (1.0 - ADAM_B2) * jnp.square(g)
            m_hat = mn / (1.0 - ADAM_B1 ** ADAM_STEP)
            v_hat = vn / (1.0 - ADAM_B2 ** ADAM_STEP)
            outs[4 * k][...] = g
            outs[4 * k + 1][...] = -ADAM_LR * (m_hat / (jnp.sqrt(v_hat) + ADAM_EPS) + ADAM_WD * w_ref[...])
            outs[4 * k + 2][...] = mn
            outs[4 * k + 3][...] = vn

    args = [a for name in names for a in params[name]] + [packed, gcc, gb_ada]
    out_shape = [_sds(params[name][0].shape) for name in names for _ in range(4)]
    outs = pl.pallas_call(body, name="adamw_small", out_shape=out_shape, compiler_params=_cp())(*args)
    return {name: list(outs[4 * k:4 * k + 4]) for k, name in enumerate(names)}


def kernel(x, c, ctx, c_ctx, w_ada, b_ada, g_attn, g_ffn, w_in, ret_decay_fwd, ret_decay_bwd, g_ret, g_q_lora, w_uq, g_kv_lora, w_ukv, w_out, w_ff1, w_ff2, g_final, loss_target, m_c_ctx, m_w_ada, m_b_ada, m_g_attn, m_g_ffn, m_w_in, m_ret_decay_fwd, m_ret_decay_bwd, m_g_ret, m_g_q_lora, m_w_uq, m_g_kv_lora, m_w_ukv, m_w_out, m_w_ff1, m_w_ff2, m_g_final, v_c_ctx, v_w_ada, v_b_ada, v_g_attn, v_g_ffn, v_w_in, v_ret_decay_fwd, v_ret_decay_bwd, v_g_ret, v_g_q_lora, v_w_uq, v_g_kv_lora, v_w_ukv, v_w_out, v_w_ff1, v_w_ff2, v_g_final):
    me = 4 * lax.axis_index("x") + 2 * lax.axis_index("y") + lax.axis_index("c")
    nb = x.shape[0]

    c_pad = jnp.pad(c, ((0, 8 - nb), (0, 0)))
    c_all, g_in, g_uq, g_ukv = _exchange([c_pad, w_in[0].astype(BF), w_uq[0].astype(BF), w_ukv[0].astype(BF)], True,
                                         "gather_weights")
    ws = (*_w_in_pad(_unshard_cols(g_in)), _w_uq_pad(_unshard_cols(g_uq)), _w_ukv_perm(_unshard_cols(g_ukv)))

    crows = jnp.concatenate([c_all[:, :nb].reshape(N_DEV * nb, D_MODEL), c_ctx[None], jnp.zeros((7, D_MODEL), F32)])
    b_blk = lax.dynamic_slice(b_ada, (0, me * 768), (1, 768))
    (mod_g,) = _exchange([_mod_fwd(crows, w_ada[0], b_blk)], True, "gather_mod")
    mod_all = _unshard_cols(mod_g)
    behind = mod_g[0, 0, 0:1] * 0.0
    st_g = _exchange_start([(w_out[0] + behind).astype(BF), w_ff1[0].astype(BF), w_ff2[0].astype(BF)], True,
                           "gather_ff_start")
    mod_all = mod_all + st_g["token"][0:1, 0:1]
    mod_mine = lax.dynamic_slice(mod_all, (me * nb, 0), (nb, 6 * D_MODEL)).reshape(nb, 6, D_MODEL)
    mod = jnp.pad(mod_mine, ((0, 0), (0, 2), (0, 0)))
    mod_c = jnp.pad(mod_all[16].reshape(1, 6, D_MODEL), ((0, 0), (0, 2), (0, 0)))

    tabs = _rope_tables()
    dec_f = ret_decay_fwd.reshape(N_HEADS, 1, 1)
    dec_b = ret_decay_bwd.reshape(N_HEADS, 1, 1)

    rkc, rvc, k_ctx, v_ctx = _k1_fwd(ctx, mod_c, g_attn, g_q_lora, g_kv_lora, ws, tabs, None, True)
    rq, rk, rv, rg, q, k_all, v_all = _k1_fwd(x, mod, g_attn, g_q_lora, g_kv_lora, ws, tabs, (k_ctx, v_ctx), False)
    o_f, o_b, sf_prev, sb_prev = _k2_fwd(rq, rk, rv, rkc, rvc, dec_f, dec_b)
    y_mla = _k3_fwd(q, k_all, v_all)
    g_out, g_ff1, g_ff2 = _exchange_wait(st_g, y_mla, "gather_ff_wait")
    wo = g_out.reshape(D_MODEL, D_MODEL)
    x_mid, h2 = _k4a_fwd(x, o_f, o_b, rg, y_mla, g_ret, wo, mod, g_ffn)
    dxm, dmlp, loss_acc, dgt_f, dg_final = _k4b_mlp_loss(h2, g_ff1, g_ff2, x_mid, mod, g_final.reshape(1, D_MODEL),
                                                         loss_target)

    dh2, dw1, dw2 = _k4d_mlp_bwd(h2, dmlp, g_ff1, g_ff2)
    st_s = _exchange_start([dw1, dw2], False, "scatter_ff_start")
    g_ret_t = g_ret + st_s["token"][0:1, 0:1]
    dx_res, do, drg, dym, dwo, dg_ret, dg_ffn, dmod_a = _k4e_bwd(x, o_f, o_b, rg, y_mla, g_ret_t, wo, mod, g_ffn, dxm, dh2)
    dq, dk_all, dv_all = _k3_bwd(q, k_all, v_all, dym)
    dqf, dkf, dvf, dqb, dkb, dvb, dkc, dvc, ddf, ddb = _k2_bwd(rq, rk, rv, do, sf_prev, sb_prev, rkc, rvc, dec_f, dec_b)
    cts = [[(dqf, 0), (dqb, 0)], [(dkf, 0), (dkb, 0)], [(dvf, 0), (dvb, 0)], [(drg, 0)], [(dq, 0)],
           [(dk_all, CTX_LEN)], [(dv_all, CTX_LEN)]]
    grad_x, accs, dmod_1 = _k1_bwd(x, mod, g_attn, g_q_lora, g_kv_lora, ws, tabs, cts, dx_res, None, False)
    cts_c = [[(dkc, 0)], [(dvc, 0)], [(dk_all, 0)], [(dv_all, 0)]]
    _, accs, dmod_c1 = _k1_bwd(ctx, mod_c, g_attn, g_q_lora, g_kv_lora, ws, tabs, cts_c, None, accs, True)
    dwa, dwb, dwq, dwk, dg_attn, dg_q, dg_kv = accs

    dmod_loc = (dmod_a + dmod_1).at[:, 5, :].set(dgt_f[:, 0, :])[:, :6, :].reshape(nb, 6 * D_MODEL)
    dmod_ctx = dmod_c1[:, :6, :].reshape(1, 6 * D_MODEL)
    small = {"g_attn": dg_attn, "g_ffn": dg_ffn, "ret_decay_fwd": jnp.sum(ddf[:, :, 0, 0], axis=0),
             "ret_decay_bwd": jnp.sum(ddb[:, :, 0, 0], axis=0), "g_ret": dg_ret, "g_q_lora": dg_q, "g_kv_lora": dg_kv,
             "g_final": dg_final}
    extra = jnp.concatenate([dmod_loc, dmod_ctx, jnp.zeros((5, 6 * D_MODEL), F32)])
    sm_g, ex_g, loss_g = _exchange([_pack_small(small), extra, loss_acc], True, "gather_small")
    dmod_all = ex_g[:, :nb].reshape(N_DEV * nb, 6 * D_MODEL)
    dmodc_parts = ex_g[:, nb]
    dmod_full = jnp.concatenate([dmod_all, jnp.zeros((8, 6 * D_MODEL), F32)])
    dmod_blk = lax.dynamic_slice(dmod_full, (0, me * 768), (24, 768))
    dmodc_blk = lax.dynamic_slice(dmodc_parts, (0, me * 768), (N_DEV, 768))
    gw_ada, gcc_part, gb_ada = _mod_bwd(crows, w_ada[0], dmod_blk, dmodc_blk, dmod_full, dmodc_parts)
    (gcc_g,) = _exchange([gcc_part], True, "gather_c_ctx")

    p_ff1, p_ff2 = _exchange_wait(st_s, dmod_c1, "scatter_ff_wait")
    pieces = _exchange([_shard_cols(_w_in_cut(dwa, dwb)).astype(BF), _shard_cols(_w_uq_cut(dwq)).astype(BF),
                        _shard_cols(_w_ukv_unperm(dwk)).astype(BF), dwo.reshape(N_DEV, 128, D_MODEL).astype(BF)],
                       False, "scatter_grads") + [p_ff1, p_ff2]

    res = {}
    big = (("w_in", w_in, m_w_in, v_w_in, pieces[0]), ("w_uq", w_uq, m_w_uq, v_w_uq, pieces[1]),
           ("w_ukv", w_ukv, m_w_ukv, v_w_ukv, pieces[2]), ("w_out", w_out, m_w_out, v_w_out, pieces[3]),
           ("w_ff1", w_ff1, m_w_ff1, v_w_ff1, pieces[4]), ("w_ff2", w_ff2, m_w_ff2, v_w_ff2, pieces[5]),
           ("w_ada", w_ada, m_w_ada, v_w_ada, gw_ada[None]))
    for name, w, m, v, pcs in big:
        res[name] = [a[None] for a in _adamw(w[0], m[0], v[0], pcs, "adamw_" + name)]

    smalls = {"c_ctx": (c_ctx, m_c_ctx, v_c_ctx), "b_ada": (b_ada, m_b_ada, v_b_ada), "g_attn": (g_attn, m_g_attn, v_g_attn),
              "g_ffn": (g_ffn, m_g_ffn, v_g_ffn), "ret_decay_fwd": (ret_decay_fwd, m_ret_decay_fwd, v_ret_decay_fwd),
              "ret_decay_bwd": (ret_decay_bwd, m_ret_decay_bwd, v_ret_decay_bwd), "g_ret": (g_ret, m_g_ret, v_g_ret),
              "g_q_lora": (g_q_lora, m_g_q_lora, v_g_q_lora), "g_kv_lora": (g_kv_lora, m_g_kv_lora, v_g_kv_lora),
              "g_final": (g_final, m_g_final, v_g_final)}
    rows = {k: tuple(a.reshape(1, -1) for a in t) for k, t in smalls.items()}
    for name, outs in _adamw_small(rows, sm_g, gcc_g, gb_ada).items():
        res[name] = [o.reshape(smalls[name][0].shape) for o in outs]

    loss = loss_g[0, 0, 0]
    for k in range(1, N_DEV):
        loss = loss + loss_g[k, 0, 0]

    order = ("c_ctx", "w_ada", "b_ada", "g_attn", "g_ffn", "w_in", "ret_decay_fwd", "ret_decay_bwd", "g_ret", "g_q_lora",
             "w_uq", "g_kv_lora", "w_ukv", "w_out", "w_ff1", "w_ff2", "g_final")
    return (loss, grad_x, *[res[n][0] for n in order], *[res[n][1] for n in order], *[res[n][2] for n in order],
            *[res[n][3] for n in order])
```

```python
import functools
import math

import jax
import jax.numpy as jnp
from jax import lax
from jax.experimental import pallas as pl
from jax.experimental.pallas import tpu as pltpu

F32 = jnp.float32
BF = jnp.bfloat16
EPS = 1e-6
LANE = 128
N_DEV = 8
D_MODEL = 1024
SEQ = 2048
CTX_LEN = 256
GRID_W = 64
N_HEADS = 4
RET_CHUNK = 512
N_CHUNK = SEQ // RET_CHUNK
D_FF = 4096
FF_BLK = D_FF // N_DEV
IN_PAD = 2816
KV_LEN = CTX_LEN + SEQ
ROPE_BASE = 10000.0
ADAM_LR, ADAM_B1, ADAM_B2, ADAM_EPS, ADAM_WD, ADAM_STEP = 0.001, 0.9, 0.999, 1e-08, 0.01, 10
TOK = 256
TOK_B = 256
VMEM_LIMIT = 56 * 1024 * 1024
ARB = "arbitrary"
MESH = pl.DeviceIdType.MESH
_HEAD_SL = [slice(LANE * h, LANE * (h + 1)) for h in range(N_HEADS)]
W_SHAPES = [(D_MODEL, 2048), (D_MODEL, 768), (384, 1024), (256, 1024)]


def _dot(a, b, ca, cb):
    return lax.dot_general(a.astype(BF), b.astype(BF), (((ca,), (cb,)), ((), ())), preferred_element_type=F32)


@jax.custom_vjp
def mm(a, b):
    return _dot(a, b, 1, 0)


@jax.custom_vjp
def mm_nt(a, b):
    return _dot(a, b, 1, 1)


@jax.custom_vjp
def mm_tn(a, b):
    return _dot(a, b, 0, 0)


mm.defvjp(lambda a, b: (_dot(a, b, 1, 0), (a, b)), lambda r, g: (mm_nt(g, r[1]), mm_tn(r[0], g)))
mm_nt.defvjp(lambda a, b: (_dot(a, b, 1, 1), (a, b)), lambda r, g: (mm(g, r[1]), mm_tn(g, r[0])))
mm_tn.defvjp(lambda a, b: (_dot(a, b, 0, 0), (a, b)), lambda r, g: (mm_nt(r[1], g), mm(r[0], g)))


@jax.custom_vjp
def _mmw(a, w, probe):
    return _dot(a, w, 1, 0)


def _mmw_bwd(r, g):
    a, w = r
    return mm_nt(g, w), jnp.zeros_like(w), mm_tn(a, g)


_mmw.defvjp(lambda a, w, probe: (_dot(a, w, 1, 0), (a, w)), _mmw_bwd)


def mmw(a, w, probe):
    return _dot(a, w, 1, 0) if probe is None else _mmw(a, w, probe)


def rmsn(x, g):
    return x * lax.rsqrt(jnp.mean(x * x, axis=-1, keepdims=True) + EPS) * g


def silu(x):
    return x * jax.nn.sigmoid(x)


def _swap32_impl(x):
    n = x.shape[-1]
    lane = lax.broadcasted_iota(jnp.int32, x.shape, x.ndim - 1) % LANE
    up = pltpu.roll(x, n - 32, x.ndim - 1)
    dn = pltpu.roll(x, 32, x.ndim - 1)
    return jnp.where(lane < 32, up, jnp.where(lane < 64, dn, 0.0))


@jax.custom_vjp
def swap32(x):
    return _swap32_impl(x)


swap32.defvjp(lambda x: (_swap32_impl(x), None), lambda _, g: (_swap32_impl(g),))


def rope(x, cs, sn):
    return x * cs + swap32(x) * sn


def k1_tile(x, sh, sc, g_attn, g_q, g_kv, ws, ps, tabs, is_ctx):
    w_a, w_b, w_uq, w_ukv = ws
    p_a, p_b, p_uq, p_ukv = ps
    cs1, sn1 = tabs
    cs, sn = jnp.concatenate([cs1] * N_HEADS, axis=-1), jnp.concatenate([sn1] * N_HEADS, axis=-1)
    cq_t = jnp.concatenate([jnp.ones_like(cs1), cs1] * N_HEADS, axis=-1)
    sq_t = jnp.concatenate([jnp.zeros_like(sn1), sn1] * N_HEADS, axis=-1)
    h = rmsn(x, g_attn) * (1.0 + sc) + sh
    pa = mmw(h, w_a, p_a)
    pb = mmw(h, w_b, p_b)
    rk = pa[:, 512:1024] * 0.125
    rv = pa[:, 1024:1536]
    kpe = pb[:, 640:768]
    kv = mmw(rmsn(pb[:, 384:640], g_kv), w_ukv, p_ukv)
    if not is_ctx:
        rk = rope(rk, cs, sn)
        kpe = rope(kpe, cs1, sn1)
    k_full = jnp.concatenate([piece for sl in _HEAD_SL for piece in (kv[:, sl], kpe)], axis=-1)
    v = kv[:, 512:]
    if is_ctx:
        return rk, rv, k_full, v
    rq = rope(pa[:, 0:512], cs, sn)
    rg = pa[:, 1536:2048]
    q = rope(mmw(rmsn(pb[:, 0:384], g_q), w_uq, p_uq), cq_t, sq_t)
    return rq, rk, rv, rg, q, k_full, v


def log_sigmoid(x):
    return jnp.minimum(x, 0.0) - jnp.log(1.0 + jnp.exp(-jnp.abs(x)))


def ret_chunk(q, k, v, s, lg, reverse):
    c = RET_CHUNK
    ii = lax.broadcasted_iota(jnp.int32, (c, c), 0).astype(F32)
    jj = lax.broadcasted_iota(jnp.int32, (c, c), 1).astype(F32)
    diff = (jj - ii) if reverse else (ii - jj)
    dec = jnp.where(diff >= 0, jnp.exp(lg * jnp.maximum(diff, 0.0)), 0.0)
    pos = lax.broadcasted_iota(jnp.int32, (c, 1), 0).astype(F32)
    if reverse:
        wk, wq = jnp.exp(lg * pos), jnp.exp(lg * (c - pos))
    else:
        wk, wq = jnp.exp(lg * (c - 1.0 - pos)), jnp.exp(lg * (pos + 1.0))
    o = mm(mm_nt(q, k) * dec, v) + mm(q * wq, s)
    s_next = jnp.exp(lg * float(c)) * s + mm_tn(k * wk, v)
    return o, s_next


def ctx_state(kc, vc, lg, reverse):
    n = kc.shape[0]
    pos = lax.broadcasted_iota(jnp.int32, (n, 1), 0).astype(F32)
    w = jnp.exp(lg * pos) if reverse else jnp.exp(lg * (n - 1.0 - pos))
    return mm_tn(kc * w, vc)


def attn_head(qn, qp, kn, kp, v):
    s = (mm_nt(qn, kn) + mm_nt(qp, kp)) * (1.0 / math.sqrt(192.0))
    e = jnp.exp(s - jnp.max(s, axis=-1, keepdims=True))
    return mm(e / jnp.sum(e, axis=-1, keepdims=True), v)


def gn_gate(o, rg, g_ret):
    ys = []
    for h in range(N_HEADS):
        sl = slice(LANE * h, LANE * (h + 1))
        oh = o[:, sl]
        mu = jnp.mean(oh, axis=-1, keepdims=True)
        var = jnp.mean(jnp.square(oh - mu), axis=-1, keepdims=True)
        ys.append((oh - mu) * lax.rsqrt(var + EPS) * g_ret[:, sl])
    return jnp.concatenate(ys, axis=-1) * silu(rg)


def k4a_tile(x, o_f, o_b, rg, y_mla, g_ret, gt_a, g_ffn, sh_f, sc_f, w_out, p_out):
    mix = jnp.concatenate([gn_gate(o_f + o_b, rg, g_ret), y_mla], axis=-1)
    x_mid = x + gt_a * mmw(mix, w_out, p_out)
    h2 = rmsn(x_mid, g_ffn) * (1.0 + sc_f) + sh_f
    return x_mid, h2


def k4c_tile(x_mid, mlp, gt_f, g_final, tgt):
    y = rmsn(x_mid + gt_f * mlp, g_final)
    per_tok = jnp.mean(jnp.square(y - tgt), axis=-1, keepdims=True)
    return 0.5 * jnp.sum(per_tok, axis=0, keepdims=True)


def _cp(sem=None, vmem=VMEM_LIMIT):
    return pltpu.CompilerParams(dimension_semantics=sem, vmem_limit_bytes=vmem)


def _acc(ref, val, first):
    @pl.when(first)
    def _():
        ref[...] = val

    @pl.when(jnp.logical_not(first))
    def _():
        ref[...] += val


def _full(shape):
    nd = len(shape)
    return pl.BlockSpec(shape, lambda *_: (0,) * nd)


ANY = pl.BlockSpec(memory_space=pl.ANY)


def _sds(shape, dtype=F32):
    return jax.ShapeDtypeStruct(shape, dtype)


def _exchange(arrs, gather, name):
    n = len(arrs)
    out_shape = [_sds(((N_DEV,) + a.shape) if gather else a.shape, a.dtype) for a in arrs]

    def body(*refs):
        ins, outs = refs[:n], refs[n:2 * n]
        send_sems, recv_sems, local_sems = refs[2 * n:]
        x, y, c = lax.axis_index("x"), lax.axis_index("y"), lax.axis_index("c")
        me = 4 * x + 2 * y + c
        sends, recvs, locs = [], [], []
        for i in range(n):
            for k in range(N_DEV - 1):
                bits = k + 1
                px = x ^ ((bits >> 2) & 1)
                py = y ^ ((bits >> 1) & 1)
                pc = c ^ (bits & 1)
                peer = 4 * px + 2 * py + pc
                src = ins[i] if gather else ins[i].at[peer]
                sem = i * (N_DEV - 1) + k
                sends.append(pltpu.make_async_remote_copy(
                    src_ref=src, dst_ref=outs[i].at[me], send_sem=send_sems.at[sem], recv_sem=recv_sems.at[sem],
                    device_id=(px, py, pc), device_id_type=MESH))
                recvs.append(pltpu.make_async_remote_copy(
                    src_ref=src, dst_ref=outs[i].at[peer], send_sem=send_sems.at[sem], recv_sem=recv_sems.at[sem],
                    device_id=(px, py, pc), device_id_type=MESH))
            locs.append(pltpu.make_async_copy(ins[i] if gather else ins[i].at[me], outs[i].at[me], local_sems.at[i]))
        for cp in locs + sends:
            cp.start()
        for cp in recvs:
            cp.wait_recv()
        for cp in sends:
            cp.wait_send()
        for cp in locs:
            cp.wait()

    outs = pl.pallas_call(
        body, name=name, out_shape=out_shape, in_specs=[ANY] * n, out_specs=[ANY] * n,
        scratch_shapes=[pltpu.SemaphoreType.DMA((n * (N_DEV - 1),)), pltpu.SemaphoreType.DMA((n * (N_DEV - 1),)),
                        pltpu.SemaphoreType.DMA((n,))],
    )(*arrs)
    return list(outs)


def _gather_two_level(arrs, name):
    n = len(arrs)

    def body(*refs):
        ins, outs = refs[:n], refs[n:2 * n]
        send_sems, recv_sems, local_sems = refs[2 * n:]
        x, y, c = lax.axis_index("x"), lax.axis_index("y"), lax.axis_index("c")
        sibling = (x, y, 1 - c)
        chips = [(1 - x, y), (x, 1 - y), (1 - x, 1 - y)]

        def slot(px, py, pc):
            return 4 * px + 2 * py + pc

        first, passed, waits, locs = [], [], [], []
        for i in range(n):
            def copy(k, block, to, src=None, i=i):
                dst = outs[i].at[slot(*block)]
                return pltpu.make_async_remote_copy(
                    src_ref=dst if src is None else src, dst_ref=dst, send_sem=send_sems.at[7 * i + k],
                    recv_sem=recv_sems.at[7 * i + k], device_id=to, device_id_type=MESH)

            locs.append(pltpu.make_async_copy(ins[i], outs[i].at[slot(x, y, c)], local_sems.at[i]))
            first.append(copy(0, (x, y, c), sibling, src=ins[i]))
            first += [copy(1 + j, (x, y, c), (*chip, c), src=ins[i]) for j, chip in enumerate(chips)]
            passed.append([copy(4 + j, (*chip, c), sibling) for j, chip in enumerate(chips)])
            waits.append([copy(1 + j, (*chip, c), (x, y, c)) for j, chip in enumerate(chips)])
        for cp in locs + first:
            cp.start()
        for j in range(3):
            for i in range(n):
                waits[i][j].wait_recv()
                passed[i][j].start()
        for i in range(n):
            def arrival(k, block, i=i):
                dst = outs[i].at[slot(*block)]
                return pltpu.make_async_remote_copy(
                    src_ref=dst, dst_ref=dst, send_sem=send_sems.at[7 * i + k], recv_sem=recv_sems.at[7 * i + k],
                    device_id=sibling, device_id_type=MESH)

            arrival(0, (x, y, 1 - c)).wait_recv()
            for j, chip in enumerate(chips):
                arrival(4 + j, (*chip, 1 - c)).wait_recv()
        for cp in first + [p for ps in passed for p in ps]:
            cp.wait_send()
        for cp in locs:
            cp.wait()

    outs = pl.pallas_call(
        body, name=name, out_shape=[_sds((N_DEV,) + a.shape, a.dtype) for a in arrs], in_specs=[ANY] * n,
        out_specs=[ANY] * n,
        scratch_shapes=[pltpu.SemaphoreType.DMA((7 * n,)), pltpu.SemaphoreType.DMA((7 * n,)),
                        pltpu.SemaphoreType.DMA((n,))],
    )(*arrs)
    return list(outs)


HBM = pl.BlockSpec(memory_space=pltpu.HBM)
SEM = pl.BlockSpec(memory_space=pltpu.SEMAPHORE)
EFFECT = pltpu.SideEffectType.DATAFLOW_SIDE_EFFECTING


def _peer(k):
    x, y, c = lax.axis_index("x"), lax.axis_index("y"), lax.axis_index("c")
    bits = k + 1
    px, py, pc = x ^ ((bits >> 2) & 1), y ^ ((bits >> 1) & 1), c ^ (bits & 1)
    return (px, py, pc), 4 * px + 2 * py + pc, 4 * x + 2 * y + c


def _exchange_start(arrs, gather, name):
    n = len(arrs)
    lands = [pltpu.with_memory_space_constraint(lax.empty(((N_DEV,) + a.shape) if gather else a.shape, a.dtype),
                                                pltpu.HBM) for a in arrs]
    srcs = [pltpu.with_memory_space_constraint(a, pltpu.HBM) for a in arrs]

    def body(*refs):
        ins, zones = refs[:n], refs[n:2 * n]
        send_sems, recv_sems, local_sems = refs[2 * n:2 * n + 3]
        token = refs[-1]
        for i in range(n):
            for k in range(N_DEV - 1):
                dev, peer, me = _peer(k)
                sem = i * (N_DEV - 1) + k
                pltpu.make_async_remote_copy(
                    src_ref=ins[i] if gather else ins[i].at[peer], dst_ref=zones[i].at[me],
                    send_sem=send_sems.at[sem], recv_sem=recv_sems.at[sem], device_id=dev, device_id_type=MESH).start()
            _, _, me = _peer(0)
            pltpu.make_async_copy(ins[i] if gather else ins[i].at[me], zones[i].at[me], local_sems.at[i]).start()
        token[...] = jnp.zeros_like(token)

    nsem = n * (N_DEV - 1)
    outs = pl.pallas_call(
        body, name=name,
        out_shape=[pltpu.SemaphoreType.DMA((nsem,)), pltpu.SemaphoreType.DMA((nsem,)), pltpu.SemaphoreType.DMA((n,))]
        + [pltpu.HBM(a.shape, a.dtype) for a in srcs] + [pltpu.HBM(z.shape, z.dtype) for z in lands]
        + [_sds((8, LANE))],
        in_specs=[HBM] * (2 * n),
        out_specs=[SEM, SEM, SEM] + [HBM] * (2 * n) + [pl.BlockSpec(memory_space=pltpu.VMEM)],
        input_output_aliases={i: 3 + i for i in range(2 * n)},
        compiler_params=pltpu.CompilerParams(has_side_effects=EFFECT),
    )(*srcs, *lands)
    return {"n": n, "gather": gather, "sems": outs[:3], "srcs": outs[3:3 + n], "lands": outs[3 + n:3 + 2 * n],
            "token": outs[-1]}


def _exchange_wait(st, after, name):
    n, gather = st["n"], st["gather"]

    def body(*refs):
        ins, zones = refs[:n], refs[n:2 * n]
        send_sems, recv_sems, local_sems = refs[2 * n:2 * n + 3]
        for i in range(n):
            for k in range(N_DEV - 1):
                dev, peer, me = _peer(k)
                sem = i * (N_DEV - 1) + k
                src = ins[i] if gather else ins[i].at[peer]
                cp = pltpu.make_async_remote_copy(
                    src_ref=src, dst_ref=zones[i].at[peer], send_sem=send_sems.at[sem], recv_sem=recv_sems.at[sem],
                    device_id=dev, device_id_type=MESH)
                cp.wait_send()
                cp.wait_recv()
            _, _, me = _peer(0)
            pltpu.make_async_copy(ins[i] if gather else ins[i].at[me], zones[i].at[me], local_sems.at[i]).wait()

    outs = pl.pallas_call(
        body, name=name,
        out_shape=[pltpu.HBM(a.shape, a.dtype) for a in st["srcs"]] + [pltpu.HBM(z.shape, z.dtype) for z in st["lands"]],
        in_specs=[HBM] * (2 * n) + [SEM, SEM, SEM, ANY], out_specs=[HBM] * (2 * n),
        input_output_aliases={i: i for i in range(2 * n)},
        compiler_params=pltpu.CompilerParams(has_side_effects=EFFECT),
    )(*st["srcs"], *st["lands"], *st["sems"], after)
    return list(outs[n:])


def _mod_fwd(crows, w_ada, b_blk):
    def body(c_ref, w_ref, b_ref, o_ref):
        o_ref[...] = mm(silu(c_ref[...]), w_ref[...]) + b_ref[...]

    return pl.pallas_call(body, name="mod_fwd", out_shape=_sds((24, 768)), compiler_params=_cp())(crows, w_ada, b_blk)


def _mod_bwd(crows, w_ada, dmod_blk, dmodc_blk, dmod_full, dmodc_full):
    def body(c_ref, w_ref, d_ref, dc_ref, df_ref, dcf_ref, gw_ref, gc_ref, gb_ref):
        cr = c_ref[...]
        dc, dcf = dc_ref[0:1, :], dcf_ref[0:1, :]
        for p in range(1, N_DEV):
            dc = dc + dc_ref[p:p + 1, :]
            dcf = dcf + dcf_ref[p:p + 1, :]
        row = lax.broadcasted_iota(jnp.int32, (24, 1), 0)
        gw_ref[...] = mm_tn(silu(cr), jnp.where(row == 16, dc, d_ref[...]))
        cc = cr[16:17, :]
        sg = jax.nn.sigmoid(cc)
        part = mm_nt(jnp.broadcast_to(dc, (8, 768)), w_ref[...])
        gc_ref[...] = part * (sg * (1.0 + cc * (1.0 - sg)))
        gb_ref[...] = jnp.sum(df_ref[...], axis=0, keepdims=True) + dcf

    return pl.pallas_call(
        body, name="mod_bwd", out_shape=[_sds((D_MODEL, 768)), _sds((8, D_MODEL)), _sds((1, 6 * D_MODEL))],
        compiler_params=_cp())(crows, w_ada, dmod_blk, dmodc_blk, dmod_full, dmodc_full)


def _tab_specs(tk):
    return [pl.BlockSpec((tk, LANE), lambda i, t: (t, 0))] * 2


def _k1_fwd(x, mod, g_attn, g_q, g_kv, ws, tabs, kv_all, is_ctx):
    b, l, _ = x.shape
    nt = l // TOK
    n_f32 = 2 if is_ctx else 4

    def body(x_ref, mod_ref, ga_ref, gq_ref, gk_ref, wa_ref, wb_ref, wq_ref, wk_ref, cs_ref, sn_ref, *rest):
        outs = rest if is_ctx else rest[2:]
        res = k1_tile(x_ref[...], mod_ref[0:1, :], mod_ref[1:2, :], ga_ref[...], gq_ref[...], gk_ref[...],
                      (wa_ref[...], wb_ref[...], wq_ref[...], wk_ref[...]), (None,) * 4,
                      (cs_ref[...], sn_ref[...]), is_ctx)
        for o_ref, r in zip(outs, res):
            o_ref[...] = r.astype(o_ref.dtype)

    tok = lambda w, off=0: pl.BlockSpec((None, TOK, w), lambda i, t: (i, t + off, 0))
    mod_spec = pl.BlockSpec((None, 8, D_MODEL), (lambda i, t: (0, 0, 0)) if is_ctx else (lambda i, t: (i, 0, 0)))
    kv_off = 0 if is_ctx else CTX_LEN // TOK
    in_specs = ([tok(D_MODEL), mod_spec, _full((1, D_MODEL)), _full((1, 384)), _full((1, 256))]
                + [_full(s) for s in W_SHAPES] + _tab_specs(TOK))
    args = [x, mod, g_attn, g_q, g_kv, *ws, *tabs]
    out_specs = [tok(512)] * n_f32 + ([] if is_ctx else [tok(1024)]) + [tok(1024, kv_off), tok(512, kv_off)]
    out_shape = ([_sds((b, l, 512))] * n_f32 + ([] if is_ctx else [_sds((b, l, 1024), BF)])
                 + [_sds((b, KV_LEN, 1024), BF), _sds((b, KV_LEN, 512), BF)])
    aliases = {}
    if not is_ctx:
        aliases = {len(args): n_f32 + 1, len(args) + 1: n_f32 + 2}
        in_specs += [ANY, ANY]
        args += list(kv_all)
    return pl.pallas_call(
        body, name="k1_fwd_ctx" if is_ctx else "k1_fwd", grid=(b, nt), in_specs=in_specs, out_specs=out_specs,
        out_shape=out_shape, input_output_aliases=aliases, compiler_params=_cp((ARB, ARB)),
    )(*args)


N_ACC = 7


def _k1_bwd(x, mod, g_attn, g_q, g_kv, ws, tabs, cts, dx_res, init, is_ctx):
    b, l, _ = x.shape
    tk = TOK_B
    nt = l // tk
    flat_cts = [a for group in cts for a in group]
    group_sizes = [len(g) for g in cts]
    n_ct = len(flat_cts)
    has_res = dx_res is not None
    has_init = init is not None
    acc_shapes = W_SHAPES + [(1, D_MODEL), (1, 384), (1, 256)]

    def body(*refs):
        it = iter(refs)
        x_ref, mod_ref, ga_ref, gq_ref, gk_ref = [next(it) for _ in range(5)]
        w_hbm = [next(it) for _ in range(4)]
        tab_refs = [next(it) for _ in range(2)]
        ct_refs = [next(it) for _ in range(n_ct)]
        res_ref = next(it) if has_res else None
        init_refs = [next(it) for _ in range(N_ACC)] if has_init else None
        gx_ref = next(it) if not is_ctx else None
        out_hbm = [next(it) for _ in range(N_ACC)]
        dmod_ref = next(it)
        w_vmem = [next(it) for _ in range(4)]
        accs = [next(it) for _ in range(N_ACC)]
        sem = next(it)
        i, t = pl.program_id(0), pl.program_id(1)
        first = jnp.logical_and(i == 0, t == 0)
        last = jnp.logical_and(i == b - 1, t == nt - 1)

        @pl.when(first)
        def _():
            for src, dst in zip(w_hbm, w_vmem):
                pltpu.sync_copy(src, dst)
            for k in range(N_ACC):
                if has_init:
                    pltpu.sync_copy(init_refs[k], accs[k])
                else:
                    accs[k][...] = jnp.zeros(acc_shapes[k], F32)

        ct_vals, pos = [], 0
        for gsz in group_sizes:
            v = ct_refs[pos][...].astype(F32)
            for r in ct_refs[pos + 1:pos + gsz]:
                v = v + r[...]
            ct_vals.append(v)
            pos += gsz
        wv = tuple(r[...] for r in w_vmem)
        tv = tuple(r[...] for r in tab_refs)

        def f(xv, sh, sc, ga, gq, gk, *probes):
            return k1_tile(xv, sh, sc, ga, gq, gk, wv, probes, tv, is_ctx)

        probes = [jnp.zeros(s, F32) for s in W_SHAPES]
        _, vjp = jax.vjp(f, x_ref[...], mod_ref[0:1, :], mod_ref[1:2, :], ga_ref[...], gq_ref[...], gk_ref[...], *probes)
        dx, dsh, dsc, dga, dgq, dgk, dwa, dwb, dwq, dwk = vjp(tuple(ct_vals))
        if not is_ctx:
            gx_ref[...] = dx + res_ref[...] if has_res else dx
        for ref, val in zip(accs, (dwa, dwb, dwq, dwk, dga, dgq, dgk)):
            ref[...] += val
        t0 = first if is_ctx else t == 0
        _acc(dmod_ref.at[0:1, :], dsh, t0)
        _acc(dmod_ref.at[1:2, :], dsc, t0)

        @pl.when(t0)
        def _():
            dmod_ref[2:8, :] = jnp.zeros((6, D_MODEL), F32)

        @pl.when(last)
        def _():
            cps = [pltpu.make_async_copy(accs[k], out_hbm[k], sem.at[k]) for k in range(N_ACC)]
            for cp in cps:
                cp.start()
            for cp in cps:
                cp.wait()

    tok = lambda w, off=0: pl.BlockSpec((None, tk, w), lambda i, t: (i, t + off, 0))
    mod_spec = pl.BlockSpec((None, 8, D_MODEL), (lambda i, t: (0, 0, 0)) if is_ctx else (lambda i, t: (i, 0, 0)))
    in_specs = ([tok(D_MODEL), mod_spec, _full((1, D_MODEL)), _full((1, 384)), _full((1, 256))] + [ANY] * 4
                + _tab_specs(tk))
    args = [x, mod, g_attn, g_q, g_kv, *ws, *tabs]
    for a, off in flat_cts:
        in_specs.append(tok(a.shape[-1], off // tk))
        args.append(a)
    if has_res:
        in_specs.append(tok(D_MODEL))
        args.append(dx_res)
    if has_init:
        in_specs += [ANY] * N_ACC
        args += list(init)
    out_shape, out_specs = [], []
    if not is_ctx:
        out_shape.append(_sds((b, l, D_MODEL)))
        out_specs.append(tok(D_MODEL))
    out_shape += [_sds(s) for s in acc_shapes] + [_sds((1 if is_ctx else b, 8, D_MODEL))]
    out_specs += [ANY] * N_ACC + [mod_spec]
    outs = pl.pallas_call(
        body, name="k1_bwd_ctx" if is_ctx else "k1_bwd", grid=(b, nt), in_specs=in_specs, out_specs=out_specs,
        out_shape=out_shape,
        scratch_shapes=[pltpu.VMEM(s, BF) for s in W_SHAPES] + [pltpu.VMEM(s, F32) for s in acc_shapes]
        + [pltpu.SemaphoreType.DMA((N_ACC,))],
        compiler_params=_cp((ARB, ARB)),
    )(*args)
    outs = list(outs)
    gx = None if is_ctx else outs.pop(0)
    return gx, outs[:N_ACC], outs[N_ACC]


def _chunk_spec(rev):
    if rev:
        return pl.BlockSpec((None, RET_CHUNK, 512), lambda i, n: (i, N_CHUNK - 1 - n, 0))
    return pl.BlockSpec((None, RET_CHUNK, 512), lambda i, n: (i, n, 0))


def _state_spec(rev):
    if rev:
        return pl.BlockSpec((None, N_HEADS, None, LANE, LANE), lambda i, n: (i, 0, N_CHUNK - 1 - n, 0, 0))
    return pl.BlockSpec((None, N_HEADS, None, LANE, LANE), lambda i, n: (i, 0, n, 0, 0))


_CTX_SPEC = pl.BlockSpec((None, CTX_LEN, 512), lambda i, n: (i, 0, 0))
_DEC_SPEC = pl.BlockSpec((N_HEADS, 1, 1), lambda i, n: (0, 0, 0))


def _k2_fwd(rq, rk, rv, rkc, rvc, dec_f, dec_b):
    b = rq.shape[0]

    def body(qf, kf, vf, qb, kb, vb, kc, vc, df, db, of_ref, ob_ref, sf_out, sb_out, sf, sb):
        n = pl.program_id(1)
        for h, sl in enumerate(_HEAD_SL):
            lgf, lgb = log_sigmoid(df[h]), log_sigmoid(db[h])

            @pl.when(n == 0)
            def _():
                sf[h] = ctx_state(kc[:, sl], vc[:, sl], lgf, False)
                sb[h] = ctx_state(kc[:, sl], vc[:, sl], lgb, True)

            sf_out[h] = sf[h]
            sb_out[h] = sb[h]
            o, s = ret_chunk(qf[:, sl], kf[:, sl], vf[:, sl], sf[h], lgf, False)
            of_ref[:, sl] = o
            sf[h] = s
            o, s = ret_chunk(qb[:, sl], kb[:, sl], vb[:, sl], sb[h], lgb, True)
            ob_ref[:, sl] = o
            sb[h] = s

    l = rq.shape[1]
    return pl.pallas_call(
        body, name="k2_fwd", grid=(b, N_CHUNK),
        in_specs=[_chunk_spec(False)] * 3 + [_chunk_spec(True)] * 3 + [_CTX_SPEC, _CTX_SPEC, _DEC_SPEC, _DEC_SPEC],
        out_specs=[_chunk_spec(False), _chunk_spec(True), _state_spec(False), _state_spec(True)],
        out_shape=[_sds((b, l, 512)), _sds((b, l, 512)), _sds((b, N_HEADS, N_CHUNK, LANE, LANE)),
                   _sds((b, N_HEADS, N_CHUNK, LANE, LANE))],
        scratch_shapes=[pltpu.VMEM((N_HEADS, LANE, LANE), F32), pltpu.VMEM((N_HEADS, LANE, LANE), F32)],
        compiler_params=_cp((ARB, ARB)),
    )(rq, rk, rv, rq, rk, rv, rkc, rvc, dec_f, dec_b)


def _k2_bwd(rq, rk, rv, do, sf_prev, sb_prev, rkc, rvc, dec_f, dec_b):
    b, l, _ = rq.shape

    def body(qf, kf, vf, gf, spf, qb, kb, vb, gb, spb, kc, vc, df, db,
             dqf, dkf, dvf, dqb, dkb, dvb, dkc, dvc, ddf, ddb, dsf, dsb):
        n = pl.program_id(1)

        @pl.when(n == 0)
        def _():
            dsf[...] = jnp.zeros((N_HEADS, LANE, LANE), F32)
            dsb[...] = jnp.zeros((N_HEADS, LANE, LANE), F32)

        def one(h, sl, q, k, v, g, sp, dec, ds, dq, dk, dv, dd, rev):
            def f(qv, kv_, vv, sv, dcy):
                return ret_chunk(qv, kv_, vv, sv, log_sigmoid(dcy), rev)

            _, vjp = jax.vjp(f, q[:, sl], k[:, sl], v[:, sl], sp[h], dec[h])
            gq, gk, gv, gs, gd = vjp((g[:, sl], ds[h]))
            dq[:, sl] = gq
            dk[:, sl] = gk
            dv[:, sl] = gv
            ds[h] = gs
            _acc(dd.at[h], jnp.broadcast_to(gd, (8, LANE)), n == 0)

        for h, sl in enumerate(_HEAD_SL):
            one(h, sl, qf, kf, vf, gf, spf, df, dsf, dqf, dkf, dvf, ddf, False)
            one(h, sl, qb, kb, vb, gb, spb, db, dsb, dqb, dkb, dvb, ddb, True)

        @pl.when(n == N_CHUNK - 1)
        def _():
            def f(kcv, vcv, dcy, rev):
                return ctx_state(kcv, vcv, log_sigmoid(dcy), rev)

            for h, sl in enumerate(_HEAD_SL):
                _, vjp_f = jax.vjp(functools.partial(f, rev=False), kc[:, sl], vc[:, sl], df[h])
                gk_f, gv_f, gd_f = vjp_f(dsf[h])
                _, vjp_b = jax.vjp(functools.partial(f, rev=True), kc[:, sl], vc[:, sl], db[h])
                gk_b, gv_b, gd_b = vjp_b(dsb[h])
                dkc[:, sl] = gk_f + gk_b
                dvc[:, sl] = gv_f + gv_b
                ddf[h] += jnp.broadcast_to(gd_f, (8, LANE))
                ddb[h] += jnp.broadcast_to(gd_b, (8, LANE))

    dd_spec = pl.BlockSpec((None, N_HEADS, 8, LANE), lambda i, n: (i, 0, 0, 0))
    return pl.pallas_call(
        body, name="k2_bwd", grid=(b, N_CHUNK),
        in_specs=[_chunk_spec(True)] * 4 + [_state_spec(True)] + [_chunk_spec(False)] * 4 + [_state_spec(False)]
        + [_CTX_SPEC, _CTX_SPEC, _DEC_SPEC, _DEC_SPEC],
        out_specs=[_chunk_spec(True)] * 3 + [_chunk_spec(False)] * 3 + [_CTX_SPEC, _CTX_SPEC, dd_spec, dd_spec],
        out_shape=[_sds((b, l, 512))] * 6 + [_sds((b, CTX_LEN, 512))] * 2 + [_sds((b, N_HEADS, 8, LANE))] * 2,
        scratch_shapes=[pltpu.VMEM((N_HEADS, LANE, LANE), F32), pltpu.VMEM((N_HEADS, LANE, LANE), F32)],
        compiler_params=_cp((ARB, ARB)),
    )(rq, rk, rv, do, sf_prev, rq, rk, rv, do, sb_prev, rkc, rvc, dec_f, dec_b)


TQ = 512
QK_W = 2 * LANE
_Q_PARTS = [slice(0, TQ // 2), slice(TQ // 2, TQ)]


def _softmax_parts(q, k):
    s = _dot(q, k, 1, 1) * (1.0 / math.sqrt(192.0))
    e = jnp.exp(s - jnp.max(s, axis=-1, keepdims=True))
    return e, 1.0 / jnp.sum(e, axis=-1, keepdims=True)


def _k3_specs():
    qs = lambda w: pl.BlockSpec((None, TQ, w), lambda i, h, t: (i, t, h))
    ks = lambda w: pl.BlockSpec((None, KV_LEN, w), lambda i, h, t: (i, 0, h))
    return qs, ks


def _k3_fwd(q, k, v):
    b, l, _ = q.shape

    def body(q_ref, k_ref, v_ref, o_ref):
        kv_, vv = k_ref[...], v_ref[...]
        for r in _Q_PARTS:
            e, inv = _softmax_parts(q_ref[r, :], kv_)
            o_ref[r, :] = _dot(e, vv, 1, 0) * inv

    qs, ks = _k3_specs()
    return pl.pallas_call(
        body, name="k3_fwd", grid=(b, N_HEADS, l // TQ), in_specs=[qs(QK_W), ks(QK_W), ks(LANE)], out_specs=qs(LANE),
        out_shape=_sds((b, l, N_HEADS * LANE)), compiler_params=_cp((ARB, ARB, ARB)),
    )(q, k, v)


def _k3_bwd(q, k, v, dy):
    b, l, _ = q.shape
    scale = 1.0 / math.sqrt(192.0)

    def body(q_ref, k_ref, v_ref, dy_ref, dq_ref, dk_ref, dv_ref):
        t0 = pl.program_id(2) == 0
        kv_, vv = k_ref[...], v_ref[...]
        dvs, dks = [], []
        for r in _Q_PARTS:
            qv = q_ref[r, :]
            g = dy_ref[r, :].astype(BF)
            e, inv = _softmax_parts(qv, kv_)
            p = e * inv
            dp = _dot(g, vv, 1, 1)
            ds = (p * (dp - jnp.sum(dp * p, axis=-1, keepdims=True)) * scale).astype(BF)
            dvs.append(_dot(p, g, 0, 0))
            dq_ref[r, :] = _dot(ds, kv_, 1, 0)
            dks.append(_dot(ds, qv, 0, 0))
        _acc(dv_ref, sum(dvs[1:], dvs[0]), t0)
        _acc(dk_ref, sum(dks[1:], dks[0]), t0)

    qs, ks = _k3_specs()
    return pl.pallas_call(
        body, name="k3_bwd", grid=(b, N_HEADS, l // TQ), in_specs=[qs(QK_W), ks(QK_W), ks(LANE), qs(LANE)],
        out_specs=[qs(QK_W), ks(QK_W), ks(LANE)],
        out_shape=[_sds((b, l, N_HEADS * QK_W)), _sds((b, KV_LEN, N_HEADS * QK_W)), _sds((b, KV_LEN, N_HEADS * LANE))],
        compiler_params=_cp((ARB, ARB, ARB)),
    )(q, k, v, dy)


def _mod_rows(mod_ref, rows):
    return [mod_ref[r:r + 1, :] for r in rows]


def _k4a_fwd(x, o_f, o_b, rg, y_mla, g_ret, w_out, mod, g_ffn):
    b, l, _ = x.shape

    def body(x_ref, of_ref, ob_ref, rg_ref, ym_ref, gr_ref, wo_ref, mod_ref, gf_ref, xm_ref, h2_ref):
        gt_a, sh_f, sc_f = _mod_rows(mod_ref, (2, 3, 4))
        x_mid, h2 = k4a_tile(x_ref[...], of_ref[...], ob_ref[...], rg_ref[...], ym_ref[...], gr_ref[...], gt_a,
                             gf_ref[...], sh_f, sc_f, wo_ref[...], None)
        xm_ref[...] = x_mid
        h2_ref[...] = h2.astype(BF)

    tok = lambda w: pl.BlockSpec((None, TOK, w), lambda i, t: (i, t, 0))
    mod_spec = pl.BlockSpec((None, 8, D_MODEL), lambda i, t: (i, 0, 0))
    return pl.pallas_call(
        body, name="k4a_fwd", grid=(b, l // TOK),
        in_specs=[tok(D_MODEL), tok(512), tok(512), tok(512), tok(512), _full((1, 512)), _full((D_MODEL, D_MODEL)),
                  mod_spec, _full((1, D_MODEL))],
        out_specs=[tok(D_MODEL), tok(D_MODEL)], out_shape=[_sds((b, l, D_MODEL)), _sds((b, l, D_MODEL), BF)],
        compiler_params=_cp((ARB, ARB)),
    )(x, o_f, o_b, rg, y_mla, g_ret, w_out, mod, g_ffn)


TOK_M = 512


def _last_j(j, idx):
    return jnp.where(j == N_DEV - 1, idx, 0)


def _k4b_mlp_loss(h2, w1, w2, x_mid, mod, g_final, tgt):
    b, l, _ = h2.shape
    nt = l // TOK_M

    def body(h2_ref, w1_ref, w2_ref, xm_ref, mod_ref, gfin_ref, tgt_ref, dxm_ref, dmlp_ref, loss_ref, dgt_ref,
             dgfin_ref, acc):
        j, i, t = pl.program_id(0), pl.program_id(1), pl.program_id(2)
        rows = pl.ds(pl.multiple_of((i * nt + t) * TOK_M, TOK_M), TOK_M)
        a = _dot(h2_ref[...], w1_ref[...], 1, 0)
        part = _dot(jnp.square(jnp.maximum(a, 0.0)), w2_ref[...], 1, 0)
        _acc(acc.at[rows, :], part, j == 0)

        @pl.when(j == N_DEV - 1)
        def _():
            (gt_f,) = _mod_rows(mod_ref, (5,))
            loss, vjp = jax.vjp(k4c_tile, xm_ref[...], acc[rows, :], gt_f, gfin_ref[...], tgt_ref[...])
            dxm, dmlp, dgt, dgfin, _ = vjp(jnp.ones((1, 1), F32))
            dxm_ref[...] = dxm
            dmlp_ref[...] = dmlp.astype(BF)
            first = jnp.logical_and(i == 0, t == 0)
            _acc(loss_ref, jnp.broadcast_to(loss, (8, LANE)), first)
            _acc(dgfin_ref, dgfin, first)
            _acc(dgt_ref, dgt, t == 0)

    tok = lambda w: pl.BlockSpec((None, TOK_M, w), lambda j, i, t: (i, t, 0))
    tok_last = lambda w: pl.BlockSpec((None, TOK_M, w), lambda j, i, t: (_last_j(j, i), _last_j(j, t), 0))
    return pl.pallas_call(
        body, name="k4b_mlp_loss", grid=(N_DEV, b, nt),
        in_specs=[tok(D_MODEL), pl.BlockSpec((None, D_MODEL, FF_BLK), lambda j, i, t: (j, 0, 0)),
                  pl.BlockSpec((None, FF_BLK, D_MODEL), lambda j, i, t: (j, 0, 0)), tok_last(D_MODEL),
                  pl.BlockSpec((None, 8, D_MODEL), lambda j, i, t: (i, 0, 0)),
                  pl.BlockSpec((1, D_MODEL), lambda j, i, t: (0, 0)), tok_last(D_MODEL)],
        out_specs=[tok_last(D_MODEL), tok_last(D_MODEL), pl.BlockSpec((8, LANE), lambda j, i, t: (0, 0)),
                   pl.BlockSpec((None, 1, D_MODEL), lambda j, i, t: (_last_j(j, i), 0, 0)),
                   pl.BlockSpec((1, D_MODEL), lambda j, i, t: (0, 0))],
        out_shape=[_sds((b, l, D_MODEL)), _sds((b, l, D_MODEL), BF), _sds((8, LANE)), _sds((b, 1, D_MODEL)),
                   _sds((1, D_MODEL))],
        scratch_shapes=[pltpu.VMEM((b * l, D_MODEL), F32)],
        compiler_params=_cp((ARB, ARB, ARB)),
    )(h2, w1, w2, x_mid, mod, g_final, tgt)


def _k4d_mlp_bwd(h2, dmlp, w1, w2):
    b, l, _ = h2.shape
    nt = l // TOK_M

    def body(h2_ref, dm_ref, w1_ref, w2_ref, dh2_ref, dw1_ref, dw2_ref, acc1, acc2, dh2s):
        j, i, t = pl.program_id(0), pl.program_id(1), pl.program_id(2)
        first = jnp.logical_and(i == 0, t == 0)
        rows = pl.ds(pl.multiple_of((i * nt + t) * TOK_M, TOK_M), TOK_M)
        h2v, dm = h2_ref[...], dm_ref[...]
        r = jnp.maximum(_dot(h2v, w1_ref[...], 1, 0), 0.0)
        da = _dot(dm, w2_ref[...], 1, 1) * (2.0 * r)
        _acc(acc2, _dot(jnp.square(r), dm, 0, 0), first)
        _acc(acc1, _dot(h2v, da, 0, 0), first)
        _acc(dh2s.at[rows, :], _dot(da, w1_ref[...], 1, 1), j == 0)

        @pl.when(jnp.logical_and(i == b - 1, t == nt - 1))
        def _():
            dw1_ref[...] = acc1[...].astype(BF)
            dw2_ref[...] = acc2[...].astype(BF)

        @pl.when(j == N_DEV - 1)
        def _():
            dh2_ref[...] = dh2s[rows, :]

    tok = lambda w: pl.BlockSpec((None, TOK_M, w), lambda j, i, t: (i, t, 0))
    return pl.pallas_call(
        body, name="k4d_mlp_bwd", grid=(N_DEV, b, nt),
        in_specs=[tok(D_MODEL), tok(D_MODEL), pl.BlockSpec((None, D_MODEL, FF_BLK), lambda j, i, t: (j, 0, 0)),
                  pl.BlockSpec((None, FF_BLK, D_MODEL), lambda j, i, t: (j, 0, 0))],
        out_specs=[pl.BlockSpec((None, TOK_M, D_MODEL), lambda j, i, t: (_last_j(j, i), _last_j(j, t), 0)),
                   pl.BlockSpec((None, D_MODEL, FF_BLK), lambda j, i, t: (j, 0, 0)),
                   pl.BlockSpec((None, FF_BLK, D_MODEL), lambda j, i, t: (j, 0, 0))],
        out_shape=[_sds((b, l, D_MODEL)), _sds((N_DEV, D_MODEL, FF_BLK), BF), _sds((N_DEV, FF_BLK, D_MODEL), BF)],
        scratch_shapes=[pltpu.VMEM((D_MODEL, FF_BLK), F32), pltpu.VMEM((FF_BLK, D_MODEL), F32),
                        pltpu.VMEM((b * l, D_MODEL), F32)],
        compiler_params=_cp((ARB, ARB, ARB)),
    )(h2, dmlp, w1, w2)


def _k4e_bwd(x, o_f, o_b, rg, y_mla, g_ret, w_out, mod, g_ffn, dxm, dh2):
    b, l, _ = x.shape

    def body(x_ref, of_ref, ob_ref, rg_ref, ym_ref, gr_ref, wo_ref, mod_ref, gf_ref, dxm_ref, dh2_ref,
             dx_ref, do_ref, drg_ref, dym_ref, dwo_ref, dgr_ref, dgf_ref, dmod_ref):
        i, t = pl.program_id(0), pl.program_id(1)
        first = jnp.logical_and(i == 0, t == 0)
        gt_a, sh_f, sc_f = _mod_rows(mod_ref, (2, 3, 4))
        wo = wo_ref[...]

        def f(xv, ofv, rgv, ymv, grv, gta, gfv, shf, scf, p_out):
            return k4a_tile(xv, ofv, ob_ref[...], rgv, ymv, grv, gta, gfv, shf, scf, wo, p_out)

        _, vjp = jax.vjp(f, x_ref[...], of_ref[...], rg_ref[...], ym_ref[...], gr_ref[...], gt_a, gf_ref[...], sh_f,
                         sc_f, jnp.zeros((D_MODEL, D_MODEL), F32))
        dx, do, drg, dym, dgr, dgta, dgf, dshf, dscf, dwo = vjp((dxm_ref[...], dh2_ref[...]))
        dx_ref[...] = dx
        do_ref[...] = do
        drg_ref[...] = drg
        dym_ref[...] = dym
        _acc(dwo_ref, dwo, first)
        _acc(dgr_ref, dgr, first)
        _acc(dgf_ref, dgf, first)
        t0 = t == 0
        _acc(dmod_ref.at[2:3, :], dgta, t0)
        _acc(dmod_ref.at[3:4, :], dshf, t0)
        _acc(dmod_ref.at[4:5, :], dscf, t0)

        @pl.when(t0)
        def _():
            dmod_ref[0:2, :] = jnp.zeros((2, D_MODEL), F32)
            dmod_ref[5:8, :] = jnp.zeros((3, D_MODEL), F32)

    tok = lambda w: pl.BlockSpec((None, TOK_B, w), lambda i, t: (i, t, 0))
    mod_spec = pl.BlockSpec((None, 8, D_MODEL), lambda i, t: (i, 0, 0))
    return pl.pallas_call(
        body, name="k4e_bwd", grid=(b, l // TOK_B),
        in_specs=[tok(D_MODEL), tok(512), tok(512), tok(512), tok(512), _full((1, 512)), _full((D_MODEL, D_MODEL)),
                  mod_spec, _full((1, D_MODEL)), tok(D_MODEL), tok(D_MODEL)],
        out_specs=[tok(D_MODEL), tok(512), tok(512), tok(512), _full((D_MODEL, D_MODEL)), _full((1, 512)),
                   _full((1, D_MODEL)), mod_spec],
        out_shape=[_sds((b, l, D_MODEL)), _sds((b, l, 512)), _sds((b, l, 512)), _sds((b, l, 512)),
                   _sds((D_MODEL, D_MODEL)), _sds((1, 512)), _sds((1, D_MODEL)), _sds((b, 8, D_MODEL))],
        compiler_params=_cp((ARB, ARB)),
    )(x, o_f, o_b, rg, y_mla, g_ret, w_out, mod, g_ffn, dxm, dh2)


def _adamw(w, m, v, pieces, name, after=None):
    r, c = w.shape
    npc = pieces.shape[0]
    rb = r
    for cand in (256, 128, 64, 32, 16, 8):
        if r > cand and r % cand == 0 and cand * c * 4 * (npc + 7) * 2 <= 24 * 1024 * 1024:
            rb = cand
            break

    def body(w_ref, m_ref, v_ref, p_ref, *rest):
        g_ref, d_ref, nm_ref, nv_ref = rest[-4:]
        g = p_ref[0].astype(F32)
        for k in range(1, npc):
            g = g + p_ref[k].astype(F32)
        wv = w_ref[...]
        mn = ADAM_B1 * m_ref[...] + (1.0 - ADAM_B1) * g
        vn = ADAM_B2 * v_ref[...] + (1.0 - ADAM_B2) * jnp.square(g)
        m_hat = mn / (1.0 - ADAM_B1 ** ADAM_STEP)
        v_hat = vn / (1.0 - ADAM_B2 ** ADAM_STEP)
        g_ref[...] = g
        d_ref[...] = -ADAM_LR * (m_hat / (jnp.sqrt(v_hat) + ADAM_EPS) + ADAM_WD * wv)
        nm_ref[...] = mn
        nv_ref[...] = vn

    blk = pl.BlockSpec((rb, c), lambda i: (i, 0))
    extra = [] if after is None else [after]
    return pl.pallas_call(
        body, name=name, grid=(r // rb,),
        in_specs=[blk, blk, blk, pl.BlockSpec((npc, rb, c), lambda i: (0, i, 0))] + [ANY] * len(extra),
        out_specs=[blk] * 4, out_shape=[_sds((r, c))] * 4, compiler_params=_cp((ARB,)),
    )(w, m, v, pieces, *extra)


def _pad_heads(w, d):
    k = w.shape[0]
    return jnp.pad(w.reshape(k, N_HEADS, d), ((0, 0), (0, 0), (0, LANE - d))).reshape(k, N_HEADS * LANE)


def _cut_heads(w, d):
    k = w.shape[0]
    return w.reshape(k, N_HEADS, LANE)[:, :, :d].reshape(k, N_HEADS * d)


def _w_in_pad(w):
    w_a = jnp.concatenate([_pad_heads(w[:, 0:256], 64), _pad_heads(w[:, 256:512], 64), w[:, 512:1536]], axis=1)
    w_b = jnp.concatenate([w[:, 1536:2176], jnp.pad(w[:, 2176:2240], ((0, 0), (0, 64)))], axis=1)
    return w_a, w_b


def _w_in_cut(g_a, g_b):
    return jnp.concatenate([_cut_heads(g_a[:, 0:512], 64), _cut_heads(g_a[:, 512:1024], 64), g_a[:, 1024:2048],
                            g_b[:, 0:704]], axis=1)


def _w_uq_pad(w):
    return jnp.pad(w.reshape(384, N_HEADS, 192), ((0, 0), (0, 0), (0, 64))).reshape(384, 1024)


def _w_uq_cut(g):
    return g.reshape(384, N_HEADS, 256)[:, :, :192].reshape(384, 768)


def _w_ukv_perm(w):
    w = w.reshape(256, N_HEADS, 256)
    return jnp.concatenate([w[:, :, :128].reshape(256, 512), w[:, :, 128:].reshape(256, 512)], axis=1)


def _w_ukv_unperm(g):
    return jnp.concatenate([g[:, :512].reshape(256, N_HEADS, 128), g[:, 512:].reshape(256, N_HEADS, 128)],
                           axis=2).reshape(256, 1024)


def _unshard_cols(g):
    return jnp.transpose(g, (1, 0, 2)).reshape(g.shape[1], N_DEV * g.shape[2])


def _shard_cols(w):
    k, n = w.shape
    return jnp.transpose(w.reshape(k, N_DEV, n // N_DEV), (1, 0, 2))


def _rope_tables():
    rows = SEQ // GRID_W
    row = jnp.repeat(jnp.arange(rows, dtype=F32), GRID_W)
    col = jnp.tile(jnp.arange(GRID_W, dtype=F32), rows)
    freq = ROPE_BASE ** (-jnp.arange(16, dtype=F32) / 16)
    ang = jnp.concatenate([row[:, None] * freq, col[:, None] * freq], axis=-1)
    cos, sin = jnp.cos(ang), jnp.sin(ang)
    z = jnp.zeros((SEQ, 64), F32)
    return jnp.concatenate([cos, cos, z], axis=1), jnp.concatenate([-sin, sin, z], axis=1)


_PACKED = (("g_attn", 1024), ("g_ffn", 1024), ("ret_decay_fwd", 4), ("ret_decay_bwd", 4), ("g_ret", 512),
           ("g_q_lora", 384), ("g_kv_lora", 256), ("g_final", 1024))
_PACK_OFF = {}
_off = 0
for _name, _n in _PACKED:
    _PACK_OFF[_name] = _off
    _off += -(-_n // LANE) * LANE
PACK_W = _off


def _pack_small(vals):
    parts = []
    for name, n in _PACKED:
        a = vals[name].reshape(-1).astype(F32)
        parts.append(jnp.pad(a, (0, -(-n // LANE) * LANE - n)))
    return jnp.concatenate(parts).reshape(1, PACK_W)


def _adamw_small(params, packed, gcc, gb_ada):
    names = list(params)
    n_p = len(names)

    def body(*refs):
        p_ref, gcc_ref, gb_ref = refs[3 * n_p:3 * n_p + 3]
        outs = refs[3 * n_p + 3:]
        for k, name in enumerate(names):
            w_ref, m_ref, v_ref = refs[3 * k:3 * k + 3]
            n = w_ref.shape[1]
            if name == "b_ada":
                g = gb_ref[...]
            elif name == "c_ctx":
                g = gcc_ref[0, 0:1, :]
                for d in range(1, N_DEV):
                    g = g + gcc_ref[d, 0:1, :]
            else:
                off = _PACK_OFF[name]
                g = p_ref[0, :, off:off + n]
                for d in range(1, N_DEV):
                    g = g + p_ref[d, :, off:off + n]
            mn = ADAM_B1 * m_ref[...] + (1.0 - ADAM_B1) * g
            vn = ADAM_B2 * v_ref[...] + (1.0 - ADAM_B2) * jnp.square(g)
            m_hat = mn / (1.0 - ADAM_B1 ** ADAM_STEP)
            v_hat = vn / (1.0 - ADAM_B2 ** ADAM_STEP)
            outs[4 * k][...] = g
            outs[4 * k + 1][...] = -ADAM_LR * (m_hat / (jnp.sqrt(v_hat) + ADAM_EPS) + ADAM_WD * w_ref[...])
            outs[4 * k + 2][...] = mn
            outs[4 * k + 3][...] = vn

    args = [a for name in names for a in params[name]] + [packed, gcc, gb_ada]
    out_shape = [_sds(params[name][0].shape) for name in names for _ in range(4)]
    outs = pl.pallas_call(body, name="adamw_small", out_shape=out_shape, compiler_params=_cp())(*args)
    return {name: list(outs[4 * k:4 * k + 4]) for k, name in enumerate(names)}


def kernel(x, c, ctx, c_ctx, w_ada, b_ada, g_attn, g_ffn, w_in, ret_decay_fwd, ret_decay_bwd, g_ret, g_q_lora, w_uq, g_kv_lora, w_ukv, w_out, w_ff1, w_ff2, g_final, loss_target, m_c_ctx, m_w_ada, m_b_ada, m_g_attn, m_g_ffn, m_w_in, m_ret_decay_fwd, m_ret_decay_bwd, m_g_ret, m_g_q_lora, m_w_uq, m_g_kv_lora, m_w_ukv, m_w_out, m_w_ff1, m_w_ff2, m_g_final, v_c_ctx, v_w_ada, v_b_ada, v_g_attn, v_g_ffn, v_w_in, v_ret_decay_fwd, v_ret_decay_bwd, v_g_ret, v_g_q_lora, v_w_uq, v_g_kv_lora, v_w_ukv, v_w_out, v_w_ff1, v_w_ff2, v_g_final):
    me = 4 * lax.axis_index("x") + 2 * lax.axis_index("y") + lax.axis_index("c")
    nb = x.shape[0]

    c_pad = jnp.pad(c, ((0, 8 - nb), (0, 0)))
    c_all, g_in, g_uq, g_ukv = _gather_two_level([c_pad, w_in[0].astype(BF), w_uq[0].astype(BF), w_ukv[0].astype(BF)],
                                                 "gather_weights")
    ws = (*_w_in_pad(_unshard_cols(g_in)), _w_uq_pad(_unshard_cols(g_uq)), _w_ukv_perm(_unshard_cols(g_ukv)))

    crows = jnp.concatenate([c_all[:, :nb].reshape(N_DEV * nb, D_MODEL), c_ctx[None], jnp.zeros((7, D_MODEL), F32)])
    b_blk = lax.dynamic_slice(b_ada, (0, me * 768), (1, 768))
    (mod_g,) = _exchange([_mod_fwd(crows, w_ada[0], b_blk)], True, "gather_mod")
    mod_all = _unshard_cols(mod_g)
    behind = mod_g[0, 0, 0:1] * 0.0
    st_g = _exchange_start([(w_out[0] + behind).astype(BF), w_ff1[0].astype(BF), w_ff2[0].astype(BF)], True,
                           "gather_ff_start")
    mod_all = mod_all + st_g["token"][0:1, 0:1]
    mod_mine = lax.dynamic_slice(mod_all, (me * nb, 0), (nb, 6 * D_MODEL)).reshape(nb, 6, D_MODEL)
    mod = jnp.pad(mod_mine, ((0, 0), (0, 2), (0, 0)))
    mod_c = jnp.pad(mod_all[16].reshape(1, 6, D_MODEL), ((0, 0), (0, 2), (0, 0)))

    tabs = _rope_tables()
    dec_f = ret_decay_fwd.reshape(N_HEADS, 1, 1)
    dec_b = ret_decay_bwd.reshape(N_HEADS, 1, 1)

    rkc, rvc, k_ctx, v_ctx = _k1_fwd(ctx, mod_c, g_attn, g_q_lora, g_kv_lora, ws, tabs, None, True)
    rq, rk, rv, rg, q, k_all, v_all = _k1_fwd(x, mod, g_attn, g_q_lora, g_kv_lora, ws, tabs, (k_ctx, v_ctx), False)
    o_f, o_b, sf_prev, sb_prev = _k2_fwd(rq, rk, rv, rkc, rvc, dec_f, dec_b)
    y_mla = _k3_fwd(q, k_all, v_all)
    g_out, g_ff1, g_ff2 = _exchange_wait(st_g, y_mla, "gather_ff_wait")
    wo = g_out.reshape(D_MODEL, D_MODEL)
    x_mid, h2 = _k4a_fwd(x, o_f, o_b, rg, y_mla, g_ret, wo, mod, g_ffn)
    dxm, dmlp, loss_acc, dgt_f, dg_final = _k4b_mlp_loss(h2, g_ff1, g_ff2, x_mid, mod, g_final.reshape(1, D_MODEL),
                                                         loss_target)

    dh2, dw1, dw2 = _k4d_mlp_bwd(h2, dmlp, g_ff1, g_ff2)
    st_s = _exchange_start([dw1, dw2], False, "scatter_ff_start")
    g_ret_t = g_ret + st_s["token"][0:1, 0:1]
    dx_res, do, drg, dym, dwo, dg_ret, dg_ffn, dmod_a = _k4e_bwd(x, o_f, o_b, rg, y_mla, g_ret_t, wo, mod, g_ffn, dxm, dh2)
    dq, dk_all, dv_all = _k3_bwd(q, k_all, v_all, dym)
    dqf, dkf, dvf, dqb, dkb, dvb, dkc, dvc, ddf, ddb = _k2_bwd(rq, rk, rv, do, sf_prev, sb_prev, rkc, rvc, dec_f, dec_b)
    cts = [[(dqf, 0), (dqb, 0)], [(dkf, 0), (dkb, 0)], [(dvf, 0), (dvb, 0)], [(drg, 0)], [(dq, 0)],
           [(dk_all, CTX_LEN)], [(dv_all, CTX_LEN)]]
    grad_x, accs, dmod_1 = _k1_bwd(x, mod, g_attn, g_q_lora, g_kv_lora, ws, tabs, cts, dx_res, None, False)
    cts_c = [[(dkc, 0)], [(dvc, 0)], [(dk_all, 0)], [(dv_all, 0)]]
    _, accs, dmod_c1 = _k1_bwd(ctx, mod_c, g_attn, g_q_lora, g_kv_lora, ws, tabs, cts_c, None, accs, True)
    dwa, dwb, dwq, dwk, dg_attn, dg_q, dg_kv = accs

    dmod_loc = (dmod_a + dmod_1).at[:, 5, :].set(dgt_f[:, 0, :])[:, :6, :].reshape(nb, 6 * D_MODEL)
    dmod_ctx = dmod_c1[:, :6, :].reshape(1, 6 * D_MODEL)
    small = {"g_attn": dg_attn, "g_ffn": dg_ffn, "ret_decay_fwd": jnp.sum(ddf[:, :, 0, 0], axis=0),
             "ret_decay_bwd": jnp.sum(ddb[:, :, 0, 0], axis=0), "g_ret": dg_ret, "g_q_lora": dg_q, "g_kv_lora": dg_kv,
             "g_final": dg_final}
    extra = jnp.concatenate([dmod_loc, dmod_ctx, jnp.zeros((5, 6 * D_MODEL), F32)])
    sm_g, ex_g, loss_g = _exchange([_pack_small(small), extra, loss_acc], True, "gather_small")
    dmod_all = ex_g[:, :nb].reshape(N_DEV * nb, 6 * D_MODEL)
    dmodc_parts = ex_g[:, nb]
    dmod_full = jnp.concatenate([dmod_all, jnp.zeros((8, 6 * D_MODEL), F32)])
    dmod_blk = lax.dynamic_slice(dmod_full, (0, me * 768), (24, 768))
    dmodc_blk = lax.dynamic_slice(dmodc_parts, (0, me * 768), (N_DEV, 768))
    gw_ada, gcc_part, gb_ada = _mod_bwd(crows, w_ada[0], dmod_blk, dmodc_blk, dmod_full, dmodc_parts)
    (gcc_g,) = _exchange([gcc_part], True, "gather_c_ctx")

    p_ff1, p_ff2 = _exchange_wait(st_s, gcc_g, "scatter_ff_wait")
    behind = gcc_g[0, 0, 0:1] * 0.0
    st_r = _exchange_start([_shard_cols(_w_in_cut(dwa, dwb)).astype(BF), _shard_cols(_w_uq_cut(dwq)).astype(BF),
                            _shard_cols(_w_ukv_unperm(dwk)).astype(BF),
                            (dwo.reshape(N_DEV, 128, D_MODEL) + behind).astype(BF)], False, "scatter_rest_start")

    res = {}
    early = (("w_ff1", w_ff1, m_w_ff1, v_w_ff1, p_ff1), ("w_ff2", w_ff2, m_w_ff2, v_w_ff2, p_ff2),
             ("w_ada", w_ada, m_w_ada, v_w_ada, gw_ada[None]))
    for name, w, m, v, pcs in early:
        res[name] = [a[None] for a in _adamw(w[0], m[0], v[0], pcs, "adamw_" + name, after=st_r["token"])]
    pieces = _exchange_wait(st_r, res["w_ada"][3], "scatter_rest_wait")
    late = (("w_in", w_in, m_w_in, v_w_in, pieces[0]), ("w_uq", w_uq, m_w_uq, v_w_uq, pieces[1]),
            ("w_ukv", w_ukv, m_w_ukv, v_w_ukv, pieces[2]), ("w_out", w_out, m_w_out, v_w_out, pieces[3]))
    for name, w, m, v, pcs in late:
        res[name] = [a[None] for a in _adamw(w[0], m[0], v[0], pcs, "adamw_" + name)]

    smalls = {"c_ctx": (c_ctx, m_c_ctx, v_c_ctx), "b_ada": (b_ada, m_b_ada, v_b_ada), "g_attn": (g_attn, m_g_attn, v_g_attn),
              "g_ffn": (g_ffn, m_g_ffn, v_g_ffn), "ret_decay_fwd": (ret_decay_fwd, m_ret_decay_fwd, v_ret_decay_fwd),
              "ret_decay_bwd": (ret_decay_bwd, m_ret_decay_bwd, v_ret_decay_bwd), "g_ret": (g_ret, m_g_ret, v_g_ret),
              "g_q_lora": (g_q_lora, m_g_q_lora, v_g_q_lora), "g_kv_lora": (g_kv_lora, m_g_kv_lora, v_g_kv_lora),
              "g_final": (g_final, m_g_final, v_g_final)}
    rows = {k: tuple(a.reshape(1, -1) for a in t) for k, t in smalls.items()}
    for name, outs in _adamw_small(rows, sm_g, gcc_g, gb_ada).items():
        res[name] = [o.reshape(smalls[name][0].shape) for o in outs]

    loss = loss_g[0, 0, 0]
    for k in range(1, N_DEV):
        loss = loss + loss_g[k, 0, 0]

    order = ("c_ctx", "w_ada", "b_ada", "g_attn", "g_ffn", "w_in", "ret_decay_fwd", "ret_decay_bwd", "g_ret", "g_q_lora",
             "w_uq", "g_kv_lora", "w_ukv", "w_out", "w_ff1", "w_ff2", "g_final")
    return (loss, grad_x, *[res[n][0] for n in order], *[res[n][1] for n in order], *[res[n][2] for n in order],
            *[res[n][3] for n in order])
```

```python
import functools
import math

import jax
import jax.numpy as jnp
from jax import lax
from jax.experimental import pallas as pl
from jax.experimental.pallas import tpu as pltpu

F32 = jnp.float32
BF = jnp.bfloat16
EPS = 1e-6
LANE = 128
N_DEV = 8
D_MODEL = 1024
SEQ = 2048
CTX_LEN = 256
GRID_W = 64
N_HEADS = 4
RET_CHUNK = 512
N_CHUNK = SEQ // RET_CHUNK
D_FF = 4096
FF_BLK = D_FF // N_DEV
IN_PAD = 2816
KV_LEN = CTX_LEN + SEQ
ROPE_BASE = 10000.0
ADAM_LR, ADAM_B1, ADAM_B2, ADAM_EPS, ADAM_WD, ADAM_STEP = 0.001, 0.9, 0.999, 1e-08, 0.01, 10
TOK = 256
TOK_B = 256
VMEM_LIMIT = 56 * 1024 * 1024
ARB = "arbitrary"
MESH = pl.DeviceIdType.MESH
_HEAD_SL = [slice(LANE * h, LANE * (h + 1)) for h in range(N_HEADS)]
W_SHAPES = [(D_MODEL, 2048), (D_MODEL, 768), (384, 1024), (256, 1024)]


def _dot(a, b, ca, cb):
    return lax.dot_general(a.astype(BF), b.astype(BF), (((ca,), (cb,)), ((), ())), preferred_element_type=F32)


@jax.custom_vjp
def mm(a, b):
    return _dot(a, b, 1, 0)


@jax.custom_vjp
def mm_nt(a, b):
    return _dot(a, b, 1, 1)


@jax.custom_vjp
def mm_tn(a, b):
    return _dot(a, b, 0, 0)


mm.defvjp(lambda a, b: (_dot(a, b, 1, 0), (a, b)), lambda r, g: (mm_nt(g, r[1]), mm_tn(r[0], g)))
mm_nt.defvjp(lambda a, b: (_dot(a, b, 1, 1), (a, b)), lambda r, g: (mm(g, r[1]), mm_tn(g, r[0])))
mm_tn.defvjp(lambda a, b: (_dot(a, b, 0, 0), (a, b)), lambda r, g: (mm_nt(r[1], g), mm(r[0], g)))


@jax.custom_vjp
def _mmw(a, w, probe):
    return _dot(a, w, 1, 0)


def _mmw_bwd(r, g):
    a, w = r
    return mm_nt(g, w), jnp.zeros_like(w), mm_tn(a, g)


_mmw.defvjp(lambda a, w, probe: (_dot(a, w, 1, 0), (a, w)), _mmw_bwd)


def mmw(a, w, probe):
    return _dot(a, w, 1, 0) if probe is None else _mmw(a, w, probe)


def rmsn(x, g):
    return x * lax.rsqrt(jnp.mean(x * x, axis=-1, keepdims=True) + EPS) * g


def silu(x):
    return x * jax.nn.sigmoid(x)


def _swap32_impl(x):
    n = x.shape[-1]
    lane = lax.broadcasted_iota(jnp.int32, x.shape, x.ndim - 1) % LANE
    up = pltpu.roll(x, n - 32, x.ndim - 1)
    dn = pltpu.roll(x, 32, x.ndim - 1)
    return jnp.where(lane < 32, up, jnp.where(lane < 64, dn, 0.0))


@jax.custom_vjp
def swap32(x):
    return _swap32_impl(x)


swap32.defvjp(lambda x: (_swap32_impl(x), None), lambda _, g: (_swap32_impl(g),))


def rope(x, cs, sn):
    return x * cs + swap32(x) * sn


def k1_tile(x, sh, sc, g_attn, g_q, g_kv, ws, ps, tabs, is_ctx):
    w_a, w_b, w_uq, w_ukv = ws
    p_a, p_b, p_uq, p_ukv = ps
    cs1, sn1 = tabs
    cs, sn = jnp.concatenate([cs1] * N_HEADS, axis=-1), jnp.concatenate([sn1] * N_HEADS, axis=-1)
    cq_t = jnp.concatenate([jnp.ones_like(cs1), cs1] * N_HEADS, axis=-1)
    sq_t = jnp.concatenate([jnp.zeros_like(sn1), sn1] * N_HEADS, axis=-1)
    h = rmsn(x, g_attn) * (1.0 + sc) + sh
    pa = mmw(h, w_a, p_a)
    pb = mmw(h, w_b, p_b)
    rk = pa[:, 512:1024] * 0.125
    rv = pa[:, 1024:1536]
    kpe = pb[:, 640:768]
    kv = mmw(rmsn(pb[:, 384:640], g_kv), w_ukv, p_ukv)
    if not is_ctx:
        rk = rope(rk, cs, sn)
        kpe = rope(kpe, cs1, sn1)
    k_full = jnp.concatenate([piece for sl in _HEAD_SL for piece in (kv[:, sl], kpe)], axis=-1)
    v = kv[:, 512:]
    if is_ctx:
        return rk, rv, k_full, v
    rq = rope(pa[:, 0:512], cs, sn)
    rg = pa[:, 1536:2048]
    q = rope(mmw(rmsn(pb[:, 0:384], g_q), w_uq, p_uq), cq_t, sq_t)
    return rq, rk, rv, rg, q, k_full, v


def log_sigmoid(x):
    return jnp.minimum(x, 0.0) - jnp.log(1.0 + jnp.exp(-jnp.abs(x)))


def ret_chunk(q, k, v, s, lg, reverse):
    c = RET_CHUNK
    ii = lax.broadcasted_iota(jnp.int32, (c, c), 0).astype(F32)
    jj = lax.broadcasted_iota(jnp.int32, (c, c), 1).astype(F32)
    diff = (jj - ii) if reverse else (ii - jj)
    dec = jnp.where(diff >= 0, jnp.exp(lg * jnp.maximum(diff, 0.0)), 0.0)
    pos = lax.broadcasted_iota(jnp.int32, (c, 1), 0).astype(F32)
    if reverse:
        wk, wq = jnp.exp(lg * pos), jnp.exp(lg * (c - pos))
    else:
        wk, wq = jnp.exp(lg * (c - 1.0 - pos)), jnp.exp(lg * (pos + 1.0))
    o = mm(mm_nt(q, k) * dec, v) + mm(q * wq, s)
    s_next = jnp.exp(lg * float(c)) * s + mm_tn(k * wk, v)
    return o, s_next


def ctx_state(kc, vc, lg, reverse):
    n = kc.shape[0]
    pos = lax.broadcasted_iota(jnp.int32, (n, 1), 0).astype(F32)
    w = jnp.exp(lg * pos) if reverse else jnp.exp(lg * (n - 1.0 - pos))
    return mm_tn(kc * w, vc)


def attn_head(qn, qp, kn, kp, v):
    s = (mm_nt(qn, kn) + mm_nt(qp, kp)) * (1.0 / math.sqrt(192.0))
    e = jnp.exp(s - jnp.max(s, axis=-1, keepdims=True))
    return mm(e / jnp.sum(e, axis=-1, keepdims=True), v)


def gn_gate(o, rg, g_ret):
    ys = []
    for h in range(N_HEADS):
        sl = slice(LANE * h, LANE * (h + 1))
        oh = o[:, sl]
        mu = jnp.mean(oh, axis=-1, keepdims=True)
        var = jnp.mean(jnp.square(oh - mu), axis=-1, keepdims=True)
        ys.append((oh - mu) * lax.rsqrt(var + EPS) * g_ret[:, sl])
    return jnp.concatenate(ys, axis=-1) * silu(rg)


def k4a_tile(x, o_f, o_b, rg, y_mla, g_ret, gt_a, g_ffn, sh_f, sc_f, w_out, p_out):
    mix = jnp.concatenate([gn_gate(o_f + o_b, rg, g_ret), y_mla], axis=-1)
    x_mid = x + gt_a * mmw(mix, w_out, p_out)
    h2 = rmsn(x_mid, g_ffn) * (1.0 + sc_f) + sh_f
    return x_mid, h2


def k4c_tile(x_mid, mlp, gt_f, g_final, tgt):
    y = rmsn(x_mid + gt_f * mlp, g_final)
    per_tok = jnp.mean(jnp.square(y - tgt), axis=-1, keepdims=True)
    return 0.5 * jnp.sum(per_tok, axis=0, keepdims=True)


def _cp(sem=None, vmem=VMEM_LIMIT):
    return pltpu.CompilerParams(dimension_semantics=sem, vmem_limit_bytes=vmem)


def _acc(ref, val, first):
    @pl.when(first)
    def _():
        ref[...] = val

    @pl.when(jnp.logical_not(first))
    def _():
        ref[...] += val


def _full(shape):
    nd = len(shape)
    return pl.BlockSpec(shape, lambda *_: (0,) * nd)


ANY = pl.BlockSpec(memory_space=pl.ANY)


def _sds(shape, dtype=F32):
    return jax.ShapeDtypeStruct(shape, dtype)


def _exchange(arrs, gather, name):
    n = len(arrs)
    out_shape = [_sds(((N_DEV,) + a.shape) if gather else a.shape, a.dtype) for a in arrs]

    def body(*refs):
        ins, outs = refs[:n], refs[n:2 * n]
        send_sems, recv_sems, local_sems = refs[2 * n:]
        x, y, c = lax.axis_index("x"), lax.axis_index("y"), lax.axis_index("c")
        me = 4 * x + 2 * y + c
        sends, recvs, locs = [], [], []
        for i in range(n):
            for k in range(N_DEV - 1):
                bits = k + 1
                px = x ^ ((bits >> 2) & 1)
                py = y ^ ((bits >> 1) & 1)
                pc = c ^ (bits & 1)
                peer = 4 * px + 2 * py + pc
                src = ins[i] if gather else ins[i].at[peer]
                sem = i * (N_DEV - 1) + k
                sends.append(pltpu.make_async_remote_copy(
                    src_ref=src, dst_ref=outs[i].at[me], send_sem=send_sems.at[sem], recv_sem=recv_sems.at[sem],
                    device_id=(px, py, pc), device_id_type=MESH))
                recvs.append(pltpu.make_async_remote_copy(
                    src_ref=src, dst_ref=outs[i].at[peer], send_sem=send_sems.at[sem], recv_sem=recv_sems.at[sem],
                    device_id=(px, py, pc), device_id_type=MESH))
            locs.append(pltpu.make_async_copy(ins[i] if gather else ins[i].at[me], outs[i].at[me], local_sems.at[i]))
        for cp in locs + sends:
            cp.start()
        for cp in recvs:
            cp.wait_recv()
        for cp in sends:
            cp.wait_send()
        for cp in locs:
            cp.wait()

    outs = pl.pallas_call(
        body, name=name, out_shape=out_shape, in_specs=[ANY] * n, out_specs=[ANY] * n,
        scratch_shapes=[pltpu.SemaphoreType.DMA((n * (N_DEV - 1),)), pltpu.SemaphoreType.DMA((n * (N_DEV - 1),)),
                        pltpu.SemaphoreType.DMA((n,))],
    )(*arrs)
    return list(outs)


def _gather_two_level(arrs, name):
    n = len(arrs)

    def body(*refs):
        ins, outs = refs[:n], refs[n:2 * n]
        send_sems, recv_sems, local_sems = refs[2 * n:]
        x, y, c = lax.axis_index("x"), lax.axis_index("y"), lax.axis_index("c")
        sibling = (x, y, 1 - c)
        chips = [(1 - x, y), (x, 1 - y), (1 - x, 1 - y)]

        def slot(px, py, pc):
            return 4 * px + 2 * py + pc

        first, passed, waits, locs = [], [], [], []
        for i in range(n):
            def copy(k, block, to, src=None, i=i):
                dst = outs[i].at[slot(*block)]
                return pltpu.make_async_remote_copy(
                    src_ref=dst if src is None else src, dst_ref=dst, send_sem=send_sems.at[7 * i + k],
                    recv_sem=recv_sems.at[7 * i + k], device_id=to, device_id_type=MESH)

            locs.append(pltpu.make_async_copy(ins[i], outs[i].at[slot(x, y, c)], local_sems.at[i]))
            first.append(copy(0, (x, y, c), sibling, src=ins[i]))
            first += [copy(1 + j, (x, y, c), (*chip, c), src=ins[i]) for j, chip in enumerate(chips)]
            passed.append([copy(4 + j, (*chip, c), sibling) for j, chip in enumerate(chips)])
            waits.append([copy(1 + j, (*chip, c), (x, y, c)) for j, chip in enumerate(chips)])
        for cp in locs + first:
            cp.start()
        for j in range(3):
            for i in range(n):
                waits[i][j].wait_recv()
                passed[i][j].start()
        for i in range(n):
            def arrival(k, block, i=i):
                dst = outs[i].at[slot(*block)]
                return pltpu.make_async_remote_copy(
                    src_ref=dst, dst_ref=dst, send_sem=send_sems.at[7 * i + k], recv_sem=recv_sems.at[7 * i + k],
                    device_id=sibling, device_id_type=MESH)

            arrival(0, (x, y, 1 - c)).wait_recv()
            for j, chip in enumerate(chips):
                arrival(4 + j, (*chip, 1 - c)).wait_recv()
        for cp in first + [p for ps in passed for p in ps]:
            cp.wait_send()
        for cp in locs:
            cp.wait()

    outs = pl.pallas_call(
        body, name=name, out_shape=[_sds((N_DEV,) + a.shape, a.dtype) for a in arrs], in_specs=[ANY] * n,
        out_specs=[ANY] * n,
        scratch_shapes=[pltpu.SemaphoreType.DMA((7 * n,)), pltpu.SemaphoreType.DMA((7 * n,)),
                        pltpu.SemaphoreType.DMA((n,))],
    )(*arrs)
    return list(outs)


HBM = pl.BlockSpec(memory_space=pltpu.HBM)
SEM = pl.BlockSpec(memory_space=pltpu.SEMAPHORE)
EFFECT = pltpu.SideEffectType.DATAFLOW_SIDE_EFFECTING


def _peer(k):
    x, y, c = lax.axis_index("x"), lax.axis_index("y"), lax.axis_index("c")
    bits = k + 1
    px, py, pc = x ^ ((bits >> 2) & 1), y ^ ((bits >> 1) & 1), c ^ (bits & 1)
    return (px, py, pc), 4 * px + 2 * py + pc, 4 * x + 2 * y + c


def _exchange_start(arrs, gather, name):
    n = len(arrs)
    lands = [pltpu.with_memory_space_constraint(lax.empty(((N_DEV,) + a.shape) if gather else a.shape, a.dtype),
                                                pltpu.HBM) for a in arrs]
    srcs = [pltpu.with_memory_space_constraint(a, pltpu.HBM) for a in arrs]

    def body(*refs):
        ins, zones = refs[:n], refs[n:2 * n]
        send_sems, recv_sems, local_sems = refs[2 * n:2 * n + 3]
        token = refs[-1]
        for i in range(n):
            for k in range(N_DEV - 1):
                dev, peer, me = _peer(k)
                sem = i * (N_DEV - 1) + k
                pltpu.make_async_remote_copy(
                    src_ref=ins[i] if gather else ins[i].at[peer], dst_ref=zones[i].at[me],
                    send_sem=send_sems.at[sem], recv_sem=recv_sems.at[sem], device_id=dev, device_id_type=MESH).start()
            _, _, me = _peer(0)
            pltpu.make_async_copy(ins[i] if gather else ins[i].at[me], zones[i].at[me], local_sems.at[i]).start()
        token[...] = jnp.zeros_like(token)

    nsem = n * (N_DEV - 1)
    outs = pl.pallas_call(
        body, name=name,
        out_shape=[pltpu.SemaphoreType.DMA((nsem,)), pltpu.SemaphoreType.DMA((nsem,)), pltpu.SemaphoreType.DMA((n,))]
        + [pltpu.HBM(a.shape, a.dtype) for a in srcs] + [pltpu.HBM(z.shape, z.dtype) for z in lands]
        + [_sds((8, LANE))],
        in_specs=[HBM] * (2 * n),
        out_specs=[SEM, SEM, SEM] + [HBM] * (2 * n) + [pl.BlockSpec(memory_space=pltpu.VMEM)],
        input_output_aliases={i: 3 + i for i in range(2 * n)},
        compiler_params=pltpu.CompilerParams(has_side_effects=EFFECT),
    )(*srcs, *lands)
    return {"n": n, "gather": gather, "sems": outs[:3], "srcs": outs[3:3 + n], "lands": outs[3 + n:3 + 2 * n],
            "token": outs[-1]}


def _exchange_wait(st, after, name):
    n, gather = st["n"], st["gather"]

    def body(*refs):
        ins, zones = refs[:n], refs[n:2 * n]
        send_sems, recv_sems, local_sems = refs[2 * n:2 * n + 3]
        for i in range(n):
            for k in range(N_DEV - 1):
                dev, peer, me = _peer(k)
                sem = i * (N_DEV - 1) + k
                src = ins[i] if gather else ins[i].at[peer]
                cp = pltpu.make_async_remote_copy(
                    src_ref=src, dst_ref=zones[i].at[peer], send_sem=send_sems.at[sem], recv_sem=recv_sems.at[sem],
                    device_id=dev, device_id_type=MESH)
                cp.wait_send()
                cp.wait_recv()
            _, _, me = _peer(0)
            pltpu.make_async_copy(ins[i] if gather else ins[i].at[me], zones[i].at[me], local_sems.at[i]).wait()

    outs = pl.pallas_call(
        body, name=name,
        out_shape=[pltpu.HBM(a.shape, a.dtype) for a in st["srcs"]] + [pltpu.HBM(z.shape, z.dtype) for z in st["lands"]],
        in_specs=[HBM] * (2 * n) + [SEM, SEM, SEM, ANY], out_specs=[HBM] * (2 * n),
        input_output_aliases={i: i for i in range(2 * n)},
        compiler_params=pltpu.CompilerParams(has_side_effects=EFFECT),
    )(*st["srcs"], *st["lands"], *st["sems"], after)
    return list(outs[n:])


def _mod_fwd(crows, w_ada, b_blk):
    def body(c_ref, w_ref, b_ref, o_ref):
        o_ref[...] = mm(silu(c_ref[...]), w_ref[...]) + b_ref[...]

    return pl.pallas_call(body, name="mod_fwd", out_shape=_sds((24, 768)), compiler_params=_cp())(crows, w_ada, b_blk)


def _mod_bwd(crows, w_ada, dmod_blk, dmodc_blk, dmod_full, dmodc_full):
    def body(c_ref, w_ref, d_ref, dc_ref, df_ref, dcf_ref, gw_ref, gc_ref, gb_ref):
        cr = c_ref[...]
        dc, dcf = dc_ref[0:1, :], dcf_ref[0:1, :]
        for p in range(1, N_DEV):
            dc = dc + dc_ref[p:p + 1, :]
            dcf = dcf + dcf_ref[p:p + 1, :]
        row = lax.broadcasted_iota(jnp.int32, (24, 1), 0)
        gw_ref[...] = mm_tn(silu(cr), jnp.where(row == 16, dc, d_ref[...]))
        cc = cr[16:17, :]
        sg = jax.nn.sigmoid(cc)
        part = mm_nt(jnp.broadcast_to(dc, (8, 768)), w_ref[...])
        gc_ref[...] = part * (sg * (1.0 + cc * (1.0 - sg)))
        gb_ref[...] = jnp.sum(df_ref[...], axis=0, keepdims=True) + dcf

    return pl.pallas_call(
        body, name="mod_bwd", out_shape=[_sds((D_MODEL, 768)), _sds((8, D_MODEL)), _sds((1, 6 * D_MODEL))],
        compiler_params=_cp())(crows, w_ada, dmod_blk, dmodc_blk, dmod_full, dmodc_full)


def _tab_specs(tk):
    return [pl.BlockSpec((tk, LANE), lambda i, t: (t, 0))] * 2


def _k1_fwd(x, mod, g_attn, g_q, g_kv, ws, tabs, kv_all, is_ctx):
    b, l, _ = x.shape
    nt = l // TOK
    n_f32 = 2 if is_ctx else 4

    def body(x_ref, mod_ref, ga_ref, gq_ref, gk_ref, wa_ref, wb_ref, wq_ref, wk_ref, cs_ref, sn_ref, *rest):
        outs = rest if is_ctx else rest[2:]
        res = k1_tile(x_ref[...], mod_ref[0:1, :], mod_ref[1:2, :], ga_ref[...], gq_ref[...], gk_ref[...],
                      (wa_ref[...], wb_ref[...], wq_ref[...], wk_ref[...]), (None,) * 4,
                      (cs_ref[...], sn_ref[...]), is_ctx)
        for o_ref, r in zip(outs, res):
            o_ref[...] = r.astype(o_ref.dtype)

    tok = lambda w, off=0: pl.BlockSpec((None, TOK, w), lambda i, t: (i, t + off, 0))
    mod_spec = pl.BlockSpec((None, 8, D_MODEL), (lambda i, t: (0, 0, 0)) if is_ctx else (lambda i, t: (i, 0, 0)))
    kv_off = 0 if is_ctx else CTX_LEN // TOK
    in_specs = ([tok(D_MODEL), mod_spec, _full((1, D_MODEL)), _full((1, 384)), _full((1, 256))]
                + [_full(s) for s in W_SHAPES] + _tab_specs(TOK))
    args = [x, mod, g_attn, g_q, g_kv, *ws, *tabs]
    out_specs = [tok(512)] * n_f32 + ([] if is_ctx else [tok(1024)]) + [tok(1024, kv_off), tok(512, kv_off)]
    out_shape = ([_sds((b, l, 512))] * n_f32 + ([] if is_ctx else [_sds((b, l, 1024), BF)])
                 + [_sds((b, KV_LEN, 1024), BF), _sds((b, KV_LEN, 512), BF)])
    aliases = {}
    if not is_ctx:
        aliases = {len(args): n_f32 + 1, len(args) + 1: n_f32 + 2}
        in_specs += [ANY, ANY]
        args += list(kv_all)
    return pl.pallas_call(
        body, name="k1_fwd_ctx" if is_ctx else "k1_fwd", grid=(b, nt), in_specs=in_specs, out_specs=out_specs,
        out_shape=out_shape, input_output_aliases=aliases, compiler_params=_cp((ARB, ARB)),
    )(*args)


N_ACC = 7


def _k1_bwd(x, mod, g_attn, g_q, g_kv, ws, tabs, cts, dx_res, init, is_ctx):
    b, l, _ = x.shape
    tk = TOK_B
    nt = l // tk
    flat_cts = [a for group in cts for a in group]
    group_sizes = [len(g) for g in cts]
    n_ct = len(flat_cts)
    has_res = dx_res is not None
    has_init = init is not None
    acc_shapes = W_SHAPES + [(1, D_MODEL), (1, 384), (1, 256)]

    def body(*refs):
        it = iter(refs)
        x_ref, mod_ref, ga_ref, gq_ref, gk_ref = [next(it) for _ in range(5)]
        w_hbm = [next(it) for _ in range(4)]
        tab_refs = [next(it) for _ in range(2)]
        ct_refs = [next(it) for _ in range(n_ct)]
        res_ref = next(it) if has_res else None
        init_refs = [next(it) for _ in range(N_ACC)] if has_init else None
        gx_ref = next(it) if not is_ctx else None
        out_hbm = [next(it) for _ in range(N_ACC)]
        dmod_ref = next(it)
        w_vmem = [next(it) for _ in range(4)]
        accs = [next(it) for _ in range(N_ACC)]
        sem = next(it)
        i, t = pl.program_id(0), pl.program_id(1)
        first = jnp.logical_and(i == 0, t == 0)
        last = jnp.logical_and(i == b - 1, t == nt - 1)

        @pl.when(first)
        def _():
            for src, dst in zip(w_hbm, w_vmem):
                pltpu.sync_copy(src, dst)
            for k in range(N_ACC):
                if has_init:
                    pltpu.sync_copy(init_refs[k], accs[k])
                else:
                    accs[k][...] = jnp.zeros(acc_shapes[k], F32)

        ct_vals, pos = [], 0
        for gsz in group_sizes:
            v = ct_refs[pos][...].astype(F32)
            for r in ct_refs[pos + 1:pos + gsz]:
                v = v + r[...]
            ct_vals.append(v)
            pos += gsz
        wv = tuple(r[...] for r in w_vmem)
        tv = tuple(r[...] for r in tab_refs)

        def f(xv, sh, sc, ga, gq, gk, *probes):
            return k1_tile(xv, sh, sc, ga, gq, gk, wv, probes, tv, is_ctx)

        probes = [jnp.zeros(s, F32) for s in W_SHAPES]
        _, vjp = jax.vjp(f, x_ref[...], mod_ref[0:1, :], mod_ref[1:2, :], ga_ref[...], gq_ref[...], gk_ref[...], *probes)
        dx, dsh, dsc, dga, dgq, dgk, dwa, dwb, dwq, dwk = vjp(tuple(ct_vals))
        if not is_ctx:
            gx_ref[...] = dx + res_ref[...] if has_res else dx
        for ref, val in zip(accs, (dwa, dwb, dwq, dwk, dga, dgq, dgk)):
            ref[...] += val
        t0 = first if is_ctx else t == 0
        _acc(dmod_ref.at[0:1, :], dsh, t0)
        _acc(dmod_ref.at[1:2, :], dsc, t0)

        @pl.when(t0)
        def _():
            dmod_ref[2:8, :] = jnp.zeros((6, D_MODEL), F32)

        @pl.when(last)
        def _():
            cps = [pltpu.make_async_copy(accs[k], out_hbm[k], sem.at[k]) for k in range(N_ACC)]
            for cp in cps:
                cp.start()
            for cp in cps:
                cp.wait()

    tok = lambda w, off=0: pl.BlockSpec((None, tk, w), lambda i, t: (i, t + off, 0))
    mod_spec = pl.BlockSpec((None, 8, D_MODEL), (lambda i, t: (0, 0, 0)) if is_ctx else (lambda i, t: (i, 0, 0)))
    in_specs = ([tok(D_MODEL), mod_spec, _full((1, D_MODEL)), _full((1, 384)), _full((1, 256))] + [ANY] * 4
                + _tab_specs(tk))
    args = [x, mod, g_attn, g_q, g_kv, *ws, *tabs]
    for a, off in flat_cts:
        in_specs.append(tok(a.shape[-1], off // tk))
        args.append(a)
    if has_res:
        in_specs.append(tok(D_MODEL))
        args.append(dx_res)
    if has_init:
        in_specs += [ANY] * N_ACC
        args += list(init)
    out_shape, out_specs = [], []
    if not is_ctx:
        out_shape.append(_sds((b, l, D_MODEL)))
        out_specs.append(tok(D_MODEL))
    out_shape += [_sds(s) for s in acc_shapes] + [_sds((1 if is_ctx else b, 8, D_MODEL))]
    out_specs += [ANY] * N_ACC + [mod_spec]
    outs = pl.pallas_call(
        body, name="k1_bwd_ctx" if is_ctx else "k1_bwd", grid=(b, nt), in_specs=in_specs, out_specs=out_specs,
        out_shape=out_shape,
        scratch_shapes=[pltpu.VMEM(s, BF) for s in W_SHAPES] + [pltpu.VMEM(s, F32) for s in acc_shapes]
        + [pltpu.SemaphoreType.DMA((N_ACC,))],
        compiler_params=_cp((ARB, ARB)),
    )(*args)
    outs = list(outs)
    gx = None if is_ctx else outs.pop(0)
    return gx, outs[:N_ACC], outs[N_ACC]


def _chunk_spec(rev):
    if rev:
        return pl.BlockSpec((None, RET_CHUNK, 512), lambda i, n: (i, N_CHUNK - 1 - n, 0))
    return pl.BlockSpec((None, RET_CHUNK, 512), lambda i, n: (i, n, 0))


def _state_spec(rev):
    if rev:
        return pl.BlockSpec((None, N_HEADS, None, LANE, LANE), lambda i, n: (i, 0, N_CHUNK - 1 - n, 0, 0))
    return pl.BlockSpec((None, N_HEADS, None, LANE, LANE), lambda i, n: (i, 0, n, 0, 0))


_CTX_SPEC = pl.BlockSpec((None, CTX_LEN, 512), lambda i, n: (i, 0, 0))
_DEC_SPEC = pl.BlockSpec((N_HEADS, 1, 1), lambda i, n: (0, 0, 0))


def _k2_fwd(rq, rk, rv, rkc, rvc, dec_f, dec_b):
    b = rq.shape[0]

    def body(qf, kf, vf, qb, kb, vb, kc, vc, df, db, of_ref, ob_ref, sf_out, sb_out, sf, sb):
        n = pl.program_id(1)
        for h, sl in enumerate(_HEAD_SL):
            lgf, lgb = log_sigmoid(df[h]), log_sigmoid(db[h])

            @pl.when(n == 0)
            def _():
                sf[h] = ctx_state(kc[:, sl], vc[:, sl], lgf, False)
                sb[h] = ctx_state(kc[:, sl], vc[:, sl], lgb, True)

            sf_out[h] = sf[h]
            sb_out[h] = sb[h]
            o, s = ret_chunk(qf[:, sl], kf[:, sl], vf[:, sl], sf[h], lgf, False)
            of_ref[:, sl] = o
            sf[h] = s
            o, s = ret_chunk(qb[:, sl], kb[:, sl], vb[:, sl], sb[h], lgb, True)
            ob_ref[:, sl] = o
            sb[h] = s

    l = rq.shape[1]
    return pl.pallas_call(
        body, name="k2_fwd", grid=(b, N_CHUNK),
        in_specs=[_chunk_spec(False)] * 3 + [_chunk_spec(True)] * 3 + [_CTX_SPEC, _CTX_SPEC, _DEC_SPEC, _DEC_SPEC],
        out_specs=[_chunk_spec(False), _chunk_spec(True), _state_spec(False), _state_spec(True)],
        out_shape=[_sds((b, l, 512)), _sds((b, l, 512)), _sds((b, N_HEADS, N_CHUNK, LANE, LANE)),
                   _sds((b, N_HEADS, N_CHUNK, LANE, LANE))],
        scratch_shapes=[pltpu.VMEM((N_HEADS, LANE, LANE), F32), pltpu.VMEM((N_HEADS, LANE, LANE), F32)],
        compiler_params=_cp((ARB, ARB)),
    )(rq, rk, rv, rq, rk, rv, rkc, rvc, dec_f, dec_b)


def _k2_bwd(rq, rk, rv, do, sf_prev, sb_prev, rkc, rvc, dec_f, dec_b):
    b, l, _ = rq.shape

    def body(qf, kf, vf, gf, spf, qb, kb, vb, gb, spb, kc, vc, df, db,
             dqf, dkf, dvf, dqb, dkb, dvb, dkc, dvc, ddf, ddb, dsf, dsb):
        n = pl.program_id(1)

        @pl.when(n == 0)
        def _():
            dsf[...] = jnp.zeros((N_HEADS, LANE, LANE), F32)
            dsb[...] = jnp.zeros((N_HEADS, LANE, LANE), F32)

        def one(h, sl, q, k, v, g, sp, dec, ds, dq, dk, dv, dd, rev):
            def f(qv, kv_, vv, sv, dcy):
                return ret_chunk(qv, kv_, vv, sv, log_sigmoid(dcy), rev)

            _, vjp = jax.vjp(f, q[:, sl], k[:, sl], v[:, sl], sp[h], dec[h])
            gq, gk, gv, gs, gd = vjp((g[:, sl], ds[h]))
            dq[:, sl] = gq
            dk[:, sl] = gk
            dv[:, sl] = gv
            ds[h] = gs
            _acc(dd.at[h], jnp.broadcast_to(gd, (8, LANE)), n == 0)

        for h, sl in enumerate(_HEAD_SL):
            one(h, sl, qf, kf, vf, gf, spf, df, dsf, dqf, dkf, dvf, ddf, False)
            one(h, sl, qb, kb, vb, gb, spb, db, dsb, dqb, dkb, dvb, ddb, True)

        @pl.when(n == N_CHUNK - 1)
        def _():
            def f(kcv, vcv, dcy, rev):
                return ctx_state(kcv, vcv, log_sigmoid(dcy), rev)

            for h, sl in enumerate(_HEAD_SL):
                _, vjp_f = jax.vjp(functools.partial(f, rev=False), kc[:, sl], vc[:, sl], df[h])
                gk_f, gv_f, gd_f = vjp_f(dsf[h])
                _, vjp_b = jax.vjp(functools.partial(f, rev=True), kc[:, sl], vc[:, sl], db[h])
                gk_b, gv_b, gd_b = vjp_b(dsb[h])
                dkc[:, sl] = gk_f + gk_b
                dvc[:, sl] = gv_f + gv_b
                ddf[h] += jnp.broadcast_to(gd_f, (8, LANE))
                ddb[h] += jnp.broadcast_to(gd_b, (8, LANE))

    dd_spec = pl.BlockSpec((None, N_HEADS, 8, LANE), lambda i, n: (i, 0, 0, 0))
    return pl.pallas_call(
        body, name="k2_bwd", grid=(b, N_CHUNK),
        in_specs=[_chunk_spec(True)] * 4 + [_state_spec(True)] + [_chunk_spec(False)] * 4 + [_state_spec(False)]
        + [_CTX_SPEC, _CTX_SPEC, _DEC_SPEC, _DEC_SPEC],
        out_specs=[_chunk_spec(True)] * 3 + [_chunk_spec(False)] * 3 + [_CTX_SPEC, _CTX_SPEC, dd_spec, dd_spec],
        out_shape=[_sds((b, l, 512))] * 6 + [_sds((b, CTX_LEN, 512))] * 2 + [_sds((b, N_HEADS, 8, LANE))] * 2,
        scratch_shapes=[pltpu.VMEM((N_HEADS, LANE, LANE), F32), pltpu.VMEM((N_HEADS, LANE, LANE), F32)],
        compiler_params=_cp((ARB, ARB)),
    )(rq, rk, rv, do, sf_prev, rq, rk, rv, do, sb_prev, rkc, rvc, dec_f, dec_b)


TQ = 512
QK_W = 2 * LANE
_Q_PARTS = [slice(0, TQ // 2), slice(TQ // 2, TQ)]


SM_SCALE = 1.0 / math.sqrt(192.0)


def _k3_specs():
    qs = lambda w: pl.BlockSpec((None, TQ, w), lambda i, h, t: (i, t, h))
    ks = lambda w: pl.BlockSpec((None, KV_LEN, w), lambda i, h, t: (i, 0, h))
    return qs, ks


def _k3_fwd(q, k, v):
    b, l, _ = q.shape

    def body(q_ref, k_ref, v_ref, o_ref, lse_ref):
        kv_, vv = k_ref[...], v_ref[...]
        for r in _Q_PARTS:
            s = _dot(q_ref[r, :], kv_, 1, 1) * SM_SCALE
            m = jnp.max(s, axis=-1, keepdims=True)
            e = jnp.exp(s - m)
            tot = jnp.sum(e, axis=-1, keepdims=True)
            o_ref[r, :] = _dot(e, vv, 1, 0) * (1.0 / tot)
            lse_ref[r, :] = jnp.broadcast_to(m + jnp.log(tot), (TQ // 2, LANE))

    qs, ks = _k3_specs()
    return pl.pallas_call(
        body, name="k3_fwd", grid=(b, N_HEADS, l // TQ), in_specs=[qs(QK_W), ks(QK_W), ks(LANE)],
        out_specs=[qs(LANE), qs(LANE)], out_shape=[_sds((b, l, N_HEADS * LANE))] * 2,
        compiler_params=_cp((ARB, ARB, ARB)),
    )(q, k, v)


def _k3_bwd(q, k, v, o, lse, dy):
    b, l, _ = q.shape

    def body(q_ref, k_ref, v_ref, o_ref, lse_ref, dy_ref, dq_ref, dk_ref, dv_ref):
        t0 = pl.program_id(2) == 0
        kv_, vv = k_ref[...], v_ref[...]
        dvs, dks = [], []
        for r in _Q_PARTS:
            qv, dyv = q_ref[r, :], dy_ref[r, :]
            g = dyv.astype(BF)
            lse_col = jnp.max(lse_ref[r, :], axis=-1, keepdims=True)
            delta = jnp.sum(dyv * o_ref[r, :], axis=-1, keepdims=True)
            p = jnp.exp(_dot(qv, kv_, 1, 1) * SM_SCALE - lse_col)
            ds = (p * (_dot(g, vv, 1, 1) - delta) * SM_SCALE).astype(BF)
            dvs.append(_dot(p, g, 0, 0))
            dq_ref[r, :] = _dot(ds, kv_, 1, 0)
            dks.append(_dot(ds, qv, 0, 0))
        _acc(dv_ref, sum(dvs[1:], dvs[0]), t0)
        _acc(dk_ref, sum(dks[1:], dks[0]), t0)

    qs, ks = _k3_specs()
    return pl.pallas_call(
        body, name="k3_bwd", grid=(b, N_HEADS, l // TQ),
        in_specs=[qs(QK_W), ks(QK_W), ks(LANE), qs(LANE), qs(LANE), qs(LANE)],
        out_specs=[qs(QK_W), ks(QK_W), ks(LANE)],
        out_shape=[_sds((b, l, N_HEADS * QK_W)), _sds((b, KV_LEN, N_HEADS * QK_W)), _sds((b, KV_LEN, N_HEADS * LANE))],
        compiler_params=_cp((ARB, ARB, ARB)),
    )(q, k, v, o, lse, dy)


def _mod_rows(mod_ref, rows):
    return [mod_ref[r:r + 1, :] for r in rows]


def _k4a_fwd(x, o_f, o_b, rg, y_mla, g_ret, w_out, mod, g_ffn):
    b, l, _ = x.shape

    def body(x_ref, of_ref, ob_ref, rg_ref, ym_ref, gr_ref, wo_ref, mod_ref, gf_ref, xm_ref, h2_ref):
        gt_a, sh_f, sc_f = _mod_rows(mod_ref, (2, 3, 4))
        x_mid, h2 = k4a_tile(x_ref[...], of_ref[...], ob_ref[...], rg_ref[...], ym_ref[...], gr_ref[...], gt_a,
                             gf_ref[...], sh_f, sc_f, wo_ref[...], None)
        xm_ref[...] = x_mid
        h2_ref[...] = h2.astype(BF)

    tok = lambda w: pl.BlockSpec((None, TOK, w), lambda i, t: (i, t, 0))
    mod_spec = pl.BlockSpec((None, 8, D_MODEL), lambda i, t: (i, 0, 0))
    return pl.pallas_call(
        body, name="k4a_fwd", grid=(b, l // TOK),
        in_specs=[tok(D_MODEL), tok(512), tok(512), tok(512), tok(512), _full((1, 512)), _full((D_MODEL, D_MODEL)),
                  mod_spec, _full((1, D_MODEL))],
        out_specs=[tok(D_MODEL), tok(D_MODEL)], out_shape=[_sds((b, l, D_MODEL)), _sds((b, l, D_MODEL), BF)],
        compiler_params=_cp((ARB, ARB)),
    )(x, o_f, o_b, rg, y_mla, g_ret, w_out, mod, g_ffn)


TOK_M = 512


def _last_j(j, idx):
    return jnp.where(j == N_DEV - 1, idx, 0)


def _k4b_mlp_loss(h2, w1, w2, x_mid, mod, g_final, tgt):
    b, l, _ = h2.shape
    nt = l // TOK_M

    def body(h2_ref, w1_ref, w2_ref, xm_ref, mod_ref, gfin_ref, tgt_ref, dxm_ref, dmlp_ref, loss_ref, dgt_ref,
             dgfin_ref, acc):
        j, i, t = pl.program_id(0), pl.program_id(1), pl.program_id(2)
        rows = pl.ds(pl.multiple_of((i * nt + t) * TOK_M, TOK_M), TOK_M)
        a = _dot(h2_ref[...], w1_ref[...], 1, 0)
        part = _dot(jnp.square(jnp.maximum(a, 0.0)), w2_ref[...], 1, 0)
        _acc(acc.at[rows, :], part, j == 0)

        @pl.when(j == N_DEV - 1)
        def _():
            (gt_f,) = _mod_rows(mod_ref, (5,))
            loss, vjp = jax.vjp(k4c_tile, xm_ref[...], acc[rows, :], gt_f, gfin_ref[...], tgt_ref[...])
            dxm, dmlp, dgt, dgfin, _ = vjp(jnp.ones((1, 1), F32))
            dxm_ref[...] = dxm
            dmlp_ref[...] = dmlp.astype(BF)
            first = jnp.logical_and(i == 0, t == 0)
            _acc(loss_ref, jnp.broadcast_to(loss, (8, LANE)), first)
            _acc(dgfin_ref, dgfin, first)
            _acc(dgt_ref, dgt, t == 0)

    tok = lambda w: pl.BlockSpec((None, TOK_M, w), lambda j, i, t: (i, t, 0))
    tok_last = lambda w: pl.BlockSpec((None, TOK_M, w), lambda j, i, t: (_last_j(j, i), _last_j(j, t), 0))
    return pl.pallas_call(
        body, name="k4b_mlp_loss", grid=(N_DEV, b, nt),
        in_specs=[tok(D_MODEL), pl.BlockSpec((None, D_MODEL, FF_BLK), lambda j, i, t: (j, 0, 0)),
                  pl.BlockSpec((None, FF_BLK, D_MODEL), lambda j, i, t: (j, 0, 0)), tok_last(D_MODEL),
                  pl.BlockSpec((None, 8, D_MODEL), lambda j, i, t: (i, 0, 0)),
                  pl.BlockSpec((1, D_MODEL), lambda j, i, t: (0, 0)), tok_last(D_MODEL)],
        out_specs=[tok_last(D_MODEL), tok_last(D_MODEL), pl.BlockSpec((8, LANE), lambda j, i, t: (0, 0)),
                   pl.BlockSpec((None, 1, D_MODEL), lambda j, i, t: (_last_j(j, i), 0, 0)),
                   pl.BlockSpec((1, D_MODEL), lambda j, i, t: (0, 0))],
        out_shape=[_sds((b, l, D_MODEL)), _sds((b, l, D_MODEL), BF), _sds((8, LANE)), _sds((b, 1, D_MODEL)),
                   _sds((1, D_MODEL))],
        scratch_shapes=[pltpu.VMEM((b * l, D_MODEL), F32)],
        compiler_params=_cp((ARB, ARB, ARB)),
    )(h2, w1, w2, x_mid, mod, g_final, tgt)


def _k4d_mlp_bwd(h2, dmlp, w1, w2):
    b, l, _ = h2.shape
    nt = l // TOK_M

    def body(h2_ref, dm_ref, w1_ref, w2_ref, dh2_ref, dw1_ref, dw2_ref, acc1, acc2, dh2s):
        j, i, t = pl.program_id(0), pl.program_id(1), pl.program_id(2)
        first = jnp.logical_and(i == 0, t == 0)
        rows = pl.ds(pl.multiple_of((i * nt + t) * TOK_M, TOK_M), TOK_M)
        h2v, dm = h2_ref[...], dm_ref[...]
        r = jnp.maximum(_dot(h2v, w1_ref[...], 1, 0), 0.0)
        da = _dot(dm, w2_ref[...], 1, 1) * (2.0 * r)
        _acc(acc2, _dot(jnp.square(r), dm, 0, 0), first)
        _acc(acc1, _dot(h2v, da, 0, 0), first)
        _acc(dh2s.at[rows, :], _dot(da, w1_ref[...], 1, 1), j == 0)

        @pl.when(jnp.logical_and(i == b - 1, t == nt - 1))
        def _():
            dw1_ref[...] = acc1[...].astype(BF)
            dw2_ref[...] = acc2[...].astype(BF)

        @pl.when(j == N_DEV - 1)
        def _():
            dh2_ref[...] = dh2s[rows, :]

    tok = lambda w: pl.BlockSpec((None, TOK_M, w), lambda j, i, t: (i, t, 0))
    return pl.pallas_call(
        body, name="k4d_mlp_bwd", grid=(N_DEV, b, nt),
        in_specs=[tok(D_MODEL), tok(D_MODEL), pl.BlockSpec((None, D_MODEL, FF_BLK), lambda j, i, t: (j, 0, 0)),
                  pl.BlockSpec((None, FF_BLK, D_MODEL), lambda j, i, t: (j, 0, 0))],
        out_specs=[pl.BlockSpec((None, TOK_M, D_MODEL), lambda j, i, t: (_last_j(j, i), _last_j(j, t), 0)),
                   pl.BlockSpec((None, D_MODEL, FF_BLK), lambda j, i, t: (j, 0, 0)),
                   pl.BlockSpec((None, FF_BLK, D_MODEL), lambda j, i, t: (j, 0, 0))],
        out_shape=[_sds((b, l, D_MODEL)), _sds((N_DEV, D_MODEL, FF_BLK), BF), _sds((N_DEV, FF_BLK, D_MODEL), BF)],
        scratch_shapes=[pltpu.VMEM((D_MODEL, FF_BLK), F32), pltpu.VMEM((FF_BLK, D_MODEL), F32),
                        pltpu.VMEM((b * l, D_MODEL), F32)],
        compiler_params=_cp((ARB, ARB, ARB)),
    )(h2, dmlp, w1, w2)


def _k4e_bwd(x, o_f, o_b, rg, y_mla, g_ret, w_out, mod, g_ffn, dxm, dh2):
    b, l, _ = x.shape

    def body(x_ref, of_ref, ob_ref, rg_ref, ym_ref, gr_ref, wo_ref, mod_ref, gf_ref, dxm_ref, dh2_ref,
             dx_ref, do_ref, drg_ref, dym_ref, dwo_ref, dgr_ref, dgf_ref, dmod_ref):
        i, t = pl.program_id(0), pl.program_id(1)
        first = jnp.logical_and(i == 0, t == 0)
        gt_a, sh_f, sc_f = _mod_rows(mod_ref, (2, 3, 4))
        wo = wo_ref[...]

        def f(xv, ofv, rgv, ymv, grv, gta, gfv, shf, scf, p_out):
            return k4a_tile(xv, ofv, ob_ref[...], rgv, ymv, grv, gta, gfv, shf, scf, wo, p_out)

        _, vjp = jax.vjp(f, x_ref[...], of_ref[...], rg_ref[...], ym_ref[...], gr_ref[...], gt_a, gf_ref[...], sh_f,
                         sc_f, jnp.zeros((D_MODEL, D_MODEL), F32))
        dx, do, drg, dym, dgr, dgta, dgf, dshf, dscf, dwo = vjp((dxm_ref[...], dh2_ref[...]))
        dx_ref[...] = dx
        do_ref[...] = do
        drg_ref[...] = drg
        dym_ref[...] = dym
        _acc(dwo_ref, dwo, first)
        _acc(dgr_ref, dgr, first)
        _acc(dgf_ref, dgf, first)
        t0 = t == 0
        _acc(dmod_ref.at[2:3, :], dgta, t0)
        _acc(dmod_ref.at[3:4, :], dshf, t0)
        _acc(dmod_ref.at[4:5, :], dscf, t0)

        @pl.when(t0)
        def _():
            dmod_ref[0:2, :] = jnp.zeros((2, D_MODEL), F32)
            dmod_ref[5:8, :] = jnp.zeros((3, D_MODEL), F32)

    tok = lambda w: pl.BlockSpec((None, TOK_B, w), lambda i, t: (i, t, 0))
    mod_spec = pl.BlockSpec((None, 8, D_MODEL), lambda i, t: (i, 0, 0))
    return pl.pallas_call(
        body, name="k4e_bwd", grid=(b, l // TOK_B),
        in_specs=[tok(D_MODEL), tok(512), tok(512), tok(512), tok(512), _full((1, 512)), _full((D_MODEL, D_MODEL)),
                  mod_spec, _full((1, D_MODEL)), tok(D_MODEL), tok(D_MODEL)],
        out_specs=[tok(D_MODEL), tok(512), tok(512), tok(512), _full((D_MODEL, D_MODEL)), _full((1, 512)),
                   _full((1, D_MODEL)), mod_spec],
        out_shape=[_sds((b, l, D_MODEL)), _sds((b, l, 512)), _sds((b, l, 512)), _sds((b, l, 512)),
                   _sds((D_MODEL, D_MODEL)), _sds((1, 512)), _sds((1, D_MODEL)), _sds((b, 8, D_MODEL))],
        compiler_params=_cp((ARB, ARB)),
    )(x, o_f, o_b, rg, y_mla, g_ret, w_out, mod, g_ffn, dxm, dh2)


def _adamw(w, m, v, pieces, name, after=None):
    r, c = w.shape
    npc = pieces.shape[0]
    rb = r
    for cand in (256, 128, 64, 32, 16, 8):
        if r > cand and r % cand == 0 and cand * c * 4 * (npc + 7) * 2 <= 24 * 1024 * 1024:
            rb = cand
            break

    def body(w_ref, m_ref, v_ref, p_ref, *rest):
        g_ref, d_ref, nm_ref, nv_ref = rest[-4:]
        g = p_ref[0].astype(F32)
        for k in range(1, npc):
            g = g + p_ref[k].astype(F32)
        wv = w_ref[...]
        mn = ADAM_B1 * m_ref[...] + (1.0 - ADAM_B1) * g
        vn = ADAM_B2 * v_ref[...] + (1.0 - ADAM_B2) * jnp.square(g)
        m_hat = mn / (1.0 - ADAM_B1 ** ADAM_STEP)
        v_hat = vn / (1.0 - ADAM_B2 ** ADAM_STEP)
        g_ref[...] = g
        d_ref[...] = -ADAM_LR * (m_hat / (jnp.sqrt(v_hat) + ADAM_EPS) + ADAM_WD * wv)
        nm_ref[...] = mn
        nv_ref[...] = vn

    blk = pl.BlockSpec((rb, c), lambda i: (i, 0))
    extra = [] if after is None else [after]
    return pl.pallas_call(
        body, name=name, grid=(r // rb,),
        in_specs=[blk, blk, blk, pl.BlockSpec((npc, rb, c), lambda i: (0, i, 0))] + [ANY] * len(extra),
        out_specs=[blk] * 4, out_shape=[_sds((r, c))] * 4, compiler_params=_cp((ARB,)),
    )(w, m, v, pieces, *extra)


def _pad_heads(w, d):
    k = w.shape[0]
    return jnp.pad(w.reshape(k, N_HEADS, d), ((0, 0), (0, 0), (0, LANE - d))).reshape(k, N_HEADS * LANE)


def _cut_heads(w, d):
    k = w.shape[0]
    return w.reshape(k, N_HEADS, LANE)[:, :, :d].reshape(k, N_HEADS * d)


def _w_in_pad(w):
    w_a = jnp.concatenate([_pad_heads(w[:, 0:256], 64), _pad_heads(w[:, 256:512], 64), w[:, 512:1536]], axis=1)
    w_b = jnp.concatenate([w[:, 1536:2176], jnp.pad(w[:, 2176:2240], ((0, 0), (0, 64)))], axis=1)
    return w_a, w_b


def _w_in_cut(g_a, g_b):
    return jnp.concatenate([_cut_heads(g_a[:, 0:512], 64), _cut_heads(g_a[:, 512:1024], 64), g_a[:, 1024:2048],
                            g_b[:, 0:704]], axis=1)


def _w_uq_pad(w):
    return jnp.pad(w.reshape(384, N_HEADS, 192), ((0, 0), (0, 0), (0, 64))).reshape(384, 1024)


def _w_uq_cut(g):
    return g.reshape(384, N_HEADS, 256)[:, :, :192].reshape(384, 768)


def _w_ukv_perm(w):
    w = w.reshape(256, N_HEADS, 256)
    return jnp.concatenate([w[:, :, :128].reshape(256, 512), w[:, :, 128:].reshape(256, 512)], axis=1)


def _w_ukv_unperm(g):
    return jnp.concatenate([g[:, :512].reshape(256, N_HEADS, 128), g[:, 512:].reshape(256, N_HEADS, 128)],
                           axis=2).reshape(256, 1024)


def _unshard_cols(g):
    return jnp.transpose(g, (1, 0, 2)).reshape(g.shape[1], N_DEV * g.shape[2])


def _shard_cols(w):
    k, n = w.shape
    return jnp.transpose(w.reshape(k, N_DEV, n // N_DEV), (1, 0, 2))


def _rope_tables():
    rows = SEQ // GRID_W
    row = jnp.repeat(jnp.arange(rows, dtype=F32), GRID_W)
    col = jnp.tile(jnp.arange(GRID_W, dtype=F32), rows)
    freq = ROPE_BASE ** (-jnp.arange(16, dtype=F32) / 16)
    ang = jnp.concatenate([row[:, None] * freq, col[:, None] * freq], axis=-1)
    cos, sin = jnp.cos(ang), jnp.sin(ang)
    z = jnp.zeros((SEQ, 64), F32)
    return jnp.concatenate([cos, cos, z], axis=1), jnp.concatenate([-sin, sin, z], axis=1)


_PACKED = (("g_attn", 1024), ("g_ffn", 1024), ("ret_decay_fwd", 4), ("ret_decay_bwd", 4), ("g_ret", 512),
           ("g_q_lora", 384), ("g_kv_lora", 256), ("g_final", 1024))
_PACK_OFF = {}
_off = 0
for _name, _n in _PACKED:
    _PACK_OFF[_name] = _off
    _off += -(-_n // LANE) * LANE
PACK_W = _off


def _pack_small(vals):
    parts = []
    for name, n in _PACKED:
        a = vals[name].reshape(-1).astype(F32)
        parts.append(jnp.pad(a, (0, -(-n // LANE) * LANE - n)))
    return jnp.concatenate(parts).reshape(1, PACK_W)


def _adamw_small(params, packed, gcc, gb_ada):
    names = list(params)
    n_p = len(names)

    def body(*refs):
        p_ref, gcc_ref, gb_ref = refs[3 * n_p:3 * n_p + 3]
        outs = refs[3 * n_p + 3:]
        for k, name in enumerate(names):
            w_ref, m_ref, v_ref = refs[3 * k:3 * k + 3]
            n = w_ref.shape[1]
            if name == "b_ada":
                g = gb_ref[...]
            elif name == "c_ctx":
                g = gcc_ref[0, 0:1, :]
                for d in range(1, N_DEV):
                    g = g + gcc_ref[d, 0:1, :]
            else:
                off = _PACK_OFF[name]
                g = p_ref[0, :, off:off + n]
                for d in range(1, N_DEV):
                    g = g + p_ref[d, :, off:off + n]
            mn = ADAM_B1 * m_ref[...] + (1.0 - ADAM_B1) * g
            vn = ADAM_B2 * v_ref[...] + (1.0 - ADAM_B2) * jnp.square(g)
            m_hat = mn / (1.0 - ADAM_B1 ** ADAM_STEP)
            v_hat = vn / (1.0 - ADAM_B2 ** ADAM_STEP)
            outs[4 * k][...] = g
            outs[4 * k + 1][...] = -ADAM_LR * (m_hat / (jnp.sqrt(v_hat) + ADAM_EPS) + ADAM_WD * w_ref[...])
            outs[4 * k + 2][...] = mn
            outs[4 * k + 3][...] = vn

    args = [a for name in names for a in params[name]] + [packed, gcc, gb_ada]
    out_shape = [_sds(params[name][0].shape) for name in names for _ in range(4)]
    outs = pl.pallas_call(body, name="adamw_small", out_shape=out_shape, compiler_params=_cp())(*args)
    return {name: list(outs[4 * k:4 * k + 4]) for k, name in enumerate(names)}


def kernel(x, c, ctx, c_ctx, w_ada, b_ada, g_attn, g_ffn, w_in, ret_decay_fwd, ret_decay_bwd, g_ret, g_q_lora, w_uq, g_kv_lora, w_ukv, w_out, w_ff1, w_ff2, g_final, loss_target, m_c_ctx, m_w_ada, m_b_ada, m_g_attn, m_g_ffn, m_w_in, m_ret_decay_fwd, m_ret_decay_bwd, m_g_ret, m_g_q_lora, m_w_uq, m_g_kv_lora, m_w_ukv, m_w_out, m_w_ff1, m_w_ff2, m_g_final, v_c_ctx, v_w_ada, v_b_ada, v_g_attn, v_g_ffn, v_w_in, v_ret_decay_fwd, v_ret_decay_bwd, v_g_ret, v_g_q_lora, v_w_uq, v_g_kv_lora, v_w_ukv, v_w_out, v_w_ff1, v_w_ff2, v_g_final):
    me = 4 * lax.axis_index("x") + 2 * lax.axis_index("y") + lax.axis_index("c")
    nb = x.shape[0]

    c_pad = jnp.pad(c, ((0, 8 - nb), (0, 0)))
    c_all, g_in, g_uq, g_ukv = _gather_two_level([c_pad, w_in[0].astype(BF), w_uq[0].astype(BF), w_ukv[0].astype(BF)],
                                                 "gather_weights")
    ws = (*_w_in_pad(_unshard_cols(g_in)), _w_uq_pad(_unshard_cols(g_uq)), _w_ukv_perm(_unshard_cols(g_ukv)))

    crows = jnp.concatenate([c_all[:, :nb].reshape(N_DEV * nb, D_MODEL), c_ctx[None], jnp.zeros((7, D_MODEL), F32)])
    b_blk = lax.dynamic_slice(b_ada, (0, me * 768), (1, 768))
    (mod_g,) = _exchange([_mod_fwd(crows, w_ada[0], b_blk)], True, "gather_mod")
    mod_all = _unshard_cols(mod_g)
    behind = mod_g[0, 0, 0:1] * 0.0
    st_g = _exchange_start([(w_out[0] + behind).astype(BF), w_ff1[0].astype(BF), w_ff2[0].astype(BF)], True,
                           "gather_ff_start")
    mod_all = mod_all + st_g["token"][0:1, 0:1]
    mod_mine = lax.dynamic_slice(mod_all, (me * nb, 0), (nb, 6 * D_MODEL)).reshape(nb, 6, D_MODEL)
    mod = jnp.pad(mod_mine, ((0, 0), (0, 2), (0, 0)))
    mod_c = jnp.pad(mod_all[16].reshape(1, 6, D_MODEL), ((0, 0), (0, 2), (0, 0)))

    tabs = _rope_tables()
    dec_f = ret_decay_fwd.reshape(N_HEADS, 1, 1)
    dec_b = ret_decay_bwd.reshape(N_HEADS, 1, 1)

    rkc, rvc, k_ctx, v_ctx = _k1_fwd(ctx, mod_c, g_attn, g_q_lora, g_kv_lora, ws, tabs, None, True)
    rq, rk, rv, rg, q, k_all, v_all = _k1_fwd(x, mod, g_attn, g_q_lora, g_kv_lora, ws, tabs, (k_ctx, v_ctx), False)
    o_f, o_b, sf_prev, sb_prev = _k2_fwd(rq, rk, rv, rkc, rvc, dec_f, dec_b)
    y_mla, lse = _k3_fwd(q, k_all, v_all)
    g_out, g_ff1, g_ff2 = _exchange_wait(st_g, y_mla, "gather_ff_wait")
    wo = g_out.reshape(D_MODEL, D_MODEL)
    x_mid, h2 = _k4a_fwd(x, o_f, o_b, rg, y_mla, g_ret, wo, mod, g_ffn)
    dxm, dmlp, loss_acc, dgt_f, dg_final = _k4b_mlp_loss(h2, g_ff1, g_ff2, x_mid, mod, g_final.reshape(1, D_MODEL),
                                                         loss_target)

    dh2, dw1, dw2 = _k4d_mlp_bwd(h2, dmlp, g_ff1, g_ff2)
    st_s = _exchange_start([dw1, dw2], False, "scatter_ff_start")
    g_ret_t = g_ret + st_s["token"][0:1, 0:1]
    dx_res, do, drg, dym, dwo, dg_ret, dg_ffn, dmod_a = _k4e_bwd(x, o_f, o_b, rg, y_mla, g_ret_t, wo, mod, g_ffn, dxm, dh2)
    dq, dk_all, dv_all = _k3_bwd(q, k_all, v_all, y_mla, lse, dym)
    dqf, dkf, dvf, dqb, dkb, dvb, dkc, dvc, ddf, ddb = _k2_bwd(rq, rk, rv, do, sf_prev, sb_prev, rkc, rvc, dec_f, dec_b)
    cts = [[(dqf, 0), (dqb, 0)], [(dkf, 0), (dkb, 0)], [(dvf, 0), (dvb, 0)], [(drg, 0)], [(dq, 0)],
           [(dk_all, CTX_LEN)], [(dv_all, CTX_LEN)]]
    grad_x, accs, dmod_1 = _k1_bwd(x, mod, g_attn, g_q_lora, g_kv_lora, ws, tabs, cts, dx_res, None, False)
    cts_c = [[(dkc, 0)], [(dvc, 0)], [(dk_all, 0)], [(dv_all, 0)]]
    _, accs, dmod_c1 = _k1_bwd(ctx, mod_c, g_attn, g_q_lora, g_kv_lora, ws, tabs, cts_c, None, accs, True)
    dwa, dwb, dwq, dwk, dg_attn, dg_q, dg_kv = accs

    dmod_loc = (dmod_a + dmod_1).at[:, 5, :].set(dgt_f[:, 0, :])[:, :6, :].reshape(nb, 6 * D_MODEL)
    dmod_ctx = dmod_c1[:, :6, :].reshape(1, 6 * D_MODEL)
    small = {"g_attn": dg_attn, "g_ffn": dg_ffn, "ret_decay_fwd": jnp.sum(ddf[:, :, 0, 0], axis=0),
             "ret_decay_bwd": jnp.sum(ddb[:, :, 0, 0], axis=0), "g_ret": dg_ret, "g_q_lora": dg_q, "g_kv_lora": dg_kv,
             "g_final": dg_final}
    extra = jnp.concatenate([dmod_loc, dmod_ctx, jnp.zeros((5, 6 * D_MODEL), F32)])
    sm_g, ex_g, loss_g = _exchange([_pack_small(small), extra, loss_acc], True, "gather_small")
    dmod_all = ex_g[:, :nb].reshape(N_DEV * nb, 6 * D_MODEL)
    dmodc_parts = ex_g[:, nb]
    dmod_full = jnp.concatenate([dmod_all, jnp.zeros((8, 6 * D_MODEL), F32)])
    dmod_blk = lax.dynamic_slice(dmod_full, (0, me * 768), (24, 768))
    dmodc_blk = lax.dynamic_slice(dmodc_parts, (0, me * 768), (N_DEV, 768))
    gw_ada, gcc_part, gb_ada = _mod_bwd(crows, w_ada[0], dmod_blk, dmodc_blk, dmod_full, dmodc_parts)
    (gcc_g,) = _exchange([gcc_part], True, "gather_c_ctx")

    p_ff1, p_ff2 = _exchange_wait(st_s, gcc_g, "scatter_ff_wait")
    behind = gcc_g[0, 0, 0:1] * 0.0
    st_r = _exchange_start([_shard_cols(_w_in_cut(dwa, dwb)).astype(BF), _shard_cols(_w_uq_cut(dwq)).astype(BF),
                            _shard_cols(_w_ukv_unperm(dwk)).astype(BF),
                            (dwo.reshape(N_DEV, 128, D_MODEL) + behind).astype(BF)], False, "scatter_rest_start")

    res = {}
    early = (("w_ff1", w_ff1, m_w_ff1, v_w_ff1, p_ff1), ("w_ff2", w_ff2, m_w_ff2, v_w_ff2, p_ff2),
             ("w_ada", w_ada, m_w_ada, v_w_ada, gw_ada[None]))
    for name, w, m, v, pcs in early:
        res[name] = [a[None] for a in _adamw(w[0], m[0], v[0], pcs, "adamw_" + name, after=st_r["token"])]
    pieces = _exchange_wait(st_r, res["w_ada"][3], "scatter_rest_wait")
    late = (("w_in", w_in, m_w_in, v_w_in, pieces[0]), ("w_uq", w_uq, m_w_uq, v_w_uq, pieces[1]),
            ("w_ukv", w_ukv, m_w_ukv, v_w_ukv, pieces[2]), ("w_out", w_out, m_w_out, v_w_out, pieces[3]))
    for name, w, m, v, pcs in late:
        res[name] = [a[None] for a in _adamw(w[0], m[0], v[0], pcs, "adamw_" + name)]

    smalls = {"c_ctx": (c_ctx, m_c_ctx, v_c_ctx), "b_ada": (b_ada, m_b_ada, v_b_ada), "g_attn": (g_attn, m_g_attn, v_g_attn),
              "g_ffn": (g_ffn, m_g_ffn, v_g_ffn), "ret_decay_fwd": (ret_decay_fwd, m_ret_decay_fwd, v_ret_decay_fwd),
              "ret_decay_bwd": (ret_decay_bwd, m_ret_decay_bwd, v_ret_decay_bwd), "g_ret": (g_ret, m_g_ret, v_g_ret),
              "g_q_lora": (g_q_lora, m_g_q_lora, v_g_q_lora), "g_kv_lora": (g_kv_lora, m_g_kv_lora, v_g_kv_lora),
              "g_final": (g_final, m_g_final, v_g_final)}
    rows = {k: tuple(a.reshape(1, -1) for a in t) for k, t in smalls.items()}
    for name, outs in _adamw_small(rows, sm_g, gcc_g, gb_ada).items():
        res[name] = [o.reshape(smalls[name][0].shape) for o in outs]

    loss = loss_g[0, 0, 0]
    for k in range(1, N_DEV):
        loss = loss + loss_g[k, 0, 0]

    order = ("c_ctx", "w_ada", "b_ada", "g_attn", "g_ffn", "w_in", "ret_decay_fwd", "ret_decay_bwd", "g_ret", "g_q_lora",
             "w_uq", "g_kv_lora", "w_ukv", "w_out", "w_ff1", "w_ff2", "g_final")
    return (loss, grad_x, *[res[n][0] for n in order], *[res[n][1] for n in order], *[res[n][2] for n in order],
            *[res[n][3] for n in order])
```

```python
import functools
import math

import jax
import jax.numpy as jnp
from jax import lax
from jax.experimental import pallas as pl
from jax.experimental.pallas import tpu as pltpu

F32 = jnp.float32
BF = jnp.bfloat16
EPS = 1e-6
LANE = 128
N_DEV = 8
D_MODEL = 1024
SEQ = 2048
CTX_LEN = 256
GRID_W = 64
N_HEADS = 4
RET_CHUNK = 512
N_CHUNK = SEQ // RET_CHUNK
D_FF = 4096
FF_BLK = D_FF // N_DEV
IN_PAD = 2816
KV_LEN = CTX_LEN + SEQ
ROPE_BASE = 10000.0
ADAM_LR, ADAM_B1, ADAM_B2, ADAM_EPS, ADAM_WD, ADAM_STEP = 0.001, 0.9, 0.999, 1e-08, 0.01, 10
TOK = 256
TOK_B = 256
VMEM_LIMIT = 56 * 1024 * 1024
ARB = "arbitrary"
MESH = pl.DeviceIdType.MESH
_HEAD_SL = [slice(LANE * h, LANE * (h + 1)) for h in range(N_HEADS)]
W_SHAPES = [(D_MODEL, 2048), (D_MODEL, 768), (384, 1024), (256, 1024)]


def _dot(a, b, ca, cb):
    return lax.dot_general(a.astype(BF), b.astype(BF), (((ca,), (cb,)), ((), ())), preferred_element_type=F32)


@jax.custom_vjp
def mm(a, b):
    return _dot(a, b, 1, 0)


@jax.custom_vjp
def mm_nt(a, b):
    return _dot(a, b, 1, 1)


@jax.custom_vjp
def mm_tn(a, b):
    return _dot(a, b, 0, 0)


mm.defvjp(lambda a, b: (_dot(a, b, 1, 0), (a, b)), lambda r, g: (mm_nt(g, r[1]), mm_tn(r[0], g)))
mm_nt.defvjp(lambda a, b: (_dot(a, b, 1, 1), (a, b)), lambda r, g: (mm(g, r[1]), mm_tn(g, r[0])))
mm_tn.defvjp(lambda a, b: (_dot(a, b, 0, 0), (a, b)), lambda r, g: (mm_nt(r[1], g), mm(r[0], g)))


@jax.custom_vjp
def _mmw(a, w, probe):
    return _dot(a, w, 1, 0)


def _mmw_bwd(r, g):
    a, w = r
    return mm_nt(g, w), jnp.zeros_like(w), mm_tn(a, g)


_mmw.defvjp(lambda a, w, probe: (_dot(a, w, 1, 0), (a, w)), _mmw_bwd)


def mmw(a, w, probe):
    return _dot(a, w, 1, 0) if probe is None else _mmw(a, w, probe)


def rmsn(x, g):
    return x * lax.rsqrt(jnp.mean(x * x, axis=-1, keepdims=True) + EPS) * g


def silu(x):
    return x * jax.nn.sigmoid(x)


def _swap32_impl(x):
    n = x.shape[-1]
    lane = lax.broadcasted_iota(jnp.int32, x.shape, x.ndim - 1) % LANE
    up = pltpu.roll(x, n - 32, x.ndim - 1)
    dn = pltpu.roll(x, 32, x.ndim - 1)
    return jnp.where(lane < 32, up, jnp.where(lane < 64, dn, 0.0))


@jax.custom_vjp
def swap32(x):
    return _swap32_impl(x)


swap32.defvjp(lambda x: (_swap32_impl(x), None), lambda _, g: (_swap32_impl(g),))


def rope(x, cs, sn):
    return x * cs + swap32(x) * sn


def k1_tile(x, sh, sc, g_attn, g_q, g_kv, ws, ps, tabs, is_ctx):
    w_a, w_b, w_uq, w_ukv = ws
    p_a, p_b, p_uq, p_ukv = ps
    cs1, sn1 = tabs
    cs, sn = jnp.concatenate([cs1] * N_HEADS, axis=-1), jnp.concatenate([sn1] * N_HEADS, axis=-1)
    cq_t = jnp.concatenate([jnp.ones_like(cs1), cs1] * N_HEADS, axis=-1)
    sq_t = jnp.concatenate([jnp.zeros_like(sn1), sn1] * N_HEADS, axis=-1)
    h = rmsn(x, g_attn) * (1.0 + sc) + sh
    pa = mmw(h, w_a, p_a)
    pb = mmw(h, w_b, p_b)
    rk = pa[:, 512:1024] * 0.125
    rv = pa[:, 1024:1536]
    kpe = pb[:, 640:768]
    kv = mmw(rmsn(pb[:, 384:640], g_kv), w_ukv, p_ukv)
    if not is_ctx:
        rk = rope(rk, cs, sn)
        kpe = rope(kpe, cs1, sn1)
    k_full = jnp.concatenate([piece for sl in _HEAD_SL for piece in (kv[:, sl], kpe)], axis=-1)
    v = kv[:, 512:]
    if is_ctx:
        return rk, rv, k_full, v
    rq = rope(pa[:, 0:512], cs, sn)
    rg = pa[:, 1536:2048]
    q = rope(mmw(rmsn(pb[:, 0:384], g_q), w_uq, p_uq), cq_t, sq_t)
    return rq, rk, rv, rg, q, k_full, v


def log_sigmoid(x):
    return jnp.minimum(x, 0.0) - jnp.log(1.0 + jnp.exp(-jnp.abs(x)))


def ret_chunk(q, k, v, s, lg, reverse):
    c = RET_CHUNK
    ii = lax.broadcasted_iota(jnp.int32, (c, c), 0).astype(F32)
    jj = lax.broadcasted_iota(jnp.int32, (c, c), 1).astype(F32)
    diff = (jj - ii) if reverse else (ii - jj)
    dec = jnp.where(diff >= 0, jnp.exp(lg * jnp.maximum(diff, 0.0)), 0.0)
    pos = lax.broadcasted_iota(jnp.int32, (c, 1), 0).astype(F32)
    if reverse:
        wk, wq = jnp.exp(lg * pos), jnp.exp(lg * (c - pos))
    else:
        wk, wq = jnp.exp(lg * (c - 1.0 - pos)), jnp.exp(lg * (pos + 1.0))
    o = mm(mm_nt(q, k) * dec, v) + mm(q * wq, s)
    s_next = jnp.exp(lg * float(c)) * s + mm_tn(k * wk, v)
    return o, s_next


def ctx_state(kc, vc, lg, reverse):
    n = kc.shape[0]
    pos = lax.broadcasted_iota(jnp.int32, (n, 1), 0).astype(F32)
    w = jnp.exp(lg * pos) if reverse else jnp.exp(lg * (n - 1.0 - pos))
    return mm_tn(kc * w, vc)


def attn_head(qn, qp, kn, kp, v):
    s = (mm_nt(qn, kn) + mm_nt(qp, kp)) * (1.0 / math.sqrt(192.0))
    e = jnp.exp(s - jnp.max(s, axis=-1, keepdims=True))
    return mm(e / jnp.sum(e, axis=-1, keepdims=True), v)


def gn_gate(o, rg, g_ret):
    ys = []
    for h in range(N_HEADS):
        sl = slice(LANE * h, LANE * (h + 1))
        oh = o[:, sl]
        mu = jnp.mean(oh, axis=-1, keepdims=True)
        var = jnp.mean(jnp.square(oh - mu), axis=-1, keepdims=True)
        ys.append((oh - mu) * lax.rsqrt(var + EPS) * g_ret[:, sl])
    return jnp.concatenate(ys, axis=-1) * silu(rg)


def k4a_tile(x, o_f, o_b, rg, y_mla, g_ret, gt_a, g_ffn, sh_f, sc_f, w_out, p_out):
    mix = jnp.concatenate([gn_gate(o_f + o_b, rg, g_ret), y_mla], axis=-1)
    x_mid = x + gt_a * mmw(mix, w_out, p_out)
    h2 = rmsn(x_mid, g_ffn) * (1.0 + sc_f) + sh_f
    return x_mid, h2


def k4c_tile(x_mid, mlp, gt_f, g_final, tgt):
    y = rmsn(x_mid + gt_f * mlp, g_final)
    per_tok = jnp.mean(jnp.square(y - tgt), axis=-1, keepdims=True)
    return 0.5 * jnp.sum(per_tok, axis=0, keepdims=True)


def _cp(sem=None, vmem=VMEM_LIMIT):
    return pltpu.CompilerParams(dimension_semantics=sem, vmem_limit_bytes=vmem)


def _acc(ref, val, first):
    @pl.when(first)
    def _():
        ref[...] = val

    @pl.when(jnp.logical_not(first))
    def _():
        ref[...] += val


def _full(shape):
    nd = len(shape)
    return pl.BlockSpec(shape, lambda *_: (0,) * nd)


ANY = pl.BlockSpec(memory_space=pl.ANY)


def _sds(shape, dtype=F32):
    return jax.ShapeDtypeStruct(shape, dtype)


def _exchange(arrs, gather, name):
    n = len(arrs)
    out_shape = [_sds(((N_DEV,) + a.shape) if gather else a.shape, a.dtype) for a in arrs]

    def body(*refs):
        ins, outs = refs[:n], refs[n:2 * n]
        send_sems, recv_sems, local_sems = refs[2 * n:]
        x, y, c = lax.axis_index("x"), lax.axis_index("y"), lax.axis_index("c")
        me = 4 * x + 2 * y + c
        sends, recvs, locs = [], [], []
        for i in range(n):
            for k in range(N_DEV - 1):
                bits = k + 1
                px = x ^ ((bits >> 2) & 1)
                py = y ^ ((bits >> 1) & 1)
                pc = c ^ (bits & 1)
                peer = 4 * px + 2 * py + pc
                src = ins[i] if gather else ins[i].at[peer]
                sem = i * (N_DEV - 1) + k
                sends.append(pltpu.make_async_remote_copy(
                    src_ref=src, dst_ref=outs[i].at[me], send_sem=send_sems.at[sem], recv_sem=recv_sems.at[sem],
                    device_id=(px, py, pc), device_id_type=MESH))
                recvs.append(pltpu.make_async_remote_copy(
                    src_ref=src, dst_ref=outs[i].at[peer], send_sem=send_sems.at[sem], recv_sem=recv_sems.at[sem],
                    device_id=(px, py, pc), device_id_type=MESH))
            locs.append(pltpu.make_async_copy(ins[i] if gather else ins[i].at[me], outs[i].at[me], local_sems.at[i]))
        for cp in locs + sends:
            cp.start()
        for cp in recvs:
            cp.wait_recv()
        for cp in sends:
            cp.wait_send()
        for cp in locs:
            cp.wait()

    outs = pl.pallas_call(
        body, name=name, out_shape=out_shape, in_specs=[ANY] * n, out_specs=[ANY] * n,
        scratch_shapes=[pltpu.SemaphoreType.DMA((n * (N_DEV - 1),)), pltpu.SemaphoreType.DMA((n * (N_DEV - 1),)),
                        pltpu.SemaphoreType.DMA((n,))],
    )(*arrs)
    return list(outs)


def _gather_two_level(arrs, name):
    n = len(arrs)

    def body(*refs):
        ins, outs = refs[:n], refs[n:2 * n]
        send_sems, recv_sems, local_sems = refs[2 * n:]
        x, y, c = lax.axis_index("x"), lax.axis_index("y"), lax.axis_index("c")
        sibling = (x, y, 1 - c)
        chips = [(1 - x, y), (x, 1 - y), (1 - x, 1 - y)]

        def slot(px, py, pc):
            return 4 * px + 2 * py + pc

        first, passed, waits, locs = [], [], [], []
        for i in range(n):
            def copy(k, block, to, src=None, i=i):
                dst = outs[i].at[slot(*block)]
                return pltpu.make_async_remote_copy(
                    src_ref=dst if src is None else src, dst_ref=dst, send_sem=send_sems.at[7 * i + k],
                    recv_sem=recv_sems.at[7 * i + k], device_id=to, device_id_type=MESH)

            locs.append(pltpu.make_async_copy(ins[i], outs[i].at[slot(x, y, c)], local_sems.at[i]))
            first.append(copy(0, (x, y, c), sibling, src=ins[i]))
            first += [copy(1 + j, (x, y, c), (*chip, c), src=ins[i]) for j, chip in enumerate(chips)]
            passed.append([copy(4 + j, (*chip, c), sibling) for j, chip in enumerate(chips)])
            waits.append([copy(1 + j, (*chip, c), (x, y, c)) for j, chip in enumerate(chips)])
        for cp in locs + first:
            cp.start()
        for j in range(3):
            for i in range(n):
                waits[i][j].wait_recv()
                passed[i][j].start()
        for i in range(n):
            def arrival(k, block, i=i):
                dst = outs[i].at[slot(*block)]
                return pltpu.make_async_remote_copy(
                    src_ref=dst, dst_ref=dst, send_sem=send_sems.at[7 * i + k], recv_sem=recv_sems.at[7 * i + k],
                    device_id=sibling, device_id_type=MESH)

            arrival(0, (x, y, 1 - c)).wait_recv()
            for j, chip in enumerate(chips):
                arrival(4 + j, (*chip, 1 - c)).wait_recv()
        for cp in first + [p for ps in passed for p in ps]:
            cp.wait_send()
        for cp in locs:
            cp.wait()

    outs = pl.pallas_call(
        body, name=name, out_shape=[_sds((N_DEV,) + a.shape, a.dtype) for a in arrs], in_specs=[ANY] * n,
        out_specs=[ANY] * n,
        scratch_shapes=[pltpu.SemaphoreType.DMA((7 * n,)), pltpu.SemaphoreType.DMA((7 * n,)),
                        pltpu.SemaphoreType.DMA((n,))],
    )(*arrs)
    return list(outs)


HBM = pl.BlockSpec(memory_space=pltpu.HBM)
SEM = pl.BlockSpec(memory_space=pltpu.SEMAPHORE)
EFFECT = pltpu.SideEffectType.DATAFLOW_SIDE_EFFECTING


def _peer(k):
    x, y, c = lax.axis_index("x"), lax.axis_index("y"), lax.axis_index("c")
    bits = k + 1
    px, py, pc = x ^ ((bits >> 2) & 1), y ^ ((bits >> 1) & 1), c ^ (bits & 1)
    return (px, py, pc), 4 * px + 2 * py + pc, 4 * x + 2 * y + c


def _exchange_start(arrs, gather, name):
    n = len(arrs)
    lands = [pltpu.with_memory_space_constraint(lax.empty(((N_DEV,) + a.shape) if gather else a.shape, a.dtype),
                                                pltpu.HBM) for a in arrs]
    srcs = [pltpu.with_memory_space_constraint(a, pltpu.HBM) for a in arrs]

    def body(*refs):
        ins, zones = refs[:n], refs[n:2 * n]
        send_sems, recv_sems, local_sems = refs[2 * n:2 * n + 3]
        token = refs[-1]
        for i in range(n):
            for k in range(N_DEV - 1):
                dev, peer, me = _peer(k)
                sem = i * (N_DEV - 1) + k
                pltpu.make_async_remote_copy(
                    src_ref=ins[i] if gather else ins[i].at[peer], dst_ref=zones[i].at[me],
                    send_sem=send_sems.at[sem], recv_sem=recv_sems.at[sem], device_id=dev, device_id_type=MESH).start()
            _, _, me = _peer(0)
            pltpu.make_async_copy(ins[i] if gather else ins[i].at[me], zones[i].at[me], local_sems.at[i]).start()
        token[...] = jnp.zeros_like(token)

    nsem = n * (N_DEV - 1)
    outs = pl.pallas_call(
        body, name=name,
        out_shape=[pltpu.SemaphoreType.DMA((nsem,)), pltpu.SemaphoreType.DMA((nsem,)), pltpu.SemaphoreType.DMA((n,))]
        + [pltpu.HBM(a.shape, a.dtype) for a in srcs] + [pltpu.HBM(z.shape, z.dtype) for z in lands]
        + [_sds((8, LANE))],
        in_specs=[HBM] * (2 * n),
        out_specs=[SEM, SEM, SEM] + [HBM] * (2 * n) + [pl.BlockSpec(memory_space=pltpu.VMEM)],
        input_output_aliases={i: 3 + i for i in range(2 * n)},
        compiler_params=pltpu.CompilerParams(has_side_effects=EFFECT),
    )(*srcs, *lands)
    return {"n": n, "gather": gather, "sems": outs[:3], "srcs": outs[3:3 + n], "lands": outs[3 + n:3 + 2 * n],
            "token": outs[-1]}


def _exchange_wait(st, after, name):
    n, gather = st["n"], st["gather"]

    def body(*refs):
        ins, zones = refs[:n], refs[n:2 * n]
        send_sems, recv_sems, local_sems = refs[2 * n:2 * n + 3]
        for i in range(n):
            for k in range(N_DEV - 1):
                dev, peer, me = _peer(k)
                sem = i * (N_DEV - 1) + k
                src = ins[i] if gather else ins[i].at[peer]
                cp = pltpu.make_async_remote_copy(
                    src_ref=src, dst_ref=zones[i].at[peer], send_sem=send_sems.at[sem], recv_sem=recv_sems.at[sem],
                    device_id=dev, device_id_type=MESH)
                cp.wait_send()
                cp.wait_recv()
            _, _, me = _peer(0)
            pltpu.make_async_copy(ins[i] if gather else ins[i].at[me], zones[i].at[me], local_sems.at[i]).wait()

    outs = pl.pallas_call(
        body, name=name,
        out_shape=[pltpu.HBM(a.shape, a.dtype) for a in st["srcs"]] + [pltpu.HBM(z.shape, z.dtype) for z in st["lands"]],
        in_specs=[HBM] * (2 * n) + [SEM, SEM, SEM, ANY], out_specs=[HBM] * (2 * n),
        input_output_aliases={i: i for i in range(2 * n)},
        compiler_params=pltpu.CompilerParams(has_side_effects=EFFECT),
    )(*st["srcs"], *st["lands"], *st["sems"], after)
    return list(outs[n:])


def _mod_fwd(crows, w_ada, b_blk):
    def body(c_ref, w_ref, b_ref, o_ref):
        o_ref[...] = mm(silu(c_ref[...]), w_ref[...]) + b_ref[...]

    return pl.pallas_call(body, name="mod_fwd", out_shape=_sds((24, 768)), compiler_params=_cp())(crows, w_ada, b_blk)


def _mod_bwd(crows, w_ada, dmod_blk, dmodc_blk, dmod_full, dmodc_full):
    def body(c_ref, w_ref, d_ref, dc_ref, df_ref, dcf_ref, gw_ref, gc_ref, gb_ref):
        cr = c_ref[...]
        dc, dcf = dc_ref[0:1, :], dcf_ref[0:1, :]
        for p in range(1, N_DEV):
            dc = dc + dc_ref[p:p + 1, :]
            dcf = dcf + dcf_ref[p:p + 1, :]
        row = lax.broadcasted_iota(jnp.int32, (24, 1), 0)
        gw_ref[...] = mm_tn(silu(cr), jnp.where(row == 16, dc, d_ref[...]))
        cc = cr[16:17, :]
        sg = jax.nn.sigmoid(cc)
        part = mm_nt(jnp.broadcast_to(dc, (8, 768)), w_ref[...])
        gc_ref[...] = part * (sg * (1.0 + cc * (1.0 - sg)))
        gb_ref[...] = jnp.sum(df_ref[...], axis=0, keepdims=True) + dcf

    return pl.pallas_call(
        body, name="mod_bwd", out_shape=[_sds((D_MODEL, 768)), _sds((8, D_MODEL)), _sds((1, 6 * D_MODEL))],
        compiler_params=_cp())(crows, w_ada, dmod_blk, dmodc_blk, dmod_full, dmodc_full)


def _tab_specs(tk):
    return [pl.BlockSpec((tk, LANE), lambda i, t: (t, 0))] * 2


def _k1_fwd(x, mod, g_attn, g_q, g_kv, ws, tabs, kv_all, is_ctx):
    b, l, _ = x.shape
    nt = l // TOK
    n_f32 = 2 if is_ctx else 4

    def body(x_ref, mod_ref, ga_ref, gq_ref, gk_ref, wa_ref, wb_ref, wq_ref, wk_ref, cs_ref, sn_ref, *rest):
        outs = rest if is_ctx else rest[2:]
        res = k1_tile(x_ref[...], mod_ref[0:1, :], mod_ref[1:2, :], ga_ref[...], gq_ref[...], gk_ref[...],
                      (wa_ref[...], wb_ref[...], wq_ref[...], wk_ref[...]), (None,) * 4,
                      (cs_ref[...], sn_ref[...]), is_ctx)
        for o_ref, r in zip(outs, res):
            o_ref[...] = r.astype(o_ref.dtype)

    tok = lambda w, off=0: pl.BlockSpec((None, TOK, w), lambda i, t: (i, t + off, 0))
    mod_spec = pl.BlockSpec((None, 8, D_MODEL), (lambda i, t: (0, 0, 0)) if is_ctx else (lambda i, t: (i, 0, 0)))
    kv_off = 0 if is_ctx else CTX_LEN // TOK
    in_specs = ([tok(D_MODEL), mod_spec, _full((1, D_MODEL)), _full((1, 384)), _full((1, 256))]
                + [_full(s) for s in W_SHAPES] + _tab_specs(TOK))
    args = [x, mod, g_attn, g_q, g_kv, *ws, *tabs]
    out_specs = [tok(512)] * n_f32 + ([] if is_ctx else [tok(1024)]) + [tok(1024, kv_off), tok(512, kv_off)]
    out_shape = ([_sds((b, l, 512))] * n_f32 + ([] if is_ctx else [_sds((b, l, 1024), BF)])
                 + [_sds((b, KV_LEN, 1024), BF), _sds((b, KV_LEN, 512), BF)])
    aliases = {}
    if not is_ctx:
        aliases = {len(args): n_f32 + 1, len(args) + 1: n_f32 + 2}
        in_specs += [ANY, ANY]
        args += list(kv_all)
    return pl.pallas_call(
        body, name="k1_fwd_ctx" if is_ctx else "k1_fwd", grid=(b, nt), in_specs=in_specs, out_specs=out_specs,
        out_shape=out_shape, input_output_aliases=aliases, compiler_params=_cp((ARB, ARB)),
    )(*args)


N_ACC = 7


def _k1_bwd(x, mod, g_attn, g_q, g_kv, ws, tabs, cts, dx_res, init, is_ctx):
    b, l, _ = x.shape
    tk = TOK_B
    nt = l // tk
    flat_cts = [a for group in cts for a in group]
    group_sizes = [len(g) for g in cts]
    n_ct = len(flat_cts)
    has_res = dx_res is not None
    has_init = init is not None
    acc_shapes = W_SHAPES + [(1, D_MODEL), (1, 384), (1, 256)]

    def body(*refs):
        it = iter(refs)
        x_ref, mod_ref, ga_ref, gq_ref, gk_ref = [next(it) for _ in range(5)]
        w_hbm = [next(it) for _ in range(4)]
        tab_refs = [next(it) for _ in range(2)]
        ct_refs = [next(it) for _ in range(n_ct)]
        res_ref = next(it) if has_res else None
        init_refs = [next(it) for _ in range(N_ACC)] if has_init else None
        gx_ref = next(it) if not is_ctx else None
        out_hbm = [next(it) for _ in range(N_ACC)]
        dmod_ref = next(it)
        w_vmem = [next(it) for _ in range(4)]
        accs = [next(it) for _ in range(N_ACC)]
        sem = next(it)
        i, t = pl.program_id(0), pl.program_id(1)
        first = jnp.logical_and(i == 0, t == 0)
        last = jnp.logical_and(i == b - 1, t == nt - 1)

        @pl.when(first)
        def _():
            for src, dst in zip(w_hbm, w_vmem):
                pltpu.sync_copy(src, dst)
            for k in range(N_ACC):
                if has_init:
                    pltpu.sync_copy(init_refs[k], accs[k])
                else:
                    accs[k][...] = jnp.zeros(acc_shapes[k], F32)

        ct_vals, pos = [], 0
        for gsz in group_sizes:
            v = ct_refs[pos][...].astype(F32)
            for r in ct_refs[pos + 1:pos + gsz]:
                v = v + r[...]
            ct_vals.append(v)
            pos += gsz
        wv = tuple(r[...] for r in w_vmem)
        tv = tuple(r[...] for r in tab_refs)

        def f(xv, sh, sc, ga, gq, gk, *probes):
            return k1_tile(xv, sh, sc, ga, gq, gk, wv, probes, tv, is_ctx)

        probes = [jnp.zeros(s, F32) for s in W_SHAPES]
        _, vjp = jax.vjp(f, x_ref[...], mod_ref[0:1, :], mod_ref[1:2, :], ga_ref[...], gq_ref[...], gk_ref[...], *probes)
        dx, dsh, dsc, dga, dgq, dgk, dwa, dwb, dwq, dwk = vjp(tuple(ct_vals))
        if not is_ctx:
            gx_ref[...] = dx + res_ref[...] if has_res else dx
        for ref, val in zip(accs, (dwa, dwb, dwq, dwk, dga, dgq, dgk)):
            ref[...] += val
        t0 = first if is_ctx else t == 0
        _acc(dmod_ref.at[0:1, :], dsh, t0)
        _acc(dmod_ref.at[1:2, :], dsc, t0)

        @pl.when(t0)
        def _():
            dmod_ref[2:8, :] = jnp.zeros((6, D_MODEL), F32)

        @pl.when(last)
        def _():
            cps = [pltpu.make_async_copy(accs[k], out_hbm[k], sem.at[k]) for k in range(N_ACC)]
            for cp in cps:
                cp.start()
            for cp in cps:
                cp.wait()

    tok = lambda w, off=0: pl.BlockSpec((None, tk, w), lambda i, t: (i, t + off, 0))
    mod_spec = pl.BlockSpec((None, 8, D_MODEL), (lambda i, t: (0, 0, 0)) if is_ctx else (lambda i, t: (i, 0, 0)))
    in_specs = ([tok(D_MODEL), mod_spec, _full((1, D_MODEL)), _full((1, 384)), _full((1, 256))] + [ANY] * 4
                + _tab_specs(tk))
    args = [x, mod, g_attn, g_q, g_kv, *ws, *tabs]
    for a, off in flat_cts:
        in_specs.append(tok(a.shape[-1], off // tk))
        args.append(a)
    if has_res:
        in_specs.append(tok(D_MODEL))
        args.append(dx_res)
    if has_init:
        in_specs += [ANY] * N_ACC
        args += list(init)
    out_shape, out_specs = [], []
    if not is_ctx:
        out_shape.append(_sds((b, l, D_MODEL)))
        out_specs.append(tok(D_MODEL))
    out_shape += [_sds(s) for s in acc_shapes] + [_sds((1 if is_ctx else b, 8, D_MODEL))]
    out_specs += [ANY] * N_ACC + [mod_spec]
    outs = pl.pallas_call(
        body, name="k1_bwd_ctx" if is_ctx else "k1_bwd", grid=(b, nt), in_specs=in_specs, out_specs=out_specs,
        out_shape=out_shape,
        scratch_shapes=[pltpu.VMEM(s, BF) for s in W_SHAPES] + [pltpu.VMEM(s, F32) for s in acc_shapes]
        + [pltpu.SemaphoreType.DMA((N_ACC,))],
        compiler_params=_cp((ARB, ARB)),
    )(*args)
    outs = list(outs)
    gx = None if is_ctx else outs.pop(0)
    return gx, outs[:N_ACC], outs[N_ACC]


def _chunk_spec(rev):
    if rev:
        return pl.BlockSpec((None, RET_CHUNK, 512), lambda i, n: (i, N_CHUNK - 1 - n, 0))
    return pl.BlockSpec((None, RET_CHUNK, 512), lambda i, n: (i, n, 0))


def _state_spec(rev):
    if rev:
        return pl.BlockSpec((None, N_HEADS, None, LANE, LANE), lambda i, n: (i, 0, N_CHUNK - 1 - n, 0, 0))
    return pl.BlockSpec((None, N_HEADS, None, LANE, LANE), lambda i, n: (i, 0, n, 0, 0))


_CTX_SPEC = pl.BlockSpec((None, CTX_LEN, 512), lambda i, n: (i, 0, 0))
_DEC_SPEC = pl.BlockSpec((N_HEADS, 1, 1), lambda i, n: (0, 0, 0))


def _k2_fwd(rq, rk, rv, rkc, rvc, dec_f, dec_b):
    b = rq.shape[0]

    def body(qf, kf, vf, qb, kb, vb, kc, vc, df, db, of_ref, ob_ref, sf_out, sb_out, sf, sb):
        n = pl.program_id(1)
        for h, sl in enumerate(_HEAD_SL):
            lgf, lgb = log_sigmoid(df[h]), log_sigmoid(db[h])

            @pl.when(n == 0)
            def _():
                sf[h] = ctx_state(kc[:, sl], vc[:, sl], lgf, False)
                sb[h] = ctx_state(kc[:, sl], vc[:, sl], lgb, True)

            sf_out[h] = sf[h]
            sb_out[h] = sb[h]
            o, s = ret_chunk(qf[:, sl], kf[:, sl], vf[:, sl], sf[h], lgf, False)
            of_ref[:, sl] = o
            sf[h] = s
            o, s = ret_chunk(qb[:, sl], kb[:, sl], vb[:, sl], sb[h], lgb, True)
            ob_ref[:, sl] = o
            sb[h] = s

    l = rq.shape[1]
    return pl.pallas_call(
        body, name="k2_fwd", grid=(b, N_CHUNK),
        in_specs=[_chunk_spec(False)] * 3 + [_chunk_spec(True)] * 3 + [_CTX_SPEC, _CTX_SPEC, _DEC_SPEC, _DEC_SPEC],
        out_specs=[_chunk_spec(False), _chunk_spec(True), _state_spec(False), _state_spec(True)],
        out_shape=[_sds((b, l, 512)), _sds((b, l, 512)), _sds((b, N_HEADS, N_CHUNK, LANE, LANE)),
                   _sds((b, N_HEADS, N_CHUNK, LANE, LANE))],
        scratch_shapes=[pltpu.VMEM((N_HEADS, LANE, LANE), F32), pltpu.VMEM((N_HEADS, LANE, LANE), F32)],
        compiler_params=_cp((ARB, ARB)),
    )(rq, rk, rv, rq, rk, rv, rkc, rvc, dec_f, dec_b)


def _k2_bwd(rq, rk, rv, do, sf_prev, sb_prev, rkc, rvc, dec_f, dec_b):
    b, l, _ = rq.shape

    def body(qf, kf, vf, gf, spf, qb, kb, vb, gb, spb, kc, vc, df, db,
             dqf, dkf, dvf, dqb, dkb, dvb, dkc, dvc, ddf, ddb, dsf, dsb):
        n = pl.program_id(1)

        @pl.when(n == 0)
        def _():
            dsf[...] = jnp.zeros((N_HEADS, LANE, LANE), F32)
            dsb[...] = jnp.zeros((N_HEADS, LANE, LANE), F32)

        def one(h, sl, q, k, v, g, sp, dec, ds, dq, dk, dv, dd, rev):
            def f(qv, kv_, vv, sv, dcy):
                return ret_chunk(qv, kv_, vv, sv, log_sigmoid(dcy), rev)

            _, vjp = jax.vjp(f, q[:, sl], k[:, sl], v[:, sl], sp[h], dec[h])
            gq, gk, gv, gs, gd = vjp((g[:, sl], ds[h]))
            dq[:, sl] = gq
            dk[:, sl] = gk
            dv[:, sl] = gv
            ds[h] = gs
            _acc(dd.at[h], jnp.broadcast_to(gd, (8, LANE)), n == 0)

        for h, sl in enumerate(_HEAD_SL):
            one(h, sl, qf, kf, vf, gf, spf, df, dsf, dqf, dkf, dvf, ddf, False)
            one(h, sl, qb, kb, vb, gb, spb, db, dsb, dqb, dkb, dvb, ddb, True)

        @pl.when(n == N_CHUNK - 1)
        def _():
            def f(kcv, vcv, dcy, rev):
                return ctx_state(kcv, vcv, log_sigmoid(dcy), rev)

            for h, sl in enumerate(_HEAD_SL):
                _, vjp_f = jax.vjp(functools.partial(f, rev=False), kc[:, sl], vc[:, sl], df[h])
                gk_f, gv_f, gd_f = vjp_f(dsf[h])
                _, vjp_b = jax.vjp(functools.partial(f, rev=True), kc[:, sl], vc[:, sl], db[h])
                gk_b, gv_b, gd_b = vjp_b(dsb[h])
                dkc[:, sl] = gk_f + gk_b
                dvc[:, sl] = gv_f + gv_b
                ddf[h] += jnp.broadcast_to(gd_f, (8, LANE))
                ddb[h] += jnp.broadcast_to(gd_b, (8, LANE))

    dd_spec = pl.BlockSpec((None, N_HEADS, 8, LANE), lambda i, n: (i, 0, 0, 0))
    return pl.pallas_call(
        body, name="k2_bwd", grid=(b, N_CHUNK),
        in_specs=[_chunk_spec(True)] * 4 + [_state_spec(True)] + [_chunk_spec(False)] * 4 + [_state_spec(False)]
        + [_CTX_SPEC, _CTX_SPEC, _DEC_SPEC, _DEC_SPEC],
        out_specs=[_chunk_spec(True)] * 3 + [_chunk_spec(False)] * 3 + [_CTX_SPEC, _CTX_SPEC, dd_spec, dd_spec],
        out_shape=[_sds((b, l, 512))] * 6 + [_sds((b, CTX_LEN, 512))] * 2 + [_sds((b, N_HEADS, 8, LANE))] * 2,
        scratch_shapes=[pltpu.VMEM((N_HEADS, LANE, LANE), F32), pltpu.VMEM((N_HEADS, LANE, LANE), F32)],
        compiler_params=_cp((ARB, ARB)),
    )(rq, rk, rv, do, sf_prev, rq, rk, rv, do, sb_prev, rkc, rvc, dec_f, dec_b)


TQ = 512
QK_W = 2 * LANE
_Q_PARTS = [slice(0, TQ // 2), slice(TQ // 2, TQ)]


SM_SCALE = 1.0 / math.sqrt(192.0)


def _k3_specs():
    qs = lambda w: pl.BlockSpec((None, TQ, w), lambda i, h, t: (i, t, h))
    ks = lambda w: pl.BlockSpec((None, KV_LEN, w), lambda i, h, t: (i, 0, h))
    return qs, ks


def _k3_fwd(q, k, v):
    b, l, _ = q.shape

    def body(q_ref, k_ref, v_ref, o_ref, lse_ref):
        kv_, vv = k_ref[...], v_ref[...]
        for r in _Q_PARTS:
            s = _dot(q_ref[r, :], kv_, 1, 1) * SM_SCALE
            m = jnp.max(s, axis=-1, keepdims=True)
            e = jnp.exp(s - m)
            tot = jnp.sum(e, axis=-1, keepdims=True)
            o_ref[r, :] = _dot(e, vv, 1, 0) * (1.0 / tot)
            lse_ref[r, :] = jnp.broadcast_to(m + jnp.log(tot), (TQ // 2, LANE))

    qs, ks = _k3_specs()
    return pl.pallas_call(
        body, name="k3_fwd", grid=(b, N_HEADS, l // TQ), in_specs=[qs(QK_W), ks(QK_W), ks(LANE)],
        out_specs=[qs(LANE), qs(LANE)], out_shape=[_sds((b, l, N_HEADS * LANE))] * 2,
        compiler_params=_cp((ARB, ARB, ARB)),
    )(q, k, v)


def _k3_bwd(q, k, v, o, lse, dy):
    b, l, _ = q.shape

    def body(q_ref, k_ref, v_ref, o_ref, lse_ref, dy_ref, dq_ref, dk_ref, dv_ref):
        t0 = pl.program_id(2) == 0
        kv_, vv = k_ref[...], v_ref[...]
        dvs, dks = [], []
        for r in _Q_PARTS:
            qv, dyv = q_ref[r, :], dy_ref[r, :]
            g = dyv.astype(BF)
            lse_col = jnp.max(lse_ref[r, :], axis=-1, keepdims=True)
            delta = jnp.sum(dyv * o_ref[r, :], axis=-1, keepdims=True)
            p = jnp.exp(_dot(qv, kv_, 1, 1) * SM_SCALE - lse_col)
            ds = (p * (_dot(g, vv, 1, 1) - delta) * SM_SCALE).astype(BF)
            dvs.append(_dot(p, g, 0, 0))
            dq_ref[r, :] = _dot(ds, kv_, 1, 0)
            dks.append(_dot(ds, qv, 0, 0))
        _acc(dv_ref, sum(dvs[1:], dvs[0]), t0)
        _acc(dk_ref, sum(dks[1:], dks[0]), t0)

    qs, ks = _k3_specs()
    return pl.pallas_call(
        body, name="k3_bwd", grid=(b, N_HEADS, l // TQ),
        in_specs=[qs(QK_W), ks(QK_W), ks(LANE), qs(LANE), qs(LANE), qs(LANE)],
        out_specs=[qs(QK_W), ks(QK_W), ks(LANE)],
        out_shape=[_sds((b, l, N_HEADS * QK_W)), _sds((b, KV_LEN, N_HEADS * QK_W)), _sds((b, KV_LEN, N_HEADS * LANE))],
        compiler_params=_cp((ARB, ARB, ARB)),
    )(q, k, v, o, lse, dy)


def _mod_rows(mod_ref, rows):
    return [mod_ref[r:r + 1, :] for r in rows]


def _k4a_fwd(x, o_f, o_b, rg, y_mla, g_ret, w_out, mod, g_ffn):
    b, l, _ = x.shape

    def body(x_ref, of_ref, ob_ref, rg_ref, ym_ref, gr_ref, wo_ref, mod_ref, gf_ref, xm_ref, h2_ref):
        gt_a, sh_f, sc_f = _mod_rows(mod_ref, (2, 3, 4))
        x_mid, h2 = k4a_tile(x_ref[...], of_ref[...], ob_ref[...], rg_ref[...], ym_ref[...], gr_ref[...], gt_a,
                             gf_ref[...], sh_f, sc_f, wo_ref[...], None)
        xm_ref[...] = x_mid
        h2_ref[...] = h2.astype(BF)

    tok = lambda w: pl.BlockSpec((None, TOK, w), lambda i, t: (i, t, 0))
    mod_spec = pl.BlockSpec((None, 8, D_MODEL), lambda i, t: (i, 0, 0))
    return pl.pallas_call(
        body, name="k4a_fwd", grid=(b, l // TOK),
        in_specs=[tok(D_MODEL), tok(512), tok(512), tok(512), tok(512), _full((1, 512)), _full((D_MODEL, D_MODEL)),
                  mod_spec, _full((1, D_MODEL))],
        out_specs=[tok(D_MODEL), tok(D_MODEL)], out_shape=[_sds((b, l, D_MODEL)), _sds((b, l, D_MODEL), BF)],
        compiler_params=_cp((ARB, ARB)),
    )(x, o_f, o_b, rg, y_mla, g_ret, w_out, mod, g_ffn)


TOK_M = 512
TOK_D = 1024
HALF_FF = D_FF // 2


def _k4b_mlp_loss(h2, w1t, w2, x_mid, mod, g_final, tgt):
    b, l, _ = h2.shape
    nt = l // TOK_M

    def body(h2_ref, w1_hbm, w2_hbm, xm_ref, mod_ref, gfin_ref, tgt_ref, dxm_ref, dmlp_ref, r_ref, loss_ref, dgt_ref,
             dgfin_ref, w1_v, w2_v):
        i, t = pl.program_id(0), pl.program_id(1)
        first = jnp.logical_and(i == 0, t == 0)

        @pl.when(first)
        def _():
            pltpu.sync_copy(w1_hbm, w1_v)
            pltpu.sync_copy(w2_hbm, w2_v)

        h2v = h2_ref[...]
        mlp = None
        for half in range(2):
            rows = slice(half * HALF_FF, (half + 1) * HALF_FF)
            r = jnp.maximum(_dot(h2v, w1_v[rows, :], 1, 1), 0.0)
            r_ref[:, rows] = r.astype(BF)
            part = _dot(jnp.square(r), w2_v[rows, :], 1, 0)
            mlp = part if mlp is None else mlp + part
        (gt_f,) = _mod_rows(mod_ref, (5,))
        loss, vjp = jax.vjp(k4c_tile, xm_ref[...], mlp, gt_f, gfin_ref[...], tgt_ref[...])
        dxm, dmlp, dgt, dgfin, _ = vjp(jnp.ones((1, 1), F32))
        dxm_ref[...] = dxm
        dmlp_ref[...] = dmlp.astype(BF)
        _acc(loss_ref, jnp.broadcast_to(loss, (8, LANE)), first)
        _acc(dgfin_ref, dgfin, first)
        _acc(dgt_ref, dgt, t == 0)

    tok = lambda w: pl.BlockSpec((None, TOK_M, w), lambda i, t: (i, t, 0))
    return pl.pallas_call(
        body, name="k4b_mlp_loss", grid=(b, nt),
        in_specs=[tok(D_MODEL), ANY, ANY, tok(D_MODEL), pl.BlockSpec((None, 8, D_MODEL), lambda i, t: (i, 0, 0)),
                  _full((1, D_MODEL)), tok(D_MODEL)],
        out_specs=[tok(D_MODEL), tok(D_MODEL), tok(D_FF), _full((8, LANE)),
                   pl.BlockSpec((None, 1, D_MODEL), lambda i, t: (i, 0, 0)), _full((1, D_MODEL))],
        out_shape=[_sds((b, l, D_MODEL)), _sds((b, l, D_MODEL), BF), _sds((b, l, D_FF), BF), _sds((8, LANE)),
                   _sds((b, 1, D_MODEL)), _sds((1, D_MODEL))],
        scratch_shapes=[pltpu.VMEM((D_FF, D_MODEL), BF), pltpu.VMEM((D_FF, D_MODEL), BF)],
        compiler_params=_cp((ARB, ARB)),
    )(h2, w1t, w2, x_mid, mod, g_final, tgt)


def _k4d_mlp_bwd(h2, dmlp, r, w2):
    b, l, _ = h2.shape
    nt = l // TOK_D

    def body(h2_ref, dm_ref, r_ref, w2_ref, da_ref, dw1_ref, dw2_ref, acc1, acc2):
        i, t = pl.program_id(1), pl.program_id(2)
        first = jnp.logical_and(i == 0, t == 0)
        rv = r_ref[...].astype(F32)
        dm = dm_ref[...]
        da = (_dot(dm, w2_ref[...], 1, 1) * (2.0 * rv)).astype(BF)
        da_ref[...] = da
        _acc(acc2, _dot(jnp.square(rv), dm, 0, 0), first)
        _acc(acc1, _dot(h2_ref[...], da, 0, 0), first)

        @pl.when(jnp.logical_and(i == b - 1, t == nt - 1))
        def _():
            dw1_ref[...] = acc1[...].astype(BF)
            dw2_ref[...] = acc2[...].astype(BF)

    tok = lambda w: pl.BlockSpec((None, TOK_D, w), lambda j, i, t: (i, t, 0))
    col = pl.BlockSpec((None, TOK_D, FF_BLK), lambda j, i, t: (i, t, j))
    return pl.pallas_call(
        body, name="k4d_mlp_bwd", grid=(N_DEV, b, nt),
        in_specs=[tok(D_MODEL), tok(D_MODEL), col, pl.BlockSpec((None, FF_BLK, D_MODEL), lambda j, i, t: (j, 0, 0))],
        out_specs=[col, pl.BlockSpec((None, D_MODEL, FF_BLK), lambda j, i, t: (j, 0, 0)),
                   pl.BlockSpec((None, FF_BLK, D_MODEL), lambda j, i, t: (j, 0, 0))],
        out_shape=[_sds((b, l, D_FF), BF), _sds((N_DEV, D_MODEL, FF_BLK), BF), _sds((N_DEV, FF_BLK, D_MODEL), BF)],
        scratch_shapes=[pltpu.VMEM((D_MODEL, FF_BLK), F32), pltpu.VMEM((FF_BLK, D_MODEL), F32)],
        compiler_params=_cp((ARB, ARB, ARB)),
    )(h2, dmlp, r, w2)


def _k4f_dh2(da, w1t, after):
    b, l, _ = da.shape

    def body(da_ref, w_ref, after_ref, o_ref):
        o_ref[...] = _dot(da_ref[...], w_ref[...], 1, 0)

    return pl.pallas_call(
        body, name="k4f_dh2", grid=(b, l // TOK_M),
        in_specs=[pl.BlockSpec((None, TOK_M, D_FF), lambda i, t: (i, t, 0)), _full((D_FF, D_MODEL)), ANY],
        out_specs=pl.BlockSpec((None, TOK_M, D_MODEL), lambda i, t: (i, t, 0)), out_shape=_sds((b, l, D_MODEL)),
        compiler_params=_cp((ARB, ARB)),
    )(da, w1t, after)


def _k4e_bwd(x, o_f, o_b, rg, y_mla, g_ret, w_out, mod, g_ffn, dxm, dh2):
    b, l, _ = x.shape

    def body(x_ref, of_ref, ob_ref, rg_ref, ym_ref, gr_ref, wo_ref, mod_ref, gf_ref, dxm_ref, dh2_ref,
             dx_ref, do_ref, drg_ref, dym_ref, dwo_ref, dgr_ref, dgf_ref, dmod_ref):
        i, t = pl.program_id(0), pl.program_id(1)
        first = jnp.logical_and(i == 0, t == 0)
        gt_a, sh_f, sc_f = _mod_rows(mod_ref, (2, 3, 4))
        wo = wo_ref[...]

        def f(xv, ofv, rgv, ymv, grv, gta, gfv, shf, scf, p_out):
            return k4a_tile(xv, ofv, ob_ref[...], rgv, ymv, grv, gta, gfv, shf, scf, wo, p_out)

        _, vjp = jax.vjp(f, x_ref[...], of_ref[...], rg_ref[...], ym_ref[...], gr_ref[...], gt_a, gf_ref[...], sh_f,
                         sc_f, jnp.zeros((D_MODEL, D_MODEL), F32))
        dx, do, drg, dym, dgr, dgta, dgf, dshf, dscf, dwo = vjp((dxm_ref[...], dh2_ref[...]))
        dx_ref[...] = dx
        do_ref[...] = do
        drg_ref[...] = drg
        dym_ref[...] = dym
        _acc(dwo_ref, dwo, first)
        _acc(dgr_ref, dgr, first)
        _acc(dgf_ref, dgf, first)
        t0 = t == 0
        _acc(dmod_ref.at[2:3, :], dgta, t0)
        _acc(dmod_ref.at[3:4, :], dshf, t0)
        _acc(dmod_ref.at[4:5, :], dscf, t0)

        @pl.when(t0)
        def _():
            dmod_ref[0:2, :] = jnp.zeros((2, D_MODEL), F32)
            dmod_ref[5:8, :] = jnp.zeros((3, D_MODEL), F32)

    tok = lambda w: pl.BlockSpec((None, TOK_B, w), lambda i, t: (i, t, 0))
    mod_spec = pl.BlockSpec((None, 8, D_MODEL), lambda i, t: (i, 0, 0))
    return pl.pallas_call(
        body, name="k4e_bwd", grid=(b, l // TOK_B),
        in_specs=[tok(D_MODEL), tok(512), tok(512), tok(512), tok(512), _full((1, 512)), _full((D_MODEL, D_MODEL)),
                  mod_spec, _full((1, D_MODEL)), tok(D_MODEL), tok(D_MODEL)],
        out_specs=[tok(D_MODEL), tok(512), tok(512), tok(512), _full((D_MODEL, D_MODEL)), _full((1, 512)),
                   _full((1, D_MODEL)), mod_spec],
        out_shape=[_sds((b, l, D_MODEL)), _sds((b, l, 512)), _sds((b, l, 512)), _sds((b, l, 512)),
                   _sds((D_MODEL, D_MODEL)), _sds((1, 512)), _sds((1, D_MODEL)), _sds((b, 8, D_MODEL))],
        compiler_params=_cp((ARB, ARB)),
    )(x, o_f, o_b, rg, y_mla, g_ret, w_out, mod, g_ffn, dxm, dh2)


def _adamw(w, m, v, pieces, name, after=None):
    r, c = w.shape
    npc = pieces.shape[0]
    rb = r
    for cand in (256, 128, 64, 32, 16, 8):
        if r > cand and r % cand == 0 and cand * c * 4 * (npc + 7) * 2 <= 24 * 1024 * 1024:
            rb = cand
            break

    def body(w_ref, m_ref, v_ref, p_ref, *rest):
        g_ref, d_ref, nm_ref, nv_ref = rest[-4:]
        g = p_ref[0].astype(F32)
        for k in range(1, npc):
            g = g + p_ref[k].astype(F32)
        wv = w_ref[...]
        mn = ADAM_B1 * m_ref[...] + (1.0 - ADAM_B1) * g
        vn = ADAM_B2 * v_ref[...] + (1.0 - ADAM_B2) * jnp.square(g)
        m_hat = mn / (1.0 - ADAM_B1 ** ADAM_STEP)
        v_hat = vn / (1.0 - ADAM_B2 ** ADAM_STEP)
        g_ref[...] = g
        d_ref[...] = -ADAM_LR * (m_hat / (jnp.sqrt(v_hat) + ADAM_EPS) + ADAM_WD * wv)
        nm_ref[...] = mn
        nv_ref[...] = vn

    blk = pl.BlockSpec((rb, c), lambda i: (i, 0))
    extra = [] if after is None else [after]
    return pl.pallas_call(
        body, name=name, grid=(r // rb,),
        in_specs=[blk, blk, blk, pl.BlockSpec((npc, rb, c), lambda i: (0, i, 0))] + [ANY] * len(extra),
        out_specs=[blk] * 4, out_shape=[_sds((r, c))] * 4, compiler_params=_cp((ARB,)),
    )(w, m, v, pieces, *extra)


def _pad_heads(w, d):
    k = w.shape[0]
    return jnp.pad(w.reshape(k, N_HEADS, d), ((0, 0), (0, 0), (0, LANE - d))).reshape(k, N_HEADS * LANE)


def _cut_heads(w, d):
    k = w.shape[0]
    return w.reshape(k, N_HEADS, LANE)[:, :, :d].reshape(k, N_HEADS * d)


def _w_in_pad(w):
    w_a = jnp.concatenate([_pad_heads(w[:, 0:256], 64), _pad_heads(w[:, 256:512], 64), w[:, 512:1536]], axis=1)
    w_b = jnp.concatenate([w[:, 1536:2176], jnp.pad(w[:, 2176:2240], ((0, 0), (0, 64)))], axis=1)
    return w_a, w_b


def _w_in_cut(g_a, g_b):
    return jnp.concatenate([_cut_heads(g_a[:, 0:512], 64), _cut_heads(g_a[:, 512:1024], 64), g_a[:, 1024:2048],
                            g_b[:, 0:704]], axis=1)


def _w_uq_pad(w):
    return jnp.pad(w.reshape(384, N_HEADS, 192), ((0, 0), (0, 0), (0, 64))).reshape(384, 1024)


def _w_uq_cut(g):
    return g.reshape(384, N_HEADS, 256)[:, :, :192].reshape(384, 768)


def _w_ukv_perm(w):
    w = w.reshape(256, N_HEADS, 256)
    return jnp.concatenate([w[:, :, :128].reshape(256, 512), w[:, :, 128:].reshape(256, 512)], axis=1)


def _w_ukv_unperm(g):
    return jnp.concatenate([g[:, :512].reshape(256, N_HEADS, 128), g[:, 512:].reshape(256, N_HEADS, 128)],
                           axis=2).reshape(256, 1024)


def _unshard_cols(g):
    return jnp.transpose(g, (1, 0, 2)).reshape(g.shape[1], N_DEV * g.shape[2])


def _shard_cols(w):
    k, n = w.shape
    return jnp.transpose(w.reshape(k, N_DEV, n // N_DEV), (1, 0, 2))


def _rope_tables():
    rows = SEQ // GRID_W
    row = jnp.repeat(jnp.arange(rows, dtype=F32), GRID_W)
    col = jnp.tile(jnp.arange(GRID_W, dtype=F32), rows)
    freq = ROPE_BASE ** (-jnp.arange(16, dtype=F32) / 16)
    ang = jnp.concatenate([row[:, None] * freq, col[:, None] * freq], axis=-1)
    cos, sin = jnp.cos(ang), jnp.sin(ang)
    z = jnp.zeros((SEQ, 64), F32)
    return jnp.concatenate([cos, cos, z], axis=1), jnp.concatenate([-sin, sin, z], axis=1)


_PACKED = (("g_attn", 1024), ("g_ffn", 1024), ("ret_decay_fwd", 4), ("ret_decay_bwd", 4), ("g_ret", 512),
           ("g_q_lora", 384), ("g_kv_lora", 256), ("g_final", 1024))
_PACK_OFF = {}
_off = 0
for _name, _n in _PACKED:
    _PACK_OFF[_name] = _off
    _off += -(-_n // LANE) * LANE
PACK_W = _off


def _pack_small(vals):
    parts = []
    for name, n in _PACKED:
        a = vals[name].reshape(-1).astype(F32)
        parts.append(jnp.pad(a, (0, -(-n // LANE) * LANE - n)))
    return jnp.concatenate(parts).reshape(1, PACK_W)


def _adamw_small(params, packed, gcc, gb_ada):
    names = list(params)
    n_p = len(names)

    def body(*refs):
        p_ref, gcc_ref, gb_ref = refs[3 * n_p:3 * n_p + 3]
        outs = refs[3 * n_p + 3:]
        for k, name in enumerate(names):
            w_ref, m_ref, v_ref = refs[3 * k:3 * k + 3]
            n = w_ref.shape[1]
            if name == "b_ada":
                g = gb_ref[...]
            elif name == "c_ctx":
                g = gcc_ref[0, 0:1, :]
                for d in range(1, N_DEV):
                    g = g + gcc_ref[d, 0:1, :]
            else:
                off = _PACK_OFF[name]
                g = p_ref[0, :, off:off + n]
                for d in range(1, N_DEV):
                    g = g + p_ref[d, :, off:off + n]
            mn = ADAM_B1 * m_ref[...] + (1.0 - ADAM_B1) * g
            vn = ADAM_B2 * v_ref[...] + (1.0 - ADAM_B2) * jnp.square(g)
            m_hat = mn / (1.0 - ADAM_B1 ** ADAM_STEP)
            v_hat = vn / (1.0 - ADAM_B2 ** ADAM_STEP)
            outs[4 * k][...] = g
            outs[4 * k + 1][...] = -ADAM_LR * (m_hat / (jnp.sqrt(v_hat) + ADAM_EPS) + ADAM_WD * w_ref[...])
            outs[4 * k + 2][...] = mn
            outs[4 * k + 3][...] = vn

    args = [a for name in names for a in params[name]] + [packed, gcc, gb_ada]
    out_shape = [_sds(params[name][0].shape) for name in names for _ in range(4)]
    outs = pl.pallas_call(body, name="adamw_small", out_shape=out_shape, compiler_params=_cp())(*args)
    return {name: list(outs[4 * k:4 * k + 4]) for k, name in enumerate(names)}


def kernel(x, c, ctx, c_ctx, w_ada, b_ada, g_attn, g_ffn, w_in, ret_decay_fwd, ret_decay_bwd, g_ret, g_q_lora, w_uq, g_kv_lora, w_ukv, w_out, w_ff1, w_ff2, g_final, loss_target, m_c_ctx, m_w_ada, m_b_ada, m_g_attn, m_g_ffn, m_w_in, m_ret_decay_fwd, m_ret_decay_bwd, m_g_ret, m_g_q_lora, m_w_uq, m_g_kv_lora, m_w_ukv, m_w_out, m_w_ff1, m_w_ff2, m_g_final, v_c_ctx, v_w_ada, v_b_ada, v_g_attn, v_g_ffn, v_w_in, v_ret_decay_fwd, v_ret_decay_bwd, v_g_ret, v_g_q_lora, v_w_uq, v_g_kv_lora, v_w_ukv, v_w_out, v_w_ff1, v_w_ff2, v_g_final):
    me = 4 * lax.axis_index("x") + 2 * lax.axis_index("y") + lax.axis_index("c")
    nb = x.shape[0]

    c_pad = jnp.pad(c, ((0, 8 - nb), (0, 0)))
    c_all, g_in, g_uq, g_ukv = _gather_two_level([c_pad, w_in[0].astype(BF), w_uq[0].astype(BF), w_ukv[0].astype(BF)],
                                                 "gather_weights")
    ws = (*_w_in_pad(_unshard_cols(g_in)), _w_uq_pad(_unshard_cols(g_uq)), _w_ukv_perm(_unshard_cols(g_ukv)))

    crows = jnp.concatenate([c_all[:, :nb].reshape(N_DEV * nb, D_MODEL), c_ctx[None], jnp.zeros((7, D_MODEL), F32)])
    b_blk = lax.dynamic_slice(b_ada, (0, me * 768), (1, 768))
    (mod_g,) = _exchange([_mod_fwd(crows, w_ada[0], b_blk)], True, "gather_mod")
    mod_all = _unshard_cols(mod_g)
    behind = mod_g[0, 0, 0:1] * 0.0
    st_g = _exchange_start([(w_out[0] + behind).astype(BF), w_ff1[0].T.astype(BF), w_ff2[0].astype(BF)], True,
                           "gather_ff_start")
    mod_all = mod_all + st_g["token"][0:1, 0:1]
    mod_mine = lax.dynamic_slice(mod_all, (me * nb, 0), (nb, 6 * D_MODEL)).reshape(nb, 6, D_MODEL)
    mod = jnp.pad(mod_mine, ((0, 0), (0, 2), (0, 0)))
    mod_c = jnp.pad(mod_all[16].reshape(1, 6, D_MODEL), ((0, 0), (0, 2), (0, 0)))

    tabs = _rope_tables()
    dec_f = ret_decay_fwd.reshape(N_HEADS, 1, 1)
    dec_b = ret_decay_bwd.reshape(N_HEADS, 1, 1)

    rkc, rvc, k_ctx, v_ctx = _k1_fwd(ctx, mod_c, g_attn, g_q_lora, g_kv_lora, ws, tabs, None, True)
    rq, rk, rv, rg, q, k_all, v_all = _k1_fwd(x, mod, g_attn, g_q_lora, g_kv_lora, ws, tabs, (k_ctx, v_ctx), False)
    o_f, o_b, sf_prev, sb_prev = _k2_fwd(rq, rk, rv, rkc, rvc, dec_f, dec_b)
    y_mla, lse = _k3_fwd(q, k_all, v_all)
    g_out, g_ff1t, g_ff2 = _exchange_wait(st_g, y_mla, "gather_ff_wait")
    wo = g_out.reshape(D_MODEL, D_MODEL)
    w1t = g_ff1t.reshape(D_FF, D_MODEL)
    x_mid, h2 = _k4a_fwd(x, o_f, o_b, rg, y_mla, g_ret, wo, mod, g_ffn)
    dxm, dmlp, relu_a, loss_acc, dgt_f, dg_final = _k4b_mlp_loss(h2, w1t, g_ff2.reshape(D_FF, D_MODEL), x_mid, mod,
                                                                 g_final.reshape(1, D_MODEL), loss_target)

    da, dw1, dw2 = _k4d_mlp_bwd(h2, dmlp, relu_a, g_ff2)
    st_s = _exchange_start([dw1, dw2], False, "scatter_ff_start")
    dh2 = _k4f_dh2(da, w1t, st_s["token"])
    g_ret_t = g_ret + st_s["token"][0:1, 0:1]
    dx_res, do, drg, dym, dwo, dg_ret, dg_ffn, dmod_a = _k4e_bwd(x, o_f, o_b, rg, y_mla, g_ret_t, wo, mod, g_ffn, dxm, dh2)
    dq, dk_all, dv_all = _k3_bwd(q, k_all, v_all, y_mla, lse, dym)
    dqf, dkf, dvf, dqb, dkb, dvb, dkc, dvc, ddf, ddb = _k2_bwd(rq, rk, rv, do, sf_prev, sb_prev, rkc, rvc, dec_f, dec_b)
    cts = [[(dqf, 0), (dqb, 0)], [(dkf, 0), (dkb, 0)], [(dvf, 0), (dvb, 0)], [(drg, 0)], [(dq, 0)],
           [(dk_all, CTX_LEN)], [(dv_all, CTX_LEN)]]
    grad_x, accs, dmod_1 = _k1_bwd(x, mod, g_attn, g_q_lora, g_kv_lora, ws, tabs, cts, dx_res, None, False)
    cts_c = [[(dkc, 0)], [(dvc, 0)], [(dk_all, 0)], [(dv_all, 0)]]
    _, accs, dmod_c1 = _k1_bwd(ctx, mod_c, g_attn, g_q_lora, g_kv_lora, ws, tabs, cts_c, None, accs, True)
    dwa, dwb, dwq, dwk, dg_attn, dg_q, dg_kv = accs

    dmod_loc = (dmod_a + dmod_1).at[:, 5, :].set(dgt_f[:, 0, :])[:, :6, :].reshape(nb, 6 * D_MODEL)
    dmod_ctx = dmod_c1[:, :6, :].reshape(1, 6 * D_MODEL)
    small = {"g_attn": dg_attn, "g_ffn": dg_ffn, "ret_decay_fwd": jnp.sum(ddf[:, :, 0, 0], axis=0),
             "ret_decay_bwd": jnp.sum(ddb[:, :, 0, 0], axis=0), "g_ret": dg_ret, "g_q_lora": dg_q, "g_kv_lora": dg_kv,
             "g_final": dg_final}
    extra = jnp.concatenate([dmod_loc, dmod_ctx, jnp.zeros((5, 6 * D_MODEL), F32)])
    sm_g, ex_g, loss_g = _exchange([_pack_small(small), extra, loss_acc], True, "gather_small")
    dmod_all = ex_g[:, :nb].reshape(N_DEV * nb, 6 * D_MODEL)
    dmodc_parts = ex_g[:, nb]
    dmod_full = jnp.concatenate([dmod_all, jnp.zeros((8, 6 * D_MODEL), F32)])
    dmod_blk = lax.dynamic_slice(dmod_full, (0, me * 768), (24, 768))
    dmodc_blk = lax.dynamic_slice(dmodc_parts, (0, me * 768), (N_DEV, 768))
    gw_ada, gcc_part, gb_ada = _mod_bwd(crows, w_ada[0], dmod_blk, dmodc_blk, dmod_full, dmodc_parts)
    (gcc_g,) = _exchange([gcc_part], True, "gather_c_ctx")

    p_ff1, p_ff2 = _exchange_wait(st_s, gcc_g, "scatter_ff_wait")
    behind = gcc_g[0, 0, 0:1] * 0.0
    st_r = _exchange_start([_shard_cols(_w_in_cut(dwa, dwb)).astype(BF), _shard_cols(_w_uq_cut(dwq)).astype(BF),
                            _shard_cols(_w_ukv_unperm(dwk)).astype(BF),
                            (dwo.reshape(N_DEV, 128, D_MODEL) + behind).astype(BF)], False, "scatter_rest_start")

    res = {}
    early = (("w_ff1", w_ff1, m_w_ff1, v_w_ff1, p_ff1), ("w_ff2", w_ff2, m_w_ff2, v_w_ff2, p_ff2),
             ("w_ada", w_ada, m_w_ada, v_w_ada, gw_ada[None]))
    for name, w, m, v, pcs in early:
        res[name] = [a[None] for a in _adamw(w[0], m[0], v[0], pcs, "adamw_" + name, after=st_r["token"])]
    pieces = _exchange_wait(st_r, res["w_ada"][3], "scatter_rest_wait")
    late = (("w_in", w_in, m_w_in, v_w_in, pieces[0]), ("w_uq", w_uq, m_w_uq, v_w_uq, pieces[1]),
            ("w_ukv", w_ukv, m_w_ukv, v_w_ukv, pieces[2]), ("w_out", w_out, m_w_out, v_w_out, pieces[3]))
    for name, w, m, v, pcs in late:
        res[name] = [a[None] for a in _adamw(w[0], m[0], v[0], pcs, "adamw_" + name)]

    smalls = {"c_ctx": (c_ctx, m_c_ctx, v_c_ctx), "b_ada": (b_ada, m_b_ada, v_b_ada), "g_attn": (g_attn, m_g_attn, v_g_attn),
              "g_ffn": (g_ffn, m_g_ffn, v_g_ffn), "ret_decay_fwd": (ret_decay_fwd, m_ret_decay_fwd, v_ret_decay_fwd),
              "ret_decay_bwd": (ret_decay_bwd, m_ret_decay_bwd, v_ret_decay_bwd), "g_ret": (g_ret, m_g_ret, v_g_ret),
              "g_q_lora": (g_q_lora, m_g_q_lora, v_g_q_lora), "g_kv_lora": (g_kv_lora, m_g_kv_lora, v_g_kv_lora),
              "g_final": (g_final, m_g_final, v_g_final)}
    rows = {k: tuple(a.reshape(1, -1) for a in t) for k, t in smalls.items()}
    for name, outs in _adamw_small(rows, sm_g, gcc_g, gb_ada).items():
        res[name] = [o.reshape(smalls[name][0].shape) for o in outs]

    loss = loss_g[0, 0, 0]
    for k in range(1, N_DEV):
        loss = loss + loss_g[k, 0, 0]

    order = ("c_ctx", "w_ada", "b_ada", "g_attn", "g_ffn", "w_in", "ret_decay_fwd", "ret_decay_bwd", "g_ret", "g_q_lora",
             "w_uq", "g_kv_lora", "w_ukv", "w_out", "w_ff1", "w_ff2", "g_final")
    return (loss, grad_x, *[res[n][0] for n in order], *[res[n][1] for n in order], *[res[n][2] for n in order],
            *[res[n][3] for n in order])
```

```python
import functools
import math

import jax
import jax.numpy as jnp
from jax import lax
from jax.experimental import pallas as pl
from jax.experimental.pallas import tpu as pltpu

F32 = jnp.float32
BF = jnp.bfloat16
EPS = 1e-6
LANE = 128
N_DEV = 8
D_MODEL = 1024
SEQ = 2048
CTX_LEN = 256
GRID_W = 64
N_HEADS = 4
RET_CHUNK = 512
N_CHUNK = SEQ // RET_CHUNK
D_FF = 4096
FF_BLK = D_FF // N_DEV
IN_PAD = 2816
KV_LEN = CTX_LEN + SEQ
ROPE_BASE = 10000.0
ADAM_LR, ADAM_B1, ADAM_B2, ADAM_EPS, ADAM_WD, ADAM_STEP = 0.001, 0.9, 0.999, 1e-08, 0.01, 10
TOK = 256
TOK_B = 256
VMEM_LIMIT = 56 * 1024 * 1024
ARB = "arbitrary"
MESH = pl.DeviceIdType.MESH
_HEAD_SL = [slice(LANE * h, LANE * (h + 1)) for h in range(N_HEADS)]
W_SHAPES = [(2048, D_MODEL), (768, D_MODEL), (1024, 384), (1024, 256)]


def _dot(a, b, ca, cb):
    return lax.dot_general(a.astype(BF), b.astype(BF), (((ca,), (cb,)), ((), ())), preferred_element_type=F32)


@jax.custom_vjp
def mm(a, b):
    return _dot(a, b, 1, 0)


@jax.custom_vjp
def mm_nt(a, b):
    return _dot(a, b, 1, 1)


@jax.custom_vjp
def mm_tn(a, b):
    return _dot(a, b, 0, 0)


mm.defvjp(lambda a, b: (_dot(a, b, 1, 0), (a, b)), lambda r, g: (mm_nt(g, r[1]), mm_tn(r[0], g)))
mm_nt.defvjp(lambda a, b: (_dot(a, b, 1, 1), (a, b)), lambda r, g: (mm(g, r[1]), mm_tn(g, r[0])))
mm_tn.defvjp(lambda a, b: (_dot(a, b, 0, 0), (a, b)), lambda r, g: (mm_nt(r[1], g), mm(r[0], g)))


@jax.custom_vjp
def _mmw(a, w, probe):
    return _dot(a, w, 1, 0)


def _mmw_bwd(r, g):
    a, w = r
    return mm_nt(g, w), jnp.zeros_like(w), mm_tn(a, g)


_mmw.defvjp(lambda a, w, probe: (_dot(a, w, 1, 0), (a, w)), _mmw_bwd)


@jax.custom_vjp
def _mmwt(a, wt, probe):
    return _dot(a, wt, 1, 1)


_mmwt.defvjp(lambda a, wt, probe: (_dot(a, wt, 1, 1), (a, wt)),
             lambda r, g: (mm(g, r[1]), jnp.zeros_like(r[1]), mm_tn(g, r[0])))


def mmwt(a, wt, probe):
    return _dot(a, wt, 1, 1) if probe is None else _mmwt(a, wt, probe)


def mmw(a, w, probe):
    return _dot(a, w, 1, 0) if probe is None else _mmw(a, w, probe)


def rmsn(x, g):
    return x * lax.rsqrt(jnp.mean(x * x, axis=-1, keepdims=True) + EPS) * g


def silu(x):
    return x * jax.nn.sigmoid(x)


def _swap32_impl(x):
    n = x.shape[-1]
    lane = lax.broadcasted_iota(jnp.int32, x.shape, x.ndim - 1) % LANE
    up = pltpu.roll(x, n - 32, x.ndim - 1)
    dn = pltpu.roll(x, 32, x.ndim - 1)
    return jnp.where(lane < 32, up, jnp.where(lane < 64, dn, 0.0))


@jax.custom_vjp
def swap32(x):
    return _swap32_impl(x)


swap32.defvjp(lambda x: (_swap32_impl(x), None), lambda _, g: (_swap32_impl(g),))


def rope(x, cs, sn):
    return x * cs + swap32(x) * sn


def k1_tile(x, sh, sc, g_attn, g_q, g_kv, ws, ps, tabs, is_ctx):
    w_a, w_b, w_uq, w_ukv = ws
    p_a, p_b, p_uq, p_ukv = ps
    cs1, sn1 = tabs
    cs, sn = jnp.concatenate([cs1] * N_HEADS, axis=-1), jnp.concatenate([sn1] * N_HEADS, axis=-1)
    cq_t = jnp.concatenate([jnp.ones_like(cs1), cs1] * N_HEADS, axis=-1)
    sq_t = jnp.concatenate([jnp.zeros_like(sn1), sn1] * N_HEADS, axis=-1)
    h = rmsn(x, g_attn) * (1.0 + sc) + sh
    pa = mmwt(h, w_a, p_a)
    pb = mmwt(h, w_b, p_b)
    rk = pa[:, 512:1024] * 0.125
    rv = pa[:, 1024:1536]
    kpe = pb[:, 640:768]
    kv = mmwt(rmsn(pb[:, 384:640], g_kv), w_ukv, p_ukv)
    if not is_ctx:
        rk = rope(rk, cs, sn)
        kpe = rope(kpe, cs1, sn1)
    k_full = jnp.concatenate([piece for sl in _HEAD_SL for piece in (kv[:, sl], kpe)], axis=-1)
    v = kv[:, 512:]
    if is_ctx:
        return rk, rv, k_full, v
    rq = rope(pa[:, 0:512], cs, sn)
    rg = pa[:, 1536:2048]
    q = rope(mmwt(rmsn(pb[:, 0:384], g_q), w_uq, p_uq), cq_t, sq_t)
    return rq, rk, rv, rg, q, k_full, v


def log_sigmoid(x):
    return jnp.minimum(x, 0.0) - jnp.log(1.0 + jnp.exp(-jnp.abs(x)))


def ret_chunk(q, k, v, s, lg, reverse):
    c = RET_CHUNK
    ii = lax.broadcasted_iota(jnp.int32, (c, c), 0).astype(F32)
    jj = lax.broadcasted_iota(jnp.int32, (c, c), 1).astype(F32)
    diff = (jj - ii) if reverse else (ii - jj)
    dec = jnp.where(diff >= 0, jnp.exp(lg * jnp.maximum(diff, 0.0)), 0.0)
    pos = lax.broadcasted_iota(jnp.int32, (c, 1), 0).astype(F32)
    if reverse:
        wk, wq = jnp.exp(lg * pos), jnp.exp(lg * (c - pos))
    else:
        wk, wq = jnp.exp(lg * (c - 1.0 - pos)), jnp.exp(lg * (pos + 1.0))
    o = mm(mm_nt(q, k) * dec, v) + mm(q * wq, s)
    s_next = jnp.exp(lg * float(c)) * s + mm_tn(k * wk, v)
    return o, s_next


def ctx_state(kc, vc, lg, reverse):
    n = kc.shape[0]
    pos = lax.broadcasted_iota(jnp.int32, (n, 1), 0).astype(F32)
    w = jnp.exp(lg * pos) if reverse else jnp.exp(lg * (n - 1.0 - pos))
    return mm_tn(kc * w, vc)


def attn_head(qn, qp, kn, kp, v):
    s = (mm_nt(qn, kn) + mm_nt(qp, kp)) * (1.0 / math.sqrt(192.0))
    e = jnp.exp(s - jnp.max(s, axis=-1, keepdims=True))
    return mm(e / jnp.sum(e, axis=-1, keepdims=True), v)


def gn_gate(o, rg, g_ret):
    ys = []
    for h in range(N_HEADS):
        sl = slice(LANE * h, LANE * (h + 1))
        oh = o[:, sl]
        mu = jnp.mean(oh, axis=-1, keepdims=True)
        var = jnp.mean(jnp.square(oh - mu), axis=-1, keepdims=True)
        ys.append((oh - mu) * lax.rsqrt(var + EPS) * g_ret[:, sl])
    return jnp.concatenate(ys, axis=-1) * silu(rg)


def k4a_tile(x, o_f, o_b, rg, y_mla, g_ret, gt_a, g_ffn, sh_f, sc_f, w_out, p_out):
    mix = jnp.concatenate([gn_gate(o_f + o_b, rg, g_ret), y_mla], axis=-1)
    x_mid = x + gt_a * mmw(mix, w_out, p_out)
    h2 = rmsn(x_mid, g_ffn) * (1.0 + sc_f) + sh_f
    return x_mid, h2


def k4c_tile(x_mid, mlp, gt_f, g_final, tgt):
    y = rmsn(x_mid + gt_f * mlp, g_final)
    per_tok = jnp.mean(jnp.square(y - tgt), axis=-1, keepdims=True)
    return 0.5 * jnp.sum(per_tok, axis=0, keepdims=True)


def _cp(sem=None, vmem=VMEM_LIMIT):
    return pltpu.CompilerParams(dimension_semantics=sem, vmem_limit_bytes=vmem)


def _acc(ref, val, first):
    @pl.when(first)
    def _():
        ref[...] = val

    @pl.when(jnp.logical_not(first))
    def _():
        ref[...] += val


def _full(shape):
    nd = len(shape)
    return pl.BlockSpec(shape, lambda *_: (0,) * nd)


ANY = pl.BlockSpec(memory_space=pl.ANY)


def _sds(shape, dtype=F32):
    return jax.ShapeDtypeStruct(shape, dtype)


def _exchange(arrs, gather, name):
    n = len(arrs)
    out_shape = [_sds(((N_DEV,) + a.shape) if gather else a.shape, a.dtype) for a in arrs]

    def body(*refs):
        ins, outs = refs[:n], refs[n:2 * n]
        send_sems, recv_sems, local_sems = refs[2 * n:]
        x, y, c = lax.axis_index("x"), lax.axis_index("y"), lax.axis_index("c")
        me = 4 * x + 2 * y + c
        sends, recvs, locs = [], [], []
        for i in range(n):
            for k in range(N_DEV - 1):
                bits = k + 1
                px = x ^ ((bits >> 2) & 1)
                py = y ^ ((bits >> 1) & 1)
                pc = c ^ (bits & 1)
                peer = 4 * px + 2 * py + pc
                src = ins[i] if gather else ins[i].at[peer]
                sem = i * (N_DEV - 1) + k
                sends.append(pltpu.make_async_remote_copy(
                    src_ref=src, dst_ref=outs[i].at[me], send_sem=send_sems.at[sem], recv_sem=recv_sems.at[sem],
                    device_id=(px, py, pc), device_id_type=MESH))
                recvs.append(pltpu.make_async_remote_copy(
                    src_ref=src, dst_ref=outs[i].at[peer], send_sem=send_sems.at[sem], recv_sem=recv_sems.at[sem],
                    device_id=(px, py, pc), device_id_type=MESH))
            locs.append(pltpu.make_async_copy(ins[i] if gather else ins[i].at[me], outs[i].at[me], local_sems.at[i]))
        for cp in locs + sends:
            cp.start()
        for cp in recvs:
            cp.wait_recv()
        for cp in sends:
            cp.wait_send()
        for cp in locs:
            cp.wait()

    outs = pl.pallas_call(
        body, name=name, out_shape=out_shape, in_specs=[ANY] * n, out_specs=[ANY] * n,
        scratch_shapes=[pltpu.SemaphoreType.DMA((n * (N_DEV - 1),)), pltpu.SemaphoreType.DMA((n * (N_DEV - 1),)),
                        pltpu.SemaphoreType.DMA((n,))],
    )(*arrs)
    return list(outs)


def _gather_two_level(arrs, name):
    n = len(arrs)

    def body(*refs):
        ins, outs = refs[:n], refs[n:2 * n]
        send_sems, recv_sems, local_sems = refs[2 * n:]
        x, y, c = lax.axis_index("x"), lax.axis_index("y"), lax.axis_index("c")
        sibling = (x, y, 1 - c)
        chips = [(1 - x, y), (x, 1 - y), (1 - x, 1 - y)]

        def slot(px, py, pc):
            return 4 * px + 2 * py + pc

        first, passed, waits, locs = [], [], [], []
        for i in range(n):
            def copy(k, block, to, src=None, i=i):
                dst = outs[i].at[slot(*block)]
                return pltpu.make_async_remote_copy(
                    src_ref=dst if src is None else src, dst_ref=dst, send_sem=send_sems.at[7 * i + k],
                    recv_sem=recv_sems.at[7 * i + k], device_id=to, device_id_type=MESH)

            locs.append(pltpu.make_async_copy(ins[i], outs[i].at[slot(x, y, c)], local_sems.at[i]))
            first.append(copy(0, (x, y, c), sibling, src=ins[i]))
            first += [copy(1 + j, (x, y, c), (*chip, c), src=ins[i]) for j, chip in enumerate(chips)]
            passed.append([copy(4 + j, (*chip, c), sibling) for j, chip in enumerate(chips)])
            waits.append([copy(1 + j, (*chip, c), (x, y, c)) for j, chip in enumerate(chips)])
        for cp in locs + first:
            cp.start()
        for j in range(3):
            for i in range(n):
                waits[i][j].wait_recv()
                passed[i][j].start()
        for i in range(n):
            def arrival(k, block, i=i):
                dst = outs[i].at[slot(*block)]
                return pltpu.make_async_remote_copy(
                    src_ref=dst, dst_ref=dst, send_sem=send_sems.at[7 * i + k], recv_sem=recv_sems.at[7 * i + k],
                    device_id=sibling, device_id_type=MESH)

            arrival(0, (x, y, 1 - c)).wait_recv()
            for j, chip in enumerate(chips):
                arrival(4 + j, (*chip, 1 - c)).wait_recv()
        for cp in first + [p for ps in passed for p in ps]:
            cp.wait_send()
        for cp in locs:
            cp.wait()

    outs = pl.pallas_call(
        body, name=name, out_shape=[_sds((N_DEV,) + a.shape, a.dtype) for a in arrs], in_specs=[ANY] * n,
        out_specs=[ANY] * n,
        scratch_shapes=[pltpu.SemaphoreType.DMA((7 * n,)), pltpu.SemaphoreType.DMA((7 * n,)),
                        pltpu.SemaphoreType.DMA((n,))],
    )(*arrs)
    return list(outs)


HBM = pl.BlockSpec(memory_space=pltpu.HBM)
SEM = pl.BlockSpec(memory_space=pltpu.SEMAPHORE)
EFFECT = pltpu.SideEffectType.DATAFLOW_SIDE_EFFECTING


def _peer(k):
    x, y, c = lax.axis_index("x"), lax.axis_index("y"), lax.axis_index("c")
    bits = k + 1
    px, py, pc = x ^ ((bits >> 2) & 1), y ^ ((bits >> 1) & 1), c ^ (bits & 1)
    return (px, py, pc), 4 * px + 2 * py + pc, 4 * x + 2 * y + c


def _exchange_start(arrs, gather, name):
    n = len(arrs)
    lands = [pltpu.with_memory_space_constraint(lax.empty(((N_DEV,) + a.shape) if gather else a.shape, a.dtype),
                                                pltpu.HBM) for a in arrs]
    srcs = [pltpu.with_memory_space_constraint(a, pltpu.HBM) for a in arrs]

    def body(*refs):
        ins, zones = refs[:n], refs[n:2 * n]
        send_sems, recv_sems, local_sems = refs[2 * n:2 * n + 3]
        token = refs[-1]
        for i in range(n):
            for k in range(N_DEV - 1):
                dev, peer, me = _peer(k)
                sem = i * (N_DEV - 1) + k
                pltpu.make_async_remote_copy(
                    src_ref=ins[i] if gather else ins[i].at[peer], dst_ref=zones[i].at[me],
                    send_sem=send_sems.at[sem], recv_sem=recv_sems.at[sem], device_id=dev, device_id_type=MESH).start()
            _, _, me = _peer(0)
            pltpu.make_async_copy(ins[i] if gather else ins[i].at[me], zones[i].at[me], local_sems.at[i]).start()
        token[...] = jnp.zeros_like(token)

    nsem = n * (N_DEV - 1)
    outs = pl.pallas_call(
        body, name=name,
        out_shape=[pltpu.SemaphoreType.DMA((nsem,)), pltpu.SemaphoreType.DMA((nsem,)), pltpu.SemaphoreType.DMA((n,))]
        + [pltpu.HBM(a.shape, a.dtype) for a in srcs] + [pltpu.HBM(z.shape, z.dtype) for z in lands]
        + [_sds((8, LANE))],
        in_specs=[HBM] * (2 * n),
        out_specs=[SEM, SEM, SEM] + [HBM] * (2 * n) + [pl.BlockSpec(memory_space=pltpu.VMEM)],
        input_output_aliases={i: 3 + i for i in range(2 * n)},
        compiler_params=pltpu.CompilerParams(has_side_effects=EFFECT),
    )(*srcs, *lands)
    return {"n": n, "gather": gather, "sems": outs[:3], "srcs": outs[3:3 + n], "lands": outs[3 + n:3 + 2 * n],
            "token": outs[-1]}


def _exchange_wait(st, after, name):
    n, gather = st["n"], st["gather"]

    def body(*refs):
        ins, zones = refs[:n], refs[n:2 * n]
        send_sems, recv_sems, local_sems = refs[2 * n:2 * n + 3]
        for i in range(n):
            for k in range(N_DEV - 1):
                dev, peer, me = _peer(k)
                sem = i * (N_DEV - 1) + k
                src = ins[i] if gather else ins[i].at[peer]
                cp = pltpu.make_async_remote_copy(
                    src_ref=src, dst_ref=zones[i].at[peer], send_sem=send_sems.at[sem], recv_sem=recv_sems.at[sem],
                    device_id=dev, device_id_type=MESH)
                cp.wait_send()
                cp.wait_recv()
            _, _, me = _peer(0)
            pltpu.make_async_copy(ins[i] if gather else ins[i].at[me], zones[i].at[me], local_sems.at[i]).wait()

    outs = pl.pallas_call(
        body, name=name,
        out_shape=[pltpu.HBM(a.shape, a.dtype) for a in st["srcs"]] + [pltpu.HBM(z.shape, z.dtype) for z in st["lands"]],
        in_specs=[HBM] * (2 * n) + [SEM, SEM, SEM, ANY], out_specs=[HBM] * (2 * n),
        input_output_aliases={i: i for i in range(2 * n)},
        compiler_params=pltpu.CompilerParams(has_side_effects=EFFECT),
    )(*st["srcs"], *st["lands"], *st["sems"], after)
    return list(outs[n:])


def _mod_fwd(crows, w_ada, b_blk):
    def body(c_ref, w_ref, b_ref, o_ref):
        o_ref[...] = mm(silu(c_ref[...]), w_ref[...]) + b_ref[...]

    return pl.pallas_call(body, name="mod_fwd", out_shape=_sds((24, 768)), compiler_params=_cp())(crows, w_ada, b_blk)


def _mod_bwd(crows, w_ada, dmod_blk, dmodc_blk, dmod_full, dmodc_full):
    def body(c_ref, w_ref, d_ref, dc_ref, df_ref, dcf_ref, gw_ref, gc_ref, gb_ref):
        cr = c_ref[...]
        dc, dcf = dc_ref[0:1, :], dcf_ref[0:1, :]
        for p in range(1, N_DEV):
            dc = dc + dc_ref[p:p + 1, :]
            dcf = dcf + dcf_ref[p:p + 1, :]
        row = lax.broadcasted_iota(jnp.int32, (24, 1), 0)
        gw_ref[...] = mm_tn(silu(cr), jnp.where(row == 16, dc, d_ref[...]))
        cc = cr[16:17, :]
        sg = jax.nn.sigmoid(cc)
        part = mm_nt(jnp.broadcast_to(dc, (8, 768)), w_ref[...])
        gc_ref[...] = part * (sg * (1.0 + cc * (1.0 - sg)))
        gb_ref[...] = jnp.sum(df_ref[...], axis=0, keepdims=True) + dcf

    return pl.pallas_call(
        body, name="mod_bwd", out_shape=[_sds((D_MODEL, 768)), _sds((8, D_MODEL)), _sds((1, 6 * D_MODEL))],
        compiler_params=_cp())(crows, w_ada, dmod_blk, dmodc_blk, dmod_full, dmodc_full)


def _tab_specs(tk):
    return [pl.BlockSpec((tk, LANE), lambda i, t: (t, 0))] * 2


def _k1_fwd(x, mod, g_attn, g_q, g_kv, ws, tabs, kv_all, is_ctx):
    b, l, _ = x.shape
    nt = l // TOK
    n_f32 = 2 if is_ctx else 4

    def body(x_ref, mod_ref, ga_ref, gq_ref, gk_ref, wa_ref, wb_ref, wq_ref, wk_ref, cs_ref, sn_ref, *rest):
        outs = rest if is_ctx else rest[2:]
        res = k1_tile(x_ref[...], mod_ref[0:1, :], mod_ref[1:2, :], ga_ref[...], gq_ref[...], gk_ref[...],
                      (wa_ref[...], wb_ref[...], wq_ref[...], wk_ref[...]), (None,) * 4,
                      (cs_ref[...], sn_ref[...]), is_ctx)
        for o_ref, r in zip(outs, res):
            o_ref[...] = r.astype(o_ref.dtype)

    tok = lambda w, off=0: pl.BlockSpec((None, TOK, w), lambda i, t: (i, t + off, 0))
    mod_spec = pl.BlockSpec((None, 8, D_MODEL), (lambda i, t: (0, 0, 0)) if is_ctx else (lambda i, t: (i, 0, 0)))
    kv_off = 0 if is_ctx else CTX_LEN // TOK
    in_specs = ([tok(D_MODEL), mod_spec, _full((1, D_MODEL)), _full((1, 384)), _full((1, 256))]
                + [_full(s) for s in W_SHAPES] + _tab_specs(TOK))
    args = [x, mod, g_attn, g_q, g_kv, *ws, *tabs]
    out_specs = [tok(512)] * n_f32 + ([] if is_ctx else [tok(1024)]) + [tok(1024, kv_off), tok(512, kv_off)]
    out_shape = ([_sds((b, l, 512))] * n_f32 + ([] if is_ctx else [_sds((b, l, 1024), BF)])
                 + [_sds((b, KV_LEN, 1024), BF), _sds((b, KV_LEN, 512), BF)])
    aliases = {}
    if not is_ctx:
        aliases = {len(args): n_f32 + 1, len(args) + 1: n_f32 + 2}
        in_specs += [ANY, ANY]
        args += list(kv_all)
    return pl.pallas_call(
        body, name="k1_fwd_ctx" if is_ctx else "k1_fwd", grid=(b, nt), in_specs=in_specs, out_specs=out_specs,
        out_shape=out_shape, input_output_aliases=aliases, compiler_params=_cp((ARB, ARB)),
    )(*args)


N_ACC = 7


def _k1_bwd(x, mod, g_attn, g_q, g_kv, ws, tabs, cts, dx_res, init, is_ctx):
    b, l, _ = x.shape
    tk = TOK_B
    nt = l // tk
    flat_cts = [a for group in cts for a in group]
    group_sizes = [len(g) for g in cts]
    n_ct = len(flat_cts)
    has_res = dx_res is not None
    has_init = init is not None
    acc_shapes = W_SHAPES + [(1, D_MODEL), (1, 384), (1, 256)]

    def body(*refs):
        it = iter(refs)
        x_ref, mod_ref, ga_ref, gq_ref, gk_ref = [next(it) for _ in range(5)]
        w_hbm = [next(it) for _ in range(4)]
        tab_refs = [next(it) for _ in range(2)]
        ct_refs = [next(it) for _ in range(n_ct)]
        res_ref = next(it) if has_res else None
        init_refs = [next(it) for _ in range(N_ACC)] if has_init else None
        gx_ref = next(it) if not is_ctx else None
        out_hbm = [next(it) for _ in range(N_ACC)]
        dmod_ref = next(it)
        w_vmem = [next(it) for _ in range(4)]
        accs = [next(it) for _ in range(N_ACC)]
        sem = next(it)
        i, t = pl.program_id(0), pl.program_id(1)
        first = jnp.logical_and(i == 0, t == 0)
        last = jnp.logical_and(i == b - 1, t == nt - 1)

        @pl.when(first)
        def _():
            for src, dst in zip(w_hbm, w_vmem):
                pltpu.sync_copy(src, dst)
            for k in range(N_ACC):
                if has_init:
                    pltpu.sync_copy(init_refs[k], accs[k])
                else:
                    accs[k][...] = jnp.zeros(acc_shapes[k], F32)

        ct_vals, pos = [], 0
        for gsz in group_sizes:
            v = ct_refs[pos][...].astype(F32)
            for r in ct_refs[pos + 1:pos + gsz]:
                v = v + r[...]
            ct_vals.append(v)
            pos += gsz
        wv = tuple(r[...] for r in w_vmem)
        tv = tuple(r[...] for r in tab_refs)

        def f(xv, sh, sc, ga, gq, gk, *probes):
            return k1_tile(xv, sh, sc, ga, gq, gk, wv, probes, tv, is_ctx)

        probes = [jnp.zeros(s, F32) for s in W_SHAPES]
        _, vjp = jax.vjp(f, x_ref[...], mod_ref[0:1, :], mod_ref[1:2, :], ga_ref[...], gq_ref[...], gk_ref[...], *probes)
        dx, dsh, dsc, dga, dgq, dgk, dwa, dwb, dwq, dwk = vjp(tuple(ct_vals))
        if not is_ctx:
            gx_ref[...] = dx + res_ref[...] if has_res else dx
        for ref, val in zip(accs, (dwa, dwb, dwq, dwk, dga, dgq, dgk)):
            ref[...] += val
        t0 = first if is_ctx else t == 0
        _acc(dmod_ref.at[0:1, :], dsh, t0)
        _acc(dmod_ref.at[1:2, :], dsc, t0)

        @pl.when(t0)
        def _():
            dmod_ref[2:8, :] = jnp.zeros((6, D_MODEL), F32)

        @pl.when(last)
        def _():
            cps = [pltpu.make_async_copy(accs[k], out_hbm[k], sem.at[k]) for k in range(N_ACC)]
            for cp in cps:
                cp.start()
            for cp in cps:
                cp.wait()

    tok = lambda w, off=0: pl.BlockSpec((None, tk, w), lambda i, t: (i, t + off, 0))
    mod_spec = pl.BlockSpec((None, 8, D_MODEL), (lambda i, t: (0, 0, 0)) if is_ctx else (lambda i, t: (i, 0, 0)))
    in_specs = ([tok(D_MODEL), mod_spec, _full((1, D_MODEL)), _full((1, 384)), _full((1, 256))] + [ANY] * 4
                + _tab_specs(tk))
    args = [x, mod, g_attn, g_q, g_kv, *ws, *tabs]
    for a, off in flat_cts:
        in_specs.append(tok(a.shape[-1], off // tk))
        args.append(a)
    if has_res:
        in_specs.append(tok(D_MODEL))
        args.append(dx_res)
    if has_init:
        in_specs += [ANY] * N_ACC
        args += list(init)
    out_shape, out_specs = [], []
    if not is_ctx:
        out_shape.append(_sds((b, l, D_MODEL)))
        out_specs.append(tok(D_MODEL))
    out_shape += [_sds(s) for s in acc_shapes] + [_sds((1 if is_ctx else b, 8, D_MODEL))]
    out_specs += [ANY] * N_ACC + [mod_spec]
    outs = pl.pallas_call(
        body, name="k1_bwd_ctx" if is_ctx else "k1_bwd", grid=(b, nt), in_specs=in_specs, out_specs=out_specs,
        out_shape=out_shape,
        scratch_shapes=[pltpu.VMEM(s, BF) for s in W_SHAPES] + [pltpu.VMEM(s, F32) for s in acc_shapes]
        + [pltpu.SemaphoreType.DMA((N_ACC,))],
        compiler_params=_cp((ARB, ARB)),
    )(*args)
    outs = list(outs)
    gx = None if is_ctx else outs.pop(0)
    return gx, outs[:N_ACC], outs[N_ACC]


def _chunk_spec(rev):
    if rev:
        return pl.BlockSpec((None, RET_CHUNK, 512), lambda i, n: (i, N_CHUNK - 1 - n, 0))
    return pl.BlockSpec((None, RET_CHUNK, 512), lambda i, n: (i, n, 0))


def _state_spec(rev):
    if rev:
        return pl.BlockSpec((None, N_HEADS, None, LANE, LANE), lambda i, n: (i, 0, N_CHUNK - 1 - n, 0, 0))
    return pl.BlockSpec((None, N_HEADS, None, LANE, LANE), lambda i, n: (i, 0, n, 0, 0))


_CTX_SPEC = pl.BlockSpec((None, CTX_LEN, 512), lambda i, n: (i, 0, 0))
_DEC_SPEC = pl.BlockSpec((N_HEADS, 1, 1), lambda i, n: (0, 0, 0))


def _k2_fwd(rq, rk, rv, rkc, rvc, dec_f, dec_b):
    b = rq.shape[0]

    def body(qf, kf, vf, qb, kb, vb, kc, vc, df, db, of_ref, ob_ref, sf_out, sb_out, sf, sb):
        n = pl.program_id(1)
        for h, sl in enumerate(_HEAD_SL):
            lgf, lgb = log_sigmoid(df[h]), log_sigmoid(db[h])

            @pl.when(n == 0)
            def _():
                sf[h] = ctx_state(kc[:, sl], vc[:, sl], lgf, False)
                sb[h] = ctx_state(kc[:, sl], vc[:, sl], lgb, True)

            sf_out[h] = sf[h]
            sb_out[h] = sb[h]
            o, s = ret_chunk(qf[:, sl], kf[:, sl], vf[:, sl], sf[h], lgf, False)
            of_ref[:, sl] = o
            sf[h] = s
            o, s = ret_chunk(qb[:, sl], kb[:, sl], vb[:, sl], sb[h], lgb, True)
            ob_ref[:, sl] = o
            sb[h] = s

    l = rq.shape[1]
    return pl.pallas_call(
        body, name="k2_fwd", grid=(b, N_CHUNK),
        in_specs=[_chunk_spec(False)] * 3 + [_chunk_spec(True)] * 3 + [_CTX_SPEC, _CTX_SPEC, _DEC_SPEC, _DEC_SPEC],
        out_specs=[_chunk_spec(False), _chunk_spec(True), _state_spec(False), _state_spec(True)],
        out_shape=[_sds((b, l, 512)), _sds((b, l, 512)), _sds((b, N_HEADS, N_CHUNK, LANE, LANE)),
                   _sds((b, N_HEADS, N_CHUNK, LANE, LANE))],
        scratch_shapes=[pltpu.VMEM((N_HEADS, LANE, LANE), F32), pltpu.VMEM((N_HEADS, LANE, LANE), F32)],
        compiler_params=_cp((ARB, ARB)),
    )(rq, rk, rv, rq, rk, rv, rkc, rvc, dec_f, dec_b)


def _k2_bwd(rq, rk, rv, do, sf_prev, sb_prev, rkc, rvc, dec_f, dec_b):
    b, l, _ = rq.shape

    def body(qf, kf, vf, gf, spf, qb, kb, vb, gb, spb, kc, vc, df, db,
             dqf, dkf, dvf, dqb, dkb, dvb, dkc, dvc, ddf, ddb, dsf, dsb):
        n = pl.program_id(1)

        @pl.when(n == 0)
        def _():
            dsf[...] = jnp.zeros((N_HEADS, LANE, LANE), F32)
            dsb[...] = jnp.zeros((N_HEADS, LANE, LANE), F32)

        def one(h, sl, q, k, v, g, sp, dec, ds, dq, dk, dv, dd, rev):
            def f(qv, kv_, vv, sv, dcy):
                return ret_chunk(qv, kv_, vv, sv, log_sigmoid(dcy), rev)

            _, vjp = jax.vjp(f, q[:, sl], k[:, sl], v[:, sl], sp[h], dec[h])
            gq, gk, gv, gs, gd = vjp((g[:, sl], ds[h]))
            dq[:, sl] = gq
            dk[:, sl] = gk
            dv[:, sl] = gv
            ds[h] = gs
            _acc(dd.at[h], jnp.broadcast_to(gd, (8, LANE)), n == 0)

        for h, sl in enumerate(_HEAD_SL):
            one(h, sl, qf, kf, vf, gf, spf, df, dsf, dqf, dkf, dvf, ddf, False)
            one(h, sl, qb, kb, vb, gb, spb, db, dsb, dqb, dkb, dvb, ddb, True)

        @pl.when(n == N_CHUNK - 1)
        def _():
            def f(kcv, vcv, dcy, rev):
                return ctx_state(kcv, vcv, log_sigmoid(dcy), rev)

            for h, sl in enumerate(_HEAD_SL):
                _, vjp_f = jax.vjp(functools.partial(f, rev=False), kc[:, sl], vc[:, sl], df[h])
                gk_f, gv_f, gd_f = vjp_f(dsf[h])
                _, vjp_b = jax.vjp(functools.partial(f, rev=True), kc[:, sl], vc[:, sl], db[h])
                gk_b, gv_b, gd_b = vjp_b(dsb[h])
                dkc[:, sl] = gk_f + gk_b
                dvc[:, sl] = gv_f + gv_b
                ddf[h] += jnp.broadcast_to(gd_f, (8, LANE))
                ddb[h] += jnp.broadcast_to(gd_b, (8, LANE))

    dd_spec = pl.BlockSpec((None, N_HEADS, 8, LANE), lambda i, n: (i, 0, 0, 0))
    return pl.pallas_call(
        body, name="k2_bwd", grid=(b, N_CHUNK),
        in_specs=[_chunk_spec(True)] * 4 + [_state_spec(True)] + [_chunk_spec(False)] * 4 + [_state_spec(False)]
        + [_CTX_SPEC, _CTX_SPEC, _DEC_SPEC, _DEC_SPEC],
        out_specs=[_chunk_spec(True)] * 3 + [_chunk_spec(False)] * 3 + [_CTX_SPEC, _CTX_SPEC, dd_spec, dd_spec],
        out_shape=[_sds((b, l, 512))] * 6 + [_sds((b, CTX_LEN, 512))] * 2 + [_sds((b, N_HEADS, 8, LANE))] * 2,
        scratch_shapes=[pltpu.VMEM((N_HEADS, LANE, LANE), F32), pltpu.VMEM((N_HEADS, LANE, LANE), F32)],
        compiler_params=_cp((ARB, ARB)),
    )(rq, rk, rv, do, sf_prev, rq, rk, rv, do, sb_prev, rkc, rvc, dec_f, dec_b)


TQ = 512
QK_W = 2 * LANE
_Q_PARTS = [slice(0, TQ // 2), slice(TQ // 2, TQ)]


SM_SCALE = 1.0 / math.sqrt(192.0)


def _k3_specs():
    qs = lambda w: pl.BlockSpec((None, TQ, w), lambda i, h, t: (i, t, h))
    ks = lambda w: pl.BlockSpec((None, KV_LEN, w), lambda i, h, t: (i, 0, h))
    return qs, ks


def _k3_fwd(q, k, v):
    b, l, _ = q.shape

    def body(q_ref, k_ref, v_ref, o_ref, lse_ref):
        kv_, vv = k_ref[...], v_ref[...]
        for r in _Q_PARTS:
            s = _dot(q_ref[r, :], kv_, 1, 1) * SM_SCALE
            m = jnp.max(s, axis=-1, keepdims=True)
            e = jnp.exp(s - m)
            tot = jnp.sum(e, axis=-1, keepdims=True)
            o_ref[r, :] = _dot(e, vv, 1, 0) * (1.0 / tot)
            lse_ref[r, :] = jnp.broadcast_to(m + jnp.log(tot), (TQ // 2, LANE))

    qs, ks = _k3_specs()
    return pl.pallas_call(
        body, name="k3_fwd", grid=(b, N_HEADS, l // TQ), in_specs=[qs(QK_W), ks(QK_W), ks(LANE)],
        out_specs=[qs(LANE), qs(LANE)], out_shape=[_sds((b, l, N_HEADS * LANE))] * 2,
        compiler_params=_cp((ARB, ARB, ARB)),
    )(q, k, v)


def _k3_bwd(q, k, v, o, lse, dy):
    b, l, _ = q.shape

    def body(q_ref, k_ref, v_ref, o_ref, lse_ref, dy_ref, dq_ref, dk_ref, dv_ref):
        t0 = pl.program_id(2) == 0
        kv_, vv = k_ref[...], v_ref[...]
        dvs, dks = [], []
        for r in _Q_PARTS:
            qv, dyv = q_ref[r, :], dy_ref[r, :]
            g = dyv.astype(BF)
            lse_col = jnp.max(lse_ref[r, :], axis=-1, keepdims=True)
            delta = jnp.sum(dyv * o_ref[r, :], axis=-1, keepdims=True)
            p = jnp.exp(_dot(qv, kv_, 1, 1) * SM_SCALE - lse_col)
            ds = (p * (_dot(g, vv, 1, 1) - delta) * SM_SCALE).astype(BF)
            dvs.append(_dot(p, g, 0, 0))
            dq_ref[r, :] = _dot(ds, kv_, 1, 0)
            dks.append(_dot(ds, qv, 0, 0))
        _acc(dv_ref, sum(dvs[1:], dvs[0]), t0)
        _acc(dk_ref, sum(dks[1:], dks[0]), t0)

    qs, ks = _k3_specs()
    return pl.pallas_call(
        body, name="k3_bwd", grid=(b, N_HEADS, l // TQ),
        in_specs=[qs(QK_W), ks(QK_W), ks(LANE), qs(LANE), qs(LANE), qs(LANE)],
        out_specs=[qs(QK_W), ks(QK_W), ks(LANE)],
        out_shape=[_sds((b, l, N_HEADS * QK_W)), _sds((b, KV_LEN, N_HEADS * QK_W)), _sds((b, KV_LEN, N_HEADS * LANE))],
        compiler_params=_cp((ARB, ARB, ARB)),
    )(q, k, v, o, lse, dy)


def _mod_rows(mod_ref, rows):
    return [mod_ref[r:r + 1, :] for r in rows]


def _k4a_fwd(x, o_f, o_b, rg, y_mla, g_ret, w_out, mod, g_ffn):
    b, l, _ = x.shape

    def body(x_ref, of_ref, ob_ref, rg_ref, ym_ref, gr_ref, wo_ref, mod_ref, gf_ref, xm_ref, h2_ref):
        gt_a, sh_f, sc_f = _mod_rows(mod_ref, (2, 3, 4))
        x_mid, h2 = k4a_tile(x_ref[...], of_ref[...], ob_ref[...], rg_ref[...], ym_ref[...], gr_ref[...], gt_a,
                             gf_ref[...], sh_f, sc_f, wo_ref[...], None)
        xm_ref[...] = x_mid
        h2_ref[...] = h2.astype(BF)

    tok = lambda w: pl.BlockSpec((None, TOK, w), lambda i, t: (i, t, 0))
    mod_spec = pl.BlockSpec((None, 8, D_MODEL), lambda i, t: (i, 0, 0))
    return pl.pallas_call(
        body, name="k4a_fwd", grid=(b, l // TOK),
        in_specs=[tok(D_MODEL), tok(512), tok(512), tok(512), tok(512), _full((1, 512)), _full((D_MODEL, D_MODEL)),
                  mod_spec, _full((1, D_MODEL))],
        out_specs=[tok(D_MODEL), tok(D_MODEL)], out_shape=[_sds((b, l, D_MODEL)), _sds((b, l, D_MODEL), BF)],
        compiler_params=_cp((ARB, ARB)),
    )(x, o_f, o_b, rg, y_mla, g_ret, w_out, mod, g_ffn)


TOK_M = 512
TOK_D = 1024
HALF_FF = D_FF // 2


def _k4b_mlp_loss(h2, w1t, w2, x_mid, mod, g_final, tgt):
    b, l, _ = h2.shape
    nt = l // TOK_M

    def body(h2_ref, w1_hbm, w2_hbm, xm_ref, mod_ref, gfin_ref, tgt_ref, dxm_ref, dmlp_ref, r_ref, loss_ref, dgt_ref,
             dgfin_ref, w1_v, w2_v):
        i, t = pl.program_id(0), pl.program_id(1)
        first = jnp.logical_and(i == 0, t == 0)

        @pl.when(first)
        def _():
            pltpu.sync_copy(w1_hbm, w1_v)
            pltpu.sync_copy(w2_hbm, w2_v)

        h2v = h2_ref[...]
        mlp = None
        for half in range(2):
            rows = slice(half * HALF_FF, (half + 1) * HALF_FF)
            r = jnp.maximum(_dot(h2v, w1_v[rows, :], 1, 1), 0.0)
            r_ref[:, rows] = r.astype(BF)
            part = _dot(jnp.square(r), w2_v[rows, :], 1, 0)
            mlp = part if mlp is None else mlp + part
        (gt_f,) = _mod_rows(mod_ref, (5,))
        loss, vjp = jax.vjp(k4c_tile, xm_ref[...], mlp, gt_f, gfin_ref[...], tgt_ref[...])
        dxm, dmlp, dgt, dgfin, _ = vjp(jnp.ones((1, 1), F32))
        dxm_ref[...] = dxm
        dmlp_ref[...] = dmlp.astype(BF)
        _acc(loss_ref, jnp.broadcast_to(loss, (8, LANE)), first)
        _acc(dgfin_ref, dgfin, first)
        _acc(dgt_ref, dgt, t == 0)

    tok = lambda w: pl.BlockSpec((None, TOK_M, w), lambda i, t: (i, t, 0))
    return pl.pallas_call(
        body, name="k4b_mlp_loss", grid=(b, nt),
        in_specs=[tok(D_MODEL), ANY, ANY, tok(D_MODEL), pl.BlockSpec((None, 8, D_MODEL), lambda i, t: (i, 0, 0)),
                  _full((1, D_MODEL)), tok(D_MODEL)],
        out_specs=[tok(D_MODEL), tok(D_MODEL), tok(D_FF), _full((8, LANE)),
                   pl.BlockSpec((None, 1, D_MODEL), lambda i, t: (i, 0, 0)), _full((1, D_MODEL))],
        out_shape=[_sds((b, l, D_MODEL)), _sds((b, l, D_MODEL), BF), _sds((b, l, D_FF), BF), _sds((8, LANE)),
                   _sds((b, 1, D_MODEL)), _sds((1, D_MODEL))],
        scratch_shapes=[pltpu.VMEM((D_FF, D_MODEL), BF), pltpu.VMEM((D_FF, D_MODEL), BF)],
        compiler_params=_cp((ARB, ARB)),
    )(h2, w1t, w2, x_mid, mod, g_final, tgt)


def _k4d_mlp_bwd(h2, dmlp, r, w2):
    b, l, _ = h2.shape
    nt = l // TOK_D

    def body(h2_ref, dm_ref, r_ref, w2_ref, da_ref, dw1_ref, dw2_ref, acc1, acc2):
        i, t = pl.program_id(1), pl.program_id(2)
        first = jnp.logical_and(i == 0, t == 0)
        rv = r_ref[...].astype(F32)
        dm = dm_ref[...]
        da = (_dot(dm, w2_ref[...], 1, 1) * (2.0 * rv)).astype(BF)
        da_ref[...] = da
        _acc(acc2, _dot(jnp.square(rv), dm, 0, 0), first)
        _acc(acc1, _dot(h2_ref[...], da, 0, 0), first)

        @pl.when(jnp.logical_and(i == b - 1, t == nt - 1))
        def _():
            dw1_ref[...] = acc1[...].astype(BF)
            dw2_ref[...] = acc2[...].astype(BF)

    tok = lambda w: pl.BlockSpec((None, TOK_D, w), lambda j, i, t: (i, t, 0))
    col = pl.BlockSpec((None, TOK_D, FF_BLK), lambda j, i, t: (i, t, j))
    return pl.pallas_call(
        body, name="k4d_mlp_bwd", grid=(N_DEV, b, nt),
        in_specs=[tok(D_MODEL), tok(D_MODEL), col, pl.BlockSpec((None, FF_BLK, D_MODEL), lambda j, i, t: (j, 0, 0))],
        out_specs=[col, pl.BlockSpec((None, D_MODEL, FF_BLK), lambda j, i, t: (j, 0, 0)),
                   pl.BlockSpec((None, FF_BLK, D_MODEL), lambda j, i, t: (j, 0, 0))],
        out_shape=[_sds((b, l, D_FF), BF), _sds((N_DEV, D_MODEL, FF_BLK), BF), _sds((N_DEV, FF_BLK, D_MODEL), BF)],
        scratch_shapes=[pltpu.VMEM((D_MODEL, FF_BLK), F32), pltpu.VMEM((FF_BLK, D_MODEL), F32)],
        compiler_params=_cp((ARB, ARB, ARB)),
    )(h2, dmlp, r, w2)


def _k4f_dh2(da, w1t, after):
    b, l, _ = da.shape

    def body(da_ref, w_ref, after_ref, o_ref):
        o_ref[...] = _dot(da_ref[...], w_ref[...], 1, 0)

    return pl.pallas_call(
        body, name="k4f_dh2", grid=(b, l // TOK_M),
        in_specs=[pl.BlockSpec((None, TOK_M, D_FF), lambda i, t: (i, t, 0)), _full((D_FF, D_MODEL)), ANY],
        out_specs=pl.BlockSpec((None, TOK_M, D_MODEL), lambda i, t: (i, t, 0)), out_shape=_sds((b, l, D_MODEL)),
        compiler_params=_cp((ARB, ARB)),
    )(da, w1t, after)


def _k4e_bwd(x, o_f, o_b, rg, y_mla, g_ret, w_out, mod, g_ffn, dxm, dh2):
    b, l, _ = x.shape

    def body(x_ref, of_ref, ob_ref, rg_ref, ym_ref, gr_ref, wo_ref, mod_ref, gf_ref, dxm_ref, dh2_ref,
             dx_ref, do_ref, drg_ref, dym_ref, dwo_ref, dgr_ref, dgf_ref, dmod_ref):
        i, t = pl.program_id(0), pl.program_id(1)
        first = jnp.logical_and(i == 0, t == 0)
        gt_a, sh_f, sc_f = _mod_rows(mod_ref, (2, 3, 4))
        wo = wo_ref[...]

        def f(xv, ofv, rgv, ymv, grv, gta, gfv, shf, scf, p_out):
            return k4a_tile(xv, ofv, ob_ref[...], rgv, ymv, grv, gta, gfv, shf, scf, wo, p_out)

        _, vjp = jax.vjp(f, x_ref[...], of_ref[...], rg_ref[...], ym_ref[...], gr_ref[...], gt_a, gf_ref[...], sh_f,
                         sc_f, jnp.zeros((D_MODEL, D_MODEL), F32))
        dx, do, drg, dym, dgr, dgta, dgf, dshf, dscf, dwo = vjp((dxm_ref[...], dh2_ref[...]))
        dx_ref[...] = dx
        do_ref[...] = do
        drg_ref[...] = drg
        dym_ref[...] = dym
        _acc(dwo_ref, dwo, first)
        _acc(dgr_ref, dgr, first)
        _acc(dgf_ref, dgf, first)
        t0 = t == 0
        _acc(dmod_ref.at[2:3, :], dgta, t0)
        _acc(dmod_ref.at[3:4, :], dshf, t0)
        _acc(dmod_ref.at[4:5, :], dscf, t0)

        @pl.when(t0)
        def _():
            dmod_ref[0:2, :] = jnp.zeros((2, D_MODEL), F32)
            dmod_ref[5:8, :] = jnp.zeros((3, D_MODEL), F32)

    tok = lambda w: pl.BlockSpec((None, TOK_B, w), lambda i, t: (i, t, 0))
    mod_spec = pl.BlockSpec((None, 8, D_MODEL), lambda i, t: (i, 0, 0))
    return pl.pallas_call(
        body, name="k4e_bwd", grid=(b, l // TOK_B),
        in_specs=[tok(D_MODEL), tok(512), tok(512), tok(512), tok(512), _full((1, 512)), _full((D_MODEL, D_MODEL)),
                  mod_spec, _full((1, D_MODEL)), tok(D_MODEL), tok(D_MODEL)],
        out_specs=[tok(D_MODEL), tok(512), tok(512), tok(512), _full((D_MODEL, D_MODEL)), _full((1, 512)),
                   _full((1, D_MODEL)), mod_spec],
        out_shape=[_sds((b, l, D_MODEL)), _sds((b, l, 512)), _sds((b, l, 512)), _sds((b, l, 512)),
                   _sds((D_MODEL, D_MODEL)), _sds((1, 512)), _sds((1, D_MODEL)), _sds((b, 8, D_MODEL))],
        compiler_params=_cp((ARB, ARB)),
    )(x, o_f, o_b, rg, y_mla, g_ret, w_out, mod, g_ffn, dxm, dh2)


def _adamw(w, m, v, pieces, name, after=None):
    r, c = w.shape
    npc = pieces.shape[0]
    per_row = c * (7 * 4 + npc * pieces.dtype.itemsize) * 2
    rb = r
    for cand in (r, 512, 256, 128, 64, 32, 16, 8):
        if r % cand == 0 and cand * per_row <= 32 * 1024 * 1024:
            rb = cand
            break

    def body(w_ref, m_ref, v_ref, p_ref, *rest):
        g_ref, d_ref, nm_ref, nv_ref = rest[-4:]
        g = p_ref[0].astype(F32)
        for k in range(1, npc):
            g = g + p_ref[k].astype(F32)
        wv = w_ref[...]
        mn = ADAM_B1 * m_ref[...] + (1.0 - ADAM_B1) * g
        vn = ADAM_B2 * v_ref[...] + (1.0 - ADAM_B2) * jnp.square(g)
        m_hat = mn / (1.0 - ADAM_B1 ** ADAM_STEP)
        v_hat = vn / (1.0 - ADAM_B2 ** ADAM_STEP)
        g_ref[...] = g
        d_ref[...] = -ADAM_LR * (m_hat / (jnp.sqrt(v_hat) + ADAM_EPS) + ADAM_WD * wv)
        nm_ref[...] = mn
        nv_ref[...] = vn

    blk = pl.BlockSpec((rb, c), lambda i: (i, 0))
    extra = [] if after is None else [after]
    return pl.pallas_call(
        body, name=name, grid=(r // rb,),
        in_specs=[blk, blk, blk, pl.BlockSpec((npc, rb, c), lambda i: (0, i, 0))] + [ANY] * len(extra),
        out_specs=[blk] * 4, out_shape=[_sds((r, c))] * 4, compiler_params=_cp((ARB,)),
    )(w, m, v, pieces, *extra)


def _pad_head_rows(w, d):
    k = w.shape[1]
    return jnp.pad(w.reshape(N_HEADS, d, k), ((0, 0), (0, LANE - d), (0, 0))).reshape(N_HEADS * LANE, k)


def _cut_head_rows(g, d):
    k = g.shape[1]
    return g.reshape(N_HEADS, LANE, k)[:, :d].reshape(N_HEADS * d, k)


def _w_in_pad(wt):
    w_a = jnp.concatenate([_pad_head_rows(wt[0:256], 64), _pad_head_rows(wt[256:512], 64), wt[512:1536]], axis=0)
    w_b = jnp.concatenate([wt[1536:2176], jnp.pad(wt[2176:2240], ((0, 64), (0, 0)))], axis=0)
    return w_a, w_b


def _w_in_cut(g_a, g_b):
    return jnp.concatenate([_cut_head_rows(g_a[0:512], 64), _cut_head_rows(g_a[512:1024], 64), g_a[1024:2048],
                            g_b[0:704]], axis=0)


def _w_uq_pad(wt):
    return jnp.pad(wt.reshape(N_HEADS, 192, 384), ((0, 0), (0, 64), (0, 0))).reshape(1024, 384)


def _w_uq_cut(g):
    return g.reshape(N_HEADS, 256, 384)[:, :192].reshape(768, 384)


def _w_ukv_perm(wt):
    return jnp.transpose(wt.reshape(N_HEADS, 2, LANE, 256), (1, 0, 2, 3)).reshape(1024, 256)


def _w_ukv_unperm(g):
    return jnp.transpose(g.reshape(2, N_HEADS, LANE, 256), (1, 0, 2, 3)).reshape(1024, 256)


def _unshard_cols(g):
    return jnp.transpose(g, (1, 0, 2)).reshape(g.shape[1], N_DEV * g.shape[2])


def _rope_tables():
    rows = SEQ // GRID_W
    row = jnp.repeat(jnp.arange(rows, dtype=F32), GRID_W)
    col = jnp.tile(jnp.arange(GRID_W, dtype=F32), rows)
    freq = ROPE_BASE ** (-jnp.arange(16, dtype=F32) / 16)
    ang = jnp.concatenate([row[:, None] * freq, col[:, None] * freq], axis=-1)
    cos, sin = jnp.cos(ang), jnp.sin(ang)
    z = jnp.zeros((SEQ, 64), F32)
    return jnp.concatenate([cos, cos, z], axis=1), jnp.concatenate([-sin, sin, z], axis=1)


_PACKED = (("g_attn", 1024), ("g_ffn", 1024), ("ret_decay_fwd", 4), ("ret_decay_bwd", 4), ("g_ret", 512),
           ("g_q_lora", 384), ("g_kv_lora", 256), ("g_final", 1024))
_PACK_OFF = {}
_off = 0
for _name, _n in _PACKED:
    _PACK_OFF[_name] = _off
    _off += -(-_n // LANE) * LANE
PACK_W = _off


def _pack_small(vals):
    parts = []
    for name, n in _PACKED:
        a = vals[name].reshape(-1).astype(F32)
        parts.append(jnp.pad(a, (0, -(-n // LANE) * LANE - n)))
    return jnp.concatenate(parts).reshape(1, PACK_W)


def _adamw_small(params, packed, gcc, gb_ada):
    names = list(params)
    n_p = len(names)

    def body(*refs):
        p_ref, gcc_ref, gb_ref = refs[3 * n_p:3 * n_p + 3]
        outs = refs[3 * n_p + 3:]
        for k, name in enumerate(names):
            w_ref, m_ref, v_ref = refs[3 * k:3 * k + 3]
            n = w_ref.shape[1]
            if name == "b_ada":
                g = gb_ref[...]
            elif name == "c_ctx":
                g = gcc_ref[0, 0:1, :]
                for d in range(1, N_DEV):
                    g = g + gcc_ref[d, 0:1, :]
            else:
                off = _PACK_OFF[name]
                g = p_ref[0, :, off:off + n]
                for d in range(1, N_DEV):
                    g = g + p_ref[d, :, off:off + n]
            mn = ADAM_B1 * m_ref[...] + (1.0 - ADAM_B1) * g
            vn = ADAM_B2 * v_ref[...] + (1.0 - ADAM_B2) * jnp.square(g)
            m_hat = mn / (1.0 - ADAM_B1 ** ADAM_STEP)
            v_hat = vn / (1.0 - ADAM_B2 ** ADAM_STEP)
            outs[4 * k][...] = g
            outs[4 * k + 1][...] = -ADAM_LR * (m_hat / (jnp.sqrt(v_hat) + ADAM_EPS) + ADAM_WD * w_ref[...])
            outs[4 * k + 2][...] = mn
            outs[4 * k + 3][...] = vn

    args = [a for name in names for a in params[name]] + [packed, gcc, gb_ada]
    out_shape = [_sds(params[name][0].shape) for name in names for _ in range(4)]
    outs = pl.pallas_call(body, name="adamw_small", out_shape=out_shape, compiler_params=_cp())(*args)
    return {name: list(outs[4 * k:4 * k + 4]) for k, name in enumerate(names)}


def kernel(x, c, ctx, c_ctx, w_ada, b_ada, g_attn, g_ffn, w_in, ret_decay_fwd, ret_decay_bwd, g_ret, g_q_lora, w_uq, g_kv_lora, w_ukv, w_out, w_ff1, w_ff2, g_final, loss_target, m_c_ctx, m_w_ada, m_b_ada, m_g_attn, m_g_ffn, m_w_in, m_ret_decay_fwd, m_ret_decay_bwd, m_g_ret, m_g_q_lora, m_w_uq, m_g_kv_lora, m_w_ukv, m_w_out, m_w_ff1, m_w_ff2, m_g_final, v_c_ctx, v_w_ada, v_b_ada, v_g_attn, v_g_ffn, v_w_in, v_ret_decay_fwd, v_ret_decay_bwd, v_g_ret, v_g_q_lora, v_w_uq, v_g_kv_lora, v_w_ukv, v_w_out, v_w_ff1, v_w_ff2, v_g_final):
    me = 4 * lax.axis_index("x") + 2 * lax.axis_index("y") + lax.axis_index("c")
    nb = x.shape[0]

    c_pad = jnp.pad(c, ((0, 8 - nb), (0, 0)))
    c_all, g_in, g_uq, g_ukv = _gather_two_level(
        [c_pad, w_in[0].T.astype(BF), w_uq[0].T.astype(BF), w_ukv[0].T.astype(BF)], "gather_weights")
    ws = (*_w_in_pad(g_in.reshape(2240, D_MODEL)), _w_uq_pad(g_uq.reshape(768, 384)),
          _w_ukv_perm(g_ukv.reshape(1024, 256)))

    crows = jnp.concatenate([c_all[:, :nb].reshape(N_DEV * nb, D_MODEL), c_ctx[None], jnp.zeros((7, D_MODEL), F32)])
    b_blk = lax.dynamic_slice(b_ada, (0, me * 768), (1, 768))
    (mod_g,) = _exchange([_mod_fwd(crows, w_ada[0], b_blk)], True, "gather_mod")
    mod_all = _unshard_cols(mod_g)
    behind = mod_g[0, 0, 0:1] * 0.0
    st_g = _exchange_start([(w_out[0] + behind).astype(BF), w_ff1[0].T.astype(BF), w_ff2[0].astype(BF)], True,
                           "gather_ff_start")
    mod_all = mod_all + st_g["token"][0:1, 0:1]
    mod_mine = lax.dynamic_slice(mod_all, (me * nb, 0), (nb, 6 * D_MODEL)).reshape(nb, 6, D_MODEL)
    mod = jnp.pad(mod_mine, ((0, 0), (0, 2), (0, 0)))
    mod_c = jnp.pad(mod_all[16].reshape(1, 6, D_MODEL), ((0, 0), (0, 2), (0, 0)))

    tabs = _rope_tables()
    dec_f = ret_decay_fwd.reshape(N_HEADS, 1, 1)
    dec_b = ret_decay_bwd.reshape(N_HEADS, 1, 1)

    rkc, rvc, k_ctx, v_ctx = _k1_fwd(ctx, mod_c, g_attn, g_q_lora, g_kv_lora, ws, tabs, None, True)
    rq, rk, rv, rg, q, k_all, v_all = _k1_fwd(x, mod, g_attn, g_q_lora, g_kv_lora, ws, tabs, (k_ctx, v_ctx), False)
    o_f, o_b, sf_prev, sb_prev = _k2_fwd(rq, rk, rv, rkc, rvc, dec_f, dec_b)
    y_mla, lse = _k3_fwd(q, k_all, v_all)
    g_out, g_ff1t, g_ff2 = _exchange_wait(st_g, y_mla, "gather_ff_wait")
    wo = g_out.reshape(D_MODEL, D_MODEL)
    w1t = g_ff1t.reshape(D_FF, D_MODEL)
    x_mid, h2 = _k4a_fwd(x, o_f, o_b, rg, y_mla, g_ret, wo, mod, g_ffn)
    dxm, dmlp, relu_a, loss_acc, dgt_f, dg_final = _k4b_mlp_loss(h2, w1t, g_ff2.reshape(D_FF, D_MODEL), x_mid, mod,
                                                                 g_final.reshape(1, D_MODEL), loss_target)

    da, dw1, dw2 = _k4d_mlp_bwd(h2, dmlp, relu_a, g_ff2)
    st_s = _exchange_start([dw1, dw2], False, "scatter_ff_start")
    dh2 = _k4f_dh2(da, w1t, st_s["token"])
    g_ret_t = g_ret + st_s["token"][0:1, 0:1]
    dx_res, do, drg, dym, dwo, dg_ret, dg_ffn, dmod_a = _k4e_bwd(x, o_f, o_b, rg, y_mla, g_ret_t, wo, mod, g_ffn, dxm, dh2)
    dq, dk_all, dv_all = _k3_bwd(q, k_all, v_all, y_mla, lse, dym)
    dqf, dkf, dvf, dqb, dkb, dvb, dkc, dvc, ddf, ddb = _k2_bwd(rq, rk, rv, do, sf_prev, sb_prev, rkc, rvc, dec_f, dec_b)
    cts = [[(dqf, 0), (dqb, 0)], [(dkf, 0), (dkb, 0)], [(dvf, 0), (dvb, 0)], [(drg, 0)], [(dq, 0)],
           [(dk_all, CTX_LEN)], [(dv_all, CTX_LEN)]]
    grad_x, accs, dmod_1 = _k1_bwd(x, mod, g_attn, g_q_lora, g_kv_lora, ws, tabs, cts, dx_res, None, False)
    cts_c = [[(dkc, 0)], [(dvc, 0)], [(dk_all, 0)], [(dv_all, 0)]]
    _, accs, dmod_c1 = _k1_bwd(ctx, mod_c, g_attn, g_q_lora, g_kv_lora, ws, tabs, cts_c, None, accs, True)
    dwa, dwb, dwq, dwk, dg_attn, dg_q, dg_kv = accs

    dmod_loc = (dmod_a + dmod_1).at[:, 5, :].set(dgt_f[:, 0, :])[:, :6, :].reshape(nb, 6 * D_MODEL)
    dmod_ctx = dmod_c1[:, :6, :].reshape(1, 6 * D_MODEL)
    small = {"g_attn": dg_attn, "g_ffn": dg_ffn, "ret_decay_fwd": jnp.sum(ddf[:, :, 0, 0], axis=0),
             "ret_decay_bwd": jnp.sum(ddb[:, :, 0, 0], axis=0), "g_ret": dg_ret, "g_q_lora": dg_q, "g_kv_lora": dg_kv,
             "g_final": dg_final}
    extra = jnp.concatenate([dmod_loc, dmod_ctx, jnp.zeros((5, 6 * D_MODEL), F32)])
    sm_g, ex_g, loss_g = _exchange([_pack_small(small), extra, loss_acc], True, "gather_small")
    dmod_all = ex_g[:, :nb].reshape(N_DEV * nb, 6 * D_MODEL)
    dmodc_parts = ex_g[:, nb]
    dmod_full = jnp.concatenate([dmod_all, jnp.zeros((8, 6 * D_MODEL), F32)])
    dmod_blk = lax.dynamic_slice(dmod_full, (0, me * 768), (24, 768))
    dmodc_blk = lax.dynamic_slice(dmodc_parts, (0, me * 768), (N_DEV, 768))
    gw_ada, gcc_part, gb_ada = _mod_bwd(crows, w_ada[0], dmod_blk, dmodc_blk, dmod_full, dmodc_parts)
    (gcc_g,) = _exchange([gcc_part], True, "gather_c_ctx")

    p_ff1, p_ff2 = _exchange_wait(st_s, gcc_g, "scatter_ff_wait")
    behind = gcc_g[0, 0, 0:1] * 0.0
    st_r = _exchange_start([_w_in_cut(dwa, dwb).astype(BF).reshape(N_DEV, 280, D_MODEL),
                            _w_uq_cut(dwq).astype(BF).reshape(N_DEV, 96, 384),
                            _w_ukv_unperm(dwk).astype(BF).reshape(N_DEV, 128, 256),
                            (dwo.reshape(N_DEV, 128, D_MODEL) + behind).astype(BF)], False, "scatter_rest_start")

    res = {}
    early = (("w_ff1", w_ff1, m_w_ff1, v_w_ff1, p_ff1), ("w_ff2", w_ff2, m_w_ff2, v_w_ff2, p_ff2),
             ("w_ada", w_ada, m_w_ada, v_w_ada, gw_ada[None]))
    for name, w, m, v, pcs in early:
        res[name] = [a[None] for a in _adamw(w[0], m[0], v[0], pcs, "adamw_" + name, after=st_r["token"])]
    pieces = _exchange_wait(st_r, res["w_ada"][3], "scatter_rest_wait")
    for name, w, m, v, pcs in (("w_in", w_in, m_w_in, v_w_in, pieces[0]), ("w_uq", w_uq, m_w_uq, v_w_uq, pieces[1])):
        res[name] = [a.T[None] for a in _adamw(w[0].T, m[0].T, v[0].T, pcs, "adamw_" + name)]
    late = (("w_ukv", w_ukv, m_w_ukv, v_w_ukv, jnp.transpose(pieces[2], (0, 2, 1))),
            ("w_out", w_out, m_w_out, v_w_out, pieces[3]))
    for name, w, m, v, pcs in late:
        res[name] = [a[None] for a in _adamw(w[0], m[0], v[0], pcs, "adamw_" + name)]

    smalls = {"c_ctx": (c_ctx, m_c_ctx, v_c_ctx), "b_ada": (b_ada, m_b_ada, v_b_ada), "g_attn": (g_attn, m_g_attn, v_g_attn),
              "g_ffn": (g_ffn, m_g_ffn, v_g_ffn), "ret_decay_fwd": (ret_decay_fwd, m_ret_decay_fwd, v_ret_decay_fwd),
              "ret_decay_bwd": (ret_decay_bwd, m_ret_decay_bwd, v_ret_decay_bwd), "g_ret": (g_ret, m_g_ret, v_g_ret),
              "g_q_lora": (g_q_lora, m_g_q_lora, v_g_q_lora), "g_kv_lora": (g_kv_lora, m_g_kv_lora, v_g_kv_lora),
              "g_final": (g_final, m_g_final, v_g_final)}
    rows = {k: tuple(a.reshape(1, -1) for a in t) for k, t in smalls.items()}
    for name, outs in _adamw_small(rows, sm_g, gcc_g, gb_ada).items():
        res[name] = [o.reshape(smalls[name][0].shape) for o in outs]

    loss = loss_g[0, 0, 0]
    for k in range(1, N_DEV):
        loss = loss + loss_g[k, 0, 0]

    order = ("c_ctx", "w_ada", "b_ada", "g_attn", "g_ffn", "w_in", "ret_decay_fwd", "ret_decay_bwd", "g_ret", "g_q_lora",
             "w_uq", "g_kv_lora", "w_ukv", "w_out", "w_ff1", "w_ff2", "g_final")
    return (loss, grad_x, *[res[n][0] for n in order], *[res[n][1] for n in order], *[res[n][2] for n in order],
            *[res[n][3] for n in order])
```

```python
import functools
import math

import jax
import jax.numpy as jnp
from jax import lax
from jax.experimental import pallas as pl
from jax.experimental.pallas import tpu as pltpu

F32 = jnp.float32
BF = jnp.bfloat16
EPS = 1e-6
LANE = 128
N_DEV = 8
D_MODEL = 1024
SEQ = 2048
CTX_LEN = 256
GRID_W = 64
N_HEADS = 4
RET_CHUNK = 512
N_CHUNK = SEQ // RET_CHUNK
D_FF = 4096
FF_BLK = D_FF // N_DEV
IN_PAD = 2816
KV_LEN = CTX_LEN + SEQ
ROPE_BASE = 10000.0
ADAM_LR, ADAM_B1, ADAM_B2, ADAM_EPS, ADAM_WD, ADAM_STEP = 0.001, 0.9, 0.999, 1e-08, 0.01, 10
TOK = 256
TOK_B = 256
VMEM_LIMIT = 56 * 1024 * 1024
ARB = "arbitrary"
MESH = pl.DeviceIdType.MESH
_HEAD_SL = [slice(LANE * h, LANE * (h + 1)) for h in range(N_HEADS)]
W_SHAPES = [(2048, D_MODEL), (768, D_MODEL), (1024, 384), (1024, 256)]


def _dot(a, b, ca, cb):
    return lax.dot_general(a.astype(BF), b.astype(BF), (((ca,), (cb,)), ((), ())), preferred_element_type=F32)


@jax.custom_vjp
def mm(a, b):
    return _dot(a, b, 1, 0)


@jax.custom_vjp
def mm_nt(a, b):
    return _dot(a, b, 1, 1)


@jax.custom_vjp
def mm_tn(a, b):
    return _dot(a, b, 0, 0)


mm.defvjp(lambda a, b: (_dot(a, b, 1, 0), (a, b)), lambda r, g: (mm_nt(g, r[1]), mm_tn(r[0], g)))
mm_nt.defvjp(lambda a, b: (_dot(a, b, 1, 1), (a, b)), lambda r, g: (mm(g, r[1]), mm_tn(g, r[0])))
mm_tn.defvjp(lambda a, b: (_dot(a, b, 0, 0), (a, b)), lambda r, g: (mm_nt(r[1], g), mm(r[0], g)))


@jax.custom_vjp
def _mmw(a, w, probe):
    return _dot(a, w, 1, 0)


def _mmw_bwd(r, g):
    a, w = r
    return mm_nt(g, w), jnp.zeros_like(w), mm_tn(a, g)


_mmw.defvjp(lambda a, w, probe: (_dot(a, w, 1, 0), (a, w)), _mmw_bwd)


@jax.custom_vjp
def _mmwt(a, wt, probe):
    return _dot(a, wt, 1, 1)


_mmwt.defvjp(lambda a, wt, probe: (_dot(a, wt, 1, 1), (a, wt)),
             lambda r, g: (mm(g, r[1]), jnp.zeros_like(r[1]), mm_tn(g, r[0])))


def mmwt(a, wt, probe):
    return _dot(a, wt, 1, 1) if probe is None else _mmwt(a, wt, probe)


def mmw(a, w, probe):
    return _dot(a, w, 1, 0) if probe is None else _mmw(a, w, probe)


def rmsn(x, g):
    return x * lax.rsqrt(jnp.mean(x * x, axis=-1, keepdims=True) + EPS) * g


def silu(x):
    return x * jax.nn.sigmoid(x)


def _swap32_impl(x):
    n = x.shape[-1]
    lane = lax.broadcasted_iota(jnp.int32, x.shape, x.ndim - 1) % LANE
    up = pltpu.roll(x, n - 32, x.ndim - 1)
    dn = pltpu.roll(x, 32, x.ndim - 1)
    return jnp.where(lane < 32, up, jnp.where(lane < 64, dn, 0.0))


@jax.custom_vjp
def swap32(x):
    return _swap32_impl(x)


swap32.defvjp(lambda x: (_swap32_impl(x), None), lambda _, g: (_swap32_impl(g),))


def rope(x, cs, sn):
    return x * cs + swap32(x) * sn


def k1_tile(x, sh, sc, g_attn, g_q, g_kv, ws, ps, tabs, is_ctx):
    w_a, w_b, w_uq, w_ukv = ws
    p_a, p_b, p_uq, p_ukv = ps
    cs1, sn1 = tabs
    cs, sn = jnp.concatenate([cs1] * N_HEADS, axis=-1), jnp.concatenate([sn1] * N_HEADS, axis=-1)
    cq_t = jnp.concatenate([jnp.ones_like(cs1), cs1] * N_HEADS, axis=-1)
    sq_t = jnp.concatenate([jnp.zeros_like(sn1), sn1] * N_HEADS, axis=-1)
    h = rmsn(x, g_attn) * (1.0 + sc) + sh
    pa = mmwt(h, w_a, p_a)
    pb = mmwt(h, w_b, p_b)
    rk = pa[:, 512:1024] * 0.125
    rv = pa[:, 1024:1536]
    kpe = pb[:, 640:768]
    kv = mmwt(rmsn(pb[:, 384:640], g_kv), w_ukv, p_ukv)
    if not is_ctx:
        rk = rope(rk, cs, sn)
        kpe = rope(kpe, cs1, sn1)
    k_full = jnp.concatenate([piece for sl in _HEAD_SL for piece in (kv[:, sl], kpe)], axis=-1)
    v = kv[:, 512:]
    if is_ctx:
        return rk, rv, k_full, v
    rq = rope(pa[:, 0:512], cs, sn)
    rg = pa[:, 1536:2048]
    q = rope(mmwt(rmsn(pb[:, 0:384], g_q), w_uq, p_uq), cq_t, sq_t)
    return rq, rk, rv, rg, q, k_full, v


def log_sigmoid(x):
    return jnp.minimum(x, 0.0) - jnp.log(1.0 + jnp.exp(-jnp.abs(x)))


def ret_chunk(q, k, v, s, lg, reverse):
    c = RET_CHUNK
    ii = lax.broadcasted_iota(jnp.int32, (c, c), 0).astype(F32)
    jj = lax.broadcasted_iota(jnp.int32, (c, c), 1).astype(F32)
    diff = (jj - ii) if reverse else (ii - jj)
    dec = jnp.where(diff >= 0, jnp.exp(lg * jnp.maximum(diff, 0.0)), 0.0)
    pos = lax.broadcasted_iota(jnp.int32, (c, 1), 0).astype(F32)
    if reverse:
        wk, wq = jnp.exp(lg * pos), jnp.exp(lg * (c - pos))
    else:
        wk, wq = jnp.exp(lg * (c - 1.0 - pos)), jnp.exp(lg * (pos + 1.0))
    o = mm(mm_nt(q, k) * dec, v) + mm(q * wq, s)
    s_next = jnp.exp(lg * float(c)) * s + mm_tn(k * wk, v)
    return o, s_next


def ctx_state(kc, vc, lg, reverse):
    n = kc.shape[0]
    pos = lax.broadcasted_iota(jnp.int32, (n, 1), 0).astype(F32)
    w = jnp.exp(lg * pos) if reverse else jnp.exp(lg * (n - 1.0 - pos))
    return mm_tn(kc * w, vc)


def attn_head(qn, qp, kn, kp, v):
    s = (mm_nt(qn, kn) + mm_nt(qp, kp)) * (1.0 / math.sqrt(192.0))
    e = jnp.exp(s - jnp.max(s, axis=-1, keepdims=True))
    return mm(e / jnp.sum(e, axis=-1, keepdims=True), v)


def gn_gate(o, rg, g_ret):
    ys = []
    for h in range(N_HEADS):
        sl = slice(LANE * h, LANE * (h + 1))
        oh = o[:, sl]
        mu = jnp.mean(oh, axis=-1, keepdims=True)
        var = jnp.mean(jnp.square(oh - mu), axis=-1, keepdims=True)
        ys.append((oh - mu) * lax.rsqrt(var + EPS) * g_ret[:, sl])
    return jnp.concatenate(ys, axis=-1) * silu(rg)


def k4a_tile(x, o_f, o_b, rg, y_mla, g_ret, gt_a, g_ffn, sh_f, sc_f, w_out, p_out):
    mix = jnp.concatenate([gn_gate(o_f + o_b, rg, g_ret), y_mla], axis=-1)
    x_mid = x + gt_a * mmw(mix, w_out, p_out)
    h2 = rmsn(x_mid, g_ffn) * (1.0 + sc_f) + sh_f
    return x_mid, h2


def k4c_tile(x_mid, mlp, gt_f, g_final, tgt):
    y = rmsn(x_mid + gt_f * mlp, g_final)
    per_tok = jnp.mean(jnp.square(y - tgt), axis=-1, keepdims=True)
    return 0.5 * jnp.sum(per_tok, axis=0, keepdims=True)


def _cp(sem=None, vmem=VMEM_LIMIT):
    return pltpu.CompilerParams(dimension_semantics=sem, vmem_limit_bytes=vmem)


def _acc(ref, val, first):
    @pl.when(first)
    def _():
        ref[...] = val

    @pl.when(jnp.logical_not(first))
    def _():
        ref[...] += val


def _full(shape):
    nd = len(shape)
    return pl.BlockSpec(shape, lambda *_: (0,) * nd)


ANY = pl.BlockSpec(memory_space=pl.ANY)


def _sds(shape, dtype=F32):
    return jax.ShapeDtypeStruct(shape, dtype)


def _exchange(arrs, gather, name):
    n = len(arrs)
    out_shape = [_sds(((N_DEV,) + a.shape) if gather else a.shape, a.dtype) for a in arrs]

    def body(*refs):
        ins, outs = refs[:n], refs[n:2 * n]
        send_sems, recv_sems, local_sems = refs[2 * n:]
        x, y, c = lax.axis_index("x"), lax.axis_index("y"), lax.axis_index("c")
        me = 4 * x + 2 * y + c
        sends, recvs, locs = [], [], []
        for i in range(n):
            for k in range(N_DEV - 1):
                bits = k + 1
                px = x ^ ((bits >> 2) & 1)
                py = y ^ ((bits >> 1) & 1)
                pc = c ^ (bits & 1)
                peer = 4 * px + 2 * py + pc
                src = ins[i] if gather else ins[i].at[peer]
                sem = i * (N_DEV - 1) + k
                sends.append(pltpu.make_async_remote_copy(
                    src_ref=src, dst_ref=outs[i].at[me], send_sem=send_sems.at[sem], recv_sem=recv_sems.at[sem],
                    device_id=(px, py, pc), device_id_type=MESH))
                recvs.append(pltpu.make_async_remote_copy(
                    src_ref=src, dst_ref=outs[i].at[peer], send_sem=send_sems.at[sem], recv_sem=recv_sems.at[sem],
                    device_id=(px, py, pc), device_id_type=MESH))
            locs.append(pltpu.make_async_copy(ins[i] if gather else ins[i].at[me], outs[i].at[me], local_sems.at[i]))
        for cp in locs + sends:
            cp.start()
        for cp in recvs:
            cp.wait_recv()
        for cp in sends:
            cp.wait_send()
        for cp in locs:
            cp.wait()

    outs = pl.pallas_call(
        body, name=name, out_shape=out_shape, in_specs=[ANY] * n, out_specs=[ANY] * n,
        scratch_shapes=[pltpu.SemaphoreType.DMA((n * (N_DEV - 1),)), pltpu.SemaphoreType.DMA((n * (N_DEV - 1),)),
                        pltpu.SemaphoreType.DMA((n,))],
    )(*arrs)
    return list(outs)


def _gather_two_level(arrs, name):
    n = len(arrs)

    def body(*refs):
        ins, outs = refs[:n], refs[n:2 * n]
        send_sems, recv_sems, local_sems = refs[2 * n:]
        x, y, c = lax.axis_index("x"), lax.axis_index("y"), lax.axis_index("c")
        sibling = (x, y, 1 - c)
        chips = [(1 - x, y), (x, 1 - y), (1 - x, 1 - y)]

        def slot(px, py, pc):
            return 4 * px + 2 * py + pc

        first, passed, waits, locs = [], [], [], []
        for i in range(n):
            def copy(k, block, to, src=None, i=i):
                dst = outs[i].at[slot(*block)]
                return pltpu.make_async_remote_copy(
                    src_ref=dst if src is None else src, dst_ref=dst, send_sem=send_sems.at[7 * i + k],
                    recv_sem=recv_sems.at[7 * i + k], device_id=to, device_id_type=MESH)

            locs.append(pltpu.make_async_copy(ins[i], outs[i].at[slot(x, y, c)], local_sems.at[i]))
            first.append(copy(0, (x, y, c), sibling, src=ins[i]))
            first += [copy(1 + j, (x, y, c), (*chip, c), src=ins[i]) for j, chip in enumerate(chips)]
            passed.append([copy(4 + j, (*chip, c), sibling) for j, chip in enumerate(chips)])
            waits.append([copy(1 + j, (*chip, c), (x, y, c)) for j, chip in enumerate(chips)])
        for cp in locs + first:
            cp.start()
        for j in range(3):
            for i in range(n):
                waits[i][j].wait_recv()
                passed[i][j].start()
        for i in range(n):
            def arrival(k, block, i=i):
                dst = outs[i].at[slot(*block)]
                return pltpu.make_async_remote_copy(
                    src_ref=dst, dst_ref=dst, send_sem=send_sems.at[7 * i + k], recv_sem=recv_sems.at[7 * i + k],
                    device_id=sibling, device_id_type=MESH)

            arrival(0, (x, y, 1 - c)).wait_recv()
            for j, chip in enumerate(chips):
                arrival(4 + j, (*chip, 1 - c)).wait_recv()
        for cp in first + [p for ps in passed for p in ps]:
            cp.wait_send()
        for cp in locs:
            cp.wait()

    outs = pl.pallas_call(
        body, name=name, out_shape=[_sds((N_DEV,) + a.shape, a.dtype) for a in arrs], in_specs=[ANY] * n,
        out_specs=[ANY] * n,
        scratch_shapes=[pltpu.SemaphoreType.DMA((7 * n,)), pltpu.SemaphoreType.DMA((7 * n,)),
                        pltpu.SemaphoreType.DMA((n,))],
    )(*arrs)
    return list(outs)


HBM = pl.BlockSpec(memory_space=pltpu.HBM)
SEM = pl.BlockSpec(memory_space=pltpu.SEMAPHORE)
EFFECT = pltpu.SideEffectType.DATAFLOW_SIDE_EFFECTING


def _peer(k):
    x, y, c = lax.axis_index("x"), lax.axis_index("y"), lax.axis_index("c")
    bits = k + 1
    px, py, pc = x ^ ((bits >> 2) & 1), y ^ ((bits >> 1) & 1), c ^ (bits & 1)
    return (px, py, pc), 4 * px + 2 * py + pc, 4 * x + 2 * y + c


def _exchange_start(arrs, gather, name, after=None):
    n = len(arrs)
    lands = [pltpu.with_memory_space_constraint(lax.empty(((N_DEV,) + a.shape) if gather else a.shape, a.dtype),
                                                pltpu.HBM) for a in arrs]
    srcs = [pltpu.with_memory_space_constraint(a, pltpu.HBM) for a in arrs]

    extra = [] if after is None else [after]

    def body(*refs):
        ins, zones = refs[:n], refs[n:2 * n]
        send_sems, recv_sems, local_sems = refs[2 * n + len(extra):2 * n + len(extra) + 3]
        token = refs[-1]
        for i in range(n):
            for k in range(N_DEV - 1):
                dev, peer, me = _peer(k)
                sem = i * (N_DEV - 1) + k
                pltpu.make_async_remote_copy(
                    src_ref=ins[i] if gather else ins[i].at[peer], dst_ref=zones[i].at[me],
                    send_sem=send_sems.at[sem], recv_sem=recv_sems.at[sem], device_id=dev, device_id_type=MESH).start()
            _, _, me = _peer(0)
            pltpu.make_async_copy(ins[i] if gather else ins[i].at[me], zones[i].at[me], local_sems.at[i]).start()
        token[...] = jnp.zeros_like(token)

    nsem = n * (N_DEV - 1)
    outs = pl.pallas_call(
        body, name=name,
        out_shape=[pltpu.SemaphoreType.DMA((nsem,)), pltpu.SemaphoreType.DMA((nsem,)), pltpu.SemaphoreType.DMA((n,))]
        + [pltpu.HBM(a.shape, a.dtype) for a in srcs] + [pltpu.HBM(z.shape, z.dtype) for z in lands]
        + [_sds((8, LANE))],
        in_specs=[HBM] * (2 * n) + [ANY] * len(extra),
        out_specs=[SEM, SEM, SEM] + [HBM] * (2 * n) + [pl.BlockSpec(memory_space=pltpu.VMEM)],
        input_output_aliases={i: 3 + i for i in range(2 * n)},
        compiler_params=pltpu.CompilerParams(has_side_effects=EFFECT),
    )(*srcs, *lands, *extra)
    return {"n": n, "gather": gather, "sems": outs[:3], "srcs": outs[3:3 + n], "lands": outs[3 + n:3 + 2 * n],
            "token": outs[-1]}


def _exchange_wait(st, after, name):
    n, gather = st["n"], st["gather"]

    def body(*refs):
        ins, zones = refs[:n], refs[n:2 * n]
        send_sems, recv_sems, local_sems = refs[2 * n:2 * n + 3]
        for i in range(n):
            for k in range(N_DEV - 1):
                dev, peer, me = _peer(k)
                sem = i * (N_DEV - 1) + k
                src = ins[i] if gather else ins[i].at[peer]
                cp = pltpu.make_async_remote_copy(
                    src_ref=src, dst_ref=zones[i].at[peer], send_sem=send_sems.at[sem], recv_sem=recv_sems.at[sem],
                    device_id=dev, device_id_type=MESH)
                cp.wait_send()
                cp.wait_recv()
            _, _, me = _peer(0)
            pltpu.make_async_copy(ins[i] if gather else ins[i].at[me], zones[i].at[me], local_sems.at[i]).wait()

    outs = pl.pallas_call(
        body, name=name,
        out_shape=[pltpu.HBM(a.shape, a.dtype) for a in st["srcs"]] + [pltpu.HBM(z.shape, z.dtype) for z in st["lands"]],
        in_specs=[HBM] * (2 * n) + [SEM, SEM, SEM, ANY], out_specs=[HBM] * (2 * n),
        input_output_aliases={i: i for i in range(2 * n)},
        compiler_params=pltpu.CompilerParams(has_side_effects=EFFECT),
    )(*st["srcs"], *st["lands"], *st["sems"], after)
    return list(outs[n:])


def _mod_fwd(crows, w_ada, b_blk):
    def body(c_ref, w_ref, b_ref, o_ref):
        o_ref[...] = mm(silu(c_ref[...]), w_ref[...]) + b_ref[...]

    return pl.pallas_call(body, name="mod_fwd", out_shape=_sds((24, 768)), compiler_params=_cp())(crows, w_ada, b_blk)


def _mod_bwd(crows, w_ada, dmod_blk, dmodc_blk, dmod_full, dmodc_full):
    def body(c_ref, w_ref, d_ref, dc_ref, df_ref, dcf_ref, gw_ref, gc_ref, gb_ref):
        cr = c_ref[...]
        dc, dcf = dc_ref[0:1, :], dcf_ref[0:1, :]
        for p in range(1, N_DEV):
            dc = dc + dc_ref[p:p + 1, :]
            dcf = dcf + dcf_ref[p:p + 1, :]
        row = lax.broadcasted_iota(jnp.int32, (24, 1), 0)
        gw_ref[...] = mm_tn(silu(cr), jnp.where(row == 16, dc, d_ref[...]))
        cc = cr[16:17, :]
        sg = jax.nn.sigmoid(cc)
        part = mm_nt(jnp.broadcast_to(dc, (8, 768)), w_ref[...])
        gc_ref[...] = part * (sg * (1.0 + cc * (1.0 - sg)))
        gb_ref[...] = jnp.sum(df_ref[...], axis=0, keepdims=True) + dcf

    return pl.pallas_call(
        body, name="mod_bwd", out_shape=[_sds((D_MODEL, 768)), _sds((8, D_MODEL)), _sds((1, 6 * D_MODEL))],
        compiler_params=_cp())(crows, w_ada, dmod_blk, dmodc_blk, dmod_full, dmodc_full)


def _tab_specs(tk):
    return [pl.BlockSpec((tk, LANE), lambda i, t: (t, 0))] * 2


def _k1_fwd(x, mod, g_attn, g_q, g_kv, ws, tabs, kv_all, is_ctx):
    b, l, _ = x.shape
    nt = l // TOK
    n_f32 = 2 if is_ctx else 4

    def body(x_ref, mod_ref, ga_ref, gq_ref, gk_ref, wa_ref, wb_ref, wq_ref, wk_ref, cs_ref, sn_ref, *rest):
        outs = rest if is_ctx else rest[2:]
        res = k1_tile(x_ref[...], mod_ref[0:1, :], mod_ref[1:2, :], ga_ref[...], gq_ref[...], gk_ref[...],
                      (wa_ref[...], wb_ref[...], wq_ref[...], wk_ref[...]), (None,) * 4,
                      (cs_ref[...], sn_ref[...]), is_ctx)
        for o_ref, r in zip(outs, res):
            o_ref[...] = r.astype(o_ref.dtype)

    tok = lambda w, off=0: pl.BlockSpec((None, TOK, w), lambda i, t: (i, t + off, 0))
    mod_spec = pl.BlockSpec((None, 8, D_MODEL), (lambda i, t: (0, 0, 0)) if is_ctx else (lambda i, t: (i, 0, 0)))
    kv_off = 0 if is_ctx else CTX_LEN // TOK
    in_specs = ([tok(D_MODEL), mod_spec, _full((1, D_MODEL)), _full((1, 384)), _full((1, 256))]
                + [_full(s) for s in W_SHAPES] + _tab_specs(TOK))
    args = [x, mod, g_attn, g_q, g_kv, *ws, *tabs]
    out_specs = [tok(512)] * n_f32 + ([] if is_ctx else [tok(1024)]) + [tok(1024, kv_off), tok(512, kv_off)]
    out_shape = ([_sds((b, l, 512))] * n_f32 + ([] if is_ctx else [_sds((b, l, 1024), BF)])
                 + [_sds((b, KV_LEN, 1024), BF), _sds((b, KV_LEN, 512), BF)])
    aliases = {}
    if not is_ctx:
        aliases = {len(args): n_f32 + 1, len(args) + 1: n_f32 + 2}
        in_specs += [ANY, ANY]
        args += list(kv_all)
    return pl.pallas_call(
        body, name="k1_fwd_ctx" if is_ctx else "k1_fwd", grid=(b, nt), in_specs=in_specs, out_specs=out_specs,
        out_shape=out_shape, input_output_aliases=aliases, compiler_params=_cp((ARB, ARB)),
    )(*args)


N_ACC = 7


def _k1_bwd(x, mod, g_attn, g_q, g_kv, ws, tabs, cts, dx_res, init, is_ctx):
    b, l, _ = x.shape
    tk = TOK_B
    nt = l // tk
    flat_cts = [a for group in cts for a in group]
    group_sizes = [len(g) for g in cts]
    n_ct = len(flat_cts)
    has_res = dx_res is not None
    has_init = init is not None
    acc_shapes = W_SHAPES + [(1, D_MODEL), (1, 384), (1, 256)]

    def body(*refs):
        it = iter(refs)
        x_ref, mod_ref, ga_ref, gq_ref, gk_ref = [next(it) for _ in range(5)]
        w_hbm = [next(it) for _ in range(4)]
        tab_refs = [next(it) for _ in range(2)]
        ct_refs = [next(it) for _ in range(n_ct)]
        res_ref = next(it) if has_res else None
        init_refs = [next(it) for _ in range(N_ACC)] if has_init else None
        gx_ref = next(it) if not is_ctx else None
        out_hbm = [next(it) for _ in range(N_ACC)]
        dmod_ref = next(it)
        w_vmem = [next(it) for _ in range(4)]
        accs = [next(it) for _ in range(N_ACC)]
        sem = next(it)
        i, t = pl.program_id(0), pl.program_id(1)
        first = jnp.logical_and(i == 0, t == 0)
        last = jnp.logical_and(i == b - 1, t == nt - 1)

        @pl.when(first)
        def _():
            for src, dst in zip(w_hbm, w_vmem):
                pltpu.sync_copy(src, dst)
            for k in range(N_ACC):
                if has_init:
                    pltpu.sync_copy(init_refs[k], accs[k])
                else:
                    accs[k][...] = jnp.zeros(acc_shapes[k], F32)

        ct_vals, pos = [], 0
        for gsz in group_sizes:
            v = ct_refs[pos][...].astype(F32)
            for r in ct_refs[pos + 1:pos + gsz]:
                v = v + r[...]
            ct_vals.append(v)
            pos += gsz
        wv = tuple(r[...] for r in w_vmem)
        tv = tuple(r[...] for r in tab_refs)

        def f(xv, sh, sc, ga, gq, gk, *probes):
            return k1_tile(xv, sh, sc, ga, gq, gk, wv, probes, tv, is_ctx)

        probes = [jnp.zeros(s, F32) for s in W_SHAPES]
        _, vjp = jax.vjp(f, x_ref[...], mod_ref[0:1, :], mod_ref[1:2, :], ga_ref[...], gq_ref[...], gk_ref[...], *probes)
        dx, dsh, dsc, dga, dgq, dgk, dwa, dwb, dwq, dwk = vjp(tuple(ct_vals))
        if not is_ctx:
            gx_ref[...] = dx + res_ref[...] if has_res else dx
        for ref, val in zip(accs, (dwa, dwb, dwq, dwk, dga, dgq, dgk)):
            ref[...] += val
        t0 = first if is_ctx else t == 0
        _acc(dmod_ref.at[0:1, :], dsh, t0)
        _acc(dmod_ref.at[1:2, :], dsc, t0)

        @pl.when(t0)
        def _():
            dmod_ref[2:8, :] = jnp.zeros((6, D_MODEL), F32)

        @pl.when(last)
        def _():
            cps = [pltpu.make_async_copy(accs[k], out_hbm[k], sem.at[k]) for k in range(N_ACC)]
            for cp in cps:
                cp.start()
            for cp in cps:
                cp.wait()

    tok = lambda w, off=0: pl.BlockSpec((None, tk, w), lambda i, t: (i, t + off, 0))
    mod_spec = pl.BlockSpec((None, 8, D_MODEL), (lambda i, t: (0, 0, 0)) if is_ctx else (lambda i, t: (i, 0, 0)))
    in_specs = ([tok(D_MODEL), mod_spec, _full((1, D_MODEL)), _full((1, 384)), _full((1, 256))] + [ANY] * 4
                + _tab_specs(tk))
    args = [x, mod, g_attn, g_q, g_kv, *ws, *tabs]
    for a, off in flat_cts:
        in_specs.append(tok(a.shape[-1], off // tk))
        args.append(a)
    if has_res:
        in_specs.append(tok(D_MODEL))
        args.append(dx_res)
    if has_init:
        in_specs += [ANY] * N_ACC
        args += list(init)
    out_shape, out_specs = [], []
    if not is_ctx:
        out_shape.append(_sds((b, l, D_MODEL)))
        out_specs.append(tok(D_MODEL))
    out_shape += [_sds(s) for s in acc_shapes] + [_sds((1 if is_ctx else b, 8, D_MODEL))]
    out_specs += [ANY] * N_ACC + [mod_spec]
    outs = pl.pallas_call(
        body, name="k1_bwd_ctx" if is_ctx else "k1_bwd", grid=(b, nt), in_specs=in_specs, out_specs=out_specs,
        out_shape=out_shape,
        scratch_shapes=[pltpu.VMEM(s, BF) for s in W_SHAPES] + [pltpu.VMEM(s, F32) for s in acc_shapes]
        + [pltpu.SemaphoreType.DMA((N_ACC,))],
        compiler_params=_cp((ARB, ARB)),
    )(*args)
    outs = list(outs)
    gx = None if is_ctx else outs.pop(0)
    return gx, outs[:N_ACC], outs[N_ACC]


def _chunk_spec(rev):
    if rev:
        return pl.BlockSpec((None, RET_CHUNK, 512), lambda i, n: (i, N_CHUNK - 1 - n, 0))
    return pl.BlockSpec((None, RET_CHUNK, 512), lambda i, n: (i, n, 0))


def _state_spec(rev):
    if rev:
        return pl.BlockSpec((None, N_HEADS, None, LANE, LANE), lambda i, n: (i, 0, N_CHUNK - 1 - n, 0, 0))
    return pl.BlockSpec((None, N_HEADS, None, LANE, LANE), lambda i, n: (i, 0, n, 0, 0))


_CTX_SPEC = pl.BlockSpec((None, CTX_LEN, 512), lambda i, n: (i, 0, 0))
_DEC_SPEC = pl.BlockSpec((N_HEADS, 1, 1), lambda i, n: (0, 0, 0))


def _k2_fwd(rq, rk, rv, rkc, rvc, dec_f, dec_b):
    b = rq.shape[0]

    def body(qf, kf, vf, qb, kb, vb, kc, vc, df, db, of_ref, ob_ref, sf_out, sb_out, sf, sb):
        n = pl.program_id(1)
        for h, sl in enumerate(_HEAD_SL):
            lgf, lgb = log_sigmoid(df[h]), log_sigmoid(db[h])

            @pl.when(n == 0)
            def _():
                sf[h] = ctx_state(kc[:, sl], vc[:, sl], lgf, False)
                sb[h] = ctx_state(kc[:, sl], vc[:, sl], lgb, True)

            sf_out[h] = sf[h]
            sb_out[h] = sb[h]
            o, s = ret_chunk(qf[:, sl], kf[:, sl], vf[:, sl], sf[h], lgf, False)
            of_ref[:, sl] = o
            sf[h] = s
            o, s = ret_chunk(qb[:, sl], kb[:, sl], vb[:, sl], sb[h], lgb, True)
            ob_ref[:, sl] = o
            sb[h] = s

    l = rq.shape[1]
    return pl.pallas_call(
        body, name="k2_fwd", grid=(b, N_CHUNK),
        in_specs=[_chunk_spec(False)] * 3 + [_chunk_spec(True)] * 3 + [_CTX_SPEC, _CTX_SPEC, _DEC_SPEC, _DEC_SPEC],
        out_specs=[_chunk_spec(False), _chunk_spec(True), _state_spec(False), _state_spec(True)],
        out_shape=[_sds((b, l, 512)), _sds((b, l, 512)), _sds((b, N_HEADS, N_CHUNK, LANE, LANE)),
                   _sds((b, N_HEADS, N_CHUNK, LANE, LANE))],
        scratch_shapes=[pltpu.VMEM((N_HEADS, LANE, LANE), F32), pltpu.VMEM((N_HEADS, LANE, LANE), F32)],
        compiler_params=_cp((ARB, ARB)),
    )(rq, rk, rv, rq, rk, rv, rkc, rvc, dec_f, dec_b)


def _k2_bwd(rq, rk, rv, do, sf_prev, sb_prev, rkc, rvc, dec_f, dec_b):
    b, l, _ = rq.shape

    def body(qf, kf, vf, gf, spf, qb, kb, vb, gb, spb, kc, vc, df, db,
             dqf, dkf, dvf, dqb, dkb, dvb, dkc, dvc, ddf, ddb, dsf, dsb):
        n = pl.program_id(1)

        @pl.when(n == 0)
        def _():
            dsf[...] = jnp.zeros((N_HEADS, LANE, LANE), F32)
            dsb[...] = jnp.zeros((N_HEADS, LANE, LANE), F32)

        def one(h, sl, q, k, v, g, sp, dec, ds, dq, dk, dv, dd, rev):
            def f(qv, kv_, vv, sv, dcy):
                return ret_chunk(qv, kv_, vv, sv, log_sigmoid(dcy), rev)

            _, vjp = jax.vjp(f, q[:, sl], k[:, sl], v[:, sl], sp[h], dec[h])
            gq, gk, gv, gs, gd = vjp((g[:, sl], ds[h]))
            dq[:, sl] = gq
            dk[:, sl] = gk
            dv[:, sl] = gv
            ds[h] = gs
            _acc(dd.at[h], jnp.broadcast_to(gd, (8, LANE)), n == 0)

        for h, sl in enumerate(_HEAD_SL):
            one(h, sl, qf, kf, vf, gf, spf, df, dsf, dqf, dkf, dvf, ddf, False)
            one(h, sl, qb, kb, vb, gb, spb, db, dsb, dqb, dkb, dvb, ddb, True)

        @pl.when(n == N_CHUNK - 1)
        def _():
            def f(kcv, vcv, dcy, rev):
                return ctx_state(kcv, vcv, log_sigmoid(dcy), rev)

            for h, sl in enumerate(_HEAD_SL):
                _, vjp_f = jax.vjp(functools.partial(f, rev=False), kc[:, sl], vc[:, sl], df[h])
                gk_f, gv_f, gd_f = vjp_f(dsf[h])
                _, vjp_b = jax.vjp(functools.partial(f, rev=True), kc[:, sl], vc[:, sl], db[h])
                gk_b, gv_b, gd_b = vjp_b(dsb[h])
                dkc[:, sl] = gk_f + gk_b
                dvc[:, sl] = gv_f + gv_b
                ddf[h] += jnp.broadcast_to(gd_f, (8, LANE))
                ddb[h] += jnp.broadcast_to(gd_b, (8, LANE))

    dd_spec = pl.BlockSpec((None, N_HEADS, 8, LANE), lambda i, n: (i, 0, 0, 0))
    return pl.pallas_call(
        body, name="k2_bwd", grid=(b, N_CHUNK),
        in_specs=[_chunk_spec(True)] * 4 + [_state_spec(True)] + [_chunk_spec(False)] * 4 + [_state_spec(False)]
        + [_CTX_SPEC, _CTX_SPEC, _DEC_SPEC, _DEC_SPEC],
        out_specs=[_chunk_spec(True)] * 3 + [_chunk_spec(False)] * 3 + [_CTX_SPEC, _CTX_SPEC, dd_spec, dd_spec],
        out_shape=[_sds((b, l, 512))] * 6 + [_sds((b, CTX_LEN, 512))] * 2 + [_sds((b, N_HEADS, 8, LANE))] * 2,
        scratch_shapes=[pltpu.VMEM((N_HEADS, LANE, LANE), F32), pltpu.VMEM((N_HEADS, LANE, LANE), F32)],
        compiler_params=_cp((ARB, ARB)),
    )(rq, rk, rv, do, sf_prev, rq, rk, rv, do, sb_prev, rkc, rvc, dec_f, dec_b)


TQ = 512
QK_W = 2 * LANE
_Q_PARTS = [slice(0, TQ // 2), slice(TQ // 2, TQ)]


SM_SCALE = 1.0 / math.sqrt(192.0)


def _k3_specs():
    qs = lambda w: pl.BlockSpec((None, TQ, w), lambda i, h, t: (i, t, h))
    ks = lambda w: pl.BlockSpec((None, KV_LEN, w), lambda i, h, t: (i, 0, h))
    return qs, ks


def _k3_fwd(q, k, v):
    b, l, _ = q.shape

    def body(q_ref, k_ref, v_ref, o_ref, lse_ref):
        kv_, vv = k_ref[...], v_ref[...]
        for r in _Q_PARTS:
            s = _dot(q_ref[r, :], kv_, 1, 1) * SM_SCALE
            m = jnp.max(s, axis=-1, keepdims=True)
            e = jnp.exp(s - m)
            tot = jnp.sum(e, axis=-1, keepdims=True)
            o_ref[r, :] = _dot(e, vv, 1, 0) * (1.0 / tot)
            lse_ref[r, :] = jnp.broadcast_to(m + jnp.log(tot), (TQ // 2, LANE))

    qs, ks = _k3_specs()
    return pl.pallas_call(
        body, name="k3_fwd", grid=(b, N_HEADS, l // TQ), in_specs=[qs(QK_W), ks(QK_W), ks(LANE)],
        out_specs=[qs(LANE), qs(LANE)], out_shape=[_sds((b, l, N_HEADS * LANE))] * 2,
        compiler_params=_cp((ARB, ARB, ARB)),
    )(q, k, v)


def _k3_bwd(q, k, v, o, lse, dy, after):
    b, l, _ = q.shape

    def body(q_ref, k_ref, v_ref, o_ref, lse_ref, dy_ref, after_ref, dq_ref, dk_ref, dv_ref):
        t0 = pl.program_id(2) == 0
        kv_, vv = k_ref[...], v_ref[...]
        dvs, dks = [], []
        for r in _Q_PARTS:
            qv, dyv = q_ref[r, :], dy_ref[r, :]
            g = dyv.astype(BF)
            lse_col = jnp.max(lse_ref[r, :], axis=-1, keepdims=True)
            delta = jnp.sum(dyv * o_ref[r, :], axis=-1, keepdims=True)
            p = jnp.exp(_dot(qv, kv_, 1, 1) * SM_SCALE - lse_col)
            ds = (p * (_dot(g, vv, 1, 1) - delta) * SM_SCALE).astype(BF)
            dvs.append(_dot(p, g, 0, 0))
            dq_ref[r, :] = _dot(ds, kv_, 1, 0)
            dks.append(_dot(ds, qv, 0, 0))
        _acc(dv_ref, sum(dvs[1:], dvs[0]), t0)
        _acc(dk_ref, sum(dks[1:], dks[0]), t0)

    qs, ks = _k3_specs()
    return pl.pallas_call(
        body, name="k3_bwd", grid=(b, N_HEADS, l // TQ),
        in_specs=[qs(QK_W), ks(QK_W), ks(LANE), qs(LANE), qs(LANE), qs(LANE), ANY],
        out_specs=[qs(QK_W), ks(QK_W), ks(LANE)],
        out_shape=[_sds((b, l, N_HEADS * QK_W)), _sds((b, KV_LEN, N_HEADS * QK_W)), _sds((b, KV_LEN, N_HEADS * LANE))],
        compiler_params=_cp((ARB, ARB, ARB)),
    )(q, k, v, o, lse, dy, after)


def _mod_rows(mod_ref, rows):
    return [mod_ref[r:r + 1, :] for r in rows]


def _k4a_fwd(x, o_f, o_b, rg, y_mla, g_ret, w_out, mod, g_ffn):
    b, l, _ = x.shape

    def body(x_ref, of_ref, ob_ref, rg_ref, ym_ref, gr_ref, wo_ref, mod_ref, gf_ref, xm_ref, h2_ref):
        gt_a, sh_f, sc_f = _mod_rows(mod_ref, (2, 3, 4))
        x_mid, h2 = k4a_tile(x_ref[...], of_ref[...], ob_ref[...], rg_ref[...], ym_ref[...], gr_ref[...], gt_a,
                             gf_ref[...], sh_f, sc_f, wo_ref[...], None)
        xm_ref[...] = x_mid
        h2_ref[...] = h2.astype(BF)

    tok = lambda w: pl.BlockSpec((None, TOK, w), lambda i, t: (i, t, 0))
    mod_spec = pl.BlockSpec((None, 8, D_MODEL), lambda i, t: (i, 0, 0))
    return pl.pallas_call(
        body, name="k4a_fwd", grid=(b, l // TOK),
        in_specs=[tok(D_MODEL), tok(512), tok(512), tok(512), tok(512), _full((1, 512)), _full((D_MODEL, D_MODEL)),
                  mod_spec, _full((1, D_MODEL))],
        out_specs=[tok(D_MODEL), tok(D_MODEL)], out_shape=[_sds((b, l, D_MODEL)), _sds((b, l, D_MODEL), BF)],
        compiler_params=_cp((ARB, ARB)),
    )(x, o_f, o_b, rg, y_mla, g_ret, w_out, mod, g_ffn)


TOK_M = 512
TOK_D = 1024
HALF_FF = D_FF // 2


def _k4b_mlp_loss(h2, w1t, w2, x_mid, mod, g_final, tgt):
    b, l, _ = h2.shape
    nt = l // TOK_M

    def body(h2_ref, w1_hbm, w2_hbm, xm_ref, mod_ref, gfin_ref, tgt_ref, dxm_ref, dmlp_ref, r_ref, loss_ref, dgt_ref,
             dgfin_ref, w1_v, w2_v):
        i, t = pl.program_id(0), pl.program_id(1)
        first = jnp.logical_and(i == 0, t == 0)

        @pl.when(first)
        def _():
            pltpu.sync_copy(w1_hbm, w1_v)
            pltpu.sync_copy(w2_hbm, w2_v)

        h2v = h2_ref[...]
        mlp = None
        for half in range(2):
            rows = slice(half * HALF_FF, (half + 1) * HALF_FF)
            r = jnp.maximum(_dot(h2v, w1_v[rows, :], 1, 1), 0.0)
            r_ref[:, rows] = r.astype(BF)
            part = _dot(jnp.square(r), w2_v[rows, :], 1, 0)
            mlp = part if mlp is None else mlp + part
        (gt_f,) = _mod_rows(mod_ref, (5,))
        loss, vjp = jax.vjp(k4c_tile, xm_ref[...], mlp, gt_f, gfin_ref[...], tgt_ref[...])
        dxm, dmlp, dgt, dgfin, _ = vjp(jnp.ones((1, 1), F32))
        dxm_ref[...] = dxm
        dmlp_ref[...] = dmlp.astype(BF)
        _acc(loss_ref, jnp.broadcast_to(loss, (8, LANE)), first)
        _acc(dgfin_ref, dgfin, first)
        _acc(dgt_ref, dgt, t == 0)

    tok = lambda w: pl.BlockSpec((None, TOK_M, w), lambda i, t: (i, t, 0))
    return pl.pallas_call(
        body, name="k4b_mlp_loss", grid=(b, nt),
        in_specs=[tok(D_MODEL), ANY, ANY, tok(D_MODEL), pl.BlockSpec((None, 8, D_MODEL), lambda i, t: (i, 0, 0)),
                  _full((1, D_MODEL)), tok(D_MODEL)],
        out_specs=[tok(D_MODEL), tok(D_MODEL), tok(D_FF), _full((8, LANE)),
                   pl.BlockSpec((None, 1, D_MODEL), lambda i, t: (i, 0, 0)), _full((1, D_MODEL))],
        out_shape=[_sds((b, l, D_MODEL)), _sds((b, l, D_MODEL), BF), _sds((b, l, D_FF), BF), _sds((8, LANE)),
                   _sds((b, 1, D_MODEL)), _sds((1, D_MODEL))],
        scratch_shapes=[pltpu.VMEM((D_FF, D_MODEL), BF), pltpu.VMEM((D_FF, D_MODEL), BF)],
        compiler_params=_cp((ARB, ARB)),
    )(h2, w1t, w2, x_mid, mod, g_final, tgt)


def _k4d_mlp_bwd(h2, dmlp, r, w2):
    b, l, _ = h2.shape
    nt = l // TOK_D

    def body(h2_ref, dm_ref, r_ref, w2_ref, da_ref, dw1_ref, dw2_ref, acc1, acc2):
        i, t = pl.program_id(1), pl.program_id(2)
        first = jnp.logical_and(i == 0, t == 0)
        rv = r_ref[...].astype(F32)
        dm = dm_ref[...]
        da = (_dot(dm, w2_ref[...], 1, 1) * (2.0 * rv)).astype(BF)
        da_ref[...] = da
        _acc(acc2, _dot(jnp.square(rv), dm, 0, 0), first)
        _acc(acc1, _dot(h2_ref[...], da, 0, 0), first)

        @pl.when(jnp.logical_and(i == b - 1, t == nt - 1))
        def _():
            dw1_ref[...] = acc1[...].astype(BF)
            dw2_ref[...] = acc2[...].astype(BF)

    tok = lambda w: pl.BlockSpec((None, TOK_D, w), lambda j, i, t: (i, t, 0))
    col = pl.BlockSpec((None, TOK_D, FF_BLK), lambda j, i, t: (i, t, j))
    return pl.pallas_call(
        body, name="k4d_mlp_bwd", grid=(N_DEV, b, nt),
        in_specs=[tok(D_MODEL), tok(D_MODEL), col, pl.BlockSpec((None, FF_BLK, D_MODEL), lambda j, i, t: (j, 0, 0))],
        out_specs=[col, pl.BlockSpec((None, D_MODEL, FF_BLK), lambda j, i, t: (j, 0, 0)),
                   pl.BlockSpec((None, FF_BLK, D_MODEL), lambda j, i, t: (j, 0, 0))],
        out_shape=[_sds((b, l, D_FF), BF), _sds((N_DEV, D_MODEL, FF_BLK), BF), _sds((N_DEV, FF_BLK, D_MODEL), BF)],
        scratch_shapes=[pltpu.VMEM((D_MODEL, FF_BLK), F32), pltpu.VMEM((FF_BLK, D_MODEL), F32)],
        compiler_params=_cp((ARB, ARB, ARB)),
    )(h2, dmlp, r, w2)


def _k4f_dh2(da, w1t, after):
    b, l, _ = da.shape

    def body(da_ref, w_ref, after_ref, o_ref):
        o_ref[...] = _dot(da_ref[...], w_ref[...], 1, 0)

    return pl.pallas_call(
        body, name="k4f_dh2", grid=(b, l // TOK_M),
        in_specs=[pl.BlockSpec((None, TOK_M, D_FF), lambda i, t: (i, t, 0)), _full((D_FF, D_MODEL)), ANY],
        out_specs=pl.BlockSpec((None, TOK_M, D_MODEL), lambda i, t: (i, t, 0)), out_shape=_sds((b, l, D_MODEL)),
        compiler_params=_cp((ARB, ARB)),
    )(da, w1t, after)


def _k4e_bwd(x, o_f, o_b, rg, y_mla, g_ret, w_out, mod, g_ffn, dxm, dh2):
    b, l, _ = x.shape

    def body(x_ref, of_ref, ob_ref, rg_ref, ym_ref, gr_ref, wo_ref, mod_ref, gf_ref, dxm_ref, dh2_ref,
             dx_ref, do_ref, drg_ref, dym_ref, dwo_ref, dgr_ref, dgf_ref, dmod_ref):
        i, t = pl.program_id(0), pl.program_id(1)
        first = jnp.logical_and(i == 0, t == 0)
        gt_a, sh_f, sc_f = _mod_rows(mod_ref, (2, 3, 4))
        wo = wo_ref[...]

        def f(xv, ofv, rgv, ymv, grv, gta, gfv, shf, scf, p_out):
            return k4a_tile(xv, ofv, ob_ref[...], rgv, ymv, grv, gta, gfv, shf, scf, wo, p_out)

        _, vjp = jax.vjp(f, x_ref[...], of_ref[...], rg_ref[...], ym_ref[...], gr_ref[...], gt_a, gf_ref[...], sh_f,
                         sc_f, jnp.zeros((D_MODEL, D_MODEL), F32))
        dx, do, drg, dym, dgr, dgta, dgf, dshf, dscf, dwo = vjp((dxm_ref[...], dh2_ref[...]))
        dx_ref[...] = dx
        do_ref[...] = do
        drg_ref[...] = drg
        dym_ref[...] = dym
        _acc(dwo_ref, dwo, first)
        _acc(dgr_ref, dgr, first)
        _acc(dgf_ref, dgf, first)
        t0 = t == 0
        _acc(dmod_ref.at[2:3, :], dgta, t0)
        _acc(dmod_ref.at[3:4, :], dshf, t0)
        _acc(dmod_ref.at[4:5, :], dscf, t0)

        @pl.when(t0)
        def _():
            dmod_ref[0:2, :] = jnp.zeros((2, D_MODEL), F32)
            dmod_ref[5:8, :] = jnp.zeros((3, D_MODEL), F32)

    tok = lambda w: pl.BlockSpec((None, TOK_B, w), lambda i, t: (i, t, 0))
    mod_spec = pl.BlockSpec((None, 8, D_MODEL), lambda i, t: (i, 0, 0))
    return pl.pallas_call(
        body, name="k4e_bwd", grid=(b, l // TOK_B),
        in_specs=[tok(D_MODEL), tok(512), tok(512), tok(512), tok(512), _full((1, 512)), _full((D_MODEL, D_MODEL)),
                  mod_spec, _full((1, D_MODEL)), tok(D_MODEL), tok(D_MODEL)],
        out_specs=[tok(D_MODEL), tok(512), tok(512), tok(512), _full((D_MODEL, D_MODEL)), _full((1, 512)),
                   _full((1, D_MODEL)), mod_spec],
        out_shape=[_sds((b, l, D_MODEL)), _sds((b, l, 512)), _sds((b, l, 512)), _sds((b, l, 512)),
                   _sds((D_MODEL, D_MODEL)), _sds((1, 512)), _sds((1, D_MODEL)), _sds((b, 8, D_MODEL))],
        compiler_params=_cp((ARB, ARB)),
    )(x, o_f, o_b, rg, y_mla, g_ret, w_out, mod, g_ffn, dxm, dh2)


def _adamw(w, m, v, pieces, name, after=None):
    r, c = w.shape
    npc = pieces.shape[0]
    per_row = c * (7 * 4 + npc * pieces.dtype.itemsize) * 2
    rb = r
    for cand in (r, 512, 256, 128, 64, 32, 16, 8):
        if r % cand == 0 and cand * per_row <= 32 * 1024 * 1024:
            rb = cand
            break

    def body(w_ref, m_ref, v_ref, p_ref, *rest):
        g_ref, d_ref, nm_ref, nv_ref = rest[-4:]
        g = p_ref[0].astype(F32)
        for k in range(1, npc):
            g = g + p_ref[k].astype(F32)
        wv = w_ref[...]
        mn = ADAM_B1 * m_ref[...] + (1.0 - ADAM_B1) * g
        vn = ADAM_B2 * v_ref[...] + (1.0 - ADAM_B2) * jnp.square(g)
        m_hat = mn / (1.0 - ADAM_B1 ** ADAM_STEP)
        v_hat = vn / (1.0 - ADAM_B2 ** ADAM_STEP)
        g_ref[...] = g
        d_ref[...] = -ADAM_LR * (m_hat / (jnp.sqrt(v_hat) + ADAM_EPS) + ADAM_WD * wv)
        nm_ref[...] = mn
        nv_ref[...] = vn

    blk = pl.BlockSpec((rb, c), lambda i: (i, 0))
    extra = [] if after is None else [after]
    return pl.pallas_call(
        body, name=name, grid=(r // rb,),
        in_specs=[blk, blk, blk, pl.BlockSpec((npc, rb, c), lambda i: (0, i, 0))] + [ANY] * len(extra),
        out_specs=[blk] * 4, out_shape=[_sds((r, c))] * 4, compiler_params=_cp((ARB,)),
    )(w, m, v, pieces, *extra)


def _pad_head_rows(w, d):
    k = w.shape[1]
    return jnp.pad(w.reshape(N_HEADS, d, k), ((0, 0), (0, LANE - d), (0, 0))).reshape(N_HEADS * LANE, k)


def _cut_head_rows(g, d):
    k = g.shape[1]
    return g.reshape(N_HEADS, LANE, k)[:, :d].reshape(N_HEADS * d, k)


def _w_in_pad(wt):
    w_a = jnp.concatenate([_pad_head_rows(wt[0:256], 64), _pad_head_rows(wt[256:512], 64), wt[512:1536]], axis=0)
    w_b = jnp.concatenate([wt[1536:2176], jnp.pad(wt[2176:2240], ((0, 64), (0, 0)))], axis=0)
    return w_a, w_b


def _w_in_cut(g_a, g_b):
    return jnp.concatenate([_cut_head_rows(g_a[0:512], 64), _cut_head_rows(g_a[512:1024], 64), g_a[1024:2048],
                            g_b[0:704]], axis=0)


def _w_uq_pad(wt):
    return jnp.pad(wt.reshape(N_HEADS, 192, 384), ((0, 0), (0, 64), (0, 0))).reshape(1024, 384)


def _w_uq_cut(g):
    return g.reshape(N_HEADS, 256, 384)[:, :192].reshape(768, 384)


def _w_ukv_perm(wt):
    return jnp.transpose(wt.reshape(N_HEADS, 2, LANE, 256), (1, 0, 2, 3)).reshape(1024, 256)


def _w_ukv_unperm(g):
    return jnp.transpose(g.reshape(2, N_HEADS, LANE, 256), (1, 0, 2, 3)).reshape(1024, 256)


def _unshard_cols(g):
    return jnp.transpose(g, (1, 0, 2)).reshape(g.shape[1], N_DEV * g.shape[2])


def _rope_tables():
    rows = SEQ // GRID_W
    row = jnp.repeat(jnp.arange(rows, dtype=F32), GRID_W)
    col = jnp.tile(jnp.arange(GRID_W, dtype=F32), rows)
    freq = ROPE_BASE ** (-jnp.arange(16, dtype=F32) / 16)
    ang = jnp.concatenate([row[:, None] * freq, col[:, None] * freq], axis=-1)
    cos, sin = jnp.cos(ang), jnp.sin(ang)
    z = jnp.zeros((SEQ, 64), F32)
    return jnp.concatenate([cos, cos, z], axis=1), jnp.concatenate([-sin, sin, z], axis=1)


_PACKED = (("g_attn", 1024), ("g_ffn", 1024), ("ret_decay_fwd", 4), ("ret_decay_bwd", 4), ("g_ret", 512),
           ("g_q_lora", 384), ("g_kv_lora", 256), ("g_final", 1024))
_PACK_OFF = {}
_off = 0
for _name, _n in _PACKED:
    _PACK_OFF[_name] = _off
    _off += -(-_n // LANE) * LANE
PACK_W = _off


def _pack_small(vals):
    parts = []
    for name, n in _PACKED:
        a = vals[name].reshape(-1).astype(F32)
        parts.append(jnp.pad(a, (0, -(-n // LANE) * LANE - n)))
    return jnp.concatenate(parts).reshape(1, PACK_W)


def _adamw_small(params, packed, gcc, gb_ada):
    names = list(params)
    n_p = len(names)

    def body(*refs):
        p_ref, gcc_ref, gb_ref = refs[3 * n_p:3 * n_p + 3]
        outs = refs[3 * n_p + 3:]
        for k, name in enumerate(names):
            w_ref, m_ref, v_ref = refs[3 * k:3 * k + 3]
            n = w_ref.shape[1]
            if name == "b_ada":
                g = gb_ref[...]
            elif name == "c_ctx":
                g = gcc_ref[0, 0:1, :]
                for d in range(1, N_DEV):
                    g = g + gcc_ref[d, 0:1, :]
            else:
                off = _PACK_OFF[name]
                g = p_ref[0, :, off:off + n]
                for d in range(1, N_DEV):
                    g = g + p_ref[d, :, off:off + n]
            mn = ADAM_B1 * m_ref[...] + (1.0 - ADAM_B1) * g
            vn = ADAM_B2 * v_ref[...] + (1.0 - ADAM_B2) * jnp.square(g)
            m_hat = mn / (1.0 - ADAM_B1 ** ADAM_STEP)
            v_hat = vn / (1.0 - ADAM_B2 ** ADAM_STEP)
            outs[4 * k][...] = g
            outs[4 * k + 1][...] = -ADAM_LR * (m_hat / (jnp.sqrt(v_hat) + ADAM_EPS) + ADAM_WD * w_ref[...])
            outs[4 * k + 2][...] = mn
            outs[4 * k + 3][...] = vn

    args = [a for name in names for a in params[name]] + [packed, gcc, gb_ada]
    out_shape = [_sds(params[name][0].shape) for name in names for _ in range(4)]
    outs = pl.pallas_call(body, name="adamw_small", out_shape=out_shape, compiler_params=_cp())(*args)
    return {name: list(outs[4 * k:4 * k + 4]) for k, name in enumerate(names)}


def kernel(x, c, ctx, c_ctx, w_ada, b_ada, g_attn, g_ffn, w_in, ret_decay_fwd, ret_decay_bwd, g_ret, g_q_lora, w_uq, g_kv_lora, w_ukv, w_out, w_ff1, w_ff2, g_final, loss_target, m_c_ctx, m_w_ada, m_b_ada, m_g_attn, m_g_ffn, m_w_in, m_ret_decay_fwd, m_ret_decay_bwd, m_g_ret, m_g_q_lora, m_w_uq, m_g_kv_lora, m_w_ukv, m_w_out, m_w_ff1, m_w_ff2, m_g_final, v_c_ctx, v_w_ada, v_b_ada, v_g_attn, v_g_ffn, v_w_in, v_ret_decay_fwd, v_ret_decay_bwd, v_g_ret, v_g_q_lora, v_w_uq, v_g_kv_lora, v_w_ukv, v_w_out, v_w_ff1, v_w_ff2, v_g_final):
    me = 4 * lax.axis_index("x") + 2 * lax.axis_index("y") + lax.axis_index("c")
    nb = x.shape[0]

    c_pad = jnp.pad(c, ((0, 8 - nb), (0, 0)))
    c_all, g_in, g_uq, g_ukv = _gather_two_level(
        [c_pad, w_in[0].T.astype(BF), w_uq[0].T.astype(BF), w_ukv[0].T.astype(BF)], "gather_weights")
    ws = (*_w_in_pad(g_in.reshape(2240, D_MODEL)), _w_uq_pad(g_uq.reshape(768, 384)),
          _w_ukv_perm(g_ukv.reshape(1024, 256)))

    crows = jnp.concatenate([c_all[:, :nb].reshape(N_DEV * nb, D_MODEL), c_ctx[None], jnp.zeros((7, D_MODEL), F32)])
    b_blk = lax.dynamic_slice(b_ada, (0, me * 768), (1, 768))
    (mod_g,) = _exchange([_mod_fwd(crows, w_ada[0], b_blk)], True, "gather_mod")
    mod_all = _unshard_cols(mod_g)
    behind = mod_g[0, 0, 0:1] * 0.0
    st_o = _exchange_start([(w_out[0] + behind).astype(BF)], True, "gather_wo_start")
    st_g = _exchange_start([w_ff1[0].T.astype(BF), w_ff2[0].astype(BF)], True, "gather_ff_start", after=st_o["token"])
    mod_all = mod_all + st_g["token"][0:1, 0:1]
    mod_mine = lax.dynamic_slice(mod_all, (me * nb, 0), (nb, 6 * D_MODEL)).reshape(nb, 6, D_MODEL)
    mod = jnp.pad(mod_mine, ((0, 0), (0, 2), (0, 0)))
    mod_c = jnp.pad(mod_all[16].reshape(1, 6, D_MODEL), ((0, 0), (0, 2), (0, 0)))

    tabs = _rope_tables()
    dec_f = ret_decay_fwd.reshape(N_HEADS, 1, 1)
    dec_b = ret_decay_bwd.reshape(N_HEADS, 1, 1)

    rkc, rvc, k_ctx, v_ctx = _k1_fwd(ctx, mod_c, g_attn, g_q_lora, g_kv_lora, ws, tabs, None, True)
    rq, rk, rv, rg, q, k_all, v_all = _k1_fwd(x, mod, g_attn, g_q_lora, g_kv_lora, ws, tabs, (k_ctx, v_ctx), False)
    o_f, o_b, sf_prev, sb_prev = _k2_fwd(rq, rk, rv, rkc, rvc, dec_f, dec_b)
    y_mla, lse = _k3_fwd(q, k_all, v_all)
    (g_out,) = _exchange_wait(st_o, y_mla, "gather_wo_wait")
    wo = g_out.reshape(D_MODEL, D_MODEL)
    x_mid, h2 = _k4a_fwd(x, o_f, o_b, rg, y_mla, g_ret, wo, mod, g_ffn)
    g_ff1t, g_ff2 = _exchange_wait(st_g, x_mid, "gather_ff_wait")
    w1t = g_ff1t.reshape(D_FF, D_MODEL)
    dxm, dmlp, relu_a, loss_acc, dgt_f, dg_final = _k4b_mlp_loss(h2, w1t, g_ff2.reshape(D_FF, D_MODEL), x_mid, mod,
                                                                 g_final.reshape(1, D_MODEL), loss_target)

    da, dw1, dw2 = _k4d_mlp_bwd(h2, dmlp, relu_a, g_ff2)
    st_s = _exchange_start([dw1, dw2], False, "scatter_ff_start")
    dh2 = _k4f_dh2(da, w1t, st_s["token"])
    g_ret_t = g_ret + st_s["token"][0:1, 0:1]
    dx_res, do, drg, dym, dwo, dg_ret, dg_ffn, dmod_a = _k4e_bwd(x, o_f, o_b, rg, y_mla, g_ret_t, wo, mod, g_ffn, dxm, dh2)
    st_w = _exchange_start([dwo.reshape(N_DEV, 128, D_MODEL).astype(BF)], False, "scatter_wo_start")
    dq, dk_all, dv_all = _k3_bwd(q, k_all, v_all, y_mla, lse, dym, st_w["token"])
    dqf, dkf, dvf, dqb, dkb, dvb, dkc, dvc, ddf, ddb = _k2_bwd(rq, rk, rv, do, sf_prev, sb_prev, rkc, rvc, dec_f, dec_b)
    cts = [[(dqf, 0), (dqb, 0)], [(dkf, 0), (dkb, 0)], [(dvf, 0), (dvb, 0)], [(drg, 0)], [(dq, 0)],
           [(dk_all, CTX_LEN)], [(dv_all, CTX_LEN)]]
    grad_x, accs, dmod_1 = _k1_bwd(x, mod, g_attn, g_q_lora, g_kv_lora, ws, tabs, cts, dx_res, None, False)
    cts_c = [[(dkc, 0)], [(dvc, 0)], [(dk_all, 0)], [(dv_all, 0)]]
    _, accs, dmod_c1 = _k1_bwd(ctx, mod_c, g_attn, g_q_lora, g_kv_lora, ws, tabs, cts_c, None, accs, True)
    dwa, dwb, dwq, dwk, dg_attn, dg_q, dg_kv = accs

    dmod_loc = (dmod_a + dmod_1).at[:, 5, :].set(dgt_f[:, 0, :])[:, :6, :].reshape(nb, 6 * D_MODEL)
    dmod_ctx = dmod_c1[:, :6, :].reshape(1, 6 * D_MODEL)
    small = {"g_attn": dg_attn, "g_ffn": dg_ffn, "ret_decay_fwd": jnp.sum(ddf[:, :, 0, 0], axis=0),
             "ret_decay_bwd": jnp.sum(ddb[:, :, 0, 0], axis=0), "g_ret": dg_ret, "g_q_lora": dg_q, "g_kv_lora": dg_kv,
             "g_final": dg_final}
    extra = jnp.concatenate([dmod_loc, dmod_ctx, jnp.zeros((5, 6 * D_MODEL), F32)])
    sm_g, ex_g, loss_g = _exchange([_pack_small(small), extra, loss_acc], True, "gather_small")
    dmod_all = ex_g[:, :nb].reshape(N_DEV * nb, 6 * D_MODEL)
    dmodc_parts = ex_g[:, nb]
    dmod_full = jnp.concatenate([dmod_all, jnp.zeros((8, 6 * D_MODEL), F32)])
    dmod_blk = lax.dynamic_slice(dmod_full, (0, me * 768), (24, 768))
    dmodc_blk = lax.dynamic_slice(dmodc_parts, (0, me * 768), (N_DEV, 768))
    gw_ada, gcc_part, gb_ada = _mod_bwd(crows, w_ada[0], dmod_blk, dmodc_blk, dmod_full, dmodc_parts)
    (gcc_g,) = _exchange([gcc_part], True, "gather_c_ctx")

    p_ff1, p_ff2 = _exchange_wait(st_s, gcc_g, "scatter_ff_wait")
    (p_wo,) = _exchange_wait(st_w, p_ff1, "scatter_wo_wait")
    st_r = _exchange_start([_w_in_cut(dwa, dwb).astype(BF).reshape(N_DEV, 280, D_MODEL),
                            _w_uq_cut(dwq).astype(BF).reshape(N_DEV, 96, 384),
                            _w_ukv_unperm(dwk).astype(BF).reshape(N_DEV, 128, 256)], False, "scatter_rest_start",
                           after=p_wo)

    res = {}
    early = (("w_ff1", w_ff1, m_w_ff1, v_w_ff1, p_ff1), ("w_ff2", w_ff2, m_w_ff2, v_w_ff2, p_ff2),
             ("w_ada", w_ada, m_w_ada, v_w_ada, gw_ada[None]))
    for name, w, m, v, pcs in early:
        res[name] = [a[None] for a in _adamw(w[0], m[0], v[0], pcs, "adamw_" + name, after=st_r["token"])]
    pieces = _exchange_wait(st_r, res["w_ada"][3], "scatter_rest_wait")
    for name, w, m, v, pcs in (("w_in", w_in, m_w_in, v_w_in, pieces[0]), ("w_uq", w_uq, m_w_uq, v_w_uq, pieces[1])):
        res[name] = [a.T[None] for a in _adamw(w[0].T, m[0].T, v[0].T, pcs, "adamw_" + name)]
    late = (("w_ukv", w_ukv, m_w_ukv, v_w_ukv, jnp.transpose(pieces[2], (0, 2, 1))),
            ("w_out", w_out, m_w_out, v_w_out, p_wo))
    for name, w, m, v, pcs in late:
        res[name] = [a[None] for a in _adamw(w[0], m[0], v[0], pcs, "adamw_" + name)]

    smalls = {"c_ctx": (c_ctx, m_c_ctx, v_c_ctx), "b_ada": (b_ada, m_b_ada, v_b_ada), "g_attn": (g_attn, m_g_attn, v_g_attn),
              "g_ffn": (g_ffn, m_g_ffn, v_g_ffn), "ret_decay_fwd": (ret_decay_fwd, m_ret_decay_fwd, v_ret_decay_fwd),
              "ret_decay_bwd": (ret_decay_bwd, m_ret_decay_bwd, v_ret_decay_bwd), "g_ret": (g_ret, m_g_ret, v_g_ret),
              "g_q_lora": (g_q_lora, m_g_q_lora, v_g_q_lora), "g_kv_lora": (g_kv_lora, m_g_kv_lora, v_g_kv_lora),
              "g_final": (g_final, m_g_final, v_g_final)}
    rows = {k: tuple(a.reshape(1, -1) for a in t) for k, t in smalls.items()}
    for name, outs in _adamw_small(rows, sm_g, gcc_g, gb_ada).items():
        res[name] = [o.reshape(smalls[name][0].shape) for o in outs]

    loss = loss_g[0, 0, 0]
    for k in range(1, N_DEV):
        loss = loss + loss_g[k, 0, 0]

    order = ("c_ctx", "w_ada", "b_ada", "g_attn", "g_ffn", "w_in", "ret_decay_fwd", "ret_decay_bwd", "g_ret", "g_q_lora",
             "w_uq", "g_kv_lora", "w_ukv", "w_out", "w_ff1", "w_ff2", "g_final")
    return (loss, grad_x, *[res[n][0] for n in order], *[res[n][1] for n in order], *[res[n][2] for n in order],
            *[res[n][3] for n in order])
```

```python
import functools
import math

import jax
import jax.numpy as jnp
from jax import lax
from jax.experimental import pallas as pl
from jax.experimental.pallas import tpu as pltpu

F32 = jnp.float32
BF = jnp.bfloat16
EPS = 1e-6
LANE = 128
N_DEV = 8
D_MODEL = 1024
SEQ = 2048
CTX_LEN = 256
GRID_W = 64
N_HEADS = 4
RET_CHUNK = 512
N_CHUNK = SEQ // RET_CHUNK
D_FF = 4096
FF_BLK = D_FF // N_DEV
IN_PAD = 2816
KV_LEN = CTX_LEN + SEQ
ROPE_BASE = 10000.0
ADAM_LR, ADAM_B1, ADAM_B2, ADAM_EPS, ADAM_WD, ADAM_STEP = 0.001, 0.9, 0.999, 1e-08, 0.01, 10
TOK = 256
TOK_B = 256
VMEM_LIMIT = 56 * 1024 * 1024
ARB = "arbitrary"
MESH = pl.DeviceIdType.MESH
_HEAD_SL = [slice(LANE * h, LANE * (h + 1)) for h in range(N_HEADS)]
W_SHAPES = [(2048, D_MODEL), (768, D_MODEL), (1024, 384), (1024, 256)]


def _dot(a, b, ca, cb):
    return lax.dot_general(a.astype(BF), b.astype(BF), (((ca,), (cb,)), ((), ())), preferred_element_type=F32)


@jax.custom_vjp
def mm(a, b):
    return _dot(a, b, 1, 0)


@jax.custom_vjp
def mm_nt(a, b):
    return _dot(a, b, 1, 1)


@jax.custom_vjp
def mm_tn(a, b):
    return _dot(a, b, 0, 0)


mm.defvjp(lambda a, b: (_dot(a, b, 1, 0), (a, b)), lambda r, g: (mm_nt(g, r[1]), mm_tn(r[0], g)))
mm_nt.defvjp(lambda a, b: (_dot(a, b, 1, 1), (a, b)), lambda r, g: (mm(g, r[1]), mm_tn(g, r[0])))
mm_tn.defvjp(lambda a, b: (_dot(a, b, 0, 0), (a, b)), lambda r, g: (mm_nt(r[1], g), mm(r[0], g)))


@jax.custom_vjp
def _mmw(a, w, probe):
    return _dot(a, w, 1, 0)


def _mmw_bwd(r, g):
    a, w = r
    return mm_nt(g, w), jnp.zeros_like(w), mm_tn(a, g)


_mmw.defvjp(lambda a, w, probe: (_dot(a, w, 1, 0), (a, w)), _mmw_bwd)


@jax.custom_vjp
def _mmwt(a, wt, probe):
    return _dot(a, wt, 1, 1)


_mmwt.defvjp(lambda a, wt, probe: (_dot(a, wt, 1, 1), (a, wt)),
             lambda r, g: (mm(g, r[1]), jnp.zeros_like(r[1]), mm_tn(g, r[0])))


def mmwt(a, wt, probe):
    return _dot(a, wt, 1, 1) if probe is None else _mmwt(a, wt, probe)


def mmw(a, w, probe):
    return _dot(a, w, 1, 0) if probe is None else _mmw(a, w, probe)


def rmsn(x, g):
    return x * lax.rsqrt(jnp.mean(x * x, axis=-1, keepdims=True) + EPS) * g


def silu(x):
    return x * jax.nn.sigmoid(x)


def _swap32_impl(x):
    n = x.shape[-1]
    lane = lax.broadcasted_iota(jnp.int32, x.shape, x.ndim - 1) % LANE
    up = pltpu.roll(x, n - 32, x.ndim - 1)
    dn = pltpu.roll(x, 32, x.ndim - 1)
    return jnp.where(lane < 32, up, jnp.where(lane < 64, dn, 0.0))


@jax.custom_vjp
def swap32(x):
    return _swap32_impl(x)


swap32.defvjp(lambda x: (_swap32_impl(x), None), lambda _, g: (_swap32_impl(g),))


def rope(x, cs, sn):
    return x * cs + swap32(x) * sn


def k1_tile(x, sh, sc, g_attn, g_q, g_kv, ws, ps, tabs, is_ctx):
    w_a, w_b, w_uq, w_ukv = ws
    p_a, p_b, p_uq, p_ukv = ps
    cs1, sn1 = tabs
    cs, sn = jnp.concatenate([cs1] * N_HEADS, axis=-1), jnp.concatenate([sn1] * N_HEADS, axis=-1)
    cq_t = jnp.concatenate([jnp.ones_like(cs1), cs1] * N_HEADS, axis=-1)
    sq_t = jnp.concatenate([jnp.zeros_like(sn1), sn1] * N_HEADS, axis=-1)
    h = rmsn(x, g_attn) * (1.0 + sc) + sh
    pa = mmwt(h, w_a, p_a)
    pb = mmwt(h, w_b, p_b)
    rk = pa[:, 512:1024] * 0.125
    rv = pa[:, 1024:1536]
    kpe = pb[:, 640:768]
    kv = mmwt(rmsn(pb[:, 384:640], g_kv), w_ukv, p_ukv)
    if not is_ctx:
        rk = rope(rk, cs, sn)
        kpe = rope(kpe, cs1, sn1)
    k_full = jnp.concatenate([piece for sl in _HEAD_SL for piece in (kv[:, sl], kpe)], axis=-1)
    v = kv[:, 512:]
    if is_ctx:
        return rk, rv, k_full, v
    rq = rope(pa[:, 0:512], cs, sn)
    rg = pa[:, 1536:2048]
    q = rope(mmwt(rmsn(pb[:, 0:384], g_q), w_uq, p_uq), cq_t, sq_t)
    return rq, rk, rv, rg, q, k_full, v


def log_sigmoid(x):
    return jnp.minimum(x, 0.0) - jnp.log(1.0 + jnp.exp(-jnp.abs(x)))


def ret_chunk(q, k, v, s, lg, reverse):
    c = RET_CHUNK
    ii = lax.broadcasted_iota(jnp.int32, (c, c), 0).astype(F32)
    jj = lax.broadcasted_iota(jnp.int32, (c, c), 1).astype(F32)
    diff = (jj - ii) if reverse else (ii - jj)
    dec = jnp.where(diff >= 0, jnp.exp(lg * jnp.maximum(diff, 0.0)), 0.0)
    pos = lax.broadcasted_iota(jnp.int32, (c, 1), 0).astype(F32)
    if reverse:
        wk, wq = jnp.exp(lg * pos), jnp.exp(lg * (c - pos))
    else:
        wk, wq = jnp.exp(lg * (c - 1.0 - pos)), jnp.exp(lg * (pos + 1.0))
    o = mm(mm_nt(q, k) * dec, v) + mm(q * wq, s)
    s_next = jnp.exp(lg * float(c)) * s + mm_tn(k * wk, v)
    return o, s_next


def ctx_state(kc, vc, lg, reverse):
    n = kc.shape[0]
    pos = lax.broadcasted_iota(jnp.int32, (n, 1), 0).astype(F32)
    w = jnp.exp(lg * pos) if reverse else jnp.exp(lg * (n - 1.0 - pos))
    return mm_tn(kc * w, vc)


def attn_head(qn, qp, kn, kp, v):
    s = (mm_nt(qn, kn) + mm_nt(qp, kp)) * (1.0 / math.sqrt(192.0))
    e = jnp.exp(s - jnp.max(s, axis=-1, keepdims=True))
    return mm(e / jnp.sum(e, axis=-1, keepdims=True), v)


def gn_gate(o, rg, g_ret):
    ys = []
    for h in range(N_HEADS):
        sl = slice(LANE * h, LANE * (h + 1))
        oh = o[:, sl]
        mu = jnp.mean(oh, axis=-1, keepdims=True)
        var = jnp.mean(jnp.square(oh - mu), axis=-1, keepdims=True)
        ys.append((oh - mu) * lax.rsqrt(var + EPS) * g_ret[:, sl])
    return jnp.concatenate(ys, axis=-1) * silu(rg)


def k4a_tile(x, o_f, o_b, rg, y_mla, g_ret, gt_a, g_ffn, sh_f, sc_f, w_out, p_out):
    mix = jnp.concatenate([gn_gate(o_f + o_b, rg, g_ret), y_mla], axis=-1)
    x_mid = x + gt_a * mmw(mix, w_out, p_out)
    h2 = rmsn(x_mid, g_ffn) * (1.0 + sc_f) + sh_f
    return x_mid, h2


def k4c_tile(x_mid, mlp, gt_f, g_final, tgt):
    y = rmsn(x_mid + gt_f * mlp, g_final)
    per_tok = jnp.mean(jnp.square(y - tgt), axis=-1, keepdims=True)
    return 0.5 * jnp.sum(per_tok, axis=0, keepdims=True)


def _cp(sem=None, vmem=VMEM_LIMIT):
    return pltpu.CompilerParams(dimension_semantics=sem, vmem_limit_bytes=vmem)


def _acc(ref, val, first):
    @pl.when(first)
    def _():
        ref[...] = val

    @pl.when(jnp.logical_not(first))
    def _():
        ref[...] += val


def _full(shape):
    nd = len(shape)
    return pl.BlockSpec(shape, lambda *_: (0,) * nd)


ANY = pl.BlockSpec(memory_space=pl.ANY)


def _sds(shape, dtype=F32):
    return jax.ShapeDtypeStruct(shape, dtype)


def _exchange(arrs, gather, name):
    n = len(arrs)
    modes = [gather] * n if isinstance(gather, bool) else list(gather)
    out_shape = [_sds(((N_DEV,) + a.shape) if g else a.shape, a.dtype) for a, g in zip(arrs, modes)]

    def body(*refs):
        ins, outs = refs[:n], refs[n:2 * n]
        send_sems, recv_sems, local_sems = refs[2 * n:]
        x, y, c = lax.axis_index("x"), lax.axis_index("y"), lax.axis_index("c")
        me = 4 * x + 2 * y + c
        sends, recvs, locs = [], [], []
        for i in range(n):
            gather = modes[i]
            for k in range(N_DEV - 1):
                bits = k + 1
                px = x ^ ((bits >> 2) & 1)
                py = y ^ ((bits >> 1) & 1)
                pc = c ^ (bits & 1)
                peer = 4 * px + 2 * py + pc
                src = ins[i] if gather else ins[i].at[peer]
                sem = i * (N_DEV - 1) + k
                sends.append(pltpu.make_async_remote_copy(
                    src_ref=src, dst_ref=outs[i].at[me], send_sem=send_sems.at[sem], recv_sem=recv_sems.at[sem],
                    device_id=(px, py, pc), device_id_type=MESH))
                recvs.append(pltpu.make_async_remote_copy(
                    src_ref=src, dst_ref=outs[i].at[peer], send_sem=send_sems.at[sem], recv_sem=recv_sems.at[sem],
                    device_id=(px, py, pc), device_id_type=MESH))
            locs.append(pltpu.make_async_copy(ins[i] if gather else ins[i].at[me], outs[i].at[me], local_sems.at[i]))
        for cp in locs + sends:
            cp.start()
        for cp in recvs:
            cp.wait_recv()
        for cp in sends:
            cp.wait_send()
        for cp in locs:
            cp.wait()

    outs = pl.pallas_call(
        body, name=name, out_shape=out_shape, in_specs=[ANY] * n, out_specs=[ANY] * n,
        scratch_shapes=[pltpu.SemaphoreType.DMA((n * (N_DEV - 1),)), pltpu.SemaphoreType.DMA((n * (N_DEV - 1),)),
                        pltpu.SemaphoreType.DMA((n,))],
    )(*arrs)
    return list(outs)


def _gather_two_level(arrs, name):
    n = len(arrs)

    def body(*refs):
        ins, outs = refs[:n], refs[n:2 * n]
        send_sems, recv_sems, local_sems = refs[2 * n:]
        x, y, c = lax.axis_index("x"), lax.axis_index("y"), lax.axis_index("c")
        sibling = (x, y, 1 - c)
        chips = [(1 - x, y), (x, 1 - y), (1 - x, 1 - y)]

        def slot(px, py, pc):
            return 4 * px + 2 * py + pc

        first, passed, waits, locs = [], [], [], []
        for i in range(n):
            def copy(k, block, to, src=None, i=i):
                dst = outs[i].at[slot(*block)]
                return pltpu.make_async_remote_copy(
                    src_ref=dst if src is None else src, dst_ref=dst, send_sem=send_sems.at[7 * i + k],
                    recv_sem=recv_sems.at[7 * i + k], device_id=to, device_id_type=MESH)

            locs.append(pltpu.make_async_copy(ins[i], outs[i].at[slot(x, y, c)], local_sems.at[i]))
            first.append(copy(0, (x, y, c), sibling, src=ins[i]))
            first += [copy(1 + j, (x, y, c), (*chip, c), src=ins[i]) for j, chip in enumerate(chips)]
            passed.append([copy(4 + j, (*chip, c), sibling) for j, chip in enumerate(chips)])
            waits.append([copy(1 + j, (*chip, c), (x, y, c)) for j, chip in enumerate(chips)])
        for cp in locs + first:
            cp.start()
        for j in range(3):
            for i in range(n):
                waits[i][j].wait_recv()
                passed[i][j].start()
        for i in range(n):
            def arrival(k, block, i=i):
                dst = outs[i].at[slot(*block)]
                return pltpu.make_async_remote_copy(
                    src_ref=dst, dst_ref=dst, send_sem=send_sems.at[7 * i + k], recv_sem=recv_sems.at[7 * i + k],
                    device_id=sibling, device_id_type=MESH)

            arrival(0, (x, y, 1 - c)).wait_recv()
            for j, chip in enumerate(chips):
                arrival(4 + j, (*chip, 1 - c)).wait_recv()
        for cp in first + [p for ps in passed for p in ps]:
            cp.wait_send()
        for cp in locs:
            cp.wait()

    outs = pl.pallas_call(
        body, name=name, out_shape=[_sds((N_DEV,) + a.shape, a.dtype) for a in arrs], in_specs=[ANY] * n,
        out_specs=[ANY] * n,
        scratch_shapes=[pltpu.SemaphoreType.DMA((7 * n,)), pltpu.SemaphoreType.DMA((7 * n,)),
                        pltpu.SemaphoreType.DMA((n,))],
    )(*arrs)
    return list(outs)


HBM = pl.BlockSpec(memory_space=pltpu.HBM)
SEM = pl.BlockSpec(memory_space=pltpu.SEMAPHORE)
EFFECT = pltpu.SideEffectType.DATAFLOW_SIDE_EFFECTING


def _peer(k):
    x, y, c = lax.axis_index("x"), lax.axis_index("y"), lax.axis_index("c")
    bits = k + 1
    px, py, pc = x ^ ((bits >> 2) & 1), y ^ ((bits >> 1) & 1), c ^ (bits & 1)
    return (px, py, pc), 4 * px + 2 * py + pc, 4 * x + 2 * y + c


def _exchange_start(arrs, gather, name, after=None):
    n = len(arrs)
    lands = [pltpu.with_memory_space_constraint(lax.empty(((N_DEV,) + a.shape) if gather else a.shape, a.dtype),
                                                pltpu.HBM) for a in arrs]
    srcs = [pltpu.with_memory_space_constraint(a, pltpu.HBM) for a in arrs]

    extra = [] if after is None else [after]

    def body(*refs):
        ins, zones = refs[:n], refs[n:2 * n]
        send_sems, recv_sems, local_sems = refs[2 * n + len(extra):2 * n + len(extra) + 3]
        token = refs[-1]
        for i in range(n):
            for k in range(N_DEV - 1):
                dev, peer, me = _peer(k)
                sem = i * (N_DEV - 1) + k
                pltpu.make_async_remote_copy(
                    src_ref=ins[i] if gather else ins[i].at[peer], dst_ref=zones[i].at[me],
                    send_sem=send_sems.at[sem], recv_sem=recv_sems.at[sem], device_id=dev, device_id_type=MESH).start()
            _, _, me = _peer(0)
            pltpu.make_async_copy(ins[i] if gather else ins[i].at[me], zones[i].at[me], local_sems.at[i]).start()
        token[...] = jnp.zeros_like(token)

    nsem = n * (N_DEV - 1)
    outs = pl.pallas_call(
        body, name=name,
        out_shape=[pltpu.SemaphoreType.DMA((nsem,)), pltpu.SemaphoreType.DMA((nsem,)), pltpu.SemaphoreType.DMA((n,))]
        + [pltpu.HBM(a.shape, a.dtype) for a in srcs] + [pltpu.HBM(z.shape, z.dtype) for z in lands]
        + [_sds((8, LANE))],
        in_specs=[HBM] * (2 * n) + [ANY] * len(extra),
        out_specs=[SEM, SEM, SEM] + [HBM] * (2 * n) + [pl.BlockSpec(memory_space=pltpu.VMEM)],
        input_output_aliases={i: 3 + i for i in range(2 * n)},
        compiler_params=pltpu.CompilerParams(has_side_effects=EFFECT),
    )(*srcs, *lands, *extra)
    return {"n": n, "gather": gather, "sems": outs[:3], "srcs": outs[3:3 + n], "lands": outs[3 + n:3 + 2 * n],
            "token": outs[-1]}


def _exchange_wait(st, after, name):
    n, gather = st["n"], st["gather"]

    def body(*refs):
        ins, zones = refs[:n], refs[n:2 * n]
        send_sems, recv_sems, local_sems = refs[2 * n:2 * n + 3]
        for i in range(n):
            for k in range(N_DEV - 1):
                dev, peer, me = _peer(k)
                sem = i * (N_DEV - 1) + k
                src = ins[i] if gather else ins[i].at[peer]
                cp = pltpu.make_async_remote_copy(
                    src_ref=src, dst_ref=zones[i].at[peer], send_sem=send_sems.at[sem], recv_sem=recv_sems.at[sem],
                    device_id=dev, device_id_type=MESH)
                cp.wait_send()
                cp.wait_recv()
            _, _, me = _peer(0)
            pltpu.make_async_copy(ins[i] if gather else ins[i].at[me], zones[i].at[me], local_sems.at[i]).wait()

    outs = pl.pallas_call(
        body, name=name,
        out_shape=[pltpu.HBM(a.shape, a.dtype) for a in st["srcs"]] + [pltpu.HBM(z.shape, z.dtype) for z in st["lands"]],
        in_specs=[HBM] * (2 * n) + [SEM, SEM, SEM, ANY], out_specs=[HBM] * (2 * n),
        input_output_aliases={i: i for i in range(2 * n)},
        compiler_params=pltpu.CompilerParams(has_side_effects=EFFECT),
    )(*st["srcs"], *st["lands"], *st["sems"], after)
    return list(outs[n:])


def _mod_fwd(crows, w_ada, b_blk):
    def body(c_ref, w_ref, b_ref, o_ref):
        o_ref[...] = mm(silu(c_ref[...]), w_ref[...]) + b_ref[...]

    return pl.pallas_call(body, name="mod_fwd", out_shape=_sds((24, 768)), compiler_params=_cp())(crows, w_ada, b_blk)


def _mod_bwd(crows, w_ada, dmod_blk, dmodc_blk):
    def body(c_ref, w_ref, d_ref, dc_ref, gw_ref, gc_ref, gb_ref):
        cr = c_ref[...]
        dc = dc_ref[0:1, :]
        for p in range(1, N_DEV):
            dc = dc + dc_ref[p:p + 1, :]
        row = lax.broadcasted_iota(jnp.int32, (24, 1), 0)
        gw_ref[...] = mm_tn(silu(cr), jnp.where(row == 16, dc, d_ref[...]))
        cc = cr[16:17, :]
        sg = jax.nn.sigmoid(cc)
        part = mm_nt(jnp.broadcast_to(dc, (8, 768)), w_ref[...])
        gc_ref[...] = part * (sg * (1.0 + cc * (1.0 - sg)))
        gb_ref[...] = jnp.broadcast_to(jnp.sum(d_ref[...], axis=0, keepdims=True) + dc, (8, 768))

    return pl.pallas_call(
        body, name="mod_bwd", out_shape=[_sds((D_MODEL, 768)), _sds((8, D_MODEL)), _sds((8, 768))],
        compiler_params=_cp())(crows, w_ada, dmod_blk, dmodc_blk)


def _tab_specs(tk):
    return [pl.BlockSpec((tk, LANE), lambda i, t: (t, 0))] * 2


def _k1_fwd(x, mod, g_attn, g_q, g_kv, ws, tabs, kv_all, is_ctx):
    b, l, _ = x.shape
    nt = l // TOK
    n_f32 = 2 if is_ctx else 4

    def body(x_ref, mod_ref, ga_ref, gq_ref, gk_ref, wa_ref, wb_ref, wq_ref, wk_ref, cs_ref, sn_ref, *rest):
        outs = rest if is_ctx else rest[2:]
        res = k1_tile(x_ref[...], mod_ref[0:1, :], mod_ref[1:2, :], ga_ref[...], gq_ref[...], gk_ref[...],
                      (wa_ref[...], wb_ref[...], wq_ref[...], wk_ref[...]), (None,) * 4,
                      (cs_ref[...], sn_ref[...]), is_ctx)
        for o_ref, r in zip(outs, res):
            o_ref[...] = r.astype(o_ref.dtype)

    tok = lambda w, off=0: pl.BlockSpec((None, TOK, w), lambda i, t: (i, t + off, 0))
    mod_spec = pl.BlockSpec((None, 8, D_MODEL), (lambda i, t: (0, 0, 0)) if is_ctx else (lambda i, t: (i, 0, 0)))
    kv_off = 0 if is_ctx else CTX_LEN // TOK
    in_specs = ([tok(D_MODEL), mod_spec, _full((1, D_MODEL)), _full((1, 384)), _full((1, 256))]
                + [_full(s) for s in W_SHAPES] + _tab_specs(TOK))
    args = [x, mod, g_attn, g_q, g_kv, *ws, *tabs]
    out_specs = [tok(512)] * n_f32 + ([] if is_ctx else [tok(1024)]) + [tok(1024, kv_off), tok(512, kv_off)]
    out_shape = ([_sds((b, l, 512))] * n_f32 + ([] if is_ctx else [_sds((b, l, 1024), BF)])
                 + [_sds((b, KV_LEN, 1024), BF), _sds((b, KV_LEN, 512), BF)])
    aliases = {}
    if not is_ctx:
        aliases = {len(args): n_f32 + 1, len(args) + 1: n_f32 + 2}
        in_specs += [ANY, ANY]
        args += list(kv_all)
    return pl.pallas_call(
        body, name="k1_fwd_ctx" if is_ctx else "k1_fwd", grid=(b, nt), in_specs=in_specs, out_specs=out_specs,
        out_shape=out_shape, input_output_aliases=aliases, compiler_params=_cp((ARB, ARB)),
    )(*args)


N_ACC = 7


def _k1_bwd(x, mod, g_attn, g_q, g_kv, ws, tabs, cts, dx_res, init, is_ctx):
    b, l, _ = x.shape
    tk = TOK_B
    nt = l // tk
    flat_cts = [a for group in cts for a in group]
    group_sizes = [len(g) for g in cts]
    n_ct = len(flat_cts)
    has_res = dx_res is not None
    has_init = init is not None
    acc_shapes = W_SHAPES + [(1, D_MODEL), (1, 384), (1, 256)]

    def body(*refs):
        it = iter(refs)
        x_ref, mod_ref, ga_ref, gq_ref, gk_ref = [next(it) for _ in range(5)]
        w_hbm = [next(it) for _ in range(4)]
        tab_refs = [next(it) for _ in range(2)]
        ct_refs = [next(it) for _ in range(n_ct)]
        res_ref = next(it) if has_res else None
        init_refs = [next(it) for _ in range(N_ACC)] if has_init else None
        gx_ref = next(it) if not is_ctx else None
        out_hbm = [next(it) for _ in range(N_ACC)]
        dmod_ref = next(it)
        w_vmem = [next(it) for _ in range(4)]
        accs = [next(it) for _ in range(N_ACC)]
        sem = next(it)
        i, t = pl.program_id(0), pl.program_id(1)
        first = jnp.logical_and(i == 0, t == 0)
        last = jnp.logical_and(i == b - 1, t == nt - 1)

        @pl.when(first)
        def _():
            for src, dst in zip(w_hbm, w_vmem):
                pltpu.sync_copy(src, dst)
            for k in range(N_ACC):
                if has_init:
                    pltpu.sync_copy(init_refs[k], accs[k])
                else:
                    accs[k][...] = jnp.zeros(acc_shapes[k], F32)

        ct_vals, pos = [], 0
        for gsz in group_sizes:
            v = ct_refs[pos][...].astype(F32)
            for r in ct_refs[pos + 1:pos + gsz]:
                v = v + r[...]
            ct_vals.append(v)
            pos += gsz
        wv = tuple(r[...] for r in w_vmem)
        tv = tuple(r[...] for r in tab_refs)

        def f(xv, sh, sc, ga, gq, gk, *probes):
            return k1_tile(xv, sh, sc, ga, gq, gk, wv, probes, tv, is_ctx)

        probes = [jnp.zeros(s, F32) for s in W_SHAPES]
        _, vjp = jax.vjp(f, x_ref[...], mod_ref[0:1, :], mod_ref[1:2, :], ga_ref[...], gq_ref[...], gk_ref[...], *probes)
        dx, dsh, dsc, dga, dgq, dgk, dwa, dwb, dwq, dwk = vjp(tuple(ct_vals))
        if not is_ctx:
            gx_ref[...] = dx + res_ref[...] if has_res else dx
        for ref, val in zip(accs, (dwa, dwb, dwq, dwk, dga, dgq, dgk)):
            ref[...] += val
        t0 = first if is_ctx else t == 0
        _acc(dmod_ref.at[0:1, :], dsh, t0)
        _acc(dmod_ref.at[1:2, :], dsc, t0)

        @pl.when(t0)
        def _():
            dmod_ref[2:8, :] = jnp.zeros((6, D_MODEL), F32)

        @pl.when(last)
        def _():
            cps = [pltpu.make_async_copy(accs[k], out_hbm[k], sem.at[k]) for k in range(N_ACC)]
            for cp in cps:
                cp.start()
            for cp in cps:
                cp.wait()

    tok = lambda w, off=0: pl.BlockSpec((None, tk, w), lambda i, t: (i, t + off, 0))
    mod_spec = pl.BlockSpec((None, 8, D_MODEL), (lambda i, t: (0, 0, 0)) if is_ctx else (lambda i, t: (i, 0, 0)))
    in_specs = ([tok(D_MODEL), mod_spec, _full((1, D_MODEL)), _full((1, 384)), _full((1, 256))] + [ANY] * 4
                + _tab_specs(tk))
    args = [x, mod, g_attn, g_q, g_kv, *ws, *tabs]
    for a, off in flat_cts:
        in_specs.append(tok(a.shape[-1], off // tk))
        args.append(a)
    if has_res:
        in_specs.append(tok(D_MODEL))
        args.append(dx_res)
    if has_init:
        in_specs += [ANY] * N_ACC
        args += list(init)
    out_shape, out_specs = [], []
    if not is_ctx:
        out_shape.append(_sds((b, l, D_MODEL)))
        out_specs.append(tok(D_MODEL))
    out_shape += [_sds(s) for s in acc_shapes] + [_sds((1 if is_ctx else b, 8, D_MODEL))]
    out_specs += [ANY] * N_ACC + [mod_spec]
    outs = pl.pallas_call(
        body, name="k1_bwd_ctx" if is_ctx else "k1_bwd", grid=(b, nt), in_specs=in_specs, out_specs=out_specs,
        out_shape=out_shape,
        scratch_shapes=[pltpu.VMEM(s, BF) for s in W_SHAPES] + [pltpu.VMEM(s, F32) for s in acc_shapes]
        + [pltpu.SemaphoreType.DMA((N_ACC,))],
        compiler_params=_cp((ARB, ARB)),
    )(*args)
    outs = list(outs)
    gx = None if is_ctx else outs.pop(0)
    return gx, outs[:N_ACC], outs[N_ACC]


def _chunk_spec(rev):
    if rev:
        return pl.BlockSpec((None, RET_CHUNK, 512), lambda i, n: (i, N_CHUNK - 1 - n, 0))
    return pl.BlockSpec((None, RET_CHUNK, 512), lambda i, n: (i, n, 0))


def _state_spec(rev):
    if rev:
        return pl.BlockSpec((None, N_HEADS, None, LANE, LANE), lambda i, n: (i, 0, N_CHUNK - 1 - n, 0, 0))
    return pl.BlockSpec((None, N_HEADS, None, LANE, LANE), lambda i, n: (i, 0, n, 0, 0))


_CTX_SPEC = pl.BlockSpec((None, CTX_LEN, 512), lambda i, n: (i, 0, 0))
_DEC_SPEC = pl.BlockSpec((N_HEADS, 1, 1), lambda i, n: (0, 0, 0))


def _k2_fwd(rq, rk, rv, rkc, rvc, dec_f, dec_b):
    b = rq.shape[0]

    def body(qf, kf, vf, qb, kb, vb, kc, vc, df, db, of_ref, ob_ref, sf_out, sb_out, sf, sb):
        n = pl.program_id(1)
        for h, sl in enumerate(_HEAD_SL):
            lgf, lgb = log_sigmoid(df[h]), log_sigmoid(db[h])

            @pl.when(n == 0)
            def _():
                sf[h] = ctx_state(kc[:, sl], vc[:, sl], lgf, False)
                sb[h] = ctx_state(kc[:, sl], vc[:, sl], lgb, True)

            sf_out[h] = sf[h]
            sb_out[h] = sb[h]
            o, s = ret_chunk(qf[:, sl], kf[:, sl], vf[:, sl], sf[h], lgf, False)
            of_ref[:, sl] = o
            sf[h] = s
            o, s = ret_chunk(qb[:, sl], kb[:, sl], vb[:, sl], sb[h], lgb, True)
            ob_ref[:, sl] = o
            sb[h] = s

    l = rq.shape[1]
    return pl.pallas_call(
        body, name="k2_fwd", grid=(b, N_CHUNK),
        in_specs=[_chunk_spec(False)] * 3 + [_chunk_spec(True)] * 3 + [_CTX_SPEC, _CTX_SPEC, _DEC_SPEC, _DEC_SPEC],
        out_specs=[_chunk_spec(False), _chunk_spec(True), _state_spec(False), _state_spec(True)],
        out_shape=[_sds((b, l, 512)), _sds((b, l, 512)), _sds((b, N_HEADS, N_CHUNK, LANE, LANE)),
                   _sds((b, N_HEADS, N_CHUNK, LANE, LANE))],
        scratch_shapes=[pltpu.VMEM((N_HEADS, LANE, LANE), F32), pltpu.VMEM((N_HEADS, LANE, LANE), F32)],
        compiler_params=_cp((ARB, ARB)),
    )(rq, rk, rv, rq, rk, rv, rkc, rvc, dec_f, dec_b)


def _k2_bwd(rq, rk, rv, do, sf_prev, sb_prev, rkc, rvc, dec_f, dec_b):
    b, l, _ = rq.shape

    def body(qf, kf, vf, gf, spf, qb, kb, vb, gb, spb, kc, vc, df, db,
             dqf, dkf, dvf, dqb, dkb, dvb, dkc, dvc, ddf, ddb, dsf, dsb):
        n = pl.program_id(1)

        @pl.when(n == 0)
        def _():
            dsf[...] = jnp.zeros((N_HEADS, LANE, LANE), F32)
            dsb[...] = jnp.zeros((N_HEADS, LANE, LANE), F32)

        def one(h, sl, q, k, v, g, sp, dec, ds, dq, dk, dv, dd, rev):
            def f(qv, kv_, vv, sv, dcy):
                return ret_chunk(qv, kv_, vv, sv, log_sigmoid(dcy), rev)

            _, vjp = jax.vjp(f, q[:, sl], k[:, sl], v[:, sl], sp[h], dec[h])
            gq, gk, gv, gs, gd = vjp((g[:, sl], ds[h]))
            dq[:, sl] = gq
            dk[:, sl] = gk
            dv[:, sl] = gv
            ds[h] = gs
            _acc(dd.at[h], jnp.broadcast_to(gd, (8, LANE)), n == 0)

        for h, sl in enumerate(_HEAD_SL):
            one(h, sl, qf, kf, vf, gf, spf, df, dsf, dqf, dkf, dvf, ddf, False)
            one(h, sl, qb, kb, vb, gb, spb, db, dsb, dqb, dkb, dvb, ddb, True)

        @pl.when(n == N_CHUNK - 1)
        def _():
            def f(kcv, vcv, dcy, rev):
                return ctx_state(kcv, vcv, log_sigmoid(dcy), rev)

            for h, sl in enumerate(_HEAD_SL):
                _, vjp_f = jax.vjp(functools.partial(f, rev=False), kc[:, sl], vc[:, sl], df[h])
                gk_f, gv_f, gd_f = vjp_f(dsf[h])
                _, vjp_b = jax.vjp(functools.partial(f, rev=True), kc[:, sl], vc[:, sl], db[h])
                gk_b, gv_b, gd_b = vjp_b(dsb[h])
                dkc[:, sl] = gk_f + gk_b
                dvc[:, sl] = gv_f + gv_b
                ddf[h] += jnp.broadcast_to(gd_f, (8, LANE))
                ddb[h] += jnp.broadcast_to(gd_b, (8, LANE))

    dd_spec = pl.BlockSpec((None, N_HEADS, 8, LANE), lambda i, n: (i, 0, 0, 0))
    return pl.pallas_call(
        body, name="k2_bwd", grid=(b, N_CHUNK),
        in_specs=[_chunk_spec(True)] * 4 + [_state_spec(True)] + [_chunk_spec(False)] * 4 + [_state_spec(False)]
        + [_CTX_SPEC, _CTX_SPEC, _DEC_SPEC, _DEC_SPEC],
        out_specs=[_chunk_spec(True)] * 3 + [_chunk_spec(False)] * 3 + [_CTX_SPEC, _CTX_SPEC, dd_spec, dd_spec],
        out_shape=[_sds((b, l, 512))] * 6 + [_sds((b, CTX_LEN, 512))] * 2 + [_sds((b, N_HEADS, 8, LANE))] * 2,
        scratch_shapes=[pltpu.VMEM((N_HEADS, LANE, LANE), F32), pltpu.VMEM((N_HEADS, LANE, LANE), F32)],
        compiler_params=_cp((ARB, ARB)),
    )(rq, rk, rv, do, sf_prev, rq, rk, rv, do, sb_prev, rkc, rvc, dec_f, dec_b)


TQ = 512
QK_W = 2 * LANE
N_QP = 2
_Q_PARTS = [slice(i * TQ // N_QP, (i + 1) * TQ // N_QP) for i in range(N_QP)]


SM_SCALE = 1.0 / math.sqrt(192.0)


def _k3_specs():
    qs = lambda w: pl.BlockSpec((None, TQ, w), lambda i, h, t: (i, t, h))
    ks = lambda w: pl.BlockSpec((None, KV_LEN, w), lambda i, h, t: (i, 0, h))
    return qs, ks


def _k3_fwd(q, k, v):
    b, l, _ = q.shape

    def body(q_ref, k_ref, v_ref, o_ref, lse_ref):
        kv_, vv = k_ref[...], v_ref[...]
        for r in _Q_PARTS:
            s = _dot(q_ref[r, :], kv_, 1, 1) * SM_SCALE
            m = jnp.max(s, axis=-1, keepdims=True)
            e = jnp.exp(s - m)
            tot = jnp.sum(e, axis=-1, keepdims=True)
            o_ref[r, :] = _dot(e, vv, 1, 0) * (1.0 / tot)
            lse_ref[r, :] = jnp.broadcast_to(m + jnp.log(tot), (TQ // N_QP, LANE))

    qs, ks = _k3_specs()
    return pl.pallas_call(
        body, name="k3_fwd", grid=(b, N_HEADS, l // TQ), in_specs=[qs(QK_W), ks(QK_W), ks(LANE)],
        out_specs=[qs(LANE), qs(LANE)], out_shape=[_sds((b, l, N_HEADS * LANE))] * 2,
        compiler_params=_cp((ARB, ARB, ARB)),
    )(q, k, v)


def _k3_bwd(q, k, v, o, lse, dy, after):
    b, l, _ = q.shape

    def body(q_ref, k_ref, v_ref, o_ref, lse_ref, dy_ref, after_ref, dq_ref, dk_ref, dv_ref):
        t0 = pl.program_id(2) == 0
        kv_, vv = k_ref[...], v_ref[...]
        qv, dyv = q_ref[...], dy_ref[...]
        g = dyv.astype(BF)
        lse_col = jnp.max(lse_ref[...], axis=-1, keepdims=True)
        delta = jnp.sum(dyv * o_ref[...], axis=-1, keepdims=True)
        p = jnp.exp(_dot(qv, kv_, 1, 1) * SM_SCALE - lse_col)
        ds = (p * (_dot(g, vv, 1, 1) - delta) * SM_SCALE).astype(BF)
        _acc(dv_ref, _dot(p, g, 0, 0), t0)
        dq_ref[...] = _dot(ds, kv_, 1, 0)
        _acc(dk_ref, _dot(ds, qv, 0, 0), t0)

    qs, ks = _k3_specs()
    return pl.pallas_call(
        body, name="k3_bwd", grid=(b, N_HEADS, l // TQ),
        in_specs=[qs(QK_W), ks(QK_W), ks(LANE), qs(LANE), qs(LANE), qs(LANE), ANY],
        out_specs=[qs(QK_W), ks(QK_W), ks(LANE)],
        out_shape=[_sds((b, l, N_HEADS * QK_W)), _sds((b, KV_LEN, N_HEADS * QK_W)), _sds((b, KV_LEN, N_HEADS * LANE))],
        compiler_params=_cp((ARB, ARB, ARB)),
    )(q, k, v, o, lse, dy, after)


def _mod_rows(mod_ref, rows):
    return [mod_ref[r:r + 1, :] for r in rows]


def _k4a_fwd(x, o_f, o_b, rg, y_mla, g_ret, w_out, mod, g_ffn):
    b, l, _ = x.shape

    def body(x_ref, of_ref, ob_ref, rg_ref, ym_ref, gr_ref, wo_ref, mod_ref, gf_ref, xm_ref, h2_ref):
        gt_a, sh_f, sc_f = _mod_rows(mod_ref, (2, 3, 4))
        x_mid, h2 = k4a_tile(x_ref[...], of_ref[...], ob_ref[...], rg_ref[...], ym_ref[...], gr_ref[...], gt_a,
                             gf_ref[...], sh_f, sc_f, wo_ref[...], None)
        xm_ref[...] = x_mid
        h2_ref[...] = h2.astype(BF)

    tok = lambda w: pl.BlockSpec((None, TOK, w), lambda i, t: (i, t, 0))
    mod_spec = pl.BlockSpec((None, 8, D_MODEL), lambda i, t: (i, 0, 0))
    return pl.pallas_call(
        body, name="k4a_fwd", grid=(b, l // TOK),
        in_specs=[tok(D_MODEL), tok(512), tok(512), tok(512), tok(512), _full((1, 512)), _full((D_MODEL, D_MODEL)),
                  mod_spec, _full((1, D_MODEL))],
        out_specs=[tok(D_MODEL), tok(D_MODEL)], out_shape=[_sds((b, l, D_MODEL)), _sds((b, l, D_MODEL), BF)],
        compiler_params=_cp((ARB, ARB)),
    )(x, o_f, o_b, rg, y_mla, g_ret, w_out, mod, g_ffn)


TOK_M = 512
TOK_D = 1024
HALF_FF = D_FF // 2


def _k4b_mlp_loss(h2, w1t, w2, x_mid, mod, g_final, tgt):
    b, l, _ = h2.shape
    nt = l // TOK_M

    def body(h2_ref, w1_hbm, w2_hbm, xm_ref, mod_ref, gfin_ref, tgt_ref, dxm_ref, dmlp_ref, r_ref, loss_ref, dgt_ref,
             dgfin_ref, w1_v, w2_v):
        i, t = pl.program_id(0), pl.program_id(1)
        first = jnp.logical_and(i == 0, t == 0)

        @pl.when(first)
        def _():
            pltpu.sync_copy(w1_hbm, w1_v)
            pltpu.sync_copy(w2_hbm, w2_v)

        h2v = h2_ref[...]
        mlp = None
        for half in range(2):
            rows = slice(half * HALF_FF, (half + 1) * HALF_FF)
            r = jnp.maximum(_dot(h2v, w1_v[rows, :], 1, 1), 0.0)
            r_ref[:, rows] = r.astype(BF)
            part = _dot(jnp.square(r), w2_v[rows, :], 1, 0)
            mlp = part if mlp is None else mlp + part
        (gt_f,) = _mod_rows(mod_ref, (5,))
        loss, vjp = jax.vjp(k4c_tile, xm_ref[...], mlp, gt_f, gfin_ref[...], tgt_ref[...])
        dxm, dmlp, dgt, dgfin, _ = vjp(jnp.ones((1, 1), F32))
        dxm_ref[...] = dxm
        dmlp_ref[...] = dmlp.astype(BF)
        _acc(loss_ref, jnp.broadcast_to(loss, (8, LANE)), first)
        _acc(dgfin_ref, dgfin, first)
        _acc(dgt_ref, dgt, t == 0)

    tok = lambda w: pl.BlockSpec((None, TOK_M, w), lambda i, t: (i, t, 0))
    return pl.pallas_call(
        body, name="k4b_mlp_loss", grid=(b, nt),
        in_specs=[tok(D_MODEL), ANY, ANY, tok(D_MODEL), pl.BlockSpec((None, 8, D_MODEL), lambda i, t: (i, 0, 0)),
                  _full((1, D_MODEL)), tok(D_MODEL)],
        out_specs=[tok(D_MODEL), tok(D_MODEL), tok(D_FF), _full((8, LANE)),
                   pl.BlockSpec((None, 1, D_MODEL), lambda i, t: (i, 0, 0)), _full((1, D_MODEL))],
        out_shape=[_sds((b, l, D_MODEL)), _sds((b, l, D_MODEL), BF), _sds((b, l, D_FF), BF), _sds((8, LANE)),
                   _sds((b, 1, D_MODEL)), _sds((1, D_MODEL))],
        scratch_shapes=[pltpu.VMEM((D_FF, D_MODEL), BF), pltpu.VMEM((D_FF, D_MODEL), BF)],
        compiler_params=_cp((ARB, ARB)),
    )(h2, w1t, w2, x_mid, mod, g_final, tgt)


def _k4d_mlp_bwd(h2, dmlp, r, w2):
    b, l, _ = h2.shape
    nt = l // TOK_D

    def body(h2_ref, dm_ref, r_ref, w2_ref, da_ref, dw1_ref, dw2_ref, acc1, acc2):
        i, t = pl.program_id(1), pl.program_id(2)
        first = jnp.logical_and(i == 0, t == 0)
        rv = r_ref[...].astype(F32)
        dm = dm_ref[...]
        da = (_dot(dm, w2_ref[...], 1, 1) * (2.0 * rv)).astype(BF)
        da_ref[...] = da
        _acc(acc2, _dot(jnp.square(rv), dm, 0, 0), first)
        _acc(acc1, _dot(h2_ref[...], da, 0, 0), first)

        @pl.when(jnp.logical_and(i == b - 1, t == nt - 1))
        def _():
            dw1_ref[...] = acc1[...].astype(BF)
            dw2_ref[...] = acc2[...].astype(BF)

    tok = lambda w: pl.BlockSpec((None, TOK_D, w), lambda j, i, t: (i, t, 0))
    col = pl.BlockSpec((None, TOK_D, FF_BLK), lambda j, i, t: (i, t, j))
    return pl.pallas_call(
        body, name="k4d_mlp_bwd", grid=(N_DEV, b, nt),
        in_specs=[tok(D_MODEL), tok(D_MODEL), col, pl.BlockSpec((None, FF_BLK, D_MODEL), lambda j, i, t: (j, 0, 0))],
        out_specs=[col, pl.BlockSpec((None, D_MODEL, FF_BLK), lambda j, i, t: (j, 0, 0)),
                   pl.BlockSpec((None, FF_BLK, D_MODEL), lambda j, i, t: (j, 0, 0))],
        out_shape=[_sds((b, l, D_FF), BF), _sds((N_DEV, D_MODEL, FF_BLK), BF), _sds((N_DEV, FF_BLK, D_MODEL), BF)],
        scratch_shapes=[pltpu.VMEM((D_MODEL, FF_BLK), F32), pltpu.VMEM((FF_BLK, D_MODEL), F32)],
        compiler_params=_cp((ARB, ARB, ARB)),
    )(h2, dmlp, r, w2)


def _k4f_dh2(da, w1t, after):
    b, l, _ = da.shape

    def body(da_ref, w_ref, after_ref, o_ref):
        o_ref[...] = _dot(da_ref[...], w_ref[...], 1, 0)

    return pl.pallas_call(
        body, name="k4f_dh2", grid=(b, l // TOK_M),
        in_specs=[pl.BlockSpec((None, TOK_M, D_FF), lambda i, t: (i, t, 0)), _full((D_FF, D_MODEL)), ANY],
        out_specs=pl.BlockSpec((None, TOK_M, D_MODEL), lambda i, t: (i, t, 0)), out_shape=_sds((b, l, D_MODEL)),
        compiler_params=_cp((ARB, ARB)),
    )(da, w1t, after)


def _k4e_bwd(x, o_f, o_b, rg, y_mla, g_ret, w_out, mod, g_ffn, dxm, dh2):
    b, l, _ = x.shape

    def body(x_ref, of_ref, ob_ref, rg_ref, ym_ref, gr_ref, wo_ref, mod_ref, gf_ref, dxm_ref, dh2_ref,
             dx_ref, do_ref, drg_ref, dym_ref, dwo_ref, dgr_ref, dgf_ref, dmod_ref):
        i, t = pl.program_id(0), pl.program_id(1)
        first = jnp.logical_and(i == 0, t == 0)
        gt_a, sh_f, sc_f = _mod_rows(mod_ref, (2, 3, 4))
        wo = wo_ref[...]

        def f(xv, ofv, rgv, ymv, grv, gta, gfv, shf, scf, p_out):
            return k4a_tile(xv, ofv, ob_ref[...], rgv, ymv, grv, gta, gfv, shf, scf, wo, p_out)

        _, vjp = jax.vjp(f, x_ref[...], of_ref[...], rg_ref[...], ym_ref[...], gr_ref[...], gt_a, gf_ref[...], sh_f,
                         sc_f, jnp.zeros((D_MODEL, D_MODEL), F32))
        dx, do, drg, dym, dgr, dgta, dgf, dshf, dscf, dwo = vjp((dxm_ref[...], dh2_ref[...]))
        dx_ref[...] = dx
        do_ref[...] = do
        drg_ref[...] = drg
        dym_ref[...] = dym
        _acc(dwo_ref, dwo, first)
        _acc(dgr_ref, dgr, first)
        _acc(dgf_ref, dgf, first)
        t0 = t == 0
        _acc(dmod_ref.at[2:3, :], dgta, t0)
        _acc(dmod_ref.at[3:4, :], dshf, t0)
        _acc(dmod_ref.at[4:5, :], dscf, t0)

        @pl.when(t0)
        def _():
            dmod_ref[0:2, :] = jnp.zeros((2, D_MODEL), F32)
            dmod_ref[5:8, :] = jnp.zeros((3, D_MODEL), F32)

    tok = lambda w: pl.BlockSpec((None, TOK_B, w), lambda i, t: (i, t, 0))
    mod_spec = pl.BlockSpec((None, 8, D_MODEL), lambda i, t: (i, 0, 0))
    return pl.pallas_call(
        body, name="k4e_bwd", grid=(b, l // TOK_B),
        in_specs=[tok(D_MODEL), tok(512), tok(512), tok(512), tok(512), _full((1, 512)), _full((D_MODEL, D_MODEL)),
                  mod_spec, _full((1, D_MODEL)), tok(D_MODEL), tok(D_MODEL)],
        out_specs=[tok(D_MODEL), tok(512), tok(512), tok(512), _full((D_MODEL, D_MODEL)), _full((1, 512)),
                   _full((1, D_MODEL)), mod_spec],
        out_shape=[_sds((b, l, D_MODEL)), _sds((b, l, 512)), _sds((b, l, 512)), _sds((b, l, 512)),
                   _sds((D_MODEL, D_MODEL)), _sds((1, 512)), _sds((1, D_MODEL)), _sds((b, 8, D_MODEL))],
        compiler_params=_cp((ARB, ARB)),
    )(x, o_f, o_b, rg, y_mla, g_ret, w_out, mod, g_ffn, dxm, dh2)


def _adamw(w, m, v, pieces, name, after=None):
    r, c = w.shape
    npc = pieces.shape[0]
    per_row = c * (7 * 4 + npc * pieces.dtype.itemsize) * 2
    rb = r
    for cand in (r, 512, 256, 128, 64, 32, 16, 8):
        if r % cand == 0 and cand * per_row <= 32 * 1024 * 1024:
            rb = cand
            break

    def body(w_ref, m_ref, v_ref, p_ref, *rest):
        g_ref, d_ref, nm_ref, nv_ref = rest[-4:]
        g = p_ref[0].astype(F32)
        for k in range(1, npc):
            g = g + p_ref[k].astype(F32)
        wv = w_ref[...]
        mn = ADAM_B1 * m_ref[...] + (1.0 - ADAM_B1) * g
        vn = ADAM_B2 * v_ref[...] + (1.0 - ADAM_B2) * jnp.square(g)
        m_hat = mn / (1.0 - ADAM_B1 ** ADAM_STEP)
        v_hat = vn / (1.0 - ADAM_B2 ** ADAM_STEP)
        g_ref[...] = g
        d_ref[...] = -ADAM_LR * (m_hat / (jnp.sqrt(v_hat) + ADAM_EPS) + ADAM_WD * wv)
        nm_ref[...] = mn
        nv_ref[...] = vn

    blk = pl.BlockSpec((rb, c), lambda i: (i, 0))
    extra = [] if after is None else [after]
    return pl.pallas_call(
        body, name=name, grid=(r // rb,),
        in_specs=[blk, blk, blk, pl.BlockSpec((npc, rb, c), lambda i: (0, i, 0))] + [ANY] * len(extra),
        out_specs=[blk] * 4, out_shape=[_sds((r, c))] * 4, compiler_params=_cp((ARB,)),
    )(w, m, v, pieces, *extra)


def _pad_head_rows(w, d):
    k = w.shape[1]
    return jnp.pad(w.reshape(N_HEADS, d, k), ((0, 0), (0, LANE - d), (0, 0))).reshape(N_HEADS * LANE, k)


def _cut_head_rows(g, d):
    k = g.shape[1]
    return g.reshape(N_HEADS, LANE, k)[:, :d].reshape(N_HEADS * d, k)


def _w_in_pad(wt):
    w_a = jnp.concatenate([_pad_head_rows(wt[0:256], 64), _pad_head_rows(wt[256:512], 64), wt[512:1536]], axis=0)
    w_b = jnp.concatenate([wt[1536:2176], jnp.pad(wt[2176:2240], ((0, 64), (0, 0)))], axis=0)
    return w_a, w_b


def _w_in_cut(g_a, g_b):
    return jnp.concatenate([_cut_head_rows(g_a[0:512], 64), _cut_head_rows(g_a[512:1024], 64), g_a[1024:2048],
                            g_b[0:704]], axis=0)


def _w_uq_pad(wt):
    return jnp.pad(wt.reshape(N_HEADS, 192, 384), ((0, 0), (0, 64), (0, 0))).reshape(1024, 384)


def _w_uq_cut(g):
    return g.reshape(N_HEADS, 256, 384)[:, :192].reshape(768, 384)


def _w_ukv_perm(wt):
    return jnp.transpose(wt.reshape(N_HEADS, 2, LANE, 256), (1, 0, 2, 3)).reshape(1024, 256)


def _w_ukv_unperm(g):
    return jnp.transpose(g.reshape(2, N_HEADS, LANE, 256), (1, 0, 2, 3)).reshape(1024, 256)


def _unshard_cols(g):
    return jnp.transpose(g, (1, 0, 2)).reshape(g.shape[1], N_DEV * g.shape[2])


def _rope_tables():
    rows = SEQ // GRID_W
    row = jnp.repeat(jnp.arange(rows, dtype=F32), GRID_W)
    col = jnp.tile(jnp.arange(GRID_W, dtype=F32), rows)
    freq = ROPE_BASE ** (-jnp.arange(16, dtype=F32) / 16)
    ang = jnp.concatenate([row[:, None] * freq, col[:, None] * freq], axis=-1)
    cos, sin = jnp.cos(ang), jnp.sin(ang)
    z = jnp.zeros((SEQ, 64), F32)
    return jnp.concatenate([cos, cos, z], axis=1), jnp.concatenate([-sin, sin, z], axis=1)


_PACKED = (("g_attn", 1024), ("g_ffn", 1024), ("ret_decay_fwd", 4), ("ret_decay_bwd", 4), ("g_ret", 512),
           ("g_q_lora", 384), ("g_kv_lora", 256), ("g_final", 1024))
_PACK_OFF = {}
_off = 0
for _name, _n in _PACKED:
    _PACK_OFF[_name] = _off
    _off += -(-_n // LANE) * LANE
PACK_W = _off


def _pack_small(vals):
    parts = []
    for name, n in _PACKED:
        a = vals[name].reshape(-1).astype(F32)
        parts.append(jnp.pad(a, (0, -(-n // LANE) * LANE - n)))
    return jnp.concatenate(parts).reshape(1, PACK_W)


def _adamw_small(params, packed, gcc, gb_ada):
    names = list(params)
    n_p = len(names)

    def body(*refs):
        p_ref, gcc_ref, gb_ref = refs[3 * n_p:3 * n_p + 3]
        outs = refs[3 * n_p + 3:]
        for k, name in enumerate(names):
            w_ref, m_ref, v_ref = refs[3 * k:3 * k + 3]
            n = w_ref.shape[1]
            if name == "b_ada":
                g = jnp.concatenate([gb_ref[d, 0:1, :] for d in range(N_DEV)], axis=-1)
            elif name == "c_ctx":
                g = gcc_ref[0, 0:1, :]
                for d in range(1, N_DEV):
                    g = g + gcc_ref[d, 0:1, :]
            else:
                off = _PACK_OFF[name]
                g = p_ref[0, :, off:off + n]
                for d in range(1, N_DEV):
                    g = g + p_ref[d, :, off:off + n]
            mn = ADAM_B1 * m_ref[...] + (1.0 - ADAM_B1) * g
            vn = ADAM_B2 * v_ref[...] + (1.0 - ADAM_B2) * jnp.square(g)
            m_hat = mn / (1.0 - ADAM_B1 ** ADAM_STEP)
            v_hat = vn / (1.0 - ADAM_B2 ** ADAM_STEP)
            outs[4 * k][...] = g
            outs[4 * k + 1][...] = -ADAM_LR * (m_hat / (jnp.sqrt(v_hat) + ADAM_EPS) + ADAM_WD * w_ref[...])
            outs[4 * k + 2][...] = mn
            outs[4 * k + 3][...] = vn

    args = [a for name in names for a in params[name]] + [packed, gcc, gb_ada]
    out_shape = [_sds(params[name][0].shape) for name in names for _ in range(4)]
    outs = pl.pallas_call(body, name="adamw_small", out_shape=out_shape, compiler_params=_cp())(*args)
    return {name: list(outs[4 * k:4 * k + 4]) for k, name in enumerate(names)}


def kernel(x, c, ctx, c_ctx, w_ada, b_ada, g_attn, g_ffn, w_in, ret_decay_fwd, ret_decay_bwd, g_ret, g_q_lora, w_uq, g_kv_lora, w_ukv, w_out, w_ff1, w_ff2, g_final, loss_target, m_c_ctx, m_w_ada, m_b_ada, m_g_attn, m_g_ffn, m_w_in, m_ret_decay_fwd, m_ret_decay_bwd, m_g_ret, m_g_q_lora, m_w_uq, m_g_kv_lora, m_w_ukv, m_w_out, m_w_ff1, m_w_ff2, m_g_final, v_c_ctx, v_w_ada, v_b_ada, v_g_attn, v_g_ffn, v_w_in, v_ret_decay_fwd, v_ret_decay_bwd, v_g_ret, v_g_q_lora, v_w_uq, v_g_kv_lora, v_w_ukv, v_w_out, v_w_ff1, v_w_ff2, v_g_final):
    me = 4 * lax.axis_index("x") + 2 * lax.axis_index("y") + lax.axis_index("c")
    nb = x.shape[0]

    c_pad = jnp.pad(c, ((0, 8 - nb), (0, 0)))
    c_all, g_in, g_uq, g_ukv = _gather_two_level(
        [c_pad, w_in[0].T.astype(BF), w_uq[0].T.astype(BF), w_ukv[0].T.astype(BF)], "gather_weights")
    ws = (*_w_in_pad(g_in.reshape(2240, D_MODEL)), _w_uq_pad(g_uq.reshape(768, 384)),
          _w_ukv_perm(g_ukv.reshape(1024, 256)))

    crows = jnp.concatenate([c_all[:, :nb].reshape(N_DEV * nb, D_MODEL), c_ctx[None], jnp.zeros((7, D_MODEL), F32)])
    b_blk = lax.dynamic_slice(b_ada, (0, me * 768), (1, 768))
    (mod_g,) = _exchange([_mod_fwd(crows, w_ada[0], b_blk)], True, "gather_mod")
    mod_all = _unshard_cols(mod_g)
    behind = mod_g[0, 0, 0:1] * 0.0
    st_o = _exchange_start([(w_out[0] + behind).astype(BF)], True, "gather_wo_start")
    st_g = _exchange_start([w_ff1[0].T.astype(BF), w_ff2[0].astype(BF)], True, "gather_ff_start", after=st_o["token"])
    mod_all = mod_all + st_g["token"][0:1, 0:1]
    mod_mine = lax.dynamic_slice(mod_all, (me * nb, 0), (nb, 6 * D_MODEL)).reshape(nb, 6, D_MODEL)
    mod = jnp.pad(mod_mine, ((0, 0), (0, 2), (0, 0)))
    mod_c = jnp.pad(mod_all[16].reshape(1, 6, D_MODEL), ((0, 0), (0, 2), (0, 0)))

    tabs = _rope_tables()
    dec_f = ret_decay_fwd.reshape(N_HEADS, 1, 1)
    dec_b = ret_decay_bwd.reshape(N_HEADS, 1, 1)

    rkc, rvc, k_ctx, v_ctx = _k1_fwd(ctx, mod_c, g_attn, g_q_lora, g_kv_lora, ws, tabs, None, True)
    rq, rk, rv, rg, q, k_all, v_all = _k1_fwd(x, mod, g_attn, g_q_lora, g_kv_lora, ws, tabs, (k_ctx, v_ctx), False)
    o_f, o_b, sf_prev, sb_prev = _k2_fwd(rq, rk, rv, rkc, rvc, dec_f, dec_b)
    y_mla, lse = _k3_fwd(q, k_all, v_all)
    (g_out,) = _exchange_wait(st_o, y_mla, "gather_wo_wait")
    wo = g_out.reshape(D_MODEL, D_MODEL)
    x_mid, h2 = _k4a_fwd(x, o_f, o_b, rg, y_mla, g_ret, wo, mod, g_ffn)
    g_ff1t, g_ff2 = _exchange_wait(st_g, x_mid, "gather_ff_wait")
    w1t = g_ff1t.reshape(D_FF, D_MODEL)
    dxm, dmlp, relu_a, loss_acc, dgt_f, dg_final = _k4b_mlp_loss(h2, w1t, g_ff2.reshape(D_FF, D_MODEL), x_mid, mod,
                                                                 g_final.reshape(1, D_MODEL), loss_target)

    da, dw1, dw2 = _k4d_mlp_bwd(h2, dmlp, relu_a, g_ff2)
    st_s = _exchange_start([dw1, dw2], False, "scatter_ff_start")
    dh2 = _k4f_dh2(da, w1t, st_s["token"])
    g_ret_t = g_ret + st_s["token"][0:1, 0:1]
    dx_res, do, drg, dym, dwo, dg_ret, dg_ffn, dmod_a = _k4e_bwd(x, o_f, o_b, rg, y_mla, g_ret_t, wo, mod, g_ffn, dxm, dh2)
    st_w = _exchange_start([dwo.reshape(N_DEV, 128, D_MODEL).astype(BF)], False, "scatter_wo_start")
    dq, dk_all, dv_all = _k3_bwd(q, k_all, v_all, y_mla, lse, dym, st_w["token"])
    dqf, dkf, dvf, dqb, dkb, dvb, dkc, dvc, ddf, ddb = _k2_bwd(rq, rk, rv, do, sf_prev, sb_prev, rkc, rvc, dec_f, dec_b)
    cts = [[(dqf, 0), (dqb, 0)], [(dkf, 0), (dkb, 0)], [(dvf, 0), (dvb, 0)], [(drg, 0)], [(dq, 0)],
           [(dk_all, CTX_LEN)], [(dv_all, CTX_LEN)]]
    grad_x, accs, dmod_1 = _k1_bwd(x, mod, g_attn, g_q_lora, g_kv_lora, ws, tabs, cts, dx_res, None, False)
    cts_c = [[(dkc, 0)], [(dvc, 0)], [(dk_all, 0)], [(dv_all, 0)]]
    _, accs, dmod_c1 = _k1_bwd(ctx, mod_c, g_attn, g_q_lora, g_kv_lora, ws, tabs, cts_c, None, accs, True)
    dwa, dwb, dwq, dwk, dg_attn, dg_q, dg_kv = accs

    dmod_loc = (dmod_a + dmod_1).at[:, 5, :].set(dgt_f[:, 0, :])[:, :6, :].reshape(nb, 6 * D_MODEL)
    dmod_ctx = dmod_c1[:, :6, :].reshape(1, 6 * D_MODEL)
    small = {"g_attn": dg_attn, "g_ffn": dg_ffn, "ret_decay_fwd": jnp.sum(ddf[:, :, 0, 0], axis=0),
             "ret_decay_bwd": jnp.sum(ddb[:, :, 0, 0], axis=0), "g_ret": dg_ret, "g_q_lora": dg_q, "g_kv_lora": dg_kv,
             "g_final": dg_final}
    extra = jnp.concatenate([dmod_loc, dmod_ctx, jnp.zeros((5, 6 * D_MODEL), F32)])
    ex_pieces = jnp.transpose(extra.reshape(8, N_DEV, 768), (1, 0, 2))
    sm_g, ex_g, loss_g = _exchange([_pack_small(small), ex_pieces, loss_acc], [True, False, True], "gather_small")
    dmod_blk = jnp.concatenate([ex_g[:, :nb].reshape(N_DEV * nb, 768), jnp.zeros((8, 768), F32)])
    gw_ada, gcc_part, gb_part = _mod_bwd(crows, w_ada[0], dmod_blk, ex_g[:, nb])
    gcc_g, gb_g = _exchange([gcc_part, gb_part], True, "gather_c_ctx")

    p_ff1, p_ff2 = _exchange_wait(st_s, gcc_g, "scatter_ff_wait")
    (p_wo,) = _exchange_wait(st_w, p_ff1, "scatter_wo_wait")
    st_r = _exchange_start([_w_in_cut(dwa, dwb).astype(BF).reshape(N_DEV, 280, D_MODEL),
                            _w_uq_cut(dwq).astype(BF).reshape(N_DEV, 96, 384),
                            _w_ukv_unperm(dwk).astype(BF).reshape(N_DEV, 128, 256)], False, "scatter_rest_start",
                           after=p_wo)

    res = {}
    early = (("w_ff1", w_ff1, m_w_ff1, v_w_ff1, p_ff1), ("w_ff2", w_ff2, m_w_ff2, v_w_ff2, p_ff2),
             ("w_ada", w_ada, m_w_ada, v_w_ada, gw_ada[None]))
    for name, w, m, v, pcs in early:
        res[name] = [a[None] for a in _adamw(w[0], m[0], v[0], pcs, "adamw_" + name, after=st_r["token"])]
    pieces = _exchange_wait(st_r, res["w_ada"][3], "scatter_rest_wait")
    for name, w, m, v, pcs in (("w_in", w_in, m_w_in, v_w_in, pieces[0]), ("w_uq", w_uq, m_w_uq, v_w_uq, pieces[1])):
        res[name] = [a.T[None] for a in _adamw(w[0].T, m[0].T, v[0].T, pcs, "adamw_" + name)]
    late = (("w_ukv", w_ukv, m_w_ukv, v_w_ukv, jnp.transpose(pieces[2], (0, 2, 1))),
            ("w_out", w_out, m_w_out, v_w_out, p_wo))
    for name, w, m, v, pcs in late:
        res[name] = [a[None] for a in _adamw(w[0], m[0], v[0], pcs, "adamw_" + name)]

    smalls = {"c_ctx": (c_ctx, m_c_ctx, v_c_ctx), "b_ada": (b_ada, m_b_ada, v_b_ada), "g_attn": (g_attn, m_g_attn, v_g_attn),
              "g_ffn": (g_ffn, m_g_ffn, v_g_ffn), "ret_decay_fwd": (ret_decay_fwd, m_ret_decay_fwd, v_ret_decay_fwd),
              "ret_decay_bwd": (ret_decay_bwd, m_ret_decay_bwd, v_ret_decay_bwd), "g_ret": (g_ret, m_g_ret, v_g_ret),
              "g_q_lora": (g_q_lora, m_g_q_lora, v_g_q_lora), "g_kv_lora": (g_kv_lora, m_g_kv_lora, v_g_kv_lora),
              "g_final": (g_final, m_g_final, v_g_final)}
    rows = {k: tuple(a.reshape(1, -1) for a in t) for k, t in smalls.items()}
    for name, outs in _adamw_small(rows, sm_g, gcc_g, gb_g).items():
        res[name] = [o.reshape(smalls[name][0].shape) for o in outs]

    loss = loss_g[0, 0, 0]
    for k in range(1, N_DEV):
        loss = loss + loss_g[k, 0, 0]

    order = ("c_ctx", "w_ada", "b_ada", "g_attn", "g_ffn", "w_in", "ret_decay_fwd", "ret_decay_bwd", "g_ret", "g_q_lora",
             "w_uq", "g_kv_lora", "w_ukv", "w_out", "w_ff1", "w_ff2", "g_final")
    return (loss, grad_x, *[res[n][0] for n in order], *[res[n][1] for n in order], *[res[n][2] for n in order],
            *[res[n][3] for n in order])
```

```python
import functools
import math

import jax
import jax.numpy as jnp
from jax import lax
from jax.experimental import pallas as pl
from jax.experimental.pallas import tpu as pltpu

F32 = jnp.float32
BF = jnp.bfloat16
EPS = 1e-6
LANE = 128
N_DEV = 8
D_MODEL = 1024
SEQ = 2048
CTX_LEN = 256
GRID_W = 64
N_HEADS = 4
RET_CHUNK = 512
N_CHUNK = SEQ // RET_CHUNK
D_FF = 4096
FF_BLK = D_FF // N_DEV
IN_PAD = 2816
KV_LEN = CTX_LEN + SEQ
ROPE_BASE = 10000.0
ADAM_LR, ADAM_B1, ADAM_B2, ADAM_EPS, ADAM_WD, ADAM_STEP = 0.001, 0.9, 0.999, 1e-08, 0.01, 10
TOK = 256
TOK_B = 256
VMEM_LIMIT = 56 * 1024 * 1024
ARB = "arbitrary"
MESH = pl.DeviceIdType.MESH
_HEAD_SL = [slice(LANE * h, LANE * (h + 1)) for h in range(N_HEADS)]
W_SHAPES = [(2048, D_MODEL), (768, D_MODEL), (1024, 384), (1024, 256)]


def _dot(a, b, ca, cb):
    return lax.dot_general(a.astype(BF), b.astype(BF), (((ca,), (cb,)), ((), ())), preferred_element_type=F32)


@jax.custom_vjp
def mm(a, b):
    return _dot(a, b, 1, 0)


@jax.custom_vjp
def mm_nt(a, b):
    return _dot(a, b, 1, 1)


@jax.custom_vjp
def mm_tn(a, b):
    return _dot(a, b, 0, 0)


mm.defvjp(lambda a, b: (_dot(a, b, 1, 0), (a, b)), lambda r, g: (mm_nt(g, r[1]), mm_tn(r[0], g)))
mm_nt.defvjp(lambda a, b: (_dot(a, b, 1, 1), (a, b)), lambda r, g: (mm(g, r[1]), mm_tn(g, r[0])))
mm_tn.defvjp(lambda a, b: (_dot(a, b, 0, 0), (a, b)), lambda r, g: (mm_nt(r[1], g), mm(r[0], g)))


@jax.custom_vjp
def _mmw(a, w, probe):
    return _dot(a, w, 1, 0)


def _mmw_bwd(r, g):
    a, w = r
    return mm_nt(g, w), jnp.zeros_like(w), mm_tn(a, g)


_mmw.defvjp(lambda a, w, probe: (_dot(a, w, 1, 0), (a, w)), _mmw_bwd)


@jax.custom_vjp
def _mmwt(a, wt, probe):
    return _dot(a, wt, 1, 1)


_mmwt.defvjp(lambda a, wt, probe: (_dot(a, wt, 1, 1), (a, wt)),
             lambda r, g: (mm(g, r[1]), jnp.zeros_like(r[1]), mm_tn(g, r[0])))


def mmwt(a, wt, probe):
    return _dot(a, wt, 1, 1) if probe is None else _mmwt(a, wt, probe)


def mmw(a, w, probe):
    return _dot(a, w, 1, 0) if probe is None else _mmw(a, w, probe)


def rmsn(x, g):
    return x * lax.rsqrt(jnp.mean(x * x, axis=-1, keepdims=True) + EPS) * g


def silu(x):
    return x * jax.nn.sigmoid(x)


def _swap32_impl(x):
    n = x.shape[-1]
    lane = lax.broadcasted_iota(jnp.int32, x.shape, x.ndim - 1) % LANE
    up = pltpu.roll(x, n - 32, x.ndim - 1)
    dn = pltpu.roll(x, 32, x.ndim - 1)
    return jnp.where(lane < 32, up, jnp.where(lane < 64, dn, 0.0))


@jax.custom_vjp
def swap32(x):
    return _swap32_impl(x)


swap32.defvjp(lambda x: (_swap32_impl(x), None), lambda _, g: (_swap32_impl(g),))


def rope(x, cs, sn):
    return x * cs + swap32(x) * sn


def k1_tile(x, sh, sc, g_attn, g_q, g_kv, ws, ps, tabs, is_ctx):
    w_a, w_b, w_uq, w_ukv = ws
    p_a, p_b, p_uq, p_ukv = ps
    cs1, sn1 = tabs
    cs, sn = jnp.concatenate([cs1] * N_HEADS, axis=-1), jnp.concatenate([sn1] * N_HEADS, axis=-1)
    cq_t = jnp.concatenate([jnp.ones_like(cs1), cs1] * N_HEADS, axis=-1)
    sq_t = jnp.concatenate([jnp.zeros_like(sn1), sn1] * N_HEADS, axis=-1)
    h = rmsn(x, g_attn) * (1.0 + sc) + sh
    pa = mmwt(h, w_a, p_a)
    pb = mmwt(h, w_b, p_b)
    rk = pa[:, 512:1024] * 0.125
    rv = pa[:, 1024:1536]
    kpe = pb[:, 640:768]
    kv = mmwt(rmsn(pb[:, 384:640], g_kv), w_ukv, p_ukv)
    if not is_ctx:
        rk = rope(rk, cs, sn)
        kpe = rope(kpe, cs1, sn1)
    k_full = jnp.concatenate([piece for sl in _HEAD_SL for piece in (kv[:, sl], kpe)], axis=-1)
    v = kv[:, 512:]
    if is_ctx:
        return rk, rv, k_full, v
    rq = rope(pa[:, 0:512], cs, sn)
    rg = pa[:, 1536:2048]
    q = rope(mmwt(rmsn(pb[:, 0:384], g_q), w_uq, p_uq), cq_t, sq_t)
    return rq, rk, rv, rg, q, k_full, v


def log_sigmoid(x):
    return jnp.minimum(x, 0.0) - jnp.log(1.0 + jnp.exp(-jnp.abs(x)))


def ret_chunk(q, k, v, s, lg, reverse):
    c = RET_CHUNK
    ii = lax.broadcasted_iota(jnp.int32, (c, c), 0).astype(F32)
    jj = lax.broadcasted_iota(jnp.int32, (c, c), 1).astype(F32)
    diff = (jj - ii) if reverse else (ii - jj)
    dec = jnp.where(diff >= 0, jnp.exp(lg * jnp.maximum(diff, 0.0)), 0.0)
    pos = lax.broadcasted_iota(jnp.int32, (c, 1), 0).astype(F32)
    if reverse:
        wk, wq = jnp.exp(lg * pos), jnp.exp(lg * (c - pos))
    else:
        wk, wq = jnp.exp(lg * (c - 1.0 - pos)), jnp.exp(lg * (pos + 1.0))
    o = mm(mm_nt(q, k) * dec, v) + mm(q * wq, s)
    s_next = jnp.exp(lg * float(c)) * s + mm_tn(k * wk, v)
    return o, s_next


def ctx_state(kc, vc, lg, reverse):
    n = kc.shape[0]
    pos = lax.broadcasted_iota(jnp.int32, (n, 1), 0).astype(F32)
    w = jnp.exp(lg * pos) if reverse else jnp.exp(lg * (n - 1.0 - pos))
    return mm_tn(kc * w, vc)


def attn_head(qn, qp, kn, kp, v):
    s = (mm_nt(qn, kn) + mm_nt(qp, kp)) * (1.0 / math.sqrt(192.0))
    e = jnp.exp(s - jnp.max(s, axis=-1, keepdims=True))
    return mm(e / jnp.sum(e, axis=-1, keepdims=True), v)


def gn_gate(o, rg, g_ret):
    ys = []
    for h in range(N_HEADS):
        sl = slice(LANE * h, LANE * (h + 1))
        oh = o[:, sl]
        mu = jnp.mean(oh, axis=-1, keepdims=True)
        var = jnp.mean(jnp.square(oh - mu), axis=-1, keepdims=True)
        ys.append((oh - mu) * lax.rsqrt(var + EPS) * g_ret[:, sl])
    return jnp.concatenate(ys, axis=-1) * silu(rg)


def k4a_tile(x, o_f, o_b, rg, y_mla, g_ret, gt_a, g_ffn, sh_f, sc_f, w_out, p_out):
    mix = jnp.concatenate([gn_gate(o_f + o_b, rg, g_ret), y_mla], axis=-1)
    x_mid = x + gt_a * mmw(mix, w_out, p_out)
    h2 = rmsn(x_mid, g_ffn) * (1.0 + sc_f) + sh_f
    return x_mid, h2


def k4c_tile(x_mid, mlp, gt_f, g_final, tgt):
    y = rmsn(x_mid + gt_f * mlp, g_final)
    per_tok = jnp.mean(jnp.square(y - tgt), axis=-1, keepdims=True)
    return 0.5 * jnp.sum(per_tok, axis=0, keepdims=True)


def _cp(sem=None, vmem=VMEM_LIMIT):
    return pltpu.CompilerParams(dimension_semantics=sem, vmem_limit_bytes=vmem)


def _acc(ref, val, first):
    @pl.when(first)
    def _():
        ref[...] = val

    @pl.when(jnp.logical_not(first))
    def _():
        ref[...] += val


def _full(shape):
    nd = len(shape)
    return pl.BlockSpec(shape, lambda *_: (0,) * nd)


ANY = pl.BlockSpec(memory_space=pl.ANY)


def _sds(shape, dtype=F32):
    return jax.ShapeDtypeStruct(shape, dtype)


def _exchange(arrs, gather, name):
    n = len(arrs)
    modes = [gather] * n if isinstance(gather, bool) else list(gather)
    out_shape = [_sds(((N_DEV,) + a.shape) if g else a.shape, a.dtype) for a, g in zip(arrs, modes)]

    def body(*refs):
        ins, outs = refs[:n], refs[n:2 * n]
        send_sems, recv_sems, local_sems = refs[2 * n:]
        x, y, c = lax.axis_index("x"), lax.axis_index("y"), lax.axis_index("c")
        me = 4 * x + 2 * y + c
        sends, recvs, locs = [], [], []
        for i in range(n):
            gather = modes[i]
            for k in range(N_DEV - 1):
                bits = k + 1
                px = x ^ ((bits >> 2) & 1)
                py = y ^ ((bits >> 1) & 1)
                pc = c ^ (bits & 1)
                peer = 4 * px + 2 * py + pc
                src = ins[i] if gather else ins[i].at[peer]
                sem = i * (N_DEV - 1) + k
                sends.append(pltpu.make_async_remote_copy(
                    src_ref=src, dst_ref=outs[i].at[me], send_sem=send_sems.at[sem], recv_sem=recv_sems.at[sem],
                    device_id=(px, py, pc), device_id_type=MESH))
                recvs.append(pltpu.make_async_remote_copy(
                    src_ref=src, dst_ref=outs[i].at[peer], send_sem=send_sems.at[sem], recv_sem=recv_sems.at[sem],
                    device_id=(px, py, pc), device_id_type=MESH))
            locs.append(pltpu.make_async_copy(ins[i] if gather else ins[i].at[me], outs[i].at[me], local_sems.at[i]))
        for cp in locs + sends:
            cp.start()
        for cp in recvs:
            cp.wait_recv()
        for cp in sends:
            cp.wait_send()
        for cp in locs:
            cp.wait()

    outs = pl.pallas_call(
        body, name=name, out_shape=out_shape, in_specs=[ANY] * n, out_specs=[ANY] * n,
        scratch_shapes=[pltpu.SemaphoreType.DMA((n * (N_DEV - 1),)), pltpu.SemaphoreType.DMA((n * (N_DEV - 1),)),
                        pltpu.SemaphoreType.DMA((n,))],
    )(*arrs)
    return list(outs)


def _gather_two_level(arrs, name):
    n = len(arrs)

    def body(*refs):
        ins, outs = refs[:n], refs[n:2 * n]
        send_sems, recv_sems, local_sems = refs[2 * n:]
        x, y, c = lax.axis_index("x"), lax.axis_index("y"), lax.axis_index("c")
        sibling = (x, y, 1 - c)
        chips = [(1 - x, y), (x, 1 - y), (1 - x, 1 - y)]

        def slot(px, py, pc):
            return 4 * px + 2 * py + pc

        first, passed, waits, locs = [], [], [], []
        for i in range(n):
            def copy(k, block, to, src=None, i=i):
                dst = outs[i].at[slot(*block)]
                return pltpu.make_async_remote_copy(
                    src_ref=dst if src is None else src, dst_ref=dst, send_sem=send_sems.at[7 * i + k],
                    recv_sem=recv_sems.at[7 * i + k], device_id=to, device_id_type=MESH)

            locs.append(pltpu.make_async_copy(ins[i], outs[i].at[slot(x, y, c)], local_sems.at[i]))
            first.append(copy(0, (x, y, c), sibling, src=ins[i]))
            first += [copy(1 + j, (x, y, c), (*chip, c), src=ins[i]) for j, chip in enumerate(chips)]
            passed.append([copy(4 + j, (*chip, c), sibling) for j, chip in enumerate(chips)])
            waits.append([copy(1 + j, (*chip, c), (x, y, c)) for j, chip in enumerate(chips)])
        for cp in locs + first:
            cp.start()
        for j in range(3):
            for i in range(n):
                waits[i][j].wait_recv()
                passed[i][j].start()
        for i in range(n):
            def arrival(k, block, i=i):
                dst = outs[i].at[slot(*block)]
                return pltpu.make_async_remote_copy(
                    src_ref=dst, dst_ref=dst, send_sem=send_sems.at[7 * i + k], recv_sem=recv_sems.at[7 * i + k],
                    device_id=sibling, device_id_type=MESH)

            arrival(0, (x, y, 1 - c)).wait_recv()
            for j, chip in enumerate(chips):
                arrival(4 + j, (*chip, 1 - c)).wait_recv()
        for cp in first + [p for ps in passed for p in ps]:
            cp.wait_send()
        for cp in locs:
            cp.wait()

    outs = pl.pallas_call(
        body, name=name, out_shape=[_sds((N_DEV,) + a.shape, a.dtype) for a in arrs], in_specs=[ANY] * n,
        out_specs=[ANY] * n,
        scratch_shapes=[pltpu.SemaphoreType.DMA((7 * n,)), pltpu.SemaphoreType.DMA((7 * n,)),
                        pltpu.SemaphoreType.DMA((n,))],
    )(*arrs)
    return list(outs)


HBM = pl.BlockSpec(memory_space=pltpu.HBM)
SEM = pl.BlockSpec(memory_space=pltpu.SEMAPHORE)
EFFECT = pltpu.SideEffectType.DATAFLOW_SIDE_EFFECTING


def _peer(k):
    x, y, c = lax.axis_index("x"), lax.axis_index("y"), lax.axis_index("c")
    bits = k + 1
    px, py, pc = x ^ ((bits >> 2) & 1), y ^ ((bits >> 1) & 1), c ^ (bits & 1)
    return (px, py, pc), 4 * px + 2 * py + pc, 4 * x + 2 * y + c


def _exchange_start(arrs, gather, name, after=None):
    n = len(arrs)
    lands = [pltpu.with_memory_space_constraint(lax.empty(((N_DEV,) + a.shape) if gather else a.shape, a.dtype),
                                                pltpu.HBM) for a in arrs]
    srcs = [pltpu.with_memory_space_constraint(a, pltpu.HBM) for a in arrs]

    extra = [] if after is None else [after]

    def body(*refs):
        ins, zones = refs[:n], refs[n:2 * n]
        send_sems, recv_sems, local_sems = refs[2 * n + len(extra):2 * n + len(extra) + 3]
        token = refs[-1]
        for i in range(n):
            for k in range(N_DEV - 1):
                dev, peer, me = _peer(k)
                sem = i * (N_DEV - 1) + k
                pltpu.make_async_remote_copy(
                    src_ref=ins[i] if gather else ins[i].at[peer], dst_ref=zones[i].at[me],
                    send_sem=send_sems.at[sem], recv_sem=recv_sems.at[sem], device_id=dev, device_id_type=MESH).start()
            _, _, me = _peer(0)
            pltpu.make_async_copy(ins[i] if gather else ins[i].at[me], zones[i].at[me], local_sems.at[i]).start()
        token[...] = jnp.zeros_like(token)

    nsem = n * (N_DEV - 1)
    outs = pl.pallas_call(
        body, name=name,
        out_shape=[pltpu.SemaphoreType.DMA((nsem,)), pltpu.SemaphoreType.DMA((nsem,)), pltpu.SemaphoreType.DMA((n,))]
        + [pltpu.HBM(a.shape, a.dtype) for a in srcs] + [pltpu.HBM(z.shape, z.dtype) for z in lands]
        + [_sds((8, LANE))],
        in_specs=[HBM] * (2 * n) + [ANY] * len(extra),
        out_specs=[SEM, SEM, SEM] + [HBM] * (2 * n) + [pl.BlockSpec(memory_space=pltpu.VMEM)],
        input_output_aliases={i: 3 + i for i in range(2 * n)},
        compiler_params=pltpu.CompilerParams(has_side_effects=EFFECT),
    )(*srcs, *lands, *extra)
    return {"n": n, "gather": gather, "sems": outs[:3], "srcs": outs[3:3 + n], "lands": outs[3 + n:3 + 2 * n],
            "token": outs[-1]}


def _exchange_wait(st, after, name):
    n, gather = st["n"], st["gather"]

    def body(*refs):
        ins, zones = refs[:n], refs[n:2 * n]
        send_sems, recv_sems, local_sems = refs[2 * n:2 * n + 3]
        for i in range(n):
            for k in range(N_DEV - 1):
                dev, peer, me = _peer(k)
                sem = i * (N_DEV - 1) + k
                src = ins[i] if gather else ins[i].at[peer]
                cp = pltpu.make_async_remote_copy(
                    src_ref=src, dst_ref=zones[i].at[peer], send_sem=send_sems.at[sem], recv_sem=recv_sems.at[sem],
                    device_id=dev, device_id_type=MESH)
                cp.wait_send()
                cp.wait_recv()
            _, _, me = _peer(0)
            pltpu.make_async_copy(ins[i] if gather else ins[i].at[me], zones[i].at[me], local_sems.at[i]).wait()

    outs = pl.pallas_call(
        body, name=name,
        out_shape=[pltpu.HBM(a.shape, a.dtype) for a in st["srcs"]] + [pltpu.HBM(z.shape, z.dtype) for z in st["lands"]],
        in_specs=[HBM] * (2 * n) + [SEM, SEM, SEM, ANY], out_specs=[HBM] * (2 * n),
        input_output_aliases={i: i for i in range(2 * n)},
        compiler_params=pltpu.CompilerParams(has_side_effects=EFFECT),
    )(*st["srcs"], *st["lands"], *st["sems"], after)
    return list(outs[n:])


def _mod_fwd(crows, w_ada, b_blk):
    def body(c_ref, w_ref, b_ref, o_ref):
        o_ref[...] = mm(silu(c_ref[...]), w_ref[...]) + b_ref[...]

    return pl.pallas_call(body, name="mod_fwd", out_shape=_sds((24, 768)), compiler_params=_cp())(crows, w_ada, b_blk)


def _mod_bwd(crows, w_ada, dmod_blk, dmodc_blk):
    def body(c_ref, w_ref, d_ref, dc_ref, gw_ref, gc_ref, gb_ref):
        cr = c_ref[...]
        dc = dc_ref[0:1, :]
        for p in range(1, N_DEV):
            dc = dc + dc_ref[p:p + 1, :]
        row = lax.broadcasted_iota(jnp.int32, (24, 1), 0)
        gw_ref[...] = mm_tn(silu(cr), jnp.where(row == 16, dc, d_ref[...]))
        cc = cr[16:17, :]
        sg = jax.nn.sigmoid(cc)
        part = mm_nt(jnp.broadcast_to(dc, (8, 768)), w_ref[...])
        gc_ref[...] = part * (sg * (1.0 + cc * (1.0 - sg)))
        gb_ref[...] = jnp.broadcast_to(jnp.sum(d_ref[...], axis=0, keepdims=True) + dc, (8, 768))

    return pl.pallas_call(
        body, name="mod_bwd", out_shape=[_sds((D_MODEL, 768)), _sds((8, D_MODEL)), _sds((8, 768))],
        compiler_params=_cp())(crows, w_ada, dmod_blk, dmodc_blk)


def _tab_specs(tk):
    return [pl.BlockSpec((tk, LANE), lambda i, t: (t, 0))] * 2


def _k1_fwd(x, mod, g_attn, g_q, g_kv, ws, tabs, kv_all, is_ctx):
    b, l, _ = x.shape
    nt = l // TOK
    n_f32 = 2 if is_ctx else 4

    def body(x_ref, mod_ref, ga_ref, gq_ref, gk_ref, wa_ref, wb_ref, wq_ref, wk_ref, cs_ref, sn_ref, *rest):
        outs = rest if is_ctx else rest[2:]
        res = k1_tile(x_ref[...], mod_ref[0:1, :], mod_ref[1:2, :], ga_ref[...], gq_ref[...], gk_ref[...],
                      (wa_ref[...], wb_ref[...], wq_ref[...], wk_ref[...]), (None,) * 4,
                      (cs_ref[...], sn_ref[...]), is_ctx)
        for o_ref, r in zip(outs, res):
            o_ref[...] = r.astype(o_ref.dtype)

    tok = lambda w, off=0: pl.BlockSpec((None, TOK, w), lambda i, t: (i, t + off, 0))
    mod_spec = pl.BlockSpec((None, 8, D_MODEL), (lambda i, t: (0, 0, 0)) if is_ctx else (lambda i, t: (i, 0, 0)))
    kv_off = 0 if is_ctx else CTX_LEN // TOK
    in_specs = ([tok(D_MODEL), mod_spec, _full((1, D_MODEL)), _full((1, 384)), _full((1, 256))]
                + [_full(s) for s in W_SHAPES] + _tab_specs(TOK))
    args = [x, mod, g_attn, g_q, g_kv, *ws, *tabs]
    out_specs = [tok(512)] * n_f32 + ([] if is_ctx else [tok(1024)]) + [tok(1024, kv_off), tok(512, kv_off)]
    out_shape = ([_sds((b, l, 512))] * n_f32 + ([] if is_ctx else [_sds((b, l, 1024), BF)])
                 + [_sds((b, KV_LEN, 1024), BF), _sds((b, KV_LEN, 512), BF)])
    aliases = {}
    if not is_ctx:
        aliases = {len(args): n_f32 + 1, len(args) + 1: n_f32 + 2}
        in_specs += [ANY, ANY]
        args += list(kv_all)
    return pl.pallas_call(
        body, name="k1_fwd_ctx" if is_ctx else "k1_fwd", grid=(b, nt), in_specs=in_specs, out_specs=out_specs,
        out_shape=out_shape, input_output_aliases=aliases, compiler_params=_cp((ARB, ARB)),
    )(*args)


N_ACC = 7


def _k1_bwd(x, ctx, mod, mod_c, g_attn, g_q, g_kv, ws, tabs, cts, cts_c, dx_res):
    b, l, _ = x.shape
    tk = TOK_B
    nt = l // tk
    flat = [[a for group in c for a in group] for c in (cts, cts_c)]
    sizes = [[len(g) for g in c] for c in (cts, cts_c)]
    acc_shapes = W_SHAPES + [(1, D_MODEL), (1, 384), (1, 256)]

    def body(*refs):
        it = iter(refs)
        x_ref, c_ref, mod_ref, modc_ref, ga_ref, gq_ref, gk_ref = [next(it) for _ in range(7)]
        w_hbm = [next(it) for _ in range(4)]
        tab_refs = [next(it) for _ in range(2)]
        ct_refs = [[next(it) for _ in f] for f in flat]
        res_ref, gx_ref = next(it), next(it)
        out_hbm = [next(it) for _ in range(N_ACC)]
        dmod_ref, dmodc_ref = next(it), next(it)
        w_vmem = [next(it) for _ in range(4)]
        accs = [next(it) for _ in range(N_ACC)]
        sem = next(it)
        i, t = pl.program_id(0), pl.program_id(1)
        first = jnp.logical_and(i == 0, t == 0)

        @pl.when(first)
        def _():
            for src, dst in zip(w_hbm, w_vmem):
                pltpu.sync_copy(src, dst)
            for k in range(N_ACC):
                accs[k][...] = jnp.zeros(acc_shapes[k], F32)

        def tile(is_ctx):
            which = 1 if is_ctx else 0
            ct_vals, pos = [], 0
            for gsz in sizes[which]:
                v = ct_refs[which][pos][...].astype(F32)
                for r in ct_refs[which][pos + 1:pos + gsz]:
                    v = v + r[...]
                ct_vals.append(v)
                pos += gsz
            wv = tuple(r[...] for r in w_vmem)
            tv = tuple(r[...] for r in tab_refs)
            m_ref = modc_ref if is_ctx else mod_ref

            def f(xv, sh, sc, ga, gq, gk, *probes):
                return k1_tile(xv, sh, sc, ga, gq, gk, wv, probes, tv, is_ctx)

            probes = [jnp.zeros(s, F32) for s in W_SHAPES]
            xin = c_ref[...] if is_ctx else x_ref[...]
            _, vjp = jax.vjp(f, xin, m_ref[0:1, :], m_ref[1:2, :], ga_ref[...], gq_ref[...], gk_ref[...], *probes)
            dx, dsh, dsc, dga, dgq, dgk, dwa, dwb, dwq, dwk = vjp(tuple(ct_vals))
            for ref, val in zip(accs, (dwa, dwb, dwq, dwk, dga, dgq, dgk)):
                ref[...] += val
            return dx, dsh, dsc

        @pl.when(t == 0)
        def _():
            _, dsh, dsc = tile(True)
            _acc(dmodc_ref.at[0:1, :], dsh, i == 0)
            _acc(dmodc_ref.at[1:2, :], dsc, i == 0)

            @pl.when(i == 0)
            def _():
                dmodc_ref[2:8, :] = jnp.zeros((6, D_MODEL), F32)

        @pl.when(t > 0)
        def _():
            dx, dsh, dsc = tile(False)
            gx_ref[...] = dx + res_ref[...]
            _acc(dmod_ref.at[0:1, :], dsh, t == 1)
            _acc(dmod_ref.at[1:2, :], dsc, t == 1)

            @pl.when(t == 1)
            def _():
                dmod_ref[2:8, :] = jnp.zeros((6, D_MODEL), F32)

        @pl.when(jnp.logical_and(i == b - 1, t == nt))
        def _():
            for k in range(4):
                w_vmem[k][...] = accs[k][...].astype(BF)
            cps = [pltpu.make_async_copy(w_vmem[k] if k < 4 else accs[k], out_hbm[k], sem.at[k]) for k in range(N_ACC)]
            for cp in cps:
                cp.start()
            for cp in cps:
                cp.wait()

    lat = lambda w, off=0: pl.BlockSpec((None, tk, w), lambda i, t: (i, jnp.maximum(t - 1, 0) + off, 0))
    con = lambda w: pl.BlockSpec((None, tk, w), lambda i, t: (i, 0, 0))
    mod_spec = pl.BlockSpec((None, 8, D_MODEL), lambda i, t: (i, 0, 0))
    modc_spec = pl.BlockSpec((None, 8, D_MODEL), lambda i, t: (0, 0, 0))
    tab_spec = pl.BlockSpec((tk, LANE), lambda i, t: (jnp.maximum(t - 1, 0), 0))
    in_specs = ([lat(D_MODEL), con(D_MODEL), mod_spec, modc_spec, _full((1, D_MODEL)), _full((1, 384)), _full((1, 256))]
                + [ANY] * 4 + [tab_spec] * 2)
    args = [x, ctx, mod, mod_c, g_attn, g_q, g_kv, *ws, *tabs]
    for a, off in flat[0]:
        in_specs.append(lat(a.shape[-1], off // tk))
        args.append(a)
    for a, off in flat[1]:
        assert off == 0
        in_specs.append(con(a.shape[-1]))
        args.append(a)
    in_specs.append(lat(D_MODEL))
    args.append(dx_res)
    out_shape = ([_sds((b, l, D_MODEL))] + [_sds(s, BF) for s in W_SHAPES] + [_sds(s) for s in acc_shapes[4:]]
                 + [_sds((b, 8, D_MODEL)), _sds((1, 8, D_MODEL))])
    out_specs = [lat(D_MODEL)] + [ANY] * N_ACC + [mod_spec, modc_spec]
    outs = pl.pallas_call(
        body, name="k1_bwd", grid=(b, nt + 1), in_specs=in_specs, out_specs=out_specs, out_shape=out_shape,
        scratch_shapes=[pltpu.VMEM(s, BF) for s in W_SHAPES] + [pltpu.VMEM(s, F32) for s in acc_shapes]
        + [pltpu.SemaphoreType.DMA((N_ACC,))],
        compiler_params=_cp((ARB, ARB)),
    )(*args)
    return outs[0], list(outs[1:1 + N_ACC]), outs[1 + N_ACC], outs[2 + N_ACC]


def _chunk_spec(rev):
    if rev:
        return pl.BlockSpec((None, RET_CHUNK, 512), lambda i, n: (i, N_CHUNK - 1 - n, 0))
    return pl.BlockSpec((None, RET_CHUNK, 512), lambda i, n: (i, n, 0))


def _state_spec(rev):
    if rev:
        return pl.BlockSpec((None, N_HEADS, None, LANE, LANE), lambda i, n: (i, 0, N_CHUNK - 1 - n, 0, 0))
    return pl.BlockSpec((None, N_HEADS, None, LANE, LANE), lambda i, n: (i, 0, n, 0, 0))


_CTX_SPEC = pl.BlockSpec((None, CTX_LEN, 512), lambda i, n: (i, 0, 0))
_DEC_SPEC = pl.BlockSpec((N_HEADS, 1, 1), lambda i, n: (0, 0, 0))


def _k2_fwd(rq, rk, rv, rkc, rvc, dec_f, dec_b):
    b = rq.shape[0]

    def body(qf, kf, vf, qb, kb, vb, kc, vc, df, db, of_ref, ob_ref, sf_out, sb_out, sf, sb):
        n = pl.program_id(1)
        for h, sl in enumerate(_HEAD_SL):
            lgf, lgb = log_sigmoid(df[h]), log_sigmoid(db[h])

            @pl.when(n == 0)
            def _():
                sf[h] = ctx_state(kc[:, sl], vc[:, sl], lgf, False)
                sb[h] = ctx_state(kc[:, sl], vc[:, sl], lgb, True)

            sf_out[h] = sf[h]
            sb_out[h] = sb[h]
            o, s = ret_chunk(qf[:, sl], kf[:, sl], vf[:, sl], sf[h], lgf, False)
            of_ref[:, sl] = o
            sf[h] = s
            o, s = ret_chunk(qb[:, sl], kb[:, sl], vb[:, sl], sb[h], lgb, True)
            ob_ref[:, sl] = o
            sb[h] = s

    l = rq.shape[1]
    return pl.pallas_call(
        body, name="k2_fwd", grid=(b, N_CHUNK),
        in_specs=[_chunk_spec(False)] * 3 + [_chunk_spec(True)] * 3 + [_CTX_SPEC, _CTX_SPEC, _DEC_SPEC, _DEC_SPEC],
        out_specs=[_chunk_spec(False), _chunk_spec(True), _state_spec(False), _state_spec(True)],
        out_shape=[_sds((b, l, 512)), _sds((b, l, 512)), _sds((b, N_HEADS, N_CHUNK, LANE, LANE)),
                   _sds((b, N_HEADS, N_CHUNK, LANE, LANE))],
        scratch_shapes=[pltpu.VMEM((N_HEADS, LANE, LANE), F32), pltpu.VMEM((N_HEADS, LANE, LANE), F32)],
        compiler_params=_cp((ARB, ARB)),
    )(rq, rk, rv, rq, rk, rv, rkc, rvc, dec_f, dec_b)


def _k2_bwd(rq, rk, rv, do, sf_prev, sb_prev, rkc, rvc, dec_f, dec_b):
    b, l, _ = rq.shape

    def body(qf, kf, vf, gf, spf, qb, kb, vb, gb, spb, kc, vc, df, db,
             dqf, dkf, dvf, dqb, dkb, dvb, dkc, dvc, ddf, ddb, dsf, dsb):
        n = pl.program_id(1)

        @pl.when(n == 0)
        def _():
            dsf[...] = jnp.zeros((N_HEADS, LANE, LANE), F32)
            dsb[...] = jnp.zeros((N_HEADS, LANE, LANE), F32)

        def one(h, sl, q, k, v, g, sp, dec, ds, dq, dk, dv, dd, rev):
            def f(qv, kv_, vv, sv, dcy):
                return ret_chunk(qv, kv_, vv, sv, log_sigmoid(dcy), rev)

            _, vjp = jax.vjp(f, q[:, sl], k[:, sl], v[:, sl], sp[h], dec[h])
            gq, gk, gv, gs, gd = vjp((g[:, sl], ds[h]))
            dq[:, sl] = gq
            dk[:, sl] = gk
            dv[:, sl] = gv
            ds[h] = gs
            _acc(dd.at[h], jnp.broadcast_to(gd, (8, LANE)), n == 0)

        for h, sl in enumerate(_HEAD_SL):
            one(h, sl, qf, kf, vf, gf, spf, df, dsf, dqf, dkf, dvf, ddf, False)
            one(h, sl, qb, kb, vb, gb, spb, db, dsb, dqb, dkb, dvb, ddb, True)

        @pl.when(n == N_CHUNK - 1)
        def _():
            def f(kcv, vcv, dcy, rev):
                return ctx_state(kcv, vcv, log_sigmoid(dcy), rev)

            for h, sl in enumerate(_HEAD_SL):
                _, vjp_f = jax.vjp(functools.partial(f, rev=False), kc[:, sl], vc[:, sl], df[h])
                gk_f, gv_f, gd_f = vjp_f(dsf[h])
                _, vjp_b = jax.vjp(functools.partial(f, rev=True), kc[:, sl], vc[:, sl], db[h])
                gk_b, gv_b, gd_b = vjp_b(dsb[h])
                dkc[:, sl] = gk_f + gk_b
                dvc[:, sl] = gv_f + gv_b
                ddf[h] += jnp.broadcast_to(gd_f, (8, LANE))
                ddb[h] += jnp.broadcast_to(gd_b, (8, LANE))

    dd_spec = pl.BlockSpec((None, N_HEADS, 8, LANE), lambda i, n: (i, 0, 0, 0))
    return pl.pallas_call(
        body, name="k2_bwd", grid=(b, N_CHUNK),
        in_specs=[_chunk_spec(True)] * 4 + [_state_spec(True)] + [_chunk_spec(False)] * 4 + [_state_spec(False)]
        + [_CTX_SPEC, _CTX_SPEC, _DEC_SPEC, _DEC_SPEC],
        out_specs=[_chunk_spec(True)] * 3 + [_chunk_spec(False)] * 3 + [_CTX_SPEC, _CTX_SPEC, dd_spec, dd_spec],
        out_shape=[_sds((b, l, 512))] * 6 + [_sds((b, CTX_LEN, 512))] * 2 + [_sds((b, N_HEADS, 8, LANE))] * 2,
        scratch_shapes=[pltpu.VMEM((N_HEADS, LANE, LANE), F32), pltpu.VMEM((N_HEADS, LANE, LANE), F32)],
        compiler_params=_cp((ARB, ARB)),
    )(rq, rk, rv, do, sf_prev, rq, rk, rv, do, sb_prev, rkc, rvc, dec_f, dec_b)


TQ = 512
QK_W = 2 * LANE
N_QP = 2
_Q_PARTS = [slice(i * TQ // N_QP, (i + 1) * TQ // N_QP) for i in range(N_QP)]


SM_SCALE = 1.0 / math.sqrt(192.0)


def _k3_specs():
    qs = lambda w: pl.BlockSpec((None, TQ, w), lambda i, h, t: (i, t, h))
    ks = lambda w: pl.BlockSpec((None, KV_LEN, w), lambda i, h, t: (i, 0, h))
    return qs, ks


def _k3_fwd(q, k, v):
    b, l, _ = q.shape

    def body(q_ref, k_ref, v_ref, o_ref, lse_ref):
        kv_, vv = k_ref[...], v_ref[...]
        for r in _Q_PARTS:
            s = _dot(q_ref[r, :], kv_, 1, 1) * SM_SCALE
            m = jnp.max(s, axis=-1, keepdims=True)
            e = jnp.exp(s - m)
            tot = jnp.sum(e, axis=-1, keepdims=True)
            o_ref[r, :] = _dot(e, vv, 1, 0) * (1.0 / tot)
            lse_ref[r, :] = jnp.broadcast_to(m + jnp.log(tot), (TQ // N_QP, LANE))

    qs, ks = _k3_specs()
    return pl.pallas_call(
        body, name="k3_fwd", grid=(b, N_HEADS, l // TQ), in_specs=[qs(QK_W), ks(QK_W), ks(LANE)],
        out_specs=[qs(LANE), qs(LANE)], out_shape=[_sds((b, l, N_HEADS * LANE))] * 2,
        compiler_params=_cp((ARB, ARB, ARB)),
    )(q, k, v)


def _k3_bwd(q, k, v, o, lse, dy, after):
    b, l, _ = q.shape

    def body(q_ref, k_ref, v_ref, o_ref, lse_ref, dy_ref, after_ref, dq_ref, dk_ref, dv_ref):
        t0 = pl.program_id(2) == 0
        kv_, vv = k_ref[...], v_ref[...]
        qv, dyv = q_ref[...], dy_ref[...]
        g = dyv.astype(BF)
        lse_col = jnp.max(lse_ref[...], axis=-1, keepdims=True)
        delta = jnp.sum(dyv * o_ref[...], axis=-1, keepdims=True)
        p = jnp.exp(_dot(qv, kv_, 1, 1) * SM_SCALE - lse_col)
        ds = (p * (_dot(g, vv, 1, 1) - delta) * SM_SCALE).astype(BF)
        _acc(dv_ref, _dot(p, g, 0, 0), t0)
        dq_ref[...] = _dot(ds, kv_, 1, 0)
        _acc(dk_ref, _dot(ds, qv, 0, 0), t0)

    qs, ks = _k3_specs()
    return pl.pallas_call(
        body, name="k3_bwd", grid=(b, N_HEADS, l // TQ),
        in_specs=[qs(QK_W), ks(QK_W), ks(LANE), qs(LANE), qs(LANE), qs(LANE), ANY],
        out_specs=[qs(QK_W), ks(QK_W), ks(LANE)],
        out_shape=[_sds((b, l, N_HEADS * QK_W)), _sds((b, KV_LEN, N_HEADS * QK_W)), _sds((b, KV_LEN, N_HEADS * LANE))],
        compiler_params=_cp((ARB, ARB, ARB)),
    )(q, k, v, o, lse, dy, after)


def _mod_rows(mod_ref, rows):
    return [mod_ref[r:r + 1, :] for r in rows]


def _k4a_fwd(x, o_f, o_b, rg, y_mla, g_ret, w_out, mod, g_ffn):
    b, l, _ = x.shape

    def body(x_ref, of_ref, ob_ref, rg_ref, ym_ref, gr_ref, wo_ref, mod_ref, gf_ref, xm_ref, h2_ref):
        gt_a, sh_f, sc_f = _mod_rows(mod_ref, (2, 3, 4))
        x_mid, h2 = k4a_tile(x_ref[...], of_ref[...], ob_ref[...], rg_ref[...], ym_ref[...], gr_ref[...], gt_a,
                             gf_ref[...], sh_f, sc_f, wo_ref[...], None)
        xm_ref[...] = x_mid
        h2_ref[...] = h2.astype(BF)

    tok = lambda w: pl.BlockSpec((None, TOK, w), lambda i, t: (i, t, 0))
    mod_spec = pl.BlockSpec((None, 8, D_MODEL), lambda i, t: (i, 0, 0))
    return pl.pallas_call(
        body, name="k4a_fwd", grid=(b, l // TOK),
        in_specs=[tok(D_MODEL), tok(512), tok(512), tok(512), tok(512), _full((1, 512)), _full((D_MODEL, D_MODEL)),
                  mod_spec, _full((1, D_MODEL))],
        out_specs=[tok(D_MODEL), tok(D_MODEL)], out_shape=[_sds((b, l, D_MODEL)), _sds((b, l, D_MODEL), BF)],
        compiler_params=_cp((ARB, ARB)),
    )(x, o_f, o_b, rg, y_mla, g_ret, w_out, mod, g_ffn)


TOK_M = 512
TOK_D = 1024
HALF_FF = D_FF // 2


def _k4b_mlp_loss(h2, w1t, w2, x_mid, mod, g_final, tgt):
    b, l, _ = h2.shape
    nt = l // TOK_M

    def body(h2_ref, w1_hbm, w2_hbm, xm_ref, mod_ref, gfin_ref, tgt_ref, dxm_ref, dmlp_ref, r_ref, loss_ref, dgt_ref,
             dgfin_ref, w1_v, w2_v):
        i, t = pl.program_id(0), pl.program_id(1)
        first = jnp.logical_and(i == 0, t == 0)

        @pl.when(first)
        def _():
            pltpu.sync_copy(w1_hbm, w1_v)
            pltpu.sync_copy(w2_hbm, w2_v)

        h2v = h2_ref[...]
        mlp = None
        for half in range(2):
            rows = slice(half * HALF_FF, (half + 1) * HALF_FF)
            r = jnp.maximum(_dot(h2v, w1_v[rows, :], 1, 1), 0.0)
            r_ref[:, rows] = r.astype(BF)
            part = _dot(jnp.square(r), w2_v[rows, :], 1, 0)
            mlp = part if mlp is None else mlp + part
        (gt_f,) = _mod_rows(mod_ref, (5,))
        loss, vjp = jax.vjp(k4c_tile, xm_ref[...], mlp, gt_f, gfin_ref[...], tgt_ref[...])
        dxm, dmlp, dgt, dgfin, _ = vjp(jnp.ones((1, 1), F32))
        dxm_ref[...] = dxm
        dmlp_ref[...] = dmlp.astype(BF)
        _acc(loss_ref, jnp.broadcast_to(loss, (8, LANE)), first)
        _acc(dgfin_ref, dgfin, first)
        _acc(dgt_ref, dgt, t == 0)

    tok = lambda w: pl.BlockSpec((None, TOK_M, w), lambda i, t: (i, t, 0))
    return pl.pallas_call(
        body, name="k4b_mlp_loss", grid=(b, nt),
        in_specs=[tok(D_MODEL), ANY, ANY, tok(D_MODEL), pl.BlockSpec((None, 8, D_MODEL), lambda i, t: (i, 0, 0)),
                  _full((1, D_MODEL)), tok(D_MODEL)],
        out_specs=[tok(D_MODEL), tok(D_MODEL), tok(D_FF), _full((8, LANE)),
                   pl.BlockSpec((None, 1, D_MODEL), lambda i, t: (i, 0, 0)), _full((1, D_MODEL))],
        out_shape=[_sds((b, l, D_MODEL)), _sds((b, l, D_MODEL), BF), _sds((b, l, D_FF), BF), _sds((8, LANE)),
                   _sds((b, 1, D_MODEL)), _sds((1, D_MODEL))],
        scratch_shapes=[pltpu.VMEM((D_FF, D_MODEL), BF), pltpu.VMEM((D_FF, D_MODEL), BF)],
        compiler_params=_cp((ARB, ARB)),
    )(h2, w1t, w2, x_mid, mod, g_final, tgt)


def _k4d_mlp_bwd(h2, dmlp, r, w2):
    b, l, _ = h2.shape
    nt = l // TOK_D

    def body(h2_ref, dm_ref, r_ref, w2_ref, da_ref, dw1_ref, dw2_ref, acc1, acc2):
        i, t = pl.program_id(1), pl.program_id(2)
        first = jnp.logical_and(i == 0, t == 0)
        rv = r_ref[...].astype(F32)
        dm = dm_ref[...]
        da = (_dot(dm, w2_ref[...], 1, 1) * (2.0 * rv)).astype(BF)
        da_ref[...] = da
        _acc(acc2, _dot(jnp.square(rv), dm, 0, 0), first)
        _acc(acc1, _dot(h2_ref[...], da, 0, 0), first)

        @pl.when(jnp.logical_and(i == b - 1, t == nt - 1))
        def _():
            dw1_ref[...] = acc1[...].astype(BF)
            dw2_ref[...] = acc2[...].astype(BF)

    tok = lambda w: pl.BlockSpec((None, TOK_D, w), lambda j, i, t: (i, t, 0))
    col = pl.BlockSpec((None, TOK_D, FF_BLK), lambda j, i, t: (i, t, j))
    return pl.pallas_call(
        body, name="k4d_mlp_bwd", grid=(N_DEV, b, nt),
        in_specs=[tok(D_MODEL), tok(D_MODEL), col, pl.BlockSpec((None, FF_BLK, D_MODEL), lambda j, i, t: (j, 0, 0))],
        out_specs=[col, pl.BlockSpec((None, D_MODEL, FF_BLK), lambda j, i, t: (j, 0, 0)),
                   pl.BlockSpec((None, FF_BLK, D_MODEL), lambda j, i, t: (j, 0, 0))],
        out_shape=[_sds((b, l, D_FF), BF), _sds((N_DEV, D_MODEL, FF_BLK), BF), _sds((N_DEV, FF_BLK, D_MODEL), BF)],
        scratch_shapes=[pltpu.VMEM((D_MODEL, FF_BLK), F32), pltpu.VMEM((FF_BLK, D_MODEL), F32)],
        compiler_params=_cp((ARB, ARB, ARB)),
    )(h2, dmlp, r, w2)


def _k4f_dh2(da, w1t, after):
    b, l, _ = da.shape

    def body(da_ref, w_ref, after_ref, o_ref):
        o_ref[...] = _dot(da_ref[...], w_ref[...], 1, 0)

    return pl.pallas_call(
        body, name="k4f_dh2", grid=(b, l // TOK_M),
        in_specs=[pl.BlockSpec((None, TOK_M, D_FF), lambda i, t: (i, t, 0)), _full((D_FF, D_MODEL)), ANY],
        out_specs=pl.BlockSpec((None, TOK_M, D_MODEL), lambda i, t: (i, t, 0)), out_shape=_sds((b, l, D_MODEL)),
        compiler_params=_cp((ARB, ARB)),
    )(da, w1t, after)


def _k4e_bwd(x, o_f, o_b, rg, y_mla, g_ret, w_out, mod, g_ffn, dxm, dh2):
    b, l, _ = x.shape

    def body(x_ref, of_ref, ob_ref, rg_ref, ym_ref, gr_ref, wo_ref, mod_ref, gf_ref, dxm_ref, dh2_ref,
             dx_ref, do_ref, drg_ref, dym_ref, dwo_ref, dgr_ref, dgf_ref, dmod_ref):
        i, t = pl.program_id(0), pl.program_id(1)
        first = jnp.logical_and(i == 0, t == 0)
        gt_a, sh_f, sc_f = _mod_rows(mod_ref, (2, 3, 4))
        wo = wo_ref[...]

        def f(xv, ofv, rgv, ymv, grv, gta, gfv, shf, scf, p_out):
            return k4a_tile(xv, ofv, ob_ref[...], rgv, ymv, grv, gta, gfv, shf, scf, wo, p_out)

        _, vjp = jax.vjp(f, x_ref[...], of_ref[...], rg_ref[...], ym_ref[...], gr_ref[...], gt_a, gf_ref[...], sh_f,
                         sc_f, jnp.zeros((D_MODEL, D_MODEL), F32))
        dx, do, drg, dym, dgr, dgta, dgf, dshf, dscf, dwo = vjp((dxm_ref[...], dh2_ref[...]))
        dx_ref[...] = dx
        do_ref[...] = do
        drg_ref[...] = drg
        dym_ref[...] = dym
        _acc(dwo_ref, dwo, first)
        _acc(dgr_ref, dgr, first)
        _acc(dgf_ref, dgf, first)
        t0 = t == 0
        _acc(dmod_ref.at[2:3, :], dgta, t0)
        _acc(dmod_ref.at[3:4, :], dshf, t0)
        _acc(dmod_ref.at[4:5, :], dscf, t0)

        @pl.when(t0)
        def _():
            dmod_ref[0:2, :] = jnp.zeros((2, D_MODEL), F32)
            dmod_ref[5:8, :] = jnp.zeros((3, D_MODEL), F32)

    tok = lambda w: pl.BlockSpec((None, TOK_B, w), lambda i, t: (i, t, 0))
    mod_spec = pl.BlockSpec((None, 8, D_MODEL), lambda i, t: (i, 0, 0))
    return pl.pallas_call(
        body, name="k4e_bwd", grid=(b, l // TOK_B),
        in_specs=[tok(D_MODEL), tok(512), tok(512), tok(512), tok(512), _full((1, 512)), _full((D_MODEL, D_MODEL)),
                  mod_spec, _full((1, D_MODEL)), tok(D_MODEL), tok(D_MODEL)],
        out_specs=[tok(D_MODEL), tok(512), tok(512), tok(512), _full((D_MODEL, D_MODEL)), _full((1, 512)),
                   _full((1, D_MODEL)), mod_spec],
        out_shape=[_sds((b, l, D_MODEL)), _sds((b, l, 512)), _sds((b, l, 512)), _sds((b, l, 512)),
                   _sds((D_MODEL, D_MODEL)), _sds((1, 512)), _sds((1, D_MODEL)), _sds((b, 8, D_MODEL))],
        compiler_params=_cp((ARB, ARB)),
    )(x, o_f, o_b, rg, y_mla, g_ret, w_out, mod, g_ffn, dxm, dh2)


def _adamw(w, m, v, pieces, name, after=None):
    r, c = w.shape
    npc = pieces.shape[0]
    per_row = c * (7 * 4 + npc * pieces.dtype.itemsize) * 2
    rb = r
    for cand in (r, 512, 256, 128, 64, 32, 16, 8):
        if r % cand == 0 and cand * per_row <= 32 * 1024 * 1024:
            rb = cand
            break

    def body(w_ref, m_ref, v_ref, p_ref, *rest):
        g_ref, d_ref, nm_ref, nv_ref = rest[-4:]
        g = p_ref[0].astype(F32)
        for k in range(1, npc):
            g = g + p_ref[k].astype(F32)
        wv = w_ref[...]
        mn = ADAM_B1 * m_ref[...] + (1.0 - ADAM_B1) * g
        vn = ADAM_B2 * v_ref[...] + (1.0 - ADAM_B2) * jnp.square(g)
        m_hat = mn / (1.0 - ADAM_B1 ** ADAM_STEP)
        v_hat = vn / (1.0 - ADAM_B2 ** ADAM_STEP)
        g_ref[...] = g
        d_ref[...] = -ADAM_LR * (m_hat / (jnp.sqrt(v_hat) + ADAM_EPS) + ADAM_WD * wv)
        nm_ref[...] = mn
        nv_ref[...] = vn

    blk = pl.BlockSpec((rb, c), lambda i: (i, 0))
    extra = [] if after is None else [after]
    return pl.pallas_call(
        body, name=name, grid=(r // rb,),
        in_specs=[blk, blk, blk, pl.BlockSpec((npc, rb, c), lambda i: (0, i, 0))] + [ANY] * len(extra),
        out_specs=[blk] * 4, out_shape=[_sds((r, c))] * 4, compiler_params=_cp((ARB,)),
    )(w, m, v, pieces, *extra)


def _pad_head_rows(w, d):
    k = w.shape[1]
    return jnp.pad(w.reshape(N_HEADS, d, k), ((0, 0), (0, LANE - d), (0, 0))).reshape(N_HEADS * LANE, k)


def _cut_head_rows(g, d):
    k = g.shape[1]
    return g.reshape(N_HEADS, LANE, k)[:, :d].reshape(N_HEADS * d, k)


def _w_in_pad(wt):
    w_a = jnp.concatenate([_pad_head_rows(wt[0:256], 64), _pad_head_rows(wt[256:512], 64), wt[512:1536]], axis=0)
    w_b = jnp.concatenate([wt[1536:2176], jnp.pad(wt[2176:2240], ((0, 64), (0, 0)))], axis=0)
    return w_a, w_b


def _w_in_cut(g_a, g_b):
    return jnp.concatenate([_cut_head_rows(g_a[0:512], 64), _cut_head_rows(g_a[512:1024], 64), g_a[1024:2048],
                            g_b[0:704]], axis=0)


def _w_uq_pad(wt):
    return jnp.pad(wt.reshape(N_HEADS, 192, 384), ((0, 0), (0, 64), (0, 0))).reshape(1024, 384)


def _w_uq_cut(g):
    return g.reshape(N_HEADS, 256, 384)[:, :192].reshape(768, 384)


def _w_ukv_perm(wt):
    return jnp.transpose(wt.reshape(N_HEADS, 2, LANE, 256), (1, 0, 2, 3)).reshape(1024, 256)


def _w_ukv_unperm(g):
    return jnp.transpose(g.reshape(2, N_HEADS, LANE, 256), (1, 0, 2, 3)).reshape(1024, 256)


def _unshard_cols(g):
    return jnp.transpose(g, (1, 0, 2)).reshape(g.shape[1], N_DEV * g.shape[2])


def _rope_tables():
    rows = SEQ // GRID_W
    row = jnp.repeat(jnp.arange(rows, dtype=F32), GRID_W)
    col = jnp.tile(jnp.arange(GRID_W, dtype=F32), rows)
    freq = ROPE_BASE ** (-jnp.arange(16, dtype=F32) / 16)
    ang = jnp.concatenate([row[:, None] * freq, col[:, None] * freq], axis=-1)
    cos, sin = jnp.cos(ang), jnp.sin(ang)
    z = jnp.zeros((SEQ, 64), F32)
    return jnp.concatenate([cos, cos, z], axis=1), jnp.concatenate([-sin, sin, z], axis=1)


_PACKED = (("g_attn", 1024), ("g_ffn", 1024), ("ret_decay_fwd", 4), ("ret_decay_bwd", 4), ("g_ret", 512),
           ("g_q_lora", 384), ("g_kv_lora", 256), ("g_final", 1024))
_PACK_OFF = {}
_off = 0
for _name, _n in _PACKED:
    _PACK_OFF[_name] = _off
    _off += -(-_n // LANE) * LANE
PACK_W = _off


def _pack_small(vals):
    parts = []
    for name, n in _PACKED:
        a = vals[name].reshape(-1).astype(F32)
        parts.append(jnp.pad(a, (0, -(-n // LANE) * LANE - n)))
    return jnp.concatenate(parts).reshape(1, PACK_W)


def _adamw_small(params, packed, gcc, gb_ada):
    names = list(params)
    n_p = len(names)

    def body(*refs):
        p_ref, gcc_ref, gb_ref = refs[3 * n_p:3 * n_p + 3]
        outs = refs[3 * n_p + 3:]
        for k, name in enumerate(names):
            w_ref, m_ref, v_ref = refs[3 * k:3 * k + 3]
            n = w_ref.shape[1]
            if name == "b_ada":
                g = jnp.concatenate([gb_ref[d, 0:1, :] for d in range(N_DEV)], axis=-1)
            elif name == "c_ctx":
                g = gcc_ref[0, 0:1, :]
                for d in range(1, N_DEV):
                    g = g + gcc_ref[d, 0:1, :]
            else:
                off = _PACK_OFF[name]
                g = p_ref[0, :, off:off + n]
                for d in range(1, N_DEV):
                    g = g + p_ref[d, :, off:off + n]
            mn = ADAM_B1 * m_ref[...] + (1.0 - ADAM_B1) * g
            vn = ADAM_B2 * v_ref[...] + (1.0 - ADAM_B2) * jnp.square(g)
            m_hat = mn / (1.0 - ADAM_B1 ** ADAM_STEP)
            v_hat = vn / (1.0 - ADAM_B2 ** ADAM_STEP)
            outs[4 * k][...] = g
            outs[4 * k + 1][...] = -ADAM_LR * (m_hat / (jnp.sqrt(v_hat) + ADAM_EPS) + ADAM_WD * w_ref[...])
            outs[4 * k + 2][...] = mn
            outs[4 * k + 3][...] = vn

    args = [a for name in names for a in params[name]] + [packed, gcc, gb_ada]
    out_shape = [_sds(params[name][0].shape) for name in names for _ in range(4)]
    outs = pl.pallas_call(body, name="adamw_small", out_shape=out_shape, compiler_params=_cp())(*args)
    return {name: list(outs[4 * k:4 * k + 4]) for k, name in enumerate(names)}


def kernel(x, c, ctx, c_ctx, w_ada, b_ada, g_attn, g_ffn, w_in, ret_decay_fwd, ret_decay_bwd, g_ret, g_q_lora, w_uq, g_kv_lora, w_ukv, w_out, w_ff1, w_ff2, g_final, loss_target, m_c_ctx, m_w_ada, m_b_ada, m_g_attn, m_g_ffn, m_w_in, m_ret_decay_fwd, m_ret_decay_bwd, m_g_ret, m_g_q_lora, m_w_uq, m_g_kv_lora, m_w_ukv, m_w_out, m_w_ff1, m_w_ff2, m_g_final, v_c_ctx, v_w_ada, v_b_ada, v_g_attn, v_g_ffn, v_w_in, v_ret_decay_fwd, v_ret_decay_bwd, v_g_ret, v_g_q_lora, v_w_uq, v_g_kv_lora, v_w_ukv, v_w_out, v_w_ff1, v_w_ff2, v_g_final):
    me = 4 * lax.axis_index("x") + 2 * lax.axis_index("y") + lax.axis_index("c")
    nb = x.shape[0]

    c_pad = jnp.pad(c, ((0, 8 - nb), (0, 0)))
    c_all, g_in, g_uq, g_ukv = _gather_two_level(
        [c_pad, w_in[0].T.astype(BF), w_uq[0].T.astype(BF), w_ukv[0].T.astype(BF)], "gather_weights")
    ws = (*_w_in_pad(g_in.reshape(2240, D_MODEL)), _w_uq_pad(g_uq.reshape(768, 384)),
          _w_ukv_perm(g_ukv.reshape(1024, 256)))

    crows = jnp.concatenate([c_all[:, :nb].reshape(N_DEV * nb, D_MODEL), c_ctx[None], jnp.zeros((7, D_MODEL), F32)])
    b_blk = lax.dynamic_slice(b_ada, (0, me * 768), (1, 768))
    (mod_g,) = _exchange([_mod_fwd(crows, w_ada[0], b_blk)], True, "gather_mod")
    mod_all = _unshard_cols(mod_g)
    behind = mod_g[0, 0, 0:1] * 0.0
    st_o = _exchange_start([(w_out[0] + behind).astype(BF)], True, "gather_wo_start")
    st_g = _exchange_start([w_ff1[0].T.astype(BF), w_ff2[0].astype(BF)], True, "gather_ff_start", after=st_o["token"])
    mod_all = mod_all + st_g["token"][0:1, 0:1]
    mod_mine = lax.dynamic_slice(mod_all, (me * nb, 0), (nb, 6 * D_MODEL)).reshape(nb, 6, D_MODEL)
    mod = jnp.pad(mod_mine, ((0, 0), (0, 2), (0, 0)))
    mod_c = jnp.pad(mod_all[16].reshape(1, 6, D_MODEL), ((0, 0), (0, 2), (0, 0)))

    tabs = _rope_tables()
    dec_f = ret_decay_fwd.reshape(N_HEADS, 1, 1)
    dec_b = ret_decay_bwd.reshape(N_HEADS, 1, 1)

    rkc, rvc, k_ctx, v_ctx = _k1_fwd(ctx, mod_c, g_attn, g_q_lora, g_kv_lora, ws, tabs, None, True)
    rq, rk, rv, rg, q, k_all, v_all = _k1_fwd(x, mod, g_attn, g_q_lora, g_kv_lora, ws, tabs, (k_ctx, v_ctx), False)
    o_f, o_b, sf_prev, sb_prev = _k2_fwd(rq, rk, rv, rkc, rvc, dec_f, dec_b)
    y_mla, lse = _k3_fwd(q, k_all, v_all)
    (g_out,) = _exchange_wait(st_o, y_mla, "gather_wo_wait")
    wo = g_out.reshape(D_MODEL, D_MODEL)
    x_mid, h2 = _k4a_fwd(x, o_f, o_b, rg, y_mla, g_ret, wo, mod, g_ffn)
    g_ff1t, g_ff2 = _exchange_wait(st_g, x_mid, "gather_ff_wait")
    w1t = g_ff1t.reshape(D_FF, D_MODEL)
    dxm, dmlp, relu_a, loss_acc, dgt_f, dg_final = _k4b_mlp_loss(h2, w1t, g_ff2.reshape(D_FF, D_MODEL), x_mid, mod,
                                                                 g_final.reshape(1, D_MODEL), loss_target)

    da, dw1, dw2 = _k4d_mlp_bwd(h2, dmlp, relu_a, g_ff2)
    st_s = _exchange_start([dw1, dw2], False, "scatter_ff_start")
    dh2 = _k4f_dh2(da, w1t, st_s["token"])
    g_ret_t = g_ret + st_s["token"][0:1, 0:1]
    dx_res, do, drg, dym, dwo, dg_ret, dg_ffn, dmod_a = _k4e_bwd(x, o_f, o_b, rg, y_mla, g_ret_t, wo, mod, g_ffn, dxm, dh2)
    st_w = _exchange_start([dwo.reshape(N_DEV, 128, D_MODEL).astype(BF)], False, "scatter_wo_start")
    dq, dk_all, dv_all = _k3_bwd(q, k_all, v_all, y_mla, lse, dym, st_w["token"])
    dqf, dkf, dvf, dqb, dkb, dvb, dkc, dvc, ddf, ddb = _k2_bwd(rq, rk, rv, do, sf_prev, sb_prev, rkc, rvc, dec_f, dec_b)
    cts = [[(dqf, 0), (dqb, 0)], [(dkf, 0), (dkb, 0)], [(dvf, 0), (dvb, 0)], [(drg, 0)], [(dq, 0)],
           [(dk_all, CTX_LEN)], [(dv_all, CTX_LEN)]]
    cts_c = [[(dkc, 0)], [(dvc, 0)], [(dk_all, 0)], [(dv_all, 0)]]
    grad_x, accs, dmod_1, dmod_c1 = _k1_bwd(x, ctx, mod, mod_c, g_attn, g_q_lora, g_kv_lora, ws, tabs, cts, cts_c,
                                            dx_res)
    dwa, dwb, dwq, dwk, dg_attn, dg_q, dg_kv = accs

    dmod_loc = (dmod_a + dmod_1).at[:, 5, :].set(dgt_f[:, 0, :])[:, :6, :].reshape(nb, 6 * D_MODEL)
    dmod_ctx = dmod_c1[:, :6, :].reshape(1, 6 * D_MODEL)
    small = {"g_attn": dg_attn, "g_ffn": dg_ffn, "ret_decay_fwd": jnp.sum(ddf[:, :, 0, 0], axis=0),
             "ret_decay_bwd": jnp.sum(ddb[:, :, 0, 0], axis=0), "g_ret": dg_ret, "g_q_lora": dg_q, "g_kv_lora": dg_kv,
             "g_final": dg_final}
    extra = jnp.concatenate([dmod_loc, dmod_ctx, jnp.zeros((5, 6 * D_MODEL), F32)])
    ex_pieces = jnp.transpose(extra.reshape(8, N_DEV, 768), (1, 0, 2))
    sm_g, ex_g, loss_g = _exchange([_pack_small(small), ex_pieces, loss_acc], [True, False, True], "gather_small")
    dmod_blk = jnp.concatenate([ex_g[:, :nb].reshape(N_DEV * nb, 768), jnp.zeros((8, 768), F32)])
    gw_ada, gcc_part, gb_part = _mod_bwd(crows, w_ada[0], dmod_blk, ex_g[:, nb])
    gcc_g, gb_g = _exchange([gcc_part, gb_part], True, "gather_c_ctx")

    p_ff1, p_ff2 = _exchange_wait(st_s, gcc_g, "scatter_ff_wait")
    (p_wo,) = _exchange_wait(st_w, p_ff1, "scatter_wo_wait")
    st_r = _exchange_start([_w_in_cut(dwa, dwb).reshape(N_DEV, 280, D_MODEL), _w_uq_cut(dwq).reshape(N_DEV, 96, 384),
                            _w_ukv_unperm(dwk).reshape(N_DEV, 128, 256)], False, "scatter_rest_start",
                           after=p_wo)

    res = {}
    early = (("w_ff1", w_ff1, m_w_ff1, v_w_ff1, p_ff1), ("w_ff2", w_ff2, m_w_ff2, v_w_ff2, p_ff2),
             ("w_ada", w_ada, m_w_ada, v_w_ada, gw_ada[None]))
    for name, w, m, v, pcs in early:
        res[name] = [a[None] for a in _adamw(w[0], m[0], v[0], pcs, "adamw_" + name, after=st_r["token"])]
    pieces = _exchange_wait(st_r, res["w_ada"][3], "scatter_rest_wait")
    for name, w, m, v, pcs in (("w_in", w_in, m_w_in, v_w_in, pieces[0]), ("w_uq", w_uq, m_w_uq, v_w_uq, pieces[1])):
        res[name] = [a.T[None] for a in _adamw(w[0].T, m[0].T, v[0].T, pcs, "adamw_" + name)]
    late = (("w_ukv", w_ukv, m_w_ukv, v_w_ukv, jnp.transpose(pieces[2], (0, 2, 1))),
            ("w_out", w_out, m_w_out, v_w_out, p_wo))
    for name, w, m, v, pcs in late:
        res[name] = [a[None] for a in _adamw(w[0], m[0], v[0], pcs, "adamw_" + name)]

    smalls = {"c_ctx": (c_ctx, m_c_ctx, v_c_ctx), "b_ada": (b_ada, m_b_ada, v_b_ada), "g_attn": (g_attn, m_g_attn, v_g_attn),
              "g_ffn": (g_ffn, m_g_ffn, v_g_ffn), "ret_decay_fwd": (ret_decay_fwd, m_ret_decay_fwd, v_ret_decay_fwd),
              "ret_decay_bwd": (ret_decay_bwd, m_ret_decay_bwd, v_ret_decay_bwd), "g_ret": (g_ret, m_g_ret, v_g_ret),
              "g_q_lora": (g_q_lora, m_g_q_lora, v_g_q_lora), "g_kv_lora": (g_kv_lora, m_g_kv_lora, v_g_kv_lora),
              "g_final": (g_final, m_g_final, v_g_final)}
    rows = {k: tuple(a.reshape(1, -1) for a in t) for k, t in smalls.items()}
    for name, outs in _adamw_small(rows, sm_g, gcc_g, gb_g).items():
        res[name] = [o.reshape(smalls[name][0].shape) for o in outs]

    loss = loss_g[0, 0, 0]
    for k in range(1, N_DEV):
        loss = loss + loss_g[k, 0, 0]

    order = ("c_ctx", "w_ada", "b_ada", "g_attn", "g_ffn", "w_in", "ret_decay_fwd", "ret_decay_bwd", "g_ret", "g_q_lora",
             "w_uq", "g_kv_lora", "w_ukv", "w_out", "w_ff1", "w_ff2", "g_final")
    return (loss, grad_x, *[res[n][0] for n in order], *[res[n][1] for n in order], *[res[n][2] for n in order],
            *[res[n][3] for n in order])
```

```python
import functools
import math

import jax
import jax.numpy as jnp
from jax import lax
from jax.experimental import pallas as pl
from jax.experimental.pallas import tpu as pltpu

F32 = jnp.float32
BF = jnp.bfloat16
EPS = 1e-6
LANE = 128
N_DEV = 8
D_MODEL = 1024
SEQ = 2048
CTX_LEN = 256
GRID_W = 64
N_HEADS = 4
RET_CHUNK = 512
N_CHUNK = SEQ // RET_CHUNK
D_FF = 4096
FF_BLK = D_FF // N_DEV
IN_PAD = 2816
KV_LEN = CTX_LEN + SEQ
ROPE_BASE = 10000.0
ADAM_LR, ADAM_B1, ADAM_B2, ADAM_EPS, ADAM_WD, ADAM_STEP = 0.001, 0.9, 0.999, 1e-08, 0.01, 10
TOK = 256
TOK_B = 256
VMEM_LIMIT = 56 * 1024 * 1024
ARB = "arbitrary"
MESH = pl.DeviceIdType.MESH
_HEAD_SL = [slice(LANE * h, LANE * (h + 1)) for h in range(N_HEADS)]
W_SHAPES = [(2048, D_MODEL), (768, D_MODEL), (1024, 384), (1024, 256)]


def _dot(a, b, ca, cb):
    return lax.dot_general(a.astype(BF), b.astype(BF), (((ca,), (cb,)), ((), ())), preferred_element_type=F32)


@jax.custom_vjp
def mm(a, b):
    return _dot(a, b, 1, 0)


@jax.custom_vjp
def mm_nt(a, b):
    return _dot(a, b, 1, 1)


@jax.custom_vjp
def mm_tn(a, b):
    return _dot(a, b, 0, 0)


mm.defvjp(lambda a, b: (_dot(a, b, 1, 0), (a, b)), lambda r, g: (mm_nt(g, r[1]), mm_tn(r[0], g)))
mm_nt.defvjp(lambda a, b: (_dot(a, b, 1, 1), (a, b)), lambda r, g: (mm(g, r[1]), mm_tn(g, r[0])))
mm_tn.defvjp(lambda a, b: (_dot(a, b, 0, 0), (a, b)), lambda r, g: (mm_nt(r[1], g), mm(r[0], g)))


@jax.custom_vjp
def _mmw(a, w, probe):
    return _dot(a, w, 1, 0)


def _mmw_bwd(r, g):
    a, w = r
    return mm_nt(g, w), jnp.zeros_like(w), mm_tn(a, g)


_mmw.defvjp(lambda a, w, probe: (_dot(a, w, 1, 0), (a, w)), _mmw_bwd)


@jax.custom_vjp
def _mmwt(a, wt, probe):
    return _dot(a, wt, 1, 1)


_mmwt.defvjp(lambda a, wt, probe: (_dot(a, wt, 1, 1), (a, wt)),
             lambda r, g: (mm(g, r[1]), jnp.zeros_like(r[1]), mm_tn(g, r[0])))


def mmwt(a, wt, probe):
    return _dot(a, wt, 1, 1) if probe is None else _mmwt(a, wt, probe)


def mmw(a, w, probe):
    return _dot(a, w, 1, 0) if probe is None else _mmw(a, w, probe)


def rmsn(x, g):
    return x * lax.rsqrt(jnp.mean(x * x, axis=-1, keepdims=True) + EPS) * g


def silu(x):
    return x * jax.nn.sigmoid(x)


def _swap32_impl(x):
    n = x.shape[-1]
    lane = lax.broadcasted_iota(jnp.int32, x.shape, x.ndim - 1) % LANE
    up = pltpu.roll(x, n - 32, x.ndim - 1)
    dn = pltpu.roll(x, 32, x.ndim - 1)
    return jnp.where(lane < 32, up, jnp.where(lane < 64, dn, 0.0))


@jax.custom_vjp
def swap32(x):
    return _swap32_impl(x)


swap32.defvjp(lambda x: (_swap32_impl(x), None), lambda _, g: (_swap32_impl(g),))


def rope(x, cs, sn):
    return x * cs + swap32(x) * sn


def k1_tile(x, sh, sc, g_attn, g_q, g_kv, ws, ps, tabs, is_ctx):
    w_a, w_b, w_uq, w_ukv = ws
    p_a, p_b, p_uq, p_ukv = ps
    cs1, sn1 = tabs
    cs, sn = jnp.concatenate([cs1] * N_HEADS, axis=-1), jnp.concatenate([sn1] * N_HEADS, axis=-1)
    cq_t = jnp.concatenate([jnp.ones_like(cs1), cs1] * N_HEADS, axis=-1)
    sq_t = jnp.concatenate([jnp.zeros_like(sn1), sn1] * N_HEADS, axis=-1)
    h = rmsn(x, g_attn) * (1.0 + sc) + sh
    pa = mmwt(h, w_a, p_a)
    pb = mmwt(h, w_b, p_b)
    rk = pa[:, 512:1024] * 0.125
    rv = pa[:, 1024:1536]
    kpe = pb[:, 640:768]
    kv = mmwt(rmsn(pb[:, 384:640], g_kv), w_ukv, p_ukv)
    if not is_ctx:
        rk = rope(rk, cs, sn)
        kpe = rope(kpe, cs1, sn1)
    k_full = jnp.concatenate([piece for sl in _HEAD_SL for piece in (kv[:, sl], kpe)], axis=-1)
    v = kv[:, 512:]
    if is_ctx:
        return rk, rv, k_full, v
    rq = rope(pa[:, 0:512], cs, sn)
    rg = pa[:, 1536:2048]
    q = rope(mmwt(rmsn(pb[:, 0:384], g_q), w_uq, p_uq), cq_t, sq_t)
    return rq, rk, rv, rg, q, k_full, v


def log_sigmoid(x):
    return jnp.minimum(x, 0.0) - jnp.log(1.0 + jnp.exp(-jnp.abs(x)))


def ret_chunk(q, k, v, s, lg, reverse):
    c = RET_CHUNK
    ii = lax.broadcasted_iota(jnp.int32, (c, c), 0).astype(F32)
    jj = lax.broadcasted_iota(jnp.int32, (c, c), 1).astype(F32)
    diff = (jj - ii) if reverse else (ii - jj)
    dec = jnp.where(diff >= 0, jnp.exp(lg * jnp.maximum(diff, 0.0)), 0.0)
    pos = lax.broadcasted_iota(jnp.int32, (c, 1), 0).astype(F32)
    if reverse:
        wk, wq = jnp.exp(lg * pos), jnp.exp(lg * (c - pos))
    else:
        wk, wq = jnp.exp(lg * (c - 1.0 - pos)), jnp.exp(lg * (pos + 1.0))
    o = mm(mm_nt(q, k) * dec, v) + mm(q * wq, s)
    s_next = jnp.exp(lg * float(c)) * s + mm_tn(k * wk, v)
    return o, s_next


def ctx_state(kc, vc, lg, reverse):
    n = kc.shape[0]
    pos = lax.broadcasted_iota(jnp.int32, (n, 1), 0).astype(F32)
    w = jnp.exp(lg * pos) if reverse else jnp.exp(lg * (n - 1.0 - pos))
    return mm_tn(kc * w, vc)


def attn_head(qn, qp, kn, kp, v):
    s = (mm_nt(qn, kn) + mm_nt(qp, kp)) * (1.0 / math.sqrt(192.0))
    e = jnp.exp(s - jnp.max(s, axis=-1, keepdims=True))
    return mm(e / jnp.sum(e, axis=-1, keepdims=True), v)


def gn_gate(o, rg, g_ret):
    ys = []
    for h in range(N_HEADS):
        sl = slice(LANE * h, LANE * (h + 1))
        oh = o[:, sl]
        mu = jnp.mean(oh, axis=-1, keepdims=True)
        var = jnp.mean(jnp.square(oh - mu), axis=-1, keepdims=True)
        ys.append((oh - mu) * lax.rsqrt(var + EPS) * g_ret[:, sl])
    return jnp.concatenate(ys, axis=-1) * silu(rg)


def k4a_tile(x, o_f, o_b, rg, y_mla, g_ret, gt_a, g_ffn, sh_f, sc_f, w_out, p_out):
    mix = jnp.concatenate([gn_gate(o_f + o_b, rg, g_ret), y_mla], axis=-1)
    x_mid = x + gt_a * mmw(mix, w_out, p_out)
    h2 = rmsn(x_mid, g_ffn) * (1.0 + sc_f) + sh_f
    return x_mid, h2


def k4c_tile(x_mid, mlp, gt_f, g_final, tgt):
    y = rmsn(x_mid + gt_f * mlp, g_final)
    per_tok = jnp.mean(jnp.square(y - tgt), axis=-1, keepdims=True)
    return 0.5 * jnp.sum(per_tok, axis=0, keepdims=True)


def _cp(sem=None, vmem=VMEM_LIMIT):
    return pltpu.CompilerParams(dimension_semantics=sem, vmem_limit_bytes=vmem)


def _acc(ref, val, first):
    @pl.when(first)
    def _():
        ref[...] = val

    @pl.when(jnp.logical_not(first))
    def _():
        ref[...] += val


def _full(shape):
    nd = len(shape)
    return pl.BlockSpec(shape, lambda *_: (0,) * nd)


ANY = pl.BlockSpec(memory_space=pl.ANY)


def _sds(shape, dtype=F32):
    return jax.ShapeDtypeStruct(shape, dtype)


def _exchange(arrs, gather, name):
    n = len(arrs)
    modes = [gather] * n if isinstance(gather, bool) else list(gather)
    out_shape = [_sds(((N_DEV,) + a.shape) if g else a.shape, a.dtype) for a, g in zip(arrs, modes)]

    def body(*refs):
        ins, outs = refs[:n], refs[n:2 * n]
        send_sems, recv_sems, local_sems = refs[2 * n:]
        x, y, c = lax.axis_index("x"), lax.axis_index("y"), lax.axis_index("c")
        me = 4 * x + 2 * y + c
        sends, recvs, locs = [], [], []
        for i in range(n):
            gather = modes[i]
            for k in range(N_DEV - 1):
                bits = k + 1
                px = x ^ ((bits >> 2) & 1)
                py = y ^ ((bits >> 1) & 1)
                pc = c ^ (bits & 1)
                peer = 4 * px + 2 * py + pc
                src = ins[i] if gather else ins[i].at[peer]
                sem = i * (N_DEV - 1) + k
                sends.append(pltpu.make_async_remote_copy(
                    src_ref=src, dst_ref=outs[i].at[me], send_sem=send_sems.at[sem], recv_sem=recv_sems.at[sem],
                    device_id=(px, py, pc), device_id_type=MESH))
                recvs.append(pltpu.make_async_remote_copy(
                    src_ref=src, dst_ref=outs[i].at[peer], send_sem=send_sems.at[sem], recv_sem=recv_sems.at[sem],
                    device_id=(px, py, pc), device_id_type=MESH))
            locs.append(pltpu.make_async_copy(ins[i] if gather else ins[i].at[me], outs[i].at[me], local_sems.at[i]))
        for cp in locs + sends:
            cp.start()
        for cp in recvs:
            cp.wait_recv()
        for cp in sends:
            cp.wait_send()
        for cp in locs:
            cp.wait()

    outs = pl.pallas_call(
        body, name=name, out_shape=out_shape, in_specs=[ANY] * n, out_specs=[ANY] * n,
        scratch_shapes=[pltpu.SemaphoreType.DMA((n * (N_DEV - 1),)), pltpu.SemaphoreType.DMA((n * (N_DEV - 1),)),
                        pltpu.SemaphoreType.DMA((n,))],
    )(*arrs)
    return list(outs)


def _gather_two_level(arrs, name):
    n = len(arrs)

    def body(*refs):
        ins, outs = refs[:n], refs[n:2 * n]
        send_sems, recv_sems, local_sems = refs[2 * n:]
        x, y, c = lax.axis_index("x"), lax.axis_index("y"), lax.axis_index("c")
        sibling = (x, y, 1 - c)
        chips = [(1 - x, y), (x, 1 - y), (1 - x, 1 - y)]

        def slot(px, py, pc):
            return 4 * px + 2 * py + pc

        first, passed, waits, locs = [], [], [], []
        for i in range(n):
            def copy(k, block, to, src=None, i=i):
                dst = outs[i].at[slot(*block)]
                return pltpu.make_async_remote_copy(
                    src_ref=dst if src is None else src, dst_ref=dst, send_sem=send_sems.at[7 * i + k],
                    recv_sem=recv_sems.at[7 * i + k], device_id=to, device_id_type=MESH)

            locs.append(pltpu.make_async_copy(ins[i], outs[i].at[slot(x, y, c)], local_sems.at[i]))
            first.append(copy(0, (x, y, c), sibling, src=ins[i]))
            first += [copy(1 + j, (x, y, c), (*chip, c), src=ins[i]) for j, chip in enumerate(chips)]
            passed.append([copy(4 + j, (*chip, c), sibling) for j, chip in enumerate(chips)])
            waits.append([copy(1 + j, (*chip, c), (x, y, c)) for j, chip in enumerate(chips)])
        for cp in locs + first:
            cp.start()
        for j in range(3):
            for i in range(n):
                waits[i][j].wait_recv()
                passed[i][j].start()
        for i in range(n):
            def arrival(k, block, i=i):
                dst = outs[i].at[slot(*block)]
                return pltpu.make_async_remote_copy(
                    src_ref=dst, dst_ref=dst, send_sem=send_sems.at[7 * i + k], recv_sem=recv_sems.at[7 * i + k],
                    device_id=sibling, device_id_type=MESH)

            arrival(0, (x, y, 1 - c)).wait_recv()
            for j, chip in enumerate(chips):
                arrival(4 + j, (*chip, 1 - c)).wait_recv()
        for cp in first + [p for ps in passed for p in ps]:
            cp.wait_send()
        for cp in locs:
            cp.wait()

    outs = pl.pallas_call(
        body, name=name, out_shape=[_sds((N_DEV,) + a.shape, a.dtype) for a in arrs], in_specs=[ANY] * n,
        out_specs=[ANY] * n,
        scratch_shapes=[pltpu.SemaphoreType.DMA((7 * n,)), pltpu.SemaphoreType.DMA((7 * n,)),
                        pltpu.SemaphoreType.DMA((n,))],
    )(*arrs)
    return list(outs)


HBM = pl.BlockSpec(memory_space=pltpu.HBM)
SEM = pl.BlockSpec(memory_space=pltpu.SEMAPHORE)
EFFECT = pltpu.SideEffectType.DATAFLOW_SIDE_EFFECTING


def _peer(k):
    x, y, c = lax.axis_index("x"), lax.axis_index("y"), lax.axis_index("c")
    bits = k + 1
    px, py, pc = x ^ ((bits >> 2) & 1), y ^ ((bits >> 1) & 1), c ^ (bits & 1)
    return (px, py, pc), 4 * px + 2 * py + pc, 4 * x + 2 * y + c


def _exchange_start(arrs, gather, name, after=None):
    n = len(arrs)
    lands = [pltpu.with_memory_space_constraint(lax.empty(((N_DEV,) + a.shape) if gather else a.shape, a.dtype),
                                                pltpu.HBM) for a in arrs]
    srcs = [pltpu.with_memory_space_constraint(a, pltpu.HBM) for a in arrs]

    extra = [] if after is None else [after]

    def body(*refs):
        ins, zones = refs[:n], refs[n:2 * n]
        send_sems, recv_sems, local_sems = refs[2 * n + len(extra):2 * n + len(extra) + 3]
        token = refs[-1]
        for i in range(n):
            for k in range(N_DEV - 1):
                dev, peer, me = _peer(k)
                sem = i * (N_DEV - 1) + k
                pltpu.make_async_remote_copy(
                    src_ref=ins[i] if gather else ins[i].at[peer], dst_ref=zones[i].at[me],
                    send_sem=send_sems.at[sem], recv_sem=recv_sems.at[sem], device_id=dev, device_id_type=MESH).start()
            _, _, me = _peer(0)
            pltpu.make_async_copy(ins[i] if gather else ins[i].at[me], zones[i].at[me], local_sems.at[i]).start()
        token[...] = jnp.zeros_like(token)

    nsem = n * (N_DEV - 1)
    outs = pl.pallas_call(
        body, name=name,
        out_shape=[pltpu.SemaphoreType.DMA((nsem,)), pltpu.SemaphoreType.DMA((nsem,)), pltpu.SemaphoreType.DMA((n,))]
        + [pltpu.HBM(a.shape, a.dtype) for a in srcs] + [pltpu.HBM(z.shape, z.dtype) for z in lands]
        + [_sds((8, LANE))],
        in_specs=[HBM] * (2 * n) + [ANY] * len(extra),
        out_specs=[SEM, SEM, SEM] + [HBM] * (2 * n) + [pl.BlockSpec(memory_space=pltpu.VMEM)],
        input_output_aliases={i: 3 + i for i in range(2 * n)},
        compiler_params=pltpu.CompilerParams(has_side_effects=EFFECT),
    )(*srcs, *lands, *extra)
    return {"n": n, "gather": gather, "sems": outs[:3], "srcs": outs[3:3 + n], "lands": outs[3 + n:3 + 2 * n],
            "token": outs[-1]}


def _exchange_wait(st, after, name):
    n, gather = st["n"], st["gather"]

    def body(*refs):
        ins, zones = refs[:n], refs[n:2 * n]
        send_sems, recv_sems, local_sems = refs[2 * n:2 * n + 3]
        for i in range(n):
            for k in range(N_DEV - 1):
                dev, peer, me = _peer(k)
                sem = i * (N_DEV - 1) + k
                src = ins[i] if gather else ins[i].at[peer]
                cp = pltpu.make_async_remote_copy(
                    src_ref=src, dst_ref=zones[i].at[peer], send_sem=send_sems.at[sem], recv_sem=recv_sems.at[sem],
                    device_id=dev, device_id_type=MESH)
                cp.wait_send()
                cp.wait_recv()
            _, _, me = _peer(0)
            pltpu.make_async_copy(ins[i] if gather else ins[i].at[me], zones[i].at[me], local_sems.at[i]).wait()

    outs = pl.pallas_call(
        body, name=name,
        out_shape=[pltpu.HBM(a.shape, a.dtype) for a in st["srcs"]] + [pltpu.HBM(z.shape, z.dtype) for z in st["lands"]],
        in_specs=[HBM] * (2 * n) + [SEM, SEM, SEM, ANY], out_specs=[HBM] * (2 * n),
        input_output_aliases={i: i for i in range(2 * n)},
        compiler_params=pltpu.CompilerParams(has_side_effects=EFFECT),
    )(*st["srcs"], *st["lands"], *st["sems"], after)
    return list(outs[n:])


def _mod_fwd(crows, w_ada, b_blk):
    def body(c_ref, w_ref, b_ref, o_ref):
        o_ref[...] = mm(silu(c_ref[...]), w_ref[...]) + b_ref[...]

    return pl.pallas_call(body, name="mod_fwd", out_shape=_sds((24, 768)), compiler_params=_cp())(crows, w_ada, b_blk)


def _mod_bwd(crows, w_ada, dmod_blk, dmodc_blk):
    def body(c_ref, w_ref, d_ref, dc_ref, gw_ref, gc_ref, gb_ref):
        cr = c_ref[...]
        dc = dc_ref[0:1, :]
        for p in range(1, N_DEV):
            dc = dc + dc_ref[p:p + 1, :]
        row = lax.broadcasted_iota(jnp.int32, (24, 1), 0)
        gw_ref[...] = mm_tn(silu(cr), jnp.where(row == 16, dc, d_ref[...]))
        cc = cr[16:17, :]
        sg = jax.nn.sigmoid(cc)
        part = mm_nt(jnp.broadcast_to(dc, (8, 768)), w_ref[...])
        gc_ref[...] = part * (sg * (1.0 + cc * (1.0 - sg)))
        gb_ref[...] = jnp.broadcast_to(jnp.sum(d_ref[...], axis=0, keepdims=True) + dc, (8, 768))

    return pl.pallas_call(
        body, name="mod_bwd", out_shape=[_sds((D_MODEL, 768)), _sds((8, D_MODEL)), _sds((8, 768))],
        compiler_params=_cp())(crows, w_ada, dmod_blk, dmodc_blk)


def _tab_specs(tk):
    return [pl.BlockSpec((tk, LANE), lambda i, t: (t, 0))] * 2


def _k1_fwd(x, mod, g_attn, g_q, g_kv, ws, tabs, kv_all, is_ctx):
    b, l, _ = x.shape
    nt = l // TOK
    n_f32 = 2 if is_ctx else 4

    def body(x_ref, mod_ref, ga_ref, gq_ref, gk_ref, wa_ref, wb_ref, wq_ref, wk_ref, cs_ref, sn_ref, *rest):
        outs = rest if is_ctx else rest[2:]
        res = k1_tile(x_ref[...], mod_ref[0:1, :], mod_ref[1:2, :], ga_ref[...], gq_ref[...], gk_ref[...],
                      (wa_ref[...], wb_ref[...], wq_ref[...], wk_ref[...]), (None,) * 4,
                      (cs_ref[...], sn_ref[...]), is_ctx)
        for o_ref, r in zip(outs, res):
            o_ref[...] = r.astype(o_ref.dtype)

    tok = lambda w, off=0: pl.BlockSpec((None, TOK, w), lambda i, t: (i, t + off, 0))
    mod_spec = pl.BlockSpec((None, 8, D_MODEL), (lambda i, t: (0, 0, 0)) if is_ctx else (lambda i, t: (i, 0, 0)))
    kv_off = 0 if is_ctx else CTX_LEN // TOK
    in_specs = ([tok(D_MODEL), mod_spec, _full((1, D_MODEL)), _full((1, 384)), _full((1, 256))]
                + [_full(s) for s in W_SHAPES] + _tab_specs(TOK))
    args = [x, mod, g_attn, g_q, g_kv, *ws, *tabs]
    out_specs = [tok(512)] * n_f32 + ([] if is_ctx else [tok(1024)]) + [tok(1024, kv_off), tok(512, kv_off)]
    out_shape = ([_sds((b, l, 512))] * n_f32 + ([] if is_ctx else [_sds((b, l, 1024), BF)])
                 + [_sds((b, KV_LEN, 1024), BF), _sds((b, KV_LEN, 512), BF)])
    aliases = {}
    if not is_ctx:
        aliases = {len(args): n_f32 + 1, len(args) + 1: n_f32 + 2}
        in_specs += [ANY, ANY]
        args += list(kv_all)
    return pl.pallas_call(
        body, name="k1_fwd_ctx" if is_ctx else "k1_fwd", grid=(b, nt), in_specs=in_specs, out_specs=out_specs,
        out_shape=out_shape, input_output_aliases=aliases, compiler_params=_cp((ARB, ARB)),
    )(*args)


N_ACC = 7


def _k1_bwd(x, ctx, mod, mod_c, g_attn, g_q, g_kv, ws, tabs, cts, cts_c, dx_res):
    b, l, _ = x.shape
    tk = TOK_B
    nt = l // tk
    flat = [[a for group in c for a in group] for c in (cts, cts_c)]
    sizes = [[len(g) for g in c] for c in (cts, cts_c)]
    acc_shapes = W_SHAPES + [(1, D_MODEL), (1, 384), (1, 256)]

    def body(*refs):
        it = iter(refs)
        x_ref, c_ref, mod_ref, modc_ref, ga_ref, gq_ref, gk_ref = [next(it) for _ in range(7)]
        w_hbm = [next(it) for _ in range(4)]
        tab_refs = [next(it) for _ in range(2)]
        ct_refs = [[next(it) for _ in f] for f in flat]
        res_ref, gx_ref = next(it), next(it)
        out_hbm = [next(it) for _ in range(N_ACC)]
        dmod_ref, dmodc_ref = next(it), next(it)
        w_vmem = [next(it) for _ in range(4)]
        accs = [next(it) for _ in range(N_ACC)]
        sem = next(it)
        i, t = pl.program_id(0), pl.program_id(1)
        first = jnp.logical_and(i == 0, t == 0)

        @pl.when(first)
        def _():
            for src, dst in zip(w_hbm, w_vmem):
                pltpu.sync_copy(src, dst)
            for k in range(N_ACC):
                accs[k][...] = jnp.zeros(acc_shapes[k], F32)

        def tile(is_ctx):
            which = 1 if is_ctx else 0
            ct_vals, pos = [], 0
            for gsz in sizes[which]:
                v = ct_refs[which][pos][...].astype(F32)
                for r in ct_refs[which][pos + 1:pos + gsz]:
                    v = v + r[...]
                ct_vals.append(v)
                pos += gsz
            wv = tuple(r[...] for r in w_vmem)
            tv = tuple(r[...] for r in tab_refs)
            m_ref = modc_ref if is_ctx else mod_ref

            def f(xv, sh, sc, ga, gq, gk, *probes):
                return k1_tile(xv, sh, sc, ga, gq, gk, wv, probes, tv, is_ctx)

            probes = [jnp.zeros(s, F32) for s in W_SHAPES]
            xin = c_ref[...] if is_ctx else x_ref[...]
            _, vjp = jax.vjp(f, xin, m_ref[0:1, :], m_ref[1:2, :], ga_ref[...], gq_ref[...], gk_ref[...], *probes)
            dx, dsh, dsc, dga, dgq, dgk, dwa, dwb, dwq, dwk = vjp(tuple(ct_vals))
            for ref, val in zip(accs, (dwa, dwb, dwq, dwk, dga, dgq, dgk)):
                ref[...] += val
            return dx, dsh, dsc

        @pl.when(t == 0)
        def _():
            _, dsh, dsc = tile(True)
            _acc(dmodc_ref.at[0:1, :], dsh, i == 0)
            _acc(dmodc_ref.at[1:2, :], dsc, i == 0)

            @pl.when(i == 0)
            def _():
                dmodc_ref[2:8, :] = jnp.zeros((6, D_MODEL), F32)

        @pl.when(t > 0)
        def _():
            dx, dsh, dsc = tile(False)
            gx_ref[...] = dx + res_ref[...]
            _acc(dmod_ref.at[0:1, :], dsh, t == 1)
            _acc(dmod_ref.at[1:2, :], dsc, t == 1)

            @pl.when(t == 1)
            def _():
                dmod_ref[2:8, :] = jnp.zeros((6, D_MODEL), F32)

        @pl.when(jnp.logical_and(i == b - 1, t == nt))
        def _():
            for k in range(4):
                w_vmem[k][...] = accs[k][...].astype(BF)
            cps = [pltpu.make_async_copy(w_vmem[k] if k < 4 else accs[k], out_hbm[k], sem.at[k]) for k in range(N_ACC)]
            for cp in cps:
                cp.start()
            for cp in cps:
                cp.wait()

    lat = lambda w, off=0: pl.BlockSpec((None, tk, w), lambda i, t: (i, jnp.maximum(t - 1, 0) + off, 0))
    con = lambda w: pl.BlockSpec((None, tk, w), lambda i, t: (i, 0, 0))
    mod_spec = pl.BlockSpec((None, 8, D_MODEL), lambda i, t: (i, 0, 0))
    modc_spec = pl.BlockSpec((None, 8, D_MODEL), lambda i, t: (0, 0, 0))
    tab_spec = pl.BlockSpec((tk, LANE), lambda i, t: (jnp.maximum(t - 1, 0), 0))
    in_specs = ([lat(D_MODEL), con(D_MODEL), mod_spec, modc_spec, _full((1, D_MODEL)), _full((1, 384)), _full((1, 256))]
                + [ANY] * 4 + [tab_spec] * 2)
    args = [x, ctx, mod, mod_c, g_attn, g_q, g_kv, *ws, *tabs]
    for a, off in flat[0]:
        in_specs.append(lat(a.shape[-1], off // tk))
        args.append(a)
    for a, off in flat[1]:
        assert off == 0
        in_specs.append(con(a.shape[-1]))
        args.append(a)
    in_specs.append(lat(D_MODEL))
    args.append(dx_res)
    out_shape = ([_sds((b, l, D_MODEL))] + [_sds(s, BF) for s in W_SHAPES] + [_sds(s) for s in acc_shapes[4:]]
                 + [_sds((b, 8, D_MODEL)), _sds((1, 8, D_MODEL))])
    out_specs = [lat(D_MODEL)] + [ANY] * N_ACC + [mod_spec, modc_spec]
    outs = pl.pallas_call(
        body, name="k1_bwd", grid=(b, nt + 1), in_specs=in_specs, out_specs=out_specs, out_shape=out_shape,
        scratch_shapes=[pltpu.VMEM(s, BF) for s in W_SHAPES] + [pltpu.VMEM(s, F32) for s in acc_shapes]
        + [pltpu.SemaphoreType.DMA((N_ACC,))],
        compiler_params=_cp((ARB, ARB)),
    )(*args)
    return outs[0], list(outs[1:1 + N_ACC]), outs[1 + N_ACC], outs[2 + N_ACC]


def _chunk_spec(rev):
    if rev:
        return pl.BlockSpec((None, RET_CHUNK, 512), lambda i, n: (i, N_CHUNK - 1 - n, 0))
    return pl.BlockSpec((None, RET_CHUNK, 512), lambda i, n: (i, n, 0))


def _state_spec(rev):
    if rev:
        return pl.BlockSpec((None, N_HEADS, None, LANE, LANE), lambda i, n: (i, 0, N_CHUNK - 1 - n, 0, 0))
    return pl.BlockSpec((None, N_HEADS, None, LANE, LANE), lambda i, n: (i, 0, n, 0, 0))


_CTX_SPEC = pl.BlockSpec((None, CTX_LEN, 512), lambda i, n: (i, 0, 0))
_DEC_SPEC = pl.BlockSpec((N_HEADS, 1, 1), lambda i, n: (0, 0, 0))


def _k2_fwd(rq, rk, rv, rkc, rvc, dec_f, dec_b):
    b = rq.shape[0]

    def body(qf, kf, vf, qb, kb, vb, kc, vc, df, db, of_ref, ob_ref, sf_out, sb_out, sf, sb):
        n = pl.program_id(1)
        for h, sl in enumerate(_HEAD_SL):
            lgf, lgb = log_sigmoid(df[h]), log_sigmoid(db[h])

            @pl.when(n == 0)
            def _():
                sf[h] = ctx_state(kc[:, sl], vc[:, sl], lgf, False)
                sb[h] = ctx_state(kc[:, sl], vc[:, sl], lgb, True)

            sf_out[h] = sf[h]
            sb_out[h] = sb[h]
            o, s = ret_chunk(qf[:, sl], kf[:, sl], vf[:, sl], sf[h], lgf, False)
            of_ref[:, sl] = o
            sf[h] = s
            o, s = ret_chunk(qb[:, sl], kb[:, sl], vb[:, sl], sb[h], lgb, True)
            ob_ref[:, sl] = o
            sb[h] = s

    l = rq.shape[1]
    return pl.pallas_call(
        body, name="k2_fwd", grid=(b, N_CHUNK),
        in_specs=[_chunk_spec(False)] * 3 + [_chunk_spec(True)] * 3 + [_CTX_SPEC, _CTX_SPEC, _DEC_SPEC, _DEC_SPEC],
        out_specs=[_chunk_spec(False), _chunk_spec(True), _state_spec(False), _state_spec(True)],
        out_shape=[_sds((b, l, 512)), _sds((b, l, 512)), _sds((b, N_HEADS, N_CHUNK, LANE, LANE)),
                   _sds((b, N_HEADS, N_CHUNK, LANE, LANE))],
        scratch_shapes=[pltpu.VMEM((N_HEADS, LANE, LANE), F32), pltpu.VMEM((N_HEADS, LANE, LANE), F32)],
        compiler_params=_cp((ARB, ARB)),
    )(rq, rk, rv, rq, rk, rv, rkc, rvc, dec_f, dec_b)


def _k2_bwd(rq, rk, rv, do, sf_prev, sb_prev, rkc, rvc, dec_f, dec_b):
    b, l, _ = rq.shape

    def body(qf, kf, vf, gf, spf, qb, kb, vb, gb, spb, kc, vc, df, db,
             dqf, dkf, dvf, dqb, dkb, dvb, dkc, dvc, ddf, ddb, dsf, dsb):
        n = pl.program_id(1)

        @pl.when(n == 0)
        def _():
            dsf[...] = jnp.zeros((N_HEADS, LANE, LANE), F32)
            dsb[...] = jnp.zeros((N_HEADS, LANE, LANE), F32)

        def one(h, sl, q, k, v, g, sp, dec, ds, dq, dk, dv, dd, rev):
            def f(qv, kv_, vv, sv, dcy):
                return ret_chunk(qv, kv_, vv, sv, log_sigmoid(dcy), rev)

            _, vjp = jax.vjp(f, q[:, sl], k[:, sl], v[:, sl], sp[h], dec[h])
            gq, gk, gv, gs, gd = vjp((g[:, sl], ds[h]))
            dq[:, sl] = gq
            dk[:, sl] = gk
            dv[:, sl] = gv
            ds[h] = gs
            _acc(dd.at[h], jnp.broadcast_to(gd, (8, LANE)), n == 0)

        for h, sl in enumerate(_HEAD_SL):
            one(h, sl, qf, kf, vf, gf, spf, df, dsf, dqf, dkf, dvf, ddf, False)
            one(h, sl, qb, kb, vb, gb, spb, db, dsb, dqb, dkb, dvb, ddb, True)

        @pl.when(n == N_CHUNK - 1)
        def _():
            def f(kcv, vcv, dcy, rev):
                return ctx_state(kcv, vcv, log_sigmoid(dcy), rev)

            for h, sl in enumerate(_HEAD_SL):
                _, vjp_f = jax.vjp(functools.partial(f, rev=False), kc[:, sl], vc[:, sl], df[h])
                gk_f, gv_f, gd_f = vjp_f(dsf[h])
                _, vjp_b = jax.vjp(functools.partial(f, rev=True), kc[:, sl], vc[:, sl], db[h])
                gk_b, gv_b, gd_b = vjp_b(dsb[h])
                dkc[:, sl] = gk_f + gk_b
                dvc[:, sl] = gv_f + gv_b
                ddf[h] += jnp.broadcast_to(gd_f, (8, LANE))
                ddb[h] += jnp.broadcast_to(gd_b, (8, LANE))

    dd_spec = pl.BlockSpec((None, N_HEADS, 8, LANE), lambda i, n: (i, 0, 0, 0))
    return pl.pallas_call(
        body, name="k2_bwd", grid=(b, N_CHUNK),
        in_specs=[_chunk_spec(True)] * 4 + [_state_spec(True)] + [_chunk_spec(False)] * 4 + [_state_spec(False)]
        + [_CTX_SPEC, _CTX_SPEC, _DEC_SPEC, _DEC_SPEC],
        out_specs=[_chunk_spec(True)] * 3 + [_chunk_spec(False)] * 3 + [_CTX_SPEC, _CTX_SPEC, dd_spec, dd_spec],
        out_shape=[_sds((b, l, 512))] * 6 + [_sds((b, CTX_LEN, 512))] * 2 + [_sds((b, N_HEADS, 8, LANE))] * 2,
        scratch_shapes=[pltpu.VMEM((N_HEADS, LANE, LANE), F32), pltpu.VMEM((N_HEADS, LANE, LANE), F32)],
        compiler_params=_cp((ARB, ARB)),
    )(rq, rk, rv, do, sf_prev, rq, rk, rv, do, sb_prev, rkc, rvc, dec_f, dec_b)


TQ = 512
QK_W = 2 * LANE
N_QP = 2
_Q_PARTS = [slice(i * TQ // N_QP, (i + 1) * TQ // N_QP) for i in range(N_QP)]


SM_SCALE = 1.0 / math.sqrt(192.0)


def _k3_specs():
    qs = lambda w: pl.BlockSpec((None, TQ, w), lambda i, h, t: (i, t, h))
    ks = lambda w: pl.BlockSpec((None, KV_LEN, w), lambda i, h, t: (i, 0, h))
    return qs, ks


def _k3_fwd(q, k, v):
    b, l, _ = q.shape

    def body(q_ref, k_ref, v_ref, o_ref, lse_ref):
        kv_, vv = k_ref[...], v_ref[...]
        for r in _Q_PARTS:
            s = _dot(q_ref[r, :], kv_, 1, 1) * SM_SCALE
            m = jnp.max(s, axis=-1, keepdims=True)
            e = jnp.exp(s - m)
            tot = jnp.sum(e, axis=-1, keepdims=True)
            o_ref[r, :] = _dot(e, vv, 1, 0) * (1.0 / tot)
            lse_ref[r, :] = jnp.broadcast_to(m + jnp.log(tot), (TQ // N_QP, LANE))

    qs, ks = _k3_specs()
    return pl.pallas_call(
        body, name="k3_fwd", grid=(b, N_HEADS, l // TQ), in_specs=[qs(QK_W), ks(QK_W), ks(LANE)],
        out_specs=[qs(LANE), qs(LANE)], out_shape=[_sds((b, l, N_HEADS * LANE))] * 2,
        compiler_params=_cp((ARB, ARB, ARB)),
    )(q, k, v)


def _k3_bwd(q, k, v, o, lse, dy, after):
    b, l, _ = q.shape

    def body(q_ref, k_ref, v_ref, o_ref, lse_ref, dy_ref, after_ref, dq_ref, dk_ref, dv_ref):
        t0 = pl.program_id(2) == 0
        kv_, vv = k_ref[...], v_ref[...]
        qv, dyv = q_ref[...], dy_ref[...]
        g = dyv.astype(BF)
        lse_col = jnp.max(lse_ref[...], axis=-1, keepdims=True)
        delta = jnp.sum(dyv * o_ref[...], axis=-1, keepdims=True)
        p = jnp.exp(_dot(qv, kv_, 1, 1) * SM_SCALE - lse_col)
        ds = (p * (_dot(g, vv, 1, 1) - delta) * SM_SCALE).astype(BF)
        _acc(dv_ref, _dot(p, g, 0, 0), t0)
        dq_ref[...] = _dot(ds, kv_, 1, 0)
        _acc(dk_ref, _dot(ds, qv, 0, 0), t0)

    qs, ks = _k3_specs()
    return pl.pallas_call(
        body, name="k3_bwd", grid=(b, N_HEADS, l // TQ),
        in_specs=[qs(QK_W), ks(QK_W), ks(LANE), qs(LANE), qs(LANE), qs(LANE), ANY],
        out_specs=[qs(QK_W), ks(QK_W), ks(LANE)],
        out_shape=[_sds((b, l, N_HEADS * QK_W)), _sds((b, KV_LEN, N_HEADS * QK_W)), _sds((b, KV_LEN, N_HEADS * LANE))],
        compiler_params=_cp((ARB, ARB, ARB)),
    )(q, k, v, o, lse, dy, after)


def _mod_rows(mod_ref, rows):
    return [mod_ref[r:r + 1, :] for r in rows]


def _k4a_fwd(x, o_f, o_b, rg, y_mla, g_ret, w_out, mod, g_ffn):
    b, l, _ = x.shape

    def body(x_ref, of_ref, ob_ref, rg_ref, ym_ref, gr_ref, wo_ref, mod_ref, gf_ref, xm_ref, h2_ref):
        gt_a, sh_f, sc_f = _mod_rows(mod_ref, (2, 3, 4))
        x_mid, h2 = k4a_tile(x_ref[...], of_ref[...], ob_ref[...], rg_ref[...], ym_ref[...], gr_ref[...], gt_a,
                             gf_ref[...], sh_f, sc_f, wo_ref[...], None)
        xm_ref[...] = x_mid
        h2_ref[...] = h2.astype(BF)

    tok = lambda w: pl.BlockSpec((None, TOK, w), lambda i, t: (i, t, 0))
    mod_spec = pl.BlockSpec((None, 8, D_MODEL), lambda i, t: (i, 0, 0))
    return pl.pallas_call(
        body, name="k4a_fwd", grid=(b, l // TOK),
        in_specs=[tok(D_MODEL), tok(512), tok(512), tok(512), tok(512), _full((1, 512)), _full((D_MODEL, D_MODEL)),
                  mod_spec, _full((1, D_MODEL))],
        out_specs=[tok(D_MODEL), tok(D_MODEL)], out_shape=[_sds((b, l, D_MODEL)), _sds((b, l, D_MODEL), BF)],
        compiler_params=_cp((ARB, ARB)),
    )(x, o_f, o_b, rg, y_mla, g_ret, w_out, mod, g_ffn)


TOK_M = 512
TOK_D = 1024
HALF_FF = D_FF // 2


def _k4b_mlp_loss(h2, w1t, w2, x_mid, mod, g_final, tgt):
    b, l, _ = h2.shape
    nt = l // TOK_M

    def body(h2_ref, w1_hbm, w2_hbm, xm_ref, mod_ref, gfin_ref, tgt_ref, dxm_ref, dmlp_ref, r_ref, loss_ref, dgt_ref,
             dgfin_ref, w1_v, w2_v):
        i, t = pl.program_id(0), pl.program_id(1)
        first = jnp.logical_and(i == 0, t == 0)

        @pl.when(first)
        def _():
            pltpu.sync_copy(w1_hbm, w1_v)
            pltpu.sync_copy(w2_hbm, w2_v)

        h2v = h2_ref[...]
        mlp = None
        for half in range(2):
            rows = slice(half * HALF_FF, (half + 1) * HALF_FF)
            r = jnp.maximum(_dot(h2v, w1_v[rows, :], 1, 1), 0.0)
            r_ref[:, rows] = r.astype(BF)
            part = _dot(jnp.square(r), w2_v[rows, :], 1, 0)
            mlp = part if mlp is None else mlp + part
        (gt_f,) = _mod_rows(mod_ref, (5,))
        loss, vjp = jax.vjp(k4c_tile, xm_ref[...], mlp, gt_f, gfin_ref[...], tgt_ref[...])
        dxm, dmlp, dgt, dgfin, _ = vjp(jnp.ones((1, 1), F32))
        dxm_ref[...] = dxm
        dmlp_ref[...] = dmlp.astype(BF)
        _acc(loss_ref, jnp.broadcast_to(loss, (8, LANE)), first)
        _acc(dgfin_ref, dgfin, first)
        _acc(dgt_ref, dgt, t == 0)

    tok = lambda w: pl.BlockSpec((None, TOK_M, w), lambda i, t: (i, t, 0))
    return pl.pallas_call(
        body, name="k4b_mlp_loss", grid=(b, nt),
        in_specs=[tok(D_MODEL), ANY, ANY, tok(D_MODEL), pl.BlockSpec((None, 8, D_MODEL), lambda i, t: (i, 0, 0)),
                  _full((1, D_MODEL)), tok(D_MODEL)],
        out_specs=[tok(D_MODEL), tok(D_MODEL), tok(D_FF), _full((8, LANE)),
                   pl.BlockSpec((None, 1, D_MODEL), lambda i, t: (i, 0, 0)), _full((1, D_MODEL))],
        out_shape=[_sds((b, l, D_MODEL)), _sds((b, l, D_MODEL), BF), _sds((b, l, D_FF), BF), _sds((8, LANE)),
                   _sds((b, 1, D_MODEL)), _sds((1, D_MODEL))],
        scratch_shapes=[pltpu.VMEM((D_FF, D_MODEL), BF), pltpu.VMEM((D_FF, D_MODEL), BF)],
        compiler_params=_cp((ARB, ARB)),
    )(h2, w1t, w2, x_mid, mod, g_final, tgt)


def _k4d_mlp_bwd(h2, dmlp, r, w2):
    b, l, _ = h2.shape
    nt = l // TOK_D

    def body(h2_ref, dm_ref, r_ref, w2_ref, da_ref, dw1_ref, dw2_ref, acc1, acc2):
        i, t = pl.program_id(1), pl.program_id(2)
        first = jnp.logical_and(i == 0, t == 0)
        rv = r_ref[...].astype(F32)
        dm = dm_ref[...]
        da = (_dot(dm, w2_ref[...], 1, 1) * (2.0 * rv)).astype(BF)
        da_ref[...] = da
        _acc(acc2, _dot(jnp.square(rv), dm, 0, 0), first)
        _acc(acc1, _dot(h2_ref[...], da, 0, 0), first)

        @pl.when(jnp.logical_and(i == b - 1, t == nt - 1))
        def _():
            dw1_ref[...] = acc1[...].astype(BF)
            dw2_ref[...] = acc2[...].astype(BF)

    tok = lambda w: pl.BlockSpec((None, TOK_D, w), lambda j, i, t: (i, t, 0))
    col = pl.BlockSpec((None, TOK_D, FF_BLK), lambda j, i, t: (i, t, j))
    return pl.pallas_call(
        body, name="k4d_mlp_bwd", grid=(N_DEV, b, nt),
        in_specs=[tok(D_MODEL), tok(D_MODEL), col, pl.BlockSpec((None, FF_BLK, D_MODEL), lambda j, i, t: (j, 0, 0))],
        out_specs=[col, pl.BlockSpec((None, D_MODEL, FF_BLK), lambda j, i, t: (j, 0, 0)),
                   pl.BlockSpec((None, FF_BLK, D_MODEL), lambda j, i, t: (j, 0, 0))],
        out_shape=[_sds((b, l, D_FF), BF), _sds((N_DEV, D_MODEL, FF_BLK), BF), _sds((N_DEV, FF_BLK, D_MODEL), BF)],
        scratch_shapes=[pltpu.VMEM((D_MODEL, FF_BLK), F32), pltpu.VMEM((FF_BLK, D_MODEL), F32)],
        compiler_params=_cp((ARB, ARB, ARB)),
    )(h2, dmlp, r, w2)


def _k4e_bwd(x, o_f, o_b, rg, y_mla, g_ret, w_out, mod, g_ffn, dxm, da, w1t):
    b, l, _ = x.shape

    def body(x_ref, of_ref, ob_ref, rg_ref, ym_ref, gr_ref, wo_ref, mod_ref, gf_ref, dxm_ref, da_ref, w1_hbm,
             dx_ref, do_ref, drg_ref, dym_ref, dwo_ref, dgr_ref, dgf_ref, dmod_ref, w1_v):
        i, t = pl.program_id(0), pl.program_id(1)
        first = jnp.logical_and(i == 0, t == 0)

        @pl.when(first)
        def _():
            pltpu.sync_copy(w1_hbm, w1_v)

        gt_a, sh_f, sc_f = _mod_rows(mod_ref, (2, 3, 4))
        wo = wo_ref[...]
        dh2 = _dot(da_ref[...], w1_v[...], 1, 0)

        def f(xv, ofv, rgv, ymv, grv, gta, gfv, shf, scf, p_out):
            return k4a_tile(xv, ofv, ob_ref[...], rgv, ymv, grv, gta, gfv, shf, scf, wo, p_out)

        _, vjp = jax.vjp(f, x_ref[...], of_ref[...], rg_ref[...], ym_ref[...], gr_ref[...], gt_a, gf_ref[...], sh_f,
                         sc_f, jnp.zeros((D_MODEL, D_MODEL), F32))
        dx, do, drg, dym, dgr, dgta, dgf, dshf, dscf, dwo = vjp((dxm_ref[...], dh2))
        dx_ref[...] = dx
        do_ref[...] = do
        drg_ref[...] = drg
        dym_ref[...] = dym
        _acc(dwo_ref, dwo, first)
        _acc(dgr_ref, dgr, first)
        _acc(dgf_ref, dgf, first)
        t0 = t == 0
        _acc(dmod_ref.at[2:3, :], dgta, t0)
        _acc(dmod_ref.at[3:4, :], dshf, t0)
        _acc(dmod_ref.at[4:5, :], dscf, t0)

        @pl.when(t0)
        def _():
            dmod_ref[0:2, :] = jnp.zeros((2, D_MODEL), F32)
            dmod_ref[5:8, :] = jnp.zeros((3, D_MODEL), F32)

    tok = lambda w: pl.BlockSpec((None, TOK_B, w), lambda i, t: (i, t, 0))
    mod_spec = pl.BlockSpec((None, 8, D_MODEL), lambda i, t: (i, 0, 0))
    return pl.pallas_call(
        body, name="k4e_bwd", grid=(b, l // TOK_B),
        in_specs=[tok(D_MODEL), tok(512), tok(512), tok(512), tok(512), _full((1, 512)), _full((D_MODEL, D_MODEL)),
                  mod_spec, _full((1, D_MODEL)), tok(D_MODEL), tok(D_FF), ANY],
        out_specs=[tok(D_MODEL), tok(512), tok(512), tok(512), _full((D_MODEL, D_MODEL)), _full((1, 512)),
                   _full((1, D_MODEL)), mod_spec],
        out_shape=[_sds((b, l, D_MODEL)), _sds((b, l, 512)), _sds((b, l, 512)), _sds((b, l, 512)),
                   _sds((D_MODEL, D_MODEL)), _sds((1, 512)), _sds((1, D_MODEL)), _sds((b, 8, D_MODEL))],
        scratch_shapes=[pltpu.VMEM((D_FF, D_MODEL), BF)],
        compiler_params=_cp((ARB, ARB)),
    )(x, o_f, o_b, rg, y_mla, g_ret, w_out, mod, g_ffn, dxm, da, w1t)


def _adamw(w, m, v, pieces, name, after=None):
    r, c = w.shape
    npc = pieces.shape[0]
    per_row = c * (7 * 4 + npc * pieces.dtype.itemsize) * 2
    rb = r
    for cand in (r, 512, 256, 128, 64, 32, 16, 8):
        if r % cand == 0 and cand * per_row <= 32 * 1024 * 1024:
            rb = cand
            break

    def body(w_ref, m_ref, v_ref, p_ref, *rest):
        g_ref, d_ref, nm_ref, nv_ref = rest[-4:]
        g = p_ref[0].astype(F32)
        for k in range(1, npc):
            g = g + p_ref[k].astype(F32)
        wv = w_ref[...]
        mn = ADAM_B1 * m_ref[...] + (1.0 - ADAM_B1) * g
        vn = ADAM_B2 * v_ref[...] + (1.0 - ADAM_B2) * jnp.square(g)
        m_hat = mn / (1.0 - ADAM_B1 ** ADAM_STEP)
        v_hat = vn / (1.0 - ADAM_B2 ** ADAM_STEP)
        g_ref[...] = g
        d_ref[...] = -ADAM_LR * (m_hat / (jnp.sqrt(v_hat) + ADAM_EPS) + ADAM_WD * wv)
        nm_ref[...] = mn
        nv_ref[...] = vn

    blk = pl.BlockSpec((rb, c), lambda i: (i, 0))
    extra = [] if after is None else [after]
    return pl.pallas_call(
        body, name=name, grid=(r // rb,),
        in_specs=[blk, blk, blk, pl.BlockSpec((npc, rb, c), lambda i: (0, i, 0))] + [ANY] * len(extra),
        out_specs=[blk] * 4, out_shape=[_sds((r, c))] * 4, compiler_params=_cp((ARB,)),
    )(w, m, v, pieces, *extra)


def _pad_head_rows(w, d):
    k = w.shape[1]
    return jnp.pad(w.reshape(N_HEADS, d, k), ((0, 0), (0, LANE - d), (0, 0))).reshape(N_HEADS * LANE, k)


def _cut_head_rows(g, d):
    k = g.shape[1]
    return g.reshape(N_HEADS, LANE, k)[:, :d].reshape(N_HEADS * d, k)


def _w_in_pad(wt):
    w_a = jnp.concatenate([_pad_head_rows(wt[0:256], 64), _pad_head_rows(wt[256:512], 64), wt[512:1536]], axis=0)
    w_b = jnp.concatenate([wt[1536:2176], jnp.pad(wt[2176:2240], ((0, 64), (0, 0)))], axis=0)
    return w_a, w_b


def _w_in_cut(g_a, g_b):
    return jnp.concatenate([_cut_head_rows(g_a[0:512], 64), _cut_head_rows(g_a[512:1024], 64), g_a[1024:2048],
                            g_b[0:704]], axis=0)


def _w_uq_pad(wt):
    return jnp.pad(wt.reshape(N_HEADS, 192, 384), ((0, 0), (0, 64), (0, 0))).reshape(1024, 384)


def _w_uq_cut(g):
    return g.reshape(N_HEADS, 256, 384)[:, :192].reshape(768, 384)


def _w_ukv_perm(wt):
    return jnp.transpose(wt.reshape(N_HEADS, 2, LANE, 256), (1, 0, 2, 3)).reshape(1024, 256)


def _w_ukv_unperm(g):
    return jnp.transpose(g.reshape(2, N_HEADS, LANE, 256), (1, 0, 2, 3)).reshape(1024, 256)


def _unshard_cols(g):
    return jnp.transpose(g, (1, 0, 2)).reshape(g.shape[1], N_DEV * g.shape[2])


def _rope_tables():
    rows = SEQ // GRID_W
    row = jnp.repeat(jnp.arange(rows, dtype=F32), GRID_W)
    col = jnp.tile(jnp.arange(GRID_W, dtype=F32), rows)
    freq = ROPE_BASE ** (-jnp.arange(16, dtype=F32) / 16)
    ang = jnp.concatenate([row[:, None] * freq, col[:, None] * freq], axis=-1)
    cos, sin = jnp.cos(ang), jnp.sin(ang)
    z = jnp.zeros((SEQ, 64), F32)
    return jnp.concatenate([cos, cos, z], axis=1), jnp.concatenate([-sin, sin, z], axis=1)


_PACKED = (("g_attn", 1024), ("g_ffn", 1024), ("ret_decay_fwd", 4), ("ret_decay_bwd", 4), ("g_ret", 512),
           ("g_q_lora", 384), ("g_kv_lora", 256), ("g_final", 1024))
_PACK_OFF = {}
_off = 0
for _name, _n in _PACKED:
    _PACK_OFF[_name] = _off
    _off += -(-_n // LANE) * LANE
PACK_W = _off


def _pack_small(vals):
    parts = []
    for name, n in _PACKED:
        a = vals[name].reshape(-1).astype(F32)
        parts.append(jnp.pad(a, (0, -(-n // LANE) * LANE - n)))
    return jnp.concatenate(parts).reshape(1, PACK_W)


def _adamw_small(params, packed, gcc, gb_ada):
    names = list(params)
    n_p = len(names)

    def body(*refs):
        p_ref, gcc_ref, gb_ref = refs[3 * n_p:3 * n_p + 3]
        outs = refs[3 * n_p + 3:]
        for k, name in enumerate(names):
            w_ref, m_ref, v_ref = refs[3 * k:3 * k + 3]
            n = w_ref.shape[1]
            if name == "b_ada":
                g = jnp.concatenate([gb_ref[d, 0:1, :] for d in range(N_DEV)], axis=-1)
            elif name == "c_ctx":
                g = gcc_ref[0, 0:1, :]
                for d in range(1, N_DEV):
                    g = g + gcc_ref[d, 0:1, :]
            else:
                off = _PACK_OFF[name]
                g = p_ref[0, :, off:off + n]
                for d in range(1, N_DEV):
                    g = g + p_ref[d, :, off:off + n]
            mn = ADAM_B1 * m_ref[...] + (1.0 - ADAM_B1) * g
            vn = ADAM_B2 * v_ref[...] + (1.0 - ADAM_B2) * jnp.square(g)
            m_hat = mn / (1.0 - ADAM_B1 ** ADAM_STEP)
            v_hat = vn / (1.0 - ADAM_B2 ** ADAM_STEP)
            outs[4 * k][...] = g
            outs[4 * k + 1][...] = -ADAM_LR * (m_hat / (jnp.sqrt(v_hat) + ADAM_EPS) + ADAM_WD * w_ref[...])
            outs[4 * k + 2][...] = mn
            outs[4 * k + 3][...] = vn

    args = [a for name in names for a in params[name]] + [packed, gcc, gb_ada]
    out_shape = [_sds(params[name][0].shape) for name in names for _ in range(4)]
    outs = pl.pallas_call(body, name="adamw_small", out_shape=out_shape, compiler_params=_cp())(*args)
    return {name: list(outs[4 * k:4 * k + 4]) for k, name in enumerate(names)}


def kernel(x, c, ctx, c_ctx, w_ada, b_ada, g_attn, g_ffn, w_in, ret_decay_fwd, ret_decay_bwd, g_ret, g_q_lora, w_uq, g_kv_lora, w_ukv, w_out, w_ff1, w_ff2, g_final, loss_target, m_c_ctx, m_w_ada, m_b_ada, m_g_attn, m_g_ffn, m_w_in, m_ret_decay_fwd, m_ret_decay_bwd, m_g_ret, m_g_q_lora, m_w_uq, m_g_kv_lora, m_w_ukv, m_w_out, m_w_ff1, m_w_ff2, m_g_final, v_c_ctx, v_w_ada, v_b_ada, v_g_attn, v_g_ffn, v_w_in, v_ret_decay_fwd, v_ret_decay_bwd, v_g_ret, v_g_q_lora, v_w_uq, v_g_kv_lora, v_w_ukv, v_w_out, v_w_ff1, v_w_ff2, v_g_final):
    me = 4 * lax.axis_index("x") + 2 * lax.axis_index("y") + lax.axis_index("c")
    nb = x.shape[0]

    c_pad = jnp.pad(c, ((0, 8 - nb), (0, 0)))
    c_all, g_in, g_uq, g_ukv = _gather_two_level(
        [c_pad, w_in[0].T.astype(BF), w_uq[0].T.astype(BF), w_ukv[0].T.astype(BF)], "gather_weights")
    ws = (*_w_in_pad(g_in.reshape(2240, D_MODEL)), _w_uq_pad(g_uq.reshape(768, 384)),
          _w_ukv_perm(g_ukv.reshape(1024, 256)))

    crows = jnp.concatenate([c_all[:, :nb].reshape(N_DEV * nb, D_MODEL), c_ctx[None], jnp.zeros((7, D_MODEL), F32)])
    b_blk = lax.dynamic_slice(b_ada, (0, me * 768), (1, 768))
    (mod_g,) = _exchange([_mod_fwd(crows, w_ada[0], b_blk)], True, "gather_mod")
    mod_all = _unshard_cols(mod_g)
    behind = mod_g[0, 0, 0:1] * 0.0
    st_o = _exchange_start([(w_out[0] + behind).astype(BF)], True, "gather_wo_start")
    st_g = _exchange_start([w_ff1[0].T.astype(BF), w_ff2[0].astype(BF)], True, "gather_ff_start", after=st_o["token"])
    mod_all = mod_all + st_g["token"][0:1, 0:1]
    mod_mine = lax.dynamic_slice(mod_all, (me * nb, 0), (nb, 6 * D_MODEL)).reshape(nb, 6, D_MODEL)
    mod = jnp.pad(mod_mine, ((0, 0), (0, 2), (0, 0)))
    mod_c = jnp.pad(mod_all[16].reshape(1, 6, D_MODEL), ((0, 0), (0, 2), (0, 0)))

    tabs = _rope_tables()
    dec_f = ret_decay_fwd.reshape(N_HEADS, 1, 1)
    dec_b = ret_decay_bwd.reshape(N_HEADS, 1, 1)

    rkc, rvc, k_ctx, v_ctx = _k1_fwd(ctx, mod_c, g_attn, g_q_lora, g_kv_lora, ws, tabs, None, True)
    rq, rk, rv, rg, q, k_all, v_all = _k1_fwd(x, mod, g_attn, g_q_lora, g_kv_lora, ws, tabs, (k_ctx, v_ctx), False)
    o_f, o_b, sf_prev, sb_prev = _k2_fwd(rq, rk, rv, rkc, rvc, dec_f, dec_b)
    y_mla, lse = _k3_fwd(q, k_all, v_all)
    (g_out,) = _exchange_wait(st_o, y_mla, "gather_wo_wait")
    wo = g_out.reshape(D_MODEL, D_MODEL)
    x_mid, h2 = _k4a_fwd(x, o_f, o_b, rg, y_mla, g_ret, wo, mod, g_ffn)
    g_ff1t, g_ff2 = _exchange_wait(st_g, x_mid, "gather_ff_wait")
    w1t = g_ff1t.reshape(D_FF, D_MODEL)
    dxm, dmlp, relu_a, loss_acc, dgt_f, dg_final = _k4b_mlp_loss(h2, w1t, g_ff2.reshape(D_FF, D_MODEL), x_mid, mod,
                                                                 g_final.reshape(1, D_MODEL), loss_target)

    da, dw1, dw2 = _k4d_mlp_bwd(h2, dmlp, relu_a, g_ff2)
    st_s = _exchange_start([dw1, dw2], False, "scatter_ff_start")
    g_ret_t = g_ret + st_s["token"][0:1, 0:1]
    dx_res, do, drg, dym, dwo, dg_ret, dg_ffn, dmod_a = _k4e_bwd(x, o_f, o_b, rg, y_mla, g_ret_t, wo, mod, g_ffn, dxm, da,
                                                                 w1t)
    st_w = _exchange_start([dwo.reshape(N_DEV, 128, D_MODEL).astype(BF)], False, "scatter_wo_start")
    dq, dk_all, dv_all = _k3_bwd(q, k_all, v_all, y_mla, lse, dym, st_w["token"])
    dqf, dkf, dvf, dqb, dkb, dvb, dkc, dvc, ddf, ddb = _k2_bwd(rq, rk, rv, do, sf_prev, sb_prev, rkc, rvc, dec_f, dec_b)
    cts = [[(dqf, 0), (dqb, 0)], [(dkf, 0), (dkb, 0)], [(dvf, 0), (dvb, 0)], [(drg, 0)], [(dq, 0)],
           [(dk_all, CTX_LEN)], [(dv_all, CTX_LEN)]]
    cts_c = [[(dkc, 0)], [(dvc, 0)], [(dk_all, 0)], [(dv_all, 0)]]
    grad_x, accs, dmod_1, dmod_c1 = _k1_bwd(x, ctx, mod, mod_c, g_attn, g_q_lora, g_kv_lora, ws, tabs, cts, cts_c,
                                            dx_res)
    dwa, dwb, dwq, dwk, dg_attn, dg_q, dg_kv = accs

    dmod_loc = (dmod_a + dmod_1).at[:, 5, :].set(dgt_f[:, 0, :])[:, :6, :].reshape(nb, 6 * D_MODEL)
    dmod_ctx = dmod_c1[:, :6, :].reshape(1, 6 * D_MODEL)
    small = {"g_attn": dg_attn, "g_ffn": dg_ffn, "ret_decay_fwd": jnp.sum(ddf[:, :, 0, 0], axis=0),
             "ret_decay_bwd": jnp.sum(ddb[:, :, 0, 0], axis=0), "g_ret": dg_ret, "g_q_lora": dg_q, "g_kv_lora": dg_kv,
             "g_final": dg_final}
    extra = jnp.concatenate([dmod_loc, dmod_ctx, jnp.zeros((5, 6 * D_MODEL), F32)])
    ex_pieces = jnp.transpose(extra.reshape(8, N_DEV, 768), (1, 0, 2))
    sm_g, ex_g, loss_g = _exchange([_pack_small(small), ex_pieces, loss_acc], [True, False, True], "gather_small")
    dmod_blk = jnp.concatenate([ex_g[:, :nb].reshape(N_DEV * nb, 768), jnp.zeros((8, 768), F32)])
    gw_ada, gcc_part, gb_part = _mod_bwd(crows, w_ada[0], dmod_blk, ex_g[:, nb])
    gcc_g, gb_g = _exchange([gcc_part, gb_part], True, "gather_c_ctx")

    p_ff1, p_ff2 = _exchange_wait(st_s, gcc_g, "scatter_ff_wait")
    (p_wo,) = _exchange_wait(st_w, p_ff1, "scatter_wo_wait")
    st_r = _exchange_start([_w_in_cut(dwa, dwb).reshape(N_DEV, 280, D_MODEL), _w_uq_cut(dwq).reshape(N_DEV, 96, 384),
                            _w_ukv_unperm(dwk).reshape(N_DEV, 128, 256)], False, "scatter_rest_start",
                           after=p_wo)

    res = {}
    early = (("w_ff1", w_ff1, m_w_ff1, v_w_ff1, p_ff1), ("w_ff2", w_ff2, m_w_ff2, v_w_ff2, p_ff2),
             ("w_ada", w_ada, m_w_ada, v_w_ada, gw_ada[None]))
    for name, w, m, v, pcs in early:
        res[name] = [a[None] for a in _adamw(w[0], m[0], v[0], pcs, "adamw_" + name, after=st_r["token"])]
    pieces = _exchange_wait(st_r, res["w_ada"][3], "scatter_rest_wait")
    for name, w, m, v, pcs in (("w_in", w_in, m_w_in, v_w_in, pieces[0]), ("w_uq", w_uq, m_w_uq, v_w_uq, pieces[1])):
        res[name] = [a.T[None] for a in _adamw(w[0].T, m[0].T, v[0].T, pcs, "adamw_" + name)]
    late = (("w_ukv", w_ukv, m_w_ukv, v_w_ukv, jnp.transpose(pieces[2], (0, 2, 1))),
            ("w_out", w_out, m_w_out, v_w_out, p_wo))
    for name, w, m, v, pcs in late:
        res[name] = [a[None] for a in _adamw(w[0], m[0], v[0], pcs, "adamw_" + name)]

    smalls = {"c_ctx": (c_ctx, m_c_ctx, v_c_ctx), "b_ada": (b_ada, m_b_ada, v_b_ada), "g_attn": (g_attn, m_g_attn, v_g_attn),
              "g_ffn": (g_ffn, m_g_ffn, v_g_ffn), "ret_decay_fwd": (ret_decay_fwd, m_ret_decay_fwd, v_ret_decay_fwd),
              "ret_decay_bwd": (ret_decay_bwd, m_ret_decay_bwd, v_ret_decay_bwd), "g_ret": (g_ret, m_g_ret, v_g_ret),
              "g_q_lora": (g_q_lora, m_g_q_lora, v_g_q_lora), "g_kv_lora": (g_kv_lora, m_g_kv_lora, v_g_kv_lora),
              "g_final": (g_final, m_g_final, v_g_final)}
    rows = {k: tuple(a.reshape(1, -1) for a in t) for k, t in smalls.items()}
    for name, outs in _adamw_small(rows, sm_g, gcc_g, gb_g).items():
        res[name] = [o.reshape(smalls[name][0].shape) for o in outs]

    loss = loss_g[0, 0, 0]
    for k in range(1, N_DEV):
        loss = loss + loss_g[k, 0, 0]

    order = ("c_ctx", "w_ada", "b_ada", "g_attn", "g_ffn", "w_in", "ret_decay_fwd", "ret_decay_bwd", "g_ret", "g_q_lora",
             "w_uq", "g_kv_lora", "w_ukv", "w_out", "w_ff1", "w_ff2", "g_final")
    return (loss, grad_x, *[res[n][0] for n in order], *[res[n][1] for n in order], *[res[n][2] for n in order],
            *[res[n][3] for n in order])
```

```python
import functools
import math

import jax
import jax.numpy as jnp
from jax import lax
from jax.experimental import pallas as pl
from jax.experimental.pallas import tpu as pltpu

F32 = jnp.float32
BF = jnp.bfloat16
EPS = 1e-6
LANE = 128
N_DEV = 8
D_MODEL = 1024
SEQ = 2048
CTX_LEN = 256
GRID_W = 64
N_HEADS = 4
RET_CHUNK = 512
N_CHUNK = SEQ // RET_CHUNK
D_FF = 4096
FF_BLK = D_FF // N_DEV
IN_PAD = 2816
KV_LEN = CTX_LEN + SEQ
ROPE_BASE = 10000.0
ADAM_LR, ADAM_B1, ADAM_B2, ADAM_EPS, ADAM_WD, ADAM_STEP = 0.001, 0.9, 0.999, 1e-08, 0.01, 10
TOK = 256
TOK_B = 256
VMEM_LIMIT = 56 * 1024 * 1024
ARB = "arbitrary"
MESH = pl.DeviceIdType.MESH
_HEAD_SL = [slice(LANE * h, LANE * (h + 1)) for h in range(N_HEADS)]
W_SHAPES = [(2048, D_MODEL), (768, D_MODEL), (1024, 384), (1024, 256)]


def _dot(a, b, ca, cb):
    return lax.dot_general(a.astype(BF), b.astype(BF), (((ca,), (cb,)), ((), ())), preferred_element_type=F32)


@jax.custom_vjp
def mm(a, b):
    return _dot(a, b, 1, 0)


@jax.custom_vjp
def mm_nt(a, b):
    return _dot(a, b, 1, 1)


@jax.custom_vjp
def mm_tn(a, b):
    return _dot(a, b, 0, 0)


mm.defvjp(lambda a, b: (_dot(a, b, 1, 0), (a, b)), lambda r, g: (mm_nt(g, r[1]), mm_tn(r[0], g)))
mm_nt.defvjp(lambda a, b: (_dot(a, b, 1, 1), (a, b)), lambda r, g: (mm(g, r[1]), mm_tn(g, r[0])))
mm_tn.defvjp(lambda a, b: (_dot(a, b, 0, 0), (a, b)), lambda r, g: (mm_nt(r[1], g), mm(r[0], g)))


@jax.custom_vjp
def _mmw(a, w, probe):
    return _dot(a, w, 1, 0)


def _mmw_bwd(r, g):
    a, w = r
    return mm_nt(g, w), jnp.zeros_like(w), mm_tn(a, g)


_mmw.defvjp(lambda a, w, probe: (_dot(a, w, 1, 0), (a, w)), _mmw_bwd)


@jax.custom_vjp
def _mmwt(a, wt, probe):
    return _dot(a, wt, 1, 1)


_mmwt.defvjp(lambda a, wt, probe: (_dot(a, wt, 1, 1), (a, wt)),
             lambda r, g: (mm(g, r[1]), jnp.zeros_like(r[1]), mm_tn(g, r[0])))


def mmwt(a, wt, probe):
    return _dot(a, wt, 1, 1) if probe is None else _mmwt(a, wt, probe)


def mmw(a, w, probe):
    return _dot(a, w, 1, 0) if probe is None else _mmw(a, w, probe)


def rmsn(x, g):
    return x * lax.rsqrt(jnp.mean(x * x, axis=-1, keepdims=True) + EPS) * g


def silu(x):
    return x * jax.nn.sigmoid(x)


def _swap32_impl(x):
    n = x.shape[-1]
    lane = lax.broadcasted_iota(jnp.int32, x.shape, x.ndim - 1) % LANE
    up = pltpu.roll(x, n - 32, x.ndim - 1)
    dn = pltpu.roll(x, 32, x.ndim - 1)
    return jnp.where(lane < 32, up, jnp.where(lane < 64, dn, 0.0))


@jax.custom_vjp
def swap32(x):
    return _swap32_impl(x)


swap32.defvjp(lambda x: (_swap32_impl(x), None), lambda _, g: (_swap32_impl(g),))


def rope(x, cs, sn):
    return x * cs + swap32(x) * sn


def k1_tile(x, sh, sc, g_attn, g_q, g_kv, ws, ps, tabs, is_ctx):
    w_a, w_b, w_uq, w_ukv = ws
    p_a, p_b, p_uq, p_ukv = ps
    cs1, sn1 = tabs
    cs, sn = jnp.concatenate([cs1] * N_HEADS, axis=-1), jnp.concatenate([sn1] * N_HEADS, axis=-1)
    cq_t = jnp.concatenate([jnp.ones_like(cs1), cs1] * N_HEADS, axis=-1)
    sq_t = jnp.concatenate([jnp.zeros_like(sn1), sn1] * N_HEADS, axis=-1)
    h = rmsn(x, g_attn) * (1.0 + sc) + sh
    pa = mmwt(h, w_a, p_a)
    pb = mmwt(h, w_b, p_b)
    rk = pa[:, 512:1024] * 0.125
    rv = pa[:, 1024:1536]
    kpe = pb[:, 640:768]
    kv = mmwt(rmsn(pb[:, 384:640], g_kv), w_ukv, p_ukv)
    if not is_ctx:
        rk = rope(rk, cs, sn)
        kpe = rope(kpe, cs1, sn1)
    k_full = jnp.concatenate([piece for sl in _HEAD_SL for piece in (kv[:, sl], kpe)], axis=-1)
    v = kv[:, 512:]
    if is_ctx:
        return rk, rv, k_full, v
    rq = rope(pa[:, 0:512], cs, sn)
    rg = pa[:, 1536:2048]
    q = rope(mmwt(rmsn(pb[:, 0:384], g_q), w_uq, p_uq), cq_t, sq_t)
    return rq, rk, rv, rg, q, k_full, v


def log_sigmoid(x):
    return jnp.minimum(x, 0.0) - jnp.log(1.0 + jnp.exp(-jnp.abs(x)))


def ret_chunk(q, k, v, s, lg, reverse):
    c = RET_CHUNK
    ii = lax.broadcasted_iota(jnp.int32, (c, c), 0).astype(F32)
    jj = lax.broadcasted_iota(jnp.int32, (c, c), 1).astype(F32)
    diff = (jj - ii) if reverse else (ii - jj)
    dec = jnp.where(diff >= 0, jnp.exp(lg * jnp.maximum(diff, 0.0)), 0.0)
    pos = lax.broadcasted_iota(jnp.int32, (c, 1), 0).astype(F32)
    if reverse:
        wk, wq = jnp.exp(lg * pos), jnp.exp(lg * (c - pos))
    else:
        wk, wq = jnp.exp(lg * (c - 1.0 - pos)), jnp.exp(lg * (pos + 1.0))
    o = mm(mm_nt(q, k) * dec, v) + mm(q * wq, s)
    s_next = jnp.exp(lg * float(c)) * s + mm_tn(k * wk, v)
    return o, s_next


def ctx_state(kc, vc, lg, reverse):
    n = kc.shape[0]
    pos = lax.broadcasted_iota(jnp.int32, (n, 1), 0).astype(F32)
    w = jnp.exp(lg * pos) if reverse else jnp.exp(lg * (n - 1.0 - pos))
    return mm_tn(kc * w, vc)


def attn_head(qn, qp, kn, kp, v):
    s = (mm_nt(qn, kn) + mm_nt(qp, kp)) * (1.0 / math.sqrt(192.0))
    e = jnp.exp(s - jnp.max(s, axis=-1, keepdims=True))
    return mm(e / jnp.sum(e, axis=-1, keepdims=True), v)


def gn_gate(o, rg, g_ret):
    ys = []
    for h in range(N_HEADS):
        sl = slice(LANE * h, LANE * (h + 1))
        oh = o[:, sl]
        mu = jnp.mean(oh, axis=-1, keepdims=True)
        var = jnp.mean(jnp.square(oh - mu), axis=-1, keepdims=True)
        ys.append((oh - mu) * lax.rsqrt(var + EPS) * g_ret[:, sl])
    return jnp.concatenate(ys, axis=-1) * silu(rg)


def k4a_tile(x, o_f, o_b, rg, y_mla, g_ret, gt_a, g_ffn, sh_f, sc_f, w_out, p_out):
    mix = jnp.concatenate([gn_gate(o_f + o_b, rg, g_ret), y_mla], axis=-1)
    x_mid = x + gt_a * mmw(mix, w_out, p_out)
    h2 = rmsn(x_mid, g_ffn) * (1.0 + sc_f) + sh_f
    return x_mid, h2


def k4c_tile(x_mid, mlp, gt_f, g_final, tgt):
    y = rmsn(x_mid + gt_f * mlp, g_final)
    per_tok = jnp.mean(jnp.square(y - tgt), axis=-1, keepdims=True)
    return 0.5 * jnp.sum(per_tok, axis=0, keepdims=True)


def _cp(sem=None, vmem=VMEM_LIMIT):
    return pltpu.CompilerParams(dimension_semantics=sem, vmem_limit_bytes=vmem)


def _acc(ref, val, first):
    @pl.when(first)
    def _():
        ref[...] = val

    @pl.when(jnp.logical_not(first))
    def _():
        ref[...] += val


def _full(shape):
    nd = len(shape)
    return pl.BlockSpec(shape, lambda *_: (0,) * nd)


ANY = pl.BlockSpec(memory_space=pl.ANY)


def _sds(shape, dtype=F32):
    return jax.ShapeDtypeStruct(shape, dtype)


def _exchange(arrs, gather, name):
    n = len(arrs)
    modes = [gather] * n if isinstance(gather, bool) else list(gather)
    out_shape = [_sds(((N_DEV,) + a.shape) if g else a.shape, a.dtype) for a, g in zip(arrs, modes)]

    def body(*refs):
        ins, outs = refs[:n], refs[n:2 * n]
        send_sems, recv_sems, local_sems = refs[2 * n:]
        x, y, c = lax.axis_index("x"), lax.axis_index("y"), lax.axis_index("c")
        me = 4 * x + 2 * y + c
        sends, recvs, locs = [], [], []
        for i in range(n):
            gather = modes[i]
            for k in range(N_DEV - 1):
                bits = k + 1
                px = x ^ ((bits >> 2) & 1)
                py = y ^ ((bits >> 1) & 1)
                pc = c ^ (bits & 1)
                peer = 4 * px + 2 * py + pc
                src = ins[i] if gather else ins[i].at[peer]
                sem = i * (N_DEV - 1) + k
                sends.append(pltpu.make_async_remote_copy(
                    src_ref=src, dst_ref=outs[i].at[me], send_sem=send_sems.at[sem], recv_sem=recv_sems.at[sem],
                    device_id=(px, py, pc), device_id_type=MESH))
                recvs.append(pltpu.make_async_remote_copy(
                    src_ref=src, dst_ref=outs[i].at[peer], send_sem=send_sems.at[sem], recv_sem=recv_sems.at[sem],
                    device_id=(px, py, pc), device_id_type=MESH))
            locs.append(pltpu.make_async_copy(ins[i] if gather else ins[i].at[me], outs[i].at[me], local_sems.at[i]))
        for cp in locs + sends:
            cp.start()
        for cp in recvs:
            cp.wait_recv()
        for cp in sends:
            cp.wait_send()
        for cp in locs:
            cp.wait()

    outs = pl.pallas_call(
        body, name=name, out_shape=out_shape, in_specs=[ANY] * n, out_specs=[ANY] * n,
        scratch_shapes=[pltpu.SemaphoreType.DMA((n * (N_DEV - 1),)), pltpu.SemaphoreType.DMA((n * (N_DEV - 1),)),
                        pltpu.SemaphoreType.DMA((n,))],
    )(*arrs)
    return list(outs)


def _gather_two_level(arrs, name):
    n = len(arrs)

    def body(*refs):
        ins, outs = refs[:n], refs[n:2 * n]
        send_sems, recv_sems, local_sems = refs[2 * n:]
        x, y, c = lax.axis_index("x"), lax.axis_index("y"), lax.axis_index("c")
        sibling = (x, y, 1 - c)
        chips = [(1 - x, y), (x, 1 - y), (1 - x, 1 - y)]

        def slot(px, py, pc):
            return 4 * px + 2 * py + pc

        first, passed, waits, locs = [], [], [], []
        for i in range(n):
            def copy(k, block, to, src=None, i=i):
                dst = outs[i].at[slot(*block)]
                return pltpu.make_async_remote_copy(
                    src_ref=dst if src is None else src, dst_ref=dst, send_sem=send_sems.at[7 * i + k],
                    recv_sem=recv_sems.at[7 * i + k], device_id=to, device_id_type=MESH)

            locs.append(pltpu.make_async_copy(ins[i], outs[i].at[slot(x, y, c)], local_sems.at[i]))
            first.append(copy(0, (x, y, c), sibling, src=ins[i]))
            first += [copy(1 + j, (x, y, c), (*chip, c), src=ins[i]) for j, chip in enumerate(chips)]
            passed.append([copy(4 + j, (*chip, c), sibling) for j, chip in enumerate(chips)])
            waits.append([copy(1 + j, (*chip, c), (x, y, c)) for j, chip in enumerate(chips)])
        for cp in locs + first:
            cp.start()
        for j in range(3):
            for i in range(n):
                waits[i][j].wait_recv()
                passed[i][j].start()
        for i in range(n):
            def arrival(k, block, i=i):
                dst = outs[i].at[slot(*block)]
                return pltpu.make_async_remote_copy(
                    src_ref=dst, dst_ref=dst, send_sem=send_sems.at[7 * i + k], recv_sem=recv_sems.at[7 * i + k],
                    device_id=sibling, device_id_type=MESH)

            arrival(0, (x, y, 1 - c)).wait_recv()
            for j, chip in enumerate(chips):
                arrival(4 + j, (*chip, 1 - c)).wait_recv()
        for cp in first + [p for ps in passed for p in ps]:
            cp.wait_send()
        for cp in locs:
            cp.wait()

    outs = pl.pallas_call(
        body, name=name, out_shape=[_sds((N_DEV,) + a.shape, a.dtype) for a in arrs], in_specs=[ANY] * n,
        out_specs=[ANY] * n,
        scratch_shapes=[pltpu.SemaphoreType.DMA((7 * n,)), pltpu.SemaphoreType.DMA((7 * n,)),
                        pltpu.SemaphoreType.DMA((n,))],
    )(*arrs)
    return list(outs)


HBM = pl.BlockSpec(memory_space=pltpu.HBM)
SEM = pl.BlockSpec(memory_space=pltpu.SEMAPHORE)
EFFECT = pltpu.SideEffectType.DATAFLOW_SIDE_EFFECTING


def _peer(k, chips=False):
    x, y, c = lax.axis_index("x"), lax.axis_index("y"), lax.axis_index("c")
    bits = (k + 1) << 1 if chips else k + 1
    px, py, pc = x ^ ((bits >> 2) & 1), y ^ ((bits >> 1) & 1), c ^ (bits & 1)
    if chips:
        return (px, py, pc), 2 * px + py, 2 * x + y
    return (px, py, pc), 4 * px + 2 * py + pc, 4 * x + 2 * y + c


def _exchange_start(arrs, gather, name, after=None, chips=False):
    n = len(arrs)
    n_peer, n_slot = (3, 4) if chips else (N_DEV - 1, N_DEV)
    lands = [pltpu.with_memory_space_constraint(lax.empty(((n_slot,) + a.shape) if gather else a.shape, a.dtype),
                                                pltpu.HBM) for a in arrs]
    srcs = [pltpu.with_memory_space_constraint(a, pltpu.HBM) for a in arrs]

    extra = [] if after is None else [after]

    def body(*refs):
        ins, zones = refs[:n], refs[n:2 * n]
        send_sems, recv_sems, local_sems = refs[2 * n + len(extra):2 * n + len(extra) + 3]
        token = refs[-1]
        for i in range(n):
            for k in range(n_peer):
                dev, peer, me = _peer(k, chips)
                sem = i * n_peer + k
                pltpu.make_async_remote_copy(
                    src_ref=ins[i] if gather else ins[i].at[peer], dst_ref=zones[i].at[me],
                    send_sem=send_sems.at[sem], recv_sem=recv_sems.at[sem], device_id=dev, device_id_type=MESH).start()
            _, _, me = _peer(0, chips)
            pltpu.make_async_copy(ins[i] if gather else ins[i].at[me], zones[i].at[me], local_sems.at[i]).start()
        token[...] = jnp.zeros_like(token)

    nsem = n * n_peer
    outs = pl.pallas_call(
        body, name=name,
        out_shape=[pltpu.SemaphoreType.DMA((nsem,)), pltpu.SemaphoreType.DMA((nsem,)), pltpu.SemaphoreType.DMA((n,))]
        + [pltpu.HBM(a.shape, a.dtype) for a in srcs] + [pltpu.HBM(z.shape, z.dtype) for z in lands]
        + [_sds((8, LANE))],
        in_specs=[HBM] * (2 * n) + [ANY] * len(extra),
        out_specs=[SEM, SEM, SEM] + [HBM] * (2 * n) + [pl.BlockSpec(memory_space=pltpu.VMEM)],
        input_output_aliases={i: 3 + i for i in range(2 * n)},
        compiler_params=pltpu.CompilerParams(has_side_effects=EFFECT),
    )(*srcs, *lands, *extra)
    return {"n": n, "gather": gather, "chips": chips, "sems": outs[:3], "srcs": outs[3:3 + n],
            "lands": outs[3 + n:3 + 2 * n], "token": outs[-1]}


def _exchange_wait(st, after, name):
    n, gather, chips = st["n"], st["gather"], st["chips"]
    n_peer = 3 if chips else N_DEV - 1

    def body(*refs):
        ins, zones = refs[:n], refs[n:2 * n]
        send_sems, recv_sems, local_sems = refs[2 * n:2 * n + 3]
        for i in range(n):
            for k in range(n_peer):
                dev, peer, me = _peer(k, chips)
                sem = i * n_peer + k
                src = ins[i] if gather else ins[i].at[peer]
                cp = pltpu.make_async_remote_copy(
                    src_ref=src, dst_ref=zones[i].at[peer], send_sem=send_sems.at[sem], recv_sem=recv_sems.at[sem],
                    device_id=dev, device_id_type=MESH)
                cp.wait_send()
                cp.wait_recv()
            _, _, me = _peer(0, chips)
            pltpu.make_async_copy(ins[i] if gather else ins[i].at[me], zones[i].at[me], local_sems.at[i]).wait()

    outs = pl.pallas_call(
        body, name=name,
        out_shape=[pltpu.HBM(a.shape, a.dtype) for a in st["srcs"]] + [pltpu.HBM(z.shape, z.dtype) for z in st["lands"]],
        in_specs=[HBM] * (2 * n) + [SEM, SEM, SEM, ANY], out_specs=[HBM] * (2 * n),
        input_output_aliases={i: i for i in range(2 * n)},
        compiler_params=pltpu.CompilerParams(has_side_effects=EFFECT),
    )(*st["srcs"], *st["lands"], *st["sems"], after)
    return list(outs[n:])


def _pair_reduce(arrs, name):
    n = len(arrs)

    def body(*refs):
        ins, outs, got, mine = refs[:n], refs[n:2 * n], refs[2 * n:3 * n], refs[3 * n:4 * n]
        send_sems, recv_sems, local_sems = refs[4 * n:]
        x, y, c = lax.axis_index("x"), lax.axis_index("y"), lax.axis_index("c")
        sends, locs = [], []
        for i in range(n):
            for q in range(4):
                sem = 4 * i + q
                sends.append(pltpu.make_async_remote_copy(
                    src_ref=ins[i].at[2 * q + 1 - c], dst_ref=got[i].at[q], send_sem=send_sems.at[sem],
                    recv_sem=recv_sems.at[sem], device_id=(x, y, 1 - c), device_id_type=MESH))
                locs.append(pltpu.make_async_copy(ins[i].at[2 * q + c], mine[i].at[q], local_sems.at[sem]))
        for cp in locs + sends:
            cp.start()
        for cp in sends:
            cp.wait_recv()
        for cp in locs:
            cp.wait()
        for i in range(n):
            outs[i][...] = (mine[i][...].astype(F32) + got[i][...].astype(F32)).astype(BF)
        for cp in sends:
            cp.wait_send()

    half = [(4,) + a.shape[1:] for a in arrs]
    outs = pl.pallas_call(
        body, name=name, out_shape=[_sds(h, BF) for h in half], in_specs=[ANY] * n,
        out_specs=[pl.BlockSpec(memory_space=pltpu.VMEM)] * n,
        scratch_shapes=[pltpu.VMEM(h, BF) for h in half] * 2
        + [pltpu.SemaphoreType.DMA((4 * n,)), pltpu.SemaphoreType.DMA((4 * n,)), pltpu.SemaphoreType.DMA((4 * n,))],
        compiler_params=_cp(),
    )(*arrs)
    return list(outs)


def _mod_fwd(crows, w_ada, b_blk):
    def body(c_ref, w_ref, b_ref, o_ref):
        o_ref[...] = mm(silu(c_ref[...]), w_ref[...]) + b_ref[...]

    return pl.pallas_call(body, name="mod_fwd", out_shape=_sds((24, 768)), compiler_params=_cp())(crows, w_ada, b_blk)


def _mod_bwd(crows, w_ada, dmod_blk, dmodc_blk):
    def body(c_ref, w_ref, d_ref, dc_ref, gw_ref, gc_ref, gb_ref):
        cr = c_ref[...]
        dc = dc_ref[0:1, :]
        for p in range(1, N_DEV):
            dc = dc + dc_ref[p:p + 1, :]
        row = lax.broadcasted_iota(jnp.int32, (24, 1), 0)
        gw_ref[...] = mm_tn(silu(cr), jnp.where(row == 16, dc, d_ref[...]))
        cc = cr[16:17, :]
        sg = jax.nn.sigmoid(cc)
        part = mm_nt(jnp.broadcast_to(dc, (8, 768)), w_ref[...])
        gc_ref[...] = part * (sg * (1.0 + cc * (1.0 - sg)))
        gb_ref[...] = jnp.broadcast_to(jnp.sum(d_ref[...], axis=0, keepdims=True) + dc, (8, 768))

    return pl.pallas_call(
        body, name="mod_bwd", out_shape=[_sds((D_MODEL, 768)), _sds((8, D_MODEL)), _sds((8, 768))],
        compiler_params=_cp())(crows, w_ada, dmod_blk, dmodc_blk)


def _tab_specs(tk):
    return [pl.BlockSpec((tk, LANE), lambda i, t: (t, 0))] * 2


def _k1_fwd(x, mod, g_attn, g_q, g_kv, ws, tabs, kv_all, is_ctx):
    b, l, _ = x.shape
    nt = l // TOK
    n_f32 = 2 if is_ctx else 4

    def body(x_ref, mod_ref, ga_ref, gq_ref, gk_ref, wa_ref, wb_ref, wq_ref, wk_ref, cs_ref, sn_ref, *rest):
        outs = rest if is_ctx else rest[2:]
        res = k1_tile(x_ref[...], mod_ref[0:1, :], mod_ref[1:2, :], ga_ref[...], gq_ref[...], gk_ref[...],
                      (wa_ref[...], wb_ref[...], wq_ref[...], wk_ref[...]), (None,) * 4,
                      (cs_ref[...], sn_ref[...]), is_ctx)
        for o_ref, r in zip(outs, res):
            o_ref[...] = r.astype(o_ref.dtype)

    tok = lambda w, off=0: pl.BlockSpec((None, TOK, w), lambda i, t: (i, t + off, 0))
    mod_spec = pl.BlockSpec((None, 8, D_MODEL), (lambda i, t: (0, 0, 0)) if is_ctx else (lambda i, t: (i, 0, 0)))
    kv_off = 0 if is_ctx else CTX_LEN // TOK
    in_specs = ([tok(D_MODEL), mod_spec, _full((1, D_MODEL)), _full((1, 384)), _full((1, 256))]
                + [_full(s) for s in W_SHAPES] + _tab_specs(TOK))
    args = [x, mod, g_attn, g_q, g_kv, *ws, *tabs]
    out_specs = [tok(512)] * n_f32 + ([] if is_ctx else [tok(1024)]) + [tok(1024, kv_off), tok(512, kv_off)]
    out_shape = ([_sds((b, l, 512))] * n_f32 + ([] if is_ctx else [_sds((b, l, 1024), BF)])
                 + [_sds((b, KV_LEN, 1024), BF), _sds((b, KV_LEN, 512), BF)])
    aliases = {}
    if not is_ctx:
        aliases = {len(args): n_f32 + 1, len(args) + 1: n_f32 + 2}
        in_specs += [ANY, ANY]
        args += list(kv_all)
    return pl.pallas_call(
        body, name="k1_fwd_ctx" if is_ctx else "k1_fwd", grid=(b, nt), in_specs=in_specs, out_specs=out_specs,
        out_shape=out_shape, input_output_aliases=aliases, compiler_params=_cp((ARB, ARB)),
    )(*args)


N_ACC = 7


def _k1_bwd(x, ctx, mod, mod_c, g_attn, g_q, g_kv, ws, tabs, cts, cts_c, dx_res):
    b, l, _ = x.shape
    tk = TOK_B
    nt = l // tk
    flat = [[a for group in c for a in group] for c in (cts, cts_c)]
    sizes = [[len(g) for g in c] for c in (cts, cts_c)]
    acc_shapes = W_SHAPES + [(1, D_MODEL), (1, 384), (1, 256)]

    def body(*refs):
        it = iter(refs)
        x_ref, c_ref, mod_ref, modc_ref, ga_ref, gq_ref, gk_ref = [next(it) for _ in range(7)]
        w_hbm = [next(it) for _ in range(4)]
        tab_refs = [next(it) for _ in range(2)]
        ct_refs = [[next(it) for _ in f] for f in flat]
        res_ref, gx_ref = next(it), next(it)
        out_hbm = [next(it) for _ in range(N_ACC)]
        dmod_ref, dmodc_ref = next(it), next(it)
        w_vmem = [next(it) for _ in range(4)]
        accs = [next(it) for _ in range(N_ACC)]
        sem = next(it)
        i, t = pl.program_id(0), pl.program_id(1)
        first = jnp.logical_and(i == 0, t == 0)

        @pl.when(first)
        def _():
            for src, dst in zip(w_hbm, w_vmem):
                pltpu.sync_copy(src, dst)
            for k in range(N_ACC):
                accs[k][...] = jnp.zeros(acc_shapes[k], F32)

        def tile(is_ctx):
            which = 1 if is_ctx else 0
            ct_vals, pos = [], 0
            for gsz in sizes[which]:
                v = ct_refs[which][pos][...].astype(F32)
                for r in ct_refs[which][pos + 1:pos + gsz]:
                    v = v + r[...]
                ct_vals.append(v)
                pos += gsz
            wv = tuple(r[...] for r in w_vmem)
            tv = tuple(r[...] for r in tab_refs)
            m_ref = modc_ref if is_ctx else mod_ref

            def f(xv, sh, sc, ga, gq, gk, *probes):
                return k1_tile(xv, sh, sc, ga, gq, gk, wv, probes, tv, is_ctx)

            probes = [jnp.zeros(s, F32) for s in W_SHAPES]
            xin = c_ref[...] if is_ctx else x_ref[...]
            _, vjp = jax.vjp(f, xin, m_ref[0:1, :], m_ref[1:2, :], ga_ref[...], gq_ref[...], gk_ref[...], *probes)
            dx, dsh, dsc, dga, dgq, dgk, dwa, dwb, dwq, dwk = vjp(tuple(ct_vals))
            for ref, val in zip(accs, (dwa, dwb, dwq, dwk, dga, dgq, dgk)):
                ref[...] += val
            return dx, dsh, dsc

        @pl.when(t == 0)
        def _():
            _, dsh, dsc = tile(True)
            _acc(dmodc_ref.at[0:1, :], dsh, i == 0)
            _acc(dmodc_ref.at[1:2, :], dsc, i == 0)

            @pl.when(i == 0)
            def _():
                dmodc_ref[2:8, :] = jnp.zeros((6, D_MODEL), F32)

        @pl.when(t > 0)
        def _():
            dx, dsh, dsc = tile(False)
            gx_ref[...] = dx + res_ref[...]
            _acc(dmod_ref.at[0:1, :], dsh, t == 1)
            _acc(dmod_ref.at[1:2, :], dsc, t == 1)

            @pl.when(t == 1)
            def _():
                dmod_ref[2:8, :] = jnp.zeros((6, D_MODEL), F32)

        @pl.when(jnp.logical_and(i == b - 1, t == nt))
        def _():
            for k in range(4):
                w_vmem[k][...] = accs[k][...].astype(BF)
            cps = [pltpu.make_async_copy(w_vmem[k] if k < 4 else accs[k], out_hbm[k], sem.at[k]) for k in range(N_ACC)]
            for cp in cps:
                cp.start()
            for cp in cps:
                cp.wait()

    lat = lambda w, off=0: pl.BlockSpec((None, tk, w), lambda i, t: (i, jnp.maximum(t - 1, 0) + off, 0))
    con = lambda w: pl.BlockSpec((None, tk, w), lambda i, t: (i, 0, 0))
    mod_spec = pl.BlockSpec((None, 8, D_MODEL), lambda i, t: (i, 0, 0))
    modc_spec = pl.BlockSpec((None, 8, D_MODEL), lambda i, t: (0, 0, 0))
    tab_spec = pl.BlockSpec((tk, LANE), lambda i, t: (jnp.maximum(t - 1, 0), 0))
    in_specs = ([lat(D_MODEL), con(D_MODEL), mod_spec, modc_spec, _full((1, D_MODEL)), _full((1, 384)), _full((1, 256))]
                + [ANY] * 4 + [tab_spec] * 2)
    args = [x, ctx, mod, mod_c, g_attn, g_q, g_kv, *ws, *tabs]
    for a, off in flat[0]:
        in_specs.append(lat(a.shape[-1], off // tk))
        args.append(a)
    for a, off in flat[1]:
        assert off == 0
        in_specs.append(con(a.shape[-1]))
        args.append(a)
    in_specs.append(lat(D_MODEL))
    args.append(dx_res)
    out_shape = ([_sds((b, l, D_MODEL))] + [_sds(s, BF) for s in W_SHAPES] + [_sds(s) for s in acc_shapes[4:]]
                 + [_sds((b, 8, D_MODEL)), _sds((1, 8, D_MODEL))])
    out_specs = [lat(D_MODEL)] + [ANY] * N_ACC + [mod_spec, modc_spec]
    outs = pl.pallas_call(
        body, name="k1_bwd", grid=(b, nt + 1), in_specs=in_specs, out_specs=out_specs, out_shape=out_shape,
        scratch_shapes=[pltpu.VMEM(s, BF) for s in W_SHAPES] + [pltpu.VMEM(s, F32) for s in acc_shapes]
        + [pltpu.SemaphoreType.DMA((N_ACC,))],
        compiler_params=_cp((ARB, ARB)),
    )(*args)
    return outs[0], list(outs[1:1 + N_ACC]), outs[1 + N_ACC], outs[2 + N_ACC]


def _chunk_spec(rev):
    if rev:
        return pl.BlockSpec((None, RET_CHUNK, 512), lambda i, n: (i, N_CHUNK - 1 - n, 0))
    return pl.BlockSpec((None, RET_CHUNK, 512), lambda i, n: (i, n, 0))


def _state_spec(rev):
    if rev:
        return pl.BlockSpec((None, N_HEADS, None, LANE, LANE), lambda i, n: (i, 0, N_CHUNK - 1 - n, 0, 0))
    return pl.BlockSpec((None, N_HEADS, None, LANE, LANE), lambda i, n: (i, 0, n, 0, 0))


_CTX_SPEC = pl.BlockSpec((None, CTX_LEN, 512), lambda i, n: (i, 0, 0))
_DEC_SPEC = pl.BlockSpec((N_HEADS, 1, 1), lambda i, n: (0, 0, 0))


def _k2_fwd(rq, rk, rv, rkc, rvc, dec_f, dec_b):
    b = rq.shape[0]

    def body(qf, kf, vf, qb, kb, vb, kc, vc, df, db, of_ref, ob_ref, sf_out, sb_out, sf, sb):
        n = pl.program_id(1)
        for h, sl in enumerate(_HEAD_SL):
            lgf, lgb = log_sigmoid(df[h]), log_sigmoid(db[h])

            @pl.when(n == 0)
            def _():
                sf[h] = ctx_state(kc[:, sl], vc[:, sl], lgf, False)
                sb[h] = ctx_state(kc[:, sl], vc[:, sl], lgb, True)

            sf_out[h] = sf[h]
            sb_out[h] = sb[h]
            o, s = ret_chunk(qf[:, sl], kf[:, sl], vf[:, sl], sf[h], lgf, False)
            of_ref[:, sl] = o
            sf[h] = s
            o, s = ret_chunk(qb[:, sl], kb[:, sl], vb[:, sl], sb[h], lgb, True)
            ob_ref[:, sl] = o
            sb[h] = s

    l = rq.shape[1]
    return pl.pallas_call(
        body, name="k2_fwd", grid=(b, N_CHUNK),
        in_specs=[_chunk_spec(False)] * 3 + [_chunk_spec(True)] * 3 + [_CTX_SPEC, _CTX_SPEC, _DEC_SPEC, _DEC_SPEC],
        out_specs=[_chunk_spec(False), _chunk_spec(True), _state_spec(False), _state_spec(True)],
        out_shape=[_sds((b, l, 512)), _sds((b, l, 512)), _sds((b, N_HEADS, N_CHUNK, LANE, LANE)),
                   _sds((b, N_HEADS, N_CHUNK, LANE, LANE))],
        scratch_shapes=[pltpu.VMEM((N_HEADS, LANE, LANE), F32), pltpu.VMEM((N_HEADS, LANE, LANE), F32)],
        compiler_params=_cp((ARB, ARB)),
    )(rq, rk, rv, rq, rk, rv, rkc, rvc, dec_f, dec_b)


def _k2_bwd(rq, rk, rv, do, sf_prev, sb_prev, rkc, rvc, dec_f, dec_b):
    b, l, _ = rq.shape

    def body(qf, kf, vf, gf, spf, qb, kb, vb, gb, spb, kc, vc, df, db,
             dqf, dkf, dvf, dqb, dkb, dvb, dkc, dvc, ddf, ddb, dsf, dsb):
        n = pl.program_id(1)

        @pl.when(n == 0)
        def _():
            dsf[...] = jnp.zeros((N_HEADS, LANE, LANE), F32)
            dsb[...] = jnp.zeros((N_HEADS, LANE, LANE), F32)

        def one(h, sl, q, k, v, g, sp, dec, ds, dq, dk, dv, dd, rev):
            def f(qv, kv_, vv, sv, dcy):
                return ret_chunk(qv, kv_, vv, sv, log_sigmoid(dcy), rev)

            _, vjp = jax.vjp(f, q[:, sl], k[:, sl], v[:, sl], sp[h], dec[h])
            gq, gk, gv, gs, gd = vjp((g[:, sl], ds[h]))
            dq[:, sl] = gq
            dk[:, sl] = gk
            dv[:, sl] = gv
            ds[h] = gs
            _acc(dd.at[h], jnp.broadcast_to(gd, (8, LANE)), n == 0)

        for h, sl in enumerate(_HEAD_SL):
            one(h, sl, qf, kf, vf, gf, spf, df, dsf, dqf, dkf, dvf, ddf, False)
            one(h, sl, qb, kb, vb, gb, spb, db, dsb, dqb, dkb, dvb, ddb, True)

        @pl.when(n == N_CHUNK - 1)
        def _():
            def f(kcv, vcv, dcy, rev):
                return ctx_state(kcv, vcv, log_sigmoid(dcy), rev)

            for h, sl in enumerate(_HEAD_SL):
                _, vjp_f = jax.vjp(functools.partial(f, rev=False), kc[:, sl], vc[:, sl], df[h])
                gk_f, gv_f, gd_f = vjp_f(dsf[h])
                _, vjp_b = jax.vjp(functools.partial(f, rev=True), kc[:, sl], vc[:, sl], db[h])
                gk_b, gv_b, gd_b = vjp_b(dsb[h])
                dkc[:, sl] = gk_f + gk_b
                dvc[:, sl] = gv_f + gv_b
                ddf[h] += jnp.broadcast_to(gd_f, (8, LANE))
                ddb[h] += jnp.broadcast_to(gd_b, (8, LANE))

    dd_spec = pl.BlockSpec((None, N_HEADS, 8, LANE), lambda i, n: (i, 0, 0, 0))
    return pl.pallas_call(
        body, name="k2_bwd", grid=(b, N_CHUNK),
        in_specs=[_chunk_spec(True)] * 4 + [_state_spec(True)] + [_chunk_spec(False)] * 4 + [_state_spec(False)]
        + [_CTX_SPEC, _CTX_SPEC, _DEC_SPEC, _DEC_SPEC],
        out_specs=[_chunk_spec(True)] * 3 + [_chunk_spec(False)] * 3 + [_CTX_SPEC, _CTX_SPEC, dd_spec, dd_spec],
        out_shape=[_sds((b, l, 512))] * 6 + [_sds((b, CTX_LEN, 512))] * 2 + [_sds((b, N_HEADS, 8, LANE))] * 2,
        scratch_shapes=[pltpu.VMEM((N_HEADS, LANE, LANE), F32), pltpu.VMEM((N_HEADS, LANE, LANE), F32)],
        compiler_params=_cp((ARB, ARB)),
    )(rq, rk, rv, do, sf_prev, rq, rk, rv, do, sb_prev, rkc, rvc, dec_f, dec_b)


TQ = 512
QK_W = 2 * LANE
N_QP = 2
_Q_PARTS = [slice(i * TQ // N_QP, (i + 1) * TQ // N_QP) for i in range(N_QP)]


SM_SCALE = 1.0 / math.sqrt(192.0)


def _k3_specs():
    qs = lambda w: pl.BlockSpec((None, TQ, w), lambda i, h, t: (i, t, h))
    ks = lambda w: pl.BlockSpec((None, KV_LEN, w), lambda i, h, t: (i, 0, h))
    return qs, ks


def _k3_fwd(q, k, v):
    b, l, _ = q.shape

    def body(q_ref, k_ref, v_ref, o_ref, lse_ref):
        kv_, vv = k_ref[...], v_ref[...]
        for r in _Q_PARTS:
            s = _dot(q_ref[r, :], kv_, 1, 1) * SM_SCALE
            m = jnp.max(s, axis=-1, keepdims=True)
            e = jnp.exp(s - m)
            tot = jnp.sum(e, axis=-1, keepdims=True)
            o_ref[r, :] = _dot(e, vv, 1, 0) * (1.0 / tot)
            lse_ref[r, :] = jnp.broadcast_to(m + jnp.log(tot), (TQ // N_QP, LANE))

    qs, ks = _k3_specs()
    return pl.pallas_call(
        body, name="k3_fwd", grid=(b, N_HEADS, l // TQ), in_specs=[qs(QK_W), ks(QK_W), ks(LANE)],
        out_specs=[qs(LANE), qs(LANE)], out_shape=[_sds((b, l, N_HEADS * LANE))] * 2,
        compiler_params=_cp((ARB, ARB, ARB)),
    )(q, k, v)


def _k3_bwd(q, k, v, o, lse, dy, after):
    b, l, _ = q.shape

    def body(q_ref, k_ref, v_ref, o_ref, lse_ref, dy_ref, after_ref, dq_ref, dk_ref, dv_ref):
        t0 = pl.program_id(2) == 0
        kv_, vv = k_ref[...], v_ref[...]
        qv, dyv = q_ref[...], dy_ref[...]
        g = dyv.astype(BF)
        lse_col = jnp.max(lse_ref[...], axis=-1, keepdims=True)
        delta = jnp.sum(dyv * o_ref[...], axis=-1, keepdims=True)
        p = jnp.exp(_dot(qv, kv_, 1, 1) * SM_SCALE - lse_col)
        ds = (p * (_dot(g, vv, 1, 1) - delta) * SM_SCALE).astype(BF)
        _acc(dv_ref, _dot(p, g, 0, 0), t0)
        dq_ref[...] = _dot(ds, kv_, 1, 0)
        _acc(dk_ref, _dot(ds, qv, 0, 0), t0)

    qs, ks = _k3_specs()
    return pl.pallas_call(
        body, name="k3_bwd", grid=(b, N_HEADS, l // TQ),
        in_specs=[qs(QK_W), ks(QK_W), ks(LANE), qs(LANE), qs(LANE), qs(LANE), ANY],
        out_specs=[qs(QK_W), ks(QK_W), ks(LANE)],
        out_shape=[_sds((b, l, N_HEADS * QK_W)), _sds((b, KV_LEN, N_HEADS * QK_W)), _sds((b, KV_LEN, N_HEADS * LANE))],
        compiler_params=_cp((ARB, ARB, ARB)),
    )(q, k, v, o, lse, dy, after)


def _mod_rows(mod_ref, rows):
    return [mod_ref[r:r + 1, :] for r in rows]


def _k4a_fwd(x, o_f, o_b, rg, y_mla, g_ret, w_out, mod, g_ffn):
    b, l, _ = x.shape

    def body(x_ref, of_ref, ob_ref, rg_ref, ym_ref, gr_ref, wo_ref, mod_ref, gf_ref, xm_ref, h2_ref):
        gt_a, sh_f, sc_f = _mod_rows(mod_ref, (2, 3, 4))
        x_mid, h2 = k4a_tile(x_ref[...], of_ref[...], ob_ref[...], rg_ref[...], ym_ref[...], gr_ref[...], gt_a,
                             gf_ref[...], sh_f, sc_f, wo_ref[...], None)
        xm_ref[...] = x_mid
        h2_ref[...] = h2.astype(BF)

    tok = lambda w: pl.BlockSpec((None, TOK, w), lambda i, t: (i, t, 0))
    mod_spec = pl.BlockSpec((None, 8, D_MODEL), lambda i, t: (i, 0, 0))
    return pl.pallas_call(
        body, name="k4a_fwd", grid=(b, l // TOK),
        in_specs=[tok(D_MODEL), tok(512), tok(512), tok(512), tok(512), _full((1, 512)), _full((D_MODEL, D_MODEL)),
                  mod_spec, _full((1, D_MODEL))],
        out_specs=[tok(D_MODEL), tok(D_MODEL)], out_shape=[_sds((b, l, D_MODEL)), _sds((b, l, D_MODEL), BF)],
        compiler_params=_cp((ARB, ARB)),
    )(x, o_f, o_b, rg, y_mla, g_ret, w_out, mod, g_ffn)


TOK_M = 512
TOK_D = 1024
HALF_FF = D_FF // 2


def _k4b_mlp_loss(h2, w1t, w2, x_mid, mod, g_final, tgt):
    b, l, _ = h2.shape
    nt = l // TOK_M

    def body(h2_ref, w1_hbm, w2_hbm, xm_ref, mod_ref, gfin_ref, tgt_ref, dxm_ref, dmlp_ref, r_ref, loss_ref, dgt_ref,
             dgfin_ref, w1_v, w2_v):
        i, t = pl.program_id(0), pl.program_id(1)
        first = jnp.logical_and(i == 0, t == 0)

        @pl.when(first)
        def _():
            pltpu.sync_copy(w1_hbm, w1_v)
            pltpu.sync_copy(w2_hbm, w2_v)

        h2v = h2_ref[...]
        mlp = None
        for half in range(2):
            rows = slice(half * HALF_FF, (half + 1) * HALF_FF)
            r = jnp.maximum(_dot(h2v, w1_v[rows, :], 1, 1), 0.0)
            r_ref[:, rows] = r.astype(BF)
            part = _dot(jnp.square(r), w2_v[rows, :], 1, 0)
            mlp = part if mlp is None else mlp + part
        (gt_f,) = _mod_rows(mod_ref, (5,))
        loss, vjp = jax.vjp(k4c_tile, xm_ref[...], mlp, gt_f, gfin_ref[...], tgt_ref[...])
        dxm, dmlp, dgt, dgfin, _ = vjp(jnp.ones((1, 1), F32))
        dxm_ref[...] = dxm
        dmlp_ref[...] = dmlp.astype(BF)
        _acc(loss_ref, jnp.broadcast_to(loss, (8, LANE)), first)
        _acc(dgfin_ref, dgfin, first)
        _acc(dgt_ref, dgt, t == 0)

    tok = lambda w: pl.BlockSpec((None, TOK_M, w), lambda i, t: (i, t, 0))
    return pl.pallas_call(
        body, name="k4b_mlp_loss", grid=(b, nt),
        in_specs=[tok(D_MODEL), ANY, ANY, tok(D_MODEL), pl.BlockSpec((None, 8, D_MODEL), lambda i, t: (i, 0, 0)),
                  _full((1, D_MODEL)), tok(D_MODEL)],
        out_specs=[tok(D_MODEL), tok(D_MODEL), tok(D_FF), _full((8, LANE)),
                   pl.BlockSpec((None, 1, D_MODEL), lambda i, t: (i, 0, 0)), _full((1, D_MODEL))],
        out_shape=[_sds((b, l, D_MODEL)), _sds((b, l, D_MODEL), BF), _sds((b, l, D_FF), BF), _sds((8, LANE)),
                   _sds((b, 1, D_MODEL)), _sds((1, D_MODEL))],
        scratch_shapes=[pltpu.VMEM((D_FF, D_MODEL), BF), pltpu.VMEM((D_FF, D_MODEL), BF)],
        compiler_params=_cp((ARB, ARB)),
    )(h2, w1t, w2, x_mid, mod, g_final, tgt)


def _k4d_mlp_bwd(h2, dmlp, r, w2):
    b, l, _ = h2.shape
    nt = l // TOK_D

    def body(h2_ref, dm_ref, r_ref, w2_ref, da_ref, dw1_ref, dw2_ref, acc1, acc2):
        i, t = pl.program_id(1), pl.program_id(2)
        first = jnp.logical_and(i == 0, t == 0)
        rv = r_ref[...].astype(F32)
        dm = dm_ref[...]
        da = (_dot(dm, w2_ref[...], 1, 1) * (2.0 * rv)).astype(BF)
        da_ref[...] = da
        _acc(acc2, _dot(jnp.square(rv), dm, 0, 0), first)
        _acc(acc1, _dot(h2_ref[...], da, 0, 0), first)

        @pl.when(jnp.logical_and(i == b - 1, t == nt - 1))
        def _():
            dw1_ref[...] = acc1[...].astype(BF)
            dw2_ref[...] = acc2[...].astype(BF)

    tok = lambda w: pl.BlockSpec((None, TOK_D, w), lambda j, i, t: (i, t, 0))
    col = pl.BlockSpec((None, TOK_D, FF_BLK), lambda j, i, t: (i, t, j))
    return pl.pallas_call(
        body, name="k4d_mlp_bwd", grid=(N_DEV, b, nt),
        in_specs=[tok(D_MODEL), tok(D_MODEL), col, pl.BlockSpec((None, FF_BLK, D_MODEL), lambda j, i, t: (j, 0, 0))],
        out_specs=[col, pl.BlockSpec((None, D_MODEL, FF_BLK), lambda j, i, t: (j, 0, 0)),
                   pl.BlockSpec((None, FF_BLK, D_MODEL), lambda j, i, t: (j, 0, 0))],
        out_shape=[_sds((b, l, D_FF), BF), _sds((N_DEV, D_MODEL, FF_BLK), BF), _sds((N_DEV, FF_BLK, D_MODEL), BF)],
        scratch_shapes=[pltpu.VMEM((D_MODEL, FF_BLK), F32), pltpu.VMEM((FF_BLK, D_MODEL), F32)],
        compiler_params=_cp((ARB, ARB, ARB)),
    )(h2, dmlp, r, w2)


def _k4e_bwd(x, o_f, o_b, rg, y_mla, g_ret, w_out, mod, g_ffn, dxm, da, w1t):
    b, l, _ = x.shape

    def body(x_ref, of_ref, ob_ref, rg_ref, ym_ref, gr_ref, wo_ref, mod_ref, gf_ref, dxm_ref, da_ref, w1_hbm,
             dx_ref, do_ref, drg_ref, dym_ref, dwo_ref, dgr_ref, dgf_ref, dmod_ref, w1_v):
        i, t = pl.program_id(0), pl.program_id(1)
        first = jnp.logical_and(i == 0, t == 0)

        @pl.when(first)
        def _():
            pltpu.sync_copy(w1_hbm, w1_v)

        gt_a, sh_f, sc_f = _mod_rows(mod_ref, (2, 3, 4))
        wo = wo_ref[...]
        dh2 = _dot(da_ref[...], w1_v[...], 1, 0)

        def f(xv, ofv, rgv, ymv, grv, gta, gfv, shf, scf, p_out):
            return k4a_tile(xv, ofv, ob_ref[...], rgv, ymv, grv, gta, gfv, shf, scf, wo, p_out)

        _, vjp = jax.vjp(f, x_ref[...], of_ref[...], rg_ref[...], ym_ref[...], gr_ref[...], gt_a, gf_ref[...], sh_f,
                         sc_f, jnp.zeros((D_MODEL, D_MODEL), F32))
        dx, do, drg, dym, dgr, dgta, dgf, dshf, dscf, dwo = vjp((dxm_ref[...], dh2))
        dx_ref[...] = dx
        do_ref[...] = do
        drg_ref[...] = drg
        dym_ref[...] = dym
        _acc(dwo_ref, dwo, first)
        _acc(dgr_ref, dgr, first)
        _acc(dgf_ref, dgf, first)
        t0 = t == 0
        _acc(dmod_ref.at[2:3, :], dgta, t0)
        _acc(dmod_ref.at[3:4, :], dshf, t0)
        _acc(dmod_ref.at[4:5, :], dscf, t0)

        @pl.when(t0)
        def _():
            dmod_ref[0:2, :] = jnp.zeros((2, D_MODEL), F32)
            dmod_ref[5:8, :] = jnp.zeros((3, D_MODEL), F32)

    tok = lambda w: pl.BlockSpec((None, TOK_B, w), lambda i, t: (i, t, 0))
    mod_spec = pl.BlockSpec((None, 8, D_MODEL), lambda i, t: (i, 0, 0))
    return pl.pallas_call(
        body, name="k4e_bwd", grid=(b, l // TOK_B),
        in_specs=[tok(D_MODEL), tok(512), tok(512), tok(512), tok(512), _full((1, 512)), _full((D_MODEL, D_MODEL)),
                  mod_spec, _full((1, D_MODEL)), tok(D_MODEL), tok(D_FF), ANY],
        out_specs=[tok(D_MODEL), tok(512), tok(512), tok(512), _full((D_MODEL, D_MODEL)), _full((1, 512)),
                   _full((1, D_MODEL)), mod_spec],
        out_shape=[_sds((b, l, D_MODEL)), _sds((b, l, 512)), _sds((b, l, 512)), _sds((b, l, 512)),
                   _sds((D_MODEL, D_MODEL)), _sds((1, 512)), _sds((1, D_MODEL)), _sds((b, 8, D_MODEL))],
        scratch_shapes=[pltpu.VMEM((D_FF, D_MODEL), BF)],
        compiler_params=_cp((ARB, ARB)),
    )(x, o_f, o_b, rg, y_mla, g_ret, w_out, mod, g_ffn, dxm, da, w1t)


def _adamw(w, m, v, pieces, name, after=None):
    r, c = w.shape
    npc = pieces.shape[0]
    per_row = c * (7 * 4 + npc * pieces.dtype.itemsize) * 2
    rb = r
    for cand in (r, 512, 256, 128, 64, 32, 16, 8):
        if r % cand == 0 and cand * per_row <= 32 * 1024 * 1024:
            rb = cand
            break

    def body(w_ref, m_ref, v_ref, p_ref, *rest):
        g_ref, d_ref, nm_ref, nv_ref = rest[-4:]
        g = p_ref[0].astype(F32)
        for k in range(1, npc):
            g = g + p_ref[k].astype(F32)
        wv = w_ref[...]
        mn = ADAM_B1 * m_ref[...] + (1.0 - ADAM_B1) * g
        vn = ADAM_B2 * v_ref[...] + (1.0 - ADAM_B2) * jnp.square(g)
        m_hat = mn / (1.0 - ADAM_B1 ** ADAM_STEP)
        v_hat = vn / (1.0 - ADAM_B2 ** ADAM_STEP)
        g_ref[...] = g
        d_ref[...] = -ADAM_LR * (m_hat / (jnp.sqrt(v_hat) + ADAM_EPS) + ADAM_WD * wv)
        nm_ref[...] = mn
        nv_ref[...] = vn

    blk = pl.BlockSpec((rb, c), lambda i: (i, 0))
    extra = [] if after is None else [after]
    return pl.pallas_call(
        body, name=name, grid=(r // rb,),
        in_specs=[blk, blk, blk, pl.BlockSpec((npc, rb, c), lambda i: (0, i, 0))] + [ANY] * len(extra),
        out_specs=[blk] * 4, out_shape=[_sds((r, c))] * 4, compiler_params=_cp((ARB,)),
    )(w, m, v, pieces, *extra)


def _pad_head_rows(w, d):
    k = w.shape[1]
    return jnp.pad(w.reshape(N_HEADS, d, k), ((0, 0), (0, LANE - d), (0, 0))).reshape(N_HEADS * LANE, k)


def _cut_head_rows(g, d):
    k = g.shape[1]
    return g.reshape(N_HEADS, LANE, k)[:, :d].reshape(N_HEADS * d, k)


def _w_in_pad(wt):
    w_a = jnp.concatenate([_pad_head_rows(wt[0:256], 64), _pad_head_rows(wt[256:512], 64), wt[512:1536]], axis=0)
    w_b = jnp.concatenate([wt[1536:2176], jnp.pad(wt[2176:2240], ((0, 64), (0, 0)))], axis=0)
    return w_a, w_b


def _w_in_cut(g_a, g_b):
    return jnp.concatenate([_cut_head_rows(g_a[0:512], 64), _cut_head_rows(g_a[512:1024], 64), g_a[1024:2048],
                            g_b[0:704]], axis=0)


def _w_uq_pad(wt):
    return jnp.pad(wt.reshape(N_HEADS, 192, 384), ((0, 0), (0, 64), (0, 0))).reshape(1024, 384)


def _w_uq_cut(g):
    return g.reshape(N_HEADS, 256, 384)[:, :192].reshape(768, 384)


def _w_ukv_perm(wt):
    return jnp.transpose(wt.reshape(N_HEADS, 2, LANE, 256), (1, 0, 2, 3)).reshape(1024, 256)


def _w_ukv_unperm(g):
    return jnp.transpose(g.reshape(2, N_HEADS, LANE, 256), (1, 0, 2, 3)).reshape(1024, 256)


def _unshard_cols(g):
    return jnp.transpose(g, (1, 0, 2)).reshape(g.shape[1], N_DEV * g.shape[2])


def _rope_tables():
    rows = SEQ // GRID_W
    row = jnp.repeat(jnp.arange(rows, dtype=F32), GRID_W)
    col = jnp.tile(jnp.arange(GRID_W, dtype=F32), rows)
    freq = ROPE_BASE ** (-jnp.arange(16, dtype=F32) / 16)
    ang = jnp.concatenate([row[:, None] * freq, col[:, None] * freq], axis=-1)
    cos, sin = jnp.cos(ang), jnp.sin(ang)
    z = jnp.zeros((SEQ, 64), F32)
    return jnp.concatenate([cos, cos, z], axis=1), jnp.concatenate([-sin, sin, z], axis=1)


_PACKED = (("g_attn", 1024), ("g_ffn", 1024), ("ret_decay_fwd", 4), ("ret_decay_bwd", 4), ("g_ret", 512),
           ("g_q_lora", 384), ("g_kv_lora", 256), ("g_final", 1024))
_PACK_OFF = {}
_off = 0
for _name, _n in _PACKED:
    _PACK_OFF[_name] = _off
    _off += -(-_n // LANE) * LANE
PACK_W = _off


def _pack_small(vals):
    parts = []
    for name, n in _PACKED:
        a = vals[name].reshape(-1).astype(F32)
        parts.append(jnp.pad(a, (0, -(-n // LANE) * LANE - n)))
    return jnp.concatenate(parts).reshape(1, PACK_W)


def _adamw_small(params, packed, gcc, gb_ada):
    names = list(params)
    n_p = len(names)

    def body(*refs):
        p_ref, gcc_ref, gb_ref = refs[3 * n_p:3 * n_p + 3]
        outs = refs[3 * n_p + 3:]
        for k, name in enumerate(names):
            w_ref, m_ref, v_ref = refs[3 * k:3 * k + 3]
            n = w_ref.shape[1]
            if name == "b_ada":
                g = jnp.concatenate([gb_ref[d, 0:1, :] for d in range(N_DEV)], axis=-1)
            elif name == "c_ctx":
                g = gcc_ref[0, 0:1, :]
                for d in range(1, N_DEV):
                    g = g + gcc_ref[d, 0:1, :]
            else:
                off = _PACK_OFF[name]
                g = p_ref[0, :, off:off + n]
                for d in range(1, N_DEV):
                    g = g + p_ref[d, :, off:off + n]
            mn = ADAM_B1 * m_ref[...] + (1.0 - ADAM_B1) * g
            vn = ADAM_B2 * v_ref[...] + (1.0 - ADAM_B2) * jnp.square(g)
            m_hat = mn / (1.0 - ADAM_B1 ** ADAM_STEP)
            v_hat = vn / (1.0 - ADAM_B2 ** ADAM_STEP)
            outs[4 * k][...] = g
            outs[4 * k + 1][...] = -ADAM_LR * (m_hat / (jnp.sqrt(v_hat) + ADAM_EPS) + ADAM_WD * w_ref[...])
            outs[4 * k + 2][...] = mn
            outs[4 * k + 3][...] = vn

    args = [a for name in names for a in params[name]] + [packed, gcc, gb_ada]
    out_shape = [_sds(params[name][0].shape) for name in names for _ in range(4)]
    outs = pl.pallas_call(body, name="adamw_small", out_shape=out_shape, compiler_params=_cp())(*args)
    return {name: list(outs[4 * k:4 * k + 4]) for k, name in enumerate(names)}


def kernel(x, c, ctx, c_ctx, w_ada, b_ada, g_attn, g_ffn, w_in, ret_decay_fwd, ret_decay_bwd, g_ret, g_q_lora, w_uq, g_kv_lora, w_ukv, w_out, w_ff1, w_ff2, g_final, loss_target, m_c_ctx, m_w_ada, m_b_ada, m_g_attn, m_g_ffn, m_w_in, m_ret_decay_fwd, m_ret_decay_bwd, m_g_ret, m_g_q_lora, m_w_uq, m_g_kv_lora, m_w_ukv, m_w_out, m_w_ff1, m_w_ff2, m_g_final, v_c_ctx, v_w_ada, v_b_ada, v_g_attn, v_g_ffn, v_w_in, v_ret_decay_fwd, v_ret_decay_bwd, v_g_ret, v_g_q_lora, v_w_uq, v_g_kv_lora, v_w_ukv, v_w_out, v_w_ff1, v_w_ff2, v_g_final):
    me = 4 * lax.axis_index("x") + 2 * lax.axis_index("y") + lax.axis_index("c")
    nb = x.shape[0]

    c_pad = jnp.pad(c, ((0, 8 - nb), (0, 0)))
    c_all, g_in, g_uq, g_ukv = _gather_two_level(
        [c_pad, w_in[0].T.astype(BF), w_uq[0].T.astype(BF), w_ukv[0].T.astype(BF)], "gather_weights")
    ws = (*_w_in_pad(g_in.reshape(2240, D_MODEL)), _w_uq_pad(g_uq.reshape(768, 384)),
          _w_ukv_perm(g_ukv.reshape(1024, 256)))

    crows = jnp.concatenate([c_all[:, :nb].reshape(N_DEV * nb, D_MODEL), c_ctx[None], jnp.zeros((7, D_MODEL), F32)])
    b_blk = lax.dynamic_slice(b_ada, (0, me * 768), (1, 768))
    (mod_g,) = _exchange([_mod_fwd(crows, w_ada[0], b_blk)], True, "gather_mod")
    mod_all = _unshard_cols(mod_g)
    behind = mod_g[0, 0, 0:1] * 0.0
    st_o = _exchange_start([(w_out[0] + behind).astype(BF)], True, "gather_wo_start")
    st_g = _exchange_start([w_ff1[0].T.astype(BF), w_ff2[0].astype(BF)], True, "gather_ff_start", after=st_o["token"])
    mod_all = mod_all + st_g["token"][0:1, 0:1]
    mod_mine = lax.dynamic_slice(mod_all, (me * nb, 0), (nb, 6 * D_MODEL)).reshape(nb, 6, D_MODEL)
    mod = jnp.pad(mod_mine, ((0, 0), (0, 2), (0, 0)))
    mod_c = jnp.pad(mod_all[16].reshape(1, 6, D_MODEL), ((0, 0), (0, 2), (0, 0)))

    tabs = _rope_tables()
    dec_f = ret_decay_fwd.reshape(N_HEADS, 1, 1)
    dec_b = ret_decay_bwd.reshape(N_HEADS, 1, 1)

    rkc, rvc, k_ctx, v_ctx = _k1_fwd(ctx, mod_c, g_attn, g_q_lora, g_kv_lora, ws, tabs, None, True)
    rq, rk, rv, rg, q, k_all, v_all = _k1_fwd(x, mod, g_attn, g_q_lora, g_kv_lora, ws, tabs, (k_ctx, v_ctx), False)
    o_f, o_b, sf_prev, sb_prev = _k2_fwd(rq, rk, rv, rkc, rvc, dec_f, dec_b)
    y_mla, lse = _k3_fwd(q, k_all, v_all)
    (g_out,) = _exchange_wait(st_o, y_mla, "gather_wo_wait")
    wo = g_out.reshape(D_MODEL, D_MODEL)
    x_mid, h2 = _k4a_fwd(x, o_f, o_b, rg, y_mla, g_ret, wo, mod, g_ffn)
    g_ff1t, g_ff2 = _exchange_wait(st_g, x_mid, "gather_ff_wait")
    w1t = g_ff1t.reshape(D_FF, D_MODEL)
    dxm, dmlp, relu_a, loss_acc, dgt_f, dg_final = _k4b_mlp_loss(h2, w1t, g_ff2.reshape(D_FF, D_MODEL), x_mid, mod,
                                                                 g_final.reshape(1, D_MODEL), loss_target)

    da, dw1, dw2 = _k4d_mlp_bwd(h2, dmlp, relu_a, g_ff2)
    st_s = _exchange_start([dw1, dw2], False, "scatter_ff_start")
    g_ret_t = g_ret + st_s["token"][0:1, 0:1]
    dx_res, do, drg, dym, dwo, dg_ret, dg_ffn, dmod_a = _k4e_bwd(x, o_f, o_b, rg, y_mla, g_ret_t, wo, mod, g_ffn, dxm, da,
                                                                 w1t)
    st_w = _exchange_start([dwo.reshape(N_DEV, 128, D_MODEL).astype(BF)], False, "scatter_wo_start")
    dq, dk_all, dv_all = _k3_bwd(q, k_all, v_all, y_mla, lse, dym, st_w["token"])
    dqf, dkf, dvf, dqb, dkb, dvb, dkc, dvc, ddf, ddb = _k2_bwd(rq, rk, rv, do, sf_prev, sb_prev, rkc, rvc, dec_f, dec_b)
    cts = [[(dqf, 0), (dqb, 0)], [(dkf, 0), (dkb, 0)], [(dvf, 0), (dvb, 0)], [(drg, 0)], [(dq, 0)],
           [(dk_all, CTX_LEN)], [(dv_all, CTX_LEN)]]
    cts_c = [[(dkc, 0)], [(dvc, 0)], [(dk_all, 0)], [(dv_all, 0)]]
    grad_x, accs, dmod_1, dmod_c1 = _k1_bwd(x, ctx, mod, mod_c, g_attn, g_q_lora, g_kv_lora, ws, tabs, cts, cts_c,
                                            dx_res)
    dwa, dwb, dwq, dwk, dg_attn, dg_q, dg_kv = accs

    dmod_loc = (dmod_a + dmod_1).at[:, 5, :].set(dgt_f[:, 0, :])[:, :6, :].reshape(nb, 6 * D_MODEL)
    dmod_ctx = dmod_c1[:, :6, :].reshape(1, 6 * D_MODEL)
    small = {"g_attn": dg_attn, "g_ffn": dg_ffn, "ret_decay_fwd": jnp.sum(ddf[:, :, 0, 0], axis=0),
             "ret_decay_bwd": jnp.sum(ddb[:, :, 0, 0], axis=0), "g_ret": dg_ret, "g_q_lora": dg_q, "g_kv_lora": dg_kv,
             "g_final": dg_final}
    extra = jnp.concatenate([dmod_loc, dmod_ctx, jnp.zeros((5, 6 * D_MODEL), F32)])
    ex_pieces = jnp.transpose(extra.reshape(8, N_DEV, 768), (1, 0, 2))
    sm_g, ex_g, loss_g = _exchange([_pack_small(small), ex_pieces, loss_acc], [True, False, True], "gather_small")
    dmod_blk = jnp.concatenate([ex_g[:, :nb].reshape(N_DEV * nb, 768), jnp.zeros((8, 768), F32)])
    gw_ada, gcc_part, gb_part = _mod_bwd(crows, w_ada[0], dmod_blk, ex_g[:, nb])
    st_c = _exchange_start([gcc_part, gb_part], True, "gather_cc_start")

    p_ff1, p_ff2 = _exchange_wait(st_s, st_c["token"], "scatter_ff_wait")
    (p_wo,) = _exchange_wait(st_w, p_ff1, "scatter_wo_wait")
    chip_sums = _pair_reduce([_w_in_cut(dwa, dwb).reshape(N_DEV, 280, D_MODEL), _w_uq_cut(dwq).reshape(N_DEV, 96, 384),
                              _w_ukv_unperm(dwk).reshape(N_DEV, 128, 256)], "pair_reduce")
    st_r = _exchange_start(chip_sums, False, "scatter_rest_start", after=p_wo, chips=True)

    res = {}
    early = (("w_ff1", w_ff1, m_w_ff1, v_w_ff1, p_ff1), ("w_ff2", w_ff2, m_w_ff2, v_w_ff2, p_ff2),
             ("w_ada", w_ada, m_w_ada, v_w_ada, gw_ada[None]))
    for name, w, m, v, pcs in early:
        res[name] = [a[None] for a in _adamw(w[0], m[0], v[0], pcs, "adamw_" + name, after=st_r["token"])]
    pieces = _exchange_wait(st_r, res["w_ada"][3], "scatter_rest_wait")
    for name, w, m, v, pcs in (("w_in", w_in, m_w_in, v_w_in, pieces[0]), ("w_uq", w_uq, m_w_uq, v_w_uq, pieces[1])):
        res[name] = [a.T[None] for a in _adamw(w[0].T, m[0].T, v[0].T, pcs, "adamw_" + name)]
    late = (("w_ukv", w_ukv, m_w_ukv, v_w_ukv, jnp.transpose(pieces[2], (0, 2, 1))),
            ("w_out", w_out, m_w_out, v_w_out, p_wo))
    for name, w, m, v, pcs in late:
        res[name] = [a[None] for a in _adamw(w[0], m[0], v[0], pcs, "adamw_" + name)]

    smalls = {"c_ctx": (c_ctx, m_c_ctx, v_c_ctx), "b_ada": (b_ada, m_b_ada, v_b_ada), "g_attn": (g_attn, m_g_attn, v_g_attn),
              "g_ffn": (g_ffn, m_g_ffn, v_g_ffn), "ret_decay_fwd": (ret_decay_fwd, m_ret_decay_fwd, v_ret_decay_fwd),
              "ret_decay_bwd": (ret_decay_bwd, m_ret_decay_bwd, v_ret_decay_bwd), "g_ret": (g_ret, m_g_ret, v_g_ret),
              "g_q_lora": (g_q_lora, m_g_q_lora, v_g_q_lora), "g_kv_lora": (g_kv_lora, m_g_kv_lora, v_g_kv_lora),
              "g_final": (g_final, m_g_final, v_g_final)}
    rows = {k: tuple(a.reshape(1, -1) for a in t) for k, t in smalls.items()}
    gcc_g, gb_g = _exchange_wait(st_c, res["w_out"][3], "gather_cc_wait")
    for name, outs in _adamw_small(rows, sm_g, gcc_g, gb_g).items():
        res[name] = [o.reshape(smalls[name][0].shape) for o in outs]

    loss = loss_g[0, 0, 0]
    for k in range(1, N_DEV):
        loss = loss + loss_g[k, 0, 0]

    order = ("c_ctx", "w_ada", "b_ada", "g_attn", "g_ffn", "w_in", "ret_decay_fwd", "ret_decay_bwd", "g_ret", "g_q_lora",
             "w_uq", "g_kv_lora", "w_ukv", "w_out", "w_ff1", "w_ff2", "g_final")
    return (loss, grad_x, *[res[n][0] for n in order], *[res[n][1] for n in order], *[res[n][2] for n in order],
            *[res[n][3] for n in order])
```

```python
import functools
import math

import jax
import jax.numpy as jnp
from jax import lax
from jax.experimental import pallas as pl
from jax.experimental.pallas import tpu as pltpu

F32 = jnp.float32
BF = jnp.bfloat16
EPS = 1e-6
LANE = 128
N_DEV = 8
D_MODEL = 1024
SEQ = 2048
CTX_LEN = 256
GRID_W = 64
N_HEADS = 4
RET_CHUNK = 512
N_CHUNK = SEQ // RET_CHUNK
D_FF = 4096
FF_BLK = D_FF // N_DEV
IN_PAD = 2816
KV_LEN = CTX_LEN + SEQ
ROPE_BASE = 10000.0
ADAM_LR, ADAM_B1, ADAM_B2, ADAM_EPS, ADAM_WD, ADAM_STEP = 0.001, 0.9, 0.999, 1e-08, 0.01, 10
TOK = 256
TOK_B = 256
VMEM_LIMIT = 56 * 1024 * 1024
ARB = "arbitrary"
MESH = pl.DeviceIdType.MESH
_HEAD_SL = [slice(LANE * h, LANE * (h + 1)) for h in range(N_HEADS)]
W_SHAPES = [(2048, D_MODEL), (768, D_MODEL), (1024, 384), (1024, 256)]


def _dot(a, b, ca, cb):
    return lax.dot_general(a.astype(BF), b.astype(BF), (((ca,), (cb,)), ((), ())), preferred_element_type=F32)


@jax.custom_vjp
def mm(a, b):
    return _dot(a, b, 1, 0)


@jax.custom_vjp
def mm_nt(a, b):
    return _dot(a, b, 1, 1)


@jax.custom_vjp
def mm_tn(a, b):
    return _dot(a, b, 0, 0)


mm.defvjp(lambda a, b: (_dot(a, b, 1, 0), (a, b)), lambda r, g: (mm_nt(g, r[1]), mm_tn(r[0], g)))
mm_nt.defvjp(lambda a, b: (_dot(a, b, 1, 1), (a, b)), lambda r, g: (mm(g, r[1]), mm_tn(g, r[0])))
mm_tn.defvjp(lambda a, b: (_dot(a, b, 0, 0), (a, b)), lambda r, g: (mm_nt(r[1], g), mm(r[0], g)))


@jax.custom_vjp
def _mmw(a, w, probe):
    return _dot(a, w, 1, 0)


def _mmw_bwd(r, g):
    a, w = r
    return mm_nt(g, w), jnp.zeros_like(w), mm_tn(a, g)


_mmw.defvjp(lambda a, w, probe: (_dot(a, w, 1, 0), (a, w)), _mmw_bwd)


@jax.custom_vjp
def _mmwt(a, wt, probe):
    return _dot(a, wt, 1, 1)


_mmwt.defvjp(lambda a, wt, probe: (_dot(a, wt, 1, 1), (a, wt)),
             lambda r, g: (mm(g, r[1]), jnp.zeros_like(r[1]), mm_tn(g, r[0])))


def mmwt(a, wt, probe):
    return _dot(a, wt, 1, 1) if probe is None else _mmwt(a, wt, probe)


def mmw(a, w, probe):
    return _dot(a, w, 1, 0) if probe is None else _mmw(a, w, probe)


def rmsn(x, g):
    return x * lax.rsqrt(jnp.mean(x * x, axis=-1, keepdims=True) + EPS) * g


def silu(x):
    return x * jax.nn.sigmoid(x)


def _swap_halves_impl(x):
    return pltpu.roll(x, 64, 1)


@jax.custom_vjp
def swap_halves(x):
    return _swap_halves_impl(x)


swap_halves.defvjp(lambda x: (_swap_halves_impl(x), None), lambda _, g: (_swap_halves_impl(g),))


def rope(x, cs1, sn1, every=1):
    blocks = []
    for i in range(x.shape[-1] // LANE):
        xb = x[:, LANE * i:LANE * (i + 1)]
        blocks.append(xb * cs1 + swap_halves(xb) * sn1 if i % every == every - 1 else xb)
    return blocks[0] if len(blocks) == 1 else jnp.concatenate(blocks, axis=-1)


def k1_tile(x, sh, sc, g_attn, g_q, g_kv, ws, ps, tabs, is_ctx):
    w_a, w_b, w_uq, w_ukv = ws
    p_a, p_b, p_uq, p_ukv = ps
    cs1, sn1 = tabs
    h = rmsn(x, g_attn) * (1.0 + sc) + sh
    pa = mmwt(h, w_a, p_a)
    pb = mmwt(h, w_b, p_b)
    rk = pa[:, 512:1024] * 0.125
    rv = pa[:, 1024:1536]
    kpe = pb[:, 640:768]
    kv = mmwt(rmsn(pb[:, 384:640], g_kv), w_ukv, p_ukv)
    if not is_ctx:
        rk = rope(rk, cs1, sn1)
        kpe = rope(kpe, cs1, sn1)
    k_full = jnp.concatenate([piece for sl in _HEAD_SL for piece in (kv[:, sl], kpe)], axis=-1)
    v = kv[:, 512:]
    if is_ctx:
        return rk, rv, k_full, v
    rq = rope(pa[:, 0:512], cs1, sn1)
    rg = pa[:, 1536:2048]
    q = rope(mmwt(rmsn(pb[:, 0:384], g_q), w_uq, p_uq), cs1, sn1, every=2)
    return rq, rk, rv, rg, q, k_full, v


def log_sigmoid(x):
    return jnp.minimum(x, 0.0) - jnp.log(1.0 + jnp.exp(-jnp.abs(x)))


def ret_chunk(q, k, v, s, lg, reverse):
    c = RET_CHUNK
    ii = lax.broadcasted_iota(jnp.int32, (c, c), 0).astype(F32)
    jj = lax.broadcasted_iota(jnp.int32, (c, c), 1).astype(F32)
    diff = (jj - ii) if reverse else (ii - jj)
    dec = jnp.where(diff >= 0, jnp.exp(lg * jnp.maximum(diff, 0.0)), 0.0)
    pos = lax.broadcasted_iota(jnp.int32, (c, 1), 0).astype(F32)
    if reverse:
        wk, wq = jnp.exp(lg * pos), jnp.exp(lg * (c - pos))
    else:
        wk, wq = jnp.exp(lg * (c - 1.0 - pos)), jnp.exp(lg * (pos + 1.0))
    o = mm(mm_nt(q, k) * dec, v) + mm(q * wq, s)
    s_next = jnp.exp(lg * float(c)) * s + mm_tn(k * wk, v)
    return o, s_next


def ctx_state(kc, vc, lg, reverse):
    n = kc.shape[0]
    pos = lax.broadcasted_iota(jnp.int32, (n, 1), 0).astype(F32)
    w = jnp.exp(lg * pos) if reverse else jnp.exp(lg * (n - 1.0 - pos))
    return mm_tn(kc * w, vc)


def attn_head(qn, qp, kn, kp, v):
    s = (mm_nt(qn, kn) + mm_nt(qp, kp)) * (1.0 / math.sqrt(192.0))
    e = jnp.exp(s - jnp.max(s, axis=-1, keepdims=True))
    return mm(e / jnp.sum(e, axis=-1, keepdims=True), v)


def gn_gate(o, rg, g_ret):
    ys = []
    for h in range(N_HEADS):
        sl = slice(LANE * h, LANE * (h + 1))
        oh = o[:, sl]
        mu = jnp.mean(oh, axis=-1, keepdims=True)
        var = jnp.mean(jnp.square(oh - mu), axis=-1, keepdims=True)
        ys.append((oh - mu) * lax.rsqrt(var + EPS) * g_ret[:, sl])
    return jnp.concatenate(ys, axis=-1) * silu(rg)


def k4a_tile(x, o_f, o_b, rg, y_mla, g_ret, gt_a, g_ffn, sh_f, sc_f, w_out, p_out):
    mix = jnp.concatenate([gn_gate(o_f + o_b, rg, g_ret), y_mla], axis=-1)
    x_mid = x + gt_a * mmw(mix, w_out, p_out)
    h2 = rmsn(x_mid, g_ffn) * (1.0 + sc_f) + sh_f
    return x_mid, h2


def k4c_tile(x_mid, mlp, gt_f, g_final, tgt):
    y = rmsn(x_mid + gt_f * mlp, g_final)
    per_tok = jnp.mean(jnp.square(y - tgt), axis=-1, keepdims=True)
    return 0.5 * jnp.sum(per_tok, axis=0, keepdims=True)


def _cp(sem=None, vmem=VMEM_LIMIT):
    return pltpu.CompilerParams(dimension_semantics=sem, vmem_limit_bytes=vmem)


def _acc(ref, val, first):
    @pl.when(first)
    def _():
        ref[...] = val

    @pl.when(jnp.logical_not(first))
    def _():
        ref[...] += val


def _full(shape):
    nd = len(shape)
    return pl.BlockSpec(shape, lambda *_: (0,) * nd)


ANY = pl.BlockSpec(memory_space=pl.ANY)


def _sds(shape, dtype=F32):
    return jax.ShapeDtypeStruct(shape, dtype)


def _exchange(arrs, gather, name):
    n = len(arrs)
    modes = [gather] * n if isinstance(gather, bool) else list(gather)
    out_shape = [_sds(((N_DEV,) + a.shape) if g else a.shape, a.dtype) for a, g in zip(arrs, modes)]

    def body(*refs):
        ins, outs = refs[:n], refs[n:2 * n]
        send_sems, recv_sems, local_sems = refs[2 * n:]
        x, y, c = lax.axis_index("x"), lax.axis_index("y"), lax.axis_index("c")
        me = 4 * x + 2 * y + c
        sends, recvs, locs = [], [], []
        for i in range(n):
            gather = modes[i]
            for k in range(N_DEV - 1):
                bits = k + 1
                px = x ^ ((bits >> 2) & 1)
                py = y ^ ((bits >> 1) & 1)
                pc = c ^ (bits & 1)
                peer = 4 * px + 2 * py + pc
                src = ins[i] if gather else ins[i].at[peer]
                sem = i * (N_DEV - 1) + k
                sends.append(pltpu.make_async_remote_copy(
                    src_ref=src, dst_ref=outs[i].at[me], send_sem=send_sems.at[sem], recv_sem=recv_sems.at[sem],
                    device_id=(px, py, pc), device_id_type=MESH))
                recvs.append(pltpu.make_async_remote_copy(
                    src_ref=src, dst_ref=outs[i].at[peer], send_sem=send_sems.at[sem], recv_sem=recv_sems.at[sem],
                    device_id=(px, py, pc), device_id_type=MESH))
            locs.append(pltpu.make_async_copy(ins[i] if gather else ins[i].at[me], outs[i].at[me], local_sems.at[i]))
        for cp in locs + sends:
            cp.start()
        for cp in recvs:
            cp.wait_recv()
        for cp in sends:
            cp.wait_send()
        for cp in locs:
            cp.wait()

    outs = pl.pallas_call(
        body, name=name, out_shape=out_shape, in_specs=[ANY] * n, out_specs=[ANY] * n,
        scratch_shapes=[pltpu.SemaphoreType.DMA((n * (N_DEV - 1),)), pltpu.SemaphoreType.DMA((n * (N_DEV - 1),)),
                        pltpu.SemaphoreType.DMA((n,))],
    )(*arrs)
    return list(outs)


def _gather_two_level(arrs, name):
    n = len(arrs)

    def body(*refs):
        ins, outs = refs[:n], refs[n:2 * n]
        send_sems, recv_sems, local_sems = refs[2 * n:]
        x, y, c = lax.axis_index("x"), lax.axis_index("y"), lax.axis_index("c")
        sibling = (x, y, 1 - c)
        chips = [(1 - x, y), (x, 1 - y), (1 - x, 1 - y)]

        def slot(px, py, pc):
            return 4 * px + 2 * py + pc

        first, passed, waits, locs = [], [], [], []
        for i in range(n):
            def copy(k, block, to, src=None, i=i):
                dst = outs[i].at[slot(*block)]
                return pltpu.make_async_remote_copy(
                    src_ref=dst if src is None else src, dst_ref=dst, send_sem=send_sems.at[7 * i + k],
                    recv_sem=recv_sems.at[7 * i + k], device_id=to, device_id_type=MESH)

            locs.append(pltpu.make_async_copy(ins[i], outs[i].at[slot(x, y, c)], local_sems.at[i]))
            first.append(copy(0, (x, y, c), sibling, src=ins[i]))
            first += [copy(1 + j, (x, y, c), (*chip, c), src=ins[i]) for j, chip in enumerate(chips)]
            passed.append([copy(4 + j, (*chip, c), sibling) for j, chip in enumerate(chips)])
            waits.append([copy(1 + j, (*chip, c), (x, y, c)) for j, chip in enumerate(chips)])
        for cp in locs + first:
            cp.start()
        for j in range(3):
            for i in range(n):
                waits[i][j].wait_recv()
                passed[i][j].start()
        for i in range(n):
            def arrival(k, block, i=i):
                dst = outs[i].at[slot(*block)]
                return pltpu.make_async_remote_copy(
                    src_ref=dst, dst_ref=dst, send_sem=send_sems.at[7 * i + k], recv_sem=recv_sems.at[7 * i + k],
                    device_id=sibling, device_id_type=MESH)

            arrival(0, (x, y, 1 - c)).wait_recv()
            for j, chip in enumerate(chips):
                arrival(4 + j, (*chip, 1 - c)).wait_recv()
        for cp in first + [p for ps in passed for p in ps]:
            cp.wait_send()
        for cp in locs:
            cp.wait()

    outs = pl.pallas_call(
        body, name=name, out_shape=[_sds((N_DEV,) + a.shape, a.dtype) for a in arrs], in_specs=[ANY] * n,
        out_specs=[ANY] * n,
        scratch_shapes=[pltpu.SemaphoreType.DMA((7 * n,)), pltpu.SemaphoreType.DMA((7 * n,)),
                        pltpu.SemaphoreType.DMA((n,))],
    )(*arrs)
    return list(outs)


HBM = pl.BlockSpec(memory_space=pltpu.HBM)
SEM = pl.BlockSpec(memory_space=pltpu.SEMAPHORE)
EFFECT = pltpu.SideEffectType.DATAFLOW_SIDE_EFFECTING


def _peer(k, chips=False):
    x, y, c = lax.axis_index("x"), lax.axis_index("y"), lax.axis_index("c")
    bits = (k + 1) << 1 if chips else k + 1
    px, py, pc = x ^ ((bits >> 2) & 1), y ^ ((bits >> 1) & 1), c ^ (bits & 1)
    if chips:
        return (px, py, pc), 2 * px + py, 2 * x + y
    return (px, py, pc), 4 * px + 2 * py + pc, 4 * x + 2 * y + c


def _exchange_start(arrs, gather, name, after=None, chips=False):
    n = len(arrs)
    n_peer, n_slot = (3, 4) if chips else (N_DEV - 1, N_DEV)
    lands = [pltpu.with_memory_space_constraint(lax.empty(((n_slot,) + a.shape) if gather else a.shape, a.dtype),
                                                pltpu.HBM) for a in arrs]
    srcs = [pltpu.with_memory_space_constraint(a, pltpu.HBM) for a in arrs]

    extra = [] if after is None else [after]

    def body(*refs):
        ins, zones = refs[:n], refs[n:2 * n]
        send_sems, recv_sems, local_sems = refs[2 * n + len(extra):2 * n + len(extra) + 3]
        token = refs[-1]
        for i in range(n):
            for k in range(n_peer):
                dev, peer, me = _peer(k, chips)
                sem = i * n_peer + k
                pltpu.make_async_remote_copy(
                    src_ref=ins[i] if gather else ins[i].at[peer], dst_ref=zones[i].at[me],
                    send_sem=send_sems.at[sem], recv_sem=recv_sems.at[sem], device_id=dev, device_id_type=MESH).start()
            _, _, me = _peer(0, chips)
            pltpu.make_async_copy(ins[i] if gather else ins[i].at[me], zones[i].at[me], local_sems.at[i]).start()
        token[...] = jnp.zeros_like(token)

    nsem = n * n_peer
    outs = pl.pallas_call(
        body, name=name,
        out_shape=[pltpu.SemaphoreType.DMA((nsem,)), pltpu.SemaphoreType.DMA((nsem,)), pltpu.SemaphoreType.DMA((n,))]
        + [pltpu.HBM(a.shape, a.dtype) for a in srcs] + [pltpu.HBM(z.shape, z.dtype) for z in lands]
        + [_sds((8, LANE))],
        in_specs=[HBM] * (2 * n) + [ANY] * len(extra),
        out_specs=[SEM, SEM, SEM] + [HBM] * (2 * n) + [pl.BlockSpec(memory_space=pltpu.VMEM)],
        input_output_aliases={i: 3 + i for i in range(2 * n)},
        compiler_params=pltpu.CompilerParams(has_side_effects=EFFECT),
    )(*srcs, *lands, *extra)
    return {"n": n, "gather": gather, "chips": chips, "sems": outs[:3], "srcs": outs[3:3 + n],
            "lands": outs[3 + n:3 + 2 * n], "token": outs[-1]}


def _exchange_wait(st, after, name):
    n, gather, chips = st["n"], st["gather"], st["chips"]
    n_peer = 3 if chips else N_DEV - 1

    def body(*refs):
        ins, zones = refs[:n], refs[n:2 * n]
        send_sems, recv_sems, local_sems = refs[2 * n:2 * n + 3]
        for i in range(n):
            for k in range(n_peer):
                dev, peer, me = _peer(k, chips)
                sem = i * n_peer + k
                src = ins[i] if gather else ins[i].at[peer]
                cp = pltpu.make_async_remote_copy(
                    src_ref=src, dst_ref=zones[i].at[peer], send_sem=send_sems.at[sem], recv_sem=recv_sems.at[sem],
                    device_id=dev, device_id_type=MESH)
                cp.wait_send()
                cp.wait_recv()
            _, _, me = _peer(0, chips)
            pltpu.make_async_copy(ins[i] if gather else ins[i].at[me], zones[i].at[me], local_sems.at[i]).wait()

    outs = pl.pallas_call(
        body, name=name,
        out_shape=[pltpu.HBM(a.shape, a.dtype) for a in st["srcs"]] + [pltpu.HBM(z.shape, z.dtype) for z in st["lands"]],
        in_specs=[HBM] * (2 * n) + [SEM, SEM, SEM, ANY], out_specs=[HBM] * (2 * n),
        input_output_aliases={i: i for i in range(2 * n)},
        compiler_params=pltpu.CompilerParams(has_side_effects=EFFECT),
    )(*st["srcs"], *st["lands"], *st["sems"], after)
    return list(outs[n:])


def _pair_reduce(arrs, name):
    n = len(arrs)

    def body(*refs):
        ins, outs, got, mine = refs[:n], refs[n:2 * n], refs[2 * n:3 * n], refs[3 * n:4 * n]
        send_sems, recv_sems, local_sems = refs[4 * n:]
        x, y, c = lax.axis_index("x"), lax.axis_index("y"), lax.axis_index("c")
        sends, locs = [], []
        for i in range(n):
            for q in range(4):
                sem = 4 * i + q
                sends.append(pltpu.make_async_remote_copy(
                    src_ref=ins[i].at[2 * q + 1 - c], dst_ref=got[i].at[q], send_sem=send_sems.at[sem],
                    recv_sem=recv_sems.at[sem], device_id=(x, y, 1 - c), device_id_type=MESH))
                locs.append(pltpu.make_async_copy(ins[i].at[2 * q + c], mine[i].at[q], local_sems.at[sem]))
        for cp in locs + sends:
            cp.start()
        for cp in sends:
            cp.wait_recv()
        for cp in locs:
            cp.wait()
        for i in range(n):
            outs[i][...] = (mine[i][...].astype(F32) + got[i][...].astype(F32)).astype(BF)
        for cp in sends:
            cp.wait_send()

    half = [(4,) + a.shape[1:] for a in arrs]
    outs = pl.pallas_call(
        body, name=name, out_shape=[_sds(h, BF) for h in half], in_specs=[ANY] * n,
        out_specs=[pl.BlockSpec(memory_space=pltpu.VMEM)] * n,
        scratch_shapes=[pltpu.VMEM(h, BF) for h in half] * 2
        + [pltpu.SemaphoreType.DMA((4 * n,)), pltpu.SemaphoreType.DMA((4 * n,)), pltpu.SemaphoreType.DMA((4 * n,))],
        compiler_params=_cp(),
    )(*arrs)
    return list(outs)


def _mod_fwd(crows, w_ada, b_blk):
    def body(c_ref, w_ref, b_ref, o_ref):
        o_ref[...] = mm(silu(c_ref[...]), w_ref[...]) + b_ref[...]

    return pl.pallas_call(body, name="mod_fwd", out_shape=_sds((24, 768)), compiler_params=_cp())(crows, w_ada, b_blk)


def _mod_bwd(crows, w_ada, dmod_blk, dmodc_blk):
    def body(c_ref, w_ref, d_ref, dc_ref, gw_ref, gc_ref, gb_ref):
        cr = c_ref[...]
        dc = dc_ref[0:1, :]
        for p in range(1, N_DEV):
            dc = dc + dc_ref[p:p + 1, :]
        row = lax.broadcasted_iota(jnp.int32, (24, 1), 0)
        gw_ref[...] = mm_tn(silu(cr), jnp.where(row == 16, dc, d_ref[...]))
        cc = cr[16:17, :]
        sg = jax.nn.sigmoid(cc)
        part = mm_nt(jnp.broadcast_to(dc, (8, 768)), w_ref[...])
        gc_ref[...] = part * (sg * (1.0 + cc * (1.0 - sg)))
        gb_ref[...] = jnp.broadcast_to(jnp.sum(d_ref[...], axis=0, keepdims=True) + dc, (8, 768))

    return pl.pallas_call(
        body, name="mod_bwd", out_shape=[_sds((D_MODEL, 768)), _sds((8, D_MODEL)), _sds((8, 768))],
        compiler_params=_cp())(crows, w_ada, dmod_blk, dmodc_blk)


def _tab_specs(tk):
    return [pl.BlockSpec((tk, LANE), lambda i, t: (t, 0))] * 2


def _k1_fwd(x, mod, g_attn, g_q, g_kv, ws, tabs, kv_all, is_ctx):
    b, l, _ = x.shape
    nt = l // TOK
    n_f32 = 2 if is_ctx else 4

    def body(x_ref, mod_ref, ga_ref, gq_ref, gk_ref, wa_ref, wb_ref, wq_ref, wk_ref, cs_ref, sn_ref, *rest):
        outs = rest if is_ctx else rest[2:]
        res = k1_tile(x_ref[...], mod_ref[0:1, :], mod_ref[1:2, :], ga_ref[...], gq_ref[...], gk_ref[...],
                      (wa_ref[...], wb_ref[...], wq_ref[...], wk_ref[...]), (None,) * 4,
                      (cs_ref[...], sn_ref[...]), is_ctx)
        for o_ref, r in zip(outs, res):
            o_ref[...] = r.astype(o_ref.dtype)

    tok = lambda w, off=0: pl.BlockSpec((None, TOK, w), lambda i, t: (i, t + off, 0))
    mod_spec = pl.BlockSpec((None, 8, D_MODEL), (lambda i, t: (0, 0, 0)) if is_ctx else (lambda i, t: (i, 0, 0)))
    kv_off = 0 if is_ctx else CTX_LEN // TOK
    in_specs = ([tok(D_MODEL), mod_spec, _full((1, D_MODEL)), _full((1, 384)), _full((1, 256))]
                + [_full(s) for s in W_SHAPES] + _tab_specs(TOK))
    args = [x, mod, g_attn, g_q, g_kv, *ws, *tabs]
    out_specs = [tok(512)] * n_f32 + ([] if is_ctx else [tok(1024)]) + [tok(1024, kv_off), tok(512, kv_off)]
    out_shape = ([_sds((b, l, 512))] * n_f32 + ([] if is_ctx else [_sds((b, l, 1024), BF)])
                 + [_sds((b, KV_LEN, 1024), BF), _sds((b, KV_LEN, 512), BF)])
    aliases = {}
    if not is_ctx:
        aliases = {len(args): n_f32 + 1, len(args) + 1: n_f32 + 2}
        in_specs += [ANY, ANY]
        args += list(kv_all)
    return pl.pallas_call(
        body, name="k1_fwd_ctx" if is_ctx else "k1_fwd", grid=(b, nt), in_specs=in_specs, out_specs=out_specs,
        out_shape=out_shape, input_output_aliases=aliases, compiler_params=_cp((ARB, ARB)),
    )(*args)


N_ACC = 7


def _k1_bwd(x, ctx, mod, mod_c, g_attn, g_q, g_kv, ws, tabs, cts, cts_c, dx_res):
    b, l, _ = x.shape
    tk = TOK_B
    nt = l // tk
    flat = [[a for group in c for a in group] for c in (cts, cts_c)]
    sizes = [[len(g) for g in c] for c in (cts, cts_c)]
    acc_shapes = W_SHAPES + [(1, D_MODEL), (1, 384), (1, 256)]

    def body(*refs):
        it = iter(refs)
        x_ref, c_ref, mod_ref, modc_ref, ga_ref, gq_ref, gk_ref = [next(it) for _ in range(7)]
        w_hbm = [next(it) for _ in range(4)]
        tab_refs = [next(it) for _ in range(2)]
        ct_refs = [[next(it) for _ in f] for f in flat]
        res_ref, gx_ref = next(it), next(it)
        out_hbm = [next(it) for _ in range(N_ACC)]
        dmod_ref, dmodc_ref = next(it), next(it)
        w_vmem = [next(it) for _ in range(4)]
        accs = [next(it) for _ in range(N_ACC)]
        sem = next(it)
        i, t = pl.program_id(0), pl.program_id(1)
        first = jnp.logical_and(i == 0, t == 0)

        @pl.when(first)
        def _():
            for src, dst in zip(w_hbm, w_vmem):
                pltpu.sync_copy(src, dst)
            for k in range(N_ACC):
                accs[k][...] = jnp.zeros(acc_shapes[k], F32)

        def tile(is_ctx):
            which = 1 if is_ctx else 0
            ct_vals, pos = [], 0
            for gsz in sizes[which]:
                v = ct_refs[which][pos][...].astype(F32)
                for r in ct_refs[which][pos + 1:pos + gsz]:
                    v = v + r[...]
                ct_vals.append(v)
                pos += gsz
            wv = tuple(r[...] for r in w_vmem)
            tv = tuple(r[...] for r in tab_refs)
            m_ref = modc_ref if is_ctx else mod_ref

            def f(xv, sh, sc, ga, gq, gk, *probes):
                return k1_tile(xv, sh, sc, ga, gq, gk, wv, probes, tv, is_ctx)

            probes = [jnp.zeros(s, F32) for s in W_SHAPES]
            xin = c_ref[...] if is_ctx else x_ref[...]
            _, vjp = jax.vjp(f, xin, m_ref[0:1, :], m_ref[1:2, :], ga_ref[...], gq_ref[...], gk_ref[...], *probes)
            dx, dsh, dsc, dga, dgq, dgk, dwa, dwb, dwq, dwk = vjp(tuple(ct_vals))
            for ref, val in zip(accs, (dwa, dwb, dwq, dwk, dga, dgq, dgk)):
                ref[...] += val
            return dx, dsh, dsc

        @pl.when(t == 0)
        def _():
            _, dsh, dsc = tile(True)
            _acc(dmodc_ref.at[0:1, :], dsh, i == 0)
            _acc(dmodc_ref.at[1:2, :], dsc, i == 0)

            @pl.when(i == 0)
            def _():
                dmodc_ref[2:8, :] = jnp.zeros((6, D_MODEL), F32)

        @pl.when(t > 0)
        def _():
            dx, dsh, dsc = tile(False)
            gx_ref[...] = dx + res_ref[...]
            _acc(dmod_ref.at[0:1, :], dsh, t == 1)
            _acc(dmod_ref.at[1:2, :], dsc, t == 1)

            @pl.when(t == 1)
            def _():
                dmod_ref[2:8, :] = jnp.zeros((6, D_MODEL), F32)

        @pl.when(jnp.logical_and(i == b - 1, t == nt))
        def _():
            for k in range(4):
                w_vmem[k][...] = accs[k][...].astype(BF)
            cps = [pltpu.make_async_copy(w_vmem[k] if k < 4 else accs[k], out_hbm[k], sem.at[k]) for k in range(N_ACC)]
            for cp in cps:
                cp.start()
            for cp in cps:
                cp.wait()

    lat = lambda w, off=0: pl.BlockSpec((None, tk, w), lambda i, t: (i, jnp.maximum(t - 1, 0) + off, 0))
    con = lambda w: pl.BlockSpec((None, tk, w), lambda i, t: (i, 0, 0))
    mod_spec = pl.BlockSpec((None, 8, D_MODEL), lambda i, t: (i, 0, 0))
    modc_spec = pl.BlockSpec((None, 8, D_MODEL), lambda i, t: (0, 0, 0))
    tab_spec = pl.BlockSpec((tk, LANE), lambda i, t: (jnp.maximum(t - 1, 0), 0))
    in_specs = ([lat(D_MODEL), con(D_MODEL), mod_spec, modc_spec, _full((1, D_MODEL)), _full((1, 384)), _full((1, 256))]
                + [ANY] * 4 + [tab_spec] * 2)
    args = [x, ctx, mod, mod_c, g_attn, g_q, g_kv, *ws, *tabs]
    for a, off in flat[0]:
        in_specs.append(lat(a.shape[-1], off // tk))
        args.append(a)
    for a, off in flat[1]:
        assert off == 0
        in_specs.append(con(a.shape[-1]))
        args.append(a)
    in_specs.append(lat(D_MODEL))
    args.append(dx_res)
    out_shape = ([_sds((b, l, D_MODEL))] + [_sds(s, BF) for s in W_SHAPES] + [_sds(s) for s in acc_shapes[4:]]
                 + [_sds((b, 8, D_MODEL)), _sds((1, 8, D_MODEL))])
    out_specs = [lat(D_MODEL)] + [ANY] * N_ACC + [mod_spec, modc_spec]
    outs = pl.pallas_call(
        body, name="k1_bwd", grid=(b, nt + 1), in_specs=in_specs, out_specs=out_specs, out_shape=out_shape,
        scratch_shapes=[pltpu.VMEM(s, BF) for s in W_SHAPES] + [pltpu.VMEM(s, F32) for s in acc_shapes]
        + [pltpu.SemaphoreType.DMA((N_ACC,))],
        compiler_params=_cp((ARB, ARB)),
    )(*args)
    return outs[0], list(outs[1:1 + N_ACC]), outs[1 + N_ACC], outs[2 + N_ACC]


def _chunk_spec(rev):
    if rev:
        return pl.BlockSpec((None, RET_CHUNK, 512), lambda i, n: (i, N_CHUNK - 1 - n, 0))
    return pl.BlockSpec((None, RET_CHUNK, 512), lambda i, n: (i, n, 0))


def _state_spec(rev):
    if rev:
        return pl.BlockSpec((None, N_HEADS, None, LANE, LANE), lambda i, n: (i, 0, N_CHUNK - 1 - n, 0, 0))
    return pl.BlockSpec((None, N_HEADS, None, LANE, LANE), lambda i, n: (i, 0, n, 0, 0))


_CTX_SPEC = pl.BlockSpec((None, CTX_LEN, 512), lambda i, n: (i, 0, 0))
_DEC_SPEC = pl.BlockSpec((N_HEADS, 1, 1), lambda i, n: (0, 0, 0))


def _k2_fwd(rq, rk, rv, rkc, rvc, dec_f, dec_b):
    b = rq.shape[0]

    def body(qf, kf, vf, qb, kb, vb, kc, vc, df, db, of_ref, ob_ref, sf_out, sb_out, sf, sb):
        n = pl.program_id(1)
        for h, sl in enumerate(_HEAD_SL):
            lgf, lgb = log_sigmoid(df[h]), log_sigmoid(db[h])

            @pl.when(n == 0)
            def _():
                sf[h] = ctx_state(kc[:, sl], vc[:, sl], lgf, False)
                sb[h] = ctx_state(kc[:, sl], vc[:, sl], lgb, True)

            sf_out[h] = sf[h]
            sb_out[h] = sb[h]
            o, s = ret_chunk(qf[:, sl], kf[:, sl], vf[:, sl], sf[h], lgf, False)
            of_ref[:, sl] = o
            sf[h] = s
            o, s = ret_chunk(qb[:, sl], kb[:, sl], vb[:, sl], sb[h], lgb, True)
            ob_ref[:, sl] = o
            sb[h] = s

    l = rq.shape[1]
    return pl.pallas_call(
        body, name="k2_fwd", grid=(b, N_CHUNK),
        in_specs=[_chunk_spec(False)] * 3 + [_chunk_spec(True)] * 3 + [_CTX_SPEC, _CTX_SPEC, _DEC_SPEC, _DEC_SPEC],
        out_specs=[_chunk_spec(False), _chunk_spec(True), _state_spec(False), _state_spec(True)],
        out_shape=[_sds((b, l, 512)), _sds((b, l, 512)), _sds((b, N_HEADS, N_CHUNK, LANE, LANE)),
                   _sds((b, N_HEADS, N_CHUNK, LANE, LANE))],
        scratch_shapes=[pltpu.VMEM((N_HEADS, LANE, LANE), F32), pltpu.VMEM((N_HEADS, LANE, LANE), F32)],
        compiler_params=_cp((ARB, ARB)),
    )(rq, rk, rv, rq, rk, rv, rkc, rvc, dec_f, dec_b)


def _k2_bwd(rq, rk, rv, do, sf_prev, sb_prev, rkc, rvc, dec_f, dec_b):
    b, l, _ = rq.shape

    def body(qf, kf, vf, gf, spf, qb, kb, vb, gb, spb, kc, vc, df, db,
             dqf, dkf, dvf, dqb, dkb, dvb, dkc, dvc, ddf, ddb, dsf, dsb):
        n = pl.program_id(1)

        @pl.when(n == 0)
        def _():
            dsf[...] = jnp.zeros((N_HEADS, LANE, LANE), F32)
            dsb[...] = jnp.zeros((N_HEADS, LANE, LANE), F32)

        def one(h, sl, q, k, v, g, sp, dec, ds, dq, dk, dv, dd, rev):
            def f(qv, kv_, vv, sv, dcy):
                return ret_chunk(qv, kv_, vv, sv, log_sigmoid(dcy), rev)

            _, vjp = jax.vjp(f, q[:, sl], k[:, sl], v[:, sl], sp[h], dec[h])
            gq, gk, gv, gs, gd = vjp((g[:, sl], ds[h]))
            dq[:, sl] = gq
            dk[:, sl] = gk
            dv[:, sl] = gv
            ds[h] = gs
            _acc(dd.at[h], jnp.broadcast_to(gd, (8, LANE)), n == 0)

        for h, sl in enumerate(_HEAD_SL):
            one(h, sl, qf, kf, vf, gf, spf, df, dsf, dqf, dkf, dvf, ddf, False)
            one(h, sl, qb, kb, vb, gb, spb, db, dsb, dqb, dkb, dvb, ddb, True)

        @pl.when(n == N_CHUNK - 1)
        def _():
            def f(kcv, vcv, dcy, rev):
                return ctx_state(kcv, vcv, log_sigmoid(dcy), rev)

            for h, sl in enumerate(_HEAD_SL):
                _, vjp_f = jax.vjp(functools.partial(f, rev=False), kc[:, sl], vc[:, sl], df[h])
                gk_f, gv_f, gd_f = vjp_f(dsf[h])
                _, vjp_b = jax.vjp(functools.partial(f, rev=True), kc[:, sl], vc[:, sl], db[h])
                gk_b, gv_b, gd_b = vjp_b(dsb[h])
                dkc[:, sl] = gk_f + gk_b
                dvc[:, sl] = gv_f + gv_b
                ddf[h] += jnp.broadcast_to(gd_f, (8, LANE))
                ddb[h] += jnp.broadcast_to(gd_b, (8, LANE))

    dd_spec = pl.BlockSpec((None, N_HEADS, 8, LANE), lambda i, n: (i, 0, 0, 0))
    return pl.pallas_call(
        body, name="k2_bwd", grid=(b, N_CHUNK),
        in_specs=[_chunk_spec(True)] * 4 + [_state_spec(True)] + [_chunk_spec(False)] * 4 + [_state_spec(False)]
        + [_CTX_SPEC, _CTX_SPEC, _DEC_SPEC, _DEC_SPEC],
        out_specs=[_chunk_spec(True)] * 3 + [_chunk_spec(False)] * 3 + [_CTX_SPEC, _CTX_SPEC, dd_spec, dd_spec],
        out_shape=[_sds((b, l, 512))] * 6 + [_sds((b, CTX_LEN, 512))] * 2 + [_sds((b, N_HEADS, 8, LANE))] * 2,
        scratch_shapes=[pltpu.VMEM((N_HEADS, LANE, LANE), F32), pltpu.VMEM((N_HEADS, LANE, LANE), F32)],
        compiler_params=_cp((ARB, ARB)),
    )(rq, rk, rv, do, sf_prev, rq, rk, rv, do, sb_prev, rkc, rvc, dec_f, dec_b)


TQ = 512
QK_W = 2 * LANE
N_QP = 2
_Q_PARTS = [slice(i * TQ // N_QP, (i + 1) * TQ // N_QP) for i in range(N_QP)]


SM_SCALE = 1.0 / math.sqrt(192.0)


def _k3_specs():
    qs = lambda w: pl.BlockSpec((None, TQ, w), lambda i, h, t: (i, t, h))
    ks = lambda w: pl.BlockSpec((None, KV_LEN, w), lambda i, h, t: (i, 0, h))
    return qs, ks


def _k3_fwd(q, k, v):
    b, l, _ = q.shape

    def body(q_ref, k_ref, v_ref, o_ref, lse_ref):
        kv_, vv = k_ref[...], v_ref[...]
        for r in _Q_PARTS:
            s = _dot(q_ref[r, :], kv_, 1, 1) * SM_SCALE
            m = jnp.max(s, axis=-1, keepdims=True)
            e = jnp.exp(s - m)
            tot = jnp.sum(e, axis=-1, keepdims=True)
            o_ref[r, :] = _dot(e, vv, 1, 0) * (1.0 / tot)
            lse_ref[r, :] = jnp.broadcast_to(m + jnp.log(tot), (TQ // N_QP, LANE))

    qs, ks = _k3_specs()
    return pl.pallas_call(
        body, name="k3_fwd", grid=(b, N_HEADS, l // TQ), in_specs=[qs(QK_W), ks(QK_W), ks(LANE)],
        out_specs=[qs(LANE), qs(LANE)], out_shape=[_sds((b, l, N_HEADS * LANE))] * 2,
        compiler_params=_cp((ARB, ARB, ARB)),
    )(q, k, v)


def _k3_bwd(q, k, v, o, lse, dy, after):
    b, l, _ = q.shape

    def body(q_ref, k_ref, v_ref, o_ref, lse_ref, dy_ref, after_ref, dq_ref, dk_ref, dv_ref):
        t0 = pl.program_id(2) == 0
        kv_, vv = k_ref[...], v_ref[...]
        qv, dyv = q_ref[...], dy_ref[...]
        g = dyv.astype(BF)
        lse_col = jnp.max(lse_ref[...], axis=-1, keepdims=True)
        delta = jnp.sum(dyv * o_ref[...], axis=-1, keepdims=True)
        p = jnp.exp(_dot(qv, kv_, 1, 1) * SM_SCALE - lse_col)
        ds = (p * (_dot(g, vv, 1, 1) - delta) * SM_SCALE).astype(BF)
        _acc(dv_ref, _dot(p, g, 0, 0), t0)
        dq_ref[...] = _dot(ds, kv_, 1, 0)
        _acc(dk_ref, _dot(ds, qv, 0, 0), t0)

    qs, ks = _k3_specs()
    return pl.pallas_call(
        body, name="k3_bwd", grid=(b, N_HEADS, l // TQ),
        in_specs=[qs(QK_W), ks(QK_W), ks(LANE), qs(LANE), qs(LANE), qs(LANE), ANY],
        out_specs=[qs(QK_W), ks(QK_W), ks(LANE)],
        out_shape=[_sds((b, l, N_HEADS * QK_W)), _sds((b, KV_LEN, N_HEADS * QK_W)), _sds((b, KV_LEN, N_HEADS * LANE))],
        compiler_params=_cp((ARB, ARB, ARB)),
    )(q, k, v, o, lse, dy, after)


def _mod_rows(mod_ref, rows):
    return [mod_ref[r:r + 1, :] for r in rows]


def _k4a_fwd(x, o_f, o_b, rg, y_mla, g_ret, w_out, mod, g_ffn):
    b, l, _ = x.shape

    def body(x_ref, of_ref, ob_ref, rg_ref, ym_ref, gr_ref, wo_ref, mod_ref, gf_ref, xm_ref, h2_ref):
        gt_a, sh_f, sc_f = _mod_rows(mod_ref, (2, 3, 4))
        x_mid, h2 = k4a_tile(x_ref[...], of_ref[...], ob_ref[...], rg_ref[...], ym_ref[...], gr_ref[...], gt_a,
                             gf_ref[...], sh_f, sc_f, wo_ref[...], None)
        xm_ref[...] = x_mid
        h2_ref[...] = h2.astype(BF)

    tok = lambda w: pl.BlockSpec((None, TOK, w), lambda i, t: (i, t, 0))
    mod_spec = pl.BlockSpec((None, 8, D_MODEL), lambda i, t: (i, 0, 0))
    return pl.pallas_call(
        body, name="k4a_fwd", grid=(b, l // TOK),
        in_specs=[tok(D_MODEL), tok(512), tok(512), tok(512), tok(512), _full((1, 512)), _full((D_MODEL, D_MODEL)),
                  mod_spec, _full((1, D_MODEL))],
        out_specs=[tok(D_MODEL), tok(D_MODEL)], out_shape=[_sds((b, l, D_MODEL)), _sds((b, l, D_MODEL), BF)],
        compiler_params=_cp((ARB, ARB)),
    )(x, o_f, o_b, rg, y_mla, g_ret, w_out, mod, g_ffn)


TOK_M = 512
TOK_D = 1024
HALF_FF = D_FF // 2


def _k4b_mlp_loss(h2, w1t, w2, x_mid, mod, g_final, tgt):
    b, l, _ = h2.shape
    nt = l // TOK_M

    def body(h2_ref, w1_hbm, w2_hbm, xm_ref, mod_ref, gfin_ref, tgt_ref, dxm_ref, dmlp_ref, r_ref, loss_ref, dgt_ref,
             dgfin_ref, w1_v, w2_v):
        i, t = pl.program_id(0), pl.program_id(1)
        first = jnp.logical_and(i == 0, t == 0)

        @pl.when(first)
        def _():
            pltpu.sync_copy(w1_hbm, w1_v)
            pltpu.sync_copy(w2_hbm, w2_v)

        h2v = h2_ref[...]
        mlp = None
        for half in range(2):
            rows = slice(half * HALF_FF, (half + 1) * HALF_FF)
            r = jnp.maximum(_dot(h2v, w1_v[rows, :], 1, 1), 0.0)
            r_ref[:, rows] = r.astype(BF)
            part = _dot(jnp.square(r), w2_v[rows, :], 1, 0)
            mlp = part if mlp is None else mlp + part
        (gt_f,) = _mod_rows(mod_ref, (5,))
        loss, vjp = jax.vjp(k4c_tile, xm_ref[...], mlp, gt_f, gfin_ref[...], tgt_ref[...])
        dxm, dmlp, dgt, dgfin, _ = vjp(jnp.ones((1, 1), F32))
        dxm_ref[...] = dxm
        dmlp_ref[...] = dmlp.astype(BF)
        _acc(loss_ref, jnp.broadcast_to(loss, (8, LANE)), first)
        _acc(dgfin_ref, dgfin, first)
        _acc(dgt_ref, dgt, t == 0)

    tok = lambda w: pl.BlockSpec((None, TOK_M, w), lambda i, t: (i, t, 0))
    return pl.pallas_call(
        body, name="k4b_mlp_loss", grid=(b, nt),
        in_specs=[tok(D_MODEL), ANY, ANY, tok(D_MODEL), pl.BlockSpec((None, 8, D_MODEL), lambda i, t: (i, 0, 0)),
                  _full((1, D_MODEL)), tok(D_MODEL)],
        out_specs=[tok(D_MODEL), tok(D_MODEL), tok(D_FF), _full((8, LANE)),
                   pl.BlockSpec((None, 1, D_MODEL), lambda i, t: (i, 0, 0)), _full((1, D_MODEL))],
        out_shape=[_sds((b, l, D_MODEL)), _sds((b, l, D_MODEL), BF), _sds((b, l, D_FF), BF), _sds((8, LANE)),
                   _sds((b, 1, D_MODEL)), _sds((1, D_MODEL))],
        scratch_shapes=[pltpu.VMEM((D_FF, D_MODEL), BF), pltpu.VMEM((D_FF, D_MODEL), BF)],
        compiler_params=_cp((ARB, ARB)),
    )(h2, w1t, w2, x_mid, mod, g_final, tgt)


def _k4d_mlp_bwd(h2, dmlp, r, w2):
    b, l, _ = h2.shape
    nt = l // TOK_D

    def body(h2_ref, dm_ref, r_ref, w2_ref, da_ref, dw1_ref, dw2_ref, acc1, acc2):
        i, t = pl.program_id(1), pl.program_id(2)
        first = jnp.logical_and(i == 0, t == 0)
        rv = r_ref[...].astype(F32)
        dm = dm_ref[...]
        da = (_dot(dm, w2_ref[...], 1, 1) * (2.0 * rv)).astype(BF)
        da_ref[...] = da
        _acc(acc2, _dot(jnp.square(rv), dm, 0, 0), first)
        _acc(acc1, _dot(h2_ref[...], da, 0, 0), first)

        @pl.when(jnp.logical_and(i == b - 1, t == nt - 1))
        def _():
            dw1_ref[...] = acc1[...].astype(BF)
            dw2_ref[...] = acc2[...].astype(BF)

    tok = lambda w: pl.BlockSpec((None, TOK_D, w), lambda j, i, t: (i, t, 0))
    col = pl.BlockSpec((None, TOK_D, FF_BLK), lambda j, i, t: (i, t, j))
    return pl.pallas_call(
        body, name="k4d_mlp_bwd", grid=(N_DEV, b, nt),
        in_specs=[tok(D_MODEL), tok(D_MODEL), col, pl.BlockSpec((None, FF_BLK, D_MODEL), lambda j, i, t: (j, 0, 0))],
        out_specs=[col, pl.BlockSpec((None, D_MODEL, FF_BLK), lambda j, i, t: (j, 0, 0)),
                   pl.BlockSpec((None, FF_BLK, D_MODEL), lambda j, i, t: (j, 0, 0))],
        out_shape=[_sds((b, l, D_FF), BF), _sds((N_DEV, D_MODEL, FF_BLK), BF), _sds((N_DEV, FF_BLK, D_MODEL), BF)],
        scratch_shapes=[pltpu.VMEM((D_MODEL, FF_BLK), F32), pltpu.VMEM((FF_BLK, D_MODEL), F32)],
        compiler_params=_cp((ARB, ARB, ARB)),
    )(h2, dmlp, r, w2)


def _k4e_bwd(x, o_f, o_b, rg, y_mla, g_ret, w_out, mod, g_ffn, dxm, da, w1t):
    b, l, _ = x.shape

    def body(x_ref, of_ref, ob_ref, rg_ref, ym_ref, gr_ref, wo_ref, mod_ref, gf_ref, dxm_ref, da_ref, w1_hbm,
             dx_ref, do_ref, drg_ref, dym_ref, dwo_ref, dgr_ref, dgf_ref, dmod_ref, w1_v):
        i, t = pl.program_id(0), pl.program_id(1)
        first = jnp.logical_and(i == 0, t == 0)

        @pl.when(first)
        def _():
            pltpu.sync_copy(w1_hbm, w1_v)

        gt_a, sh_f, sc_f = _mod_rows(mod_ref, (2, 3, 4))
        wo = wo_ref[...]
        dh2 = _dot(da_ref[...], w1_v[...], 1, 0)

        def f(xv, ofv, rgv, ymv, grv, gta, gfv, shf, scf, p_out):
            return k4a_tile(xv, ofv, ob_ref[...], rgv, ymv, grv, gta, gfv, shf, scf, wo, p_out)

        _, vjp = jax.vjp(f, x_ref[...], of_ref[...], rg_ref[...], ym_ref[...], gr_ref[...], gt_a, gf_ref[...], sh_f,
                         sc_f, jnp.zeros((D_MODEL, D_MODEL), F32))
        dx, do, drg, dym, dgr, dgta, dgf, dshf, dscf, dwo = vjp((dxm_ref[...], dh2))
        dx_ref[...] = dx
        do_ref[...] = do
        drg_ref[...] = drg
        dym_ref[...] = dym
        _acc(dwo_ref, dwo, first)
        _acc(dgr_ref, dgr, first)
        _acc(dgf_ref, dgf, first)
        t0 = t == 0
        _acc(dmod_ref.at[2:3, :], dgta, t0)
        _acc(dmod_ref.at[3:4, :], dshf, t0)
        _acc(dmod_ref.at[4:5, :], dscf, t0)

        @pl.when(t0)
        def _():
            dmod_ref[0:2, :] = jnp.zeros((2, D_MODEL), F32)
            dmod_ref[5:8, :] = jnp.zeros((3, D_MODEL), F32)

    tok = lambda w: pl.BlockSpec((None, TOK_B, w), lambda i, t: (i, t, 0))
    mod_spec = pl.BlockSpec((None, 8, D_MODEL), lambda i, t: (i, 0, 0))
    return pl.pallas_call(
        body, name="k4e_bwd", grid=(b, l // TOK_B),
        in_specs=[tok(D_MODEL), tok(512), tok(512), tok(512), tok(512), _full((1, 512)), _full((D_MODEL, D_MODEL)),
                  mod_spec, _full((1, D_MODEL)), tok(D_MODEL), tok(D_FF), ANY],
        out_specs=[tok(D_MODEL), tok(512), tok(512), tok(512), _full((D_MODEL, D_MODEL)), _full((1, 512)),
                   _full((1, D_MODEL)), mod_spec],
        out_shape=[_sds((b, l, D_MODEL)), _sds((b, l, 512)), _sds((b, l, 512)), _sds((b, l, 512)),
                   _sds((D_MODEL, D_MODEL)), _sds((1, 512)), _sds((1, D_MODEL)), _sds((b, 8, D_MODEL))],
        scratch_shapes=[pltpu.VMEM((D_FF, D_MODEL), BF)],
        compiler_params=_cp((ARB, ARB)),
    )(x, o_f, o_b, rg, y_mla, g_ret, w_out, mod, g_ffn, dxm, da, w1t)


def _adamw(w, m, v, pieces, name, after=None):
    r, c = w.shape
    npc = pieces.shape[0]
    per_row = c * (7 * 4 + npc * pieces.dtype.itemsize) * 2
    rb = r
    for cand in (r, 512, 256, 128, 64, 32, 16, 8):
        if r % cand == 0 and cand * per_row <= 32 * 1024 * 1024:
            rb = cand
            break

    def body(w_ref, m_ref, v_ref, p_ref, *rest):
        g_ref, d_ref, nm_ref, nv_ref = rest[-4:]
        g = p_ref[0].astype(F32)
        for k in range(1, npc):
            g = g + p_ref[k].astype(F32)
        wv = w_ref[...]
        mn = ADAM_B1 * m_ref[...] + (1.0 - ADAM_B1) * g
        vn = ADAM_B2 * v_ref[...] + (1.0 - ADAM_B2) * jnp.square(g)
        m_hat = mn / (1.0 - ADAM_B1 ** ADAM_STEP)
        v_hat = vn / (1.0 - ADAM_B2 ** ADAM_STEP)
        g_ref[...] = g
        d_ref[...] = -ADAM_LR * (m_hat / (jnp.sqrt(v_hat) + ADAM_EPS) + ADAM_WD * wv)
        nm_ref[...] = mn
        nv_ref[...] = vn

    blk = pl.BlockSpec((rb, c), lambda i: (i, 0))
    extra = [] if after is None else [after]
    return pl.pallas_call(
        body, name=name, grid=(r // rb,),
        in_specs=[blk, blk, blk, pl.BlockSpec((npc, rb, c), lambda i: (0, i, 0))] + [ANY] * len(extra),
        out_specs=[blk] * 4, out_shape=[_sds((r, c))] * 4, compiler_params=_cp((ARB,)),
    )(w, m, v, pieces, *extra)


def _pad_rot_rows(w):
    k = w.shape[1]
    return jnp.pad(w.reshape(-1, 2, 32, k), ((0, 0), (0, 0), (0, 32), (0, 0))).reshape(-1, k)


def _cut_rot_rows(g):
    k = g.shape[1]
    return g.reshape(-1, 2, 64, k)[:, :, :32].reshape(-1, k)


def _w_in_pad(wt):
    w_a = jnp.concatenate([_pad_rot_rows(wt[0:512]), wt[512:1536]], axis=0)
    w_b = jnp.concatenate([wt[1536:2176], _pad_rot_rows(wt[2176:2240])], axis=0)
    return w_a, w_b


def _w_in_cut(g_a, g_b):
    return jnp.concatenate([_cut_rot_rows(g_a[0:1024]), g_a[1024:2048], g_b[0:640], _cut_rot_rows(g_b[640:768])], axis=0)


def _w_uq_pad(wt):
    w = wt.reshape(N_HEADS, 192, 384)
    rot = _pad_rot_rows(w[:, 128:].reshape(N_HEADS * 64, 384)).reshape(N_HEADS, LANE, 384)
    return jnp.concatenate([w[:, :128], rot], axis=1).reshape(1024, 384)


def _w_uq_cut(g):
    g = g.reshape(N_HEADS, 256, 384)
    rot = _cut_rot_rows(g[:, 128:].reshape(N_HEADS * LANE, 384)).reshape(N_HEADS, 64, 384)
    return jnp.concatenate([g[:, :128], rot], axis=1).reshape(768, 384)


def _w_ukv_perm(wt):
    return jnp.transpose(wt.reshape(N_HEADS, 2, LANE, 256), (1, 0, 2, 3)).reshape(1024, 256)


def _w_ukv_unperm(g):
    return jnp.transpose(g.reshape(2, N_HEADS, LANE, 256), (1, 0, 2, 3)).reshape(1024, 256)


def _unshard_cols(g):
    return jnp.transpose(g, (1, 0, 2)).reshape(g.shape[1], N_DEV * g.shape[2])


def _rope_tables():
    rows = SEQ // GRID_W
    row = jnp.repeat(jnp.arange(rows, dtype=F32), GRID_W)
    col = jnp.tile(jnp.arange(GRID_W, dtype=F32), rows)
    freq = ROPE_BASE ** (-jnp.arange(16, dtype=F32) / 16)
    ang = jnp.concatenate([row[:, None] * freq, col[:, None] * freq], axis=-1)
    cos, sin = jnp.cos(ang), jnp.sin(ang)
    z = jnp.zeros((SEQ, 32), F32)
    return jnp.concatenate([cos, z, cos, z], axis=1), jnp.concatenate([-sin, z, sin, z], axis=1)


_PACKED = (("g_attn", 1024), ("g_ffn", 1024), ("ret_decay_fwd", 4), ("ret_decay_bwd", 4), ("g_ret", 512),
           ("g_q_lora", 384), ("g_kv_lora", 256), ("g_final", 1024))
_PACK_OFF = {}
_off = 0
for _name, _n in _PACKED:
    _PACK_OFF[_name] = _off
    _off += -(-_n // LANE) * LANE
PACK_W = _off


def _pack_small(vals):
    parts = []
    for name, n in _PACKED:
        a = vals[name].reshape(-1).astype(F32)
        parts.append(jnp.pad(a, (0, -(-n // LANE) * LANE - n)))
    return jnp.concatenate(parts).reshape(1, PACK_W)


def _adamw_small(params, packed, gcc, gb_ada):
    names = list(params)
    n_p = len(names)

    def body(*refs):
        p_ref, gcc_ref, gb_ref = refs[3 * n_p:3 * n_p + 3]
        outs = refs[3 * n_p + 3:]
        for k, name in enumerate(names):
            w_ref, m_ref, v_ref = refs[3 * k:3 * k + 3]
            n = w_ref.shape[1]
            if name == "b_ada":
                g = jnp.concatenate([gb_ref[d, 0:1, :] for d in range(N_DEV)], axis=-1)
            elif name == "c_ctx":
                g = gcc_ref[0, 0:1, :]
                for d in range(1, N_DEV):
                    g = g + gcc_ref[d, 0:1, :]
            else:
                off = _PACK_OFF[name]
                g = p_ref[0, :, off:off + n]
                for d in range(1, N_DEV):
                    g = g + p_ref[d, :, off:off + n]
            mn = ADAM_B1 * m_ref[...] + (1.0 - ADAM_B1) * g
            vn = ADAM_B2 * v_ref[...] + (1.0 - ADAM_B2) * jnp.square(g)
            m_hat = mn / (1.0 - ADAM_B1 ** ADAM_STEP)
            v_hat = vn / (1.0 - ADAM_B2 ** ADAM_STEP)
            outs[4 * k][...] = g
            outs[4 * k + 1][...] = -ADAM_LR * (m_hat / (jnp.sqrt(v_hat) + ADAM_EPS) + ADAM_WD * w_ref[...])
            outs[4 * k + 2][...] = mn
            outs[4 * k + 3][...] = vn

    args = [a for name in names for a in params[name]] + [packed, gcc, gb_ada]
    out_shape = [_sds(params[name][0].shape) for name in names for _ in range(4)]
    outs = pl.pallas_call(body, name="adamw_small", out_shape=out_shape, compiler_params=_cp())(*args)
    return {name: list(outs[4 * k:4 * k + 4]) for k, name in enumerate(names)}


def kernel(x, c, ctx, c_ctx, w_ada, b_ada, g_attn, g_ffn, w_in, ret_decay_fwd, ret_decay_bwd, g_ret, g_q_lora, w_uq, g_kv_lora, w_ukv, w_out, w_ff1, w_ff2, g_final, loss_target, m_c_ctx, m_w_ada, m_b_ada, m_g_attn, m_g_ffn, m_w_in, m_ret_decay_fwd, m_ret_decay_bwd, m_g_ret, m_g_q_lora, m_w_uq, m_g_kv_lora, m_w_ukv, m_w_out, m_w_ff1, m_w_ff2, m_g_final, v_c_ctx, v_w_ada, v_b_ada, v_g_attn, v_g_ffn, v_w_in, v_ret_decay_fwd, v_ret_decay_bwd, v_g_ret, v_g_q_lora, v_w_uq, v_g_kv_lora, v_w_ukv, v_w_out, v_w_ff1, v_w_ff2, v_g_final):
    me = 4 * lax.axis_index("x") + 2 * lax.axis_index("y") + lax.axis_index("c")
    nb = x.shape[0]

    c_pad = jnp.pad(c, ((0, 8 - nb), (0, 0)))
    c_all, g_in, g_uq, g_ukv = _gather_two_level(
        [c_pad, w_in[0].T.astype(BF), w_uq[0].T.astype(BF), w_ukv[0].T.astype(BF)], "gather_weights")
    ws = (*_w_in_pad(g_in.reshape(2240, D_MODEL)), _w_uq_pad(g_uq.reshape(768, 384)),
          _w_ukv_perm(g_ukv.reshape(1024, 256)))

    crows = jnp.concatenate([c_all[:, :nb].reshape(N_DEV * nb, D_MODEL), c_ctx[None], jnp.zeros((7, D_MODEL), F32)])
    b_blk = lax.dynamic_slice(b_ada, (0, me * 768), (1, 768))
    (mod_g,) = _exchange([_mod_fwd(crows, w_ada[0], b_blk)], True, "gather_mod")
    mod_all = _unshard_cols(mod_g)
    behind = mod_g[0, 0, 0:1] * 0.0
    st_o = _exchange_start([(w_out[0] + behind).astype(BF)], True, "gather_wo_start")
    st_g = _exchange_start([w_ff1[0].T.astype(BF), w_ff2[0].astype(BF)], True, "gather_ff_start", after=st_o["token"])
    mod_all = mod_all + st_g["token"][0:1, 0:1]
    mod_mine = lax.dynamic_slice(mod_all, (me * nb, 0), (nb, 6 * D_MODEL)).reshape(nb, 6, D_MODEL)
    mod = jnp.pad(mod_mine, ((0, 0), (0, 2), (0, 0)))
    mod_c = jnp.pad(mod_all[16].reshape(1, 6, D_MODEL), ((0, 0), (0, 2), (0, 0)))

    tabs = _rope_tables()
    dec_f = ret_decay_fwd.reshape(N_HEADS, 1, 1)
    dec_b = ret_decay_bwd.reshape(N_HEADS, 1, 1)

    rkc, rvc, k_ctx, v_ctx = _k1_fwd(ctx, mod_c, g_attn, g_q_lora, g_kv_lora, ws, tabs, None, True)
    rq, rk, rv, rg, q, k_all, v_all = _k1_fwd(x, mod, g_attn, g_q_lora, g_kv_lora, ws, tabs, (k_ctx, v_ctx), False)
    o_f, o_b, sf_prev, sb_prev = _k2_fwd(rq, rk, rv, rkc, rvc, dec_f, dec_b)
    y_mla, lse = _k3_fwd(q, k_all, v_all)
    (g_out,) = _exchange_wait(st_o, y_mla, "gather_wo_wait")
    wo = g_out.reshape(D_MODEL, D_MODEL)
    x_mid, h2 = _k4a_fwd(x, o_f, o_b, rg, y_mla, g_ret, wo, mod, g_ffn)
    g_ff1t, g_ff2 = _exchange_wait(st_g, x_mid, "gather_ff_wait")
    w1t = g_ff1t.reshape(D_FF, D_MODEL)
    dxm, dmlp, relu_a, loss_acc, dgt_f, dg_final = _k4b_mlp_loss(h2, w1t, g_ff2.reshape(D_FF, D_MODEL), x_mid, mod,
                                                                 g_final.reshape(1, D_MODEL), loss_target)

    da, dw1, dw2 = _k4d_mlp_bwd(h2, dmlp, relu_a, g_ff2)
    st_s = _exchange_start([dw1, dw2], False, "scatter_ff_start")
    g_ret_t = g_ret + st_s["token"][0:1, 0:1]
    dx_res, do, drg, dym, dwo, dg_ret, dg_ffn, dmod_a = _k4e_bwd(x, o_f, o_b, rg, y_mla, g_ret_t, wo, mod, g_ffn, dxm, da,
                                                                 w1t)
    st_w = _exchange_start([dwo.reshape(N_DEV, 128, D_MODEL).astype(BF)], False, "scatter_wo_start")
    dq, dk_all, dv_all = _k3_bwd(q, k_all, v_all, y_mla, lse, dym, st_w["token"])
    dqf, dkf, dvf, dqb, dkb, dvb, dkc, dvc, ddf, ddb = _k2_bwd(rq, rk, rv, do, sf_prev, sb_prev, rkc, rvc, dec_f, dec_b)
    cts = [[(dqf, 0), (dqb, 0)], [(dkf, 0), (dkb, 0)], [(dvf, 0), (dvb, 0)], [(drg, 0)], [(dq, 0)],
           [(dk_all, CTX_LEN)], [(dv_all, CTX_LEN)]]
    cts_c = [[(dkc, 0)], [(dvc, 0)], [(dk_all, 0)], [(dv_all, 0)]]
    grad_x, accs, dmod_1, dmod_c1 = _k1_bwd(x, ctx, mod, mod_c, g_attn, g_q_lora, g_kv_lora, ws, tabs, cts, cts_c,
                                            dx_res)
    dwa, dwb, dwq, dwk, dg_attn, dg_q, dg_kv = accs

    dmod_loc = (dmod_a + dmod_1).at[:, 5, :].set(dgt_f[:, 0, :])[:, :6, :].reshape(nb, 6 * D_MODEL)
    dmod_ctx = dmod_c1[:, :6, :].reshape(1, 6 * D_MODEL)
    small = {"g_attn": dg_attn, "g_ffn": dg_ffn, "ret_decay_fwd": jnp.sum(ddf[:, :, 0, 0], axis=0),
             "ret_decay_bwd": jnp.sum(ddb[:, :, 0, 0], axis=0), "g_ret": dg_ret, "g_q_lora": dg_q, "g_kv_lora": dg_kv,
             "g_final": dg_final}
    extra = jnp.concatenate([dmod_loc, dmod_ctx, jnp.zeros((5, 6 * D_MODEL), F32)])
    ex_pieces = jnp.transpose(extra.reshape(8, N_DEV, 768), (1, 0, 2))
    sm_g, ex_g, loss_g = _exchange([_pack_small(small), ex_pieces, loss_acc], [True, False, True], "gather_small")
    dmod_blk = jnp.concatenate([ex_g[:, :nb].reshape(N_DEV * nb, 768), jnp.zeros((8, 768), F32)])
    gw_ada, gcc_part, gb_part = _mod_bwd(crows, w_ada[0], dmod_blk, ex_g[:, nb])
    st_c = _exchange_start([gcc_part, gb_part], True, "gather_cc_start")

    p_ff1, p_ff2 = _exchange_wait(st_s, st_c["token"], "scatter_ff_wait")
    (p_wo,) = _exchange_wait(st_w, p_ff1, "scatter_wo_wait")
    chip_sums = _pair_reduce([_w_in_cut(dwa, dwb).reshape(N_DEV, 280, D_MODEL), _w_uq_cut(dwq).reshape(N_DEV, 96, 384),
                              _w_ukv_unperm(dwk).reshape(N_DEV, 128, 256)], "pair_reduce")
    st_r = _exchange_start(chip_sums, False, "scatter_rest_start", after=p_wo, chips=True)

    res = {}
    early = (("w_ff1", w_ff1, m_w_ff1, v_w_ff1, p_ff1), ("w_ff2", w_ff2, m_w_ff2, v_w_ff2, p_ff2),
             ("w_ada", w_ada, m_w_ada, v_w_ada, gw_ada[None]))
    for name, w, m, v, pcs in early:
        res[name] = [a[None] for a in _adamw(w[0], m[0], v[0], pcs, "adamw_" + name, after=st_r["token"])]
    pieces = _exchange_wait(st_r, res["w_ada"][3], "scatter_rest_wait")
    for name, w, m, v, pcs in (("w_in", w_in, m_w_in, v_w_in, pieces[0]), ("w_uq", w_uq, m_w_uq, v_w_uq, pieces[1])):
        res[name] = [a.T[None] for a in _adamw(w[0].T, m[0].T, v[0].T, pcs, "adamw_" + name)]
    late = (("w_ukv", w_ukv, m_w_ukv, v_w_ukv, jnp.transpose(pieces[2], (0, 2, 1))),
            ("w_out", w_out, m_w_out, v_w_out, p_wo))
    for name, w, m, v, pcs in late:
        res[name] = [a[None] for a in _adamw(w[0], m[0], v[0], pcs, "adamw_" + name)]

    smalls = {"c_ctx": (c_ctx, m_c_ctx, v_c_ctx), "b_ada": (b_ada, m_b_ada, v_b_ada), "g_attn": (g_attn, m_g_attn, v_g_attn),
              "g_ffn": (g_ffn, m_g_ffn, v_g_ffn), "ret_decay_fwd": (ret_decay_fwd, m_ret_decay_fwd, v_ret_decay_fwd),
              "ret_decay_bwd": (ret_decay_bwd, m_ret_decay_bwd, v_ret_decay_bwd), "g_ret": (g_ret, m_g_ret, v_g_ret),
              "g_q_lora": (g_q_lora, m_g_q_lora, v_g_q_lora), "g_kv_lora": (g_kv_lora, m_g_kv_lora, v_g_kv_lora),
              "g_final": (g_final, m_g_final, v_g_final)}
    rows = {k: tuple(a.reshape(1, -1) for a in t) for k, t in smalls.items()}
    gcc_g, gb_g = _exchange_wait(st_c, res["w_out"][3], "gather_cc_wait")
    for name, outs in _adamw_small(rows, sm_g, gcc_g, gb_g).items():
        res[name] = [o.reshape(smalls[name][0].shape) for o in outs]

    loss = loss_g[0, 0, 0]
    for k in range(1, N_DEV):
        loss = loss + loss_g[k, 0, 0]

    order = ("c_ctx", "w_ada", "b_ada", "g_attn", "g_ffn", "w_in", "ret_decay_fwd", "ret_decay_bwd", "g_ret", "g_q_lora",
             "w_uq", "g_kv_lora", "w_ukv", "w_out", "w_ff1", "w_ff2", "g_final")
    return (loss, grad_x, *[res[n][0] for n in order], *[res[n][1] for n in order], *[res[n][2] for n in order],
            *[res[n][3] for n in order])
```

```python
import functools
import math

import jax
import jax.numpy as jnp
from jax import lax
from jax.experimental import pallas as pl
from jax.experimental.pallas import tpu as pltpu

F32 = jnp.float32
BF = jnp.bfloat16
EPS = 1e-6
LANE = 128
N_DEV = 8
D_MODEL = 1024
SEQ = 2048
CTX_LEN = 256
GRID_W = 64
N_HEADS = 4
RET_CHUNK = 512
N_CHUNK = SEQ // RET_CHUNK
D_FF = 4096
FF_BLK = D_FF // N_DEV
IN_PAD = 2816
KV_LEN = CTX_LEN + SEQ
ROPE_BASE = 10000.0
ADAM_LR, ADAM_B1, ADAM_B2, ADAM_EPS, ADAM_WD, ADAM_STEP = 0.001, 0.9, 0.999, 1e-08, 0.01, 10
TOK = 512
TOK_B = 256
VMEM_LIMIT = 56 * 1024 * 1024
ARB = "arbitrary"
MESH = pl.DeviceIdType.MESH
_HEAD_SL = [slice(LANE * h, LANE * (h + 1)) for h in range(N_HEADS)]
W_SHAPES = [(2048, D_MODEL), (768, D_MODEL), (1024, 384), (1024, 256)]


def _dot(a, b, ca, cb):
    return lax.dot_general(a.astype(BF), b.astype(BF), (((ca,), (cb,)), ((), ())), preferred_element_type=F32)


@jax.custom_vjp
def mm(a, b):
    return _dot(a, b, 1, 0)


@jax.custom_vjp
def mm_nt(a, b):
    return _dot(a, b, 1, 1)


@jax.custom_vjp
def mm_tn(a, b):
    return _dot(a, b, 0, 0)


mm.defvjp(lambda a, b: (_dot(a, b, 1, 0), (a, b)), lambda r, g: (mm_nt(g, r[1]), mm_tn(r[0], g)))
mm_nt.defvjp(lambda a, b: (_dot(a, b, 1, 1), (a, b)), lambda r, g: (mm(g, r[1]), mm_tn(g, r[0])))
mm_tn.defvjp(lambda a, b: (_dot(a, b, 0, 0), (a, b)), lambda r, g: (mm_nt(r[1], g), mm(r[0], g)))


@jax.custom_vjp
def _mmw(a, w, probe):
    return _dot(a, w, 1, 0)


def _mmw_bwd(r, g):
    a, w = r
    return mm_nt(g, w), jnp.zeros_like(w), mm_tn(a, g)


_mmw.defvjp(lambda a, w, probe: (_dot(a, w, 1, 0), (a, w)), _mmw_bwd)


@jax.custom_vjp
def _mmwt(a, wt, probe):
    return _dot(a, wt, 1, 1)


_mmwt.defvjp(lambda a, wt, probe: (_dot(a, wt, 1, 1), (a, wt)),
             lambda r, g: (mm(g, r[1]), jnp.zeros_like(r[1]), mm_tn(g, r[0])))


def mmwt(a, wt, probe):
    return _dot(a, wt, 1, 1) if probe is None else _mmwt(a, wt, probe)


def mmw(a, w, probe):
    return _dot(a, w, 1, 0) if probe is None else _mmw(a, w, probe)


def rmsn(x, g):
    return x * lax.rsqrt(jnp.mean(x * x, axis=-1, keepdims=True) + EPS) * g


def silu(x):
    return x * jax.nn.sigmoid(x)


def _swap_halves_impl(x):
    return pltpu.roll(x, 64, 1)


@jax.custom_vjp
def swap_halves(x):
    return _swap_halves_impl(x)


swap_halves.defvjp(lambda x: (_swap_halves_impl(x), None), lambda _, g: (_swap_halves_impl(g),))


def rope(x, cs1, sn1, every=1):
    blocks = []
    for i in range(x.shape[-1] // LANE):
        xb = x[:, LANE * i:LANE * (i + 1)]
        blocks.append(xb * cs1 + swap_halves(xb) * sn1 if i % every == every - 1 else xb)
    return blocks[0] if len(blocks) == 1 else jnp.concatenate(blocks, axis=-1)


def k1_tile(x, sh, sc, g_attn, g_q, g_kv, ws, ps, tabs, is_ctx):
    w_a, w_b, w_uq, w_ukv = ws
    p_a, p_b, p_uq, p_ukv = ps
    cs1, sn1 = tabs
    h = rmsn(x, g_attn) * (1.0 + sc) + sh
    pa = mmwt(h, w_a, p_a)
    pb = mmwt(h, w_b, p_b)
    rk = pa[:, 512:1024] * 0.125
    rv = pa[:, 1024:1536]
    kpe = pb[:, 640:768]
    kv = mmwt(rmsn(pb[:, 384:640], g_kv), w_ukv, p_ukv)
    if not is_ctx:
        rk = rope(rk, cs1, sn1)
        kpe = rope(kpe, cs1, sn1)
    k_full = jnp.concatenate([piece for sl in _HEAD_SL for piece in (kv[:, sl], kpe)], axis=-1)
    v = kv[:, 512:]
    if is_ctx:
        return rk, rv, k_full, v
    rq = rope(pa[:, 0:512], cs1, sn1)
    rg = pa[:, 1536:2048]
    q = rope(mmwt(rmsn(pb[:, 0:384], g_q), w_uq, p_uq), cs1, sn1, every=2)
    return rq, rk, rv, rg, q, k_full, v


def log_sigmoid(x):
    return jnp.minimum(x, 0.0) - jnp.log(1.0 + jnp.exp(-jnp.abs(x)))


def ret_chunk(q, k, v, s, lg, reverse):
    c = RET_CHUNK
    ii = lax.broadcasted_iota(jnp.int32, (c, c), 0).astype(F32)
    jj = lax.broadcasted_iota(jnp.int32, (c, c), 1).astype(F32)
    diff = (jj - ii) if reverse else (ii - jj)
    dec = jnp.where(diff >= 0, jnp.exp(lg * jnp.maximum(diff, 0.0)), 0.0)
    pos = lax.broadcasted_iota(jnp.int32, (c, 1), 0).astype(F32)
    if reverse:
        wk, wq = jnp.exp(lg * pos), jnp.exp(lg * (c - pos))
    else:
        wk, wq = jnp.exp(lg * (c - 1.0 - pos)), jnp.exp(lg * (pos + 1.0))
    o = mm(mm_nt(q, k) * dec, v) + mm(q * wq, s)
    s_next = jnp.exp(lg * float(c)) * s + mm_tn(k * wk, v)
    return o, s_next


def ctx_state(kc, vc, lg, reverse):
    n = kc.shape[0]
    pos = lax.broadcasted_iota(jnp.int32, (n, 1), 0).astype(F32)
    w = jnp.exp(lg * pos) if reverse else jnp.exp(lg * (n - 1.0 - pos))
    return mm_tn(kc * w, vc)


def attn_head(qn, qp, kn, kp, v):
    s = (mm_nt(qn, kn) + mm_nt(qp, kp)) * (1.0 / math.sqrt(192.0))
    e = jnp.exp(s - jnp.max(s, axis=-1, keepdims=True))
    return mm(e / jnp.sum(e, axis=-1, keepdims=True), v)


def gn_gate(o, rg, g_ret):
    ys = []
    for h in range(N_HEADS):
        sl = slice(LANE * h, LANE * (h + 1))
        oh = o[:, sl]
        mu = jnp.mean(oh, axis=-1, keepdims=True)
        var = jnp.mean(jnp.square(oh - mu), axis=-1, keepdims=True)
        ys.append((oh - mu) * lax.rsqrt(var + EPS) * g_ret[:, sl])
    return jnp.concatenate(ys, axis=-1) * silu(rg)


def k4a_tile(x, o_f, o_b, rg, y_mla, g_ret, gt_a, g_ffn, sh_f, sc_f, w_out, p_out):
    mix = jnp.concatenate([gn_gate(o_f + o_b, rg, g_ret), y_mla], axis=-1)
    x_mid = x + gt_a * mmw(mix, w_out, p_out)
    h2 = rmsn(x_mid, g_ffn) * (1.0 + sc_f) + sh_f
    return x_mid, h2


def k4c_tile(x_mid, mlp, gt_f, g_final, tgt):
    y = rmsn(x_mid + gt_f * mlp, g_final)
    per_tok = jnp.mean(jnp.square(y - tgt), axis=-1, keepdims=True)
    return 0.5 * jnp.sum(per_tok, axis=0, keepdims=True)


def _cp(sem=None, vmem=VMEM_LIMIT):
    return pltpu.CompilerParams(dimension_semantics=sem, vmem_limit_bytes=vmem)


def _acc(ref, val, first):
    @pl.when(first)
    def _():
        ref[...] = val

    @pl.when(jnp.logical_not(first))
    def _():
        ref[...] += val


def _full(shape):
    nd = len(shape)
    return pl.BlockSpec(shape, lambda *_: (0,) * nd)


ANY = pl.BlockSpec(memory_space=pl.ANY)


def _sds(shape, dtype=F32):
    return jax.ShapeDtypeStruct(shape, dtype)


def _exchange(arrs, gather, name):
    n = len(arrs)
    modes = [gather] * n if isinstance(gather, bool) else list(gather)
    out_shape = [_sds(((N_DEV,) + a.shape) if g else a.shape, a.dtype) for a, g in zip(arrs, modes)]

    def body(*refs):
        ins, outs = refs[:n], refs[n:2 * n]
        send_sems, recv_sems, local_sems = refs[2 * n:]
        x, y, c = lax.axis_index("x"), lax.axis_index("y"), lax.axis_index("c")
        me = 4 * x + 2 * y + c
        sends, recvs, locs = [], [], []
        for i in range(n):
            gather = modes[i]
            for k in range(N_DEV - 1):
                bits = k + 1
                px = x ^ ((bits >> 2) & 1)
                py = y ^ ((bits >> 1) & 1)
                pc = c ^ (bits & 1)
                peer = 4 * px + 2 * py + pc
                src = ins[i] if gather else ins[i].at[peer]
                sem = i * (N_DEV - 1) + k
                sends.append(pltpu.make_async_remote_copy(
                    src_ref=src, dst_ref=outs[i].at[me], send_sem=send_sems.at[sem], recv_sem=recv_sems.at[sem],
                    device_id=(px, py, pc), device_id_type=MESH))
                recvs.append(pltpu.make_async_remote_copy(
                    src_ref=src, dst_ref=outs[i].at[peer], send_sem=send_sems.at[sem], recv_sem=recv_sems.at[sem],
                    device_id=(px, py, pc), device_id_type=MESH))
            locs.append(pltpu.make_async_copy(ins[i] if gather else ins[i].at[me], outs[i].at[me], local_sems.at[i]))
        for cp in locs + sends:
            cp.start()
        for cp in recvs:
            cp.wait_recv()
        for cp in sends:
            cp.wait_send()
        for cp in locs:
            cp.wait()

    outs = pl.pallas_call(
        body, name=name, out_shape=out_shape, in_specs=[ANY] * n, out_specs=[ANY] * n,
        scratch_shapes=[pltpu.SemaphoreType.DMA((n * (N_DEV - 1),)), pltpu.SemaphoreType.DMA((n * (N_DEV - 1),)),
                        pltpu.SemaphoreType.DMA((n,))],
    )(*arrs)
    return list(outs)


def _gather_two_level(arrs, name):
    n = len(arrs)

    def body(*refs):
        ins, outs = refs[:n], refs[n:2 * n]
        send_sems, recv_sems, local_sems = refs[2 * n:]
        x, y, c = lax.axis_index("x"), lax.axis_index("y"), lax.axis_index("c")
        sibling = (x, y, 1 - c)
        chips = [(1 - x, y), (x, 1 - y), (1 - x, 1 - y)]

        def slot(px, py, pc):
            return 4 * px + 2 * py + pc

        first, passed, waits, locs = [], [], [], []
        for i in range(n):
            def copy(k, block, to, src=None, i=i):
                dst = outs[i].at[slot(*block)]
                return pltpu.make_async_remote_copy(
                    src_ref=dst if src is None else src, dst_ref=dst, send_sem=send_sems.at[7 * i + k],
                    recv_sem=recv_sems.at[7 * i + k], device_id=to, device_id_type=MESH)

            locs.append(pltpu.make_async_copy(ins[i], outs[i].at[slot(x, y, c)], local_sems.at[i]))
            first.append(copy(0, (x, y, c), sibling, src=ins[i]))
            first += [copy(1 + j, (x, y, c), (*chip, c), src=ins[i]) for j, chip in enumerate(chips)]
            passed.append([copy(4 + j, (*chip, c), sibling) for j, chip in enumerate(chips)])
            waits.append([copy(1 + j, (*chip, c), (x, y, c)) for j, chip in enumerate(chips)])
        for cp in locs + first:
            cp.start()
        for j in range(3):
            for i in range(n):
                waits[i][j].wait_recv()
                passed[i][j].start()
        for i in range(n):
            def arrival(k, block, i=i):
                dst = outs[i].at[slot(*block)]
                return pltpu.make_async_remote_copy(
                    src_ref=dst, dst_ref=dst, send_sem=send_sems.at[7 * i + k], recv_sem=recv_sems.at[7 * i + k],
                    device_id=sibling, device_id_type=MESH)

            arrival(0, (x, y, 1 - c)).wait_recv()
            for j, chip in enumerate(chips):
                arrival(4 + j, (*chip, 1 - c)).wait_recv()
        for cp in first + [p for ps in passed for p in ps]:
            cp.wait_send()
        for cp in locs:
            cp.wait()

    outs = pl.pallas_call(
        body, name=name, out_shape=[_sds((N_DEV,) + a.shape, a.dtype) for a in arrs], in_specs=[ANY] * n,
        out_specs=[ANY] * n,
        scratch_shapes=[pltpu.SemaphoreType.DMA((7 * n,)), pltpu.SemaphoreType.DMA((7 * n,)),
                        pltpu.SemaphoreType.DMA((n,))],
    )(*arrs)
    return list(outs)


HBM = pl.BlockSpec(memory_space=pltpu.HBM)
SEM = pl.BlockSpec(memory_space=pltpu.SEMAPHORE)
EFFECT = pltpu.SideEffectType.DATAFLOW_SIDE_EFFECTING


def _peer(k, chips=False):
    x, y, c = lax.axis_index("x"), lax.axis_index("y"), lax.axis_index("c")
    bits = (k + 1) << 1 if chips else k + 1
    px, py, pc = x ^ ((bits >> 2) & 1), y ^ ((bits >> 1) & 1), c ^ (bits & 1)
    if chips:
        return (px, py, pc), 2 * px + py, 2 * x + y
    return (px, py, pc), 4 * px + 2 * py + pc, 4 * x + 2 * y + c


def _exchange_start(arrs, gather, name, after=None, chips=False):
    n = len(arrs)
    n_peer, n_slot = (3, 4) if chips else (N_DEV - 1, N_DEV)
    lands = [pltpu.with_memory_space_constraint(lax.empty(((n_slot,) + a.shape) if gather else a.shape, a.dtype),
                                                pltpu.HBM) for a in arrs]
    srcs = [pltpu.with_memory_space_constraint(a, pltpu.HBM) for a in arrs]

    extra = [] if after is None else [after]

    def body(*refs):
        ins, zones = refs[:n], refs[n:2 * n]
        send_sems, recv_sems, local_sems = refs[2 * n + len(extra):2 * n + len(extra) + 3]
        token = refs[-1]
        for i in range(n):
            for k in range(n_peer):
                dev, peer, me = _peer(k, chips)
                sem = i * n_peer + k
                pltpu.make_async_remote_copy(
                    src_ref=ins[i] if gather else ins[i].at[peer], dst_ref=zones[i].at[me],
                    send_sem=send_sems.at[sem], recv_sem=recv_sems.at[sem], device_id=dev, device_id_type=MESH).start()
            _, _, me = _peer(0, chips)
            pltpu.make_async_copy(ins[i] if gather else ins[i].at[me], zones[i].at[me], local_sems.at[i]).start()
        token[...] = jnp.zeros_like(token)

    nsem = n * n_peer
    outs = pl.pallas_call(
        body, name=name,
        out_shape=[pltpu.SemaphoreType.DMA((nsem,)), pltpu.SemaphoreType.DMA((nsem,)), pltpu.SemaphoreType.DMA((n,))]
        + [pltpu.HBM(a.shape, a.dtype) for a in srcs] + [pltpu.HBM(z.shape, z.dtype) for z in lands]
        + [_sds((8, LANE))],
        in_specs=[HBM] * (2 * n) + [ANY] * len(extra),
        out_specs=[SEM, SEM, SEM] + [HBM] * (2 * n) + [pl.BlockSpec(memory_space=pltpu.VMEM)],
        input_output_aliases={i: 3 + i for i in range(2 * n)},
        compiler_params=pltpu.CompilerParams(has_side_effects=EFFECT),
    )(*srcs, *lands, *extra)
    return {"n": n, "gather": gather, "chips": chips, "sems": outs[:3], "srcs": outs[3:3 + n],
            "lands": outs[3 + n:3 + 2 * n], "token": outs[-1]}


def _exchange_wait(st, after, name):
    n, gather, chips = st["n"], st["gather"], st["chips"]
    n_peer = 3 if chips else N_DEV - 1

    def body(*refs):
        ins, zones = refs[:n], refs[n:2 * n]
        send_sems, recv_sems, local_sems = refs[2 * n:2 * n + 3]
        for i in range(n):
            for k in range(n_peer):
                dev, peer, me = _peer(k, chips)
                sem = i * n_peer + k
                src = ins[i] if gather else ins[i].at[peer]
                cp = pltpu.make_async_remote_copy(
                    src_ref=src, dst_ref=zones[i].at[peer], send_sem=send_sems.at[sem], recv_sem=recv_sems.at[sem],
                    device_id=dev, device_id_type=MESH)
                cp.wait_send()
                cp.wait_recv()
            _, _, me = _peer(0, chips)
            pltpu.make_async_copy(ins[i] if gather else ins[i].at[me], zones[i].at[me], local_sems.at[i]).wait()

    outs = pl.pallas_call(
        body, name=name,
        out_shape=[pltpu.HBM(a.shape, a.dtype) for a in st["srcs"]] + [pltpu.HBM(z.shape, z.dtype) for z in st["lands"]],
        in_specs=[HBM] * (2 * n) + [SEM, SEM, SEM, ANY], out_specs=[HBM] * (2 * n),
        input_output_aliases={i: i for i in range(2 * n)},
        compiler_params=pltpu.CompilerParams(has_side_effects=EFFECT),
    )(*st["srcs"], *st["lands"], *st["sems"], after)
    return list(outs[n:])


def _pair_reduce(arrs, name):
    n = len(arrs)

    def body(*refs):
        ins, outs, got, mine = refs[:n], refs[n:2 * n], refs[2 * n:3 * n], refs[3 * n:4 * n]
        send_sems, recv_sems, local_sems = refs[4 * n:]
        x, y, c = lax.axis_index("x"), lax.axis_index("y"), lax.axis_index("c")
        sends, locs = [], []
        for i in range(n):
            for q in range(4):
                sem = 4 * i + q
                sends.append(pltpu.make_async_remote_copy(
                    src_ref=ins[i].at[2 * q + 1 - c], dst_ref=got[i].at[q], send_sem=send_sems.at[sem],
                    recv_sem=recv_sems.at[sem], device_id=(x, y, 1 - c), device_id_type=MESH))
                locs.append(pltpu.make_async_copy(ins[i].at[2 * q + c], mine[i].at[q], local_sems.at[sem]))
        for cp in locs + sends:
            cp.start()
        for cp in sends:
            cp.wait_recv()
        for cp in locs:
            cp.wait()
        for i in range(n):
            outs[i][...] = (mine[i][...].astype(F32) + got[i][...].astype(F32)).astype(BF)
        for cp in sends:
            cp.wait_send()

    half = [(4,) + a.shape[1:] for a in arrs]
    outs = pl.pallas_call(
        body, name=name, out_shape=[_sds(h, BF) for h in half], in_specs=[ANY] * n,
        out_specs=[pl.BlockSpec(memory_space=pltpu.VMEM)] * n,
        scratch_shapes=[pltpu.VMEM(h, BF) for h in half] * 2
        + [pltpu.SemaphoreType.DMA((4 * n,)), pltpu.SemaphoreType.DMA((4 * n,)), pltpu.SemaphoreType.DMA((4 * n,))],
        compiler_params=_cp(),
    )(*arrs)
    return list(outs)


def _mod_fwd(crows, w_ada, b_blk):
    def body(c_ref, w_ref, b_ref, o_ref):
        o_ref[...] = mm(silu(c_ref[...]), w_ref[...]) + b_ref[...]

    return pl.pallas_call(body, name="mod_fwd", out_shape=_sds((24, 768)), compiler_params=_cp())(crows, w_ada, b_blk)


def _mod_bwd(crows, w_ada, dmod_blk, dmodc_blk):
    def body(c_ref, w_ref, d_ref, dc_ref, gw_ref, gc_ref, gb_ref):
        cr = c_ref[...]
        dc = dc_ref[0:1, :]
        for p in range(1, N_DEV):
            dc = dc + dc_ref[p:p + 1, :]
        row = lax.broadcasted_iota(jnp.int32, (24, 1), 0)
        gw_ref[...] = mm_tn(silu(cr), jnp.where(row == 16, dc, d_ref[...]))
        cc = cr[16:17, :]
        sg = jax.nn.sigmoid(cc)
        part = mm_nt(jnp.broadcast_to(dc, (8, 768)), w_ref[...])
        gc_ref[...] = part * (sg * (1.0 + cc * (1.0 - sg)))
        gb_ref[...] = jnp.broadcast_to(jnp.sum(d_ref[...], axis=0, keepdims=True) + dc, (8, 768))

    return pl.pallas_call(
        body, name="mod_bwd", out_shape=[_sds((D_MODEL, 768)), _sds((8, D_MODEL)), _sds((8, 768))],
        compiler_params=_cp())(crows, w_ada, dmod_blk, dmodc_blk)


def _tab_specs(tk):
    return [pl.BlockSpec((tk, LANE), lambda i, t: (t, 0))] * 2


def _k1_fwd(x, mod, g_attn, g_q, g_kv, ws, tabs, kv_all, is_ctx):
    b, l, _ = x.shape
    tk = CTX_LEN if is_ctx else TOK
    nt = l // tk
    n_f32 = 2 if is_ctx else 4

    def body(x_ref, mod_ref, ga_ref, gq_ref, gk_ref, wa_ref, wb_ref, wq_ref, wk_ref, cs_ref, sn_ref, *rest):
        outs = rest if is_ctx else rest[2:]
        res = k1_tile(x_ref[...], mod_ref[0:1, :], mod_ref[1:2, :], ga_ref[...], gq_ref[...], gk_ref[...],
                      (wa_ref[...], wb_ref[...], wq_ref[...], wk_ref[...]), (None,) * 4,
                      (cs_ref[...], sn_ref[...]), is_ctx)
        for o_ref, r in zip(outs, res):
            o_ref[...] = r.astype(o_ref.dtype)

    tok = lambda w, off=0: pl.BlockSpec((None, tk, w), lambda i, t: (i, t + off, 0))
    mod_spec = pl.BlockSpec((None, 8, D_MODEL), (lambda i, t: (0, 0, 0)) if is_ctx else (lambda i, t: (i, 0, 0)))
    kv_off = SEQ // tk if is_ctx else 0
    in_specs = ([tok(D_MODEL), mod_spec, _full((1, D_MODEL)), _full((1, 384)), _full((1, 256))]
                + [_full(s) for s in W_SHAPES] + _tab_specs(tk))
    args = [x, mod, g_attn, g_q, g_kv, *ws, *tabs]
    out_specs = [tok(512)] * n_f32 + ([] if is_ctx else [tok(1024)]) + [tok(1024, kv_off), tok(512, kv_off)]
    out_shape = ([_sds((b, l, 512))] * n_f32 + ([] if is_ctx else [_sds((b, l, 1024), BF)])
                 + [_sds((b, KV_LEN, 1024), BF), _sds((b, KV_LEN, 512), BF)])
    aliases = {}
    if not is_ctx:
        aliases = {len(args): n_f32 + 1, len(args) + 1: n_f32 + 2}
        in_specs += [ANY, ANY]
        args += list(kv_all)
    return pl.pallas_call(
        body, name="k1_fwd_ctx" if is_ctx else "k1_fwd", grid=(b, nt), in_specs=in_specs, out_specs=out_specs,
        out_shape=out_shape, input_output_aliases=aliases, compiler_params=_cp((ARB, ARB)),
    )(*args)


N_ACC = 7


def _k1_bwd(x, ctx, mod, mod_c, g_attn, g_q, g_kv, ws, tabs, cts, cts_c, dx_res):
    b, l, _ = x.shape
    tk = TOK_B
    nt = l // tk
    flat = [[a for group in c for a in group] for c in (cts, cts_c)]
    sizes = [[len(g) for g in c] for c in (cts, cts_c)]
    acc_shapes = W_SHAPES + [(1, D_MODEL), (1, 384), (1, 256)]

    def body(*refs):
        it = iter(refs)
        x_ref, c_ref, mod_ref, modc_ref, ga_ref, gq_ref, gk_ref = [next(it) for _ in range(7)]
        w_hbm = [next(it) for _ in range(4)]
        tab_refs = [next(it) for _ in range(2)]
        ct_refs = [[next(it) for _ in f] for f in flat]
        res_ref, gx_ref = next(it), next(it)
        out_hbm = [next(it) for _ in range(N_ACC)]
        dmod_ref, dmodc_ref = next(it), next(it)
        w_vmem = [next(it) for _ in range(4)]
        accs = [next(it) for _ in range(N_ACC)]
        sem = next(it)
        i, t = pl.program_id(0), pl.program_id(1)
        first = jnp.logical_and(i == 0, t == 0)

        @pl.when(first)
        def _():
            for src, dst in zip(w_hbm, w_vmem):
                pltpu.sync_copy(src, dst)
            for k in range(N_ACC):
                accs[k][...] = jnp.zeros(acc_shapes[k], F32)

        def tile(is_ctx):
            which = 1 if is_ctx else 0
            ct_vals, pos = [], 0
            for gsz in sizes[which]:
                v = ct_refs[which][pos][...].astype(F32)
                for r in ct_refs[which][pos + 1:pos + gsz]:
                    v = v + r[...]
                ct_vals.append(v)
                pos += gsz
            wv = tuple(r[...] for r in w_vmem)
            tv = tuple(r[...] for r in tab_refs)
            m_ref = modc_ref if is_ctx else mod_ref

            def f(xv, sh, sc, ga, gq, gk, *probes):
                return k1_tile(xv, sh, sc, ga, gq, gk, wv, probes, tv, is_ctx)

            probes = [jnp.zeros(s, F32) for s in W_SHAPES]
            xin = c_ref[...] if is_ctx else x_ref[...]
            _, vjp = jax.vjp(f, xin, m_ref[0:1, :], m_ref[1:2, :], ga_ref[...], gq_ref[...], gk_ref[...], *probes)
            dx, dsh, dsc, dga, dgq, dgk, dwa, dwb, dwq, dwk = vjp(tuple(ct_vals))
            for ref, val in zip(accs, (dwa, dwb, dwq, dwk, dga, dgq, dgk)):
                ref[...] += val
            return dx, dsh, dsc

        @pl.when(t == 0)
        def _():
            _, dsh, dsc = tile(True)
            _acc(dmodc_ref.at[0:1, :], dsh, i == 0)
            _acc(dmodc_ref.at[1:2, :], dsc, i == 0)

            @pl.when(i == 0)
            def _():
                dmodc_ref[2:8, :] = jnp.zeros((6, D_MODEL), F32)

        @pl.when(t > 0)
        def _():
            dx, dsh, dsc = tile(False)
            gx_ref[...] = dx + res_ref[...]
            _acc(dmod_ref.at[0:1, :], dsh, t == 1)
            _acc(dmod_ref.at[1:2, :], dsc, t == 1)

            @pl.when(t == 1)
            def _():
                dmod_ref[2:8, :] = jnp.zeros((6, D_MODEL), F32)

        @pl.when(jnp.logical_and(i == b - 1, t == nt))
        def _():
            for k in range(4):
                w_vmem[k][...] = accs[k][...].astype(BF)
            cps = [pltpu.make_async_copy(w_vmem[k] if k < 4 else accs[k], out_hbm[k], sem.at[k]) for k in range(N_ACC)]
            for cp in cps:
                cp.start()
            for cp in cps:
                cp.wait()

    lat = lambda w, off=0: pl.BlockSpec((None, tk, w), lambda i, t: (i, jnp.maximum(t - 1, 0) + off, 0))
    con = lambda w, off=0: pl.BlockSpec((None, tk, w), lambda i, t: (i, off, 0))
    mod_spec = pl.BlockSpec((None, 8, D_MODEL), lambda i, t: (i, 0, 0))
    modc_spec = pl.BlockSpec((None, 8, D_MODEL), lambda i, t: (0, 0, 0))
    tab_spec = pl.BlockSpec((tk, LANE), lambda i, t: (jnp.maximum(t - 1, 0), 0))
    in_specs = ([lat(D_MODEL), con(D_MODEL), mod_spec, modc_spec, _full((1, D_MODEL)), _full((1, 384)), _full((1, 256))]
                + [ANY] * 4 + [tab_spec] * 2)
    args = [x, ctx, mod, mod_c, g_attn, g_q, g_kv, *ws, *tabs]
    for a, off in flat[0]:
        in_specs.append(lat(a.shape[-1], off // tk))
        args.append(a)
    for a, off in flat[1]:
        in_specs.append(con(a.shape[-1], off // tk))
        args.append(a)
    in_specs.append(lat(D_MODEL))
    args.append(dx_res)
    out_shape = ([_sds((b, l, D_MODEL))] + [_sds(s, BF) for s in W_SHAPES] + [_sds(s) for s in acc_shapes[4:]]
                 + [_sds((b, 8, D_MODEL)), _sds((1, 8, D_MODEL))])
    out_specs = [lat(D_MODEL)] + [ANY] * N_ACC + [mod_spec, modc_spec]
    outs = pl.pallas_call(
        body, name="k1_bwd", grid=(b, nt + 1), in_specs=in_specs, out_specs=out_specs, out_shape=out_shape,
        scratch_shapes=[pltpu.VMEM(s, BF) for s in W_SHAPES] + [pltpu.VMEM(s, F32) for s in acc_shapes]
        + [pltpu.SemaphoreType.DMA((N_ACC,))],
        compiler_params=_cp((ARB, ARB)),
    )(*args)
    return outs[0], list(outs[1:1 + N_ACC]), outs[1 + N_ACC], outs[2 + N_ACC]


def _chunk_spec(rev):
    if rev:
        return pl.BlockSpec((None, RET_CHUNK, 512), lambda i, n: (i, N_CHUNK - 1 - n, 0))
    return pl.BlockSpec((None, RET_CHUNK, 512), lambda i, n: (i, n, 0))


def _state_spec(rev):
    if rev:
        return pl.BlockSpec((None, N_HEADS, None, LANE, LANE), lambda i, n: (i, 0, N_CHUNK - 1 - n, 0, 0))
    return pl.BlockSpec((None, N_HEADS, None, LANE, LANE), lambda i, n: (i, 0, n, 0, 0))


_CTX_SPEC = pl.BlockSpec((None, CTX_LEN, 512), lambda i, n: (i, 0, 0))
_DEC_SPEC = pl.BlockSpec((N_HEADS, 1, 1), lambda i, n: (0, 0, 0))


def _k2_fwd(rq, rk, rv, rkc, rvc, dec_f, dec_b):
    b = rq.shape[0]

    def body(qf, kf, vf, qb, kb, vb, kc, vc, df, db, of_ref, ob_ref, sf_out, sb_out, sf, sb):
        n = pl.program_id(1)
        for h, sl in enumerate(_HEAD_SL):
            lgf, lgb = log_sigmoid(df[h]), log_sigmoid(db[h])

            @pl.when(n == 0)
            def _():
                sf[h] = ctx_state(kc[:, sl], vc[:, sl], lgf, False)
                sb[h] = ctx_state(kc[:, sl], vc[:, sl], lgb, True)

            sf_out[h] = sf[h]
            sb_out[h] = sb[h]
            o, s = ret_chunk(qf[:, sl], kf[:, sl], vf[:, sl], sf[h], lgf, False)
            of_ref[:, sl] = o
            sf[h] = s
            o, s = ret_chunk(qb[:, sl], kb[:, sl], vb[:, sl], sb[h], lgb, True)
            ob_ref[:, sl] = o
            sb[h] = s

    l = rq.shape[1]
    return pl.pallas_call(
        body, name="k2_fwd", grid=(b, N_CHUNK),
        in_specs=[_chunk_spec(False)] * 3 + [_chunk_spec(True)] * 3 + [_CTX_SPEC, _CTX_SPEC, _DEC_SPEC, _DEC_SPEC],
        out_specs=[_chunk_spec(False), _chunk_spec(True), _state_spec(False), _state_spec(True)],
        out_shape=[_sds((b, l, 512)), _sds((b, l, 512)), _sds((b, N_HEADS, N_CHUNK, LANE, LANE)),
                   _sds((b, N_HEADS, N_CHUNK, LANE, LANE))],
        scratch_shapes=[pltpu.VMEM((N_HEADS, LANE, LANE), F32), pltpu.VMEM((N_HEADS, LANE, LANE), F32)],
        compiler_params=_cp((ARB, ARB)),
    )(rq, rk, rv, rq, rk, rv, rkc, rvc, dec_f, dec_b)


def _k2_bwd(rq, rk, rv, do, sf_prev, sb_prev, rkc, rvc, dec_f, dec_b):
    b, l, _ = rq.shape

    def body(qf, kf, vf, gf, spf, qb, kb, vb, gb, spb, kc, vc, df, db,
             dqf, dkf, dvf, dqb, dkb, dvb, dkc, dvc, ddf, ddb, dsf, dsb):
        n = pl.program_id(1)

        @pl.when(n == 0)
        def _():
            dsf[...] = jnp.zeros((N_HEADS, LANE, LANE), F32)
            dsb[...] = jnp.zeros((N_HEADS, LANE, LANE), F32)

        def one(h, sl, q, k, v, g, sp, dec, ds, dq, dk, dv, dd, rev):
            def f(qv, kv_, vv, sv, dcy):
                return ret_chunk(qv, kv_, vv, sv, log_sigmoid(dcy), rev)

            _, vjp = jax.vjp(f, q[:, sl], k[:, sl], v[:, sl], sp[h], dec[h])
            gq, gk, gv, gs, gd = vjp((g[:, sl], ds[h]))
            dq[:, sl] = gq
            dk[:, sl] = gk
            dv[:, sl] = gv
            ds[h] = gs
            _acc(dd.at[h], jnp.broadcast_to(gd, (8, LANE)), n == 0)

        for h, sl in enumerate(_HEAD_SL):
            one(h, sl, qf, kf, vf, gf, spf, df, dsf, dqf, dkf, dvf, ddf, False)
            one(h, sl, qb, kb, vb, gb, spb, db, dsb, dqb, dkb, dvb, ddb, True)

        @pl.when(n == N_CHUNK - 1)
        def _():
            def f(kcv, vcv, dcy, rev):
                return ctx_state(kcv, vcv, log_sigmoid(dcy), rev)

            for h, sl in enumerate(_HEAD_SL):
                _, vjp_f = jax.vjp(functools.partial(f, rev=False), kc[:, sl], vc[:, sl], df[h])
                gk_f, gv_f, gd_f = vjp_f(dsf[h])
                _, vjp_b = jax.vjp(functools.partial(f, rev=True), kc[:, sl], vc[:, sl], db[h])
                gk_b, gv_b, gd_b = vjp_b(dsb[h])
                dkc[:, sl] = gk_f + gk_b
                dvc[:, sl] = gv_f + gv_b
                ddf[h] += jnp.broadcast_to(gd_f, (8, LANE))
                ddb[h] += jnp.broadcast_to(gd_b, (8, LANE))

    dd_spec = pl.BlockSpec((None, N_HEADS, 8, LANE), lambda i, n: (i, 0, 0, 0))
    return pl.pallas_call(
        body, name="k2_bwd", grid=(b, N_CHUNK),
        in_specs=[_chunk_spec(True)] * 4 + [_state_spec(True)] + [_chunk_spec(False)] * 4 + [_state_spec(False)]
        + [_CTX_SPEC, _CTX_SPEC, _DEC_SPEC, _DEC_SPEC],
        out_specs=[_chunk_spec(True)] * 3 + [_chunk_spec(False)] * 3 + [_CTX_SPEC, _CTX_SPEC, dd_spec, dd_spec],
        out_shape=[_sds((b, l, 512))] * 6 + [_sds((b, CTX_LEN, 512))] * 2 + [_sds((b, N_HEADS, 8, LANE))] * 2,
        scratch_shapes=[pltpu.VMEM((N_HEADS, LANE, LANE), F32), pltpu.VMEM((N_HEADS, LANE, LANE), F32)],
        compiler_params=_cp((ARB, ARB)),
    )(rq, rk, rv, do, sf_prev, rq, rk, rv, do, sb_prev, rkc, rvc, dec_f, dec_b)


TQ = 1024
TQ_F = 512
QK_W = 2 * LANE
N_QP = 2
_Q_PARTS = [slice(i * TQ_F // N_QP, (i + 1) * TQ_F // N_QP) for i in range(N_QP)]


SM_SCALE = 1.0 / math.sqrt(192.0)


def _k3_specs(tq):
    qs = lambda w: pl.BlockSpec((None, tq, w), lambda i, h, t: (i, t, h))
    ks = lambda w: pl.BlockSpec((None, KV_LEN, w), lambda i, h, t: (i, 0, h))
    return qs, ks


def _k3_fwd(q, k, v):
    b, l, _ = q.shape

    def body(q_ref, k_ref, v_ref, o_ref, lse_ref):
        kv_, vv = k_ref[...], v_ref[...]
        for r in _Q_PARTS:
            s = _dot(q_ref[r, :], kv_, 1, 1) * SM_SCALE
            m = jnp.max(s, axis=-1, keepdims=True)
            e = jnp.exp(s - m)
            tot = jnp.sum(e, axis=-1, keepdims=True)
            o_ref[r, :] = _dot(e, vv, 1, 0) * (1.0 / tot)
            lse_ref[r, :] = jnp.broadcast_to(m + jnp.log(tot), (TQ_F // N_QP, LANE))

    qs, ks = _k3_specs(TQ_F)
    return pl.pallas_call(
        body, name="k3_fwd", grid=(b, N_HEADS, l // TQ_F), in_specs=[qs(QK_W), ks(QK_W), ks(LANE)],
        out_specs=[qs(LANE), qs(LANE)], out_shape=[_sds((b, l, N_HEADS * LANE))] * 2,
        compiler_params=_cp((ARB, ARB, ARB)),
    )(q, k, v)


def _k3_bwd(q, k, v, o, lse, dy, after):
    b, l, _ = q.shape

    def body(q_ref, k_ref, v_ref, o_ref, lse_ref, dy_ref, after_ref, dq_ref, dk_ref, dv_ref):
        t0 = pl.program_id(2) == 0
        kv_, vv = k_ref[...], v_ref[...]
        qv, dyv = q_ref[...], dy_ref[...]
        g = dyv.astype(BF)
        lse_col = jnp.max(lse_ref[...], axis=-1, keepdims=True)
        delta = jnp.sum(dyv * o_ref[...], axis=-1, keepdims=True)
        p = jnp.exp(_dot(qv, kv_, 1, 1) * SM_SCALE - lse_col)
        ds = (p * (_dot(g, vv, 1, 1) - delta) * SM_SCALE).astype(BF)
        _acc(dv_ref, _dot(p, g, 0, 0), t0)
        dq_ref[...] = _dot(ds, kv_, 1, 0)
        _acc(dk_ref, _dot(ds, qv, 0, 0), t0)

    qs, ks = _k3_specs(TQ)
    return pl.pallas_call(
        body, name="k3_bwd", grid=(b, N_HEADS, l // TQ),
        in_specs=[qs(QK_W), ks(QK_W), ks(LANE), qs(LANE), qs(LANE), qs(LANE), ANY],
        out_specs=[qs(QK_W), ks(QK_W), ks(LANE)],
        out_shape=[_sds((b, l, N_HEADS * QK_W)), _sds((b, KV_LEN, N_HEADS * QK_W)), _sds((b, KV_LEN, N_HEADS * LANE))],
        compiler_params=_cp((ARB, ARB, ARB)),
    )(q, k, v, o, lse, dy, after)


def _mod_rows(mod_ref, rows):
    return [mod_ref[r:r + 1, :] for r in rows]


def _k4a_fwd(x, o_f, o_b, rg, y_mla, g_ret, w_out, mod, g_ffn):
    b, l, _ = x.shape

    def body(x_ref, of_ref, ob_ref, rg_ref, ym_ref, gr_ref, wo_ref, mod_ref, gf_ref, xm_ref, h2_ref):
        gt_a, sh_f, sc_f = _mod_rows(mod_ref, (2, 3, 4))
        x_mid, h2 = k4a_tile(x_ref[...], of_ref[...], ob_ref[...], rg_ref[...], ym_ref[...], gr_ref[...], gt_a,
                             gf_ref[...], sh_f, sc_f, wo_ref[...], None)
        xm_ref[...] = x_mid
        h2_ref[...] = h2.astype(BF)

    tok = lambda w: pl.BlockSpec((None, TOK, w), lambda i, t: (i, t, 0))
    mod_spec = pl.BlockSpec((None, 8, D_MODEL), lambda i, t: (i, 0, 0))
    return pl.pallas_call(
        body, name="k4a_fwd", grid=(b, l // TOK),
        in_specs=[tok(D_MODEL), tok(512), tok(512), tok(512), tok(512), _full((1, 512)), _full((D_MODEL, D_MODEL)),
                  mod_spec, _full((1, D_MODEL))],
        out_specs=[tok(D_MODEL), tok(D_MODEL)], out_shape=[_sds((b, l, D_MODEL)), _sds((b, l, D_MODEL), BF)],
        compiler_params=_cp((ARB, ARB)),
    )(x, o_f, o_b, rg, y_mla, g_ret, w_out, mod, g_ffn)


TOK_M = 512
TOK_D = 2048
HALF_FF = D_FF // 2


def _k4b_mlp_loss(h2, w1t, w2, x_mid, mod, g_final, tgt):
    b, l, _ = h2.shape
    nt = l // TOK_M

    def body(h2_ref, w1_hbm, w2_hbm, xm_ref, mod_ref, gfin_ref, tgt_ref, dxm_ref, dmlp_ref, r_ref, loss_ref, dgt_ref,
             dgfin_ref, w1_v, w2_v):
        i, t = pl.program_id(0), pl.program_id(1)
        first = jnp.logical_and(i == 0, t == 0)

        @pl.when(first)
        def _():
            pltpu.sync_copy(w1_hbm, w1_v)
            pltpu.sync_copy(w2_hbm, w2_v)

        h2v = h2_ref[...]
        mlp = None
        for half in range(2):
            rows = slice(half * HALF_FF, (half + 1) * HALF_FF)
            r = jnp.maximum(_dot(h2v, w1_v[rows, :], 1, 1), 0.0)
            r_ref[:, rows] = r.astype(BF)
            part = _dot(jnp.square(r), w2_v[rows, :], 1, 0)
            mlp = part if mlp is None else mlp + part
        (gt_f,) = _mod_rows(mod_ref, (5,))
        loss, vjp = jax.vjp(k4c_tile, xm_ref[...], mlp, gt_f, gfin_ref[...], tgt_ref[...])
        dxm, dmlp, dgt, dgfin, _ = vjp(jnp.ones((1, 1), F32))
        dxm_ref[...] = dxm
        dmlp_ref[...] = dmlp.astype(BF)
        _acc(loss_ref, jnp.broadcast_to(loss, (8, LANE)), first)
        _acc(dgfin_ref, dgfin, first)
        _acc(dgt_ref, dgt, t == 0)

    tok = lambda w: pl.BlockSpec((None, TOK_M, w), lambda i, t: (i, t, 0))
    return pl.pallas_call(
        body, name="k4b_mlp_loss", grid=(b, nt),
        in_specs=[tok(D_MODEL), ANY, ANY, tok(D_MODEL), pl.BlockSpec((None, 8, D_MODEL), lambda i, t: (i, 0, 0)),
                  _full((1, D_MODEL)), tok(D_MODEL)],
        out_specs=[tok(D_MODEL), tok(D_MODEL), tok(D_FF), _full((8, LANE)),
                   pl.BlockSpec((None, 1, D_MODEL), lambda i, t: (i, 0, 0)), _full((1, D_MODEL))],
        out_shape=[_sds((b, l, D_MODEL)), _sds((b, l, D_MODEL), BF), _sds((b, l, D_FF), BF), _sds((8, LANE)),
                   _sds((b, 1, D_MODEL)), _sds((1, D_MODEL))],
        scratch_shapes=[pltpu.VMEM((D_FF, D_MODEL), BF), pltpu.VMEM((D_FF, D_MODEL), BF)],
        compiler_params=_cp((ARB, ARB)),
    )(h2, w1t, w2, x_mid, mod, g_final, tgt)


def _k4d_mlp_bwd(h2, dmlp, r, w2):
    b, l, _ = h2.shape
    nt = l // TOK_D

    def body(h2_ref, dm_ref, r_ref, w2_ref, da_ref, dw1_ref, dw2_ref, acc1, acc2):
        i, t = pl.program_id(1), pl.program_id(2)
        first = jnp.logical_and(i == 0, t == 0)
        rv = r_ref[...].astype(F32)
        dm = dm_ref[...]
        da = (_dot(dm, w2_ref[...], 1, 1) * (2.0 * rv)).astype(BF)
        da_ref[...] = da
        _acc(acc2, _dot(jnp.square(rv), dm, 0, 0), first)
        _acc(acc1, _dot(h2_ref[...], da, 0, 0), first)

        @pl.when(jnp.logical_and(i == b - 1, t == nt - 1))
        def _():
            dw1_ref[...] = acc1[...].astype(BF)
            dw2_ref[...] = acc2[...].astype(BF)

    tok = lambda w: pl.BlockSpec((None, TOK_D, w), lambda j, i, t: (i, t, 0))
    col = pl.BlockSpec((None, TOK_D, FF_BLK), lambda j, i, t: (i, t, j))
    return pl.pallas_call(
        body, name="k4d_mlp_bwd", grid=(N_DEV, b, nt),
        in_specs=[tok(D_MODEL), tok(D_MODEL), col, pl.BlockSpec((None, FF_BLK, D_MODEL), lambda j, i, t: (j, 0, 0))],
        out_specs=[col, pl.BlockSpec((None, D_MODEL, FF_BLK), lambda j, i, t: (j, 0, 0)),
                   pl.BlockSpec((None, FF_BLK, D_MODEL), lambda j, i, t: (j, 0, 0))],
        out_shape=[_sds((b, l, D_FF), BF), _sds((N_DEV, D_MODEL, FF_BLK), BF), _sds((N_DEV, FF_BLK, D_MODEL), BF)],
        scratch_shapes=[pltpu.VMEM((D_MODEL, FF_BLK), F32), pltpu.VMEM((FF_BLK, D_MODEL), F32)],
        compiler_params=_cp((ARB, ARB, ARB)),
    )(h2, dmlp, r, w2)


def _k4e_bwd(x, o_f, o_b, rg, y_mla, g_ret, w_out, mod, g_ffn, dxm, da, w1t):
    b, l, _ = x.shape

    def body(x_ref, of_ref, ob_ref, rg_ref, ym_ref, gr_ref, wo_ref, mod_ref, gf_ref, dxm_ref, da_ref, w1_hbm,
             dx_ref, do_ref, drg_ref, dym_ref, dwo_ref, dgr_ref, dgf_ref, dmod_ref, w1_v):
        i, t = pl.program_id(0), pl.program_id(1)
        first = jnp.logical_and(i == 0, t == 0)

        @pl.when(first)
        def _():
            pltpu.sync_copy(w1_hbm, w1_v)

        gt_a, sh_f, sc_f = _mod_rows(mod_ref, (2, 3, 4))
        wo = wo_ref[...]
        dh2 = _dot(da_ref[...], w1_v[...], 1, 0)

        def f(xv, ofv, rgv, ymv, grv, gta, gfv, shf, scf, p_out):
            return k4a_tile(xv, ofv, ob_ref[...], rgv, ymv, grv, gta, gfv, shf, scf, wo, p_out)

        _, vjp = jax.vjp(f, x_ref[...], of_ref[...], rg_ref[...], ym_ref[...], gr_ref[...], gt_a, gf_ref[...], sh_f,
                         sc_f, jnp.zeros((D_MODEL, D_MODEL), F32))
        dx, do, drg, dym, dgr, dgta, dgf, dshf, dscf, dwo = vjp((dxm_ref[...], dh2))
        dx_ref[...] = dx
        do_ref[...] = do
        drg_ref[...] = drg
        dym_ref[...] = dym
        _acc(dwo_ref, dwo, first)
        _acc(dgr_ref, dgr, first)
        _acc(dgf_ref, dgf, first)
        t0 = t == 0
        _acc(dmod_ref.at[2:3, :], dgta, t0)
        _acc(dmod_ref.at[3:4, :], dshf, t0)
        _acc(dmod_ref.at[4:5, :], dscf, t0)

        @pl.when(t0)
        def _():
            dmod_ref[0:2, :] = jnp.zeros((2, D_MODEL), F32)
            dmod_ref[5:8, :] = jnp.zeros((3, D_MODEL), F32)

    tok = lambda w: pl.BlockSpec((None, TOK_B, w), lambda i, t: (i, t, 0))
    mod_spec = pl.BlockSpec((None, 8, D_MODEL), lambda i, t: (i, 0, 0))
    return pl.pallas_call(
        body, name="k4e_bwd", grid=(b, l // TOK_B),
        in_specs=[tok(D_MODEL), tok(512), tok(512), tok(512), tok(512), _full((1, 512)), _full((D_MODEL, D_MODEL)),
                  mod_spec, _full((1, D_MODEL)), tok(D_MODEL), tok(D_FF), ANY],
        out_specs=[tok(D_MODEL), tok(512), tok(512), tok(512), _full((D_MODEL, D_MODEL)), _full((1, 512)),
                   _full((1, D_MODEL)), mod_spec],
        out_shape=[_sds((b, l, D_MODEL)), _sds((b, l, 512)), _sds((b, l, 512)), _sds((b, l, 512)),
                   _sds((D_MODEL, D_MODEL)), _sds((1, 512)), _sds((1, D_MODEL)), _sds((b, 8, D_MODEL))],
        scratch_shapes=[pltpu.VMEM((D_FF, D_MODEL), BF)],
        compiler_params=_cp((ARB, ARB)),
    )(x, o_f, o_b, rg, y_mla, g_ret, w_out, mod, g_ffn, dxm, da, w1t)


def _adamw(w, m, v, pieces, name, after=None):
    r, c = w.shape
    npc = pieces.shape[0]
    per_row = c * (7 * 4 + npc * pieces.dtype.itemsize) * 2
    rb = r
    for cand in (r, 512, 256, 128, 64, 32, 16, 8):
        if r % cand == 0 and cand * per_row <= 32 * 1024 * 1024:
            rb = cand
            break

    def body(w_ref, m_ref, v_ref, p_ref, *rest):
        g_ref, d_ref, nm_ref, nv_ref = rest[-4:]
        g = p_ref[0].astype(F32)
        for k in range(1, npc):
            g = g + p_ref[k].astype(F32)
        wv = w_ref[...]
        mn = ADAM_B1 * m_ref[...] + (1.0 - ADAM_B1) * g
        vn = ADAM_B2 * v_ref[...] + (1.0 - ADAM_B2) * jnp.square(g)
        m_hat = mn / (1.0 - ADAM_B1 ** ADAM_STEP)
        v_hat = vn / (1.0 - ADAM_B2 ** ADAM_STEP)
        g_ref[...] = g
        d_ref[...] = -ADAM_LR * (m_hat / (jnp.sqrt(v_hat) + ADAM_EPS) + ADAM_WD * wv)
        nm_ref[...] = mn
        nv_ref[...] = vn

    blk = pl.BlockSpec((rb, c), lambda i: (i, 0))
    extra = [] if after is None else [after]
    return pl.pallas_call(
        body, name=name, grid=(r // rb,),
        in_specs=[blk, blk, blk, pl.BlockSpec((npc, rb, c), lambda i: (0, i, 0))] + [ANY] * len(extra),
        out_specs=[blk] * 4, out_shape=[_sds((r, c))] * 4, compiler_params=_cp((ARB,)),
    )(w, m, v, pieces, *extra)


def _pad_rot_rows(w):
    k = w.shape[1]
    return jnp.pad(w.reshape(-1, 2, 32, k), ((0, 0), (0, 0), (0, 32), (0, 0))).reshape(-1, k)


def _cut_rot_rows(g):
    k = g.shape[1]
    return g.reshape(-1, 2, 64, k)[:, :, :32].reshape(-1, k)


def _w_in_pad(wt):
    w_a = jnp.concatenate([_pad_rot_rows(wt[0:512]), wt[512:1536]], axis=0)
    w_b = jnp.concatenate([wt[1536:2176], _pad_rot_rows(wt[2176:2240])], axis=0)
    return w_a, w_b


def _w_in_cut(g_a, g_b):
    return jnp.concatenate([_cut_rot_rows(g_a[0:1024]), g_a[1024:2048], g_b[0:640], _cut_rot_rows(g_b[640:768])], axis=0)


def _w_uq_pad(wt):
    w = wt.reshape(N_HEADS, 192, 384)
    rot = _pad_rot_rows(w[:, 128:].reshape(N_HEADS * 64, 384)).reshape(N_HEADS, LANE, 384)
    return jnp.concatenate([w[:, :128], rot], axis=1).reshape(1024, 384)


def _w_uq_cut(g):
    g = g.reshape(N_HEADS, 256, 384)
    rot = _cut_rot_rows(g[:, 128:].reshape(N_HEADS * LANE, 384)).reshape(N_HEADS, 64, 384)
    return jnp.concatenate([g[:, :128], rot], axis=1).reshape(768, 384)


def _w_ukv_perm(wt):
    return jnp.transpose(wt.reshape(N_HEADS, 2, LANE, 256), (1, 0, 2, 3)).reshape(1024, 256)


def _w_ukv_unperm(g):
    return jnp.transpose(g.reshape(2, N_HEADS, LANE, 256), (1, 0, 2, 3)).reshape(1024, 256)


def _unshard_cols(g):
    return jnp.transpose(g, (1, 0, 2)).reshape(g.shape[1], N_DEV * g.shape[2])


def _rope_tables():
    rows = SEQ // GRID_W
    row = jnp.repeat(jnp.arange(rows, dtype=F32), GRID_W)
    col = jnp.tile(jnp.arange(GRID_W, dtype=F32), rows)
    freq = ROPE_BASE ** (-jnp.arange(16, dtype=F32) / 16)
    ang = jnp.concatenate([row[:, None] * freq, col[:, None] * freq], axis=-1)
    cos, sin = jnp.cos(ang), jnp.sin(ang)
    z = jnp.zeros((SEQ, 32), F32)
    return jnp.concatenate([cos, z, cos, z], axis=1), jnp.concatenate([-sin, z, sin, z], axis=1)


_PACKED = (("g_attn", 1024), ("g_ffn", 1024), ("ret_decay_fwd", 4), ("ret_decay_bwd", 4), ("g_ret", 512),
           ("g_q_lora", 384), ("g_kv_lora", 256), ("g_final", 1024))
_PACK_OFF = {}
_off = 0
for _name, _n in _PACKED:
    _PACK_OFF[_name] = _off
    _off += -(-_n // LANE) * LANE
PACK_W = _off


def _pack_small(vals):
    parts = []
    for name, n in _PACKED:
        a = vals[name].reshape(-1).astype(F32)
        parts.append(jnp.pad(a, (0, -(-n // LANE) * LANE - n)))
    return jnp.concatenate(parts).reshape(1, PACK_W)


def _adamw_small(params, packed, gcc, gb_ada):
    names = list(params)
    n_p = len(names)

    def body(*refs):
        p_ref, gcc_ref, gb_ref = refs[3 * n_p:3 * n_p + 3]
        outs = refs[3 * n_p + 3:]
        for k, name in enumerate(names):
            w_ref, m_ref, v_ref = refs[3 * k:3 * k + 3]
            n = w_ref.shape[1]
            if name == "b_ada":
                g = jnp.concatenate([gb_ref[d, 0:1, :] for d in range(N_DEV)], axis=-1)
            elif name == "c_ctx":
                g = gcc_ref[0, 0:1, :]
                for d in range(1, N_DEV):
                    g = g + gcc_ref[d, 0:1, :]
            else:
                off = _PACK_OFF[name]
                g = p_ref[0, :, off:off + n]
                for d in range(1, N_DEV):
                    g = g + p_ref[d, :, off:off + n]
            mn = ADAM_B1 * m_ref[...] + (1.0 - ADAM_B1) * g
            vn = ADAM_B2 * v_ref[...] + (1.0 - ADAM_B2) * jnp.square(g)
            m_hat = mn / (1.0 - ADAM_B1 ** ADAM_STEP)
            v_hat = vn / (1.0 - ADAM_B2 ** ADAM_STEP)
            outs[4 * k][...] = g
            outs[4 * k + 1][...] = -ADAM_LR * (m_hat / (jnp.sqrt(v_hat) + ADAM_EPS) + ADAM_WD * w_ref[...])
            outs[4 * k + 2][...] = mn
            outs[4 * k + 3][...] = vn

    args = [a for name in names for a in params[name]] + [packed, gcc, gb_ada]
    out_shape = [_sds(params[name][0].shape) for name in names for _ in range(4)]
    outs = pl.pallas_call(body, name="adamw_small", out_shape=out_shape, compiler_params=_cp())(*args)
    return {name: list(outs[4 * k:4 * k + 4]) for k, name in enumerate(names)}


def kernel(x, c, ctx, c_ctx, w_ada, b_ada, g_attn, g_ffn, w_in, ret_decay_fwd, ret_decay_bwd, g_ret, g_q_lora, w_uq, g_kv_lora, w_ukv, w_out, w_ff1, w_ff2, g_final, loss_target, m_c_ctx, m_w_ada, m_b_ada, m_g_attn, m_g_ffn, m_w_in, m_ret_decay_fwd, m_ret_decay_bwd, m_g_ret, m_g_q_lora, m_w_uq, m_g_kv_lora, m_w_ukv, m_w_out, m_w_ff1, m_w_ff2, m_g_final, v_c_ctx, v_w_ada, v_b_ada, v_g_attn, v_g_ffn, v_w_in, v_ret_decay_fwd, v_ret_decay_bwd, v_g_ret, v_g_q_lora, v_w_uq, v_g_kv_lora, v_w_ukv, v_w_out, v_w_ff1, v_w_ff2, v_g_final):
    me = 4 * lax.axis_index("x") + 2 * lax.axis_index("y") + lax.axis_index("c")
    nb = x.shape[0]

    c_pad = jnp.pad(c, ((0, 8 - nb), (0, 0)))
    c_all, g_in, g_uq, g_ukv = _gather_two_level(
        [c_pad, w_in[0].T.astype(BF), w_uq[0].T.astype(BF), w_ukv[0].T.astype(BF)], "gather_weights")
    ws = (*_w_in_pad(g_in.reshape(2240, D_MODEL)), _w_uq_pad(g_uq.reshape(768, 384)),
          _w_ukv_perm(g_ukv.reshape(1024, 256)))

    crows = jnp.concatenate([c_all[:, :nb].reshape(N_DEV * nb, D_MODEL), c_ctx[None], jnp.zeros((7, D_MODEL), F32)])
    b_blk = lax.dynamic_slice(b_ada, (0, me * 768), (1, 768))
    (mod_g,) = _exchange([_mod_fwd(crows, w_ada[0], b_blk)], True, "gather_mod")
    mod_all = _unshard_cols(mod_g)
    behind = mod_g[0, 0, 0:1] * 0.0
    st_o = _exchange_start([(w_out[0] + behind).astype(BF)], True, "gather_wo_start")
    st_g = _exchange_start([w_ff1[0].T.astype(BF), w_ff2[0].astype(BF)], True, "gather_ff_start", after=st_o["token"])
    mod_all = mod_all + st_g["token"][0:1, 0:1]
    mod_mine = lax.dynamic_slice(mod_all, (me * nb, 0), (nb, 6 * D_MODEL)).reshape(nb, 6, D_MODEL)
    mod = jnp.pad(mod_mine, ((0, 0), (0, 2), (0, 0)))
    mod_c = jnp.pad(mod_all[16].reshape(1, 6, D_MODEL), ((0, 0), (0, 2), (0, 0)))

    tabs = _rope_tables()
    dec_f = ret_decay_fwd.reshape(N_HEADS, 1, 1)
    dec_b = ret_decay_bwd.reshape(N_HEADS, 1, 1)

    rkc, rvc, k_ctx, v_ctx = _k1_fwd(ctx, mod_c, g_attn, g_q_lora, g_kv_lora, ws, tabs, None, True)
    rq, rk, rv, rg, q, k_all, v_all = _k1_fwd(x, mod, g_attn, g_q_lora, g_kv_lora, ws, tabs, (k_ctx, v_ctx), False)
    o_f, o_b, sf_prev, sb_prev = _k2_fwd(rq, rk, rv, rkc, rvc, dec_f, dec_b)
    y_mla, lse = _k3_fwd(q, k_all, v_all)
    (g_out,) = _exchange_wait(st_o, y_mla, "gather_wo_wait")
    wo = g_out.reshape(D_MODEL, D_MODEL)
    x_mid, h2 = _k4a_fwd(x, o_f, o_b, rg, y_mla, g_ret, wo, mod, g_ffn)
    g_ff1t, g_ff2 = _exchange_wait(st_g, x_mid, "gather_ff_wait")
    w1t = g_ff1t.reshape(D_FF, D_MODEL)
    dxm, dmlp, relu_a, loss_acc, dgt_f, dg_final = _k4b_mlp_loss(h2, w1t, g_ff2.reshape(D_FF, D_MODEL), x_mid, mod,
                                                                 g_final.reshape(1, D_MODEL), loss_target)

    da, dw1, dw2 = _k4d_mlp_bwd(h2, dmlp, relu_a, g_ff2)
    st_s = _exchange_start([dw1, dw2], False, "scatter_ff_start")
    g_ret_t = g_ret + st_s["token"][0:1, 0:1]
    dx_res, do, drg, dym, dwo, dg_ret, dg_ffn, dmod_a = _k4e_bwd(x, o_f, o_b, rg, y_mla, g_ret_t, wo, mod, g_ffn, dxm, da,
                                                                 w1t)
    st_w = _exchange_start([dwo.reshape(N_DEV, 128, D_MODEL).astype(BF)], False, "scatter_wo_start")
    dq, dk_all, dv_all = _k3_bwd(q, k_all, v_all, y_mla, lse, dym, st_w["token"])
    dqf, dkf, dvf, dqb, dkb, dvb, dkc, dvc, ddf, ddb = _k2_bwd(rq, rk, rv, do, sf_prev, sb_prev, rkc, rvc, dec_f, dec_b)
    cts = [[(dqf, 0), (dqb, 0)], [(dkf, 0), (dkb, 0)], [(dvf, 0), (dvb, 0)], [(drg, 0)], [(dq, 0)],
           [(dk_all, 0)], [(dv_all, 0)]]
    cts_c = [[(dkc, 0)], [(dvc, 0)], [(dk_all, SEQ)], [(dv_all, SEQ)]]
    grad_x, accs, dmod_1, dmod_c1 = _k1_bwd(x, ctx, mod, mod_c, g_attn, g_q_lora, g_kv_lora, ws, tabs, cts, cts_c,
                                            dx_res)
    dwa, dwb, dwq, dwk, dg_attn, dg_q, dg_kv = accs

    dmod_loc = (dmod_a + dmod_1).at[:, 5, :].set(dgt_f[:, 0, :])[:, :6, :].reshape(nb, 6 * D_MODEL)
    dmod_ctx = dmod_c1[:, :6, :].reshape(1, 6 * D_MODEL)
    small = {"g_attn": dg_attn, "g_ffn": dg_ffn, "ret_decay_fwd": jnp.sum(ddf[:, :, 0, 0], axis=0),
             "ret_decay_bwd": jnp.sum(ddb[:, :, 0, 0], axis=0), "g_ret": dg_ret, "g_q_lora": dg_q, "g_kv_lora": dg_kv,
             "g_final": dg_final}
    extra = jnp.concatenate([dmod_loc, dmod_ctx, jnp.zeros((5, 6 * D_MODEL), F32)])
    ex_pieces = jnp.transpose(extra.reshape(8, N_DEV, 768), (1, 0, 2))
    sm_g, ex_g, loss_g = _exchange([_pack_small(small), ex_pieces, loss_acc], [True, False, True], "gather_small")
    dmod_blk = jnp.concatenate([ex_g[:, :nb].reshape(N_DEV * nb, 768), jnp.zeros((8, 768), F32)])
    gw_ada, gcc_part, gb_part = _mod_bwd(crows, w_ada[0], dmod_blk, ex_g[:, nb])
    st_c = _exchange_start([gcc_part, gb_part], True, "gather_cc_start")

    p_ff1, p_ff2 = _exchange_wait(st_s, st_c["token"], "scatter_ff_wait")
    (p_wo,) = _exchange_wait(st_w, p_ff1, "scatter_wo_wait")
    chip_sums = _pair_reduce([_w_in_cut(dwa, dwb).reshape(N_DEV, 280, D_MODEL), _w_uq_cut(dwq).reshape(N_DEV, 96, 384),
                              _w_ukv_unperm(dwk).reshape(N_DEV, 128, 256)], "pair_reduce")
    st_r = _exchange_start(chip_sums, False, "scatter_rest_start", after=p_wo, chips=True)

    res = {}
    early = (("w_ff1", w_ff1, m_w_ff1, v_w_ff1, p_ff1), ("w_ff2", w_ff2, m_w_ff2, v_w_ff2, p_ff2),
             ("w_ada", w_ada, m_w_ada, v_w_ada, gw_ada[None]))
    for name, w, m, v, pcs in early:
        res[name] = [a[None] for a in _adamw(w[0], m[0], v[0], pcs, "adamw_" + name, after=st_r["token"])]
    pieces = _exchange_wait(st_r, res["w_ada"][3], "scatter_rest_wait")
    for name, w, m, v, pcs in (("w_in", w_in, m_w_in, v_w_in, pieces[0]), ("w_uq", w_uq, m_w_uq, v_w_uq, pieces[1])):
        res[name] = [a.T[None] for a in _adamw(w[0].T, m[0].T, v[0].T, pcs, "adamw_" + name)]
    late = (("w_ukv", w_ukv, m_w_ukv, v_w_ukv, jnp.transpose(pieces[2], (0, 2, 1))),
            ("w_out", w_out, m_w_out, v_w_out, p_wo))
    for name, w, m, v, pcs in late:
        res[name] = [a[None] for a in _adamw(w[0], m[0], v[0], pcs, "adamw_" + name)]

    smalls = {"c_ctx": (c_ctx, m_c_ctx, v_c_ctx), "b_ada": (b_ada, m_b_ada, v_b_ada), "g_attn": (g_attn, m_g_attn, v_g_attn),
              "g_ffn": (g_ffn, m_g_ffn, v_g_ffn), "ret_decay_fwd": (ret_decay_fwd, m_ret_decay_fwd, v_ret_decay_fwd),
              "ret_decay_bwd": (ret_decay_bwd, m_ret_decay_bwd, v_ret_decay_bwd), "g_ret": (g_ret, m_g_ret, v_g_ret),
              "g_q_lora": (g_q_lora, m_g_q_lora, v_g_q_lora), "g_kv_lora": (g_kv_lora, m_g_kv_lora, v_g_kv_lora),
              "g_final": (g_final, m_g_final, v_g_final)}
    rows = {k: tuple(a.reshape(1, -1) for a in t) for k, t in smalls.items()}
    gcc_g, gb_g = _exchange_wait(st_c, res["w_out"][3], "gather_cc_wait")
    for name, outs in _adamw_small(rows, sm_g, gcc_g, gb_g).items():
        res[name] = [o.reshape(smalls[name][0].shape) for o in outs]

    loss = loss_g[0, 0, 0]
    for k in range(1, N_DEV):
        loss = loss + loss_g[k, 0, 0]

    order = ("c_ctx", "w_ada", "b_ada", "g_attn", "g_ffn", "w_in", "ret_decay_fwd", "ret_decay_bwd", "g_ret", "g_q_lora",
             "w_uq", "g_kv_lora", "w_ukv", "w_out", "w_ff1", "w_ff2", "g_final")
    return (loss, grad_x, *[res[n][0] for n in order], *[res[n][1] for n in order], *[res[n][2] for n in order],
            *[res[n][3] for n in order])
```

```python
import functools
import math

import jax
import jax.numpy as jnp
from jax import lax
from jax.experimental import pallas as pl
from jax.experimental.pallas import tpu as pltpu

F32 = jnp.float32
BF = jnp.bfloat16
EPS = 1e-6
LANE = 128
N_DEV = 8
D_MODEL = 1024
SEQ = 2048
CTX_LEN = 256
GRID_W = 64
N_HEADS = 4
RET_CHUNK = 512
N_CHUNK = SEQ // RET_CHUNK
D_FF = 4096
FF_BLK = D_FF // N_DEV
IN_PAD = 2816
KV_LEN = CTX_LEN + SEQ
ROPE_BASE = 10000.0
ADAM_LR, ADAM_B1, ADAM_B2, ADAM_EPS, ADAM_WD, ADAM_STEP = 0.001, 0.9, 0.999, 1e-08, 0.01, 10
TOK = 512
TOK_B = 256
VMEM_LIMIT = 56 * 1024 * 1024
ARB = "arbitrary"
MESH = pl.DeviceIdType.MESH
_HEAD_SL = [slice(LANE * h, LANE * (h + 1)) for h in range(N_HEADS)]
W_SHAPES = [(2048, D_MODEL), (768, D_MODEL), (1024, 384), (1024, 256)]


def _dot(a, b, ca, cb):
    return lax.dot_general(a.astype(BF), b.astype(BF), (((ca,), (cb,)), ((), ())), preferred_element_type=F32)


@jax.custom_vjp
def mm(a, b):
    return _dot(a, b, 1, 0)


@jax.custom_vjp
def mm_nt(a, b):
    return _dot(a, b, 1, 1)


@jax.custom_vjp
def mm_tn(a, b):
    return _dot(a, b, 0, 0)


mm.defvjp(lambda a, b: (_dot(a, b, 1, 0), (a, b)), lambda r, g: (mm_nt(g, r[1]), mm_tn(r[0], g)))
mm_nt.defvjp(lambda a, b: (_dot(a, b, 1, 1), (a, b)), lambda r, g: (mm(g, r[1]), mm_tn(g, r[0])))
mm_tn.defvjp(lambda a, b: (_dot(a, b, 0, 0), (a, b)), lambda r, g: (mm_nt(r[1], g), mm(r[0], g)))


@jax.custom_vjp
def _mmw(a, w, probe):
    return _dot(a, w, 1, 0)


def _mmw_bwd(r, g):
    a, w = r
    return mm_nt(g, w), jnp.zeros_like(w), mm_tn(a, g)


_mmw.defvjp(lambda a, w, probe: (_dot(a, w, 1, 0), (a, w)), _mmw_bwd)


@jax.custom_vjp
def _mmwt(a, wt, probe):
    return _dot(a, wt, 1, 1)


_mmwt.defvjp(lambda a, wt, probe: (_dot(a, wt, 1, 1), (a, wt)),
             lambda r, g: (mm(g, r[1]), jnp.zeros_like(r[1]), mm_tn(g, r[0])))


def mmwt(a, wt, probe):
    return _dot(a, wt, 1, 1) if probe is None else _mmwt(a, wt, probe)


def mmw(a, w, probe):
    return _dot(a, w, 1, 0) if probe is None else _mmw(a, w, probe)


def rmsn(x, g):
    return x * lax.rsqrt(jnp.mean(x * x, axis=-1, keepdims=True) + EPS) * g


def silu(x):
    return x * jax.nn.sigmoid(x)


def _swap_halves_impl(x):
    return pltpu.roll(x, 64, 1)


@jax.custom_vjp
def swap_halves(x):
    return _swap_halves_impl(x)


swap_halves.defvjp(lambda x: (_swap_halves_impl(x), None), lambda _, g: (_swap_halves_impl(g),))


def rope(x, cs1, sn1, every=1):
    blocks = []
    for i in range(x.shape[-1] // LANE):
        xb = x[:, LANE * i:LANE * (i + 1)]
        blocks.append(xb * cs1 + swap_halves(xb) * sn1 if i % every == every - 1 else xb)
    return blocks[0] if len(blocks) == 1 else jnp.concatenate(blocks, axis=-1)


def k1_tile(x, sh, sc, g_attn, g_q, g_kv, ws, ps, tabs, is_ctx):
    w_a, w_b, w_uq, w_ukv = ws
    p_a, p_b, p_uq, p_ukv = ps
    cs1, sn1 = tabs
    h = rmsn(x, g_attn) * (1.0 + sc) + sh
    pa = mmwt(h, w_a, p_a)
    pb = mmwt(h, w_b, p_b)
    rk = pa[:, 512:1024] * 0.125
    rv = pa[:, 1024:1536]
    kpe = pb[:, 640:768]
    kv = mmwt(rmsn(pb[:, 384:640], g_kv), w_ukv, p_ukv)
    if not is_ctx:
        rk = rope(rk, cs1, sn1)
        kpe = rope(kpe, cs1, sn1)
    k_full = jnp.concatenate([piece for sl in _HEAD_SL for piece in (kv[:, sl], kpe)], axis=-1)
    v = kv[:, 512:]
    if is_ctx:
        return rk, rv, k_full, v
    rq = rope(pa[:, 0:512], cs1, sn1)
    rg = pa[:, 1536:2048]
    q = rope(mmwt(rmsn(pb[:, 0:384], g_q), w_uq, p_uq), cs1, sn1, every=2)
    return rq, rk, rv, rg, q, k_full, v


def log_sigmoid(x):
    return jnp.minimum(x, 0.0) - jnp.log(1.0 + jnp.exp(-jnp.abs(x)))


def ret_chunk(q, k, v, s, lg, reverse):
    c = RET_CHUNK
    ii = lax.broadcasted_iota(jnp.int32, (c, c), 0).astype(F32)
    jj = lax.broadcasted_iota(jnp.int32, (c, c), 1).astype(F32)
    diff = (jj - ii) if reverse else (ii - jj)
    dec = jnp.where(diff >= 0, jnp.exp(lg * jnp.maximum(diff, 0.0)), 0.0)
    pos = lax.broadcasted_iota(jnp.int32, (c, 1), 0).astype(F32)
    if reverse:
        wk, wq = jnp.exp(lg * pos), jnp.exp(lg * (c - pos))
    else:
        wk, wq = jnp.exp(lg * (c - 1.0 - pos)), jnp.exp(lg * (pos + 1.0))
    o = mm(mm_nt(q, k) * dec, v) + mm(q * wq, s)
    s_next = jnp.exp(lg * float(c)) * s + mm_tn(k * wk, v)
    return o, s_next


def ctx_state(kc, vc, lg, reverse):
    n = kc.shape[0]
    pos = lax.broadcasted_iota(jnp.int32, (n, 1), 0).astype(F32)
    w = jnp.exp(lg * pos) if reverse else jnp.exp(lg * (n - 1.0 - pos))
    return mm_tn(kc * w, vc)


def attn_head(qn, qp, kn, kp, v):
    s = (mm_nt(qn, kn) + mm_nt(qp, kp)) * (1.0 / math.sqrt(192.0))
    e = jnp.exp(s - jnp.max(s, axis=-1, keepdims=True))
    return mm(e / jnp.sum(e, axis=-1, keepdims=True), v)


def gn_gate(o, rg, g_ret):
    ys = []
    for h in range(N_HEADS):
        sl = slice(LANE * h, LANE * (h + 1))
        oh = o[:, sl]
        mu = jnp.mean(oh, axis=-1, keepdims=True)
        var = jnp.mean(jnp.square(oh - mu), axis=-1, keepdims=True)
        ys.append((oh - mu) * lax.rsqrt(var + EPS) * g_ret[:, sl])
    return jnp.concatenate(ys, axis=-1) * silu(rg)


def k4a_tile(x, o_f, o_b, rg, y_mla, g_ret, gt_a, g_ffn, sh_f, sc_f, w_out, p_out):
    mix = jnp.concatenate([gn_gate(o_f + o_b, rg, g_ret), y_mla], axis=-1)
    x_mid = x + gt_a * mmw(mix, w_out, p_out)
    h2 = rmsn(x_mid, g_ffn) * (1.0 + sc_f) + sh_f
    return x_mid, h2


def k4c_tile(x_mid, mlp, gt_f, g_final, tgt):
    y = rmsn(x_mid + gt_f * mlp, g_final)
    per_tok = jnp.mean(jnp.square(y - tgt), axis=-1, keepdims=True)
    return 0.5 * jnp.sum(per_tok, axis=0, keepdims=True)


def _cp(sem=None, vmem=VMEM_LIMIT):
    return pltpu.CompilerParams(dimension_semantics=sem, vmem_limit_bytes=vmem)


def _acc(ref, val, first):
    @pl.when(first)
    def _():
        ref[...] = val

    @pl.when(jnp.logical_not(first))
    def _():
        ref[...] += val


def _full(shape):
    nd = len(shape)
    return pl.BlockSpec(shape, lambda *_: (0,) * nd)


ANY = pl.BlockSpec(memory_space=pl.ANY)


def _sds(shape, dtype=F32):
    return jax.ShapeDtypeStruct(shape, dtype)


def _exchange(arrs, gather, name):
    n = len(arrs)
    modes = [gather] * n if isinstance(gather, bool) else list(gather)
    out_shape = [_sds(((N_DEV,) + a.shape) if g else a.shape, a.dtype) for a, g in zip(arrs, modes)]

    def body(*refs):
        ins, outs = refs[:n], refs[n:2 * n]
        send_sems, recv_sems, local_sems = refs[2 * n:]
        x, y, c = lax.axis_index("x"), lax.axis_index("y"), lax.axis_index("c")
        me = 4 * x + 2 * y + c
        sends, recvs, locs = [], [], []
        for i in range(n):
            gather = modes[i]
            for k in range(N_DEV - 1):
                bits = k + 1
                px = x ^ ((bits >> 2) & 1)
                py = y ^ ((bits >> 1) & 1)
                pc = c ^ (bits & 1)
                peer = 4 * px + 2 * py + pc
                src = ins[i] if gather else ins[i].at[peer]
                sem = i * (N_DEV - 1) + k
                sends.append(pltpu.make_async_remote_copy(
                    src_ref=src, dst_ref=outs[i].at[me], send_sem=send_sems.at[sem], recv_sem=recv_sems.at[sem],
                    device_id=(px, py, pc), device_id_type=MESH))
                recvs.append(pltpu.make_async_remote_copy(
                    src_ref=src, dst_ref=outs[i].at[peer], send_sem=send_sems.at[sem], recv_sem=recv_sems.at[sem],
                    device_id=(px, py, pc), device_id_type=MESH))
            locs.append(pltpu.make_async_copy(ins[i] if gather else ins[i].at[me], outs[i].at[me], local_sems.at[i]))
        for cp in locs + sends:
            cp.start()
        for cp in recvs:
            cp.wait_recv()
        for cp in sends:
            cp.wait_send()
        for cp in locs:
            cp.wait()

    outs = pl.pallas_call(
        body, name=name, out_shape=out_shape, in_specs=[ANY] * n, out_specs=[ANY] * n,
        scratch_shapes=[pltpu.SemaphoreType.DMA((n * (N_DEV - 1),)), pltpu.SemaphoreType.DMA((n * (N_DEV - 1),)),
                        pltpu.SemaphoreType.DMA((n,))],
    )(*arrs)
    return list(outs)


def _gather_two_level(arrs, name):
    n = len(arrs)

    def body(*refs):
        ins, outs = refs[:n], refs[n:2 * n]
        send_sems, recv_sems, local_sems = refs[2 * n:]
        x, y, c = lax.axis_index("x"), lax.axis_index("y"), lax.axis_index("c")
        sibling = (x, y, 1 - c)
        chips = [(1 - x, y), (x, 1 - y), (1 - x, 1 - y)]

        def slot(px, py, pc):
            return 4 * px + 2 * py + pc

        first, passed, waits, locs = [], [], [], []
        for i in range(n):
            def copy(k, block, to, src=None, i=i):
                dst = outs[i].at[slot(*block)]
                return pltpu.make_async_remote_copy(
                    src_ref=dst if src is None else src, dst_ref=dst, send_sem=send_sems.at[7 * i + k],
                    recv_sem=recv_sems.at[7 * i + k], device_id=to, device_id_type=MESH)

            locs.append(pltpu.make_async_copy(ins[i], outs[i].at[slot(x, y, c)], local_sems.at[i]))
            first.append(copy(0, (x, y, c), sibling, src=ins[i]))
            first += [copy(1 + j, (x, y, c), (*chip, c), src=ins[i]) for j, chip in enumerate(chips)]
            passed.append([copy(4 + j, (*chip, c), sibling) for j, chip in enumerate(chips)])
            waits.append([copy(1 + j, (*chip, c), (x, y, c)) for j, chip in enumerate(chips)])
        for cp in locs + first:
            cp.start()
        for j in range(3):
            for i in range(n):
                waits[i][j].wait_recv()
                passed[i][j].start()
        for i in range(n):
            def arrival(k, block, i=i):
                dst = outs[i].at[slot(*block)]
                return pltpu.make_async_remote_copy(
                    src_ref=dst, dst_ref=dst, send_sem=send_sems.at[7 * i + k], recv_sem=recv_sems.at[7 * i + k],
                    device_id=sibling, device_id_type=MESH)

            arrival(0, (x, y, 1 - c)).wait_recv()
            for j, chip in enumerate(chips):
                arrival(4 + j, (*chip, 1 - c)).wait_recv()
        for cp in first + [p for ps in passed for p in ps]:
            cp.wait_send()
        for cp in locs:
            cp.wait()

    outs = pl.pallas_call(
        body, name=name, out_shape=[_sds((N_DEV,) + a.shape, a.dtype) for a in arrs], in_specs=[ANY] * n,
        out_specs=[ANY] * n,
        scratch_shapes=[pltpu.SemaphoreType.DMA((7 * n,)), pltpu.SemaphoreType.DMA((7 * n,)),
                        pltpu.SemaphoreType.DMA((n,))],
    )(*arrs)
    return list(outs)


HBM = pl.BlockSpec(memory_space=pltpu.HBM)
SEM = pl.BlockSpec(memory_space=pltpu.SEMAPHORE)
EFFECT = pltpu.SideEffectType.DATAFLOW_SIDE_EFFECTING


def _peer(k, chips=False):
    x, y, c = lax.axis_index("x"), lax.axis_index("y"), lax.axis_index("c")
    bits = (k + 1) << 1 if chips else k + 1
    px, py, pc = x ^ ((bits >> 2) & 1), y ^ ((bits >> 1) & 1), c ^ (bits & 1)
    if chips:
        return (px, py, pc), 2 * px + py, 2 * x + y
    return (px, py, pc), 4 * px + 2 * py + pc, 4 * x + 2 * y + c


def _exchange_start(arrs, gather, name, after=None, chips=False):
    n = len(arrs)
    n_peer, n_slot = (3, 4) if chips else (N_DEV - 1, N_DEV)
    lands = [pltpu.with_memory_space_constraint(lax.empty(((n_slot,) + a.shape) if gather else a.shape, a.dtype),
                                                pltpu.HBM) for a in arrs]
    srcs = [pltpu.with_memory_space_constraint(a, pltpu.HBM) for a in arrs]

    extra = [] if after is None else [after]

    def body(*refs):
        ins, zones = refs[:n], refs[n:2 * n]
        send_sems, recv_sems, local_sems = refs[2 * n + len(extra):2 * n + len(extra) + 3]
        token = refs[-1]
        for i in range(n):
            for k in range(n_peer):
                dev, peer, me = _peer(k, chips)
                sem = i * n_peer + k
                pltpu.make_async_remote_copy(
                    src_ref=ins[i] if gather else ins[i].at[peer], dst_ref=zones[i].at[me],
                    send_sem=send_sems.at[sem], recv_sem=recv_sems.at[sem], device_id=dev, device_id_type=MESH).start()
            _, _, me = _peer(0, chips)
            pltpu.make_async_copy(ins[i] if gather else ins[i].at[me], zones[i].at[me], local_sems.at[i]).start()
        token[...] = jnp.zeros_like(token)

    nsem = n * n_peer
    outs = pl.pallas_call(
        body, name=name,
        out_shape=[pltpu.SemaphoreType.DMA((nsem,)), pltpu.SemaphoreType.DMA((nsem,)), pltpu.SemaphoreType.DMA((n,))]
        + [pltpu.HBM(a.shape, a.dtype) for a in srcs] + [pltpu.HBM(z.shape, z.dtype) for z in lands]
        + [_sds((8, LANE))],
        in_specs=[HBM] * (2 * n) + [ANY] * len(extra),
        out_specs=[SEM, SEM, SEM] + [HBM] * (2 * n) + [pl.BlockSpec(memory_space=pltpu.VMEM)],
        input_output_aliases={i: 3 + i for i in range(2 * n)},
        compiler_params=pltpu.CompilerParams(has_side_effects=EFFECT),
    )(*srcs, *lands, *extra)
    return {"n": n, "gather": gather, "chips": chips, "sems": outs[:3], "srcs": outs[3:3 + n],
            "lands": outs[3 + n:3 + 2 * n], "token": outs[-1]}


def _exchange_wait(st, after, name):
    n, gather, chips = st["n"], st["gather"], st["chips"]
    n_peer = 3 if chips else N_DEV - 1

    def body(*refs):
        ins, zones = refs[:n], refs[n:2 * n]
        send_sems, recv_sems, local_sems = refs[2 * n:2 * n + 3]
        for i in range(n):
            for k in range(n_peer):
                dev, peer, me = _peer(k, chips)
                sem = i * n_peer + k
                src = ins[i] if gather else ins[i].at[peer]
                cp = pltpu.make_async_remote_copy(
                    src_ref=src, dst_ref=zones[i].at[peer], send_sem=send_sems.at[sem], recv_sem=recv_sems.at[sem],
                    device_id=dev, device_id_type=MESH)
                cp.wait_send()
                cp.wait_recv()
            _, _, me = _peer(0, chips)
            pltpu.make_async_copy(ins[i] if gather else ins[i].at[me], zones[i].at[me], local_sems.at[i]).wait()

    outs = pl.pallas_call(
        body, name=name,
        out_shape=[pltpu.HBM(a.shape, a.dtype) for a in st["srcs"]] + [pltpu.HBM(z.shape, z.dtype) for z in st["lands"]],
        in_specs=[HBM] * (2 * n) + [SEM, SEM, SEM, ANY], out_specs=[HBM] * (2 * n),
        input_output_aliases={i: i for i in range(2 * n)},
        compiler_params=pltpu.CompilerParams(has_side_effects=EFFECT),
    )(*st["srcs"], *st["lands"], *st["sems"], after)
    return list(outs[n:])


def _pair_reduce(arrs, name):
    n = len(arrs)

    def body(*refs):
        ins, outs, got, mine = refs[:n], refs[n:2 * n], refs[2 * n:3 * n], refs[3 * n:4 * n]
        send_sems, recv_sems, local_sems = refs[4 * n:]
        x, y, c = lax.axis_index("x"), lax.axis_index("y"), lax.axis_index("c")
        sends, locs = [], []
        for i in range(n):
            for q in range(4):
                sem = 4 * i + q
                sends.append(pltpu.make_async_remote_copy(
                    src_ref=ins[i].at[2 * q + 1 - c], dst_ref=got[i].at[q], send_sem=send_sems.at[sem],
                    recv_sem=recv_sems.at[sem], device_id=(x, y, 1 - c), device_id_type=MESH))
                locs.append(pltpu.make_async_copy(ins[i].at[2 * q + c], mine[i].at[q], local_sems.at[sem]))
        for cp in locs + sends:
            cp.start()
        for cp in sends:
            cp.wait_recv()
        for cp in locs:
            cp.wait()
        for i in range(n):
            outs[i][...] = (mine[i][...].astype(F32) + got[i][...].astype(F32)).astype(BF)
        for cp in sends:
            cp.wait_send()

    half = [(4,) + a.shape[1:] for a in arrs]
    outs = pl.pallas_call(
        body, name=name, out_shape=[_sds(h, BF) for h in half], in_specs=[ANY] * n,
        out_specs=[pl.BlockSpec(memory_space=pltpu.VMEM)] * n,
        scratch_shapes=[pltpu.VMEM(h, BF) for h in half] * 2
        + [pltpu.SemaphoreType.DMA((4 * n,)), pltpu.SemaphoreType.DMA((4 * n,)), pltpu.SemaphoreType.DMA((4 * n,))],
        compiler_params=_cp(),
    )(*arrs)
    return list(outs)


def _mod_fwd(crows, w_ada, b_blk):
    def body(c_ref, w_ref, b_ref, o_ref):
        o_ref[...] = mm(silu(c_ref[...]), w_ref[...]) + b_ref[...]

    return pl.pallas_call(body, name="mod_fwd", out_shape=_sds((24, 768)), compiler_params=_cp())(crows, w_ada, b_blk)


def _mod_bwd(crows, w_ada, dmod_blk, dmodc_blk):
    def body(c_ref, w_ref, d_ref, dc_ref, gw_ref, gc_ref, gb_ref):
        cr = c_ref[...]
        dc = dc_ref[0:1, :]
        for p in range(1, N_DEV):
            dc = dc + dc_ref[p:p + 1, :]
        row = lax.broadcasted_iota(jnp.int32, (24, 1), 0)
        gw_ref[...] = mm_tn(silu(cr), jnp.where(row == 16, dc, d_ref[...]))
        cc = cr[16:17, :]
        sg = jax.nn.sigmoid(cc)
        part = mm_nt(jnp.broadcast_to(dc, (8, 768)), w_ref[...])
        gc_ref[...] = part * (sg * (1.0 + cc * (1.0 - sg)))
        gb_ref[...] = jnp.broadcast_to(jnp.sum(d_ref[...], axis=0, keepdims=True) + dc, (8, 768))

    return pl.pallas_call(
        body, name="mod_bwd", out_shape=[_sds((D_MODEL, 768)), _sds((8, D_MODEL)), _sds((8, 768))],
        compiler_params=_cp())(crows, w_ada, dmod_blk, dmodc_blk)


def _tab_specs(tk):
    return [pl.BlockSpec((tk, LANE), lambda i, t: (t, 0))] * 2


def _k1_fwd(x, mod, g_attn, g_q, g_kv, ws, tabs, kv_all, is_ctx):
    b, l, _ = x.shape
    tk = CTX_LEN if is_ctx else TOK
    nt = l // tk
    n_f32 = 2 if is_ctx else 4

    def body(x_ref, mod_ref, ga_ref, gq_ref, gk_ref, wa_ref, wb_ref, wq_ref, wk_ref, cs_ref, sn_ref, *rest):
        outs = rest if is_ctx else rest[2:]
        res = k1_tile(x_ref[...], mod_ref[0:1, :], mod_ref[1:2, :], ga_ref[...], gq_ref[...], gk_ref[...],
                      (wa_ref[...], wb_ref[...], wq_ref[...], wk_ref[...]), (None,) * 4,
                      (cs_ref[...], sn_ref[...]), is_ctx)
        for o_ref, r in zip(outs, res):
            o_ref[...] = r.astype(o_ref.dtype)

    tok = lambda w, off=0: pl.BlockSpec((None, tk, w), lambda i, t: (i, t + off, 0))
    mod_spec = pl.BlockSpec((None, 8, D_MODEL), (lambda i, t: (0, 0, 0)) if is_ctx else (lambda i, t: (i, 0, 0)))
    kv_off = SEQ // tk if is_ctx else 0
    in_specs = ([tok(D_MODEL), mod_spec, _full((1, D_MODEL)), _full((1, 384)), _full((1, 256))]
                + [_full(s) for s in W_SHAPES] + _tab_specs(tk))
    args = [x, mod, g_attn, g_q, g_kv, *ws, *tabs]
    out_specs = [tok(512)] * n_f32 + ([] if is_ctx else [tok(1024)]) + [tok(1024, kv_off), tok(512, kv_off)]
    out_shape = ([_sds((b, l, 512))] * n_f32 + ([] if is_ctx else [_sds((b, l, 1024), BF)])
                 + [_sds((b, KV_LEN, 1024), BF), _sds((b, KV_LEN, 512), BF)])
    aliases = {}
    if not is_ctx:
        aliases = {len(args): n_f32 + 1, len(args) + 1: n_f32 + 2}
        in_specs += [ANY, ANY]
        args += list(kv_all)
    return pl.pallas_call(
        body, name="k1_fwd_ctx" if is_ctx else "k1_fwd", grid=(b, nt), in_specs=in_specs, out_specs=out_specs,
        out_shape=out_shape, input_output_aliases=aliases, compiler_params=_cp((ARB, ARB)),
    )(*args)


N_ACC = 7


def _k1_bwd(x, ctx, mod, mod_c, g_attn, g_q, g_kv, ws, tabs, cts, cts_c, dx_res):
    b, l, _ = x.shape
    tk = TOK_B
    nt = l // tk
    flat = [[a for group in c for a in group] for c in (cts, cts_c)]
    sizes = [[len(g) for g in c] for c in (cts, cts_c)]
    acc_shapes = W_SHAPES + [(1, D_MODEL), (1, 384), (1, 256)]

    def body(*refs):
        it = iter(refs)
        x_ref, c_ref, mod_ref, modc_ref, ga_ref, gq_ref, gk_ref = [next(it) for _ in range(7)]
        w_hbm = [next(it) for _ in range(4)]
        tab_refs = [next(it) for _ in range(2)]
        ct_refs = [[next(it) for _ in f] for f in flat]
        res_ref, gx_ref = next(it), next(it)
        out_hbm = [next(it) for _ in range(N_ACC)]
        dmod_ref, dmodc_ref = next(it), next(it)
        w_vmem = [next(it) for _ in range(4)]
        accs = [next(it) for _ in range(N_ACC)]
        sem = next(it)
        i, t = pl.program_id(0), pl.program_id(1)
        first = jnp.logical_and(i == 0, t == 0)

        @pl.when(first)
        def _():
            for src, dst in zip(w_hbm, w_vmem):
                pltpu.sync_copy(src, dst)
            for k in range(N_ACC):
                accs[k][...] = jnp.zeros(acc_shapes[k], F32)

        def tile(is_ctx):
            which = 1 if is_ctx else 0
            ct_vals, pos = [], 0
            for gsz in sizes[which]:
                v = ct_refs[which][pos][...].astype(F32)
                for r in ct_refs[which][pos + 1:pos + gsz]:
                    v = v + r[...]
                ct_vals.append(v)
                pos += gsz
            wv = tuple(r[...] for r in w_vmem)
            tv = tuple(r[...] for r in tab_refs)
            m_ref = modc_ref if is_ctx else mod_ref

            def f(xv, sh, sc, ga, gq, gk, *probes):
                return k1_tile(xv, sh, sc, ga, gq, gk, wv, probes, tv, is_ctx)

            probes = [jnp.zeros(s, F32) for s in W_SHAPES]
            xin = c_ref[...] if is_ctx else x_ref[...]
            _, vjp = jax.vjp(f, xin, m_ref[0:1, :], m_ref[1:2, :], ga_ref[...], gq_ref[...], gk_ref[...], *probes)
            dx, dsh, dsc, dga, dgq, dgk, dwa, dwb, dwq, dwk = vjp(tuple(ct_vals))
            for ref, val in zip(accs, (dwa, dwb, dwq, dwk, dga, dgq, dgk)):
                ref[...] += val
            return dx, dsh, dsc

        @pl.when(t == 0)
        def _():
            _, dsh, dsc = tile(True)
            _acc(dmodc_ref.at[0:1, :], dsh, i == 0)
            _acc(dmodc_ref.at[1:2, :], dsc, i == 0)

            @pl.when(i == 0)
            def _():
                dmodc_ref[2:8, :] = jnp.zeros((6, D_MODEL), F32)

        @pl.when(t > 0)
        def _():
            dx, dsh, dsc = tile(False)
            gx_ref[...] = dx + res_ref[...]
            _acc(dmod_ref.at[0:1, :], dsh, t == 1)
            _acc(dmod_ref.at[1:2, :], dsc, t == 1)

            @pl.when(t == 1)
            def _():
                dmod_ref[2:8, :] = jnp.zeros((6, D_MODEL), F32)

        @pl.when(jnp.logical_and(i == b - 1, t == nt))
        def _():
            for k in range(4):
                w_vmem[k][...] = accs[k][...].astype(BF)
            cps = [pltpu.make_async_copy(w_vmem[k] if k < 4 else accs[k], out_hbm[k], sem.at[k]) for k in range(N_ACC)]
            for cp in cps:
                cp.start()
            for cp in cps:
                cp.wait()

    lat = lambda w, off=0: pl.BlockSpec((None, tk, w), lambda i, t: (i, jnp.maximum(t - 1, 0) + off, 0))
    con = lambda w, off=0: pl.BlockSpec((None, tk, w), lambda i, t: (i, off, 0))
    mod_spec = pl.BlockSpec((None, 8, D_MODEL), lambda i, t: (i, 0, 0))
    modc_spec = pl.BlockSpec((None, 8, D_MODEL), lambda i, t: (0, 0, 0))
    tab_spec = pl.BlockSpec((tk, LANE), lambda i, t: (jnp.maximum(t - 1, 0), 0))
    in_specs = ([lat(D_MODEL), con(D_MODEL), mod_spec, modc_spec, _full((1, D_MODEL)), _full((1, 384)), _full((1, 256))]
                + [ANY] * 4 + [tab_spec] * 2)
    args = [x, ctx, mod, mod_c, g_attn, g_q, g_kv, *ws, *tabs]
    for a, off in flat[0]:
        in_specs.append(lat(a.shape[-1], off // tk))
        args.append(a)
    for a, off in flat[1]:
        in_specs.append(con(a.shape[-1], off // tk))
        args.append(a)
    in_specs.append(lat(D_MODEL))
    args.append(dx_res)
    out_shape = ([_sds((b, l, D_MODEL))] + [_sds(s, BF) for s in W_SHAPES] + [_sds(s) for s in acc_shapes[4:]]
                 + [_sds((b, 8, D_MODEL)), _sds((1, 8, D_MODEL))])
    out_specs = [lat(D_MODEL)] + [ANY] * N_ACC + [mod_spec, modc_spec]
    outs = pl.pallas_call(
        body, name="k1_bwd", grid=(b, nt + 1), in_specs=in_specs, out_specs=out_specs, out_shape=out_shape,
        scratch_shapes=[pltpu.VMEM(s, BF) for s in W_SHAPES] + [pltpu.VMEM(s, F32) for s in acc_shapes]
        + [pltpu.SemaphoreType.DMA((N_ACC,))],
        compiler_params=_cp((ARB, ARB)),
    )(*args)
    return outs[0], list(outs[1:1 + N_ACC]), outs[1 + N_ACC], outs[2 + N_ACC]


def _chunk_spec(rev):
    if rev:
        return pl.BlockSpec((None, RET_CHUNK, 512), lambda i, n: (i, N_CHUNK - 1 - n, 0))
    return pl.BlockSpec((None, RET_CHUNK, 512), lambda i, n: (i, n, 0))


def _state_spec(rev):
    if rev:
        return pl.BlockSpec((None, N_HEADS, None, LANE, LANE), lambda i, n: (i, 0, N_CHUNK - 1 - n, 0, 0))
    return pl.BlockSpec((None, N_HEADS, None, LANE, LANE), lambda i, n: (i, 0, n, 0, 0))


_CTX_SPEC = pl.BlockSpec((None, CTX_LEN, 512), lambda i, n: (i, 0, 0))
_DEC_SPEC = pl.BlockSpec((N_HEADS, 1, 1), lambda i, n: (0, 0, 0))


def _k2_fwd(rq, rk, rv, rkc, rvc, dec_f, dec_b):
    b = rq.shape[0]

    def body(qf, kf, vf, qb, kb, vb, kc, vc, df, db, of_ref, ob_ref, sf_out, sb_out, sf, sb):
        n = pl.program_id(1)
        for h, sl in enumerate(_HEAD_SL):
            lgf, lgb = log_sigmoid(df[h]), log_sigmoid(db[h])

            @pl.when(n == 0)
            def _():
                sf[h] = ctx_state(kc[:, sl], vc[:, sl], lgf, False)
                sb[h] = ctx_state(kc[:, sl], vc[:, sl], lgb, True)

            sf_out[h] = sf[h]
            sb_out[h] = sb[h]
            o, s = ret_chunk(qf[:, sl], kf[:, sl], vf[:, sl], sf[h], lgf, False)
            of_ref[:, sl] = o
            sf[h] = s
            o, s = ret_chunk(qb[:, sl], kb[:, sl], vb[:, sl], sb[h], lgb, True)
            ob_ref[:, sl] = o
            sb[h] = s

    l = rq.shape[1]
    return pl.pallas_call(
        body, name="k2_fwd", grid=(b, N_CHUNK),
        in_specs=[_chunk_spec(False)] * 3 + [_chunk_spec(True)] * 3 + [_CTX_SPEC, _CTX_SPEC, _DEC_SPEC, _DEC_SPEC],
        out_specs=[_chunk_spec(False), _chunk_spec(True), _state_spec(False), _state_spec(True)],
        out_shape=[_sds((b, l, 512)), _sds((b, l, 512)), _sds((b, N_HEADS, N_CHUNK, LANE, LANE)),
                   _sds((b, N_HEADS, N_CHUNK, LANE, LANE))],
        scratch_shapes=[pltpu.VMEM((N_HEADS, LANE, LANE), F32), pltpu.VMEM((N_HEADS, LANE, LANE), F32)],
        compiler_params=_cp((ARB, ARB)),
    )(rq, rk, rv, rq, rk, rv, rkc, rvc, dec_f, dec_b)


def _k2_bwd(rq, rk, rv, do, sf_prev, sb_prev, rkc, rvc, dec_f, dec_b):
    b, l, _ = rq.shape

    def body(qf, kf, vf, gf, spf, qb, kb, vb, gb, spb, kc, vc, df, db,
             dqf, dkf, dvf, dqb, dkb, dvb, dkc, dvc, ddf, ddb, dsf, dsb):
        n = pl.program_id(1)

        @pl.when(n == 0)
        def _():
            dsf[...] = jnp.zeros((N_HEADS, LANE, LANE), F32)
            dsb[...] = jnp.zeros((N_HEADS, LANE, LANE), F32)

        def one(h, sl, q, k, v, g, sp, dec, ds, dq, dk, dv, dd, rev):
            def f(qv, kv_, vv, sv, dcy):
                return ret_chunk(qv, kv_, vv, sv, log_sigmoid(dcy), rev)

            _, vjp = jax.vjp(f, q[:, sl], k[:, sl], v[:, sl], sp[h], dec[h])
            gq, gk, gv, gs, gd = vjp((g[:, sl], ds[h]))
            dq[:, sl] = gq
            dk[:, sl] = gk
            dv[:, sl] = gv
            ds[h] = gs
            _acc(dd.at[h], jnp.broadcast_to(gd, (8, LANE)), n == 0)

        for h, sl in enumerate(_HEAD_SL):
            one(h, sl, qf, kf, vf, gf, spf, df, dsf, dqf, dkf, dvf, ddf, False)
            one(h, sl, qb, kb, vb, gb, spb, db, dsb, dqb, dkb, dvb, ddb, True)

        @pl.when(n == N_CHUNK - 1)
        def _():
            def f(kcv, vcv, dcy, rev):
                return ctx_state(kcv, vcv, log_sigmoid(dcy), rev)

            for h, sl in enumerate(_HEAD_SL):
                _, vjp_f = jax.vjp(functools.partial(f, rev=False), kc[:, sl], vc[:, sl], df[h])
                gk_f, gv_f, gd_f = vjp_f(dsf[h])
                _, vjp_b = jax.vjp(functools.partial(f, rev=True), kc[:, sl], vc[:, sl], db[h])
                gk_b, gv_b, gd_b = vjp_b(dsb[h])
                dkc[:, sl] = gk_f + gk_b
                dvc[:, sl] = gv_f + gv_b
                ddf[h] += jnp.broadcast_to(gd_f, (8, LANE))
                ddb[h] += jnp.broadcast_to(gd_b, (8, LANE))

    dd_spec = pl.BlockSpec((None, N_HEADS, 8, LANE), lambda i, n: (i, 0, 0, 0))
    return pl.pallas_call(
        body, name="k2_bwd", grid=(b, N_CHUNK),
        in_specs=[_chunk_spec(True)] * 4 + [_state_spec(True)] + [_chunk_spec(False)] * 4 + [_state_spec(False)]
        + [_CTX_SPEC, _CTX_SPEC, _DEC_SPEC, _DEC_SPEC],
        out_specs=[_chunk_spec(True)] * 3 + [_chunk_spec(False)] * 3 + [_CTX_SPEC, _CTX_SPEC, dd_spec, dd_spec],
        out_shape=[_sds((b, l, 512))] * 6 + [_sds((b, CTX_LEN, 512))] * 2 + [_sds((b, N_HEADS, 8, LANE))] * 2,
        scratch_shapes=[pltpu.VMEM((N_HEADS, LANE, LANE), F32), pltpu.VMEM((N_HEADS, LANE, LANE), F32)],
        compiler_params=_cp((ARB, ARB)),
    )(rq, rk, rv, do, sf_prev, rq, rk, rv, do, sb_prev, rkc, rvc, dec_f, dec_b)


TQ = 1024
TQ_F = 512
QK_W = 2 * LANE
N_QP = 2
_Q_PARTS = [slice(i * TQ_F // N_QP, (i + 1) * TQ_F // N_QP) for i in range(N_QP)]


SM_SCALE = 1.0 / math.sqrt(192.0)


def _k3_specs(tq):
    qs = lambda w: pl.BlockSpec((None, tq, w), lambda i, h, t: (i, t, h))
    ks = lambda w: pl.BlockSpec((None, KV_LEN, w), lambda i, h, t: (i, 0, h))
    return qs, ks


def _k3_fwd(q, k, v):
    b, l, _ = q.shape

    def body(q_ref, k_ref, v_ref, o_ref, lse_ref):
        kv_, vv = k_ref[...], v_ref[...]
        for r in _Q_PARTS:
            s = _dot(q_ref[r, :], kv_, 1, 1) * SM_SCALE
            m = jnp.max(s, axis=-1, keepdims=True)
            e = jnp.exp(s - m)
            tot = jnp.sum(e, axis=-1, keepdims=True)
            o_ref[r, :] = _dot(e, vv, 1, 0) * (1.0 / tot)
            lse_ref[r, :] = jnp.broadcast_to(m + jnp.log(tot), (TQ_F // N_QP, LANE))

    qs, ks = _k3_specs(TQ_F)
    return pl.pallas_call(
        body, name="k3_fwd", grid=(b, N_HEADS, l // TQ_F), in_specs=[qs(QK_W), ks(QK_W), ks(LANE)],
        out_specs=[qs(LANE), qs(LANE)], out_shape=[_sds((b, l, N_HEADS * LANE))] * 2,
        compiler_params=_cp((ARB, ARB, ARB)),
    )(q, k, v)


def _k3_bwd(q, k, v, o, lse, dy, after):
    b, l, _ = q.shape

    def body(q_ref, k_ref, v_ref, o_ref, lse_ref, dy_ref, after_ref, dq_ref, dk_ref, dv_ref):
        t0 = pl.program_id(2) == 0
        kv_, vv = k_ref[...], v_ref[...]
        qv, dyv = q_ref[...], dy_ref[...]
        g = dyv.astype(BF)
        lse_col = jnp.max(lse_ref[...], axis=-1, keepdims=True)
        delta = jnp.sum(dyv * o_ref[...], axis=-1, keepdims=True)
        p = jnp.exp(_dot(qv, kv_, 1, 1) * SM_SCALE - lse_col)
        ds = (p * (_dot(g, vv, 1, 1) - delta) * SM_SCALE).astype(BF)
        _acc(dv_ref, _dot(p, g, 0, 0), t0)
        dq_ref[...] = _dot(ds, kv_, 1, 0)
        _acc(dk_ref, _dot(ds, qv, 0, 0), t0)

    qs, ks = _k3_specs(TQ)
    return pl.pallas_call(
        body, name="k3_bwd", grid=(b, N_HEADS, l // TQ),
        in_specs=[qs(QK_W), ks(QK_W), ks(LANE), qs(LANE), qs(LANE), qs(LANE), ANY],
        out_specs=[qs(QK_W), ks(QK_W), ks(LANE)],
        out_shape=[_sds((b, l, N_HEADS * QK_W)), _sds((b, KV_LEN, N_HEADS * QK_W)), _sds((b, KV_LEN, N_HEADS * LANE))],
        compiler_params=_cp((ARB, ARB, ARB)),
    )(q, k, v, o, lse, dy, after)


def _mod_rows(mod_ref, rows):
    return [mod_ref[r:r + 1, :] for r in rows]


def _k4a_fwd(x, o_f, o_b, rg, y_mla, g_ret, w_out, mod, g_ffn):
    b, l, _ = x.shape

    def body(x_ref, of_ref, ob_ref, rg_ref, ym_ref, gr_ref, wo_ref, mod_ref, gf_ref, xm_ref, h2_ref):
        gt_a, sh_f, sc_f = _mod_rows(mod_ref, (2, 3, 4))
        x_mid, h2 = k4a_tile(x_ref[...], of_ref[...], ob_ref[...], rg_ref[...], ym_ref[...], gr_ref[...], gt_a,
                             gf_ref[...], sh_f, sc_f, wo_ref[...], None)
        xm_ref[...] = x_mid
        h2_ref[...] = h2.astype(BF)

    tok = lambda w: pl.BlockSpec((None, TOK, w), lambda i, t: (i, t, 0))
    mod_spec = pl.BlockSpec((None, 8, D_MODEL), lambda i, t: (i, 0, 0))
    return pl.pallas_call(
        body, name="k4a_fwd", grid=(b, l // TOK),
        in_specs=[tok(D_MODEL), tok(512), tok(512), tok(512), tok(512), _full((1, 512)), _full((D_MODEL, D_MODEL)),
                  mod_spec, _full((1, D_MODEL))],
        out_specs=[tok(D_MODEL), tok(D_MODEL)], out_shape=[_sds((b, l, D_MODEL)), _sds((b, l, D_MODEL), BF)],
        compiler_params=_cp((ARB, ARB)),
    )(x, o_f, o_b, rg, y_mla, g_ret, w_out, mod, g_ffn)


TOK_M = 512
TOK_D = 2048
HALF_FF = D_FF // 2


def _k4b_mlp_loss(h2, w1t, w2, x_mid, mod, g_final, tgt):
    b, l, _ = h2.shape
    nt = l // TOK_M

    def body(h2_ref, w1_hbm, w2_hbm, xm_ref, mod_ref, gfin_ref, tgt_ref, dxm_ref, dmlp_ref, r_ref, loss_ref, dgt_ref,
             dgfin_ref, w1_v, w2_v):
        i, t = pl.program_id(0), pl.program_id(1)
        first = jnp.logical_and(i == 0, t == 0)

        @pl.when(first)
        def _():
            pltpu.sync_copy(w1_hbm, w1_v)
            pltpu.sync_copy(w2_hbm, w2_v)

        h2v = h2_ref[...]
        mlp = None
        for half in range(2):
            rows = slice(half * HALF_FF, (half + 1) * HALF_FF)
            r = jnp.maximum(_dot(h2v, w1_v[rows, :], 1, 1), 0.0)
            r_ref[:, rows] = r.astype(BF)
            part = _dot(jnp.square(r), w2_v[rows, :], 1, 0)
            mlp = part if mlp is None else mlp + part
        (gt_f,) = _mod_rows(mod_ref, (5,))
        loss, vjp = jax.vjp(k4c_tile, xm_ref[...], mlp, gt_f, gfin_ref[...], tgt_ref[...])
        dxm, dmlp, dgt, dgfin, _ = vjp(jnp.ones((1, 1), F32))
        dxm_ref[...] = dxm
        dmlp_ref[...] = dmlp.astype(BF)
        _acc(loss_ref, jnp.broadcast_to(loss, (8, LANE)), first)
        _acc(dgfin_ref, dgfin, first)
        _acc(dgt_ref, dgt, t == 0)

    tok = lambda w: pl.BlockSpec((None, TOK_M, w), lambda i, t: (i, t, 0))
    return pl.pallas_call(
        body, name="k4b_mlp_loss", grid=(b, nt),
        in_specs=[tok(D_MODEL), ANY, ANY, tok(D_MODEL), pl.BlockSpec((None, 8, D_MODEL), lambda i, t: (i, 0, 0)),
                  _full((1, D_MODEL)), tok(D_MODEL)],
        out_specs=[tok(D_MODEL), tok(D_MODEL), tok(D_FF), _full((8, LANE)),
                   pl.BlockSpec((None, 1, D_MODEL), lambda i, t: (i, 0, 0)), _full((1, D_MODEL))],
        out_shape=[_sds((b, l, D_MODEL)), _sds((b, l, D_MODEL), BF), _sds((b, l, D_FF), BF), _sds((8, LANE)),
                   _sds((b, 1, D_MODEL)), _sds((1, D_MODEL))],
        scratch_shapes=[pltpu.VMEM((D_FF, D_MODEL), BF), pltpu.VMEM((D_FF, D_MODEL), BF)],
        compiler_params=_cp((ARB, ARB)),
    )(h2, w1t, w2, x_mid, mod, g_final, tgt)


def _k4d_mlp_bwd(h2, dmlp, r, w2):
    b, l, _ = h2.shape
    nt = l // TOK_D

    def body(h2_ref, dm_ref, r_ref, w2_ref, da_ref, dw1_ref, dw2_ref, acc1, acc2):
        i, t = pl.program_id(1), pl.program_id(2)
        first = jnp.logical_and(i == 0, t == 0)
        rv = r_ref[...].astype(F32)
        dm = dm_ref[...]
        da = (_dot(dm, w2_ref[...], 1, 1) * (2.0 * rv)).astype(BF)
        da_ref[...] = da
        _acc(acc2, _dot(jnp.square(rv), dm, 0, 0), first)
        _acc(acc1, _dot(h2_ref[...], da, 0, 0), first)

        @pl.when(jnp.logical_and(i == b - 1, t == nt - 1))
        def _():
            dw1_ref[...] = acc1[...].astype(BF)
            dw2_ref[...] = acc2[...].astype(BF)

    tok = lambda w: pl.BlockSpec((None, TOK_D, w), lambda j, i, t: (i, t, 0))
    col = pl.BlockSpec((None, TOK_D, FF_BLK), lambda j, i, t: (i, t, j))
    return pl.pallas_call(
        body, name="k4d_mlp_bwd", grid=(N_DEV, b, nt),
        in_specs=[tok(D_MODEL), tok(D_MODEL), col, pl.BlockSpec((None, FF_BLK, D_MODEL), lambda j, i, t: (j, 0, 0))],
        out_specs=[col, pl.BlockSpec((None, D_MODEL, FF_BLK), lambda j, i, t: (j, 0, 0)),
                   pl.BlockSpec((None, FF_BLK, D_MODEL), lambda j, i, t: (j, 0, 0))],
        out_shape=[_sds((b, l, D_FF), BF), _sds((N_DEV, D_MODEL, FF_BLK), BF), _sds((N_DEV, FF_BLK, D_MODEL), BF)],
        scratch_shapes=[pltpu.VMEM((D_MODEL, FF_BLK), F32), pltpu.VMEM((FF_BLK, D_MODEL), F32)],
        compiler_params=_cp((ARB, ARB, ARB)),
    )(h2, dmlp, r, w2)


def _k4e_bwd(x, o_f, o_b, rg, y_mla, g_ret, w_out, mod, g_ffn, dxm, da, w1t):
    b, l, _ = x.shape

    def body(x_ref, of_ref, ob_ref, rg_ref, ym_ref, gr_ref, wo_ref, mod_ref, gf_ref, dxm_ref, da_ref, w1_hbm,
             dx_ref, do_ref, drg_ref, dym_ref, dwo_ref, dgr_ref, dgf_ref, dmod_ref, w1_v):
        i, t = pl.program_id(0), pl.program_id(1)
        first = jnp.logical_and(i == 0, t == 0)

        @pl.when(first)
        def _():
            pltpu.sync_copy(w1_hbm, w1_v)

        gt_a, sh_f, sc_f = _mod_rows(mod_ref, (2, 3, 4))
        wo = wo_ref[...]
        dh2 = _dot(da_ref[...], w1_v[...], 1, 0)

        def f(xv, ofv, rgv, ymv, grv, gta, gfv, shf, scf, p_out):
            return k4a_tile(xv, ofv, ob_ref[...], rgv, ymv, grv, gta, gfv, shf, scf, wo, p_out)

        _, vjp = jax.vjp(f, x_ref[...], of_ref[...], rg_ref[...], ym_ref[...], gr_ref[...], gt_a, gf_ref[...], sh_f,
                         sc_f, jnp.zeros((D_MODEL, D_MODEL), F32))
        dx, do, drg, dym, dgr, dgta, dgf, dshf, dscf, dwo = vjp((dxm_ref[...], dh2))
        dx_ref[...] = dx
        do_ref[...] = do
        drg_ref[...] = drg
        dym_ref[...] = dym
        _acc(dwo_ref, dwo, first)
        _acc(dgr_ref, dgr, first)
        _acc(dgf_ref, dgf, first)
        t0 = t == 0
        _acc(dmod_ref.at[2:3, :], dgta, t0)
        _acc(dmod_ref.at[3:4, :], dshf, t0)
        _acc(dmod_ref.at[4:5, :], dscf, t0)

        @pl.when(t0)
        def _():
            dmod_ref[0:2, :] = jnp.zeros((2, D_MODEL), F32)
            dmod_ref[5:8, :] = jnp.zeros((3, D_MODEL), F32)

    tok = lambda w: pl.BlockSpec((None, TOK_B, w), lambda i, t: (i, t, 0))
    mod_spec = pl.BlockSpec((None, 8, D_MODEL), lambda i, t: (i, 0, 0))
    return pl.pallas_call(
        body, name="k4e_bwd", grid=(b, l // TOK_B),
        in_specs=[tok(D_MODEL), tok(512), tok(512), tok(512), tok(512), _full((1, 512)), _full((D_MODEL, D_MODEL)),
                  mod_spec, _full((1, D_MODEL)), tok(D_MODEL), tok(D_FF), ANY],
        out_specs=[tok(D_MODEL), tok(512), tok(512), tok(512), _full((D_MODEL, D_MODEL)), _full((1, 512)),
                   _full((1, D_MODEL)), mod_spec],
        out_shape=[_sds((b, l, D_MODEL)), _sds((b, l, 512)), _sds((b, l, 512)), _sds((b, l, 512)),
                   _sds((D_MODEL, D_MODEL)), _sds((1, 512)), _sds((1, D_MODEL)), _sds((b, 8, D_MODEL))],
        scratch_shapes=[pltpu.VMEM((D_FF, D_MODEL), BF)],
        compiler_params=_cp((ARB, ARB)),
    )(x, o_f, o_b, rg, y_mla, g_ret, w_out, mod, g_ffn, dxm, da, w1t)


def _adamw(w, m, v, pieces, name, after=None):
    r, c = w.shape
    npc = pieces.shape[0]
    per_row = c * (7 * 4 + npc * pieces.dtype.itemsize) * 2
    rb = r
    for cand in (r, 512, 256, 128, 64, 32, 16, 8):
        if r % cand == 0 and cand * per_row <= 32 * 1024 * 1024:
            rb = cand
            break

    def body(w_ref, m_ref, v_ref, p_ref, *rest):
        g_ref, d_ref, nm_ref, nv_ref = rest[-4:]
        g = p_ref[0].astype(F32)
        for k in range(1, npc):
            g = g + p_ref[k].astype(F32)
        wv = w_ref[...]
        mn = ADAM_B1 * m_ref[...] + (1.0 - ADAM_B1) * g
        vn = ADAM_B2 * v_ref[...] + (1.0 - ADAM_B2) * jnp.square(g)
        m_hat = mn / (1.0 - ADAM_B1 ** ADAM_STEP)
        v_hat = vn / (1.0 - ADAM_B2 ** ADAM_STEP)
        g_ref[...] = g
        d_ref[...] = -ADAM_LR * (m_hat / (jnp.sqrt(v_hat) + ADAM_EPS) + ADAM_WD * wv)
        nm_ref[...] = mn
        nv_ref[...] = vn

    blk = pl.BlockSpec((rb, c), lambda i: (i, 0))
    extra = [] if after is None else [after]
    return pl.pallas_call(
        body, name=name, grid=(r // rb,),
        in_specs=[blk, blk, blk, pl.BlockSpec((npc, rb, c), lambda i: (0, i, 0))] + [ANY] * len(extra),
        out_specs=[blk] * 4, out_shape=[_sds((r, c))] * 4, compiler_params=_cp((ARB,)),
    )(w, m, v, pieces, *extra)


def _pad_rot_rows(w):
    k = w.shape[1]
    return jnp.pad(w.reshape(-1, 2, 32, k), ((0, 0), (0, 0), (0, 32), (0, 0))).reshape(-1, k)


def _cut_rot_rows(g):
    k = g.shape[1]
    return g.reshape(-1, 2, 64, k)[:, :, :32].reshape(-1, k)


def _w_in_pad(wt):
    w_a = jnp.concatenate([_pad_rot_rows(wt[0:512]), wt[512:1536]], axis=0)
    w_b = jnp.concatenate([wt[1536:2176], _pad_rot_rows(wt[2176:2240])], axis=0)
    return w_a, w_b


def _w_in_cut(g_a, g_b):
    return jnp.concatenate([_cut_rot_rows(g_a[0:1024]), g_a[1024:2048], g_b[0:640], _cut_rot_rows(g_b[640:768])], axis=0)


def _w_uq_pad(wt):
    w = wt.reshape(N_HEADS, 192, 384)
    rot = _pad_rot_rows(w[:, 128:].reshape(N_HEADS * 64, 384)).reshape(N_HEADS, LANE, 384)
    return jnp.concatenate([w[:, :128], rot], axis=1).reshape(1024, 384)


def _w_uq_cut(g):
    g = g.reshape(N_HEADS, 256, 384)
    rot = _cut_rot_rows(g[:, 128:].reshape(N_HEADS * LANE, 384)).reshape(N_HEADS, 64, 384)
    return jnp.concatenate([g[:, :128], rot], axis=1).reshape(768, 384)


def _w_ukv_perm(wt):
    return jnp.transpose(wt.reshape(N_HEADS, 2, LANE, 256), (1, 0, 2, 3)).reshape(1024, 256)


def _w_ukv_unperm(g):
    return jnp.transpose(g.reshape(2, N_HEADS, LANE, 256), (1, 0, 2, 3)).reshape(1024, 256)


def _unshard_cols(g):
    return jnp.transpose(g, (1, 0, 2)).reshape(g.shape[1], N_DEV * g.shape[2])


def _rope_tables():
    rows = SEQ // GRID_W
    row = jnp.repeat(jnp.arange(rows, dtype=F32), GRID_W)
    col = jnp.tile(jnp.arange(GRID_W, dtype=F32), rows)
    freq = ROPE_BASE ** (-jnp.arange(16, dtype=F32) / 16)
    ang = jnp.concatenate([row[:, None] * freq, col[:, None] * freq], axis=-1)
    cos, sin = jnp.cos(ang), jnp.sin(ang)
    z = jnp.zeros((SEQ, 32), F32)
    return jnp.concatenate([cos, z, cos, z], axis=1), jnp.concatenate([-sin, z, sin, z], axis=1)


_PACKED = (("g_attn", 1024), ("g_ffn", 1024), ("ret_decay_fwd", 4), ("ret_decay_bwd", 4), ("g_ret", 512),
           ("g_q_lora", 384), ("g_kv_lora", 256), ("g_final", 1024))
_PACK_OFF = {}
_off = 0
for _name, _n in _PACKED:
    _PACK_OFF[_name] = _off
    _off += -(-_n // LANE) * LANE
PACK_W = _off


def _pack_small(vals):
    parts = []
    for name, n in _PACKED:
        a = vals[name].reshape(-1).astype(F32)
        parts.append(jnp.pad(a, (0, -(-n // LANE) * LANE - n)))
    return jnp.concatenate(parts).reshape(1, PACK_W)


def _adamw_small(params, packed, gcc, gb_ada):
    names = list(params)
    n_p = len(names)

    def body(*refs):
        p_ref, gcc_ref, gb_ref = refs[3 * n_p:3 * n_p + 3]
        outs = refs[3 * n_p + 3:]
        for k, name in enumerate(names):
            w_ref, m_ref, v_ref = refs[3 * k:3 * k + 3]
            n = w_ref.shape[1]
            if name == "b_ada":
                g = jnp.concatenate([gb_ref[d, 0:1, :] for d in range(N_DEV)], axis=-1)
            elif name == "c_ctx":
                g = gcc_ref[0, 0:1, :]
                for d in range(1, N_DEV):
                    g = g + gcc_ref[d, 0:1, :]
            else:
                off = _PACK_OFF[name]
                g = p_ref[0, :, off:off + n]
                for d in range(1, N_DEV):
                    g = g + p_ref[d, :, off:off + n]
            mn = ADAM_B1 * m_ref[...] + (1.0 - ADAM_B1) * g
            vn = ADAM_B2 * v_ref[...] + (1.0 - ADAM_B2) * jnp.square(g)
            m_hat = mn / (1.0 - ADAM_B1 ** ADAM_STEP)
            v_hat = vn / (1.0 - ADAM_B2 ** ADAM_STEP)
            outs[4 * k][...] = g
            outs[4 * k + 1][...] = -ADAM_LR * (m_hat / (jnp.sqrt(v_hat) + ADAM_EPS) + ADAM_WD * w_ref[...])
            outs[4 * k + 2][...] = mn
            outs[4 * k + 3][...] = vn

    args = [a for name in names for a in params[name]] + [packed, gcc, gb_ada]
    out_shape = [_sds(params[name][0].shape) for name in names for _ in range(4)]
    outs = pl.pallas_call(body, name="adamw_small", out_shape=out_shape, compiler_params=_cp())(*args)
    return {name: list(outs[4 * k:4 * k + 4]) for k, name in enumerate(names)}


def kernel(x, c, ctx, c_ctx, w_ada, b_ada, g_attn, g_ffn, w_in, ret_decay_fwd, ret_decay_bwd, g_ret, g_q_lora, w_uq, g_kv_lora, w_ukv, w_out, w_ff1, w_ff2, g_final, loss_target, m_c_ctx, m_w_ada, m_b_ada, m_g_attn, m_g_ffn, m_w_in, m_ret_decay_fwd, m_ret_decay_bwd, m_g_ret, m_g_q_lora, m_w_uq, m_g_kv_lora, m_w_ukv, m_w_out, m_w_ff1, m_w_ff2, m_g_final, v_c_ctx, v_w_ada, v_b_ada, v_g_attn, v_g_ffn, v_w_in, v_ret_decay_fwd, v_ret_decay_bwd, v_g_ret, v_g_q_lora, v_w_uq, v_g_kv_lora, v_w_ukv, v_w_out, v_w_ff1, v_w_ff2, v_g_final):
    me = 4 * lax.axis_index("x") + 2 * lax.axis_index("y") + lax.axis_index("c")
    nb = x.shape[0]

    c_pad = jnp.pad(c, ((0, 8 - nb), (0, 0)))
    c_all, g_in, g_uq, g_ukv = _gather_two_level(
        [c_pad, w_in[0].T.astype(BF), w_uq[0].T.astype(BF), w_ukv[0].T.astype(BF)], "gather_weights")
    ws = (*_w_in_pad(g_in.reshape(2240, D_MODEL)), _w_uq_pad(g_uq.reshape(768, 384)),
          _w_ukv_perm(g_ukv.reshape(1024, 256)))

    crows = jnp.concatenate([c_all[:, :nb].reshape(N_DEV * nb, D_MODEL), c_ctx[None], jnp.zeros((7, D_MODEL), F32)])
    b_blk = lax.dynamic_slice(b_ada, (0, me * 768), (1, 768))
    (mod_g,) = _exchange([_mod_fwd(crows, w_ada[0], b_blk)], True, "gather_mod")
    mod_all = _unshard_cols(mod_g)
    behind = mod_g[0, 0, 0:1] * 0.0
    st_o = _exchange_start([(w_out[0] + behind).astype(BF)], True, "gather_wo_start")
    st_g = _exchange_start([w_ff1[0].T.astype(BF), w_ff2[0].astype(BF)], True, "gather_ff_start", after=st_o["token"])
    mod_all = mod_all + st_g["token"][0:1, 0:1]
    mod_mine = lax.dynamic_slice(mod_all, (me * nb, 0), (nb, 6 * D_MODEL)).reshape(nb, 6, D_MODEL)
    mod = jnp.pad(mod_mine, ((0, 0), (0, 2), (0, 0)))
    mod_c = jnp.pad(mod_all[16].reshape(1, 6, D_MODEL), ((0, 0), (0, 2), (0, 0)))

    tabs = _rope_tables()
    dec_f = ret_decay_fwd.reshape(N_HEADS, 1, 1)
    dec_b = ret_decay_bwd.reshape(N_HEADS, 1, 1)

    rkc, rvc, k_ctx, v_ctx = _k1_fwd(ctx, mod_c, g_attn, g_q_lora, g_kv_lora, ws, tabs, None, True)
    rq, rk, rv, rg, q, k_all, v_all = _k1_fwd(x, mod, g_attn, g_q_lora, g_kv_lora, ws, tabs, (k_ctx, v_ctx), False)
    o_f, o_b, sf_prev, sb_prev = _k2_fwd(rq, rk, rv, rkc, rvc, dec_f, dec_b)
    y_mla, lse = _k3_fwd(q, k_all, v_all)
    (g_out,) = _exchange_wait(st_o, y_mla, "gather_wo_wait")
    wo = g_out.reshape(D_MODEL, D_MODEL)
    x_mid, h2 = _k4a_fwd(x, o_f, o_b, rg, y_mla, g_ret, wo, mod, g_ffn)
    g_ff1t, g_ff2 = _exchange_wait(st_g, x_mid, "gather_ff_wait")
    w1t = g_ff1t.reshape(D_FF, D_MODEL)
    dxm, dmlp, relu_a, loss_acc, dgt_f, dg_final = _k4b_mlp_loss(h2, w1t, g_ff2.reshape(D_FF, D_MODEL), x_mid, mod,
                                                                 g_final.reshape(1, D_MODEL), loss_target)

    da, dw1, dw2 = _k4d_mlp_bwd(h2, dmlp, relu_a, g_ff2)
    st_s = _exchange_start([dw1, dw2], False, "scatter_ff_start")
    g_ret_t = g_ret + st_s["token"][0:1, 0:1]
    dx_res, do, drg, dym, dwo, dg_ret, dg_ffn, dmod_a = _k4e_bwd(x, o_f, o_b, rg, y_mla, g_ret_t, wo, mod, g_ffn, dxm, da,
                                                                 w1t)
    st_w = _exchange_start([dwo.reshape(N_DEV, 128, D_MODEL).astype(BF)], False, "scatter_wo_start")
    dq, dk_all, dv_all = _k3_bwd(q, k_all, v_all, y_mla, lse, dym, st_w["token"])
    dqf, dkf, dvf, dqb, dkb, dvb, dkc, dvc, ddf, ddb = _k2_bwd(rq, rk, rv, do, sf_prev, sb_prev, rkc, rvc, dec_f, dec_b)
    cts = [[(dqf, 0), (dqb, 0)], [(dkf, 0), (dkb, 0)], [(dvf, 0), (dvb, 0)], [(drg, 0)], [(dq, 0)],
           [(dk_all, 0)], [(dv_all, 0)]]
    cts_c = [[(dkc, 0)], [(dvc, 0)], [(dk_all, SEQ)], [(dv_all, SEQ)]]
    grad_x, accs, dmod_1, dmod_c1 = _k1_bwd(x, ctx, mod, mod_c, g_attn, g_q_lora, g_kv_lora, ws, tabs, cts, cts_c,
                                            dx_res)
    dwa, dwb, dwq, dwk, dg_attn, dg_q, dg_kv = accs

    dmod_loc = (dmod_a + dmod_1).at[:, 5, :].set(dgt_f[:, 0, :])[:, :6, :].reshape(nb, 6 * D_MODEL)
    dmod_ctx = dmod_c1[:, :6, :].reshape(1, 6 * D_MODEL)
    small = {"g_attn": dg_attn, "g_ffn": dg_ffn, "ret_decay_fwd": jnp.sum(ddf[:, :, 0, 0], axis=0),
             "ret_decay_bwd": jnp.sum(ddb[:, :, 0, 0], axis=0), "g_ret": dg_ret, "g_q_lora": dg_q, "g_kv_lora": dg_kv,
             "g_final": dg_final}
    extra = jnp.concatenate([dmod_loc, dmod_ctx, jnp.zeros((5, 6 * D_MODEL), F32)])
    ex_pieces = jnp.transpose(extra.reshape(8, N_DEV, 768), (1, 0, 2))
    sm_g, ex_g, loss_g = _exchange([_pack_small(small), ex_pieces, loss_acc], [True, False, True], "gather_small")
    dmod_blk = jnp.concatenate([ex_g[:, :nb].reshape(N_DEV * nb, 768), jnp.zeros((8, 768), F32)])
    gw_ada, gcc_part, gb_part = _mod_bwd(crows, w_ada[0], dmod_blk, ex_g[:, nb])
    st_c = _exchange_start([gcc_part, gb_part], True, "gather_cc_start")

    p_ff1, p_ff2 = _exchange_wait(st_s, st_c["token"], "scatter_ff_wait")
    (p_wo,) = _exchange_wait(st_w, p_ff1, "scatter_wo_wait")
    chip_sums = _pair_reduce([_w_in_cut(dwa, dwb).reshape(N_DEV, 280, D_MODEL), _w_uq_cut(dwq).reshape(N_DEV, 96, 384),
                              _w_ukv_unperm(dwk).reshape(N_DEV, 128, 256)], "pair_reduce")
    st_r = _exchange_start(chip_sums, False, "scatter_rest_start", after=p_wo, chips=True)

    res = {}
    early = (("w_ff1", w_ff1, m_w_ff1, v_w_ff1, p_ff1), ("w_ff2", w_ff2, m_w_ff2, v_w_ff2, p_ff2),
             ("w_ada", w_ada, m_w_ada, v_w_ada, gw_ada[None]), ("w_out", w_out, m_w_out, v_w_out, p_wo))
    behind = st_r["token"]
    for name, w, m, v, pcs in early:
        res[name] = [a[None] for a in _adamw(w[0], m[0], v[0], pcs, "adamw_" + name, after=behind)]
        behind = res[name][3]

    smalls = {"c_ctx": (c_ctx, m_c_ctx, v_c_ctx), "b_ada": (b_ada, m_b_ada, v_b_ada), "g_attn": (g_attn, m_g_attn, v_g_attn),
              "g_ffn": (g_ffn, m_g_ffn, v_g_ffn), "ret_decay_fwd": (ret_decay_fwd, m_ret_decay_fwd, v_ret_decay_fwd),
              "ret_decay_bwd": (ret_decay_bwd, m_ret_decay_bwd, v_ret_decay_bwd), "g_ret": (g_ret, m_g_ret, v_g_ret),
              "g_q_lora": (g_q_lora, m_g_q_lora, v_g_q_lora), "g_kv_lora": (g_kv_lora, m_g_kv_lora, v_g_kv_lora),
              "g_final": (g_final, m_g_final, v_g_final)}
    rows = {k: tuple(a.reshape(1, -1) for a in t) for k, t in smalls.items()}
    gcc_g, gb_g = _exchange_wait(st_c, behind, "gather_cc_wait")
    small_out = _adamw_small(rows, sm_g, gcc_g, gb_g)
    for name, outs in small_out.items():
        res[name] = [o.reshape(smalls[name][0].shape) for o in outs]

    pieces = _exchange_wait(st_r, small_out["g_final"][3], "scatter_rest_wait")
    for name, w, m, v, pcs in (("w_in", w_in, m_w_in, v_w_in, pieces[0]), ("w_uq", w_uq, m_w_uq, v_w_uq, pieces[1])):
        res[name] = [a.T[None] for a in _adamw(w[0].T, m[0].T, v[0].T, pcs, "adamw_" + name)]
    res["w_ukv"] = [a[None] for a in _adamw(w_ukv[0], m_w_ukv[0], v_w_ukv[0], jnp.transpose(pieces[2], (0, 2, 1)),
                                            "adamw_w_ukv")]

    loss = loss_g[0, 0, 0]
    for k in range(1, N_DEV):
        loss = loss + loss_g[k, 0, 0]

    order = ("c_ctx", "w_ada", "b_ada", "g_attn", "g_ffn", "w_in", "ret_decay_fwd", "ret_decay_bwd", "g_ret", "g_q_lora",
             "w_uq", "g_kv_lora", "w_ukv", "w_out", "w_ff1", "w_ff2", "g_final")
    return (loss, grad_x, *[res[n][0] for n in order], *[res[n][1] for n in order], *[res[n][2] for n in order],
            *[res[n][3] for n in order])
```

```python
import functools
import math

import jax
import jax.numpy as jnp
from jax import lax
from jax.experimental import pallas as pl
from jax.experimental.pallas import tpu as pltpu

F32 = jnp.float32
BF = jnp.bfloat16
EPS = 1e-6
LANE = 128
N_DEV = 8
D_MODEL = 1024
SEQ = 2048
CTX_LEN = 256
GRID_W = 64
N_HEADS = 4
RET_CHUNK = 512
N_CHUNK = SEQ // RET_CHUNK
D_FF = 4096
FF_BLK = D_FF // N_DEV
IN_PAD = 2816
KV_LEN = CTX_LEN + SEQ
ROPE_BASE = 10000.0
ADAM_LR, ADAM_B1, ADAM_B2, ADAM_EPS, ADAM_WD, ADAM_STEP = 0.001, 0.9, 0.999, 1e-08, 0.01, 10
TOK = 512
TOK_B = 256
VMEM_LIMIT = 56 * 1024 * 1024
ARB = "arbitrary"
MESH = pl.DeviceIdType.MESH
_HEAD_SL = [slice(LANE * h, LANE * (h + 1)) for h in range(N_HEADS)]
W_SHAPES = [(2048, D_MODEL), (768, D_MODEL), (1024, 384), (1024, 256)]


def _dot(a, b, ca, cb):
    return lax.dot_general(a.astype(BF), b.astype(BF), (((ca,), (cb,)), ((), ())), preferred_element_type=F32)


@jax.custom_vjp
def mm(a, b):
    return _dot(a, b, 1, 0)


@jax.custom_vjp
def mm_nt(a, b):
    return _dot(a, b, 1, 1)


@jax.custom_vjp
def mm_tn(a, b):
    return _dot(a, b, 0, 0)


mm.defvjp(lambda a, b: (_dot(a, b, 1, 0), (a, b)), lambda r, g: (mm_nt(g, r[1]), mm_tn(r[0], g)))
mm_nt.defvjp(lambda a, b: (_dot(a, b, 1, 1), (a, b)), lambda r, g: (mm(g, r[1]), mm_tn(g, r[0])))
mm_tn.defvjp(lambda a, b: (_dot(a, b, 0, 0), (a, b)), lambda r, g: (mm_nt(r[1], g), mm(r[0], g)))


@jax.custom_vjp
def _mmw(a, w, probe):
    return _dot(a, w, 1, 0)


def _mmw_bwd(r, g):
    a, w = r
    return mm_nt(g, w), jnp.zeros_like(w), mm_tn(a, g)


_mmw.defvjp(lambda a, w, probe: (_dot(a, w, 1, 0), (a, w)), _mmw_bwd)


@jax.custom_vjp
def _mmwt(a, wt, probe):
    return _dot(a, wt, 1, 1)


_mmwt.defvjp(lambda a, wt, probe: (_dot(a, wt, 1, 1), (a, wt)),
             lambda r, g: (mm(g, r[1]), jnp.zeros_like(r[1]), mm_tn(g, r[0])))


def mmwt(a, wt, probe):
    return _dot(a, wt, 1, 1) if probe is None else _mmwt(a, wt, probe)


def mmw(a, w, probe):
    return _dot(a, w, 1, 0) if probe is None else _mmw(a, w, probe)


def rmsn(x, g):
    return x * lax.rsqrt(jnp.mean(x * x, axis=-1, keepdims=True) + EPS) * g


def silu(x):
    return x * jax.nn.sigmoid(x)


def _swap_halves_impl(x):
    return pltpu.roll(x, 64, 1)


@jax.custom_vjp
def swap_halves(x):
    return _swap_halves_impl(x)


swap_halves.defvjp(lambda x: (_swap_halves_impl(x), None), lambda _, g: (_swap_halves_impl(g),))


def rope(x, cs1, sn1, every=1):
    blocks = []
    for i in range(x.shape[-1] // LANE):
        xb = x[:, LANE * i:LANE * (i + 1)]
        blocks.append(xb * cs1 + swap_halves(xb) * sn1 if i % every == every - 1 else xb)
    return blocks[0] if len(blocks) == 1 else jnp.concatenate(blocks, axis=-1)


def k1_tile(x, sh, sc, g_attn, g_q, g_kv, ws, ps, tabs, is_ctx):
    w_a, w_b, w_uq, w_ukv = ws
    p_a, p_b, p_uq, p_ukv = ps
    cs1, sn1 = tabs
    h = rmsn(x, g_attn) * (1.0 + sc) + sh
    pa = mmwt(h, w_a, p_a)
    pb = mmwt(h, w_b, p_b)
    rk = pa[:, 512:1024] * 0.125
    rv = pa[:, 1024:1536]
    kpe = pb[:, 640:768]
    kv = mmwt(rmsn(pb[:, 384:640], g_kv), w_ukv, p_ukv)
    if not is_ctx:
        rk = rope(rk, cs1, sn1)
        kpe = rope(kpe, cs1, sn1)
    k_full = jnp.concatenate([piece for sl in _HEAD_SL for piece in (kv[:, sl], kpe)], axis=-1)
    v = kv[:, 512:]
    if is_ctx:
        return rk, rv, k_full, v
    rq = rope(pa[:, 0:512], cs1, sn1)
    rg = pa[:, 1536:2048]
    q = rope(mmwt(rmsn(pb[:, 0:384], g_q), w_uq, p_uq), cs1, sn1, every=2)
    return rq, rk, rv, rg, q, k_full, v


def log_sigmoid(x):
    return jnp.minimum(x, 0.0) - jnp.log(1.0 + jnp.exp(-jnp.abs(x)))


def ret_chunk(q, k, v, s, lg, reverse):
    c = RET_CHUNK
    ii = lax.broadcasted_iota(jnp.int32, (c, c), 0).astype(F32)
    jj = lax.broadcasted_iota(jnp.int32, (c, c), 1).astype(F32)
    diff = (jj - ii) if reverse else (ii - jj)
    dec = jnp.where(diff >= 0, jnp.exp(lg * jnp.maximum(diff, 0.0)), 0.0)
    pos = lax.broadcasted_iota(jnp.int32, (c, 1), 0).astype(F32)
    if reverse:
        wk, wq = jnp.exp(lg * pos), jnp.exp(lg * (c - pos))
    else:
        wk, wq = jnp.exp(lg * (c - 1.0 - pos)), jnp.exp(lg * (pos + 1.0))
    o = mm(mm_nt(q, k) * dec, v) + mm(q * wq, s)
    s_next = jnp.exp(lg * float(c)) * s + mm_tn(k * wk, v)
    return o, s_next


def ctx_state(kc, vc, lg, reverse):
    n = kc.shape[0]
    pos = lax.broadcasted_iota(jnp.int32, (n, 1), 0).astype(F32)
    w = jnp.exp(lg * pos) if reverse else jnp.exp(lg * (n - 1.0 - pos))
    return mm_tn(kc * w, vc)


def attn_head(qn, qp, kn, kp, v):
    s = (mm_nt(qn, kn) + mm_nt(qp, kp)) * (1.0 / math.sqrt(192.0))
    e = jnp.exp(s - jnp.max(s, axis=-1, keepdims=True))
    return mm(e / jnp.sum(e, axis=-1, keepdims=True), v)


def gn_gate(o, rg, g_ret):
    ys = []
    for h in range(N_HEADS):
        sl = slice(LANE * h, LANE * (h + 1))
        oh = o[:, sl]
        mu = jnp.mean(oh, axis=-1, keepdims=True)
        var = jnp.mean(jnp.square(oh - mu), axis=-1, keepdims=True)
        ys.append((oh - mu) * lax.rsqrt(var + EPS) * g_ret[:, sl])
    return jnp.concatenate(ys, axis=-1) * silu(rg)


def k4a_tile(x, o_f, o_b, rg, y_mla, g_ret, gt_a, g_ffn, sh_f, sc_f, w_out, p_out):
    mix = jnp.concatenate([gn_gate(o_f + o_b, rg, g_ret), y_mla], axis=-1)
    x_mid = x + gt_a * mmw(mix, w_out, p_out)
    h2 = rmsn(x_mid, g_ffn) * (1.0 + sc_f) + sh_f
    return x_mid, h2


def k4c_tile(x_mid, mlp, gt_f, g_final, tgt):
    y = rmsn(x_mid + gt_f * mlp, g_final)
    per_tok = jnp.mean(jnp.square(y - tgt), axis=-1, keepdims=True)
    return 0.5 * jnp.sum(per_tok, axis=0, keepdims=True)


def _cp(sem=None, vmem=VMEM_LIMIT):
    return pltpu.CompilerParams(dimension_semantics=sem, vmem_limit_bytes=vmem)


def _acc(ref, val, first):
    @pl.when(first)
    def _():
        ref[...] = val

    @pl.when(jnp.logical_not(first))
    def _():
        ref[...] += val


def _full(shape):
    nd = len(shape)
    return pl.BlockSpec(shape, lambda *_: (0,) * nd)


ANY = pl.BlockSpec(memory_space=pl.ANY)


def _sds(shape, dtype=F32):
    return jax.ShapeDtypeStruct(shape, dtype)


def _exchange(arrs, gather, name):
    n = len(arrs)
    modes = [gather] * n if isinstance(gather, bool) else list(gather)
    out_shape = [_sds(((N_DEV,) + a.shape) if g else a.shape, a.dtype) for a, g in zip(arrs, modes)]

    def body(*refs):
        ins, outs = refs[:n], refs[n:2 * n]
        send_sems, recv_sems, local_sems = refs[2 * n:]
        x, y, c = lax.axis_index("x"), lax.axis_index("y"), lax.axis_index("c")
        me = 4 * x + 2 * y + c
        sends, recvs, locs = [], [], []
        for i in range(n):
            gather = modes[i]
            for k in range(N_DEV - 1):
                bits = k + 1
                px = x ^ ((bits >> 2) & 1)
                py = y ^ ((bits >> 1) & 1)
                pc = c ^ (bits & 1)
                peer = 4 * px + 2 * py + pc
                src = ins[i] if gather else ins[i].at[peer]
                sem = i * (N_DEV - 1) + k
                sends.append(pltpu.make_async_remote_copy(
                    src_ref=src, dst_ref=outs[i].at[me], send_sem=send_sems.at[sem], recv_sem=recv_sems.at[sem],
                    device_id=(px, py, pc), device_id_type=MESH))
                recvs.append(pltpu.make_async_remote_copy(
                    src_ref=src, dst_ref=outs[i].at[peer], send_sem=send_sems.at[sem], recv_sem=recv_sems.at[sem],
                    device_id=(px, py, pc), device_id_type=MESH))
            locs.append(pltpu.make_async_copy(ins[i] if gather else ins[i].at[me], outs[i].at[me], local_sems.at[i]))
        for cp in locs + sends:
            cp.start()
        for cp in recvs:
            cp.wait_recv()
        for cp in sends:
            cp.wait_send()
        for cp in locs:
            cp.wait()

    outs = pl.pallas_call(
        body, name=name, out_shape=out_shape, in_specs=[ANY] * n, out_specs=[ANY] * n,
        scratch_shapes=[pltpu.SemaphoreType.DMA((n * (N_DEV - 1),)), pltpu.SemaphoreType.DMA((n * (N_DEV - 1),)),
                        pltpu.SemaphoreType.DMA((n,))],
    )(*arrs)
    return list(outs)


def _gather_two_level(arrs, name):
    n = len(arrs)

    def body(*refs):
        ins, outs = refs[:n], refs[n:2 * n]
        send_sems, recv_sems, local_sems = refs[2 * n:]
        x, y, c = lax.axis_index("x"), lax.axis_index("y"), lax.axis_index("c")
        sibling = (x, y, 1 - c)
        chips = [(1 - x, y), (x, 1 - y), (1 - x, 1 - y)]

        def slot(px, py, pc):
            return 4 * px + 2 * py + pc

        first, passed, waits, locs = [], [], [], []
        for i in range(n):
            def copy(k, block, to, src=None, i=i):
                dst = outs[i].at[slot(*block)]
                return pltpu.make_async_remote_copy(
                    src_ref=dst if src is None else src, dst_ref=dst, send_sem=send_sems.at[7 * i + k],
                    recv_sem=recv_sems.at[7 * i + k], device_id=to, device_id_type=MESH)

            locs.append(pltpu.make_async_copy(ins[i], outs[i].at[slot(x, y, c)], local_sems.at[i]))
            first.append(copy(0, (x, y, c), sibling, src=ins[i]))
            first += [copy(1 + j, (x, y, c), (*chip, c), src=ins[i]) for j, chip in enumerate(chips)]
            passed.append([copy(4 + j, (*chip, c), sibling) for j, chip in enumerate(chips)])
            waits.append([copy(1 + j, (*chip, c), (x, y, c)) for j, chip in enumerate(chips)])
        for cp in locs + first:
            cp.start()
        for j in range(3):
            for i in range(n):
                waits[i][j].wait_recv()
                passed[i][j].start()
        for i in range(n):
            def arrival(k, block, i=i):
                dst = outs[i].at[slot(*block)]
                return pltpu.make_async_remote_copy(
                    src_ref=dst, dst_ref=dst, send_sem=send_sems.at[7 * i + k], recv_sem=recv_sems.at[7 * i + k],
                    device_id=sibling, device_id_type=MESH)

            arrival(0, (x, y, 1 - c)).wait_recv()
            for j, chip in enumerate(chips):
                arrival(4 + j, (*chip, 1 - c)).wait_recv()
        for cp in first + [p for ps in passed for p in ps]:
            cp.wait_send()
        for cp in locs:
            cp.wait()

    outs = pl.pallas_call(
        body, name=name, out_shape=[_sds((N_DEV,) + a.shape, a.dtype) for a in arrs], in_specs=[ANY] * n,
        out_specs=[ANY] * n,
        scratch_shapes=[pltpu.SemaphoreType.DMA((7 * n,)), pltpu.SemaphoreType.DMA((7 * n,)),
                        pltpu.SemaphoreType.DMA((n,))],
    )(*arrs)
    return list(outs)


HBM = pl.BlockSpec(memory_space=pltpu.HBM)
SEM = pl.BlockSpec(memory_space=pltpu.SEMAPHORE)
EFFECT = pltpu.SideEffectType.DATAFLOW_SIDE_EFFECTING


def _peer(k, chips=False):
    x, y, c = lax.axis_index("x"), lax.axis_index("y"), lax.axis_index("c")
    bits = (k + 1) << 1 if chips else k + 1
    px, py, pc = x ^ ((bits >> 2) & 1), y ^ ((bits >> 1) & 1), c ^ (bits & 1)
    if chips:
        return (px, py, pc), 2 * px + py, 2 * x + y
    return (px, py, pc), 4 * px + 2 * py + pc, 4 * x + 2 * y + c


def _exchange_start(arrs, gather, name, after=None, chips=False):
    n = len(arrs)
    n_peer, n_slot = (3, 4) if chips else (N_DEV - 1, N_DEV)
    modes = [gather] * n if isinstance(gather, bool) else list(gather)
    lands = [pltpu.with_memory_space_constraint(lax.empty(((n_slot,) + a.shape) if g else a.shape, a.dtype), pltpu.HBM)
             for a, g in zip(arrs, modes)]
    srcs = [pltpu.with_memory_space_constraint(a, pltpu.HBM) for a in arrs]

    extra = [] if after is None else [after]

    def body(*refs):
        ins, zones = refs[:n], refs[n:2 * n]
        send_sems, recv_sems, local_sems = refs[2 * n + len(extra):2 * n + len(extra) + 3]
        token = refs[-1]
        for i in range(n):
            gather = modes[i]
            for k in range(n_peer):
                dev, peer, me = _peer(k, chips)
                sem = i * n_peer + k
                pltpu.make_async_remote_copy(
                    src_ref=ins[i] if gather else ins[i].at[peer], dst_ref=zones[i].at[me],
                    send_sem=send_sems.at[sem], recv_sem=recv_sems.at[sem], device_id=dev, device_id_type=MESH).start()
            _, _, me = _peer(0, chips)
            pltpu.make_async_copy(ins[i] if gather else ins[i].at[me], zones[i].at[me], local_sems.at[i]).start()
        token[...] = jnp.zeros_like(token)

    nsem = n * n_peer
    outs = pl.pallas_call(
        body, name=name,
        out_shape=[pltpu.SemaphoreType.DMA((nsem,)), pltpu.SemaphoreType.DMA((nsem,)), pltpu.SemaphoreType.DMA((n,))]
        + [pltpu.HBM(a.shape, a.dtype) for a in srcs] + [pltpu.HBM(z.shape, z.dtype) for z in lands]
        + [_sds((8, LANE))],
        in_specs=[HBM] * (2 * n) + [ANY] * len(extra),
        out_specs=[SEM, SEM, SEM] + [HBM] * (2 * n) + [pl.BlockSpec(memory_space=pltpu.VMEM)],
        input_output_aliases={i: 3 + i for i in range(2 * n)},
        compiler_params=pltpu.CompilerParams(has_side_effects=EFFECT),
    )(*srcs, *lands, *extra)
    return {"n": n, "gather": modes, "chips": chips, "sems": outs[:3], "srcs": outs[3:3 + n],
            "lands": outs[3 + n:3 + 2 * n], "token": outs[-1]}


def _exchange_wait(st, after, name):
    n, modes, chips = st["n"], st["gather"], st["chips"]
    n_peer = 3 if chips else N_DEV - 1

    def body(*refs):
        ins, zones = refs[:n], refs[n:2 * n]
        send_sems, recv_sems, local_sems = refs[2 * n:2 * n + 3]
        for i in range(n):
            gather = modes[i]
            for k in range(n_peer):
                dev, peer, me = _peer(k, chips)
                sem = i * n_peer + k
                src = ins[i] if gather else ins[i].at[peer]
                cp = pltpu.make_async_remote_copy(
                    src_ref=src, dst_ref=zones[i].at[peer], send_sem=send_sems.at[sem], recv_sem=recv_sems.at[sem],
                    device_id=dev, device_id_type=MESH)
                cp.wait_send()
                cp.wait_recv()
            _, _, me = _peer(0, chips)
            pltpu.make_async_copy(ins[i] if gather else ins[i].at[me], zones[i].at[me], local_sems.at[i]).wait()

    outs = pl.pallas_call(
        body, name=name,
        out_shape=[pltpu.HBM(a.shape, a.dtype) for a in st["srcs"]] + [pltpu.HBM(z.shape, z.dtype) for z in st["lands"]],
        in_specs=[HBM] * (2 * n) + [SEM, SEM, SEM, ANY], out_specs=[HBM] * (2 * n),
        input_output_aliases={i: i for i in range(2 * n)},
        compiler_params=pltpu.CompilerParams(has_side_effects=EFFECT),
    )(*st["srcs"], *st["lands"], *st["sems"], after)
    return list(outs[n:])


def _pair_reduce(arrs, name):
    n = len(arrs)

    def body(*refs):
        ins, outs, got, mine = refs[:n], refs[n:2 * n], refs[2 * n:3 * n], refs[3 * n:4 * n]
        send_sems, recv_sems, local_sems = refs[4 * n:]
        x, y, c = lax.axis_index("x"), lax.axis_index("y"), lax.axis_index("c")
        sends, locs = [], []
        for i in range(n):
            for q in range(4):
                sem = 4 * i + q
                sends.append(pltpu.make_async_remote_copy(
                    src_ref=ins[i].at[2 * q + 1 - c], dst_ref=got[i].at[q], send_sem=send_sems.at[sem],
                    recv_sem=recv_sems.at[sem], device_id=(x, y, 1 - c), device_id_type=MESH))
                locs.append(pltpu.make_async_copy(ins[i].at[2 * q + c], mine[i].at[q], local_sems.at[sem]))
        for cp in locs + sends:
            cp.start()
        for cp in sends:
            cp.wait_recv()
        for cp in locs:
            cp.wait()
        for i in range(n):
            outs[i][...] = (mine[i][...].astype(F32) + got[i][...].astype(F32)).astype(BF)
        for cp in sends:
            cp.wait_send()

    half = [(4,) + a.shape[1:] for a in arrs]
    outs = pl.pallas_call(
        body, name=name, out_shape=[_sds(h, BF) for h in half], in_specs=[ANY] * n,
        out_specs=[pl.BlockSpec(memory_space=pltpu.VMEM)] * n,
        scratch_shapes=[pltpu.VMEM(h, BF) for h in half] * 2
        + [pltpu.SemaphoreType.DMA((4 * n,)), pltpu.SemaphoreType.DMA((4 * n,)), pltpu.SemaphoreType.DMA((4 * n,))],
        compiler_params=_cp(),
    )(*arrs)
    return list(outs)


def _mod_fwd(crows, w_ada, b_blk):
    def body(c_ref, w_ref, b_ref, o_ref):
        o_ref[...] = mm(silu(c_ref[...]), w_ref[...]) + b_ref[...]

    return pl.pallas_call(body, name="mod_fwd", out_shape=_sds((24, 768)), compiler_params=_cp())(crows, w_ada, b_blk)


def _mod_bwd(crows, w_ada, dmod_blk, dmodc_blk):
    def body(c_ref, w_ref, d_ref, dc_ref, gw_ref, gc_ref, gb_ref):
        cr = c_ref[...]
        dc = dc_ref[0:1, :]
        for p in range(1, N_DEV):
            dc = dc + dc_ref[p:p + 1, :]
        row = lax.broadcasted_iota(jnp.int32, (24, 1), 0)
        gw_ref[...] = mm_tn(silu(cr), jnp.where(row == 16, dc, d_ref[...]))
        cc = cr[16:17, :]
        sg = jax.nn.sigmoid(cc)
        part = mm_nt(jnp.broadcast_to(dc, (8, 768)), w_ref[...])
        gc_ref[...] = part * (sg * (1.0 + cc * (1.0 - sg)))
        gb_ref[...] = jnp.broadcast_to(jnp.sum(d_ref[...], axis=0, keepdims=True) + dc, (8, 768))

    return pl.pallas_call(
        body, name="mod_bwd", out_shape=[_sds((D_MODEL, 768)), _sds((8, D_MODEL)), _sds((8, 768))],
        compiler_params=_cp())(crows, w_ada, dmod_blk, dmodc_blk)


def _tab_specs(tk):
    return [pl.BlockSpec((tk, LANE), lambda i, t: (t, 0))] * 2


def _k1_fwd(x, mod, g_attn, g_q, g_kv, ws, tabs, kv_all, is_ctx):
    b, l, _ = x.shape
    tk = CTX_LEN if is_ctx else TOK
    nt = l // tk
    n_f32 = 2 if is_ctx else 4

    def body(x_ref, mod_ref, ga_ref, gq_ref, gk_ref, wa_ref, wb_ref, wq_ref, wk_ref, cs_ref, sn_ref, *rest):
        outs = rest if is_ctx else rest[2:]
        res = k1_tile(x_ref[...], mod_ref[0:1, :], mod_ref[1:2, :], ga_ref[...], gq_ref[...], gk_ref[...],
                      (wa_ref[...], wb_ref[...], wq_ref[...], wk_ref[...]), (None,) * 4,
                      (cs_ref[...], sn_ref[...]), is_ctx)
        for o_ref, r in zip(outs, res):
            o_ref[...] = r.astype(o_ref.dtype)

    tok = lambda w, off=0: pl.BlockSpec((None, tk, w), lambda i, t: (i, t + off, 0))
    mod_spec = pl.BlockSpec((None, 8, D_MODEL), (lambda i, t: (0, 0, 0)) if is_ctx else (lambda i, t: (i, 0, 0)))
    kv_off = SEQ // tk if is_ctx else 0
    in_specs = ([tok(D_MODEL), mod_spec, _full((1, D_MODEL)), _full((1, 384)), _full((1, 256))]
                + [_full(s) for s in W_SHAPES] + _tab_specs(tk))
    args = [x, mod, g_attn, g_q, g_kv, *ws, *tabs]
    out_specs = [tok(512)] * n_f32 + ([] if is_ctx else [tok(1024)]) + [tok(1024, kv_off), tok(512, kv_off)]
    out_shape = ([_sds((b, l, 512))] * n_f32 + ([] if is_ctx else [_sds((b, l, 1024), BF)])
                 + [_sds((b, KV_LEN, 1024), BF), _sds((b, KV_LEN, 512), BF)])
    aliases = {}
    if not is_ctx:
        aliases = {len(args): n_f32 + 1, len(args) + 1: n_f32 + 2}
        in_specs += [ANY, ANY]
        args += list(kv_all)
    return pl.pallas_call(
        body, name="k1_fwd_ctx" if is_ctx else "k1_fwd", grid=(b, nt), in_specs=in_specs, out_specs=out_specs,
        out_shape=out_shape, input_output_aliases=aliases, compiler_params=_cp((ARB, ARB)),
    )(*args)


N_ACC = 7


def _k1_bwd(x, ctx, mod, mod_c, g_attn, g_q, g_kv, ws, tabs, cts, cts_c, dx_res):
    b, l, _ = x.shape
    tk = TOK_B
    nt = l // tk
    flat = [[a for group in c for a in group] for c in (cts, cts_c)]
    sizes = [[len(g) for g in c] for c in (cts, cts_c)]
    acc_shapes = W_SHAPES + [(1, D_MODEL), (1, 384), (1, 256)]

    def body(*refs):
        it = iter(refs)
        x_ref, c_ref, mod_ref, modc_ref, ga_ref, gq_ref, gk_ref = [next(it) for _ in range(7)]
        w_hbm = [next(it) for _ in range(4)]
        tab_refs = [next(it) for _ in range(2)]
        ct_refs = [[next(it) for _ in f] for f in flat]
        res_ref, gx_ref = next(it), next(it)
        out_hbm = [next(it) for _ in range(N_ACC)]
        dmod_ref, dmodc_ref = next(it), next(it)
        w_vmem = [next(it) for _ in range(4)]
        accs = [next(it) for _ in range(N_ACC)]
        sem = next(it)
        i, t = pl.program_id(0), pl.program_id(1)
        first = jnp.logical_and(i == 0, t == 0)

        @pl.when(first)
        def _():
            for src, dst in zip(w_hbm, w_vmem):
                pltpu.sync_copy(src, dst)
            for k in range(N_ACC):
                accs[k][...] = jnp.zeros(acc_shapes[k], F32)

        def tile(is_ctx):
            which = 1 if is_ctx else 0
            ct_vals, pos = [], 0
            for gsz in sizes[which]:
                v = ct_refs[which][pos][...].astype(F32)
                for r in ct_refs[which][pos + 1:pos + gsz]:
                    v = v + r[...]
                ct_vals.append(v)
                pos += gsz
            wv = tuple(r[...] for r in w_vmem)
            tv = tuple(r[...] for r in tab_refs)
            m_ref = modc_ref if is_ctx else mod_ref

            def f(xv, sh, sc, ga, gq, gk, *probes):
                return k1_tile(xv, sh, sc, ga, gq, gk, wv, probes, tv, is_ctx)

            probes = [jnp.zeros(s, F32) for s in W_SHAPES]
            xin = c_ref[...] if is_ctx else x_ref[...]
            _, vjp = jax.vjp(f, xin, m_ref[0:1, :], m_ref[1:2, :], ga_ref[...], gq_ref[...], gk_ref[...], *probes)
            dx, dsh, dsc, dga, dgq, dgk, dwa, dwb, dwq, dwk = vjp(tuple(ct_vals))
            for ref, val in zip(accs, (dwa, dwb, dwq, dwk, dga, dgq, dgk)):
                ref[...] += val
            return dx, dsh, dsc

        @pl.when(t == 0)
        def _():
            _, dsh, dsc = tile(True)
            _acc(dmodc_ref.at[0:1, :], dsh, i == 0)
            _acc(dmodc_ref.at[1:2, :], dsc, i == 0)

            @pl.when(i == 0)
            def _():
                dmodc_ref[2:8, :] = jnp.zeros((6, D_MODEL), F32)

        @pl.when(t > 0)
        def _():
            dx, dsh, dsc = tile(False)
            gx_ref[...] = dx + res_ref[...]
            _acc(dmod_ref.at[0:1, :], dsh, t == 1)
            _acc(dmod_ref.at[1:2, :], dsc, t == 1)

            @pl.when(t == 1)
            def _():
                dmod_ref[2:8, :] = jnp.zeros((6, D_MODEL), F32)

        @pl.when(jnp.logical_and(i == b - 1, t == nt))
        def _():
            for k in range(4):
                w_vmem[k][...] = accs[k][...].astype(BF)
            cps = [pltpu.make_async_copy(w_vmem[k] if k < 4 else accs[k], out_hbm[k], sem.at[k]) for k in range(N_ACC)]
            for cp in cps:
                cp.start()
            for cp in cps:
                cp.wait()

    lat = lambda w, off=0: pl.BlockSpec((None, tk, w), lambda i, t: (i, jnp.maximum(t - 1, 0) + off, 0))
    con = lambda w, off=0: pl.BlockSpec((None, tk, w), lambda i, t: (i, off, 0))
    mod_spec = pl.BlockSpec((None, 8, D_MODEL), lambda i, t: (i, 0, 0))
    modc_spec = pl.BlockSpec((None, 8, D_MODEL), lambda i, t: (0, 0, 0))
    tab_spec = pl.BlockSpec((tk, LANE), lambda i, t: (jnp.maximum(t - 1, 0), 0))
    in_specs = ([lat(D_MODEL), con(D_MODEL), mod_spec, modc_spec, _full((1, D_MODEL)), _full((1, 384)), _full((1, 256))]
                + [ANY] * 4 + [tab_spec] * 2)
    args = [x, ctx, mod, mod_c, g_attn, g_q, g_kv, *ws, *tabs]
    for a, off in flat[0]:
        in_specs.append(lat(a.shape[-1], off // tk))
        args.append(a)
    for a, off in flat[1]:
        in_specs.append(con(a.shape[-1], off // tk))
        args.append(a)
    in_specs.append(lat(D_MODEL))
    args.append(dx_res)
    out_shape = ([_sds((b, l, D_MODEL))] + [_sds(s, BF) for s in W_SHAPES] + [_sds(s) for s in acc_shapes[4:]]
                 + [_sds((b, 8, D_MODEL)), _sds((1, 8, D_MODEL))])
    out_specs = [lat(D_MODEL)] + [ANY] * N_ACC + [mod_spec, modc_spec]
    outs = pl.pallas_call(
        body, name="k1_bwd", grid=(b, nt + 1), in_specs=in_specs, out_specs=out_specs, out_shape=out_shape,
        scratch_shapes=[pltpu.VMEM(s, BF) for s in W_SHAPES] + [pltpu.VMEM(s, F32) for s in acc_shapes]
        + [pltpu.SemaphoreType.DMA((N_ACC,))],
        compiler_params=_cp((ARB, ARB)),
    )(*args)
    return outs[0], list(outs[1:1 + N_ACC]), outs[1 + N_ACC], outs[2 + N_ACC]


def _chunk_spec(rev):
    if rev:
        return pl.BlockSpec((None, RET_CHUNK, 512), lambda i, n: (i, N_CHUNK - 1 - n, 0))
    return pl.BlockSpec((None, RET_CHUNK, 512), lambda i, n: (i, n, 0))


def _state_spec(rev):
    if rev:
        return pl.BlockSpec((None, N_HEADS, None, LANE, LANE), lambda i, n: (i, 0, N_CHUNK - 1 - n, 0, 0))
    return pl.BlockSpec((None, N_HEADS, None, LANE, LANE), lambda i, n: (i, 0, n, 0, 0))


_CTX_SPEC = pl.BlockSpec((None, CTX_LEN, 512), lambda i, n: (i, 0, 0))
_DEC_SPEC = pl.BlockSpec((N_HEADS, 1, 1), lambda i, n: (0, 0, 0))


def _k2_fwd(rq, rk, rv, rkc, rvc, dec_f, dec_b):
    b = rq.shape[0]

    def body(qf, kf, vf, qb, kb, vb, kc, vc, df, db, of_ref, ob_ref, sf_out, sb_out, sf, sb):
        n = pl.program_id(1)
        for h, sl in enumerate(_HEAD_SL):
            lgf, lgb = log_sigmoid(df[h]), log_sigmoid(db[h])

            @pl.when(n == 0)
            def _():
                sf[h] = ctx_state(kc[:, sl], vc[:, sl], lgf, False)
                sb[h] = ctx_state(kc[:, sl], vc[:, sl], lgb, True)

            sf_out[h] = sf[h]
            sb_out[h] = sb[h]
            o, s = ret_chunk(qf[:, sl], kf[:, sl], vf[:, sl], sf[h], lgf, False)
            of_ref[:, sl] = o
            sf[h] = s
            o, s = ret_chunk(qb[:, sl], kb[:, sl], vb[:, sl], sb[h], lgb, True)
            ob_ref[:, sl] = o
            sb[h] = s

    l = rq.shape[1]
    return pl.pallas_call(
        body, name="k2_fwd", grid=(b, N_CHUNK),
        in_specs=[_chunk_spec(False)] * 3 + [_chunk_spec(True)] * 3 + [_CTX_SPEC, _CTX_SPEC, _DEC_SPEC, _DEC_SPEC],
        out_specs=[_chunk_spec(False), _chunk_spec(True), _state_spec(False), _state_spec(True)],
        out_shape=[_sds((b, l, 512)), _sds((b, l, 512)), _sds((b, N_HEADS, N_CHUNK, LANE, LANE)),
                   _sds((b, N_HEADS, N_CHUNK, LANE, LANE))],
        scratch_shapes=[pltpu.VMEM((N_HEADS, LANE, LANE), F32), pltpu.VMEM((N_HEADS, LANE, LANE), F32)],
        compiler_params=_cp((ARB, ARB)),
    )(rq, rk, rv, rq, rk, rv, rkc, rvc, dec_f, dec_b)


def _k2_bwd(rq, rk, rv, do, sf_prev, sb_prev, rkc, rvc, dec_f, dec_b):
    b, l, _ = rq.shape

    def body(qf, kf, vf, gf, spf, qb, kb, vb, gb, spb, kc, vc, df, db,
             dqf, dkf, dvf, dqb, dkb, dvb, dkc, dvc, ddf, ddb, dsf, dsb):
        n = pl.program_id(1)

        @pl.when(n == 0)
        def _():
            dsf[...] = jnp.zeros((N_HEADS, LANE, LANE), F32)
            dsb[...] = jnp.zeros((N_HEADS, LANE, LANE), F32)

        def one(h, sl, q, k, v, g, sp, dec, ds, dq, dk, dv, dd, rev):
            def f(qv, kv_, vv, sv, dcy):
                return ret_chunk(qv, kv_, vv, sv, log_sigmoid(dcy), rev)

            _, vjp = jax.vjp(f, q[:, sl], k[:, sl], v[:, sl], sp[h], dec[h])
            gq, gk, gv, gs, gd = vjp((g[:, sl], ds[h]))
            dq[:, sl] = gq
            dk[:, sl] = gk
            dv[:, sl] = gv
            ds[h] = gs
            _acc(dd.at[h], jnp.broadcast_to(gd, (8, LANE)), n == 0)

        for h, sl in enumerate(_HEAD_SL):
            one(h, sl, qf, kf, vf, gf, spf, df, dsf, dqf, dkf, dvf, ddf, False)
            one(h, sl, qb, kb, vb, gb, spb, db, dsb, dqb, dkb, dvb, ddb, True)

        @pl.when(n == N_CHUNK - 1)
        def _():
            def f(kcv, vcv, dcy, rev):
                return ctx_state(kcv, vcv, log_sigmoid(dcy), rev)

            for h, sl in enumerate(_HEAD_SL):
                _, vjp_f = jax.vjp(functools.partial(f, rev=False), kc[:, sl], vc[:, sl], df[h])
                gk_f, gv_f, gd_f = vjp_f(dsf[h])
                _, vjp_b = jax.vjp(functools.partial(f, rev=True), kc[:, sl], vc[:, sl], db[h])
                gk_b, gv_b, gd_b = vjp_b(dsb[h])
                dkc[:, sl] = gk_f + gk_b
                dvc[:, sl] = gv_f + gv_b
                ddf[h] += jnp.broadcast_to(gd_f, (8, LANE))
                ddb[h] += jnp.broadcast_to(gd_b, (8, LANE))

    dd_spec = pl.BlockSpec((None, N_HEADS, 8, LANE), lambda i, n: (i, 0, 0, 0))
    return pl.pallas_call(
        body, name="k2_bwd", grid=(b, N_CHUNK),
        in_specs=[_chunk_spec(True)] * 4 + [_state_spec(True)] + [_chunk_spec(False)] * 4 + [_state_spec(False)]
        + [_CTX_SPEC, _CTX_SPEC, _DEC_SPEC, _DEC_SPEC],
        out_specs=[_chunk_spec(True)] * 3 + [_chunk_spec(False)] * 3 + [_CTX_SPEC, _CTX_SPEC, dd_spec, dd_spec],
        out_shape=[_sds((b, l, 512))] * 6 + [_sds((b, CTX_LEN, 512))] * 2 + [_sds((b, N_HEADS, 8, LANE))] * 2,
        scratch_shapes=[pltpu.VMEM((N_HEADS, LANE, LANE), F32), pltpu.VMEM((N_HEADS, LANE, LANE), F32)],
        compiler_params=_cp((ARB, ARB)),
    )(rq, rk, rv, do, sf_prev, rq, rk, rv, do, sb_prev, rkc, rvc, dec_f, dec_b)


TQ = 1024
TQ_F = 512
QK_W = 2 * LANE
N_QP = 2
_Q_PARTS = [slice(i * TQ_F // N_QP, (i + 1) * TQ_F // N_QP) for i in range(N_QP)]


SM_SCALE = 1.0 / math.sqrt(192.0)


def _k3_specs(tq):
    qs = lambda w: pl.BlockSpec((None, tq, w), lambda i, h, t: (i, t, h))
    ks = lambda w: pl.BlockSpec((None, KV_LEN, w), lambda i, h, t: (i, 0, h))
    return qs, ks


def _k3_fwd(q, k, v):
    b, l, _ = q.shape

    def body(q_ref, k_ref, v_ref, o_ref, lse_ref):
        kv_, vv = k_ref[...], v_ref[...]
        for r in _Q_PARTS:
            s = _dot(q_ref[r, :], kv_, 1, 1) * SM_SCALE
            m = jnp.max(s, axis=-1, keepdims=True)
            e = jnp.exp(s - m)
            tot = jnp.sum(e, axis=-1, keepdims=True)
            o_ref[r, :] = _dot(e, vv, 1, 0) * (1.0 / tot)
            lse_ref[r, :] = jnp.broadcast_to(m + jnp.log(tot), (TQ_F // N_QP, LANE))

    qs, ks = _k3_specs(TQ_F)
    return pl.pallas_call(
        body, name="k3_fwd", grid=(b, N_HEADS, l // TQ_F), in_specs=[qs(QK_W), ks(QK_W), ks(LANE)],
        out_specs=[qs(LANE), qs(LANE)], out_shape=[_sds((b, l, N_HEADS * LANE))] * 2,
        compiler_params=_cp((ARB, ARB, ARB)),
    )(q, k, v)


def _k3_bwd(q, k, v, o, lse, dy, after):
    b, l, _ = q.shape

    def body(q_ref, k_ref, v_ref, o_ref, lse_ref, dy_ref, after_ref, dq_ref, dk_ref, dv_ref):
        t0 = pl.program_id(2) == 0
        kv_, vv = k_ref[...], v_ref[...]
        qv, dyv = q_ref[...], dy_ref[...]
        g = dyv.astype(BF)
        lse_col = jnp.max(lse_ref[...], axis=-1, keepdims=True)
        delta = jnp.sum(dyv * o_ref[...], axis=-1, keepdims=True)
        p = jnp.exp(_dot(qv, kv_, 1, 1) * SM_SCALE - lse_col)
        ds = (p * (_dot(g, vv, 1, 1) - delta) * SM_SCALE).astype(BF)
        _acc(dv_ref, _dot(p, g, 0, 0), t0)
        dq_ref[...] = _dot(ds, kv_, 1, 0)
        _acc(dk_ref, _dot(ds, qv, 0, 0), t0)

    qs, ks = _k3_specs(TQ)
    return pl.pallas_call(
        body, name="k3_bwd", grid=(b, N_HEADS, l // TQ),
        in_specs=[qs(QK_W), ks(QK_W), ks(LANE), qs(LANE), qs(LANE), qs(LANE), ANY],
        out_specs=[qs(QK_W), ks(QK_W), ks(LANE)],
        out_shape=[_sds((b, l, N_HEADS * QK_W)), _sds((b, KV_LEN, N_HEADS * QK_W)), _sds((b, KV_LEN, N_HEADS * LANE))],
        compiler_params=_cp((ARB, ARB, ARB)),
    )(q, k, v, o, lse, dy, after)


def _mod_rows(mod_ref, rows):
    return [mod_ref[r:r + 1, :] for r in rows]


def _k4a_fwd(x, o_f, o_b, rg, y_mla, g_ret, w_out, mod, g_ffn):
    b, l, _ = x.shape

    def body(x_ref, of_ref, ob_ref, rg_ref, ym_ref, gr_ref, wo_ref, mod_ref, gf_ref, xm_ref, h2_ref):
        gt_a, sh_f, sc_f = _mod_rows(mod_ref, (2, 3, 4))
        x_mid, h2 = k4a_tile(x_ref[...], of_ref[...], ob_ref[...], rg_ref[...], ym_ref[...], gr_ref[...], gt_a,
                             gf_ref[...], sh_f, sc_f, wo_ref[...], None)
        xm_ref[...] = x_mid
        h2_ref[...] = h2.astype(BF)

    tok = lambda w: pl.BlockSpec((None, TOK, w), lambda i, t: (i, t, 0))
    mod_spec = pl.BlockSpec((None, 8, D_MODEL), lambda i, t: (i, 0, 0))
    return pl.pallas_call(
        body, name="k4a_fwd", grid=(b, l // TOK),
        in_specs=[tok(D_MODEL), tok(512), tok(512), tok(512), tok(512), _full((1, 512)), _full((D_MODEL, D_MODEL)),
                  mod_spec, _full((1, D_MODEL))],
        out_specs=[tok(D_MODEL), tok(D_MODEL)], out_shape=[_sds((b, l, D_MODEL)), _sds((b, l, D_MODEL), BF)],
        compiler_params=_cp((ARB, ARB)),
    )(x, o_f, o_b, rg, y_mla, g_ret, w_out, mod, g_ffn)


TOK_M = 512
TOK_D = 2048
HALF_FF = D_FF // 2


def _k4b_mlp_loss(h2, w1t, w2, x_mid, mod, g_final, tgt):
    b, l, _ = h2.shape
    nt = l // TOK_M

    def body(h2_ref, w1_hbm, w2_hbm, xm_ref, mod_ref, gfin_ref, tgt_ref, dxm_ref, dmlp_ref, r_ref, loss_ref, dgt_ref,
             dgfin_ref, w1_v, w2_v):
        i, t = pl.program_id(0), pl.program_id(1)
        first = jnp.logical_and(i == 0, t == 0)

        @pl.when(first)
        def _():
            pltpu.sync_copy(w1_hbm, w1_v)
            pltpu.sync_copy(w2_hbm, w2_v)

        h2v = h2_ref[...]
        mlp = None
        for half in range(2):
            rows = slice(half * HALF_FF, (half + 1) * HALF_FF)
            r = jnp.maximum(_dot(h2v, w1_v[rows, :], 1, 1), 0.0)
            r_ref[:, rows] = r.astype(BF)
            part = _dot(jnp.square(r), w2_v[rows, :], 1, 0)
            mlp = part if mlp is None else mlp + part
        (gt_f,) = _mod_rows(mod_ref, (5,))
        loss, vjp = jax.vjp(k4c_tile, xm_ref[...], mlp, gt_f, gfin_ref[...], tgt_ref[...])
        dxm, dmlp, dgt, dgfin, _ = vjp(jnp.ones((1, 1), F32))
        dxm_ref[...] = dxm
        dmlp_ref[...] = dmlp.astype(BF)
        _acc(loss_ref, jnp.broadcast_to(loss, (8, LANE)), first)
        _acc(dgfin_ref, dgfin, first)
        _acc(dgt_ref, dgt, t == 0)

    tok = lambda w: pl.BlockSpec((None, TOK_M, w), lambda i, t: (i, t, 0))
    return pl.pallas_call(
        body, name="k4b_mlp_loss", grid=(b, nt),
        in_specs=[tok(D_MODEL), ANY, ANY, tok(D_MODEL), pl.BlockSpec((None, 8, D_MODEL), lambda i, t: (i, 0, 0)),
                  _full((1, D_MODEL)), tok(D_MODEL)],
        out_specs=[tok(D_MODEL), tok(D_MODEL), tok(D_FF), _full((8, LANE)),
                   pl.BlockSpec((None, 1, D_MODEL), lambda i, t: (i, 0, 0)), _full((1, D_MODEL))],
        out_shape=[_sds((b, l, D_MODEL)), _sds((b, l, D_MODEL), BF), _sds((b, l, D_FF), BF), _sds((8, LANE)),
                   _sds((b, 1, D_MODEL)), _sds((1, D_MODEL))],
        scratch_shapes=[pltpu.VMEM((D_FF, D_MODEL), BF), pltpu.VMEM((D_FF, D_MODEL), BF)],
        compiler_params=_cp((ARB, ARB)),
    )(h2, w1t, w2, x_mid, mod, g_final, tgt)


def _k4d_mlp_bwd(h2, dmlp, r, w2):
    b, l, _ = h2.shape
    nt = l // TOK_D

    def body(h2_ref, dm_ref, r_ref, w2_ref, da_ref, dw1_ref, dw2_ref, acc1, acc2):
        i, t = pl.program_id(1), pl.program_id(2)
        first = jnp.logical_and(i == 0, t == 0)
        rv = r_ref[...].astype(F32)
        dm = dm_ref[...]
        da = (_dot(dm, w2_ref[...], 1, 1) * (2.0 * rv)).astype(BF)
        da_ref[...] = da
        _acc(acc2, _dot(jnp.square(rv), dm, 0, 0), first)
        _acc(acc1, _dot(h2_ref[...], da, 0, 0), first)

        @pl.when(jnp.logical_and(i == b - 1, t == nt - 1))
        def _():
            dw1_ref[...] = acc1[...].astype(BF)
            dw2_ref[...] = acc2[...].astype(BF)

    tok = lambda w: pl.BlockSpec((None, TOK_D, w), lambda j, i, t: (i, t, 0))
    col = pl.BlockSpec((None, TOK_D, FF_BLK), lambda j, i, t: (i, t, j))
    return pl.pallas_call(
        body, name="k4d_mlp_bwd", grid=(N_DEV, b, nt),
        in_specs=[tok(D_MODEL), tok(D_MODEL), col, pl.BlockSpec((None, FF_BLK, D_MODEL), lambda j, i, t: (j, 0, 0))],
        out_specs=[col, pl.BlockSpec((None, D_MODEL, FF_BLK), lambda j, i, t: (j, 0, 0)),
                   pl.BlockSpec((None, FF_BLK, D_MODEL), lambda j, i, t: (j, 0, 0))],
        out_shape=[_sds((b, l, D_FF), BF), _sds((N_DEV, D_MODEL, FF_BLK), BF), _sds((N_DEV, FF_BLK, D_MODEL), BF)],
        scratch_shapes=[pltpu.VMEM((D_MODEL, FF_BLK), F32), pltpu.VMEM((FF_BLK, D_MODEL), F32)],
        compiler_params=_cp((ARB, ARB, ARB)),
    )(h2, dmlp, r, w2)


def _k4e_bwd(x, o_f, o_b, rg, y_mla, g_ret, w_out, mod, g_ffn, dxm, da, w1t):
    b, l, _ = x.shape

    def body(x_ref, of_ref, ob_ref, rg_ref, ym_ref, gr_ref, wo_ref, mod_ref, gf_ref, dxm_ref, da_ref, w1_hbm,
             dx_ref, do_ref, drg_ref, dym_ref, dwo_ref, dgr_ref, dgf_ref, dmod_ref, w1_v):
        i, t = pl.program_id(0), pl.program_id(1)
        first = jnp.logical_and(i == 0, t == 0)

        @pl.when(first)
        def _():
            pltpu.sync_copy(w1_hbm, w1_v)

        gt_a, sh_f, sc_f = _mod_rows(mod_ref, (2, 3, 4))
        wo = wo_ref[...]
        dh2 = _dot(da_ref[...], w1_v[...], 1, 0)

        def f(xv, ofv, rgv, ymv, grv, gta, gfv, shf, scf, p_out):
            return k4a_tile(xv, ofv, ob_ref[...], rgv, ymv, grv, gta, gfv, shf, scf, wo, p_out)

        _, vjp = jax.vjp(f, x_ref[...], of_ref[...], rg_ref[...], ym_ref[...], gr_ref[...], gt_a, gf_ref[...], sh_f,
                         sc_f, jnp.zeros((D_MODEL, D_MODEL), F32))
        dx, do, drg, dym, dgr, dgta, dgf, dshf, dscf, dwo = vjp((dxm_ref[...], dh2))
        dx_ref[...] = dx
        do_ref[...] = do
        drg_ref[...] = drg
        dym_ref[...] = dym
        _acc(dwo_ref, dwo, first)
        _acc(dgr_ref, dgr, first)
        _acc(dgf_ref, dgf, first)
        t0 = t == 0
        _acc(dmod_ref.at[2:3, :], dgta, t0)
        _acc(dmod_ref.at[3:4, :], dshf, t0)
        _acc(dmod_ref.at[4:5, :], dscf, t0)

        @pl.when(t0)
        def _():
            dmod_ref[0:2, :] = jnp.zeros((2, D_MODEL), F32)
            dmod_ref[5:8, :] = jnp.zeros((3, D_MODEL), F32)

    tok = lambda w: pl.BlockSpec((None, TOK_B, w), lambda i, t: (i, t, 0))
    mod_spec = pl.BlockSpec((None, 8, D_MODEL), lambda i, t: (i, 0, 0))
    return pl.pallas_call(
        body, name="k4e_bwd", grid=(b, l // TOK_B),
        in_specs=[tok(D_MODEL), tok(512), tok(512), tok(512), tok(512), _full((1, 512)), _full((D_MODEL, D_MODEL)),
                  mod_spec, _full((1, D_MODEL)), tok(D_MODEL), tok(D_FF), ANY],
        out_specs=[tok(D_MODEL), tok(512), tok(512), tok(512), _full((D_MODEL, D_MODEL)), _full((1, 512)),
                   _full((1, D_MODEL)), mod_spec],
        out_shape=[_sds((b, l, D_MODEL)), _sds((b, l, 512)), _sds((b, l, 512)), _sds((b, l, 512)),
                   _sds((D_MODEL, D_MODEL)), _sds((1, 512)), _sds((1, D_MODEL)), _sds((b, 8, D_MODEL))],
        scratch_shapes=[pltpu.VMEM((D_FF, D_MODEL), BF)],
        compiler_params=_cp((ARB, ARB)),
    )(x, o_f, o_b, rg, y_mla, g_ret, w_out, mod, g_ffn, dxm, da, w1t)


def _adamw(w, m, v, pieces, name, after=None):
    r, c = w.shape
    npc = pieces.shape[0]
    per_row = c * (7 * 4 + npc * pieces.dtype.itemsize) * 2
    rb = r
    for cand in (r, 512, 256, 128, 64, 32, 16, 8):
        if r % cand == 0 and cand * per_row <= 32 * 1024 * 1024:
            rb = cand
            break

    def body(w_ref, m_ref, v_ref, p_ref, *rest):
        g_ref, d_ref, nm_ref, nv_ref = rest[-4:]
        g = p_ref[0].astype(F32)
        for k in range(1, npc):
            g = g + p_ref[k].astype(F32)
        wv = w_ref[...]
        mn = ADAM_B1 * m_ref[...] + (1.0 - ADAM_B1) * g
        vn = ADAM_B2 * v_ref[...] + (1.0 - ADAM_B2) * jnp.square(g)
        m_hat = mn / (1.0 - ADAM_B1 ** ADAM_STEP)
        v_hat = vn / (1.0 - ADAM_B2 ** ADAM_STEP)
        g_ref[...] = g
        d_ref[...] = -ADAM_LR * (m_hat / (jnp.sqrt(v_hat) + ADAM_EPS) + ADAM_WD * wv)
        nm_ref[...] = mn
        nv_ref[...] = vn

    blk = pl.BlockSpec((rb, c), lambda i: (i, 0))
    extra = [] if after is None else [after]
    return pl.pallas_call(
        body, name=name, grid=(r // rb,),
        in_specs=[blk, blk, blk, pl.BlockSpec((npc, rb, c), lambda i: (0, i, 0))] + [ANY] * len(extra),
        out_specs=[blk] * 4, out_shape=[_sds((r, c))] * 4, compiler_params=_cp((ARB,)),
    )(w, m, v, pieces, *extra)


def _pad_rot_rows(w):
    k = w.shape[1]
    return jnp.pad(w.reshape(-1, 2, 32, k), ((0, 0), (0, 0), (0, 32), (0, 0))).reshape(-1, k)


def _cut_rot_rows(g):
    k = g.shape[1]
    return g.reshape(-1, 2, 64, k)[:, :, :32].reshape(-1, k)


def _w_in_pad(wt):
    w_a = jnp.concatenate([_pad_rot_rows(wt[0:512]), wt[512:1536]], axis=0)
    w_b = jnp.concatenate([wt[1536:2176], _pad_rot_rows(wt[2176:2240])], axis=0)
    return w_a, w_b


def _w_in_cut(g_a, g_b):
    return jnp.concatenate([_cut_rot_rows(g_a[0:1024]), g_a[1024:2048], g_b[0:640], _cut_rot_rows(g_b[640:768])], axis=0)


def _w_uq_pad(wt):
    w = wt.reshape(N_HEADS, 192, 384)
    rot = _pad_rot_rows(w[:, 128:].reshape(N_HEADS * 64, 384)).reshape(N_HEADS, LANE, 384)
    return jnp.concatenate([w[:, :128], rot], axis=1).reshape(1024, 384)


def _w_uq_cut(g):
    g = g.reshape(N_HEADS, 256, 384)
    rot = _cut_rot_rows(g[:, 128:].reshape(N_HEADS * LANE, 384)).reshape(N_HEADS, 64, 384)
    return jnp.concatenate([g[:, :128], rot], axis=1).reshape(768, 384)


def _w_ukv_perm(wt):
    return jnp.transpose(wt.reshape(N_HEADS, 2, LANE, 256), (1, 0, 2, 3)).reshape(1024, 256)


def _w_ukv_unperm(g):
    return jnp.transpose(g.reshape(2, N_HEADS, LANE, 256), (1, 0, 2, 3)).reshape(1024, 256)


def _unshard_cols(g):
    return jnp.transpose(g, (1, 0, 2)).reshape(g.shape[1], N_DEV * g.shape[2])


def _rope_tables():
    rows = SEQ // GRID_W
    row = jnp.repeat(jnp.arange(rows, dtype=F32), GRID_W)
    col = jnp.tile(jnp.arange(GRID_W, dtype=F32), rows)
    freq = ROPE_BASE ** (-jnp.arange(16, dtype=F32) / 16)
    ang = jnp.concatenate([row[:, None] * freq, col[:, None] * freq], axis=-1)
    cos, sin = jnp.cos(ang), jnp.sin(ang)
    z = jnp.zeros((SEQ, 32), F32)
    return jnp.concatenate([cos, z, cos, z], axis=1), jnp.concatenate([-sin, z, sin, z], axis=1)


_PACKED = (("g_attn", 1024), ("g_ffn", 1024), ("ret_decay_fwd", 4), ("ret_decay_bwd", 4), ("g_ret", 512),
           ("g_q_lora", 384), ("g_kv_lora", 256), ("g_final", 1024))
_PACK_OFF = {}
_off = 0
for _name, _n in _PACKED:
    _PACK_OFF[_name] = _off
    _off += -(-_n // LANE) * LANE
PACK_W = _off


def _pack_small(vals):
    parts = []
    for name, n in _PACKED:
        a = vals[name].reshape(-1).astype(F32)
        parts.append(jnp.pad(a, (0, -(-n // LANE) * LANE - n)))
    return jnp.concatenate(parts).reshape(1, PACK_W)


def _adamw_small(params, packed, gcc, gb_ada):
    names = list(params)
    n_p = len(names)

    def body(*refs):
        p_ref, gcc_ref, gb_ref = refs[3 * n_p:3 * n_p + 3]
        outs = refs[3 * n_p + 3:]
        for k, name in enumerate(names):
            w_ref, m_ref, v_ref = refs[3 * k:3 * k + 3]
            n = w_ref.shape[1]
            if name == "b_ada":
                g = jnp.concatenate([gb_ref[d, 0:1, :] for d in range(N_DEV)], axis=-1)
            elif name == "c_ctx":
                g = gcc_ref[0, 0:1, :]
                for d in range(1, N_DEV):
                    g = g + gcc_ref[d, 0:1, :]
            else:
                off = _PACK_OFF[name]
                g = p_ref[0, :, off:off + n]
                for d in range(1, N_DEV):
                    g = g + p_ref[d, :, off:off + n]
            mn = ADAM_B1 * m_ref[...] + (1.0 - ADAM_B1) * g
            vn = ADAM_B2 * v_ref[...] + (1.0 - ADAM_B2) * jnp.square(g)
            m_hat = mn / (1.0 - ADAM_B1 ** ADAM_STEP)
            v_hat = vn / (1.0 - ADAM_B2 ** ADAM_STEP)
            outs[4 * k][...] = g
            outs[4 * k + 1][...] = -ADAM_LR * (m_hat / (jnp.sqrt(v_hat) + ADAM_EPS) + ADAM_WD * w_ref[...])
            outs[4 * k + 2][...] = mn
            outs[4 * k + 3][...] = vn

    args = [a for name in names for a in params[name]] + [packed, gcc, gb_ada]
    out_shape = [_sds(params[name][0].shape) for name in names for _ in range(4)]
    outs = pl.pallas_call(body, name="adamw_small", out_shape=out_shape, compiler_params=_cp())(*args)
    return {name: list(outs[4 * k:4 * k + 4]) for k, name in enumerate(names)}


def kernel(x, c, ctx, c_ctx, w_ada, b_ada, g_attn, g_ffn, w_in, ret_decay_fwd, ret_decay_bwd, g_ret, g_q_lora, w_uq, g_kv_lora, w_ukv, w_out, w_ff1, w_ff2, g_final, loss_target, m_c_ctx, m_w_ada, m_b_ada, m_g_attn, m_g_ffn, m_w_in, m_ret_decay_fwd, m_ret_decay_bwd, m_g_ret, m_g_q_lora, m_w_uq, m_g_kv_lora, m_w_ukv, m_w_out, m_w_ff1, m_w_ff2, m_g_final, v_c_ctx, v_w_ada, v_b_ada, v_g_attn, v_g_ffn, v_w_in, v_ret_decay_fwd, v_ret_decay_bwd, v_g_ret, v_g_q_lora, v_w_uq, v_g_kv_lora, v_w_ukv, v_w_out, v_w_ff1, v_w_ff2, v_g_final):
    me = 4 * lax.axis_index("x") + 2 * lax.axis_index("y") + lax.axis_index("c")
    nb = x.shape[0]

    c_pad = jnp.pad(c, ((0, 8 - nb), (0, 0)))
    c_all, g_in, g_uq, g_ukv = _gather_two_level(
        [c_pad, w_in[0].T.astype(BF), w_uq[0].T.astype(BF), w_ukv[0].T.astype(BF)], "gather_weights")

    crows = jnp.concatenate([c_all[:, :nb].reshape(N_DEV * nb, D_MODEL), c_ctx[None], jnp.zeros((7, D_MODEL), F32)])
    b_blk = lax.dynamic_slice(b_ada, (0, me * 768), (1, 768))
    st_m = _exchange_start([_mod_fwd(crows, w_ada[0], b_blk)], True, "gather_mod_start")
    st_o = _exchange_start([w_out[0].astype(BF)], True, "gather_wo_start", after=st_m["token"])
    st_g = _exchange_start([w_ff1[0].T.astype(BF), w_ff2[0].astype(BF)], True, "gather_ff_start", after=st_o["token"])
    tok = st_g["token"][0:1, 0:1].astype(BF)
    ws = (*_w_in_pad(g_in.reshape(2240, D_MODEL) + tok), _w_uq_pad(g_uq.reshape(768, 384)),
          _w_ukv_perm(g_ukv.reshape(1024, 256)))
    (mod_g,) = _exchange_wait(st_m, ws[0], "gather_mod_wait")
    mod_all = _unshard_cols(mod_g)
    mod_mine = lax.dynamic_slice(mod_all, (me * nb, 0), (nb, 6 * D_MODEL)).reshape(nb, 6, D_MODEL)
    mod = jnp.pad(mod_mine, ((0, 0), (0, 2), (0, 0)))
    mod_c = jnp.pad(mod_all[16].reshape(1, 6, D_MODEL), ((0, 0), (0, 2), (0, 0)))

    tabs = _rope_tables()
    dec_f = ret_decay_fwd.reshape(N_HEADS, 1, 1)
    dec_b = ret_decay_bwd.reshape(N_HEADS, 1, 1)

    rkc, rvc, k_ctx, v_ctx = _k1_fwd(ctx, mod_c, g_attn, g_q_lora, g_kv_lora, ws, tabs, None, True)
    rq, rk, rv, rg, q, k_all, v_all = _k1_fwd(x, mod, g_attn, g_q_lora, g_kv_lora, ws, tabs, (k_ctx, v_ctx), False)
    o_f, o_b, sf_prev, sb_prev = _k2_fwd(rq, rk, rv, rkc, rvc, dec_f, dec_b)
    y_mla, lse = _k3_fwd(q, k_all, v_all)
    (g_out,) = _exchange_wait(st_o, y_mla, "gather_wo_wait")
    wo = g_out.reshape(D_MODEL, D_MODEL)
    x_mid, h2 = _k4a_fwd(x, o_f, o_b, rg, y_mla, g_ret, wo, mod, g_ffn)
    g_ff1t, g_ff2 = _exchange_wait(st_g, x_mid, "gather_ff_wait")
    w1t = g_ff1t.reshape(D_FF, D_MODEL)
    dxm, dmlp, relu_a, loss_acc, dgt_f, dg_final = _k4b_mlp_loss(h2, w1t, g_ff2.reshape(D_FF, D_MODEL), x_mid, mod,
                                                                 g_final.reshape(1, D_MODEL), loss_target)

    da, dw1, dw2 = _k4d_mlp_bwd(h2, dmlp, relu_a, g_ff2)
    st_s = _exchange_start([dw1, dw2], False, "scatter_ff_start")
    g_ret_t = g_ret + st_s["token"][0:1, 0:1]
    dx_res, do, drg, dym, dwo, dg_ret, dg_ffn, dmod_a = _k4e_bwd(x, o_f, o_b, rg, y_mla, g_ret_t, wo, mod, g_ffn, dxm, da,
                                                                 w1t)
    st_w = _exchange_start([dwo.reshape(N_DEV, 128, D_MODEL).astype(BF)], False, "scatter_wo_start")
    dq, dk_all, dv_all = _k3_bwd(q, k_all, v_all, y_mla, lse, dym, st_w["token"])
    dqf, dkf, dvf, dqb, dkb, dvb, dkc, dvc, ddf, ddb = _k2_bwd(rq, rk, rv, do, sf_prev, sb_prev, rkc, rvc, dec_f, dec_b)
    cts = [[(dqf, 0), (dqb, 0)], [(dkf, 0), (dkb, 0)], [(dvf, 0), (dvb, 0)], [(drg, 0)], [(dq, 0)],
           [(dk_all, 0)], [(dv_all, 0)]]
    cts_c = [[(dkc, 0)], [(dvc, 0)], [(dk_all, SEQ)], [(dv_all, SEQ)]]
    grad_x, accs, dmod_1, dmod_c1 = _k1_bwd(x, ctx, mod, mod_c, g_attn, g_q_lora, g_kv_lora, ws, tabs, cts, cts_c,
                                            dx_res)
    dwa, dwb, dwq, dwk, dg_attn, dg_q, dg_kv = accs

    dmod_loc = (dmod_a + dmod_1).at[:, 5, :].set(dgt_f[:, 0, :])[:, :6, :].reshape(nb, 6 * D_MODEL)
    dmod_ctx = dmod_c1[:, :6, :].reshape(1, 6 * D_MODEL)
    small = {"g_attn": dg_attn, "g_ffn": dg_ffn, "ret_decay_fwd": jnp.sum(ddf[:, :, 0, 0], axis=0),
             "ret_decay_bwd": jnp.sum(ddb[:, :, 0, 0], axis=0), "g_ret": dg_ret, "g_q_lora": dg_q, "g_kv_lora": dg_kv,
             "g_final": dg_final}
    extra = jnp.concatenate([dmod_loc, dmod_ctx, jnp.zeros((5, 6 * D_MODEL), F32)])
    ex_pieces = jnp.transpose(extra.reshape(8, N_DEV, 768), (1, 0, 2))
    st_sm = _exchange_start([_pack_small(small), ex_pieces, loss_acc], [True, False, True], "gather_small_start")
    tok = st_sm["token"][0:1, 0:1].astype(BF)
    chip_sums = _pair_reduce([(_w_in_cut(dwa, dwb) + tok).reshape(N_DEV, 280, D_MODEL),
                              _w_uq_cut(dwq).reshape(N_DEV, 96, 384), _w_ukv_unperm(dwk).reshape(N_DEV, 128, 256)],
                             "pair_reduce")
    sm_g, ex_g, loss_g = _exchange_wait(st_sm, chip_sums[0], "gather_small_wait")
    dmod_blk = jnp.concatenate([ex_g[:, :nb].reshape(N_DEV * nb, 768), jnp.zeros((8, 768), F32)])
    gw_ada, gcc_part, gb_part = _mod_bwd(crows, w_ada[0], dmod_blk, ex_g[:, nb])
    st_c = _exchange_start([gcc_part, gb_part], True, "gather_cc_start")
    p_ff1, p_ff2 = _exchange_wait(st_s, st_c["token"], "scatter_ff_wait")
    (p_wo,) = _exchange_wait(st_w, p_ff1, "scatter_wo_wait")
    st_r = _exchange_start(chip_sums, False, "scatter_rest_start", after=p_wo, chips=True)

    res = {}
    early = (("w_ff1", w_ff1, m_w_ff1, v_w_ff1, p_ff1), ("w_ff2", w_ff2, m_w_ff2, v_w_ff2, p_ff2),
             ("w_ada", w_ada, m_w_ada, v_w_ada, gw_ada[None]), ("w_out", w_out, m_w_out, v_w_out, p_wo))
    behind = st_r["token"]
    for name, w, m, v, pcs in early:
        res[name] = [a[None] for a in _adamw(w[0], m[0], v[0], pcs, "adamw_" + name, after=behind)]
        behind = res[name][3]

    smalls = {"c_ctx": (c_ctx, m_c_ctx, v_c_ctx), "b_ada": (b_ada, m_b_ada, v_b_ada), "g_attn": (g_attn, m_g_attn, v_g_attn),
              "g_ffn": (g_ffn, m_g_ffn, v_g_ffn), "ret_decay_fwd": (ret_decay_fwd, m_ret_decay_fwd, v_ret_decay_fwd),
              "ret_decay_bwd": (ret_decay_bwd, m_ret_decay_bwd, v_ret_decay_bwd), "g_ret": (g_ret, m_g_ret, v_g_ret),
              "g_q_lora": (g_q_lora, m_g_q_lora, v_g_q_lora), "g_kv_lora": (g_kv_lora, m_g_kv_lora, v_g_kv_lora),
              "g_final": (g_final, m_g_final, v_g_final)}
    rows = {k: tuple(a.reshape(1, -1) for a in t) for k, t in smalls.items()}
    gcc_g, gb_g = _exchange_wait(st_c, behind, "gather_cc_wait")
    small_out = _adamw_small(rows, sm_g, gcc_g, gb_g)
    for name, outs in small_out.items():
        res[name] = [o.reshape(smalls[name][0].shape) for o in outs]

    pieces = _exchange_wait(st_r, small_out["g_final"][3], "scatter_rest_wait")
    for name, w, m, v, pcs in (("w_in", w_in, m_w_in, v_w_in, pieces[0]), ("w_uq", w_uq, m_w_uq, v_w_uq, pieces[1])):
        res[name] = [a.T[None] for a in _adamw(w[0].T, m[0].T, v[0].T, pcs, "adamw_" + name)]
    res["w_ukv"] = [a[None] for a in _adamw(w_ukv[0], m_w_ukv[0], v_w_ukv[0], jnp.transpose(pieces[2], (0, 2, 1)),
                                            "adamw_w_ukv")]

    loss = loss_g[0, 0, 0]
    for k in range(1, N_DEV):
        loss = loss + loss_g[k, 0, 0]

    order = ("c_ctx", "w_ada", "b_ada", "g_attn", "g_ffn", "w_in", "ret_decay_fwd", "ret_decay_bwd", "g_ret", "g_q_lora",
             "w_uq", "g_kv_lora", "w_ukv", "w_out", "w_ff1", "w_ff2", "g_final")
    return (loss, grad_x, *[res[n][0] for n in order], *[res[n][1] for n in order], *[res[n][2] for n in order],
            *[res[n][3] for n in order])
```

```python
import functools
import math

import jax
import jax.numpy as jnp
from jax import lax
from jax.experimental import pallas as pl
from jax.experimental.pallas import tpu as pltpu

F32 = jnp.float32
BF = jnp.bfloat16
EPS = 1e-6
LANE = 128
N_DEV = 8
D_MODEL = 1024
SEQ = 2048
CTX_LEN = 256
GRID_W = 64
N_HEADS = 4
RET_CHUNK = 512
N_CHUNK = SEQ // RET_CHUNK
D_FF = 4096
FF_BLK = D_FF // N_DEV
IN_PAD = 2816
KV_LEN = CTX_LEN + SEQ
ROPE_BASE = 10000.0
ADAM_LR, ADAM_B1, ADAM_B2, ADAM_EPS, ADAM_WD, ADAM_STEP = 0.001, 0.9, 0.999, 1e-08, 0.01, 10
TOK = 512
TOK_B = 256
VMEM_LIMIT = 56 * 1024 * 1024
ARB = "arbitrary"
MESH = pl.DeviceIdType.MESH
_HEAD_SL = [slice(LANE * h, LANE * (h + 1)) for h in range(N_HEADS)]
W_SHAPES = [(2048, D_MODEL), (768, D_MODEL), (1024, 384), (1024, 256)]


def _dot(a, b, ca, cb):
    return lax.dot_general(a.astype(BF), b.astype(BF), (((ca,), (cb,)), ((), ())), preferred_element_type=F32)


@jax.custom_vjp
def mm(a, b):
    return _dot(a, b, 1, 0)


@jax.custom_vjp
def mm_nt(a, b):
    return _dot(a, b, 1, 1)


@jax.custom_vjp
def mm_tn(a, b):
    return _dot(a, b, 0, 0)


mm.defvjp(lambda a, b: (_dot(a, b, 1, 0), (a, b)), lambda r, g: (mm_nt(g, r[1]), mm_tn(r[0], g)))
mm_nt.defvjp(lambda a, b: (_dot(a, b, 1, 1), (a, b)), lambda r, g: (mm(g, r[1]), mm_tn(g, r[0])))
mm_tn.defvjp(lambda a, b: (_dot(a, b, 0, 0), (a, b)), lambda r, g: (mm_nt(r[1], g), mm(r[0], g)))


@jax.custom_vjp
def _mmw(a, w, probe):
    return _dot(a, w, 1, 0)


def _mmw_bwd(r, g):
    a, w = r
    return mm_nt(g, w), jnp.zeros_like(w), mm_tn(a, g)


_mmw.defvjp(lambda a, w, probe: (_dot(a, w, 1, 0), (a, w)), _mmw_bwd)


@jax.custom_vjp
def _mmwt(a, wt, probe):
    return _dot(a, wt, 1, 1)


_mmwt.defvjp(lambda a, wt, probe: (_dot(a, wt, 1, 1), (a, wt)),
             lambda r, g: (mm(g, r[1]), jnp.zeros_like(r[1]), mm_tn(g, r[0])))


def mmwt(a, wt, probe):
    return _dot(a, wt, 1, 1) if probe is None else _mmwt(a, wt, probe)


def mmw(a, w, probe):
    return _dot(a, w, 1, 0) if probe is None else _mmw(a, w, probe)


def rmsn(x, g):
    return x * lax.rsqrt(jnp.mean(x * x, axis=-1, keepdims=True) + EPS) * g


def silu(x):
    return x * jax.nn.sigmoid(x)


def _swap_halves_impl(x):
    return pltpu.roll(x, 64, 1)


@jax.custom_vjp
def swap_halves(x):
    return _swap_halves_impl(x)


swap_halves.defvjp(lambda x: (_swap_halves_impl(x), None), lambda _, g: (_swap_halves_impl(g),))


def rope(x, cs1, sn1, every=1):
    blocks = []
    for i in range(x.shape[-1] // LANE):
        xb = x[:, LANE * i:LANE * (i + 1)]
        blocks.append(xb * cs1 + swap_halves(xb) * sn1 if i % every == every - 1 else xb)
    return blocks[0] if len(blocks) == 1 else jnp.concatenate(blocks, axis=-1)


def k1_tile(x, sh, sc, g_attn, g_q, g_kv, ws, ps, tabs, is_ctx):
    w_a, w_b, w_uq, w_ukv = ws
    p_a, p_b, p_uq, p_ukv = ps
    cs1, sn1 = tabs
    h = rmsn(x, g_attn) * (1.0 + sc) + sh
    pa = mmwt(h, w_a, p_a)
    pb = mmwt(h, w_b, p_b)
    rk = pa[:, 512:1024] * 0.125
    rv = pa[:, 1024:1536]
    kpe = pb[:, 640:768]
    kv = mmwt(rmsn(pb[:, 384:640], g_kv), w_ukv, p_ukv)
    if not is_ctx:
        rk = rope(rk, cs1, sn1)
        kpe = rope(kpe, cs1, sn1)
    k_full = jnp.concatenate([piece for sl in _HEAD_SL for piece in (kv[:, sl], kpe)], axis=-1)
    v = kv[:, 512:]
    if is_ctx:
        return rk, rv, k_full, v
    rq = rope(pa[:, 0:512], cs1, sn1)
    rg = pa[:, 1536:2048]
    q = rope(mmwt(rmsn(pb[:, 0:384], g_q), w_uq, p_uq), cs1, sn1, every=2)
    return rq, rk, rv, rg, q, k_full, v


def log_sigmoid(x):
    return jnp.minimum(x, 0.0) - jnp.log(1.0 + jnp.exp(-jnp.abs(x)))


def ret_chunk(q, k, v, s, lg, reverse):
    c = RET_CHUNK
    ii = lax.broadcasted_iota(jnp.int32, (c, c), 0).astype(F32)
    jj = lax.broadcasted_iota(jnp.int32, (c, c), 1).astype(F32)
    diff = (jj - ii) if reverse else (ii - jj)
    dec = jnp.where(diff >= 0, jnp.exp(lg * jnp.maximum(diff, 0.0)), 0.0)
    pos = lax.broadcasted_iota(jnp.int32, (c, 1), 0).astype(F32)
    if reverse:
        wk, wq = jnp.exp(lg * pos), jnp.exp(lg * (c - pos))
    else:
        wk, wq = jnp.exp(lg * (c - 1.0 - pos)), jnp.exp(lg * (pos + 1.0))
    o = mm(mm_nt(q, k) * dec, v) + mm(q * wq, s)
    s_next = jnp.exp(lg * float(c)) * s + mm_tn(k * wk, v)
    return o, s_next


def ctx_state(kc, vc, lg, reverse):
    n = kc.shape[0]
    pos = lax.broadcasted_iota(jnp.int32, (n, 1), 0).astype(F32)
    w = jnp.exp(lg * pos) if reverse else jnp.exp(lg * (n - 1.0 - pos))
    return mm_tn(kc * w, vc)


def attn_head(qn, qp, kn, kp, v):
    s = (mm_nt(qn, kn) + mm_nt(qp, kp)) * (1.0 / math.sqrt(192.0))
    e = jnp.exp(s - jnp.max(s, axis=-1, keepdims=True))
    return mm(e / jnp.sum(e, axis=-1, keepdims=True), v)


def gn_gate(o, rg, g_ret):
    ys = []
    for h in range(N_HEADS):
        sl = slice(LANE * h, LANE * (h + 1))
        oh = o[:, sl]
        mu = jnp.mean(oh, axis=-1, keepdims=True)
        var = jnp.mean(jnp.square(oh - mu), axis=-1, keepdims=True)
        ys.append((oh - mu) * lax.rsqrt(var + EPS) * g_ret[:, sl])
    return jnp.concatenate(ys, axis=-1) * silu(rg)


def k4a_tile(x, o_f, o_b, rg, y_mla, g_ret, gt_a, g_ffn, sh_f, sc_f, w_out, p_out):
    mix = jnp.concatenate([gn_gate(o_f + o_b, rg, g_ret), y_mla], axis=-1)
    x_mid = x + gt_a * mmw(mix, w_out, p_out)
    h2 = rmsn(x_mid, g_ffn) * (1.0 + sc_f) + sh_f
    return x_mid, h2


def k4c_tile(x_mid, mlp, gt_f, g_final, tgt):
    y = rmsn(x_mid + gt_f * mlp, g_final)
    per_tok = jnp.mean(jnp.square(y - tgt), axis=-1, keepdims=True)
    return 0.5 * jnp.sum(per_tok, axis=0, keepdims=True)


def _cp(sem=None, vmem=VMEM_LIMIT):
    return pltpu.CompilerParams(dimension_semantics=sem, vmem_limit_bytes=vmem)


def _acc(ref, val, first):
    @pl.when(first)
    def _():
        ref[...] = val

    @pl.when(jnp.logical_not(first))
    def _():
        ref[...] += val


def _full(shape):
    nd = len(shape)
    return pl.BlockSpec(shape, lambda *_: (0,) * nd)


ANY = pl.BlockSpec(memory_space=pl.ANY)


def _sds(shape, dtype=F32):
    return jax.ShapeDtypeStruct(shape, dtype)


def _exchange(arrs, gather, name):
    n = len(arrs)
    modes = [gather] * n if isinstance(gather, bool) else list(gather)
    out_shape = [_sds(((N_DEV,) + a.shape) if g else a.shape, a.dtype) for a, g in zip(arrs, modes)]

    def body(*refs):
        ins, outs = refs[:n], refs[n:2 * n]
        send_sems, recv_sems, local_sems = refs[2 * n:]
        x, y, c = lax.axis_index("x"), lax.axis_index("y"), lax.axis_index("c")
        me = 4 * x + 2 * y + c
        sends, recvs, locs = [], [], []
        for i in range(n):
            gather = modes[i]
            for k in range(N_DEV - 1):
                bits = k + 1
                px = x ^ ((bits >> 2) & 1)
                py = y ^ ((bits >> 1) & 1)
                pc = c ^ (bits & 1)
                peer = 4 * px + 2 * py + pc
                src = ins[i] if gather else ins[i].at[peer]
                sem = i * (N_DEV - 1) + k
                sends.append(pltpu.make_async_remote_copy(
                    src_ref=src, dst_ref=outs[i].at[me], send_sem=send_sems.at[sem], recv_sem=recv_sems.at[sem],
                    device_id=(px, py, pc), device_id_type=MESH))
                recvs.append(pltpu.make_async_remote_copy(
                    src_ref=src, dst_ref=outs[i].at[peer], send_sem=send_sems.at[sem], recv_sem=recv_sems.at[sem],
                    device_id=(px, py, pc), device_id_type=MESH))
            locs.append(pltpu.make_async_copy(ins[i] if gather else ins[i].at[me], outs[i].at[me], local_sems.at[i]))
        for cp in locs + sends:
            cp.start()
        for cp in recvs:
            cp.wait_recv()
        for cp in sends:
            cp.wait_send()
        for cp in locs:
            cp.wait()

    outs = pl.pallas_call(
        body, name=name, out_shape=out_shape, in_specs=[ANY] * n, out_specs=[ANY] * n,
        scratch_shapes=[pltpu.SemaphoreType.DMA((n * (N_DEV - 1),)), pltpu.SemaphoreType.DMA((n * (N_DEV - 1),)),
                        pltpu.SemaphoreType.DMA((n,))],
    )(*arrs)
    return list(outs)


def _gather_two_level(arrs, name):
    n = len(arrs)

    def body(*refs):
        ins, outs = refs[:n], refs[n:2 * n]
        send_sems, recv_sems, local_sems = refs[2 * n:]
        x, y, c = lax.axis_index("x"), lax.axis_index("y"), lax.axis_index("c")
        sibling = (x, y, 1 - c)
        chips = [(1 - x, y), (x, 1 - y), (1 - x, 1 - y)]

        def slot(px, py, pc):
            return 4 * px + 2 * py + pc

        first, passed, waits, locs = [], [], [], []
        for i in range(n):
            def copy(k, block, to, src=None, i=i):
                dst = outs[i].at[slot(*block)]
                return pltpu.make_async_remote_copy(
                    src_ref=dst if src is None else src, dst_ref=dst, send_sem=send_sems.at[7 * i + k],
                    recv_sem=recv_sems.at[7 * i + k], device_id=to, device_id_type=MESH)

            locs.append(pltpu.make_async_copy(ins[i], outs[i].at[slot(x, y, c)], local_sems.at[i]))
            first.append(copy(0, (x, y, c), sibling, src=ins[i]))
            first += [copy(1 + j, (x, y, c), (*chip, c), src=ins[i]) for j, chip in enumerate(chips)]
            passed.append([copy(4 + j, (*chip, c), sibling) for j, chip in enumerate(chips)])
            waits.append([copy(1 + j, (*chip, c), (x, y, c)) for j, chip in enumerate(chips)])
        for cp in locs + first:
            cp.start()
        for j in range(3):
            for i in range(n):
                waits[i][j].wait_recv()
                passed[i][j].start()
        for i in range(n):
            def arrival(k, block, i=i):
                dst = outs[i].at[slot(*block)]
                return pltpu.make_async_remote_copy(
                    src_ref=dst, dst_ref=dst, send_sem=send_sems.at[7 * i + k], recv_sem=recv_sems.at[7 * i + k],
                    device_id=sibling, device_id_type=MESH)

            arrival(0, (x, y, 1 - c)).wait_recv()
            for j, chip in enumerate(chips):
                arrival(4 + j, (*chip, 1 - c)).wait_recv()
        for cp in first + [p for ps in passed for p in ps]:
            cp.wait_send()
        for cp in locs:
            cp.wait()

    outs = pl.pallas_call(
        body, name=name, out_shape=[_sds((N_DEV,) + a.shape, a.dtype) for a in arrs], in_specs=[ANY] * n,
        out_specs=[ANY] * n,
        scratch_shapes=[pltpu.SemaphoreType.DMA((7 * n,)), pltpu.SemaphoreType.DMA((7 * n,)),
                        pltpu.SemaphoreType.DMA((n,))],
    )(*arrs)
    return list(outs)


HBM = pl.BlockSpec(memory_space=pltpu.HBM)
SEM = pl.BlockSpec(memory_space=pltpu.SEMAPHORE)
EFFECT = pltpu.SideEffectType.DATAFLOW_SIDE_EFFECTING


def _peer(k, chips=False):
    x, y, c = lax.axis_index("x"), lax.axis_index("y"), lax.axis_index("c")
    bits = (k + 1) << 1 if chips else k + 1
    px, py, pc = x ^ ((bits >> 2) & 1), y ^ ((bits >> 1) & 1), c ^ (bits & 1)
    if chips:
        return (px, py, pc), 2 * px + py, 2 * x + y
    return (px, py, pc), 4 * px + 2 * py + pc, 4 * x + 2 * y + c


def _exchange_start(arrs, gather, name, after=None, chips=False):
    n = len(arrs)
    n_peer, n_slot = (3, 4) if chips else (N_DEV - 1, N_DEV)
    modes = [gather] * n if isinstance(gather, bool) else list(gather)
    lands = [pltpu.with_memory_space_constraint(lax.empty(((n_slot,) + a.shape) if g else a.shape, a.dtype), pltpu.HBM)
             for a, g in zip(arrs, modes)]
    srcs = [pltpu.with_memory_space_constraint(a, pltpu.HBM) for a in arrs]

    extra = [] if after is None else [after]

    def body(*refs):
        ins, zones = refs[:n], refs[n:2 * n]
        send_sems, recv_sems, local_sems = refs[2 * n + len(extra):2 * n + len(extra) + 3]
        token = refs[-1]
        for i in range(n):
            gather = modes[i]
            for k in range(n_peer):
                dev, peer, me = _peer(k, chips)
                sem = i * n_peer + k
                pltpu.make_async_remote_copy(
                    src_ref=ins[i] if gather else ins[i].at[peer], dst_ref=zones[i].at[me],
                    send_sem=send_sems.at[sem], recv_sem=recv_sems.at[sem], device_id=dev, device_id_type=MESH).start()
            _, _, me = _peer(0, chips)
            pltpu.make_async_copy(ins[i] if gather else ins[i].at[me], zones[i].at[me], local_sems.at[i]).start()
        token[...] = jnp.zeros_like(token)

    nsem = n * n_peer
    outs = pl.pallas_call(
        body, name=name,
        out_shape=[pltpu.SemaphoreType.DMA((nsem,)), pltpu.SemaphoreType.DMA((nsem,)), pltpu.SemaphoreType.DMA((n,))]
        + [pltpu.HBM(a.shape, a.dtype) for a in srcs] + [pltpu.HBM(z.shape, z.dtype) for z in lands]
        + [_sds((8, LANE))],
        in_specs=[HBM] * (2 * n) + [ANY] * len(extra),
        out_specs=[SEM, SEM, SEM] + [HBM] * (2 * n) + [pl.BlockSpec(memory_space=pltpu.VMEM)],
        input_output_aliases={i: 3 + i for i in range(2 * n)},
        compiler_params=pltpu.CompilerParams(has_side_effects=EFFECT),
    )(*srcs, *lands, *extra)
    return {"n": n, "gather": modes, "chips": chips, "sems": outs[:3], "srcs": outs[3:3 + n],
            "lands": outs[3 + n:3 + 2 * n], "token": outs[-1]}


def _exchange_wait(st, after, name, which=None):
    modes, chips = st["gather"], st["chips"]
    which = list(range(st["n"])) if which is None else which
    n = len(which)
    n_peer = 3 if chips else N_DEV - 1
    srcs, lands = [st["srcs"][i] for i in which], [st["lands"][i] for i in which]

    def body(*refs):
        ins, zones = refs[:n], refs[n:2 * n]
        send_sems, recv_sems, local_sems = refs[2 * n:2 * n + 3]
        for j, i in enumerate(which):
            gather = modes[i]
            for k in range(n_peer):
                dev, peer, me = _peer(k, chips)
                sem = i * n_peer + k
                src = ins[j] if gather else ins[j].at[peer]
                cp = pltpu.make_async_remote_copy(
                    src_ref=src, dst_ref=zones[j].at[peer], send_sem=send_sems.at[sem], recv_sem=recv_sems.at[sem],
                    device_id=dev, device_id_type=MESH)
                cp.wait_send()
                cp.wait_recv()
            _, _, me = _peer(0, chips)
            pltpu.make_async_copy(ins[j] if gather else ins[j].at[me], zones[j].at[me], local_sems.at[i]).wait()

    outs = pl.pallas_call(
        body, name=name,
        out_shape=[pltpu.HBM(a.shape, a.dtype) for a in srcs] + [pltpu.HBM(z.shape, z.dtype) for z in lands],
        in_specs=[HBM] * (2 * n) + [SEM, SEM, SEM, ANY], out_specs=[HBM] * (2 * n),
        input_output_aliases={i: i for i in range(2 * n)},
        compiler_params=pltpu.CompilerParams(has_side_effects=EFFECT),
    )(*srcs, *lands, *st["sems"], after)
    return list(outs[n:])


def _pair_reduce(arrs, name):
    n = len(arrs)

    def body(*refs):
        ins, outs, got, mine = refs[:n], refs[n:2 * n], refs[2 * n:3 * n], refs[3 * n:4 * n]
        send_sems, recv_sems, local_sems = refs[4 * n:]
        x, y, c = lax.axis_index("x"), lax.axis_index("y"), lax.axis_index("c")
        sends, locs = [], []
        for i in range(n):
            for q in range(4):
                sem = 4 * i + q
                sends.append(pltpu.make_async_remote_copy(
                    src_ref=ins[i].at[2 * q + 1 - c], dst_ref=got[i].at[q], send_sem=send_sems.at[sem],
                    recv_sem=recv_sems.at[sem], device_id=(x, y, 1 - c), device_id_type=MESH))
                locs.append(pltpu.make_async_copy(ins[i].at[2 * q + c], mine[i].at[q], local_sems.at[sem]))
        for cp in locs + sends:
            cp.start()
        for cp in sends:
            cp.wait_recv()
        for cp in locs:
            cp.wait()
        for i in range(n):
            outs[i][...] = (mine[i][...].astype(F32) + got[i][...].astype(F32)).astype(BF)
        for cp in sends:
            cp.wait_send()

    half = [(4,) + a.shape[1:] for a in arrs]
    outs = pl.pallas_call(
        body, name=name, out_shape=[_sds(h, BF) for h in half], in_specs=[ANY] * n,
        out_specs=[pl.BlockSpec(memory_space=pltpu.VMEM)] * n,
        scratch_shapes=[pltpu.VMEM(h, BF) for h in half] * 2
        + [pltpu.SemaphoreType.DMA((4 * n,)), pltpu.SemaphoreType.DMA((4 * n,)), pltpu.SemaphoreType.DMA((4 * n,))],
        compiler_params=_cp(),
    )(*arrs)
    return list(outs)


def _mod_fwd(crows, w_ada, b_blk):
    def body(c_ref, w_ref, b_ref, o_ref):
        o_ref[...] = mm(silu(c_ref[...]), w_ref[...]) + b_ref[...]

    return pl.pallas_call(body, name="mod_fwd", out_shape=_sds((24, 768)), compiler_params=_cp())(crows, w_ada, b_blk)


def _mod_bwd(crows, w_ada, dmod_blk, dmodc_blk):
    def body(c_ref, w_ref, d_ref, dc_ref, gw_ref, gc_ref, gb_ref):
        cr = c_ref[...]
        dc = dc_ref[0:1, :]
        for p in range(1, N_DEV):
            dc = dc + dc_ref[p:p + 1, :]
        row = lax.broadcasted_iota(jnp.int32, (24, 1), 0)
        gw_ref[...] = mm_tn(silu(cr), jnp.where(row == 16, dc, d_ref[...]))
        cc = cr[16:17, :]
        sg = jax.nn.sigmoid(cc)
        part = mm_nt(jnp.broadcast_to(dc, (8, 768)), w_ref[...])
        gc_ref[...] = part * (sg * (1.0 + cc * (1.0 - sg)))
        gb_ref[...] = jnp.broadcast_to(jnp.sum(d_ref[...], axis=0, keepdims=True) + dc, (8, 768))

    return pl.pallas_call(
        body, name="mod_bwd", out_shape=[_sds((D_MODEL, 768)), _sds((8, D_MODEL)), _sds((8, 768))],
        compiler_params=_cp())(crows, w_ada, dmod_blk, dmodc_blk)


def _tab_specs(tk):
    return [pl.BlockSpec((tk, LANE), lambda i, t: (t, 0))] * 2


def _k1_fwd(x, mod, g_attn, g_q, g_kv, ws, tabs, kv_all, is_ctx):
    b, l, _ = x.shape
    tk = CTX_LEN if is_ctx else TOK
    nt = l // tk
    n_f32 = 2 if is_ctx else 4

    def body(x_ref, mod_ref, ga_ref, gq_ref, gk_ref, wa_ref, wb_ref, wq_ref, wk_ref, cs_ref, sn_ref, *rest):
        outs = rest if is_ctx else rest[2:]
        res = k1_tile(x_ref[...], mod_ref[0:1, :], mod_ref[1:2, :], ga_ref[...], gq_ref[...], gk_ref[...],
                      (wa_ref[...], wb_ref[...], wq_ref[...], wk_ref[...]), (None,) * 4,
                      (cs_ref[...], sn_ref[...]), is_ctx)
        for o_ref, r in zip(outs, res):
            o_ref[...] = r.astype(o_ref.dtype)

    tok = lambda w, off=0: pl.BlockSpec((None, tk, w), lambda i, t: (i, t + off, 0))
    mod_spec = pl.BlockSpec((None, 8, D_MODEL), (lambda i, t: (0, 0, 0)) if is_ctx else (lambda i, t: (i, 0, 0)))
    kv_off = SEQ // tk if is_ctx else 0
    in_specs = ([tok(D_MODEL), mod_spec, _full((1, D_MODEL)), _full((1, 384)), _full((1, 256))]
                + [_full(s) for s in W_SHAPES] + _tab_specs(tk))
    args = [x, mod, g_attn, g_q, g_kv, *ws, *tabs]
    out_specs = [tok(512)] * n_f32 + ([] if is_ctx else [tok(1024)]) + [tok(1024, kv_off), tok(512, kv_off)]
    out_shape = ([_sds((b, l, 512))] * n_f32 + ([] if is_ctx else [_sds((b, l, 1024), BF)])
                 + [_sds((b, KV_LEN, 1024), BF), _sds((b, KV_LEN, 512), BF)])
    aliases = {}
    if not is_ctx:
        aliases = {len(args): n_f32 + 1, len(args) + 1: n_f32 + 2}
        in_specs += [ANY, ANY]
        args += list(kv_all)
    return pl.pallas_call(
        body, name="k1_fwd_ctx" if is_ctx else "k1_fwd", grid=(b, nt), in_specs=in_specs, out_specs=out_specs,
        out_shape=out_shape, input_output_aliases=aliases, compiler_params=_cp((ARB, ARB)),
    )(*args)


N_ACC = 7


def _k1_bwd(x, ctx, mod, mod_c, g_attn, g_q, g_kv, ws, tabs, cts, cts_c, dx_res):
    b, l, _ = x.shape
    tk = TOK_B
    nt = l // tk
    flat = [[a for group in c for a in group] for c in (cts, cts_c)]
    sizes = [[len(g) for g in c] for c in (cts, cts_c)]
    acc_shapes = W_SHAPES + [(1, D_MODEL), (1, 384), (1, 256)]

    def body(*refs):
        it = iter(refs)
        x_ref, c_ref, mod_ref, modc_ref, ga_ref, gq_ref, gk_ref = [next(it) for _ in range(7)]
        w_hbm = [next(it) for _ in range(4)]
        tab_refs = [next(it) for _ in range(2)]
        ct_refs = [[next(it) for _ in f] for f in flat]
        res_ref, gx_ref = next(it), next(it)
        out_hbm = [next(it) for _ in range(N_ACC)]
        dmod_ref, dmodc_ref = next(it), next(it)
        w_vmem = [next(it) for _ in range(4)]
        accs = [next(it) for _ in range(N_ACC)]
        sem = next(it)
        i, t = pl.program_id(0), pl.program_id(1)
        first = jnp.logical_and(i == 0, t == 0)

        @pl.when(first)
        def _():
            for src, dst in zip(w_hbm, w_vmem):
                pltpu.sync_copy(src, dst)
            for k in range(N_ACC):
                accs[k][...] = jnp.zeros(acc_shapes[k], F32)

        def tile(is_ctx):
            which = 1 if is_ctx else 0
            ct_vals, pos = [], 0
            for gsz in sizes[which]:
                v = ct_refs[which][pos][...].astype(F32)
                for r in ct_refs[which][pos + 1:pos + gsz]:
                    v = v + r[...]
                ct_vals.append(v)
                pos += gsz
            wv = tuple(r[...] for r in w_vmem)
            tv = tuple(r[...] for r in tab_refs)
            m_ref = modc_ref if is_ctx else mod_ref

            def f(xv, sh, sc, ga, gq, gk, *probes):
                return k1_tile(xv, sh, sc, ga, gq, gk, wv, probes, tv, is_ctx)

            probes = [jnp.zeros(s, F32) for s in W_SHAPES]
            xin = c_ref[...] if is_ctx else x_ref[...]
            _, vjp = jax.vjp(f, xin, m_ref[0:1, :], m_ref[1:2, :], ga_ref[...], gq_ref[...], gk_ref[...], *probes)
            dx, dsh, dsc, dga, dgq, dgk, dwa, dwb, dwq, dwk = vjp(tuple(ct_vals))
            for ref, val in zip(accs, (dwa, dwb, dwq, dwk, dga, dgq, dgk)):
                ref[...] += val
            return dx, dsh, dsc

        @pl.when(t == 0)
        def _():
            _, dsh, dsc = tile(True)
            _acc(dmodc_ref.at[0:1, :], dsh, i == 0)
            _acc(dmodc_ref.at[1:2, :], dsc, i == 0)

            @pl.when(i == 0)
            def _():
                dmodc_ref[2:8, :] = jnp.zeros((6, D_MODEL), F32)

        @pl.when(t > 0)
        def _():
            dx, dsh, dsc = tile(False)
            gx_ref[...] = dx + res_ref[...]
            _acc(dmod_ref.at[0:1, :], dsh, t == 1)
            _acc(dmod_ref.at[1:2, :], dsc, t == 1)

            @pl.when(t == 1)
            def _():
                dmod_ref[2:8, :] = jnp.zeros((6, D_MODEL), F32)

        @pl.when(jnp.logical_and(i == b - 1, t == nt))
        def _():
            for k in range(4):
                w_vmem[k][...] = accs[k][...].astype(BF)
            cps = [pltpu.make_async_copy(w_vmem[k] if k < 4 else accs[k], out_hbm[k], sem.at[k]) for k in range(N_ACC)]
            for cp in cps:
                cp.start()
            for cp in cps:
                cp.wait()

    lat = lambda w, off=0: pl.BlockSpec((None, tk, w), lambda i, t: (i, jnp.maximum(t - 1, 0) + off, 0))
    con = lambda w, off=0: pl.BlockSpec((None, tk, w), lambda i, t: (i, off, 0))
    mod_spec = pl.BlockSpec((None, 8, D_MODEL), lambda i, t: (i, 0, 0))
    modc_spec = pl.BlockSpec((None, 8, D_MODEL), lambda i, t: (0, 0, 0))
    tab_spec = pl.BlockSpec((tk, LANE), lambda i, t: (jnp.maximum(t - 1, 0), 0))
    in_specs = ([lat(D_MODEL), con(D_MODEL), mod_spec, modc_spec, _full((1, D_MODEL)), _full((1, 384)), _full((1, 256))]
                + [ANY] * 4 + [tab_spec] * 2)
    args = [x, ctx, mod, mod_c, g_attn, g_q, g_kv, *ws, *tabs]
    for a, off in flat[0]:
        in_specs.append(lat(a.shape[-1], off // tk))
        args.append(a)
    for a, off in flat[1]:
        in_specs.append(con(a.shape[-1], off // tk))
        args.append(a)
    in_specs.append(lat(D_MODEL))
    args.append(dx_res)
    out_shape = ([_sds((b, l, D_MODEL))] + [_sds(s, BF) for s in W_SHAPES] + [_sds(s) for s in acc_shapes[4:]]
                 + [_sds((b, 8, D_MODEL)), _sds((1, 8, D_MODEL))])
    out_specs = [lat(D_MODEL)] + [ANY] * N_ACC + [mod_spec, modc_spec]
    outs = pl.pallas_call(
        body, name="k1_bwd", grid=(b, nt + 1), in_specs=in_specs, out_specs=out_specs, out_shape=out_shape,
        scratch_shapes=[pltpu.VMEM(s, BF) for s in W_SHAPES] + [pltpu.VMEM(s, F32) for s in acc_shapes]
        + [pltpu.SemaphoreType.DMA((N_ACC,))],
        compiler_params=_cp((ARB, ARB)),
    )(*args)
    return outs[0], list(outs[1:1 + N_ACC]), outs[1 + N_ACC], outs[2 + N_ACC]


def _chunk_spec(rev):
    if rev:
        return pl.BlockSpec((None, RET_CHUNK, 512), lambda i, n: (i, N_CHUNK - 1 - n, 0))
    return pl.BlockSpec((None, RET_CHUNK, 512), lambda i, n: (i, n, 0))


def _state_spec(rev):
    if rev:
        return pl.BlockSpec((None, N_HEADS, None, LANE, LANE), lambda i, n: (i, 0, N_CHUNK - 1 - n, 0, 0))
    return pl.BlockSpec((None, N_HEADS, None, LANE, LANE), lambda i, n: (i, 0, n, 0, 0))


_CTX_SPEC = pl.BlockSpec((None, CTX_LEN, 512), lambda i, n: (i, 0, 0))
_DEC_SPEC = pl.BlockSpec((N_HEADS, 1, 1), lambda i, n: (0, 0, 0))


def _k2_fwd(rq, rk, rv, rkc, rvc, dec_f, dec_b):
    b = rq.shape[0]

    def body(qf, kf, vf, qb, kb, vb, kc, vc, df, db, of_ref, ob_ref, sf_out, sb_out, sf, sb):
        n = pl.program_id(1)
        for h, sl in enumerate(_HEAD_SL):
            lgf, lgb = log_sigmoid(df[h]), log_sigmoid(db[h])

            @pl.when(n == 0)
            def _():
                sf[h] = ctx_state(kc[:, sl], vc[:, sl], lgf, False)
                sb[h] = ctx_state(kc[:, sl], vc[:, sl], lgb, True)

            sf_out[h] = sf[h]
            sb_out[h] = sb[h]
            o, s = ret_chunk(qf[:, sl], kf[:, sl], vf[:, sl], sf[h], lgf, False)
            of_ref[:, sl] = o
            sf[h] = s
            o, s = ret_chunk(qb[:, sl], kb[:, sl], vb[:, sl], sb[h], lgb, True)
            ob_ref[:, sl] = o
            sb[h] = s

    l = rq.shape[1]
    return pl.pallas_call(
        body, name="k2_fwd", grid=(b, N_CHUNK),
        in_specs=[_chunk_spec(False)] * 3 + [_chunk_spec(True)] * 3 + [_CTX_SPEC, _CTX_SPEC, _DEC_SPEC, _DEC_SPEC],
        out_specs=[_chunk_spec(False), _chunk_spec(True), _state_spec(False), _state_spec(True)],
        out_shape=[_sds((b, l, 512)), _sds((b, l, 512)), _sds((b, N_HEADS, N_CHUNK, LANE, LANE)),
                   _sds((b, N_HEADS, N_CHUNK, LANE, LANE))],
        scratch_shapes=[pltpu.VMEM((N_HEADS, LANE, LANE), F32), pltpu.VMEM((N_HEADS, LANE, LANE), F32)],
        compiler_params=_cp((ARB, ARB)),
    )(rq, rk, rv, rq, rk, rv, rkc, rvc, dec_f, dec_b)


def _k2_bwd(rq, rk, rv, do, sf_prev, sb_prev, rkc, rvc, dec_f, dec_b):
    b, l, _ = rq.shape

    def body(qf, kf, vf, gf, spf, qb, kb, vb, gb, spb, kc, vc, df, db,
             dqf, dkf, dvf, dqb, dkb, dvb, dkc, dvc, ddf, ddb, dsf, dsb):
        n = pl.program_id(1)

        @pl.when(n == 0)
        def _():
            dsf[...] = jnp.zeros((N_HEADS, LANE, LANE), F32)
            dsb[...] = jnp.zeros((N_HEADS, LANE, LANE), F32)

        def one(h, sl, q, k, v, g, sp, dec, ds, dq, dk, dv, dd, rev):
            def f(qv, kv_, vv, sv, dcy):
                return ret_chunk(qv, kv_, vv, sv, log_sigmoid(dcy), rev)

            _, vjp = jax.vjp(f, q[:, sl], k[:, sl], v[:, sl], sp[h], dec[h])
            gq, gk, gv, gs, gd = vjp((g[:, sl], ds[h]))
            dq[:, sl] = gq
            dk[:, sl] = gk
            dv[:, sl] = gv
            ds[h] = gs
            _acc(dd.at[h], jnp.broadcast_to(gd, (8, LANE)), n == 0)

        for h, sl in enumerate(_HEAD_SL):
            one(h, sl, qf, kf, vf, gf, spf, df, dsf, dqf, dkf, dvf, ddf, False)
            one(h, sl, qb, kb, vb, gb, spb, db, dsb, dqb, dkb, dvb, ddb, True)

        @pl.when(n == N_CHUNK - 1)
        def _():
            def f(kcv, vcv, dcy, rev):
                return ctx_state(kcv, vcv, log_sigmoid(dcy), rev)

            for h, sl in enumerate(_HEAD_SL):
                _, vjp_f = jax.vjp(functools.partial(f, rev=False), kc[:, sl], vc[:, sl], df[h])
                gk_f, gv_f, gd_f = vjp_f(dsf[h])
                _, vjp_b = jax.vjp(functools.partial(f, rev=True), kc[:, sl], vc[:, sl], db[h])
                gk_b, gv_b, gd_b = vjp_b(dsb[h])
                dkc[:, sl] = gk_f + gk_b
                dvc[:, sl] = gv_f + gv_b
                ddf[h] += jnp.broadcast_to(gd_f, (8, LANE))
                ddb[h] += jnp.broadcast_to(gd_b, (8, LANE))

    dd_spec = pl.BlockSpec((None, N_HEADS, 8, LANE), lambda i, n: (i, 0, 0, 0))
    return pl.pallas_call(
        body, name="k2_bwd", grid=(b, N_CHUNK),
        in_specs=[_chunk_spec(True)] * 4 + [_state_spec(True)] + [_chunk_spec(False)] * 4 + [_state_spec(False)]
        + [_CTX_SPEC, _CTX_SPEC, _DEC_SPEC, _DEC_SPEC],
        out_specs=[_chunk_spec(True)] * 3 + [_chunk_spec(False)] * 3 + [_CTX_SPEC, _CTX_SPEC, dd_spec, dd_spec],
        out_shape=[_sds((b, l, 512))] * 6 + [_sds((b, CTX_LEN, 512))] * 2 + [_sds((b, N_HEADS, 8, LANE))] * 2,
        scratch_shapes=[pltpu.VMEM((N_HEADS, LANE, LANE), F32), pltpu.VMEM((N_HEADS, LANE, LANE), F32)],
        compiler_params=_cp((ARB, ARB)),
    )(rq, rk, rv, do, sf_prev, rq, rk, rv, do, sb_prev, rkc, rvc, dec_f, dec_b)


TQ = 1024
TQ_F = 512
QK_W = 2 * LANE
N_QP = 2
_Q_PARTS = [slice(i * TQ_F // N_QP, (i + 1) * TQ_F // N_QP) for i in range(N_QP)]


SM_SCALE = 1.0 / math.sqrt(192.0)


def _k3_specs(tq):
    qs = lambda w: pl.BlockSpec((None, tq, w), lambda i, h, t: (i, t, h))
    ks = lambda w: pl.BlockSpec((None, KV_LEN, w), lambda i, h, t: (i, 0, h))
    return qs, ks


def _k3_fwd(q, k, v):
    b, l, _ = q.shape

    def body(q_ref, k_ref, v_ref, o_ref, lse_ref):
        kv_, vv = k_ref[...], v_ref[...]
        for r in _Q_PARTS:
            s = _dot(q_ref[r, :], kv_, 1, 1) * SM_SCALE
            m = jnp.max(s, axis=-1, keepdims=True)
            e = jnp.exp(s - m)
            tot = jnp.sum(e, axis=-1, keepdims=True)
            o_ref[r, :] = _dot(e, vv, 1, 0) * (1.0 / tot)
            lse_ref[r, :] = jnp.broadcast_to(m + jnp.log(tot), (TQ_F // N_QP, LANE))

    qs, ks = _k3_specs(TQ_F)
    return pl.pallas_call(
        body, name="k3_fwd", grid=(b, N_HEADS, l // TQ_F), in_specs=[qs(QK_W), ks(QK_W), ks(LANE)],
        out_specs=[qs(LANE), qs(LANE)], out_shape=[_sds((b, l, N_HEADS * LANE))] * 2,
        compiler_params=_cp((ARB, ARB, ARB)),
    )(q, k, v)


def _k3_bwd(q, k, v, o, lse, dy, after):
    b, l, _ = q.shape

    def body(q_ref, k_ref, v_ref, o_ref, lse_ref, dy_ref, after_ref, dq_ref, dk_ref, dv_ref):
        t0 = pl.program_id(2) == 0
        kv_, vv = k_ref[...], v_ref[...]
        qv, dyv = q_ref[...], dy_ref[...]
        g = dyv.astype(BF)
        lse_col = jnp.max(lse_ref[...], axis=-1, keepdims=True)
        delta = jnp.sum(dyv * o_ref[...], axis=-1, keepdims=True)
        p = jnp.exp(_dot(qv, kv_, 1, 1) * SM_SCALE - lse_col)
        ds = (p * (_dot(g, vv, 1, 1) - delta) * SM_SCALE).astype(BF)
        _acc(dv_ref, _dot(p, g, 0, 0), t0)
        dq_ref[...] = _dot(ds, kv_, 1, 0)
        _acc(dk_ref, _dot(ds, qv, 0, 0), t0)

    qs, ks = _k3_specs(TQ)
    return pl.pallas_call(
        body, name="k3_bwd", grid=(b, N_HEADS, l // TQ),
        in_specs=[qs(QK_W), ks(QK_W), ks(LANE), qs(LANE), qs(LANE), qs(LANE), ANY],
        out_specs=[qs(QK_W), ks(QK_W), ks(LANE)],
        out_shape=[_sds((b, l, N_HEADS * QK_W)), _sds((b, KV_LEN, N_HEADS * QK_W)), _sds((b, KV_LEN, N_HEADS * LANE))],
        compiler_params=_cp((ARB, ARB, ARB)),
    )(q, k, v, o, lse, dy, after)


def _mod_rows(mod_ref, rows):
    return [mod_ref[r:r + 1, :] for r in rows]


def _k4a_fwd(x, o_f, o_b, rg, y_mla, g_ret, w_out, mod, g_ffn):
    b, l, _ = x.shape

    def body(x_ref, of_ref, ob_ref, rg_ref, ym_ref, gr_ref, wo_ref, mod_ref, gf_ref, xm_ref, h2_ref):
        gt_a, sh_f, sc_f = _mod_rows(mod_ref, (2, 3, 4))
        x_mid, h2 = k4a_tile(x_ref[...], of_ref[...], ob_ref[...], rg_ref[...], ym_ref[...], gr_ref[...], gt_a,
                             gf_ref[...], sh_f, sc_f, wo_ref[...], None)
        xm_ref[...] = x_mid
        h2_ref[...] = h2.astype(BF)

    tok = lambda w: pl.BlockSpec((None, TOK, w), lambda i, t: (i, t, 0))
    mod_spec = pl.BlockSpec((None, 8, D_MODEL), lambda i, t: (i, 0, 0))
    return pl.pallas_call(
        body, name="k4a_fwd", grid=(b, l // TOK),
        in_specs=[tok(D_MODEL), tok(512), tok(512), tok(512), tok(512), _full((1, 512)), _full((D_MODEL, D_MODEL)),
                  mod_spec, _full((1, D_MODEL))],
        out_specs=[tok(D_MODEL), tok(D_MODEL)], out_shape=[_sds((b, l, D_MODEL)), _sds((b, l, D_MODEL), BF)],
        compiler_params=_cp((ARB, ARB)),
    )(x, o_f, o_b, rg, y_mla, g_ret, w_out, mod, g_ffn)


TOK_M = 512
TOK_D = 2048
HALF_FF = D_FF // 2


def _k4b_mlp_loss(h2, w1t, w2, x_mid, mod, g_final, tgt):
    b, l, _ = h2.shape
    nt = l // TOK_M

    def body(h2_ref, w1_hbm, w2_hbm, xm_ref, mod_ref, gfin_ref, tgt_ref, dxm_ref, dmlp_ref, r_ref, loss_ref, dgt_ref,
             dgfin_ref, w1_v, w2_v):
        i, t = pl.program_id(0), pl.program_id(1)
        first = jnp.logical_and(i == 0, t == 0)

        @pl.when(first)
        def _():
            pltpu.sync_copy(w1_hbm, w1_v)
            pltpu.sync_copy(w2_hbm, w2_v)

        h2v = h2_ref[...]
        mlp = None
        for half in range(2):
            rows = slice(half * HALF_FF, (half + 1) * HALF_FF)
            r = jnp.maximum(_dot(h2v, w1_v[rows, :], 1, 1), 0.0)
            r_ref[:, rows] = r.astype(BF)
            part = _dot(jnp.square(r), w2_v[rows, :], 1, 0)
            mlp = part if mlp is None else mlp + part
        (gt_f,) = _mod_rows(mod_ref, (5,))
        loss, vjp = jax.vjp(k4c_tile, xm_ref[...], mlp, gt_f, gfin_ref[...], tgt_ref[...])
        dxm, dmlp, dgt, dgfin, _ = vjp(jnp.ones((1, 1), F32))
        dxm_ref[...] = dxm
        dmlp_ref[...] = dmlp.astype(BF)
        _acc(loss_ref, jnp.broadcast_to(loss, (8, LANE)), first)
        _acc(dgfin_ref, dgfin, first)
        _acc(dgt_ref, dgt, t == 0)

    tok = lambda w: pl.BlockSpec((None, TOK_M, w), lambda i, t: (i, t, 0))
    return pl.pallas_call(
        body, name="k4b_mlp_loss", grid=(b, nt),
        in_specs=[tok(D_MODEL), ANY, ANY, tok(D_MODEL), pl.BlockSpec((None, 8, D_MODEL), lambda i, t: (i, 0, 0)),
                  _full((1, D_MODEL)), tok(D_MODEL)],
        out_specs=[tok(D_MODEL), tok(D_MODEL), tok(D_FF), _full((8, LANE)),
                   pl.BlockSpec((None, 1, D_MODEL), lambda i, t: (i, 0, 0)), _full((1, D_MODEL))],
        out_shape=[_sds((b, l, D_MODEL)), _sds((b, l, D_MODEL), BF), _sds((b, l, D_FF), BF), _sds((8, LANE)),
                   _sds((b, 1, D_MODEL)), _sds((1, D_MODEL))],
        scratch_shapes=[pltpu.VMEM((D_FF, D_MODEL), BF), pltpu.VMEM((D_FF, D_MODEL), BF)],
        compiler_params=_cp((ARB, ARB)),
    )(h2, w1t, w2, x_mid, mod, g_final, tgt)


def _k4d_mlp_bwd(h2, dmlp, r, w2):
    b, l, _ = h2.shape
    nt = l // TOK_D

    def body(h2_ref, dm_ref, r_ref, w2_ref, da_ref, dw1_ref, dw2_ref, acc1, acc2):
        i, t = pl.program_id(1), pl.program_id(2)
        first = jnp.logical_and(i == 0, t == 0)
        rv = r_ref[...].astype(F32)
        dm = dm_ref[...]
        da = (_dot(dm, w2_ref[...], 1, 1) * (2.0 * rv)).astype(BF)
        da_ref[...] = da
        _acc(acc2, _dot(jnp.square(rv), dm, 0, 0), first)
        _acc(acc1, _dot(h2_ref[...], da, 0, 0), first)

        @pl.when(jnp.logical_and(i == b - 1, t == nt - 1))
        def _():
            dw1_ref[...] = acc1[...].astype(BF)
            dw2_ref[...] = acc2[...].astype(BF)

    tok = lambda w: pl.BlockSpec((None, TOK_D, w), lambda j, i, t: (i, t, 0))
    col = pl.BlockSpec((None, TOK_D, FF_BLK), lambda j, i, t: (i, t, j))
    return pl.pallas_call(
        body, name="k4d_mlp_bwd", grid=(N_DEV, b, nt),
        in_specs=[tok(D_MODEL), tok(D_MODEL), col, pl.BlockSpec((None, FF_BLK, D_MODEL), lambda j, i, t: (j, 0, 0))],
        out_specs=[col, pl.BlockSpec((None, D_MODEL, FF_BLK), lambda j, i, t: (j, 0, 0)),
                   pl.BlockSpec((None, FF_BLK, D_MODEL), lambda j, i, t: (j, 0, 0))],
        out_shape=[_sds((b, l, D_FF), BF), _sds((N_DEV, D_MODEL, FF_BLK), BF), _sds((N_DEV, FF_BLK, D_MODEL), BF)],
        scratch_shapes=[pltpu.VMEM((D_MODEL, FF_BLK), F32), pltpu.VMEM((FF_BLK, D_MODEL), F32)],
        compiler_params=_cp((ARB, ARB, ARB)),
    )(h2, dmlp, r, w2)


def _k4e_bwd(x, o_f, o_b, rg, y_mla, g_ret, w_out, mod, g_ffn, dxm, da, w1t):
    b, l, _ = x.shape

    def body(x_ref, of_ref, ob_ref, rg_ref, ym_ref, gr_ref, wo_ref, mod_ref, gf_ref, dxm_ref, da_ref, w1_hbm,
             dx_ref, do_ref, drg_ref, dym_ref, dwo_ref, dgr_ref, dgf_ref, dmod_ref, w1_v, dwo_acc):
        i, t = pl.program_id(0), pl.program_id(1)
        first = jnp.logical_and(i == 0, t == 0)

        @pl.when(first)
        def _():
            pltpu.sync_copy(w1_hbm, w1_v)

        gt_a, sh_f, sc_f = _mod_rows(mod_ref, (2, 3, 4))
        wo = wo_ref[...]
        dh2 = _dot(da_ref[...], w1_v[...], 1, 0)

        def f(xv, ofv, rgv, ymv, grv, gta, gfv, shf, scf, p_out):
            return k4a_tile(xv, ofv, ob_ref[...], rgv, ymv, grv, gta, gfv, shf, scf, wo, p_out)

        _, vjp = jax.vjp(f, x_ref[...], of_ref[...], rg_ref[...], ym_ref[...], gr_ref[...], gt_a, gf_ref[...], sh_f,
                         sc_f, jnp.zeros((D_MODEL, D_MODEL), F32))
        dx, do, drg, dym, dgr, dgta, dgf, dshf, dscf, dwo = vjp((dxm_ref[...], dh2))
        dx_ref[...] = dx
        do_ref[...] = do
        drg_ref[...] = drg
        dym_ref[...] = dym
        _acc(dwo_acc, dwo, first)
        _acc(dgr_ref, dgr, first)
        _acc(dgf_ref, dgf, first)
        t0 = t == 0
        _acc(dmod_ref.at[2:3, :], dgta, t0)
        _acc(dmod_ref.at[3:4, :], dshf, t0)
        _acc(dmod_ref.at[4:5, :], dscf, t0)

        @pl.when(t0)
        def _():
            dmod_ref[0:2, :] = jnp.zeros((2, D_MODEL), F32)
            dmod_ref[5:8, :] = jnp.zeros((3, D_MODEL), F32)

        @pl.when(jnp.logical_and(i == b - 1, t == l // TOK_B - 1))
        def _():
            dwo_ref[...] = dwo_acc[...].astype(BF)

    tok = lambda w: pl.BlockSpec((None, TOK_B, w), lambda i, t: (i, t, 0))
    mod_spec = pl.BlockSpec((None, 8, D_MODEL), lambda i, t: (i, 0, 0))
    return pl.pallas_call(
        body, name="k4e_bwd", grid=(b, l // TOK_B),
        in_specs=[tok(D_MODEL), tok(512), tok(512), tok(512), tok(512), _full((1, 512)), _full((D_MODEL, D_MODEL)),
                  mod_spec, _full((1, D_MODEL)), tok(D_MODEL), tok(D_FF), ANY],
        out_specs=[tok(D_MODEL), tok(512), tok(512), tok(512), _full((D_MODEL, D_MODEL)), _full((1, 512)),
                   _full((1, D_MODEL)), mod_spec],
        out_shape=[_sds((b, l, D_MODEL)), _sds((b, l, 512)), _sds((b, l, 512)), _sds((b, l, 512)),
                   _sds((D_MODEL, D_MODEL), BF), _sds((1, 512)), _sds((1, D_MODEL)), _sds((b, 8, D_MODEL))],
        scratch_shapes=[pltpu.VMEM((D_FF, D_MODEL), BF), pltpu.VMEM((D_MODEL, D_MODEL), F32)],
        compiler_params=_cp((ARB, ARB)),
    )(x, o_f, o_b, rg, y_mla, g_ret, w_out, mod, g_ffn, dxm, da, w1t)


ADAM_BLOCK_BYTES = 8 * 1024 * 1024


def _adamw(w, m, v, pieces, name, after=None):
    r, c = w.shape
    npc = pieces.shape[0]
    per_row = c * (7 * 4 + npc * pieces.dtype.itemsize) * 2
    rb = max(d for d in range(8, r + 1, 8) if r % d == 0 and d * per_row <= ADAM_BLOCK_BYTES)

    def body(w_ref, m_ref, v_ref, p_ref, *rest):
        g_ref, d_ref, nm_ref, nv_ref = rest[-4:]
        g = p_ref[0].astype(F32)
        for k in range(1, npc):
            g = g + p_ref[k].astype(F32)
        wv = w_ref[...]
        mn = ADAM_B1 * m_ref[...] + (1.0 - ADAM_B1) * g
        vn = ADAM_B2 * v_ref[...] + (1.0 - ADAM_B2) * jnp.square(g)
        m_hat = mn / (1.0 - ADAM_B1 ** ADAM_STEP)
        v_hat = vn / (1.0 - ADAM_B2 ** ADAM_STEP)
        g_ref[...] = g
        d_ref[...] = -ADAM_LR * (m_hat / (jnp.sqrt(v_hat) + ADAM_EPS) + ADAM_WD * wv)
        nm_ref[...] = mn
        nv_ref[...] = vn

    blk = pl.BlockSpec((rb, c), lambda i: (i, 0))
    extra = [] if after is None else [after]
    return pl.pallas_call(
        body, name=name, grid=(r // rb,),
        in_specs=[blk, blk, blk, pl.BlockSpec((npc, rb, c), lambda i: (0, i, 0))] + [ANY] * len(extra),
        out_specs=[blk] * 4, out_shape=[_sds((r, c))] * 4, compiler_params=_cp((ARB,)),
    )(w, m, v, pieces, *extra)


def _pad_rot_rows(w):
    k = w.shape[1]
    return jnp.pad(w.reshape(-1, 2, 32, k), ((0, 0), (0, 0), (0, 32), (0, 0))).reshape(-1, k)


def _cut_rot_rows(g):
    k = g.shape[1]
    return g.reshape(-1, 2, 64, k)[:, :, :32].reshape(-1, k)


def _w_in_pad(wt):
    w_a = jnp.concatenate([_pad_rot_rows(wt[0:512]), wt[512:1536]], axis=0)
    w_b = jnp.concatenate([wt[1536:2176], _pad_rot_rows(wt[2176:2240])], axis=0)
    return w_a, w_b


def _w_in_cut(g_a, g_b):
    return jnp.concatenate([_cut_rot_rows(g_a[0:1024]), g_a[1024:2048], g_b[0:640], _cut_rot_rows(g_b[640:768])], axis=0)


def _w_uq_pad(wt):
    w = wt.reshape(N_HEADS, 192, 384)
    rot = _pad_rot_rows(w[:, 128:].reshape(N_HEADS * 64, 384)).reshape(N_HEADS, LANE, 384)
    return jnp.concatenate([w[:, :128], rot], axis=1).reshape(1024, 384)


def _w_uq_cut(g):
    g = g.reshape(N_HEADS, 256, 384)
    rot = _cut_rot_rows(g[:, 128:].reshape(N_HEADS * LANE, 384)).reshape(N_HEADS, 64, 384)
    return jnp.concatenate([g[:, :128], rot], axis=1).reshape(768, 384)


def _w_ukv_perm(wt):
    return jnp.transpose(wt.reshape(N_HEADS, 2, LANE, 256), (1, 0, 2, 3)).reshape(1024, 256)


def _w_ukv_unperm(g):
    return jnp.transpose(g.reshape(2, N_HEADS, LANE, 256), (1, 0, 2, 3)).reshape(1024, 256)


def _unshard_cols(g):
    return jnp.transpose(g, (1, 0, 2)).reshape(g.shape[1], N_DEV * g.shape[2])


def _rope_tables():
    rows = SEQ // GRID_W
    row = jnp.repeat(jnp.arange(rows, dtype=F32), GRID_W)
    col = jnp.tile(jnp.arange(GRID_W, dtype=F32), rows)
    freq = ROPE_BASE ** (-jnp.arange(16, dtype=F32) / 16)
    ang = jnp.concatenate([row[:, None] * freq, col[:, None] * freq], axis=-1)
    cos, sin = jnp.cos(ang), jnp.sin(ang)
    z = jnp.zeros((SEQ, 32), F32)
    return jnp.concatenate([cos, z, cos, z], axis=1), jnp.concatenate([-sin, z, sin, z], axis=1)


_PACKED = (("g_attn", 1024), ("g_ffn", 1024), ("ret_decay_fwd", 4), ("ret_decay_bwd", 4), ("g_ret", 512),
           ("g_q_lora", 384), ("g_kv_lora", 256), ("g_final", 1024))
_PACK_OFF = {}
_off = 0
for _name, _n in _PACKED:
    _PACK_OFF[_name] = _off
    _off += -(-_n // LANE) * LANE
PACK_W = _off


def _pack_small(vals):
    parts = []
    for name, n in _PACKED:
        a = vals[name].reshape(-1).astype(F32)
        parts.append(jnp.pad(a, (0, -(-n // LANE) * LANE - n)))
    return jnp.concatenate(parts).reshape(1, PACK_W)


def _adamw_small(params, packed, gcc, gb_ada):
    names = list(params)
    n_p = len(names)

    def body(*refs):
        p_ref, gcc_ref, gb_ref = refs[3 * n_p:3 * n_p + 3]
        outs = refs[3 * n_p + 3:]
        for k, name in enumerate(names):
            w_ref, m_ref, v_ref = refs[3 * k:3 * k + 3]
            n = w_ref.shape[1]
            if name == "b_ada":
                g = jnp.concatenate([gb_ref[d, 0:1, :] for d in range(N_DEV)], axis=-1)
            elif name == "c_ctx":
                g = gcc_ref[0, 0:1, :]
                for d in range(1, N_DEV):
                    g = g + gcc_ref[d, 0:1, :]
            else:
                off = _PACK_OFF[name]
                g = p_ref[0, :, off:off + n]
                for d in range(1, N_DEV):
                    g = g + p_ref[d, :, off:off + n]
            mn = ADAM_B1 * m_ref[...] + (1.0 - ADAM_B1) * g
            vn = ADAM_B2 * v_ref[...] + (1.0 - ADAM_B2) * jnp.square(g)
            m_hat = mn / (1.0 - ADAM_B1 ** ADAM_STEP)
            v_hat = vn / (1.0 - ADAM_B2 ** ADAM_STEP)
            outs[4 * k][...] = g
            outs[4 * k + 1][...] = -ADAM_LR * (m_hat / (jnp.sqrt(v_hat) + ADAM_EPS) + ADAM_WD * w_ref[...])
            outs[4 * k + 2][...] = mn
            outs[4 * k + 3][...] = vn

    args = [a for name in names for a in params[name]] + [packed, gcc, gb_ada]
    out_shape = [_sds(params[name][0].shape) for name in names for _ in range(4)]
    outs = pl.pallas_call(body, name="adamw_small", out_shape=out_shape, compiler_params=_cp())(*args)
    return {name: list(outs[4 * k:4 * k + 4]) for k, name in enumerate(names)}


def kernel(x, c, ctx, c_ctx, w_ada, b_ada, g_attn, g_ffn, w_in, ret_decay_fwd, ret_decay_bwd, g_ret, g_q_lora, w_uq, g_kv_lora, w_ukv, w_out, w_ff1, w_ff2, g_final, loss_target, m_c_ctx, m_w_ada, m_b_ada, m_g_attn, m_g_ffn, m_w_in, m_ret_decay_fwd, m_ret_decay_bwd, m_g_ret, m_g_q_lora, m_w_uq, m_g_kv_lora, m_w_ukv, m_w_out, m_w_ff1, m_w_ff2, m_g_final, v_c_ctx, v_w_ada, v_b_ada, v_g_attn, v_g_ffn, v_w_in, v_ret_decay_fwd, v_ret_decay_bwd, v_g_ret, v_g_q_lora, v_w_uq, v_g_kv_lora, v_w_ukv, v_w_out, v_w_ff1, v_w_ff2, v_g_final):
    me = 4 * lax.axis_index("x") + 2 * lax.axis_index("y") + lax.axis_index("c")
    nb = x.shape[0]

    c_pad = jnp.pad(c, ((0, 8 - nb), (0, 0)))
    c_all, g_in, g_uq, g_ukv = _gather_two_level(
        [c_pad, w_in[0].T.astype(BF), w_uq[0].T.astype(BF), w_ukv[0].T.astype(BF)], "gather_weights")

    crows = jnp.concatenate([c_all[:, :nb].reshape(N_DEV * nb, D_MODEL), c_ctx[None], jnp.zeros((7, D_MODEL), F32)])
    b_blk = lax.dynamic_slice(b_ada, (0, me * 768), (1, 768))
    st_f = _exchange_start([_mod_fwd(crows, w_ada[0], b_blk), w_out[0].astype(BF), w_ff1[0].T.astype(BF),
                            w_ff2[0].astype(BF)], True, "gather_fwd_start")
    g_in, g_uq, g_ukv, _ = lax.optimization_barrier((g_in, g_uq, g_ukv, st_f["token"]))
    ws = (*_w_in_pad(g_in.reshape(2240, D_MODEL)), _w_uq_pad(g_uq.reshape(768, 384)),
          _w_ukv_perm(g_ukv.reshape(1024, 256)))
    (mod_g,) = _exchange_wait(st_f, ws[0], "gather_mod_wait", [0])
    mod_all = _unshard_cols(mod_g)
    mod_mine = lax.dynamic_slice(mod_all, (me * nb, 0), (nb, 6 * D_MODEL)).reshape(nb, 6, D_MODEL)
    mod = jnp.pad(mod_mine, ((0, 0), (0, 2), (0, 0)))
    mod_c = jnp.pad(mod_all[16].reshape(1, 6, D_MODEL), ((0, 0), (0, 2), (0, 0)))

    tabs = _rope_tables()
    dec_f = ret_decay_fwd.reshape(N_HEADS, 1, 1)
    dec_b = ret_decay_bwd.reshape(N_HEADS, 1, 1)

    rkc, rvc, k_ctx, v_ctx = _k1_fwd(ctx, mod_c, g_attn, g_q_lora, g_kv_lora, ws, tabs, None, True)
    rq, rk, rv, rg, q, k_all, v_all = _k1_fwd(x, mod, g_attn, g_q_lora, g_kv_lora, ws, tabs, (k_ctx, v_ctx), False)
    o_f, o_b, sf_prev, sb_prev = _k2_fwd(rq, rk, rv, rkc, rvc, dec_f, dec_b)
    y_mla, lse = _k3_fwd(q, k_all, v_all)
    (g_out,) = _exchange_wait(st_f, y_mla, "gather_wo_wait", [1])
    wo = g_out.reshape(D_MODEL, D_MODEL)
    x_mid, h2 = _k4a_fwd(x, o_f, o_b, rg, y_mla, g_ret, wo, mod, g_ffn)
    g_ff1t, g_ff2 = _exchange_wait(st_f, x_mid, "gather_ff_wait", [2, 3])
    w1t = g_ff1t.reshape(D_FF, D_MODEL)
    dxm, dmlp, relu_a, loss_acc, dgt_f, dg_final = _k4b_mlp_loss(h2, w1t, g_ff2.reshape(D_FF, D_MODEL), x_mid, mod,
                                                                 g_final.reshape(1, D_MODEL), loss_target)

    da, dw1, dw2 = _k4d_mlp_bwd(h2, dmlp, relu_a, g_ff2)
    st_s = _exchange_start([dw1, dw2], False, "scatter_ff_start")
    g_ret_t = g_ret + st_s["token"][0:1, 0:1]
    dx_res, do, drg, dym, dwo, dg_ret, dg_ffn, dmod_a = _k4e_bwd(x, o_f, o_b, rg, y_mla, g_ret_t, wo, mod, g_ffn, dxm, da,
                                                                 w1t)
    st_w = _exchange_start([dwo.reshape(N_DEV, 128, D_MODEL)], False, "scatter_wo_start")
    dq, dk_all, dv_all = _k3_bwd(q, k_all, v_all, y_mla, lse, dym, st_w["token"])
    dqf, dkf, dvf, dqb, dkb, dvb, dkc, dvc, ddf, ddb = _k2_bwd(rq, rk, rv, do, sf_prev, sb_prev, rkc, rvc, dec_f, dec_b)
    cts = [[(dqf, 0), (dqb, 0)], [(dkf, 0), (dkb, 0)], [(dvf, 0), (dvb, 0)], [(drg, 0)], [(dq, 0)],
           [(dk_all, 0)], [(dv_all, 0)]]
    cts_c = [[(dkc, 0)], [(dvc, 0)], [(dk_all, SEQ)], [(dv_all, SEQ)]]
    grad_x, accs, dmod_1, dmod_c1 = _k1_bwd(x, ctx, mod, mod_c, g_attn, g_q_lora, g_kv_lora, ws, tabs, cts, cts_c,
                                            dx_res)
    dwa, dwb, dwq, dwk, dg_attn, dg_q, dg_kv = accs

    dmod_loc = (dmod_a + dmod_1).at[:, 5, :].set(dgt_f[:, 0, :])[:, :6, :].reshape(nb, 6 * D_MODEL)
    dmod_ctx = dmod_c1[:, :6, :].reshape(1, 6 * D_MODEL)
    small = {"g_attn": dg_attn, "g_ffn": dg_ffn, "ret_decay_fwd": jnp.sum(ddf[:, :, 0, 0], axis=0),
             "ret_decay_bwd": jnp.sum(ddb[:, :, 0, 0], axis=0), "g_ret": dg_ret, "g_q_lora": dg_q, "g_kv_lora": dg_kv,
             "g_final": dg_final}
    extra = jnp.concatenate([dmod_loc, dmod_ctx, jnp.zeros((5, 6 * D_MODEL), F32)])
    ex_pieces = jnp.transpose(extra.reshape(8, N_DEV, 768), (1, 0, 2))
    st_sm = _exchange_start([_pack_small(small), ex_pieces, loss_acc], [True, False, True], "gather_small_start")
    dwa, dwb, dwq, dwk, _ = lax.optimization_barrier((dwa, dwb, dwq, dwk, st_sm["token"]))
    chip_sums = _pair_reduce([_w_in_cut(dwa, dwb).reshape(N_DEV, 280, D_MODEL), _w_uq_cut(dwq).reshape(N_DEV, 96, 384),
                              _w_ukv_unperm(dwk).reshape(N_DEV, 128, 256)], "pair_reduce")
    sm_g, ex_g, loss_g = _exchange_wait(st_sm, chip_sums[0], "gather_small_wait")
    dmod_blk = jnp.concatenate([ex_g[:, :nb].reshape(N_DEV * nb, 768), jnp.zeros((8, 768), F32)])
    gw_ada, gcc_part, gb_part = _mod_bwd(crows, w_ada[0], dmod_blk, ex_g[:, nb])
    st_c = _exchange_start([gcc_part, gb_part], True, "gather_cc_start")
    p_ff1, p_ff2 = _exchange_wait(st_s, st_c["token"], "scatter_ff_wait")
    (p_wo,) = _exchange_wait(st_w, p_ff1, "scatter_wo_wait")
    st_r = _exchange_start(chip_sums, False, "scatter_rest_start", after=p_wo, chips=True)

    res = {}
    early = (("w_ff1", w_ff1, m_w_ff1, v_w_ff1, p_ff1), ("w_ff2", w_ff2, m_w_ff2, v_w_ff2, p_ff2),
             ("w_ada", w_ada, m_w_ada, v_w_ada, gw_ada[None]), ("w_out", w_out, m_w_out, v_w_out, p_wo))
    behind = st_r["token"]
    for name, w, m, v, pcs in early:
        res[name] = [a[None] for a in _adamw(w[0], m[0], v[0], pcs, "adamw_" + name, after=behind)]
        behind = res[name][3]

    smalls = {"c_ctx": (c_ctx, m_c_ctx, v_c_ctx), "b_ada": (b_ada, m_b_ada, v_b_ada), "g_attn": (g_attn, m_g_attn, v_g_attn),
              "g_ffn": (g_ffn, m_g_ffn, v_g_ffn), "ret_decay_fwd": (ret_decay_fwd, m_ret_decay_fwd, v_ret_decay_fwd),
              "ret_decay_bwd": (ret_decay_bwd, m_ret_decay_bwd, v_ret_decay_bwd), "g_ret": (g_ret, m_g_ret, v_g_ret),
              "g_q_lora": (g_q_lora, m_g_q_lora, v_g_q_lora), "g_kv_lora": (g_kv_lora, m_g_kv_lora, v_g_kv_lora),
              "g_final": (g_final, m_g_final, v_g_final)}
    rows = {k: tuple(a.reshape(1, -1) for a in t) for k, t in smalls.items()}
    gcc_g, gb_g = _exchange_wait(st_c, behind, "gather_cc_wait")
    small_out = _adamw_small(rows, sm_g, gcc_g, gb_g)
    for name, outs in small_out.items():
        res[name] = [o.reshape(smalls[name][0].shape) for o in outs]

    pieces = _exchange_wait(st_r, small_out["g_final"][3], "scatter_rest_wait")
    for name, w, m, v, pcs in (("w_in", w_in, m_w_in, v_w_in, pieces[0]), ("w_uq", w_uq, m_w_uq, v_w_uq, pieces[1])):
        res[name] = [a.T[None] for a in _adamw(w[0].T, m[0].T, v[0].T, pcs, "adamw_" + name)]
    res["w_ukv"] = [a[None] for a in _adamw(w_ukv[0], m_w_ukv[0], v_w_ukv[0], jnp.transpose(pieces[2], (0, 2, 1)),
                                            "adamw_w_ukv")]

    loss = loss_g[0, 0, 0]
    for k in range(1, N_DEV):
        loss = loss + loss_g[k, 0, 0]

    order = ("c_ctx", "w_ada", "b_ada", "g_attn", "g_ffn", "w_in", "ret_decay_fwd", "ret_decay_bwd", "g_ret", "g_q_lora",
             "w_uq", "g_kv_lora", "w_ukv", "w_out", "w_ff1", "w_ff2", "g_final")
    return (loss, grad_x, *[res[n][0] for n in order], *[res[n][1] for n in order], *[res[n][2] for n in order],
            *[res[n][3] for n in order])
```

```python
import functools
import math

import jax
import jax.numpy as jnp
from jax import lax
from jax.experimental import pallas as pl
from jax.experimental.pallas import tpu as pltpu

F32 = jnp.float32
BF = jnp.bfloat16
EPS = 1e-6
LANE = 128
N_DEV = 8
D_MODEL = 1024
SEQ = 2048
CTX_LEN = 256
GRID_W = 64
N_HEADS = 4
RET_CHUNK = 512
N_CHUNK = SEQ // RET_CHUNK
D_FF = 4096
FF_BLK = D_FF // N_DEV
IN_PAD = 2816
KV_LEN = CTX_LEN + SEQ
ROPE_BASE = 10000.0
ADAM_LR, ADAM_B1, ADAM_B2, ADAM_EPS, ADAM_WD, ADAM_STEP = 0.001, 0.9, 0.999, 1e-08, 0.01, 10
TOK = 512
TOK_B = 256
VMEM_LIMIT = 56 * 1024 * 1024
ARB = "arbitrary"
MESH = pl.DeviceIdType.MESH
_HEAD_SL = [slice(LANE * h, LANE * (h + 1)) for h in range(N_HEADS)]
W_SHAPES = [(2048, D_MODEL), (768, D_MODEL), (1024, 384), (1024, 256)]


def _dot(a, b, ca, cb):
    return lax.dot_general(a.astype(BF), b.astype(BF), (((ca,), (cb,)), ((), ())), preferred_element_type=F32)


@jax.custom_vjp
def mm(a, b):
    return _dot(a, b, 1, 0)


@jax.custom_vjp
def mm_nt(a, b):
    return _dot(a, b, 1, 1)


@jax.custom_vjp
def mm_tn(a, b):
    return _dot(a, b, 0, 0)


mm.defvjp(lambda a, b: (_dot(a, b, 1, 0), (a, b)), lambda r, g: (mm_nt(g, r[1]), mm_tn(r[0], g)))
mm_nt.defvjp(lambda a, b: (_dot(a, b, 1, 1), (a, b)), lambda r, g: (mm(g, r[1]), mm_tn(g, r[0])))
mm_tn.defvjp(lambda a, b: (_dot(a, b, 0, 0), (a, b)), lambda r, g: (mm_nt(r[1], g), mm(r[0], g)))


@jax.custom_vjp
def _mmw(a, w, probe):
    return _dot(a, w, 1, 0)


def _mmw_bwd(r, g):
    a, w = r
    return mm_nt(g, w), jnp.zeros_like(w), mm_tn(a, g)


_mmw.defvjp(lambda a, w, probe: (_dot(a, w, 1, 0), (a, w)), _mmw_bwd)


@jax.custom_vjp
def _mmwt(a, wt, probe):
    return _dot(a, wt, 1, 1)


_mmwt.defvjp(lambda a, wt, probe: (_dot(a, wt, 1, 1), (a, wt)),
             lambda r, g: (mm(g, r[1]), jnp.zeros_like(r[1]), mm_tn(g, r[0])))


def mmwt(a, wt, probe):
    return _dot(a, wt, 1, 1) if probe is None else _mmwt(a, wt, probe)


def mmw(a, w, probe):
    return _dot(a, w, 1, 0) if probe is None else _mmw(a, w, probe)


def rmsn(x, g):
    return x * lax.rsqrt(jnp.mean(x * x, axis=-1, keepdims=True) + EPS) * g


def silu(x):
    return x * jax.nn.sigmoid(x)


def _swap_halves_impl(x):
    return pltpu.roll(x, 64, 1)


@jax.custom_vjp
def swap_halves(x):
    return _swap_halves_impl(x)


swap_halves.defvjp(lambda x: (_swap_halves_impl(x), None), lambda _, g: (_swap_halves_impl(g),))


def rope(x, cs1, sn1, every=1):
    blocks = []
    for i in range(x.shape[-1] // LANE):
        xb = x[:, LANE * i:LANE * (i + 1)]
        blocks.append(xb * cs1 + swap_halves(xb) * sn1 if i % every == every - 1 else xb)
    return blocks[0] if len(blocks) == 1 else jnp.concatenate(blocks, axis=-1)


def k1_tile(x, sh, sc, g_attn, g_q, g_kv, ws, ps, tabs, is_ctx):
    w_a, w_b, w_uq, w_ukv = ws
    p_a, p_b, p_uq, p_ukv = ps
    cs1, sn1 = tabs
    h = rmsn(x, g_attn) * (1.0 + sc) + sh
    pa = mmwt(h, w_a, p_a)
    pb = mmwt(h, w_b, p_b)
    rk = pa[:, 512:1024] * 0.125
    rv = pa[:, 1024:1536]
    kpe = pb[:, 640:768]
    kv = mmwt(rmsn(pb[:, 384:640], g_kv), w_ukv, p_ukv)
    if not is_ctx:
        rk = rope(rk, cs1, sn1)
        kpe = rope(kpe, cs1, sn1)
    k_full = jnp.concatenate([piece for sl in _HEAD_SL for piece in (kv[:, sl], kpe)], axis=-1)
    v = kv[:, 512:]
    if is_ctx:
        return rk, rv, k_full, v
    rq = rope(pa[:, 0:512], cs1, sn1)
    rg = pa[:, 1536:2048]
    q = rope(mmwt(rmsn(pb[:, 0:384], g_q), w_uq, p_uq), cs1, sn1, every=2)
    return rq, rk, rv, rg, q, k_full, v


def log_sigmoid(x):
    return jnp.minimum(x, 0.0) - jnp.log(1.0 + jnp.exp(-jnp.abs(x)))


def ret_chunk(q, k, v, s, lg, reverse):
    c = RET_CHUNK
    ii = lax.broadcasted_iota(jnp.int32, (c, c), 0).astype(F32)
    jj = lax.broadcasted_iota(jnp.int32, (c, c), 1).astype(F32)
    diff = (jj - ii) if reverse else (ii - jj)
    dec = jnp.where(diff >= 0, jnp.exp(lg * jnp.maximum(diff, 0.0)), 0.0)
    pos = lax.broadcasted_iota(jnp.int32, (c, 1), 0).astype(F32)
    if reverse:
        wk, wq = jnp.exp(lg * pos), jnp.exp(lg * (c - pos))
    else:
        wk, wq = jnp.exp(lg * (c - 1.0 - pos)), jnp.exp(lg * (pos + 1.0))
    o = mm(mm_nt(q, k) * dec, v) + mm(q * wq, s)
    s_next = jnp.exp(lg * float(c)) * s + mm_tn(k * wk, v)
    return o, s_next


def ctx_state(kc, vc, lg, reverse):
    n = kc.shape[0]
    pos = lax.broadcasted_iota(jnp.int32, (n, 1), 0).astype(F32)
    w = jnp.exp(lg * pos) if reverse else jnp.exp(lg * (n - 1.0 - pos))
    return mm_tn(kc * w, vc)


def attn_head(qn, qp, kn, kp, v):
    s = (mm_nt(qn, kn) + mm_nt(qp, kp)) * (1.0 / math.sqrt(192.0))
    e = jnp.exp(s - jnp.max(s, axis=-1, keepdims=True))
    return mm(e / jnp.sum(e, axis=-1, keepdims=True), v)


def gn_gate(o, rg, g_ret):
    ys = []
    for h in range(N_HEADS):
        sl = slice(LANE * h, LANE * (h + 1))
        oh = o[:, sl]
        mu = jnp.mean(oh, axis=-1, keepdims=True)
        var = jnp.mean(jnp.square(oh - mu), axis=-1, keepdims=True)
        ys.append((oh - mu) * lax.rsqrt(var + EPS) * g_ret[:, sl])
    return jnp.concatenate(ys, axis=-1) * silu(rg)


def k4a_tile(x, o_f, o_b, rg, y_mla, g_ret, gt_a, g_ffn, sh_f, sc_f, w_out, p_out):
    mix = jnp.concatenate([gn_gate(o_f + o_b, rg, g_ret), y_mla], axis=-1)
    x_mid = x + gt_a * mmw(mix, w_out, p_out)
    h2 = rmsn(x_mid, g_ffn) * (1.0 + sc_f) + sh_f
    return x_mid, h2


def k4c_tile(x_mid, mlp, gt_f, g_final, tgt):
    y = rmsn(x_mid + gt_f * mlp, g_final)
    per_tok = jnp.mean(jnp.square(y - tgt), axis=-1, keepdims=True)
    return 0.5 * jnp.sum(per_tok, axis=0, keepdims=True)


def _cp(sem=None, vmem=VMEM_LIMIT):
    return pltpu.CompilerParams(dimension_semantics=sem, vmem_limit_bytes=vmem)


def _acc(ref, val, first):
    @pl.when(first)
    def _():
        ref[...] = val

    @pl.when(jnp.logical_not(first))
    def _():
        ref[...] += val


def _full(shape):
    nd = len(shape)
    return pl.BlockSpec(shape, lambda *_: (0,) * nd)


ANY = pl.BlockSpec(memory_space=pl.ANY)


def _sds(shape, dtype=F32):
    return jax.ShapeDtypeStruct(shape, dtype)


def _exchange(arrs, gather, name):
    n = len(arrs)
    modes = [gather] * n if isinstance(gather, bool) else list(gather)
    out_shape = [_sds(((N_DEV,) + a.shape) if g else a.shape, a.dtype) for a, g in zip(arrs, modes)]

    def body(*refs):
        ins, outs = refs[:n], refs[n:2 * n]
        send_sems, recv_sems, local_sems = refs[2 * n:]
        x, y, c = lax.axis_index("x"), lax.axis_index("y"), lax.axis_index("c")
        me = 4 * x + 2 * y + c
        sends, recvs, locs = [], [], []
        for i in range(n):
            gather = modes[i]
            for k in range(N_DEV - 1):
                bits = k + 1
                px = x ^ ((bits >> 2) & 1)
                py = y ^ ((bits >> 1) & 1)
                pc = c ^ (bits & 1)
                peer = 4 * px + 2 * py + pc
                src = ins[i] if gather else ins[i].at[peer]
                sem = i * (N_DEV - 1) + k
                sends.append(pltpu.make_async_remote_copy(
                    src_ref=src, dst_ref=outs[i].at[me], send_sem=send_sems.at[sem], recv_sem=recv_sems.at[sem],
                    device_id=(px, py, pc), device_id_type=MESH))
                recvs.append(pltpu.make_async_remote_copy(
                    src_ref=src, dst_ref=outs[i].at[peer], send_sem=send_sems.at[sem], recv_sem=recv_sems.at[sem],
                    device_id=(px, py, pc), device_id_type=MESH))
            locs.append(pltpu.make_async_copy(ins[i] if gather else ins[i].at[me], outs[i].at[me], local_sems.at[i]))
        for cp in locs + sends:
            cp.start()
        for cp in recvs:
            cp.wait_recv()
        for cp in sends:
            cp.wait_send()
        for cp in locs:
            cp.wait()

    outs = pl.pallas_call(
        body, name=name, out_shape=out_shape, in_specs=[ANY] * n, out_specs=[ANY] * n,
        scratch_shapes=[pltpu.SemaphoreType.DMA((n * (N_DEV - 1),)), pltpu.SemaphoreType.DMA((n * (N_DEV - 1),)),
                        pltpu.SemaphoreType.DMA((n,))],
    )(*arrs)
    return list(outs)


def _gather_two_level(arrs, name):
    n = len(arrs)

    def body(*refs):
        ins, outs = refs[:n], refs[n:2 * n]
        send_sems, recv_sems, local_sems = refs[2 * n:]
        x, y, c = lax.axis_index("x"), lax.axis_index("y"), lax.axis_index("c")
        sibling = (x, y, 1 - c)
        chips = [(1 - x, y), (x, 1 - y), (1 - x, 1 - y)]

        def slot(px, py, pc):
            return 4 * px + 2 * py + pc

        first, passed, waits, locs = [], [], [], []
        for i in range(n):
            def copy(k, block, to, src=None, i=i):
                dst = outs[i].at[slot(*block)]
                return pltpu.make_async_remote_copy(
                    src_ref=dst if src is None else src, dst_ref=dst, send_sem=send_sems.at[7 * i + k],
                    recv_sem=recv_sems.at[7 * i + k], device_id=to, device_id_type=MESH)

            locs.append(pltpu.make_async_copy(ins[i], outs[i].at[slot(x, y, c)], local_sems.at[i]))
            first.append(copy(0, (x, y, c), sibling, src=ins[i]))
            first += [copy(1 + j, (x, y, c), (*chip, c), src=ins[i]) for j, chip in enumerate(chips)]
            passed.append([copy(4 + j, (*chip, c), sibling) for j, chip in enumerate(chips)])
            waits.append([copy(1 + j, (*chip, c), (x, y, c)) for j, chip in enumerate(chips)])
        for cp in locs + first:
            cp.start()
        for j in range(3):
            for i in range(n):
                waits[i][j].wait_recv()
                passed[i][j].start()
        for i in range(n):
            def arrival(k, block, i=i):
                dst = outs[i].at[slot(*block)]
                return pltpu.make_async_remote_copy(
                    src_ref=dst, dst_ref=dst, send_sem=send_sems.at[7 * i + k], recv_sem=recv_sems.at[7 * i + k],
                    device_id=sibling, device_id_type=MESH)

            arrival(0, (x, y, 1 - c)).wait_recv()
            for j, chip in enumerate(chips):
                arrival(4 + j, (*chip, 1 - c)).wait_recv()
        for cp in first + [p for ps in passed for p in ps]:
            cp.wait_send()
        for cp in locs:
            cp.wait()

    outs = pl.pallas_call(
        body, name=name, out_shape=[_sds((N_DEV,) + a.shape, a.dtype) for a in arrs], in_specs=[ANY] * n,
        out_specs=[ANY] * n,
        scratch_shapes=[pltpu.SemaphoreType.DMA((7 * n,)), pltpu.SemaphoreType.DMA((7 * n,)),
                        pltpu.SemaphoreType.DMA((n,))],
    )(*arrs)
    return list(outs)


HBM = pl.BlockSpec(memory_space=pltpu.HBM)
SEM = pl.BlockSpec(memory_space=pltpu.SEMAPHORE)
EFFECT = pltpu.SideEffectType.DATAFLOW_SIDE_EFFECTING


def _peer(k, chips=False):
    x, y, c = lax.axis_index("x"), lax.axis_index("y"), lax.axis_index("c")
    bits = (k + 1) << 1 if chips else k + 1
    px, py, pc = x ^ ((bits >> 2) & 1), y ^ ((bits >> 1) & 1), c ^ (bits & 1)
    if chips:
        return (px, py, pc), 2 * px + py, 2 * x + y
    return (px, py, pc), 4 * px + 2 * py + pc, 4 * x + 2 * y + c


def _exchange_start(arrs, gather, name, after=None, chips=False):
    n = len(arrs)
    n_peer, n_slot = (3, 4) if chips else (N_DEV - 1, N_DEV)
    modes = [gather] * n if isinstance(gather, bool) else list(gather)
    lands = [pltpu.with_memory_space_constraint(lax.empty(((n_slot,) + a.shape) if g else a.shape, a.dtype), pltpu.HBM)
             for a, g in zip(arrs, modes)]
    srcs = [pltpu.with_memory_space_constraint(a, pltpu.HBM) for a in arrs]

    extra = [] if after is None else [after]

    def body(*refs):
        ins, zones = refs[:n], refs[n:2 * n]
        send_sems, recv_sems, local_sems = refs[2 * n + len(extra):2 * n + len(extra) + 3]
        token = refs[-1]
        for i in range(n):
            gather = modes[i]
            for k in range(n_peer):
                dev, peer, me = _peer(k, chips)
                sem = i * n_peer + k
                pltpu.make_async_remote_copy(
                    src_ref=ins[i] if gather else ins[i].at[peer], dst_ref=zones[i].at[me],
                    send_sem=send_sems.at[sem], recv_sem=recv_sems.at[sem], device_id=dev, device_id_type=MESH).start()
            _, _, me = _peer(0, chips)
            pltpu.make_async_copy(ins[i] if gather else ins[i].at[me], zones[i].at[me], local_sems.at[i]).start()
        token[...] = jnp.zeros_like(token)

    nsem = n * n_peer
    outs = pl.pallas_call(
        body, name=name,
        out_shape=[pltpu.SemaphoreType.DMA((nsem,)), pltpu.SemaphoreType.DMA((nsem,)), pltpu.SemaphoreType.DMA((n,))]
        + [pltpu.HBM(a.shape, a.dtype) for a in srcs] + [pltpu.HBM(z.shape, z.dtype) for z in lands]
        + [_sds((8, LANE))],
        in_specs=[HBM] * (2 * n) + [ANY] * len(extra),
        out_specs=[SEM, SEM, SEM] + [HBM] * (2 * n) + [pl.BlockSpec(memory_space=pltpu.VMEM)],
        input_output_aliases={i: 3 + i for i in range(2 * n)},
        compiler_params=pltpu.CompilerParams(has_side_effects=EFFECT),
    )(*srcs, *lands, *extra)
    return {"n": n, "gather": modes, "chips": chips, "sems": outs[:3], "srcs": outs[3:3 + n],
            "lands": outs[3 + n:3 + 2 * n], "token": outs[-1]}


def _exchange_wait(st, after, name, which=None):
    modes, chips = st["gather"], st["chips"]
    afters = list(after) if isinstance(after, (list, tuple)) else [after]
    which = list(range(st["n"])) if which is None else which
    n = len(which)
    n_peer = 3 if chips else N_DEV - 1
    srcs, lands = [st["srcs"][i] for i in which], [st["lands"][i] for i in which]

    def body(*refs):
        ins, zones = refs[:n], refs[n:2 * n]
        send_sems, recv_sems, local_sems = refs[2 * n:2 * n + 3]
        for j, i in enumerate(which):
            gather = modes[i]
            for k in range(n_peer):
                dev, peer, me = _peer(k, chips)
                sem = i * n_peer + k
                src = ins[j] if gather else ins[j].at[peer]
                cp = pltpu.make_async_remote_copy(
                    src_ref=src, dst_ref=zones[j].at[peer], send_sem=send_sems.at[sem], recv_sem=recv_sems.at[sem],
                    device_id=dev, device_id_type=MESH)
                cp.wait_send()
                cp.wait_recv()
            _, _, me = _peer(0, chips)
            pltpu.make_async_copy(ins[j] if gather else ins[j].at[me], zones[j].at[me], local_sems.at[i]).wait()

    outs = pl.pallas_call(
        body, name=name,
        out_shape=[pltpu.HBM(a.shape, a.dtype) for a in srcs] + [pltpu.HBM(z.shape, z.dtype) for z in lands],
        in_specs=[HBM] * (2 * n) + [SEM, SEM, SEM] + [ANY] * len(afters), out_specs=[HBM] * (2 * n),
        input_output_aliases={i: i for i in range(2 * n)},
        compiler_params=pltpu.CompilerParams(has_side_effects=EFFECT),
    )(*srcs, *lands, *st["sems"], *afters)
    return list(outs[n:])


def _pair_reduce(arrs, name, after):
    n = len(arrs)

    def body(*refs):
        ins, refs = refs[:n], refs[n + 1:]
        outs, got, mine = refs[:n], refs[n:2 * n], refs[2 * n:3 * n]
        send_sems, recv_sems, local_sems = refs[3 * n:]
        x, y, c = lax.axis_index("x"), lax.axis_index("y"), lax.axis_index("c")
        sends, locs = [], []
        for i in range(n):
            for q in range(4):
                sem = 4 * i + q
                sends.append(pltpu.make_async_remote_copy(
                    src_ref=ins[i].at[2 * q + 1 - c], dst_ref=got[i].at[q], send_sem=send_sems.at[sem],
                    recv_sem=recv_sems.at[sem], device_id=(x, y, 1 - c), device_id_type=MESH))
                locs.append(pltpu.make_async_copy(ins[i].at[2 * q + c], mine[i].at[q], local_sems.at[sem]))
        for cp in locs + sends:
            cp.start()
        for cp in sends:
            cp.wait_recv()
        for cp in locs:
            cp.wait()
        for i in range(n):
            outs[i][...] = (mine[i][...].astype(F32) + got[i][...].astype(F32)).astype(BF)
        for cp in sends:
            cp.wait_send()

    half = [(4,) + a.shape[1:] for a in arrs]
    outs = pl.pallas_call(
        body, name=name, out_shape=[_sds(h, BF) for h in half], in_specs=[ANY] * (n + 1),
        out_specs=[pl.BlockSpec(memory_space=pltpu.VMEM)] * n,
        scratch_shapes=[pltpu.VMEM(h, BF) for h in half] * 2
        + [pltpu.SemaphoreType.DMA((4 * n,)), pltpu.SemaphoreType.DMA((4 * n,)), pltpu.SemaphoreType.DMA((4 * n,))],
        compiler_params=_cp(),
    )(*arrs, after)
    return list(outs)


def _mod_fwd(crows, w_ada, b_blk):
    def body(c_ref, w_ref, b_ref, o_ref):
        o_ref[...] = mm(silu(c_ref[...]), w_ref[...]) + b_ref[...]

    return pl.pallas_call(body, name="mod_fwd", out_shape=_sds((24, 768)), compiler_params=_cp())(crows, w_ada, b_blk)


def _mod_bwd(crows, w_ada, dmod_blk, dmodc_blk):
    def body(c_ref, w_ref, d_ref, dc_ref, gw_ref, gc_ref, gb_ref):
        cr = c_ref[...]
        dc = dc_ref[0:1, :]
        for p in range(1, N_DEV):
            dc = dc + dc_ref[p:p + 1, :]
        row = lax.broadcasted_iota(jnp.int32, (24, 1), 0)
        gw_ref[...] = mm_tn(silu(cr), jnp.where(row == 16, dc, d_ref[...]))
        cc = cr[16:17, :]
        sg = jax.nn.sigmoid(cc)
        part = mm_nt(jnp.broadcast_to(dc, (8, 768)), w_ref[...])
        gc_ref[...] = part * (sg * (1.0 + cc * (1.0 - sg)))
        gb_ref[...] = jnp.broadcast_to(jnp.sum(d_ref[...], axis=0, keepdims=True) + dc, (8, 768))

    return pl.pallas_call(
        body, name="mod_bwd", out_shape=[_sds((D_MODEL, 768)), _sds((8, D_MODEL)), _sds((8, 768))],
        compiler_params=_cp())(crows, w_ada, dmod_blk, dmodc_blk)


def _tab_specs(tk):
    return [pl.BlockSpec((tk, LANE), lambda i, t: (t, 0))] * 2


def _k1_fwd(x, mod, g_attn, g_q, g_kv, ws, tabs, kv_all, is_ctx):
    b, l, _ = x.shape
    tk = CTX_LEN if is_ctx else TOK
    nt = l // tk
    n_f32 = 2 if is_ctx else 4

    def body(x_ref, mod_ref, ga_ref, gq_ref, gk_ref, wa_ref, wb_ref, wq_ref, wk_ref, cs_ref, sn_ref, *rest):
        outs = rest if is_ctx else rest[2:]
        res = k1_tile(x_ref[...], mod_ref[0:1, :], mod_ref[1:2, :], ga_ref[...], gq_ref[...], gk_ref[...],
                      (wa_ref[...], wb_ref[...], wq_ref[...], wk_ref[...]), (None,) * 4,
                      (cs_ref[...], sn_ref[...]), is_ctx)
        for o_ref, r in zip(outs, res):
            o_ref[...] = r.astype(o_ref.dtype)

    tok = lambda w, off=0: pl.BlockSpec((None, tk, w), lambda i, t: (i, t + off, 0))
    mod_spec = pl.BlockSpec((None, 8, D_MODEL), (lambda i, t: (0, 0, 0)) if is_ctx else (lambda i, t: (i, 0, 0)))
    kv_off = SEQ // tk if is_ctx else 0
    in_specs = ([tok(D_MODEL), mod_spec, _full((1, D_MODEL)), _full((1, 384)), _full((1, 256))]
                + [_full(s) for s in W_SHAPES] + _tab_specs(tk))
    args = [x, mod, g_attn, g_q, g_kv, *ws, *tabs]
    out_specs = [tok(512)] * n_f32 + ([] if is_ctx else [tok(1024)]) + [tok(1024, kv_off), tok(512, kv_off)]
    out_shape = ([_sds((b, l, 512))] * n_f32 + ([] if is_ctx else [_sds((b, l, 1024), BF)])
                 + [_sds((b, KV_LEN, 1024), BF), _sds((b, KV_LEN, 512), BF)])
    aliases = {}
    if not is_ctx:
        aliases = {len(args): n_f32 + 1, len(args) + 1: n_f32 + 2}
        in_specs += [ANY, ANY]
        args += list(kv_all)
    return pl.pallas_call(
        body, name="k1_fwd_ctx" if is_ctx else "k1_fwd", grid=(b, nt), in_specs=in_specs, out_specs=out_specs,
        out_shape=out_shape, input_output_aliases=aliases, compiler_params=_cp((ARB, ARB)),
    )(*args)


N_ACC = 7


def _k1_bwd(x, ctx, mod, mod_c, g_attn, g_q, g_kv, ws, tabs, cts, cts_c, dx_res):
    b, l, _ = x.shape
    tk = TOK_B
    nt = l // tk
    flat = [[a for group in c for a in group] for c in (cts, cts_c)]
    sizes = [[len(g) for g in c] for c in (cts, cts_c)]
    acc_shapes = W_SHAPES + [(1, D_MODEL), (1, 384), (1, 256)]

    def body(*refs):
        it = iter(refs)
        x_ref, c_ref, mod_ref, modc_ref, ga_ref, gq_ref, gk_ref = [next(it) for _ in range(7)]
        w_hbm = [next(it) for _ in range(4)]
        tab_refs = [next(it) for _ in range(2)]
        ct_refs = [[next(it) for _ in f] for f in flat]
        res_ref, gx_ref = next(it), next(it)
        out_hbm = [next(it) for _ in range(N_ACC)]
        dmod_ref, dmodc_ref = next(it), next(it)
        w_vmem = [next(it) for _ in range(4)]
        accs = [next(it) for _ in range(N_ACC)]
        sem = next(it)
        i, t = pl.program_id(0), pl.program_id(1)
        first = jnp.logical_and(i == 0, t == 0)

        @pl.when(first)
        def _():
            for src, dst in zip(w_hbm, w_vmem):
                pltpu.sync_copy(src, dst)
            for k in range(N_ACC):
                accs[k][...] = jnp.zeros(acc_shapes[k], F32)

        def tile(is_ctx):
            which = 1 if is_ctx else 0
            ct_vals, pos = [], 0
            for gsz in sizes[which]:
                v = ct_refs[which][pos][...].astype(F32)
                for r in ct_refs[which][pos + 1:pos + gsz]:
                    v = v + r[...]
                ct_vals.append(v)
                pos += gsz
            wv = tuple(r[...] for r in w_vmem)
            tv = tuple(r[...] for r in tab_refs)
            m_ref = modc_ref if is_ctx else mod_ref

            def f(xv, sh, sc, ga, gq, gk, *probes):
                return k1_tile(xv, sh, sc, ga, gq, gk, wv, probes, tv, is_ctx)

            probes = [jnp.zeros(s, F32) for s in W_SHAPES]
            xin = c_ref[...] if is_ctx else x_ref[...]
            _, vjp = jax.vjp(f, xin, m_ref[0:1, :], m_ref[1:2, :], ga_ref[...], gq_ref[...], gk_ref[...], *probes)
            dx, dsh, dsc, dga, dgq, dgk, dwa, dwb, dwq, dwk = vjp(tuple(ct_vals))
            for ref, val in zip(accs, (dwa, dwb, dwq, dwk, dga, dgq, dgk)):
                ref[...] += val
            return dx, dsh, dsc

        @pl.when(t == 0)
        def _():
            _, dsh, dsc = tile(True)
            _acc(dmodc_ref.at[0:1, :], dsh, i == 0)
            _acc(dmodc_ref.at[1:2, :], dsc, i == 0)

            @pl.when(i == 0)
            def _():
                dmodc_ref[2:8, :] = jnp.zeros((6, D_MODEL), F32)

        @pl.when(t > 0)
        def _():
            dx, dsh, dsc = tile(False)
            gx_ref[...] = dx + res_ref[...]
            _acc(dmod_ref.at[0:1, :], dsh, t == 1)
            _acc(dmod_ref.at[1:2, :], dsc, t == 1)

            @pl.when(t == 1)
            def _():
                dmod_ref[2:8, :] = jnp.zeros((6, D_MODEL), F32)

        @pl.when(jnp.logical_and(i == b - 1, t == nt))
        def _():
            for k in range(4):
                w_vmem[k][...] = accs[k][...].astype(BF)
            cps = [pltpu.make_async_copy(w_vmem[k] if k < 4 else accs[k], out_hbm[k], sem.at[k]) for k in range(N_ACC)]
            for cp in cps:
                cp.start()
            for cp in cps:
                cp.wait()

    lat = lambda w, off=0: pl.BlockSpec((None, tk, w), lambda i, t: (i, jnp.maximum(t - 1, 0) + off, 0))
    con = lambda w, off=0: pl.BlockSpec((None, tk, w), lambda i, t: (i, off, 0))
    mod_spec = pl.BlockSpec((None, 8, D_MODEL), lambda i, t: (i, 0, 0))
    modc_spec = pl.BlockSpec((None, 8, D_MODEL), lambda i, t: (0, 0, 0))
    tab_spec = pl.BlockSpec((tk, LANE), lambda i, t: (jnp.maximum(t - 1, 0), 0))
    in_specs = ([lat(D_MODEL), con(D_MODEL), mod_spec, modc_spec, _full((1, D_MODEL)), _full((1, 384)), _full((1, 256))]
                + [ANY] * 4 + [tab_spec] * 2)
    args = [x, ctx, mod, mod_c, g_attn, g_q, g_kv, *ws, *tabs]
    for a, off in flat[0]:
        in_specs.append(lat(a.shape[-1], off // tk))
        args.append(a)
    for a, off in flat[1]:
        in_specs.append(con(a.shape[-1], off // tk))
        args.append(a)
    in_specs.append(lat(D_MODEL))
    args.append(dx_res)
    out_shape = ([_sds((b, l, D_MODEL))] + [_sds(s, BF) for s in W_SHAPES] + [_sds(s) for s in acc_shapes[4:]]
                 + [_sds((b, 8, D_MODEL)), _sds((1, 8, D_MODEL))])
    out_specs = [lat(D_MODEL)] + [ANY] * N_ACC + [mod_spec, modc_spec]
    outs = pl.pallas_call(
        body, name="k1_bwd", grid=(b, nt + 1), in_specs=in_specs, out_specs=out_specs, out_shape=out_shape,
        scratch_shapes=[pltpu.VMEM(s, BF) for s in W_SHAPES] + [pltpu.VMEM(s, F32) for s in acc_shapes]
        + [pltpu.SemaphoreType.DMA((N_ACC,))],
        compiler_params=_cp((ARB, ARB)),
    )(*args)
    return outs[0], list(outs[1:1 + N_ACC]), outs[1 + N_ACC], outs[2 + N_ACC]


def _chunk_spec(rev):
    if rev:
        return pl.BlockSpec((None, RET_CHUNK, 512), lambda i, n: (i, N_CHUNK - 1 - n, 0))
    return pl.BlockSpec((None, RET_CHUNK, 512), lambda i, n: (i, n, 0))


def _state_spec(rev):
    if rev:
        return pl.BlockSpec((None, N_HEADS, None, LANE, LANE), lambda i, n: (i, 0, N_CHUNK - 1 - n, 0, 0))
    return pl.BlockSpec((None, N_HEADS, None, LANE, LANE), lambda i, n: (i, 0, n, 0, 0))


_CTX_SPEC = pl.BlockSpec((None, CTX_LEN, 512), lambda i, n: (i, 0, 0))
_DEC_SPEC = pl.BlockSpec((N_HEADS, 1, 1), lambda i, n: (0, 0, 0))


def _k2_fwd(rq, rk, rv, rkc, rvc, dec_f, dec_b):
    b = rq.shape[0]

    def body(qf, kf, vf, qb, kb, vb, kc, vc, df, db, of_ref, ob_ref, sf_out, sb_out, sf, sb):
        n = pl.program_id(1)
        for h, sl in enumerate(_HEAD_SL):
            lgf, lgb = log_sigmoid(df[h]), log_sigmoid(db[h])

            @pl.when(n == 0)
            def _():
                sf[h] = ctx_state(kc[:, sl], vc[:, sl], lgf, False)
                sb[h] = ctx_state(kc[:, sl], vc[:, sl], lgb, True)

            sf_out[h] = sf[h]
            sb_out[h] = sb[h]
            o, s = ret_chunk(qf[:, sl], kf[:, sl], vf[:, sl], sf[h], lgf, False)
            of_ref[:, sl] = o
            sf[h] = s
            o, s = ret_chunk(qb[:, sl], kb[:, sl], vb[:, sl], sb[h], lgb, True)
            ob_ref[:, sl] = o
            sb[h] = s

    l = rq.shape[1]
    return pl.pallas_call(
        body, name="k2_fwd", grid=(b, N_CHUNK),
        in_specs=[_chunk_spec(False)] * 3 + [_chunk_spec(True)] * 3 + [_CTX_SPEC, _CTX_SPEC, _DEC_SPEC, _DEC_SPEC],
        out_specs=[_chunk_spec(False), _chunk_spec(True), _state_spec(False), _state_spec(True)],
        out_shape=[_sds((b, l, 512)), _sds((b, l, 512)), _sds((b, N_HEADS, N_CHUNK, LANE, LANE)),
                   _sds((b, N_HEADS, N_CHUNK, LANE, LANE))],
        scratch_shapes=[pltpu.VMEM((N_HEADS, LANE, LANE), F32), pltpu.VMEM((N_HEADS, LANE, LANE), F32)],
        compiler_params=_cp((ARB, ARB)),
    )(rq, rk, rv, rq, rk, rv, rkc, rvc, dec_f, dec_b)


def _k2_bwd(rq, rk, rv, do, sf_prev, sb_prev, rkc, rvc, dec_f, dec_b):
    b, l, _ = rq.shape

    def body(qf, kf, vf, gf, spf, qb, kb, vb, gb, spb, kc, vc, df, db,
             dqf, dkf, dvf, dqb, dkb, dvb, dkc, dvc, ddf, ddb, dsf, dsb):
        n = pl.program_id(1)

        @pl.when(n == 0)
        def _():
            dsf[...] = jnp.zeros((N_HEADS, LANE, LANE), F32)
            dsb[...] = jnp.zeros((N_HEADS, LANE, LANE), F32)

        def one(h, sl, q, k, v, g, sp, dec, ds, dq, dk, dv, dd, rev):
            def f(qv, kv_, vv, sv, dcy):
                return ret_chunk(qv, kv_, vv, sv, log_sigmoid(dcy), rev)

            _, vjp = jax.vjp(f, q[:, sl], k[:, sl], v[:, sl], sp[h], dec[h])
            gq, gk, gv, gs, gd = vjp((g[:, sl], ds[h]))
            dq[:, sl] = gq
            dk[:, sl] = gk
            dv[:, sl] = gv
            ds[h] = gs
            _acc(dd.at[h], jnp.broadcast_to(gd, (8, LANE)), n == 0)

        for h, sl in enumerate(_HEAD_SL):
            one(h, sl, qf, kf, vf, gf, spf, df, dsf, dqf, dkf, dvf, ddf, False)
            one(h, sl, qb, kb, vb, gb, spb, db, dsb, dqb, dkb, dvb, ddb, True)

        @pl.when(n == N_CHUNK - 1)
        def _():
            def f(kcv, vcv, dcy, rev):
                return ctx_state(kcv, vcv, log_sigmoid(dcy), rev)

            for h, sl in enumerate(_HEAD_SL):
                _, vjp_f = jax.vjp(functools.partial(f, rev=False), kc[:, sl], vc[:, sl], df[h])
                gk_f, gv_f, gd_f = vjp_f(dsf[h])
                _, vjp_b = jax.vjp(functools.partial(f, rev=True), kc[:, sl], vc[:, sl], db[h])
                gk_b, gv_b, gd_b = vjp_b(dsb[h])
                dkc[:, sl] = gk_f + gk_b
                dvc[:, sl] = gv_f + gv_b
                ddf[h] += jnp.broadcast_to(gd_f, (8, LANE))
                ddb[h] += jnp.broadcast_to(gd_b, (8, LANE))

    dd_spec = pl.BlockSpec((None, N_HEADS, 8, LANE), lambda i, n: (i, 0, 0, 0))
    return pl.pallas_call(
        body, name="k2_bwd", grid=(b, N_CHUNK),
        in_specs=[_chunk_spec(True)] * 4 + [_state_spec(True)] + [_chunk_spec(False)] * 4 + [_state_spec(False)]
        + [_CTX_SPEC, _CTX_SPEC, _DEC_SPEC, _DEC_SPEC],
        out_specs=[_chunk_spec(True)] * 3 + [_chunk_spec(False)] * 3 + [_CTX_SPEC, _CTX_SPEC, dd_spec, dd_spec],
        out_shape=[_sds((b, l, 512))] * 6 + [_sds((b, CTX_LEN, 512))] * 2 + [_sds((b, N_HEADS, 8, LANE))] * 2,
        scratch_shapes=[pltpu.VMEM((N_HEADS, LANE, LANE), F32), pltpu.VMEM((N_HEADS, LANE, LANE), F32)],
        compiler_params=_cp((ARB, ARB)),
    )(rq, rk, rv, do, sf_prev, rq, rk, rv, do, sb_prev, rkc, rvc, dec_f, dec_b)


TQ = 1024
TQ_F = 512
QK_W = 2 * LANE
N_QP = 2
_Q_PARTS = [slice(i * TQ_F // N_QP, (i + 1) * TQ_F // N_QP) for i in range(N_QP)]


SM_SCALE = 1.0 / math.sqrt(192.0)


def _k3_specs(tq):
    qs = lambda w: pl.BlockSpec((None, tq, w), lambda i, h, t: (i, t, h))
    ks = lambda w: pl.BlockSpec((None, KV_LEN, w), lambda i, h, t: (i, 0, h))
    return qs, ks


def _k3_fwd(q, k, v):
    b, l, _ = q.shape

    def body(q_ref, k_ref, v_ref, o_ref, lse_ref):
        kv_, vv = k_ref[...], v_ref[...]
        for r in _Q_PARTS:
            s = _dot(q_ref[r, :], kv_, 1, 1) * SM_SCALE
            m = jnp.max(s, axis=-1, keepdims=True)
            e = jnp.exp(s - m)
            tot = jnp.sum(e, axis=-1, keepdims=True)
            o_ref[r, :] = _dot(e, vv, 1, 0) * (1.0 / tot)
            lse_ref[r, :] = jnp.broadcast_to(m + jnp.log(tot), (TQ_F // N_QP, LANE))

    qs, ks = _k3_specs(TQ_F)
    return pl.pallas_call(
        body, name="k3_fwd", grid=(b, N_HEADS, l // TQ_F), in_specs=[qs(QK_W), ks(QK_W), ks(LANE)],
        out_specs=[qs(LANE), qs(LANE)], out_shape=[_sds((b, l, N_HEADS * LANE))] * 2,
        compiler_params=_cp((ARB, ARB, ARB)),
    )(q, k, v)


def _k3_bwd(q, k, v, o, lse, dy, after):
    b, l, _ = q.shape

    def body(q_ref, k_ref, v_ref, o_ref, lse_ref, dy_ref, after_ref, dq_ref, dk_ref, dv_ref):
        t0 = pl.program_id(2) == 0
        kv_, vv = k_ref[...], v_ref[...]
        qv, dyv = q_ref[...], dy_ref[...]
        g = dyv.astype(BF)
        lse_col = jnp.max(lse_ref[...], axis=-1, keepdims=True)
        delta = jnp.sum(dyv * o_ref[...], axis=-1, keepdims=True)
        p = jnp.exp(_dot(qv, kv_, 1, 1) * SM_SCALE - lse_col)
        ds = (p * (_dot(g, vv, 1, 1) - delta) * SM_SCALE).astype(BF)
        _acc(dv_ref, _dot(p, g, 0, 0), t0)
        dq_ref[...] = _dot(ds, kv_, 1, 0)
        _acc(dk_ref, _dot(ds, qv, 0, 0), t0)

    qs, ks = _k3_specs(TQ)
    return pl.pallas_call(
        body, name="k3_bwd", grid=(b, N_HEADS, l // TQ),
        in_specs=[qs(QK_W), ks(QK_W), ks(LANE), qs(LANE), qs(LANE), qs(LANE), ANY],
        out_specs=[qs(QK_W), ks(QK_W), ks(LANE)],
        out_shape=[_sds((b, l, N_HEADS * QK_W)), _sds((b, KV_LEN, N_HEADS * QK_W)), _sds((b, KV_LEN, N_HEADS * LANE))],
        compiler_params=_cp((ARB, ARB, ARB)),
    )(q, k, v, o, lse, dy, after)


def _mod_rows(mod_ref, rows):
    return [mod_ref[r:r + 1, :] for r in rows]


def _k4a_fwd(x, o_f, o_b, rg, y_mla, g_ret, w_out, mod, g_ffn):
    b, l, _ = x.shape

    def body(x_ref, of_ref, ob_ref, rg_ref, ym_ref, gr_ref, wo_ref, mod_ref, gf_ref, xm_ref, h2_ref):
        gt_a, sh_f, sc_f = _mod_rows(mod_ref, (2, 3, 4))
        x_mid, h2 = k4a_tile(x_ref[...], of_ref[...], ob_ref[...], rg_ref[...], ym_ref[...], gr_ref[...], gt_a,
                             gf_ref[...], sh_f, sc_f, wo_ref[...], None)
        xm_ref[...] = x_mid
        h2_ref[...] = h2.astype(BF)

    tok = lambda w: pl.BlockSpec((None, TOK, w), lambda i, t: (i, t, 0))
    mod_spec = pl.BlockSpec((None, 8, D_MODEL), lambda i, t: (i, 0, 0))
    return pl.pallas_call(
        body, name="k4a_fwd", grid=(b, l // TOK),
        in_specs=[tok(D_MODEL), tok(512), tok(512), tok(512), tok(512), _full((1, 512)), _full((D_MODEL, D_MODEL)),
                  mod_spec, _full((1, D_MODEL))],
        out_specs=[tok(D_MODEL), tok(D_MODEL)], out_shape=[_sds((b, l, D_MODEL)), _sds((b, l, D_MODEL), BF)],
        compiler_params=_cp((ARB, ARB)),
    )(x, o_f, o_b, rg, y_mla, g_ret, w_out, mod, g_ffn)


TOK_M = 512
TOK_D = 2048
HALF_FF = D_FF // 2


def _k4b_mlp_loss(h2, w1t, w2, x_mid, mod, g_final, tgt):
    b, l, _ = h2.shape
    nt = l // TOK_M

    def body(h2_ref, w1_hbm, w2_hbm, xm_ref, mod_ref, gfin_ref, tgt_ref, dxm_ref, dmlp_ref, r_ref, loss_ref, dgt_ref,
             dgfin_ref, w1_v, w2_v):
        i, t = pl.program_id(0), pl.program_id(1)
        first = jnp.logical_and(i == 0, t == 0)

        @pl.when(first)
        def _():
            pltpu.sync_copy(w1_hbm, w1_v)
            pltpu.sync_copy(w2_hbm, w2_v)

        h2v = h2_ref[...]
        mlp = None
        for half in range(2):
            rows = slice(half * HALF_FF, (half + 1) * HALF_FF)
            r = jnp.maximum(_dot(h2v, w1_v[rows, :], 1, 1), 0.0)
            r_ref[:, rows] = r.astype(BF)
            part = _dot(jnp.square(r), w2_v[rows, :], 1, 0)
            mlp = part if mlp is None else mlp + part
        (gt_f,) = _mod_rows(mod_ref, (5,))
        loss, vjp = jax.vjp(k4c_tile, xm_ref[...], mlp, gt_f, gfin_ref[...], tgt_ref[...])
        dxm, dmlp, dgt, dgfin, _ = vjp(jnp.ones((1, 1), F32))
        dxm_ref[...] = dxm
        dmlp_ref[...] = dmlp.astype(BF)
        _acc(loss_ref, jnp.broadcast_to(loss, (8, LANE)), first)
        _acc(dgfin_ref, dgfin, first)
        _acc(dgt_ref, dgt, t == 0)

    tok = lambda w: pl.BlockSpec((None, TOK_M, w), lambda i, t: (i, t, 0))
    return pl.pallas_call(
        body, name="k4b_mlp_loss", grid=(b, nt),
        in_specs=[tok(D_MODEL), ANY, ANY, tok(D_MODEL), pl.BlockSpec((None, 8, D_MODEL), lambda i, t: (i, 0, 0)),
                  _full((1, D_MODEL)), tok(D_MODEL)],
        out_specs=[tok(D_MODEL), tok(D_MODEL), tok(D_FF), _full((8, LANE)),
                   pl.BlockSpec((None, 1, D_MODEL), lambda i, t: (i, 0, 0)), _full((1, D_MODEL))],
        out_shape=[_sds((b, l, D_MODEL)), _sds((b, l, D_MODEL), BF), _sds((b, l, D_FF), BF), _sds((8, LANE)),
                   _sds((b, 1, D_MODEL)), _sds((1, D_MODEL))],
        scratch_shapes=[pltpu.VMEM((D_FF, D_MODEL), BF), pltpu.VMEM((D_FF, D_MODEL), BF)],
        compiler_params=_cp((ARB, ARB)),
    )(h2, w1t, w2, x_mid, mod, g_final, tgt)


def _k4d_mlp_bwd(h2, dmlp, r, w2):
    b, l, _ = h2.shape
    nt = l // TOK_D

    def body(h2_ref, dm_ref, r_ref, w2_ref, da_ref, dw1_ref, dw2_ref, acc1, acc2):
        i, t = pl.program_id(1), pl.program_id(2)
        first = jnp.logical_and(i == 0, t == 0)
        rv = r_ref[...].astype(F32)
        dm = dm_ref[...]
        da = (_dot(dm, w2_ref[...], 1, 1) * (2.0 * rv)).astype(BF)
        da_ref[...] = da
        _acc(acc2, _dot(jnp.square(rv), dm, 0, 0), first)
        _acc(acc1, _dot(h2_ref[...], da, 0, 0), first)

        @pl.when(jnp.logical_and(i == b - 1, t == nt - 1))
        def _():
            dw1_ref[...] = acc1[...].astype(BF)
            dw2_ref[...] = acc2[...].astype(BF)

    tok = lambda w: pl.BlockSpec((None, TOK_D, w), lambda j, i, t: (i, t, 0))
    col = pl.BlockSpec((None, TOK_D, FF_BLK), lambda j, i, t: (i, t, j))
    return pl.pallas_call(
        body, name="k4d_mlp_bwd", grid=(N_DEV, b, nt),
        in_specs=[tok(D_MODEL), tok(D_MODEL), col, pl.BlockSpec((None, FF_BLK, D_MODEL), lambda j, i, t: (j, 0, 0))],
        out_specs=[col, pl.BlockSpec((None, D_MODEL, FF_BLK), lambda j, i, t: (j, 0, 0)),
                   pl.BlockSpec((None, FF_BLK, D_MODEL), lambda j, i, t: (j, 0, 0))],
        out_shape=[_sds((b, l, D_FF), BF), _sds((N_DEV, D_MODEL, FF_BLK), BF), _sds((N_DEV, FF_BLK, D_MODEL), BF)],
        scratch_shapes=[pltpu.VMEM((D_MODEL, FF_BLK), F32), pltpu.VMEM((FF_BLK, D_MODEL), F32)],
        compiler_params=_cp((ARB, ARB, ARB)),
    )(h2, dmlp, r, w2)


def _k4e_bwd(x, o_f, o_b, rg, y_mla, g_ret, w_out, mod, g_ffn, dxm, da, w1t):
    b, l, _ = x.shape

    def body(x_ref, of_ref, ob_ref, rg_ref, ym_ref, gr_ref, wo_ref, mod_ref, gf_ref, dxm_ref, da_ref, w1_hbm,
             dx_ref, do_ref, drg_ref, dym_ref, dwo_ref, dgr_ref, dgf_ref, dmod_ref, w1_v, dwo_acc):
        i, t = pl.program_id(0), pl.program_id(1)
        first = jnp.logical_and(i == 0, t == 0)

        @pl.when(first)
        def _():
            pltpu.sync_copy(w1_hbm, w1_v)

        gt_a, sh_f, sc_f = _mod_rows(mod_ref, (2, 3, 4))
        wo = wo_ref[...]
        dh2 = _dot(da_ref[...], w1_v[...], 1, 0)

        def f(xv, ofv, rgv, ymv, grv, gta, gfv, shf, scf, p_out):
            return k4a_tile(xv, ofv, ob_ref[...], rgv, ymv, grv, gta, gfv, shf, scf, wo, p_out)

        _, vjp = jax.vjp(f, x_ref[...], of_ref[...], rg_ref[...], ym_ref[...], gr_ref[...], gt_a, gf_ref[...], sh_f,
                         sc_f, jnp.zeros((D_MODEL, D_MODEL), F32))
        dx, do, drg, dym, dgr, dgta, dgf, dshf, dscf, dwo = vjp((dxm_ref[...], dh2))
        dx_ref[...] = dx
        do_ref[...] = do
        drg_ref[...] = drg
        dym_ref[...] = dym
        _acc(dwo_acc, dwo, first)
        _acc(dgr_ref, dgr, first)
        _acc(dgf_ref, dgf, first)
        t0 = t == 0
        _acc(dmod_ref.at[2:3, :], dgta, t0)
        _acc(dmod_ref.at[3:4, :], dshf, t0)
        _acc(dmod_ref.at[4:5, :], dscf, t0)

        @pl.when(t0)
        def _():
            dmod_ref[0:2, :] = jnp.zeros((2, D_MODEL), F32)
            dmod_ref[5:8, :] = jnp.zeros((3, D_MODEL), F32)

        @pl.when(jnp.logical_and(i == b - 1, t == l // TOK_B - 1))
        def _():
            dwo_ref[...] = dwo_acc[...].astype(BF)

    tok = lambda w: pl.BlockSpec((None, TOK_B, w), lambda i, t: (i, t, 0))
    mod_spec = pl.BlockSpec((None, 8, D_MODEL), lambda i, t: (i, 0, 0))
    return pl.pallas_call(
        body, name="k4e_bwd", grid=(b, l // TOK_B),
        in_specs=[tok(D_MODEL), tok(512), tok(512), tok(512), tok(512), _full((1, 512)), _full((D_MODEL, D_MODEL)),
                  mod_spec, _full((1, D_MODEL)), tok(D_MODEL), tok(D_FF), ANY],
        out_specs=[tok(D_MODEL), tok(512), tok(512), tok(512), _full((D_MODEL, D_MODEL)), _full((1, 512)),
                   _full((1, D_MODEL)), mod_spec],
        out_shape=[_sds((b, l, D_MODEL)), _sds((b, l, 512)), _sds((b, l, 512)), _sds((b, l, 512)),
                   _sds((D_MODEL, D_MODEL), BF), _sds((1, 512)), _sds((1, D_MODEL)), _sds((b, 8, D_MODEL))],
        scratch_shapes=[pltpu.VMEM((D_FF, D_MODEL), BF), pltpu.VMEM((D_MODEL, D_MODEL), F32)],
        compiler_params=_cp((ARB, ARB)),
    )(x, o_f, o_b, rg, y_mla, g_ret, w_out, mod, g_ffn, dxm, da, w1t)


ADAM_BLOCK_BYTES = 32 * 1024 * 1024


def _adamw(w, m, v, pieces, name, after=None):
    r, c = w.shape
    npc = pieces.shape[0]
    per_row = c * (7 * 4 + npc * pieces.dtype.itemsize) * 2
    rb = max(d for d in range(8, r + 1, 8) if r % d == 0 and d * per_row <= ADAM_BLOCK_BYTES)

    def body(w_ref, m_ref, v_ref, p_ref, *rest):
        g_ref, d_ref, nm_ref, nv_ref = rest[-4:]
        g = p_ref[0].astype(F32)
        for k in range(1, npc):
            g = g + p_ref[k].astype(F32)
        wv = w_ref[...]
        mn = ADAM_B1 * m_ref[...] + (1.0 - ADAM_B1) * g
        vn = ADAM_B2 * v_ref[...] + (1.0 - ADAM_B2) * jnp.square(g)
        m_hat = mn / (1.0 - ADAM_B1 ** ADAM_STEP)
        v_hat = vn / (1.0 - ADAM_B2 ** ADAM_STEP)
        g_ref[...] = g
        d_ref[...] = -ADAM_LR * (m_hat / (jnp.sqrt(v_hat) + ADAM_EPS) + ADAM_WD * wv)
        nm_ref[...] = mn
        nv_ref[...] = vn

    blk = pl.BlockSpec((rb, c), lambda i: (i, 0))
    extra = [] if after is None else [after]
    return pl.pallas_call(
        body, name=name, grid=(r // rb,),
        in_specs=[blk, blk, blk, pl.BlockSpec((npc, rb, c), lambda i: (0, i, 0))] + [ANY] * len(extra),
        out_specs=[blk] * 4, out_shape=[_sds((r, c))] * 4, compiler_params=_cp((ARB,)),
    )(w, m, v, pieces, *extra)


def _pad_rot_rows(w, zero):
    k = w.shape[1]
    return lax.pad(w.reshape(-1, 2, 32, k), zero, ((0, 0, 0), (0, 0, 0), (0, 32, 0), (0, 0, 0))).reshape(-1, k)


def _cut_rot_rows(g):
    k = g.shape[1]
    return g.reshape(-1, 2, 64, k)[:, :, :32].reshape(-1, k)


def _w_in_pad(wt, zero):
    w_a = jnp.concatenate([_pad_rot_rows(wt[0:512], zero), wt[512:1536]], axis=0)
    w_b = jnp.concatenate([wt[1536:2176], _pad_rot_rows(wt[2176:2240], zero)], axis=0)
    return w_a, w_b


def _w_in_cut(g_a, g_b):
    return jnp.concatenate([_cut_rot_rows(g_a[0:1024]), g_a[1024:2048], g_b[0:640], _cut_rot_rows(g_b[640:768])], axis=0)


def _w_uq_pad(wt, zero):
    w = wt.reshape(N_HEADS, 192, 384)
    rot = _pad_rot_rows(w[:, 128:].reshape(N_HEADS * 64, 384), zero).reshape(N_HEADS, LANE, 384)
    return jnp.concatenate([w[:, :128], rot], axis=1).reshape(1024, 384)


def _w_uq_cut(g):
    g = g.reshape(N_HEADS, 256, 384)
    rot = _cut_rot_rows(g[:, 128:].reshape(N_HEADS * LANE, 384)).reshape(N_HEADS, 64, 384)
    return jnp.concatenate([g[:, :128], rot], axis=1).reshape(768, 384)


def _w_ukv_perm(wt):
    return jnp.transpose(wt.reshape(N_HEADS, 2, LANE, 256), (1, 0, 2, 3)).reshape(1024, 256)


def _w_ukv_unperm(g):
    return jnp.transpose(g.reshape(2, N_HEADS, LANE, 256), (1, 0, 2, 3)).reshape(1024, 256)


def _unshard_cols(g):
    return jnp.transpose(g, (1, 0, 2)).reshape(g.shape[1], N_DEV * g.shape[2])


def _rope_tables():
    rows = SEQ // GRID_W
    row = jnp.repeat(jnp.arange(rows, dtype=F32), GRID_W)
    col = jnp.tile(jnp.arange(GRID_W, dtype=F32), rows)
    freq = ROPE_BASE ** (-jnp.arange(16, dtype=F32) / 16)
    ang = jnp.concatenate([row[:, None] * freq, col[:, None] * freq], axis=-1)
    cos, sin = jnp.cos(ang), jnp.sin(ang)
    z = jnp.zeros((SEQ, 32), F32)
    return jnp.concatenate([cos, z, cos, z], axis=1), jnp.concatenate([-sin, z, sin, z], axis=1)


_PACKED = (("g_attn", 1024), ("g_ffn", 1024), ("ret_decay_fwd", 4), ("ret_decay_bwd", 4), ("g_ret", 512),
           ("g_q_lora", 384), ("g_kv_lora", 256), ("g_final", 1024))
_PACK_OFF = {}
_off = 0
for _name, _n in _PACKED:
    _PACK_OFF[_name] = _off
    _off += -(-_n // LANE) * LANE
PACK_W = _off


def _pack_small(vals):
    parts = []
    for name, n in _PACKED:
        a = vals[name].reshape(-1).astype(F32)
        parts.append(jnp.pad(a, (0, -(-n // LANE) * LANE - n)))
    return jnp.concatenate(parts).reshape(1, PACK_W)


def _adamw_small(params, packed, gcc, gb_ada):
    names = list(params)
    n_p = len(names)

    def body(*refs):
        p_ref, gcc_ref, gb_ref = refs[3 * n_p:3 * n_p + 3]
        outs = refs[3 * n_p + 3:]
        for k, name in enumerate(names):
            w_ref, m_ref, v_ref = refs[3 * k:3 * k + 3]
            n = w_ref.shape[1]
            if name == "b_ada":
                g = jnp.concatenate([gb_ref[d, 0:1, :] for d in range(N_DEV)], axis=-1)
            elif name == "c_ctx":
                g = gcc_ref[0, 0:1, :]
                for d in range(1, N_DEV):
                    g = g + gcc_ref[d, 0:1, :]
            else:
                off = _PACK_OFF[name]
                g = p_ref[0, :, off:off + n]
                for d in range(1, N_DEV):
                    g = g + p_ref[d, :, off:off + n]
            mn = ADAM_B1 * m_ref[...] + (1.0 - ADAM_B1) * g
            vn = ADAM_B2 * v_ref[...] + (1.0 - ADAM_B2) * jnp.square(g)
            m_hat = mn / (1.0 - ADAM_B1 ** ADAM_STEP)
            v_hat = vn / (1.0 - ADAM_B2 ** ADAM_STEP)
            outs[4 * k][...] = g
            outs[4 * k + 1][...] = -ADAM_LR * (m_hat / (jnp.sqrt(v_hat) + ADAM_EPS) + ADAM_WD * w_ref[...])
            outs[4 * k + 2][...] = mn
            outs[4 * k + 3][...] = vn

    args = [a for name in names for a in params[name]] + [packed, gcc, gb_ada]
    out_shape = [_sds(params[name][0].shape) for name in names for _ in range(4)]
    outs = pl.pallas_call(body, name="adamw_small", out_shape=out_shape, compiler_params=_cp())(*args)
    return {name: list(outs[4 * k:4 * k + 4]) for k, name in enumerate(names)}


def kernel(x, c, ctx, c_ctx, w_ada, b_ada, g_attn, g_ffn, w_in, ret_decay_fwd, ret_decay_bwd, g_ret, g_q_lora, w_uq, g_kv_lora, w_ukv, w_out, w_ff1, w_ff2, g_final, loss_target, m_c_ctx, m_w_ada, m_b_ada, m_g_attn, m_g_ffn, m_w_in, m_ret_decay_fwd, m_ret_decay_bwd, m_g_ret, m_g_q_lora, m_w_uq, m_g_kv_lora, m_w_ukv, m_w_out, m_w_ff1, m_w_ff2, m_g_final, v_c_ctx, v_w_ada, v_b_ada, v_g_attn, v_g_ffn, v_w_in, v_ret_decay_fwd, v_ret_decay_bwd, v_g_ret, v_g_q_lora, v_w_uq, v_g_kv_lora, v_w_ukv, v_w_out, v_w_ff1, v_w_ff2, v_g_final):
    me = 4 * lax.axis_index("x") + 2 * lax.axis_index("y") + lax.axis_index("c")
    nb = x.shape[0]

    c_pad = jnp.pad(c, ((0, 8 - nb), (0, 0)))
    c_all, g_in, g_uq, g_ukv = _gather_two_level(
        [c_pad, w_in[0].T.astype(BF), w_uq[0].T.astype(BF), w_ukv[0].T.astype(BF)], "gather_weights")

    crows = jnp.concatenate([c_all[:, :nb].reshape(N_DEV * nb, D_MODEL), c_ctx[None], jnp.zeros((7, D_MODEL), F32)])
    b_blk = lax.dynamic_slice(b_ada, (0, me * 768), (1, 768))
    st_f = _exchange_start([_mod_fwd(crows, w_ada[0], b_blk), w_out[0].astype(BF), w_ff1[0].T.astype(BF),
                            w_ff2[0].astype(BF)], True, "gather_fwd_start")
    zero = st_f["token"][0, 0].astype(BF)
    ws = (*_w_in_pad(g_in.reshape(2240, D_MODEL), zero), _w_uq_pad(g_uq.reshape(768, 384), zero),
          _w_ukv_perm(g_ukv.reshape(1024, 256)))
    (mod_g,) = _exchange_wait(st_f, ws, "gather_mod_wait", [0])
    mod_all = _unshard_cols(mod_g)
    mod_mine = lax.dynamic_slice(mod_all, (me * nb, 0), (nb, 6 * D_MODEL)).reshape(nb, 6, D_MODEL)
    mod = jnp.pad(mod_mine, ((0, 0), (0, 2), (0, 0)))
    mod_c = jnp.pad(mod_all[16].reshape(1, 6, D_MODEL), ((0, 0), (0, 2), (0, 0)))

    tabs = _rope_tables()
    dec_f = ret_decay_fwd.reshape(N_HEADS, 1, 1)
    dec_b = ret_decay_bwd.reshape(N_HEADS, 1, 1)

    rkc, rvc, k_ctx, v_ctx = _k1_fwd(ctx, mod_c, g_attn, g_q_lora, g_kv_lora, ws, tabs, None, True)
    rq, rk, rv, rg, q, k_all, v_all = _k1_fwd(x, mod, g_attn, g_q_lora, g_kv_lora, ws, tabs, (k_ctx, v_ctx), False)
    o_f, o_b, sf_prev, sb_prev = _k2_fwd(rq, rk, rv, rkc, rvc, dec_f, dec_b)
    y_mla, lse = _k3_fwd(q, k_all, v_all)
    (g_out,) = _exchange_wait(st_f, y_mla, "gather_wo_wait", [1])
    wo = g_out.reshape(D_MODEL, D_MODEL)
    x_mid, h2 = _k4a_fwd(x, o_f, o_b, rg, y_mla, g_ret, wo, mod, g_ffn)
    g_ff1t, g_ff2 = _exchange_wait(st_f, x_mid, "gather_ff_wait", [2, 3])
    w1t = g_ff1t.reshape(D_FF, D_MODEL)
    dxm, dmlp, relu_a, loss_acc, dgt_f, dg_final = _k4b_mlp_loss(h2, w1t, g_ff2.reshape(D_FF, D_MODEL), x_mid, mod,
                                                                 g_final.reshape(1, D_MODEL), loss_target)

    da, dw1, dw2 = _k4d_mlp_bwd(h2, dmlp, relu_a, g_ff2)
    st_s = _exchange_start([dw1, dw2], False, "scatter_ff_start")
    g_ret_t = g_ret + st_s["token"][0:1, 0:1]
    dx_res, do, drg, dym, dwo, dg_ret, dg_ffn, dmod_a = _k4e_bwd(x, o_f, o_b, rg, y_mla, g_ret_t, wo, mod, g_ffn, dxm, da,
                                                                 w1t)
    st_w = _exchange_start([dwo.reshape(N_DEV, 128, D_MODEL)], False, "scatter_wo_start")
    dq, dk_all, dv_all = _k3_bwd(q, k_all, v_all, y_mla, lse, dym, st_w["token"])
    dqf, dkf, dvf, dqb, dkb, dvb, dkc, dvc, ddf, ddb = _k2_bwd(rq, rk, rv, do, sf_prev, sb_prev, rkc, rvc, dec_f, dec_b)
    cts = [[(dqf, 0), (dqb, 0)], [(dkf, 0), (dkb, 0)], [(dvf, 0), (dvb, 0)], [(drg, 0)], [(dq, 0)],
           [(dk_all, 0)], [(dv_all, 0)]]
    cts_c = [[(dkc, 0)], [(dvc, 0)], [(dk_all, SEQ)], [(dv_all, SEQ)]]
    grad_x, accs, dmod_1, dmod_c1 = _k1_bwd(x, ctx, mod, mod_c, g_attn, g_q_lora, g_kv_lora, ws, tabs, cts, cts_c,
                                            dx_res)
    dwa, dwb, dwq, dwk, dg_attn, dg_q, dg_kv = accs

    dmod_loc = (dmod_a + dmod_1).at[:, 5, :].set(dgt_f[:, 0, :])[:, :6, :].reshape(nb, 6 * D_MODEL)
    dmod_ctx = dmod_c1[:, :6, :].reshape(1, 6 * D_MODEL)
    small = {"g_attn": dg_attn, "g_ffn": dg_ffn, "ret_decay_fwd": jnp.sum(ddf[:, :, 0, 0], axis=0),
             "ret_decay_bwd": jnp.sum(ddb[:, :, 0, 0], axis=0), "g_ret": dg_ret, "g_q_lora": dg_q, "g_kv_lora": dg_kv,
             "g_final": dg_final}
    extra = jnp.concatenate([dmod_loc, dmod_ctx, jnp.zeros((5, 6 * D_MODEL), F32)])
    ex_pieces = jnp.transpose(extra.reshape(8, N_DEV, 768), (1, 0, 2))
    st_sm = _exchange_start([_pack_small(small), ex_pieces, loss_acc], [True, False, True], "gather_small_start")
    chip_sums = _pair_reduce([_w_in_cut(dwa, dwb).reshape(N_DEV, 280, D_MODEL), _w_uq_cut(dwq).reshape(N_DEV, 96, 384),
                              _w_ukv_unperm(dwk).reshape(N_DEV, 128, 256)], "pair_reduce", st_sm["token"])
    sm_g, ex_g, loss_g = _exchange_wait(st_sm, chip_sums[0], "gather_small_wait")
    dmod_blk = jnp.concatenate([ex_g[:, :nb].reshape(N_DEV * nb, 768), jnp.zeros((8, 768), F32)])
    gw_ada, gcc_part, gb_part = _mod_bwd(crows, w_ada[0], dmod_blk, ex_g[:, nb])
    st_c = _exchange_start([gcc_part, gb_part], True, "gather_cc_start")
    p_ff1, p_ff2 = _exchange_wait(st_s, st_c["token"], "scatter_ff_wait")
    (p_wo,) = _exchange_wait(st_w, p_ff1, "scatter_wo_wait")
    st_r = _exchange_start(chip_sums, False, "scatter_rest_start", after=p_wo, chips=True)

    res = {}
    early = (("w_ff1", w_ff1, m_w_ff1, v_w_ff1, p_ff1), ("w_ff2", w_ff2, m_w_ff2, v_w_ff2, p_ff2),
             ("w_ada", w_ada, m_w_ada, v_w_ada, gw_ada[None]), ("w_out", w_out, m_w_out, v_w_out, p_wo))
    behind = st_r["token"]
    for name, w, m, v, pcs in early:
        res[name] = [a[None] for a in _adamw(w[0], m[0], v[0], pcs, "adamw_" + name, after=behind)]
        behind = res[name][3]

    smalls = {"c_ctx": (c_ctx, m_c_ctx, v_c_ctx), "b_ada": (b_ada, m_b_ada, v_b_ada), "g_attn": (g_attn, m_g_attn, v_g_attn),
              "g_ffn": (g_ffn, m_g_ffn, v_g_ffn), "ret_decay_fwd": (ret_decay_fwd, m_ret_decay_fwd, v_ret_decay_fwd),
              "ret_decay_bwd": (ret_decay_bwd, m_ret_decay_bwd, v_ret_decay_bwd), "g_ret": (g_ret, m_g_ret, v_g_ret),
              "g_q_lora": (g_q_lora, m_g_q_lora, v_g_q_lora), "g_kv_lora": (g_kv_lora, m_g_kv_lora, v_g_kv_lora),
              "g_final": (g_final, m_g_final, v_g_final)}
    rows = {k: tuple(a.reshape(1, -1) for a in t) for k, t in smalls.items()}
    gcc_g, gb_g = _exchange_wait(st_c, behind, "gather_cc_wait")
    small_out = _adamw_small(rows, sm_g, gcc_g, gb_g)
    for name, outs in small_out.items():
        res[name] = [o.reshape(smalls[name][0].shape) for o in outs]

    pieces = _exchange_wait(st_r, small_out["g_final"][3], "scatter_rest_wait")
    for name, w, m, v, pcs in (("w_in", w_in, m_w_in, v_w_in, pieces[0]), ("w_uq", w_uq, m_w_uq, v_w_uq, pieces[1])):
        res[name] = [a.T[None] for a in _adamw(w[0].T, m[0].T, v[0].T, pcs, "adamw_" + name)]
    res["w_ukv"] = [a[None] for a in _adamw(w_ukv[0], m_w_ukv[0], v_w_ukv[0], jnp.transpose(pieces[2], (0, 2, 1)),
                                            "adamw_w_ukv")]

    loss = loss_g[0, 0, 0]
    for k in range(1, N_DEV):
        loss = loss + loss_g[k, 0, 0]

    order = ("c_ctx", "w_ada", "b_ada", "g_attn", "g_ffn", "w_in", "ret_decay_fwd", "ret_decay_bwd", "g_ret", "g_q_lora",
             "w_uq", "g_kv_lora", "w_ukv", "w_out", "w_ff1", "w_ff2", "g_final")
    return (loss, grad_x, *[res[n][0] for n in order], *[res[n][1] for n in order], *[res[n][2] for n in order],
            *[res[n][3] for n in order])
```

```python
import functools
import math

import jax
import jax.numpy as jnp
from jax import lax
from jax.experimental import pallas as pl
from jax.experimental.pallas import tpu as pltpu

F32 = jnp.float32
BF = jnp.bfloat16
EPS = 1e-6
LANE = 128
N_DEV = 8
D_MODEL = 1024
SEQ = 2048
CTX_LEN = 256
GRID_W = 64
N_HEADS = 4
RET_CHUNK = 512
N_CHUNK = SEQ // RET_CHUNK
D_FF = 4096
FF_BLK = D_FF // N_DEV
IN_PAD = 2816
KV_LEN = CTX_LEN + SEQ
ROPE_BASE = 10000.0
ADAM_LR, ADAM_B1, ADAM_B2, ADAM_EPS, ADAM_WD, ADAM_STEP = 0.001, 0.9, 0.999, 1e-08, 0.01, 10
TOK = 512
TOK_B = 256
VMEM_LIMIT = 56 * 1024 * 1024
ARB = "arbitrary"
MESH = pl.DeviceIdType.MESH
_HEAD_SL = [slice(LANE * h, LANE * (h + 1)) for h in range(N_HEADS)]
W_SHAPES = [(2048, D_MODEL), (768, D_MODEL), (1024, 384), (1024, 256)]


def _dot(a, b, ca, cb):
    return lax.dot_general(a.astype(BF), b.astype(BF), (((ca,), (cb,)), ((), ())), preferred_element_type=F32)


@jax.custom_vjp
def mm(a, b):
    return _dot(a, b, 1, 0)


@jax.custom_vjp
def mm_nt(a, b):
    return _dot(a, b, 1, 1)


@jax.custom_vjp
def mm_tn(a, b):
    return _dot(a, b, 0, 0)


mm.defvjp(lambda a, b: (_dot(a, b, 1, 0), (a, b)), lambda r, g: (mm_nt(g, r[1]), mm_tn(r[0], g)))
mm_nt.defvjp(lambda a, b: (_dot(a, b, 1, 1), (a, b)), lambda r, g: (mm(g, r[1]), mm_tn(g, r[0])))
mm_tn.defvjp(lambda a, b: (_dot(a, b, 0, 0), (a, b)), lambda r, g: (mm_nt(r[1], g), mm(r[0], g)))


@jax.custom_vjp
def _mmw(a, w, probe):
    return _dot(a, w, 1, 0)


def _mmw_bwd(r, g):
    a, w = r
    return mm_nt(g, w), jnp.zeros_like(w), mm_tn(a, g)


_mmw.defvjp(lambda a, w, probe: (_dot(a, w, 1, 0), (a, w)), _mmw_bwd)


@jax.custom_vjp
def _mmwt(a, wt, probe):
    return _dot(a, wt, 1, 1)


_mmwt.defvjp(lambda a, wt, probe: (_dot(a, wt, 1, 1), (a, wt)),
             lambda r, g: (mm(g, r[1]), jnp.zeros_like(r[1]), mm_tn(g, r[0])))


def mmwt(a, wt, probe):
    return _dot(a, wt, 1, 1) if probe is None else _mmwt(a, wt, probe)


def mmw(a, w, probe):
    return _dot(a, w, 1, 0) if probe is None else _mmw(a, w, probe)


def rmsn(x, g):
    return x * lax.rsqrt(jnp.mean(x * x, axis=-1, keepdims=True) + EPS) * g


def silu(x):
    return x * jax.nn.sigmoid(x)


def _swap_halves_impl(x):
    return pltpu.roll(x, 64, 1)


@jax.custom_vjp
def swap_halves(x):
    return _swap_halves_impl(x)


swap_halves.defvjp(lambda x: (_swap_halves_impl(x), None), lambda _, g: (_swap_halves_impl(g),))


def rope(x, cs1, sn1, every=1):
    blocks = []
    for i in range(x.shape[-1] // LANE):
        xb = x[:, LANE * i:LANE * (i + 1)]
        blocks.append(xb * cs1 + swap_halves(xb) * sn1 if i % every == every - 1 else xb)
    return blocks[0] if len(blocks) == 1 else jnp.concatenate(blocks, axis=-1)


def k1_tile(x, sh, sc, g_attn, g_q, g_kv, ws, ps, tabs, is_ctx):
    w_a, w_b, w_uq, w_ukv = ws
    p_a, p_b, p_uq, p_ukv = ps
    cs1, sn1 = tabs
    h = rmsn(x, g_attn) * (1.0 + sc) + sh
    pa = mmwt(h, w_a, p_a)
    pb = mmwt(h, w_b, p_b)
    rk = pa[:, 512:1024] * 0.125
    rv = pa[:, 1024:1536]
    kpe = pb[:, 640:768]
    kv = mmwt(rmsn(pb[:, 384:640], g_kv), w_ukv, p_ukv)
    if not is_ctx:
        rk = rope(rk, cs1, sn1)
        kpe = rope(kpe, cs1, sn1)
    k_full = jnp.concatenate([piece for sl in _HEAD_SL for piece in (kv[:, sl], kpe)], axis=-1)
    v = kv[:, 512:]
    if is_ctx:
        return rk, rv, k_full, v
    rq = rope(pa[:, 0:512], cs1, sn1)
    rg = pa[:, 1536:2048]
    q = rope(mmwt(rmsn(pb[:, 0:384], g_q), w_uq, p_uq), cs1, sn1, every=2)
    return rq, rk, rv, rg, q, k_full, v


def log_sigmoid(x):
    return jnp.minimum(x, 0.0) - jnp.log(1.0 + jnp.exp(-jnp.abs(x)))


def ret_chunk(q, k, v, s, lg, reverse):
    c = RET_CHUNK
    ii = lax.broadcasted_iota(jnp.int32, (c, c), 0).astype(F32)
    jj = lax.broadcasted_iota(jnp.int32, (c, c), 1).astype(F32)
    diff = (jj - ii) if reverse else (ii - jj)
    dec = jnp.where(diff >= 0, jnp.exp(lg * jnp.maximum(diff, 0.0)), 0.0)
    pos = lax.broadcasted_iota(jnp.int32, (c, 1), 0).astype(F32)
    if reverse:
        wk, wq = jnp.exp(lg * pos), jnp.exp(lg * (c - pos))
    else:
        wk, wq = jnp.exp(lg * (c - 1.0 - pos)), jnp.exp(lg * (pos + 1.0))
    o = mm(mm_nt(q, k) * dec, v) + mm(q * wq, s)
    s_next = jnp.exp(lg * float(c)) * s + mm_tn(k * wk, v)
    return o, s_next


def ctx_state(kc, vc, lg, reverse):
    n = kc.shape[0]
    pos = lax.broadcasted_iota(jnp.int32, (n, 1), 0).astype(F32)
    w = jnp.exp(lg * pos) if reverse else jnp.exp(lg * (n - 1.0 - pos))
    return mm_tn(kc * w, vc)


def attn_head(qn, qp, kn, kp, v):
    s = (mm_nt(qn, kn) + mm_nt(qp, kp)) * (1.0 / math.sqrt(192.0))
    e = jnp.exp(s - jnp.max(s, axis=-1, keepdims=True))
    return mm(e / jnp.sum(e, axis=-1, keepdims=True), v)


def gn_gate(o, rg, g_ret):
    ys = []
    for h in range(N_HEADS):
        sl = slice(LANE * h, LANE * (h + 1))
        oh = o[:, sl]
        mu = jnp.mean(oh, axis=-1, keepdims=True)
        var = jnp.mean(jnp.square(oh - mu), axis=-1, keepdims=True)
        ys.append((oh - mu) * lax.rsqrt(var + EPS) * g_ret[:, sl])
    return jnp.concatenate(ys, axis=-1) * silu(rg)


def k4a_tile(x, o_f, o_b, rg, y_mla, g_ret, gt_a, g_ffn, sh_f, sc_f, w_out, p_out):
    mix = jnp.concatenate([gn_gate(o_f + o_b, rg, g_ret), y_mla], axis=-1)
    x_mid = x + gt_a * mmw(mix, w_out, p_out)
    h2 = rmsn(x_mid, g_ffn) * (1.0 + sc_f) + sh_f
    return x_mid, h2


def k4c_tile(x_mid, mlp, gt_f, g_final, tgt):
    y = rmsn(x_mid + gt_f * mlp, g_final)
    per_tok = jnp.mean(jnp.square(y - tgt), axis=-1, keepdims=True)
    return 0.5 * jnp.sum(per_tok, axis=0, keepdims=True)


def _cp(sem=None, vmem=VMEM_LIMIT):
    return pltpu.CompilerParams(dimension_semantics=sem, vmem_limit_bytes=vmem)


def _acc(ref, val, first):
    @pl.when(first)
    def _():
        ref[...] = val

    @pl.when(jnp.logical_not(first))
    def _():
        ref[...] += val


def _full(shape):
    nd = len(shape)
    return pl.BlockSpec(shape, lambda *_: (0,) * nd)


ANY = pl.BlockSpec(memory_space=pl.ANY)


def _sds(shape, dtype=F32):
    return jax.ShapeDtypeStruct(shape, dtype)


def _exchange(arrs, gather, name):
    n = len(arrs)
    modes = [gather] * n if isinstance(gather, bool) else list(gather)
    out_shape = [_sds(((N_DEV,) + a.shape) if g else a.shape, a.dtype) for a, g in zip(arrs, modes)]

    def body(*refs):
        ins, outs = refs[:n], refs[n:2 * n]
        send_sems, recv_sems, local_sems = refs[2 * n:]
        x, y, c = lax.axis_index("x"), lax.axis_index("y"), lax.axis_index("c")
        me = 4 * x + 2 * y + c
        sends, recvs, locs = [], [], []
        for i in range(n):
            gather = modes[i]
            for k in range(N_DEV - 1):
                bits = k + 1
                px = x ^ ((bits >> 2) & 1)
                py = y ^ ((bits >> 1) & 1)
                pc = c ^ (bits & 1)
                peer = 4 * px + 2 * py + pc
                src = ins[i] if gather else ins[i].at[peer]
                sem = i * (N_DEV - 1) + k
                sends.append(pltpu.make_async_remote_copy(
                    src_ref=src, dst_ref=outs[i].at[me], send_sem=send_sems.at[sem], recv_sem=recv_sems.at[sem],
                    device_id=(px, py, pc), device_id_type=MESH))
                recvs.append(pltpu.make_async_remote_copy(
                    src_ref=src, dst_ref=outs[i].at[peer], send_sem=send_sems.at[sem], recv_sem=recv_sems.at[sem],
                    device_id=(px, py, pc), device_id_type=MESH))
            locs.append(pltpu.make_async_copy(ins[i] if gather else ins[i].at[me], outs[i].at[me], local_sems.at[i]))
        for cp in locs + sends:
            cp.start()
        for cp in recvs:
            cp.wait_recv()
        for cp in sends:
            cp.wait_send()
        for cp in locs:
            cp.wait()

    outs = pl.pallas_call(
        body, name=name, out_shape=out_shape, in_specs=[ANY] * n, out_specs=[ANY] * n,
        scratch_shapes=[pltpu.SemaphoreType.DMA((n * (N_DEV - 1),)), pltpu.SemaphoreType.DMA((n * (N_DEV - 1),)),
                        pltpu.SemaphoreType.DMA((n,))],
    )(*arrs)
    return list(outs)


def _gather_two_level(arrs, name):
    n = len(arrs)

    def body(*refs):
        ins, outs = refs[:n], refs[n:2 * n]
        send_sems, recv_sems, local_sems = refs[2 * n:]
        x, y, c = lax.axis_index("x"), lax.axis_index("y"), lax.axis_index("c")
        sibling = (x, y, 1 - c)
        chips = [(1 - x, y), (x, 1 - y), (1 - x, 1 - y)]

        def slot(px, py, pc):
            return 4 * px + 2 * py + pc

        first, passed, waits, locs = [], [], [], []
        for i in range(n):
            def copy(k, block, to, src=None, i=i):
                dst = outs[i].at[slot(*block)]
                return pltpu.make_async_remote_copy(
                    src_ref=dst if src is None else src, dst_ref=dst, send_sem=send_sems.at[7 * i + k],
                    recv_sem=recv_sems.at[7 * i + k], device_id=to, device_id_type=MESH)

            locs.append(pltpu.make_async_copy(ins[i], outs[i].at[slot(x, y, c)], local_sems.at[i]))
            first.append(copy(0, (x, y, c), sibling, src=ins[i]))
            first += [copy(1 + j, (x, y, c), (*chip, c), src=ins[i]) for j, chip in enumerate(chips)]
            passed.append([copy(4 + j, (*chip, c), sibling) for j, chip in enumerate(chips)])
            waits.append([copy(1 + j, (*chip, c), (x, y, c)) for j, chip in enumerate(chips)])
        for cp in locs + first:
            cp.start()
        for j in range(3):
            for i in range(n):
                waits[i][j].wait_recv()
                passed[i][j].start()
        for i in range(n):
            def arrival(k, block, i=i):
                dst = outs[i].at[slot(*block)]
                return pltpu.make_async_remote_copy(
                    src_ref=dst, dst_ref=dst, send_sem=send_sems.at[7 * i + k], recv_sem=recv_sems.at[7 * i + k],
                    device_id=sibling, device_id_type=MESH)

            arrival(0, (x, y, 1 - c)).wait_recv()
            for j, chip in enumerate(chips):
                arrival(4 + j, (*chip, 1 - c)).wait_recv()
        for cp in first + [p for ps in passed for p in ps]:
            cp.wait_send()
        for cp in locs:
            cp.wait()

    outs = pl.pallas_call(
        body, name=name, out_shape=[_sds((N_DEV,) + a.shape, a.dtype) for a in arrs], in_specs=[ANY] * n,
        out_specs=[ANY] * n,
        scratch_shapes=[pltpu.SemaphoreType.DMA((7 * n,)), pltpu.SemaphoreType.DMA((7 * n,)),
                        pltpu.SemaphoreType.DMA((n,))],
    )(*arrs)
    return list(outs)


HBM = pl.BlockSpec(memory_space=pltpu.HBM)
SEM = pl.BlockSpec(memory_space=pltpu.SEMAPHORE)
EFFECT = pltpu.SideEffectType.DATAFLOW_SIDE_EFFECTING


def _peer(k, chips=False):
    x, y, c = lax.axis_index("x"), lax.axis_index("y"), lax.axis_index("c")
    bits = (k + 1) << 1 if chips else k + 1
    px, py, pc = x ^ ((bits >> 2) & 1), y ^ ((bits >> 1) & 1), c ^ (bits & 1)
    if chips:
        return (px, py, pc), 2 * px + py, 2 * x + y
    return (px, py, pc), 4 * px + 2 * py + pc, 4 * x + 2 * y + c


def _exchange_start(arrs, gather, name, after=None, chips=False):
    n = len(arrs)
    n_peer, n_slot = (3, 4) if chips else (N_DEV - 1, N_DEV)
    modes = [gather] * n if isinstance(gather, bool) else list(gather)
    lands = [pltpu.with_memory_space_constraint(lax.empty(((n_slot,) + a.shape) if g else a.shape, a.dtype), pltpu.HBM)
             for a, g in zip(arrs, modes)]
    srcs = [pltpu.with_memory_space_constraint(a, pltpu.HBM) for a in arrs]

    extra = [] if after is None else [after]

    def body(*refs):
        ins, zones = refs[:n], refs[n:2 * n]
        send_sems, recv_sems, local_sems = refs[2 * n + len(extra):2 * n + len(extra) + 3]
        token = refs[-1]
        for i in range(n):
            gather = modes[i]
            for k in range(n_peer):
                dev, peer, me = _peer(k, chips)
                sem = i * n_peer + k
                pltpu.make_async_remote_copy(
                    src_ref=ins[i] if gather else ins[i].at[peer], dst_ref=zones[i].at[me],
                    send_sem=send_sems.at[sem], recv_sem=recv_sems.at[sem], device_id=dev, device_id_type=MESH).start()
            _, _, me = _peer(0, chips)
            pltpu.make_async_copy(ins[i] if gather else ins[i].at[me], zones[i].at[me], local_sems.at[i]).start()
        token[...] = jnp.zeros_like(token)

    nsem = n * n_peer
    outs = pl.pallas_call(
        body, name=name,
        out_shape=[pltpu.SemaphoreType.DMA((nsem,)), pltpu.SemaphoreType.DMA((nsem,)), pltpu.SemaphoreType.DMA((n,))]
        + [pltpu.HBM(a.shape, a.dtype) for a in srcs] + [pltpu.HBM(z.shape, z.dtype) for z in lands]
        + [_sds((8, LANE))],
        in_specs=[HBM] * (2 * n) + [ANY] * len(extra),
        out_specs=[SEM, SEM, SEM] + [HBM] * (2 * n) + [pl.BlockSpec(memory_space=pltpu.VMEM)],
        input_output_aliases={i: 3 + i for i in range(2 * n)},
        compiler_params=pltpu.CompilerParams(has_side_effects=EFFECT),
    )(*srcs, *lands, *extra)
    return {"n": n, "gather": modes, "chips": chips, "sems": outs[:3], "srcs": outs[3:3 + n],
            "lands": outs[3 + n:3 + 2 * n], "token": outs[-1]}


def _exchange_wait(st, after, name, which=None):
    modes, chips = st["gather"], st["chips"]
    afters = list(after) if isinstance(after, (list, tuple)) else [after]
    which = list(range(st["n"])) if which is None else which
    n = len(which)
    n_peer = 3 if chips else N_DEV - 1
    srcs, lands = [st["srcs"][i] for i in which], [st["lands"][i] for i in which]

    def body(*refs):
        ins, zones = refs[:n], refs[n:2 * n]
        send_sems, recv_sems, local_sems = refs[2 * n:2 * n + 3]
        for j, i in enumerate(which):
            gather = modes[i]
            for k in range(n_peer):
                dev, peer, me = _peer(k, chips)
                sem = i * n_peer + k
                src = ins[j] if gather else ins[j].at[peer]
                cp = pltpu.make_async_remote_copy(
                    src_ref=src, dst_ref=zones[j].at[peer], send_sem=send_sems.at[sem], recv_sem=recv_sems.at[sem],
                    device_id=dev, device_id_type=MESH)
                cp.wait_send()
                cp.wait_recv()
            _, _, me = _peer(0, chips)
            pltpu.make_async_copy(ins[j] if gather else ins[j].at[me], zones[j].at[me], local_sems.at[i]).wait()

    outs = pl.pallas_call(
        body, name=name,
        out_shape=[pltpu.HBM(a.shape, a.dtype) for a in srcs] + [pltpu.HBM(z.shape, z.dtype) for z in lands],
        in_specs=[HBM] * (2 * n) + [SEM, SEM, SEM] + [ANY] * len(afters), out_specs=[HBM] * (2 * n),
        input_output_aliases={i: i for i in range(2 * n)},
        compiler_params=pltpu.CompilerParams(has_side_effects=EFFECT),
    )(*srcs, *lands, *st["sems"], *afters)
    return list(outs[n:])


def _pair_reduce(arrs, name, after):
    n = len(arrs)

    def body(*refs):
        ins, refs = refs[:n], refs[n + 1:]
        outs, got, mine = refs[:n], refs[n:2 * n], refs[2 * n:3 * n]
        send_sems, recv_sems, local_sems = refs[3 * n:]
        x, y, c = lax.axis_index("x"), lax.axis_index("y"), lax.axis_index("c")
        sends, locs = [], []
        for i in range(n):
            for q in range(4):
                sem = 4 * i + q
                sends.append(pltpu.make_async_remote_copy(
                    src_ref=ins[i].at[2 * q + 1 - c], dst_ref=got[i].at[q], send_sem=send_sems.at[sem],
                    recv_sem=recv_sems.at[sem], device_id=(x, y, 1 - c), device_id_type=MESH))
                locs.append(pltpu.make_async_copy(ins[i].at[2 * q + c], mine[i].at[q], local_sems.at[sem]))
        for cp in locs + sends:
            cp.start()
        for cp in sends:
            cp.wait_recv()
        for cp in locs:
            cp.wait()
        for i in range(n):
            outs[i][...] = (mine[i][...].astype(F32) + got[i][...].astype(F32)).astype(BF)
        for cp in sends:
            cp.wait_send()

    half = [(4,) + a.shape[1:] for a in arrs]
    outs = pl.pallas_call(
        body, name=name, out_shape=[_sds(h, BF) for h in half], in_specs=[ANY] * (n + 1),
        out_specs=[pl.BlockSpec(memory_space=pltpu.VMEM)] * n,
        scratch_shapes=[pltpu.VMEM(h, BF) for h in half] * 2
        + [pltpu.SemaphoreType.DMA((4 * n,)), pltpu.SemaphoreType.DMA((4 * n,)), pltpu.SemaphoreType.DMA((4 * n,))],
        compiler_params=_cp(),
    )(*arrs, after)
    return list(outs)


def _mod_fwd(crows, w_ada, b_blk):
    def body(c_ref, w_ref, b_ref, o_ref):
        o_ref[...] = mm(silu(c_ref[...]), w_ref[...]) + b_ref[...]

    return pl.pallas_call(body, name="mod_fwd", out_shape=_sds((24, 768)), compiler_params=_cp())(crows, w_ada, b_blk)


def _mod_bwd(crows, w_ada, dmod_blk, dmodc_blk):
    def body(c_ref, w_ref, d_ref, dc_ref, gw_ref, gc_ref, gb_ref):
        cr = c_ref[...]
        dc = dc_ref[0:1, :]
        for p in range(1, N_DEV):
            dc = dc + dc_ref[p:p + 1, :]
        row = lax.broadcasted_iota(jnp.int32, (24, 1), 0)
        gw_ref[...] = mm_tn(silu(cr), jnp.where(row == 16, dc, d_ref[...]))
        cc = cr[16:17, :]
        sg = jax.nn.sigmoid(cc)
        part = mm_nt(jnp.broadcast_to(dc, (8, 768)), w_ref[...])
        gc_ref[...] = part * (sg * (1.0 + cc * (1.0 - sg)))
        gb_ref[...] = jnp.broadcast_to(jnp.sum(d_ref[...], axis=0, keepdims=True) + dc, (8, 768))

    return pl.pallas_call(
        body, name="mod_bwd", out_shape=[_sds((D_MODEL, 768)), _sds((8, D_MODEL)), _sds((8, 768))],
        compiler_params=_cp())(crows, w_ada, dmod_blk, dmodc_blk)


def _tab_specs(tk):
    return [pl.BlockSpec((tk, LANE), lambda i, t: (t, 0))] * 2


def _k1_fwd(x, mod, g_attn, g_q, g_kv, ws, tabs, kv_all, is_ctx):
    b, l, _ = x.shape
    tk = CTX_LEN if is_ctx else TOK
    nt = l // tk
    n_f32 = 2 if is_ctx else 4

    def body(x_ref, mod_ref, ga_ref, gq_ref, gk_ref, wa_ref, wb_ref, wq_ref, wk_ref, cs_ref, sn_ref, *rest):
        outs = rest if is_ctx else rest[2:]
        res = k1_tile(x_ref[...], mod_ref[0:1, :], mod_ref[1:2, :], ga_ref[...], gq_ref[...], gk_ref[...],
                      (wa_ref[...], wb_ref[...], wq_ref[...], wk_ref[...]), (None,) * 4,
                      (cs_ref[...], sn_ref[...]), is_ctx)
        for o_ref, r in zip(outs, res):
            o_ref[...] = r.astype(o_ref.dtype)

    tok = lambda w, off=0: pl.BlockSpec((None, tk, w), lambda i, t: (i, t + off, 0))
    mod_spec = pl.BlockSpec((None, 8, D_MODEL), (lambda i, t: (0, 0, 0)) if is_ctx else (lambda i, t: (i, 0, 0)))
    kv_off = SEQ // tk if is_ctx else 0
    in_specs = ([tok(D_MODEL), mod_spec, _full((1, D_MODEL)), _full((1, 384)), _full((1, 256))]
                + [_full(s) for s in W_SHAPES] + _tab_specs(tk))
    args = [x, mod, g_attn, g_q, g_kv, *ws, *tabs]
    out_specs = [tok(512)] * n_f32 + ([] if is_ctx else [tok(1024)]) + [tok(1024, kv_off), tok(512, kv_off)]
    out_shape = ([_sds((b, l, 512))] * n_f32 + ([] if is_ctx else [_sds((b, l, 1024), BF)])
                 + [_sds((b, KV_LEN, 1024), BF), _sds((b, KV_LEN, 512), BF)])
    aliases = {}
    if not is_ctx:
        aliases = {len(args): n_f32 + 1, len(args) + 1: n_f32 + 2}
        in_specs += [ANY, ANY]
        args += list(kv_all)
    return pl.pallas_call(
        body, name="k1_fwd_ctx" if is_ctx else "k1_fwd", grid=(b, nt), in_specs=in_specs, out_specs=out_specs,
        out_shape=out_shape, input_output_aliases=aliases, compiler_params=_cp((ARB, ARB)),
    )(*args)


N_ACC = 7


def _k1_bwd(x, ctx, mod, mod_c, g_attn, g_q, g_kv, ws, tabs, cts, cts_c, dx_res):
    b, l, _ = x.shape
    tk = TOK_B
    nt = l // tk
    flat = [[a for group in c for a in group] for c in (cts, cts_c)]
    sizes = [[len(g) for g in c] for c in (cts, cts_c)]
    acc_shapes = W_SHAPES + [(1, D_MODEL), (1, 384), (1, 256)]

    def body(*refs):
        it = iter(refs)
        x_ref, c_ref, mod_ref, modc_ref, ga_ref, gq_ref, gk_ref = [next(it) for _ in range(7)]
        w_hbm = [next(it) for _ in range(4)]
        tab_refs = [next(it) for _ in range(2)]
        ct_refs = [[next(it) for _ in f] for f in flat]
        res_ref, gx_ref = next(it), next(it)
        out_hbm = [next(it) for _ in range(N_ACC)]
        dmod_ref, dmodc_ref = next(it), next(it)
        w_vmem = [next(it) for _ in range(4)]
        accs = [next(it) for _ in range(N_ACC)]
        sem = next(it)
        i, t = pl.program_id(0), pl.program_id(1)
        first = jnp.logical_and(i == 0, t == 0)

        @pl.when(first)
        def _():
            for src, dst in zip(w_hbm, w_vmem):
                pltpu.sync_copy(src, dst)
            for k in range(N_ACC):
                accs[k][...] = jnp.zeros(acc_shapes[k], F32)

        def tile(is_ctx):
            which = 1 if is_ctx else 0
            ct_vals, pos = [], 0
            for gsz in sizes[which]:
                v = ct_refs[which][pos][...].astype(F32)
                for r in ct_refs[which][pos + 1:pos + gsz]:
                    v = v + r[...]
                ct_vals.append(v)
                pos += gsz
            wv = tuple(r[...] for r in w_vmem)
            tv = tuple(r[...] for r in tab_refs)
            m_ref = modc_ref if is_ctx else mod_ref

            def f(xv, sh, sc, ga, gq, gk, *probes):
                return k1_tile(xv, sh, sc, ga, gq, gk, wv, probes, tv, is_ctx)

            probes = [jnp.zeros(s, F32) for s in W_SHAPES]
            xin = c_ref[...] if is_ctx else x_ref[...]
            _, vjp = jax.vjp(f, xin, m_ref[0:1, :], m_ref[1:2, :], ga_ref[...], gq_ref[...], gk_ref[...], *probes)
            dx, dsh, dsc, dga, dgq, dgk, dwa, dwb, dwq, dwk = vjp(tuple(ct_vals))
            for ref, val in zip(accs, (dwa, dwb, dwq, dwk, dga, dgq, dgk)):
                ref[...] += val
            return dx, dsh, dsc

        @pl.when(t == 0)
        def _():
            _, dsh, dsc = tile(True)
            _acc(dmodc_ref.at[0:1, :], dsh, i == 0)
            _acc(dmodc_ref.at[1:2, :], dsc, i == 0)

            @pl.when(i == 0)
            def _():
                dmodc_ref[2:8, :] = jnp.zeros((6, D_MODEL), F32)

        @pl.when(t > 0)
        def _():
            dx, dsh, dsc = tile(False)
            gx_ref[...] = dx + res_ref[...]
            _acc(dmod_ref.at[0:1, :], dsh, t == 1)
            _acc(dmod_ref.at[1:2, :], dsc, t == 1)

            @pl.when(t == 1)
            def _():
                dmod_ref[2:8, :] = jnp.zeros((6, D_MODEL), F32)

        @pl.when(jnp.logical_and(i == b - 1, t == nt))
        def _():
            for k in range(4):
                w_vmem[k][...] = accs[k][...].astype(BF)
            cps = [pltpu.make_async_copy(w_vmem[k] if k < 4 else accs[k], out_hbm[k], sem.at[k]) for k in range(N_ACC)]
            for cp in cps:
                cp.start()
            for cp in cps:
                cp.wait()

    lat = lambda w, off=0: pl.BlockSpec((None, tk, w), lambda i, t: (i, jnp.maximum(t - 1, 0) + off, 0))
    con = lambda w, off=0: pl.BlockSpec((None, tk, w), lambda i, t: (i, off, 0))
    mod_spec = pl.BlockSpec((None, 8, D_MODEL), lambda i, t: (i, 0, 0))
    modc_spec = pl.BlockSpec((None, 8, D_MODEL), lambda i, t: (0, 0, 0))
    tab_spec = pl.BlockSpec((tk, LANE), lambda i, t: (jnp.maximum(t - 1, 0), 0))
    in_specs = ([lat(D_MODEL), con(D_MODEL), mod_spec, modc_spec, _full((1, D_MODEL)), _full((1, 384)), _full((1, 256))]
                + [ANY] * 4 + [tab_spec] * 2)
    args = [x, ctx, mod, mod_c, g_attn, g_q, g_kv, *ws, *tabs]
    for a, off in flat[0]:
        in_specs.append(lat(a.shape[-1], off // tk))
        args.append(a)
    for a, off in flat[1]:
        in_specs.append(con(a.shape[-1], off // tk))
        args.append(a)
    in_specs.append(lat(D_MODEL))
    args.append(dx_res)
    out_shape = ([_sds((b, l, D_MODEL))] + [_sds(s, BF) for s in W_SHAPES] + [_sds(s) for s in acc_shapes[4:]]
                 + [_sds((b, 8, D_MODEL)), _sds((1, 8, D_MODEL))])
    out_specs = [lat(D_MODEL)] + [ANY] * N_ACC + [mod_spec, modc_spec]
    outs = pl.pallas_call(
        body, name="k1_bwd", grid=(b, nt + 1), in_specs=in_specs, out_specs=out_specs, out_shape=out_shape,
        scratch_shapes=[pltpu.VMEM(s, BF) for s in W_SHAPES] + [pltpu.VMEM(s, F32) for s in acc_shapes]
        + [pltpu.SemaphoreType.DMA((N_ACC,))],
        compiler_params=_cp((ARB, ARB)),
    )(*args)
    return outs[0], list(outs[1:1 + N_ACC]), outs[1 + N_ACC], outs[2 + N_ACC]


def _chunk_spec(rev):
    if rev:
        return pl.BlockSpec((None, RET_CHUNK, 512), lambda i, n: (i, N_CHUNK - 1 - n, 0))
    return pl.BlockSpec((None, RET_CHUNK, 512), lambda i, n: (i, n, 0))


def _state_spec(rev):
    if rev:
        return pl.BlockSpec((None, N_HEADS, None, LANE, LANE), lambda i, n: (i, 0, N_CHUNK - 1 - n, 0, 0))
    return pl.BlockSpec((None, N_HEADS, None, LANE, LANE), lambda i, n: (i, 0, n, 0, 0))


_CTX_SPEC = pl.BlockSpec((None, CTX_LEN, 512), lambda i, n: (i, 0, 0))
_DEC_SPEC = pl.BlockSpec((N_HEADS, 1, 1), lambda i, n: (0, 0, 0))


def _k2_fwd(rq, rk, rv, rkc, rvc, dec_f, dec_b):
    b = rq.shape[0]

    def body(qf, kf, vf, qb, kb, vb, kc, vc, df, db, of_ref, ob_ref, sf_out, sb_out, sf, sb):
        n = pl.program_id(1)
        for h, sl in enumerate(_HEAD_SL):
            lgf, lgb = log_sigmoid(df[h]), log_sigmoid(db[h])

            @pl.when(n == 0)
            def _():
                sf[h] = ctx_state(kc[:, sl], vc[:, sl], lgf, False)
                sb[h] = ctx_state(kc[:, sl], vc[:, sl], lgb, True)

            sf_out[h] = sf[h]
            sb_out[h] = sb[h]
            o, s = ret_chunk(qf[:, sl], kf[:, sl], vf[:, sl], sf[h], lgf, False)
            of_ref[:, sl] = o
            sf[h] = s
            o, s = ret_chunk(qb[:, sl], kb[:, sl], vb[:, sl], sb[h], lgb, True)
            ob_ref[:, sl] = o
            sb[h] = s

    l = rq.shape[1]
    return pl.pallas_call(
        body, name="k2_fwd", grid=(b, N_CHUNK),
        in_specs=[_chunk_spec(False)] * 3 + [_chunk_spec(True)] * 3 + [_CTX_SPEC, _CTX_SPEC, _DEC_SPEC, _DEC_SPEC],
        out_specs=[_chunk_spec(False), _chunk_spec(True), _state_spec(False), _state_spec(True)],
        out_shape=[_sds((b, l, 512)), _sds((b, l, 512)), _sds((b, N_HEADS, N_CHUNK, LANE, LANE)),
                   _sds((b, N_HEADS, N_CHUNK, LANE, LANE))],
        scratch_shapes=[pltpu.VMEM((N_HEADS, LANE, LANE), F32), pltpu.VMEM((N_HEADS, LANE, LANE), F32)],
        compiler_params=_cp((ARB, ARB)),
    )(rq, rk, rv, rq, rk, rv, rkc, rvc, dec_f, dec_b)


def _k2_bwd(rq, rk, rv, do, sf_prev, sb_prev, rkc, rvc, dec_f, dec_b):
    b, l, _ = rq.shape

    def body(qf, kf, vf, gf, spf, qb, kb, vb, gb, spb, kc, vc, df, db,
             dqf, dkf, dvf, dqb, dkb, dvb, dkc, dvc, ddf, ddb, dsf, dsb):
        n = pl.program_id(1)

        @pl.when(n == 0)
        def _():
            dsf[...] = jnp.zeros((N_HEADS, LANE, LANE), F32)
            dsb[...] = jnp.zeros((N_HEADS, LANE, LANE), F32)

        def one(h, sl, q, k, v, g, sp, dec, ds, dq, dk, dv, dd, rev):
            def f(qv, kv_, vv, sv, dcy):
                return ret_chunk(qv, kv_, vv, sv, log_sigmoid(dcy), rev)

            _, vjp = jax.vjp(f, q[:, sl], k[:, sl], v[:, sl], sp[h], dec[h])
            gq, gk, gv, gs, gd = vjp((g[:, sl], ds[h]))
            dq[:, sl] = gq
            dk[:, sl] = gk
            dv[:, sl] = gv
            ds[h] = gs
            _acc(dd.at[h], jnp.broadcast_to(gd, (8, LANE)), n == 0)

        for h, sl in enumerate(_HEAD_SL):
            one(h, sl, qf, kf, vf, gf, spf, df, dsf, dqf, dkf, dvf, ddf, False)
            one(h, sl, qb, kb, vb, gb, spb, db, dsb, dqb, dkb, dvb, ddb, True)

        @pl.when(n == N_CHUNK - 1)
        def _():
            def f(kcv, vcv, dcy, rev):
                return ctx_state(kcv, vcv, log_sigmoid(dcy), rev)

            for h, sl in enumerate(_HEAD_SL):
                _, vjp_f = jax.vjp(functools.partial(f, rev=False), kc[:, sl], vc[:, sl], df[h])
                gk_f, gv_f, gd_f = vjp_f(dsf[h])
                _, vjp_b = jax.vjp(functools.partial(f, rev=True), kc[:, sl], vc[:, sl], db[h])
                gk_b, gv_b, gd_b = vjp_b(dsb[h])
                dkc[:, sl] = gk_f + gk_b
                dvc[:, sl] = gv_f + gv_b
                ddf[h] += jnp.broadcast_to(gd_f, (8, LANE))
                ddb[h] += jnp.broadcast_to(gd_b, (8, LANE))

    dd_spec = pl.BlockSpec((None, N_HEADS, 8, LANE), lambda i, n: (i, 0, 0, 0))
    return pl.pallas_call(
        body, name="k2_bwd", grid=(b, N_CHUNK),
        in_specs=[_chunk_spec(True)] * 4 + [_state_spec(True)] + [_chunk_spec(False)] * 4 + [_state_spec(False)]
        + [_CTX_SPEC, _CTX_SPEC, _DEC_SPEC, _DEC_SPEC],
        out_specs=[_chunk_spec(True)] * 3 + [_chunk_spec(False)] * 3 + [_CTX_SPEC, _CTX_SPEC, dd_spec, dd_spec],
        out_shape=[_sds((b, l, 512))] * 6 + [_sds((b, CTX_LEN, 512))] * 2 + [_sds((b, N_HEADS, 8, LANE))] * 2,
        scratch_shapes=[pltpu.VMEM((N_HEADS, LANE, LANE), F32), pltpu.VMEM((N_HEADS, LANE, LANE), F32)],
        compiler_params=_cp((ARB, ARB)),
    )(rq, rk, rv, do, sf_prev, rq, rk, rv, do, sb_prev, rkc, rvc, dec_f, dec_b)


TQ = 1024
TQ_F = 512
QK_W = 2 * LANE
N_QP = 2
_Q_PARTS = [slice(i * TQ_F // N_QP, (i + 1) * TQ_F // N_QP) for i in range(N_QP)]


SM_SCALE = 1.0 / math.sqrt(192.0)
LOG2_E = 1.0 / math.log(2.0)


def _k3_specs(tq):
    qs = lambda w: pl.BlockSpec((None, tq, w), lambda i, h, t: (i, t, h))
    ks = lambda w: pl.BlockSpec((None, KV_LEN, w), lambda i, h, t: (i, 0, h))
    return qs, ks


def _k3_fwd(q, k, v):
    b, l, _ = q.shape

    def body(q_ref, k_ref, v_ref, o_ref, lse_ref):
        kv_, vv = k_ref[...], v_ref[...]
        for r in _Q_PARTS:
            s = _dot(q_ref[r, :], kv_, 1, 1)
            m = jnp.max(s, axis=-1, keepdims=True)
            e = jnp.exp2((s - m) * (SM_SCALE * LOG2_E))
            tot = jnp.sum(e, axis=-1, keepdims=True)
            o_ref[r, :] = _dot(e, vv, 1, 0) * (1.0 / tot)
            lse_ref[r, :] = jnp.broadcast_to(m * SM_SCALE + jnp.log(tot), (TQ_F // N_QP, LANE))

    qs, ks = _k3_specs(TQ_F)
    return pl.pallas_call(
        body, name="k3_fwd", grid=(b, N_HEADS, l // TQ_F), in_specs=[qs(QK_W), ks(QK_W), ks(LANE)],
        out_specs=[qs(LANE), qs(LANE)], out_shape=[_sds((b, l, N_HEADS * LANE))] * 2,
        compiler_params=_cp((ARB, ARB, ARB)),
    )(q, k, v)


def _k3_bwd(q, k, v, o, lse, dy, after):
    b, l, _ = q.shape

    def body(q_ref, k_ref, v_ref, o_ref, lse_ref, dy_ref, after_ref, dq_ref, dk_ref, dv_ref):
        t0 = pl.program_id(2) == 0
        kv_, vv = k_ref[...], v_ref[...]
        qv, dyv = q_ref[...], dy_ref[...]
        g = dyv.astype(BF)
        lse_col = jnp.max(lse_ref[...], axis=-1, keepdims=True)
        delta = jnp.sum(dyv * o_ref[...], axis=-1, keepdims=True)
        p = jnp.exp2(_dot(qv, kv_, 1, 1) * (SM_SCALE * LOG2_E) - lse_col * LOG2_E)
        ds = (p * (_dot(g, vv, 1, 1) - delta) * SM_SCALE).astype(BF)
        _acc(dv_ref, _dot(p, g, 0, 0), t0)
        dq_ref[...] = _dot(ds, kv_, 1, 0)
        _acc(dk_ref, _dot(ds, qv, 0, 0), t0)

    qs, ks = _k3_specs(TQ)
    return pl.pallas_call(
        body, name="k3_bwd", grid=(b, N_HEADS, l // TQ),
        in_specs=[qs(QK_W), ks(QK_W), ks(LANE), qs(LANE), qs(LANE), qs(LANE), ANY],
        out_specs=[qs(QK_W), ks(QK_W), ks(LANE)],
        out_shape=[_sds((b, l, N_HEADS * QK_W)), _sds((b, KV_LEN, N_HEADS * QK_W)), _sds((b, KV_LEN, N_HEADS * LANE))],
        compiler_params=_cp((ARB, ARB, ARB)),
    )(q, k, v, o, lse, dy, after)


def _mod_rows(mod_ref, rows):
    return [mod_ref[r:r + 1, :] for r in rows]


def _k4a_fwd(x, o_f, o_b, rg, y_mla, g_ret, w_out, mod, g_ffn):
    b, l, _ = x.shape

    def body(x_ref, of_ref, ob_ref, rg_ref, ym_ref, gr_ref, wo_ref, mod_ref, gf_ref, xm_ref, h2_ref):
        gt_a, sh_f, sc_f = _mod_rows(mod_ref, (2, 3, 4))
        x_mid, h2 = k4a_tile(x_ref[...], of_ref[...], ob_ref[...], rg_ref[...], ym_ref[...], gr_ref[...], gt_a,
                             gf_ref[...], sh_f, sc_f, wo_ref[...], None)
        xm_ref[...] = x_mid
        h2_ref[...] = h2.astype(BF)

    tok = lambda w: pl.BlockSpec((None, TOK, w), lambda i, t: (i, t, 0))
    mod_spec = pl.BlockSpec((None, 8, D_MODEL), lambda i, t: (i, 0, 0))
    return pl.pallas_call(
        body, name="k4a_fwd", grid=(b, l // TOK),
        in_specs=[tok(D_MODEL), tok(512), tok(512), tok(512), tok(512), _full((1, 512)), _full((D_MODEL, D_MODEL)),
                  mod_spec, _full((1, D_MODEL))],
        out_specs=[tok(D_MODEL), tok(D_MODEL)], out_shape=[_sds((b, l, D_MODEL)), _sds((b, l, D_MODEL), BF)],
        compiler_params=_cp((ARB, ARB)),
    )(x, o_f, o_b, rg, y_mla, g_ret, w_out, mod, g_ffn)


TOK_M = 512
TOK_D = 2048
HALF_FF = D_FF // 2


def _k4b_mlp_loss(h2, w1t, w2, x_mid, mod, g_final, tgt):
    b, l, _ = h2.shape
    nt = l // TOK_M

    def body(h2_ref, w1_hbm, w2_hbm, xm_ref, mod_ref, gfin_ref, tgt_ref, dxm_ref, dmlp_ref, r_ref, loss_ref, dgt_ref,
             dgfin_ref, w1_v, w2_v):
        i, t = pl.program_id(0), pl.program_id(1)
        first = jnp.logical_and(i == 0, t == 0)

        @pl.when(first)
        def _():
            pltpu.sync_copy(w1_hbm, w1_v)
            pltpu.sync_copy(w2_hbm, w2_v)

        h2v = h2_ref[...]
        mlp = None
        for half in range(2):
            rows = slice(half * HALF_FF, (half + 1) * HALF_FF)
            r = jnp.maximum(_dot(h2v, w1_v[rows, :], 1, 1), 0.0)
            r_ref[:, rows] = r.astype(BF)
            part = _dot(jnp.square(r), w2_v[rows, :], 1, 0)
            mlp = part if mlp is None else mlp + part
        (gt_f,) = _mod_rows(mod_ref, (5,))
        loss, vjp = jax.vjp(k4c_tile, xm_ref[...], mlp, gt_f, gfin_ref[...], tgt_ref[...])
        dxm, dmlp, dgt, dgfin, _ = vjp(jnp.ones((1, 1), F32))
        dxm_ref[...] = dxm
        dmlp_ref[...] = dmlp.astype(BF)
        _acc(loss_ref, jnp.broadcast_to(loss, (8, LANE)), first)
        _acc(dgfin_ref, dgfin, first)
        _acc(dgt_ref, dgt, t == 0)

    tok = lambda w: pl.BlockSpec((None, TOK_M, w), lambda i, t: (i, t, 0))
    return pl.pallas_call(
        body, name="k4b_mlp_loss", grid=(b, nt),
        in_specs=[tok(D_MODEL), ANY, ANY, tok(D_MODEL), pl.BlockSpec((None, 8, D_MODEL), lambda i, t: (i, 0, 0)),
                  _full((1, D_MODEL)), tok(D_MODEL)],
        out_specs=[tok(D_MODEL), tok(D_MODEL), tok(D_FF), _full((8, LANE)),
                   pl.BlockSpec((None, 1, D_MODEL), lambda i, t: (i, 0, 0)), _full((1, D_MODEL))],
        out_shape=[_sds((b, l, D_MODEL)), _sds((b, l, D_MODEL), BF), _sds((b, l, D_FF), BF), _sds((8, LANE)),
                   _sds((b, 1, D_MODEL)), _sds((1, D_MODEL))],
        scratch_shapes=[pltpu.VMEM((D_FF, D_MODEL), BF), pltpu.VMEM((D_FF, D_MODEL), BF)],
        compiler_params=_cp((ARB, ARB)),
    )(h2, w1t, w2, x_mid, mod, g_final, tgt)


def _k4d_mlp_bwd(h2, dmlp, r, w2):
    b, l, _ = h2.shape
    nt = l // TOK_D

    def body(h2_ref, dm_ref, r_ref, w2_ref, da_ref, dw1_ref, dw2_ref, acc1, acc2):
        i, t = pl.program_id(1), pl.program_id(2)
        first = jnp.logical_and(i == 0, t == 0)
        rv = r_ref[...].astype(F32)
        dm = dm_ref[...]
        da = (_dot(dm, w2_ref[...], 1, 1) * (2.0 * rv)).astype(BF)
        da_ref[...] = da
        _acc(acc2, _dot(jnp.square(rv), dm, 0, 0), first)
        _acc(acc1, _dot(h2_ref[...], da, 0, 0), first)

        @pl.when(jnp.logical_and(i == b - 1, t == nt - 1))
        def _():
            dw1_ref[...] = acc1[...].astype(BF)
            dw2_ref[...] = acc2[...].astype(BF)

    tok = lambda w: pl.BlockSpec((None, TOK_D, w), lambda j, i, t: (i, t, 0))
    col = pl.BlockSpec((None, TOK_D, FF_BLK), lambda j, i, t: (i, t, j))
    return pl.pallas_call(
        body, name="k4d_mlp_bwd", grid=(N_DEV, b, nt),
        in_specs=[tok(D_MODEL), tok(D_MODEL), col, pl.BlockSpec((None, FF_BLK, D_MODEL), lambda j, i, t: (j, 0, 0))],
        out_specs=[col, pl.BlockSpec((None, D_MODEL, FF_BLK), lambda j, i, t: (j, 0, 0)),
                   pl.BlockSpec((None, FF_BLK, D_MODEL), lambda j, i, t: (j, 0, 0))],
        out_shape=[_sds((b, l, D_FF), BF), _sds((N_DEV, D_MODEL, FF_BLK), BF), _sds((N_DEV, FF_BLK, D_MODEL), BF)],
        scratch_shapes=[pltpu.VMEM((D_MODEL, FF_BLK), F32), pltpu.VMEM((FF_BLK, D_MODEL), F32)],
        compiler_params=_cp((ARB, ARB, ARB)),
    )(h2, dmlp, r, w2)


def _k4e_bwd(x, o_f, o_b, rg, y_mla, g_ret, w_out, mod, g_ffn, dxm, da, w1t):
    b, l, _ = x.shape

    def body(x_ref, of_ref, ob_ref, rg_ref, ym_ref, gr_ref, wo_ref, mod_ref, gf_ref, dxm_ref, da_ref, w1_hbm,
             dx_ref, do_ref, drg_ref, dym_ref, dwo_ref, dgr_ref, dgf_ref, dmod_ref, w1_v, dwo_acc):
        i, t = pl.program_id(0), pl.program_id(1)
        first = jnp.logical_and(i == 0, t == 0)

        @pl.when(first)
        def _():
            pltpu.sync_copy(w1_hbm, w1_v)

        gt_a, sh_f, sc_f = _mod_rows(mod_ref, (2, 3, 4))
        wo = wo_ref[...]
        dh2 = _dot(da_ref[...], w1_v[...], 1, 0)

        def f(xv, ofv, rgv, ymv, grv, gta, gfv, shf, scf, p_out):
            return k4a_tile(xv, ofv, ob_ref[...], rgv, ymv, grv, gta, gfv, shf, scf, wo, p_out)

        _, vjp = jax.vjp(f, x_ref[...], of_ref[...], rg_ref[...], ym_ref[...], gr_ref[...], gt_a, gf_ref[...], sh_f,
                         sc_f, jnp.zeros((D_MODEL, D_MODEL), F32))
        dx, do, drg, dym, dgr, dgta, dgf, dshf, dscf, dwo = vjp((dxm_ref[...], dh2))
        dx_ref[...] = dx
        do_ref[...] = do
        drg_ref[...] = drg
        dym_ref[...] = dym
        _acc(dwo_acc, dwo, first)
        _acc(dgr_ref, dgr, first)
        _acc(dgf_ref, dgf, first)
        t0 = t == 0
        _acc(dmod_ref.at[2:3, :], dgta, t0)
        _acc(dmod_ref.at[3:4, :], dshf, t0)
        _acc(dmod_ref.at[4:5, :], dscf, t0)

        @pl.when(t0)
        def _():
            dmod_ref[0:2, :] = jnp.zeros((2, D_MODEL), F32)
            dmod_ref[5:8, :] = jnp.zeros((3, D_MODEL), F32)

        @pl.when(jnp.logical_and(i == b - 1, t == l // TOK_B - 1))
        def _():
            dwo_ref[...] = dwo_acc[...].astype(BF)

    tok = lambda w: pl.BlockSpec((None, TOK_B, w), lambda i, t: (i, t, 0))
    mod_spec = pl.BlockSpec((None, 8, D_MODEL), lambda i, t: (i, 0, 0))
    return pl.pallas_call(
        body, name="k4e_bwd", grid=(b, l // TOK_B),
        in_specs=[tok(D_MODEL), tok(512), tok(512), tok(512), tok(512), _full((1, 512)), _full((D_MODEL, D_MODEL)),
                  mod_spec, _full((1, D_MODEL)), tok(D_MODEL), tok(D_FF), ANY],
        out_specs=[tok(D_MODEL), tok(512), tok(512), tok(512), _full((D_MODEL, D_MODEL)), _full((1, 512)),
                   _full((1, D_MODEL)), mod_spec],
        out_shape=[_sds((b, l, D_MODEL)), _sds((b, l, 512)), _sds((b, l, 512)), _sds((b, l, 512)),
                   _sds((D_MODEL, D_MODEL), BF), _sds((1, 512)), _sds((1, D_MODEL)), _sds((b, 8, D_MODEL))],
        scratch_shapes=[pltpu.VMEM((D_FF, D_MODEL), BF), pltpu.VMEM((D_MODEL, D_MODEL), F32)],
        compiler_params=_cp((ARB, ARB)),
    )(x, o_f, o_b, rg, y_mla, g_ret, w_out, mod, g_ffn, dxm, da, w1t)


ADAM_BLOCK_BYTES = 32 * 1024 * 1024


def _adamw(w, m, v, pieces, name, after=None):
    r, c = w.shape
    npc = pieces.shape[0]
    per_row = c * (7 * 4 + npc * pieces.dtype.itemsize) * 2
    rb = max(d for d in range(8, r + 1, 8) if r % d == 0 and d * per_row <= ADAM_BLOCK_BYTES)

    def body(w_ref, m_ref, v_ref, p_ref, *rest):
        g_ref, d_ref, nm_ref, nv_ref = rest[-4:]
        g = p_ref[0].astype(F32)
        for k in range(1, npc):
            g = g + p_ref[k].astype(F32)
        wv = w_ref[...]
        mn = ADAM_B1 * m_ref[...] + (1.0 - ADAM_B1) * g
        vn = ADAM_B2 * v_ref[...] + (1.0 - ADAM_B2) * jnp.square(g)
        m_hat = mn / (1.0 - ADAM_B1 ** ADAM_STEP)
        v_hat = vn / (1.0 - ADAM_B2 ** ADAM_STEP)
        g_ref[...] = g
        d_ref[...] = -ADAM_LR * (m_hat / (jnp.sqrt(v_hat) + ADAM_EPS) + ADAM_WD * wv)
        nm_ref[...] = mn
        nv_ref[...] = vn

    blk = pl.BlockSpec((rb, c), lambda i: (i, 0))
    extra = [] if after is None else [after]
    return pl.pallas_call(
        body, name=name, grid=(r // rb,),
        in_specs=[blk, blk, blk, pl.BlockSpec((npc, rb, c), lambda i: (0, i, 0))] + [ANY] * len(extra),
        out_specs=[blk] * 4, out_shape=[_sds((r, c))] * 4, compiler_params=_cp((ARB,)),
    )(w, m, v, pieces, *extra)


def _pad_rot_rows(w, zero):
    k = w.shape[1]
    return lax.pad(w.reshape(-1, 2, 32, k), zero, ((0, 0, 0), (0, 0, 0), (0, 32, 0), (0, 0, 0))).reshape(-1, k)


def _cut_rot_rows(g):
    k = g.shape[1]
    return g.reshape(-1, 2, 64, k)[:, :, :32].reshape(-1, k)


def _w_in_pad(wt, zero):
    w_a = jnp.concatenate([_pad_rot_rows(wt[0:512], zero), wt[512:1536]], axis=0)
    w_b = jnp.concatenate([wt[1536:2176], _pad_rot_rows(wt[2176:2240], zero)], axis=0)
    return w_a, w_b


def _w_in_cut(g_a, g_b):
    return jnp.concatenate([_cut_rot_rows(g_a[0:1024]), g_a[1024:2048], g_b[0:640], _cut_rot_rows(g_b[640:768])], axis=0)


def _w_uq_pad(wt, zero):
    w = wt.reshape(N_HEADS, 192, 384)
    rot = _pad_rot_rows(w[:, 128:].reshape(N_HEADS * 64, 384), zero).reshape(N_HEADS, LANE, 384)
    return jnp.concatenate([w[:, :128], rot], axis=1).reshape(1024, 384)


def _w_uq_cut(g):
    g = g.reshape(N_HEADS, 256, 384)
    rot = _cut_rot_rows(g[:, 128:].reshape(N_HEADS * LANE, 384)).reshape(N_HEADS, 64, 384)
    return jnp.concatenate([g[:, :128], rot], axis=1).reshape(768, 384)


def _w_ukv_perm(wt):
    return jnp.transpose(wt.reshape(N_HEADS, 2, LANE, 256), (1, 0, 2, 3)).reshape(1024, 256)


def _w_ukv_unperm(g):
    return jnp.transpose(g.reshape(2, N_HEADS, LANE, 256), (1, 0, 2, 3)).reshape(1024, 256)


def _unshard_cols(g):
    return jnp.transpose(g, (1, 0, 2)).reshape(g.shape[1], N_DEV * g.shape[2])


def _rope_tables():
    rows = SEQ // GRID_W
    row = jnp.repeat(jnp.arange(rows, dtype=F32), GRID_W)
    col = jnp.tile(jnp.arange(GRID_W, dtype=F32), rows)
    freq = ROPE_BASE ** (-jnp.arange(16, dtype=F32) / 16)
    ang = jnp.concatenate([row[:, None] * freq, col[:, None] * freq], axis=-1)
    cos, sin = jnp.cos(ang), jnp.sin(ang)
    z = jnp.zeros((SEQ, 32), F32)
    return jnp.concatenate([cos, z, cos, z], axis=1), jnp.concatenate([-sin, z, sin, z], axis=1)


_PACKED = (("g_attn", 1024), ("g_ffn", 1024), ("ret_decay_fwd", 4), ("ret_decay_bwd", 4), ("g_ret", 512),
           ("g_q_lora", 384), ("g_kv_lora", 256), ("g_final", 1024))
_PACK_OFF = {}
_off = 0
for _name, _n in _PACKED:
    _PACK_OFF[_name] = _off
    _off += -(-_n // LANE) * LANE
PACK_W = _off


def _pack_small(vals):
    parts = []
    for name, n in _PACKED:
        a = vals[name].reshape(-1).astype(F32)
        parts.append(jnp.pad(a, (0, -(-n // LANE) * LANE - n)))
    return jnp.concatenate(parts).reshape(1, PACK_W)


def _adamw_small(params, packed, gcc, gb_ada):
    names = list(params)
    n_p = len(names)

    def body(*refs):
        p_ref, gcc_ref, gb_ref = refs[3 * n_p:3 * n_p + 3]
        outs = refs[3 * n_p + 3:]
        for k, name in enumerate(names):
            w_ref, m_ref, v_ref = refs[3 * k:3 * k + 3]
            n = w_ref.shape[1]
            if name == "b_ada":
                g = jnp.concatenate([gb_ref[d, 0:1, :] for d in range(N_DEV)], axis=-1)
            elif name == "c_ctx":
                g = gcc_ref[0, 0:1, :]
                for d in range(1, N_DEV):
                    g = g + gcc_ref[d, 0:1, :]
            else:
                off = _PACK_OFF[name]
                g = p_ref[0, :, off:off + n]
                for d in range(1, N_DEV):
                    g = g + p_ref[d, :, off:off + n]
            mn = ADAM_B1 * m_ref[...] + (1.0 - ADAM_B1) * g
            vn = ADAM_B2 * v_ref[...] + (1.0 - ADAM_B2) * jnp.square(g)
            m_hat = mn / (1.0 - ADAM_B1 ** ADAM_STEP)
            v_hat = vn / (1.0 - ADAM_B2 ** ADAM_STEP)
            outs[4 * k][...] = g
            outs[4 * k + 1][...] = -ADAM_LR * (m_hat / (jnp.sqrt(v_hat) + ADAM_EPS) + ADAM_WD * w_ref[...])
            outs[4 * k + 2][...] = mn
            outs[4 * k + 3][...] = vn

    args = [a for name in names for a in params[name]] + [packed, gcc, gb_ada]
    out_shape = [_sds(params[name][0].shape) for name in names for _ in range(4)]
    outs = pl.pallas_call(body, name="adamw_small", out_shape=out_shape, compiler_params=_cp())(*args)
    return {name: list(outs[4 * k:4 * k + 4]) for k, name in enumerate(names)}


def kernel(x, c, ctx, c_ctx, w_ada, b_ada, g_attn, g_ffn, w_in, ret_decay_fwd, ret_decay_bwd, g_ret, g_q_lora, w_uq, g_kv_lora, w_ukv, w_out, w_ff1, w_ff2, g_final, loss_target, m_c_ctx, m_w_ada, m_b_ada, m_g_attn, m_g_ffn, m_w_in, m_ret_decay_fwd, m_ret_decay_bwd, m_g_ret, m_g_q_lora, m_w_uq, m_g_kv_lora, m_w_ukv, m_w_out, m_w_ff1, m_w_ff2, m_g_final, v_c_ctx, v_w_ada, v_b_ada, v_g_attn, v_g_ffn, v_w_in, v_ret_decay_fwd, v_ret_decay_bwd, v_g_ret, v_g_q_lora, v_w_uq, v_g_kv_lora, v_w_ukv, v_w_out, v_w_ff1, v_w_ff2, v_g_final):
    me = 4 * lax.axis_index("x") + 2 * lax.axis_index("y") + lax.axis_index("c")
    nb = x.shape[0]

    c_pad = jnp.pad(c, ((0, 8 - nb), (0, 0)))
    c_all, g_in, g_uq, g_ukv = _gather_two_level(
        [c_pad, w_in[0].T.astype(BF), w_uq[0].T.astype(BF), w_ukv[0].T.astype(BF)], "gather_weights")

    crows = jnp.concatenate([c_all[:, :nb].reshape(N_DEV * nb, D_MODEL), c_ctx[None], jnp.zeros((7, D_MODEL), F32)])
    b_blk = lax.dynamic_slice(b_ada, (0, me * 768), (1, 768))
    st_f = _exchange_start([_mod_fwd(crows, w_ada[0], b_blk), w_out[0].astype(BF), w_ff1[0].T.astype(BF),
                            w_ff2[0].astype(BF)], True, "gather_fwd_start")
    zero = st_f["token"][0, 0].astype(BF)
    ws = (*_w_in_pad(g_in.reshape(2240, D_MODEL), zero), _w_uq_pad(g_uq.reshape(768, 384), zero),
          _w_ukv_perm(g_ukv.reshape(1024, 256)))
    (mod_g,) = _exchange_wait(st_f, ws, "gather_mod_wait", [0])
    mod_all = _unshard_cols(mod_g)
    mod_mine = lax.dynamic_slice(mod_all, (me * nb, 0), (nb, 6 * D_MODEL)).reshape(nb, 6, D_MODEL)
    mod = jnp.pad(mod_mine, ((0, 0), (0, 2), (0, 0)))
    mod_c = jnp.pad(mod_all[16].reshape(1, 6, D_MODEL), ((0, 0), (0, 2), (0, 0)))

    tabs = _rope_tables()
    dec_f = ret_decay_fwd.reshape(N_HEADS, 1, 1)
    dec_b = ret_decay_bwd.reshape(N_HEADS, 1, 1)

    rkc, rvc, k_ctx, v_ctx = _k1_fwd(ctx, mod_c, g_attn, g_q_lora, g_kv_lora, ws, tabs, None, True)
    rq, rk, rv, rg, q, k_all, v_all = _k1_fwd(x, mod, g_attn, g_q_lora, g_kv_lora, ws, tabs, (k_ctx, v_ctx), False)
    o_f, o_b, sf_prev, sb_prev = _k2_fwd(rq, rk, rv, rkc, rvc, dec_f, dec_b)
    y_mla, lse = _k3_fwd(q, k_all, v_all)
    (g_out,) = _exchange_wait(st_f, y_mla, "gather_wo_wait", [1])
    wo = g_out.reshape(D_MODEL, D_MODEL)
    x_mid, h2 = _k4a_fwd(x, o_f, o_b, rg, y_mla, g_ret, wo, mod, g_ffn)
    g_ff1t, g_ff2 = _exchange_wait(st_f, x_mid, "gather_ff_wait", [2, 3])
    w1t = g_ff1t.reshape(D_FF, D_MODEL)
    dxm, dmlp, relu_a, loss_acc, dgt_f, dg_final = _k4b_mlp_loss(h2, w1t, g_ff2.reshape(D_FF, D_MODEL), x_mid, mod,
                                                                 g_final.reshape(1, D_MODEL), loss_target)

    da, dw1, dw2 = _k4d_mlp_bwd(h2, dmlp, relu_a, g_ff2)
    dx_res, do, drg, dym, dwo, dg_ret, dg_ffn, dmod_a = _k4e_bwd(x, o_f, o_b, rg, y_mla, g_ret, wo, mod, g_ffn, dxm, da,
                                                                 w1t)
    st_s = _exchange_start([dw1, dw2, dwo.reshape(N_DEV, 128, D_MODEL)], False, "scatter_grads_start")
    dq, dk_all, dv_all = _k3_bwd(q, k_all, v_all, y_mla, lse, dym, st_s["token"])
    dqf, dkf, dvf, dqb, dkb, dvb, dkc, dvc, ddf, ddb = _k2_bwd(rq, rk, rv, do, sf_prev, sb_prev, rkc, rvc, dec_f, dec_b)
    cts = [[(dqf, 0), (dqb, 0)], [(dkf, 0), (dkb, 0)], [(dvf, 0), (dvb, 0)], [(drg, 0)], [(dq, 0)],
           [(dk_all, 0)], [(dv_all, 0)]]
    cts_c = [[(dkc, 0)], [(dvc, 0)], [(dk_all, SEQ)], [(dv_all, SEQ)]]
    grad_x, accs, dmod_1, dmod_c1 = _k1_bwd(x, ctx, mod, mod_c, g_attn, g_q_lora, g_kv_lora, ws, tabs, cts, cts_c,
                                            dx_res)
    dwa, dwb, dwq, dwk, dg_attn, dg_q, dg_kv = accs

    dmod_loc = (dmod_a + dmod_1).at[:, 5, :].set(dgt_f[:, 0, :])[:, :6, :].reshape(nb, 6 * D_MODEL)
    dmod_ctx = dmod_c1[:, :6, :].reshape(1, 6 * D_MODEL)
    small = {"g_attn": dg_attn, "g_ffn": dg_ffn, "ret_decay_fwd": jnp.sum(ddf[:, :, 0, 0], axis=0),
             "ret_decay_bwd": jnp.sum(ddb[:, :, 0, 0], axis=0), "g_ret": dg_ret, "g_q_lora": dg_q, "g_kv_lora": dg_kv,
             "g_final": dg_final}
    extra = jnp.concatenate([dmod_loc, dmod_ctx, jnp.zeros((5, 6 * D_MODEL), F32)])
    ex_pieces = jnp.transpose(extra.reshape(8, N_DEV, 768), (1, 0, 2))
    st_sm = _exchange_start([_pack_small(small), ex_pieces, loss_acc], [True, False, True], "gather_small_start")
    chip_sums = _pair_reduce([_w_in_cut(dwa, dwb).reshape(N_DEV, 280, D_MODEL), _w_uq_cut(dwq).reshape(N_DEV, 96, 384),
                              _w_ukv_unperm(dwk).reshape(N_DEV, 128, 256)], "pair_reduce", st_sm["token"])
    sm_g, ex_g, loss_g = _exchange_wait(st_sm, chip_sums[0], "gather_small_wait")
    dmod_blk = jnp.concatenate([ex_g[:, :nb].reshape(N_DEV * nb, 768), jnp.zeros((8, 768), F32)])
    gw_ada, gcc_part, gb_part = _mod_bwd(crows, w_ada[0], dmod_blk, ex_g[:, nb])
    st_c = _exchange_start([gcc_part, gb_part], True, "gather_cc_start")
    p_ff1, p_ff2, p_wo = _exchange_wait(st_s, st_c["token"], "scatter_grads_wait")
    st_r = _exchange_start(chip_sums, False, "scatter_rest_start", after=p_wo, chips=True)

    res = {}
    early = (("w_ff1", w_ff1, m_w_ff1, v_w_ff1, p_ff1), ("w_ff2", w_ff2, m_w_ff2, v_w_ff2, p_ff2),
             ("w_ada", w_ada, m_w_ada, v_w_ada, gw_ada[None]), ("w_out", w_out, m_w_out, v_w_out, p_wo))
    behind = st_r["token"]
    for name, w, m, v, pcs in early:
        res[name] = [a[None] for a in _adamw(w[0], m[0], v[0], pcs, "adamw_" + name, after=behind)]
        behind = res[name][3]

    smalls = {"c_ctx": (c_ctx, m_c_ctx, v_c_ctx), "b_ada": (b_ada, m_b_ada, v_b_ada), "g_attn": (g_attn, m_g_attn, v_g_attn),
              "g_ffn": (g_ffn, m_g_ffn, v_g_ffn), "ret_decay_fwd": (ret_decay_fwd, m_ret_decay_fwd, v_ret_decay_fwd),
              "ret_decay_bwd": (ret_decay_bwd, m_ret_decay_bwd, v_ret_decay_bwd), "g_ret": (g_ret, m_g_ret, v_g_ret),
              "g_q_lora": (g_q_lora, m_g_q_lora, v_g_q_lora), "g_kv_lora": (g_kv_lora, m_g_kv_lora, v_g_kv_lora),
              "g_final": (g_final, m_g_final, v_g_final)}
    rows = {k: tuple(a.reshape(1, -1) for a in t) for k, t in smalls.items()}
    gcc_g, gb_g = _exchange_wait(st_c, behind, "gather_cc_wait")
    small_out = _adamw_small(rows, sm_g, gcc_g, gb_g)
    for name, outs in small_out.items():
        res[name] = [o.reshape(smalls[name][0].shape) for o in outs]

    pieces = _exchange_wait(st_r, small_out["g_final"][3], "scatter_rest_wait")
    for name, w, m, v, pcs in (("w_in", w_in, m_w_in, v_w_in, pieces[0]), ("w_uq", w_uq, m_w_uq, v_w_uq, pieces[1])):
        res[name] = [a.T[None] for a in _adamw(w[0].T, m[0].T, v[0].T, pcs, "adamw_" + name)]
    res["w_ukv"] = [a[None] for a in _adamw(w_ukv[0], m_w_ukv[0], v_w_ukv[0], jnp.transpose(pieces[2], (0, 2, 1)),
                                            "adamw_w_ukv")]

    loss = loss_g[0, 0, 0]
    for k in range(1, N_DEV):
        loss = loss + loss_g[k, 0, 0]

    order = ("c_ctx", "w_ada", "b_ada", "g_attn", "g_ffn", "w_in", "ret_decay_fwd", "ret_decay_bwd", "g_ret", "g_q_lora",
             "w_uq", "g_kv_lora", "w_ukv", "w_out", "w_ff1", "w_ff2", "g_final")
    return (loss, grad_x, *[res[n][0] for n in order], *[res[n][1] for n in order], *[res[n][2] for n in order],
            *[res[n][3] for n in order])
```

```python
import functools
import math

import jax
import jax.numpy as jnp
from jax import lax
from jax.experimental import pallas as pl
from jax.experimental.pallas import tpu as pltpu

F32 = jnp.float32
BF = jnp.bfloat16
EPS = 1e-6
LANE = 128
LOG2_E = 1.0 / math.log(2.0)
N_DEV = 8
D_MODEL = 1024
SEQ = 2048
CTX_LEN = 256
GRID_W = 64
N_HEADS = 4
RET_CHUNK = 512
N_CHUNK = SEQ // RET_CHUNK
D_FF = 4096
FF_BLK = D_FF // N_DEV
IN_PAD = 2816
KV_LEN = CTX_LEN + SEQ
ROPE_BASE = 10000.0
ADAM_LR, ADAM_B1, ADAM_B2, ADAM_EPS, ADAM_WD, ADAM_STEP = 0.001, 0.9, 0.999, 1e-08, 0.01, 10
TOK = 512
TOK_B = 256
VMEM_LIMIT = 56 * 1024 * 1024
ARB = "arbitrary"
MESH = pl.DeviceIdType.MESH
_HEAD_SL = [slice(LANE * h, LANE * (h + 1)) for h in range(N_HEADS)]
W_SHAPES = [(2048, D_MODEL), (768, D_MODEL), (1024, 384), (1024, 256)]


def _dot(a, b, ca, cb):
    return lax.dot_general(a.astype(BF), b.astype(BF), (((ca,), (cb,)), ((), ())), preferred_element_type=F32)


@jax.custom_vjp
def mm(a, b):
    return _dot(a, b, 1, 0)


@jax.custom_vjp
def mm_nt(a, b):
    return _dot(a, b, 1, 1)


@jax.custom_vjp
def mm_tn(a, b):
    return _dot(a, b, 0, 0)


mm.defvjp(lambda a, b: (_dot(a, b, 1, 0), (a, b)), lambda r, g: (mm_nt(g, r[1]), mm_tn(r[0], g)))
mm_nt.defvjp(lambda a, b: (_dot(a, b, 1, 1), (a, b)), lambda r, g: (mm(g, r[1]), mm_tn(g, r[0])))
mm_tn.defvjp(lambda a, b: (_dot(a, b, 0, 0), (a, b)), lambda r, g: (mm_nt(r[1], g), mm(r[0], g)))


@jax.custom_vjp
def _mmw(a, w, probe):
    return _dot(a, w, 1, 0)


def _mmw_bwd(r, g):
    a, w = r
    return mm_nt(g, w), jnp.zeros_like(w), mm_tn(a, g)


_mmw.defvjp(lambda a, w, probe: (_dot(a, w, 1, 0), (a, w)), _mmw_bwd)


@jax.custom_vjp
def _mmwt(a, wt, probe):
    return _dot(a, wt, 1, 1)


_mmwt.defvjp(lambda a, wt, probe: (_dot(a, wt, 1, 1), (a, wt)),
             lambda r, g: (mm(g, r[1]), jnp.zeros_like(r[1]), mm_tn(g, r[0])))


def mmwt(a, wt, probe):
    return _dot(a, wt, 1, 1) if probe is None else _mmwt(a, wt, probe)


def mmw(a, w, probe):
    return _dot(a, w, 1, 0) if probe is None else _mmw(a, w, probe)


def rmsn(x, g):
    return x * lax.rsqrt(jnp.mean(x * x, axis=-1, keepdims=True) + EPS) * g


def silu(x):
    return x * jax.nn.sigmoid(x)


def _swap_halves_impl(x):
    return pltpu.roll(x, 64, 1)


@jax.custom_vjp
def swap_halves(x):
    return _swap_halves_impl(x)


swap_halves.defvjp(lambda x: (_swap_halves_impl(x), None), lambda _, g: (_swap_halves_impl(g),))


def rope(x, cs1, sn1, every=1):
    blocks = []
    for i in range(x.shape[-1] // LANE):
        xb = x[:, LANE * i:LANE * (i + 1)]
        blocks.append(xb * cs1 + swap_halves(xb) * sn1 if i % every == every - 1 else xb)
    return blocks[0] if len(blocks) == 1 else jnp.concatenate(blocks, axis=-1)


def k1_tile(x, sh, sc, g_attn, g_q, g_kv, ws, ps, tabs, is_ctx):
    w_a, w_b, w_uq, w_ukv = ws
    p_a, p_b, p_uq, p_ukv = ps
    cs1, sn1 = tabs
    h = rmsn(x, g_attn) * (1.0 + sc) + sh
    pa = mmwt(h, w_a, p_a)
    pb = mmwt(h, w_b, p_b)
    rk = pa[:, 512:1024] * 0.125
    rv = pa[:, 1024:1536]
    kpe = pb[:, 640:768]
    kv = mmwt(rmsn(pb[:, 384:640], g_kv), w_ukv, p_ukv)
    if not is_ctx:
        rk = rope(rk, cs1, sn1)
        kpe = rope(kpe, cs1, sn1)
    k_full = jnp.concatenate([piece for sl in _HEAD_SL for piece in (kv[:, sl], kpe)], axis=-1)
    v = kv[:, 512:]
    if is_ctx:
        return rk, rv, k_full, v
    rq = rope(pa[:, 0:512], cs1, sn1)
    rg = pa[:, 1536:2048]
    q = rope(mmwt(rmsn(pb[:, 0:384], g_q), w_uq, p_uq), cs1, sn1, every=2)
    return rq, rk, rv, rg, q, k_full, v


def log_sigmoid(x):
    return jnp.minimum(x, 0.0) - jnp.log(1.0 + jnp.exp(-jnp.abs(x)))


def decay_mask(lg, reverse):
    c = RET_CHUNK
    ii = lax.broadcasted_iota(jnp.int32, (c, c), 0).astype(F32)
    jj = lax.broadcasted_iota(jnp.int32, (c, c), 1).astype(F32)
    diff = (jj - ii) if reverse else (ii - jj)
    return jnp.where(diff >= 0, jnp.exp2((lg * LOG2_E) * jnp.maximum(diff, 0.0)), 0.0)


def decay_rows(lg, reverse):
    c = RET_CHUNK
    pos = lax.broadcasted_iota(jnp.int32, (c, 1), 0).astype(F32)
    if reverse:
        return jnp.exp(lg * pos), jnp.exp(lg * (c - pos))
    return jnp.exp(lg * (c - 1.0 - pos)), jnp.exp(lg * (pos + 1.0))


def ret_chunk(q, k, v, s, lg, reverse, pre=None):
    c = RET_CHUNK
    dec, wk, wq = (decay_mask(lg, reverse), *decay_rows(lg, reverse)) if pre is None else pre
    o = mm(mm_nt(q, k) * dec, v) + mm(q * wq, s)
    s_next = jnp.exp(lg * float(c)) * s + mm_tn(k * wk, v)
    return o, s_next


def ctx_state(kc, vc, lg, reverse):
    n = kc.shape[0]
    pos = lax.broadcasted_iota(jnp.int32, (n, 1), 0).astype(F32)
    w = jnp.exp(lg * pos) if reverse else jnp.exp(lg * (n - 1.0 - pos))
    return mm_tn(kc * w, vc)


def attn_head(qn, qp, kn, kp, v):
    s = (mm_nt(qn, kn) + mm_nt(qp, kp)) * (1.0 / math.sqrt(192.0))
    e = jnp.exp(s - jnp.max(s, axis=-1, keepdims=True))
    return mm(e / jnp.sum(e, axis=-1, keepdims=True), v)


def gn_gate(o, rg, g_ret):
    ys = []
    for h in range(N_HEADS):
        sl = slice(LANE * h, LANE * (h + 1))
        oh = o[:, sl]
        mu = jnp.mean(oh, axis=-1, keepdims=True)
        var = jnp.mean(jnp.square(oh - mu), axis=-1, keepdims=True)
        ys.append((oh - mu) * lax.rsqrt(var + EPS) * g_ret[:, sl])
    return jnp.concatenate(ys, axis=-1) * silu(rg)


def k4a_tile(x, o_f, o_b, rg, y_mla, g_ret, gt_a, g_ffn, sh_f, sc_f, w_out, p_out):
    mix = jnp.concatenate([gn_gate(o_f + o_b, rg, g_ret), y_mla], axis=-1)
    x_mid = x + gt_a * mmw(mix, w_out, p_out)
    h2 = rmsn(x_mid, g_ffn) * (1.0 + sc_f) + sh_f
    return x_mid, h2


def k4c_tile(x_mid, mlp, gt_f, g_final, tgt):
    y = rmsn(x_mid + gt_f * mlp, g_final)
    per_tok = jnp.mean(jnp.square(y - tgt), axis=-1, keepdims=True)
    return 0.5 * jnp.sum(per_tok, axis=0, keepdims=True)


def _cp(sem=None, vmem=VMEM_LIMIT):
    return pltpu.CompilerParams(dimension_semantics=sem, vmem_limit_bytes=vmem)


def _acc(ref, val, first):
    @pl.when(first)
    def _():
        ref[...] = val

    @pl.when(jnp.logical_not(first))
    def _():
        ref[...] += val


def _full(shape):
    nd = len(shape)
    return pl.BlockSpec(shape, lambda *_: (0,) * nd)


ANY = pl.BlockSpec(memory_space=pl.ANY)


def _sds(shape, dtype=F32):
    return jax.ShapeDtypeStruct(shape, dtype)


def _exchange(arrs, gather, name):
    n = len(arrs)
    modes = [gather] * n if isinstance(gather, bool) else list(gather)
    out_shape = [_sds(((N_DEV,) + a.shape) if g else a.shape, a.dtype) for a, g in zip(arrs, modes)]

    def body(*refs):
        ins, outs = refs[:n], refs[n:2 * n]
        send_sems, recv_sems, local_sems = refs[2 * n:]
        x, y, c = lax.axis_index("x"), lax.axis_index("y"), lax.axis_index("c")
        me = 4 * x + 2 * y + c
        sends, recvs, locs = [], [], []
        for i in range(n):
            gather = modes[i]
            for k in range(N_DEV - 1):
                bits = k + 1
                px = x ^ ((bits >> 2) & 1)
                py = y ^ ((bits >> 1) & 1)
                pc = c ^ (bits & 1)
                peer = 4 * px + 2 * py + pc
                src = ins[i] if gather else ins[i].at[peer]
                sem = i * (N_DEV - 1) + k
                sends.append(pltpu.make_async_remote_copy(
                    src_ref=src, dst_ref=outs[i].at[me], send_sem=send_sems.at[sem], recv_sem=recv_sems.at[sem],
                    device_id=(px, py, pc), device_id_type=MESH))
                recvs.append(pltpu.make_async_remote_copy(
                    src_ref=src, dst_ref=outs[i].at[peer], send_sem=send_sems.at[sem], recv_sem=recv_sems.at[sem],
                    device_id=(px, py, pc), device_id_type=MESH))
            locs.append(pltpu.make_async_copy(ins[i] if gather else ins[i].at[me], outs[i].at[me], local_sems.at[i]))
        for cp in locs + sends:
            cp.start()
        for cp in recvs:
            cp.wait_recv()
        for cp in sends:
            cp.wait_send()
        for cp in locs:
            cp.wait()

    outs = pl.pallas_call(
        body, name=name, out_shape=out_shape, in_specs=[ANY] * n, out_specs=[ANY] * n,
        scratch_shapes=[pltpu.SemaphoreType.DMA((n * (N_DEV - 1),)), pltpu.SemaphoreType.DMA((n * (N_DEV - 1),)),
                        pltpu.SemaphoreType.DMA((n,))],
    )(*arrs)
    return list(outs)


def _gather_two_level(arrs, name):
    n = len(arrs)

    def body(*refs):
        ins, outs = refs[:n], refs[n:2 * n]
        send_sems, recv_sems, local_sems = refs[2 * n:]
        x, y, c = lax.axis_index("x"), lax.axis_index("y"), lax.axis_index("c")
        sibling = (x, y, 1 - c)
        chips = [(1 - x, y), (x, 1 - y), (1 - x, 1 - y)]

        def slot(px, py, pc):
            return 4 * px + 2 * py + pc

        first, passed, waits, locs = [], [], [], []
        for i in range(n):
            def copy(k, block, to, src=None, i=i):
                dst = outs[i].at[slot(*block)]
                return pltpu.make_async_remote_copy(
                    src_ref=dst if src is None else src, dst_ref=dst, send_sem=send_sems.at[7 * i + k],
                    recv_sem=recv_sems.at[7 * i + k], device_id=to, device_id_type=MESH)

            locs.append(pltpu.make_async_copy(ins[i], outs[i].at[slot(x, y, c)], local_sems.at[i]))
            first.append(copy(0, (x, y, c), sibling, src=ins[i]))
            first += [copy(1 + j, (x, y, c), (*chip, c), src=ins[i]) for j, chip in enumerate(chips)]
            passed.append([copy(4 + j, (*chip, c), sibling) for j, chip in enumerate(chips)])
            waits.append([copy(1 + j, (*chip, c), (x, y, c)) for j, chip in enumerate(chips)])
        for cp in locs + first:
            cp.start()
        for j in range(3):
            for i in range(n):
                waits[i][j].wait_recv()
                passed[i][j].start()
        for i in range(n):
            def arrival(k, block, i=i):
                dst = outs[i].at[slot(*block)]
                return pltpu.make_async_remote_copy(
                    src_ref=dst, dst_ref=dst, send_sem=send_sems.at[7 * i + k], recv_sem=recv_sems.at[7 * i + k],
                    device_id=sibling, device_id_type=MESH)

            arrival(0, (x, y, 1 - c)).wait_recv()
            for j, chip in enumerate(chips):
                arrival(4 + j, (*chip, 1 - c)).wait_recv()
        for cp in first + [p for ps in passed for p in ps]:
            cp.wait_send()
        for cp in locs:
            cp.wait()

    outs = pl.pallas_call(
        body, name=name, out_shape=[_sds((N_DEV,) + a.shape, a.dtype) for a in arrs], in_specs=[ANY] * n,
        out_specs=[ANY] * n,
        scratch_shapes=[pltpu.SemaphoreType.DMA((7 * n,)), pltpu.SemaphoreType.DMA((7 * n,)),
                        pltpu.SemaphoreType.DMA((n,))],
    )(*arrs)
    return list(outs)


HBM = pl.BlockSpec(memory_space=pltpu.HBM)
SEM = pl.BlockSpec(memory_space=pltpu.SEMAPHORE)
EFFECT = pltpu.SideEffectType.DATAFLOW_SIDE_EFFECTING


def _peer(k, chips=False):
    x, y, c = lax.axis_index("x"), lax.axis_index("y"), lax.axis_index("c")
    bits = (k + 1) << 1 if chips else k + 1
    px, py, pc = x ^ ((bits >> 2) & 1), y ^ ((bits >> 1) & 1), c ^ (bits & 1)
    if chips:
        return (px, py, pc), 2 * px + py, 2 * x + y
    return (px, py, pc), 4 * px + 2 * py + pc, 4 * x + 2 * y + c


def _exchange_start(arrs, gather, name, after=None, chips=False):
    n = len(arrs)
    n_peer, n_slot = (3, 4) if chips else (N_DEV - 1, N_DEV)
    modes = [gather] * n if isinstance(gather, bool) else list(gather)
    lands = [pltpu.with_memory_space_constraint(lax.empty(((n_slot,) + a.shape) if g else a.shape, a.dtype), pltpu.HBM)
             for a, g in zip(arrs, modes)]
    srcs = [pltpu.with_memory_space_constraint(a, pltpu.HBM) for a in arrs]

    extra = [] if after is None else [after]

    def body(*refs):
        ins, zones = refs[:n], refs[n:2 * n]
        send_sems, recv_sems, local_sems = refs[2 * n + len(extra):2 * n + len(extra) + 3]
        token = refs[-1]
        for i in range(n):
            gather = modes[i]
            for k in range(n_peer):
                dev, peer, me = _peer(k, chips)
                sem = i * n_peer + k
                pltpu.make_async_remote_copy(
                    src_ref=ins[i] if gather else ins[i].at[peer], dst_ref=zones[i].at[me],
                    send_sem=send_sems.at[sem], recv_sem=recv_sems.at[sem], device_id=dev, device_id_type=MESH).start()
            _, _, me = _peer(0, chips)
            pltpu.make_async_copy(ins[i] if gather else ins[i].at[me], zones[i].at[me], local_sems.at[i]).start()
        token[...] = jnp.zeros_like(token)

    nsem = n * n_peer
    outs = pl.pallas_call(
        body, name=name,
        out_shape=[pltpu.SemaphoreType.DMA((nsem,)), pltpu.SemaphoreType.DMA((nsem,)), pltpu.SemaphoreType.DMA((n,))]
        + [pltpu.HBM(a.shape, a.dtype) for a in srcs] + [pltpu.HBM(z.shape, z.dtype) for z in lands]
        + [_sds((8, LANE))],
        in_specs=[HBM] * (2 * n) + [ANY] * len(extra),
        out_specs=[SEM, SEM, SEM] + [HBM] * (2 * n) + [pl.BlockSpec(memory_space=pltpu.VMEM)],
        input_output_aliases={i: 3 + i for i in range(2 * n)},
        compiler_params=pltpu.CompilerParams(has_side_effects=EFFECT),
    )(*srcs, *lands, *extra)
    return {"n": n, "gather": modes, "chips": chips, "sems": outs[:3], "srcs": outs[3:3 + n],
            "lands": outs[3 + n:3 + 2 * n], "token": outs[-1]}


def _exchange_wait(st, after, name, which=None):
    modes, chips = st["gather"], st["chips"]
    afters = list(after) if isinstance(after, (list, tuple)) else [after]
    which = list(range(st["n"])) if which is None else which
    n = len(which)
    n_peer = 3 if chips else N_DEV - 1
    srcs, lands = [st["srcs"][i] for i in which], [st["lands"][i] for i in which]

    def body(*refs):
        ins, zones = refs[:n], refs[n:2 * n]
        send_sems, recv_sems, local_sems = refs[2 * n:2 * n + 3]
        for j, i in enumerate(which):
            gather = modes[i]
            for k in range(n_peer):
                dev, peer, me = _peer(k, chips)
                sem = i * n_peer + k
                src = ins[j] if gather else ins[j].at[peer]
                cp = pltpu.make_async_remote_copy(
                    src_ref=src, dst_ref=zones[j].at[peer], send_sem=send_sems.at[sem], recv_sem=recv_sems.at[sem],
                    device_id=dev, device_id_type=MESH)
                cp.wait_send()
                cp.wait_recv()
            _, _, me = _peer(0, chips)
            pltpu.make_async_copy(ins[j] if gather else ins[j].at[me], zones[j].at[me], local_sems.at[i]).wait()

    outs = pl.pallas_call(
        body, name=name,
        out_shape=[pltpu.HBM(a.shape, a.dtype) for a in srcs] + [pltpu.HBM(z.shape, z.dtype) for z in lands],
        in_specs=[HBM] * (2 * n) + [SEM, SEM, SEM] + [ANY] * len(afters), out_specs=[HBM] * (2 * n),
        input_output_aliases={i: i for i in range(2 * n)},
        compiler_params=pltpu.CompilerParams(has_side_effects=EFFECT),
    )(*srcs, *lands, *st["sems"], *afters)
    return list(outs[n:])


def _pair_reduce(arrs, name, after):
    n = len(arrs)

    def body(*refs):
        ins, refs = refs[:n], refs[n + 1:]
        outs, got, mine = refs[:n], refs[n:2 * n], refs[2 * n:3 * n]
        send_sems, recv_sems, local_sems = refs[3 * n:]
        x, y, c = lax.axis_index("x"), lax.axis_index("y"), lax.axis_index("c")
        sends, locs = [], []
        for i in range(n):
            for q in range(4):
                sem = 4 * i + q
                sends.append(pltpu.make_async_remote_copy(
                    src_ref=ins[i].at[2 * q + 1 - c], dst_ref=got[i].at[q], send_sem=send_sems.at[sem],
                    recv_sem=recv_sems.at[sem], device_id=(x, y, 1 - c), device_id_type=MESH))
                locs.append(pltpu.make_async_copy(ins[i].at[2 * q + c], mine[i].at[q], local_sems.at[sem]))
        for cp in locs + sends:
            cp.start()
        for cp in sends:
            cp.wait_recv()
        for cp in locs:
            cp.wait()
        for i in range(n):
            outs[i][...] = (mine[i][...].astype(F32) + got[i][...].astype(F32)).astype(BF)
        for cp in sends:
            cp.wait_send()

    half = [(4,) + a.shape[1:] for a in arrs]
    outs = pl.pallas_call(
        body, name=name, out_shape=[_sds(h, BF) for h in half], in_specs=[ANY] * (n + 1),
        out_specs=[pl.BlockSpec(memory_space=pltpu.VMEM)] * n,
        scratch_shapes=[pltpu.VMEM(h, BF) for h in half] * 2
        + [pltpu.SemaphoreType.DMA((4 * n,)), pltpu.SemaphoreType.DMA((4 * n,)), pltpu.SemaphoreType.DMA((4 * n,))],
        compiler_params=_cp(),
    )(*arrs, after)
    return list(outs)


def _mod_fwd(crows, w_ada, b_blk):
    def body(c_ref, w_ref, b_ref, o_ref):
        o_ref[...] = mm(silu(c_ref[...]), w_ref[...]) + b_ref[...]

    return pl.pallas_call(body, name="mod_fwd", out_shape=_sds((24, 768)), compiler_params=_cp())(crows, w_ada, b_blk)


def _mod_bwd(crows, w_ada, dmod_blk, dmodc_blk):
    def body(c_ref, w_ref, d_ref, dc_ref, gw_ref, gc_ref, gb_ref):
        cr = c_ref[...]
        dc = dc_ref[0:1, :]
        for p in range(1, N_DEV):
            dc = dc + dc_ref[p:p + 1, :]
        row = lax.broadcasted_iota(jnp.int32, (24, 1), 0)
        gw_ref[...] = mm_tn(silu(cr), jnp.where(row == 16, dc, d_ref[...]))
        cc = cr[16:17, :]
        sg = jax.nn.sigmoid(cc)
        part = mm_nt(jnp.broadcast_to(dc, (8, 768)), w_ref[...])
        gc_ref[...] = part * (sg * (1.0 + cc * (1.0 - sg)))
        gb_ref[...] = jnp.broadcast_to(jnp.sum(d_ref[...], axis=0, keepdims=True) + dc, (8, 768))

    return pl.pallas_call(
        body, name="mod_bwd", out_shape=[_sds((D_MODEL, 768)), _sds((8, D_MODEL)), _sds((8, 768))],
        compiler_params=_cp())(crows, w_ada, dmod_blk, dmodc_blk)


def _tab_specs(tk):
    return [pl.BlockSpec((tk, LANE), lambda i, t: (t, 0))] * 2


def _k1_fwd(x, mod, g_attn, g_q, g_kv, ws, tabs, kv_all, is_ctx):
    b, l, _ = x.shape
    tk = CTX_LEN if is_ctx else TOK
    nt = l // tk
    n_f32 = 2 if is_ctx else 4

    def body(x_ref, mod_ref, ga_ref, gq_ref, gk_ref, wa_ref, wb_ref, wq_ref, wk_ref, cs_ref, sn_ref, *rest):
        outs = rest if is_ctx else rest[2:]
        res = k1_tile(x_ref[...], mod_ref[0:1, :], mod_ref[1:2, :], ga_ref[...], gq_ref[...], gk_ref[...],
                      (wa_ref[...], wb_ref[...], wq_ref[...], wk_ref[...]), (None,) * 4,
                      (cs_ref[...], sn_ref[...]), is_ctx)
        for o_ref, r in zip(outs, res):
            o_ref[...] = r.astype(o_ref.dtype)

    tok = lambda w, off=0: pl.BlockSpec((None, tk, w), lambda i, t: (i, t + off, 0))
    mod_spec = pl.BlockSpec((None, 8, D_MODEL), (lambda i, t: (0, 0, 0)) if is_ctx else (lambda i, t: (i, 0, 0)))
    kv_off = SEQ // tk if is_ctx else 0
    in_specs = ([tok(D_MODEL), mod_spec, _full((1, D_MODEL)), _full((1, 384)), _full((1, 256))]
                + [_full(s) for s in W_SHAPES] + _tab_specs(tk))
    args = [x, mod, g_attn, g_q, g_kv, *ws, *tabs]
    out_specs = [tok(512)] * n_f32 + ([] if is_ctx else [tok(1024)]) + [tok(1024, kv_off), tok(512, kv_off)]
    out_shape = ([_sds((b, l, 512))] * n_f32 + ([] if is_ctx else [_sds((b, l, 1024), BF)])
                 + [_sds((b, KV_LEN, 1024), BF), _sds((b, KV_LEN, 512), BF)])
    aliases = {}
    if not is_ctx:
        aliases = {len(args): n_f32 + 1, len(args) + 1: n_f32 + 2}
        in_specs += [ANY, ANY]
        args += list(kv_all)
    return pl.pallas_call(
        body, name="k1_fwd_ctx" if is_ctx else "k1_fwd", grid=(b, nt), in_specs=in_specs, out_specs=out_specs,
        out_shape=out_shape, input_output_aliases=aliases, compiler_params=_cp((ARB, ARB)),
    )(*args)


N_ACC = 7


def _k1_bwd(x, ctx, mod, mod_c, g_attn, g_q, g_kv, ws, tabs, cts, cts_c, dx_res):
    b, l, _ = x.shape
    tk = TOK_B
    nt = l // tk
    flat = [[a for group in c for a in group] for c in (cts, cts_c)]
    sizes = [[len(g) for g in c] for c in (cts, cts_c)]
    acc_shapes = W_SHAPES + [(1, D_MODEL), (1, 384), (1, 256)]

    def body(*refs):
        it = iter(refs)
        x_ref, c_ref, mod_ref, modc_ref, ga_ref, gq_ref, gk_ref = [next(it) for _ in range(7)]
        w_hbm = [next(it) for _ in range(4)]
        tab_refs = [next(it) for _ in range(2)]
        ct_refs = [[next(it) for _ in f] for f in flat]
        res_ref, gx_ref = next(it), next(it)
        out_hbm = [next(it) for _ in range(N_ACC)]
        dmod_ref, dmodc_ref = next(it), next(it)
        w_vmem = [next(it) for _ in range(4)]
        accs = [next(it) for _ in range(N_ACC)]
        sem = next(it)
        i, t = pl.program_id(0), pl.program_id(1)
        first = jnp.logical_and(i == 0, t == 0)

        @pl.when(first)
        def _():
            for src, dst in zip(w_hbm, w_vmem):
                pltpu.sync_copy(src, dst)
            for k in range(N_ACC):
                accs[k][...] = jnp.zeros(acc_shapes[k], F32)

        def tile(is_ctx):
            which = 1 if is_ctx else 0
            ct_vals, pos = [], 0
            for gsz in sizes[which]:
                v = ct_refs[which][pos][...].astype(F32)
                for r in ct_refs[which][pos + 1:pos + gsz]:
                    v = v + r[...]
                ct_vals.append(v)
                pos += gsz
            wv = tuple(r[...] for r in w_vmem)
            tv = tuple(r[...] for r in tab_refs)
            m_ref = modc_ref if is_ctx else mod_ref

            def f(xv, sh, sc, ga, gq, gk, *probes):
                return k1_tile(xv, sh, sc, ga, gq, gk, wv, probes, tv, is_ctx)

            probes = [jnp.zeros(s, F32) for s in W_SHAPES]
            xin = c_ref[...] if is_ctx else x_ref[...]
            _, vjp = jax.vjp(f, xin, m_ref[0:1, :], m_ref[1:2, :], ga_ref[...], gq_ref[...], gk_ref[...], *probes)
            dx, dsh, dsc, dga, dgq, dgk, dwa, dwb, dwq, dwk = vjp(tuple(ct_vals))
            for ref, val in zip(accs, (dwa, dwb, dwq, dwk, dga, dgq, dgk)):
                ref[...] += val
            return dx, dsh, dsc

        @pl.when(t == 0)
        def _():
            _, dsh, dsc = tile(True)
            _acc(dmodc_ref.at[0:1, :], dsh, i == 0)
            _acc(dmodc_ref.at[1:2, :], dsc, i == 0)

            @pl.when(i == 0)
            def _():
                dmodc_ref[2:8, :] = jnp.zeros((6, D_MODEL), F32)

        @pl.when(t > 0)
        def _():
            dx, dsh, dsc = tile(False)
            gx_ref[...] = dx + res_ref[...]
            _acc(dmod_ref.at[0:1, :], dsh, t == 1)
            _acc(dmod_ref.at[1:2, :], dsc, t == 1)

            @pl.when(t == 1)
            def _():
                dmod_ref[2:8, :] = jnp.zeros((6, D_MODEL), F32)

        @pl.when(jnp.logical_and(i == b - 1, t == nt))
        def _():
            for k in range(4):
                w_vmem[k][...] = accs[k][...].astype(BF)
            cps = [pltpu.make_async_copy(w_vmem[k] if k < 4 else accs[k], out_hbm[k], sem.at[k]) for k in range(N_ACC)]
            for cp in cps:
                cp.start()
            for cp in cps:
                cp.wait()

    lat = lambda w, off=0: pl.BlockSpec((None, tk, w), lambda i, t: (i, jnp.maximum(t - 1, 0) + off, 0))
    con = lambda w, off=0: pl.BlockSpec((None, tk, w), lambda i, t: (i, off, 0))
    mod_spec = pl.BlockSpec((None, 8, D_MODEL), lambda i, t: (i, 0, 0))
    modc_spec = pl.BlockSpec((None, 8, D_MODEL), lambda i, t: (0, 0, 0))
    tab_spec = pl.BlockSpec((tk, LANE), lambda i, t: (jnp.maximum(t - 1, 0), 0))
    in_specs = ([lat(D_MODEL), con(D_MODEL), mod_spec, modc_spec, _full((1, D_MODEL)), _full((1, 384)), _full((1, 256))]
                + [ANY] * 4 + [tab_spec] * 2)
    args = [x, ctx, mod, mod_c, g_attn, g_q, g_kv, *ws, *tabs]
    for a, off in flat[0]:
        in_specs.append(lat(a.shape[-1], off // tk))
        args.append(a)
    for a, off in flat[1]:
        in_specs.append(con(a.shape[-1], off // tk))
        args.append(a)
    in_specs.append(lat(D_MODEL))
    args.append(dx_res)
    out_shape = ([_sds((b, l, D_MODEL))] + [_sds(s, BF) for s in W_SHAPES] + [_sds(s) for s in acc_shapes[4:]]
                 + [_sds((b, 8, D_MODEL)), _sds((1, 8, D_MODEL))])
    out_specs = [lat(D_MODEL)] + [ANY] * N_ACC + [mod_spec, modc_spec]
    outs = pl.pallas_call(
        body, name="k1_bwd", grid=(b, nt + 1), in_specs=in_specs, out_specs=out_specs, out_shape=out_shape,
        scratch_shapes=[pltpu.VMEM(s, BF) for s in W_SHAPES] + [pltpu.VMEM(s, F32) for s in acc_shapes]
        + [pltpu.SemaphoreType.DMA((N_ACC,))],
        compiler_params=_cp((ARB, ARB)),
    )(*args)
    return outs[0], list(outs[1:1 + N_ACC]), outs[1 + N_ACC], outs[2 + N_ACC]


def _chunk_spec(rev):
    if rev:
        return pl.BlockSpec((None, RET_CHUNK, 512), lambda i, n: (i, N_CHUNK - 1 - n, 0))
    return pl.BlockSpec((None, RET_CHUNK, 512), lambda i, n: (i, n, 0))


def _state_spec(rev):
    if rev:
        return pl.BlockSpec((None, N_HEADS, None, LANE, LANE), lambda i, n: (i, 0, N_CHUNK - 1 - n, 0, 0))
    return pl.BlockSpec((None, N_HEADS, None, LANE, LANE), lambda i, n: (i, 0, n, 0, 0))


_CTX_SPEC = pl.BlockSpec((None, CTX_LEN, 512), lambda i, n: (i, 0, 0))
_DEC_SPEC = pl.BlockSpec((N_HEADS, 1, 1), lambda i, n: (0, 0, 0))


def _k2_fwd(rq, rk, rv, rkc, rvc, dec_f, dec_b):
    b = rq.shape[0]

    def body(qf, kf, vf, qb, kb, vb, kc, vc, df, db, of_ref, ob_ref, sf_out, sb_out, sf, sb, masks, rows):
        n = pl.program_id(1)
        first = jnp.logical_and(pl.program_id(0) == 0, n == 0)
        for h, sl in enumerate(_HEAD_SL):
            lgf, lgb = log_sigmoid(df[h]), log_sigmoid(db[h])

            @pl.when(first)
            def _():
                for d, (lg, rev) in enumerate(((lgf, False), (lgb, True))):
                    masks[2 * h + d] = decay_mask(lg, rev)
                    for j, w in enumerate(decay_rows(lg, rev)):
                        rows[2 * h + d, j] = jnp.broadcast_to(w, (RET_CHUNK, LANE))

            @pl.when(n == 0)
            def _():
                sf[h] = ctx_state(kc[:, sl], vc[:, sl], lgf, False)
                sb[h] = ctx_state(kc[:, sl], vc[:, sl], lgb, True)

            sf_out[h] = sf[h]
            sb_out[h] = sb[h]
            o, s = ret_chunk(qf[:, sl], kf[:, sl], vf[:, sl], sf[h], lgf, False,
                             (masks[2 * h], rows[2 * h, 0], rows[2 * h, 1]))
            of_ref[:, sl] = o
            sf[h] = s
            o, s = ret_chunk(qb[:, sl], kb[:, sl], vb[:, sl], sb[h], lgb, True,
                             (masks[2 * h + 1], rows[2 * h + 1, 0], rows[2 * h + 1, 1]))
            ob_ref[:, sl] = o
            sb[h] = s

    l = rq.shape[1]
    return pl.pallas_call(
        body, name="k2_fwd", grid=(b, N_CHUNK),
        in_specs=[_chunk_spec(False)] * 3 + [_chunk_spec(True)] * 3 + [_CTX_SPEC, _CTX_SPEC, _DEC_SPEC, _DEC_SPEC],
        out_specs=[_chunk_spec(False), _chunk_spec(True), _state_spec(False), _state_spec(True)],
        out_shape=[_sds((b, l, 512)), _sds((b, l, 512)), _sds((b, N_HEADS, N_CHUNK, LANE, LANE)),
                   _sds((b, N_HEADS, N_CHUNK, LANE, LANE))],
        scratch_shapes=[pltpu.VMEM((N_HEADS, LANE, LANE), F32), pltpu.VMEM((N_HEADS, LANE, LANE), F32),
                        pltpu.VMEM((2 * N_HEADS, RET_CHUNK, RET_CHUNK), F32),
                        pltpu.VMEM((2 * N_HEADS, 2, RET_CHUNK, LANE), F32)],
        compiler_params=_cp((ARB, ARB)),
    )(rq, rk, rv, rq, rk, rv, rkc, rvc, dec_f, dec_b)


def _k2_bwd(rq, rk, rv, do, sf_prev, sb_prev, rkc, rvc, dec_f, dec_b):
    b, l, _ = rq.shape

    def body(qf, kf, vf, gf, spf, qb, kb, vb, gb, spb, kc, vc, df, db,
             dqf, dkf, dvf, dqb, dkb, dvb, dkc, dvc, ddf, ddb, dsf, dsb):
        n = pl.program_id(1)

        @pl.when(n == 0)
        def _():
            dsf[...] = jnp.zeros((N_HEADS, LANE, LANE), F32)
            dsb[...] = jnp.zeros((N_HEADS, LANE, LANE), F32)

        def one(h, sl, q, k, v, g, sp, dec, ds, dq, dk, dv, dd, rev):
            def f(qv, kv_, vv, sv, dcy):
                return ret_chunk(qv, kv_, vv, sv, log_sigmoid(dcy), rev)

            _, vjp = jax.vjp(f, q[:, sl], k[:, sl], v[:, sl], sp[h], dec[h])
            gq, gk, gv, gs, gd = vjp((g[:, sl], ds[h]))
            dq[:, sl] = gq
            dk[:, sl] = gk
            dv[:, sl] = gv
            ds[h] = gs
            _acc(dd.at[h], jnp.broadcast_to(gd, (8, LANE)), n == 0)

        for h, sl in enumerate(_HEAD_SL):
            one(h, sl, qf, kf, vf, gf, spf, df, dsf, dqf, dkf, dvf, ddf, False)
            one(h, sl, qb, kb, vb, gb, spb, db, dsb, dqb, dkb, dvb, ddb, True)

        @pl.when(n == N_CHUNK - 1)
        def _():
            def f(kcv, vcv, dcy, rev):
                return ctx_state(kcv, vcv, log_sigmoid(dcy), rev)

            for h, sl in enumerate(_HEAD_SL):
                _, vjp_f = jax.vjp(functools.partial(f, rev=False), kc[:, sl], vc[:, sl], df[h])
                gk_f, gv_f, gd_f = vjp_f(dsf[h])
                _, vjp_b = jax.vjp(functools.partial(f, rev=True), kc[:, sl], vc[:, sl], db[h])
                gk_b, gv_b, gd_b = vjp_b(dsb[h])
                dkc[:, sl] = gk_f + gk_b
                dvc[:, sl] = gv_f + gv_b
                ddf[h] += jnp.broadcast_to(gd_f, (8, LANE))
                ddb[h] += jnp.broadcast_to(gd_b, (8, LANE))

    dd_spec = pl.BlockSpec((None, N_HEADS, 8, LANE), lambda i, n: (i, 0, 0, 0))
    return pl.pallas_call(
        body, name="k2_bwd", grid=(b, N_CHUNK),
        in_specs=[_chunk_spec(True)] * 4 + [_state_spec(True)] + [_chunk_spec(False)] * 4 + [_state_spec(False)]
        + [_CTX_SPEC, _CTX_SPEC, _DEC_SPEC, _DEC_SPEC],
        out_specs=[_chunk_spec(True)] * 3 + [_chunk_spec(False)] * 3 + [_CTX_SPEC, _CTX_SPEC, dd_spec, dd_spec],
        out_shape=[_sds((b, l, 512))] * 6 + [_sds((b, CTX_LEN, 512))] * 2 + [_sds((b, N_HEADS, 8, LANE))] * 2,
        scratch_shapes=[pltpu.VMEM((N_HEADS, LANE, LANE), F32), pltpu.VMEM((N_HEADS, LANE, LANE), F32)],
        compiler_params=_cp((ARB, ARB)),
    )(rq, rk, rv, do, sf_prev, rq, rk, rv, do, sb_prev, rkc, rvc, dec_f, dec_b)


TQ = 1024
TQ_F = 512
QK_W = 2 * LANE
N_QP = 2
_Q_PARTS = [slice(i * TQ_F // N_QP, (i + 1) * TQ_F // N_QP) for i in range(N_QP)]


SM_SCALE = 1.0 / math.sqrt(192.0)


def _k3_specs(tq):
    qs = lambda w: pl.BlockSpec((None, tq, w), lambda i, h, t: (i, t, h))
    ks = lambda w: pl.BlockSpec((None, KV_LEN, w), lambda i, h, t: (i, 0, h))
    return qs, ks


def _k3_fwd(q, k, v):
    b, l, _ = q.shape

    def body(q_ref, k_ref, v_ref, o_ref, lse_ref):
        kv_, vv = k_ref[...], v_ref[...]
        for r in _Q_PARTS:
            s = _dot(q_ref[r, :], kv_, 1, 1)
            m = jnp.max(s, axis=-1, keepdims=True)
            e = jnp.exp2((s - m) * (SM_SCALE * LOG2_E))
            tot = jnp.sum(e, axis=-1, keepdims=True)
            o_ref[r, :] = _dot(e, vv, 1, 0) * (1.0 / tot)
            lse_ref[r, :] = jnp.broadcast_to(m * SM_SCALE + jnp.log(tot), (TQ_F // N_QP, LANE))

    qs, ks = _k3_specs(TQ_F)
    return pl.pallas_call(
        body, name="k3_fwd", grid=(b, N_HEADS, l // TQ_F), in_specs=[qs(QK_W), ks(QK_W), ks(LANE)],
        out_specs=[qs(LANE), qs(LANE)], out_shape=[_sds((b, l, N_HEADS * LANE))] * 2,
        compiler_params=_cp((ARB, ARB, ARB)),
    )(q, k, v)


def _k3_bwd(q, k, v, o, lse, dy, after):
    b, l, _ = q.shape

    def body(q_ref, k_ref, v_ref, o_ref, lse_ref, dy_ref, after_ref, dq_ref, dk_ref, dv_ref):
        t0 = pl.program_id(2) == 0
        kv_, vv = k_ref[...], v_ref[...]
        qv, dyv = q_ref[...], dy_ref[...]
        g = dyv.astype(BF)
        lse_col = jnp.max(lse_ref[...], axis=-1, keepdims=True)
        delta = jnp.sum(dyv * o_ref[...], axis=-1, keepdims=True)
        p = jnp.exp2(_dot(qv, kv_, 1, 1) * (SM_SCALE * LOG2_E) - lse_col * LOG2_E)
        ds = (p * (_dot(g, vv, 1, 1) - delta) * SM_SCALE).astype(BF)
        _acc(dv_ref, _dot(p, g, 0, 0), t0)
        dq_ref[...] = _dot(ds, kv_, 1, 0)
        _acc(dk_ref, _dot(ds, qv, 0, 0), t0)

    qs, ks = _k3_specs(TQ)
    return pl.pallas_call(
        body, name="k3_bwd", grid=(b, N_HEADS, l // TQ),
        in_specs=[qs(QK_W), ks(QK_W), ks(LANE), qs(LANE), qs(LANE), qs(LANE), ANY],
        out_specs=[qs(QK_W), ks(QK_W), ks(LANE)],
        out_shape=[_sds((b, l, N_HEADS * QK_W)), _sds((b, KV_LEN, N_HEADS * QK_W)), _sds((b, KV_LEN, N_HEADS * LANE))],
        compiler_params=_cp((ARB, ARB, ARB)),
    )(q, k, v, o, lse, dy, after)


def _mod_rows(mod_ref, rows):
    return [mod_ref[r:r + 1, :] for r in rows]


def _k4a_fwd(x, o_f, o_b, rg, y_mla, g_ret, w_out, mod, g_ffn):
    b, l, _ = x.shape

    def body(x_ref, of_ref, ob_ref, rg_ref, ym_ref, gr_ref, wo_ref, mod_ref, gf_ref, xm_ref, h2_ref):
        gt_a, sh_f, sc_f = _mod_rows(mod_ref, (2, 3, 4))
        x_mid, h2 = k4a_tile(x_ref[...], of_ref[...], ob_ref[...], rg_ref[...], ym_ref[...], gr_ref[...], gt_a,
                             gf_ref[...], sh_f, sc_f, wo_ref[...], None)
        xm_ref[...] = x_mid
        h2_ref[...] = h2.astype(BF)

    tok = lambda w: pl.BlockSpec((None, TOK, w), lambda i, t: (i, t, 0))
    mod_spec = pl.BlockSpec((None, 8, D_MODEL), lambda i, t: (i, 0, 0))
    return pl.pallas_call(
        body, name="k4a_fwd", grid=(b, l // TOK),
        in_specs=[tok(D_MODEL), tok(512), tok(512), tok(512), tok(512), _full((1, 512)), _full((D_MODEL, D_MODEL)),
                  mod_spec, _full((1, D_MODEL))],
        out_specs=[tok(D_MODEL), tok(D_MODEL)], out_shape=[_sds((b, l, D_MODEL)), _sds((b, l, D_MODEL), BF)],
        compiler_params=_cp((ARB, ARB)),
    )(x, o_f, o_b, rg, y_mla, g_ret, w_out, mod, g_ffn)


TOK_M = 512
TOK_D = 2048
HALF_FF = D_FF // 2


def _k4b_mlp_loss(h2, w1t, w2, x_mid, mod, g_final, tgt):
    b, l, _ = h2.shape
    nt = l // TOK_M

    def body(h2_ref, w1_hbm, w2_hbm, xm_ref, mod_ref, gfin_ref, tgt_ref, dxm_ref, dmlp_ref, r_ref, loss_ref, dgt_ref,
             dgfin_ref, w1_v, w2_v):
        i, t = pl.program_id(0), pl.program_id(1)
        first = jnp.logical_and(i == 0, t == 0)

        @pl.when(first)
        def _():
            pltpu.sync_copy(w1_hbm, w1_v)
            pltpu.sync_copy(w2_hbm, w2_v)

        h2v = h2_ref[...]
        mlp = None
        for half in range(2):
            rows = slice(half * HALF_FF, (half + 1) * HALF_FF)
            r = jnp.maximum(_dot(h2v, w1_v[rows, :], 1, 1), 0.0)
            r_ref[:, rows] = r.astype(BF)
            part = _dot(jnp.square(r), w2_v[rows, :], 1, 0)
            mlp = part if mlp is None else mlp + part
        (gt_f,) = _mod_rows(mod_ref, (5,))
        loss, vjp = jax.vjp(k4c_tile, xm_ref[...], mlp, gt_f, gfin_ref[...], tgt_ref[...])
        dxm, dmlp, dgt, dgfin, _ = vjp(jnp.ones((1, 1), F32))
        dxm_ref[...] = dxm
        dmlp_ref[...] = dmlp.astype(BF)
        _acc(loss_ref, jnp.broadcast_to(loss, (8, LANE)), first)
        _acc(dgfin_ref, dgfin, first)
        _acc(dgt_ref, dgt, t == 0)

    tok = lambda w: pl.BlockSpec((None, TOK_M, w), lambda i, t: (i, t, 0))
    return pl.pallas_call(
        body, name="k4b_mlp_loss", grid=(b, nt),
        in_specs=[tok(D_MODEL), ANY, ANY, tok(D_MODEL), pl.BlockSpec((None, 8, D_MODEL), lambda i, t: (i, 0, 0)),
                  _full((1, D_MODEL)), tok(D_MODEL)],
        out_specs=[tok(D_MODEL), tok(D_MODEL), tok(D_FF), _full((8, LANE)),
                   pl.BlockSpec((None, 1, D_MODEL), lambda i, t: (i, 0, 0)), _full((1, D_MODEL))],
        out_shape=[_sds((b, l, D_MODEL)), _sds((b, l, D_MODEL), BF), _sds((b, l, D_FF), BF), _sds((8, LANE)),
                   _sds((b, 1, D_MODEL)), _sds((1, D_MODEL))],
        scratch_shapes=[pltpu.VMEM((D_FF, D_MODEL), BF), pltpu.VMEM((D_FF, D_MODEL), BF)],
        compiler_params=_cp((ARB, ARB)),
    )(h2, w1t, w2, x_mid, mod, g_final, tgt)


def _k4d_mlp_bwd(h2, dmlp, r, w2):
    b, l, _ = h2.shape
    nt = l // TOK_D

    def body(h2_ref, dm_ref, r_ref, w2_ref, da_ref, dw1_ref, dw2_ref, acc1, acc2):
        i, t = pl.program_id(1), pl.program_id(2)
        first = jnp.logical_and(i == 0, t == 0)
        rv = r_ref[...].astype(F32)
        dm = dm_ref[...]
        da = (_dot(dm, w2_ref[...], 1, 1) * (2.0 * rv)).astype(BF)
        da_ref[...] = da
        _acc(acc2, _dot(jnp.square(rv), dm, 0, 0), first)
        _acc(acc1, _dot(h2_ref[...], da, 0, 0), first)

        @pl.when(jnp.logical_and(i == b - 1, t == nt - 1))
        def _():
            dw1_ref[...] = acc1[...].astype(BF)
            dw2_ref[...] = acc2[...].astype(BF)

    tok = lambda w: pl.BlockSpec((None, TOK_D, w), lambda j, i, t: (i, t, 0))
    col = pl.BlockSpec((None, TOK_D, FF_BLK), lambda j, i, t: (i, t, j))
    return pl.pallas_call(
        body, name="k4d_mlp_bwd", grid=(N_DEV, b, nt),
        in_specs=[tok(D_MODEL), tok(D_MODEL), col, pl.BlockSpec((None, FF_BLK, D_MODEL), lambda j, i, t: (j, 0, 0))],
        out_specs=[col, pl.BlockSpec((None, D_MODEL, FF_BLK), lambda j, i, t: (j, 0, 0)),
                   pl.BlockSpec((None, FF_BLK, D_MODEL), lambda j, i, t: (j, 0, 0))],
        out_shape=[_sds((b, l, D_FF), BF), _sds((N_DEV, D_MODEL, FF_BLK), BF), _sds((N_DEV, FF_BLK, D_MODEL), BF)],
        scratch_shapes=[pltpu.VMEM((D_MODEL, FF_BLK), F32), pltpu.VMEM((FF_BLK, D_MODEL), F32)],
        compiler_params=_cp((ARB, ARB, ARB)),
    )(h2, dmlp, r, w2)


def _k4e_bwd(x, o_f, o_b, rg, y_mla, g_ret, w_out, mod, g_ffn, dxm, da, w1t):
    b, l, _ = x.shape

    def body(x_ref, of_ref, ob_ref, rg_ref, ym_ref, gr_ref, wo_ref, mod_ref, gf_ref, dxm_ref, da_ref, w1_hbm,
             dx_ref, do_ref, drg_ref, dym_ref, dwo_ref, dgr_ref, dgf_ref, dmod_ref, w1_v, dwo_acc):
        i, t = pl.program_id(0), pl.program_id(1)
        first = jnp.logical_and(i == 0, t == 0)

        @pl.when(first)
        def _():
            pltpu.sync_copy(w1_hbm, w1_v)

        gt_a, sh_f, sc_f = _mod_rows(mod_ref, (2, 3, 4))
        wo = wo_ref[...]
        dh2 = _dot(da_ref[...], w1_v[...], 1, 0)

        def f(xv, ofv, rgv, ymv, grv, gta, gfv, shf, scf, p_out):
            return k4a_tile(xv, ofv, ob_ref[...], rgv, ymv, grv, gta, gfv, shf, scf, wo, p_out)

        _, vjp = jax.vjp(f, x_ref[...], of_ref[...], rg_ref[...], ym_ref[...], gr_ref[...], gt_a, gf_ref[...], sh_f,
                         sc_f, jnp.zeros((D_MODEL, D_MODEL), F32))
        dx, do, drg, dym, dgr, dgta, dgf, dshf, dscf, dwo = vjp((dxm_ref[...], dh2))
        dx_ref[...] = dx
        do_ref[...] = do
        drg_ref[...] = drg
        dym_ref[...] = dym
        _acc(dwo_acc, dwo, first)
        _acc(dgr_ref, dgr, first)
        _acc(dgf_ref, dgf, first)
        t0 = t == 0
        _acc(dmod_ref.at[2:3, :], dgta, t0)
        _acc(dmod_ref.at[3:4, :], dshf, t0)
        _acc(dmod_ref.at[4:5, :], dscf, t0)

        @pl.when(t0)
        def _():
            dmod_ref[0:2, :] = jnp.zeros((2, D_MODEL), F32)
            dmod_ref[5:8, :] = jnp.zeros((3, D_MODEL), F32)

        @pl.when(jnp.logical_and(i == b - 1, t == l // TOK_B - 1))
        def _():
            dwo_ref[...] = dwo_acc[...].astype(BF)

    tok = lambda w: pl.BlockSpec((None, TOK_B, w), lambda i, t: (i, t, 0))
    mod_spec = pl.BlockSpec((None, 8, D_MODEL), lambda i, t: (i, 0, 0))
    return pl.pallas_call(
        body, name="k4e_bwd", grid=(b, l // TOK_B),
        in_specs=[tok(D_MODEL), tok(512), tok(512), tok(512), tok(512), _full((1, 512)), _full((D_MODEL, D_MODEL)),
                  mod_spec, _full((1, D_MODEL)), tok(D_MODEL), tok(D_FF), ANY],
        out_specs=[tok(D_MODEL), tok(512), tok(512), tok(512), _full((D_MODEL, D_MODEL)), _full((1, 512)),
                   _full((1, D_MODEL)), mod_spec],
        out_shape=[_sds((b, l, D_MODEL)), _sds((b, l, 512)), _sds((b, l, 512)), _sds((b, l, 512)),
                   _sds((D_MODEL, D_MODEL), BF), _sds((1, 512)), _sds((1, D_MODEL)), _sds((b, 8, D_MODEL))],
        scratch_shapes=[pltpu.VMEM((D_FF, D_MODEL), BF), pltpu.VMEM((D_MODEL, D_MODEL), F32)],
        compiler_params=_cp((ARB, ARB)),
    )(x, o_f, o_b, rg, y_mla, g_ret, w_out, mod, g_ffn, dxm, da, w1t)


ADAM_BLOCK_BYTES = 32 * 1024 * 1024


def _adamw(w, m, v, pieces, name, after=None):
    r, c = w.shape
    npc = pieces.shape[0]
    per_row = c * (7 * 4 + npc * pieces.dtype.itemsize) * 2
    rb = max(d for d in range(8, r + 1, 8) if r % d == 0 and d * per_row <= ADAM_BLOCK_BYTES)

    def body(w_ref, m_ref, v_ref, p_ref, *rest):
        g_ref, d_ref, nm_ref, nv_ref = rest[-4:]
        g = p_ref[0].astype(F32)
        for k in range(1, npc):
            g = g + p_ref[k].astype(F32)
        wv = w_ref[...]
        mn = ADAM_B1 * m_ref[...] + (1.0 - ADAM_B1) * g
        vn = ADAM_B2 * v_ref[...] + (1.0 - ADAM_B2) * jnp.square(g)
        m_hat = mn / (1.0 - ADAM_B1 ** ADAM_STEP)
        v_hat = vn / (1.0 - ADAM_B2 ** ADAM_STEP)
        g_ref[...] = g
        d_ref[...] = -ADAM_LR * (m_hat / (jnp.sqrt(v_hat) + ADAM_EPS) + ADAM_WD * wv)
        nm_ref[...] = mn
        nv_ref[...] = vn

    blk = pl.BlockSpec((rb, c), lambda i: (i, 0))
    extra = [] if after is None else [after]
    return pl.pallas_call(
        body, name=name, grid=(r // rb,),
        in_specs=[blk, blk, blk, pl.BlockSpec((npc, rb, c), lambda i: (0, i, 0))] + [ANY] * len(extra),
        out_specs=[blk] * 4, out_shape=[_sds((r, c))] * 4, compiler_params=_cp((ARB,)),
    )(w, m, v, pieces, *extra)


def _pad_rot_rows(w, zero):
    k = w.shape[1]
    return lax.pad(w.reshape(-1, 2, 32, k), zero, ((0, 0, 0), (0, 0, 0), (0, 32, 0), (0, 0, 0))).reshape(-1, k)


def _cut_rot_rows(g):
    k = g.shape[1]
    return g.reshape(-1, 2, 64, k)[:, :, :32].reshape(-1, k)


def _w_in_pad(wt, zero):
    w_a = jnp.concatenate([_pad_rot_rows(wt[0:512], zero), wt[512:1536]], axis=0)
    w_b = jnp.concatenate([wt[1536:2176], _pad_rot_rows(wt[2176:2240], zero)], axis=0)
    return w_a, w_b


def _w_in_cut(g_a, g_b):
    return jnp.concatenate([_cut_rot_rows(g_a[0:1024]), g_a[1024:2048], g_b[0:640], _cut_rot_rows(g_b[640:768])], axis=0)


def _w_uq_pad(wt, zero):
    w = wt.reshape(N_HEADS, 192, 384)
    rot = _pad_rot_rows(w[:, 128:].reshape(N_HEADS * 64, 384), zero).reshape(N_HEADS, LANE, 384)
    return jnp.concatenate([w[:, :128], rot], axis=1).reshape(1024, 384)


def _w_uq_cut(g):
    g = g.reshape(N_HEADS, 256, 384)
    rot = _cut_rot_rows(g[:, 128:].reshape(N_HEADS * LANE, 384)).reshape(N_HEADS, 64, 384)
    return jnp.concatenate([g[:, :128], rot], axis=1).reshape(768, 384)


def _w_ukv_perm(wt):
    return jnp.transpose(wt.reshape(N_HEADS, 2, LANE, 256), (1, 0, 2, 3)).reshape(1024, 256)


def _w_ukv_unperm(g):
    return jnp.transpose(g.reshape(2, N_HEADS, LANE, 256), (1, 0, 2, 3)).reshape(1024, 256)


def _unshard_cols(g):
    return jnp.transpose(g, (1, 0, 2)).reshape(g.shape[1], N_DEV * g.shape[2])


def _rope_tables():
    rows = SEQ // GRID_W
    row = jnp.repeat(jnp.arange(rows, dtype=F32), GRID_W)
    col = jnp.tile(jnp.arange(GRID_W, dtype=F32), rows)
    freq = ROPE_BASE ** (-jnp.arange(16, dtype=F32) / 16)
    ang = jnp.concatenate([row[:, None] * freq, col[:, None] * freq], axis=-1)
    cos, sin = jnp.cos(ang), jnp.sin(ang)
    z = jnp.zeros((SEQ, 32), F32)
    return jnp.concatenate([cos, z, cos, z], axis=1), jnp.concatenate([-sin, z, sin, z], axis=1)


_PACKED = (("g_attn", 1024), ("g_ffn", 1024), ("ret_decay_fwd", 4), ("ret_decay_bwd", 4), ("g_ret", 512),
           ("g_q_lora", 384), ("g_kv_lora", 256), ("g_final", 1024))
_PACK_OFF = {}
_off = 0
for _name, _n in _PACKED:
    _PACK_OFF[_name] = _off
    _off += -(-_n // LANE) * LANE
PACK_W = _off


def _pack_small(vals):
    parts = []
    for name, n in _PACKED:
        a = vals[name].reshape(-1).astype(F32)
        parts.append(jnp.pad(a, (0, -(-n // LANE) * LANE - n)))
    return jnp.concatenate(parts).reshape(1, PACK_W)


def _adamw_small(params, packed, gcc, gb_ada):
    names = list(params)
    n_p = len(names)

    def body(*refs):
        p_ref, gcc_ref, gb_ref = refs[3 * n_p:3 * n_p + 3]
        outs = refs[3 * n_p + 3:]
        for k, name in enumerate(names):
            w_ref, m_ref, v_ref = refs[3 * k:3 * k + 3]
            n = w_ref.shape[1]
            if name == "b_ada":
                g = jnp.concatenate([gb_ref[d, 0:1, :] for d in range(N_DEV)], axis=-1)
            elif name == "c_ctx":
                g = gcc_ref[0, 0:1, :]
                for d in range(1, N_DEV):
                    g = g + gcc_ref[d, 0:1, :]
            else:
                off = _PACK_OFF[name]
                g = p_ref[0, :, off:off + n]
                for d in range(1, N_DEV):
                    g = g + p_ref[d, :, off:off + n]
            mn = ADAM_B1 * m_ref[...] + (1.0 - ADAM_B1) * g
            vn = ADAM_B2 * v_ref[...] + (1.0 - ADAM_B2) * jnp.square(g)
            m_hat = mn / (1.0 - ADAM_B1 ** ADAM_STEP)
            v_hat = vn / (1.0 - ADAM_B2 ** ADAM_STEP)
            outs[4 * k][...] = g
            outs[4 * k + 1][...] = -ADAM_LR * (m_hat / (jnp.sqrt(v_hat) + ADAM_EPS) + ADAM_WD * w_ref[...])
            outs[4 * k + 2][...] = mn
            outs[4 * k + 3][...] = vn

    args = [a for name in names for a in params[name]] + [packed, gcc, gb_ada]
    out_shape = [_sds(params[name][0].shape) for name in names for _ in range(4)]
    outs = pl.pallas_call(body, name="adamw_small", out_shape=out_shape, compiler_params=_cp())(*args)
    return {name: list(outs[4 * k:4 * k + 4]) for k, name in enumerate(names)}


def kernel(x, c, ctx, c_ctx, w_ada, b_ada, g_attn, g_ffn, w_in, ret_decay_fwd, ret_decay_bwd, g_ret, g_q_lora, w_uq, g_kv_lora, w_ukv, w_out, w_ff1, w_ff2, g_final, loss_target, m_c_ctx, m_w_ada, m_b_ada, m_g_attn, m_g_ffn, m_w_in, m_ret_decay_fwd, m_ret_decay_bwd, m_g_ret, m_g_q_lora, m_w_uq, m_g_kv_lora, m_w_ukv, m_w_out, m_w_ff1, m_w_ff2, m_g_final, v_c_ctx, v_w_ada, v_b_ada, v_g_attn, v_g_ffn, v_w_in, v_ret_decay_fwd, v_ret_decay_bwd, v_g_ret, v_g_q_lora, v_w_uq, v_g_kv_lora, v_w_ukv, v_w_out, v_w_ff1, v_w_ff2, v_g_final):
    me = 4 * lax.axis_index("x") + 2 * lax.axis_index("y") + lax.axis_index("c")
    nb = x.shape[0]

    c_pad = jnp.pad(c, ((0, 8 - nb), (0, 0)))
    c_all, g_in, g_uq, g_ukv = _gather_two_level(
        [c_pad, w_in[0].T.astype(BF), w_uq[0].T.astype(BF), w_ukv[0].T.astype(BF)], "gather_weights")

    crows = jnp.concatenate([c_all[:, :nb].reshape(N_DEV * nb, D_MODEL), c_ctx[None], jnp.zeros((7, D_MODEL), F32)])
    b_blk = lax.dynamic_slice(b_ada, (0, me * 768), (1, 768))
    st_f = _exchange_start([_mod_fwd(crows, w_ada[0], b_blk), w_out[0].astype(BF), w_ff1[0].T.astype(BF),
                            w_ff2[0].astype(BF)], True, "gather_fwd_start")
    zero = st_f["token"][0, 0].astype(BF)
    ws = (*_w_in_pad(g_in.reshape(2240, D_MODEL), zero), _w_uq_pad(g_uq.reshape(768, 384), zero),
          _w_ukv_perm(g_ukv.reshape(1024, 256)))
    (mod_g,) = _exchange_wait(st_f, ws, "gather_mod_wait", [0])
    mod_all = _unshard_cols(mod_g)
    mod_mine = lax.dynamic_slice(mod_all, (me * nb, 0), (nb, 6 * D_MODEL)).reshape(nb, 6, D_MODEL)
    mod = jnp.pad(mod_mine, ((0, 0), (0, 2), (0, 0)))
    mod_c = jnp.pad(mod_all[16].reshape(1, 6, D_MODEL), ((0, 0), (0, 2), (0, 0)))

    tabs = _rope_tables()
    dec_f = ret_decay_fwd.reshape(N_HEADS, 1, 1)
    dec_b = ret_decay_bwd.reshape(N_HEADS, 1, 1)

    rkc, rvc, k_ctx, v_ctx = _k1_fwd(ctx, mod_c, g_attn, g_q_lora, g_kv_lora, ws, tabs, None, True)
    rq, rk, rv, rg, q, k_all, v_all = _k1_fwd(x, mod, g_attn, g_q_lora, g_kv_lora, ws, tabs, (k_ctx, v_ctx), False)
    o_f, o_b, sf_prev, sb_prev = _k2_fwd(rq, rk, rv, rkc, rvc, dec_f, dec_b)
    y_mla, lse = _k3_fwd(q, k_all, v_all)
    (g_out,) = _exchange_wait(st_f, y_mla, "gather_wo_wait", [1])
    wo = g_out.reshape(D_MODEL, D_MODEL)
    x_mid, h2 = _k4a_fwd(x, o_f, o_b, rg, y_mla, g_ret, wo, mod, g_ffn)
    g_ff1t, g_ff2 = _exchange_wait(st_f, x_mid, "gather_ff_wait", [2, 3])
    w1t = g_ff1t.reshape(D_FF, D_MODEL)
    dxm, dmlp, relu_a, loss_acc, dgt_f, dg_final = _k4b_mlp_loss(h2, w1t, g_ff2.reshape(D_FF, D_MODEL), x_mid, mod,
                                                                 g_final.reshape(1, D_MODEL), loss_target)

    da, dw1, dw2 = _k4d_mlp_bwd(h2, dmlp, relu_a, g_ff2)
    dx_res, do, drg, dym, dwo, dg_ret, dg_ffn, dmod_a = _k4e_bwd(x, o_f, o_b, rg, y_mla, g_ret, wo, mod, g_ffn, dxm, da,
                                                                 w1t)
    st_s = _exchange_start([dw1, dw2, dwo.reshape(N_DEV, 128, D_MODEL)], False, "scatter_grads_start")
    dq, dk_all, dv_all = _k3_bwd(q, k_all, v_all, y_mla, lse, dym, st_s["token"])
    dqf, dkf, dvf, dqb, dkb, dvb, dkc, dvc, ddf, ddb = _k2_bwd(rq, rk, rv, do, sf_prev, sb_prev, rkc, rvc, dec_f, dec_b)
    cts = [[(dqf, 0), (dqb, 0)], [(dkf, 0), (dkb, 0)], [(dvf, 0), (dvb, 0)], [(drg, 0)], [(dq, 0)],
           [(dk_all, 0)], [(dv_all, 0)]]
    cts_c = [[(dkc, 0)], [(dvc, 0)], [(dk_all, SEQ)], [(dv_all, SEQ)]]
    grad_x, accs, dmod_1, dmod_c1 = _k1_bwd(x, ctx, mod, mod_c, g_attn, g_q_lora, g_kv_lora, ws, tabs, cts, cts_c,
                                            dx_res)
    dwa, dwb, dwq, dwk, dg_attn, dg_q, dg_kv = accs

    dmod_loc = (dmod_a + dmod_1).at[:, 5, :].set(dgt_f[:, 0, :])[:, :6, :].reshape(nb, 6 * D_MODEL)
    dmod_ctx = dmod_c1[:, :6, :].reshape(1, 6 * D_MODEL)
    small = {"g_attn": dg_attn, "g_ffn": dg_ffn, "ret_decay_fwd": jnp.sum(ddf[:, :, 0, 0], axis=0),
             "ret_decay_bwd": jnp.sum(ddb[:, :, 0, 0], axis=0), "g_ret": dg_ret, "g_q_lora": dg_q, "g_kv_lora": dg_kv,
             "g_final": dg_final}
    extra = jnp.concatenate([dmod_loc, dmod_ctx, jnp.zeros((5, 6 * D_MODEL), F32)])
    ex_pieces = jnp.transpose(extra.reshape(8, N_DEV, 768), (1, 0, 2))
    st_sm = _exchange_start([_pack_small(small), ex_pieces, loss_acc], [True, False, True], "gather_small_start")
    chip_sums = _pair_reduce([_w_in_cut(dwa, dwb).reshape(N_DEV, 280, D_MODEL), _w_uq_cut(dwq).reshape(N_DEV, 96, 384),
                              _w_ukv_unperm(dwk).reshape(N_DEV, 128, 256)], "pair_reduce", st_sm["token"])
    sm_g, ex_g, loss_g = _exchange_wait(st_sm, chip_sums[0], "gather_small_wait")
    dmod_blk = jnp.concatenate([ex_g[:, :nb].reshape(N_DEV * nb, 768), jnp.zeros((8, 768), F32)])
    gw_ada, gcc_part, gb_part = _mod_bwd(crows, w_ada[0], dmod_blk, ex_g[:, nb])
    st_c = _exchange_start([gcc_part, gb_part], True, "gather_cc_start")
    p_ff1, p_ff2, p_wo = _exchange_wait(st_s, st_c["token"], "scatter_grads_wait")
    st_r = _exchange_start(chip_sums, False, "scatter_rest_start", after=p_wo, chips=True)

    res = {}
    early = (("w_ff1", w_ff1, m_w_ff1, v_w_ff1, p_ff1), ("w_ff2", w_ff2, m_w_ff2, v_w_ff2, p_ff2),
             ("w_ada", w_ada, m_w_ada, v_w_ada, gw_ada[None]), ("w_out", w_out, m_w_out, v_w_out, p_wo))
    behind = st_r["token"]
    for name, w, m, v, pcs in early:
        res[name] = [a[None] for a in _adamw(w[0], m[0], v[0], pcs, "adamw_" + name, after=behind)]
        behind = res[name][3]

    smalls = {"c_ctx": (c_ctx, m_c_ctx, v_c_ctx), "b_ada": (b_ada, m_b_ada, v_b_ada), "g_attn": (g_attn, m_g_attn, v_g_attn),
              "g_ffn": (g_ffn, m_g_ffn, v_g_ffn), "ret_decay_fwd": (ret_decay_fwd, m_ret_decay_fwd, v_ret_decay_fwd),
              "ret_decay_bwd": (ret_decay_bwd, m_ret_decay_bwd, v_ret_decay_bwd), "g_ret": (g_ret, m_g_ret, v_g_ret),
              "g_q_lora": (g_q_lora, m_g_q_lora, v_g_q_lora), "g_kv_lora": (g_kv_lora, m_g_kv_lora, v_g_kv_lora),
              "g_final": (g_final, m_g_final, v_g_final)}
    rows = {k: tuple(a.reshape(1, -1) for a in t) for k, t in smalls.items()}
    gcc_g, gb_g = _exchange_wait(st_c, behind, "gather_cc_wait")
    small_out = _adamw_small(rows, sm_g, gcc_g, gb_g)
    for name, outs in small_out.items():
        res[name] = [o.reshape(smalls[name][0].shape) for o in outs]

    pieces = _exchange_wait(st_r, small_out["g_final"][3], "scatter_rest_wait")
    for name, w, m, v, pcs in (("w_in", w_in, m_w_in, v_w_in, pieces[0]), ("w_uq", w_uq, m_w_uq, v_w_uq, pieces[1])):
        res[name] = [a.T[None] for a in _adamw(w[0].T, m[0].T, v[0].T, pcs, "adamw_" + name)]
    res["w_ukv"] = [a[None] for a in _adamw(w_ukv[0], m_w_ukv[0], v_w_ukv[0], jnp.transpose(pieces[2], (0, 2, 1)),
                                            "adamw_w_ukv")]

    loss = loss_g[0, 0, 0]
    for k in range(1, N_DEV):
        loss = loss + loss_g[k, 0, 0]

    order = ("c_ctx", "w_ada", "b_ada", "g_attn", "g_ffn", "w_in", "ret_decay_fwd", "ret_decay_bwd", "g_ret", "g_q_lora",
             "w_uq", "g_kv_lora", "w_ukv", "w_out", "w_ff1", "w_ff2", "g_final")
    return (loss, grad_x, *[res[n][0] for n in order], *[res[n][1] for n in order], *[res[n][2] for n in order],
            *[res[n][3] for n in order])
```

```python
import functools
import math

import jax
import jax.numpy as jnp
from jax import lax
from jax.experimental import pallas as pl
from jax.experimental.pallas import tpu as pltpu

F32 = jnp.float32
BF = jnp.bfloat16
EPS = 1e-6
LANE = 128
LOG2_E = 1.0 / math.log(2.0)
N_DEV = 8
D_MODEL = 1024
SEQ = 2048
CTX_LEN = 256
GRID_W = 64
N_HEADS = 4
RET_CHUNK = 512
N_CHUNK = SEQ // RET_CHUNK
D_FF = 4096
FF_BLK = D_FF // N_DEV
IN_PAD = 2816
KV_LEN = CTX_LEN + SEQ
ROPE_BASE = 10000.0
ADAM_LR, ADAM_B1, ADAM_B2, ADAM_EPS, ADAM_WD, ADAM_STEP = 0.001, 0.9, 0.999, 1e-08, 0.01, 10
TOK = 512
TOK_B = 256
VMEM_LIMIT = 56 * 1024 * 1024
ARB = "arbitrary"
MESH = pl.DeviceIdType.MESH
_HEAD_SL = [slice(LANE * h, LANE * (h + 1)) for h in range(N_HEADS)]
W_SHAPES = [(2048, D_MODEL), (768, D_MODEL), (1024, 384), (1024, 256)]


def _dot(a, b, ca, cb):
    return lax.dot_general(a.astype(BF), b.astype(BF), (((ca,), (cb,)), ((), ())), preferred_element_type=F32)


@jax.custom_vjp
def mm(a, b):
    return _dot(a, b, 1, 0)


@jax.custom_vjp
def mm_nt(a, b):
    return _dot(a, b, 1, 1)


@jax.custom_vjp
def mm_tn(a, b):
    return _dot(a, b, 0, 0)


mm.defvjp(lambda a, b: (_dot(a, b, 1, 0), (a, b)), lambda r, g: (mm_nt(g, r[1]), mm_tn(r[0], g)))
mm_nt.defvjp(lambda a, b: (_dot(a, b, 1, 1), (a, b)), lambda r, g: (mm(g, r[1]), mm_tn(g, r[0])))
mm_tn.defvjp(lambda a, b: (_dot(a, b, 0, 0), (a, b)), lambda r, g: (mm_nt(r[1], g), mm(r[0], g)))


@jax.custom_vjp
def _mmw(a, w, probe):
    return _dot(a, w, 1, 0)


def _mmw_bwd(r, g):
    a, w = r
    return mm_nt(g, w), jnp.zeros_like(w), mm_tn(a, g)


_mmw.defvjp(lambda a, w, probe: (_dot(a, w, 1, 0), (a, w)), _mmw_bwd)


@jax.custom_vjp
def _mmwt(a, wt, probe):
    return _dot(a, wt, 1, 1)


_mmwt.defvjp(lambda a, wt, probe: (_dot(a, wt, 1, 1), (a, wt)),
             lambda r, g: (mm(g, r[1]), jnp.zeros_like(r[1]), mm_tn(g, r[0])))


def mmwt(a, wt, probe):
    return _dot(a, wt, 1, 1) if probe is None else _mmwt(a, wt, probe)


def mmw(a, w, probe):
    return _dot(a, w, 1, 0) if probe is None else _mmw(a, w, probe)


def rmsn(x, g):
    return x * lax.rsqrt(jnp.mean(x * x, axis=-1, keepdims=True) + EPS) * g


def silu(x):
    return x * jax.nn.sigmoid(x)


def _swap_halves_impl(x):
    return pltpu.roll(x, 64, 1)


@jax.custom_vjp
def swap_halves(x):
    return _swap_halves_impl(x)


swap_halves.defvjp(lambda x: (_swap_halves_impl(x), None), lambda _, g: (_swap_halves_impl(g),))


def rope(x, cs1, sn1, every=1):
    blocks = []
    for i in range(x.shape[-1] // LANE):
        xb = x[:, LANE * i:LANE * (i + 1)]
        blocks.append(xb * cs1 + swap_halves(xb) * sn1 if i % every == every - 1 else xb)
    return blocks[0] if len(blocks) == 1 else jnp.concatenate(blocks, axis=-1)


def k1_tile(x, sh, sc, g_attn, g_q, g_kv, ws, ps, tabs, is_ctx):
    w_a, w_b, w_uq, w_ukv = ws
    p_a, p_b, p_uq, p_ukv = ps
    cs1, sn1 = tabs
    h = rmsn(x, g_attn) * (1.0 + sc) + sh
    pa = mmwt(h, w_a, p_a)
    pb = mmwt(h, w_b, p_b)
    rk = pa[:, 512:1024] * 0.125
    rv = pa[:, 1024:1536]
    kpe = pb[:, 640:768]
    kv = mmwt(rmsn(pb[:, 384:640], g_kv), w_ukv, p_ukv)
    if not is_ctx:
        rk = rope(rk, cs1, sn1)
        kpe = rope(kpe, cs1, sn1)
    k_full = jnp.concatenate([piece for sl in _HEAD_SL for piece in (kv[:, sl], kpe)], axis=-1)
    v = kv[:, 512:]
    if is_ctx:
        return rk, rv, k_full, v
    rq = rope(pa[:, 0:512], cs1, sn1)
    rg = pa[:, 1536:2048]
    q = rope(mmwt(rmsn(pb[:, 0:384], g_q), w_uq, p_uq), cs1, sn1, every=2)
    return rq, rk, rv, rg, q, k_full, v


def log_sigmoid(x):
    return jnp.minimum(x, 0.0) - jnp.log(1.0 + jnp.exp(-jnp.abs(x)))


def _distance(reverse):
    c = RET_CHUNK
    ii = lax.broadcasted_iota(jnp.int32, (c, c), 0).astype(F32)
    jj = lax.broadcasted_iota(jnp.int32, (c, c), 1).astype(F32)
    return (jj - ii) if reverse else (ii - jj)


def decay_mask(lg, reverse):
    diff = _distance(reverse)
    return jnp.where(diff >= 0, jnp.exp2((lg * LOG2_E) * jnp.maximum(diff, 0.0)), 0.0)


@functools.partial(jax.custom_vjp, nondiff_argnums=(2,))
def kept_decay_mask(lg, mask, reverse):
    return mask


def _kept_decay_mask_fwd(lg, mask, reverse):
    return mask, mask


def _kept_decay_mask_bwd(reverse, mask, g):
    return jnp.sum(g * mask * _distance(reverse), keepdims=True).reshape(1, 1), jnp.zeros_like(mask)


kept_decay_mask.defvjp(_kept_decay_mask_fwd, _kept_decay_mask_bwd)


def decay_rows(lg, reverse):
    c = RET_CHUNK
    pos = lax.broadcasted_iota(jnp.int32, (c, 1), 0).astype(F32)
    if reverse:
        return jnp.exp(lg * pos), jnp.exp(lg * (c - pos))
    return jnp.exp(lg * (c - 1.0 - pos)), jnp.exp(lg * (pos + 1.0))


def ret_chunk(q, k, v, s, lg, reverse, pre=None):
    c = RET_CHUNK
    dec, wk, wq = (decay_mask(lg, reverse), *decay_rows(lg, reverse)) if pre is None else pre
    o = mm(mm_nt(q, k) * dec, v) + mm(q * wq, s)
    s_next = jnp.exp(lg * float(c)) * s + mm_tn(k * wk, v)
    return o, s_next


def ctx_state(kc, vc, lg, reverse):
    n = kc.shape[0]
    pos = lax.broadcasted_iota(jnp.int32, (n, 1), 0).astype(F32)
    w = jnp.exp(lg * pos) if reverse else jnp.exp(lg * (n - 1.0 - pos))
    return mm_tn(kc * w, vc)


def attn_head(qn, qp, kn, kp, v):
    s = (mm_nt(qn, kn) + mm_nt(qp, kp)) * (1.0 / math.sqrt(192.0))
    e = jnp.exp(s - jnp.max(s, axis=-1, keepdims=True))
    return mm(e / jnp.sum(e, axis=-1, keepdims=True), v)


def gn_gate(o, rg, g_ret):
    ys = []
    for h in range(N_HEADS):
        sl = slice(LANE * h, LANE * (h + 1))
        oh = o[:, sl]
        mu = jnp.mean(oh, axis=-1, keepdims=True)
        var = jnp.mean(jnp.square(oh - mu), axis=-1, keepdims=True)
        ys.append((oh - mu) * lax.rsqrt(var + EPS) * g_ret[:, sl])
    return jnp.concatenate(ys, axis=-1) * silu(rg)


def k4a_tile(x, o_f, o_b, rg, y_mla, g_ret, gt_a, g_ffn, sh_f, sc_f, w_out, p_out):
    mix = jnp.concatenate([gn_gate(o_f + o_b, rg, g_ret), y_mla], axis=-1)
    x_mid = x + gt_a * mmw(mix, w_out, p_out)
    h2 = rmsn(x_mid, g_ffn) * (1.0 + sc_f) + sh_f
    return x_mid, h2


def k4c_tile(x_mid, mlp, gt_f, g_final, tgt):
    y = rmsn(x_mid + gt_f * mlp, g_final)
    per_tok = jnp.mean(jnp.square(y - tgt), axis=-1, keepdims=True)
    return 0.5 * jnp.sum(per_tok, axis=0, keepdims=True)


def _cp(sem=None, vmem=VMEM_LIMIT):
    return pltpu.CompilerParams(dimension_semantics=sem, vmem_limit_bytes=vmem)


def _acc(ref, val, first):
    @pl.when(first)
    def _():
        ref[...] = val

    @pl.when(jnp.logical_not(first))
    def _():
        ref[...] += val


def _full(shape):
    nd = len(shape)
    return pl.BlockSpec(shape, lambda *_: (0,) * nd)


ANY = pl.BlockSpec(memory_space=pl.ANY)


def _sds(shape, dtype=F32):
    return jax.ShapeDtypeStruct(shape, dtype)


def _exchange(arrs, gather, name):
    n = len(arrs)
    modes = [gather] * n if isinstance(gather, bool) else list(gather)
    out_shape = [_sds(((N_DEV,) + a.shape) if g else a.shape, a.dtype) for a, g in zip(arrs, modes)]

    def body(*refs):
        ins, outs = refs[:n], refs[n:2 * n]
        send_sems, recv_sems, local_sems = refs[2 * n:]
        x, y, c = lax.axis_index("x"), lax.axis_index("y"), lax.axis_index("c")
        me = 4 * x + 2 * y + c
        sends, recvs, locs = [], [], []
        for i in range(n):
            gather = modes[i]
            for k in range(N_DEV - 1):
                bits = k + 1
                px = x ^ ((bits >> 2) & 1)
                py = y ^ ((bits >> 1) & 1)
                pc = c ^ (bits & 1)
                peer = 4 * px + 2 * py + pc
                src = ins[i] if gather else ins[i].at[peer]
                sem = i * (N_DEV - 1) + k
                sends.append(pltpu.make_async_remote_copy(
                    src_ref=src, dst_ref=outs[i].at[me], send_sem=send_sems.at[sem], recv_sem=recv_sems.at[sem],
                    device_id=(px, py, pc), device_id_type=MESH))
                recvs.append(pltpu.make_async_remote_copy(
                    src_ref=src, dst_ref=outs[i].at[peer], send_sem=send_sems.at[sem], recv_sem=recv_sems.at[sem],
                    device_id=(px, py, pc), device_id_type=MESH))
            locs.append(pltpu.make_async_copy(ins[i] if gather else ins[i].at[me], outs[i].at[me], local_sems.at[i]))
        for cp in locs + sends:
            cp.start()
        for cp in recvs:
            cp.wait_recv()
        for cp in sends:
            cp.wait_send()
        for cp in locs:
            cp.wait()

    outs = pl.pallas_call(
        body, name=name, out_shape=out_shape, in_specs=[ANY] * n, out_specs=[ANY] * n,
        scratch_shapes=[pltpu.SemaphoreType.DMA((n * (N_DEV - 1),)), pltpu.SemaphoreType.DMA((n * (N_DEV - 1),)),
                        pltpu.SemaphoreType.DMA((n,))],
    )(*arrs)
    return list(outs)


def _gather_two_level(arrs, name):
    n = len(arrs)

    def body(*refs):
        ins, outs = refs[:n], refs[n:2 * n]
        send_sems, recv_sems, local_sems = refs[2 * n:]
        x, y, c = lax.axis_index("x"), lax.axis_index("y"), lax.axis_index("c")
        sibling = (x, y, 1 - c)
        chips = [(1 - x, y), (x, 1 - y), (1 - x, 1 - y)]

        def slot(px, py, pc):
            return 4 * px + 2 * py + pc

        first, passed, waits, locs = [], [], [], []
        for i in range(n):
            def copy(k, block, to, src=None, i=i):
                dst = outs[i].at[slot(*block)]
                return pltpu.make_async_remote_copy(
                    src_ref=dst if src is None else src, dst_ref=dst, send_sem=send_sems.at[7 * i + k],
                    recv_sem=recv_sems.at[7 * i + k], device_id=to, device_id_type=MESH)

            locs.append(pltpu.make_async_copy(ins[i], outs[i].at[slot(x, y, c)], local_sems.at[i]))
            first.append(copy(0, (x, y, c), sibling, src=ins[i]))
            first += [copy(1 + j, (x, y, c), (*chip, c), src=ins[i]) for j, chip in enumerate(chips)]
            passed.append([copy(4 + j, (*chip, c), sibling) for j, chip in enumerate(chips)])
            waits.append([copy(1 + j, (*chip, c), (x, y, c)) for j, chip in enumerate(chips)])
        for cp in locs + first:
            cp.start()
        for j in range(3):
            for i in range(n):
                waits[i][j].wait_recv()
                passed[i][j].start()
        for i in range(n):
            def arrival(k, block, i=i):
                dst = outs[i].at[slot(*block)]
                return pltpu.make_async_remote_copy(
                    src_ref=dst, dst_ref=dst, send_sem=send_sems.at[7 * i + k], recv_sem=recv_sems.at[7 * i + k],
                    device_id=sibling, device_id_type=MESH)

            arrival(0, (x, y, 1 - c)).wait_recv()
            for j, chip in enumerate(chips):
                arrival(4 + j, (*chip, 1 - c)).wait_recv()
        for cp in first + [p for ps in passed for p in ps]:
            cp.wait_send()
        for cp in locs:
            cp.wait()

    outs = pl.pallas_call(
        body, name=name, out_shape=[_sds((N_DEV,) + a.shape, a.dtype) for a in arrs], in_specs=[ANY] * n,
        out_specs=[ANY] * n,
        scratch_shapes=[pltpu.SemaphoreType.DMA((7 * n,)), pltpu.SemaphoreType.DMA((7 * n,)),
                        pltpu.SemaphoreType.DMA((n,))],
    )(*arrs)
    return list(outs)


HBM = pl.BlockSpec(memory_space=pltpu.HBM)
SEM = pl.BlockSpec(memory_space=pltpu.SEMAPHORE)
EFFECT = pltpu.SideEffectType.DATAFLOW_SIDE_EFFECTING


def _peer(k, chips=False):
    x, y, c = lax.axis_index("x"), lax.axis_index("y"), lax.axis_index("c")
    bits = (k + 1) << 1 if chips else k + 1
    px, py, pc = x ^ ((bits >> 2) & 1), y ^ ((bits >> 1) & 1), c ^ (bits & 1)
    if chips:
        return (px, py, pc), 2 * px + py, 2 * x + y
    return (px, py, pc), 4 * px + 2 * py + pc, 4 * x + 2 * y + c


def _exchange_start(arrs, gather, name, after=None, chips=False):
    n = len(arrs)
    n_peer, n_slot = (3, 4) if chips else (N_DEV - 1, N_DEV)
    modes = [gather] * n if isinstance(gather, bool) else list(gather)
    lands = [pltpu.with_memory_space_constraint(lax.empty(((n_slot,) + a.shape) if g else a.shape, a.dtype), pltpu.HBM)
             for a, g in zip(arrs, modes)]
    srcs = [pltpu.with_memory_space_constraint(a, pltpu.HBM) for a in arrs]

    extra = [] if after is None else [after]

    def body(*refs):
        ins, zones = refs[:n], refs[n:2 * n]
        send_sems, recv_sems, local_sems = refs[2 * n + len(extra):2 * n + len(extra) + 3]
        token = refs[-1]
        for i in range(n):
            gather = modes[i]
            for k in range(n_peer):
                dev, peer, me = _peer(k, chips)
                sem = i * n_peer + k
                pltpu.make_async_remote_copy(
                    src_ref=ins[i] if gather else ins[i].at[peer], dst_ref=zones[i].at[me],
                    send_sem=send_sems.at[sem], recv_sem=recv_sems.at[sem], device_id=dev, device_id_type=MESH).start()
            _, _, me = _peer(0, chips)
            pltpu.make_async_copy(ins[i] if gather else ins[i].at[me], zones[i].at[me], local_sems.at[i]).start()
        token[...] = jnp.zeros_like(token)

    nsem = n * n_peer
    outs = pl.pallas_call(
        body, name=name,
        out_shape=[pltpu.SemaphoreType.DMA((nsem,)), pltpu.SemaphoreType.DMA((nsem,)), pltpu.SemaphoreType.DMA((n,))]
        + [pltpu.HBM(a.shape, a.dtype) for a in srcs] + [pltpu.HBM(z.shape, z.dtype) for z in lands]
        + [_sds((8, LANE))],
        in_specs=[HBM] * (2 * n) + [ANY] * len(extra),
        out_specs=[SEM, SEM, SEM] + [HBM] * (2 * n) + [pl.BlockSpec(memory_space=pltpu.VMEM)],
        input_output_aliases={i: 3 + i for i in range(2 * n)},
        compiler_params=pltpu.CompilerParams(has_side_effects=EFFECT),
    )(*srcs, *lands, *extra)
    return {"n": n, "gather": modes, "chips": chips, "sems": outs[:3], "srcs": outs[3:3 + n],
            "lands": outs[3 + n:3 + 2 * n], "token": outs[-1]}


def _exchange_wait(st, after, name, which=None):
    modes, chips = st["gather"], st["chips"]
    afters = list(after) if isinstance(after, (list, tuple)) else [after]
    which = list(range(st["n"])) if which is None else which
    n = len(which)
    n_peer = 3 if chips else N_DEV - 1
    srcs, lands = [st["srcs"][i] for i in which], [st["lands"][i] for i in which]

    def body(*refs):
        ins, zones = refs[:n], refs[n:2 * n]
        send_sems, recv_sems, local_sems = refs[2 * n:2 * n + 3]
        for j, i in enumerate(which):
            gather = modes[i]
            for k in range(n_peer):
                dev, peer, me = _peer(k, chips)
                sem = i * n_peer + k
                src = ins[j] if gather else ins[j].at[peer]
                cp = pltpu.make_async_remote_copy(
                    src_ref=src, dst_ref=zones[j].at[peer], send_sem=send_sems.at[sem], recv_sem=recv_sems.at[sem],
                    device_id=dev, device_id_type=MESH)
                cp.wait_send()
                cp.wait_recv()
            _, _, me = _peer(0, chips)
            pltpu.make_async_copy(ins[j] if gather else ins[j].at[me], zones[j].at[me], local_sems.at[i]).wait()

    outs = pl.pallas_call(
        body, name=name,
        out_shape=[pltpu.HBM(a.shape, a.dtype) for a in srcs] + [pltpu.HBM(z.shape, z.dtype) for z in lands],
        in_specs=[HBM] * (2 * n) + [SEM, SEM, SEM] + [ANY] * len(afters), out_specs=[HBM] * (2 * n),
        input_output_aliases={i: i for i in range(2 * n)},
        compiler_params=pltpu.CompilerParams(has_side_effects=EFFECT),
    )(*srcs, *lands, *st["sems"], *afters)
    return list(outs[n:])


def _pair_reduce(arrs, name, after):
    n = len(arrs)

    def body(*refs):
        ins, refs = refs[:n], refs[n + 1:]
        outs, got, mine = refs[:n], refs[n:2 * n], refs[2 * n:3 * n]
        send_sems, recv_sems, local_sems = refs[3 * n:]
        x, y, c = lax.axis_index("x"), lax.axis_index("y"), lax.axis_index("c")
        sends, locs = [], []
        for i in range(n):
            for q in range(4):
                sem = 4 * i + q
                sends.append(pltpu.make_async_remote_copy(
                    src_ref=ins[i].at[2 * q + 1 - c], dst_ref=got[i].at[q], send_sem=send_sems.at[sem],
                    recv_sem=recv_sems.at[sem], device_id=(x, y, 1 - c), device_id_type=MESH))
                locs.append(pltpu.make_async_copy(ins[i].at[2 * q + c], mine[i].at[q], local_sems.at[sem]))
        for cp in locs + sends:
            cp.start()
        for cp in sends:
            cp.wait_recv()
        for cp in locs:
            cp.wait()
        for i in range(n):
            outs[i][...] = (mine[i][...].astype(F32) + got[i][...].astype(F32)).astype(BF)
        for cp in sends:
            cp.wait_send()

    half = [(4,) + a.shape[1:] for a in arrs]
    outs = pl.pallas_call(
        body, name=name, out_shape=[_sds(h, BF) for h in half], in_specs=[ANY] * (n + 1),
        out_specs=[pl.BlockSpec(memory_space=pltpu.VMEM)] * n,
        scratch_shapes=[pltpu.VMEM(h, BF) for h in half] * 2
        + [pltpu.SemaphoreType.DMA((4 * n,)), pltpu.SemaphoreType.DMA((4 * n,)), pltpu.SemaphoreType.DMA((4 * n,))],
        compiler_params=_cp(),
    )(*arrs, after)
    return list(outs)


def _mod_fwd(crows, w_ada, b_blk):
    def body(c_ref, w_ref, b_ref, o_ref):
        o_ref[...] = mm(silu(c_ref[...]), w_ref[...]) + b_ref[...]

    return pl.pallas_call(body, name="mod_fwd", out_shape=_sds((24, 768)), compiler_params=_cp())(crows, w_ada, b_blk)


def _mod_bwd(crows, w_ada, dmod_blk, dmodc_blk):
    def body(c_ref, w_ref, d_ref, dc_ref, gw_ref, gc_ref, gb_ref):
        cr = c_ref[...]
        dc = dc_ref[0:1, :]
        for p in range(1, N_DEV):
            dc = dc + dc_ref[p:p + 1, :]
        row = lax.broadcasted_iota(jnp.int32, (24, 1), 0)
        gw_ref[...] = mm_tn(silu(cr), jnp.where(row == 16, dc, d_ref[...]))
        cc = cr[16:17, :]
        sg = jax.nn.sigmoid(cc)
        part = mm_nt(jnp.broadcast_to(dc, (8, 768)), w_ref[...])
        gc_ref[...] = part * (sg * (1.0 + cc * (1.0 - sg)))
        gb_ref[...] = jnp.broadcast_to(jnp.sum(d_ref[...], axis=0, keepdims=True) + dc, (8, 768))

    return pl.pallas_call(
        body, name="mod_bwd", out_shape=[_sds((D_MODEL, 768)), _sds((8, D_MODEL)), _sds((8, 768))],
        compiler_params=_cp())(crows, w_ada, dmod_blk, dmodc_blk)


def _tab_specs(tk):
    return [pl.BlockSpec((tk, LANE), lambda i, t: (t, 0))] * 2


def _k1_fwd(x, mod, g_attn, g_q, g_kv, ws, tabs, kv_all, is_ctx):
    b, l, _ = x.shape
    tk = CTX_LEN if is_ctx else TOK
    nt = l // tk
    n_f32 = 2 if is_ctx else 4

    def body(x_ref, mod_ref, ga_ref, gq_ref, gk_ref, wa_ref, wb_ref, wq_ref, wk_ref, cs_ref, sn_ref, *rest):
        outs = rest if is_ctx else rest[2:]
        res = k1_tile(x_ref[...], mod_ref[0:1, :], mod_ref[1:2, :], ga_ref[...], gq_ref[...], gk_ref[...],
                      (wa_ref[...], wb_ref[...], wq_ref[...], wk_ref[...]), (None,) * 4,
                      (cs_ref[...], sn_ref[...]), is_ctx)
        for o_ref, r in zip(outs, res):
            o_ref[...] = r.astype(o_ref.dtype)

    tok = lambda w, off=0: pl.BlockSpec((None, tk, w), lambda i, t: (i, t + off, 0))
    mod_spec = pl.BlockSpec((None, 8, D_MODEL), (lambda i, t: (0, 0, 0)) if is_ctx else (lambda i, t: (i, 0, 0)))
    kv_off = SEQ // tk if is_ctx else 0
    in_specs = ([tok(D_MODEL), mod_spec, _full((1, D_MODEL)), _full((1, 384)), _full((1, 256))]
                + [_full(s) for s in W_SHAPES] + _tab_specs(tk))
    args = [x, mod, g_attn, g_q, g_kv, *ws, *tabs]
    out_specs = [tok(512)] * n_f32 + ([] if is_ctx else [tok(1024)]) + [tok(1024, kv_off), tok(512, kv_off)]
    out_shape = ([_sds((b, l, 512))] * n_f32 + ([] if is_ctx else [_sds((b, l, 1024), BF)])
                 + [_sds((b, KV_LEN, 1024), BF), _sds((b, KV_LEN, 512), BF)])
    aliases = {}
    if not is_ctx:
        aliases = {len(args): n_f32 + 1, len(args) + 1: n_f32 + 2}
        in_specs += [ANY, ANY]
        args += list(kv_all)
    return pl.pallas_call(
        body, name="k1_fwd_ctx" if is_ctx else "k1_fwd", grid=(b, nt), in_specs=in_specs, out_specs=out_specs,
        out_shape=out_shape, input_output_aliases=aliases, compiler_params=_cp((ARB, ARB)),
    )(*args)


N_ACC = 7


def _k1_bwd(x, ctx, mod, mod_c, g_attn, g_q, g_kv, ws, tabs, cts, cts_c, dx_res):
    b, l, _ = x.shape
    tk = TOK_B
    nt = l // tk
    flat = [[a for group in c for a in group] for c in (cts, cts_c)]
    sizes = [[len(g) for g in c] for c in (cts, cts_c)]
    acc_shapes = W_SHAPES + [(1, D_MODEL), (1, 384), (1, 256)]

    def body(*refs):
        it = iter(refs)
        x_ref, c_ref, mod_ref, modc_ref, ga_ref, gq_ref, gk_ref = [next(it) for _ in range(7)]
        w_hbm = [next(it) for _ in range(4)]
        tab_refs = [next(it) for _ in range(2)]
        ct_refs = [[next(it) for _ in f] for f in flat]
        res_ref, gx_ref = next(it), next(it)
        out_hbm = [next(it) for _ in range(N_ACC)]
        dmod_ref, dmodc_ref = next(it), next(it)
        w_vmem = [next(it) for _ in range(4)]
        accs = [next(it) for _ in range(N_ACC)]
        sem = next(it)
        i, t = pl.program_id(0), pl.program_id(1)
        first = jnp.logical_and(i == 0, t == 0)

        @pl.when(first)
        def _():
            for src, dst in zip(w_hbm, w_vmem):
                pltpu.sync_copy(src, dst)
            for k in range(N_ACC):
                accs[k][...] = jnp.zeros(acc_shapes[k], F32)

        def tile(is_ctx):
            which = 1 if is_ctx else 0
            ct_vals, pos = [], 0
            for gsz in sizes[which]:
                v = ct_refs[which][pos][...].astype(F32)
                for r in ct_refs[which][pos + 1:pos + gsz]:
                    v = v + r[...]
                ct_vals.append(v)
                pos += gsz
            wv = tuple(r[...] for r in w_vmem)
            tv = tuple(r[...] for r in tab_refs)
            m_ref = modc_ref if is_ctx else mod_ref

            def f(xv, sh, sc, ga, gq, gk, *probes):
                return k1_tile(xv, sh, sc, ga, gq, gk, wv, probes, tv, is_ctx)

            probes = [jnp.zeros(s, F32) for s in W_SHAPES]
            xin = c_ref[...] if is_ctx else x_ref[...]
            _, vjp = jax.vjp(f, xin, m_ref[0:1, :], m_ref[1:2, :], ga_ref[...], gq_ref[...], gk_ref[...], *probes)
            dx, dsh, dsc, dga, dgq, dgk, dwa, dwb, dwq, dwk = vjp(tuple(ct_vals))
            for ref, val in zip(accs, (dwa, dwb, dwq, dwk, dga, dgq, dgk)):
                ref[...] += val
            return dx, dsh, dsc

        @pl.when(t == 0)
        def _():
            _, dsh, dsc = tile(True)
            _acc(dmodc_ref.at[0:1, :], dsh, i == 0)
            _acc(dmodc_ref.at[1:2, :], dsc, i == 0)

            @pl.when(i == 0)
            def _():
                dmodc_ref[2:8, :] = jnp.zeros((6, D_MODEL), F32)

        @pl.when(t > 0)
        def _():
            dx, dsh, dsc = tile(False)
            gx_ref[...] = dx + res_ref[...]
            _acc(dmod_ref.at[0:1, :], dsh, t == 1)
            _acc(dmod_ref.at[1:2, :], dsc, t == 1)

            @pl.when(t == 1)
            def _():
                dmod_ref[2:8, :] = jnp.zeros((6, D_MODEL), F32)

        @pl.when(jnp.logical_and(i == b - 1, t == nt))
        def _():
            for k in range(4):
                w_vmem[k][...] = accs[k][...].astype(BF)
            cps = [pltpu.make_async_copy(w_vmem[k] if k < 4 else accs[k], out_hbm[k], sem.at[k]) for k in range(N_ACC)]
            for cp in cps:
                cp.start()
            for cp in cps:
                cp.wait()

    lat = lambda w, off=0: pl.BlockSpec((None, tk, w), lambda i, t: (i, jnp.maximum(t - 1, 0) + off, 0))
    con = lambda w, off=0: pl.BlockSpec((None, tk, w), lambda i, t: (i, off, 0))
    mod_spec = pl.BlockSpec((None, 8, D_MODEL), lambda i, t: (i, 0, 0))
    modc_spec = pl.BlockSpec((None, 8, D_MODEL), lambda i, t: (0, 0, 0))
    tab_spec = pl.BlockSpec((tk, LANE), lambda i, t: (jnp.maximum(t - 1, 0), 0))
    in_specs = ([lat(D_MODEL), con(D_MODEL), mod_spec, modc_spec, _full((1, D_MODEL)), _full((1, 384)), _full((1, 256))]
                + [ANY] * 4 + [tab_spec] * 2)
    args = [x, ctx, mod, mod_c, g_attn, g_q, g_kv, *ws, *tabs]
    for a, off in flat[0]:
        in_specs.append(lat(a.shape[-1], off // tk))
        args.append(a)
    for a, off in flat[1]:
        in_specs.append(con(a.shape[-1], off // tk))
        args.append(a)
    in_specs.append(lat(D_MODEL))
    args.append(dx_res)
    out_shape = ([_sds((b, l, D_MODEL))] + [_sds(s, BF) for s in W_SHAPES] + [_sds(s) for s in acc_shapes[4:]]
                 + [_sds((b, 8, D_MODEL)), _sds((1, 8, D_MODEL))])
    out_specs = [lat(D_MODEL)] + [ANY] * N_ACC + [mod_spec, modc_spec]
    outs = pl.pallas_call(
        body, name="k1_bwd", grid=(b, nt + 1), in_specs=in_specs, out_specs=out_specs, out_shape=out_shape,
        scratch_shapes=[pltpu.VMEM(s, BF) for s in W_SHAPES] + [pltpu.VMEM(s, F32) for s in acc_shapes]
        + [pltpu.SemaphoreType.DMA((N_ACC,))],
        compiler_params=_cp((ARB, ARB)),
    )(*args)
    return outs[0], list(outs[1:1 + N_ACC]), outs[1 + N_ACC], outs[2 + N_ACC]


def _chunk_spec(rev):
    if rev:
        return pl.BlockSpec((None, RET_CHUNK, 512), lambda i, n: (i, N_CHUNK - 1 - n, 0))
    return pl.BlockSpec((None, RET_CHUNK, 512), lambda i, n: (i, n, 0))


def _state_spec(rev):
    if rev:
        return pl.BlockSpec((None, N_HEADS, None, LANE, LANE), lambda i, n: (i, 0, N_CHUNK - 1 - n, 0, 0))
    return pl.BlockSpec((None, N_HEADS, None, LANE, LANE), lambda i, n: (i, 0, n, 0, 0))


_CTX_SPEC = pl.BlockSpec((None, CTX_LEN, 512), lambda i, n: (i, 0, 0))
_DEC_SPEC = pl.BlockSpec((N_HEADS, 1, 1), lambda i, n: (0, 0, 0))


def _k2_fwd(rq, rk, rv, rkc, rvc, dec_f, dec_b):
    b = rq.shape[0]

    def body(qf, kf, vf, qb, kb, vb, kc, vc, df, db, of_ref, ob_ref, sf_out, sb_out, sf, sb, masks, rows):
        n = pl.program_id(1)
        first = jnp.logical_and(pl.program_id(0) == 0, n == 0)
        for h, sl in enumerate(_HEAD_SL):
            lgf, lgb = log_sigmoid(df[h]), log_sigmoid(db[h])

            @pl.when(first)
            def _():
                for d, (lg, rev) in enumerate(((lgf, False), (lgb, True))):
                    masks[2 * h + d] = decay_mask(lg, rev)
                    for j, w in enumerate(decay_rows(lg, rev)):
                        rows[2 * h + d, j] = jnp.broadcast_to(w, (RET_CHUNK, LANE))

            @pl.when(n == 0)
            def _():
                sf[h] = ctx_state(kc[:, sl], vc[:, sl], lgf, False)
                sb[h] = ctx_state(kc[:, sl], vc[:, sl], lgb, True)

            sf_out[h] = sf[h]
            sb_out[h] = sb[h]
            o, s = ret_chunk(qf[:, sl], kf[:, sl], vf[:, sl], sf[h], lgf, False,
                             (masks[2 * h], rows[2 * h, 0], rows[2 * h, 1]))
            of_ref[:, sl] = o
            sf[h] = s
            o, s = ret_chunk(qb[:, sl], kb[:, sl], vb[:, sl], sb[h], lgb, True,
                             (masks[2 * h + 1], rows[2 * h + 1, 0], rows[2 * h + 1, 1]))
            ob_ref[:, sl] = o
            sb[h] = s

    l = rq.shape[1]
    return pl.pallas_call(
        body, name="k2_fwd", grid=(b, N_CHUNK),
        in_specs=[_chunk_spec(False)] * 3 + [_chunk_spec(True)] * 3 + [_CTX_SPEC, _CTX_SPEC, _DEC_SPEC, _DEC_SPEC],
        out_specs=[_chunk_spec(False), _chunk_spec(True), _state_spec(False), _state_spec(True)],
        out_shape=[_sds((b, l, 512)), _sds((b, l, 512)), _sds((b, N_HEADS, N_CHUNK, LANE, LANE)),
                   _sds((b, N_HEADS, N_CHUNK, LANE, LANE))],
        scratch_shapes=[pltpu.VMEM((N_HEADS, LANE, LANE), F32), pltpu.VMEM((N_HEADS, LANE, LANE), F32),
                        pltpu.VMEM((2 * N_HEADS, RET_CHUNK, RET_CHUNK), F32),
                        pltpu.VMEM((2 * N_HEADS, 2, RET_CHUNK, LANE), F32)],
        compiler_params=_cp((ARB, ARB)),
    )(rq, rk, rv, rq, rk, rv, rkc, rvc, dec_f, dec_b)


def _k2_bwd(rq, rk, rv, do, sf_prev, sb_prev, rkc, rvc, dec_f, dec_b):
    b, l, _ = rq.shape

    def body(qf, kf, vf, gf, spf, qb, kb, vb, gb, spb, kc, vc, df, db,
             dqf, dkf, dvf, dqb, dkb, dvb, dkc, dvc, ddf, ddb, dsf, dsb, masks):
        n = pl.program_id(1)

        @pl.when(jnp.logical_and(pl.program_id(0) == 0, n == 0))
        def _():
            for h in range(N_HEADS):
                masks[2 * h] = decay_mask(log_sigmoid(df[h]), False)
                masks[2 * h + 1] = decay_mask(log_sigmoid(db[h]), True)

        @pl.when(n == 0)
        def _():
            dsf[...] = jnp.zeros((N_HEADS, LANE, LANE), F32)
            dsb[...] = jnp.zeros((N_HEADS, LANE, LANE), F32)

        def one(h, sl, q, k, v, g, sp, dec, ds, dq, dk, dv, dd, rev):
            mask = masks[2 * h + int(rev)]

            def f(qv, kv_, vv, sv, dcy):
                lg = log_sigmoid(dcy)
                return ret_chunk(qv, kv_, vv, sv, lg, rev, (kept_decay_mask(lg, mask, rev), *decay_rows(lg, rev)))

            _, vjp = jax.vjp(f, q[:, sl], k[:, sl], v[:, sl], sp[h], dec[h])
            gq, gk, gv, gs, gd = vjp((g[:, sl], ds[h]))
            dq[:, sl] = gq
            dk[:, sl] = gk
            dv[:, sl] = gv
            ds[h] = gs
            _acc(dd.at[h], jnp.broadcast_to(gd, (8, LANE)), n == 0)

        for h, sl in enumerate(_HEAD_SL):
            one(h, sl, qf, kf, vf, gf, spf, df, dsf, dqf, dkf, dvf, ddf, False)
            one(h, sl, qb, kb, vb, gb, spb, db, dsb, dqb, dkb, dvb, ddb, True)

        @pl.when(n == N_CHUNK - 1)
        def _():
            def f(kcv, vcv, dcy, rev):
                return ctx_state(kcv, vcv, log_sigmoid(dcy), rev)

            for h, sl in enumerate(_HEAD_SL):
                _, vjp_f = jax.vjp(functools.partial(f, rev=False), kc[:, sl], vc[:, sl], df[h])
                gk_f, gv_f, gd_f = vjp_f(dsf[h])
                _, vjp_b = jax.vjp(functools.partial(f, rev=True), kc[:, sl], vc[:, sl], db[h])
                gk_b, gv_b, gd_b = vjp_b(dsb[h])
                dkc[:, sl] = gk_f + gk_b
                dvc[:, sl] = gv_f + gv_b
                ddf[h] += jnp.broadcast_to(gd_f, (8, LANE))
                ddb[h] += jnp.broadcast_to(gd_b, (8, LANE))

    dd_spec = pl.BlockSpec((None, N_HEADS, 8, LANE), lambda i, n: (i, 0, 0, 0))
    return pl.pallas_call(
        body, name="k2_bwd", grid=(b, N_CHUNK),
        in_specs=[_chunk_spec(True)] * 4 + [_state_spec(True)] + [_chunk_spec(False)] * 4 + [_state_spec(False)]
        + [_CTX_SPEC, _CTX_SPEC, _DEC_SPEC, _DEC_SPEC],
        out_specs=[_chunk_spec(True)] * 3 + [_chunk_spec(False)] * 3 + [_CTX_SPEC, _CTX_SPEC, dd_spec, dd_spec],
        out_shape=[_sds((b, l, 512))] * 6 + [_sds((b, CTX_LEN, 512))] * 2 + [_sds((b, N_HEADS, 8, LANE))] * 2,
        scratch_shapes=[pltpu.VMEM((N_HEADS, LANE, LANE), F32), pltpu.VMEM((N_HEADS, LANE, LANE), F32),
                        pltpu.VMEM((2 * N_HEADS, RET_CHUNK, RET_CHUNK), F32)],
        compiler_params=_cp((ARB, ARB)),
    )(rq, rk, rv, do, sf_prev, rq, rk, rv, do, sb_prev, rkc, rvc, dec_f, dec_b)


TQ = 1024
TQ_F = 512
QK_W = 2 * LANE
N_QP = 2
_Q_PARTS = [slice(i * TQ_F // N_QP, (i + 1) * TQ_F // N_QP) for i in range(N_QP)]


SM_SCALE = 1.0 / math.sqrt(192.0)


def _k3_specs(tq):
    qs = lambda w: pl.BlockSpec((None, tq, w), lambda i, h, t: (i, t, h))
    ks = lambda w: pl.BlockSpec((None, KV_LEN, w), lambda i, h, t: (i, 0, h))
    return qs, ks


def _k3_fwd(q, k, v):
    b, l, _ = q.shape

    def body(q_ref, k_ref, v_ref, o_ref, lse_ref):
        kv_, vv = k_ref[...], v_ref[...]
        for r in _Q_PARTS:
            s = _dot(q_ref[r, :], kv_, 1, 1)
            m = jnp.max(s, axis=-1, keepdims=True)
            e = jnp.exp2((s - m) * (SM_SCALE * LOG2_E))
            tot = jnp.sum(e, axis=-1, keepdims=True)
            o_ref[r, :] = _dot(e, vv, 1, 0) * (1.0 / tot)
            lse_ref[r, :] = jnp.broadcast_to(m * SM_SCALE + jnp.log(tot), (TQ_F // N_QP, LANE))

    qs, ks = _k3_specs(TQ_F)
    return pl.pallas_call(
        body, name="k3_fwd", grid=(b, N_HEADS, l // TQ_F), in_specs=[qs(QK_W), ks(QK_W), ks(LANE)],
        out_specs=[qs(LANE), qs(LANE)], out_shape=[_sds((b, l, N_HEADS * LANE))] * 2,
        compiler_params=_cp((ARB, ARB, ARB)),
    )(q, k, v)


def _k3_bwd(q, k, v, o, lse, dy, after):
    b, l, _ = q.shape

    def body(q_ref, k_ref, v_ref, o_ref, lse_ref, dy_ref, after_ref, dq_ref, dk_ref, dv_ref):
        t0 = pl.program_id(2) == 0
        kv_, vv = k_ref[...], v_ref[...]
        qv, dyv = q_ref[...], dy_ref[...]
        g = dyv.astype(BF)
        lse_col = jnp.max(lse_ref[...], axis=-1, keepdims=True)
        delta = jnp.sum(dyv * o_ref[...], axis=-1, keepdims=True)
        p = jnp.exp2(_dot(qv, kv_, 1, 1) * (SM_SCALE * LOG2_E) - lse_col * LOG2_E)
        ds = (p * (_dot(g, vv, 1, 1) - delta) * SM_SCALE).astype(BF)
        _acc(dv_ref, _dot(p, g, 0, 0), t0)
        dq_ref[...] = _dot(ds, kv_, 1, 0)
        _acc(dk_ref, _dot(ds, qv, 0, 0), t0)

    qs, ks = _k3_specs(TQ)
    return pl.pallas_call(
        body, name="k3_bwd", grid=(b, N_HEADS, l // TQ),
        in_specs=[qs(QK_W), ks(QK_W), ks(LANE), qs(LANE), qs(LANE), qs(LANE), ANY],
        out_specs=[qs(QK_W), ks(QK_W), ks(LANE)],
        out_shape=[_sds((b, l, N_HEADS * QK_W)), _sds((b, KV_LEN, N_HEADS * QK_W)), _sds((b, KV_LEN, N_HEADS * LANE))],
        compiler_params=_cp((ARB, ARB, ARB)),
    )(q, k, v, o, lse, dy, after)


def _mod_rows(mod_ref, rows):
    return [mod_ref[r:r + 1, :] for r in rows]


def _k4a_fwd(x, o_f, o_b, rg, y_mla, g_ret, w_out, mod, g_ffn):
    b, l, _ = x.shape

    def body(x_ref, of_ref, ob_ref, rg_ref, ym_ref, gr_ref, wo_ref, mod_ref, gf_ref, xm_ref, h2_ref):
        gt_a, sh_f, sc_f = _mod_rows(mod_ref, (2, 3, 4))
        x_mid, h2 = k4a_tile(x_ref[...], of_ref[...], ob_ref[...], rg_ref[...], ym_ref[...], gr_ref[...], gt_a,
                             gf_ref[...], sh_f, sc_f, wo_ref[...], None)
        xm_ref[...] = x_mid
        h2_ref[...] = h2.astype(BF)

    tok = lambda w: pl.BlockSpec((None, TOK, w), lambda i, t: (i, t, 0))
    mod_spec = pl.BlockSpec((None, 8, D_MODEL), lambda i, t: (i, 0, 0))
    return pl.pallas_call(
        body, name="k4a_fwd", grid=(b, l // TOK),
        in_specs=[tok(D_MODEL), tok(512), tok(512), tok(512), tok(512), _full((1, 512)), _full((D_MODEL, D_MODEL)),
                  mod_spec, _full((1, D_MODEL))],
        out_specs=[tok(D_MODEL), tok(D_MODEL)], out_shape=[_sds((b, l, D_MODEL)), _sds((b, l, D_MODEL), BF)],
        compiler_params=_cp((ARB, ARB)),
    )(x, o_f, o_b, rg, y_mla, g_ret, w_out, mod, g_ffn)


TOK_M = 512
TOK_D = 2048
HALF_FF = D_FF // 2


def _k4b_mlp_loss(h2, w1t, w2, x_mid, mod, g_final, tgt):
    b, l, _ = h2.shape
    nt = l // TOK_M

    def body(h2_ref, w1_hbm, w2_hbm, xm_ref, mod_ref, gfin_ref, tgt_ref, dxm_ref, dmlp_ref, r_ref, loss_ref, dgt_ref,
             dgfin_ref, w1_v, w2_v):
        i, t = pl.program_id(0), pl.program_id(1)
        first = jnp.logical_and(i == 0, t == 0)

        @pl.when(first)
        def _():
            pltpu.sync_copy(w1_hbm, w1_v)
            pltpu.sync_copy(w2_hbm, w2_v)

        h2v = h2_ref[...]
        mlp = None
        for half in range(2):
            rows = slice(half * HALF_FF, (half + 1) * HALF_FF)
            r = jnp.maximum(_dot(h2v, w1_v[rows, :], 1, 1), 0.0)
            r_ref[:, rows] = r.astype(BF)
            part = _dot(jnp.square(r), w2_v[rows, :], 1, 0)
            mlp = part if mlp is None else mlp + part
        (gt_f,) = _mod_rows(mod_ref, (5,))
        loss, vjp = jax.vjp(k4c_tile, xm_ref[...], mlp, gt_f, gfin_ref[...], tgt_ref[...])
        dxm, dmlp, dgt, dgfin, _ = vjp(jnp.ones((1, 1), F32))
        dxm_ref[...] = dxm
        dmlp_ref[...] = dmlp.astype(BF)
        _acc(loss_ref, jnp.broadcast_to(loss, (8, LANE)), first)
        _acc(dgfin_ref, dgfin, first)
        _acc(dgt_ref, dgt, t == 0)

    tok = lambda w: pl.BlockSpec((None, TOK_M, w), lambda i, t: (i, t, 0))
    return pl.pallas_call(
        body, name="k4b_mlp_loss", grid=(b, nt),
        in_specs=[tok(D_MODEL), ANY, ANY, tok(D_MODEL), pl.BlockSpec((None, 8, D_MODEL), lambda i, t: (i, 0, 0)),
                  _full((1, D_MODEL)), tok(D_MODEL)],
        out_specs=[tok(D_MODEL), tok(D_MODEL), tok(D_FF), _full((8, LANE)),
                   pl.BlockSpec((None, 1, D_MODEL), lambda i, t: (i, 0, 0)), _full((1, D_MODEL))],
        out_shape=[_sds((b, l, D_MODEL)), _sds((b, l, D_MODEL), BF), _sds((b, l, D_FF), BF), _sds((8, LANE)),
                   _sds((b, 1, D_MODEL)), _sds((1, D_MODEL))],
        scratch_shapes=[pltpu.VMEM((D_FF, D_MODEL), BF), pltpu.VMEM((D_FF, D_MODEL), BF)],
        compiler_params=_cp((ARB, ARB)),
    )(h2, w1t, w2, x_mid, mod, g_final, tgt)


def _k4d_mlp_bwd(h2, dmlp, r, w2):
    b, l, _ = h2.shape
    nt = l // TOK_D

    def body(h2_ref, dm_ref, r_ref, w2_ref, da_ref, dw1_ref, dw2_ref, acc1, acc2):
        i, t = pl.program_id(1), pl.program_id(2)
        first = jnp.logical_and(i == 0, t == 0)
        rv = r_ref[...].astype(F32)
        dm = dm_ref[...]
        da = (_dot(dm, w2_ref[...], 1, 1) * (2.0 * rv)).astype(BF)
        da_ref[...] = da
        _acc(acc2, _dot(jnp.square(rv), dm, 0, 0), first)
        _acc(acc1, _dot(h2_ref[...], da, 0, 0), first)

        @pl.when(jnp.logical_and(i == b - 1, t == nt - 1))
        def _():
            dw1_ref[...] = acc1[...].astype(BF)
            dw2_ref[...] = acc2[...].astype(BF)

    tok = lambda w: pl.BlockSpec((None, TOK_D, w), lambda j, i, t: (i, t, 0))
    col = pl.BlockSpec((None, TOK_D, FF_BLK), lambda j, i, t: (i, t, j))
    return pl.pallas_call(
        body, name="k4d_mlp_bwd", grid=(N_DEV, b, nt),
        in_specs=[tok(D_MODEL), tok(D_MODEL), col, pl.BlockSpec((None, FF_BLK, D_MODEL), lambda j, i, t: (j, 0, 0))],
        out_specs=[col, pl.BlockSpec((None, D_MODEL, FF_BLK), lambda j, i, t: (j, 0, 0)),
                   pl.BlockSpec((None, FF_BLK, D_MODEL), lambda j, i, t: (j, 0, 0))],
        out_shape=[_sds((b, l, D_FF), BF), _sds((N_DEV, D_MODEL, FF_BLK), BF), _sds((N_DEV, FF_BLK, D_MODEL), BF)],
        scratch_shapes=[pltpu.VMEM((D_MODEL, FF_BLK), F32), pltpu.VMEM((FF_BLK, D_MODEL), F32)],
        compiler_params=_cp((ARB, ARB, ARB)),
    )(h2, dmlp, r, w2)


def _k4e_bwd(x, o_f, o_b, rg, y_mla, g_ret, w_out, mod, g_ffn, dxm, da, w1t):
    b, l, _ = x.shape

    def body(x_ref, of_ref, ob_ref, rg_ref, ym_ref, gr_ref, wo_ref, mod_ref, gf_ref, dxm_ref, da_ref, w1_hbm,
             dx_ref, do_ref, drg_ref, dym_ref, dwo_ref, dgr_ref, dgf_ref, dmod_ref, w1_v, dwo_acc):
        i, t = pl.program_id(0), pl.program_id(1)
        first = jnp.logical_and(i == 0, t == 0)

        @pl.when(first)
        def _():
            pltpu.sync_copy(w1_hbm, w1_v)

        gt_a, sh_f, sc_f = _mod_rows(mod_ref, (2, 3, 4))
        wo = wo_ref[...]
        dh2 = _dot(da_ref[...], w1_v[...], 1, 0)

        def f(xv, ofv, rgv, ymv, grv, gta, gfv, shf, scf, p_out):
            return k4a_tile(xv, ofv, ob_ref[...], rgv, ymv, grv, gta, gfv, shf, scf, wo, p_out)

        _, vjp = jax.vjp(f, x_ref[...], of_ref[...], rg_ref[...], ym_ref[...], gr_ref[...], gt_a, gf_ref[...], sh_f,
                         sc_f, jnp.zeros((D_MODEL, D_MODEL), F32))
        dx, do, drg, dym, dgr, dgta, dgf, dshf, dscf, dwo = vjp((dxm_ref[...], dh2))
        dx_ref[...] = dx
        do_ref[...] = do
        drg_ref[...] = drg
        dym_ref[...] = dym
        _acc(dwo_acc, dwo, first)
        _acc(dgr_ref, dgr, first)
        _acc(dgf_ref, dgf, first)
        t0 = t == 0
        _acc(dmod_ref.at[2:3, :], dgta, t0)
        _acc(dmod_ref.at[3:4, :], dshf, t0)
        _acc(dmod_ref.at[4:5, :], dscf, t0)

        @pl.when(t0)
        def _():
            dmod_ref[0:2, :] = jnp.zeros((2, D_MODEL), F32)
            dmod_ref[5:8, :] = jnp.zeros((3, D_MODEL), F32)

        @pl.when(jnp.logical_and(i == b - 1, t == l // TOK_B - 1))
        def _():
            dwo_ref[...] = dwo_acc[...].astype(BF)

    tok = lambda w: pl.BlockSpec((None, TOK_B, w), lambda i, t: (i, t, 0))
    mod_spec = pl.BlockSpec((None, 8, D_MODEL), lambda i, t: (i, 0, 0))
    return pl.pallas_call(
        body, name="k4e_bwd", grid=(b, l // TOK_B),
        in_specs=[tok(D_MODEL), tok(512), tok(512), tok(512), tok(512), _full((1, 512)), _full((D_MODEL, D_MODEL)),
                  mod_spec, _full((1, D_MODEL)), tok(D_MODEL), tok(D_FF), ANY],
        out_specs=[tok(D_MODEL), tok(512), tok(512), tok(512), _full((D_MODEL, D_MODEL)), _full((1, 512)),
                   _full((1, D_MODEL)), mod_spec],
        out_shape=[_sds((b, l, D_MODEL)), _sds((b, l, 512)), _sds((b, l, 512)), _sds((b, l, 512)),
                   _sds((D_MODEL, D_MODEL), BF), _sds((1, 512)), _sds((1, D_MODEL)), _sds((b, 8, D_MODEL))],
        scratch_shapes=[pltpu.VMEM((D_FF, D_MODEL), BF), pltpu.VMEM((D_MODEL, D_MODEL), F32)],
        compiler_params=_cp((ARB, ARB)),
    )(x, o_f, o_b, rg, y_mla, g_ret, w_out, mod, g_ffn, dxm, da, w1t)


ADAM_BLOCK_BYTES = 32 * 1024 * 1024


def _adamw(w, m, v, pieces, name, after=None):
    r, c = w.shape
    npc = pieces.shape[0]
    per_row = c * (7 * 4 + npc * pieces.dtype.itemsize) * 2
    rb = max(d for d in range(8, r + 1, 8) if r % d == 0 and d * per_row <= ADAM_BLOCK_BYTES)

    def body(w_ref, m_ref, v_ref, p_ref, *rest):
        g_ref, d_ref, nm_ref, nv_ref = rest[-4:]
        g = p_ref[0].astype(F32)
        for k in range(1, npc):
            g = g + p_ref[k].astype(F32)
        wv = w_ref[...]
        mn = ADAM_B1 * m_ref[...] + (1.0 - ADAM_B1) * g
        vn = ADAM_B2 * v_ref[...] + (1.0 - ADAM_B2) * jnp.square(g)
        m_hat = mn / (1.0 - ADAM_B1 ** ADAM_STEP)
        v_hat = vn / (1.0 - ADAM_B2 ** ADAM_STEP)
        g_ref[...] = g
        d_ref[...] = -ADAM_LR * (m_hat / (jnp.sqrt(v_hat) + ADAM_EPS) + ADAM_WD * wv)
        nm_ref[...] = mn
        nv_ref[...] = vn

    blk = pl.BlockSpec((rb, c), lambda i: (i, 0))
    extra = [] if after is None else [after]
    return pl.pallas_call(
        body, name=name, grid=(r // rb,),
        in_specs=[blk, blk, blk, pl.BlockSpec((npc, rb, c), lambda i: (0, i, 0))] + [ANY] * len(extra),
        out_specs=[blk] * 4, out_shape=[_sds((r, c))] * 4, compiler_params=_cp((ARB,)),
    )(w, m, v, pieces, *extra)


def _pad_rot_rows(w, zero):
    k = w.shape[1]
    return lax.pad(w.reshape(-1, 2, 32, k), zero, ((0, 0, 0), (0, 0, 0), (0, 32, 0), (0, 0, 0))).reshape(-1, k)


def _cut_rot_rows(g):
    k = g.shape[1]
    return g.reshape(-1, 2, 64, k)[:, :, :32].reshape(-1, k)


def _w_in_pad(wt, zero):
    w_a = jnp.concatenate([_pad_rot_rows(wt[0:512], zero), wt[512:1536]], axis=0)
    w_b = jnp.concatenate([wt[1536:2176], _pad_rot_rows(wt[2176:2240], zero)], axis=0)
    return w_a, w_b


def _w_in_cut(g_a, g_b):
    return jnp.concatenate([_cut_rot_rows(g_a[0:1024]), g_a[1024:2048], g_b[0:640], _cut_rot_rows(g_b[640:768])], axis=0)


def _w_uq_pad(wt, zero):
    w = wt.reshape(N_HEADS, 192, 384)
    rot = _pad_rot_rows(w[:, 128:].reshape(N_HEADS * 64, 384), zero).reshape(N_HEADS, LANE, 384)
    return jnp.concatenate([w[:, :128], rot], axis=1).reshape(1024, 384)


def _w_uq_cut(g):
    g = g.reshape(N_HEADS, 256, 384)
    rot = _cut_rot_rows(g[:, 128:].reshape(N_HEADS * LANE, 384)).reshape(N_HEADS, 64, 384)
    return jnp.concatenate([g[:, :128], rot], axis=1).reshape(768, 384)


def _w_ukv_perm(wt):
    return jnp.transpose(wt.reshape(N_HEADS, 2, LANE, 256), (1, 0, 2, 3)).reshape(1024, 256)


def _w_ukv_unperm(g):
    return jnp.transpose(g.reshape(2, N_HEADS, LANE, 256), (1, 0, 2, 3)).reshape(1024, 256)


def _unshard_cols(g):
    return jnp.transpose(g, (1, 0, 2)).reshape(g.shape[1], N_DEV * g.shape[2])


def _rope_tables():
    rows = SEQ // GRID_W
    row = jnp.repeat(jnp.arange(rows, dtype=F32), GRID_W)
    col = jnp.tile(jnp.arange(GRID_W, dtype=F32), rows)
    freq = ROPE_BASE ** (-jnp.arange(16, dtype=F32) / 16)
    ang = jnp.concatenate([row[:, None] * freq, col[:, None] * freq], axis=-1)
    cos, sin = jnp.cos(ang), jnp.sin(ang)
    z = jnp.zeros((SEQ, 32), F32)
    return jnp.concatenate([cos, z, cos, z], axis=1), jnp.concatenate([-sin, z, sin, z], axis=1)


_PACKED = (("g_attn", 1024), ("g_ffn", 1024), ("ret_decay_fwd", 4), ("ret_decay_bwd", 4), ("g_ret", 512),
           ("g_q_lora", 384), ("g_kv_lora", 256), ("g_final", 1024))
_PACK_OFF = {}
_off = 0
for _name, _n in _PACKED:
    _PACK_OFF[_name] = _off
    _off += -(-_n // LANE) * LANE
PACK_W = _off


def _pack_small(vals):
    parts = []
    for name, n in _PACKED:
        a = vals[name].reshape(-1).astype(F32)
        parts.append(jnp.pad(a, (0, -(-n // LANE) * LANE - n)))
    return jnp.concatenate(parts).reshape(1, PACK_W)


def _adamw_small(params, packed, gcc, gb_ada):
    names = list(params)
    n_p = len(names)

    def body(*refs):
        p_ref, gcc_ref, gb_ref = refs[3 * n_p:3 * n_p + 3]
        outs = refs[3 * n_p + 3:]
        for k, name in enumerate(names):
            w_ref, m_ref, v_ref = refs[3 * k:3 * k + 3]
            n = w_ref.shape[1]
            if name == "b_ada":
                g = jnp.concatenate([gb_ref[d, 0:1, :] for d in range(N_DEV)], axis=-1)
            elif name == "c_ctx":
                g = gcc_ref[0, 0:1, :]
                for d in range(1, N_DEV):
                    g = g + gcc_ref[d, 0:1, :]
            else:
                off = _PACK_OFF[name]
                g = p_ref[0, :, off:off + n]
                for d in range(1, N_DEV):
                    g = g + p_ref[d, :, off:off + n]
            mn = ADAM_B1 * m_ref[...] + (1.0 - ADAM_B1) * g
            vn = ADAM_B2 * v_ref[...] + (1.0 - ADAM_B2) * jnp.square(g)
            m_hat = mn / (1.0 - ADAM_B1 ** ADAM_STEP)
            v_hat = vn / (1.0 - ADAM_B2 ** ADAM_STEP)
            outs[4 * k][...] = g
            outs[4 * k + 1][...] = -ADAM_LR * (m_hat / (jnp.sqrt(v_hat) + ADAM_EPS) + ADAM_WD * w_ref[...])
            outs[4 * k + 2][...] = mn
            outs[4 * k + 3][...] = vn

    args = [a for name in names for a in params[name]] + [packed, gcc, gb_ada]
    out_shape = [_sds(params[name][0].shape) for name in names for _ in range(4)]
    outs = pl.pallas_call(body, name="adamw_small", out_shape=out_shape, compiler_params=_cp())(*args)
    return {name: list(outs[4 * k:4 * k + 4]) for k, name in enumerate(names)}


def kernel(x, c, ctx, c_ctx, w_ada, b_ada, g_attn, g_ffn, w_in, ret_decay_fwd, ret_decay_bwd, g_ret, g_q_lora, w_uq, g_kv_lora, w_ukv, w_out, w_ff1, w_ff2, g_final, loss_target, m_c_ctx, m_w_ada, m_b_ada, m_g_attn, m_g_ffn, m_w_in, m_ret_decay_fwd, m_ret_decay_bwd, m_g_ret, m_g_q_lora, m_w_uq, m_g_kv_lora, m_w_ukv, m_w_out, m_w_ff1, m_w_ff2, m_g_final, v_c_ctx, v_w_ada, v_b_ada, v_g_attn, v_g_ffn, v_w_in, v_ret_decay_fwd, v_ret_decay_bwd, v_g_ret, v_g_q_lora, v_w_uq, v_g_kv_lora, v_w_ukv, v_w_out, v_w_ff1, v_w_ff2, v_g_final):
    me = 4 * lax.axis_index("x") + 2 * lax.axis_index("y") + lax.axis_index("c")
    nb = x.shape[0]

    c_pad = jnp.pad(c, ((0, 8 - nb), (0, 0)))
    c_all, g_in, g_uq, g_ukv = _gather_two_level(
        [c_pad, w_in[0].T.astype(BF), w_uq[0].T.astype(BF), w_ukv[0].T.astype(BF)], "gather_weights")

    crows = jnp.concatenate([c_all[:, :nb].reshape(N_DEV * nb, D_MODEL), c_ctx[None], jnp.zeros((7, D_MODEL), F32)])
    b_blk = lax.dynamic_slice(b_ada, (0, me * 768), (1, 768))
    st_f = _exchange_start([_mod_fwd(crows, w_ada[0], b_blk), w_out[0].astype(BF), w_ff1[0].T.astype(BF),
                            w_ff2[0].astype(BF)], True, "gather_fwd_start")
    zero = st_f["token"][0, 0].astype(BF)
    ws = (*_w_in_pad(g_in.reshape(2240, D_MODEL), zero), _w_uq_pad(g_uq.reshape(768, 384), zero),
          _w_ukv_perm(g_ukv.reshape(1024, 256)))
    (mod_g,) = _exchange_wait(st_f, ws, "gather_mod_wait", [0])
    mod_all = _unshard_cols(mod_g)
    mod_mine = lax.dynamic_slice(mod_all, (me * nb, 0), (nb, 6 * D_MODEL)).reshape(nb, 6, D_MODEL)
    mod = jnp.pad(mod_mine, ((0, 0), (0, 2), (0, 0)))
    mod_c = jnp.pad(mod_all[16].reshape(1, 6, D_MODEL), ((0, 0), (0, 2), (0, 0)))

    tabs = _rope_tables()
    dec_f = ret_decay_fwd.reshape(N_HEADS, 1, 1)
    dec_b = ret_decay_bwd.reshape(N_HEADS, 1, 1)

    rkc, rvc, k_ctx, v_ctx = _k1_fwd(ctx, mod_c, g_attn, g_q_lora, g_kv_lora, ws, tabs, None, True)
    rq, rk, rv, rg, q, k_all, v_all = _k1_fwd(x, mod, g_attn, g_q_lora, g_kv_lora, ws, tabs, (k_ctx, v_ctx), False)
    o_f, o_b, sf_prev, sb_prev = _k2_fwd(rq, rk, rv, rkc, rvc, dec_f, dec_b)
    y_mla, lse = _k3_fwd(q, k_all, v_all)
    (g_out,) = _exchange_wait(st_f, y_mla, "gather_wo_wait", [1])
    wo = g_out.reshape(D_MODEL, D_MODEL)
    x_mid, h2 = _k4a_fwd(x, o_f, o_b, rg, y_mla, g_ret, wo, mod, g_ffn)
    g_ff1t, g_ff2 = _exchange_wait(st_f, x_mid, "gather_ff_wait", [2, 3])
    w1t = g_ff1t.reshape(D_FF, D_MODEL)
    dxm, dmlp, relu_a, loss_acc, dgt_f, dg_final = _k4b_mlp_loss(h2, w1t, g_ff2.reshape(D_FF, D_MODEL), x_mid, mod,
                                                                 g_final.reshape(1, D_MODEL), loss_target)

    da, dw1, dw2 = _k4d_mlp_bwd(h2, dmlp, relu_a, g_ff2)
    dx_res, do, drg, dym, dwo, dg_ret, dg_ffn, dmod_a = _k4e_bwd(x, o_f, o_b, rg, y_mla, g_ret, wo, mod, g_ffn, dxm, da,
                                                                 w1t)
    st_s = _exchange_start([dw1, dw2, dwo.reshape(N_DEV, 128, D_MODEL)], False, "scatter_grads_start")
    dq, dk_all, dv_all = _k3_bwd(q, k_all, v_all, y_mla, lse, dym, st_s["token"])
    dqf, dkf, dvf, dqb, dkb, dvb, dkc, dvc, ddf, ddb = _k2_bwd(rq, rk, rv, do, sf_prev, sb_prev, rkc, rvc, dec_f, dec_b)
    cts = [[(dqf, 0), (dqb, 0)], [(dkf, 0), (dkb, 0)], [(dvf, 0), (dvb, 0)], [(drg, 0)], [(dq, 0)],
           [(dk_all, 0)], [(dv_all, 0)]]
    cts_c = [[(dkc, 0)], [(dvc, 0)], [(dk_all, SEQ)], [(dv_all, SEQ)]]
    grad_x, accs, dmod_1, dmod_c1 = _k1_bwd(x, ctx, mod, mod_c, g_attn, g_q_lora, g_kv_lora, ws, tabs, cts, cts_c,
                                            dx_res)
    dwa, dwb, dwq, dwk, dg_attn, dg_q, dg_kv = accs

    dmod_loc = (dmod_a + dmod_1).at[:, 5, :].set(dgt_f[:, 0, :])[:, :6, :].reshape(nb, 6 * D_MODEL)
    dmod_ctx = dmod_c1[:, :6, :].reshape(1, 6 * D_MODEL)
    small = {"g_attn": dg_attn, "g_ffn": dg_ffn, "ret_decay_fwd": jnp.sum(ddf[:, :, 0, 0], axis=0),
             "ret_decay_bwd": jnp.sum(ddb[:, :, 0, 0], axis=0), "g_ret": dg_ret, "g_q_lora": dg_q, "g_kv_lora": dg_kv,
             "g_final": dg_final}
    extra = jnp.concatenate([dmod_loc, dmod_ctx, jnp.zeros((5, 6 * D_MODEL), F32)])
    ex_pieces = jnp.transpose(extra.reshape(8, N_DEV, 768), (1, 0, 2))
    st_sm = _exchange_start([_pack_small(small), ex_pieces, loss_acc], [True, False, True], "gather_small_start")
    chip_sums = _pair_reduce([_w_in_cut(dwa, dwb).reshape(N_DEV, 280, D_MODEL), _w_uq_cut(dwq).reshape(N_DEV, 96, 384),
                              _w_ukv_unperm(dwk).reshape(N_DEV, 128, 256)], "pair_reduce", st_sm["token"])
    sm_g, ex_g, loss_g = _exchange_wait(st_sm, chip_sums[0], "gather_small_wait")
    dmod_blk = jnp.concatenate([ex_g[:, :nb].reshape(N_DEV * nb, 768), jnp.zeros((8, 768), F32)])
    gw_ada, gcc_part, gb_part = _mod_bwd(crows, w_ada[0], dmod_blk, ex_g[:, nb])
    st_c = _exchange_start([gcc_part, gb_part], True, "gather_cc_start")
    p_ff1, p_ff2, p_wo = _exchange_wait(st_s, st_c["token"], "scatter_grads_wait")
    st_r = _exchange_start(chip_sums, False, "scatter_rest_start", after=p_wo, chips=True)

    res = {}
    early = (("w_ff1", w_ff1, m_w_ff1, v_w_ff1, p_ff1), ("w_ff2", w_ff2, m_w_ff2, v_w_ff2, p_ff2),
             ("w_ada", w_ada, m_w_ada, v_w_ada, gw_ada[None]), ("w_out", w_out, m_w_out, v_w_out, p_wo))
    behind = st_r["token"]
    for name, w, m, v, pcs in early:
        res[name] = [a[None] for a in _adamw(w[0], m[0], v[0], pcs, "adamw_" + name, after=behind)]
        behind = res[name][3]

    smalls = {"c_ctx": (c_ctx, m_c_ctx, v_c_ctx), "b_ada": (b_ada, m_b_ada, v_b_ada), "g_attn": (g_attn, m_g_attn, v_g_attn),
              "g_ffn": (g_ffn, m_g_ffn, v_g_ffn), "ret_decay_fwd": (ret_decay_fwd, m_ret_decay_fwd, v_ret_decay_fwd),
              "ret_decay_bwd": (ret_decay_bwd, m_ret_decay_bwd, v_ret_decay_bwd), "g_ret": (g_ret, m_g_ret, v_g_ret),
              "g_q_lora": (g_q_lora, m_g_q_lora, v_g_q_lora), "g_kv_lora": (g_kv_lora, m_g_kv_lora, v_g_kv_lora),
              "g_final": (g_final, m_g_final, v_g_final)}
    rows = {k: tuple(a.reshape(1, -1) for a in t) for k, t in smalls.items()}
    gcc_g, gb_g = _exchange_wait(st_c, behind, "gather_cc_wait")
    small_out = _adamw_small(rows, sm_g, gcc_g, gb_g)
    for name, outs in small_out.items():
        res[name] = [o.reshape(smalls[name][0].shape) for o in outs]

    pieces = _exchange_wait(st_r, small_out["g_final"][3], "scatter_rest_wait")
    for name, w, m, v, pcs in (("w_in", w_in, m_w_in, v_w_in, pieces[0]), ("w_uq", w_uq, m_w_uq, v_w_uq, pieces[1])):
        res[name] = [a.T[None] for a in _adamw(w[0].T, m[0].T, v[0].T, pcs, "adamw_" + name)]
    res["w_ukv"] = [a[None] for a in _adamw(w_ukv[0], m_w_ukv[0], v_w_ukv[0], jnp.transpose(pieces[2], (0, 2, 1)),
                                            "adamw_w_ukv")]

    loss = loss_g[0, 0, 0]
    for k in range(1, N_DEV):
        loss = loss + loss_g[k, 0, 0]

    order = ("c_ctx", "w_ada", "b_ada", "g_attn", "g_ffn", "w_in", "ret_decay_fwd", "ret_decay_bwd", "g_ret", "g_q_lora",
             "w_uq", "g_kv_lora", "w_ukv", "w_out", "w_ff1", "w_ff2", "g_final")
    return (loss, grad_x, *[res[n][0] for n in order], *[res[n][1] for n in order], *[res[n][2] for n in order],
            *[res[n][3] for n in order])
```

```python
import functools
import math

import jax
import jax.numpy as jnp
from jax import lax
from jax.experimental import pallas as pl
from jax.experimental.pallas import tpu as pltpu

F32 = jnp.float32
BF = jnp.bfloat16
EPS = 1e-6
LANE = 128
LOG2_E = 1.0 / math.log(2.0)
N_DEV = 8
D_MODEL = 1024
SEQ = 2048
CTX_LEN = 256
GRID_W = 64
N_HEADS = 4
RET_CHUNK = 512
N_CHUNK = SEQ // RET_CHUNK
D_FF = 4096
FF_BLK = D_FF // N_DEV
IN_PAD = 2816
KV_LEN = CTX_LEN + SEQ
ROPE_BASE = 10000.0
ADAM_LR, ADAM_B1, ADAM_B2, ADAM_EPS, ADAM_WD, ADAM_STEP = 0.001, 0.9, 0.999, 1e-08, 0.01, 10
TOK = 512
TOK_B = 256
VMEM_LIMIT = 56 * 1024 * 1024
ARB = "arbitrary"
MESH = pl.DeviceIdType.MESH
_HEAD_SL = [slice(LANE * h, LANE * (h + 1)) for h in range(N_HEADS)]
W_SHAPES = [(2048, D_MODEL), (768, D_MODEL), (1024, 384), (1024, 256)]


def _dot(a, b, ca, cb):
    return lax.dot_general(a.astype(BF), b.astype(BF), (((ca,), (cb,)), ((), ())), preferred_element_type=F32)


@jax.custom_vjp
def mm(a, b):
    return _dot(a, b, 1, 0)


@jax.custom_vjp
def mm_nt(a, b):
    return _dot(a, b, 1, 1)


@jax.custom_vjp
def mm_tn(a, b):
    return _dot(a, b, 0, 0)


mm.defvjp(lambda a, b: (_dot(a, b, 1, 0), (a, b)), lambda r, g: (mm_nt(g, r[1]), mm_tn(r[0], g)))
mm_nt.defvjp(lambda a, b: (_dot(a, b, 1, 1), (a, b)), lambda r, g: (mm(g, r[1]), mm_tn(g, r[0])))
mm_tn.defvjp(lambda a, b: (_dot(a, b, 0, 0), (a, b)), lambda r, g: (mm_nt(r[1], g), mm(r[0], g)))


@jax.custom_vjp
def _mmw(a, w, probe):
    return _dot(a, w, 1, 0)


def _mmw_bwd(r, g):
    a, w = r
    return mm_nt(g, w), jnp.zeros_like(w), mm_tn(a, g)


_mmw.defvjp(lambda a, w, probe: (_dot(a, w, 1, 0), (a, w)), _mmw_bwd)


@jax.custom_vjp
def _mmwt(a, wt, probe):
    return _dot(a, wt, 1, 1)


_mmwt.defvjp(lambda a, wt, probe: (_dot(a, wt, 1, 1), (a, wt)),
             lambda r, g: (mm(g, r[1]), jnp.zeros_like(r[1]), mm_tn(g, r[0])))


def mmwt(a, wt, probe):
    return _dot(a, wt, 1, 1) if probe is None else _mmwt(a, wt, probe)


def mmw(a, w, probe):
    return _dot(a, w, 1, 0) if probe is None else _mmw(a, w, probe)


def rmsn(x, g):
    return x * lax.rsqrt(jnp.mean(x * x, axis=-1, keepdims=True) + EPS) * g


def silu(x):
    return x * jax.nn.sigmoid(x)


def _swap_halves_impl(x):
    return pltpu.roll(x, 64, 1)


@jax.custom_vjp
def swap_halves(x):
    return _swap_halves_impl(x)


swap_halves.defvjp(lambda x: (_swap_halves_impl(x), None), lambda _, g: (_swap_halves_impl(g),))


def rope(x, cs1, sn1, every=1):
    blocks = []
    for i in range(x.shape[-1] // LANE):
        xb = x[:, LANE * i:LANE * (i + 1)]
        blocks.append(xb * cs1 + swap_halves(xb) * sn1 if i % every == every - 1 else xb)
    return blocks[0] if len(blocks) == 1 else jnp.concatenate(blocks, axis=-1)


def k1_tile(x, sh, sc, g_attn, g_q, g_kv, ws, ps, tabs, is_ctx):
    w_a, w_b, w_uq, w_ukv = ws
    p_a, p_b, p_uq, p_ukv = ps
    cs1, sn1 = tabs
    h = rmsn(x, g_attn) * (1.0 + sc) + sh
    pa = mmwt(h, w_a, p_a)
    pb = mmwt(h, w_b, p_b)
    rk = pa[:, 512:1024] * 0.125
    rv = pa[:, 1024:1536]
    kpe = pb[:, 640:768]
    kv = mmwt(rmsn(pb[:, 384:640], g_kv), w_ukv, p_ukv)
    if not is_ctx:
        rk = rope(rk, cs1, sn1)
        kpe = rope(kpe, cs1, sn1)
    k_full = jnp.concatenate([piece for sl in _HEAD_SL for piece in (kv[:, sl], kpe)], axis=-1)
    v = kv[:, 512:]
    if is_ctx:
        return rk, rv, k_full, v
    rq = rope(pa[:, 0:512], cs1, sn1)
    rg = pa[:, 1536:2048]
    q = rope(mmwt(rmsn(pb[:, 0:384], g_q), w_uq, p_uq), cs1, sn1, every=2)
    return rq, rk, rv, rg, q, k_full, v


def log_sigmoid(x):
    return jnp.minimum(x, 0.0) - jnp.log(1.0 + jnp.exp(-jnp.abs(x)))


def _distance(reverse):
    c = RET_CHUNK
    ii = lax.broadcasted_iota(jnp.int32, (c, c), 0).astype(F32)
    jj = lax.broadcasted_iota(jnp.int32, (c, c), 1).astype(F32)
    return (jj - ii) if reverse else (ii - jj)


def decay_mask(lg, reverse):
    diff = _distance(reverse)
    return jnp.where(diff >= 0, jnp.exp2((lg * LOG2_E) * jnp.maximum(diff, 0.0)), 0.0)


@functools.partial(jax.custom_vjp, nondiff_argnums=(2,))
def kept_decay_mask(lg, mask, reverse):
    return mask


def _kept_decay_mask_fwd(lg, mask, reverse):
    return mask, mask


def _kept_decay_mask_bwd(reverse, mask, g):
    return jnp.sum(g * mask * _distance(reverse), keepdims=True).reshape(1, 1), jnp.zeros_like(mask)


kept_decay_mask.defvjp(_kept_decay_mask_fwd, _kept_decay_mask_bwd)


def _row_steps(reverse):
    c = RET_CHUNK
    pos = lax.broadcasted_iota(jnp.int32, (c, 1), 0).astype(F32)
    return (pos, c - pos) if reverse else (c - 1.0 - pos, pos + 1.0)


def decay_rows(lg, reverse):
    return tuple(jnp.exp(lg * steps) for steps in _row_steps(reverse))


@functools.partial(jax.custom_vjp, nondiff_argnums=(2, 3))
def kept_decay_row(lg, row, reverse, which):
    return row


def _kept_decay_row_fwd(lg, row, reverse, which):
    return row, row


def _kept_decay_row_bwd(reverse, which, row, g):
    return jnp.sum(g * row * _row_steps(reverse)[which], keepdims=True).reshape(1, 1), jnp.zeros_like(row)


kept_decay_row.defvjp(_kept_decay_row_fwd, _kept_decay_row_bwd)


def ret_chunk(q, k, v, s, lg, reverse, pre=None):
    c = RET_CHUNK
    dec, wk, wq = (decay_mask(lg, reverse), *decay_rows(lg, reverse)) if pre is None else pre
    o = mm(mm_nt(q, k) * dec, v) + mm(q * wq, s)
    s_next = jnp.exp(lg * float(c)) * s + mm_tn(k * wk, v)
    return o, s_next


def ctx_state(kc, vc, lg, reverse):
    n = kc.shape[0]
    pos = lax.broadcasted_iota(jnp.int32, (n, 1), 0).astype(F32)
    w = jnp.exp(lg * pos) if reverse else jnp.exp(lg * (n - 1.0 - pos))
    return mm_tn(kc * w, vc)


def attn_head(qn, qp, kn, kp, v):
    s = (mm_nt(qn, kn) + mm_nt(qp, kp)) * (1.0 / math.sqrt(192.0))
    e = jnp.exp(s - jnp.max(s, axis=-1, keepdims=True))
    return mm(e / jnp.sum(e, axis=-1, keepdims=True), v)


def gn_gate(o, rg, g_ret):
    ys = []
    for h in range(N_HEADS):
        sl = slice(LANE * h, LANE * (h + 1))
        oh = o[:, sl]
        mu = jnp.mean(oh, axis=-1, keepdims=True)
        var = jnp.mean(jnp.square(oh - mu), axis=-1, keepdims=True)
        ys.append((oh - mu) * lax.rsqrt(var + EPS) * g_ret[:, sl])
    return jnp.concatenate(ys, axis=-1) * silu(rg)


def k4a_tile(x, o_f, o_b, rg, y_mla, g_ret, gt_a, g_ffn, sh_f, sc_f, w_out, p_out):
    mix = jnp.concatenate([gn_gate(o_f + o_b, rg, g_ret), y_mla], axis=-1)
    x_mid = x + gt_a * mmw(mix, w_out, p_out)
    h2 = rmsn(x_mid, g_ffn) * (1.0 + sc_f) + sh_f
    return x_mid, h2


def k4c_tile(x_mid, mlp, gt_f, g_final, tgt):
    y = rmsn(x_mid + gt_f * mlp, g_final)
    per_tok = jnp.mean(jnp.square(y - tgt), axis=-1, keepdims=True)
    return 0.5 * jnp.sum(per_tok, axis=0, keepdims=True)


def _cp(sem=None, vmem=VMEM_LIMIT):
    return pltpu.CompilerParams(dimension_semantics=sem, vmem_limit_bytes=vmem)


def _acc(ref, val, first):
    @pl.when(first)
    def _():
        ref[...] = val

    @pl.when(jnp.logical_not(first))
    def _():
        ref[...] += val


def _full(shape):
    nd = len(shape)
    return pl.BlockSpec(shape, lambda *_: (0,) * nd)


ANY = pl.BlockSpec(memory_space=pl.ANY)


def _sds(shape, dtype=F32):
    return jax.ShapeDtypeStruct(shape, dtype)


def _exchange(arrs, gather, name):
    n = len(arrs)
    modes = [gather] * n if isinstance(gather, bool) else list(gather)
    out_shape = [_sds(((N_DEV,) + a.shape) if g else a.shape, a.dtype) for a, g in zip(arrs, modes)]

    def body(*refs):
        ins, outs = refs[:n], refs[n:2 * n]
        send_sems, recv_sems, local_sems = refs[2 * n:]
        x, y, c = lax.axis_index("x"), lax.axis_index("y"), lax.axis_index("c")
        me = 4 * x + 2 * y + c
        sends, recvs, locs = [], [], []
        for i in range(n):
            gather = modes[i]
            for k in range(N_DEV - 1):
                bits = k + 1
                px = x ^ ((bits >> 2) & 1)
                py = y ^ ((bits >> 1) & 1)
                pc = c ^ (bits & 1)
                peer = 4 * px + 2 * py + pc
                src = ins[i] if gather else ins[i].at[peer]
                sem = i * (N_DEV - 1) + k
                sends.append(pltpu.make_async_remote_copy(
                    src_ref=src, dst_ref=outs[i].at[me], send_sem=send_sems.at[sem], recv_sem=recv_sems.at[sem],
                    device_id=(px, py, pc), device_id_type=MESH))
                recvs.append(pltpu.make_async_remote_copy(
                    src_ref=src, dst_ref=outs[i].at[peer], send_sem=send_sems.at[sem], recv_sem=recv_sems.at[sem],
                    device_id=(px, py, pc), device_id_type=MESH))
            locs.append(pltpu.make_async_copy(ins[i] if gather else ins[i].at[me], outs[i].at[me], local_sems.at[i]))
        for cp in locs + sends:
            cp.start()
        for cp in recvs:
            cp.wait_recv()
        for cp in sends:
            cp.wait_send()
        for cp in locs:
            cp.wait()

    outs = pl.pallas_call(
        body, name=name, out_shape=out_shape, in_specs=[ANY] * n, out_specs=[ANY] * n,
        scratch_shapes=[pltpu.SemaphoreType.DMA((n * (N_DEV - 1),)), pltpu.SemaphoreType.DMA((n * (N_DEV - 1),)),
                        pltpu.SemaphoreType.DMA((n,))],
    )(*arrs)
    return list(outs)


def _gather_two_level(arrs, name):
    n = len(arrs)

    def body(*refs):
        ins, outs = refs[:n], refs[n:2 * n]
        send_sems, recv_sems, local_sems = refs[2 * n:]
        x, y, c = lax.axis_index("x"), lax.axis_index("y"), lax.axis_index("c")
        sibling = (x, y, 1 - c)
        chips = [(1 - x, y), (x, 1 - y), (1 - x, 1 - y)]

        def slot(px, py, pc):
            return 4 * px + 2 * py + pc

        first, passed, waits, locs = [], [], [], []
        for i in range(n):
            def copy(k, block, to, src=None, i=i):
                dst = outs[i].at[slot(*block)]
                return pltpu.make_async_remote_copy(
                    src_ref=dst if src is None else src, dst_ref=dst, send_sem=send_sems.at[7 * i + k],
                    recv_sem=recv_sems.at[7 * i + k], device_id=to, device_id_type=MESH)

            locs.append(pltpu.make_async_copy(ins[i], outs[i].at[slot(x, y, c)], local_sems.at[i]))
            first.append(copy(0, (x, y, c), sibling, src=ins[i]))
            first += [copy(1 + j, (x, y, c), (*chip, c), src=ins[i]) for j, chip in enumerate(chips)]
            passed.append([copy(4 + j, (*chip, c), sibling) for j, chip in enumerate(chips)])
            waits.append([copy(1 + j, (*chip, c), (x, y, c)) for j, chip in enumerate(chips)])
        for cp in locs + first:
            cp.start()
        for j in range(3):
            for i in range(n):
                waits[i][j].wait_recv()
                passed[i][j].start()
        for i in range(n):
            def arrival(k, block, i=i):
                dst = outs[i].at[slot(*block)]
                return pltpu.make_async_remote_copy(
                    src_ref=dst, dst_ref=dst, send_sem=send_sems.at[7 * i + k], recv_sem=recv_sems.at[7 * i + k],
                    device_id=sibling, device_id_type=MESH)

            arrival(0, (x, y, 1 - c)).wait_recv()
            for j, chip in enumerate(chips):
                arrival(4 + j, (*chip, 1 - c)).wait_recv()
        for cp in first + [p for ps in passed for p in ps]:
            cp.wait_send()
        for cp in locs:
            cp.wait()

    outs = pl.pallas_call(
        body, name=name, out_shape=[_sds((N_DEV,) + a.shape, a.dtype) for a in arrs], in_specs=[ANY] * n,
        out_specs=[ANY] * n,
        scratch_shapes=[pltpu.SemaphoreType.DMA((7 * n,)), pltpu.SemaphoreType.DMA((7 * n,)),
                        pltpu.SemaphoreType.DMA((n,))],
    )(*arrs)
    return list(outs)


HBM = pl.BlockSpec(memory_space=pltpu.HBM)
SEM = pl.BlockSpec(memory_space=pltpu.SEMAPHORE)
EFFECT = pltpu.SideEffectType.DATAFLOW_SIDE_EFFECTING


def _peer(k, chips=False):
    x, y, c = lax.axis_index("x"), lax.axis_index("y"), lax.axis_index("c")
    bits = (k + 1) << 1 if chips else k + 1
    px, py, pc = x ^ ((bits >> 2) & 1), y ^ ((bits >> 1) & 1), c ^ (bits & 1)
    if chips:
        return (px, py, pc), 2 * px + py, 2 * x + y
    return (px, py, pc), 4 * px + 2 * py + pc, 4 * x + 2 * y + c


def _exchange_start(arrs, gather, name, after=None, chips=False):
    n = len(arrs)
    n_peer, n_slot = (3, 4) if chips else (N_DEV - 1, N_DEV)
    modes = [gather] * n if isinstance(gather, bool) else list(gather)
    lands = [pltpu.with_memory_space_constraint(lax.empty(((n_slot,) + a.shape) if g else a.shape, a.dtype), pltpu.HBM)
             for a, g in zip(arrs, modes)]
    srcs = [pltpu.with_memory_space_constraint(a, pltpu.HBM) for a in arrs]

    extra = [] if after is None else [after]

    def body(*refs):
        ins, zones = refs[:n], refs[n:2 * n]
        send_sems, recv_sems, local_sems = refs[2 * n + len(extra):2 * n + len(extra) + 3]
        token = refs[-1]
        for i in range(n):
            gather = modes[i]
            for k in range(n_peer):
                dev, peer, me = _peer(k, chips)
                sem = i * n_peer + k
                pltpu.make_async_remote_copy(
                    src_ref=ins[i] if gather else ins[i].at[peer], dst_ref=zones[i].at[me],
                    send_sem=send_sems.at[sem], recv_sem=recv_sems.at[sem], device_id=dev, device_id_type=MESH).start()
            _, _, me = _peer(0, chips)
            pltpu.make_async_copy(ins[i] if gather else ins[i].at[me], zones[i].at[me], local_sems.at[i]).start()
        token[...] = jnp.zeros_like(token)

    nsem = n * n_peer
    outs = pl.pallas_call(
        body, name=name,
        out_shape=[pltpu.SemaphoreType.DMA((nsem,)), pltpu.SemaphoreType.DMA((nsem,)), pltpu.SemaphoreType.DMA((n,))]
        + [pltpu.HBM(a.shape, a.dtype) for a in srcs] + [pltpu.HBM(z.shape, z.dtype) for z in lands]
        + [_sds((8, LANE))],
        in_specs=[HBM] * (2 * n) + [ANY] * len(extra),
        out_specs=[SEM, SEM, SEM] + [HBM] * (2 * n) + [pl.BlockSpec(memory_space=pltpu.VMEM)],
        input_output_aliases={i: 3 + i for i in range(2 * n)},
        compiler_params=pltpu.CompilerParams(has_side_effects=EFFECT),
    )(*srcs, *lands, *extra)
    return {"n": n, "gather": modes, "chips": chips, "sems": outs[:3], "srcs": outs[3:3 + n],
            "lands": outs[3 + n:3 + 2 * n], "token": outs[-1]}


def _exchange_wait(st, after, name, which=None):
    modes, chips = st["gather"], st["chips"]
    afters = list(after) if isinstance(after, (list, tuple)) else [after]
    which = list(range(st["n"])) if which is None else which
    n = len(which)
    n_peer = 3 if chips else N_DEV - 1
    srcs, lands = [st["srcs"][i] for i in which], [st["lands"][i] for i in which]

    def body(*refs):
        ins, zones = refs[:n], refs[n:2 * n]
        send_sems, recv_sems, local_sems = refs[2 * n:2 * n + 3]
        for j, i in enumerate(which):
            gather = modes[i]
            for k in range(n_peer):
                dev, peer, me = _peer(k, chips)
                sem = i * n_peer + k
                src = ins[j] if gather else ins[j].at[peer]
                cp = pltpu.make_async_remote_copy(
                    src_ref=src, dst_ref=zones[j].at[peer], send_sem=send_sems.at[sem], recv_sem=recv_sems.at[sem],
                    device_id=dev, device_id_type=MESH)
                cp.wait_send()
                cp.wait_recv()
            _, _, me = _peer(0, chips)
            pltpu.make_async_copy(ins[j] if gather else ins[j].at[me], zones[j].at[me], local_sems.at[i]).wait()

    outs = pl.pallas_call(
        body, name=name,
        out_shape=[pltpu.HBM(a.shape, a.dtype) for a in srcs] + [pltpu.HBM(z.shape, z.dtype) for z in lands],
        in_specs=[HBM] * (2 * n) + [SEM, SEM, SEM] + [ANY] * len(afters), out_specs=[HBM] * (2 * n),
        input_output_aliases={i: i for i in range(2 * n)},
        compiler_params=pltpu.CompilerParams(has_side_effects=EFFECT),
    )(*srcs, *lands, *st["sems"], *afters)
    return list(outs[n:])


def _pair_reduce(arrs, name, after):
    n = len(arrs)

    def body(*refs):
        ins, refs = refs[:n], refs[n + 1:]
        outs, got, mine = refs[:n], refs[n:2 * n], refs[2 * n:3 * n]
        send_sems, recv_sems, local_sems = refs[3 * n:]
        x, y, c = lax.axis_index("x"), lax.axis_index("y"), lax.axis_index("c")
        sends, locs = [], []
        for i in range(n):
            for q in range(4):
                sem = 4 * i + q
                sends.append(pltpu.make_async_remote_copy(
                    src_ref=ins[i].at[2 * q + 1 - c], dst_ref=got[i].at[q], send_sem=send_sems.at[sem],
                    recv_sem=recv_sems.at[sem], device_id=(x, y, 1 - c), device_id_type=MESH))
                locs.append(pltpu.make_async_copy(ins[i].at[2 * q + c], mine[i].at[q], local_sems.at[sem]))
        for cp in locs + sends:
            cp.start()
        for cp in sends:
            cp.wait_recv()
        for cp in locs:
            cp.wait()
        for i in range(n):
            outs[i][...] = (mine[i][...].astype(F32) + got[i][...].astype(F32)).astype(BF)
        for cp in sends:
            cp.wait_send()

    half = [(4,) + a.shape[1:] for a in arrs]
    outs = pl.pallas_call(
        body, name=name, out_shape=[_sds(h, BF) for h in half], in_specs=[ANY] * (n + 1),
        out_specs=[pl.BlockSpec(memory_space=pltpu.VMEM)] * n,
        scratch_shapes=[pltpu.VMEM(h, BF) for h in half] * 2
        + [pltpu.SemaphoreType.DMA((4 * n,)), pltpu.SemaphoreType.DMA((4 * n,)), pltpu.SemaphoreType.DMA((4 * n,))],
        compiler_params=_cp(),
    )(*arrs, after)
    return list(outs)


def _mod_fwd(crows, w_ada, b_blk):
    def body(c_ref, w_ref, b_ref, o_ref):
        o_ref[...] = mm(silu(c_ref[...]), w_ref[...]) + b_ref[...]

    return pl.pallas_call(body, name="mod_fwd", out_shape=_sds((24, 768)), compiler_params=_cp())(crows, w_ada, b_blk)


def _mod_bwd(crows, w_ada, dmod_blk, dmodc_blk):
    def body(c_ref, w_ref, d_ref, dc_ref, gw_ref, gc_ref, gb_ref):
        cr = c_ref[...]
        dc = dc_ref[0:1, :]
        for p in range(1, N_DEV):
            dc = dc + dc_ref[p:p + 1, :]
        row = lax.broadcasted_iota(jnp.int32, (24, 1), 0)
        gw_ref[...] = mm_tn(silu(cr), jnp.where(row == 16, dc, d_ref[...]))
        cc = cr[16:17, :]
        sg = jax.nn.sigmoid(cc)
        part = mm_nt(jnp.broadcast_to(dc, (8, 768)), w_ref[...])
        gc_ref[...] = part * (sg * (1.0 + cc * (1.0 - sg)))
        gb_ref[...] = jnp.broadcast_to(jnp.sum(d_ref[...], axis=0, keepdims=True) + dc, (8, 768))

    return pl.pallas_call(
        body, name="mod_bwd", out_shape=[_sds((D_MODEL, 768)), _sds((8, D_MODEL)), _sds((8, 768))],
        compiler_params=_cp())(crows, w_ada, dmod_blk, dmodc_blk)


def _tab_specs(tk):
    return [pl.BlockSpec((tk, LANE), lambda i, t: (t, 0))] * 2


def _k1_fwd(x, mod, g_attn, g_q, g_kv, ws, tabs, kv_all, is_ctx):
    b, l, _ = x.shape
    tk = CTX_LEN if is_ctx else TOK
    nt = l // tk
    n_f32 = 2 if is_ctx else 4

    def body(x_ref, mod_ref, ga_ref, gq_ref, gk_ref, wa_ref, wb_ref, wq_ref, wk_ref, cs_ref, sn_ref, *rest):
        outs = rest if is_ctx else rest[2:]
        res = k1_tile(x_ref[...], mod_ref[0:1, :], mod_ref[1:2, :], ga_ref[...], gq_ref[...], gk_ref[...],
                      (wa_ref[...], wb_ref[...], wq_ref[...], wk_ref[...]), (None,) * 4,
                      (cs_ref[...], sn_ref[...]), is_ctx)
        for o_ref, r in zip(outs, res):
            o_ref[...] = r.astype(o_ref.dtype)

    tok = lambda w, off=0: pl.BlockSpec((None, tk, w), lambda i, t: (i, t + off, 0))
    mod_spec = pl.BlockSpec((None, 8, D_MODEL), (lambda i, t: (0, 0, 0)) if is_ctx else (lambda i, t: (i, 0, 0)))
    kv_off = SEQ // tk if is_ctx else 0
    in_specs = ([tok(D_MODEL), mod_spec, _full((1, D_MODEL)), _full((1, 384)), _full((1, 256))]
                + [_full(s) for s in W_SHAPES] + _tab_specs(tk))
    args = [x, mod, g_attn, g_q, g_kv, *ws, *tabs]
    out_specs = [tok(512)] * n_f32 + ([] if is_ctx else [tok(1024)]) + [tok(1024, kv_off), tok(512, kv_off)]
    out_shape = ([_sds((b, l, 512))] * n_f32 + ([] if is_ctx else [_sds((b, l, 1024), BF)])
                 + [_sds((b, KV_LEN, 1024), BF), _sds((b, KV_LEN, 512), BF)])
    aliases = {}
    if not is_ctx:
        aliases = {len(args): n_f32 + 1, len(args) + 1: n_f32 + 2}
        in_specs += [ANY, ANY]
        args += list(kv_all)
    return pl.pallas_call(
        body, name="k1_fwd_ctx" if is_ctx else "k1_fwd", grid=(b, nt), in_specs=in_specs, out_specs=out_specs,
        out_shape=out_shape, input_output_aliases=aliases, compiler_params=_cp((ARB, ARB)),
    )(*args)


N_ACC = 7


def _k1_bwd(x, ctx, mod, mod_c, g_attn, g_q, g_kv, ws, tabs, cts, cts_c, dx_res):
    b, l, _ = x.shape
    tk = TOK_B
    nt = l // tk
    flat = [[a for group in c for a in group] for c in (cts, cts_c)]
    sizes = [[len(g) for g in c] for c in (cts, cts_c)]
    acc_shapes = W_SHAPES + [(1, D_MODEL), (1, 384), (1, 256)]

    def body(*refs):
        it = iter(refs)
        x_ref, c_ref, mod_ref, modc_ref, ga_ref, gq_ref, gk_ref = [next(it) for _ in range(7)]
        w_hbm = [next(it) for _ in range(4)]
        tab_refs = [next(it) for _ in range(2)]
        ct_refs = [[next(it) for _ in f] for f in flat]
        res_ref, gx_ref = next(it), next(it)
        out_hbm = [next(it) for _ in range(N_ACC)]
        dmod_ref, dmodc_ref = next(it), next(it)
        w_vmem = [next(it) for _ in range(4)]
        accs = [next(it) for _ in range(N_ACC)]
        sem = next(it)
        i, t = pl.program_id(0), pl.program_id(1)
        first = jnp.logical_and(i == 0, t == 0)

        @pl.when(first)
        def _():
            for src, dst in zip(w_hbm, w_vmem):
                pltpu.sync_copy(src, dst)
            for k in range(N_ACC):
                accs[k][...] = jnp.zeros(acc_shapes[k], F32)

        def tile(is_ctx):
            which = 1 if is_ctx else 0
            ct_vals, pos = [], 0
            for gsz in sizes[which]:
                v = ct_refs[which][pos][...].astype(F32)
                for r in ct_refs[which][pos + 1:pos + gsz]:
                    v = v + r[...]
                ct_vals.append(v)
                pos += gsz
            wv = tuple(r[...] for r in w_vmem)
            tv = tuple(r[...] for r in tab_refs)
            m_ref = modc_ref if is_ctx else mod_ref

            def f(xv, sh, sc, ga, gq, gk, *probes):
                return k1_tile(xv, sh, sc, ga, gq, gk, wv, probes, tv, is_ctx)

            probes = [jnp.zeros(s, F32) for s in W_SHAPES]
            xin = c_ref[...] if is_ctx else x_ref[...]
            _, vjp = jax.vjp(f, xin, m_ref[0:1, :], m_ref[1:2, :], ga_ref[...], gq_ref[...], gk_ref[...], *probes)
            dx, dsh, dsc, dga, dgq, dgk, dwa, dwb, dwq, dwk = vjp(tuple(ct_vals))
            for ref, val in zip(accs, (dwa, dwb, dwq, dwk, dga, dgq, dgk)):
                ref[...] += val
            return dx, dsh, dsc

        @pl.when(t == 0)
        def _():
            _, dsh, dsc = tile(True)
            _acc(dmodc_ref.at[0:1, :], dsh, i == 0)
            _acc(dmodc_ref.at[1:2, :], dsc, i == 0)

            @pl.when(i == 0)
            def _():
                dmodc_ref[2:8, :] = jnp.zeros((6, D_MODEL), F32)

        @pl.when(t > 0)
        def _():
            dx, dsh, dsc = tile(False)
            gx_ref[...] = dx + res_ref[...]
            _acc(dmod_ref.at[0:1, :], dsh, t == 1)
            _acc(dmod_ref.at[1:2, :], dsc, t == 1)

            @pl.when(t == 1)
            def _():
                dmod_ref[2:8, :] = jnp.zeros((6, D_MODEL), F32)

        @pl.when(jnp.logical_and(i == b - 1, t == nt))
        def _():
            for k in range(4):
                w_vmem[k][...] = accs[k][...].astype(BF)
            cps = [pltpu.make_async_copy(w_vmem[k] if k < 4 else accs[k], out_hbm[k], sem.at[k]) for k in range(N_ACC)]
            for cp in cps:
                cp.start()
            for cp in cps:
                cp.wait()

    lat = lambda w, off=0: pl.BlockSpec((None, tk, w), lambda i, t: (i, jnp.maximum(t - 1, 0) + off, 0))
    con = lambda w, off=0: pl.BlockSpec((None, tk, w), lambda i, t: (i, off, 0))
    mod_spec = pl.BlockSpec((None, 8, D_MODEL), lambda i, t: (i, 0, 0))
    modc_spec = pl.BlockSpec((None, 8, D_MODEL), lambda i, t: (0, 0, 0))
    tab_spec = pl.BlockSpec((tk, LANE), lambda i, t: (jnp.maximum(t - 1, 0), 0))
    in_specs = ([lat(D_MODEL), con(D_MODEL), mod_spec, modc_spec, _full((1, D_MODEL)), _full((1, 384)), _full((1, 256))]
                + [ANY] * 4 + [tab_spec] * 2)
    args = [x, ctx, mod, mod_c, g_attn, g_q, g_kv, *ws, *tabs]
    for a, off in flat[0]:
        in_specs.append(lat(a.shape[-1], off // tk))
        args.append(a)
    for a, off in flat[1]:
        in_specs.append(con(a.shape[-1], off // tk))
        args.append(a)
    in_specs.append(lat(D_MODEL))
    args.append(dx_res)
    out_shape = ([_sds((b, l, D_MODEL))] + [_sds(s, BF) for s in W_SHAPES] + [_sds(s) for s in acc_shapes[4:]]
                 + [_sds((b, 8, D_MODEL)), _sds((1, 8, D_MODEL))])
    out_specs = [lat(D_MODEL)] + [ANY] * N_ACC + [mod_spec, modc_spec]
    outs = pl.pallas_call(
        body, name="k1_bwd", grid=(b, nt + 1), in_specs=in_specs, out_specs=out_specs, out_shape=out_shape,
        scratch_shapes=[pltpu.VMEM(s, BF) for s in W_SHAPES] + [pltpu.VMEM(s, F32) for s in acc_shapes]
        + [pltpu.SemaphoreType.DMA((N_ACC,))],
        compiler_params=_cp((ARB, ARB)),
    )(*args)
    return outs[0], list(outs[1:1 + N_ACC]), outs[1 + N_ACC], outs[2 + N_ACC]


def _chunk_spec(rev):
    if rev:
        return pl.BlockSpec((None, RET_CHUNK, 512), lambda i, n: (i, N_CHUNK - 1 - n, 0))
    return pl.BlockSpec((None, RET_CHUNK, 512), lambda i, n: (i, n, 0))


def _state_spec(rev):
    if rev:
        return pl.BlockSpec((None, N_HEADS, None, LANE, LANE), lambda i, n: (i, 0, N_CHUNK - 1 - n, 0, 0))
    return pl.BlockSpec((None, N_HEADS, None, LANE, LANE), lambda i, n: (i, 0, n, 0, 0))


_CTX_SPEC = pl.BlockSpec((None, CTX_LEN, 512), lambda i, n: (i, 0, 0))
_DEC_SPEC = pl.BlockSpec((N_HEADS, 1, 1), lambda i, n: (0, 0, 0))


def _k2_fwd(rq, rk, rv, rkc, rvc, dec_f, dec_b):
    b = rq.shape[0]

    def body(qf, kf, vf, qb, kb, vb, kc, vc, df, db, of_ref, ob_ref, sf_out, sb_out, sf, sb, masks, rows):
        n = pl.program_id(1)
        first = jnp.logical_and(pl.program_id(0) == 0, n == 0)
        for h, sl in enumerate(_HEAD_SL):
            lgf, lgb = log_sigmoid(df[h]), log_sigmoid(db[h])

            @pl.when(first)
            def _():
                for d, (lg, rev) in enumerate(((lgf, False), (lgb, True))):
                    masks[2 * h + d] = decay_mask(lg, rev)
                    for j, w in enumerate(decay_rows(lg, rev)):
                        rows[2 * h + d, j] = jnp.broadcast_to(w, (RET_CHUNK, LANE))

            @pl.when(n == 0)
            def _():
                sf[h] = ctx_state(kc[:, sl], vc[:, sl], lgf, False)
                sb[h] = ctx_state(kc[:, sl], vc[:, sl], lgb, True)

            sf_out[h] = sf[h]
            sb_out[h] = sb[h]
            o, s = ret_chunk(qf[:, sl], kf[:, sl], vf[:, sl], sf[h], lgf, False,
                             (masks[2 * h], rows[2 * h, 0], rows[2 * h, 1]))
            of_ref[:, sl] = o
            sf[h] = s
            o, s = ret_chunk(qb[:, sl], kb[:, sl], vb[:, sl], sb[h], lgb, True,
                             (masks[2 * h + 1], rows[2 * h + 1, 0], rows[2 * h + 1, 1]))
            ob_ref[:, sl] = o
            sb[h] = s

    l = rq.shape[1]
    return pl.pallas_call(
        body, name="k2_fwd", grid=(b, N_CHUNK),
        in_specs=[_chunk_spec(False)] * 3 + [_chunk_spec(True)] * 3 + [_CTX_SPEC, _CTX_SPEC, _DEC_SPEC, _DEC_SPEC],
        out_specs=[_chunk_spec(False), _chunk_spec(True), _state_spec(False), _state_spec(True)],
        out_shape=[_sds((b, l, 512)), _sds((b, l, 512)), _sds((b, N_HEADS, N_CHUNK, LANE, LANE)),
                   _sds((b, N_HEADS, N_CHUNK, LANE, LANE))],
        scratch_shapes=[pltpu.VMEM((N_HEADS, LANE, LANE), F32), pltpu.VMEM((N_HEADS, LANE, LANE), F32),
                        pltpu.VMEM((2 * N_HEADS, RET_CHUNK, RET_CHUNK), F32),
                        pltpu.VMEM((2 * N_HEADS, 2, RET_CHUNK, LANE), F32)],
        compiler_params=_cp((ARB, ARB)),
    )(rq, rk, rv, rq, rk, rv, rkc, rvc, dec_f, dec_b)


def _k2_bwd(rq, rk, rv, do, sf_prev, sb_prev, rkc, rvc, dec_f, dec_b):
    b, l, _ = rq.shape

    def body(qf, kf, vf, gf, spf, qb, kb, vb, gb, spb, kc, vc, df, db,
             dqf, dkf, dvf, dqb, dkb, dvb, dkc, dvc, ddf, ddb, dsf, dsb, masks, rows):
        n = pl.program_id(1)

        @pl.when(jnp.logical_and(pl.program_id(0) == 0, n == 0))
        def _():
            for h in range(N_HEADS):
                for d, (lg, rev) in enumerate(((log_sigmoid(df[h]), False), (log_sigmoid(db[h]), True))):
                    masks[2 * h + d] = decay_mask(lg, rev)
                    for j, w in enumerate(decay_rows(lg, rev)):
                        rows[2 * h + d, j] = jnp.broadcast_to(w, (RET_CHUNK, LANE))

        @pl.when(n == 0)
        def _():
            dsf[...] = jnp.zeros((N_HEADS, LANE, LANE), F32)
            dsb[...] = jnp.zeros((N_HEADS, LANE, LANE), F32)

        def one(h, sl, q, k, v, g, sp, dec, ds, dq, dk, dv, dd, rev):
            hd = 2 * h + int(rev)
            mask, wk, wq = masks[hd], rows[hd, 0], rows[hd, 1]

            def f(qv, kv_, vv, sv, dcy):
                lg = log_sigmoid(dcy)
                return ret_chunk(qv, kv_, vv, sv, lg, rev, (kept_decay_mask(lg, mask, rev),
                                                            kept_decay_row(lg, wk, rev, 0), kept_decay_row(lg, wq, rev, 1)))

            _, vjp = jax.vjp(f, q[:, sl], k[:, sl], v[:, sl], sp[h], dec[h])
            gq, gk, gv, gs, gd = vjp((g[:, sl], ds[h]))
            dq[:, sl] = gq
            dk[:, sl] = gk
            dv[:, sl] = gv
            ds[h] = gs
            _acc(dd.at[h], jnp.broadcast_to(gd, (8, LANE)), n == 0)

        for h, sl in enumerate(_HEAD_SL):
            one(h, sl, qf, kf, vf, gf, spf, df, dsf, dqf, dkf, dvf, ddf, False)
            one(h, sl, qb, kb, vb, gb, spb, db, dsb, dqb, dkb, dvb, ddb, True)

        @pl.when(n == N_CHUNK - 1)
        def _():
            def f(kcv, vcv, dcy, rev):
                return ctx_state(kcv, vcv, log_sigmoid(dcy), rev)

            for h, sl in enumerate(_HEAD_SL):
                _, vjp_f = jax.vjp(functools.partial(f, rev=False), kc[:, sl], vc[:, sl], df[h])
                gk_f, gv_f, gd_f = vjp_f(dsf[h])
                _, vjp_b = jax.vjp(functools.partial(f, rev=True), kc[:, sl], vc[:, sl], db[h])
                gk_b, gv_b, gd_b = vjp_b(dsb[h])
                dkc[:, sl] = gk_f + gk_b
                dvc[:, sl] = gv_f + gv_b
                ddf[h] += jnp.broadcast_to(gd_f, (8, LANE))
                ddb[h] += jnp.broadcast_to(gd_b, (8, LANE))

    dd_spec = pl.BlockSpec((None, N_HEADS, 8, LANE), lambda i, n: (i, 0, 0, 0))
    return pl.pallas_call(
        body, name="k2_bwd", grid=(b, N_CHUNK),
        in_specs=[_chunk_spec(True)] * 4 + [_state_spec(True)] + [_chunk_spec(False)] * 4 + [_state_spec(False)]
        + [_CTX_SPEC, _CTX_SPEC, _DEC_SPEC, _DEC_SPEC],
        out_specs=[_chunk_spec(True)] * 3 + [_chunk_spec(False)] * 3 + [_CTX_SPEC, _CTX_SPEC, dd_spec, dd_spec],
        out_shape=[_sds((b, l, 512))] * 6 + [_sds((b, CTX_LEN, 512))] * 2 + [_sds((b, N_HEADS, 8, LANE))] * 2,
        scratch_shapes=[pltpu.VMEM((N_HEADS, LANE, LANE), F32), pltpu.VMEM((N_HEADS, LANE, LANE), F32),
                        pltpu.VMEM((2 * N_HEADS, RET_CHUNK, RET_CHUNK), F32),
                        pltpu.VMEM((2 * N_HEADS, 2, RET_CHUNK, LANE), F32)],
        compiler_params=_cp((ARB, ARB)),
    )(rq, rk, rv, do, sf_prev, rq, rk, rv, do, sb_prev, rkc, rvc, dec_f, dec_b)


TQ = 1024
TQ_F = 512
QK_W = 2 * LANE
N_QP = 2
_Q_PARTS = [slice(i * TQ_F // N_QP, (i + 1) * TQ_F // N_QP) for i in range(N_QP)]


SM_SCALE = 1.0 / math.sqrt(192.0)


def _k3_specs(tq):
    qs = lambda w: pl.BlockSpec((None, tq, w), lambda i, h, t: (i, t, h))
    ks = lambda w: pl.BlockSpec((None, KV_LEN, w), lambda i, h, t: (i, 0, h))
    return qs, ks


def _k3_fwd(q, k, v):
    b, l, _ = q.shape

    def body(q_ref, k_ref, v_ref, o_ref, lse_ref):
        kv_, vv = k_ref[...], v_ref[...]
        for r in _Q_PARTS:
            s = _dot(q_ref[r, :], kv_, 1, 1)
            m = jnp.max(s, axis=-1, keepdims=True)
            e = jnp.exp2((s - m) * (SM_SCALE * LOG2_E))
            tot = jnp.sum(e, axis=-1, keepdims=True)
            o_ref[r, :] = _dot(e, vv, 1, 0) * (1.0 / tot)
            lse_ref[r, :] = jnp.broadcast_to(m * SM_SCALE + jnp.log(tot), (TQ_F // N_QP, LANE))

    qs, ks = _k3_specs(TQ_F)
    return pl.pallas_call(
        body, name="k3_fwd", grid=(b, N_HEADS, l // TQ_F), in_specs=[qs(QK_W), ks(QK_W), ks(LANE)],
        out_specs=[qs(LANE), qs(LANE)], out_shape=[_sds((b, l, N_HEADS * LANE))] * 2,
        compiler_params=_cp((ARB, ARB, ARB)),
    )(q, k, v)


def _k3_bwd(q, k, v, o, lse, dy, after):
    b, l, _ = q.shape

    def body(q_ref, k_ref, v_ref, o_ref, lse_ref, dy_ref, after_ref, dq_ref, dk_ref, dv_ref):
        t0 = pl.program_id(2) == 0
        kv_, vv = k_ref[...], v_ref[...]
        qv, dyv = q_ref[...], dy_ref[...]
        g = dyv.astype(BF)
        lse_col = jnp.max(lse_ref[...], axis=-1, keepdims=True)
        delta = jnp.sum(dyv * o_ref[...], axis=-1, keepdims=True)
        p = jnp.exp2(_dot(qv, kv_, 1, 1) * (SM_SCALE * LOG2_E) - lse_col * LOG2_E)
        ds = (p * (_dot(g, vv, 1, 1) - delta) * SM_SCALE).astype(BF)
        _acc(dv_ref, _dot(p, g, 0, 0), t0)
        dq_ref[...] = _dot(ds, kv_, 1, 0)
        _acc(dk_ref, _dot(ds, qv, 0, 0), t0)

    qs, ks = _k3_specs(TQ)
    return pl.pallas_call(
        body, name="k3_bwd", grid=(b, N_HEADS, l // TQ),
        in_specs=[qs(QK_W), ks(QK_W), ks(LANE), qs(LANE), qs(LANE), qs(LANE), ANY],
        out_specs=[qs(QK_W), ks(QK_W), ks(LANE)],
        out_shape=[_sds((b, l, N_HEADS * QK_W)), _sds((b, KV_LEN, N_HEADS * QK_W)), _sds((b, KV_LEN, N_HEADS * LANE))],
        compiler_params=_cp((ARB, ARB, ARB)),
    )(q, k, v, o, lse, dy, after)


def _mod_rows(mod_ref, rows):
    return [mod_ref[r:r + 1, :] for r in rows]


def _k4a_fwd(x, o_f, o_b, rg, y_mla, g_ret, w_out, mod, g_ffn):
    b, l, _ = x.shape

    def body(x_ref, of_ref, ob_ref, rg_ref, ym_ref, gr_ref, wo_ref, mod_ref, gf_ref, xm_ref, h2_ref):
        gt_a, sh_f, sc_f = _mod_rows(mod_ref, (2, 3, 4))
        x_mid, h2 = k4a_tile(x_ref[...], of_ref[...], ob_ref[...], rg_ref[...], ym_ref[...], gr_ref[...], gt_a,
                             gf_ref[...], sh_f, sc_f, wo_ref[...], None)
        xm_ref[...] = x_mid
        h2_ref[...] = h2.astype(BF)

    tok = lambda w: pl.BlockSpec((None, TOK, w), lambda i, t: (i, t, 0))
    mod_spec = pl.BlockSpec((None, 8, D_MODEL), lambda i, t: (i, 0, 0))
    return pl.pallas_call(
        body, name="k4a_fwd", grid=(b, l // TOK),
        in_specs=[tok(D_MODEL), tok(512), tok(512), tok(512), tok(512), _full((1, 512)), _full((D_MODEL, D_MODEL)),
                  mod_spec, _full((1, D_MODEL))],
        out_specs=[tok(D_MODEL), tok(D_MODEL)], out_shape=[_sds((b, l, D_MODEL)), _sds((b, l, D_MODEL), BF)],
        compiler_params=_cp((ARB, ARB)),
    )(x, o_f, o_b, rg, y_mla, g_ret, w_out, mod, g_ffn)


TOK_M = 512
TOK_D = 2048
HALF_FF = D_FF // 2


def _k4b_mlp_loss(h2, w1t, w2, x_mid, mod, g_final, tgt):
    b, l, _ = h2.shape
    nt = l // TOK_M

    def body(h2_ref, w1_hbm, w2_hbm, xm_ref, mod_ref, gfin_ref, tgt_ref, dxm_ref, dmlp_ref, r_ref, loss_ref, dgt_ref,
             dgfin_ref, w1_v, w2_v):
        i, t = pl.program_id(0), pl.program_id(1)
        first = jnp.logical_and(i == 0, t == 0)

        @pl.when(first)
        def _():
            pltpu.sync_copy(w1_hbm, w1_v)
            pltpu.sync_copy(w2_hbm, w2_v)

        h2v = h2_ref[...]
        mlp = None
        for half in range(2):
            rows = slice(half * HALF_FF, (half + 1) * HALF_FF)
            r = jnp.maximum(_dot(h2v, w1_v[rows, :], 1, 1), 0.0)
            r_ref[:, rows] = r.astype(BF)
            part = _dot(jnp.square(r), w2_v[rows, :], 1, 0)
            mlp = part if mlp is None else mlp + part
        (gt_f,) = _mod_rows(mod_ref, (5,))
        loss, vjp = jax.vjp(k4c_tile, xm_ref[...], mlp, gt_f, gfin_ref[...], tgt_ref[...])
        dxm, dmlp, dgt, dgfin, _ = vjp(jnp.ones((1, 1), F32))
        dxm_ref[...] = dxm
        dmlp_ref[...] = dmlp.astype(BF)
        _acc(loss_ref, jnp.broadcast_to(loss, (8, LANE)), first)
        _acc(dgfin_ref, dgfin, first)
        _acc(dgt_ref, dgt, t == 0)

    tok = lambda w: pl.BlockSpec((None, TOK_M, w), lambda i, t: (i, t, 0))
    return pl.pallas_call(
        body, name="k4b_mlp_loss", grid=(b, nt),
        in_specs=[tok(D_MODEL), ANY, ANY, tok(D_MODEL), pl.BlockSpec((None, 8, D_MODEL), lambda i, t: (i, 0, 0)),
                  _full((1, D_MODEL)), tok(D_MODEL)],
        out_specs=[tok(D_MODEL), tok(D_MODEL), tok(D_FF), _full((8, LANE)),
                   pl.BlockSpec((None, 1, D_MODEL), lambda i, t: (i, 0, 0)), _full((1, D_MODEL))],
        out_shape=[_sds((b, l, D_MODEL)), _sds((b, l, D_MODEL), BF), _sds((b, l, D_FF), BF), _sds((8, LANE)),
                   _sds((b, 1, D_MODEL)), _sds((1, D_MODEL))],
        scratch_shapes=[pltpu.VMEM((D_FF, D_MODEL), BF), pltpu.VMEM((D_FF, D_MODEL), BF)],
        compiler_params=_cp((ARB, ARB)),
    )(h2, w1t, w2, x_mid, mod, g_final, tgt)


def _k4d_mlp_bwd(h2, dmlp, r, w2):
    b, l, _ = h2.shape
    nt = l // TOK_D

    def body(h2_ref, dm_ref, r_ref, w2_ref, da_ref, dw1_ref, dw2_ref, acc1, acc2):
        i, t = pl.program_id(1), pl.program_id(2)
        first = jnp.logical_and(i == 0, t == 0)
        rv = r_ref[...].astype(F32)
        dm = dm_ref[...]
        da = (_dot(dm, w2_ref[...], 1, 1) * (2.0 * rv)).astype(BF)
        da_ref[...] = da
        _acc(acc2, _dot(jnp.square(rv), dm, 0, 0), first)
        _acc(acc1, _dot(h2_ref[...], da, 0, 0), first)

        @pl.when(jnp.logical_and(i == b - 1, t == nt - 1))
        def _():
            dw1_ref[...] = acc1[...].astype(BF)
            dw2_ref[...] = acc2[...].astype(BF)

    tok = lambda w: pl.BlockSpec((None, TOK_D, w), lambda j, i, t: (i, t, 0))
    col = pl.BlockSpec((None, TOK_D, FF_BLK), lambda j, i, t: (i, t, j))
    return pl.pallas_call(
        body, name="k4d_mlp_bwd", grid=(N_DEV, b, nt),
        in_specs=[tok(D_MODEL), tok(D_MODEL), col, pl.BlockSpec((None, FF_BLK, D_MODEL), lambda j, i, t: (j, 0, 0))],
        out_specs=[col, pl.BlockSpec((None, D_MODEL, FF_BLK), lambda j, i, t: (j, 0, 0)),
                   pl.BlockSpec((None, FF_BLK, D_MODEL), lambda j, i, t: (j, 0, 0))],
        out_shape=[_sds((b, l, D_FF), BF), _sds((N_DEV, D_MODEL, FF_BLK), BF), _sds((N_DEV, FF_BLK, D_MODEL), BF)],
        scratch_shapes=[pltpu.VMEM((D_MODEL, FF_BLK), F32), pltpu.VMEM((FF_BLK, D_MODEL), F32)],
        compiler_params=_cp((ARB, ARB, ARB)),
    )(h2, dmlp, r, w2)


def _k4e_bwd(x, o_f, o_b, rg, y_mla, g_ret, w_out, mod, g_ffn, dxm, da, w1t):
    b, l, _ = x.shape

    def body(x_ref, of_ref, ob_ref, rg_ref, ym_ref, gr_ref, wo_ref, mod_ref, gf_ref, dxm_ref, da_ref, w1_hbm,
             dx_ref, do_ref, drg_ref, dym_ref, dwo_ref, dgr_ref, dgf_ref, dmod_ref, w1_v, dwo_acc):
        i, t = pl.program_id(0), pl.program_id(1)
        first = jnp.logical_and(i == 0, t == 0)

        @pl.when(first)
        def _():
            pltpu.sync_copy(w1_hbm, w1_v)

        gt_a, sh_f, sc_f = _mod_rows(mod_ref, (2, 3, 4))
        wo = wo_ref[...]
        dh2 = _dot(da_ref[...], w1_v[...], 1, 0)

        def f(xv, ofv, rgv, ymv, grv, gta, gfv, shf, scf, p_out):
            return k4a_tile(xv, ofv, ob_ref[...], rgv, ymv, grv, gta, gfv, shf, scf, wo, p_out)

        _, vjp = jax.vjp(f, x_ref[...], of_ref[...], rg_ref[...], ym_ref[...], gr_ref[...], gt_a, gf_ref[...], sh_f,
                         sc_f, jnp.zeros((D_MODEL, D_MODEL), F32))
        dx, do, drg, dym, dgr, dgta, dgf, dshf, dscf, dwo = vjp((dxm_ref[...], dh2))
        dx_ref[...] = dx
        do_ref[...] = do
        drg_ref[...] = drg
        dym_ref[...] = dym
        _acc(dwo_acc, dwo, first)
        _acc(dgr_ref, dgr, first)
        _acc(dgf_ref, dgf, first)
        t0 = t == 0
        _acc(dmod_ref.at[2:3, :], dgta, t0)
        _acc(dmod_ref.at[3:4, :], dshf, t0)
        _acc(dmod_ref.at[4:5, :], dscf, t0)

        @pl.when(t0)
        def _():
            dmod_ref[0:2, :] = jnp.zeros((2, D_MODEL), F32)
            dmod_ref[5:8, :] = jnp.zeros((3, D_MODEL), F32)

        @pl.when(jnp.logical_and(i == b - 1, t == l // TOK_B - 1))
        def _():
            dwo_ref[...] = dwo_acc[...].astype(BF)

    tok = lambda w: pl.BlockSpec((None, TOK_B, w), lambda i, t: (i, t, 0))
    mod_spec = pl.BlockSpec((None, 8, D_MODEL), lambda i, t: (i, 0, 0))
    return pl.pallas_call(
        body, name="k4e_bwd", grid=(b, l // TOK_B),
        in_specs=[tok(D_MODEL), tok(512), tok(512), tok(512), tok(512), _full((1, 512)), _full((D_MODEL, D_MODEL)),
                  mod_spec, _full((1, D_MODEL)), tok(D_MODEL), tok(D_FF), ANY],
        out_specs=[tok(D_MODEL), tok(512), tok(512), tok(512), _full((D_MODEL, D_MODEL)), _full((1, 512)),
                   _full((1, D_MODEL)), mod_spec],
        out_shape=[_sds((b, l, D_MODEL)), _sds((b, l, 512)), _sds((b, l, 512)), _sds((b, l, 512)),
                   _sds((D_MODEL, D_MODEL), BF), _sds((1, 512)), _sds((1, D_MODEL)), _sds((b, 8, D_MODEL))],
        scratch_shapes=[pltpu.VMEM((D_FF, D_MODEL), BF), pltpu.VMEM((D_MODEL, D_MODEL), F32)],
        compiler_params=_cp((ARB, ARB)),
    )(x, o_f, o_b, rg, y_mla, g_ret, w_out, mod, g_ffn, dxm, da, w1t)


ADAM_BLOCK_BYTES = 32 * 1024 * 1024


def _adamw(w, m, v, pieces, name, after=None):
    r, c = w.shape
    npc = pieces.shape[0]
    per_row = c * (7 * 4 + npc * pieces.dtype.itemsize) * 2
    rb = max(d for d in range(8, r + 1, 8) if r % d == 0 and d * per_row <= ADAM_BLOCK_BYTES)

    def body(w_ref, m_ref, v_ref, p_ref, *rest):
        g_ref, d_ref, nm_ref, nv_ref = rest[-4:]
        g = p_ref[0].astype(F32)
        for k in range(1, npc):
            g = g + p_ref[k].astype(F32)
        wv = w_ref[...]
        mn = ADAM_B1 * m_ref[...] + (1.0 - ADAM_B1) * g
        vn = ADAM_B2 * v_ref[...] + (1.0 - ADAM_B2) * jnp.square(g)
        m_hat = mn / (1.0 - ADAM_B1 ** ADAM_STEP)
        v_hat = vn / (1.0 - ADAM_B2 ** ADAM_STEP)
        g_ref[...] = g
        d_ref[...] = -ADAM_LR * (m_hat / (jnp.sqrt(v_hat) + ADAM_EPS) + ADAM_WD * wv)
        nm_ref[...] = mn
        nv_ref[...] = vn

    blk = pl.BlockSpec((rb, c), lambda i: (i, 0))
    extra = [] if after is None else [after]
    return pl.pallas_call(
        body, name=name, grid=(r // rb,),
        in_specs=[blk, blk, blk, pl.BlockSpec((npc, rb, c), lambda i: (0, i, 0))] + [ANY] * len(extra),
        out_specs=[blk] * 4, out_shape=[_sds((r, c))] * 4, compiler_params=_cp((ARB,)),
    )(w, m, v, pieces, *extra)


def _pad_rot_rows(w, zero):
    k = w.shape[1]
    return lax.pad(w.reshape(-1, 2, 32, k), zero, ((0, 0, 0), (0, 0, 0), (0, 32, 0), (0, 0, 0))).reshape(-1, k)


def _cut_rot_rows(g):
    k = g.shape[1]
    return g.reshape(-1, 2, 64, k)[:, :, :32].reshape(-1, k)


def _w_in_pad(wt, zero):
    w_a = jnp.concatenate([_pad_rot_rows(wt[0:512], zero), wt[512:1536]], axis=0)
    w_b = jnp.concatenate([wt[1536:2176], _pad_rot_rows(wt[2176:2240], zero)], axis=0)
    return w_a, w_b


def _w_in_cut(g_a, g_b):
    return jnp.concatenate([_cut_rot_rows(g_a[0:1024]), g_a[1024:2048], g_b[0:640], _cut_rot_rows(g_b[640:768])], axis=0)


def _w_uq_pad(wt, zero):
    w = wt.reshape(N_HEADS, 192, 384)
    rot = _pad_rot_rows(w[:, 128:].reshape(N_HEADS * 64, 384), zero).reshape(N_HEADS, LANE, 384)
    return jnp.concatenate([w[:, :128], rot], axis=1).reshape(1024, 384)


def _w_uq_cut(g):
    g = g.reshape(N_HEADS, 256, 384)
    rot = _cut_rot_rows(g[:, 128:].reshape(N_HEADS * LANE, 384)).reshape(N_HEADS, 64, 384)
    return jnp.concatenate([g[:, :128], rot], axis=1).reshape(768, 384)


def _w_ukv_perm(wt):
    return jnp.transpose(wt.reshape(N_HEADS, 2, LANE, 256), (1, 0, 2, 3)).reshape(1024, 256)


def _w_ukv_unperm(g):
    return jnp.transpose(g.reshape(2, N_HEADS, LANE, 256), (1, 0, 2, 3)).reshape(1024, 256)


def _unshard_cols(g):
    return jnp.transpose(g, (1, 0, 2)).reshape(g.shape[1], N_DEV * g.shape[2])


def _rope_tables():
    rows = SEQ // GRID_W
    row = jnp.repeat(jnp.arange(rows, dtype=F32), GRID_W)
    col = jnp.tile(jnp.arange(GRID_W, dtype=F32), rows)
    freq = ROPE_BASE ** (-jnp.arange(16, dtype=F32) / 16)
    ang = jnp.concatenate([row[:, None] * freq, col[:, None] * freq], axis=-1)
    cos, sin = jnp.cos(ang), jnp.sin(ang)
    z = jnp.zeros((SEQ, 32), F32)
    return jnp.concatenate([cos, z, cos, z], axis=1), jnp.concatenate([-sin, z, sin, z], axis=1)


_PACKED = (("g_attn", 1024), ("g_ffn", 1024), ("ret_decay_fwd", 4), ("ret_decay_bwd", 4), ("g_ret", 512),
           ("g_q_lora", 384), ("g_kv_lora", 256), ("g_final", 1024))
_PACK_OFF = {}
_off = 0
for _name, _n in _PACKED:
    _PACK_OFF[_name] = _off
    _off += -(-_n // LANE) * LANE
PACK_W = _off


def _pack_small(vals):
    parts = []
    for name, n in _PACKED:
        a = vals[name].reshape(-1).astype(F32)
        parts.append(jnp.pad(a, (0, -(-n // LANE) * LANE - n)))
    return jnp.concatenate(parts).reshape(1, PACK_W)


def _adamw_small(params, packed, gcc, gb_ada):
    names = list(params)
    n_p = len(names)

    def body(*refs):
        p_ref, gcc_ref, gb_ref = refs[3 * n_p:3 * n_p + 3]
        outs = refs[3 * n_p + 3:]
        for k, name in enumerate(names):
            w_ref, m_ref, v_ref = refs[3 * k:3 * k + 3]
            n = w_ref.shape[1]
            if name == "b_ada":
                g = jnp.concatenate([gb_ref[d, 0:1, :] for d in range(N_DEV)], axis=-1)
            elif name == "c_ctx":
                g = gcc_ref[0, 0:1, :]
                for d in range(1, N_DEV):
                    g = g + gcc_ref[d, 0:1, :]
            else:
                off = _PACK_OFF[name]
                g = p_ref[0, :, off:off + n]
                for d in range(1, N_DEV):
                    g = g + p_ref[d, :, off:off + n]
            mn = ADAM_B1 * m_ref[...] + (1.0 - ADAM_B1) * g
            vn = ADAM_B2 * v_ref[...] + (1.0 - ADAM_B2) * jnp.square(g)
            m_hat = mn / (1.0 - ADAM_B1 ** ADAM_STEP)
            v_hat = vn / (1.0 - ADAM_B2 ** ADAM_STEP)
            outs[4 * k][...] = g
            outs[4 * k + 1][...] = -ADAM_LR * (m_hat / (jnp.sqrt(v_hat) + ADAM_EPS) + ADAM_WD * w_ref[...])
            outs[4 * k + 2][...] = mn
            outs[4 * k + 3][...] = vn

    args = [a for name in names for a in params[name]] + [packed, gcc, gb_ada]
    out_shape = [_sds(params[name][0].shape) for name in names for _ in range(4)]
    outs = pl.pallas_call(body, name="adamw_small", out_shape=out_shape, compiler_params=_cp())(*args)
    return {name: list(outs[4 * k:4 * k + 4]) for k, name in enumerate(names)}


def kernel(x, c, ctx, c_ctx, w_ada, b_ada, g_attn, g_ffn, w_in, ret_decay_fwd, ret_decay_bwd, g_ret, g_q_lora, w_uq, g_kv_lora, w_ukv, w_out, w_ff1, w_ff2, g_final, loss_target, m_c_ctx, m_w_ada, m_b_ada, m_g_attn, m_g_ffn, m_w_in, m_ret_decay_fwd, m_ret_decay_bwd, m_g_ret, m_g_q_lora, m_w_uq, m_g_kv_lora, m_w_ukv, m_w_out, m_w_ff1, m_w_ff2, m_g_final, v_c_ctx, v_w_ada, v_b_ada, v_g_attn, v_g_ffn, v_w_in, v_ret_decay_fwd, v_ret_decay_bwd, v_g_ret, v_g_q_lora, v_w_uq, v_g_kv_lora, v_w_ukv, v_w_out, v_w_ff1, v_w_ff2, v_g_final):
    me = 4 * lax.axis_index("x") + 2 * lax.axis_index("y") + lax.axis_index("c")
    nb = x.shape[0]

    c_pad = jnp.pad(c, ((0, 8 - nb), (0, 0)))
    c_all, g_in, g_uq, g_ukv = _gather_two_level(
        [c_pad, w_in[0].T.astype(BF), w_uq[0].T.astype(BF), w_ukv[0].T.astype(BF)], "gather_weights")

    crows = jnp.concatenate([c_all[:, :nb].reshape(N_DEV * nb, D_MODEL), c_ctx[None], jnp.zeros((7, D_MODEL), F32)])
    b_blk = lax.dynamic_slice(b_ada, (0, me * 768), (1, 768))
    st_f = _exchange_start([_mod_fwd(crows, w_ada[0], b_blk), w_out[0].astype(BF), w_ff1[0].T.astype(BF),
                            w_ff2[0].astype(BF)], True, "gather_fwd_start")
    zero = st_f["token"][0, 0].astype(BF)
    ws = (*_w_in_pad(g_in.reshape(2240, D_MODEL), zero), _w_uq_pad(g_uq.reshape(768, 384), zero),
          _w_ukv_perm(g_ukv.reshape(1024, 256)))
    (mod_g,) = _exchange_wait(st_f, ws, "gather_mod_wait", [0])
    mod_all = _unshard_cols(mod_g)
    mod_mine = lax.dynamic_slice(mod_all, (me * nb, 0), (nb, 6 * D_MODEL)).reshape(nb, 6, D_MODEL)
    mod = jnp.pad(mod_mine, ((0, 0), (0, 2), (0, 0)))
    mod_c = jnp.pad(mod_all[16].reshape(1, 6, D_MODEL), ((0, 0), (0, 2), (0, 0)))

    tabs = _rope_tables()
    dec_f = ret_decay_fwd.reshape(N_HEADS, 1, 1)
    dec_b = ret_decay_bwd.reshape(N_HEADS, 1, 1)

    rkc, rvc, k_ctx, v_ctx = _k1_fwd(ctx, mod_c, g_attn, g_q_lora, g_kv_lora, ws, tabs, None, True)
    rq, rk, rv, rg, q, k_all, v_all = _k1_fwd(x, mod, g_attn, g_q_lora, g_kv_lora, ws, tabs, (k_ctx, v_ctx), False)
    o_f, o_b, sf_prev, sb_prev = _k2_fwd(rq, rk, rv, rkc, rvc, dec_f, dec_b)
    y_mla, lse = _k3_fwd(q, k_all, v_all)
    (g_out,) = _exchange_wait(st_f, y_mla, "gather_wo_wait", [1])
    wo = g_out.reshape(D_MODEL, D_MODEL)
    x_mid, h2 = _k4a_fwd(x, o_f, o_b, rg, y_mla, g_ret, wo, mod, g_ffn)
    g_ff1t, g_ff2 = _exchange_wait(st_f, x_mid, "gather_ff_wait", [2, 3])
    w1t = g_ff1t.reshape(D_FF, D_MODEL)
    dxm, dmlp, relu_a, loss_acc, dgt_f, dg_final = _k4b_mlp_loss(h2, w1t, g_ff2.reshape(D_FF, D_MODEL), x_mid, mod,
                                                                 g_final.reshape(1, D_MODEL), loss_target)

    da, dw1, dw2 = _k4d_mlp_bwd(h2, dmlp, relu_a, g_ff2)
    dx_res, do, drg, dym, dwo, dg_ret, dg_ffn, dmod_a = _k4e_bwd(x, o_f, o_b, rg, y_mla, g_ret, wo, mod, g_ffn, dxm, da,
                                                                 w1t)
    st_s = _exchange_start([dw1, dw2, dwo.reshape(N_DEV, 128, D_MODEL)], False, "scatter_grads_start")
    dq, dk_all, dv_all = _k3_bwd(q, k_all, v_all, y_mla, lse, dym, st_s["token"])
    dqf, dkf, dvf, dqb, dkb, dvb, dkc, dvc, ddf, ddb = _k2_bwd(rq, rk, rv, do, sf_prev, sb_prev, rkc, rvc, dec_f, dec_b)
    cts = [[(dqf, 0), (dqb, 0)], [(dkf, 0), (dkb, 0)], [(dvf, 0), (dvb, 0)], [(drg, 0)], [(dq, 0)],
           [(dk_all, 0)], [(dv_all, 0)]]
    cts_c = [[(dkc, 0)], [(dvc, 0)], [(dk_all, SEQ)], [(dv_all, SEQ)]]
    grad_x, accs, dmod_1, dmod_c1 = _k1_bwd(x, ctx, mod, mod_c, g_attn, g_q_lora, g_kv_lora, ws, tabs, cts, cts_c,
                                            dx_res)
    dwa, dwb, dwq, dwk, dg_attn, dg_q, dg_kv = accs

    dmod_loc = (dmod_a + dmod_1).at[:, 5, :].set(dgt_f[:, 0, :])[:, :6, :].reshape(nb, 6 * D_MODEL)
    dmod_ctx = dmod_c1[:, :6, :].reshape(1, 6 * D_MODEL)
    small = {"g_attn": dg_attn, "g_ffn": dg_ffn, "ret_decay_fwd": jnp.sum(ddf[:, :, 0, 0], axis=0),
             "ret_decay_bwd": jnp.sum(ddb[:, :, 0, 0], axis=0), "g_ret": dg_ret, "g_q_lora": dg_q, "g_kv_lora": dg_kv,
             "g_final": dg_final}
    extra = jnp.concatenate([dmod_loc, dmod_ctx, jnp.zeros((5, 6 * D_MODEL), F32)])
    ex_pieces = jnp.transpose(extra.reshape(8, N_DEV, 768), (1, 0, 2))
    st_sm = _exchange_start([_pack_small(small), ex_pieces, loss_acc], [True, False, True], "gather_small_start")
    chip_sums = _pair_reduce([_w_in_cut(dwa, dwb).reshape(N_DEV, 280, D_MODEL), _w_uq_cut(dwq).reshape(N_DEV, 96, 384),
                              _w_ukv_unperm(dwk).reshape(N_DEV, 128, 256)], "pair_reduce", st_sm["token"])
    sm_g, ex_g, loss_g = _exchange_wait(st_sm, chip_sums[0], "gather_small_wait")
    dmod_blk = jnp.concatenate([ex_g[:, :nb].reshape(N_DEV * nb, 768), jnp.zeros((8, 768), F32)])
    gw_ada, gcc_part, gb_part = _mod_bwd(crows, w_ada[0], dmod_blk, ex_g[:, nb])
    st_c = _exchange_start([gcc_part, gb_part], True, "gather_cc_start")
    p_ff1, p_ff2, p_wo = _exchange_wait(st_s, st_c["token"], "scatter_grads_wait")
    st_r = _exchange_start(chip_sums, False, "scatter_rest_start", after=p_wo, chips=True)

    res = {}
    early = (("w_ff1", w_ff1, m_w_ff1, v_w_ff1, p_ff1), ("w_ff2", w_ff2, m_w_ff2, v_w_ff2, p_ff2),
             ("w_ada", w_ada, m_w_ada, v_w_ada, gw_ada[None]), ("w_out", w_out, m_w_out, v_w_out, p_wo))
    behind = st_r["token"]
    for name, w, m, v, pcs in early:
        res[name] = [a[None] for a in _adamw(w[0], m[0], v[0], pcs, "adamw_" + name, after=behind)]
        behind = res[name][3]

    smalls = {"c_ctx": (c_ctx, m_c_ctx, v_c_ctx), "b_ada": (b_ada, m_b_ada, v_b_ada), "g_attn": (g_attn, m_g_attn, v_g_attn),
              "g_ffn": (g_ffn, m_g_ffn, v_g_ffn), "ret_decay_fwd": (ret_decay_fwd, m_ret_decay_fwd, v_ret_decay_fwd),
              "ret_decay_bwd": (ret_decay_bwd, m_ret_decay_bwd, v_ret_decay_bwd), "g_ret": (g_ret, m_g_ret, v_g_ret),
              "g_q_lora": (g_q_lora, m_g_q_lora, v_g_q_lora), "g_kv_lora": (g_kv_lora, m_g_kv_lora, v_g_kv_lora),
              "g_final": (g_final, m_g_final, v_g_final)}
    rows = {k: tuple(a.reshape(1, -1) for a in t) for k, t in smalls.items()}
    gcc_g, gb_g = _exchange_wait(st_c, behind, "gather_cc_wait")
    small_out = _adamw_small(rows, sm_g, gcc_g, gb_g)
    for name, outs in small_out.items():
        res[name] = [o.reshape(smalls[name][0].shape) for o in outs]

    pieces = _exchange_wait(st_r, small_out["g_final"][3], "scatter_rest_wait")
    for name, w, m, v, pcs in (("w_in", w_in, m_w_in, v_w_in, pieces[0]), ("w_uq", w_uq, m_w_uq, v_w_uq, pieces[1])):
        res[name] = [a.T[None] for a in _adamw(w[0].T, m[0].T, v[0].T, pcs, "adamw_" + name)]
    res["w_ukv"] = [a[None] for a in _adamw(w_ukv[0], m_w_ukv[0], v_w_ukv[0], jnp.transpose(pieces[2], (0, 2, 1)),
                                            "adamw_w_ukv")]

    loss = loss_g[0, 0, 0]
    for k in range(1, N_DEV):
        loss = loss + loss_g[k, 0, 0]

    order = ("c_ctx", "w_ada", "b_ada", "g_attn", "g_ffn", "w_in", "ret_decay_fwd", "ret_decay_bwd", "g_ret", "g_q_lora",
             "w_uq", "g_kv_lora", "w_ukv", "w_out", "w_ff1", "w_ff2", "g_final")
    return (loss, grad_x, *[res[n][0] for n in order], *[res[n][1] for n in order], *[res[n][2] for n in order],
            *[res[n][3] for n in order])
```

```python
import functools
import math

import jax
import jax.numpy as jnp
from jax import lax
from jax.experimental import pallas as pl
from jax.experimental.pallas import tpu as pltpu

F32 = jnp.float32
BF = jnp.bfloat16
EPS = 1e-6
LANE = 128
LOG2_E = 1.0 / math.log(2.0)
N_DEV = 8
D_MODEL = 1024
SEQ = 2048
CTX_LEN = 256
GRID_W = 64
N_HEADS = 4
RET_CHUNK = 512
N_CHUNK = SEQ // RET_CHUNK
D_FF = 4096
FF_BLK = D_FF // N_DEV
IN_PAD = 2816
KV_LEN = CTX_LEN + SEQ
ROPE_BASE = 10000.0
ADAM_LR, ADAM_B1, ADAM_B2, ADAM_EPS, ADAM_WD, ADAM_STEP = 0.001, 0.9, 0.999, 1e-08, 0.01, 10
TOK = 512
TOK_B = 256
VMEM_LIMIT = 56 * 1024 * 1024
VMEM_LIMIT_K1_BWD = 61 * 1024 * 1024
ARB = "arbitrary"
MESH = pl.DeviceIdType.MESH
_HEAD_SL = [slice(LANE * h, LANE * (h + 1)) for h in range(N_HEADS)]
W_SHAPES = [(2048, D_MODEL), (768, D_MODEL), (1024, 384), (1024, 256)]


def _dot(a, b, ca, cb):
    return lax.dot_general(a.astype(BF), b.astype(BF), (((ca,), (cb,)), ((), ())), preferred_element_type=F32)


@jax.custom_vjp
def mm(a, b):
    return _dot(a, b, 1, 0)


@jax.custom_vjp
def mm_nt(a, b):
    return _dot(a, b, 1, 1)


@jax.custom_vjp
def mm_tn(a, b):
    return _dot(a, b, 0, 0)


mm.defvjp(lambda a, b: (_dot(a, b, 1, 0), (a, b)), lambda r, g: (mm_nt(g, r[1]), mm_tn(r[0], g)))
mm_nt.defvjp(lambda a, b: (_dot(a, b, 1, 1), (a, b)), lambda r, g: (mm(g, r[1]), mm_tn(g, r[0])))
mm_tn.defvjp(lambda a, b: (_dot(a, b, 0, 0), (a, b)), lambda r, g: (mm_nt(r[1], g), mm(r[0], g)))


@jax.custom_vjp
def _mmw(a, w, probe):
    return _dot(a, w, 1, 0)


def _mmw_bwd(r, g):
    a, w = r
    return mm_nt(g, w), jnp.zeros_like(w), mm_tn(a, g)


_mmw.defvjp(lambda a, w, probe: (_dot(a, w, 1, 0), (a, w)), _mmw_bwd)


@jax.custom_vjp
def _mmwt(a, wt, probe):
    return _dot(a, wt, 1, 1)


_mmwt.defvjp(lambda a, wt, probe: (_dot(a, wt, 1, 1), (a, wt)),
             lambda r, g: (mm(g, r[1]), jnp.zeros_like(r[1]), mm_tn(g, r[0])))


def mmwt(a, wt, probe):
    return _dot(a, wt, 1, 1) if probe is None else _mmwt(a, wt, probe)


def mmw(a, w, probe):
    return _dot(a, w, 1, 0) if probe is None else _mmw(a, w, probe)


def rmsn(x, g):
    return x * lax.rsqrt(jnp.mean(x * x, axis=-1, keepdims=True) + EPS) * g


def silu(x):
    return x * jax.nn.sigmoid(x)


def _swap_halves_impl(x):
    return pltpu.roll(x, 64, 1)


@jax.custom_vjp
def swap_halves(x):
    return _swap_halves_impl(x)


swap_halves.defvjp(lambda x: (_swap_halves_impl(x), None), lambda _, g: (_swap_halves_impl(g),))


def rope(x, cs1, sn1, every=1):
    blocks = []
    for i in range(x.shape[-1] // LANE):
        xb = x[:, LANE * i:LANE * (i + 1)]
        blocks.append(xb * cs1 + swap_halves(xb) * sn1 if i % every == every - 1 else xb)
    return blocks[0] if len(blocks) == 1 else jnp.concatenate(blocks, axis=-1)


def k1_tile(x, sh, sc, g_attn, g_q, g_kv, ws, ps, tabs, is_ctx):
    w_a, w_b, w_uq, w_ukv = ws
    p_a, p_b, p_uq, p_ukv = ps
    cs1, sn1 = tabs
    h = rmsn(x, g_attn) * (1.0 + sc) + sh
    pa = mmwt(h, w_a, p_a)
    pb = mmwt(h, w_b, p_b)
    rk = pa[:, 512:1024] * 0.125
    rv = pa[:, 1024:1536]
    kpe = pb[:, 640:768]
    kv = mmwt(rmsn(pb[:, 384:640], g_kv), w_ukv, p_ukv)
    if not is_ctx:
        rk = rope(rk, cs1, sn1)
        kpe = rope(kpe, cs1, sn1)
    k_full = jnp.concatenate([piece for sl in _HEAD_SL for piece in (kv[:, sl], kpe)], axis=-1)
    v = kv[:, 512:]
    if is_ctx:
        return rk, rv, k_full, v
    rq = rope(pa[:, 0:512], cs1, sn1)
    rg = pa[:, 1536:2048]
    q = rope(mmwt(rmsn(pb[:, 0:384], g_q), w_uq, p_uq), cs1, sn1, every=2)
    return rq, rk, rv, rg, q, k_full, v


def log_sigmoid(x):
    return jnp.minimum(x, 0.0) - jnp.log(1.0 + jnp.exp(-jnp.abs(x)))


def _distance(reverse):
    c = RET_CHUNK
    ii = lax.broadcasted_iota(jnp.int32, (c, c), 0).astype(F32)
    jj = lax.broadcasted_iota(jnp.int32, (c, c), 1).astype(F32)
    return (jj - ii) if reverse else (ii - jj)


def decay_mask(lg, reverse):
    diff = _distance(reverse)
    return jnp.where(diff >= 0, jnp.exp2((lg * LOG2_E) * jnp.maximum(diff, 0.0)), 0.0)


@functools.partial(jax.custom_vjp, nondiff_argnums=(2,))
def kept_decay_mask(lg, mask, reverse):
    return mask


def _kept_decay_mask_fwd(lg, mask, reverse):
    return mask, mask


def _kept_decay_mask_bwd(reverse, mask, g):
    return jnp.sum(g * mask * _distance(reverse), keepdims=True).reshape(1, 1), jnp.zeros_like(mask)


kept_decay_mask.defvjp(_kept_decay_mask_fwd, _kept_decay_mask_bwd)


def decay_rows(lg, reverse):
    c = RET_CHUNK
    pos = lax.broadcasted_iota(jnp.int32, (c, 1), 0).astype(F32)
    if reverse:
        return jnp.exp(lg * pos), jnp.exp(lg * (c - pos))
    return jnp.exp(lg * (c - 1.0 - pos)), jnp.exp(lg * (pos + 1.0))


def ret_chunk(q, k, v, s, lg, reverse, pre=None):
    c = RET_CHUNK
    dec, wk, wq = (decay_mask(lg, reverse), *decay_rows(lg, reverse)) if pre is None else pre
    o = mm(mm_nt(q, k) * dec, v) + mm(q * wq, s)
    s_next = jnp.exp(lg * float(c)) * s + mm_tn(k * wk, v)
    return o, s_next


def ctx_state(kc, vc, lg, reverse):
    n = kc.shape[0]
    pos = lax.broadcasted_iota(jnp.int32, (n, 1), 0).astype(F32)
    w = jnp.exp(lg * pos) if reverse else jnp.exp(lg * (n - 1.0 - pos))
    return mm_tn(kc * w, vc)


def attn_head(qn, qp, kn, kp, v):
    s = (mm_nt(qn, kn) + mm_nt(qp, kp)) * (1.0 / math.sqrt(192.0))
    e = jnp.exp(s - jnp.max(s, axis=-1, keepdims=True))
    return mm(e / jnp.sum(e, axis=-1, keepdims=True), v)


def gn_gate(o, rg, g_ret):
    ys = []
    for h in range(N_HEADS):
        sl = slice(LANE * h, LANE * (h + 1))
        oh = o[:, sl]
        mu = jnp.mean(oh, axis=-1, keepdims=True)
        var = jnp.mean(jnp.square(oh - mu), axis=-1, keepdims=True)
        ys.append((oh - mu) * lax.rsqrt(var + EPS) * g_ret[:, sl])
    return jnp.concatenate(ys, axis=-1) * silu(rg)


def k4a_tile(x, o_f, o_b, rg, y_mla, g_ret, gt_a, g_ffn, sh_f, sc_f, w_out, p_out):
    mix = jnp.concatenate([gn_gate(o_f + o_b, rg, g_ret), y_mla], axis=-1)
    x_mid = x + gt_a * mmw(mix, w_out, p_out)
    h2 = rmsn(x_mid, g_ffn) * (1.0 + sc_f) + sh_f
    return x_mid, h2


def k4c_tile(x_mid, mlp, gt_f, g_final, tgt):
    y = rmsn(x_mid + gt_f * mlp, g_final)
    per_tok = jnp.mean(jnp.square(y - tgt), axis=-1, keepdims=True)
    return 0.5 * jnp.sum(per_tok, axis=0, keepdims=True)


def _cp(sem=None, vmem=VMEM_LIMIT):
    return pltpu.CompilerParams(dimension_semantics=sem, vmem_limit_bytes=vmem)


def _acc(ref, val, first):
    @pl.when(first)
    def _():
        ref[...] = val

    @pl.when(jnp.logical_not(first))
    def _():
        ref[...] += val


def _full(shape):
    nd = len(shape)
    return pl.BlockSpec(shape, lambda *_: (0,) * nd)


ANY = pl.BlockSpec(memory_space=pl.ANY)


def _sds(shape, dtype=F32):
    return jax.ShapeDtypeStruct(shape, dtype)


def _exchange(arrs, gather, name):
    n = len(arrs)
    modes = [gather] * n if isinstance(gather, bool) else list(gather)
    out_shape = [_sds(((N_DEV,) + a.shape) if g else a.shape, a.dtype) for a, g in zip(arrs, modes)]

    def body(*refs):
        ins, outs = refs[:n], refs[n:2 * n]
        send_sems, recv_sems, local_sems = refs[2 * n:]
        x, y, c = lax.axis_index("x"), lax.axis_index("y"), lax.axis_index("c")
        me = 4 * x + 2 * y + c
        sends, recvs, locs = [], [], []
        for i in range(n):
            gather = modes[i]
            for k in range(N_DEV - 1):
                bits = k + 1
                px = x ^ ((bits >> 2) & 1)
                py = y ^ ((bits >> 1) & 1)
                pc = c ^ (bits & 1)
                peer = 4 * px + 2 * py + pc
                src = ins[i] if gather else ins[i].at[peer]
                sem = i * (N_DEV - 1) + k
                sends.append(pltpu.make_async_remote_copy(
                    src_ref=src, dst_ref=outs[i].at[me], send_sem=send_sems.at[sem], recv_sem=recv_sems.at[sem],
                    device_id=(px, py, pc), device_id_type=MESH))
                recvs.append(pltpu.make_async_remote_copy(
                    src_ref=src, dst_ref=outs[i].at[peer], send_sem=send_sems.at[sem], recv_sem=recv_sems.at[sem],
                    device_id=(px, py, pc), device_id_type=MESH))
            locs.append(pltpu.make_async_copy(ins[i] if gather else ins[i].at[me], outs[i].at[me], local_sems.at[i]))
        for cp in locs + sends:
            cp.start()
        for cp in recvs:
            cp.wait_recv()
        for cp in sends:
            cp.wait_send()
        for cp in locs:
            cp.wait()

    outs = pl.pallas_call(
        body, name=name, out_shape=out_shape, in_specs=[ANY] * n, out_specs=[ANY] * n,
        scratch_shapes=[pltpu.SemaphoreType.DMA((n * (N_DEV - 1),)), pltpu.SemaphoreType.DMA((n * (N_DEV - 1),)),
                        pltpu.SemaphoreType.DMA((n,))],
    )(*arrs)
    return list(outs)


def _gather_two_level(arrs, name):
    n = len(arrs)

    def body(*refs):
        ins, outs = refs[:n], refs[n:2 * n]
        send_sems, recv_sems, local_sems = refs[2 * n:]
        x, y, c = lax.axis_index("x"), lax.axis_index("y"), lax.axis_index("c")
        sibling = (x, y, 1 - c)
        chips = [(1 - x, y), (x, 1 - y), (1 - x, 1 - y)]

        def slot(px, py, pc):
            return 4 * px + 2 * py + pc

        first, passed, waits, locs = [], [], [], []
        for i in range(n):
            def copy(k, block, to, src=None, i=i):
                dst = outs[i].at[slot(*block)]
                return pltpu.make_async_remote_copy(
                    src_ref=dst if src is None else src, dst_ref=dst, send_sem=send_sems.at[7 * i + k],
                    recv_sem=recv_sems.at[7 * i + k], device_id=to, device_id_type=MESH)

            locs.append(pltpu.make_async_copy(ins[i], outs[i].at[slot(x, y, c)], local_sems.at[i]))
            first.append(copy(0, (x, y, c), sibling, src=ins[i]))
            first += [copy(1 + j, (x, y, c), (*chip, c), src=ins[i]) for j, chip in enumerate(chips)]
            passed.append([copy(4 + j, (*chip, c), sibling) for j, chip in enumerate(chips)])
            waits.append([copy(1 + j, (*chip, c), (x, y, c)) for j, chip in enumerate(chips)])
        for cp in locs + first:
            cp.start()
        for j in range(3):
            for i in range(n):
                waits[i][j].wait_recv()
                passed[i][j].start()
        for i in range(n):
            def arrival(k, block, i=i):
                dst = outs[i].at[slot(*block)]
                return pltpu.make_async_remote_copy(
                    src_ref=dst, dst_ref=dst, send_sem=send_sems.at[7 * i + k], recv_sem=recv_sems.at[7 * i + k],
                    device_id=sibling, device_id_type=MESH)

            arrival(0, (x, y, 1 - c)).wait_recv()
            for j, chip in enumerate(chips):
                arrival(4 + j, (*chip, 1 - c)).wait_recv()
        for cp in first + [p for ps in passed for p in ps]:
            cp.wait_send()
        for cp in locs:
            cp.wait()

    outs = pl.pallas_call(
        body, name=name, out_shape=[_sds((N_DEV,) + a.shape, a.dtype) for a in arrs], in_specs=[ANY] * n,
        out_specs=[ANY] * n,
        scratch_shapes=[pltpu.SemaphoreType.DMA((7 * n,)), pltpu.SemaphoreType.DMA((7 * n,)),
                        pltpu.SemaphoreType.DMA((n,))],
    )(*arrs)
    return list(outs)


HBM = pl.BlockSpec(memory_space=pltpu.HBM)
SEM = pl.BlockSpec(memory_space=pltpu.SEMAPHORE)
EFFECT = pltpu.SideEffectType.DATAFLOW_SIDE_EFFECTING


def _peer(k, chips=False):
    x, y, c = lax.axis_index("x"), lax.axis_index("y"), lax.axis_index("c")
    bits = (k + 1) << 1 if chips else k + 1
    px, py, pc = x ^ ((bits >> 2) & 1), y ^ ((bits >> 1) & 1), c ^ (bits & 1)
    if chips:
        return (px, py, pc), 2 * px + py, 2 * x + y
    return (px, py, pc), 4 * px + 2 * py + pc, 4 * x + 2 * y + c


def _exchange_start(arrs, gather, name, after=None, chips=False):
    n = len(arrs)
    n_peer, n_slot = (3, 4) if chips else (N_DEV - 1, N_DEV)
    modes = [gather] * n if isinstance(gather, bool) else list(gather)
    lands = [pltpu.with_memory_space_constraint(lax.empty(((n_slot,) + a.shape) if g else a.shape, a.dtype), pltpu.HBM)
             for a, g in zip(arrs, modes)]
    srcs = [pltpu.with_memory_space_constraint(a, pltpu.HBM) for a in arrs]

    extra = [] if after is None else [after]

    def body(*refs):
        ins, zones = refs[:n], refs[n:2 * n]
        send_sems, recv_sems, local_sems = refs[2 * n + len(extra):2 * n + len(extra) + 3]
        token = refs[-1]
        for i in range(n):
            gather = modes[i]
            for k in range(n_peer):
                dev, peer, me = _peer(k, chips)
                sem = i * n_peer + k
                pltpu.make_async_remote_copy(
                    src_ref=ins[i] if gather else ins[i].at[peer], dst_ref=zones[i].at[me],
                    send_sem=send_sems.at[sem], recv_sem=recv_sems.at[sem], device_id=dev, device_id_type=MESH).start()
            _, _, me = _peer(0, chips)
            pltpu.make_async_copy(ins[i] if gather else ins[i].at[me], zones[i].at[me], local_sems.at[i]).start()
        token[...] = jnp.zeros_like(token)

    nsem = n * n_peer
    outs = pl.pallas_call(
        body, name=name,
        out_shape=[pltpu.SemaphoreType.DMA((nsem,)), pltpu.SemaphoreType.DMA((nsem,)), pltpu.SemaphoreType.DMA((n,))]
        + [pltpu.HBM(a.shape, a.dtype) for a in srcs] + [pltpu.HBM(z.shape, z.dtype) for z in lands]
        + [_sds((8, LANE))],
        in_specs=[HBM] * (2 * n) + [ANY] * len(extra),
        out_specs=[SEM, SEM, SEM] + [HBM] * (2 * n) + [pl.BlockSpec(memory_space=pltpu.VMEM)],
        input_output_aliases={i: 3 + i for i in range(2 * n)},
        compiler_params=pltpu.CompilerParams(has_side_effects=EFFECT),
    )(*srcs, *lands, *extra)
    return {"n": n, "gather": modes, "chips": chips, "sems": outs[:3], "srcs": outs[3:3 + n],
            "lands": outs[3 + n:3 + 2 * n], "token": outs[-1]}


def _exchange_wait(st, after, name, which=None):
    modes, chips = st["gather"], st["chips"]
    afters = list(after) if isinstance(after, (list, tuple)) else [after]
    which = list(range(st["n"])) if which is None else which
    n = len(which)
    n_peer = 3 if chips else N_DEV - 1
    srcs, lands = [st["srcs"][i] for i in which], [st["lands"][i] for i in which]

    def body(*refs):
        ins, zones = refs[:n], refs[n:2 * n]
        send_sems, recv_sems, local_sems = refs[2 * n:2 * n + 3]
        for j, i in enumerate(which):
            gather = modes[i]
            for k in range(n_peer):
                dev, peer, me = _peer(k, chips)
                sem = i * n_peer + k
                src = ins[j] if gather else ins[j].at[peer]
                cp = pltpu.make_async_remote_copy(
                    src_ref=src, dst_ref=zones[j].at[peer], send_sem=send_sems.at[sem], recv_sem=recv_sems.at[sem],
                    device_id=dev, device_id_type=MESH)
                cp.wait_send()
                cp.wait_recv()
            _, _, me = _peer(0, chips)
            pltpu.make_async_copy(ins[j] if gather else ins[j].at[me], zones[j].at[me], local_sems.at[i]).wait()

    outs = pl.pallas_call(
        body, name=name,
        out_shape=[pltpu.HBM(a.shape, a.dtype) for a in srcs] + [pltpu.HBM(z.shape, z.dtype) for z in lands],
        in_specs=[HBM] * (2 * n) + [SEM, SEM, SEM] + [ANY] * len(afters), out_specs=[HBM] * (2 * n),
        input_output_aliases={i: i for i in range(2 * n)},
        compiler_params=pltpu.CompilerParams(has_side_effects=EFFECT),
    )(*srcs, *lands, *st["sems"], *afters)
    return list(outs[n:])


def _pair_reduce(arrs, name, after):
    n = len(arrs)

    def body(*refs):
        ins, refs = refs[:n], refs[n + 1:]
        outs, got, mine = refs[:n], refs[n:2 * n], refs[2 * n:3 * n]
        send_sems, recv_sems, local_sems = refs[3 * n:]
        x, y, c = lax.axis_index("x"), lax.axis_index("y"), lax.axis_index("c")
        sends, locs = [], []
        for i in range(n):
            for q in range(4):
                sem = 4 * i + q
                sends.append(pltpu.make_async_remote_copy(
                    src_ref=ins[i].at[2 * q + 1 - c], dst_ref=got[i].at[q], send_sem=send_sems.at[sem],
                    recv_sem=recv_sems.at[sem], device_id=(x, y, 1 - c), device_id_type=MESH))
                locs.append(pltpu.make_async_copy(ins[i].at[2 * q + c], mine[i].at[q], local_sems.at[sem]))
        for cp in locs + sends:
            cp.start()
        for cp in sends:
            cp.wait_recv()
        for cp in locs:
            cp.wait()
        for i in range(n):
            outs[i][...] = (mine[i][...].astype(F32) + got[i][...].astype(F32)).astype(BF)
        for cp in sends:
            cp.wait_send()

    half = [(4,) + a.shape[1:] for a in arrs]
    outs = pl.pallas_call(
        body, name=name, out_shape=[_sds(h, BF) for h in half], in_specs=[ANY] * (n + 1),
        out_specs=[pl.BlockSpec(memory_space=pltpu.VMEM)] * n,
        scratch_shapes=[pltpu.VMEM(h, BF) for h in half] * 2
        + [pltpu.SemaphoreType.DMA((4 * n,)), pltpu.SemaphoreType.DMA((4 * n,)), pltpu.SemaphoreType.DMA((4 * n,))],
        compiler_params=_cp(),
    )(*arrs, after)
    return list(outs)


def _mod_fwd(crows, w_ada, b_blk):
    def body(c_ref, w_ref, b_ref, o_ref):
        o_ref[...] = mm(silu(c_ref[...]), w_ref[...]) + b_ref[...]

    return pl.pallas_call(body, name="mod_fwd", out_shape=_sds((24, 768)), compiler_params=_cp())(crows, w_ada, b_blk)


def _mod_bwd(crows, w_ada, dmod_blk, dmodc_blk):
    def body(c_ref, w_ref, d_ref, dc_ref, gw_ref, gc_ref, gb_ref):
        cr = c_ref[...]
        dc = dc_ref[0:1, :]
        for p in range(1, N_DEV):
            dc = dc + dc_ref[p:p + 1, :]
        row = lax.broadcasted_iota(jnp.int32, (24, 1), 0)
        gw_ref[...] = mm_tn(silu(cr), jnp.where(row == 16, dc, d_ref[...]))
        cc = cr[16:17, :]
        sg = jax.nn.sigmoid(cc)
        part = mm_nt(jnp.broadcast_to(dc, (8, 768)), w_ref[...])
        gc_ref[...] = part * (sg * (1.0 + cc * (1.0 - sg)))
        gb_ref[...] = jnp.broadcast_to(jnp.sum(d_ref[...], axis=0, keepdims=True) + dc, (8, 768))

    return pl.pallas_call(
        body, name="mod_bwd", out_shape=[_sds((D_MODEL, 768)), _sds((8, D_MODEL)), _sds((8, 768))],
        compiler_params=_cp())(crows, w_ada, dmod_blk, dmodc_blk)


def _tab_specs(tk):
    return [pl.BlockSpec((tk, LANE), lambda i, t: (t, 0))] * 2


def _k1_fwd(x, mod, g_attn, g_q, g_kv, ws, tabs, kv_all, is_ctx):
    b, l, _ = x.shape
    tk = CTX_LEN if is_ctx else TOK
    nt = l // tk
    n_f32 = 2 if is_ctx else 4

    def body(x_ref, mod_ref, ga_ref, gq_ref, gk_ref, wa_ref, wb_ref, wq_ref, wk_ref, cs_ref, sn_ref, *rest):
        outs = rest if is_ctx else rest[2:]
        res = k1_tile(x_ref[...], mod_ref[0:1, :], mod_ref[1:2, :], ga_ref[...], gq_ref[...], gk_ref[...],
                      (wa_ref[...], wb_ref[...], wq_ref[...], wk_ref[...]), (None,) * 4,
                      (cs_ref[...], sn_ref[...]), is_ctx)
        for o_ref, r in zip(outs, res):
            o_ref[...] = r.astype(o_ref.dtype)

    tok = lambda w, off=0: pl.BlockSpec((None, tk, w), lambda i, t: (i, t + off, 0))
    mod_spec = pl.BlockSpec((None, 8, D_MODEL), (lambda i, t: (0, 0, 0)) if is_ctx else (lambda i, t: (i, 0, 0)))
    kv_off = SEQ // tk if is_ctx else 0
    in_specs = ([tok(D_MODEL), mod_spec, _full((1, D_MODEL)), _full((1, 384)), _full((1, 256))]
                + [_full(s) for s in W_SHAPES] + _tab_specs(tk))
    args = [x, mod, g_attn, g_q, g_kv, *ws, *tabs]
    out_specs = [tok(512)] * n_f32 + ([] if is_ctx else [tok(1024)]) + [tok(1024, kv_off), tok(512, kv_off)]
    out_shape = ([_sds((b, l, 512))] * n_f32 + ([] if is_ctx else [_sds((b, l, 1024), BF)])
                 + [_sds((b, KV_LEN, 1024), BF), _sds((b, KV_LEN, 512), BF)])
    aliases = {}
    if not is_ctx:
        aliases = {len(args): n_f32 + 1, len(args) + 1: n_f32 + 2}
        in_specs += [ANY, ANY]
        args += list(kv_all)
    return pl.pallas_call(
        body, name="k1_fwd_ctx" if is_ctx else "k1_fwd", grid=(b, nt), in_specs=in_specs, out_specs=out_specs,
        out_shape=out_shape, input_output_aliases=aliases, compiler_params=_cp((ARB, ARB)),
    )(*args)


N_ACC = 7


def _k1_bwd(x, ctx, mod, mod_c, g_attn, g_q, g_kv, ws, tabs, cts, cts_c, dx_res, adam):
    b, l, _ = x.shape
    tk = TOK_B
    nt = l // tk
    flat = [[a for group in c for a in group] for c in (cts, cts_c)]
    sizes = [[len(g) for g in c] for c in (cts, cts_c)]
    acc_shapes = W_SHAPES + [(1, D_MODEL), (1, 384), (1, 256)]
    n_steps = b * (nt + 1)
    adam_nblk = [max(d for d in range(1, n_steps + 1) if w.shape[0] % d == 0 and (w.shape[0] // d) % 8 == 0)
                 for w, _, _, _ in adam]

    def body(*refs):
        it = iter(refs)
        x_ref, c_ref, mod_ref, modc_ref, ga_ref, gq_ref, gk_ref = [next(it) for _ in range(7)]
        w_hbm = [next(it) for _ in range(4)]
        tab_refs = [next(it) for _ in range(2)]
        ct_refs = [[next(it) for _ in f] for f in flat]
        res_ref = next(it)
        adam_in = [[next(it) for _ in range(4)] for _ in adam]
        gx_ref = next(it)
        out_hbm = [next(it) for _ in range(N_ACC)]
        dmod_ref, dmodc_ref = next(it), next(it)
        adam_out = [[next(it) for _ in range(4)] for _ in adam]
        w_vmem = [next(it) for _ in range(4)]
        accs = [next(it) for _ in range(N_ACC)]
        sem = next(it)
        i, t = pl.program_id(0), pl.program_id(1)
        first = jnp.logical_and(i == 0, t == 0)

        @pl.when(first)
        def _():
            for src, dst in zip(w_hbm, w_vmem):
                pltpu.sync_copy(src, dst)
            for k in range(N_ACC):
                accs[k][...] = jnp.zeros(acc_shapes[k], F32)

        def tile(is_ctx):
            which = 1 if is_ctx else 0
            ct_vals, pos = [], 0
            for gsz in sizes[which]:
                v = ct_refs[which][pos][...].astype(F32)
                for r in ct_refs[which][pos + 1:pos + gsz]:
                    v = v + r[...]
                ct_vals.append(v)
                pos += gsz
            wv = tuple(r[...] for r in w_vmem)
            tv = tuple(r[...] for r in tab_refs)
            m_ref = modc_ref if is_ctx else mod_ref

            def f(xv, sh, sc, ga, gq, gk, *probes):
                return k1_tile(xv, sh, sc, ga, gq, gk, wv, probes, tv, is_ctx)

            probes = [jnp.zeros(s, F32) for s in W_SHAPES]
            xin = c_ref[...] if is_ctx else x_ref[...]
            _, vjp = jax.vjp(f, xin, m_ref[0:1, :], m_ref[1:2, :], ga_ref[...], gq_ref[...], gk_ref[...], *probes)
            dx, dsh, dsc, dga, dgq, dgk, dwa, dwb, dwq, dwk = vjp(tuple(ct_vals))
            for ref, val in zip(accs, (dwa, dwb, dwq, dwk, dga, dgq, dgk)):
                ref[...] += val
            return dx, dsh, dsc

        def adam_block(ins, outs, nblk):
            @pl.when(i * (nt + 1) + t < nblk)
            def _():
                w_ref, m_ref, v_ref, p_ref = ins
                for o_ref, val in zip(outs, adam_update(w_ref[...], m_ref[...], v_ref[...], p_ref)):
                    o_ref[...] = val

        for ins, outs, nblk in zip(adam_in, adam_out, adam_nblk):
            adam_block(ins, outs, nblk)

        @pl.when(t == 0)
        def _():
            _, dsh, dsc = tile(True)
            _acc(dmodc_ref.at[0:1, :], dsh, i == 0)
            _acc(dmodc_ref.at[1:2, :], dsc, i == 0)

            @pl.when(i == 0)
            def _():
                dmodc_ref[2:8, :] = jnp.zeros((6, D_MODEL), F32)

        @pl.when(t > 0)
        def _():
            dx, dsh, dsc = tile(False)
            gx_ref[...] = dx + res_ref[...]
            _acc(dmod_ref.at[0:1, :], dsh, t == 1)
            _acc(dmod_ref.at[1:2, :], dsc, t == 1)

            @pl.when(t == 1)
            def _():
                dmod_ref[2:8, :] = jnp.zeros((6, D_MODEL), F32)

        @pl.when(jnp.logical_and(i == b - 1, t == nt))
        def _():
            for k in range(4):
                w_vmem[k][...] = accs[k][...].astype(BF)
            cps = [pltpu.make_async_copy(w_vmem[k] if k < 4 else accs[k], out_hbm[k], sem.at[k]) for k in range(N_ACC)]
            for cp in cps:
                cp.start()
            for cp in cps:
                cp.wait()

    lat = lambda w, off=0: pl.BlockSpec((None, tk, w), lambda i, t: (i, jnp.maximum(t - 1, 0) + off, 0))
    con = lambda w, off=0: pl.BlockSpec((None, tk, w), lambda i, t: (i, off, 0))
    mod_spec = pl.BlockSpec((None, 8, D_MODEL), lambda i, t: (i, 0, 0))
    modc_spec = pl.BlockSpec((None, 8, D_MODEL), lambda i, t: (0, 0, 0))
    tab_spec = pl.BlockSpec((tk, LANE), lambda i, t: (jnp.maximum(t - 1, 0), 0))
    in_specs = ([lat(D_MODEL), con(D_MODEL), mod_spec, modc_spec, _full((1, D_MODEL)), _full((1, 384)), _full((1, 256))]
                + [ANY] * 4 + [tab_spec] * 2)
    args = [x, ctx, mod, mod_c, g_attn, g_q, g_kv, *ws, *tabs]
    for a, off in flat[0]:
        in_specs.append(lat(a.shape[-1], off // tk))
        args.append(a)
    for a, off in flat[1]:
        in_specs.append(con(a.shape[-1], off // tk))
        args.append(a)
    in_specs.append(lat(D_MODEL))
    args.append(dx_res)
    out_shape = ([_sds((b, l, D_MODEL))] + [_sds(s, BF) for s in W_SHAPES] + [_sds(s) for s in acc_shapes[4:]]
                 + [_sds((b, 8, D_MODEL)), _sds((1, 8, D_MODEL))])
    out_specs = [lat(D_MODEL)] + [ANY] * N_ACC + [mod_spec, modc_spec]
    for (w, m, v, pieces), nblk in zip(adam, adam_nblk):
        r, c = w.shape
        blk = lambda i, t, nblk=nblk: jnp.minimum(i * (nt + 1) + t, nblk - 1)
        row_spec = pl.BlockSpec((r // nblk, c), lambda i, t, blk=blk: (blk(i, t), 0))
        in_specs += [row_spec] * 3 + [pl.BlockSpec((pieces.shape[0], r // nblk, c), lambda i, t, blk=blk: (0, blk(i, t), 0))]
        args += [w, m, v, pieces]
        out_specs += [row_spec] * 4
        out_shape += [_sds((r, c))] * 4
    outs = pl.pallas_call(
        body, name="k1_bwd", grid=(b, nt + 1), in_specs=in_specs, out_specs=out_specs, out_shape=out_shape,
        scratch_shapes=[pltpu.VMEM(s, BF) for s in W_SHAPES] + [pltpu.VMEM(s, F32) for s in acc_shapes]
        + [pltpu.SemaphoreType.DMA((N_ACC,))],
        compiler_params=_cp((ARB, ARB), vmem=VMEM_LIMIT_K1_BWD),
    )(*args)
    rest = outs[3 + N_ACC:]
    return (outs[0], list(outs[1:1 + N_ACC]), outs[1 + N_ACC], outs[2 + N_ACC],
            [rest[4 * k:4 * k + 4] for k in range(len(adam))])


def _chunk_spec(rev):
    if rev:
        return pl.BlockSpec((None, RET_CHUNK, 512), lambda i, n: (i, N_CHUNK - 1 - n, 0))
    return pl.BlockSpec((None, RET_CHUNK, 512), lambda i, n: (i, n, 0))


def _state_spec(rev):
    if rev:
        return pl.BlockSpec((None, N_HEADS, None, LANE, LANE), lambda i, n: (i, 0, N_CHUNK - 1 - n, 0, 0))
    return pl.BlockSpec((None, N_HEADS, None, LANE, LANE), lambda i, n: (i, 0, n, 0, 0))


_CTX_SPEC = pl.BlockSpec((None, CTX_LEN, 512), lambda i, n: (i, 0, 0))
_DEC_SPEC = pl.BlockSpec((N_HEADS, 1, 1), lambda i, n: (0, 0, 0))


def _k2_fwd(rq, rk, rv, rkc, rvc, dec_f, dec_b):
    b = rq.shape[0]

    def body(qf, kf, vf, qb, kb, vb, kc, vc, df, db, of_ref, ob_ref, sf_out, sb_out, sf, sb, masks, rows):
        n = pl.program_id(1)
        first = jnp.logical_and(pl.program_id(0) == 0, n == 0)
        for h, sl in enumerate(_HEAD_SL):
            lgf, lgb = log_sigmoid(df[h]), log_sigmoid(db[h])

            @pl.when(first)
            def _():
                for d, (lg, rev) in enumerate(((lgf, False), (lgb, True))):
                    masks[2 * h + d] = decay_mask(lg, rev)
                    for j, w in enumerate(decay_rows(lg, rev)):
                        rows[2 * h + d, j] = jnp.broadcast_to(w, (RET_CHUNK, LANE))

            @pl.when(n == 0)
            def _():
                sf[h] = ctx_state(kc[:, sl], vc[:, sl], lgf, False)
                sb[h] = ctx_state(kc[:, sl], vc[:, sl], lgb, True)

            sf_out[h] = sf[h]
            sb_out[h] = sb[h]
            o, s = ret_chunk(qf[:, sl], kf[:, sl], vf[:, sl], sf[h], lgf, False,
                             (masks[2 * h], rows[2 * h, 0], rows[2 * h, 1]))
            of_ref[:, sl] = o
            sf[h] = s
            o, s = ret_chunk(qb[:, sl], kb[:, sl], vb[:, sl], sb[h], lgb, True,
                             (masks[2 * h + 1], rows[2 * h + 1, 0], rows[2 * h + 1, 1]))
            ob_ref[:, sl] = o
            sb[h] = s

    l = rq.shape[1]
    return pl.pallas_call(
        body, name="k2_fwd", grid=(b, N_CHUNK),
        in_specs=[_chunk_spec(False)] * 3 + [_chunk_spec(True)] * 3 + [_CTX_SPEC, _CTX_SPEC, _DEC_SPEC, _DEC_SPEC],
        out_specs=[_chunk_spec(False), _chunk_spec(True), _state_spec(False), _state_spec(True)],
        out_shape=[_sds((b, l, 512)), _sds((b, l, 512)), _sds((b, N_HEADS, N_CHUNK, LANE, LANE)),
                   _sds((b, N_HEADS, N_CHUNK, LANE, LANE))],
        scratch_shapes=[pltpu.VMEM((N_HEADS, LANE, LANE), F32), pltpu.VMEM((N_HEADS, LANE, LANE), F32),
                        pltpu.VMEM((2 * N_HEADS, RET_CHUNK, RET_CHUNK), F32),
                        pltpu.VMEM((2 * N_HEADS, 2, RET_CHUNK, LANE), F32)],
        compiler_params=_cp((ARB, ARB)),
    )(rq, rk, rv, rq, rk, rv, rkc, rvc, dec_f, dec_b)


def _k2_bwd(rq, rk, rv, do, sf_prev, sb_prev, rkc, rvc, dec_f, dec_b):
    b, l, _ = rq.shape

    def body(qf, kf, vf, gf, spf, qb, kb, vb, gb, spb, kc, vc, df, db,
             dqf, dkf, dvf, dqb, dkb, dvb, dkc, dvc, ddf, ddb, dsf, dsb, masks):
        n = pl.program_id(1)

        @pl.when(jnp.logical_and(pl.program_id(0) == 0, n == 0))
        def _():
            for h in range(N_HEADS):
                masks[2 * h] = decay_mask(log_sigmoid(df[h]), False)
                masks[2 * h + 1] = decay_mask(log_sigmoid(db[h]), True)

        @pl.when(n == 0)
        def _():
            dsf[...] = jnp.zeros((N_HEADS, LANE, LANE), F32)
            dsb[...] = jnp.zeros((N_HEADS, LANE, LANE), F32)

        def one(h, sl, q, k, v, g, sp, dec, ds, dq, dk, dv, dd, rev):
            mask = masks[2 * h + int(rev)]

            def f(qv, kv_, vv, sv, dcy):
                lg = log_sigmoid(dcy)
                return ret_chunk(qv, kv_, vv, sv, lg, rev, (kept_decay_mask(lg, mask, rev), *decay_rows(lg, rev)))

            _, vjp = jax.vjp(f, q[:, sl], k[:, sl], v[:, sl], sp[h], dec[h])
            gq, gk, gv, gs, gd = vjp((g[:, sl], ds[h]))
            dq[:, sl] = gq
            dk[:, sl] = gk
            dv[:, sl] = gv
            ds[h] = gs
            _acc(dd.at[h], jnp.broadcast_to(gd, (8, LANE)), n == 0)

        for h, sl in enumerate(_HEAD_SL):
            one(h, sl, qf, kf, vf, gf, spf, df, dsf, dqf, dkf, dvf, ddf, False)
            one(h, sl, qb, kb, vb, gb, spb, db, dsb, dqb, dkb, dvb, ddb, True)

        @pl.when(n == N_CHUNK - 1)
        def _():
            def f(kcv, vcv, dcy, rev):
                return ctx_state(kcv, vcv, log_sigmoid(dcy), rev)

            for h, sl in enumerate(_HEAD_SL):
                _, vjp_f = jax.vjp(functools.partial(f, rev=False), kc[:, sl], vc[:, sl], df[h])
                gk_f, gv_f, gd_f = vjp_f(dsf[h])
                _, vjp_b = jax.vjp(functools.partial(f, rev=True), kc[:, sl], vc[:, sl], db[h])
                gk_b, gv_b, gd_b = vjp_b(dsb[h])
                dkc[:, sl] = gk_f + gk_b
                dvc[:, sl] = gv_f + gv_b
                ddf[h] += jnp.broadcast_to(gd_f, (8, LANE))
                ddb[h] += jnp.broadcast_to(gd_b, (8, LANE))

    dd_spec = pl.BlockSpec((None, N_HEADS, 8, LANE), lambda i, n: (i, 0, 0, 0))
    return pl.pallas_call(
        body, name="k2_bwd", grid=(b, N_CHUNK),
        in_specs=[_chunk_spec(True)] * 4 + [_state_spec(True)] + [_chunk_spec(False)] * 4 + [_state_spec(False)]
        + [_CTX_SPEC, _CTX_SPEC, _DEC_SPEC, _DEC_SPEC],
        out_specs=[_chunk_spec(True)] * 3 + [_chunk_spec(False)] * 3 + [_CTX_SPEC, _CTX_SPEC, dd_spec, dd_spec],
        out_shape=[_sds((b, l, 512))] * 6 + [_sds((b, CTX_LEN, 512))] * 2 + [_sds((b, N_HEADS, 8, LANE))] * 2,
        scratch_shapes=[pltpu.VMEM((N_HEADS, LANE, LANE), F32), pltpu.VMEM((N_HEADS, LANE, LANE), F32),
                        pltpu.VMEM((2 * N_HEADS, RET_CHUNK, RET_CHUNK), F32)],
        compiler_params=_cp((ARB, ARB)),
    )(rq, rk, rv, do, sf_prev, rq, rk, rv, do, sb_prev, rkc, rvc, dec_f, dec_b)


TQ = 1024
TQ_F = 512
QK_W = 2 * LANE
N_QP = 2
_Q_PARTS = [slice(i * TQ_F // N_QP, (i + 1) * TQ_F // N_QP) for i in range(N_QP)]


SM_SCALE = 1.0 / math.sqrt(192.0)


def _k3_specs(tq):
    qs = lambda w: pl.BlockSpec((None, tq, w), lambda i, h, t: (i, t, h))
    ks = lambda w: pl.BlockSpec((None, KV_LEN, w), lambda i, h, t: (i, 0, h))
    return qs, ks


def _k3_fwd(q, k, v):
    b, l, _ = q.shape

    def body(q_ref, k_ref, v_ref, o_ref, lse_ref):
        kv_, vv = k_ref[...], v_ref[...]
        for r in _Q_PARTS:
            s = _dot(q_ref[r, :], kv_, 1, 1)
            m = jnp.max(s, axis=-1, keepdims=True)
            e = jnp.exp2((s - m) * (SM_SCALE * LOG2_E))
            tot = jnp.sum(e, axis=-1, keepdims=True)
            o_ref[r, :] = _dot(e, vv, 1, 0) * (1.0 / tot)
            lse_ref[r, :] = jnp.broadcast_to(m * SM_SCALE + jnp.log(tot), (TQ_F // N_QP, LANE))

    qs, ks = _k3_specs(TQ_F)
    return pl.pallas_call(
        body, name="k3_fwd", grid=(b, N_HEADS, l // TQ_F), in_specs=[qs(QK_W), ks(QK_W), ks(LANE)],
        out_specs=[qs(LANE), qs(LANE)], out_shape=[_sds((b, l, N_HEADS * LANE))] * 2,
        compiler_params=_cp((ARB, ARB, ARB)),
    )(q, k, v)


def _k3_bwd(q, k, v, o, lse, dy, after):
    b, l, _ = q.shape

    def body(q_ref, k_ref, v_ref, o_ref, lse_ref, dy_ref, after_ref, dq_ref, dk_ref, dv_ref):
        t0 = pl.program_id(2) == 0
        kv_, vv = k_ref[...], v_ref[...]
        qv, dyv = q_ref[...], dy_ref[...]
        g = dyv.astype(BF)
        lse_col = jnp.max(lse_ref[...], axis=-1, keepdims=True)
        delta = jnp.sum(dyv * o_ref[...], axis=-1, keepdims=True)
        p = jnp.exp2(_dot(qv, kv_, 1, 1) * (SM_SCALE * LOG2_E) - lse_col * LOG2_E)
        ds = (p * (_dot(g, vv, 1, 1) - delta) * SM_SCALE).astype(BF)
        _acc(dv_ref, _dot(p, g, 0, 0), t0)
        dq_ref[...] = _dot(ds, kv_, 1, 0)
        _acc(dk_ref, _dot(ds, qv, 0, 0), t0)

    qs, ks = _k3_specs(TQ)
    return pl.pallas_call(
        body, name="k3_bwd", grid=(b, N_HEADS, l // TQ),
        in_specs=[qs(QK_W), ks(QK_W), ks(LANE), qs(LANE), qs(LANE), qs(LANE), ANY],
        out_specs=[qs(QK_W), ks(QK_W), ks(LANE)],
        out_shape=[_sds((b, l, N_HEADS * QK_W)), _sds((b, KV_LEN, N_HEADS * QK_W)), _sds((b, KV_LEN, N_HEADS * LANE))],
        compiler_params=_cp((ARB, ARB, ARB)),
    )(q, k, v, o, lse, dy, after)


def _mod_rows(mod_ref, rows):
    return [mod_ref[r:r + 1, :] for r in rows]


def _k4a_fwd(x, o_f, o_b, rg, y_mla, g_ret, w_out, mod, g_ffn):
    b, l, _ = x.shape

    def body(x_ref, of_ref, ob_ref, rg_ref, ym_ref, gr_ref, wo_ref, mod_ref, gf_ref, xm_ref, h2_ref):
        gt_a, sh_f, sc_f = _mod_rows(mod_ref, (2, 3, 4))
        x_mid, h2 = k4a_tile(x_ref[...], of_ref[...], ob_ref[...], rg_ref[...], ym_ref[...], gr_ref[...], gt_a,
                             gf_ref[...], sh_f, sc_f, wo_ref[...], None)
        xm_ref[...] = x_mid
        h2_ref[...] = h2.astype(BF)

    tok = lambda w: pl.BlockSpec((None, TOK, w), lambda i, t: (i, t, 0))
    mod_spec = pl.BlockSpec((None, 8, D_MODEL), lambda i, t: (i, 0, 0))
    return pl.pallas_call(
        body, name="k4a_fwd", grid=(b, l // TOK),
        in_specs=[tok(D_MODEL), tok(512), tok(512), tok(512), tok(512), _full((1, 512)), _full((D_MODEL, D_MODEL)),
                  mod_spec, _full((1, D_MODEL))],
        out_specs=[tok(D_MODEL), tok(D_MODEL)], out_shape=[_sds((b, l, D_MODEL)), _sds((b, l, D_MODEL), BF)],
        compiler_params=_cp((ARB, ARB)),
    )(x, o_f, o_b, rg, y_mla, g_ret, w_out, mod, g_ffn)


TOK_M = 512
TOK_D = 2048
HALF_FF = D_FF // 2


def _k4b_mlp_loss(h2, w1t, w2, x_mid, mod, g_final, tgt):
    b, l, _ = h2.shape
    nt = l // TOK_M

    def body(h2_ref, w1_hbm, w2_hbm, xm_ref, mod_ref, gfin_ref, tgt_ref, dxm_ref, dmlp_ref, r_ref, loss_ref, dgt_ref,
             dgfin_ref, w1_v, w2_v):
        i, t = pl.program_id(0), pl.program_id(1)
        first = jnp.logical_and(i == 0, t == 0)

        @pl.when(first)
        def _():
            pltpu.sync_copy(w1_hbm, w1_v)
            pltpu.sync_copy(w2_hbm, w2_v)

        h2v = h2_ref[...]
        mlp = None
        for half in range(2):
            rows = slice(half * HALF_FF, (half + 1) * HALF_FF)
            r = jnp.maximum(_dot(h2v, w1_v[rows, :], 1, 1), 0.0)
            r_ref[:, rows] = r.astype(BF)
            part = _dot(jnp.square(r), w2_v[rows, :], 1, 0)
            mlp = part if mlp is None else mlp + part
        (gt_f,) = _mod_rows(mod_ref, (5,))
        loss, vjp = jax.vjp(k4c_tile, xm_ref[...], mlp, gt_f, gfin_ref[...], tgt_ref[...])
        dxm, dmlp, dgt, dgfin, _ = vjp(jnp.ones((1, 1), F32))
        dxm_ref[...] = dxm
        dmlp_ref[...] = dmlp.astype(BF)
        _acc(loss_ref, jnp.broadcast_to(loss, (8, LANE)), first)
        _acc(dgfin_ref, dgfin, first)
        _acc(dgt_ref, dgt, t == 0)

    tok = lambda w: pl.BlockSpec((None, TOK_M, w), lambda i, t: (i, t, 0))
    return pl.pallas_call(
        body, name="k4b_mlp_loss", grid=(b, nt),
        in_specs=[tok(D_MODEL), ANY, ANY, tok(D_MODEL), pl.BlockSpec((None, 8, D_MODEL), lambda i, t: (i, 0, 0)),
                  _full((1, D_MODEL)), tok(D_MODEL)],
        out_specs=[tok(D_MODEL), tok(D_MODEL), tok(D_FF), _full((8, LANE)),
                   pl.BlockSpec((None, 1, D_MODEL), lambda i, t: (i, 0, 0)), _full((1, D_MODEL))],
        out_shape=[_sds((b, l, D_MODEL)), _sds((b, l, D_MODEL), BF), _sds((b, l, D_FF), BF), _sds((8, LANE)),
                   _sds((b, 1, D_MODEL)), _sds((1, D_MODEL))],
        scratch_shapes=[pltpu.VMEM((D_FF, D_MODEL), BF), pltpu.VMEM((D_FF, D_MODEL), BF)],
        compiler_params=_cp((ARB, ARB)),
    )(h2, w1t, w2, x_mid, mod, g_final, tgt)


def _k4d_mlp_bwd(h2, dmlp, r, w2):
    b, l, _ = h2.shape
    nt = l // TOK_D

    def body(h2_ref, dm_ref, r_ref, w2_ref, da_ref, dw1_ref, dw2_ref, acc1, acc2):
        i, t = pl.program_id(1), pl.program_id(2)
        first = jnp.logical_and(i == 0, t == 0)
        rv = r_ref[...].astype(F32)
        dm = dm_ref[...]
        da = (_dot(dm, w2_ref[...], 1, 1) * (2.0 * rv)).astype(BF)
        da_ref[...] = da
        _acc(acc2, _dot(jnp.square(rv), dm, 0, 0), first)
        _acc(acc1, _dot(h2_ref[...], da, 0, 0), first)

        @pl.when(jnp.logical_and(i == b - 1, t == nt - 1))
        def _():
            dw1_ref[...] = acc1[...].astype(BF)
            dw2_ref[...] = acc2[...].astype(BF)

    tok = lambda w: pl.BlockSpec((None, TOK_D, w), lambda j, i, t: (i, t, 0))
    col = pl.BlockSpec((None, TOK_D, FF_BLK), lambda j, i, t: (i, t, j))
    return pl.pallas_call(
        body, name="k4d_mlp_bwd", grid=(N_DEV, b, nt),
        in_specs=[tok(D_MODEL), tok(D_MODEL), col, pl.BlockSpec((None, FF_BLK, D_MODEL), lambda j, i, t: (j, 0, 0))],
        out_specs=[col, pl.BlockSpec((None, D_MODEL, FF_BLK), lambda j, i, t: (j, 0, 0)),
                   pl.BlockSpec((None, FF_BLK, D_MODEL), lambda j, i, t: (j, 0, 0))],
        out_shape=[_sds((b, l, D_FF), BF), _sds((N_DEV, D_MODEL, FF_BLK), BF), _sds((N_DEV, FF_BLK, D_MODEL), BF)],
        scratch_shapes=[pltpu.VMEM((D_MODEL, FF_BLK), F32), pltpu.VMEM((FF_BLK, D_MODEL), F32)],
        compiler_params=_cp((ARB, ARB, ARB)),
    )(h2, dmlp, r, w2)


def _k4e_bwd(x, o_f, o_b, rg, y_mla, g_ret, w_out, mod, g_ffn, dxm, da, w1t):
    b, l, _ = x.shape

    def body(x_ref, of_ref, ob_ref, rg_ref, ym_ref, gr_ref, wo_ref, mod_ref, gf_ref, dxm_ref, da_ref, w1_hbm,
             dx_ref, do_ref, drg_ref, dym_ref, dwo_ref, dgr_ref, dgf_ref, dmod_ref, w1_v, dwo_acc):
        i, t = pl.program_id(0), pl.program_id(1)
        first = jnp.logical_and(i == 0, t == 0)

        @pl.when(first)
        def _():
            pltpu.sync_copy(w1_hbm, w1_v)

        gt_a, sh_f, sc_f = _mod_rows(mod_ref, (2, 3, 4))
        wo = wo_ref[...]
        dh2 = _dot(da_ref[...], w1_v[...], 1, 0)

        def f(xv, ofv, rgv, ymv, grv, gta, gfv, shf, scf, p_out):
            return k4a_tile(xv, ofv, ob_ref[...], rgv, ymv, grv, gta, gfv, shf, scf, wo, p_out)

        _, vjp = jax.vjp(f, x_ref[...], of_ref[...], rg_ref[...], ym_ref[...], gr_ref[...], gt_a, gf_ref[...], sh_f,
                         sc_f, jnp.zeros((D_MODEL, D_MODEL), F32))
        dx, do, drg, dym, dgr, dgta, dgf, dshf, dscf, dwo = vjp((dxm_ref[...], dh2))
        dx_ref[...] = dx
        do_ref[...] = do
        drg_ref[...] = drg
        dym_ref[...] = dym
        _acc(dwo_acc, dwo, first)
        _acc(dgr_ref, dgr, first)
        _acc(dgf_ref, dgf, first)
        t0 = t == 0
        _acc(dmod_ref.at[2:3, :], dgta, t0)
        _acc(dmod_ref.at[3:4, :], dshf, t0)
        _acc(dmod_ref.at[4:5, :], dscf, t0)

        @pl.when(t0)
        def _():
            dmod_ref[0:2, :] = jnp.zeros((2, D_MODEL), F32)
            dmod_ref[5:8, :] = jnp.zeros((3, D_MODEL), F32)

        @pl.when(jnp.logical_and(i == b - 1, t == l // TOK_B - 1))
        def _():
            dwo_ref[...] = dwo_acc[...].astype(BF)

    tok = lambda w: pl.BlockSpec((None, TOK_B, w), lambda i, t: (i, t, 0))
    mod_spec = pl.BlockSpec((None, 8, D_MODEL), lambda i, t: (i, 0, 0))
    return pl.pallas_call(
        body, name="k4e_bwd", grid=(b, l // TOK_B),
        in_specs=[tok(D_MODEL), tok(512), tok(512), tok(512), tok(512), _full((1, 512)), _full((D_MODEL, D_MODEL)),
                  mod_spec, _full((1, D_MODEL)), tok(D_MODEL), tok(D_FF), ANY],
        out_specs=[tok(D_MODEL), tok(512), tok(512), tok(512), _full((D_MODEL, D_MODEL)), _full((1, 512)),
                   _full((1, D_MODEL)), mod_spec],
        out_shape=[_sds((b, l, D_MODEL)), _sds((b, l, 512)), _sds((b, l, 512)), _sds((b, l, 512)),
                   _sds((D_MODEL, D_MODEL), BF), _sds((1, 512)), _sds((1, D_MODEL)), _sds((b, 8, D_MODEL))],
        scratch_shapes=[pltpu.VMEM((D_FF, D_MODEL), BF), pltpu.VMEM((D_MODEL, D_MODEL), F32)],
        compiler_params=_cp((ARB, ARB)),
    )(x, o_f, o_b, rg, y_mla, g_ret, w_out, mod, g_ffn, dxm, da, w1t)


ADAM_BLOCK_BYTES = 32 * 1024 * 1024


def adam_update(wv, mv, vv, p_ref):
    g = p_ref[0].astype(F32)
    for k in range(1, p_ref.shape[0]):
        g = g + p_ref[k].astype(F32)
    mn = ADAM_B1 * mv + (1.0 - ADAM_B1) * g
    vn = ADAM_B2 * vv + (1.0 - ADAM_B2) * jnp.square(g)
    m_hat = mn / (1.0 - ADAM_B1 ** ADAM_STEP)
    v_hat = vn / (1.0 - ADAM_B2 ** ADAM_STEP)
    return g, -ADAM_LR * (m_hat / (jnp.sqrt(v_hat) + ADAM_EPS) + ADAM_WD * wv), mn, vn


def _adamw(w, m, v, pieces, name, after=None):
    r, c = w.shape
    npc = pieces.shape[0]
    per_row = c * (7 * 4 + npc * pieces.dtype.itemsize) * 2
    rb = max(d for d in range(8, r + 1, 8) if r % d == 0 and d * per_row <= ADAM_BLOCK_BYTES)

    def body(w_ref, m_ref, v_ref, p_ref, *rest):
        for o_ref, val in zip(rest[-4:], adam_update(w_ref[...], m_ref[...], v_ref[...], p_ref)):
            o_ref[...] = val

    blk = pl.BlockSpec((rb, c), lambda i: (i, 0))
    extra = [] if after is None else [after]
    return pl.pallas_call(
        body, name=name, grid=(r // rb,),
        in_specs=[blk, blk, blk, pl.BlockSpec((npc, rb, c), lambda i: (0, i, 0))] + [ANY] * len(extra),
        out_specs=[blk] * 4, out_shape=[_sds((r, c))] * 4, compiler_params=_cp((ARB,)),
    )(w, m, v, pieces, *extra)


def _pad_rot_rows(w, zero):
    k = w.shape[1]
    return lax.pad(w.reshape(-1, 2, 32, k), zero, ((0, 0, 0), (0, 0, 0), (0, 32, 0), (0, 0, 0))).reshape(-1, k)


def _cut_rot_rows(g):
    k = g.shape[1]
    return g.reshape(-1, 2, 64, k)[:, :, :32].reshape(-1, k)


def _w_in_pad(wt, zero):
    w_a = jnp.concatenate([_pad_rot_rows(wt[0:512], zero), wt[512:1536]], axis=0)
    w_b = jnp.concatenate([wt[1536:2176], _pad_rot_rows(wt[2176:2240], zero)], axis=0)
    return w_a, w_b


def _w_in_cut(g_a, g_b):
    return jnp.concatenate([_cut_rot_rows(g_a[0:1024]), g_a[1024:2048], g_b[0:640], _cut_rot_rows(g_b[640:768])], axis=0)


def _w_uq_pad(wt, zero):
    w = wt.reshape(N_HEADS, 192, 384)
    rot = _pad_rot_rows(w[:, 128:].reshape(N_HEADS * 64, 384), zero).reshape(N_HEADS, LANE, 384)
    return jnp.concatenate([w[:, :128], rot], axis=1).reshape(1024, 384)


def _w_uq_cut(g):
    g = g.reshape(N_HEADS, 256, 384)
    rot = _cut_rot_rows(g[:, 128:].reshape(N_HEADS * LANE, 384)).reshape(N_HEADS, 64, 384)
    return jnp.concatenate([g[:, :128], rot], axis=1).reshape(768, 384)


def _w_ukv_perm(wt):
    return jnp.transpose(wt.reshape(N_HEADS, 2, LANE, 256), (1, 0, 2, 3)).reshape(1024, 256)


def _w_ukv_unperm(g):
    return jnp.transpose(g.reshape(2, N_HEADS, LANE, 256), (1, 0, 2, 3)).reshape(1024, 256)


def _unshard_cols(g):
    return jnp.transpose(g, (1, 0, 2)).reshape(g.shape[1], N_DEV * g.shape[2])


def _rope_tables():
    rows = SEQ // GRID_W
    row = jnp.repeat(jnp.arange(rows, dtype=F32), GRID_W)
    col = jnp.tile(jnp.arange(GRID_W, dtype=F32), rows)
    freq = ROPE_BASE ** (-jnp.arange(16, dtype=F32) / 16)
    ang = jnp.concatenate([row[:, None] * freq, col[:, None] * freq], axis=-1)
    cos, sin = jnp.cos(ang), jnp.sin(ang)
    z = jnp.zeros((SEQ, 32), F32)
    return jnp.concatenate([cos, z, cos, z], axis=1), jnp.concatenate([-sin, z, sin, z], axis=1)


_PACKED = (("g_attn", 1024), ("g_ffn", 1024), ("ret_decay_fwd", 4), ("ret_decay_bwd", 4), ("g_ret", 512),
           ("g_q_lora", 384), ("g_kv_lora", 256), ("g_final", 1024))
_PACK_OFF = {}
_off = 0
for _name, _n in _PACKED:
    _PACK_OFF[_name] = _off
    _off += -(-_n // LANE) * LANE
PACK_W = _off


def _pack_small(vals):
    parts = []
    for name, n in _PACKED:
        a = vals[name].reshape(-1).astype(F32)
        parts.append(jnp.pad(a, (0, -(-n // LANE) * LANE - n)))
    return jnp.concatenate(parts).reshape(1, PACK_W)


def _adamw_small(params, packed, gcc, gb_ada):
    names = list(params)
    n_p = len(names)

    def body(*refs):
        p_ref, gcc_ref, gb_ref = refs[3 * n_p:3 * n_p + 3]
        outs = refs[3 * n_p + 3:]
        for k, name in enumerate(names):
            w_ref, m_ref, v_ref = refs[3 * k:3 * k + 3]
            n = w_ref.shape[1]
            if name == "b_ada":
                g = jnp.concatenate([gb_ref[d, 0:1, :] for d in range(N_DEV)], axis=-1)
            elif name == "c_ctx":
                g = gcc_ref[0, 0:1, :]
                for d in range(1, N_DEV):
                    g = g + gcc_ref[d, 0:1, :]
            else:
                off = _PACK_OFF[name]
                g = p_ref[0, :, off:off + n]
                for d in range(1, N_DEV):
                    g = g + p_ref[d, :, off:off + n]
            mn = ADAM_B1 * m_ref[...] + (1.0 - ADAM_B1) * g
            vn = ADAM_B2 * v_ref[...] + (1.0 - ADAM_B2) * jnp.square(g)
            m_hat = mn / (1.0 - ADAM_B1 ** ADAM_STEP)
            v_hat = vn / (1.0 - ADAM_B2 ** ADAM_STEP)
            outs[4 * k][...] = g
            outs[4 * k + 1][...] = -ADAM_LR * (m_hat / (jnp.sqrt(v_hat) + ADAM_EPS) + ADAM_WD * w_ref[...])
            outs[4 * k + 2][...] = mn
            outs[4 * k + 3][...] = vn

    args = [a for name in names for a in params[name]] + [packed, gcc, gb_ada]
    out_shape = [_sds(params[name][0].shape) for name in names for _ in range(4)]
    outs = pl.pallas_call(body, name="adamw_small", out_shape=out_shape, compiler_params=_cp())(*args)
    return {name: list(outs[4 * k:4 * k + 4]) for k, name in enumerate(names)}


def kernel(x, c, ctx, c_ctx, w_ada, b_ada, g_attn, g_ffn, w_in, ret_decay_fwd, ret_decay_bwd, g_ret, g_q_lora, w_uq, g_kv_lora, w_ukv, w_out, w_ff1, w_ff2, g_final, loss_target, m_c_ctx, m_w_ada, m_b_ada, m_g_attn, m_g_ffn, m_w_in, m_ret_decay_fwd, m_ret_decay_bwd, m_g_ret, m_g_q_lora, m_w_uq, m_g_kv_lora, m_w_ukv, m_w_out, m_w_ff1, m_w_ff2, m_g_final, v_c_ctx, v_w_ada, v_b_ada, v_g_attn, v_g_ffn, v_w_in, v_ret_decay_fwd, v_ret_decay_bwd, v_g_ret, v_g_q_lora, v_w_uq, v_g_kv_lora, v_w_ukv, v_w_out, v_w_ff1, v_w_ff2, v_g_final):
    me = 4 * lax.axis_index("x") + 2 * lax.axis_index("y") + lax.axis_index("c")
    nb = x.shape[0]

    c_pad = jnp.pad(c, ((0, 8 - nb), (0, 0)))
    c_all, g_in, g_uq, g_ukv = _gather_two_level(
        [c_pad, w_in[0].T.astype(BF), w_uq[0].T.astype(BF), w_ukv[0].T.astype(BF)], "gather_weights")

    crows = jnp.concatenate([c_all[:, :nb].reshape(N_DEV * nb, D_MODEL), c_ctx[None], jnp.zeros((7, D_MODEL), F32)])
    b_blk = lax.dynamic_slice(b_ada, (0, me * 768), (1, 768))
    st_f = _exchange_start([_mod_fwd(crows, w_ada[0], b_blk), w_out[0].astype(BF), w_ff1[0].T.astype(BF),
                            w_ff2[0].astype(BF)], True, "gather_fwd_start")
    zero = st_f["token"][0, 0].astype(BF)
    ws = (*_w_in_pad(g_in.reshape(2240, D_MODEL), zero), _w_uq_pad(g_uq.reshape(768, 384), zero),
          _w_ukv_perm(g_ukv.reshape(1024, 256)))
    (mod_g,) = _exchange_wait(st_f, ws, "gather_mod_wait", [0])
    mod_all = _unshard_cols(mod_g)
    mod_mine = lax.dynamic_slice(mod_all, (me * nb, 0), (nb, 6 * D_MODEL)).reshape(nb, 6, D_MODEL)
    mod = jnp.pad(mod_mine, ((0, 0), (0, 2), (0, 0)))
    mod_c = jnp.pad(mod_all[16].reshape(1, 6, D_MODEL), ((0, 0), (0, 2), (0, 0)))

    tabs = _rope_tables()
    dec_f = ret_decay_fwd.reshape(N_HEADS, 1, 1)
    dec_b = ret_decay_bwd.reshape(N_HEADS, 1, 1)

    rkc, rvc, k_ctx, v_ctx = _k1_fwd(ctx, mod_c, g_attn, g_q_lora, g_kv_lora, ws, tabs, None, True)
    rq, rk, rv, rg, q, k_all, v_all = _k1_fwd(x, mod, g_attn, g_q_lora, g_kv_lora, ws, tabs, (k_ctx, v_ctx), False)
    o_f, o_b, sf_prev, sb_prev = _k2_fwd(rq, rk, rv, rkc, rvc, dec_f, dec_b)
    y_mla, lse = _k3_fwd(q, k_all, v_all)
    (g_out,) = _exchange_wait(st_f, y_mla, "gather_wo_wait", [1])
    wo = g_out.reshape(D_MODEL, D_MODEL)
    x_mid, h2 = _k4a_fwd(x, o_f, o_b, rg, y_mla, g_ret, wo, mod, g_ffn)
    g_ff1t, g_ff2 = _exchange_wait(st_f, x_mid, "gather_ff_wait", [2, 3])
    w1t = g_ff1t.reshape(D_FF, D_MODEL)
    dxm, dmlp, relu_a, loss_acc, dgt_f, dg_final = _k4b_mlp_loss(h2, w1t, g_ff2.reshape(D_FF, D_MODEL), x_mid, mod,
                                                                 g_final.reshape(1, D_MODEL), loss_target)

    da, dw1, dw2 = _k4d_mlp_bwd(h2, dmlp, relu_a, g_ff2)
    dx_res, do, drg, dym, dwo, dg_ret, dg_ffn, dmod_a = _k4e_bwd(x, o_f, o_b, rg, y_mla, g_ret, wo, mod, g_ffn, dxm, da,
                                                                 w1t)
    st_s = _exchange_start([dw1, dw2, dwo.reshape(N_DEV, 128, D_MODEL)], False, "scatter_grads_start")
    dq, dk_all, dv_all = _k3_bwd(q, k_all, v_all, y_mla, lse, dym, st_s["token"])
    dqf, dkf, dvf, dqb, dkb, dvb, dkc, dvc, ddf, ddb = _k2_bwd(rq, rk, rv, do, sf_prev, sb_prev, rkc, rvc, dec_f, dec_b)
    cts = [[(dqf, 0), (dqb, 0)], [(dkf, 0), (dkb, 0)], [(dvf, 0), (dvb, 0)], [(drg, 0)], [(dq, 0)],
           [(dk_all, 0)], [(dv_all, 0)]]
    cts_c = [[(dkc, 0)], [(dvc, 0)], [(dk_all, SEQ)], [(dv_all, SEQ)]]
    p_ff1, p_ff2, p_wo = _exchange_wait(st_s, dqf, "scatter_grads_wait")
    riding = (("w_ff1", w_ff1, m_w_ff1, v_w_ff1, p_ff1), ("w_ff2", w_ff2, m_w_ff2, v_w_ff2, p_ff2),
              ("w_out", w_out, m_w_out, v_w_out, p_wo))
    grad_x, accs, dmod_1, dmod_c1, rode = _k1_bwd(x, ctx, mod, mod_c, g_attn, g_q_lora, g_kv_lora, ws, tabs, cts, cts_c,
                                                  dx_res, [(w[0], m[0], v[0], pcs) for _, w, m, v, pcs in riding])
    dwa, dwb, dwq, dwk, dg_attn, dg_q, dg_kv = accs

    dmod_loc = (dmod_a + dmod_1).at[:, 5, :].set(dgt_f[:, 0, :])[:, :6, :].reshape(nb, 6 * D_MODEL)
    dmod_ctx = dmod_c1[:, :6, :].reshape(1, 6 * D_MODEL)
    small = {"g_attn": dg_attn, "g_ffn": dg_ffn, "ret_decay_fwd": jnp.sum(ddf[:, :, 0, 0], axis=0),
             "ret_decay_bwd": jnp.sum(ddb[:, :, 0, 0], axis=0), "g_ret": dg_ret, "g_q_lora": dg_q, "g_kv_lora": dg_kv,
             "g_final": dg_final}
    extra = jnp.concatenate([dmod_loc, dmod_ctx, jnp.zeros((5, 6 * D_MODEL), F32)])
    ex_pieces = jnp.transpose(extra.reshape(8, N_DEV, 768), (1, 0, 2))
    st_sm = _exchange_start([_pack_small(small), ex_pieces, loss_acc], [True, False, True], "gather_small_start")
    chip_sums = _pair_reduce([_w_in_cut(dwa, dwb).reshape(N_DEV, 280, D_MODEL), _w_uq_cut(dwq).reshape(N_DEV, 96, 384),
                              _w_ukv_unperm(dwk).reshape(N_DEV, 128, 256)], "pair_reduce", st_sm["token"])
    sm_g, ex_g, loss_g = _exchange_wait(st_sm, chip_sums[0], "gather_small_wait")
    dmod_blk = jnp.concatenate([ex_g[:, :nb].reshape(N_DEV * nb, 768), jnp.zeros((8, 768), F32)])
    gw_ada, gcc_part, gb_part = _mod_bwd(crows, w_ada[0], dmod_blk, ex_g[:, nb])
    st_c = _exchange_start([gcc_part, gb_part], True, "gather_cc_start")
    st_r = _exchange_start(chip_sums, False, "scatter_rest_start", after=st_c["token"], chips=True)

    res = {name: [a[None] for a in outs] for (name, *_), outs in zip(riding, rode)}
    res["w_ada"] = [a[None] for a in _adamw(w_ada[0], m_w_ada[0], v_w_ada[0], gw_ada[None], "adamw_w_ada",
                                            after=st_r["token"])]
    behind = res["w_ada"][3]

    smalls = {"c_ctx": (c_ctx, m_c_ctx, v_c_ctx), "b_ada": (b_ada, m_b_ada, v_b_ada), "g_attn": (g_attn, m_g_attn, v_g_attn),
              "g_ffn": (g_ffn, m_g_ffn, v_g_ffn), "ret_decay_fwd": (ret_decay_fwd, m_ret_decay_fwd, v_ret_decay_fwd),
              "ret_decay_bwd": (ret_decay_bwd, m_ret_decay_bwd, v_ret_decay_bwd), "g_ret": (g_ret, m_g_ret, v_g_ret),
              "g_q_lora": (g_q_lora, m_g_q_lora, v_g_q_lora), "g_kv_lora": (g_kv_lora, m_g_kv_lora, v_g_kv_lora),
              "g_final": (g_final, m_g_final, v_g_final)}
    rows = {k: tuple(a.reshape(1, -1) for a in t) for k, t in smalls.items()}
    gcc_g, gb_g = _exchange_wait(st_c, behind, "gather_cc_wait")
    small_out = _adamw_small(rows, sm_g, gcc_g, gb_g)
    for name, outs in small_out.items():
        res[name] = [o.reshape(smalls[name][0].shape) for o in outs]

    pieces = _exchange_wait(st_r, small_out["g_final"][3], "scatter_rest_wait")
    for name, w, m, v, pcs in (("w_in", w_in, m_w_in, v_w_in, pieces[0]), ("w_uq", w_uq, m_w_uq, v_w_uq, pieces[1])):
        res[name] = [a.T[None] for a in _adamw(w[0].T, m[0].T, v[0].T, pcs, "adamw_" + name)]
    res["w_ukv"] = [a[None] for a in _adamw(w_ukv[0], m_w_ukv[0], v_w_ukv[0], jnp.transpose(pieces[2], (0, 2, 1)),
                                            "adamw_w_ukv")]

    loss = loss_g[0, 0, 0]
    for k in range(1, N_DEV):
        loss = loss + loss_g[k, 0, 0]

    order = ("c_ctx", "w_ada", "b_ada", "g_attn", "g_ffn", "w_in", "ret_decay_fwd", "ret_decay_bwd", "g_ret", "g_q_lora",
             "w_uq", "g_kv_lora", "w_ukv", "w_out", "w_ff1", "w_ff2", "g_final")
    return (loss, grad_x, *[res[n][0] for n in order], *[res[n][1] for n in order], *[res[n][2] for n in order],
            *[res[n][3] for n in order])
```

```python
import functools
import math

import jax
import jax.numpy as jnp
from jax import lax
from jax.experimental import pallas as pl
from jax.experimental.pallas import tpu as pltpu

F32 = jnp.float32
BF = jnp.bfloat16
EPS = 1e-6
LANE = 128
LOG2_E = 1.0 / math.log(2.0)
N_DEV = 8
D_MODEL = 1024
SEQ = 2048
CTX_LEN = 256
GRID_W = 64
N_HEADS = 4
RET_CHUNK = 512
N_CHUNK = SEQ // RET_CHUNK
D_FF = 4096
FF_BLK = D_FF // N_DEV
IN_PAD = 2816
KV_LEN = CTX_LEN + SEQ
ROPE_BASE = 10000.0
ADAM_LR, ADAM_B1, ADAM_B2, ADAM_EPS, ADAM_WD, ADAM_STEP = 0.001, 0.9, 0.999, 1e-08, 0.01, 10
TOK = 512
TOK_B = 256
VMEM_LIMIT = 56 * 1024 * 1024
ARB = "arbitrary"
MESH = pl.DeviceIdType.MESH
_HEAD_SL = [slice(LANE * h, LANE * (h + 1)) for h in range(N_HEADS)]
W_SHAPES = [(2048, D_MODEL), (768, D_MODEL), (1024, 384), (1024, 256)]


def _dot(a, b, ca, cb):
    return lax.dot_general(a.astype(BF), b.astype(BF), (((ca,), (cb,)), ((), ())), preferred_element_type=F32)


@jax.custom_vjp
def mm(a, b):
    return _dot(a, b, 1, 0)


@jax.custom_vjp
def mm_nt(a, b):
    return _dot(a, b, 1, 1)


@jax.custom_vjp
def mm_tn(a, b):
    return _dot(a, b, 0, 0)


mm.defvjp(lambda a, b: (_dot(a, b, 1, 0), (a, b)), lambda r, g: (mm_nt(g, r[1]), mm_tn(r[0], g)))
mm_nt.defvjp(lambda a, b: (_dot(a, b, 1, 1), (a, b)), lambda r, g: (mm(g, r[1]), mm_tn(g, r[0])))
mm_tn.defvjp(lambda a, b: (_dot(a, b, 0, 0), (a, b)), lambda r, g: (mm_nt(r[1], g), mm(r[0], g)))


@jax.custom_vjp
def _mmw(a, w, probe):
    return _dot(a, w, 1, 0)


def _mmw_bwd(r, g):
    a, w = r
    return mm_nt(g, w), jnp.zeros_like(w), mm_tn(a, g)


_mmw.defvjp(lambda a, w, probe: (_dot(a, w, 1, 0), (a, w)), _mmw_bwd)


@jax.custom_vjp
def _mmwt(a, wt, probe):
    return _dot(a, wt, 1, 1)


_mmwt.defvjp(lambda a, wt, probe: (_dot(a, wt, 1, 1), (a, wt)),
             lambda r, g: (mm(g, r[1]), jnp.zeros_like(r[1]), mm_tn(g, r[0])))


def mmwt(a, wt, probe):
    return _dot(a, wt, 1, 1) if probe is None else _mmwt(a, wt, probe)


def mmw(a, w, probe):
    return _dot(a, w, 1, 0) if probe is None else _mmw(a, w, probe)


def rmsn(x, g):
    return x * lax.rsqrt(jnp.mean(x * x, axis=-1, keepdims=True) + EPS) * g


def silu(x):
    return x * jax.nn.sigmoid(x)


def _swap_halves_impl(x):
    return pltpu.roll(x, 64, 1)


@jax.custom_vjp
def swap_halves(x):
    return _swap_halves_impl(x)


swap_halves.defvjp(lambda x: (_swap_halves_impl(x), None), lambda _, g: (_swap_halves_impl(g),))


def rope(x, cs1, sn1, every=1):
    blocks = []
    for i in range(x.shape[-1] // LANE):
        xb = x[:, LANE * i:LANE * (i + 1)]
        blocks.append(xb * cs1 + swap_halves(xb) * sn1 if i % every == every - 1 else xb)
    return blocks[0] if len(blocks) == 1 else jnp.concatenate(blocks, axis=-1)


def k1_tile(x, sh, sc, g_attn, g_q, g_kv, ws, ps, tabs, is_ctx):
    w_a, w_b, w_uq, w_ukv = ws
    p_a, p_b, p_uq, p_ukv = ps
    cs1, sn1 = tabs
    h = rmsn(x, g_attn) * (1.0 + sc) + sh
    pa = mmwt(h, w_a, p_a)
    pb = mmwt(h, w_b, p_b)
    rk = pa[:, 512:1024] * 0.125
    rv = pa[:, 1024:1536]
    kpe = pb[:, 640:768]
    kv = mmwt(rmsn(pb[:, 384:640], g_kv), w_ukv, p_ukv)
    if not is_ctx:
        rk = rope(rk, cs1, sn1)
        kpe = rope(kpe, cs1, sn1)
    k_full = jnp.concatenate([piece for sl in _HEAD_SL for piece in (kv[:, sl], kpe)], axis=-1)
    v = kv[:, 512:]
    if is_ctx:
        return rk, rv, k_full, v
    rq = rope(pa[:, 0:512], cs1, sn1)
    rg = pa[:, 1536:2048]
    q = rope(mmwt(rmsn(pb[:, 0:384], g_q), w_uq, p_uq), cs1, sn1, every=2)
    return rq, rk, rv, rg, q, k_full, v


def log_sigmoid(x):
    return jnp.minimum(x, 0.0) - jnp.log(1.0 + jnp.exp(-jnp.abs(x)))


def _distance(reverse):
    c = RET_CHUNK
    ii = lax.broadcasted_iota(jnp.int32, (c, c), 0).astype(F32)
    jj = lax.broadcasted_iota(jnp.int32, (c, c), 1).astype(F32)
    return (jj - ii) if reverse else (ii - jj)


def decay_mask(lg, reverse):
    diff = _distance(reverse)
    return jnp.where(diff >= 0, jnp.exp2((lg * LOG2_E) * jnp.maximum(diff, 0.0)), 0.0)


@functools.partial(jax.custom_vjp, nondiff_argnums=(2,))
def kept_decay_mask(lg, mask, reverse):
    return mask


def _kept_decay_mask_fwd(lg, mask, reverse):
    return mask, mask


def _kept_decay_mask_bwd(reverse, mask, g):
    return jnp.sum(g * mask * _distance(reverse), keepdims=True).reshape(1, 1), jnp.zeros_like(mask)


kept_decay_mask.defvjp(_kept_decay_mask_fwd, _kept_decay_mask_bwd)


def decay_rows(lg, reverse):
    c = RET_CHUNK
    pos = lax.broadcasted_iota(jnp.int32, (c, 1), 0).astype(F32)
    if reverse:
        return jnp.exp(lg * pos), jnp.exp(lg * (c - pos))
    return jnp.exp(lg * (c - 1.0 - pos)), jnp.exp(lg * (pos + 1.0))


def ret_chunk(q, k, v, s, lg, reverse, pre=None):
    c = RET_CHUNK
    dec, wk, wq = (decay_mask(lg, reverse), *decay_rows(lg, reverse)) if pre is None else pre
    o = mm(mm_nt(q, k) * dec, v) + mm(q * wq, s)
    s_next = jnp.exp(lg * float(c)) * s + mm_tn(k * wk, v)
    return o, s_next


def ctx_state(kc, vc, lg, reverse):
    n = kc.shape[0]
    pos = lax.broadcasted_iota(jnp.int32, (n, 1), 0).astype(F32)
    w = jnp.exp(lg * pos) if reverse else jnp.exp(lg * (n - 1.0 - pos))
    return mm_tn(kc * w, vc)


def attn_head(qn, qp, kn, kp, v):
    s = (mm_nt(qn, kn) + mm_nt(qp, kp)) * (1.0 / math.sqrt(192.0))
    e = jnp.exp(s - jnp.max(s, axis=-1, keepdims=True))
    return mm(e / jnp.sum(e, axis=-1, keepdims=True), v)


def gn_gate(o, rg, g_ret):
    ys = []
    for h in range(N_HEADS):
        sl = slice(LANE * h, LANE * (h + 1))
        oh = o[:, sl]
        mu = jnp.mean(oh, axis=-1, keepdims=True)
        var = jnp.mean(jnp.square(oh - mu), axis=-1, keepdims=True)
        ys.append((oh - mu) * lax.rsqrt(var + EPS) * g_ret[:, sl])
    return jnp.concatenate(ys, axis=-1) * silu(rg)


def k4a_tile(x, o_f, o_b, rg, y_mla, g_ret, gt_a, g_ffn, sh_f, sc_f, w_out, p_out):
    mix = jnp.concatenate([gn_gate(o_f + o_b, rg, g_ret), y_mla], axis=-1)
    x_mid = x + gt_a * mmw(mix, w_out, p_out)
    h2 = rmsn(x_mid, g_ffn) * (1.0 + sc_f) + sh_f
    return x_mid, h2


def k4c_tile(x_mid, mlp, gt_f, g_final, tgt):
    y = rmsn(x_mid + gt_f * mlp, g_final)
    per_tok = jnp.mean(jnp.square(y - tgt), axis=-1, keepdims=True)
    return 0.5 * jnp.sum(per_tok, axis=0, keepdims=True)


def _cp(sem=None, vmem=VMEM_LIMIT):
    return pltpu.CompilerParams(dimension_semantics=sem, vmem_limit_bytes=vmem)


def _acc(ref, val, first):
    @pl.when(first)
    def _():
        ref[...] = val

    @pl.when(jnp.logical_not(first))
    def _():
        ref[...] += val


def _full(shape):
    nd = len(shape)
    return pl.BlockSpec(shape, lambda *_: (0,) * nd)


ANY = pl.BlockSpec(memory_space=pl.ANY)


def _sds(shape, dtype=F32):
    return jax.ShapeDtypeStruct(shape, dtype)


def _exchange(arrs, gather, name):
    n = len(arrs)
    modes = [gather] * n if isinstance(gather, bool) else list(gather)
    out_shape = [_sds(((N_DEV,) + a.shape) if g else a.shape, a.dtype) for a, g in zip(arrs, modes)]

    def body(*refs):
        ins, outs = refs[:n], refs[n:2 * n]
        send_sems, recv_sems, local_sems = refs[2 * n:]
        x, y, c = lax.axis_index("x"), lax.axis_index("y"), lax.axis_index("c")
        me = 4 * x + 2 * y + c
        sends, recvs, locs = [], [], []
        for i in range(n):
            gather = modes[i]
            for k in range(N_DEV - 1):
                bits = k + 1
                px = x ^ ((bits >> 2) & 1)
                py = y ^ ((bits >> 1) & 1)
                pc = c ^ (bits & 1)
                peer = 4 * px + 2 * py + pc
                src = ins[i] if gather else ins[i].at[peer]
                sem = i * (N_DEV - 1) + k
                sends.append(pltpu.make_async_remote_copy(
                    src_ref=src, dst_ref=outs[i].at[me], send_sem=send_sems.at[sem], recv_sem=recv_sems.at[sem],
                    device_id=(px, py, pc), device_id_type=MESH))
                recvs.append(pltpu.make_async_remote_copy(
                    src_ref=src, dst_ref=outs[i].at[peer], send_sem=send_sems.at[sem], recv_sem=recv_sems.at[sem],
                    device_id=(px, py, pc), device_id_type=MESH))
            locs.append(pltpu.make_async_copy(ins[i] if gather else ins[i].at[me], outs[i].at[me], local_sems.at[i]))
        for cp in locs + sends:
            cp.start()
        for cp in recvs:
            cp.wait_recv()
        for cp in sends:
            cp.wait_send()
        for cp in locs:
            cp.wait()

    outs = pl.pallas_call(
        body, name=name, out_shape=out_shape, in_specs=[ANY] * n, out_specs=[ANY] * n,
        scratch_shapes=[pltpu.SemaphoreType.DMA((n * (N_DEV - 1),)), pltpu.SemaphoreType.DMA((n * (N_DEV - 1),)),
                        pltpu.SemaphoreType.DMA((n,))],
    )(*arrs)
    return list(outs)


def _gather_two_level(arrs, name):
    n = len(arrs)

    def body(*refs):
        ins, outs = refs[:n], refs[n:2 * n]
        send_sems, recv_sems, local_sems = refs[2 * n:]
        x, y, c = lax.axis_index("x"), lax.axis_index("y"), lax.axis_index("c")
        sibling = (x, y, 1 - c)
        chips = [(1 - x, y), (x, 1 - y), (1 - x, 1 - y)]

        def slot(px, py, pc):
            return 4 * px + 2 * py + pc

        first, passed, waits, locs = [], [], [], []
        for i in range(n):
            def copy(k, block, to, src=None, i=i):
                dst = outs[i].at[slot(*block)]
                return pltpu.make_async_remote_copy(
                    src_ref=dst if src is None else src, dst_ref=dst, send_sem=send_sems.at[7 * i + k],
                    recv_sem=recv_sems.at[7 * i + k], device_id=to, device_id_type=MESH)

            locs.append(pltpu.make_async_copy(ins[i], outs[i].at[slot(x, y, c)], local_sems.at[i]))
            first.append(copy(0, (x, y, c), sibling, src=ins[i]))
            first += [copy(1 + j, (x, y, c), (*chip, c), src=ins[i]) for j, chip in enumerate(chips)]
            passed.append([copy(4 + j, (*chip, c), sibling) for j, chip in enumerate(chips)])
            waits.append([copy(1 + j, (*chip, c), (x, y, c)) for j, chip in enumerate(chips)])
        for cp in locs + first:
            cp.start()
        for j in range(3):
            for i in range(n):
                waits[i][j].wait_recv()
                passed[i][j].start()
        for i in range(n):
            def arrival(k, block, i=i):
                dst = outs[i].at[slot(*block)]
                return pltpu.make_async_remote_copy(
                    src_ref=dst, dst_ref=dst, send_sem=send_sems.at[7 * i + k], recv_sem=recv_sems.at[7 * i + k],
                    device_id=sibling, device_id_type=MESH)

            arrival(0, (x, y, 1 - c)).wait_recv()
            for j, chip in enumerate(chips):
                arrival(4 + j, (*chip, 1 - c)).wait_recv()
        for cp in first + [p for ps in passed for p in ps]:
            cp.wait_send()
        for cp in locs:
            cp.wait()

    outs = pl.pallas_call(
        body, name=name, out_shape=[_sds((N_DEV,) + a.shape, a.dtype) for a in arrs], in_specs=[ANY] * n,
        out_specs=[ANY] * n,
        scratch_shapes=[pltpu.SemaphoreType.DMA((7 * n,)), pltpu.SemaphoreType.DMA((7 * n,)),
                        pltpu.SemaphoreType.DMA((n,))],
    )(*arrs)
    return list(outs)


HBM = pl.BlockSpec(memory_space=pltpu.HBM)
SEM = pl.BlockSpec(memory_space=pltpu.SEMAPHORE)
EFFECT = pltpu.SideEffectType.DATAFLOW_SIDE_EFFECTING


def _peer(k, chips=False):
    x, y, c = lax.axis_index("x"), lax.axis_index("y"), lax.axis_index("c")
    bits = (k + 1) << 1 if chips else k + 1
    px, py, pc = x ^ ((bits >> 2) & 1), y ^ ((bits >> 1) & 1), c ^ (bits & 1)
    if chips:
        return (px, py, pc), 2 * px + py, 2 * x + y
    return (px, py, pc), 4 * px + 2 * py + pc, 4 * x + 2 * y + c


def _exchange_start(arrs, gather, name, after=None, chips=False):
    n = len(arrs)
    n_peer, n_slot = (3, 4) if chips else (N_DEV - 1, N_DEV)
    modes = [gather] * n if isinstance(gather, bool) else list(gather)
    lands = [pltpu.with_memory_space_constraint(lax.empty(((n_slot,) + a.shape) if g else a.shape, a.dtype), pltpu.HBM)
             for a, g in zip(arrs, modes)]
    srcs = [pltpu.with_memory_space_constraint(a, pltpu.HBM) for a in arrs]

    extra = [] if after is None else [after]

    def body(*refs):
        ins, zones = refs[:n], refs[n:2 * n]
        send_sems, recv_sems, local_sems = refs[2 * n + len(extra):2 * n + len(extra) + 3]
        token = refs[-1]
        for i in range(n):
            gather = modes[i]
            for k in range(n_peer):
                dev, peer, me = _peer(k, chips)
                sem = i * n_peer + k
                pltpu.make_async_remote_copy(
                    src_ref=ins[i] if gather else ins[i].at[peer], dst_ref=zones[i].at[me],
                    send_sem=send_sems.at[sem], recv_sem=recv_sems.at[sem], device_id=dev, device_id_type=MESH).start()
            _, _, me = _peer(0, chips)
            pltpu.make_async_copy(ins[i] if gather else ins[i].at[me], zones[i].at[me], local_sems.at[i]).start()
        token[...] = jnp.zeros_like(token)

    nsem = n * n_peer
    outs = pl.pallas_call(
        body, name=name,
        out_shape=[pltpu.SemaphoreType.DMA((nsem,)), pltpu.SemaphoreType.DMA((nsem,)), pltpu.SemaphoreType.DMA((n,))]
        + [pltpu.HBM(a.shape, a.dtype) for a in srcs] + [pltpu.HBM(z.shape, z.dtype) for z in lands]
        + [_sds((8, LANE))],
        in_specs=[HBM] * (2 * n) + [ANY] * len(extra),
        out_specs=[SEM, SEM, SEM] + [HBM] * (2 * n) + [pl.BlockSpec(memory_space=pltpu.VMEM)],
        input_output_aliases={i: 3 + i for i in range(2 * n)},
        compiler_params=pltpu.CompilerParams(has_side_effects=EFFECT),
    )(*srcs, *lands, *extra)
    return {"n": n, "gather": modes, "chips": chips, "sems": outs[:3], "srcs": outs[3:3 + n],
            "lands": outs[3 + n:3 + 2 * n], "token": outs[-1]}


def _exchange_wait(st, after, name, which=None):
    modes, chips = st["gather"], st["chips"]
    afters = list(after) if isinstance(after, (list, tuple)) else [after]
    which = list(range(st["n"])) if which is None else which
    n = len(which)
    n_peer = 3 if chips else N_DEV - 1
    srcs, lands = [st["srcs"][i] for i in which], [st["lands"][i] for i in which]

    def body(*refs):
        ins, zones = refs[:n], refs[n:2 * n]
        send_sems, recv_sems, local_sems = refs[2 * n:2 * n + 3]
        for j, i in enumerate(which):
            gather = modes[i]
            for k in range(n_peer):
                dev, peer, me = _peer(k, chips)
                sem = i * n_peer + k
                src = ins[j] if gather else ins[j].at[peer]
                cp = pltpu.make_async_remote_copy(
                    src_ref=src, dst_ref=zones[j].at[peer], send_sem=send_sems.at[sem], recv_sem=recv_sems.at[sem],
                    device_id=dev, device_id_type=MESH)
                cp.wait_send()
                cp.wait_recv()
            _, _, me = _peer(0, chips)
            pltpu.make_async_copy(ins[j] if gather else ins[j].at[me], zones[j].at[me], local_sems.at[i]).wait()

    outs = pl.pallas_call(
        body, name=name,
        out_shape=[pltpu.HBM(a.shape, a.dtype) for a in srcs] + [pltpu.HBM(z.shape, z.dtype) for z in lands],
        in_specs=[HBM] * (2 * n) + [SEM, SEM, SEM] + [ANY] * len(afters), out_specs=[HBM] * (2 * n),
        input_output_aliases={i: i for i in range(2 * n)},
        compiler_params=pltpu.CompilerParams(has_side_effects=EFFECT),
    )(*srcs, *lands, *st["sems"], *afters)
    return list(outs[n:])


def _pair_reduce(arrs, name, after):
    n = len(arrs)

    def body(*refs):
        ins, refs = refs[:n], refs[n + 1:]
        outs, got, mine = refs[:n], refs[n:2 * n], refs[2 * n:3 * n]
        send_sems, recv_sems, local_sems = refs[3 * n:]
        x, y, c = lax.axis_index("x"), lax.axis_index("y"), lax.axis_index("c")
        sends, locs = [], []
        for i in range(n):
            for q in range(4):
                sem = 4 * i + q
                sends.append(pltpu.make_async_remote_copy(
                    src_ref=ins[i].at[2 * q + 1 - c], dst_ref=got[i].at[q], send_sem=send_sems.at[sem],
                    recv_sem=recv_sems.at[sem], device_id=(x, y, 1 - c), device_id_type=MESH))
                locs.append(pltpu.make_async_copy(ins[i].at[2 * q + c], mine[i].at[q], local_sems.at[sem]))
        for cp in locs + sends:
            cp.start()
        for cp in sends:
            cp.wait_recv()
        for cp in locs:
            cp.wait()
        for i in range(n):
            outs[i][...] = (mine[i][...].astype(F32) + got[i][...].astype(F32)).astype(BF)
        for cp in sends:
            cp.wait_send()

    half = [(4,) + a.shape[1:] for a in arrs]
    outs = pl.pallas_call(
        body, name=name, out_shape=[_sds(h, BF) for h in half], in_specs=[ANY] * (n + 1),
        out_specs=[pl.BlockSpec(memory_space=pltpu.VMEM)] * n,
        scratch_shapes=[pltpu.VMEM(h, BF) for h in half] * 2
        + [pltpu.SemaphoreType.DMA((4 * n,)), pltpu.SemaphoreType.DMA((4 * n,)), pltpu.SemaphoreType.DMA((4 * n,))],
        compiler_params=_cp(),
    )(*arrs, after)
    return list(outs)


def _mod_fwd(crows, w_ada, b_blk):
    def body(c_ref, w_ref, b_ref, o_ref):
        o_ref[...] = mm(silu(c_ref[...]), w_ref[...]) + b_ref[...]

    return pl.pallas_call(body, name="mod_fwd", out_shape=_sds((24, 768)), compiler_params=_cp())(crows, w_ada, b_blk)


def _mod_bwd(crows, w_ada, dmod_blk, dmodc_blk):
    def body(c_ref, w_ref, d_ref, dc_ref, gw_ref, gc_ref, gb_ref):
        cr = c_ref[...]
        dc = dc_ref[0:1, :]
        for p in range(1, N_DEV):
            dc = dc + dc_ref[p:p + 1, :]
        row = lax.broadcasted_iota(jnp.int32, (24, 1), 0)
        gw_ref[...] = mm_tn(silu(cr), jnp.where(row == 16, dc, d_ref[...]))
        cc = cr[16:17, :]
        sg = jax.nn.sigmoid(cc)
        part = mm_nt(jnp.broadcast_to(dc, (8, 768)), w_ref[...])
        gc_ref[...] = part * (sg * (1.0 + cc * (1.0 - sg)))
        gb_ref[...] = jnp.broadcast_to(jnp.sum(d_ref[...], axis=0, keepdims=True) + dc, (8, 768))

    return pl.pallas_call(
        body, name="mod_bwd", out_shape=[_sds((D_MODEL, 768)), _sds((8, D_MODEL)), _sds((8, 768))],
        compiler_params=_cp())(crows, w_ada, dmod_blk, dmodc_blk)


def _tab_specs(tk):
    return [pl.BlockSpec((tk, LANE), lambda i, t: (t, 0))] * 2


def _k1_fwd(x, mod, g_attn, g_q, g_kv, ws, tabs, kv_all, is_ctx):
    b, l, _ = x.shape
    tk = CTX_LEN if is_ctx else TOK
    nt = l // tk
    n_f32 = 2 if is_ctx else 4

    def body(x_ref, mod_ref, ga_ref, gq_ref, gk_ref, wa_ref, wb_ref, wq_ref, wk_ref, cs_ref, sn_ref, *rest):
        outs = rest if is_ctx else rest[2:]
        res = k1_tile(x_ref[...], mod_ref[0:1, :], mod_ref[1:2, :], ga_ref[...], gq_ref[...], gk_ref[...],
                      (wa_ref[...], wb_ref[...], wq_ref[...], wk_ref[...]), (None,) * 4,
                      (cs_ref[...], sn_ref[...]), is_ctx)
        for o_ref, r in zip(outs, res):
            o_ref[...] = r.astype(o_ref.dtype)

    tok = lambda w, off=0: pl.BlockSpec((None, tk, w), lambda i, t: (i, t + off, 0))
    mod_spec = pl.BlockSpec((None, 8, D_MODEL), (lambda i, t: (0, 0, 0)) if is_ctx else (lambda i, t: (i, 0, 0)))
    kv_off = SEQ // tk if is_ctx else 0
    in_specs = ([tok(D_MODEL), mod_spec, _full((1, D_MODEL)), _full((1, 384)), _full((1, 256))]
                + [_full(s) for s in W_SHAPES] + _tab_specs(tk))
    args = [x, mod, g_attn, g_q, g_kv, *ws, *tabs]
    out_specs = [tok(512)] * n_f32 + ([] if is_ctx else [tok(1024)]) + [tok(1024, kv_off), tok(512, kv_off)]
    out_shape = ([_sds((b, l, 512))] * n_f32 + ([] if is_ctx else [_sds((b, l, 1024), BF)])
                 + [_sds((b, KV_LEN, 1024), BF), _sds((b, KV_LEN, 512), BF)])
    aliases = {}
    if not is_ctx:
        aliases = {len(args): n_f32 + 1, len(args) + 1: n_f32 + 2}
        in_specs += [ANY, ANY]
        args += list(kv_all)
    return pl.pallas_call(
        body, name="k1_fwd_ctx" if is_ctx else "k1_fwd", grid=(b, nt), in_specs=in_specs, out_specs=out_specs,
        out_shape=out_shape, input_output_aliases=aliases, compiler_params=_cp((ARB, ARB)),
    )(*args)


N_ACC = 7


def _k1_bwd(x, ctx, mod, mod_c, g_attn, g_q, g_kv, ws, tabs, cts, cts_c, dx_res):
    b, l, _ = x.shape
    tk = TOK_B
    nt = l // tk
    flat = [[a for group in c for a in group] for c in (cts, cts_c)]
    sizes = [[len(g) for g in c] for c in (cts, cts_c)]
    acc_shapes = W_SHAPES + [(1, D_MODEL), (1, 384), (1, 256)]

    def body(*refs):
        it = iter(refs)
        x_ref, c_ref, mod_ref, modc_ref, ga_ref, gq_ref, gk_ref = [next(it) for _ in range(7)]
        w_hbm = [next(it) for _ in range(4)]
        tab_refs = [next(it) for _ in range(2)]
        ct_refs = [[next(it) for _ in f] for f in flat]
        res_ref, gx_ref = next(it), next(it)
        out_hbm = [next(it) for _ in range(N_ACC)]
        dmod_ref, dmodc_ref = next(it), next(it)
        w_vmem = [next(it) for _ in range(4)]
        accs = [next(it) for _ in range(N_ACC)]
        sem = next(it)
        i, t = pl.program_id(0), pl.program_id(1)
        first = jnp.logical_and(i == 0, t == 0)

        @pl.when(first)
        def _():
            for src, dst in zip(w_hbm, w_vmem):
                pltpu.sync_copy(src, dst)
            for k in range(N_ACC):
                accs[k][...] = jnp.zeros(acc_shapes[k], F32)

        def tile(is_ctx):
            which = 1 if is_ctx else 0
            ct_vals, pos = [], 0
            for gsz in sizes[which]:
                v = ct_refs[which][pos][...].astype(F32)
                for r in ct_refs[which][pos + 1:pos + gsz]:
                    v = v + r[...]
                ct_vals.append(v)
                pos += gsz
            wv = tuple(r[...] for r in w_vmem)
            tv = tuple(r[...] for r in tab_refs)
            m_ref = modc_ref if is_ctx else mod_ref

            def f(xv, sh, sc, ga, gq, gk, *probes):
                return k1_tile(xv, sh, sc, ga, gq, gk, wv, probes, tv, is_ctx)

            probes = [jnp.zeros(s, F32) for s in W_SHAPES]
            xin = c_ref[...] if is_ctx else x_ref[...]
            _, vjp = jax.vjp(f, xin, m_ref[0:1, :], m_ref[1:2, :], ga_ref[...], gq_ref[...], gk_ref[...], *probes)
            dx, dsh, dsc, dga, dgq, dgk, dwa, dwb, dwq, dwk = vjp(tuple(ct_vals))
            for ref, val in zip(accs, (dwa, dwb, dwq, dwk, dga, dgq, dgk)):
                ref[...] += val
            return dx, dsh, dsc

        @pl.when(t == 0)
        def _():
            _, dsh, dsc = tile(True)
            _acc(dmodc_ref.at[0:1, :], dsh, i == 0)
            _acc(dmodc_ref.at[1:2, :], dsc, i == 0)

            @pl.when(i == 0)
            def _():
                dmodc_ref[2:8, :] = jnp.zeros((6, D_MODEL), F32)

        @pl.when(t > 0)
        def _():
            dx, dsh, dsc = tile(False)
            gx_ref[...] = dx + res_ref[...]
            _acc(dmod_ref.at[0:1, :], dsh, t == 1)
            _acc(dmod_ref.at[1:2, :], dsc, t == 1)

            @pl.when(t == 1)
            def _():
                dmod_ref[2:8, :] = jnp.zeros((6, D_MODEL), F32)

        @pl.when(jnp.logical_and(i == b - 1, t == nt))
        def _():
            for k in range(4):
                w_vmem[k][...] = accs[k][...].astype(BF)
            cps = [pltpu.make_async_copy(w_vmem[k] if k < 4 else accs[k], out_hbm[k], sem.at[k]) for k in range(N_ACC)]
            for cp in cps:
                cp.start()
            for cp in cps:
                cp.wait()

    lat = lambda w, off=0: pl.BlockSpec((None, tk, w), lambda i, t: (i, jnp.maximum(t - 1, 0) + off, 0))
    con = lambda w, off=0: pl.BlockSpec((None, tk, w), lambda i, t: (i, off, 0))
    mod_spec = pl.BlockSpec((None, 8, D_MODEL), lambda i, t: (i, 0, 0))
    modc_spec = pl.BlockSpec((None, 8, D_MODEL), lambda i, t: (0, 0, 0))
    tab_spec = pl.BlockSpec((tk, LANE), lambda i, t: (jnp.maximum(t - 1, 0), 0))
    in_specs = ([lat(D_MODEL), con(D_MODEL), mod_spec, modc_spec, _full((1, D_MODEL)), _full((1, 384)), _full((1, 256))]
                + [ANY] * 4 + [tab_spec] * 2)
    args = [x, ctx, mod, mod_c, g_attn, g_q, g_kv, *ws, *tabs]
    for a, off in flat[0]:
        in_specs.append(lat(a.shape[-1], off // tk))
        args.append(a)
    for a, off in flat[1]:
        in_specs.append(con(a.shape[-1], off // tk))
        args.append(a)
    in_specs.append(lat(D_MODEL))
    args.append(dx_res)
    out_shape = ([_sds((b, l, D_MODEL))] + [_sds(s, BF) for s in W_SHAPES] + [_sds(s) for s in acc_shapes[4:]]
                 + [_sds((b, 8, D_MODEL)), _sds((1, 8, D_MODEL))])
    out_specs = [lat(D_MODEL)] + [ANY] * N_ACC + [mod_spec, modc_spec]
    outs = pl.pallas_call(
        body, name="k1_bwd", grid=(b, nt + 1), in_specs=in_specs, out_specs=out_specs, out_shape=out_shape,
        scratch_shapes=[pltpu.VMEM(s, BF) for s in W_SHAPES] + [pltpu.VMEM(s, F32) for s in acc_shapes]
        + [pltpu.SemaphoreType.DMA((N_ACC,))],
        compiler_params=_cp((ARB, ARB)),
    )(*args)
    return outs[0], list(outs[1:1 + N_ACC]), outs[1 + N_ACC], outs[2 + N_ACC]


def _chunk_spec(rev):
    if rev:
        return pl.BlockSpec((None, RET_CHUNK, 512), lambda i, n: (i, N_CHUNK - 1 - n, 0))
    return pl.BlockSpec((None, RET_CHUNK, 512), lambda i, n: (i, n, 0))


def _state_spec(rev):
    if rev:
        return pl.BlockSpec((None, N_HEADS, None, LANE, LANE), lambda i, n: (i, 0, N_CHUNK - 1 - n, 0, 0))
    return pl.BlockSpec((None, N_HEADS, None, LANE, LANE), lambda i, n: (i, 0, n, 0, 0))


_CTX_SPEC = pl.BlockSpec((None, CTX_LEN, 512), lambda i, n: (i, 0, 0))
_DEC_SPEC = pl.BlockSpec((N_HEADS, 1, 1), lambda i, n: (0, 0, 0))


def _k2_fwd(rq, rk, rv, rkc, rvc, dec_f, dec_b):
    b = rq.shape[0]

    def body(qf, kf, vf, qb, kb, vb, kc, vc, df, db, of_ref, ob_ref, sf_out, sb_out, sf, sb, masks, rows):
        n = pl.program_id(1)
        first = jnp.logical_and(pl.program_id(0) == 0, n == 0)
        for h, sl in enumerate(_HEAD_SL):
            lgf, lgb = log_sigmoid(df[h]), log_sigmoid(db[h])

            @pl.when(first)
            def _():
                for d, (lg, rev) in enumerate(((lgf, False), (lgb, True))):
                    masks[2 * h + d] = decay_mask(lg, rev)
                    for j, w in enumerate(decay_rows(lg, rev)):
                        rows[2 * h + d, j] = jnp.broadcast_to(w, (RET_CHUNK, LANE))

            @pl.when(n == 0)
            def _():
                sf[h] = ctx_state(kc[:, sl], vc[:, sl], lgf, False)
                sb[h] = ctx_state(kc[:, sl], vc[:, sl], lgb, True)

            sf_out[h] = sf[h]
            sb_out[h] = sb[h]
            o, s = ret_chunk(qf[:, sl], kf[:, sl], vf[:, sl], sf[h], lgf, False,
                             (masks[2 * h], rows[2 * h, 0], rows[2 * h, 1]))
            of_ref[:, sl] = o
            sf[h] = s
            o, s = ret_chunk(qb[:, sl], kb[:, sl], vb[:, sl], sb[h], lgb, True,
                             (masks[2 * h + 1], rows[2 * h + 1, 0], rows[2 * h + 1, 1]))
            ob_ref[:, sl] = o
            sb[h] = s

    l = rq.shape[1]
    return pl.pallas_call(
        body, name="k2_fwd", grid=(b, N_CHUNK),
        in_specs=[_chunk_spec(False)] * 3 + [_chunk_spec(True)] * 3 + [_CTX_SPEC, _CTX_SPEC, _DEC_SPEC, _DEC_SPEC],
        out_specs=[_chunk_spec(False), _chunk_spec(True), _state_spec(False), _state_spec(True)],
        out_shape=[_sds((b, l, 512)), _sds((b, l, 512)), _sds((b, N_HEADS, N_CHUNK, LANE, LANE)),
                   _sds((b, N_HEADS, N_CHUNK, LANE, LANE))],
        scratch_shapes=[pltpu.VMEM((N_HEADS, LANE, LANE), F32), pltpu.VMEM((N_HEADS, LANE, LANE), F32),
                        pltpu.VMEM((2 * N_HEADS, RET_CHUNK, RET_CHUNK), F32),
                        pltpu.VMEM((2 * N_HEADS, 2, RET_CHUNK, LANE), F32)],
        compiler_params=_cp((ARB, ARB)),
    )(rq, rk, rv, rq, rk, rv, rkc, rvc, dec_f, dec_b)


def _k2_bwd(rq, rk, rv, do, sf_prev, sb_prev, rkc, rvc, dec_f, dec_b):
    b, l, _ = rq.shape

    def body(qf, kf, vf, gf, spf, qb, kb, vb, gb, spb, kc, vc, df, db,
             dqf, dkf, dvf, dqb, dkb, dvb, dkc, dvc, ddf, ddb, dsf, dsb, masks):
        n = pl.program_id(1)

        @pl.when(jnp.logical_and(pl.program_id(0) == 0, n == 0))
        def _():
            for h in range(N_HEADS):
                masks[2 * h] = decay_mask(log_sigmoid(df[h]), False)
                masks[2 * h + 1] = decay_mask(log_sigmoid(db[h]), True)

        @pl.when(n == 0)
        def _():
            dsf[...] = jnp.zeros((N_HEADS, LANE, LANE), F32)
            dsb[...] = jnp.zeros((N_HEADS, LANE, LANE), F32)

        def one(h, sl, q, k, v, g, sp, dec, ds, dq, dk, dv, dd, rev):
            mask = masks[2 * h + int(rev)]

            def f(qv, kv_, vv, sv, dcy):
                lg = log_sigmoid(dcy)
                return ret_chunk(qv, kv_, vv, sv, lg, rev, (kept_decay_mask(lg, mask, rev), *decay_rows(lg, rev)))

            _, vjp = jax.vjp(f, q[:, sl], k[:, sl], v[:, sl], sp[h], dec[h])
            gq, gk, gv, gs, gd = vjp((g[:, sl], ds[h]))
            dq[:, sl] = gq
            dk[:, sl] = gk
            dv[:, sl] = gv
            ds[h] = gs
            _acc(dd.at[h], jnp.broadcast_to(gd, (8, LANE)), n == 0)

        for h, sl in enumerate(_HEAD_SL):
            one(h, sl, qf, kf, vf, gf, spf, df, dsf, dqf, dkf, dvf, ddf, False)
            one(h, sl, qb, kb, vb, gb, spb, db, dsb, dqb, dkb, dvb, ddb, True)

        @pl.when(n == N_CHUNK - 1)
        def _():
            def f(kcv, vcv, dcy, rev):
                return ctx_state(kcv, vcv, log_sigmoid(dcy), rev)

            for h, sl in enumerate(_HEAD_SL):
                _, vjp_f = jax.vjp(functools.partial(f, rev=False), kc[:, sl], vc[:, sl], df[h])
                gk_f, gv_f, gd_f = vjp_f(dsf[h])
                _, vjp_b = jax.vjp(functools.partial(f, rev=True), kc[:, sl], vc[:, sl], db[h])
                gk_b, gv_b, gd_b = vjp_b(dsb[h])
                dkc[:, sl] = gk_f + gk_b
                dvc[:, sl] = gv_f + gv_b
                ddf[h] += jnp.broadcast_to(gd_f, (8, LANE))
                ddb[h] += jnp.broadcast_to(gd_b, (8, LANE))

    dd_spec = pl.BlockSpec((None, N_HEADS, 8, LANE), lambda i, n: (i, 0, 0, 0))
    return pl.pallas_call(
        body, name="k2_bwd", grid=(b, N_CHUNK),
        in_specs=[_chunk_spec(True)] * 4 + [_state_spec(True)] + [_chunk_spec(False)] * 4 + [_state_spec(False)]
        + [_CTX_SPEC, _CTX_SPEC, _DEC_SPEC, _DEC_SPEC],
        out_specs=[_chunk_spec(True)] * 3 + [_chunk_spec(False)] * 3 + [_CTX_SPEC, _CTX_SPEC, dd_spec, dd_spec],
        out_shape=[_sds((b, l, 512))] * 6 + [_sds((b, CTX_LEN, 512))] * 2 + [_sds((b, N_HEADS, 8, LANE))] * 2,
        scratch_shapes=[pltpu.VMEM((N_HEADS, LANE, LANE), F32), pltpu.VMEM((N_HEADS, LANE, LANE), F32),
                        pltpu.VMEM((2 * N_HEADS, RET_CHUNK, RET_CHUNK), F32)],
        compiler_params=_cp((ARB, ARB)),
    )(rq, rk, rv, do, sf_prev, rq, rk, rv, do, sb_prev, rkc, rvc, dec_f, dec_b)


TQ = 1024
TQ_F = 512
QK_W = 2 * LANE
N_QP = 2
_Q_PARTS = [slice(i * TQ_F // N_QP, (i + 1) * TQ_F // N_QP) for i in range(N_QP)]


SM_SCALE = 1.0 / math.sqrt(192.0)


def _k3_specs(tq):
    qs = lambda w: pl.BlockSpec((None, tq, w), lambda i, h, t: (i, t, h))
    ks = lambda w: pl.BlockSpec((None, KV_LEN, w), lambda i, h, t: (i, 0, h))
    return qs, ks


def _k3_fwd(q, k, v):
    b, l, _ = q.shape

    def body(q_ref, k_ref, v_ref, o_ref, lse_ref):
        kv_, vv = k_ref[...], v_ref[...]
        for r in _Q_PARTS:
            s = _dot(q_ref[r, :], kv_, 1, 1)
            m = jnp.max(s, axis=-1, keepdims=True)
            e = jnp.exp2((s - m) * (SM_SCALE * LOG2_E))
            tot = jnp.sum(e, axis=-1, keepdims=True)
            o_ref[r, :] = _dot(e, vv, 1, 0) * (1.0 / tot)
            lse_ref[r, :] = jnp.broadcast_to(m * SM_SCALE + jnp.log(tot), (TQ_F // N_QP, LANE))

    qs, ks = _k3_specs(TQ_F)
    return pl.pallas_call(
        body, name="k3_fwd", grid=(b, N_HEADS, l // TQ_F), in_specs=[qs(QK_W), ks(QK_W), ks(LANE)],
        out_specs=[qs(LANE), qs(LANE)], out_shape=[_sds((b, l, N_HEADS * LANE))] * 2,
        compiler_params=_cp((ARB, ARB, ARB)),
    )(q, k, v)


def _k3_bwd(q, k, v, o, lse, dy, after):
    b, l, _ = q.shape

    def body(q_ref, k_ref, v_ref, o_ref, lse_ref, dy_ref, after_ref, dq_ref, dk_ref, dv_ref):
        t0 = pl.program_id(2) == 0
        kv_, vv = k_ref[...], v_ref[...]
        qv, dyv = q_ref[...], dy_ref[...]
        g = dyv.astype(BF)
        lse_col = jnp.max(lse_ref[...], axis=-1, keepdims=True)
        delta = jnp.sum(dyv * o_ref[...], axis=-1, keepdims=True)
        p = jnp.exp2(_dot(qv, kv_, 1, 1) * (SM_SCALE * LOG2_E) - lse_col * LOG2_E)
        ds = (p * (_dot(g, vv, 1, 1) - delta) * SM_SCALE).astype(BF)
        _acc(dv_ref, _dot(p, g, 0, 0), t0)
        dq_ref[...] = _dot(ds, kv_, 1, 0)
        _acc(dk_ref, _dot(ds, qv, 0, 0), t0)

    qs, ks = _k3_specs(TQ)
    return pl.pallas_call(
        body, name="k3_bwd", grid=(b, N_HEADS, l // TQ),
        in_specs=[qs(QK_W), ks(QK_W), ks(LANE), qs(LANE), qs(LANE), qs(LANE), ANY],
        out_specs=[qs(QK_W), ks(QK_W), ks(LANE)],
        out_shape=[_sds((b, l, N_HEADS * QK_W)), _sds((b, KV_LEN, N_HEADS * QK_W)), _sds((b, KV_LEN, N_HEADS * LANE))],
        compiler_params=_cp((ARB, ARB, ARB)),
    )(q, k, v, o, lse, dy, after)


def _mod_rows(mod_ref, rows):
    return [mod_ref[r:r + 1, :] for r in rows]


def _k4a_fwd(x, o_f, o_b, rg, y_mla, g_ret, w_out, mod, g_ffn):
    b, l, _ = x.shape

    def body(x_ref, of_ref, ob_ref, rg_ref, ym_ref, gr_ref, wo_ref, mod_ref, gf_ref, xm_ref, h2_ref):
        gt_a, sh_f, sc_f = _mod_rows(mod_ref, (2, 3, 4))
        x_mid, h2 = k4a_tile(x_ref[...], of_ref[...], ob_ref[...], rg_ref[...], ym_ref[...], gr_ref[...], gt_a,
                             gf_ref[...], sh_f, sc_f, wo_ref[...], None)
        xm_ref[...] = x_mid
        h2_ref[...] = h2.astype(BF)

    tok = lambda w: pl.BlockSpec((None, TOK, w), lambda i, t: (i, t, 0))
    mod_spec = pl.BlockSpec((None, 8, D_MODEL), lambda i, t: (i, 0, 0))
    return pl.pallas_call(
        body, name="k4a_fwd", grid=(b, l // TOK),
        in_specs=[tok(D_MODEL), tok(512), tok(512), tok(512), tok(512), _full((1, 512)), _full((D_MODEL, D_MODEL)),
                  mod_spec, _full((1, D_MODEL))],
        out_specs=[tok(D_MODEL), tok(D_MODEL)], out_shape=[_sds((b, l, D_MODEL)), _sds((b, l, D_MODEL), BF)],
        compiler_params=_cp((ARB, ARB)),
    )(x, o_f, o_b, rg, y_mla, g_ret, w_out, mod, g_ffn)


TOK_M = 512
TOK_D = 2048
HALF_FF = D_FF // 2


def _k4b_mlp_loss(h2, w1t, w2, x_mid, mod, g_final, tgt):
    b, l, _ = h2.shape
    nt = l // TOK_M

    def body(h2_ref, w1_hbm, w2_hbm, xm_ref, mod_ref, gfin_ref, tgt_ref, dxm_ref, dmlp_ref, r_ref, loss_ref, dgt_ref,
             dgfin_ref, w1_v, w2_v):
        i, t = pl.program_id(0), pl.program_id(1)
        first = jnp.logical_and(i == 0, t == 0)

        @pl.when(first)
        def _():
            pltpu.sync_copy(w1_hbm, w1_v)
            pltpu.sync_copy(w2_hbm, w2_v)

        h2v = h2_ref[...]
        mlp = None
        for half in range(2):
            rows = slice(half * HALF_FF, (half + 1) * HALF_FF)
            r = jnp.maximum(_dot(h2v, w1_v[rows, :], 1, 1), 0.0)
            r_ref[:, rows] = r.astype(BF)
            part = _dot(jnp.square(r), w2_v[rows, :], 1, 0)
            mlp = part if mlp is None else mlp + part
        (gt_f,) = _mod_rows(mod_ref, (5,))
        loss, vjp = jax.vjp(k4c_tile, xm_ref[...], mlp, gt_f, gfin_ref[...], tgt_ref[...])
        dxm, dmlp, dgt, dgfin, _ = vjp(jnp.ones((1, 1), F32))
        dxm_ref[...] = dxm
        dmlp_ref[...] = dmlp.astype(BF)
        _acc(loss_ref, jnp.broadcast_to(loss, (8, LANE)), first)
        _acc(dgfin_ref, dgfin, first)
        _acc(dgt_ref, dgt, t == 0)

    tok = lambda w: pl.BlockSpec((None, TOK_M, w), lambda i, t: (i, t, 0))
    return pl.pallas_call(
        body, name="k4b_mlp_loss", grid=(b, nt),
        in_specs=[tok(D_MODEL), ANY, ANY, tok(D_MODEL), pl.BlockSpec((None, 8, D_MODEL), lambda i, t: (i, 0, 0)),
                  _full((1, D_MODEL)), tok(D_MODEL)],
        out_specs=[tok(D_MODEL), tok(D_MODEL), tok(D_FF), _full((8, LANE)),
                   pl.BlockSpec((None, 1, D_MODEL), lambda i, t: (i, 0, 0)), _full((1, D_MODEL))],
        out_shape=[_sds((b, l, D_MODEL)), _sds((b, l, D_MODEL), BF), _sds((b, l, D_FF), BF), _sds((8, LANE)),
                   _sds((b, 1, D_MODEL)), _sds((1, D_MODEL))],
        scratch_shapes=[pltpu.VMEM((D_FF, D_MODEL), BF), pltpu.VMEM((D_FF, D_MODEL), BF)],
        compiler_params=_cp((ARB, ARB)),
    )(h2, w1t, w2, x_mid, mod, g_final, tgt)


def _k4d_mlp_bwd(h2, dmlp, r, w2):
    b, l, _ = h2.shape
    nt = l // TOK_D

    def body(h2_ref, dm_ref, r_ref, w2_ref, da_ref, dw1_ref, dw2_ref, acc1, acc2):
        i, t = pl.program_id(1), pl.program_id(2)
        first = jnp.logical_and(i == 0, t == 0)
        rv = r_ref[...].astype(F32)
        dm = dm_ref[...]
        da = (_dot(dm, w2_ref[...], 1, 1) * (2.0 * rv)).astype(BF)
        da_ref[...] = da
        _acc(acc2, _dot(jnp.square(rv), dm, 0, 0), first)
        _acc(acc1, _dot(h2_ref[...], da, 0, 0), first)

        @pl.when(jnp.logical_and(i == b - 1, t == nt - 1))
        def _():
            dw1_ref[...] = acc1[...].astype(BF)
            dw2_ref[...] = acc2[...].astype(BF)

    tok = lambda w: pl.BlockSpec((None, TOK_D, w), lambda j, i, t: (i, t, 0))
    col = pl.BlockSpec((None, TOK_D, FF_BLK), lambda j, i, t: (i, t, j))
    return pl.pallas_call(
        body, name="k4d_mlp_bwd", grid=(N_DEV, b, nt),
        in_specs=[tok(D_MODEL), tok(D_MODEL), col, pl.BlockSpec((None, FF_BLK, D_MODEL), lambda j, i, t: (j, 0, 0))],
        out_specs=[col, pl.BlockSpec((None, D_MODEL, FF_BLK), lambda j, i, t: (j, 0, 0)),
                   pl.BlockSpec((None, FF_BLK, D_MODEL), lambda j, i, t: (j, 0, 0))],
        out_shape=[_sds((b, l, D_FF), BF), _sds((N_DEV, D_MODEL, FF_BLK), BF), _sds((N_DEV, FF_BLK, D_MODEL), BF)],
        scratch_shapes=[pltpu.VMEM((D_MODEL, FF_BLK), F32), pltpu.VMEM((FF_BLK, D_MODEL), F32)],
        compiler_params=_cp((ARB, ARB, ARB)),
    )(h2, dmlp, r, w2)


def _k4e_bwd(x, o_f, o_b, rg, y_mla, g_ret, w_out, mod, g_ffn, dxm, da, w1t):
    b, l, _ = x.shape

    def body(x_ref, of_ref, ob_ref, rg_ref, ym_ref, gr_ref, wo_ref, mod_ref, gf_ref, dxm_ref, da_ref, w1_hbm,
             dx_ref, do_ref, drg_ref, dym_ref, dwo_ref, dgr_ref, dgf_ref, dmod_ref, w1_v, dwo_acc):
        i, t = pl.program_id(0), pl.program_id(1)
        first = jnp.logical_and(i == 0, t == 0)

        @pl.when(first)
        def _():
            pltpu.sync_copy(w1_hbm, w1_v)

        gt_a, sh_f, sc_f = _mod_rows(mod_ref, (2, 3, 4))
        wo = wo_ref[...]
        dh2 = _dot(da_ref[...], w1_v[...], 1, 0)

        def f(xv, ofv, rgv, ymv, grv, gta, gfv, shf, scf, p_out):
            return k4a_tile(xv, ofv, ob_ref[...], rgv, ymv, grv, gta, gfv, shf, scf, wo, p_out)

        _, vjp = jax.vjp(f, x_ref[...], of_ref[...], rg_ref[...], ym_ref[...], gr_ref[...], gt_a, gf_ref[...], sh_f,
                         sc_f, jnp.zeros((D_MODEL, D_MODEL), F32))
        dx, do, drg, dym, dgr, dgta, dgf, dshf, dscf, dwo = vjp((dxm_ref[...], dh2))
        dx_ref[...] = dx
        do_ref[...] = do
        drg_ref[...] = drg
        dym_ref[...] = dym
        _acc(dwo_acc, dwo, first)
        _acc(dgr_ref, dgr, first)
        _acc(dgf_ref, dgf, first)
        t0 = t == 0
        _acc(dmod_ref.at[2:3, :], dgta, t0)
        _acc(dmod_ref.at[3:4, :], dshf, t0)
        _acc(dmod_ref.at[4:5, :], dscf, t0)

        @pl.when(t0)
        def _():
            dmod_ref[0:2, :] = jnp.zeros((2, D_MODEL), F32)
            dmod_ref[5:8, :] = jnp.zeros((3, D_MODEL), F32)

        @pl.when(jnp.logical_and(i == b - 1, t == l // TOK_B - 1))
        def _():
            dwo_ref[...] = dwo_acc[...].astype(BF)

    tok = lambda w: pl.BlockSpec((None, TOK_B, w), lambda i, t: (i, t, 0))
    mod_spec = pl.BlockSpec((None, 8, D_MODEL), lambda i, t: (i, 0, 0))
    return pl.pallas_call(
        body, name="k4e_bwd", grid=(b, l // TOK_B),
        in_specs=[tok(D_MODEL), tok(512), tok(512), tok(512), tok(512), _full((1, 512)), _full((D_MODEL, D_MODEL)),
                  mod_spec, _full((1, D_MODEL)), tok(D_MODEL), tok(D_FF), ANY],
        out_specs=[tok(D_MODEL), tok(512), tok(512), tok(512), _full((D_MODEL, D_MODEL)), _full((1, 512)),
                   _full((1, D_MODEL)), mod_spec],
        out_shape=[_sds((b, l, D_MODEL)), _sds((b, l, 512)), _sds((b, l, 512)), _sds((b, l, 512)),
                   _sds((D_MODEL, D_MODEL), BF), _sds((1, 512)), _sds((1, D_MODEL)), _sds((b, 8, D_MODEL))],
        scratch_shapes=[pltpu.VMEM((D_FF, D_MODEL), BF), pltpu.VMEM((D_MODEL, D_MODEL), F32)],
        compiler_params=_cp((ARB, ARB)),
    )(x, o_f, o_b, rg, y_mla, g_ret, w_out, mod, g_ffn, dxm, da, w1t)


ADAM_BLOCK_BYTES = 32 * 1024 * 1024


def _adamw(w, m, v, pieces, name, after=None, riders=()):
    r, c = w.shape
    npc = pieces.shape[0]
    per_row = c * (7 * 4 + npc * pieces.dtype.itemsize) * 2
    rb = max(d for d in range(8, r + 1, 8) if r % d == 0 and d * per_row <= ADAM_BLOCK_BYTES)
    n_r = len(riders)

    def update(w_ref, m_ref, v_ref, p_ref, outs):
        g_ref, d_ref, nm_ref, nv_ref = outs
        g = p_ref[0].astype(F32)
        for k in range(1, p_ref.shape[0]):
            g = g + p_ref[k].astype(F32)
        wv = w_ref[...]
        mn = ADAM_B1 * m_ref[...] + (1.0 - ADAM_B1) * g
        vn = ADAM_B2 * v_ref[...] + (1.0 - ADAM_B2) * jnp.square(g)
        m_hat = mn / (1.0 - ADAM_B1 ** ADAM_STEP)
        v_hat = vn / (1.0 - ADAM_B2 ** ADAM_STEP)
        g_ref[...] = g
        d_ref[...] = -ADAM_LR * (m_hat / (jnp.sqrt(v_hat) + ADAM_EPS) + ADAM_WD * wv)
        nm_ref[...] = mn
        nv_ref[...] = vn

    def body(*refs):
        ins = [refs[4 * k:4 * k + 4] for k in range(1 + n_r)]
        outs = [refs[len(refs) - 4 * (1 + n_r) + 4 * k:][:4] for k in range(1 + n_r)]
        update(*ins[0], outs[0])

        @pl.when(pl.program_id(0) == 0)
        def _():
            for k in range(1, 1 + n_r):
                update(*ins[k], outs[k])

    blk = pl.BlockSpec((rb, c), lambda i: (i, 0))
    whole = lambda a: pl.BlockSpec(a.shape, lambda i, nd=a.ndim: (0,) * nd)
    extra = [] if after is None else [after]
    outs = pl.pallas_call(
        body, name=name, grid=(r // rb,),
        in_specs=[blk, blk, blk, pl.BlockSpec((npc, rb, c), lambda i: (0, i, 0))]
        + [whole(a) for rider in riders for a in rider] + [ANY] * len(extra),
        out_specs=[blk] * 4 + [whole(rider[0]) for rider in riders for _ in range(4)],
        out_shape=[_sds((r, c))] * 4 + [_sds(rider[0].shape) for rider in riders for _ in range(4)],
        compiler_params=_cp((ARB,)),
    )(w, m, v, pieces, *[a for rider in riders for a in rider], *extra)
    return list(outs) if not riders else [list(outs[4 * k:4 * k + 4]) for k in range(1 + n_r)]


def _pad_rot_rows(w, zero):
    k = w.shape[1]
    return lax.pad(w.reshape(-1, 2, 32, k), zero, ((0, 0, 0), (0, 0, 0), (0, 32, 0), (0, 0, 0))).reshape(-1, k)


def _cut_rot_rows(g):
    k = g.shape[1]
    return g.reshape(-1, 2, 64, k)[:, :, :32].reshape(-1, k)


def _w_in_pad(wt, zero):
    w_a = jnp.concatenate([_pad_rot_rows(wt[0:512], zero), wt[512:1536]], axis=0)
    w_b = jnp.concatenate([wt[1536:2176], _pad_rot_rows(wt[2176:2240], zero)], axis=0)
    return w_a, w_b


def _w_in_cut(g_a, g_b):
    return jnp.concatenate([_cut_rot_rows(g_a[0:1024]), g_a[1024:2048], g_b[0:640], _cut_rot_rows(g_b[640:768])], axis=0)


def _w_uq_pad(wt, zero):
    w = wt.reshape(N_HEADS, 192, 384)
    rot = _pad_rot_rows(w[:, 128:].reshape(N_HEADS * 64, 384), zero).reshape(N_HEADS, LANE, 384)
    return jnp.concatenate([w[:, :128], rot], axis=1).reshape(1024, 384)


def _w_uq_cut(g):
    g = g.reshape(N_HEADS, 256, 384)
    rot = _cut_rot_rows(g[:, 128:].reshape(N_HEADS * LANE, 384)).reshape(N_HEADS, 64, 384)
    return jnp.concatenate([g[:, :128], rot], axis=1).reshape(768, 384)


def _w_ukv_perm(wt):
    return jnp.transpose(wt.reshape(N_HEADS, 2, LANE, 256), (1, 0, 2, 3)).reshape(1024, 256)


def _w_ukv_unperm(g):
    return jnp.transpose(g.reshape(2, N_HEADS, LANE, 256), (1, 0, 2, 3)).reshape(1024, 256)


def _unshard_cols(g):
    return jnp.transpose(g, (1, 0, 2)).reshape(g.shape[1], N_DEV * g.shape[2])


def _rope_tables():
    rows = SEQ // GRID_W
    row = jnp.repeat(jnp.arange(rows, dtype=F32), GRID_W)
    col = jnp.tile(jnp.arange(GRID_W, dtype=F32), rows)
    freq = ROPE_BASE ** (-jnp.arange(16, dtype=F32) / 16)
    ang = jnp.concatenate([row[:, None] * freq, col[:, None] * freq], axis=-1)
    cos, sin = jnp.cos(ang), jnp.sin(ang)
    z = jnp.zeros((SEQ, 32), F32)
    return jnp.concatenate([cos, z, cos, z], axis=1), jnp.concatenate([-sin, z, sin, z], axis=1)


_PACKED = (("g_attn", 1024), ("g_ffn", 1024), ("ret_decay_fwd", 4), ("ret_decay_bwd", 4), ("g_ret", 512),
           ("g_q_lora", 384), ("g_kv_lora", 256), ("g_final", 1024))
_PACK_OFF = {}
_off = 0
for _name, _n in _PACKED:
    _PACK_OFF[_name] = _off
    _off += -(-_n // LANE) * LANE
PACK_W = _off


def _pack_small(vals):
    parts = []
    for name, n in _PACKED:
        a = vals[name].reshape(-1).astype(F32)
        parts.append(jnp.pad(a, (0, -(-n // LANE) * LANE - n)))
    return jnp.concatenate(parts).reshape(1, PACK_W)


def _adamw_small(params, packed, gcc, gb_ada):
    names = list(params)
    n_p = len(names)

    def body(*refs):
        p_ref, gcc_ref, gb_ref = refs[3 * n_p:3 * n_p + 3]
        outs = refs[3 * n_p + 3:]
        for k, name in enumerate(names):
            w_ref, m_ref, v_ref = refs[3 * k:3 * k + 3]
            n = w_ref.shape[1]
            if name == "b_ada":
                g = jnp.concatenate([gb_ref[d, 0:1, :] for d in range(N_DEV)], axis=-1)
            elif name == "c_ctx":
                g = gcc_ref[0, 0:1, :]
                for d in range(1, N_DEV):
                    g = g + gcc_ref[d, 0:1, :]
            else:
                off = _PACK_OFF[name]
                g = p_ref[0, :, off:off + n]
                for d in range(1, N_DEV):
                    g = g + p_ref[d, :, off:off + n]
            mn = ADAM_B1 * m_ref[...] + (1.0 - ADAM_B1) * g
            vn = ADAM_B2 * v_ref[...] + (1.0 - ADAM_B2) * jnp.square(g)
            m_hat = mn / (1.0 - ADAM_B1 ** ADAM_STEP)
            v_hat = vn / (1.0 - ADAM_B2 ** ADAM_STEP)
            outs[4 * k][...] = g
            outs[4 * k + 1][...] = -ADAM_LR * (m_hat / (jnp.sqrt(v_hat) + ADAM_EPS) + ADAM_WD * w_ref[...])
            outs[4 * k + 2][...] = mn
            outs[4 * k + 3][...] = vn

    args = [a for name in names for a in params[name]] + [packed, gcc, gb_ada]
    out_shape = [_sds(params[name][0].shape) for name in names for _ in range(4)]
    outs = pl.pallas_call(body, name="adamw_small", out_shape=out_shape, compiler_params=_cp())(*args)
    return {name: list(outs[4 * k:4 * k + 4]) for k, name in enumerate(names)}


def kernel(x, c, ctx, c_ctx, w_ada, b_ada, g_attn, g_ffn, w_in, ret_decay_fwd, ret_decay_bwd, g_ret, g_q_lora, w_uq, g_kv_lora, w_ukv, w_out, w_ff1, w_ff2, g_final, loss_target, m_c_ctx, m_w_ada, m_b_ada, m_g_attn, m_g_ffn, m_w_in, m_ret_decay_fwd, m_ret_decay_bwd, m_g_ret, m_g_q_lora, m_w_uq, m_g_kv_lora, m_w_ukv, m_w_out, m_w_ff1, m_w_ff2, m_g_final, v_c_ctx, v_w_ada, v_b_ada, v_g_attn, v_g_ffn, v_w_in, v_ret_decay_fwd, v_ret_decay_bwd, v_g_ret, v_g_q_lora, v_w_uq, v_g_kv_lora, v_w_ukv, v_w_out, v_w_ff1, v_w_ff2, v_g_final):
    me = 4 * lax.axis_index("x") + 2 * lax.axis_index("y") + lax.axis_index("c")
    nb = x.shape[0]

    c_pad = jnp.pad(c, ((0, 8 - nb), (0, 0)))
    c_all, g_in, g_uq, g_ukv = _gather_two_level(
        [c_pad, w_in[0].T.astype(BF), w_uq[0].T.astype(BF), w_ukv[0].T.astype(BF)], "gather_weights")

    crows = jnp.concatenate([c_all[:, :nb].reshape(N_DEV * nb, D_MODEL), c_ctx[None], jnp.zeros((7, D_MODEL), F32)])
    b_blk = lax.dynamic_slice(b_ada, (0, me * 768), (1, 768))
    st_f = _exchange_start([_mod_fwd(crows, w_ada[0], b_blk), w_out[0].astype(BF), w_ff1[0].T.astype(BF),
                            w_ff2[0].astype(BF)], True, "gather_fwd_start")
    zero = st_f["token"][0, 0].astype(BF)
    ws = (*_w_in_pad(g_in.reshape(2240, D_MODEL), zero), _w_uq_pad(g_uq.reshape(768, 384), zero),
          _w_ukv_perm(g_ukv.reshape(1024, 256)))
    (mod_g,) = _exchange_wait(st_f, ws, "gather_mod_wait", [0])
    mod_all = _unshard_cols(mod_g)
    mod_mine = lax.dynamic_slice(mod_all, (me * nb, 0), (nb, 6 * D_MODEL)).reshape(nb, 6, D_MODEL)
    mod = jnp.pad(mod_mine, ((0, 0), (0, 2), (0, 0)))
    mod_c = jnp.pad(mod_all[16].reshape(1, 6, D_MODEL), ((0, 0), (0, 2), (0, 0)))

    tabs = _rope_tables()
    dec_f = ret_decay_fwd.reshape(N_HEADS, 1, 1)
    dec_b = ret_decay_bwd.reshape(N_HEADS, 1, 1)

    rkc, rvc, k_ctx, v_ctx = _k1_fwd(ctx, mod_c, g_attn, g_q_lora, g_kv_lora, ws, tabs, None, True)
    rq, rk, rv, rg, q, k_all, v_all = _k1_fwd(x, mod, g_attn, g_q_lora, g_kv_lora, ws, tabs, (k_ctx, v_ctx), False)
    o_f, o_b, sf_prev, sb_prev = _k2_fwd(rq, rk, rv, rkc, rvc, dec_f, dec_b)
    y_mla, lse = _k3_fwd(q, k_all, v_all)
    (g_out,) = _exchange_wait(st_f, y_mla, "gather_wo_wait", [1])
    wo = g_out.reshape(D_MODEL, D_MODEL)
    x_mid, h2 = _k4a_fwd(x, o_f, o_b, rg, y_mla, g_ret, wo, mod, g_ffn)
    g_ff1t, g_ff2 = _exchange_wait(st_f, x_mid, "gather_ff_wait", [2, 3])
    w1t = g_ff1t.reshape(D_FF, D_MODEL)
    dxm, dmlp, relu_a, loss_acc, dgt_f, dg_final = _k4b_mlp_loss(h2, w1t, g_ff2.reshape(D_FF, D_MODEL), x_mid, mod,
                                                                 g_final.reshape(1, D_MODEL), loss_target)

    da, dw1, dw2 = _k4d_mlp_bwd(h2, dmlp, relu_a, g_ff2)
    dx_res, do, drg, dym, dwo, dg_ret, dg_ffn, dmod_a = _k4e_bwd(x, o_f, o_b, rg, y_mla, g_ret, wo, mod, g_ffn, dxm, da,
                                                                 w1t)
    st_s = _exchange_start([dw1, dw2, dwo.reshape(N_DEV, 128, D_MODEL)], False, "scatter_grads_start")
    dq, dk_all, dv_all = _k3_bwd(q, k_all, v_all, y_mla, lse, dym, st_s["token"])
    dqf, dkf, dvf, dqb, dkb, dvb, dkc, dvc, ddf, ddb = _k2_bwd(rq, rk, rv, do, sf_prev, sb_prev, rkc, rvc, dec_f, dec_b)
    cts = [[(dqf, 0), (dqb, 0)], [(dkf, 0), (dkb, 0)], [(dvf, 0), (dvb, 0)], [(drg, 0)], [(dq, 0)],
           [(dk_all, 0)], [(dv_all, 0)]]
    cts_c = [[(dkc, 0)], [(dvc, 0)], [(dk_all, SEQ)], [(dv_all, SEQ)]]
    grad_x, accs, dmod_1, dmod_c1 = _k1_bwd(x, ctx, mod, mod_c, g_attn, g_q_lora, g_kv_lora, ws, tabs, cts, cts_c,
                                            dx_res)
    dwa, dwb, dwq, dwk, dg_attn, dg_q, dg_kv = accs

    dmod_loc = (dmod_a + dmod_1).at[:, 5, :].set(dgt_f[:, 0, :])[:, :6, :].reshape(nb, 6 * D_MODEL)
    dmod_ctx = dmod_c1[:, :6, :].reshape(1, 6 * D_MODEL)
    small = {"g_attn": dg_attn, "g_ffn": dg_ffn, "ret_decay_fwd": jnp.sum(ddf[:, :, 0, 0], axis=0),
             "ret_decay_bwd": jnp.sum(ddb[:, :, 0, 0], axis=0), "g_ret": dg_ret, "g_q_lora": dg_q, "g_kv_lora": dg_kv,
             "g_final": dg_final}
    extra = jnp.concatenate([dmod_loc, dmod_ctx, jnp.zeros((5, 6 * D_MODEL), F32)])
    ex_pieces = jnp.transpose(extra.reshape(8, N_DEV, 768), (1, 0, 2))
    st_sm = _exchange_start([_pack_small(small), ex_pieces, loss_acc], [True, False, True], "gather_small_start")
    chip_sums = _pair_reduce([_w_in_cut(dwa, dwb).reshape(N_DEV, 280, D_MODEL), _w_uq_cut(dwq).reshape(N_DEV, 96, 384),
                              _w_ukv_unperm(dwk).reshape(N_DEV, 128, 256)], "pair_reduce", st_sm["token"])
    sm_g, ex_g, loss_g = _exchange_wait(st_sm, chip_sums[0], "gather_small_wait")
    dmod_blk = jnp.concatenate([ex_g[:, :nb].reshape(N_DEV * nb, 768), jnp.zeros((8, 768), F32)])
    gw_ada, gcc_part, gb_part = _mod_bwd(crows, w_ada[0], dmod_blk, ex_g[:, nb])
    st_c = _exchange_start([gcc_part, gb_part], True, "gather_cc_start")
    p_ff1, p_ff2, p_wo = _exchange_wait(st_s, st_c["token"], "scatter_grads_wait")
    st_r = _exchange_start(chip_sums, False, "scatter_rest_start", after=p_wo, chips=True)

    res = {}
    early = (("w_ff1", w_ff1, m_w_ff1, v_w_ff1, p_ff1), ("w_ff2", w_ff2, m_w_ff2, v_w_ff2, p_ff2))
    behind = st_r["token"]
    for name, w, m, v, pcs in early:
        res[name] = [a[None] for a in _adamw(w[0], m[0], v[0], pcs, "adamw_" + name, after=behind)]
        behind = res[name][3]
    o_ada, o_out = _adamw(w_ada[0], m_w_ada[0], v_w_ada[0], gw_ada[None], "adamw_w_ada", after=behind,
                          riders=[(w_out[0], m_w_out[0], v_w_out[0], p_wo)])
    res["w_ada"], res["w_out"] = [a[None] for a in o_ada], [a[None] for a in o_out]
    behind = res["w_ada"][3]

    smalls = {"c_ctx": (c_ctx, m_c_ctx, v_c_ctx), "b_ada": (b_ada, m_b_ada, v_b_ada), "g_attn": (g_attn, m_g_attn, v_g_attn),
              "g_ffn": (g_ffn, m_g_ffn, v_g_ffn), "ret_decay_fwd": (ret_decay_fwd, m_ret_decay_fwd, v_ret_decay_fwd),
              "ret_decay_bwd": (ret_decay_bwd, m_ret_decay_bwd, v_ret_decay_bwd), "g_ret": (g_ret, m_g_ret, v_g_ret),
              "g_q_lora": (g_q_lora, m_g_q_lora, v_g_q_lora), "g_kv_lora": (g_kv_lora, m_g_kv_lora, v_g_kv_lora),
              "g_final": (g_final, m_g_final, v_g_final)}
    rows = {k: tuple(a.reshape(1, -1) for a in t) for k, t in smalls.items()}
    gcc_g, gb_g = _exchange_wait(st_c, behind, "gather_cc_wait")
    small_out = _adamw_small(rows, sm_g, gcc_g, gb_g)
    for name, outs in small_out.items():
        res[name] = [o.reshape(smalls[name][0].shape) for o in outs]

    pieces = _exchange_wait(st_r, small_out["g_final"][3], "scatter_rest_wait")
    o_in, o_uq, o_ukv = _adamw(w_in[0].T, m_w_in[0].T, v_w_in[0].T, pieces[0], "adamw_w_in", riders=[
        (w_uq[0].T, m_w_uq[0].T, v_w_uq[0].T, pieces[1]),
        (w_ukv[0], m_w_ukv[0], v_w_ukv[0], jnp.transpose(pieces[2], (0, 2, 1)))])
    res["w_in"], res["w_uq"] = [a.T[None] for a in o_in], [a.T[None] for a in o_uq]
    res["w_ukv"] = [a[None] for a in o_ukv]

    loss = loss_g[0, 0, 0]
    for k in range(1, N_DEV):
        loss = loss + loss_g[k, 0, 0]

    order = ("c_ctx", "w_ada", "b_ada", "g_attn", "g_ffn", "w_in", "ret_decay_fwd", "ret_decay_bwd", "g_ret", "g_q_lora",
             "w_uq", "g_kv_lora", "w_ukv", "w_out", "w_ff1", "w_ff2", "g_final")
    return (loss, grad_x, *[res[n][0] for n in order], *[res[n][1] for n in order], *[res[n][2] for n in order],
            *[res[n][3] for n in order])
```
